```python
import jax, jax.numpy as jnp
from jax import lax
import numpy as np

D_MODEL = 1024
BATCH = 8
SEQ = 8192
DEPTH = 1

PLE_DIM = 256
HEAD_DIM = 64
FOX_HEADS = 8
SB_HEADS = 8
FOX_WIDTH = FOX_HEADS * HEAD_DIM
SB_WIDTH = SB_HEADS * HEAD_DIM
D_FF = 2816
Q_BLOCK = 128
EPS = 1e-6
FORGET_BIAS_INIT = 2.0
IN_SIZES = (FOX_WIDTH, FOX_WIDTH, FOX_WIDTH, FOX_HEADS, SB_WIDTH, SB_WIDTH, SB_WIDTH, D_MODEL, D_MODEL)
IN_WIDTH = 3 * FOX_WIDTH + FOX_HEADS + 3 * SB_WIDTH + 2 * D_MODEL

kernel_name = "hybrid_fox_stickbreak_macaron_ple"


def rms_norm(x, g):
    xf = x.astype(jnp.float32)
    y = xf * lax.rsqrt(jnp.mean(xf * xf, axis=-1, keepdims=True) + EPS)
    return (y * g.astype(jnp.float32)).astype(x.dtype)


def swiglu(h, w_gate, w_up, w_down):
    return (jax.nn.silu(h @ w_gate) * (h @ w_up)) @ w_down


def to_blocks(t):
    b, s = t.shape[0], t.shape[1]
    return jnp.moveaxis(t.reshape(b, s // Q_BLOCK, Q_BLOCK, *t.shape[2:]), 1, 0)


def from_blocks(t):
    nb, b = t.shape[0], t.shape[1]
    return jnp.moveaxis(t, 0, 1).reshape(b, nb * Q_BLOCK, -1)


def forgetting_attention(q, k, v, log_f):
    s_len = q.shape[1]
    scale = HEAD_DIM ** -0.5
    F = jnp.cumsum(log_f, axis=1)
    Fk = jnp.transpose(F, (0, 2, 1))[:, :, None, :]
    kpos = jnp.arange(s_len)
    qpos = kpos.reshape(s_len // Q_BLOCK, Q_BLOCK)

    def one_block(args):
        qi, Fi, pi = args
        logits = jnp.einsum('bqhd,bkhd->bhqk', qi, k).astype(jnp.float32) * scale
        logits = logits + jnp.transpose(Fi, (0, 2, 1))[..., None] - Fk
        mask = pi[:, None] >= kpos[None, :]
        logits = jnp.where(mask, logits, -jnp.inf)
        w = jax.nn.softmax(logits, axis=-1)
        return jnp.einsum('bhqk,bkhd->bqhd', w.astype(v.dtype), v)

    out = lax.map(one_block, (to_blocks(q), to_blocks(F), qpos))
    return from_blocks(out)


def stick_breaking_attention(q, k, v):
    s_len = q.shape[1]
    scale = HEAD_DIM ** -0.5
    kpos = jnp.arange(s_len)
    qpos = kpos.reshape(s_len // Q_BLOCK, Q_BLOCK)

    def one_block(args):
        qi, pi = args
        z = jnp.einsum('bqhd,bkhd->bhqk', qi, k).astype(jnp.float32) * scale
        mask = kpos[None, :] < pi[:, None]
        log_beta = jax.nn.log_sigmoid(z)
        log_1m = jnp.where(mask, jax.nn.log_sigmoid(-z), 0.0)
        after = lax.cumsum(log_1m, axis=3, reverse=True) - log_1m
        a = jnp.where(mask, jnp.exp(log_beta + after), 0.0)
        return jnp.einsum('bhqk,bkhd->bqhd', a.astype(v.dtype), v)

    out = lax.map(one_block, (to_blocks(q), qpos))
    return from_blocks(out)


def split_columns(t):
    outs, start = [], 0
    for size in IN_SIZES:
        outs.append(t[..., start:start + size])
        start += size
    return outs


def _fwd_setup_inputs(seed: int = 0) -> dict:
    key = jax.random.key(seed)
    ks = jax.random.split(key, 24)

    def w(k, shape, fan_in):
        return jax.random.normal(k, shape, jnp.float32) * (fan_in ** -0.5)

    def gain(k, shape):
        return 1.0 + 0.05 * jax.random.normal(k, shape, jnp.float32)

    L = DEPTH
    return {
        "x": jax.random.normal(ks[0], (BATCH, SEQ, D_MODEL), jnp.float32),
        "p": jax.random.normal(ks[1], (DEPTH, BATCH, SEQ, PLE_DIM), jnp.float32),
        "ffn1_norm": gain(ks[2], (L, D_MODEL)),
        "ffn1_w_gate": w(ks[3], (L, D_MODEL, D_FF), D_MODEL),
        "ffn1_w_up": w(ks[4], (L, D_MODEL, D_FF), D_MODEL),
        "ffn1_w_down": w(ks[5], (L, D_FF, D_MODEL), D_FF),
        "mix_norm": gain(ks[6], (L, D_MODEL)),
        "w_in": w(ks[7], (L, D_MODEL, IN_WIDTH), D_MODEL),
        "forget_bias": FORGET_BIAS_INIT + 0.1 * jax.random.normal(ks[8], (L, FOX_HEADS), jnp.float32),
        "q_norm": gain(ks[9], (L, HEAD_DIM)),
        "k_norm": gain(ks[10], (L, HEAD_DIM)),
        "w_branch_fox": w(ks[11], (L, FOX_WIDTH, D_MODEL), FOX_WIDTH),
        "w_branch_sb": w(ks[12], (L, SB_WIDTH, D_MODEL), SB_WIDTH),
        "w_out": w(ks[13], (L, D_MODEL, D_MODEL), D_MODEL),
        "ffn2_norm": gain(ks[14], (L, D_MODEL)),
        "ffn2_w_gate": w(ks[15], (L, D_MODEL, D_FF), D_MODEL),
        "ffn2_w_up": w(ks[16], (L, D_MODEL, D_FF), D_MODEL),
        "ffn2_w_down": w(ks[17], (L, D_FF, D_MODEL), D_FF),
        "ple_norm": gain(ks[18], (L, D_MODEL)),
        "w_ple_gate": w(ks[19], (L, D_MODEL, D_MODEL), D_MODEL),
        "w_ple_proj": w(ks[20], (L, PLE_DIM, D_MODEL), PLE_DIM),
    }


def _fwd_reference(x, p, ffn1_norm, ffn1_w_gate, ffn1_w_up, ffn1_w_down, mix_norm, w_in,
              forget_bias, q_norm, k_norm, w_branch_fox, w_branch_sb, w_out,
              ffn2_norm, ffn2_w_gate, ffn2_w_up, ffn2_w_down, ple_norm,
              w_ple_gate, w_ple_proj):
    b, s_len, _ = x.shape
    for i in range(DEPTH):
        x = x + 0.5 * swiglu(rms_norm(x, ffn1_norm[i]), ffn1_w_gate[i], ffn1_w_up[i], ffn1_w_down[i])

        h = rms_norm(x, mix_norm[i])
        fq, fk, fv, f_logit, sq, sk, sv, g_fox, g_sb = split_columns(h @ w_in[i])
        fq = rms_norm(fq.reshape(b, s_len, FOX_HEADS, HEAD_DIM), q_norm[i])
        fk = rms_norm(fk.reshape(b, s_len, FOX_HEADS, HEAD_DIM), k_norm[i])
        fv = fv.reshape(b, s_len, FOX_HEADS, HEAD_DIM)
        log_f = jax.nn.log_sigmoid((f_logit + forget_bias[i]).astype(jnp.float32))
        y_fox = forgetting_attention(fq, fk, fv, log_f)

        sq = sq.reshape(b, s_len, SB_HEADS, HEAD_DIM)
        sk = sk.reshape(b, s_len, SB_HEADS, HEAD_DIM)
        sv = sv.reshape(b, s_len, SB_HEADS, HEAD_DIM)
        y_sb = stick_breaking_attention(sq, sk, sv)

        merged = (jax.nn.sigmoid(g_fox) * (y_fox @ w_branch_fox[i])
                  + jax.nn.sigmoid(g_sb) * (y_sb @ w_branch_sb[i]))
        x = x + merged @ w_out[i]

        x = x + 0.5 * swiglu(rms_norm(x, ffn2_norm[i]), ffn2_w_gate[i], ffn2_w_up[i], ffn2_w_down[i])

        x = x + jax.nn.sigmoid(rms_norm(x, ple_norm[i]) @ w_ple_gate[i]) * (p[i] @ w_ple_proj[i])
    return x


import jax as _jax
import jax.numpy as _jnp

TWIN_FORMAT = 'train_step'
FWD_PARAMS = ['x', 'p', 'ffn1_norm', 'ffn1_w_gate', 'ffn1_w_up', 'ffn1_w_down', 'mix_norm', 'w_in', 'forget_bias', 'q_norm', 'k_norm', 'w_branch_fox', 'w_branch_sb', 'w_out', 'ffn2_norm', 'ffn2_w_gate', 'ffn2_w_up', 'ffn2_w_down', 'ple_norm', 'w_ple_gate', 'w_ple_proj']
TWIN_WEIGHTS = ['ffn1_norm', 'ffn1_w_gate', 'ffn1_w_up', 'ffn1_w_down', 'mix_norm', 'w_in', 'forget_bias', 'q_norm', 'k_norm', 'w_branch_fox', 'w_branch_sb', 'w_out', 'ffn2_norm', 'ffn2_w_gate', 'ffn2_w_up', 'ffn2_w_down', 'ple_norm', 'w_ple_gate', 'w_ple_proj']
TWIN_DIFF_INPUT = 'x'
TWIN_INPUTS = ['x', 'p', 'ffn1_norm', 'ffn1_w_gate', 'ffn1_w_up', 'ffn1_w_down', 'mix_norm', 'w_in', 'forget_bias', 'q_norm', 'k_norm', 'w_branch_fox', 'w_branch_sb', 'w_out', 'ffn2_norm', 'ffn2_w_gate', 'ffn2_w_up', 'ffn2_w_down', 'ple_norm', 'w_ple_gate', 'w_ple_proj', 'loss_target', 'm_ffn1_norm', 'm_ffn1_w_gate', 'm_ffn1_w_up', 'm_ffn1_w_down', 'm_mix_norm', 'm_w_in', 'm_forget_bias', 'm_q_norm', 'm_k_norm', 'm_w_branch_fox', 'm_w_branch_sb', 'm_w_out', 'm_ffn2_norm', 'm_ffn2_w_gate', 'm_ffn2_w_up', 'm_ffn2_w_down', 'm_ple_norm', 'm_w_ple_gate', 'm_w_ple_proj', 'v_ffn1_norm', 'v_ffn1_w_gate', 'v_ffn1_w_up', 'v_ffn1_w_down', 'v_mix_norm', 'v_w_in', 'v_forget_bias', 'v_q_norm', 'v_k_norm', 'v_w_branch_fox', 'v_w_branch_sb', 'v_w_out', 'v_ffn2_norm', 'v_ffn2_w_gate', 'v_ffn2_w_up', 'v_ffn2_w_down', 'v_ple_norm', 'v_w_ple_gate', 'v_w_ple_proj']
TWIN_OUTPUTS = ['loss', 'grad_x', 'grad_ffn1_norm', 'grad_ffn1_w_gate', 'grad_ffn1_w_up', 'grad_ffn1_w_down', 'grad_mix_norm', 'grad_w_in', 'grad_forget_bias', 'grad_q_norm', 'grad_k_norm', 'grad_w_branch_fox', 'grad_w_branch_sb', 'grad_w_out', 'grad_ffn2_norm', 'grad_ffn2_w_gate', 'grad_ffn2_w_up', 'grad_ffn2_w_down', 'grad_ple_norm', 'grad_w_ple_gate', 'grad_w_ple_proj', 'delta_ffn1_norm', 'delta_ffn1_w_gate', 'delta_ffn1_w_up', 'delta_ffn1_w_down', 'delta_mix_norm', 'delta_w_in', 'delta_forget_bias', 'delta_q_norm', 'delta_k_norm', 'delta_w_branch_fox', 'delta_w_branch_sb', 'delta_w_out', 'delta_ffn2_norm', 'delta_ffn2_w_gate', 'delta_ffn2_w_up', 'delta_ffn2_w_down', 'delta_ple_norm', 'delta_w_ple_gate', 'delta_w_ple_proj', 'new_m_ffn1_norm', 'new_m_ffn1_w_gate', 'new_m_ffn1_w_up', 'new_m_ffn1_w_down', 'new_m_mix_norm', 'new_m_w_in', 'new_m_forget_bias', 'new_m_q_norm', 'new_m_k_norm', 'new_m_w_branch_fox', 'new_m_w_branch_sb', 'new_m_w_out', 'new_m_ffn2_norm', 'new_m_ffn2_w_gate', 'new_m_ffn2_w_up', 'new_m_ffn2_w_down', 'new_m_ple_norm', 'new_m_w_ple_gate', 'new_m_w_ple_proj', 'new_v_ffn1_norm', 'new_v_ffn1_w_gate', 'new_v_ffn1_w_up', 'new_v_ffn1_w_down', 'new_v_mix_norm', 'new_v_w_in', 'new_v_forget_bias', 'new_v_q_norm', 'new_v_k_norm', 'new_v_w_branch_fox', 'new_v_w_branch_sb', 'new_v_w_out', 'new_v_ffn2_norm', 'new_v_ffn2_w_gate', 'new_v_ffn2_w_up', 'new_v_ffn2_w_down', 'new_v_ple_norm', 'new_v_w_ple_gate', 'new_v_w_ple_proj']
TWIN_LEAF_KINDS = {'loss': 'loss', 'grad_x': 'grad_x', 'grad_ffn1_norm': 'grad_w', 'grad_ffn1_w_gate': 'grad_w', 'grad_ffn1_w_up': 'grad_w', 'grad_ffn1_w_down': 'grad_w', 'grad_mix_norm': 'grad_w', 'grad_w_in': 'grad_w', 'grad_forget_bias': 'grad_w', 'grad_q_norm': 'grad_w', 'grad_k_norm': 'grad_w', 'grad_w_branch_fox': 'grad_w', 'grad_w_branch_sb': 'grad_w', 'grad_w_out': 'grad_w', 'grad_ffn2_norm': 'grad_w', 'grad_ffn2_w_gate': 'grad_w', 'grad_ffn2_w_up': 'grad_w', 'grad_ffn2_w_down': 'grad_w', 'grad_ple_norm': 'grad_w', 'grad_w_ple_gate': 'grad_w', 'grad_w_ple_proj': 'grad_w', 'delta_ffn1_norm': 'delta_w', 'delta_ffn1_w_gate': 'delta_w', 'delta_ffn1_w_up': 'delta_w', 'delta_ffn1_w_down': 'delta_w', 'delta_mix_norm': 'delta_w', 'delta_w_in': 'delta_w', 'delta_forget_bias': 'delta_w', 'delta_q_norm': 'delta_w', 'delta_k_norm': 'delta_w', 'delta_w_branch_fox': 'delta_w', 'delta_w_branch_sb': 'delta_w', 'delta_w_out': 'delta_w', 'delta_ffn2_norm': 'delta_w', 'delta_ffn2_w_gate': 'delta_w', 'delta_ffn2_w_up': 'delta_w', 'delta_ffn2_w_down': 'delta_w', 'delta_ple_norm': 'delta_w', 'delta_w_ple_gate': 'delta_w', 'delta_w_ple_proj': 'delta_w', 'new_m_ffn1_norm': 'new_m', 'new_m_ffn1_w_gate': 'new_m', 'new_m_ffn1_w_up': 'new_m', 'new_m_ffn1_w_down': 'new_m', 'new_m_mix_norm': 'new_m', 'new_m_w_in': 'new_m', 'new_m_forget_bias': 'new_m', 'new_m_q_norm': 'new_m', 'new_m_k_norm': 'new_m', 'new_m_w_branch_fox': 'new_m', 'new_m_w_branch_sb': 'new_m', 'new_m_w_out': 'new_m', 'new_m_ffn2_norm': 'new_m', 'new_m_ffn2_w_gate': 'new_m', 'new_m_ffn2_w_up': 'new_m', 'new_m_ffn2_w_down': 'new_m', 'new_m_ple_norm': 'new_m', 'new_m_w_ple_gate': 'new_m', 'new_m_w_ple_proj': 'new_m', 'new_v_ffn1_norm': 'new_v', 'new_v_ffn1_w_gate': 'new_v', 'new_v_ffn1_w_up': 'new_v', 'new_v_ffn1_w_down': 'new_v', 'new_v_mix_norm': 'new_v', 'new_v_w_in': 'new_v', 'new_v_forget_bias': 'new_v', 'new_v_q_norm': 'new_v', 'new_v_k_norm': 'new_v', 'new_v_w_branch_fox': 'new_v', 'new_v_w_branch_sb': 'new_v', 'new_v_w_out': 'new_v', 'new_v_ffn2_norm': 'new_v', 'new_v_ffn2_w_gate': 'new_v', 'new_v_ffn2_w_up': 'new_v', 'new_v_ffn2_w_down': 'new_v', 'new_v_ple_norm': 'new_v', 'new_v_w_ple_gate': 'new_v', 'new_v_w_ple_proj': 'new_v'}


def _forward(args):
    return _fwd_reference(*[args[k] for k in FWD_PARAMS])


def _output_shape():
    def fwd():
        inp = _fwd_setup_inputs(0)
        return _fwd_reference(*[inp[k] for k in FWD_PARAMS])
    out = _jax.eval_shape(fwd)
    return out.shape, out.dtype

N_MICROBATCH = 1
ADAM_LR = 0.001
ADAM_B1 = 0.9
ADAM_B2 = 0.999
ADAM_EPS = 1e-08
ADAM_WD = 0.01
ADAM_STEP = 10
PER_EXAMPLE_BATCH_AXIS = {'x': 0, 'p': 1, 'loss_target': 0}
SHARED_INPUTS = []
_WEIGHT_DTYPES = {'ffn1_norm': _jnp.float32, 'ffn1_w_gate': _jnp.float32, 'ffn1_w_up': _jnp.float32, 'ffn1_w_down': _jnp.float32, 'mix_norm': _jnp.float32, 'w_in': _jnp.float32, 'forget_bias': _jnp.float32, 'q_norm': _jnp.float32, 'k_norm': _jnp.float32, 'w_branch_fox': _jnp.float32, 'w_branch_sb': _jnp.float32, 'w_out': _jnp.float32, 'ffn2_norm': _jnp.float32, 'ffn2_w_gate': _jnp.float32, 'ffn2_w_up': _jnp.float32, 'ffn2_w_down': _jnp.float32, 'ple_norm': _jnp.float32, 'w_ple_gate': _jnp.float32, 'w_ple_proj': _jnp.float32}
MOMENT_SCALE = {'ffn1_norm': 1.223908e+01, 'ffn1_w_gate': 1.156305e-01, 'ffn1_w_up': 1.333181e-01, 'ffn1_w_down': 2.188472e-01, 'mix_norm': 1.515019e+01, 'w_in': 2.080343e-01, 'forget_bias': 1.246790e+02, 'q_norm': 1.929562e+01, 'k_norm': 1.931753e+01, 'w_branch_fox': 1.994304e-01, 'w_branch_sb': 3.804699e-01, 'w_out': 3.861243e-01, 'ffn2_norm': 1.250357e+01, 'ffn2_w_gate': 9.332643e-02, 'ffn2_w_up': 1.229579e-01, 'ffn2_w_down': 2.000476e-01, 'ple_norm': 1.925783e+00, 'w_ple_gate': 1.070075e-01, 'w_ple_proj': 9.021303e-01}


def _to_microbatches(a, axis):
    t = _jnp.moveaxis(a, axis, 0)
    t = t.reshape((N_MICROBATCH, t.shape[0] // N_MICROBATCH) + t.shape[1:])
    return _jnp.moveaxis(t, 1, axis + 1)


def setup_inputs(seed: int = 0) -> dict:
    inp = _fwd_setup_inputs(seed)
    key = _jax.random.fold_in(_jax.random.key(seed), 7919)
    shape, _ = _output_shape()
    out = dict(inp)
    out["loss_target"] = _jax.random.normal(_jax.random.fold_in(key, 0), shape, _jnp.float32)
    for i, name in enumerate(TWIN_WEIGHTS):
        w = inp[name].astype(_jnp.float32)
        if MOMENT_SCALE is None:
            s = _jnp.sqrt(_jnp.mean(_jnp.square(w)) + 1e-30)
        else:
            s = MOMENT_SCALE[name]
        km, kv = _jax.random.split(_jax.random.fold_in(key, i + 1))
        out[name] = w
        out["m_" + name] = s * _jax.random.normal(km, w.shape, _jnp.float32)
        out["v_" + name] = (s * s) * _jax.random.uniform(kv, w.shape, _jnp.float32, 0.5, 1.5)
    if N_MICROBATCH > 1:
        for name, axis in PER_EXAMPLE_BATCH_AXIS.items():
            out[name] = _to_microbatches(out[name], axis)
    return {'x': out['x'], 'p': out['p'], 'ffn1_norm': out['ffn1_norm'], 'ffn1_w_gate': out['ffn1_w_gate'], 'ffn1_w_up': out['ffn1_w_up'], 'ffn1_w_down': out['ffn1_w_down'], 'mix_norm': out['mix_norm'], 'w_in': out['w_in'], 'forget_bias': out['forget_bias'], 'q_norm': out['q_norm'], 'k_norm': out['k_norm'], 'w_branch_fox': out['w_branch_fox'], 'w_branch_sb': out['w_branch_sb'], 'w_out': out['w_out'], 'ffn2_norm': out['ffn2_norm'], 'ffn2_w_gate': out['ffn2_w_gate'], 'ffn2_w_up': out['ffn2_w_up'], 'ffn2_w_down': out['ffn2_w_down'], 'ple_norm': out['ple_norm'], 'w_ple_gate': out['w_ple_gate'], 'w_ple_proj': out['w_ple_proj'], 'loss_target': out['loss_target'], 'm_ffn1_norm': out['m_ffn1_norm'], 'm_ffn1_w_gate': out['m_ffn1_w_gate'], 'm_ffn1_w_up': out['m_ffn1_w_up'], 'm_ffn1_w_down': out['m_ffn1_w_down'], 'm_mix_norm': out['m_mix_norm'], 'm_w_in': out['m_w_in'], 'm_forget_bias': out['m_forget_bias'], 'm_q_norm': out['m_q_norm'], 'm_k_norm': out['m_k_norm'], 'm_w_branch_fox': out['m_w_branch_fox'], 'm_w_branch_sb': out['m_w_branch_sb'], 'm_w_out': out['m_w_out'], 'm_ffn2_norm': out['m_ffn2_norm'], 'm_ffn2_w_gate': out['m_ffn2_w_gate'], 'm_ffn2_w_up': out['m_ffn2_w_up'], 'm_ffn2_w_down': out['m_ffn2_w_down'], 'm_ple_norm': out['m_ple_norm'], 'm_w_ple_gate': out['m_w_ple_gate'], 'm_w_ple_proj': out['m_w_ple_proj'], 'v_ffn1_norm': out['v_ffn1_norm'], 'v_ffn1_w_gate': out['v_ffn1_w_gate'], 'v_ffn1_w_up': out['v_ffn1_w_up'], 'v_ffn1_w_down': out['v_ffn1_w_down'], 'v_mix_norm': out['v_mix_norm'], 'v_w_in': out['v_w_in'], 'v_forget_bias': out['v_forget_bias'], 'v_q_norm': out['v_q_norm'], 'v_k_norm': out['v_k_norm'], 'v_w_branch_fox': out['v_w_branch_fox'], 'v_w_branch_sb': out['v_w_branch_sb'], 'v_w_out': out['v_w_out'], 'v_ffn2_norm': out['v_ffn2_norm'], 'v_ffn2_w_gate': out['v_ffn2_w_gate'], 'v_ffn2_w_up': out['v_ffn2_w_up'], 'v_ffn2_w_down': out['v_ffn2_w_down'], 'v_ple_norm': out['v_ple_norm'], 'v_w_ple_gate': out['v_w_ple_gate'], 'v_w_ple_proj': out['v_w_ple_proj']}


def _loss(weights, diff, rest, loss_target):
    with _jax.named_scope("forward"):
        args = {**rest, TWIN_DIFF_INPUT: diff, **{k: w.astype(_WEIGHT_DTYPES[k]) for k, w in weights.items()}}
        y = _forward(args)
    with _jax.named_scope("loss_head"):
        err = _jnp.square(y.astype(_jnp.float32) - loss_target)
        return 0.5 * _jnp.sum(_jnp.mean(err, axis=-1)) if err.ndim else 0.5 * err


def _adamw(w, g, m, v):
    m = ADAM_B1 * m + (1.0 - ADAM_B1) * g
    v = ADAM_B2 * v + (1.0 - ADAM_B2) * _jnp.square(g)
    m_hat = m / (1.0 - ADAM_B1 ** ADAM_STEP)
    v_hat = v / (1.0 - ADAM_B2 ** ADAM_STEP)
    delta = -ADAM_LR * (m_hat / (_jnp.sqrt(v_hat) + ADAM_EPS) + ADAM_WD * w)
    return delta, m, v


def reference(x, p, ffn1_norm, ffn1_w_gate, ffn1_w_up, ffn1_w_down, mix_norm, w_in, forget_bias, q_norm, k_norm, w_branch_fox, w_branch_sb, w_out, ffn2_norm, ffn2_w_gate, ffn2_w_up, ffn2_w_down, ple_norm, w_ple_gate, w_ple_proj, loss_target, m_ffn1_norm, m_ffn1_w_gate, m_ffn1_w_up, m_ffn1_w_down, m_mix_norm, m_w_in, m_forget_bias, m_q_norm, m_k_norm, m_w_branch_fox, m_w_branch_sb, m_w_out, m_ffn2_norm, m_ffn2_w_gate, m_ffn2_w_up, m_ffn2_w_down, m_ple_norm, m_w_ple_gate, m_w_ple_proj, v_ffn1_norm, v_ffn1_w_gate, v_ffn1_w_up, v_ffn1_w_down, v_mix_norm, v_w_in, v_forget_bias, v_q_norm, v_k_norm, v_w_branch_fox, v_w_branch_sb, v_w_out, v_ffn2_norm, v_ffn2_w_gate, v_ffn2_w_up, v_ffn2_w_down, v_ple_norm, v_w_ple_gate, v_w_ple_proj):
    given = dict(x=x, p=p, ffn1_norm=ffn1_norm, ffn1_w_gate=ffn1_w_gate, ffn1_w_up=ffn1_w_up, ffn1_w_down=ffn1_w_down, mix_norm=mix_norm, w_in=w_in, forget_bias=forget_bias, q_norm=q_norm, k_norm=k_norm, w_branch_fox=w_branch_fox, w_branch_sb=w_branch_sb, w_out=w_out, ffn2_norm=ffn2_norm, ffn2_w_gate=ffn2_w_gate, ffn2_w_up=ffn2_w_up, ffn2_w_down=ffn2_w_down, ple_norm=ple_norm, w_ple_gate=w_ple_gate, w_ple_proj=w_ple_proj, loss_target=loss_target, m_ffn1_norm=m_ffn1_norm, m_ffn1_w_gate=m_ffn1_w_gate, m_ffn1_w_up=m_ffn1_w_up, m_ffn1_w_down=m_ffn1_w_down, m_mix_norm=m_mix_norm, m_w_in=m_w_in, m_forget_bias=m_forget_bias, m_q_norm=m_q_norm, m_k_norm=m_k_norm, m_w_branch_fox=m_w_branch_fox, m_w_branch_sb=m_w_branch_sb, m_w_out=m_w_out, m_ffn2_norm=m_ffn2_norm, m_ffn2_w_gate=m_ffn2_w_gate, m_ffn2_w_up=m_ffn2_w_up, m_ffn2_w_down=m_ffn2_w_down, m_ple_norm=m_ple_norm, m_w_ple_gate=m_w_ple_gate, m_w_ple_proj=m_w_ple_proj, v_ffn1_norm=v_ffn1_norm, v_ffn1_w_gate=v_ffn1_w_gate, v_ffn1_w_up=v_ffn1_w_up, v_ffn1_w_down=v_ffn1_w_down, v_mix_norm=v_mix_norm, v_w_in=v_w_in, v_forget_bias=v_forget_bias, v_q_norm=v_q_norm, v_k_norm=v_k_norm, v_w_branch_fox=v_w_branch_fox, v_w_branch_sb=v_w_branch_sb, v_w_out=v_w_out, v_ffn2_norm=v_ffn2_norm, v_ffn2_w_gate=v_ffn2_w_gate, v_ffn2_w_up=v_ffn2_w_up, v_ffn2_w_down=v_ffn2_w_down, v_ple_norm=v_ple_norm, v_w_ple_gate=v_w_ple_gate, v_w_ple_proj=v_w_ple_proj)
    weights = {n: given[n] for n in TWIN_WEIGHTS}
    shared = {n: given[n] for n in SHARED_INPUTS}
    per_example = {n: given[n] for n in ['x', 'p']}
    grad_fn = _jax.value_and_grad(_loss, argnums=(0, 1))

    def one_microbatch(ex, loss_target):
        ex = dict(ex)
        diff = ex.pop(TWIN_DIFF_INPUT)
        return grad_fn(weights, diff, {**shared, **ex}, loss_target)

    if N_MICROBATCH == 1:
        loss, (grad_w, grad_x) = one_microbatch(per_example, given["loss_target"])
    else:
        def body(carry, xs):
            loss_sum, grad_sum = carry
            l_k, (gw_k, gx_k) = one_microbatch(xs[0], xs[1])
            with _jax.named_scope("update"):
                return (loss_sum + l_k, _jax.tree.map(_jnp.add, grad_sum, gw_k)), gx_k

        init = (_jnp.zeros((), _jnp.float32), _jax.tree.map(_jnp.zeros_like, weights))
        (loss, grad_w), grad_x = _jax.lax.scan(body, init, (per_example, given["loss_target"]))
    with _jax.named_scope("update"):
        delta_w, new_m, new_v = {}, {}, {}
        for n in TWIN_WEIGHTS:
            delta_w[n], new_m[n], new_v[n] = _adamw(weights[n], grad_w[n], given["m_" + n], given["v_" + n])
    return (loss, grad_x, *[grad_w[n] for n in TWIN_WEIGHTS], *[delta_w[n] for n in TWIN_WEIGHTS],
            *[new_m[n] for n in TWIN_WEIGHTS], *[new_v[n] for n in TWIN_WEIGHTS])
```

```python
import functools

import jax
import jax.numpy as jnp
from jax import lax
from jax.experimental import pallas as pl
from jax.experimental.pallas import tpu as pltpu

F32 = jnp.float32
BF16 = jnp.bfloat16

D_MODEL = 1024
D_FF = 2816
N_CHIPS = 4
FF_SHARD = D_FF // N_CHIPS
HEAD_DIM = 64
N_HEADS = 8
ATT_W = N_HEADS * HEAD_DIM
PAIR_W = 2 * HEAD_DIM
N_PAIRS = N_HEADS // 2
PLE_DIM = 256
IN_WIDTH = 3 * ATT_W + N_HEADS + 3 * ATT_W + 2 * D_MODEL
EPS = 1e-6
QK_SCALE = HEAD_DIM ** -0.5
LANES = 128
ATT_BLOCK = 256
NEG_BIG = -1e30

ADAM_LR = 0.001
ADAM_B1 = 0.9
ADAM_B2 = 0.999
ADAM_EPS = 1e-08
ADAM_WD = 0.01
ADAM_STEP = 10

MESH = pl.DeviceIdType.MESH
MIB = 1024 * 1024


def _cparams(vmem_mib=48):
    return pltpu.CompilerParams(vmem_limit_bytes=vmem_mib * MIB)


def _dot(a, b):
    return jnp.dot(a, b, preferred_element_type=F32)


def _dot_tn(a, b):
    return lax.dot_general(a, b, (((0,), (0,)), ((), ())), preferred_element_type=F32)


def _sigmoid(x):
    return 1.0 / (1.0 + jnp.exp(-x))


def _split2(x):
    hi = x.astype(BF16)
    lo = (x - hi.astype(F32)).astype(BF16)
    return hi, lo


def _dot_split2(x, m):
    hi, lo = _split2(x)
    return _dot(hi, m) + _dot(lo, m)


def _split3(x):
    hi = x.astype(BF16)
    rest = x - hi.astype(F32)
    mid = rest.astype(BF16)
    lo = (rest - mid.astype(F32)).astype(BF16)
    return hi, mid, lo


def _dot_split3(x, m):
    hi, mid, lo = _split3(x)
    return _dot(hi, m) + _dot(mid, m) + _dot(lo, m)


def _rms(x):
    r = lax.rsqrt(jnp.mean(x * x, axis=-1, keepdims=True) + EPS)
    return x * r, r


def _rms_bwd(dh, xn, r, g):
    dxn = dh * g
    return r * (dxn - xn * jnp.mean(dxn * xn, axis=-1, keepdims=True))


def _colsum(x):
    return jnp.sum(x, axis=0, keepdims=True)


def _row_block(rows, row_bytes, budget):
    best = None
    for t in range(8, rows + 1, 8):
        if rows % t == 0 and t * row_bytes <= budget:
            best = t
    return best if best is not None else rows


def _ffn_fwd(x, g, wg, wu, wd, tm=512):
    s_len = x.shape[0]

    def body(x_ref, g_ref, wg_ref, wu_ref, wd_ref, o_ref, h_s, acc_s):
        j = pl.program_id(1)

        @pl.when(j == 0)
        def _():
            xn, _ = _rms(x_ref[...])
            h_s[...] = (xn * g_ref[...]).astype(BF16)
            acc_s[...] = jnp.zeros_like(acc_s)

        h = h_s[...]
        a = _dot(h, wg_ref[0])
        b = _dot(h, wu_ref[0])
        u = (a * _sigmoid(a) * b).astype(BF16)
        acc_s[...] += _dot(u, wd_ref[0])

        @pl.when(j == N_CHIPS - 1)
        def _():
            o_ref[...] = x_ref[...] + 0.5 * acc_s[...]

    return pl.pallas_call(
        body,
        grid=(s_len // tm, N_CHIPS),
        in_specs=[
            pl.BlockSpec((tm, D_MODEL), lambda i, j: (i, 0)),
            pl.BlockSpec((1, D_MODEL), lambda i, j: (0, 0)),
            pl.BlockSpec((1, D_MODEL, FF_SHARD), lambda i, j: (j, 0, 0)),
            pl.BlockSpec((1, D_MODEL, FF_SHARD), lambda i, j: (j, 0, 0)),
            pl.BlockSpec((1, FF_SHARD, D_MODEL), lambda i, j: (j, 0, 0)),
        ],
        out_specs=pl.BlockSpec((tm, D_MODEL), lambda i, j: (i, 0)),
        out_shape=jax.ShapeDtypeStruct((s_len, D_MODEL), F32),
        scratch_shapes=[pltpu.VMEM((tm, D_MODEL), BF16), pltpu.VMEM((tm, D_MODEL), F32)],
        compiler_params=_cparams(48),
        name="ffn_fwd",
    )(x, g, wg, wu, wd)


def _ffn_bwd(x, d, g, wg, wu, wd_t, wg_t, wu_t, tm=512):
    s_len = x.shape[0]
    nb = s_len // tm

    def body(x_ref, d_ref, g_ref, wg_ref, wu_ref, wdt_ref, wgt_ref, wut_ref,
             dx_ref, u_ref, da_ref, db_ref, h_ref, dbf_ref, dg_ref, h_s, dbf_s, dh_s):
        i = pl.program_id(0)
        j = pl.program_id(1)

        @pl.when(j == 0)
        def _():
            xn, _ = _rms(x_ref[...])
            h = (xn * g_ref[...]).astype(BF16)
            h_s[...] = h
            h_ref[...] = h
            dbf = d_ref[...].astype(BF16)
            dbf_s[...] = dbf
            dbf_ref[...] = dbf
            dh_s[...] = jnp.zeros_like(dh_s)

        @pl.when((i == 0) & (j == 0))
        def _():
            dg_ref[...] = jnp.zeros_like(dg_ref)

        h = h_s[...]
        a = _dot(h, wg_ref[0])
        b = _dot(h, wu_ref[0])
        du = 0.5 * _dot(dbf_s[...], wdt_ref[0])
        s = _sigmoid(a)
        silu = a * s
        da = (du * b * (s * (1.0 + a * (1.0 - s)))).astype(BF16)
        db = (du * silu).astype(BF16)
        u_ref[0] = (silu * b).astype(BF16)
        da_ref[0] = da
        db_ref[0] = db
        dh_s[...] += _dot(da, wgt_ref[0]) + _dot(db, wut_ref[0])

        @pl.when(j == N_CHIPS - 1)
        def _():
            xn, r = _rms(x_ref[...])
            dh = dh_s[...]
            dx_ref[...] = d_ref[...] + _rms_bwd(dh, xn, r, g_ref[...])
            dg_ref[0:1, :] += _colsum(dh * xn)

    row = lambda i, j: (i, 0)
    shard = lambda i, j: (j, 0, 0)
    act = lambda i, j: (j, i, 0)
    return pl.pallas_call(
        body,
        grid=(nb, N_CHIPS),
        in_specs=[
            pl.BlockSpec((tm, D_MODEL), row),
            pl.BlockSpec((tm, D_MODEL), row),
            pl.BlockSpec((1, D_MODEL), lambda i, j: (0, 0)),
            pl.BlockSpec((1, D_MODEL, FF_SHARD), shard),
            pl.BlockSpec((1, D_MODEL, FF_SHARD), shard),
            pl.BlockSpec((1, D_MODEL, FF_SHARD), shard),
            pl.BlockSpec((1, FF_SHARD, D_MODEL), shard),
            pl.BlockSpec((1, FF_SHARD, D_MODEL), shard),
        ],
        out_specs=[
            pl.BlockSpec((tm, D_MODEL), row),
            pl.BlockSpec((1, tm, FF_SHARD), act),
            pl.BlockSpec((1, tm, FF_SHARD), act),
            pl.BlockSpec((1, tm, FF_SHARD), act),
            pl.BlockSpec((tm, D_MODEL), row),
            pl.BlockSpec((tm, D_MODEL), row),
            pl.BlockSpec((8, D_MODEL), lambda i, j: (0, 0)),
        ],
        out_shape=[
            jax.ShapeDtypeStruct((s_len, D_MODEL), F32),
            jax.ShapeDtypeStruct((N_CHIPS, s_len, FF_SHARD), BF16),
            jax.ShapeDtypeStruct((N_CHIPS, s_len, FF_SHARD), BF16),
            jax.ShapeDtypeStruct((N_CHIPS, s_len, FF_SHARD), BF16),
            jax.ShapeDtypeStruct((s_len, D_MODEL), BF16),
            jax.ShapeDtypeStruct((s_len, D_MODEL), BF16),
            jax.ShapeDtypeStruct((8, D_MODEL), F32),
        ],
        scratch_shapes=[
            pltpu.VMEM((tm, D_MODEL), BF16),
            pltpu.VMEM((tm, D_MODEL), BF16),
            pltpu.VMEM((tm, D_MODEL), F32),
        ],
        compiler_params=_cparams(56),
        name="ffn_bwd",
    )(x, d, g, wg, wu, wd_t, wg_t, wu_t)


def _wgrad(a, b, scale=1.0, name="wgrad"):
    na, s_len, k_dim = a.shape
    nb, _, n_dim = b.shape
    n = max(na, nb)
    ts = min(s_len, 1024)
    steps = s_len // ts

    def body(a_ref, b_ref, o_ref, acc_s):
        s = pl.program_id(1)

        @pl.when(s == 0)
        def _():
            acc_s[...] = jnp.zeros_like(acc_s)

        acc_s[...] += _dot_tn(a_ref[0].astype(BF16), b_ref[0].astype(BF16))

        @pl.when(s == steps - 1)
        def _():
            o_ref[0] = (acc_s[...] * scale).astype(BF16)

    a_map = (lambda m, s: (m, s, 0)) if na > 1 else (lambda m, s: (0, s, 0))
    b_map = (lambda m, s: (m, s, 0)) if nb > 1 else (lambda m, s: (0, s, 0))
    return pl.pallas_call(
        body,
        grid=(n, steps),
        in_specs=[pl.BlockSpec((1, ts, k_dim), a_map), pl.BlockSpec((1, ts, n_dim), b_map)],
        out_specs=pl.BlockSpec((1, k_dim, n_dim), lambda m, s: (m, 0, 0)),
        out_shape=jax.ShapeDtypeStruct((n, k_dim, n_dim), BF16),
        scratch_shapes=[pltpu.VMEM((k_dim, n_dim), F32)],
        compiler_params=_cparams(56),
        name=name,
    )(a, b)


def _head_sum_matrices():
    lane = lax.broadcasted_iota(jnp.int32, (ATT_W, LANES), 0) // HEAD_DIM
    col = lax.broadcasted_iota(jnp.int32, (ATT_W, LANES), 1)
    bd = (lane == col).astype(BF16)
    return bd, bd.T


def _head_mean(t, bd, bd_t):
    per_head = _dot_split2(t, bd) * (1.0 / HEAD_DIM)
    return _dot_split2(per_head, bd_t)


def _head_rms(x, bd, bd_t):
    per_head = _dot_split2(x * x, bd) * (1.0 / HEAD_DIM)
    r = lax.rsqrt(per_head + EPS)
    rw = _dot_split2(r, bd_t)
    return x * rw, rw


def _log_sigmoid(z):
    return jnp.minimum(z, 0.0) - jnp.log(1.0 + jnp.exp(-jnp.abs(z)))


def _inproj_fwd(x1, g, w_fox, w_fl, w_sb, w_gates, bias, qn, kn, bd, bd_t, tm=256):
    s_len = x1.shape[0]

    def body(x_ref, g_ref, wf_ref, wl_ref, ws_ref, wg_ref, bias_ref, qn_ref, kn_ref, bd_ref, bdt_ref,
             fq_ref, fk_ref, qs_ref, kf_ref, vf_ref, logf_ref, sq_ref, sk_ref, sv_ref, gates_ref):
        xn, _ = _rms(x_ref[...])
        h = (xn * g_ref[...]).astype(BF16)
        zf = _dot(h, wf_ref[...])
        fq = zf[:, 0:ATT_W]
        fk = zf[:, ATT_W:2 * ATT_W]
        fq_ref[...] = fq
        fk_ref[...] = fk
        bd_m = bd_ref[...]
        bdt_m = bdt_ref[...]
        fqn, _ = _head_rms(fq, bd_m, bdt_m)
        fkn, _ = _head_rms(fk, bd_m, bdt_m)
        qs_ref[...] = (fqn * qn_ref[...]).astype(BF16) * QK_SCALE
        kf_ref[...] = (fkn * kn_ref[...]).astype(BF16)
        vf_ref[...] = zf[:, 2 * ATT_W:3 * ATT_W].astype(BF16)
        logf_ref[...] = _log_sigmoid(_dot(h, wl_ref[...]) + bias_ref[...])
        zs = _dot(h, ws_ref[...])
        sq_ref[...] = zs[:, 0:ATT_W].astype(BF16) * QK_SCALE
        sk_ref[...] = zs[:, ATT_W:2 * ATT_W].astype(BF16)
        sv_ref[...] = zs[:, 2 * ATT_W:3 * ATT_W].astype(BF16)
        gates_ref[...] = _dot(h, wg_ref[...])

    row = lambda i: (i, 0)
    full = lambda i: (0, 0)
    att = lambda dt: jax.ShapeDtypeStruct((s_len, ATT_W), dt)
    return pl.pallas_call(
        body,
        grid=(s_len // tm,),
        in_specs=[
            pl.BlockSpec((tm, D_MODEL), row),
            pl.BlockSpec((1, D_MODEL), full),
            pl.BlockSpec((D_MODEL, 3 * ATT_W), full),
            pl.BlockSpec((D_MODEL, LANES), full),
            pl.BlockSpec((D_MODEL, 3 * ATT_W), full),
            pl.BlockSpec((D_MODEL, 2 * D_MODEL), full),
            pl.BlockSpec((1, LANES), full),
            pl.BlockSpec((1, ATT_W), full),
            pl.BlockSpec((1, ATT_W), full),
            pl.BlockSpec((ATT_W, LANES), full),
            pl.BlockSpec((LANES, ATT_W), full),
        ],
        out_specs=[
            pl.BlockSpec((tm, ATT_W), row), pl.BlockSpec((tm, ATT_W), row),
            pl.BlockSpec((tm, ATT_W), row), pl.BlockSpec((tm, ATT_W), row), pl.BlockSpec((tm, ATT_W), row),
            pl.BlockSpec((tm, LANES), row),
            pl.BlockSpec((tm, ATT_W), row), pl.BlockSpec((tm, ATT_W), row), pl.BlockSpec((tm, ATT_W), row),
            pl.BlockSpec((tm, 2 * D_MODEL), row),
        ],
        out_shape=[
            att(F32), att(F32), att(BF16), att(BF16), att(BF16),
            jax.ShapeDtypeStruct((s_len, LANES), F32),
            att(BF16), att(BF16), att(BF16),
            jax.ShapeDtypeStruct((s_len, 2 * D_MODEL), F32),
        ],
        compiler_params=_cparams(56),
        name="inproj_fwd",
    )(x1, g, w_fox, w_fl, w_sb, w_gates, bias, qn, kn, bd, bd_t)


def _tri(n, kind):
    r = lax.broadcasted_iota(jnp.int32, (n, n), 0)
    c = lax.broadcasted_iota(jnp.int32, (n, n), 1)
    m = {"row_ge_col": r >= c, "row_le_col": r <= c, "row_gt_col": r > c, "row_lt_col": r < c}[kind]
    return m.astype(BF16)


def _cumsum_rows(x, reverse, tm=256):
    s_len = x.shape[0]
    nb = s_len // tm
    tri = _tri(tm, "row_le_col" if reverse else "row_ge_col")
    edge = 0 if reverse else tm - 1

    def body(x_ref, tri_ref, o_ref, carry_s):
        @pl.when(pl.program_id(0) == 0)
        def _():
            carry_s[...] = jnp.zeros_like(carry_s)

        hi, mid, lo = _split3(x_ref[...])
        t = tri_ref[...]
        y = _dot(t, hi) + _dot(t, mid) + _dot(t, lo) + carry_s[...]
        o_ref[...] = y
        carry_s[...] = y[edge:edge + 1, :]

    order = (lambda i: (nb - 1 - i, 0)) if reverse else (lambda i: (i, 0))
    return pl.pallas_call(
        body,
        grid=(nb,),
        in_specs=[pl.BlockSpec((tm, LANES), order), pl.BlockSpec((tm, tm), lambda i: (0, 0))],
        out_specs=pl.BlockSpec((tm, LANES), order),
        out_shape=jax.ShapeDtypeStruct((s_len, LANES), F32),
        scratch_shapes=[pltpu.VMEM((1, LANES), F32)],
        name="cumsum_rev" if reverse else "cumsum_fwd",
    )(x, tri)


def _blocked_t(t, blk):
    nb = t.shape[0] // blk
    return t.reshape(nb, blk, N_PAIRS, PAIR_W).transpose(2, 0, 3, 1)


def _unblocked_t(t4):
    _, nb, _, blk = t4.shape
    return t4.transpose(1, 3, 0, 2).reshape(nb * blk, ATT_W)


def _blocked_rows(t, blk):
    return t.reshape(t.shape[0] // blk, blk, t.shape[1])


def _pair_rows_t(f8, blk):
    nb = f8.shape[0] // blk
    t = f8.reshape(nb, blk, N_PAIRS, 2).transpose(2, 0, 3, 1)
    return jnp.pad(t, ((0, 0), (0, 0), (0, 6), (0, 0)))


def _unpair_rows_t(t4):
    _, nb, _, blk = t4.shape
    return t4[:, :, 0:2, :].transpose(1, 3, 0, 2).reshape(nb * blk, N_HEADS)


def _head_masks(tq):
    lane = lax.broadcasted_iota(jnp.int32, (tq, PAIR_W), 1)
    return lane < HEAD_DIM


def _causal(tq, strict):
    r = lax.broadcasted_iota(jnp.int32, (tq, tq), 0)
    c = lax.broadcasted_iota(jnp.int32, (tq, tq), 1)
    return (c < r) if strict else (c <= r)


def _fox_fwd(qs, kt4, v3, fw, ft4):
    s_len = qs.shape[0]
    tq = ATT_BLOCK
    nq = s_len // tq

    def body(q_ref, kt_ref, v_ref, fw_ref, ft_ref, y_ref, lse_ref):
        i = pl.program_id(1)
        first = _head_masks(tq)
        q = q_ref[...]
        zero = jnp.zeros_like(q)
        qh = (jnp.where(first, q, zero), jnp.where(first, zero, q))
        fq = fw_ref[...]
        fqh = (fq[:, 0:1], fq[:, HEAD_DIM:HEAD_DIM + 1])
        mask = _causal(tq, strict=False)

        def block(j, carry, diag):
            kt = kt_ref[0, j]
            v = v_ref[j]
            fk = ft_ref[0, j]
            out = []
            for hh in range(2):
                m, l, acc = carry[3 * hh:3 * hh + 3]
                s = _dot(qh[hh], kt) + fqh[hh] - fk[hh:hh + 1, :]
                if diag:
                    s = jnp.where(mask, s, NEG_BIG)
                m_new = jnp.maximum(m, jnp.max(s, axis=-1, keepdims=True))
                p = jnp.exp(s - m_new)
                alpha = jnp.exp(m - m_new)
                l = alpha * l + jnp.sum(p, axis=-1, keepdims=True)
                acc = alpha * acc + _dot(p.astype(BF16), v)
                out += [m_new, l, acc]
            return tuple(out)

        init = (jnp.full((tq, 1), NEG_BIG, F32), jnp.zeros((tq, 1), F32), jnp.zeros((tq, PAIR_W), F32)) * 2
        carry = lax.fori_loop(0, i, lambda j, c: block(j, c, False), init)
        m0, l0, a0, m1, l1, a1 = block(i, carry, True)
        y_ref[...] = jnp.where(first, a0 / l0, a1 / l1)
        lse_ref[...] = jnp.where(first, m0 + jnp.log(l0), m1 + jnp.log(l1))

    nk = s_len // tq
    return pl.pallas_call(
        body,
        grid=(N_PAIRS, nq),
        in_specs=[
            pl.BlockSpec((tq, PAIR_W), lambda p, i: (i, p)),
            pl.BlockSpec((1, nk, PAIR_W, tq), lambda p, i: (p, 0, 0, 0)),
            pl.BlockSpec((nk, tq, PAIR_W), lambda p, i: (0, 0, p)),
            pl.BlockSpec((tq, PAIR_W), lambda p, i: (i, p)),
            pl.BlockSpec((1, nk, 8, tq), lambda p, i: (p, 0, 0, 0)),
        ],
        out_specs=[pl.BlockSpec((tq, PAIR_W), lambda p, i: (i, p)), pl.BlockSpec((tq, PAIR_W), lambda p, i: (i, p))],
        out_shape=[jax.ShapeDtypeStruct((s_len, ATT_W), F32), jax.ShapeDtypeStruct((s_len, ATT_W), F32)],
        compiler_params=_cparams(48),
        name="fox_fwd",
    )(qs, kt4, v3, fw, ft4)


def _fox_bwd(qs, qst4, kt4, k3, vt4, dy, dyt4, y, lse, fw, ft4):
    s_len = qs.shape[0]
    tq = ATT_BLOCK
    nq = s_len // tq

    def body(q_ref, qt_ref, kt_ref, k_ref, vt_ref, dy_ref, dyt_ref, y_ref, lse_ref, fw_ref, ft_ref,
             dq_ref, dfq_ref, dkt_ref, dvt_ref, dft_ref):
        i = pl.program_id(1)

        @pl.when(i == 0)
        def _():
            dkt_ref[...] = jnp.zeros_like(dkt_ref)
            dvt_ref[...] = jnp.zeros_like(dvt_ref)
            dft_ref[...] = jnp.zeros_like(dft_ref)

        first = _head_masks(tq)
        first_t = lax.broadcasted_iota(jnp.int32, (PAIR_W, tq), 0) < HEAD_DIM
        q = q_ref[...]
        zero = jnp.zeros_like(q)
        qh = (jnp.where(first, q, zero), jnp.where(first, zero, q))
        qt = qt_ref[0, 0]
        zero_t = jnp.zeros_like(qt)
        qth = (jnp.where(first_t, qt, zero_t), jnp.where(first_t, zero_t, qt))
        dyv = dy_ref[...]
        dyb = dyv.astype(BF16)
        dyh = (jnp.where(first, dyb, zero), jnp.where(first, zero, dyb))
        dyt = dyt_ref[0, 0]
        dyth = (jnp.where(first_t, dyt, zero_t), jnp.where(first_t, zero_t, dyt))
        prod = dyv * y_ref[...]
        zf = jnp.zeros_like(prod)
        delta = (jnp.sum(jnp.where(first, prod, zf), axis=-1, keepdims=True),
                 jnp.sum(jnp.where(first, zf, prod), axis=-1, keepdims=True))
        fq = fw_ref[...]
        fqh = (fq[:, 0:1], fq[:, HEAD_DIM:HEAD_DIM + 1])
        lse_v = lse_ref[...]
        lseh = (lse_v[:, 0:1], lse_v[:, HEAD_DIM:HEAD_DIM + 1])
        mask = _causal(tq, strict=False)

        def block(j, carry, diag):
            kt = kt_ref[0, j]
            k = k_ref[j]
            vt = vt_ref[0, j]
            fk = ft_ref[0, j]
            out = []
            dkt = jnp.zeros((PAIR_W, tq), F32)
            dvt = jnp.zeros((PAIR_W, tq), F32)
            for hh in range(2):
                s = _dot(qh[hh], kt) + fqh[hh] - fk[hh:hh + 1, :]
                p = jnp.exp(s - lseh[hh])
                if diag:
                    p = jnp.where(mask, p, 0.0)
                dp = _dot(dyh[hh], vt)
                ds = p * (dp - delta[hh])
                dsb = ds.astype(BF16)
                out.append(carry[2 * hh] + _dot(dsb, k))
                out.append(carry[2 * hh + 1] + jnp.sum(ds, axis=-1, keepdims=True))
                dkt += _dot(qth[hh], dsb)
                dvt += _dot(dyth[hh], p.astype(BF16))
                dft_ref[0, j, hh:hh + 1, :] -= _colsum(ds)
            dkt_ref[0, j] += dkt
            dvt_ref[0, j] += dvt
            return tuple(out)

        init = (jnp.zeros((tq, PAIR_W), F32), jnp.zeros((tq, 1), F32)) * 2
        carry = lax.fori_loop(0, i, lambda j, c: block(j, c, False), init)
        d0, r0, d1, r1 = block(i, carry, True)
        dq_ref[...] = jnp.where(first, d0, d1)
        dfq_ref[...] = jnp.where(first, r0, r1)

    nk = nq
    qblk = lambda p, i: (i, p)
    tblk = lambda p, i: (p, i, 0, 0)
    res4 = lambda p, i: (p, 0, 0, 0)
    return pl.pallas_call(
        body,
        grid=(N_PAIRS, nq),
        in_specs=[
            pl.BlockSpec((tq, PAIR_W), qblk),
            pl.BlockSpec((1, 1, PAIR_W, tq), tblk),
            pl.BlockSpec((1, nk, PAIR_W, tq), res4),
            pl.BlockSpec((nk, tq, PAIR_W), lambda p, i: (0, 0, p)),
            pl.BlockSpec((1, nk, PAIR_W, tq), res4),
            pl.BlockSpec((tq, PAIR_W), qblk),
            pl.BlockSpec((1, 1, PAIR_W, tq), tblk),
            pl.BlockSpec((tq, PAIR_W), qblk),
            pl.BlockSpec((tq, PAIR_W), qblk),
            pl.BlockSpec((tq, PAIR_W), qblk),
            pl.BlockSpec((1, nk, 8, tq), res4),
        ],
        out_specs=[
            pl.BlockSpec((tq, PAIR_W), qblk),
            pl.BlockSpec((tq, PAIR_W), qblk),
            pl.BlockSpec((1, nk, PAIR_W, tq), res4),
            pl.BlockSpec((1, nk, PAIR_W, tq), res4),
            pl.BlockSpec((1, nk, 8, tq), res4),
        ],
        out_shape=[
            jax.ShapeDtypeStruct((s_len, ATT_W), F32),
            jax.ShapeDtypeStruct((s_len, ATT_W), F32),
            jax.ShapeDtypeStruct((N_PAIRS, nk, PAIR_W, tq), F32),
            jax.ShapeDtypeStruct((N_PAIRS, nk, PAIR_W, tq), F32),
            jax.ShapeDtypeStruct((N_PAIRS, nk, 8, tq), F32),
        ],
        compiler_params=_cparams(56),
        name="fox_bwd",
    )(qs, qst4, kt4, k3, vt4, dy, dyt4, y, lse, fw, ft4)


def _sb_terms(z, mask, diag):
    soft = jnp.log(1.0 + jnp.exp(-jnp.abs(z)))
    lb = jnp.minimum(z, 0.0) - soft
    l1m = jnp.minimum(-z, 0.0) - soft
    if diag:
        l1m = jnp.where(mask, l1m, 0.0)
    return lb, l1m


def _sb_fwd(qs, kt4, v3):
    s_len = qs.shape[0]
    tq = ATT_BLOCK
    nq = s_len // tq
    upper = _tri(tq, "row_gt_col")

    def body(q_ref, kt_ref, v_ref, u_ref, y_ref, rtot_ref):
        i = pl.program_id(1)
        first = _head_masks(tq)
        q = q_ref[...]
        zero = jnp.zeros_like(q)
        qh = (jnp.where(first, q, zero), jnp.where(first, zero, q))
        mask = _causal(tq, strict=True)
        u = u_ref[...]

        def block(j, carry, diag):
            kt = kt_ref[0, j]
            v = v_ref[j]
            out = []
            for hh in range(2):
                r, acc = carry[2 * hh:2 * hh + 2]
                lb, l1m = _sb_terms(_dot(qh[hh], kt), mask, diag)
                a = jnp.exp(lb + _dot_split2(l1m, u) + r)
                if diag:
                    a = jnp.where(mask, a, 0.0)
                acc = acc + _dot(a.astype(BF16), v)
                r = r + jnp.sum(l1m, axis=-1, keepdims=True)
                out += [r, acc]
            return tuple(out)

        init = (jnp.zeros((tq, 1), F32), jnp.zeros((tq, PAIR_W), F32)) * 2
        carry = block(i, init, True)
        carry = lax.fori_loop(0, i, lambda n, c: block(i - 1 - n, c, False), carry)
        y_ref[...] = jnp.where(first, carry[1], carry[3])
        rtot_ref[...] = jnp.where(first, carry[0], carry[2])

    nk = nq
    return pl.pallas_call(
        body,
        grid=(N_PAIRS, nq),
        in_specs=[
            pl.BlockSpec((tq, PAIR_W), lambda p, i: (i, p)),
            pl.BlockSpec((1, nk, PAIR_W, tq), lambda p, i: (p, 0, 0, 0)),
            pl.BlockSpec((nk, tq, PAIR_W), lambda p, i: (0, 0, p)),
            pl.BlockSpec((tq, tq), lambda p, i: (0, 0)),
        ],
        out_specs=[pl.BlockSpec((tq, PAIR_W), lambda p, i: (i, p)), pl.BlockSpec((tq, PAIR_W), lambda p, i: (i, p))],
        out_shape=[jax.ShapeDtypeStruct((s_len, ATT_W), F32), jax.ShapeDtypeStruct((s_len, ATT_W), F32)],
        compiler_params=_cparams(48),
        name="sb_fwd",
    )(qs, kt4, v3, upper)


def _sb_bwd(qs, qst4, kt4, k3, vt4, dy, dyt4, rtot):
    s_len = qs.shape[0]
    tq = ATT_BLOCK
    nq = s_len // tq
    lower_in = _tri(tq, "row_le_col")
    lower = _tri(tq, "row_lt_col")

    def body(q_ref, qt_ref, kt_ref, k_ref, vt_ref, dy_ref, dyt_ref, rtot_ref, li_ref, l_ref,
             dq_ref, dkt_ref, dvt_ref):
        i = pl.program_id(1)

        @pl.when(i == 0)
        def _():
            dkt_ref[...] = jnp.zeros_like(dkt_ref)
            dvt_ref[...] = jnp.zeros_like(dvt_ref)

        first = _head_masks(tq)
        first_t = lax.broadcasted_iota(jnp.int32, (PAIR_W, tq), 0) < HEAD_DIM
        q = q_ref[...]
        zero = jnp.zeros_like(q)
        qh = (jnp.where(first, q, zero), jnp.where(first, zero, q))
        qt = qt_ref[0, 0]
        zero_t = jnp.zeros_like(qt)
        qth = (jnp.where(first_t, qt, zero_t), jnp.where(first_t, zero_t, qt))
        dyv = dy_ref[...]
        dyb = dyv.astype(BF16)
        dyh = (jnp.where(first, dyb, zero), jnp.where(first, zero, dyb))
        dyt = dyt_ref[0, 0]
        dyth = (jnp.where(first_t, dyt, zero_t), jnp.where(first_t, zero_t, dyt))
        rt = rtot_ref[...]
        rtoth = (rt[:, 0:1], rt[:, HEAD_DIM:HEAD_DIM + 1])
        mask = _causal(tq, strict=True)
        li = li_ref[...]
        lo_tri = l_ref[...]

        def block(j, carry, diag):
            kt = kt_ref[0, j]
            k = k_ref[j]
            vt = vt_ref[0, j]
            out = []
            dkt = jnp.zeros((PAIR_W, tq), F32)
            dvt = jnp.zeros((PAIR_W, tq), F32)
            for hh in range(2):
                pl_sum, c, dq = carry[3 * hh:3 * hh + 3]
                lb, l1m = _sb_terms(_dot(qh[hh], kt), mask, diag)
                after = (rtoth[hh] - pl_sum) - _dot_split3(l1m, li)
                a = jnp.exp(lb + after)
                if diag:
                    a = jnp.where(mask, a, 0.0)
                de = a * _dot(dyh[hh], vt)
                left = c + _dot_split2(de, lo_tri)
                beta = jnp.exp(lb)
                dz = de * (1.0 - beta) - left * beta
                if diag:
                    dz = jnp.where(mask, dz, 0.0)
                dzb = dz.astype(BF16)
                dq = dq + _dot(dzb, k)
                dkt += _dot(qth[hh], dzb)
                dvt += _dot(dyth[hh], a.astype(BF16))
                pl_sum = pl_sum + jnp.sum(l1m, axis=-1, keepdims=True)
                c = c + jnp.sum(de, axis=-1, keepdims=True)
                out += [pl_sum, c, dq]
            dkt_ref[0, j] += dkt
            dvt_ref[0, j] += dvt
            return tuple(out)

        init = (jnp.zeros((tq, 1), F32), jnp.zeros((tq, 1), F32), jnp.zeros((tq, PAIR_W), F32)) * 2
        carry = lax.fori_loop(0, i, lambda j, c: block(j, c, False), init)
        carry = block(i, carry, True)
        dq_ref[...] = jnp.where(first, carry[2], carry[5])

    nk = nq
    qblk = lambda p, i: (i, p)
    tblk = lambda p, i: (p, i, 0, 0)
    res4 = lambda p, i: (p, 0, 0, 0)
    tri_spec = pl.BlockSpec((tq, tq), lambda p, i: (0, 0))
    return pl.pallas_call(
        body,
        grid=(N_PAIRS, nq),
        in_specs=[
            pl.BlockSpec((tq, PAIR_W), qblk),
            pl.BlockSpec((1, 1, PAIR_W, tq), tblk),
            pl.BlockSpec((1, nk, PAIR_W, tq), res4),
            pl.BlockSpec((nk, tq, PAIR_W), lambda p, i: (0, 0, p)),
            pl.BlockSpec((1, nk, PAIR_W, tq), res4),
            pl.BlockSpec((tq, PAIR_W), qblk),
            pl.BlockSpec((1, 1, PAIR_W, tq), tblk),
            pl.BlockSpec((tq, PAIR_W), qblk),
            tri_spec,
            tri_spec,
        ],
        out_specs=[
            pl.BlockSpec((tq, PAIR_W), qblk),
            pl.BlockSpec((1, nk, PAIR_W, tq), res4),
            pl.BlockSpec((1, nk, PAIR_W, tq), res4),
        ],
        out_shape=[
            jax.ShapeDtypeStruct((s_len, ATT_W), F32),
            jax.ShapeDtypeStruct((N_PAIRS, nk, PAIR_W, tq), F32),
            jax.ShapeDtypeStruct((N_PAIRS, nk, PAIR_W, tq), F32),
        ],
        compiler_params=_cparams(56),
        name="sb_bwd",
    )(qs, qst4, kt4, k3, vt4, dy, dyt4, rtot, lower_in, lower)


def _merge_fwd(x1, gates, y_fox, y_sb, w_bf, w_bs, w_out, tm=512):
    s_len = x1.shape[0]

    def body(x_ref, g_ref, yf_ref, ys_ref, wbf_ref, wbs_ref, wo_ref, o_ref):
        g = g_ref[...]
        of = _dot(yf_ref[...].astype(BF16), wbf_ref[...])
        os_ = _dot(ys_ref[...].astype(BF16), wbs_ref[...])
        merged = _sigmoid(g[:, 0:D_MODEL]) * of + _sigmoid(g[:, D_MODEL:]) * os_
        o_ref[...] = x_ref[...] + _dot(merged.astype(BF16), wo_ref[...])

    row = lambda i: (i, 0)
    full = lambda i: (0, 0)
    return pl.pallas_call(
        body,
        grid=(s_len // tm,),
        in_specs=[
            pl.BlockSpec((tm, D_MODEL), row),
            pl.BlockSpec((tm, 2 * D_MODEL), row),
            pl.BlockSpec((tm, ATT_W), row),
            pl.BlockSpec((tm, ATT_W), row),
            pl.BlockSpec((ATT_W, D_MODEL), full),
            pl.BlockSpec((ATT_W, D_MODEL), full),
            pl.BlockSpec((D_MODEL, D_MODEL), full),
        ],
        out_specs=pl.BlockSpec((tm, D_MODEL), row),
        out_shape=jax.ShapeDtypeStruct((s_len, D_MODEL), F32),
        compiler_params=_cparams(48),
        name="merge_fwd",
    )(x1, gates, y_fox, y_sb, w_bf, w_bs, w_out)


def _merge_bwd(dx2, gates, y_fox, y_sb, w_bf, w_bs, w_out_t, w_bf_t, w_bs_t, tm=512):
    s_len = dx2.shape[0]

    def body(d_ref, g_ref, yf_ref, ys_ref, wbf_ref, wbs_ref, wot_ref, wbft_ref, wbst_ref,
             dyf_ref, dys_ref, dg_ref, dof_ref, dos_ref, m_ref, dbf_ref):
        dbf = d_ref[...].astype(BF16)
        dbf_ref[...] = dbf
        dm = _dot(dbf, wot_ref[...])
        g = g_ref[...]
        of = _dot(yf_ref[...].astype(BF16), wbf_ref[...])
        os_ = _dot(ys_ref[...].astype(BF16), wbs_ref[...])
        sf = _sigmoid(g[:, 0:D_MODEL])
        ss = _sigmoid(g[:, D_MODEL:])
        m_ref[...] = (sf * of + ss * os_).astype(BF16)
        d_of = (dm * sf).astype(BF16)
        d_os = (dm * ss).astype(BF16)
        dof_ref[...] = d_of
        dos_ref[...] = d_os
        dg_ref[:, 0:D_MODEL] = (dm * of * sf * (1.0 - sf)).astype(BF16)
        dg_ref[:, D_MODEL:] = (dm * os_ * ss * (1.0 - ss)).astype(BF16)
        dyf_ref[...] = _dot(d_of, wbft_ref[...])
        dys_ref[...] = _dot(d_os, wbst_ref[...])

    row = lambda i: (i, 0)
    full = lambda i: (0, 0)
    return pl.pallas_call(
        body,
        grid=(s_len // tm,),
        in_specs=[
            pl.BlockSpec((tm, D_MODEL), row),
            pl.BlockSpec((tm, 2 * D_MODEL), row),
            pl.BlockSpec((tm, ATT_W), row),
            pl.BlockSpec((tm, ATT_W), row),
            pl.BlockSpec((ATT_W, D_MODEL), full),
            pl.BlockSpec((ATT_W, D_MODEL), full),
            pl.BlockSpec((D_MODEL, D_MODEL), full),
            pl.BlockSpec((D_MODEL, ATT_W), full),
            pl.BlockSpec((D_MODEL, ATT_W), full),
        ],
        out_specs=[
            pl.BlockSpec((tm, ATT_W), row), pl.BlockSpec((tm, ATT_W), row),
            pl.BlockSpec((tm, 2 * D_MODEL), row),
            pl.BlockSpec((tm, D_MODEL), row), pl.BlockSpec((tm, D_MODEL), row),
            pl.BlockSpec((tm, D_MODEL), row), pl.BlockSpec((tm, D_MODEL), row),
        ],
        out_shape=[
            jax.ShapeDtypeStruct((s_len, ATT_W), F32), jax.ShapeDtypeStruct((s_len, ATT_W), F32),
            jax.ShapeDtypeStruct((s_len, 2 * D_MODEL), BF16),
            jax.ShapeDtypeStruct((s_len, D_MODEL), BF16), jax.ShapeDtypeStruct((s_len, D_MODEL), BF16),
            jax.ShapeDtypeStruct((s_len, D_MODEL), BF16), jax.ShapeDtypeStruct((s_len, D_MODEL), BF16),
        ],
        compiler_params=_cparams(56),
        name="merge_bwd",
    )(dx2, gates, y_fox, y_sb, w_bf, w_bs, w_out_t, w_bf_t, w_bs_t)


def _ple_loss(x3, p, g, w_pg, w_pg_t, w_pp, target, tm=512):
    s_len = x3.shape[0]
    inv_d = 1.0 / D_MODEL

    def body(x_ref, p_ref, g_ref, wpg_ref, wpgt_ref, wpp_ref, t_ref,
             dx_ref, du_ref, dt_ref, hn_ref, dg_ref, loss_ref):
        @pl.when(pl.program_id(0) == 0)
        def _():
            dg_ref[...] = jnp.zeros_like(dg_ref)
            loss_ref[...] = jnp.zeros_like(loss_ref)

        x = x_ref[...]
        xn, r = _rms(x)
        gain = g_ref[...]
        hn = (xn * gain).astype(BF16)
        hn_ref[...] = hn
        sg = _sigmoid(_dot(hn, wpg_ref[...]))
        t = _dot(p_ref[...].astype(BF16), wpp_ref[...])
        err = x + sg * t - t_ref[...]
        sq = jnp.sum(_colsum(err * err), axis=-1, keepdims=True)
        loss_ref[...] += (0.5 * inv_d) * sq
        dy = err * inv_d
        du = (dy * t * sg * (1.0 - sg)).astype(BF16)
        du_ref[...] = du
        dt_ref[...] = (dy * sg).astype(BF16)
        dh = _dot(du, wpgt_ref[...])
        dx_ref[...] = dy + _rms_bwd(dh, xn, r, gain)
        dg_ref[0:1, :] += _colsum(dh * xn)

    row = lambda i: (i, 0)
    full = lambda i: (0, 0)
    bf = jax.ShapeDtypeStruct((s_len, D_MODEL), BF16)
    return pl.pallas_call(
        body,
        grid=(s_len // tm,),
        in_specs=[
            pl.BlockSpec((tm, D_MODEL), row),
            pl.BlockSpec((tm, PLE_DIM), row),
            pl.BlockSpec((1, D_MODEL), full),
            pl.BlockSpec((D_MODEL, D_MODEL), full),
            pl.BlockSpec((D_MODEL, D_MODEL), full),
            pl.BlockSpec((PLE_DIM, D_MODEL), full),
            pl.BlockSpec((tm, D_MODEL), row),
        ],
        out_specs=[
            pl.BlockSpec((tm, D_MODEL), row), pl.BlockSpec((tm, D_MODEL), row),
            pl.BlockSpec((tm, D_MODEL), row), pl.BlockSpec((tm, D_MODEL), row),
            pl.BlockSpec((8, D_MODEL), full), pl.BlockSpec((8, LANES), full),
        ],
        out_shape=[
            jax.ShapeDtypeStruct((s_len, D_MODEL), F32), bf, bf, bf,
            jax.ShapeDtypeStruct((8, D_MODEL), F32), jax.ShapeDtypeStruct((8, LANES), F32),
        ],
        compiler_params=_cparams(48),
        name="ple_loss",
    )(x3, p, g, w_pg, w_pg_t, w_pp, target)


def _qknorm_bwd(fq, fk, dqs, dk, dv, qn, kn, bd, bd_t, tm=256):
    s_len = fq.shape[0]

    def body(fq_ref, fk_ref, dq_ref, dk_ref, dv_ref, qn_ref, kn_ref, bd_ref, bdt_ref,
             dz_ref, dqn_ref, dkn_ref):
        @pl.when(pl.program_id(0) == 0)
        def _():
            dqn_ref[...] = jnp.zeros_like(dqn_ref)
            dkn_ref[...] = jnp.zeros_like(dkn_ref)

        bd_m = bd_ref[...]
        bdt_m = bdt_ref[...]

        def one(x, dy, gain, dgain_ref):
            xn, rw = _head_rms(x, bd_m, bdt_m)
            dgain_ref[0:1, :] += _colsum(dy * xn)
            dxn = dy * gain
            return rw * (dxn - xn * _head_mean(dxn * xn, bd_m, bdt_m))

        dz_ref[:, 0:ATT_W] = one(fq_ref[...], dq_ref[...] * QK_SCALE, qn_ref[...], dqn_ref).astype(BF16)
        dz_ref[:, ATT_W:2 * ATT_W] = one(fk_ref[...], dk_ref[...], kn_ref[...], dkn_ref).astype(BF16)
        dz_ref[:, 2 * ATT_W:] = dv_ref[...].astype(BF16)

    row = lambda i: (i, 0)
    full = lambda i: (0, 0)
    att = pl.BlockSpec((tm, ATT_W), row)
    return pl.pallas_call(
        body,
        grid=(s_len // tm,),
        in_specs=[att, att, att, att, att,
                  pl.BlockSpec((1, ATT_W), full), pl.BlockSpec((1, ATT_W), full),
                  pl.BlockSpec((ATT_W, LANES), full), pl.BlockSpec((LANES, ATT_W), full)],
        out_specs=[pl.BlockSpec((tm, 3 * ATT_W), row), pl.BlockSpec((8, ATT_W), full), pl.BlockSpec((8, ATT_W), full)],
        out_shape=[jax.ShapeDtypeStruct((s_len, 3 * ATT_W), BF16),
                   jax.ShapeDtypeStruct((8, ATT_W), F32), jax.ShapeDtypeStruct((8, ATT_W), F32)],
        name="qknorm_bwd",
    )(fq, fk, dqs, dk, dv, qn, kn, bd, bd_t)


def _inproj_bwd(x1, dx2, g, dzf, dlogf, logf, dzs, dgates, w_fox_t, w_fl_t, w_sb_t, w_gates_t, tm=256):
    s_len = x1.shape[0]

    def body(x_ref, d_ref, g_ref, dzf_ref, dlf_ref, lf_ref, dzs_ref, dgt_ref, wf_ref, wl_ref, ws_ref, wg_ref,
             dx_ref, h_ref, dfl_ref, dg_ref, db_ref):
        @pl.when(pl.program_id(0) == 0)
        def _():
            dg_ref[...] = jnp.zeros_like(dg_ref)
            db_ref[...] = jnp.zeros_like(db_ref)

        xn, r = _rms(x_ref[...])
        gain = g_ref[...]
        h_ref[...] = (xn * gain).astype(BF16)
        lane = lax.broadcasted_iota(jnp.int32, (tm, LANES), 1)
        dfl = jnp.where(lane < N_HEADS, dlf_ref[...] * (1.0 - jnp.exp(lf_ref[...])), 0.0)
        db_ref[0:1, :] += _colsum(dfl)
        dflb = dfl.astype(BF16)
        dfl_ref[...] = dflb
        dh = (_dot(dzf_ref[...], wf_ref[...]) + _dot(dflb, wl_ref[...])
              + _dot(dzs_ref[...], ws_ref[...]) + _dot(dgt_ref[...], wg_ref[...]))
        dx_ref[...] = d_ref[...] + _rms_bwd(dh, xn, r, gain)
        dg_ref[0:1, :] += _colsum(dh * xn)

    row = lambda i: (i, 0)
    full = lambda i: (0, 0)
    return pl.pallas_call(
        body,
        grid=(s_len // tm,),
        in_specs=[
            pl.BlockSpec((tm, D_MODEL), row),
            pl.BlockSpec((tm, D_MODEL), row),
            pl.BlockSpec((1, D_MODEL), full),
            pl.BlockSpec((tm, 3 * ATT_W), row),
            pl.BlockSpec((tm, LANES), row),
            pl.BlockSpec((tm, LANES), row),
            pl.BlockSpec((tm, 3 * ATT_W), row),
            pl.BlockSpec((tm, 2 * D_MODEL), row),
            pl.BlockSpec((3 * ATT_W, D_MODEL), full),
            pl.BlockSpec((LANES, D_MODEL), full),
            pl.BlockSpec((3 * ATT_W, D_MODEL), full),
            pl.BlockSpec((2 * D_MODEL, D_MODEL), full),
        ],
        out_specs=[
            pl.BlockSpec((tm, D_MODEL), row), pl.BlockSpec((tm, D_MODEL), row), pl.BlockSpec((tm, LANES), row),
            pl.BlockSpec((8, D_MODEL), full), pl.BlockSpec((8, LANES), full),
        ],
        out_shape=[
            jax.ShapeDtypeStruct((s_len, D_MODEL), F32), jax.ShapeDtypeStruct((s_len, D_MODEL), BF16),
            jax.ShapeDtypeStruct((s_len, LANES), BF16),
            jax.ShapeDtypeStruct((8, D_MODEL), F32), jax.ShapeDtypeStruct((8, LANES), F32),
        ],
        compiler_params=_cparams(56),
        name="inproj_bwd",
    )(x1, dx2, g, dzf, dlogf, logf, dzs, dgates, w_fox_t, w_fl_t, w_sb_t, w_gates_t)


def _split_w_in(w_in):
    o = 3 * ATT_W
    w_fox = w_in[:, 0:o]
    w_fl = jnp.pad(w_in[:, o:o + N_HEADS], ((0, 0), (0, LANES - N_HEADS)))
    w_sb = w_in[:, o + N_HEADS:2 * o + N_HEADS]
    w_gates = w_in[:, 2 * o + N_HEADS:]
    return w_fox, w_fl, w_sb, w_gates


def _local_grads(x, p, target, small, full):
    blk = ATT_BLOCK
    bd, bd_t = _head_sum_matrices()
    tr = lambda w: jnp.swapaxes(w, -1, -2)

    w_fox, w_fl, w_sb, w_gates = _split_w_in(full["w_in"])
    bias = jnp.pad(small["forget_bias"], ((0, 0), (0, LANES - N_HEADS)))
    qn = jnp.tile(small["q_norm"], (1, N_HEADS))
    kn = jnp.tile(small["k_norm"], (1, N_HEADS))

    x1 = _ffn_fwd(x, small["ffn1_norm"], full["ffn1_w_gate"], full["ffn1_w_up"], full["ffn1_w_down"])
    fq, fk, f_qs, f_k, f_v, logf, s_qs, s_k, s_v, gates = _inproj_fwd(
        x1, small["mix_norm"], w_fox, w_fl, w_sb, w_gates, bias, qn, kn, bd, bd_t)
    f_cum = _cumsum_rows(logf, reverse=False)
    f8 = f_cum[:, 0:N_HEADS]
    fw = jnp.repeat(f8, HEAD_DIM, axis=1)
    ft4 = _pair_rows_t(f8, blk)
    f_kt4 = _blocked_t(f_k, blk)
    f_v3 = _blocked_rows(f_v, blk)
    y_fox, lse = _fox_fwd(f_qs, f_kt4, f_v3, fw, ft4)
    s_kt4 = _blocked_t(s_k, blk)
    s_v3 = _blocked_rows(s_v, blk)
    y_sb, s_rtot = _sb_fwd(s_qs, s_kt4, s_v3)
    x2 = _merge_fwd(x1, gates, y_fox, y_sb, full["w_branch_fox"], full["w_branch_sb"], full["w_out"])
    x3 = _ffn_fwd(x2, small["ffn2_norm"], full["ffn2_w_gate"], full["ffn2_w_up"], full["ffn2_w_down"])

    dx3, du_ple, dt_ple, hn_ple, dg_ple, loss_sum = _ple_loss(
        x3, p, small["ple_norm"], full["w_ple_gate"], tr(full["w_ple_gate"]), full["w_ple_proj"], target)
    dx2, u2, da2, db2, h_ffn2, d3_bf, dg_ffn2 = _ffn_bwd(
        x2, dx3, small["ffn2_norm"], full["ffn2_w_gate"], full["ffn2_w_up"],
        tr(full["ffn2_w_down"]), tr(full["ffn2_w_gate"]), tr(full["ffn2_w_up"]))
    dy_fox, dy_sb, dgates, d_of, d_os, merged, d2_bf = _merge_bwd(
        dx2, gates, y_fox, y_sb, full["w_branch_fox"], full["w_branch_sb"],
        tr(full["w_out"]), tr(full["w_branch_fox"]), tr(full["w_branch_sb"]))

    f_dqs, dfq_w, f_dkt4, f_dvt4, dft4 = _fox_bwd(
        f_qs, _blocked_t(f_qs, blk), f_kt4, _blocked_rows(f_k, blk), _blocked_t(f_v, blk),
        dy_fox, _blocked_t(dy_fox.astype(BF16), blk), y_fox, lse, fw, ft4)
    s_dqs, s_dkt4, s_dvt4 = _sb_bwd(
        s_qs, _blocked_t(s_qs, blk), s_kt4, _blocked_rows(s_k, blk), _blocked_t(s_v, blk),
        dy_sb, _blocked_t(dy_sb.astype(BF16), blk), s_rtot)

    dzf, dqn8, dkn8 = _qknorm_bwd(fq, fk, f_dqs, _unblocked_t(f_dkt4), _unblocked_t(f_dvt4), qn, kn, bd, bd_t)
    dzs = jnp.concatenate([s_dqs * QK_SCALE, _unblocked_t(s_dkt4), _unblocked_t(s_dvt4)], axis=1).astype(BF16)
    df8 = _unpair_rows_t(dft4) + dfq_w[:, ::HEAD_DIM]
    dlogf = _cumsum_rows(jnp.pad(df8, ((0, 0), (0, LANES - N_HEADS))), reverse=True)
    dx1, h_mix, dfl, dg_mix, dbias8 = _inproj_bwd(
        x1, dx2, small["mix_norm"], dzf, dlogf, logf, dzs, dgates,
        tr(w_fox), tr(w_fl), tr(w_sb), tr(w_gates))
    grad_x, u1, da1, db1, h_ffn1, d1_bf, dg_ffn1 = _ffn_bwd(
        x, dx1, small["ffn1_norm"], full["ffn1_w_gate"], full["ffn1_w_up"],
        tr(full["ffn1_w_down"]), tr(full["ffn1_w_gate"]), tr(full["ffn1_w_up"]))

    one = lambda t: t[None]
    gw = {}
    gw["ffn1_w_gate"] = _wgrad(one(h_ffn1), da1, name="wgrad_ffn1_gate")
    gw["ffn1_w_up"] = _wgrad(one(h_ffn1), db1, name="wgrad_ffn1_up")
    gw["ffn1_w_down"] = _wgrad(u1, one(d1_bf), scale=0.5, name="wgrad_ffn1_down")
    gw["ffn2_w_gate"] = _wgrad(one(h_ffn2), da2, name="wgrad_ffn2_gate")
    gw["ffn2_w_up"] = _wgrad(one(h_ffn2), db2, name="wgrad_ffn2_up")
    gw["ffn2_w_down"] = _wgrad(u2, one(d3_bf), scale=0.5, name="wgrad_ffn2_down")
    g_fox = _wgrad(one(h_mix), one(dzf), name="wgrad_in_fox")[0]
    g_fl = _wgrad(one(h_mix), one(dfl), name="wgrad_in_forget")[0]
    g_sb = _wgrad(one(h_mix), one(dzs), name="wgrad_in_sb")[0]
    g_gt = _wgrad(one(h_mix), one(dgates), name="wgrad_in_gates")[0]
    gw["w_in"] = jnp.concatenate([g_fox, g_fl[:, 0:N_HEADS], g_sb, g_gt], axis=1)
    gw["w_branch_fox"] = _wgrad(one(y_fox), one(d_of), name="wgrad_branch_fox")[0]
    gw["w_branch_sb"] = _wgrad(one(y_sb), one(d_os), name="wgrad_branch_sb")[0]
    gw["w_out"] = _wgrad(one(merged), one(d2_bf), name="wgrad_out")[0]
    gw["w_ple_gate"] = _wgrad(one(hn_ple), one(du_ple), name="wgrad_ple_gate")[0]
    gw["w_ple_proj"] = _wgrad(one(p), one(dt_ple), name="wgrad_ple_proj")[0]

    fold = lambda t: jnp.sum(t[0:1].reshape(N_HEADS, HEAD_DIM), axis=0, keepdims=True)
    gs = {
        "ffn1_norm": dg_ffn1[0:1], "mix_norm": dg_mix[0:1], "ffn2_norm": dg_ffn2[0:1], "ple_norm": dg_ple[0:1],
        "forget_bias": dbias8[0:1, 0:N_HEADS], "q_norm": fold(dqn8), "k_norm": fold(dkn8),
    }
    return loss_sum, grad_x, gw, gs


def _position():
    return lax.axis_index("x"), lax.axis_index("y"), lax.axis_index("c")


def _other_chips(x, y):
    return [(1 - x, y), (x, 1 - y), (1 - x, 1 - y)]


ANY = pl.BlockSpec(memory_space=pl.ANY)


def _allgather_weights(shards):
    n = len(shards)

    def body(*refs):
        ins, outs = refs[0:n], refs[n:2 * n]
        send_sems, recv_sems, local_sems = refs[2 * n:]
        x, y, c = _position()
        q = 2 * x + y
        chips = _other_chips(x, y)
        sibling = (x, y, 1 - c)

        def half(a, slot, which):
            r2 = shards[a].shape[0] // 2
            return outs[a].at[slot, pl.ds(which * r2, r2), :]

        def copy(a, k, src, dst, to):
            return pltpu.make_async_remote_copy(
                src_ref=src, dst_ref=dst, send_sem=send_sems.at[6 * a + k], recv_sem=recv_sems.at[6 * a + k],
                device_id=to, device_id_type=MESH)

        local, sent = [], []
        for a in range(n):
            r2 = shards[a].shape[0] // 2
            mine = pltpu.make_async_copy(ins[a], outs[a].at[q], local_sems.at[a])
            mine.start()
            local.append(mine)
            for k, (tx, ty) in enumerate(chips):
                cp = copy(a, k, ins[a].at[pl.ds(c * r2, r2), :], half(a, q, c), (tx, ty, c))
                cp.start()
                sent.append(cp)
        for a in range(n):
            for k, (tx, ty) in enumerate(chips):
                landed = half(a, 2 * tx + ty, c)
                copy(a, k, landed, landed, (tx, ty, c)).wait_recv()
                fwd = copy(a, 3 + k, landed, landed, sibling)
                fwd.start()
                sent.append(fwd)
        for a in range(n):
            for k, (tx, ty) in enumerate(chips):
                other = half(a, 2 * tx + ty, 1 - c)
                copy(a, 3 + k, other, other, sibling).wait_recv()
        for cp in sent:
            cp.wait_send()
        for cp in local:
            cp.wait()

    return pl.pallas_call(
        body,
        in_specs=[ANY] * n,
        out_specs=[ANY] * n,
        out_shape=[jax.ShapeDtypeStruct((N_CHIPS,) + s.shape, s.dtype) for s in shards],
        scratch_shapes=[pltpu.SemaphoreType.DMA((6 * n,)), pltpu.SemaphoreType.DMA((6 * n,)),
                        pltpu.SemaphoreType.DMA((n,))],
        name="allgather_weights",
    )(*shards)


def _exchange_pair_halves(grads):
    n = len(grads)

    def body(*refs):
        ins, outs = refs[0:n], refs[n:2 * n]
        send_sems, recv_sems = refs[2 * n:]
        x, y, c = _position()
        copies = []
        for a in range(n):
            r2 = grads[a].shape[1] // 2
            cp = pltpu.make_async_remote_copy(
                src_ref=ins[a].at[:, pl.ds((1 - c) * r2, r2), :], dst_ref=outs[a],
                send_sem=send_sems.at[a], recv_sem=recv_sems.at[a], device_id=(x, y, 1 - c), device_id_type=MESH)
            cp.start()
            copies.append(cp)
        for cp in copies:
            cp.wait()

    return pl.pallas_call(
        body,
        in_specs=[ANY] * n,
        out_specs=[ANY] * n,
        out_shape=[jax.ShapeDtypeStruct((N_CHIPS, g.shape[1] // 2, g.shape[2]), g.dtype) for g in grads],
        scratch_shapes=[pltpu.SemaphoreType.DMA((n,)), pltpu.SemaphoreType.DMA((n,))],
        name="rs_pair_exchange",
    )(*grads)


def _scatter_to_owner_chips(pairs):
    n = len(pairs)

    def body(*refs):
        ins, outs = refs[0:n], refs[n:2 * n]
        send_sems, recv_sems, local_sems = refs[2 * n:]
        x, y, c = _position()
        q = 2 * x + y
        chips = _other_chips(x, y)
        started = []
        for a in range(n):
            mine = pltpu.make_async_copy(ins[a].at[q], outs[a].at[q], local_sems.at[a])
            mine.start()
            started.append(mine)
            for k, (tx, ty) in enumerate(chips):
                cp = pltpu.make_async_remote_copy(
                    src_ref=ins[a].at[2 * tx + ty], dst_ref=outs[a].at[q],
                    send_sem=send_sems.at[3 * a + k], recv_sem=recv_sems.at[3 * a + k],
                    device_id=(tx, ty, c), device_id_type=MESH)
                cp.start()
                started.append(cp)
        for cp in started:
            cp.wait()

    return pl.pallas_call(
        body,
        in_specs=[ANY] * n,
        out_specs=[ANY] * n,
        out_shape=[jax.ShapeDtypeStruct(p.shape, p.dtype) for p in pairs],
        scratch_shapes=[pltpu.SemaphoreType.DMA((3 * n,)), pltpu.SemaphoreType.DMA((3 * n,)),
                        pltpu.SemaphoreType.DMA((n,))],
        name="rs_scatter",
    )(*pairs)


def _join_halves(totals):
    n = len(totals)

    def body(*refs):
        ins, outs = refs[0:n], refs[n:2 * n]
        send_sems, recv_sems, local_sems = refs[2 * n:]
        x, y, c = _position()
        started = []
        for a in range(n):
            r2 = totals[a].shape[0]
            place = outs[a].at[pl.ds(c * r2, r2), :]
            mine = pltpu.make_async_copy(ins[a], place, local_sems.at[a])
            mine.start()
            cp = pltpu.make_async_remote_copy(
                src_ref=ins[a], dst_ref=place, send_sem=send_sems.at[a], recv_sem=recv_sems.at[a],
                device_id=(x, y, 1 - c), device_id_type=MESH)
            cp.start()
            started += [mine, cp]
        for cp in started:
            cp.wait()

    return pl.pallas_call(
        body,
        in_specs=[ANY] * n,
        out_specs=[ANY] * n,
        out_shape=[jax.ShapeDtypeStruct((2 * t.shape[0], t.shape[1]), t.dtype) for t in totals],
        scratch_shapes=[pltpu.SemaphoreType.DMA((n,)), pltpu.SemaphoreType.DMA((n,)),
                        pltpu.SemaphoreType.DMA((n,))],
        name="rs_join_halves",
    )(*totals)


def _add_pair(g, got, c):
    _, r2, cols = got.shape

    def body(c_ref, g_ref, got_ref, o_ref):
        o_ref[...] = (g_ref[...].astype(F32) + got_ref[...].astype(F32)).astype(BF16)

    spec = pl.BlockSpec((1, r2, cols), lambda s, c_ref: (s, 0, 0))
    return pl.pallas_call(
        body,
        grid_spec=pltpu.PrefetchScalarGridSpec(
            num_scalar_prefetch=1,
            grid=(N_CHIPS,),
            in_specs=[pl.BlockSpec((1, r2, cols), lambda s, c_ref: (s, c_ref[0], 0)), spec],
            out_specs=spec,
        ),
        out_shape=jax.ShapeDtypeStruct(got.shape, BF16),
        name="rs_add_pair",
    )(c, g, got)


def _add_chips(parts):
    _, r2, cols = parts.shape

    def body(p0, p1, p2, p3, o_ref):
        o_ref[...] = ((p0[0].astype(F32) + p1[0].astype(F32)) + p2[0].astype(F32)) + p3[0].astype(F32)

    specs = [pl.BlockSpec((1, r2, cols), functools.partial(lambda i, s: (s, 0, 0), s=s)) for s in range(N_CHIPS)]
    return pl.pallas_call(
        body,
        grid=(1,),
        in_specs=specs,
        out_specs=pl.BlockSpec((r2, cols), lambda i: (0, 0)),
        out_shape=jax.ShapeDtypeStruct((r2, cols), F32),
        name="rs_add_chips",
    )(parts, parts, parts, parts)


def _allreduce_small(part):
    shape = part.shape

    def body(in_ref, out_ref, gather_ref, send_sems, recv_sems):
        x, y, c = _position()
        me = 4 * x + 2 * y + c
        relations = [(a, b, d) for a in (0, 1) for b in (0, 1) for d in (0, 1)][1:]
        flip = lambda v, f: 1 - v if f else v
        copies = []
        for k, (a, b, d) in enumerate(relations):
            cp = pltpu.make_async_remote_copy(
                src_ref=in_ref, dst_ref=gather_ref.at[me], send_sem=send_sems.at[k], recv_sem=recv_sems.at[k],
                device_id=(flip(x, a), flip(y, b), flip(c, d)), device_id_type=MESH)
            cp.start()
            copies.append(cp)
        gather_ref[me] = in_ref[...]
        for cp in copies:
            cp.wait()
        total = gather_ref[0]
        for dev in range(1, 8):
            total = total + gather_ref[dev]
        out_ref[...] = total

    vmem = pl.BlockSpec(memory_space=pltpu.VMEM)
    return pl.pallas_call(
        body,
        in_specs=[vmem],
        out_specs=vmem,
        out_shape=jax.ShapeDtypeStruct(shape, F32),
        scratch_shapes=[pltpu.VMEM((8,) + shape, F32), pltpu.SemaphoreType.DMA((7,)), pltpu.SemaphoreType.DMA((7,))],
        name="allreduce_small",
    )(part)


def _adamw(w, g, m, v):
    rows, cols = w.shape
    tr = _row_block(rows, cols * 4, budget=MIB)
    c1 = 1.0 / (1.0 - ADAM_B1 ** ADAM_STEP)
    c2 = 1.0 / (1.0 - ADAM_B2 ** ADAM_STEP)

    def body(w_ref, g_ref, m_ref, v_ref, d_ref, nm_ref, nv_ref):
        g_ = g_ref[...]
        nm = ADAM_B1 * m_ref[...] + (1.0 - ADAM_B1) * g_
        nv = ADAM_B2 * v_ref[...] + (1.0 - ADAM_B2) * (g_ * g_)
        nm_ref[...] = nm
        nv_ref[...] = nv
        d_ref[...] = -ADAM_LR * ((nm * c1) / (jnp.sqrt(nv * c2) + ADAM_EPS) + ADAM_WD * w_ref[...])

    spec = pl.BlockSpec((tr, cols), lambda i: (i, 0))
    out = jax.ShapeDtypeStruct((rows, cols), F32)
    return pl.pallas_call(
        body,
        grid=(rows // tr,),
        in_specs=[spec] * 4,
        out_specs=[spec] * 3,
        out_shape=[out] * 3,
        name="adamw",
    )(w, g, m, v)


BIG = ["ffn1_w_gate", "ffn1_w_up", "ffn1_w_down", "w_in", "w_branch_fox", "w_branch_sb", "w_out",
       "ffn2_w_gate", "ffn2_w_up", "ffn2_w_down", "w_ple_gate", "w_ple_proj"]
SMALL = ["ffn1_norm", "mix_norm", "ffn2_norm", "ple_norm", "forget_bias", "q_norm", "k_norm"]
COLUMN_SHARDED = ["ffn1_w_gate", "ffn1_w_up", "w_in", "w_branch_fox", "w_branch_sb",
                  "ffn2_w_gate", "ffn2_w_up", "w_ple_proj"]
KEPT_AS_SHARDS = ["ffn1_w_gate", "ffn1_w_up", "ffn1_w_down", "ffn2_w_gate", "ffn2_w_up", "ffn2_w_down"]
ORDER = ["ffn1_norm", "ffn1_w_gate", "ffn1_w_up", "ffn1_w_down", "mix_norm", "w_in", "forget_bias", "q_norm",
         "k_norm", "w_branch_fox", "w_branch_sb", "w_out", "ffn2_norm", "ffn2_w_gate", "ffn2_w_up",
         "ffn2_w_down", "ple_norm", "w_ple_gate", "w_ple_proj"]
SMALL_ROWS = {"ffn1_norm": 0, "mix_norm": 1, "ffn2_norm": 2, "ple_norm": 3}
SMALL_COLS = {"forget_bias": (0, N_HEADS), "q_norm": (N_HEADS, HEAD_DIM), "k_norm": (N_HEADS + HEAD_DIM, HEAD_DIM)}
LOSS_ROW = 5


def _whole(name, gathered):
    if name in COLUMN_SHARDED:
        return jnp.concatenate([gathered[s] for s in range(N_CHIPS)], axis=1)
    return gathered.reshape(-1, gathered.shape[-1])


def _as_shards(name, whole):
    if name in COLUMN_SHARDED:
        k, n = whole.shape
        return whole.reshape(k, N_CHIPS, n // N_CHIPS).transpose(1, 0, 2)
    return whole.reshape(N_CHIPS, whole.shape[0] // N_CHIPS, whole.shape[1])


def _pack_small(values, extra=None):
    rows = [values[k] for k in ("ffn1_norm", "mix_norm", "ffn2_norm", "ple_norm")]
    tail = jnp.concatenate([values["forget_bias"], values["q_norm"], values["k_norm"]], axis=1)
    rows.append(jnp.pad(tail, ((0, 0), (0, D_MODEL - tail.shape[1]))))
    packed = jnp.concatenate(rows + [jnp.zeros((3, D_MODEL), F32)], axis=0)
    if extra is not None:
        packed = packed.at[LOSS_ROW, 0].set(extra)
    return packed


def _unpack_small(packed):
    out = {k: packed[r:r + 1] for k, r in SMALL_ROWS.items()}
    for k, (start, size) in SMALL_COLS.items():
        out[k] = packed[4:5, start:start + size]
    return out


def kernel(x, p, ffn1_norm, ffn1_w_gate, ffn1_w_up, ffn1_w_down, mix_norm, w_in, forget_bias, q_norm, k_norm, w_branch_fox, w_branch_sb, w_out, ffn2_norm, ffn2_w_gate, ffn2_w_up, ffn2_w_down, ple_norm, w_ple_gate, w_ple_proj, loss_target, m_ffn1_norm, m_ffn1_w_gate, m_ffn1_w_up, m_ffn1_w_down, m_mix_norm, m_w_in, m_forget_bias, m_q_norm, m_k_norm, m_w_branch_fox, m_w_branch_sb, m_w_out, m_ffn2_norm, m_ffn2_w_gate, m_ffn2_w_up, m_ffn2_w_down, m_ple_norm, m_w_ple_gate, m_w_ple_proj, v_ffn1_norm, v_ffn1_w_gate, v_ffn1_w_up, v_ffn1_w_down, v_mix_norm, v_w_in, v_forget_bias, v_q_norm, v_k_norm, v_w_branch_fox, v_w_branch_sb, v_w_out, v_ffn2_norm, v_ffn2_w_gate, v_ffn2_w_up, v_ffn2_w_down, v_ple_norm, v_w_ple_gate, v_w_ple_proj):
    args = dict(locals())
    weights = {k: args[k] for k in ORDER}
    moments_m = {k: args["m_" + k] for k in ORDER}
    moments_v = {k: args["v_" + k] for k in ORDER}

    gathered = _allgather_weights([weights[k][0].astype(BF16) for k in BIG])
    full = {}
    for k, gth in zip(BIG, gathered):
        full[k] = gth if k in KEPT_AS_SHARDS else _whole(k, gth)
    small = {k: weights[k] for k in SMALL}

    loss_sum, grad_x, gw, gs = _local_grads(x[0], p[0, 0], loss_target[0], small, full)

    c_idx = lax.axis_index("c").astype(jnp.int32).reshape(1)
    slots = [gw[k] if k in KEPT_AS_SHARDS else _as_shards(k, gw[k]) for k in BIG]
    from_core = _exchange_pair_halves(slots)
    pairs = [_add_pair(g, got, c_idx) for g, got in zip(slots, from_core)]
    parts = _scatter_to_owner_chips(pairs)
    totals = [_add_chips(t) for t in parts]
    grads_big = dict(zip(BIG, _join_halves(totals)))
    reduced = _allreduce_small(_pack_small(gs, extra=loss_sum[0, 0]))
    grads_small = _unpack_small(reduced)
    loss = reduced[LOSS_ROW, 0]

    grads, deltas, new_m, new_v = {}, {}, {}, {}
    for k in BIG:
        grads[k] = grads_big[k][None]
        d, nm, nv = _adamw(weights[k][0], grads_big[k], moments_m[k][0], moments_v[k][0])
        deltas[k], new_m[k], new_v[k] = d[None], nm[None], nv[None]
    d_s, nm_s, nv_s = _adamw(_pack_small({k: weights[k] for k in SMALL}), reduced,
                             _pack_small({k: moments_m[k] for k in SMALL}),
                             _pack_small({k: moments_v[k] for k in SMALL}))
    for k in SMALL:
        grads[k] = grads_small[k]
    for name, packed in (("d", d_s), ("m", nm_s), ("v", nv_s)):
        target = {"d": deltas, "m": new_m, "v": new_v}[name]
        target.update(_unpack_small(packed))

    return (loss, grad_x[None], *[grads[k] for k in ORDER], *[deltas[k] for k in ORDER],
            *[new_m[k] for k in ORDER], *[new_v[k] for k in ORDER])
```

```python
import functools

import jax
import jax.numpy as jnp
from jax import lax
from jax.experimental import pallas as pl
from jax.experimental.pallas import tpu as pltpu

F32 = jnp.float32
BF16 = jnp.bfloat16

D_MODEL = 1024
D_FF = 2816
N_CHIPS = 4
FF_SHARD = D_FF // N_CHIPS
HEAD_DIM = 64
N_HEADS = 8
ATT_W = N_HEADS * HEAD_DIM
PAIR_W = 2 * HEAD_DIM
N_PAIRS = N_HEADS // 2
PLE_DIM = 256
IN_WIDTH = 3 * ATT_W + N_HEADS + 3 * ATT_W + 2 * D_MODEL
EPS = 1e-6
QK_SCALE = HEAD_DIM ** -0.5
LANES = 128
ATT_BLOCK = 256
ATT_Q_BLOCK = 512
NEG_BIG = -1e30

ADAM_LR = 0.001
ADAM_B1 = 0.9
ADAM_B2 = 0.999
ADAM_EPS = 1e-08
ADAM_WD = 0.01
ADAM_STEP = 10

MESH = pl.DeviceIdType.MESH
MIB = 1024 * 1024


def _cparams(vmem_mib=48):
    return pltpu.CompilerParams(vmem_limit_bytes=vmem_mib * MIB)


def _dot(a, b):
    return jnp.dot(a, b, preferred_element_type=F32)


def _dot_tn(a, b):
    return lax.dot_general(a, b, (((0,), (0,)), ((), ())), preferred_element_type=F32)


def _sigmoid(x):
    return 1.0 / (1.0 + jnp.exp(-x))


def _split2(x):
    hi = x.astype(BF16)
    lo = (x - hi.astype(F32)).astype(BF16)
    return hi, lo


def _dot_split2(x, m):
    hi, lo = _split2(x)
    return _dot(hi, m) + _dot(lo, m)


def _split3(x):
    hi = x.astype(BF16)
    rest = x - hi.astype(F32)
    mid = rest.astype(BF16)
    lo = (rest - mid.astype(F32)).astype(BF16)
    return hi, mid, lo


def _rms(x):
    r = lax.rsqrt(jnp.mean(x * x, axis=-1, keepdims=True) + EPS)
    return x * r, r


def _rms_bwd(dh, xn, r, g):
    dxn = dh * g
    return r * (dxn - xn * jnp.mean(dxn * xn, axis=-1, keepdims=True))


def _colsum(x):
    return jnp.sum(x, axis=0, keepdims=True)


def _row_block(rows, row_bytes, budget):
    best = None
    for t in range(8, rows + 1, 8):
        if rows % t == 0 and t * row_bytes <= budget:
            best = t
    return best if best is not None else rows


def _ffn_fwd(x, g, wg, wu, wd, tm=512):
    s_len = x.shape[0]

    def body(x_ref, g_ref, wg_ref, wu_ref, wd_ref, o_ref, h_s, acc_s):
        j = pl.program_id(1)

        @pl.when(j == 0)
        def _():
            xn, _ = _rms(x_ref[...])
            h_s[...] = (xn * g_ref[...]).astype(BF16)
            acc_s[...] = jnp.zeros_like(acc_s)

        h = h_s[...]
        a = _dot(h, wg_ref[0])
        b = _dot(h, wu_ref[0])
        u = (a * _sigmoid(a) * b).astype(BF16)
        acc_s[...] += _dot(u, wd_ref[0])

        @pl.when(j == N_CHIPS - 1)
        def _():
            o_ref[...] = x_ref[...] + 0.5 * acc_s[...]

    return pl.pallas_call(
        body,
        grid=(s_len // tm, N_CHIPS),
        in_specs=[
            pl.BlockSpec((tm, D_MODEL), lambda i, j: (i, 0)),
            pl.BlockSpec((1, D_MODEL), lambda i, j: (0, 0)),
            pl.BlockSpec((1, D_MODEL, FF_SHARD), lambda i, j: (j, 0, 0)),
            pl.BlockSpec((1, D_MODEL, FF_SHARD), lambda i, j: (j, 0, 0)),
            pl.BlockSpec((1, FF_SHARD, D_MODEL), lambda i, j: (j, 0, 0)),
        ],
        out_specs=pl.BlockSpec((tm, D_MODEL), lambda i, j: (i, 0)),
        out_shape=jax.ShapeDtypeStruct((s_len, D_MODEL), F32),
        scratch_shapes=[pltpu.VMEM((tm, D_MODEL), BF16), pltpu.VMEM((tm, D_MODEL), F32)],
        compiler_params=_cparams(48),
        name="ffn_fwd",
    )(x, g, wg, wu, wd)


def _ffn_bwd(x, d, g, wg, wu, wd_t, wg_t, wu_t, tm=512):
    s_len = x.shape[0]
    nb = s_len // tm

    def body(x_ref, d_ref, g_ref, wg_ref, wu_ref, wdt_ref, wgt_ref, wut_ref,
             dx_ref, u_ref, da_ref, db_ref, h_ref, dbf_ref, dg_ref, h_s, dbf_s, dh_s):
        i = pl.program_id(0)
        j = pl.program_id(1)

        @pl.when(j == 0)
        def _():
            xn, _ = _rms(x_ref[...])
            h = (xn * g_ref[...]).astype(BF16)
            h_s[...] = h
            h_ref[...] = h
            dbf = d_ref[...].astype(BF16)
            dbf_s[...] = dbf
            dbf_ref[...] = dbf
            dh_s[...] = jnp.zeros_like(dh_s)

        @pl.when((i == 0) & (j == 0))
        def _():
            dg_ref[...] = jnp.zeros_like(dg_ref)

        h = h_s[...]
        a = _dot(h, wg_ref[0])
        b = _dot(h, wu_ref[0])
        du = 0.5 * _dot(dbf_s[...], wdt_ref[0])
        s = _sigmoid(a)
        silu = a * s
        da = (du * b * (s * (1.0 + a * (1.0 - s)))).astype(BF16)
        db = (du * silu).astype(BF16)
        u_ref[0] = (silu * b).astype(BF16)
        da_ref[0] = da
        db_ref[0] = db
        dh_s[...] += _dot(da, wgt_ref[0]) + _dot(db, wut_ref[0])

        @pl.when(j == N_CHIPS - 1)
        def _():
            xn, r = _rms(x_ref[...])
            dh = dh_s[...]
            dx_ref[...] = d_ref[...] + _rms_bwd(dh, xn, r, g_ref[...])
            dg_ref[0:1, :] += _colsum(dh * xn)

    row = lambda i, j: (i, 0)
    shard = lambda i, j: (j, 0, 0)
    act = lambda i, j: (j, i, 0)
    return pl.pallas_call(
        body,
        grid=(nb, N_CHIPS),
        in_specs=[
            pl.BlockSpec((tm, D_MODEL), row),
            pl.BlockSpec((tm, D_MODEL), row),
            pl.BlockSpec((1, D_MODEL), lambda i, j: (0, 0)),
            pl.BlockSpec((1, D_MODEL, FF_SHARD), shard),
            pl.BlockSpec((1, D_MODEL, FF_SHARD), shard),
            pl.BlockSpec((1, D_MODEL, FF_SHARD), shard),
            pl.BlockSpec((1, FF_SHARD, D_MODEL), shard),
            pl.BlockSpec((1, FF_SHARD, D_MODEL), shard),
        ],
        out_specs=[
            pl.BlockSpec((tm, D_MODEL), row),
            pl.BlockSpec((1, tm, FF_SHARD), act),
            pl.BlockSpec((1, tm, FF_SHARD), act),
            pl.BlockSpec((1, tm, FF_SHARD), act),
            pl.BlockSpec((tm, D_MODEL), row),
            pl.BlockSpec((tm, D_MODEL), row),
            pl.BlockSpec((8, D_MODEL), lambda i, j: (0, 0)),
        ],
        out_shape=[
            jax.ShapeDtypeStruct((s_len, D_MODEL), F32),
            jax.ShapeDtypeStruct((N_CHIPS, s_len, FF_SHARD), BF16),
            jax.ShapeDtypeStruct((N_CHIPS, s_len, FF_SHARD), BF16),
            jax.ShapeDtypeStruct((N_CHIPS, s_len, FF_SHARD), BF16),
            jax.ShapeDtypeStruct((s_len, D_MODEL), BF16),
            jax.ShapeDtypeStruct((s_len, D_MODEL), BF16),
            jax.ShapeDtypeStruct((8, D_MODEL), F32),
        ],
        scratch_shapes=[
            pltpu.VMEM((tm, D_MODEL), BF16),
            pltpu.VMEM((tm, D_MODEL), BF16),
            pltpu.VMEM((tm, D_MODEL), F32),
        ],
        compiler_params=_cparams(56),
        name="ffn_bwd",
    )(x, d, g, wg, wu, wd_t, wg_t, wu_t)


def _wgrad(a, b, scale=1.0, name="wgrad"):
    na, s_len, k_dim = a.shape
    nb, _, n_dim = b.shape
    n = max(na, nb)
    ts = min(s_len, 1024)
    steps = s_len // ts

    def body(a_ref, b_ref, o_ref, acc_s):
        s = pl.program_id(1)

        @pl.when(s == 0)
        def _():
            acc_s[...] = jnp.zeros_like(acc_s)

        acc_s[...] += _dot_tn(a_ref[0].astype(BF16), b_ref[0].astype(BF16))

        @pl.when(s == steps - 1)
        def _():
            o_ref[0] = (acc_s[...] * scale).astype(BF16)

    a_map = (lambda m, s: (m, s, 0)) if na > 1 else (lambda m, s: (0, s, 0))
    b_map = (lambda m, s: (m, s, 0)) if nb > 1 else (lambda m, s: (0, s, 0))
    return pl.pallas_call(
        body,
        grid=(n, steps),
        in_specs=[pl.BlockSpec((1, ts, k_dim), a_map), pl.BlockSpec((1, ts, n_dim), b_map)],
        out_specs=pl.BlockSpec((1, k_dim, n_dim), lambda m, s: (m, 0, 0)),
        out_shape=jax.ShapeDtypeStruct((n, k_dim, n_dim), BF16),
        scratch_shapes=[pltpu.VMEM((k_dim, n_dim), F32)],
        compiler_params=_cparams(56),
        name=name,
    )(a, b)


def _head_sum_matrices():
    lane = lax.broadcasted_iota(jnp.int32, (ATT_W, LANES), 0) // HEAD_DIM
    col = lax.broadcasted_iota(jnp.int32, (ATT_W, LANES), 1)
    bd = (lane == col).astype(BF16)
    return bd, bd.T


def _head_mean(t, bd, bd_t):
    per_head = _dot_split2(t, bd) * (1.0 / HEAD_DIM)
    return _dot_split2(per_head, bd_t)


def _head_rms(x, bd, bd_t):
    per_head = _dot_split2(x * x, bd) * (1.0 / HEAD_DIM)
    r = lax.rsqrt(per_head + EPS)
    rw = _dot_split2(r, bd_t)
    return x * rw, rw


def _log_sigmoid(z):
    return jnp.minimum(z, 0.0) - jnp.log(1.0 + jnp.exp(-jnp.abs(z)))


def _inproj_fwd(x1, g, w_fox, w_fl, w_sb, w_gates, bias, qn, kn, bd, bd_t, tm=256):
    s_len = x1.shape[0]

    def body(x_ref, g_ref, wf_ref, wl_ref, ws_ref, wg_ref, bias_ref, qn_ref, kn_ref, bd_ref, bdt_ref,
             fq_ref, fk_ref, qs_ref, kf_ref, vf_ref, logf_ref, sq_ref, sk_ref, sv_ref, gates_ref):
        xn, _ = _rms(x_ref[...])
        h = (xn * g_ref[...]).astype(BF16)
        zf = _dot(h, wf_ref[...])
        fq = zf[:, 0:ATT_W]
        fk = zf[:, ATT_W:2 * ATT_W]
        fq_ref[...] = fq
        fk_ref[...] = fk
        bd_m = bd_ref[...]
        bdt_m = bdt_ref[...]
        fqn, _ = _head_rms(fq, bd_m, bdt_m)
        fkn, _ = _head_rms(fk, bd_m, bdt_m)
        qs_ref[...] = (fqn * qn_ref[...]).astype(BF16) * QK_SCALE
        kf_ref[...] = (fkn * kn_ref[...]).astype(BF16)
        vf_ref[...] = zf[:, 2 * ATT_W:3 * ATT_W].astype(BF16)
        logf_ref[...] = _log_sigmoid(_dot(h, wl_ref[...]) + bias_ref[...])
        zs = _dot(h, ws_ref[...])
        sq_ref[...] = zs[:, 0:ATT_W].astype(BF16) * QK_SCALE
        sk_ref[...] = zs[:, ATT_W:2 * ATT_W].astype(BF16)
        sv_ref[...] = zs[:, 2 * ATT_W:3 * ATT_W].astype(BF16)
        gates_ref[...] = _dot(h, wg_ref[...])

    row = lambda i: (i, 0)
    full = lambda i: (0, 0)
    att = lambda dt: jax.ShapeDtypeStruct((s_len, ATT_W), dt)
    return pl.pallas_call(
        body,
        grid=(s_len // tm,),
        in_specs=[
            pl.BlockSpec((tm, D_MODEL), row),
            pl.BlockSpec((1, D_MODEL), full),
            pl.BlockSpec((D_MODEL, 3 * ATT_W), full),
            pl.BlockSpec((D_MODEL, LANES), full),
            pl.BlockSpec((D_MODEL, 3 * ATT_W), full),
            pl.BlockSpec((D_MODEL, 2 * D_MODEL), full),
            pl.BlockSpec((1, LANES), full),
            pl.BlockSpec((1, ATT_W), full),
            pl.BlockSpec((1, ATT_W), full),
            pl.BlockSpec((ATT_W, LANES), full),
            pl.BlockSpec((LANES, ATT_W), full),
        ],
        out_specs=[
            pl.BlockSpec((tm, ATT_W), row), pl.BlockSpec((tm, ATT_W), row),
            pl.BlockSpec((tm, ATT_W), row), pl.BlockSpec((tm, ATT_W), row), pl.BlockSpec((tm, ATT_W), row),
            pl.BlockSpec((tm, LANES), row),
            pl.BlockSpec((tm, ATT_W), row), pl.BlockSpec((tm, ATT_W), row), pl.BlockSpec((tm, ATT_W), row),
            pl.BlockSpec((tm, 2 * D_MODEL), row),
        ],
        out_shape=[
            att(F32), att(F32), att(BF16), att(BF16), att(BF16),
            jax.ShapeDtypeStruct((s_len, LANES), F32),
            att(BF16), att(BF16), att(BF16),
            jax.ShapeDtypeStruct((s_len, 2 * D_MODEL), F32),
        ],
        compiler_params=_cparams(56),
        name="inproj_fwd",
    )(x1, g, w_fox, w_fl, w_sb, w_gates, bias, qn, kn, bd, bd_t)


def _tri(n, kind):
    r = lax.broadcasted_iota(jnp.int32, (n, n), 0)
    c = lax.broadcasted_iota(jnp.int32, (n, n), 1)
    m = {"row_ge_col": r >= c, "row_le_col": r <= c, "row_gt_col": r > c, "row_lt_col": r < c}[kind]
    return m.astype(BF16)


def _cumsum_rows(x, reverse, tm=256):
    s_len = x.shape[0]
    nb = s_len // tm
    tri = _tri(tm, "row_le_col" if reverse else "row_ge_col")
    edge = 0 if reverse else tm - 1

    def body(x_ref, tri_ref, o_ref, carry_s):
        @pl.when(pl.program_id(0) == 0)
        def _():
            carry_s[...] = jnp.zeros_like(carry_s)

        hi, mid, lo = _split3(x_ref[...])
        t = tri_ref[...]
        y = _dot(t, hi) + _dot(t, mid) + _dot(t, lo) + carry_s[...]
        o_ref[...] = y
        carry_s[...] = y[edge:edge + 1, :]

    order = (lambda i: (nb - 1 - i, 0)) if reverse else (lambda i: (i, 0))
    return pl.pallas_call(
        body,
        grid=(nb,),
        in_specs=[pl.BlockSpec((tm, LANES), order), pl.BlockSpec((tm, tm), lambda i: (0, 0))],
        out_specs=pl.BlockSpec((tm, LANES), order),
        out_shape=jax.ShapeDtypeStruct((s_len, LANES), F32),
        scratch_shapes=[pltpu.VMEM((1, LANES), F32)],
        name="cumsum_rev" if reverse else "cumsum_fwd",
    )(x, tri)


def _blocked_t(t, blk):
    nb = t.shape[0] // blk
    return t.reshape(nb, blk, N_PAIRS, PAIR_W).transpose(2, 0, 3, 1)


def _unblocked_t(t4):
    _, nb, _, blk = t4.shape
    return t4.transpose(1, 3, 0, 2).reshape(nb * blk, ATT_W)


def _blocked_rows(t, blk):
    return t.reshape(t.shape[0] // blk, blk, t.shape[1])


def _pair_rows_t(f8, blk):
    nb = f8.shape[0] // blk
    t = f8.reshape(nb, blk, N_PAIRS, 2).transpose(2, 0, 3, 1)
    return jnp.pad(t, ((0, 0), (0, 0), (0, 6), (0, 0)))


def _unpair_rows_t(t4):
    _, nb, _, blk = t4.shape
    return t4[:, :, 0:2, :].transpose(1, 3, 0, 2).reshape(nb * blk, N_HEADS)


def _head_masks(tq):
    lane = lax.broadcasted_iota(jnp.int32, (tq, PAIR_W), 1)
    return lane < HEAD_DIM


def _causal_mask(tq, tk, offset, strict):
    d = lax.broadcasted_iota(jnp.int32, (tq, tk), 1) - lax.broadcasted_iota(jnp.int32, (tq, tk), 0)
    return (d < offset) if strict else (d <= offset)


def _heads_of(ref, first):
    t = ref[...]
    zero = jnp.zeros_like(t)
    return [jnp.where(first, t, zero), jnp.where(first, zero, t)]


def _head_cols(ref):
    t = ref[...]
    return [t[:, 0:1], t[:, HEAD_DIM:HEAD_DIM + 1]]


def _att_specs(s_len):
    tq, tk = ATT_Q_BLOCK, ATT_BLOCK
    nq, nk = s_len // tq, s_len // tk
    return dict(
        nq=nq,
        q=pl.BlockSpec((tq, PAIR_W), lambda p, i: (i, p)),
        q_t=pl.BlockSpec((1, 1, PAIR_W, tq), lambda p, i: (p, i, 0, 0)),
        k_t=pl.BlockSpec((1, nk, PAIR_W, tk), lambda p, i: (p, 0, 0, 0)),
        k_rows=pl.BlockSpec((nk, tk, PAIR_W), lambda p, i: (0, 0, p)),
        f_t=pl.BlockSpec((1, nk, 8, tk), lambda p, i: (p, 0, 0, 0)),
        wide=jax.ShapeDtypeStruct((s_len, ATT_W), F32),
        k_t_out=jax.ShapeDtypeStruct((N_PAIRS, nk, PAIR_W, tk), F32),
        f_t_out=jax.ShapeDtypeStruct((N_PAIRS, nk, 8, tk), F32),
        acc=pltpu.VMEM((2, tq, PAIR_W), F32),
    )


def _fox_fwd(qs, kt4, v3, fw, ft4):
    sp = _att_specs(qs.shape[0])
    tq, tk = ATT_Q_BLOCK, ATT_BLOCK
    ratio = tq // tk

    def body(q_ref, kt_ref, v_ref, fw_ref, ft_ref, y_ref, lse_ref, acc_ref):
        i = pl.program_id(1)
        first = _head_masks(tq)
        qh = _heads_of(q_ref, first)
        fqh = _head_cols(fw_ref)
        acc_ref[...] = jnp.zeros_like(acc_ref)

        def block(j, carry, diag):
            mask = _causal_mask(tq, tk, i * tq - j * tk, strict=False) if diag else None
            kt, v, fk = kt_ref[0, j], v_ref[j], ft_ref[0, j]
            logits = [_dot(qh[n], kt) for n in range(2)]
            probs, out = [], []
            for n in range(2):
                m, l = carry[2 * n:2 * n + 2]
                s = logits[n] + (fqh[n] - fk[n:n + 1, :])
                if diag:
                    s = jnp.where(mask, s, NEG_BIG)
                m_new = jnp.maximum(m, jnp.max(s, axis=-1, keepdims=True))
                p = jnp.exp(s - m_new)
                alpha = jnp.exp(m - m_new)
                out += [m_new, alpha * l + jnp.sum(p, axis=-1, keepdims=True)]
                probs.append((p.astype(BF16), alpha))
            for n in range(2):
                acc_ref[n] = probs[n][1] * acc_ref[n] + _dot(probs[n][0], v)
            return tuple(out)

        carry = (jnp.full((tq, 1), NEG_BIG, F32), jnp.zeros((tq, 1), F32)) * 2
        carry = lax.fori_loop(0, ratio * i, lambda j, c: block(j, c, False), carry)
        for d in range(ratio):
            carry = block(ratio * i + d, carry, True)
        m0, l0, m1, l1 = carry
        y_ref[...] = jnp.where(first, acc_ref[0] / l0, acc_ref[1] / l1)
        lse_ref[...] = jnp.where(first, m0 + jnp.log(l0), m1 + jnp.log(l1))

    return pl.pallas_call(
        body,
        grid=(N_PAIRS, sp["nq"]),
        in_specs=[sp["q"], sp["k_t"], sp["k_rows"], sp["q"], sp["f_t"]],
        out_specs=[sp["q"], sp["q"]],
        out_shape=[sp["wide"], sp["wide"]],
        scratch_shapes=[sp["acc"]],
        compiler_params=_cparams(56),
        name="fox_fwd",
    )(qs, kt4, v3, fw, ft4)


def _fox_bwd(qs, qst4, kt4, k3, vt4, dy, dyt4, y, lse, fw, ft4):
    sp = _att_specs(qs.shape[0])
    tq, tk = ATT_Q_BLOCK, ATT_BLOCK
    ratio = tq // tk

    def body(q_ref, qt_ref, kt_ref, k_ref, vt_ref, dy_ref, dyt_ref, y_ref, lse_ref, fw_ref, ft_ref,
             dq_ref, dfq_ref, dkt_ref, dvt_ref, dft_ref, acc_ref):
        i = pl.program_id(1)

        @pl.when(i == 0)
        def _():
            dkt_ref[...] = jnp.zeros_like(dkt_ref)
            dvt_ref[...] = jnp.zeros_like(dvt_ref)
            dft_ref[...] = jnp.zeros_like(dft_ref)

        first = _head_masks(tq)
        first_t = lax.broadcasted_iota(jnp.int32, (PAIR_W, tq), 0) < HEAD_DIM
        qh = _heads_of(q_ref, first)
        qth = _heads_of(qt_ref.at[0, 0], first_t)
        dyth = _heads_of(dyt_ref.at[0, 0], first_t)
        dyv = dy_ref[...]
        dyb = dyv.astype(BF16)
        zero = jnp.zeros_like(dyb)
        dyh = [jnp.where(first, dyb, zero), jnp.where(first, zero, dyb)]
        prod = dyv * y_ref[...]
        zf = jnp.zeros_like(prod)
        delta = [jnp.sum(jnp.where(first, prod, zf), axis=-1, keepdims=True),
                 jnp.sum(jnp.where(first, zf, prod), axis=-1, keepdims=True)]
        fqh = _head_cols(fw_ref)
        lseh = _head_cols(lse_ref)
        shift = [fqh[n] - lseh[n] for n in range(2)]
        acc_ref[...] = jnp.zeros_like(acc_ref)

        def block(j, rows, diag):
            mask = _causal_mask(tq, tk, i * tq - j * tk, strict=False) if diag else None
            kt, k, vt, fk = kt_ref[0, j], k_ref[j], vt_ref[0, j], ft_ref[0, j]
            logits = [_dot(qh[n], kt) for n in range(2)]
            dps = [_dot(dyh[n], vt) for n in range(2)]
            pbs, dsbs, out = [], [], []
            for n in range(2):
                p = jnp.exp(logits[n] + (shift[n] - fk[n:n + 1, :]))
                if diag:
                    p = jnp.where(mask, p, 0.0)
                ds = p * (dps[n] - delta[n])
                pbs.append(p.astype(BF16))
                dsbs.append(ds.astype(BF16))
                out.append(rows[n] + jnp.sum(ds, axis=-1, keepdims=True))
                dft_ref[0, j, n:n + 1, :] -= _colsum(ds)
            for n in range(2):
                acc_ref[n] += _dot(dsbs[n], k)
            dkt_ref[0, j] += _dot(qth[0], dsbs[0]) + _dot(qth[1], dsbs[1])
            dvt_ref[0, j] += _dot(dyth[0], pbs[0]) + _dot(dyth[1], pbs[1])
            return tuple(out)

        rows = (jnp.zeros((tq, 1), F32),) * 2
        rows = lax.fori_loop(0, ratio * i, lambda j, c: block(j, c, False), rows)
        for d in range(ratio):
            rows = block(ratio * i + d, rows, True)
        dq_ref[...] = jnp.where(first, acc_ref[0], acc_ref[1])
        dfq_ref[...] = jnp.where(first, rows[0], rows[1])

    return pl.pallas_call(
        body,
        grid=(N_PAIRS, sp["nq"]),
        in_specs=[sp["q"], sp["q_t"], sp["k_t"], sp["k_rows"], sp["k_t"], sp["q"], sp["q_t"],
                  sp["q"], sp["q"], sp["q"], sp["f_t"]],
        out_specs=[sp["q"], sp["q"], sp["k_t"], sp["k_t"], sp["f_t"]],
        out_shape=[sp["wide"], sp["wide"], sp["k_t_out"], sp["k_t_out"], sp["f_t_out"]],
        scratch_shapes=[sp["acc"]],
        compiler_params=_cparams(56),
        name="fox_bwd",
    )(qs, qst4, kt4, k3, vt4, dy, dyt4, y, lse, fw, ft4)


SIGN_BIT = 0x80000000


def _sb_terms(z, mask, diag):
    neg_abs = pltpu.bitcast(pltpu.bitcast(z, jnp.uint32) | jnp.uint32(SIGN_BIT), F32)
    lb = jnp.minimum(z, 0.0) - jnp.log(1.0 + jnp.exp(neg_abs))
    l1m = lb - z
    if diag:
        l1m = jnp.where(mask, l1m, 0.0)
    return lb, l1m


def _dot_split2_stacked(x, m2):
    hi, lo = _split2(x)
    return _dot(jnp.concatenate([hi, lo], axis=1), m2)


def _tri_stacked(kind):
    t = _tri(ATT_BLOCK, kind)
    return jnp.concatenate([t, t], axis=0)


def _sb_fwd(qs, kt4, v3):
    sp = _att_specs(qs.shape[0])
    tq, tk = ATT_Q_BLOCK, ATT_BLOCK
    ratio = tq // tk
    upper = _tri_stacked("row_gt_col")

    def body(q_ref, kt_ref, v_ref, u_ref, y_ref, rtot_ref, acc_ref):
        i = pl.program_id(1)
        first = _head_masks(tq)
        qh = _heads_of(q_ref, first)
        u = u_ref[...]
        acc_ref[...] = jnp.zeros_like(acc_ref)

        def block(j, rs, diag):
            mask = _causal_mask(tq, tk, i * tq - j * tk, strict=True) if diag else None
            kt, v = kt_ref[0, j], v_ref[j]
            logits = [_dot(qh[n], kt) for n in range(2)]
            terms = [_sb_terms(z, mask, diag) for z in logits]
            right = [_dot_split2_stacked(l1m, u) for _, l1m in terms]
            weights = []
            for n in range(2):
                a = jnp.exp(terms[n][0] + right[n] + rs[n])
                if diag:
                    a = jnp.where(mask, a, 0.0)
                weights.append(a.astype(BF16))
            for n in range(2):
                acc_ref[n] += _dot(weights[n], v)
            return tuple(rs[n] + jnp.sum(terms[n][1], axis=-1, keepdims=True) for n in range(2))

        rs = (jnp.zeros((tq, 1), F32),) * 2
        for d in range(ratio):
            rs = block(ratio * i + (ratio - 1 - d), rs, True)
        rs = lax.fori_loop(0, ratio * i, lambda n, c: block(ratio * i - 1 - n, c, False), rs)
        y_ref[...] = jnp.where(first, acc_ref[0], acc_ref[1])
        rtot_ref[...] = jnp.where(first, rs[0], rs[1])

    return pl.pallas_call(
        body,
        grid=(N_PAIRS, sp["nq"]),
        in_specs=[sp["q"], sp["k_t"], sp["k_rows"], pl.BlockSpec((2 * tk, tk), lambda p, i: (0, 0))],
        out_specs=[sp["q"], sp["q"]],
        out_shape=[sp["wide"], sp["wide"]],
        scratch_shapes=[sp["acc"]],
        compiler_params=_cparams(56),
        name="sb_fwd",
    )(qs, kt4, v3, upper)


def _sb_bwd(qs, qst4, kt4, k3, vt4, dy, dyt4, rtot):
    sp = _att_specs(qs.shape[0])
    tq, tk = ATT_Q_BLOCK, ATT_BLOCK
    ratio = tq // tk
    lower_in = _tri_stacked("row_le_col")
    lower = _tri_stacked("row_lt_col")

    def body(q_ref, qt_ref, kt_ref, k_ref, vt_ref, dy_ref, dyt_ref, rtot_ref, li_ref, l_ref,
             dq_ref, dkt_ref, dvt_ref, acc_ref):
        i = pl.program_id(1)

        @pl.when(i == 0)
        def _():
            dkt_ref[...] = jnp.zeros_like(dkt_ref)
            dvt_ref[...] = jnp.zeros_like(dvt_ref)

        first = _head_masks(tq)
        first_t = lax.broadcasted_iota(jnp.int32, (PAIR_W, tq), 0) < HEAD_DIM
        qh = _heads_of(q_ref, first)
        qth = _heads_of(qt_ref.at[0, 0], first_t)
        dyth = _heads_of(dyt_ref.at[0, 0], first_t)
        dyb = dy_ref[...].astype(BF16)
        zero = jnp.zeros_like(dyb)
        dyh = [jnp.where(first, dyb, zero), jnp.where(first, zero, dyb)]
        rtoth = _head_cols(rtot_ref)
        li = li_ref[...]
        lo_tri = l_ref[...]
        acc_ref[...] = jnp.zeros_like(acc_ref)

        def block(j, carry, diag):
            mask = _causal_mask(tq, tk, i * tq - j * tk, strict=True) if diag else None
            kt, k, vt = kt_ref[0, j], k_ref[j], vt_ref[0, j]
            logits = [_dot(qh[n], kt) for n in range(2)]
            das = [_dot(dyh[n], vt) for n in range(2)]
            terms = [_sb_terms(z, mask, diag) for z in logits]
            upto = [_dot_split2_stacked(l1m, li) for _, l1m in terms]
            des, weights = [], []
            for n in range(2):
                a = jnp.exp(terms[n][0] + ((rtoth[n] - carry[2 * n]) - upto[n]))
                if diag:
                    a = jnp.where(mask, a, 0.0)
                des.append(a * das[n])
                weights.append(a.astype(BF16))
            lefts = [_dot_split2_stacked(de, lo_tri) for de in des]
            dzbs, out = [], []
            for n in range(2):
                beta = jnp.exp(terms[n][0])
                dz = des[n] - (des[n] + (carry[2 * n + 1] + lefts[n])) * beta
                if diag:
                    dz = jnp.where(mask, dz, 0.0)
                dzbs.append(dz.astype(BF16))
                out += [carry[2 * n] + jnp.sum(terms[n][1], axis=-1, keepdims=True),
                        carry[2 * n + 1] + jnp.sum(des[n], axis=-1, keepdims=True)]
            for n in range(2):
                acc_ref[n] += _dot(dzbs[n], k)
            dkt_ref[0, j] += _dot(qth[0], dzbs[0]) + _dot(qth[1], dzbs[1])
            dvt_ref[0, j] += _dot(dyth[0], weights[0]) + _dot(dyth[1], weights[1])
            return tuple(out)

        carry = (jnp.zeros((tq, 1), F32),) * 4
        carry = lax.fori_loop(0, ratio * i, lambda j, c: block(j, c, False), carry)
        for d in range(ratio):
            carry = block(ratio * i + d, carry, True)
        dq_ref[...] = jnp.where(first, acc_ref[0], acc_ref[1])

    tri_spec = pl.BlockSpec((2 * tk, tk), lambda p, i: (0, 0))
    return pl.pallas_call(
        body,
        grid=(N_PAIRS, sp["nq"]),
        in_specs=[sp["q"], sp["q_t"], sp["k_t"], sp["k_rows"], sp["k_t"], sp["q"], sp["q_t"], sp["q"],
                  tri_spec, tri_spec],
        out_specs=[sp["q"], sp["k_t"], sp["k_t"]],
        out_shape=[sp["wide"], sp["k_t_out"], sp["k_t_out"]],
        scratch_shapes=[sp["acc"]],
        compiler_params=_cparams(56),
        name="sb_bwd",
    )(qs, qst4, kt4, k3, vt4, dy, dyt4, rtot, lower_in, lower)


def _merge_fwd(x1, gates, y_fox, y_sb, w_bf, w_bs, w_out, tm=512):
    s_len = x1.shape[0]

    def body(x_ref, g_ref, yf_ref, ys_ref, wbf_ref, wbs_ref, wo_ref, o_ref):
        g = g_ref[...]
        of = _dot(yf_ref[...].astype(BF16), wbf_ref[...])
        os_ = _dot(ys_ref[...].astype(BF16), wbs_ref[...])
        merged = _sigmoid(g[:, 0:D_MODEL]) * of + _sigmoid(g[:, D_MODEL:]) * os_
        o_ref[...] = x_ref[...] + _dot(merged.astype(BF16), wo_ref[...])

    row = lambda i: (i, 0)
    full = lambda i: (0, 0)
    return pl.pallas_call(
        body,
        grid=(s_len // tm,),
        in_specs=[
            pl.BlockSpec((tm, D_MODEL), row),
            pl.BlockSpec((tm, 2 * D_MODEL), row),
            pl.BlockSpec((tm, ATT_W), row),
            pl.BlockSpec((tm, ATT_W), row),
            pl.BlockSpec((ATT_W, D_MODEL), full),
            pl.BlockSpec((ATT_W, D_MODEL), full),
            pl.BlockSpec((D_MODEL, D_MODEL), full),
        ],
        out_specs=pl.BlockSpec((tm, D_MODEL), row),
        out_shape=jax.ShapeDtypeStruct((s_len, D_MODEL), F32),
        compiler_params=_cparams(48),
        name="merge_fwd",
    )(x1, gates, y_fox, y_sb, w_bf, w_bs, w_out)


def _merge_bwd(dx2, gates, y_fox, y_sb, w_bf, w_bs, w_out_t, w_bf_t, w_bs_t, tm=512):
    s_len = dx2.shape[0]

    def body(d_ref, g_ref, yf_ref, ys_ref, wbf_ref, wbs_ref, wot_ref, wbft_ref, wbst_ref,
             dyf_ref, dys_ref, dg_ref, dof_ref, dos_ref, m_ref, dbf_ref):
        dbf = d_ref[...].astype(BF16)
        dbf_ref[...] = dbf
        dm = _dot(dbf, wot_ref[...])
        g = g_ref[...]
        of = _dot(yf_ref[...].astype(BF16), wbf_ref[...])
        os_ = _dot(ys_ref[...].astype(BF16), wbs_ref[...])
        sf = _sigmoid(g[:, 0:D_MODEL])
        ss = _sigmoid(g[:, D_MODEL:])
        m_ref[...] = (sf * of + ss * os_).astype(BF16)
        d_of = (dm * sf).astype(BF16)
        d_os = (dm * ss).astype(BF16)
        dof_ref[...] = d_of
        dos_ref[...] = d_os
        dg_ref[:, 0:D_MODEL] = (dm * of * sf * (1.0 - sf)).astype(BF16)
        dg_ref[:, D_MODEL:] = (dm * os_ * ss * (1.0 - ss)).astype(BF16)
        dyf_ref[...] = _dot(d_of, wbft_ref[...])
        dys_ref[...] = _dot(d_os, wbst_ref[...])

    row = lambda i: (i, 0)
    full = lambda i: (0, 0)
    return pl.pallas_call(
        body,
        grid=(s_len // tm,),
        in_specs=[
            pl.BlockSpec((tm, D_MODEL), row),
            pl.BlockSpec((tm, 2 * D_MODEL), row),
            pl.BlockSpec((tm, ATT_W), row),
            pl.BlockSpec((tm, ATT_W), row),
            pl.BlockSpec((ATT_W, D_MODEL), full),
            pl.BlockSpec((ATT_W, D_MODEL), full),
            pl.BlockSpec((D_MODEL, D_MODEL), full),
            pl.BlockSpec((D_MODEL, ATT_W), full),
            pl.BlockSpec((D_MODEL, ATT_W), full),
        ],
        out_specs=[
            pl.BlockSpec((tm, ATT_W), row), pl.BlockSpec((tm, ATT_W), row),
            pl.BlockSpec((tm, 2 * D_MODEL), row),
            pl.BlockSpec((tm, D_MODEL), row), pl.BlockSpec((tm, D_MODEL), row),
            pl.BlockSpec((tm, D_MODEL), row), pl.BlockSpec((tm, D_MODEL), row),
        ],
        out_shape=[
            jax.ShapeDtypeStruct((s_len, ATT_W), F32), jax.ShapeDtypeStruct((s_len, ATT_W), F32),
            jax.ShapeDtypeStruct((s_len, 2 * D_MODEL), BF16),
            jax.ShapeDtypeStruct((s_len, D_MODEL), BF16), jax.ShapeDtypeStruct((s_len, D_MODEL), BF16),
            jax.ShapeDtypeStruct((s_len, D_MODEL), BF16), jax.ShapeDtypeStruct((s_len, D_MODEL), BF16),
        ],
        compiler_params=_cparams(56),
        name="merge_bwd",
    )(dx2, gates, y_fox, y_sb, w_bf, w_bs, w_out_t, w_bf_t, w_bs_t)


def _ple_loss(x3, p, g, w_pg, w_pg_t, w_pp, target, tm=512):
    s_len = x3.shape[0]
    inv_d = 1.0 / D_MODEL

    def body(x_ref, p_ref, g_ref, wpg_ref, wpgt_ref, wpp_ref, t_ref,
             dx_ref, du_ref, dt_ref, hn_ref, dg_ref, loss_ref):
        @pl.when(pl.program_id(0) == 0)
        def _():
            dg_ref[...] = jnp.zeros_like(dg_ref)
            loss_ref[...] = jnp.zeros_like(loss_ref)

        x = x_ref[...]
        xn, r = _rms(x)
        gain = g_ref[...]
        hn = (xn * gain).astype(BF16)
        hn_ref[...] = hn
        sg = _sigmoid(_dot(hn, wpg_ref[...]))
        t = _dot(p_ref[...].astype(BF16), wpp_ref[...])
        err = x + sg * t - t_ref[...]
        sq = jnp.sum(_colsum(err * err), axis=-1, keepdims=True)
        loss_ref[...] += (0.5 * inv_d) * sq
        dy = err * inv_d
        du = (dy * t * sg * (1.0 - sg)).astype(BF16)
        du_ref[...] = du
        dt_ref[...] = (dy * sg).astype(BF16)
        dh = _dot(du, wpgt_ref[...])
        dx_ref[...] = dy + _rms_bwd(dh, xn, r, gain)
        dg_ref[0:1, :] += _colsum(dh * xn)

    row = lambda i: (i, 0)
    full = lambda i: (0, 0)
    bf = jax.ShapeDtypeStruct((s_len, D_MODEL), BF16)
    return pl.pallas_call(
        body,
        grid=(s_len // tm,),
        in_specs=[
            pl.BlockSpec((tm, D_MODEL), row),
            pl.BlockSpec((tm, PLE_DIM), row),
            pl.BlockSpec((1, D_MODEL), full),
            pl.BlockSpec((D_MODEL, D_MODEL), full),
            pl.BlockSpec((D_MODEL, D_MODEL), full),
            pl.BlockSpec((PLE_DIM, D_MODEL), full),
            pl.BlockSpec((tm, D_MODEL), row),
        ],
        out_specs=[
            pl.BlockSpec((tm, D_MODEL), row), pl.BlockSpec((tm, D_MODEL), row),
            pl.BlockSpec((tm, D_MODEL), row), pl.BlockSpec((tm, D_MODEL), row),
            pl.BlockSpec((8, D_MODEL), full), pl.BlockSpec((8, LANES), full),
        ],
        out_shape=[
            jax.ShapeDtypeStruct((s_len, D_MODEL), F32), bf, bf, bf,
            jax.ShapeDtypeStruct((8, D_MODEL), F32), jax.ShapeDtypeStruct((8, LANES), F32),
        ],
        compiler_params=_cparams(48),
        name="ple_loss",
    )(x3, p, g, w_pg, w_pg_t, w_pp, target)


def _qknorm_bwd(fq, fk, dqs, dk, dv, qn, kn, bd, bd_t, tm=256):
    s_len = fq.shape[0]

    def body(fq_ref, fk_ref, dq_ref, dk_ref, dv_ref, qn_ref, kn_ref, bd_ref, bdt_ref,
             dz_ref, dqn_ref, dkn_ref):
        @pl.when(pl.program_id(0) == 0)
        def _():
            dqn_ref[...] = jnp.zeros_like(dqn_ref)
            dkn_ref[...] = jnp.zeros_like(dkn_ref)

        bd_m = bd_ref[...]
        bdt_m = bdt_ref[...]

        def one(x, dy, gain, dgain_ref):
            xn, rw = _head_rms(x, bd_m, bdt_m)
            dgain_ref[0:1, :] += _colsum(dy * xn)
            dxn = dy * gain
            return rw * (dxn - xn * _head_mean(dxn * xn, bd_m, bdt_m))

        dz_ref[:, 0:ATT_W] = one(fq_ref[...], dq_ref[...] * QK_SCALE, qn_ref[...], dqn_ref).astype(BF16)
        dz_ref[:, ATT_W:2 * ATT_W] = one(fk_ref[...], dk_ref[...], kn_ref[...], dkn_ref).astype(BF16)
        dz_ref[:, 2 * ATT_W:] = dv_ref[...].astype(BF16)

    row = lambda i: (i, 0)
    full = lambda i: (0, 0)
    att = pl.BlockSpec((tm, ATT_W), row)
    return pl.pallas_call(
        body,
        grid=(s_len // tm,),
        in_specs=[att, att, att, att, att,
                  pl.BlockSpec((1, ATT_W), full), pl.BlockSpec((1, ATT_W), full),
                  pl.BlockSpec((ATT_W, LANES), full), pl.BlockSpec((LANES, ATT_W), full)],
        out_specs=[pl.BlockSpec((tm, 3 * ATT_W), row), pl.BlockSpec((8, ATT_W), full), pl.BlockSpec((8, ATT_W), full)],
        out_shape=[jax.ShapeDtypeStruct((s_len, 3 * ATT_W), BF16),
                   jax.ShapeDtypeStruct((8, ATT_W), F32), jax.ShapeDtypeStruct((8, ATT_W), F32)],
        name="qknorm_bwd",
    )(fq, fk, dqs, dk, dv, qn, kn, bd, bd_t)


def _inproj_bwd(x1, dx2, g, dzf, dlogf, logf, dzs, dgates, w_fox_t, w_fl_t, w_sb_t, w_gates_t, tm=256):
    s_len = x1.shape[0]

    def body(x_ref, d_ref, g_ref, dzf_ref, dlf_ref, lf_ref, dzs_ref, dgt_ref, wf_ref, wl_ref, ws_ref, wg_ref,
             dx_ref, h_ref, dfl_ref, dg_ref, db_ref):
        @pl.when(pl.program_id(0) == 0)
        def _():
            dg_ref[...] = jnp.zeros_like(dg_ref)
            db_ref[...] = jnp.zeros_like(db_ref)

        xn, r = _rms(x_ref[...])
        gain = g_ref[...]
        h_ref[...] = (xn * gain).astype(BF16)
        lane = lax.broadcasted_iota(jnp.int32, (tm, LANES), 1)
        dfl = jnp.where(lane < N_HEADS, dlf_ref[...] * (1.0 - jnp.exp(lf_ref[...])), 0.0)
        db_ref[0:1, :] += _colsum(dfl)
        dflb = dfl.astype(BF16)
        dfl_ref[...] = dflb
        dh = (_dot(dzf_ref[...], wf_ref[...]) + _dot(dflb, wl_ref[...])
              + _dot(dzs_ref[...], ws_ref[...]) + _dot(dgt_ref[...], wg_ref[...]))
        dx_ref[...] = d_ref[...] + _rms_bwd(dh, xn, r, gain)
        dg_ref[0:1, :] += _colsum(dh * xn)

    row = lambda i: (i, 0)
    full = lambda i: (0, 0)
    return pl.pallas_call(
        body,
        grid=(s_len // tm,),
        in_specs=[
            pl.BlockSpec((tm, D_MODEL), row),
            pl.BlockSpec((tm, D_MODEL), row),
            pl.BlockSpec((1, D_MODEL), full),
            pl.BlockSpec((tm, 3 * ATT_W), row),
            pl.BlockSpec((tm, LANES), row),
            pl.BlockSpec((tm, LANES), row),
            pl.BlockSpec((tm, 3 * ATT_W), row),
            pl.BlockSpec((tm, 2 * D_MODEL), row),
            pl.BlockSpec((3 * ATT_W, D_MODEL), full),
            pl.BlockSpec((LANES, D_MODEL), full),
            pl.BlockSpec((3 * ATT_W, D_MODEL), full),
            pl.BlockSpec((2 * D_MODEL, D_MODEL), full),
        ],
        out_specs=[
            pl.BlockSpec((tm, D_MODEL), row), pl.BlockSpec((tm, D_MODEL), row), pl.BlockSpec((tm, LANES), row),
            pl.BlockSpec((8, D_MODEL), full), pl.BlockSpec((8, LANES), full),
        ],
        out_shape=[
            jax.ShapeDtypeStruct((s_len, D_MODEL), F32), jax.ShapeDtypeStruct((s_len, D_MODEL), BF16),
            jax.ShapeDtypeStruct((s_len, LANES), BF16),
            jax.ShapeDtypeStruct((8, D_MODEL), F32), jax.ShapeDtypeStruct((8, LANES), F32),
        ],
        compiler_params=_cparams(56),
        name="inproj_bwd",
    )(x1, dx2, g, dzf, dlogf, logf, dzs, dgates, w_fox_t, w_fl_t, w_sb_t, w_gates_t)


def _split_w_in(w_in):
    o = 3 * ATT_W
    w_fox = w_in[:, 0:o]
    w_fl = jnp.pad(w_in[:, o:o + N_HEADS], ((0, 0), (0, LANES - N_HEADS)))
    w_sb = w_in[:, o + N_HEADS:2 * o + N_HEADS]
    w_gates = w_in[:, 2 * o + N_HEADS:]
    return w_fox, w_fl, w_sb, w_gates


def _local_grads(x, p, target, small, full):
    blk, qblk = ATT_BLOCK, ATT_Q_BLOCK
    bd, bd_t = _head_sum_matrices()
    tr = lambda w: jnp.swapaxes(w, -1, -2)

    w_fox, w_fl, w_sb, w_gates = _split_w_in(full["w_in"])
    bias = jnp.pad(small["forget_bias"], ((0, 0), (0, LANES - N_HEADS)))
    qn = jnp.tile(small["q_norm"], (1, N_HEADS))
    kn = jnp.tile(small["k_norm"], (1, N_HEADS))

    x1 = _ffn_fwd(x, small["ffn1_norm"], full["ffn1_w_gate"], full["ffn1_w_up"], full["ffn1_w_down"])
    fq, fk, f_qs, f_k, f_v, logf, s_qs, s_k, s_v, gates = _inproj_fwd(
        x1, small["mix_norm"], w_fox, w_fl, w_sb, w_gates, bias, qn, kn, bd, bd_t)
    f_cum = _cumsum_rows(logf, reverse=False)
    f8 = f_cum[:, 0:N_HEADS]
    fw = jnp.repeat(f8, HEAD_DIM, axis=1)
    ft4 = _pair_rows_t(f8, blk)
    f_kt4 = _blocked_t(f_k, blk)
    f_v3 = _blocked_rows(f_v, blk)
    y_fox, lse = _fox_fwd(f_qs, f_kt4, f_v3, fw, ft4)
    s_kt4 = _blocked_t(s_k, blk)
    s_v3 = _blocked_rows(s_v, blk)
    y_sb, s_rtot = _sb_fwd(s_qs, s_kt4, s_v3)
    x2 = _merge_fwd(x1, gates, y_fox, y_sb, full["w_branch_fox"], full["w_branch_sb"], full["w_out"])
    x3 = _ffn_fwd(x2, small["ffn2_norm"], full["ffn2_w_gate"], full["ffn2_w_up"], full["ffn2_w_down"])

    dx3, du_ple, dt_ple, hn_ple, dg_ple, loss_sum = _ple_loss(
        x3, p, small["ple_norm"], full["w_ple_gate"], tr(full["w_ple_gate"]), full["w_ple_proj"], target)
    dx2, u2, da2, db2, h_ffn2, d3_bf, dg_ffn2 = _ffn_bwd(
        x2, dx3, small["ffn2_norm"], full["ffn2_w_gate"], full["ffn2_w_up"],
        tr(full["ffn2_w_down"]), tr(full["ffn2_w_gate"]), tr(full["ffn2_w_up"]))
    dy_fox, dy_sb, dgates, d_of, d_os, merged, d2_bf = _merge_bwd(
        dx2, gates, y_fox, y_sb, full["w_branch_fox"], full["w_branch_sb"],
        tr(full["w_out"]), tr(full["w_branch_fox"]), tr(full["w_branch_sb"]))

    f_dqs, dfq_w, f_dkt4, f_dvt4, dft4 = _fox_bwd(
        f_qs, _blocked_t(f_qs, qblk), f_kt4, _blocked_rows(f_k, blk), _blocked_t(f_v, blk),
        dy_fox, _blocked_t(dy_fox.astype(BF16), qblk), y_fox, lse, fw, ft4)
    s_dqs, s_dkt4, s_dvt4 = _sb_bwd(
        s_qs, _blocked_t(s_qs, qblk), s_kt4, _blocked_rows(s_k, blk), _blocked_t(s_v, blk),
        dy_sb, _blocked_t(dy_sb.astype(BF16), qblk), s_rtot)

    dzf, dqn8, dkn8 = _qknorm_bwd(fq, fk, f_dqs, _unblocked_t(f_dkt4), _unblocked_t(f_dvt4), qn, kn, bd, bd_t)
    dzs = jnp.concatenate([s_dqs * QK_SCALE, _unblocked_t(s_dkt4), _unblocked_t(s_dvt4)], axis=1).astype(BF16)
    df8 = _unpair_rows_t(dft4) + dfq_w[:, ::HEAD_DIM]
    dlogf = _cumsum_rows(jnp.pad(df8, ((0, 0), (0, LANES - N_HEADS))), reverse=True)
    dx1, h_mix, dfl, dg_mix, dbias8 = _inproj_bwd(
        x1, dx2, small["mix_norm"], dzf, dlogf, logf, dzs, dgates,
        tr(w_fox), tr(w_fl), tr(w_sb), tr(w_gates))
    grad_x, u1, da1, db1, h_ffn1, d1_bf, dg_ffn1 = _ffn_bwd(
        x, dx1, small["ffn1_norm"], full["ffn1_w_gate"], full["ffn1_w_up"],
        tr(full["ffn1_w_down"]), tr(full["ffn1_w_gate"]), tr(full["ffn1_w_up"]))

    one = lambda t: t[None]
    gw = {}
    gw["ffn1_w_gate"] = _wgrad(one(h_ffn1), da1, name="wgrad_ffn1_gate")
    gw["ffn1_w_up"] = _wgrad(one(h_ffn1), db1, name="wgrad_ffn1_up")
    gw["ffn1_w_down"] = _wgrad(u1, one(d1_bf), scale=0.5, name="wgrad_ffn1_down")
    gw["ffn2_w_gate"] = _wgrad(one(h_ffn2), da2, name="wgrad_ffn2_gate")
    gw["ffn2_w_up"] = _wgrad(one(h_ffn2), db2, name="wgrad_ffn2_up")
    gw["ffn2_w_down"] = _wgrad(u2, one(d3_bf), scale=0.5, name="wgrad_ffn2_down")
    g_fox = _wgrad(one(h_mix), one(dzf), name="wgrad_in_fox")[0]
    g_fl = _wgrad(one(h_mix), one(dfl), name="wgrad_in_forget")[0]
    g_sb = _wgrad(one(h_mix), one(dzs), name="wgrad_in_sb")[0]
    g_gt = _wgrad(one(h_mix), one(dgates), name="wgrad_in_gates")[0]
    gw["w_in"] = jnp.concatenate([g_fox, g_fl[:, 0:N_HEADS], g_sb, g_gt], axis=1)
    gw["w_branch_fox"] = _wgrad(one(y_fox), one(d_of), name="wgrad_branch_fox")[0]
    gw["w_branch_sb"] = _wgrad(one(y_sb), one(d_os), name="wgrad_branch_sb")[0]
    gw["w_out"] = _wgrad(one(merged), one(d2_bf), name="wgrad_out")[0]
    gw["w_ple_gate"] = _wgrad(one(hn_ple), one(du_ple), name="wgrad_ple_gate")[0]
    gw["w_ple_proj"] = _wgrad(one(p), one(dt_ple), name="wgrad_ple_proj")[0]

    fold = lambda t: jnp.sum(t[0:1].reshape(N_HEADS, HEAD_DIM), axis=0, keepdims=True)
    gs = {
        "ffn1_norm": dg_ffn1[0:1], "mix_norm": dg_mix[0:1], "ffn2_norm": dg_ffn2[0:1], "ple_norm": dg_ple[0:1],
        "forget_bias": dbias8[0:1, 0:N_HEADS], "q_norm": fold(dqn8), "k_norm": fold(dkn8),
    }
    return loss_sum, grad_x, gw, gs


def _position():
    return lax.axis_index("x"), lax.axis_index("y"), lax.axis_index("c")


def _other_chips(x, y):
    return [(1 - x, y), (x, 1 - y), (1 - x, 1 - y)]


ANY = pl.BlockSpec(memory_space=pl.ANY)


def _allgather_weights(shards):
    n = len(shards)

    def body(*refs):
        ins, outs = refs[0:n], refs[n:2 * n]
        send_sems, recv_sems, local_sems = refs[2 * n:]
        x, y, c = _position()
        q = 2 * x + y
        chips = _other_chips(x, y)
        sibling = (x, y, 1 - c)

        def half(a, slot, which):
            r2 = shards[a].shape[0] // 2
            return outs[a].at[slot, pl.ds(which * r2, r2), :]

        def copy(a, k, src, dst, to):
            return pltpu.make_async_remote_copy(
                src_ref=src, dst_ref=dst, send_sem=send_sems.at[6 * a + k], recv_sem=recv_sems.at[6 * a + k],
                device_id=to, device_id_type=MESH)

        local, sent = [], []
        for a in range(n):
            r2 = shards[a].shape[0] // 2
            mine = pltpu.make_async_copy(ins[a], outs[a].at[q], local_sems.at[a])
            mine.start()
            local.append(mine)
            for k, (tx, ty) in enumerate(chips):
                cp = copy(a, k, ins[a].at[pl.ds(c * r2, r2), :], half(a, q, c), (tx, ty, c))
                cp.start()
                sent.append(cp)
        for a in range(n):
            for k, (tx, ty) in enumerate(chips):
                landed = half(a, 2 * tx + ty, c)
                copy(a, k, landed, landed, (tx, ty, c)).wait_recv()
                fwd = copy(a, 3 + k, landed, landed, sibling)
                fwd.start()
                sent.append(fwd)
        for a in range(n):
            for k, (tx, ty) in enumerate(chips):
                other = half(a, 2 * tx + ty, 1 - c)
                copy(a, 3 + k, other, other, sibling).wait_recv()
        for cp in sent:
            cp.wait_send()
        for cp in local:
            cp.wait()

    return pl.pallas_call(
        body,
        in_specs=[ANY] * n,
        out_specs=[ANY] * n,
        out_shape=[jax.ShapeDtypeStruct((N_CHIPS,) + s.shape, s.dtype) for s in shards],
        scratch_shapes=[pltpu.SemaphoreType.DMA((6 * n,)), pltpu.SemaphoreType.DMA((6 * n,)),
                        pltpu.SemaphoreType.DMA((n,))],
        name="allgather_weights",
    )(*shards)


def _exchange_pair_halves(grads):
    n = len(grads)

    def body(*refs):
        ins, outs = refs[0:n], refs[n:2 * n]
        send_sems, recv_sems = refs[2 * n:]
        x, y, c = _position()
        copies = []
        for a in range(n):
            r2 = grads[a].shape[1] // 2
            cp = pltpu.make_async_remote_copy(
                src_ref=ins[a].at[:, pl.ds((1 - c) * r2, r2), :], dst_ref=outs[a],
                send_sem=send_sems.at[a], recv_sem=recv_sems.at[a], device_id=(x, y, 1 - c), device_id_type=MESH)
            cp.start()
            copies.append(cp)
        for cp in copies:
            cp.wait()

    return pl.pallas_call(
        body,
        in_specs=[ANY] * n,
        out_specs=[ANY] * n,
        out_shape=[jax.ShapeDtypeStruct((N_CHIPS, g.shape[1] // 2, g.shape[2]), g.dtype) for g in grads],
        scratch_shapes=[pltpu.SemaphoreType.DMA((n,)), pltpu.SemaphoreType.DMA((n,))],
        name="rs_pair_exchange",
    )(*grads)


def _scatter_to_owner_chips(pairs):
    n = len(pairs)

    def body(*refs):
        ins, outs = refs[0:n], refs[n:2 * n]
        send_sems, recv_sems, local_sems = refs[2 * n:]
        x, y, c = _position()
        q = 2 * x + y
        chips = _other_chips(x, y)
        started = []
        for a in range(n):
            mine = pltpu.make_async_copy(ins[a].at[q], outs[a].at[q], local_sems.at[a])
            mine.start()
            started.append(mine)
            for k, (tx, ty) in enumerate(chips):
                cp = pltpu.make_async_remote_copy(
                    src_ref=ins[a].at[2 * tx + ty], dst_ref=outs[a].at[q],
                    send_sem=send_sems.at[3 * a + k], recv_sem=recv_sems.at[3 * a + k],
                    device_id=(tx, ty, c), device_id_type=MESH)
                cp.start()
                started.append(cp)
        for cp in started:
            cp.wait()

    return pl.pallas_call(
        body,
        in_specs=[ANY] * n,
        out_specs=[ANY] * n,
        out_shape=[jax.ShapeDtypeStruct(p.shape, p.dtype) for p in pairs],
        scratch_shapes=[pltpu.SemaphoreType.DMA((3 * n,)), pltpu.SemaphoreType.DMA((3 * n,)),
                        pltpu.SemaphoreType.DMA((n,))],
        name="rs_scatter",
    )(*pairs)


def _join_halves(totals):
    n = len(totals)

    def body(*refs):
        ins, outs = refs[0:n], refs[n:2 * n]
        send_sems, recv_sems, local_sems = refs[2 * n:]
        x, y, c = _position()
        started = []
        for a in range(n):
            r2 = totals[a].shape[0]
            place = outs[a].at[pl.ds(c * r2, r2), :]
            mine = pltpu.make_async_copy(ins[a], place, local_sems.at[a])
            mine.start()
            cp = pltpu.make_async_remote_copy(
                src_ref=ins[a], dst_ref=place, send_sem=send_sems.at[a], recv_sem=recv_sems.at[a],
                device_id=(x, y, 1 - c), device_id_type=MESH)
            cp.start()
            started += [mine, cp]
        for cp in started:
            cp.wait()

    return pl.pallas_call(
        body,
        in_specs=[ANY] * n,
        out_specs=[ANY] * n,
        out_shape=[jax.ShapeDtypeStruct((2 * t.shape[0], t.shape[1]), t.dtype) for t in totals],
        scratch_shapes=[pltpu.SemaphoreType.DMA((n,)), pltpu.SemaphoreType.DMA((n,)),
                        pltpu.SemaphoreType.DMA((n,))],
        name="rs_join_halves",
    )(*totals)


def _add_pair(g, got, c):
    _, r2, cols = got.shape

    def body(c_ref, g_ref, got_ref, o_ref):
        o_ref[...] = (g_ref[...].astype(F32) + got_ref[...].astype(F32)).astype(BF16)

    spec = pl.BlockSpec((1, r2, cols), lambda s, c_ref: (s, 0, 0))
    return pl.pallas_call(
        body,
        grid_spec=pltpu.PrefetchScalarGridSpec(
            num_scalar_prefetch=1,
            grid=(N_CHIPS,),
            in_specs=[pl.BlockSpec((1, r2, cols), lambda s, c_ref: (s, c_ref[0], 0)), spec],
            out_specs=spec,
        ),
        out_shape=jax.ShapeDtypeStruct(got.shape, BF16),
        name="rs_add_pair",
    )(c, g, got)


def _add_chips(parts):
    _, r2, cols = parts.shape

    def body(p0, p1, p2, p3, o_ref):
        o_ref[...] = ((p0[0].astype(F32) + p1[0].astype(F32)) + p2[0].astype(F32)) + p3[0].astype(F32)

    specs = [pl.BlockSpec((1, r2, cols), functools.partial(lambda i, s: (s, 0, 0), s=s)) for s in range(N_CHIPS)]
    return pl.pallas_call(
        body,
        grid=(1,),
        in_specs=specs,
        out_specs=pl.BlockSpec((r2, cols), lambda i: (0, 0)),
        out_shape=jax.ShapeDtypeStruct((r2, cols), F32),
        name="rs_add_chips",
    )(parts, parts, parts, parts)


def _allreduce_small(part):
    shape = part.shape

    def body(in_ref, out_ref, gather_ref, send_sems, recv_sems):
        x, y, c = _position()
        me = 4 * x + 2 * y + c
        relations = [(a, b, d) for a in (0, 1) for b in (0, 1) for d in (0, 1)][1:]
        flip = lambda v, f: 1 - v if f else v
        copies = []
        for k, (a, b, d) in enumerate(relations):
            cp = pltpu.make_async_remote_copy(
                src_ref=in_ref, dst_ref=gather_ref.at[me], send_sem=send_sems.at[k], recv_sem=recv_sems.at[k],
                device_id=(flip(x, a), flip(y, b), flip(c, d)), device_id_type=MESH)
            cp.start()
            copies.append(cp)
        gather_ref[me] = in_ref[...]
        for cp in copies:
            cp.wait()
        total = gather_ref[0]
        for dev in range(1, 8):
            total = total + gather_ref[dev]
        out_ref[...] = total

    vmem = pl.BlockSpec(memory_space=pltpu.VMEM)
    return pl.pallas_call(
        body,
        in_specs=[vmem],
        out_specs=vmem,
        out_shape=jax.ShapeDtypeStruct(shape, F32),
        scratch_shapes=[pltpu.VMEM((8,) + shape, F32), pltpu.SemaphoreType.DMA((7,)), pltpu.SemaphoreType.DMA((7,))],
        name="allreduce_small",
    )(part)


def _adamw(w, g, m, v):
    rows, cols = w.shape
    tr = _row_block(rows, cols * 4, budget=MIB)
    c1 = 1.0 / (1.0 - ADAM_B1 ** ADAM_STEP)
    c2 = 1.0 / (1.0 - ADAM_B2 ** ADAM_STEP)

    def body(w_ref, g_ref, m_ref, v_ref, d_ref, nm_ref, nv_ref):
        g_ = g_ref[...]
        nm = ADAM_B1 * m_ref[...] + (1.0 - ADAM_B1) * g_
        nv = ADAM_B2 * v_ref[...] + (1.0 - ADAM_B2) * (g_ * g_)
        nm_ref[...] = nm
        nv_ref[...] = nv
        d_ref[...] = -ADAM_LR * ((nm * c1) / (jnp.sqrt(nv * c2) + ADAM_EPS) + ADAM_WD * w_ref[...])

    spec = pl.BlockSpec((tr, cols), lambda i: (i, 0))
    out = jax.ShapeDtypeStruct((rows, cols), F32)
    return pl.pallas_call(
        body,
        grid=(rows // tr,),
        in_specs=[spec] * 4,
        out_specs=[spec] * 3,
        out_shape=[out] * 3,
        name="adamw",
    )(w, g, m, v)


BIG = ["ffn1_w_gate", "ffn1_w_up", "ffn1_w_down", "w_in", "w_branch_fox", "w_branch_sb", "w_out",
       "ffn2_w_gate", "ffn2_w_up", "ffn2_w_down", "w_ple_gate", "w_ple_proj"]
SMALL = ["ffn1_norm", "mix_norm", "ffn2_norm", "ple_norm", "forget_bias", "q_norm", "k_norm"]
COLUMN_SHARDED = ["ffn1_w_gate", "ffn1_w_up", "w_in", "w_branch_fox", "w_branch_sb",
                  "ffn2_w_gate", "ffn2_w_up", "w_ple_proj"]
KEPT_AS_SHARDS = ["ffn1_w_gate", "ffn1_w_up", "ffn1_w_down", "ffn2_w_gate", "ffn2_w_up", "ffn2_w_down"]
ORDER = ["ffn1_norm", "ffn1_w_gate", "ffn1_w_up", "ffn1_w_down", "mix_norm", "w_in", "forget_bias", "q_norm",
         "k_norm", "w_branch_fox", "w_branch_sb", "w_out", "ffn2_norm", "ffn2_w_gate", "ffn2_w_up",
         "ffn2_w_down", "ple_norm", "w_ple_gate", "w_ple_proj"]
SMALL_ROWS = {"ffn1_norm": 0, "mix_norm": 1, "ffn2_norm": 2, "ple_norm": 3}
SMALL_COLS = {"forget_bias": (0, N_HEADS), "q_norm": (N_HEADS, HEAD_DIM), "k_norm": (N_HEADS + HEAD_DIM, HEAD_DIM)}
LOSS_ROW = 5


def _whole(name, gathered):
    if name in COLUMN_SHARDED:
        return jnp.concatenate([gathered[s] for s in range(N_CHIPS)], axis=1)
    return gathered.reshape(-1, gathered.shape[-1])


def _as_shards(name, whole):
    if name in COLUMN_SHARDED:
        k, n = whole.shape
        return whole.reshape(k, N_CHIPS, n // N_CHIPS).transpose(1, 0, 2)
    return whole.reshape(N_CHIPS, whole.shape[0] // N_CHIPS, whole.shape[1])


def _pack_small(values, extra=None):
    rows = [values[k] for k in ("ffn1_norm", "mix_norm", "ffn2_norm", "ple_norm")]
    tail = jnp.concatenate([values["forget_bias"], values["q_norm"], values["k_norm"]], axis=1)
    rows.append(jnp.pad(tail, ((0, 0), (0, D_MODEL - tail.shape[1]))))
    packed = jnp.concatenate(rows + [jnp.zeros((3, D_MODEL), F32)], axis=0)
    if extra is not None:
        packed = packed.at[LOSS_ROW, 0].set(extra)
    return packed


def _unpack_small(packed):
    out = {k: packed[r:r + 1] for k, r in SMALL_ROWS.items()}
    for k, (start, size) in SMALL_COLS.items():
        out[k] = packed[4:5, start:start + size]
    return out


def kernel(x, p, ffn1_norm, ffn1_w_gate, ffn1_w_up, ffn1_w_down, mix_norm, w_in, forget_bias, q_norm, k_norm, w_branch_fox, w_branch_sb, w_out, ffn2_norm, ffn2_w_gate, ffn2_w_up, ffn2_w_down, ple_norm, w_ple_gate, w_ple_proj, loss_target, m_ffn1_norm, m_ffn1_w_gate, m_ffn1_w_up, m_ffn1_w_down, m_mix_norm, m_w_in, m_forget_bias, m_q_norm, m_k_norm, m_w_branch_fox, m_w_branch_sb, m_w_out, m_ffn2_norm, m_ffn2_w_gate, m_ffn2_w_up, m_ffn2_w_down, m_ple_norm, m_w_ple_gate, m_w_ple_proj, v_ffn1_norm, v_ffn1_w_gate, v_ffn1_w_up, v_ffn1_w_down, v_mix_norm, v_w_in, v_forget_bias, v_q_norm, v_k_norm, v_w_branch_fox, v_w_branch_sb, v_w_out, v_ffn2_norm, v_ffn2_w_gate, v_ffn2_w_up, v_ffn2_w_down, v_ple_norm, v_w_ple_gate, v_w_ple_proj):
    args = dict(locals())
    weights = {k: args[k] for k in ORDER}
    moments_m = {k: args["m_" + k] for k in ORDER}
    moments_v = {k: args["v_" + k] for k in ORDER}

    gathered = _allgather_weights([weights[k][0].astype(BF16) for k in BIG])
    full = {}
    for k, gth in zip(BIG, gathered):
        full[k] = gth if k in KEPT_AS_SHARDS else _whole(k, gth)
    small = {k: weights[k] for k in SMALL}

    loss_sum, grad_x, gw, gs = _local_grads(x[0], p[0, 0], loss_target[0], small, full)

    c_idx = lax.axis_index("c").astype(jnp.int32).reshape(1)
    slots = [gw[k] if k in KEPT_AS_SHARDS else _as_shards(k, gw[k]) for k in BIG]
    from_core = _exchange_pair_halves(slots)
    pairs = [_add_pair(g, got, c_idx) for g, got in zip(slots, from_core)]
    parts = _scatter_to_owner_chips(pairs)
    totals = [_add_chips(t) for t in parts]
    grads_big = dict(zip(BIG, _join_halves(totals)))
    reduced = _allreduce_small(_pack_small(gs, extra=loss_sum[0, 0]))
    grads_small = _unpack_small(reduced)
    loss = reduced[LOSS_ROW, 0]

    grads, deltas, new_m, new_v = {}, {}, {}, {}
    for k in BIG:
        grads[k] = grads_big[k][None]
        d, nm, nv = _adamw(weights[k][0], grads_big[k], moments_m[k][0], moments_v[k][0])
        deltas[k], new_m[k], new_v[k] = d[None], nm[None], nv[None]
    d_s, nm_s, nv_s = _adamw(_pack_small({k: weights[k] for k in SMALL}), reduced,
                             _pack_small({k: moments_m[k] for k in SMALL}),
                             _pack_small({k: moments_v[k] for k in SMALL}))
    for k in SMALL:
        grads[k] = grads_small[k]
    for name, packed in (("d", d_s), ("m", nm_s), ("v", nv_s)):
        target = {"d": deltas, "m": new_m, "v": new_v}[name]
        target.update(_unpack_small(packed))

    return (loss, grad_x[None], *[grads[k] for k in ORDER], *[deltas[k] for k in ORDER],
            *[new_m[k] for k in ORDER], *[new_v[k] for k in ORDER])
```

```python
import functools

import jax
import jax.numpy as jnp
from jax import lax
from jax.experimental import pallas as pl
from jax.experimental.pallas import tpu as pltpu

F32 = jnp.float32
BF16 = jnp.bfloat16

D_MODEL = 1024
D_FF = 2816
N_CHIPS = 4
FF_SHARD = D_FF // N_CHIPS
HEAD_DIM = 64
N_HEADS = 8
ATT_W = N_HEADS * HEAD_DIM
PAIR_W = 2 * HEAD_DIM
N_PAIRS = N_HEADS // 2
PLE_DIM = 256
IN_WIDTH = 3 * ATT_W + N_HEADS + 3 * ATT_W + 2 * D_MODEL
EPS = 1e-6
QK_SCALE = HEAD_DIM ** -0.5
LANES = 128
ATT_BLOCK = 256
ATT_Q_BLOCK = 512
NEG_BIG = -1e30

ADAM_LR = 0.001
ADAM_B1 = 0.9
ADAM_B2 = 0.999
ADAM_EPS = 1e-08
ADAM_WD = 0.01
ADAM_STEP = 10

MESH = pl.DeviceIdType.MESH
MIB = 1024 * 1024


def _cparams(vmem_mib=48):
    return pltpu.CompilerParams(vmem_limit_bytes=vmem_mib * MIB)


def _dot(a, b):
    return jnp.dot(a, b, preferred_element_type=F32)


def _dot_tn(a, b):
    return lax.dot_general(a, b, (((0,), (0,)), ((), ())), preferred_element_type=F32)


def _sigmoid(x):
    return 1.0 / (1.0 + jnp.exp(-x))


def _split2(x):
    hi = x.astype(BF16)
    lo = (x - hi.astype(F32)).astype(BF16)
    return hi, lo


def _dot_split2(x, m):
    hi, lo = _split2(x)
    return _dot(hi, m) + _dot(lo, m)


def _split3(x):
    hi = x.astype(BF16)
    rest = x - hi.astype(F32)
    mid = rest.astype(BF16)
    lo = (rest - mid.astype(F32)).astype(BF16)
    return hi, mid, lo


def _rms(x):
    r = lax.rsqrt(jnp.mean(x * x, axis=-1, keepdims=True) + EPS)
    return x * r, r


def _rms_bwd(dh, xn, r, g):
    dxn = dh * g
    return r * (dxn - xn * jnp.mean(dxn * xn, axis=-1, keepdims=True))


def _colsum(x):
    return jnp.sum(x, axis=0, keepdims=True)


def _row_block(rows, row_bytes, budget):
    best = None
    for t in range(8, rows + 1, 8):
        if rows % t == 0 and t * row_bytes <= budget:
            best = t
    return best if best is not None else rows


def _ffn_fwd(x, g, wg, wu, wd, tm=512):
    s_len = x.shape[0]

    def body(x_ref, g_ref, wg_ref, wu_ref, wd_ref, o_ref, h_s, acc_s):
        j = pl.program_id(1)

        @pl.when(j == 0)
        def _():
            xn, _ = _rms(x_ref[...])
            h_s[...] = (xn * g_ref[...]).astype(BF16)
            acc_s[...] = jnp.zeros_like(acc_s)

        h = h_s[...]
        a = _dot(h, wg_ref[0])
        b = _dot(h, wu_ref[0])
        u = (a * _sigmoid(a) * b).astype(BF16)
        acc_s[...] += _dot(u, wd_ref[0])

        @pl.when(j == N_CHIPS - 1)
        def _():
            o_ref[...] = x_ref[...] + 0.5 * acc_s[...]

    return pl.pallas_call(
        body,
        grid=(s_len // tm, N_CHIPS),
        in_specs=[
            pl.BlockSpec((tm, D_MODEL), lambda i, j: (i, 0)),
            pl.BlockSpec((1, D_MODEL), lambda i, j: (0, 0)),
            pl.BlockSpec((1, D_MODEL, FF_SHARD), lambda i, j: (j, 0, 0)),
            pl.BlockSpec((1, D_MODEL, FF_SHARD), lambda i, j: (j, 0, 0)),
            pl.BlockSpec((1, FF_SHARD, D_MODEL), lambda i, j: (j, 0, 0)),
        ],
        out_specs=pl.BlockSpec((tm, D_MODEL), lambda i, j: (i, 0)),
        out_shape=jax.ShapeDtypeStruct((s_len, D_MODEL), F32),
        scratch_shapes=[pltpu.VMEM((tm, D_MODEL), BF16), pltpu.VMEM((tm, D_MODEL), F32)],
        compiler_params=_cparams(48),
        name="ffn_fwd",
    )(x, g, wg, wu, wd)


def _ffn_bwd(x, d, g, wg, wu, wd_t, wg_t, wu_t, tm=512):
    s_len = x.shape[0]
    nb = s_len // tm

    def body(x_ref, d_ref, g_ref, wg_ref, wu_ref, wdt_ref, wgt_ref, wut_ref,
             dx_ref, u_ref, da_ref, db_ref, h_ref, dbf_ref, dg_ref, h_s, dbf_s, dh_s):
        i = pl.program_id(0)
        j = pl.program_id(1)

        @pl.when(j == 0)
        def _():
            xn, _ = _rms(x_ref[...])
            h = (xn * g_ref[...]).astype(BF16)
            h_s[...] = h
            h_ref[...] = h
            dbf = d_ref[...].astype(BF16)
            dbf_s[...] = dbf
            dbf_ref[...] = dbf
            dh_s[...] = jnp.zeros_like(dh_s)

        @pl.when((i == 0) & (j == 0))
        def _():
            dg_ref[...] = jnp.zeros_like(dg_ref)

        h = h_s[...]
        a = _dot(h, wg_ref[0])
        b = _dot(h, wu_ref[0])
        du = 0.5 * _dot(dbf_s[...], wdt_ref[0])
        s = _sigmoid(a)
        silu = a * s
        da = (du * b * (s * (1.0 + a * (1.0 - s)))).astype(BF16)
        db = (du * silu).astype(BF16)
        u_ref[0] = (silu * b).astype(BF16)
        da_ref[0] = da
        db_ref[0] = db
        dh_s[...] += _dot(da, wgt_ref[0]) + _dot(db, wut_ref[0])

        @pl.when(j == N_CHIPS - 1)
        def _():
            xn, r = _rms(x_ref[...])
            dh = dh_s[...]
            dx_ref[...] = d_ref[...] + _rms_bwd(dh, xn, r, g_ref[...])
            dg_ref[0:1, :] += _colsum(dh * xn)

    row = lambda i, j: (i, 0)
    shard = lambda i, j: (j, 0, 0)
    act = lambda i, j: (j, i, 0)
    return pl.pallas_call(
        body,
        grid=(nb, N_CHIPS),
        in_specs=[
            pl.BlockSpec((tm, D_MODEL), row),
            pl.BlockSpec((tm, D_MODEL), row),
            pl.BlockSpec((1, D_MODEL), lambda i, j: (0, 0)),
            pl.BlockSpec((1, D_MODEL, FF_SHARD), shard),
            pl.BlockSpec((1, D_MODEL, FF_SHARD), shard),
            pl.BlockSpec((1, D_MODEL, FF_SHARD), shard),
            pl.BlockSpec((1, FF_SHARD, D_MODEL), shard),
            pl.BlockSpec((1, FF_SHARD, D_MODEL), shard),
        ],
        out_specs=[
            pl.BlockSpec((tm, D_MODEL), row),
            pl.BlockSpec((1, tm, FF_SHARD), act),
            pl.BlockSpec((1, tm, FF_SHARD), act),
            pl.BlockSpec((1, tm, FF_SHARD), act),
            pl.BlockSpec((tm, D_MODEL), row),
            pl.BlockSpec((tm, D_MODEL), row),
            pl.BlockSpec((8, D_MODEL), lambda i, j: (0, 0)),
        ],
        out_shape=[
            jax.ShapeDtypeStruct((s_len, D_MODEL), F32),
            jax.ShapeDtypeStruct((N_CHIPS, s_len, FF_SHARD), BF16),
            jax.ShapeDtypeStruct((N_CHIPS, s_len, FF_SHARD), BF16),
            jax.ShapeDtypeStruct((N_CHIPS, s_len, FF_SHARD), BF16),
            jax.ShapeDtypeStruct((s_len, D_MODEL), BF16),
            jax.ShapeDtypeStruct((s_len, D_MODEL), BF16),
            jax.ShapeDtypeStruct((8, D_MODEL), F32),
        ],
        scratch_shapes=[
            pltpu.VMEM((tm, D_MODEL), BF16),
            pltpu.VMEM((tm, D_MODEL), BF16),
            pltpu.VMEM((tm, D_MODEL), F32),
        ],
        compiler_params=_cparams(56),
        name="ffn_bwd",
    )(x, d, g, wg, wu, wd_t, wg_t, wu_t)


def _wgrad(a, b, scale=1.0, name="wgrad"):
    na, s_len, k_dim = a.shape
    nb, _, n_dim = b.shape
    n = max(na, nb)
    ts = min(s_len, 1024)
    steps = s_len // ts

    def body(a_ref, b_ref, o_ref, acc_s):
        s = pl.program_id(1)

        @pl.when(s == 0)
        def _():
            acc_s[...] = jnp.zeros_like(acc_s)

        acc_s[...] += _dot_tn(a_ref[0].astype(BF16), b_ref[0].astype(BF16))

        @pl.when(s == steps - 1)
        def _():
            o_ref[0] = (acc_s[...] * scale).astype(BF16)

    a_map = (lambda m, s: (m, s, 0)) if na > 1 else (lambda m, s: (0, s, 0))
    b_map = (lambda m, s: (m, s, 0)) if nb > 1 else (lambda m, s: (0, s, 0))
    return pl.pallas_call(
        body,
        grid=(n, steps),
        in_specs=[pl.BlockSpec((1, ts, k_dim), a_map), pl.BlockSpec((1, ts, n_dim), b_map)],
        out_specs=pl.BlockSpec((1, k_dim, n_dim), lambda m, s: (m, 0, 0)),
        out_shape=jax.ShapeDtypeStruct((n, k_dim, n_dim), BF16),
        scratch_shapes=[pltpu.VMEM((k_dim, n_dim), F32)],
        compiler_params=_cparams(56),
        name=name,
    )(a, b)


def _head_sum_matrices():
    lane = lax.broadcasted_iota(jnp.int32, (ATT_W, LANES), 0) // HEAD_DIM
    col = lax.broadcasted_iota(jnp.int32, (ATT_W, LANES), 1)
    bd = (lane == col).astype(BF16)
    return bd, bd.T


def _head_mean(t, bd, bd_t):
    per_head = _dot_split2(t, bd) * (1.0 / HEAD_DIM)
    return _dot_split2(per_head, bd_t)


def _head_rms(x, bd, bd_t):
    per_head = _dot_split2(x * x, bd) * (1.0 / HEAD_DIM)
    r = lax.rsqrt(per_head + EPS)
    rw = _dot_split2(r, bd_t)
    return x * rw, rw


def _log_sigmoid(z):
    return jnp.minimum(z, 0.0) - jnp.log(1.0 + jnp.exp(-jnp.abs(z)))


def _inproj_fwd(x1, g, w_fox, w_fl, w_sb, w_gates, bias, qn, kn, bd, bd_t, tm=256):
    s_len = x1.shape[0]

    def body(x_ref, g_ref, wf_ref, wl_ref, ws_ref, wg_ref, bias_ref, qn_ref, kn_ref, bd_ref, bdt_ref,
             fq_ref, fk_ref, qs_ref, kf_ref, vf_ref, logf_ref, sq_ref, sk_ref, sv_ref, gates_ref):
        xn, _ = _rms(x_ref[...])
        h = (xn * g_ref[...]).astype(BF16)
        zf = _dot(h, wf_ref[...])
        fq = zf[:, 0:ATT_W]
        fk = zf[:, ATT_W:2 * ATT_W]
        fq_ref[...] = fq
        fk_ref[...] = fk
        bd_m = bd_ref[...]
        bdt_m = bdt_ref[...]
        fqn, _ = _head_rms(fq, bd_m, bdt_m)
        fkn, _ = _head_rms(fk, bd_m, bdt_m)
        qs_ref[...] = (fqn * qn_ref[...]).astype(BF16) * QK_SCALE
        kf_ref[...] = (fkn * kn_ref[...]).astype(BF16)
        vf_ref[...] = zf[:, 2 * ATT_W:3 * ATT_W].astype(BF16)
        logf_ref[...] = _log_sigmoid(_dot(h, wl_ref[...]) + bias_ref[...])
        zs = _dot(h, ws_ref[...])
        sq_ref[...] = zs[:, 0:ATT_W].astype(BF16) * QK_SCALE
        sk_ref[...] = zs[:, ATT_W:2 * ATT_W].astype(BF16)
        sv_ref[...] = zs[:, 2 * ATT_W:3 * ATT_W].astype(BF16)
        gates_ref[...] = _dot(h, wg_ref[...])

    row = lambda i: (i, 0)
    full = lambda i: (0, 0)
    att = lambda dt: jax.ShapeDtypeStruct((s_len, ATT_W), dt)
    return pl.pallas_call(
        body,
        grid=(s_len // tm,),
        in_specs=[
            pl.BlockSpec((tm, D_MODEL), row),
            pl.BlockSpec((1, D_MODEL), full),
            pl.BlockSpec((D_MODEL, 3 * ATT_W), full),
            pl.BlockSpec((D_MODEL, LANES), full),
            pl.BlockSpec((D_MODEL, 3 * ATT_W), full),
            pl.BlockSpec((D_MODEL, 2 * D_MODEL), full),
            pl.BlockSpec((1, LANES), full),
            pl.BlockSpec((1, ATT_W), full),
            pl.BlockSpec((1, ATT_W), full),
            pl.BlockSpec((ATT_W, LANES), full),
            pl.BlockSpec((LANES, ATT_W), full),
        ],
        out_specs=[
            pl.BlockSpec((tm, ATT_W), row), pl.BlockSpec((tm, ATT_W), row),
            pl.BlockSpec((tm, ATT_W), row), pl.BlockSpec((tm, ATT_W), row), pl.BlockSpec((tm, ATT_W), row),
            pl.BlockSpec((tm, LANES), row),
            pl.BlockSpec((tm, ATT_W), row), pl.BlockSpec((tm, ATT_W), row), pl.BlockSpec((tm, ATT_W), row),
            pl.BlockSpec((tm, 2 * D_MODEL), row),
        ],
        out_shape=[
            att(F32), att(F32), att(BF16), att(BF16), att(BF16),
            jax.ShapeDtypeStruct((s_len, LANES), F32),
            att(BF16), att(BF16), att(BF16),
            jax.ShapeDtypeStruct((s_len, 2 * D_MODEL), F32),
        ],
        compiler_params=_cparams(56),
        name="inproj_fwd",
    )(x1, g, w_fox, w_fl, w_sb, w_gates, bias, qn, kn, bd, bd_t)


def _tri(n, kind):
    r = lax.broadcasted_iota(jnp.int32, (n, n), 0)
    c = lax.broadcasted_iota(jnp.int32, (n, n), 1)
    m = {"row_ge_col": r >= c, "row_le_col": r <= c, "row_gt_col": r > c, "row_lt_col": r < c}[kind]
    return m.astype(BF16)


def _cumsum_rows(x, reverse, tm=256):
    s_len = x.shape[0]
    nb = s_len // tm
    tri = _tri(tm, "row_le_col" if reverse else "row_ge_col")
    edge = 0 if reverse else tm - 1

    def body(x_ref, tri_ref, o_ref, carry_s):
        @pl.when(pl.program_id(0) == 0)
        def _():
            carry_s[...] = jnp.zeros_like(carry_s)

        hi, mid, lo = _split3(x_ref[...])
        t = tri_ref[...]
        y = _dot(t, hi) + _dot(t, mid) + _dot(t, lo) + carry_s[...]
        o_ref[...] = y
        carry_s[...] = y[edge:edge + 1, :]

    order = (lambda i: (nb - 1 - i, 0)) if reverse else (lambda i: (i, 0))
    return pl.pallas_call(
        body,
        grid=(nb,),
        in_specs=[pl.BlockSpec((tm, LANES), order), pl.BlockSpec((tm, tm), lambda i: (0, 0))],
        out_specs=pl.BlockSpec((tm, LANES), order),
        out_shape=jax.ShapeDtypeStruct((s_len, LANES), F32),
        scratch_shapes=[pltpu.VMEM((1, LANES), F32)],
        name="cumsum_rev" if reverse else "cumsum_fwd",
    )(x, tri)


def _blocked_t(t, blk):
    nb = t.shape[0] // blk
    return t.reshape(nb, blk, N_PAIRS, PAIR_W).transpose(2, 0, 3, 1)


def _unblocked_t(t4):
    _, nb, _, blk = t4.shape
    return t4.transpose(1, 3, 0, 2).reshape(nb * blk, ATT_W)


def _blocked_rows(t, blk):
    return t.reshape(t.shape[0] // blk, blk, t.shape[1])


def _pair_rows_t(f8, blk):
    nb = f8.shape[0] // blk
    t = f8.reshape(nb, blk, N_PAIRS, 2).transpose(2, 0, 3, 1)
    return jnp.pad(t, ((0, 0), (0, 0), (0, 6), (0, 0)))


def _unpair_rows_t(t4):
    _, nb, _, blk = t4.shape
    return t4[:, :, 0:2, :].transpose(1, 3, 0, 2).reshape(nb * blk, N_HEADS)


def _head_masks(tq):
    lane = lax.broadcasted_iota(jnp.int32, (tq, PAIR_W), 1)
    return lane < HEAD_DIM


def _causal_mask(tq, tk, offset, strict):
    d = lax.broadcasted_iota(jnp.int32, (tq, tk), 1) - lax.broadcasted_iota(jnp.int32, (tq, tk), 0)
    return (d < offset) if strict else (d <= offset)


def _heads_of(ref, first):
    t = ref[...]
    zero = jnp.zeros_like(t)
    return [jnp.where(first, t, zero), jnp.where(first, zero, t)]


def _head_cols(ref):
    t = ref[...]
    return [t[:, 0:1], t[:, HEAD_DIM:HEAD_DIM + 1]]


def _att_specs(s_len):
    tq, tk = ATT_Q_BLOCK, ATT_BLOCK
    nq, nk = s_len // tq, s_len // tk
    return dict(
        nq=nq,
        q=pl.BlockSpec((tq, PAIR_W), lambda p, i: (i, p)),
        q_t=pl.BlockSpec((1, 1, PAIR_W, tq), lambda p, i: (p, i, 0, 0)),
        k_t=pl.BlockSpec((1, nk, PAIR_W, tk), lambda p, i: (p, 0, 0, 0)),
        k_rows=pl.BlockSpec((nk, tk, PAIR_W), lambda p, i: (0, 0, p)),
        f_t=pl.BlockSpec((1, nk, 8, tk), lambda p, i: (p, 0, 0, 0)),
        wide=jax.ShapeDtypeStruct((s_len, ATT_W), F32),
        k_t_out=jax.ShapeDtypeStruct((N_PAIRS, nk, PAIR_W, tk), F32),
        f_t_out=jax.ShapeDtypeStruct((N_PAIRS, nk, 8, tk), F32),
        acc=pltpu.VMEM((2, tq, PAIR_W), F32),
    )


def _fox_fwd(qs, kt4, v3, fw, ft4):
    sp = _att_specs(qs.shape[0])
    tq, tk = ATT_Q_BLOCK, ATT_BLOCK
    ratio = tq // tk

    def body(q_ref, kt_ref, v_ref, fw_ref, ft_ref, y_ref, lse_ref, acc_ref):
        i = pl.program_id(1)
        first = _head_masks(tq)
        qh = _heads_of(q_ref, first)
        fqh = _head_cols(fw_ref)
        acc_ref[...] = jnp.zeros_like(acc_ref)

        def block(j, carry, diag):
            mask = _causal_mask(tq, tk, i * tq - j * tk, strict=False) if diag else None
            kt, v, fk = kt_ref[0, j], v_ref[j], ft_ref[0, j]
            logits = [_dot(qh[n], kt) for n in range(2)]
            probs, out = [], []
            for n in range(2):
                m, l = carry[2 * n:2 * n + 2]
                s = logits[n] + (fqh[n] - fk[n:n + 1, :])
                if diag:
                    s = jnp.where(mask, s, NEG_BIG)
                m_new = jnp.maximum(m, jnp.max(s, axis=-1, keepdims=True))
                p = jnp.exp(s - m_new)
                alpha = jnp.exp(m - m_new)
                out += [m_new, alpha * l + jnp.sum(p, axis=-1, keepdims=True)]
                probs.append((p.astype(BF16), alpha))
            for n in range(2):
                acc_ref[n] = probs[n][1] * acc_ref[n] + _dot(probs[n][0], v)
            return tuple(out)

        carry = (jnp.full((tq, 1), NEG_BIG, F32), jnp.zeros((tq, 1), F32)) * 2
        carry = lax.fori_loop(0, ratio * i, lambda j, c: block(j, c, False), carry)
        for d in range(ratio):
            carry = block(ratio * i + d, carry, True)
        m0, l0, m1, l1 = carry
        y_ref[...] = jnp.where(first, acc_ref[0] / l0, acc_ref[1] / l1)
        lse_ref[...] = jnp.where(first, m0 + jnp.log(l0), m1 + jnp.log(l1))

    return pl.pallas_call(
        body,
        grid=(N_PAIRS, sp["nq"]),
        in_specs=[sp["q"], sp["k_t"], sp["k_rows"], sp["q"], sp["f_t"]],
        out_specs=[sp["q"], sp["q"]],
        out_shape=[sp["wide"], sp["wide"]],
        scratch_shapes=[sp["acc"]],
        compiler_params=_cparams(56),
        name="fox_fwd",
    )(qs, kt4, v3, fw, ft4)


def _fox_bwd(qs, qst4, kt4, k3, vt4, dy, dyt4, y, lse, fw, ft4):
    sp = _att_specs(qs.shape[0])
    tq, tk = ATT_Q_BLOCK, ATT_BLOCK
    ratio = tq // tk

    def body(q_ref, qt_ref, kt_ref, k_ref, vt_ref, dy_ref, dyt_ref, y_ref, lse_ref, fw_ref, ft_ref,
             dq_ref, dfq_ref, dkt_ref, dvt_ref, dft_ref, acc_ref):
        i = pl.program_id(1)

        @pl.when(i == 0)
        def _():
            dkt_ref[...] = jnp.zeros_like(dkt_ref)
            dvt_ref[...] = jnp.zeros_like(dvt_ref)
            dft_ref[...] = jnp.zeros_like(dft_ref)

        first = _head_masks(tq)
        first_t = lax.broadcasted_iota(jnp.int32, (PAIR_W, tq), 0) < HEAD_DIM
        qh = _heads_of(q_ref, first)
        qth = _heads_of(qt_ref.at[0, 0], first_t)
        dyth = _heads_of(dyt_ref.at[0, 0], first_t)
        dyv = dy_ref[...]
        dyb = dyv.astype(BF16)
        zero = jnp.zeros_like(dyb)
        dyh = [jnp.where(first, dyb, zero), jnp.where(first, zero, dyb)]
        prod = dyv * y_ref[...]
        zf = jnp.zeros_like(prod)
        delta = [jnp.sum(jnp.where(first, prod, zf), axis=-1, keepdims=True),
                 jnp.sum(jnp.where(first, zf, prod), axis=-1, keepdims=True)]
        fqh = _head_cols(fw_ref)
        lseh = _head_cols(lse_ref)
        shift = [fqh[n] - lseh[n] for n in range(2)]
        acc_ref[...] = jnp.zeros_like(acc_ref)

        def block(j, rows, diag):
            mask = _causal_mask(tq, tk, i * tq - j * tk, strict=False) if diag else None
            kt, k, vt, fk = kt_ref[0, j], k_ref[j], vt_ref[0, j], ft_ref[0, j]
            logits = [_dot(qh[n], kt) for n in range(2)]
            dps = [_dot(dyh[n], vt) for n in range(2)]
            pbs, dsbs, out = [], [], []
            for n in range(2):
                p = jnp.exp(logits[n] + (shift[n] - fk[n:n + 1, :]))
                if diag:
                    p = jnp.where(mask, p, 0.0)
                ds = p * (dps[n] - delta[n])
                pbs.append(p.astype(BF16))
                dsbs.append(ds.astype(BF16))
                out.append(rows[n] + jnp.sum(ds, axis=-1, keepdims=True))
                dft_ref[0, j, n:n + 1, :] -= _colsum(ds)
            for n in range(2):
                acc_ref[n] += _dot(dsbs[n], k)
            dkt_ref[0, j] += _dot(qth[0], dsbs[0]) + _dot(qth[1], dsbs[1])
            dvt_ref[0, j] += _dot(dyth[0], pbs[0]) + _dot(dyth[1], pbs[1])
            return tuple(out)

        rows = (jnp.zeros((tq, 1), F32),) * 2
        rows = lax.fori_loop(0, ratio * i, lambda j, c: block(j, c, False), rows)
        for d in range(ratio):
            rows = block(ratio * i + d, rows, True)
        dq_ref[...] = jnp.where(first, acc_ref[0], acc_ref[1])
        dfq_ref[...] = jnp.where(first, rows[0], rows[1])

    return pl.pallas_call(
        body,
        grid=(N_PAIRS, sp["nq"]),
        in_specs=[sp["q"], sp["q_t"], sp["k_t"], sp["k_rows"], sp["k_t"], sp["q"], sp["q_t"],
                  sp["q"], sp["q"], sp["q"], sp["f_t"]],
        out_specs=[sp["q"], sp["q"], sp["k_t"], sp["k_t"], sp["f_t"]],
        out_shape=[sp["wide"], sp["wide"], sp["k_t_out"], sp["k_t_out"], sp["f_t_out"]],
        scratch_shapes=[sp["acc"]],
        compiler_params=_cparams(56),
        name="fox_bwd",
    )(qs, qst4, kt4, k3, vt4, dy, dyt4, y, lse, fw, ft4)


SIGN_BIT = 0x80000000


def _sb_terms(z, mask, diag):
    neg_abs = pltpu.bitcast(pltpu.bitcast(z, jnp.uint32) | jnp.uint32(SIGN_BIT), F32)
    lb = jnp.minimum(z, 0.0) - jnp.log(1.0 + jnp.exp(neg_abs))
    l1m = lb - z
    if diag:
        l1m = jnp.where(mask, l1m, 0.0)
    return lb, l1m


def _dot_split2_stacked(x, m2):
    hi, lo = _split2(x)
    return _dot(jnp.concatenate([hi, lo], axis=1), m2)


def _tri_stacked(kind):
    t = _tri(ATT_BLOCK, kind)
    return jnp.concatenate([t, t], axis=0)


def _sb_fwd(qs, kt4, v3):
    sp = _att_specs(qs.shape[0])
    tq, tk = ATT_Q_BLOCK, ATT_BLOCK
    ratio = tq // tk
    upper = _tri_stacked("row_gt_col")

    def body(q_ref, kt_ref, v_ref, u_ref, y_ref, rtot_ref, acc_ref):
        i = pl.program_id(1)
        first = _head_masks(tq)
        qh = _heads_of(q_ref, first)
        u = u_ref[...]
        acc_ref[...] = jnp.zeros_like(acc_ref)

        def block(j, rs, diag):
            mask = _causal_mask(tq, tk, i * tq - j * tk, strict=True) if diag else None
            kt, v = kt_ref[0, j], v_ref[j]
            logits = [_dot(qh[n], kt) for n in range(2)]
            terms = [_sb_terms(z, mask, diag) for z in logits]
            right = [_dot_split2_stacked(l1m, u) for _, l1m in terms]
            weights = []
            for n in range(2):
                a = jnp.exp(terms[n][0] + right[n] + rs[n])
                if diag:
                    a = jnp.where(mask, a, 0.0)
                weights.append(a.astype(BF16))
            for n in range(2):
                acc_ref[n] += _dot(weights[n], v)
            return tuple(rs[n] + jnp.sum(terms[n][1], axis=-1, keepdims=True) for n in range(2))

        rs = (jnp.zeros((tq, 1), F32),) * 2
        for d in range(ratio):
            rs = block(ratio * i + (ratio - 1 - d), rs, True)
        rs = lax.fori_loop(0, ratio * i, lambda n, c: block(ratio * i - 1 - n, c, False), rs)
        y_ref[...] = jnp.where(first, acc_ref[0], acc_ref[1])
        rtot_ref[...] = jnp.where(first, rs[0], rs[1])

    return pl.pallas_call(
        body,
        grid=(N_PAIRS, sp["nq"]),
        in_specs=[sp["q"], sp["k_t"], sp["k_rows"], pl.BlockSpec((2 * tk, tk), lambda p, i: (0, 0))],
        out_specs=[sp["q"], sp["q"]],
        out_shape=[sp["wide"], sp["wide"]],
        scratch_shapes=[sp["acc"]],
        compiler_params=_cparams(56),
        name="sb_fwd",
    )(qs, kt4, v3, upper)


def _sb_bwd(qs, qst4, kt4, k3, vt4, dy, dyt4, rtot):
    sp = _att_specs(qs.shape[0])
    tq, tk = ATT_Q_BLOCK, ATT_BLOCK
    ratio = tq // tk
    lower_in = _tri_stacked("row_le_col")
    lower = _tri_stacked("row_lt_col")

    def body(q_ref, qt_ref, kt_ref, k_ref, vt_ref, dy_ref, dyt_ref, rtot_ref, li_ref, l_ref,
             dq_ref, dkt_ref, dvt_ref, acc_ref):
        i = pl.program_id(1)

        @pl.when(i == 0)
        def _():
            dkt_ref[...] = jnp.zeros_like(dkt_ref)
            dvt_ref[...] = jnp.zeros_like(dvt_ref)

        first = _head_masks(tq)
        first_t = lax.broadcasted_iota(jnp.int32, (PAIR_W, tq), 0) < HEAD_DIM
        qh = _heads_of(q_ref, first)
        qth = _heads_of(qt_ref.at[0, 0], first_t)
        dyth = _heads_of(dyt_ref.at[0, 0], first_t)
        dyb = dy_ref[...].astype(BF16)
        zero = jnp.zeros_like(dyb)
        dyh = [jnp.where(first, dyb, zero), jnp.where(first, zero, dyb)]
        rtoth = _head_cols(rtot_ref)
        li = li_ref[...]
        lo_tri = l_ref[...]
        acc_ref[...] = jnp.zeros_like(acc_ref)

        def block(j, carry, diag):
            mask = _causal_mask(tq, tk, i * tq - j * tk, strict=True) if diag else None
            kt, k, vt = kt_ref[0, j], k_ref[j], vt_ref[0, j]
            logits = [_dot(qh[n], kt) for n in range(2)]
            das = [_dot(dyh[n], vt) for n in range(2)]
            terms = [_sb_terms(z, mask, diag) for z in logits]
            upto = [_dot_split2_stacked(l1m, li) for _, l1m in terms]
            des, weights = [], []
            for n in range(2):
                a = jnp.exp(terms[n][0] + ((rtoth[n] - carry[2 * n]) - upto[n]))
                if diag:
                    a = jnp.where(mask, a, 0.0)
                des.append(a * das[n])
                weights.append(a.astype(BF16))
            lefts = [_dot_split2_stacked(de, lo_tri) for de in des]
            dzbs, out = [], []
            for n in range(2):
                beta = jnp.exp(terms[n][0])
                dz = des[n] - (des[n] + (carry[2 * n + 1] + lefts[n])) * beta
                if diag:
                    dz = jnp.where(mask, dz, 0.0)
                dzbs.append(dz.astype(BF16))
                out += [carry[2 * n] + jnp.sum(terms[n][1], axis=-1, keepdims=True),
                        carry[2 * n + 1] + jnp.sum(des[n], axis=-1, keepdims=True)]
            for n in range(2):
                acc_ref[n] += _dot(dzbs[n], k)
            dkt_ref[0, j] += _dot(qth[0], dzbs[0]) + _dot(qth[1], dzbs[1])
            dvt_ref[0, j] += _dot(dyth[0], weights[0]) + _dot(dyth[1], weights[1])
            return tuple(out)

        carry = (jnp.zeros((tq, 1), F32),) * 4
        carry = lax.fori_loop(0, ratio * i, lambda j, c: block(j, c, False), carry)
        for d in range(ratio):
            carry = block(ratio * i + d, carry, True)
        dq_ref[...] = jnp.where(first, acc_ref[0], acc_ref[1])

    tri_spec = pl.BlockSpec((2 * tk, tk), lambda p, i: (0, 0))
    return pl.pallas_call(
        body,
        grid=(N_PAIRS, sp["nq"]),
        in_specs=[sp["q"], sp["q_t"], sp["k_t"], sp["k_rows"], sp["k_t"], sp["q"], sp["q_t"], sp["q"],
                  tri_spec, tri_spec],
        out_specs=[sp["q"], sp["k_t"], sp["k_t"]],
        out_shape=[sp["wide"], sp["k_t_out"], sp["k_t_out"]],
        scratch_shapes=[sp["acc"]],
        compiler_params=_cparams(56),
        name="sb_bwd",
    )(qs, qst4, kt4, k3, vt4, dy, dyt4, rtot, lower_in, lower)


def _merge_fwd(x1, gates, y_fox, y_sb, w_bf, w_bs, w_out, tm=512):
    s_len = x1.shape[0]

    def body(x_ref, g_ref, yf_ref, ys_ref, wbf_ref, wbs_ref, wo_ref, o_ref):
        g = g_ref[...]
        of = _dot(yf_ref[...].astype(BF16), wbf_ref[...])
        os_ = _dot(ys_ref[...].astype(BF16), wbs_ref[...])
        merged = _sigmoid(g[:, 0:D_MODEL]) * of + _sigmoid(g[:, D_MODEL:]) * os_
        o_ref[...] = x_ref[...] + _dot(merged.astype(BF16), wo_ref[...])

    row = lambda i: (i, 0)
    full = lambda i: (0, 0)
    return pl.pallas_call(
        body,
        grid=(s_len // tm,),
        in_specs=[
            pl.BlockSpec((tm, D_MODEL), row),
            pl.BlockSpec((tm, 2 * D_MODEL), row),
            pl.BlockSpec((tm, ATT_W), row),
            pl.BlockSpec((tm, ATT_W), row),
            pl.BlockSpec((ATT_W, D_MODEL), full),
            pl.BlockSpec((ATT_W, D_MODEL), full),
            pl.BlockSpec((D_MODEL, D_MODEL), full),
        ],
        out_specs=pl.BlockSpec((tm, D_MODEL), row),
        out_shape=jax.ShapeDtypeStruct((s_len, D_MODEL), F32),
        compiler_params=_cparams(48),
        name="merge_fwd",
    )(x1, gates, y_fox, y_sb, w_bf, w_bs, w_out)


def _merge_bwd(dx2, gates, y_fox, y_sb, w_bf, w_bs, w_out_t, w_bf_t, w_bs_t, tm=512):
    s_len = dx2.shape[0]

    def body(d_ref, g_ref, yf_ref, ys_ref, wbf_ref, wbs_ref, wot_ref, wbft_ref, wbst_ref,
             dyf_ref, dys_ref, dg_ref, dof_ref, dos_ref, m_ref, dbf_ref):
        dbf = d_ref[...].astype(BF16)
        dbf_ref[...] = dbf
        dm = _dot(dbf, wot_ref[...])
        g = g_ref[...]
        of = _dot(yf_ref[...].astype(BF16), wbf_ref[...])
        os_ = _dot(ys_ref[...].astype(BF16), wbs_ref[...])
        sf = _sigmoid(g[:, 0:D_MODEL])
        ss = _sigmoid(g[:, D_MODEL:])
        m_ref[...] = (sf * of + ss * os_).astype(BF16)
        d_of = (dm * sf).astype(BF16)
        d_os = (dm * ss).astype(BF16)
        dof_ref[...] = d_of
        dos_ref[...] = d_os
        dg_ref[:, 0:D_MODEL] = (dm * of * sf * (1.0 - sf)).astype(BF16)
        dg_ref[:, D_MODEL:] = (dm * os_ * ss * (1.0 - ss)).astype(BF16)
        dyf_ref[...] = _dot(d_of, wbft_ref[...])
        dys_ref[...] = _dot(d_os, wbst_ref[...])

    row = lambda i: (i, 0)
    full = lambda i: (0, 0)
    return pl.pallas_call(
        body,
        grid=(s_len // tm,),
        in_specs=[
            pl.BlockSpec((tm, D_MODEL), row),
            pl.BlockSpec((tm, 2 * D_MODEL), row),
            pl.BlockSpec((tm, ATT_W), row),
            pl.BlockSpec((tm, ATT_W), row),
            pl.BlockSpec((ATT_W, D_MODEL), full),
            pl.BlockSpec((ATT_W, D_MODEL), full),
            pl.BlockSpec((D_MODEL, D_MODEL), full),
            pl.BlockSpec((D_MODEL, ATT_W), full),
            pl.BlockSpec((D_MODEL, ATT_W), full),
        ],
        out_specs=[
            pl.BlockSpec((tm, ATT_W), row), pl.BlockSpec((tm, ATT_W), row),
            pl.BlockSpec((tm, 2 * D_MODEL), row),
            pl.BlockSpec((tm, D_MODEL), row), pl.BlockSpec((tm, D_MODEL), row),
            pl.BlockSpec((tm, D_MODEL), row), pl.BlockSpec((tm, D_MODEL), row),
        ],
        out_shape=[
            jax.ShapeDtypeStruct((s_len, ATT_W), F32), jax.ShapeDtypeStruct((s_len, ATT_W), F32),
            jax.ShapeDtypeStruct((s_len, 2 * D_MODEL), BF16),
            jax.ShapeDtypeStruct((s_len, D_MODEL), BF16), jax.ShapeDtypeStruct((s_len, D_MODEL), BF16),
            jax.ShapeDtypeStruct((s_len, D_MODEL), BF16), jax.ShapeDtypeStruct((s_len, D_MODEL), BF16),
        ],
        compiler_params=_cparams(56),
        name="merge_bwd",
    )(dx2, gates, y_fox, y_sb, w_bf, w_bs, w_out_t, w_bf_t, w_bs_t)


def _ple_loss(x3, p, g, w_pg, w_pg_t, w_pp, target, tm=512):
    s_len = x3.shape[0]
    inv_d = 1.0 / D_MODEL

    def body(x_ref, p_ref, g_ref, wpg_ref, wpgt_ref, wpp_ref, t_ref,
             dx_ref, du_ref, dt_ref, hn_ref, dg_ref, loss_ref):
        @pl.when(pl.program_id(0) == 0)
        def _():
            dg_ref[...] = jnp.zeros_like(dg_ref)
            loss_ref[...] = jnp.zeros_like(loss_ref)

        x = x_ref[...]
        xn, r = _rms(x)
        gain = g_ref[...]
        hn = (xn * gain).astype(BF16)
        hn_ref[...] = hn
        sg = _sigmoid(_dot(hn, wpg_ref[...]))
        t = _dot(p_ref[...].astype(BF16), wpp_ref[...])
        err = x + sg * t - t_ref[...]
        sq = jnp.sum(_colsum(err * err), axis=-1, keepdims=True)
        loss_ref[...] += (0.5 * inv_d) * sq
        dy = err * inv_d
        du = (dy * t * sg * (1.0 - sg)).astype(BF16)
        du_ref[...] = du
        dt_ref[...] = (dy * sg).astype(BF16)
        dh = _dot(du, wpgt_ref[...])
        dx_ref[...] = dy + _rms_bwd(dh, xn, r, gain)
        dg_ref[0:1, :] += _colsum(dh * xn)

    row = lambda i: (i, 0)
    full = lambda i: (0, 0)
    bf = jax.ShapeDtypeStruct((s_len, D_MODEL), BF16)
    return pl.pallas_call(
        body,
        grid=(s_len // tm,),
        in_specs=[
            pl.BlockSpec((tm, D_MODEL), row),
            pl.BlockSpec((tm, PLE_DIM), row),
            pl.BlockSpec((1, D_MODEL), full),
            pl.BlockSpec((D_MODEL, D_MODEL), full),
            pl.BlockSpec((D_MODEL, D_MODEL), full),
            pl.BlockSpec((PLE_DIM, D_MODEL), full),
            pl.BlockSpec((tm, D_MODEL), row),
        ],
        out_specs=[
            pl.BlockSpec((tm, D_MODEL), row), pl.BlockSpec((tm, D_MODEL), row),
            pl.BlockSpec((tm, D_MODEL), row), pl.BlockSpec((tm, D_MODEL), row),
            pl.BlockSpec((8, D_MODEL), full), pl.BlockSpec((8, LANES), full),
        ],
        out_shape=[
            jax.ShapeDtypeStruct((s_len, D_MODEL), F32), bf, bf, bf,
            jax.ShapeDtypeStruct((8, D_MODEL), F32), jax.ShapeDtypeStruct((8, LANES), F32),
        ],
        compiler_params=_cparams(48),
        name="ple_loss",
    )(x3, p, g, w_pg, w_pg_t, w_pp, target)


def _qknorm_bwd(fq, fk, dqs, dk, dv, qn, kn, bd, bd_t, tm=256):
    s_len = fq.shape[0]

    def body(fq_ref, fk_ref, dq_ref, dk_ref, dv_ref, qn_ref, kn_ref, bd_ref, bdt_ref,
             dz_ref, dqn_ref, dkn_ref):
        @pl.when(pl.program_id(0) == 0)
        def _():
            dqn_ref[...] = jnp.zeros_like(dqn_ref)
            dkn_ref[...] = jnp.zeros_like(dkn_ref)

        bd_m = bd_ref[...]
        bdt_m = bdt_ref[...]

        def one(x, dy, gain, dgain_ref):
            xn, rw = _head_rms(x, bd_m, bdt_m)
            dgain_ref[0:1, :] += _colsum(dy * xn)
            dxn = dy * gain
            return rw * (dxn - xn * _head_mean(dxn * xn, bd_m, bdt_m))

        dz_ref[:, 0:ATT_W] = one(fq_ref[...], dq_ref[...] * QK_SCALE, qn_ref[...], dqn_ref).astype(BF16)
        dz_ref[:, ATT_W:2 * ATT_W] = one(fk_ref[...], dk_ref[...], kn_ref[...], dkn_ref).astype(BF16)
        dz_ref[:, 2 * ATT_W:] = dv_ref[...].astype(BF16)

    row = lambda i: (i, 0)
    full = lambda i: (0, 0)
    att = pl.BlockSpec((tm, ATT_W), row)
    return pl.pallas_call(
        body,
        grid=(s_len // tm,),
        in_specs=[att, att, att, att, att,
                  pl.BlockSpec((1, ATT_W), full), pl.BlockSpec((1, ATT_W), full),
                  pl.BlockSpec((ATT_W, LANES), full), pl.BlockSpec((LANES, ATT_W), full)],
        out_specs=[pl.BlockSpec((tm, 3 * ATT_W), row), pl.BlockSpec((8, ATT_W), full), pl.BlockSpec((8, ATT_W), full)],
        out_shape=[jax.ShapeDtypeStruct((s_len, 3 * ATT_W), BF16),
                   jax.ShapeDtypeStruct((8, ATT_W), F32), jax.ShapeDtypeStruct((8, ATT_W), F32)],
        name="qknorm_bwd",
    )(fq, fk, dqs, dk, dv, qn, kn, bd, bd_t)


def _inproj_bwd(x1, dx2, g, dzf, dlogf, logf, dzs, dgates, w_fox_t, w_fl_t, w_sb_t, w_gates_t, tm=256):
    s_len = x1.shape[0]

    def body(x_ref, d_ref, g_ref, dzf_ref, dlf_ref, lf_ref, dzs_ref, dgt_ref, wf_ref, wl_ref, ws_ref, wg_ref,
             dx_ref, h_ref, dfl_ref, dg_ref, db_ref):
        @pl.when(pl.program_id(0) == 0)
        def _():
            dg_ref[...] = jnp.zeros_like(dg_ref)
            db_ref[...] = jnp.zeros_like(db_ref)

        xn, r = _rms(x_ref[...])
        gain = g_ref[...]
        h_ref[...] = (xn * gain).astype(BF16)
        lane = lax.broadcasted_iota(jnp.int32, (tm, LANES), 1)
        dfl = jnp.where(lane < N_HEADS, dlf_ref[...] * (1.0 - jnp.exp(lf_ref[...])), 0.0)
        db_ref[0:1, :] += _colsum(dfl)
        dflb = dfl.astype(BF16)
        dfl_ref[...] = dflb
        dh = (_dot(dzf_ref[...], wf_ref[...]) + _dot(dflb, wl_ref[...])
              + _dot(dzs_ref[...], ws_ref[...]) + _dot(dgt_ref[...], wg_ref[...]))
        dx_ref[...] = d_ref[...] + _rms_bwd(dh, xn, r, gain)
        dg_ref[0:1, :] += _colsum(dh * xn)

    row = lambda i: (i, 0)
    full = lambda i: (0, 0)
    return pl.pallas_call(
        body,
        grid=(s_len // tm,),
        in_specs=[
            pl.BlockSpec((tm, D_MODEL), row),
            pl.BlockSpec((tm, D_MODEL), row),
            pl.BlockSpec((1, D_MODEL), full),
            pl.BlockSpec((tm, 3 * ATT_W), row),
            pl.BlockSpec((tm, LANES), row),
            pl.BlockSpec((tm, LANES), row),
            pl.BlockSpec((tm, 3 * ATT_W), row),
            pl.BlockSpec((tm, 2 * D_MODEL), row),
            pl.BlockSpec((3 * ATT_W, D_MODEL), full),
            pl.BlockSpec((LANES, D_MODEL), full),
            pl.BlockSpec((3 * ATT_W, D_MODEL), full),
            pl.BlockSpec((2 * D_MODEL, D_MODEL), full),
        ],
        out_specs=[
            pl.BlockSpec((tm, D_MODEL), row), pl.BlockSpec((tm, D_MODEL), row), pl.BlockSpec((tm, LANES), row),
            pl.BlockSpec((8, D_MODEL), full), pl.BlockSpec((8, LANES), full),
        ],
        out_shape=[
            jax.ShapeDtypeStruct((s_len, D_MODEL), F32), jax.ShapeDtypeStruct((s_len, D_MODEL), BF16),
            jax.ShapeDtypeStruct((s_len, LANES), BF16),
            jax.ShapeDtypeStruct((8, D_MODEL), F32), jax.ShapeDtypeStruct((8, LANES), F32),
        ],
        compiler_params=_cparams(56),
        name="inproj_bwd",
    )(x1, dx2, g, dzf, dlogf, logf, dzs, dgates, w_fox_t, w_fl_t, w_sb_t, w_gates_t)


def _split_w_in(w_in):
    o = 3 * ATT_W
    w_fox = w_in[:, 0:o]
    w_fl = jnp.pad(w_in[:, o:o + N_HEADS], ((0, 0), (0, LANES - N_HEADS)))
    w_sb = w_in[:, o + N_HEADS:2 * o + N_HEADS]
    w_gates = w_in[:, 2 * o + N_HEADS:]
    return w_fox, w_fl, w_sb, w_gates


def _local_grads(x, p, target, small, full):
    blk, qblk = ATT_BLOCK, ATT_Q_BLOCK
    bd, bd_t = _head_sum_matrices()
    tr = lambda w: jnp.swapaxes(w, -1, -2)

    w_fox, w_fl, w_sb, w_gates = _split_w_in(full["w_in"])
    bias = jnp.pad(small["forget_bias"], ((0, 0), (0, LANES - N_HEADS)))
    qn = jnp.tile(small["q_norm"], (1, N_HEADS))
    kn = jnp.tile(small["k_norm"], (1, N_HEADS))

    x1 = _ffn_fwd(x, small["ffn1_norm"], full["ffn1_w_gate"], full["ffn1_w_up"], full["ffn1_w_down"])
    fq, fk, f_qs, f_k, f_v, logf, s_qs, s_k, s_v, gates = _inproj_fwd(
        x1, small["mix_norm"], w_fox, w_fl, w_sb, w_gates, bias, qn, kn, bd, bd_t)
    f_cum = _cumsum_rows(logf, reverse=False)
    f8 = f_cum[:, 0:N_HEADS]
    fw = jnp.repeat(f8, HEAD_DIM, axis=1)
    ft4 = _pair_rows_t(f8, blk)
    f_kt4 = _blocked_t(f_k, blk)
    f_v3 = _blocked_rows(f_v, blk)
    y_fox, lse = _fox_fwd(f_qs, f_kt4, f_v3, fw, ft4)
    s_kt4 = _blocked_t(s_k, blk)
    s_v3 = _blocked_rows(s_v, blk)
    y_sb, s_rtot = _sb_fwd(s_qs, s_kt4, s_v3)
    x2 = _merge_fwd(x1, gates, y_fox, y_sb, full["w_branch_fox"], full["w_branch_sb"], full["w_out"])
    x3 = _ffn_fwd(x2, small["ffn2_norm"], full["ffn2_w_gate"], full["ffn2_w_up"], full["ffn2_w_down"])

    dx3, du_ple, dt_ple, hn_ple, dg_ple, loss_sum = _ple_loss(
        x3, p, small["ple_norm"], full["w_ple_gate"], tr(full["w_ple_gate"]), full["w_ple_proj"], target)
    dx2, u2, da2, db2, h_ffn2, d3_bf, dg_ffn2 = _ffn_bwd(
        x2, dx3, small["ffn2_norm"], full["ffn2_w_gate"], full["ffn2_w_up"],
        tr(full["ffn2_w_down"]), tr(full["ffn2_w_gate"]), tr(full["ffn2_w_up"]))
    dy_fox, dy_sb, dgates, d_of, d_os, merged, d2_bf = _merge_bwd(
        dx2, gates, y_fox, y_sb, full["w_branch_fox"], full["w_branch_sb"],
        tr(full["w_out"]), tr(full["w_branch_fox"]), tr(full["w_branch_sb"]))

    f_dqs, dfq_w, f_dkt4, f_dvt4, dft4 = _fox_bwd(
        f_qs, _blocked_t(f_qs, qblk), f_kt4, _blocked_rows(f_k, blk), _blocked_t(f_v, blk),
        dy_fox, _blocked_t(dy_fox.astype(BF16), qblk), y_fox, lse, fw, ft4)
    s_dqs, s_dkt4, s_dvt4 = _sb_bwd(
        s_qs, _blocked_t(s_qs, qblk), s_kt4, _blocked_rows(s_k, blk), _blocked_t(s_v, blk),
        dy_sb, _blocked_t(dy_sb.astype(BF16), qblk), s_rtot)

    dzf, dqn8, dkn8 = _qknorm_bwd(fq, fk, f_dqs, _unblocked_t(f_dkt4), _unblocked_t(f_dvt4), qn, kn, bd, bd_t)
    dzs = jnp.concatenate([s_dqs * QK_SCALE, _unblocked_t(s_dkt4), _unblocked_t(s_dvt4)], axis=1).astype(BF16)
    df8 = _unpair_rows_t(dft4) + dfq_w[:, ::HEAD_DIM]
    dlogf = _cumsum_rows(jnp.pad(df8, ((0, 0), (0, LANES - N_HEADS))), reverse=True)
    dx1, h_mix, dfl, dg_mix, dbias8 = _inproj_bwd(
        x1, dx2, small["mix_norm"], dzf, dlogf, logf, dzs, dgates,
        tr(w_fox), tr(w_fl), tr(w_sb), tr(w_gates))
    grad_x, u1, da1, db1, h_ffn1, d1_bf, dg_ffn1 = _ffn_bwd(
        x, dx1, small["ffn1_norm"], full["ffn1_w_gate"], full["ffn1_w_up"],
        tr(full["ffn1_w_down"]), tr(full["ffn1_w_gate"]), tr(full["ffn1_w_up"]))

    one = lambda t: t[None]
    gw = {}
    gw["ffn1_w_gate"] = _wgrad(one(h_ffn1), da1, name="wgrad_ffn1_gate")
    gw["ffn1_w_up"] = _wgrad(one(h_ffn1), db1, name="wgrad_ffn1_up")
    gw["ffn1_w_down"] = _wgrad(u1, one(d1_bf), scale=0.5, name="wgrad_ffn1_down")
    gw["ffn2_w_gate"] = _wgrad(one(h_ffn2), da2, name="wgrad_ffn2_gate")
    gw["ffn2_w_up"] = _wgrad(one(h_ffn2), db2, name="wgrad_ffn2_up")
    gw["ffn2_w_down"] = _wgrad(u2, one(d3_bf), scale=0.5, name="wgrad_ffn2_down")
    g_fox = _wgrad(one(h_mix), one(dzf), name="wgrad_in_fox")[0]
    g_fl = _wgrad(one(h_mix), one(dfl), name="wgrad_in_forget")[0]
    g_sb = _wgrad(one(h_mix), one(dzs), name="wgrad_in_sb")[0]
    g_gt = _wgrad(one(h_mix), one(dgates), name="wgrad_in_gates")[0]
    gw["w_in"] = jnp.concatenate([g_fox, g_fl[:, 0:N_HEADS], g_sb, g_gt], axis=1)
    gw["w_branch_fox"] = _wgrad(one(y_fox), one(d_of), name="wgrad_branch_fox")[0]
    gw["w_branch_sb"] = _wgrad(one(y_sb), one(d_os), name="wgrad_branch_sb")[0]
    gw["w_out"] = _wgrad(one(merged), one(d2_bf), name="wgrad_out")[0]
    gw["w_ple_gate"] = _wgrad(one(hn_ple), one(du_ple), name="wgrad_ple_gate")[0]
    gw["w_ple_proj"] = _wgrad(one(p), one(dt_ple), name="wgrad_ple_proj")[0]

    fold = lambda t: jnp.sum(t[0:1].reshape(N_HEADS, HEAD_DIM), axis=0, keepdims=True)
    gs = {
        "ffn1_norm": dg_ffn1[0:1], "mix_norm": dg_mix[0:1], "ffn2_norm": dg_ffn2[0:1], "ple_norm": dg_ple[0:1],
        "forget_bias": dbias8[0:1, 0:N_HEADS], "q_norm": fold(dqn8), "k_norm": fold(dkn8),
    }
    return loss_sum, grad_x, gw, gs


def _position():
    return lax.axis_index("x"), lax.axis_index("y"), lax.axis_index("c")


def _other_chips(x, y):
    return [(1 - x, y), (x, 1 - y), (1 - x, 1 - y)]


ANY = pl.BlockSpec(memory_space=pl.ANY)


def _place_own_shard(w, q):
    rows, cols = w.shape
    tr = _row_block(rows, cols * 4, budget=2 * MIB)

    def body(q_ref, w_ref, o_ref):
        o_ref[0] = w_ref[...].astype(BF16)

    return pl.pallas_call(
        body,
        grid_spec=pltpu.PrefetchScalarGridSpec(
            num_scalar_prefetch=1,
            grid=(rows // tr,),
            in_specs=[pl.BlockSpec((tr, cols), lambda i, q_ref: (i, 0))],
            out_specs=pl.BlockSpec((1, tr, cols), lambda i, q_ref: (q_ref[0], i, 0)),
        ),
        out_shape=jax.ShapeDtypeStruct((N_CHIPS, rows, cols), BF16),
        name="place_own_shard",
    )(q, w)


def _allgather_weights(slots):
    n = len(slots)

    def body(*refs):
        bufs = refs[n:2 * n]
        send_sems, recv_sems = refs[2 * n:]
        x, y, c = _position()
        q = 2 * x + y
        chips = _other_chips(x, y)
        sibling = (x, y, 1 - c)

        def half(a, slot, which):
            r2 = slots[a].shape[1] // 2
            return bufs[a].at[slot, pl.ds(which * r2, r2), :]

        def copy(a, k, region, to):
            return pltpu.make_async_remote_copy(
                src_ref=region, dst_ref=region, send_sem=send_sems.at[6 * a + k], recv_sem=recv_sems.at[6 * a + k],
                device_id=to, device_id_type=MESH)

        sent = []
        for a in range(n):
            for k, (tx, ty) in enumerate(chips):
                cp = copy(a, k, half(a, q, c), (tx, ty, c))
                cp.start()
                sent.append(cp)
        for a in range(n):
            for k, (tx, ty) in enumerate(chips):
                landed = half(a, 2 * tx + ty, c)
                copy(a, k, landed, (tx, ty, c)).wait_recv()
                fwd = copy(a, 3 + k, landed, sibling)
                fwd.start()
                sent.append(fwd)
        for a in range(n):
            for k, (tx, ty) in enumerate(chips):
                copy(a, 3 + k, half(a, 2 * tx + ty, 1 - c), sibling).wait_recv()
        for cp in sent:
            cp.wait_send()

    return pl.pallas_call(
        body,
        in_specs=[ANY] * n,
        out_specs=[ANY] * n,
        out_shape=[jax.ShapeDtypeStruct(s.shape, s.dtype) for s in slots],
        input_output_aliases={a: a for a in range(n)},
        scratch_shapes=[pltpu.SemaphoreType.DMA((6 * n,)), pltpu.SemaphoreType.DMA((6 * n,))],
        name="allgather_weights",
    )(*slots)


def _exchange_pair_halves(grads):
    n = len(grads)

    def body(*refs):
        ins, outs = refs[0:n], refs[n:2 * n]
        send_sems, recv_sems = refs[2 * n:]
        x, y, c = _position()
        copies = []
        for a in range(n):
            r2 = grads[a].shape[1] // 2
            cp = pltpu.make_async_remote_copy(
                src_ref=ins[a].at[:, pl.ds((1 - c) * r2, r2), :], dst_ref=outs[a],
                send_sem=send_sems.at[a], recv_sem=recv_sems.at[a], device_id=(x, y, 1 - c), device_id_type=MESH)
            cp.start()
            copies.append(cp)
        for cp in copies:
            cp.wait()

    return pl.pallas_call(
        body,
        in_specs=[ANY] * n,
        out_specs=[ANY] * n,
        out_shape=[jax.ShapeDtypeStruct((N_CHIPS, g.shape[1] // 2, g.shape[2]), g.dtype) for g in grads],
        scratch_shapes=[pltpu.SemaphoreType.DMA((n,)), pltpu.SemaphoreType.DMA((n,))],
        name="rs_pair_exchange",
    )(*grads)


def _scatter_to_owner_chips(pairs):
    n = len(pairs)

    def body(*refs):
        ins, outs = refs[0:n], refs[n:2 * n]
        send_sems, recv_sems, local_sems = refs[2 * n:]
        x, y, c = _position()
        q = 2 * x + y
        chips = _other_chips(x, y)
        started = []
        for a in range(n):
            mine = pltpu.make_async_copy(ins[a].at[q], outs[a].at[q], local_sems.at[a])
            mine.start()
            started.append(mine)
            for k, (tx, ty) in enumerate(chips):
                cp = pltpu.make_async_remote_copy(
                    src_ref=ins[a].at[2 * tx + ty], dst_ref=outs[a].at[q],
                    send_sem=send_sems.at[3 * a + k], recv_sem=recv_sems.at[3 * a + k],
                    device_id=(tx, ty, c), device_id_type=MESH)
                cp.start()
                started.append(cp)
        for cp in started:
            cp.wait()

    return pl.pallas_call(
        body,
        in_specs=[ANY] * n,
        out_specs=[ANY] * n,
        out_shape=[jax.ShapeDtypeStruct(p.shape, p.dtype) for p in pairs],
        scratch_shapes=[pltpu.SemaphoreType.DMA((3 * n,)), pltpu.SemaphoreType.DMA((3 * n,)),
                        pltpu.SemaphoreType.DMA((n,))],
        name="rs_scatter",
    )(*pairs)


def _join_halves(shards):
    n = len(shards)

    def body(*refs):
        bufs = refs[n:2 * n]
        send_sems, recv_sems = refs[2 * n:]
        x, y, c = _position()
        started = []
        for a in range(n):
            r2 = shards[a].shape[0] // 2
            mine = bufs[a].at[pl.ds(c * r2, r2), :]
            cp = pltpu.make_async_remote_copy(
                src_ref=mine, dst_ref=mine, send_sem=send_sems.at[a], recv_sem=recv_sems.at[a],
                device_id=(x, y, 1 - c), device_id_type=MESH)
            cp.start()
            started.append(cp)
        for cp in started:
            cp.wait()

    return pl.pallas_call(
        body,
        in_specs=[ANY] * n,
        out_specs=[ANY] * n,
        out_shape=[jax.ShapeDtypeStruct(t.shape, t.dtype) for t in shards],
        input_output_aliases={a: a for a in range(n)},
        scratch_shapes=[pltpu.SemaphoreType.DMA((n,)), pltpu.SemaphoreType.DMA((n,))],
        name="rs_join_halves",
    )(*shards)


def _add_pair(g, got, c):
    _, r2, cols = got.shape

    def body(c_ref, g_ref, got_ref, o_ref):
        o_ref[...] = (g_ref[...].astype(F32) + got_ref[...].astype(F32)).astype(BF16)

    spec = pl.BlockSpec((1, r2, cols), lambda s, c_ref: (s, 0, 0))
    return pl.pallas_call(
        body,
        grid_spec=pltpu.PrefetchScalarGridSpec(
            num_scalar_prefetch=1,
            grid=(N_CHIPS,),
            in_specs=[pl.BlockSpec((1, r2, cols), lambda s, c_ref: (s, c_ref[0], 0)), spec],
            out_specs=spec,
        ),
        out_shape=jax.ShapeDtypeStruct(got.shape, BF16),
        name="rs_add_pair",
    )(c, g, got)


def _add_chips(parts, c):
    _, r2, cols = parts.shape

    def body(c_ref, p0, p1, p2, p3, o_ref):
        o_ref[...] = ((p0[0].astype(F32) + p1[0].astype(F32)) + p2[0].astype(F32)) + p3[0].astype(F32)

    specs = [pl.BlockSpec((1, r2, cols), functools.partial(lambda i, c_ref, s: (s, 0, 0), s=s))
             for s in range(N_CHIPS)]
    return pl.pallas_call(
        body,
        grid_spec=pltpu.PrefetchScalarGridSpec(
            num_scalar_prefetch=1,
            grid=(1,),
            in_specs=specs,
            out_specs=pl.BlockSpec((r2, cols), lambda i, c_ref: (c_ref[0], 0)),
        ),
        out_shape=jax.ShapeDtypeStruct((2 * r2, cols), F32),
        name="rs_add_chips",
    )(c, parts, parts, parts, parts)


def _allreduce_small(part):
    shape = part.shape

    def body(in_ref, out_ref, gather_ref, send_sems, recv_sems):
        x, y, c = _position()
        me = 4 * x + 2 * y + c
        relations = [(a, b, d) for a in (0, 1) for b in (0, 1) for d in (0, 1)][1:]
        flip = lambda v, f: 1 - v if f else v
        copies = []
        for k, (a, b, d) in enumerate(relations):
            cp = pltpu.make_async_remote_copy(
                src_ref=in_ref, dst_ref=gather_ref.at[me], send_sem=send_sems.at[k], recv_sem=recv_sems.at[k],
                device_id=(flip(x, a), flip(y, b), flip(c, d)), device_id_type=MESH)
            cp.start()
            copies.append(cp)
        gather_ref[me] = in_ref[...]
        for cp in copies:
            cp.wait()
        total = gather_ref[0]
        for dev in range(1, 8):
            total = total + gather_ref[dev]
        out_ref[...] = total

    vmem = pl.BlockSpec(memory_space=pltpu.VMEM)
    return pl.pallas_call(
        body,
        in_specs=[vmem],
        out_specs=vmem,
        out_shape=jax.ShapeDtypeStruct(shape, F32),
        scratch_shapes=[pltpu.VMEM((8,) + shape, F32), pltpu.SemaphoreType.DMA((7,)), pltpu.SemaphoreType.DMA((7,))],
        name="allreduce_small",
    )(part)


def _adamw(w, g, m, v):
    rows, cols = w.shape
    tr = _row_block(rows, cols * 4, budget=MIB)
    c1 = 1.0 / (1.0 - ADAM_B1 ** ADAM_STEP)
    c2 = 1.0 / (1.0 - ADAM_B2 ** ADAM_STEP)

    def body(w_ref, g_ref, m_ref, v_ref, d_ref, nm_ref, nv_ref):
        g_ = g_ref[...]
        nm = ADAM_B1 * m_ref[...] + (1.0 - ADAM_B1) * g_
        nv = ADAM_B2 * v_ref[...] + (1.0 - ADAM_B2) * (g_ * g_)
        nm_ref[...] = nm
        nv_ref[...] = nv
        d_ref[...] = -ADAM_LR * ((nm * c1) / (jnp.sqrt(nv * c2) + ADAM_EPS) + ADAM_WD * w_ref[...])

    spec = pl.BlockSpec((tr, cols), lambda i: (i, 0))
    out = jax.ShapeDtypeStruct((rows, cols), F32)
    return pl.pallas_call(
        body,
        grid=(rows // tr,),
        in_specs=[spec] * 4,
        out_specs=[spec] * 3,
        out_shape=[out] * 3,
        name="adamw",
    )(w, g, m, v)


BIG = ["ffn1_w_gate", "ffn1_w_up", "ffn1_w_down", "w_in", "w_branch_fox", "w_branch_sb", "w_out",
       "ffn2_w_gate", "ffn2_w_up", "ffn2_w_down", "w_ple_gate", "w_ple_proj"]
SMALL = ["ffn1_norm", "mix_norm", "ffn2_norm", "ple_norm", "forget_bias", "q_norm", "k_norm"]
COLUMN_SHARDED = ["ffn1_w_gate", "ffn1_w_up", "w_in", "w_branch_fox", "w_branch_sb",
                  "ffn2_w_gate", "ffn2_w_up", "w_ple_proj"]
KEPT_AS_SHARDS = ["ffn1_w_gate", "ffn1_w_up", "ffn1_w_down", "ffn2_w_gate", "ffn2_w_up", "ffn2_w_down"]
ORDER = ["ffn1_norm", "ffn1_w_gate", "ffn1_w_up", "ffn1_w_down", "mix_norm", "w_in", "forget_bias", "q_norm",
         "k_norm", "w_branch_fox", "w_branch_sb", "w_out", "ffn2_norm", "ffn2_w_gate", "ffn2_w_up",
         "ffn2_w_down", "ple_norm", "w_ple_gate", "w_ple_proj"]
SMALL_ROWS = {"ffn1_norm": 0, "mix_norm": 1, "ffn2_norm": 2, "ple_norm": 3}
SMALL_COLS = {"forget_bias": (0, N_HEADS), "q_norm": (N_HEADS, HEAD_DIM), "k_norm": (N_HEADS + HEAD_DIM, HEAD_DIM)}
LOSS_ROW = 5


def _whole(name, gathered):
    if name in COLUMN_SHARDED:
        return jnp.concatenate([gathered[s] for s in range(N_CHIPS)], axis=1)
    return gathered.reshape(-1, gathered.shape[-1])


def _as_shards(name, whole):
    if name in COLUMN_SHARDED:
        k, n = whole.shape
        return whole.reshape(k, N_CHIPS, n // N_CHIPS).transpose(1, 0, 2)
    return whole.reshape(N_CHIPS, whole.shape[0] // N_CHIPS, whole.shape[1])


def _pack_small(values, extra=None):
    rows = [values[k] for k in ("ffn1_norm", "mix_norm", "ffn2_norm", "ple_norm")]
    tail = jnp.concatenate([values["forget_bias"], values["q_norm"], values["k_norm"]], axis=1)
    rows.append(jnp.pad(tail, ((0, 0), (0, D_MODEL - tail.shape[1]))))
    packed = jnp.concatenate(rows + [jnp.zeros((3, D_MODEL), F32)], axis=0)
    if extra is not None:
        packed = packed.at[LOSS_ROW, 0].set(extra)
    return packed


def _unpack_small(packed):
    out = {k: packed[r:r + 1] for k, r in SMALL_ROWS.items()}
    for k, (start, size) in SMALL_COLS.items():
        out[k] = packed[4:5, start:start + size]
    return out


def kernel(x, p, ffn1_norm, ffn1_w_gate, ffn1_w_up, ffn1_w_down, mix_norm, w_in, forget_bias, q_norm, k_norm, w_branch_fox, w_branch_sb, w_out, ffn2_norm, ffn2_w_gate, ffn2_w_up, ffn2_w_down, ple_norm, w_ple_gate, w_ple_proj, loss_target, m_ffn1_norm, m_ffn1_w_gate, m_ffn1_w_up, m_ffn1_w_down, m_mix_norm, m_w_in, m_forget_bias, m_q_norm, m_k_norm, m_w_branch_fox, m_w_branch_sb, m_w_out, m_ffn2_norm, m_ffn2_w_gate, m_ffn2_w_up, m_ffn2_w_down, m_ple_norm, m_w_ple_gate, m_w_ple_proj, v_ffn1_norm, v_ffn1_w_gate, v_ffn1_w_up, v_ffn1_w_down, v_mix_norm, v_w_in, v_forget_bias, v_q_norm, v_k_norm, v_w_branch_fox, v_w_branch_sb, v_w_out, v_ffn2_norm, v_ffn2_w_gate, v_ffn2_w_up, v_ffn2_w_down, v_ple_norm, v_w_ple_gate, v_w_ple_proj):
    args = dict(locals())
    weights = {k: args[k] for k in ORDER}
    moments_m = {k: args["m_" + k] for k in ORDER}
    moments_v = {k: args["v_" + k] for k in ORDER}

    c_idx = lax.axis_index("c").astype(jnp.int32).reshape(1)
    q_idx = (2 * lax.axis_index("x") + lax.axis_index("y")).astype(jnp.int32).reshape(1)
    gathered = _allgather_weights([_place_own_shard(weights[k][0], q_idx) for k in BIG])
    full = {}
    for k, gth in zip(BIG, gathered):
        full[k] = gth if k in KEPT_AS_SHARDS else _whole(k, gth)
    small = {k: weights[k] for k in SMALL}

    loss_sum, grad_x, gw, gs = _local_grads(x[0], p[0, 0], loss_target[0], small, full)

    slots = [gw[k] if k in KEPT_AS_SHARDS else _as_shards(k, gw[k]) for k in BIG]
    from_core = _exchange_pair_halves(slots)
    pairs = [_add_pair(g, got, c_idx) for g, got in zip(slots, from_core)]
    parts = _scatter_to_owner_chips(pairs)
    grads_big = dict(zip(BIG, _join_halves([_add_chips(t, c_idx) for t in parts])))
    reduced = _allreduce_small(_pack_small(gs, extra=loss_sum[0, 0]))
    grads_small = _unpack_small(reduced)
    loss = reduced[LOSS_ROW, 0]

    grads, deltas, new_m, new_v = {}, {}, {}, {}
    for k in BIG:
        grads[k] = grads_big[k][None]
        d, nm, nv = _adamw(weights[k][0], grads_big[k], moments_m[k][0], moments_v[k][0])
        deltas[k], new_m[k], new_v[k] = d[None], nm[None], nv[None]
    d_s, nm_s, nv_s = _adamw(_pack_small({k: weights[k] for k in SMALL}), reduced,
                             _pack_small({k: moments_m[k] for k in SMALL}),
                             _pack_small({k: moments_v[k] for k in SMALL}))
    for k in SMALL:
        grads[k] = grads_small[k]
    for name, packed in (("d", d_s), ("m", nm_s), ("v", nv_s)):
        target = {"d": deltas, "m": new_m, "v": new_v}[name]
        target.update(_unpack_small(packed))

    return (loss, grad_x[None], *[grads[k] for k in ORDER], *[deltas[k] for k in ORDER],
            *[new_m[k] for k in ORDER], *[new_v[k] for k in ORDER])
```

```python
import functools

import jax
import jax.numpy as jnp
from jax import lax
from jax.experimental import pallas as pl
from jax.experimental.pallas import tpu as pltpu

F32 = jnp.float32
BF16 = jnp.bfloat16

D_MODEL = 1024
D_FF = 2816
N_CHIPS = 4
FF_SHARD = D_FF // N_CHIPS
HEAD_DIM = 64
N_HEADS = 8
ATT_W = N_HEADS * HEAD_DIM
PAIR_W = 2 * HEAD_DIM
N_PAIRS = N_HEADS // 2
PLE_DIM = 256
IN_WIDTH = 3 * ATT_W + N_HEADS + 3 * ATT_W + 2 * D_MODEL
EPS = 1e-6
QK_SCALE = HEAD_DIM ** -0.5
LANES = 128
ATT_BLOCK = 256
ATT_Q_BLOCK = 512
NEG_BIG = -1e30

ADAM_LR = 0.001
ADAM_B1 = 0.9
ADAM_B2 = 0.999
ADAM_EPS = 1e-08
ADAM_WD = 0.01
ADAM_STEP = 10

MESH = pl.DeviceIdType.MESH
MIB = 1024 * 1024


def _cparams(vmem_mib=48):
    return pltpu.CompilerParams(vmem_limit_bytes=vmem_mib * MIB)


def _dot(a, b):
    return jnp.dot(a, b, preferred_element_type=F32)


def _dot_tn(a, b):
    return lax.dot_general(a, b, (((0,), (0,)), ((), ())), preferred_element_type=F32)


def _dot_nt(a, b):
    return lax.dot_general(a, b, (((1,), (1,)), ((), ())), preferred_element_type=F32)


def _sigmoid(x):
    return 1.0 / (1.0 + jnp.exp(-x))


def _split2(x):
    hi = x.astype(BF16)
    lo = (x - hi.astype(F32)).astype(BF16)
    return hi, lo


def _dot_split2(x, m):
    hi, lo = _split2(x)
    return _dot(hi, m) + _dot(lo, m)


def _split3(x):
    hi = x.astype(BF16)
    rest = x - hi.astype(F32)
    mid = rest.astype(BF16)
    lo = (rest - mid.astype(F32)).astype(BF16)
    return hi, mid, lo


def _rms(x):
    r = lax.rsqrt(jnp.mean(x * x, axis=-1, keepdims=True) + EPS)
    return x * r, r


def _rms_bwd(dh, xn, r, g):
    dxn = dh * g
    return r * (dxn - xn * jnp.mean(dxn * xn, axis=-1, keepdims=True))


def _colsum(x):
    return jnp.sum(x, axis=0, keepdims=True)


def _row_block(rows, row_bytes, budget):
    best = None
    for t in range(8, rows + 1, 8):
        if rows % t == 0 and t * row_bytes <= budget:
            best = t
    return best if best is not None else rows


def _ffn_fwd(x, g, wg, wu, wd, tm=512):
    s_len = x.shape[0]

    def body(x_ref, g_ref, wg_ref, wu_ref, wd_ref, o_ref, h_s, acc_s):
        j = pl.program_id(1)

        @pl.when(j == 0)
        def _():
            xn, _ = _rms(x_ref[...])
            h_s[...] = (xn * g_ref[...]).astype(BF16)
            acc_s[...] = jnp.zeros_like(acc_s)

        h = h_s[...]
        a = _dot(h, wg_ref[0])
        b = _dot(h, wu_ref[0])
        u = (a * _sigmoid(a) * b).astype(BF16)
        acc_s[...] += _dot(u, wd_ref[0])

        @pl.when(j == N_CHIPS - 1)
        def _():
            o_ref[...] = x_ref[...] + 0.5 * acc_s[...]

    return pl.pallas_call(
        body,
        grid=(s_len // tm, N_CHIPS),
        in_specs=[
            pl.BlockSpec((tm, D_MODEL), lambda i, j: (i, 0)),
            pl.BlockSpec((1, D_MODEL), lambda i, j: (0, 0)),
            pl.BlockSpec((1, D_MODEL, FF_SHARD), lambda i, j: (j, 0, 0)),
            pl.BlockSpec((1, D_MODEL, FF_SHARD), lambda i, j: (j, 0, 0)),
            pl.BlockSpec((1, FF_SHARD, D_MODEL), lambda i, j: (j, 0, 0)),
        ],
        out_specs=pl.BlockSpec((tm, D_MODEL), lambda i, j: (i, 0)),
        out_shape=jax.ShapeDtypeStruct((s_len, D_MODEL), F32),
        scratch_shapes=[pltpu.VMEM((tm, D_MODEL), BF16), pltpu.VMEM((tm, D_MODEL), F32)],
        compiler_params=_cparams(48),
        name="ffn_fwd",
    )(x, g, wg, wu, wd)


def _ffn_bwd(x, d, g, wg, wu, wd, tm=512):
    s_len = x.shape[0]
    nb = s_len // tm

    def body(x_ref, d_ref, g_ref, wg_ref, wu_ref, wd_ref,
             dx_ref, u_ref, da_ref, db_ref, h_ref, dbf_ref, dg_ref, h_s, dbf_s, dh_s):
        i = pl.program_id(0)
        j = pl.program_id(1)

        @pl.when(j == 0)
        def _():
            xn, _ = _rms(x_ref[...])
            h = (xn * g_ref[...]).astype(BF16)
            h_s[...] = h
            h_ref[...] = h
            dbf = d_ref[...].astype(BF16)
            dbf_s[...] = dbf
            dbf_ref[...] = dbf
            dh_s[...] = jnp.zeros_like(dh_s)

        @pl.when((i == 0) & (j == 0))
        def _():
            dg_ref[...] = jnp.zeros_like(dg_ref)

        h = h_s[...]
        a = _dot(h, wg_ref[0])
        b = _dot(h, wu_ref[0])
        du = 0.5 * _dot_nt(dbf_s[...], wd_ref[0])
        s = _sigmoid(a)
        silu = a * s
        da = (du * b * (s * (1.0 + a * (1.0 - s)))).astype(BF16)
        db = (du * silu).astype(BF16)
        u_ref[0] = (silu * b).astype(BF16)
        da_ref[0] = da
        db_ref[0] = db
        dh_s[...] += _dot_nt(da, wg_ref[0]) + _dot_nt(db, wu_ref[0])

        @pl.when(j == N_CHIPS - 1)
        def _():
            xn, r = _rms(x_ref[...])
            dh = dh_s[...]
            dx_ref[...] = d_ref[...] + _rms_bwd(dh, xn, r, g_ref[...])
            dg_ref[0:1, :] += _colsum(dh * xn)

    row = lambda i, j: (i, 0)
    shard = lambda i, j: (j, 0, 0)
    act = lambda i, j: (j, i, 0)
    return pl.pallas_call(
        body,
        grid=(nb, N_CHIPS),
        in_specs=[
            pl.BlockSpec((tm, D_MODEL), row),
            pl.BlockSpec((tm, D_MODEL), row),
            pl.BlockSpec((1, D_MODEL), lambda i, j: (0, 0)),
            pl.BlockSpec((1, D_MODEL, FF_SHARD), shard),
            pl.BlockSpec((1, D_MODEL, FF_SHARD), shard),
            pl.BlockSpec((1, FF_SHARD, D_MODEL), shard),
        ],
        out_specs=[
            pl.BlockSpec((tm, D_MODEL), row),
            pl.BlockSpec((1, tm, FF_SHARD), act),
            pl.BlockSpec((1, tm, FF_SHARD), act),
            pl.BlockSpec((1, tm, FF_SHARD), act),
            pl.BlockSpec((tm, D_MODEL), row),
            pl.BlockSpec((tm, D_MODEL), row),
            pl.BlockSpec((8, D_MODEL), lambda i, j: (0, 0)),
        ],
        out_shape=[
            jax.ShapeDtypeStruct((s_len, D_MODEL), F32),
            jax.ShapeDtypeStruct((N_CHIPS, s_len, FF_SHARD), BF16),
            jax.ShapeDtypeStruct((N_CHIPS, s_len, FF_SHARD), BF16),
            jax.ShapeDtypeStruct((N_CHIPS, s_len, FF_SHARD), BF16),
            jax.ShapeDtypeStruct((s_len, D_MODEL), BF16),
            jax.ShapeDtypeStruct((s_len, D_MODEL), BF16),
            jax.ShapeDtypeStruct((8, D_MODEL), F32),
        ],
        scratch_shapes=[
            pltpu.VMEM((tm, D_MODEL), BF16),
            pltpu.VMEM((tm, D_MODEL), BF16),
            pltpu.VMEM((tm, D_MODEL), F32),
        ],
        compiler_params=_cparams(56),
        name="ffn_bwd",
    )(x, d, g, wg, wu, wd)


def _wgrad(a, b, scale=1.0, name="wgrad"):
    na, s_len, k_dim = a.shape
    nb, _, n_dim = b.shape
    n = max(na, nb)
    ts = min(s_len, 1024)
    steps = s_len // ts

    def body(a_ref, b_ref, o_ref, acc_s):
        s = pl.program_id(1)

        @pl.when(s == 0)
        def _():
            acc_s[...] = jnp.zeros_like(acc_s)

        acc_s[...] += _dot_tn(a_ref[0].astype(BF16), b_ref[0].astype(BF16))

        @pl.when(s == steps - 1)
        def _():
            o_ref[0] = (acc_s[...] * scale).astype(BF16)

    a_map = (lambda m, s: (m, s, 0)) if na > 1 else (lambda m, s: (0, s, 0))
    b_map = (lambda m, s: (m, s, 0)) if nb > 1 else (lambda m, s: (0, s, 0))
    return pl.pallas_call(
        body,
        grid=(n, steps),
        in_specs=[pl.BlockSpec((1, ts, k_dim), a_map), pl.BlockSpec((1, ts, n_dim), b_map)],
        out_specs=pl.BlockSpec((1, k_dim, n_dim), lambda m, s: (m, 0, 0)),
        out_shape=jax.ShapeDtypeStruct((n, k_dim, n_dim), BF16),
        scratch_shapes=[pltpu.VMEM((k_dim, n_dim), F32)],
        compiler_params=_cparams(56),
        name=name,
    )(a, b)


def _head_sum_matrices():
    lane = lax.broadcasted_iota(jnp.int32, (ATT_W, LANES), 0) // HEAD_DIM
    col = lax.broadcasted_iota(jnp.int32, (ATT_W, LANES), 1)
    bd = (lane == col).astype(BF16)
    return bd, bd.T


def _head_mean(t, bd, bd_t):
    per_head = _dot_split2(t, bd) * (1.0 / HEAD_DIM)
    return _dot_split2(per_head, bd_t)


def _head_rms(x, bd, bd_t):
    per_head = _dot_split2(x * x, bd) * (1.0 / HEAD_DIM)
    r = lax.rsqrt(per_head + EPS)
    rw = _dot_split2(r, bd_t)
    return x * rw, rw


def _log_sigmoid(z):
    return jnp.minimum(z, 0.0) - jnp.log(1.0 + jnp.exp(-jnp.abs(z)))


def _inproj_fwd(x1, g, w_fox, w_fl, w_sb, w_gates, bias, qn, kn, bd, bd_t, tm=256):
    s_len = x1.shape[0]

    def body(x_ref, g_ref, wf_ref, wl_ref, ws_ref, wg_ref, bias_ref, qn_ref, kn_ref, bd_ref, bdt_ref,
             fq_ref, fk_ref, qs_ref, kf_ref, vf_ref, logf_ref, sq_ref, sk_ref, sv_ref, gates_ref):
        xn, _ = _rms(x_ref[...])
        h = (xn * g_ref[...]).astype(BF16)
        zf = _dot(h, wf_ref[...])
        fq = zf[:, 0:ATT_W]
        fk = zf[:, ATT_W:2 * ATT_W]
        fq_ref[...] = fq
        fk_ref[...] = fk
        bd_m = bd_ref[...]
        bdt_m = bdt_ref[...]
        fqn, _ = _head_rms(fq, bd_m, bdt_m)
        fkn, _ = _head_rms(fk, bd_m, bdt_m)
        qs_ref[...] = (fqn * qn_ref[...]).astype(BF16) * QK_SCALE
        kf_ref[...] = (fkn * kn_ref[...]).astype(BF16)
        vf_ref[...] = zf[:, 2 * ATT_W:3 * ATT_W].astype(BF16)
        logf_ref[...] = _log_sigmoid(_dot(h, wl_ref[...]) + bias_ref[...])
        zs = _dot(h, ws_ref[...])
        sq_ref[...] = zs[:, 0:ATT_W].astype(BF16) * QK_SCALE
        sk_ref[...] = zs[:, ATT_W:2 * ATT_W].astype(BF16)
        sv_ref[...] = zs[:, 2 * ATT_W:3 * ATT_W].astype(BF16)
        gates_ref[...] = _dot(h, wg_ref[...])

    row = lambda i: (i, 0)
    full = lambda i: (0, 0)
    att = lambda dt: jax.ShapeDtypeStruct((s_len, ATT_W), dt)
    return pl.pallas_call(
        body,
        grid=(s_len // tm,),
        in_specs=[
            pl.BlockSpec((tm, D_MODEL), row),
            pl.BlockSpec((1, D_MODEL), full),
            pl.BlockSpec((D_MODEL, 3 * ATT_W), full),
            pl.BlockSpec((D_MODEL, LANES), full),
            pl.BlockSpec((D_MODEL, 3 * ATT_W), full),
            pl.BlockSpec((D_MODEL, 2 * D_MODEL), full),
            pl.BlockSpec((1, LANES), full),
            pl.BlockSpec((1, ATT_W), full),
            pl.BlockSpec((1, ATT_W), full),
            pl.BlockSpec((ATT_W, LANES), full),
            pl.BlockSpec((LANES, ATT_W), full),
        ],
        out_specs=[
            pl.BlockSpec((tm, ATT_W), row), pl.BlockSpec((tm, ATT_W), row),
            pl.BlockSpec((tm, ATT_W), row), pl.BlockSpec((tm, ATT_W), row), pl.BlockSpec((tm, ATT_W), row),
            pl.BlockSpec((tm, LANES), row),
            pl.BlockSpec((tm, ATT_W), row), pl.BlockSpec((tm, ATT_W), row), pl.BlockSpec((tm, ATT_W), row),
            pl.BlockSpec((tm, 2 * D_MODEL), row),
        ],
        out_shape=[
            att(F32), att(F32), att(BF16), att(BF16), att(BF16),
            jax.ShapeDtypeStruct((s_len, LANES), F32),
            att(BF16), att(BF16), att(BF16),
            jax.ShapeDtypeStruct((s_len, 2 * D_MODEL), F32),
        ],
        compiler_params=_cparams(56),
        name="inproj_fwd",
    )(x1, g, w_fox, w_fl, w_sb, w_gates, bias, qn, kn, bd, bd_t)


def _tri(n, kind):
    r = lax.broadcasted_iota(jnp.int32, (n, n), 0)
    c = lax.broadcasted_iota(jnp.int32, (n, n), 1)
    m = {"row_ge_col": r >= c, "row_le_col": r <= c, "row_gt_col": r > c, "row_lt_col": r < c}[kind]
    return m.astype(BF16)


def _cumsum_rows(x, reverse, tm=256):
    s_len = x.shape[0]
    nb = s_len // tm
    tri = _tri(tm, "row_le_col" if reverse else "row_ge_col")
    edge = 0 if reverse else tm - 1

    def body(x_ref, tri_ref, o_ref, carry_s):
        @pl.when(pl.program_id(0) == 0)
        def _():
            carry_s[...] = jnp.zeros_like(carry_s)

        hi, mid, lo = _split3(x_ref[...])
        t = tri_ref[...]
        y = _dot(t, hi) + _dot(t, mid) + _dot(t, lo) + carry_s[...]
        o_ref[...] = y
        carry_s[...] = y[edge:edge + 1, :]

    order = (lambda i: (nb - 1 - i, 0)) if reverse else (lambda i: (i, 0))
    return pl.pallas_call(
        body,
        grid=(nb,),
        in_specs=[pl.BlockSpec((tm, LANES), order), pl.BlockSpec((tm, tm), lambda i: (0, 0))],
        out_specs=pl.BlockSpec((tm, LANES), order),
        out_shape=jax.ShapeDtypeStruct((s_len, LANES), F32),
        scratch_shapes=[pltpu.VMEM((1, LANES), F32)],
        name="cumsum_rev" if reverse else "cumsum_fwd",
    )(x, tri)


def _unblocked_t(t4):
    _, nb, _, blk = t4.shape
    return t4.transpose(1, 3, 0, 2).reshape(nb * blk, ATT_W)


def _blocked_rows(t, blk):
    return t.reshape(t.shape[0] // blk, blk, t.shape[1])


def _pair_rows_t(f8, blk):
    nb = f8.shape[0] // blk
    t = f8.reshape(nb, blk, N_PAIRS, 2).transpose(2, 0, 3, 1)
    return jnp.pad(t, ((0, 0), (0, 0), (0, 6), (0, 0)))


def _unpair_rows_t(t4):
    _, nb, _, blk = t4.shape
    return t4[:, :, 0:2, :].transpose(1, 3, 0, 2).reshape(nb * blk, N_HEADS)


def _head_masks(tq):
    lane = lax.broadcasted_iota(jnp.int32, (tq, PAIR_W), 1)
    return lane < HEAD_DIM


def _causal_mask(tq, tk, offset, strict):
    d = lax.broadcasted_iota(jnp.int32, (tq, tk), 1) - lax.broadcasted_iota(jnp.int32, (tq, tk), 0)
    return (d < offset) if strict else (d <= offset)


def _heads_of(ref, first):
    t = ref[...]
    zero = jnp.zeros_like(t)
    return [jnp.where(first, t, zero), jnp.where(first, zero, t)]


def _head_cols(ref):
    t = ref[...]
    return [t[:, 0:1], t[:, HEAD_DIM:HEAD_DIM + 1]]


def _att_specs(s_len):
    tq, tk = ATT_Q_BLOCK, ATT_BLOCK
    nq, nk = s_len // tq, s_len // tk
    return dict(
        nq=nq,
        q=pl.BlockSpec((tq, PAIR_W), lambda p, i: (i, p)),
        k_t=pl.BlockSpec((1, nk, PAIR_W, tk), lambda p, i: (p, 0, 0, 0)),
        k_rows=pl.BlockSpec((nk, tk, PAIR_W), lambda p, i: (0, 0, p)),
        f_t=pl.BlockSpec((1, nk, 8, tk), lambda p, i: (p, 0, 0, 0)),
        wide=jax.ShapeDtypeStruct((s_len, ATT_W), F32),
        k_t_out=jax.ShapeDtypeStruct((N_PAIRS, nk, PAIR_W, tk), F32),
        f_t_out=jax.ShapeDtypeStruct((N_PAIRS, nk, 8, tk), F32),
        acc=pltpu.VMEM((2, tq, PAIR_W), F32),
    )


def _fox_fwd(qs, k3, v3, fw, ft4):
    sp = _att_specs(qs.shape[0])
    tq, tk = ATT_Q_BLOCK, ATT_BLOCK
    ratio = tq // tk

    def body(q_ref, k_ref, v_ref, fw_ref, ft_ref, y_ref, lse_ref, acc_ref, max_ref, sum_ref):
        i = pl.program_id(1)
        first = _head_masks(tq)
        qh = _heads_of(q_ref, first)
        fqh = _head_cols(fw_ref)
        acc_ref[...] = jnp.zeros_like(acc_ref)
        sum_ref[...] = jnp.zeros_like(sum_ref)
        max_ref[...] = jnp.full(max_ref.shape, NEG_BIG, F32)

        def logits(j, shift, diag):
            k, fk = k_ref[j], ft_ref[0, j]
            raw = [_dot_nt(qh[n], k) for n in range(2)]
            out = []
            for n in range(2):
                s = raw[n] + (shift[n] - fk[n:n + 1, :])
                if diag:
                    s = jnp.where(_causal_mask(tq, tk, i * tq - j * tk, strict=False), s, NEG_BIG)
                out.append(s)
            return out

        def max_pass(j, diag):
            ss = logits(j, fqh, diag)
            for n in range(2):
                max_ref[n] = jnp.maximum(max_ref[n], ss[n])

        def sum_pass(j, shift, diag):
            ps = [jnp.exp(s) for s in logits(j, shift, diag)]
            v = v_ref[j]
            for n in range(2):
                sum_ref[n] += ps[n]
            for n in range(2):
                acc_ref[n] += _dot(ps[n].astype(BF16), v)

        def over_blocks(step):
            def one(j, c):
                step(j, False)
                return c
            lax.fori_loop(0, ratio * i, one, 0)
            for d in range(ratio):
                step(ratio * i + d, True)

        over_blocks(max_pass)
        m = [jnp.max(max_ref[n], axis=-1, keepdims=True) for n in range(2)]
        shift = [fqh[n] - m[n] for n in range(2)]
        over_blocks(lambda j, diag: sum_pass(j, shift, diag))
        l = [jnp.sum(sum_ref[n], axis=-1, keepdims=True) for n in range(2)]
        y_ref[...] = jnp.where(first, acc_ref[0] / l[0], acc_ref[1] / l[1])
        lse_ref[...] = jnp.where(first, m[0] + jnp.log(l[0]), m[1] + jnp.log(l[1]))

    tile = pltpu.VMEM((2, tq, tk), F32)
    return pl.pallas_call(
        body,
        grid=(N_PAIRS, sp["nq"]),
        in_specs=[sp["q"], sp["k_rows"], sp["k_rows"], sp["q"], sp["f_t"]],
        out_specs=[sp["q"], sp["q"]],
        out_shape=[sp["wide"], sp["wide"]],
        scratch_shapes=[sp["acc"], tile, tile],
        compiler_params=_cparams(56),
        name="fox_fwd",
    )(qs, k3, v3, fw, ft4)


def _fox_bwd(qs, k3, v3, dy, y, lse, fw, ft4):
    sp = _att_specs(qs.shape[0])
    tq, tk = ATT_Q_BLOCK, ATT_BLOCK
    ratio = tq // tk

    def body(q_ref, k_ref, v_ref, dy_ref, y_ref, lse_ref, fw_ref, ft_ref,
             dq_ref, dfq_ref, dkt_ref, dvt_ref, dft_ref, acc_ref):
        i = pl.program_id(1)

        @pl.when(i == 0)
        def _():
            dkt_ref[...] = jnp.zeros_like(dkt_ref)
            dvt_ref[...] = jnp.zeros_like(dvt_ref)
            dft_ref[...] = jnp.zeros_like(dft_ref)

        first = _head_masks(tq)
        qh = _heads_of(q_ref, first)
        dyv = dy_ref[...]
        dyb = dyv.astype(BF16)
        zero = jnp.zeros_like(dyb)
        dyh = [jnp.where(first, dyb, zero), jnp.where(first, zero, dyb)]
        prod = dyv * y_ref[...]
        zf = jnp.zeros_like(prod)
        delta = [jnp.sum(jnp.where(first, prod, zf), axis=-1, keepdims=True),
                 jnp.sum(jnp.where(first, zf, prod), axis=-1, keepdims=True)]
        fqh = _head_cols(fw_ref)
        lseh = _head_cols(lse_ref)
        shift = [fqh[n] - lseh[n] for n in range(2)]
        acc_ref[...] = jnp.zeros_like(acc_ref)

        def block(j, rows, diag):
            mask = _causal_mask(tq, tk, i * tq - j * tk, strict=False) if diag else None
            k, v, fk = k_ref[j], v_ref[j], ft_ref[0, j]
            logits = [_dot_nt(qh[n], k) for n in range(2)]
            dps = [_dot_nt(dyh[n], v) for n in range(2)]
            pbs, dsbs, out = [], [], []
            for n in range(2):
                p = jnp.exp(logits[n] + (shift[n] - fk[n:n + 1, :]))
                if diag:
                    p = jnp.where(mask, p, 0.0)
                ds = p * (dps[n] - delta[n])
                pbs.append(p.astype(BF16))
                dsbs.append(ds.astype(BF16))
                out.append(rows[n] + jnp.sum(ds, axis=-1, keepdims=True))
                dft_ref[0, j, n:n + 1, :] -= _colsum(ds)
            for n in range(2):
                acc_ref[n] += _dot(dsbs[n], k)
            dkt_ref[0, j] += _dot_tn(qh[0], dsbs[0]) + _dot_tn(qh[1], dsbs[1])
            dvt_ref[0, j] += _dot_tn(dyh[0], pbs[0]) + _dot_tn(dyh[1], pbs[1])
            return tuple(out)

        rows = (jnp.zeros((tq, 1), F32),) * 2
        rows = lax.fori_loop(0, ratio * i, lambda j, c: block(j, c, False), rows)
        for d in range(ratio):
            rows = block(ratio * i + d, rows, True)
        dq_ref[...] = jnp.where(first, acc_ref[0], acc_ref[1])
        dfq_ref[...] = jnp.where(first, rows[0], rows[1])

    return pl.pallas_call(
        body,
        grid=(N_PAIRS, sp["nq"]),
        in_specs=[sp["q"], sp["k_rows"], sp["k_rows"], sp["q"], sp["q"], sp["q"], sp["q"], sp["f_t"]],
        out_specs=[sp["q"], sp["q"], sp["k_t"], sp["k_t"], sp["f_t"]],
        out_shape=[sp["wide"], sp["wide"], sp["k_t_out"], sp["k_t_out"], sp["f_t_out"]],
        scratch_shapes=[sp["acc"]],
        compiler_params=_cparams(56),
        name="fox_bwd",
    )(qs, k3, v3, dy, y, lse, fw, ft4)


SIGN_BIT = 0x80000000


def _sb_terms(z, mask, diag):
    neg_abs = pltpu.bitcast(pltpu.bitcast(z, jnp.uint32) | jnp.uint32(SIGN_BIT), F32)
    lb = jnp.minimum(z, 0.0) - jnp.log(1.0 + jnp.exp(neg_abs))
    l1m = lb - z
    if diag:
        l1m = jnp.where(mask, l1m, 0.0)
    return lb, l1m


def _dot_split2_stacked(x, m2):
    hi, lo = _split2(x)
    return _dot(jnp.concatenate([hi, lo], axis=1), m2)


def _tri_stacked(kind):
    t = _tri(ATT_BLOCK, kind)
    return jnp.concatenate([t, t], axis=0)


def _sb_fwd(qs, k3, v3):
    sp = _att_specs(qs.shape[0])
    tq, tk = ATT_Q_BLOCK, ATT_BLOCK
    ratio = tq // tk
    upper = _tri_stacked("row_gt_col")

    def body(q_ref, k_ref, v_ref, u_ref, y_ref, rtot_ref, acc_ref):
        i = pl.program_id(1)
        first = _head_masks(tq)
        qh = _heads_of(q_ref, first)
        u = u_ref[...]
        acc_ref[...] = jnp.zeros_like(acc_ref)

        def block(j, rs, diag):
            mask = _causal_mask(tq, tk, i * tq - j * tk, strict=True) if diag else None
            k, v = k_ref[j], v_ref[j]
            logits = [_dot_nt(qh[n], k) for n in range(2)]
            terms = [_sb_terms(z, mask, diag) for z in logits]
            right = [_dot_split2_stacked(l1m, u) for _, l1m in terms]
            weights = []
            for n in range(2):
                a = jnp.exp(terms[n][0] + right[n] + rs[n])
                if diag:
                    a = jnp.where(mask, a, 0.0)
                weights.append(a.astype(BF16))
            for n in range(2):
                acc_ref[n] += _dot(weights[n], v)
            return tuple(rs[n] + jnp.sum(terms[n][1], axis=-1, keepdims=True) for n in range(2))

        rs = (jnp.zeros((tq, 1), F32),) * 2
        for d in range(ratio):
            rs = block(ratio * i + (ratio - 1 - d), rs, True)
        rs = lax.fori_loop(0, ratio * i, lambda n, c: block(ratio * i - 1 - n, c, False), rs)
        y_ref[...] = jnp.where(first, acc_ref[0], acc_ref[1])
        rtot_ref[...] = jnp.where(first, rs[0], rs[1])

    return pl.pallas_call(
        body,
        grid=(N_PAIRS, sp["nq"]),
        in_specs=[sp["q"], sp["k_rows"], sp["k_rows"], pl.BlockSpec((2 * tk, tk), lambda p, i: (0, 0))],
        out_specs=[sp["q"], sp["q"]],
        out_shape=[sp["wide"], sp["wide"]],
        scratch_shapes=[sp["acc"]],
        compiler_params=_cparams(56),
        name="sb_fwd",
    )(qs, k3, v3, upper)


def _sb_bwd(qs, k3, v3, dy, rtot):
    sp = _att_specs(qs.shape[0])
    tq, tk = ATT_Q_BLOCK, ATT_BLOCK
    ratio = tq // tk
    lower_in = _tri_stacked("row_le_col")
    lower = _tri(tk, "row_lt_col")

    def body(q_ref, k_ref, v_ref, dy_ref, rtot_ref, li_ref, l_ref, dq_ref, dkt_ref, dvt_ref, acc_ref):
        i = pl.program_id(1)

        @pl.when(i == 0)
        def _():
            dkt_ref[...] = jnp.zeros_like(dkt_ref)
            dvt_ref[...] = jnp.zeros_like(dvt_ref)

        first = _head_masks(tq)
        qh = _heads_of(q_ref, first)
        dyb = dy_ref[...].astype(BF16)
        zero = jnp.zeros_like(dyb)
        dyh = [jnp.where(first, dyb, zero), jnp.where(first, zero, dyb)]
        rtoth = _head_cols(rtot_ref)
        li = li_ref[...]
        lo_tri = l_ref[...]
        acc_ref[...] = jnp.zeros_like(acc_ref)

        def block(j, carry, diag):
            mask = _causal_mask(tq, tk, i * tq - j * tk, strict=True) if diag else None
            k, v = k_ref[j], v_ref[j]
            logits = [_dot_nt(qh[n], k) for n in range(2)]
            das = [_dot_nt(dyh[n], v) for n in range(2)]
            terms = [_sb_terms(z, mask, diag) for z in logits]
            upto = [_dot_split2_stacked(l1m, li) for _, l1m in terms]
            des, weights = [], []
            for n in range(2):
                a = jnp.exp(terms[n][0] + ((rtoth[n] - carry[2 * n]) - upto[n]))
                if diag:
                    a = jnp.where(mask, a, 0.0)
                des.append(a * das[n])
                weights.append(a.astype(BF16))
            lefts = [_dot(de.astype(BF16), lo_tri) for de in des]
            dzbs, out = [], []
            for n in range(2):
                beta = jnp.exp(terms[n][0])
                dz = des[n] - (des[n] + (carry[2 * n + 1] + lefts[n])) * beta
                if diag:
                    dz = jnp.where(mask, dz, 0.0)
                dzbs.append(dz.astype(BF16))
                out += [carry[2 * n] + jnp.sum(terms[n][1], axis=-1, keepdims=True),
                        carry[2 * n + 1] + jnp.sum(des[n], axis=-1, keepdims=True)]
            for n in range(2):
                acc_ref[n] += _dot(dzbs[n], k)
            dkt_ref[0, j] += _dot_tn(qh[0], dzbs[0]) + _dot_tn(qh[1], dzbs[1])
            dvt_ref[0, j] += _dot_tn(dyh[0], weights[0]) + _dot_tn(dyh[1], weights[1])
            return tuple(out)

        carry = (jnp.zeros((tq, 1), F32),) * 4
        carry = lax.fori_loop(0, ratio * i, lambda j, c: block(j, c, False), carry)
        for d in range(ratio):
            carry = block(ratio * i + d, carry, True)
        dq_ref[...] = jnp.where(first, acc_ref[0], acc_ref[1])

    return pl.pallas_call(
        body,
        grid=(N_PAIRS, sp["nq"]),
        in_specs=[sp["q"], sp["k_rows"], sp["k_rows"], sp["q"], sp["q"],
                  pl.BlockSpec((2 * tk, tk), lambda p, i: (0, 0)), pl.BlockSpec((tk, tk), lambda p, i: (0, 0))],
        out_specs=[sp["q"], sp["k_t"], sp["k_t"]],
        out_shape=[sp["wide"], sp["k_t_out"], sp["k_t_out"]],
        scratch_shapes=[sp["acc"]],
        compiler_params=_cparams(56),
        name="sb_bwd",
    )(qs, k3, v3, dy, rtot, lower_in, lower)


def _merge_fwd(x1, gates, y_fox, y_sb, w_bf, w_bs, w_out, tm=512):
    s_len = x1.shape[0]

    def body(x_ref, g_ref, yf_ref, ys_ref, wbf_ref, wbs_ref, wo_ref, o_ref):
        g = g_ref[...]
        of = _dot(yf_ref[...].astype(BF16), wbf_ref[...])
        os_ = _dot(ys_ref[...].astype(BF16), wbs_ref[...])
        merged = _sigmoid(g[:, 0:D_MODEL]) * of + _sigmoid(g[:, D_MODEL:]) * os_
        o_ref[...] = x_ref[...] + _dot(merged.astype(BF16), wo_ref[...])

    row = lambda i: (i, 0)
    full = lambda i: (0, 0)
    return pl.pallas_call(
        body,
        grid=(s_len // tm,),
        in_specs=[
            pl.BlockSpec((tm, D_MODEL), row),
            pl.BlockSpec((tm, 2 * D_MODEL), row),
            pl.BlockSpec((tm, ATT_W), row),
            pl.BlockSpec((tm, ATT_W), row),
            pl.BlockSpec((ATT_W, D_MODEL), full),
            pl.BlockSpec((ATT_W, D_MODEL), full),
            pl.BlockSpec((D_MODEL, D_MODEL), full),
        ],
        out_specs=pl.BlockSpec((tm, D_MODEL), row),
        out_shape=jax.ShapeDtypeStruct((s_len, D_MODEL), F32),
        compiler_params=_cparams(48),
        name="merge_fwd",
    )(x1, gates, y_fox, y_sb, w_bf, w_bs, w_out)


def _merge_bwd(dx2, gates, y_fox, y_sb, w_bf, w_bs, w_out, tm=512):
    s_len = dx2.shape[0]

    def body(d_ref, g_ref, yf_ref, ys_ref, wbf_ref, wbs_ref, wo_ref,
             dyf_ref, dys_ref, dg_ref, dof_ref, dos_ref, m_ref, dbf_ref):
        dbf = d_ref[...].astype(BF16)
        dbf_ref[...] = dbf
        dm = _dot_nt(dbf, wo_ref[...])
        g = g_ref[...]
        of = _dot(yf_ref[...].astype(BF16), wbf_ref[...])
        os_ = _dot(ys_ref[...].astype(BF16), wbs_ref[...])
        sf = _sigmoid(g[:, 0:D_MODEL])
        ss = _sigmoid(g[:, D_MODEL:])
        m_ref[...] = (sf * of + ss * os_).astype(BF16)
        d_of = (dm * sf).astype(BF16)
        d_os = (dm * ss).astype(BF16)
        dof_ref[...] = d_of
        dos_ref[...] = d_os
        dg_ref[:, 0:D_MODEL] = (dm * of * sf * (1.0 - sf)).astype(BF16)
        dg_ref[:, D_MODEL:] = (dm * os_ * ss * (1.0 - ss)).astype(BF16)
        dyf_ref[...] = _dot_nt(d_of, wbf_ref[...])
        dys_ref[...] = _dot_nt(d_os, wbs_ref[...])

    row = lambda i: (i, 0)
    full = lambda i: (0, 0)
    return pl.pallas_call(
        body,
        grid=(s_len // tm,),
        in_specs=[
            pl.BlockSpec((tm, D_MODEL), row),
            pl.BlockSpec((tm, 2 * D_MODEL), row),
            pl.BlockSpec((tm, ATT_W), row),
            pl.BlockSpec((tm, ATT_W), row),
            pl.BlockSpec((ATT_W, D_MODEL), full),
            pl.BlockSpec((ATT_W, D_MODEL), full),
            pl.BlockSpec((D_MODEL, D_MODEL), full),
        ],
        out_specs=[
            pl.BlockSpec((tm, ATT_W), row), pl.BlockSpec((tm, ATT_W), row),
            pl.BlockSpec((tm, 2 * D_MODEL), row),
            pl.BlockSpec((tm, D_MODEL), row), pl.BlockSpec((tm, D_MODEL), row),
            pl.BlockSpec((tm, D_MODEL), row), pl.BlockSpec((tm, D_MODEL), row),
        ],
        out_shape=[
            jax.ShapeDtypeStruct((s_len, ATT_W), F32), jax.ShapeDtypeStruct((s_len, ATT_W), F32),
            jax.ShapeDtypeStruct((s_len, 2 * D_MODEL), BF16),
            jax.ShapeDtypeStruct((s_len, D_MODEL), BF16), jax.ShapeDtypeStruct((s_len, D_MODEL), BF16),
            jax.ShapeDtypeStruct((s_len, D_MODEL), BF16), jax.ShapeDtypeStruct((s_len, D_MODEL), BF16),
        ],
        compiler_params=_cparams(56),
        name="merge_bwd",
    )(dx2, gates, y_fox, y_sb, w_bf, w_bs, w_out)


def _ple_loss(x3, p, g, w_pg, w_pp, target, tm=512):
    s_len = x3.shape[0]
    inv_d = 1.0 / D_MODEL

    def body(x_ref, p_ref, g_ref, wpg_ref, wpp_ref, t_ref,
             dx_ref, du_ref, dt_ref, hn_ref, dg_ref, loss_ref):
        @pl.when(pl.program_id(0) == 0)
        def _():
            dg_ref[...] = jnp.zeros_like(dg_ref)
            loss_ref[...] = jnp.zeros_like(loss_ref)

        x = x_ref[...]
        xn, r = _rms(x)
        gain = g_ref[...]
        hn = (xn * gain).astype(BF16)
        hn_ref[...] = hn
        sg = _sigmoid(_dot(hn, wpg_ref[...]))
        t = _dot(p_ref[...].astype(BF16), wpp_ref[...])
        err = x + sg * t - t_ref[...]
        sq = jnp.sum(_colsum(err * err), axis=-1, keepdims=True)
        loss_ref[...] += (0.5 * inv_d) * sq
        dy = err * inv_d
        du = (dy * t * sg * (1.0 - sg)).astype(BF16)
        du_ref[...] = du
        dt_ref[...] = (dy * sg).astype(BF16)
        dh = _dot_nt(du, wpg_ref[...])
        dx_ref[...] = dy + _rms_bwd(dh, xn, r, gain)
        dg_ref[0:1, :] += _colsum(dh * xn)

    row = lambda i: (i, 0)
    full = lambda i: (0, 0)
    bf = jax.ShapeDtypeStruct((s_len, D_MODEL), BF16)
    return pl.pallas_call(
        body,
        grid=(s_len // tm,),
        in_specs=[
            pl.BlockSpec((tm, D_MODEL), row),
            pl.BlockSpec((tm, PLE_DIM), row),
            pl.BlockSpec((1, D_MODEL), full),
            pl.BlockSpec((D_MODEL, D_MODEL), full),
            pl.BlockSpec((PLE_DIM, D_MODEL), full),
            pl.BlockSpec((tm, D_MODEL), row),
        ],
        out_specs=[
            pl.BlockSpec((tm, D_MODEL), row), pl.BlockSpec((tm, D_MODEL), row),
            pl.BlockSpec((tm, D_MODEL), row), pl.BlockSpec((tm, D_MODEL), row),
            pl.BlockSpec((8, D_MODEL), full), pl.BlockSpec((8, LANES), full),
        ],
        out_shape=[
            jax.ShapeDtypeStruct((s_len, D_MODEL), F32), bf, bf, bf,
            jax.ShapeDtypeStruct((8, D_MODEL), F32), jax.ShapeDtypeStruct((8, LANES), F32),
        ],
        compiler_params=_cparams(48),
        name="ple_loss",
    )(x3, p, g, w_pg, w_pp, target)


def _qknorm_bwd(fq, fk, dqs, dk, dv, qn, kn, bd, bd_t, tm=256):
    s_len = fq.shape[0]

    def body(fq_ref, fk_ref, dq_ref, dk_ref, dv_ref, qn_ref, kn_ref, bd_ref, bdt_ref,
             dz_ref, dqn_ref, dkn_ref):
        @pl.when(pl.program_id(0) == 0)
        def _():
            dqn_ref[...] = jnp.zeros_like(dqn_ref)
            dkn_ref[...] = jnp.zeros_like(dkn_ref)

        bd_m = bd_ref[...]
        bdt_m = bdt_ref[...]

        def one(x, dy, gain, dgain_ref):
            xn, rw = _head_rms(x, bd_m, bdt_m)
            dgain_ref[0:1, :] += _colsum(dy * xn)
            dxn = dy * gain
            return rw * (dxn - xn * _head_mean(dxn * xn, bd_m, bdt_m))

        dz_ref[:, 0:ATT_W] = one(fq_ref[...], dq_ref[...] * QK_SCALE, qn_ref[...], dqn_ref).astype(BF16)
        dz_ref[:, ATT_W:2 * ATT_W] = one(fk_ref[...], dk_ref[...], kn_ref[...], dkn_ref).astype(BF16)
        dz_ref[:, 2 * ATT_W:] = dv_ref[...].astype(BF16)

    row = lambda i: (i, 0)
    full = lambda i: (0, 0)
    att = pl.BlockSpec((tm, ATT_W), row)
    return pl.pallas_call(
        body,
        grid=(s_len // tm,),
        in_specs=[att, att, att, att, att,
                  pl.BlockSpec((1, ATT_W), full), pl.BlockSpec((1, ATT_W), full),
                  pl.BlockSpec((ATT_W, LANES), full), pl.BlockSpec((LANES, ATT_W), full)],
        out_specs=[pl.BlockSpec((tm, 3 * ATT_W), row), pl.BlockSpec((8, ATT_W), full), pl.BlockSpec((8, ATT_W), full)],
        out_shape=[jax.ShapeDtypeStruct((s_len, 3 * ATT_W), BF16),
                   jax.ShapeDtypeStruct((8, ATT_W), F32), jax.ShapeDtypeStruct((8, ATT_W), F32)],
        name="qknorm_bwd",
    )(fq, fk, dqs, dk, dv, qn, kn, bd, bd_t)


def _inproj_bwd(x1, dx2, g, dzf, dlogf, logf, dzs, dgates, w_fox, w_fl, w_sb, w_gates, tm=256):
    s_len = x1.shape[0]

    def body(x_ref, d_ref, g_ref, dzf_ref, dlf_ref, lf_ref, dzs_ref, dgt_ref, wf_ref, wl_ref, ws_ref, wg_ref,
             dx_ref, h_ref, dfl_ref, dg_ref, db_ref):
        @pl.when(pl.program_id(0) == 0)
        def _():
            dg_ref[...] = jnp.zeros_like(dg_ref)
            db_ref[...] = jnp.zeros_like(db_ref)

        xn, r = _rms(x_ref[...])
        gain = g_ref[...]
        h_ref[...] = (xn * gain).astype(BF16)
        lane = lax.broadcasted_iota(jnp.int32, (tm, LANES), 1)
        dfl = jnp.where(lane < N_HEADS, dlf_ref[...] * (1.0 - jnp.exp(lf_ref[...])), 0.0)
        db_ref[0:1, :] += _colsum(dfl)
        dflb = dfl.astype(BF16)
        dfl_ref[...] = dflb
        dh = (_dot_nt(dzf_ref[...], wf_ref[...]) + _dot_nt(dflb, wl_ref[...])
              + _dot_nt(dzs_ref[...], ws_ref[...]) + _dot_nt(dgt_ref[...], wg_ref[...]))
        dx_ref[...] = d_ref[...] + _rms_bwd(dh, xn, r, gain)
        dg_ref[0:1, :] += _colsum(dh * xn)

    row = lambda i: (i, 0)
    full = lambda i: (0, 0)
    return pl.pallas_call(
        body,
        grid=(s_len // tm,),
        in_specs=[
            pl.BlockSpec((tm, D_MODEL), row),
            pl.BlockSpec((tm, D_MODEL), row),
            pl.BlockSpec((1, D_MODEL), full),
            pl.BlockSpec((tm, 3 * ATT_W), row),
            pl.BlockSpec((tm, LANES), row),
            pl.BlockSpec((tm, LANES), row),
            pl.BlockSpec((tm, 3 * ATT_W), row),
            pl.BlockSpec((tm, 2 * D_MODEL), row),
            pl.BlockSpec((D_MODEL, 3 * ATT_W), full),
            pl.BlockSpec((D_MODEL, LANES), full),
            pl.BlockSpec((D_MODEL, 3 * ATT_W), full),
            pl.BlockSpec((D_MODEL, 2 * D_MODEL), full),
        ],
        out_specs=[
            pl.BlockSpec((tm, D_MODEL), row), pl.BlockSpec((tm, D_MODEL), row), pl.BlockSpec((tm, LANES), row),
            pl.BlockSpec((8, D_MODEL), full), pl.BlockSpec((8, LANES), full),
        ],
        out_shape=[
            jax.ShapeDtypeStruct((s_len, D_MODEL), F32), jax.ShapeDtypeStruct((s_len, D_MODEL), BF16),
            jax.ShapeDtypeStruct((s_len, LANES), BF16),
            jax.ShapeDtypeStruct((8, D_MODEL), F32), jax.ShapeDtypeStruct((8, LANES), F32),
        ],
        compiler_params=_cparams(56),
        name="inproj_bwd",
    )(x1, dx2, g, dzf, dlogf, logf, dzs, dgates, w_fox, w_fl, w_sb, w_gates)


def _split_w_in(w_in):
    o = 3 * ATT_W
    w_fox = w_in[:, 0:o]
    w_fl = jnp.pad(w_in[:, o:o + N_HEADS], ((0, 0), (0, LANES - N_HEADS)))
    w_sb = w_in[:, o + N_HEADS:2 * o + N_HEADS]
    w_gates = w_in[:, 2 * o + N_HEADS:]
    return w_fox, w_fl, w_sb, w_gates


def _local_grads(x, p, target, small, full):
    blk = ATT_BLOCK
    bd, bd_t = _head_sum_matrices()

    w_fox, w_fl, w_sb, w_gates = _split_w_in(full["w_in"])
    bias = jnp.pad(small["forget_bias"], ((0, 0), (0, LANES - N_HEADS)))
    qn = jnp.tile(small["q_norm"], (1, N_HEADS))
    kn = jnp.tile(small["k_norm"], (1, N_HEADS))

    x1 = _ffn_fwd(x, small["ffn1_norm"], full["ffn1_w_gate"], full["ffn1_w_up"], full["ffn1_w_down"])
    fq, fk, f_qs, f_k, f_v, logf, s_qs, s_k, s_v, gates = _inproj_fwd(
        x1, small["mix_norm"], w_fox, w_fl, w_sb, w_gates, bias, qn, kn, bd, bd_t)
    f_cum = _cumsum_rows(logf, reverse=False)
    f8 = f_cum[:, 0:N_HEADS]
    fw = jnp.repeat(f8, HEAD_DIM, axis=1)
    ft4 = _pair_rows_t(f8, blk)
    f_k3, f_v3 = _blocked_rows(f_k, blk), _blocked_rows(f_v, blk)
    y_fox, lse = _fox_fwd(f_qs, f_k3, f_v3, fw, ft4)
    s_k3, s_v3 = _blocked_rows(s_k, blk), _blocked_rows(s_v, blk)
    y_sb, s_rtot = _sb_fwd(s_qs, s_k3, s_v3)
    x2 = _merge_fwd(x1, gates, y_fox, y_sb, full["w_branch_fox"], full["w_branch_sb"], full["w_out"])
    x3 = _ffn_fwd(x2, small["ffn2_norm"], full["ffn2_w_gate"], full["ffn2_w_up"], full["ffn2_w_down"])

    dx3, du_ple, dt_ple, hn_ple, dg_ple, loss_sum = _ple_loss(
        x3, p, small["ple_norm"], full["w_ple_gate"], full["w_ple_proj"], target)
    dx2, u2, da2, db2, h_ffn2, d3_bf, dg_ffn2 = _ffn_bwd(
        x2, dx3, small["ffn2_norm"], full["ffn2_w_gate"], full["ffn2_w_up"], full["ffn2_w_down"])
    dy_fox, dy_sb, dgates, d_of, d_os, merged, d2_bf = _merge_bwd(
        dx2, gates, y_fox, y_sb, full["w_branch_fox"], full["w_branch_sb"], full["w_out"])

    f_dqs, dfq_w, f_dkt4, f_dvt4, dft4 = _fox_bwd(f_qs, f_k3, f_v3, dy_fox, y_fox, lse, fw, ft4)
    s_dqs, s_dkt4, s_dvt4 = _sb_bwd(s_qs, s_k3, s_v3, dy_sb, s_rtot)

    dzf, dqn8, dkn8 = _qknorm_bwd(fq, fk, f_dqs, _unblocked_t(f_dkt4), _unblocked_t(f_dvt4), qn, kn, bd, bd_t)
    dzs = jnp.concatenate([s_dqs * QK_SCALE, _unblocked_t(s_dkt4), _unblocked_t(s_dvt4)], axis=1).astype(BF16)
    df8 = _unpair_rows_t(dft4) + dfq_w[:, ::HEAD_DIM]
    dlogf = _cumsum_rows(jnp.pad(df8, ((0, 0), (0, LANES - N_HEADS))), reverse=True)
    dx1, h_mix, dfl, dg_mix, dbias8 = _inproj_bwd(
        x1, dx2, small["mix_norm"], dzf, dlogf, logf, dzs, dgates, w_fox, w_fl, w_sb, w_gates)
    grad_x, u1, da1, db1, h_ffn1, d1_bf, dg_ffn1 = _ffn_bwd(
        x, dx1, small["ffn1_norm"], full["ffn1_w_gate"], full["ffn1_w_up"], full["ffn1_w_down"])

    one = lambda t: t[None]
    gw = {}
    gw["ffn1_w_gate"] = _wgrad(one(h_ffn1), da1, name="wgrad_ffn1_gate")
    gw["ffn1_w_up"] = _wgrad(one(h_ffn1), db1, name="wgrad_ffn1_up")
    gw["ffn1_w_down"] = _wgrad(u1, one(d1_bf), scale=0.5, name="wgrad_ffn1_down")
    gw["ffn2_w_gate"] = _wgrad(one(h_ffn2), da2, name="wgrad_ffn2_gate")
    gw["ffn2_w_up"] = _wgrad(one(h_ffn2), db2, name="wgrad_ffn2_up")
    gw["ffn2_w_down"] = _wgrad(u2, one(d3_bf), scale=0.5, name="wgrad_ffn2_down")
    g_fox = _wgrad(one(h_mix), one(dzf), name="wgrad_in_fox")[0]
    g_fl = _wgrad(one(h_mix), one(dfl), name="wgrad_in_forget")[0]
    g_sb = _wgrad(one(h_mix), one(dzs), name="wgrad_in_sb")[0]
    g_gt = _wgrad(one(h_mix), one(dgates), name="wgrad_in_gates")[0]
    gw["w_in"] = jnp.concatenate([g_fox, g_fl[:, 0:N_HEADS], g_sb, g_gt], axis=1)
    gw["w_branch_fox"] = _wgrad(one(y_fox), one(d_of), name="wgrad_branch_fox")[0]
    gw["w_branch_sb"] = _wgrad(one(y_sb), one(d_os), name="wgrad_branch_sb")[0]
    gw["w_out"] = _wgrad(one(merged), one(d2_bf), name="wgrad_out")[0]
    gw["w_ple_gate"] = _wgrad(one(hn_ple), one(du_ple), name="wgrad_ple_gate")[0]
    gw["w_ple_proj"] = _wgrad(one(p), one(dt_ple), name="wgrad_ple_proj")[0]

    fold = lambda t: jnp.sum(t[0:1].reshape(N_HEADS, HEAD_DIM), axis=0, keepdims=True)
    gs = {
        "ffn1_norm": dg_ffn1[0:1], "mix_norm": dg_mix[0:1], "ffn2_norm": dg_ffn2[0:1], "ple_norm": dg_ple[0:1],
        "forget_bias": dbias8[0:1, 0:N_HEADS], "q_norm": fold(dqn8), "k_norm": fold(dkn8),
    }
    return loss_sum, grad_x, gw, gs


def _position():
    return lax.axis_index("x"), lax.axis_index("y"), lax.axis_index("c")


def _other_chips(x, y):
    return [(1 - x, y), (x, 1 - y), (1 - x, 1 - y)]


ANY = pl.BlockSpec(memory_space=pl.ANY)


def _place_own_shard(w, q):
    rows, cols = w.shape
    tr = _row_block(rows, cols * 4, budget=2 * MIB)

    def body(q_ref, w_ref, o_ref):
        o_ref[0] = w_ref[...].astype(BF16)

    return pl.pallas_call(
        body,
        grid_spec=pltpu.PrefetchScalarGridSpec(
            num_scalar_prefetch=1,
            grid=(rows // tr,),
            in_specs=[pl.BlockSpec((tr, cols), lambda i, q_ref: (i, 0))],
            out_specs=pl.BlockSpec((1, tr, cols), lambda i, q_ref: (q_ref[0], i, 0)),
        ),
        out_shape=jax.ShapeDtypeStruct((N_CHIPS, rows, cols), BF16),
        name="place_own_shard",
    )(q, w)


def _allgather_weights(slots):
    n = len(slots)

    def body(*refs):
        bufs = refs[n:2 * n]
        send_sems, recv_sems = refs[2 * n:]
        x, y, c = _position()
        q = 2 * x + y
        chips = _other_chips(x, y)
        sibling = (x, y, 1 - c)

        def half(a, slot, which):
            r2 = slots[a].shape[1] // 2
            return bufs[a].at[slot, pl.ds(which * r2, r2), :]

        def copy(a, k, region, to):
            return pltpu.make_async_remote_copy(
                src_ref=region, dst_ref=region, send_sem=send_sems.at[6 * a + k], recv_sem=recv_sems.at[6 * a + k],
                device_id=to, device_id_type=MESH)

        sent = []
        for a in range(n):
            for k, (tx, ty) in enumerate(chips):
                cp = copy(a, k, half(a, q, c), (tx, ty, c))
                cp.start()
                sent.append(cp)
        for a in range(n):
            for k, (tx, ty) in enumerate(chips):
                landed = half(a, 2 * tx + ty, c)
                copy(a, k, landed, (tx, ty, c)).wait_recv()
                fwd = copy(a, 3 + k, landed, sibling)
                fwd.start()
                sent.append(fwd)
        for a in range(n):
            for k, (tx, ty) in enumerate(chips):
                copy(a, 3 + k, half(a, 2 * tx + ty, 1 - c), sibling).wait_recv()
        for cp in sent:
            cp.wait_send()

    return pl.pallas_call(
        body,
        in_specs=[ANY] * n,
        out_specs=[ANY] * n,
        out_shape=[jax.ShapeDtypeStruct(s.shape, s.dtype) for s in slots],
        input_output_aliases={a: a for a in range(n)},
        scratch_shapes=[pltpu.SemaphoreType.DMA((6 * n,)), pltpu.SemaphoreType.DMA((6 * n,))],
        name="allgather_weights",
    )(*slots)


def _exchange_pair_halves(grads):
    n = len(grads)

    def body(*refs):
        ins, outs = refs[0:n], refs[n:2 * n]
        send_sems, recv_sems = refs[2 * n:]
        x, y, c = _position()
        copies = []
        for a in range(n):
            r2 = grads[a].shape[1] // 2
            cp = pltpu.make_async_remote_copy(
                src_ref=ins[a].at[:, pl.ds((1 - c) * r2, r2), :], dst_ref=outs[a],
                send_sem=send_sems.at[a], recv_sem=recv_sems.at[a], device_id=(x, y, 1 - c), device_id_type=MESH)
            cp.start()
            copies.append(cp)
        for cp in copies:
            cp.wait()

    return pl.pallas_call(
        body,
        in_specs=[ANY] * n,
        out_specs=[ANY] * n,
        out_shape=[jax.ShapeDtypeStruct((N_CHIPS, g.shape[1] // 2, g.shape[2]), g.dtype) for g in grads],
        scratch_shapes=[pltpu.SemaphoreType.DMA((n,)), pltpu.SemaphoreType.DMA((n,))],
        name="rs_pair_exchange",
    )(*grads)


def _scatter_to_owner_chips(pairs):
    n = len(pairs)

    def body(*refs):
        ins, outs = refs[0:n], refs[n:2 * n]
        send_sems, recv_sems, local_sems = refs[2 * n:]
        x, y, c = _position()
        q = 2 * x + y
        chips = _other_chips(x, y)
        started = []
        for a in range(n):
            mine = pltpu.make_async_copy(ins[a].at[q], outs[a].at[q], local_sems.at[a])
            mine.start()
            started.append(mine)
            for k, (tx, ty) in enumerate(chips):
                cp = pltpu.make_async_remote_copy(
                    src_ref=ins[a].at[2 * tx + ty], dst_ref=outs[a].at[q],
                    send_sem=send_sems.at[3 * a + k], recv_sem=recv_sems.at[3 * a + k],
                    device_id=(tx, ty, c), device_id_type=MESH)
                cp.start()
                started.append(cp)
        for cp in started:
            cp.wait()

    return pl.pallas_call(
        body,
        in_specs=[ANY] * n,
        out_specs=[ANY] * n,
        out_shape=[jax.ShapeDtypeStruct(p.shape, p.dtype) for p in pairs],
        scratch_shapes=[pltpu.SemaphoreType.DMA((3 * n,)), pltpu.SemaphoreType.DMA((3 * n,)),
                        pltpu.SemaphoreType.DMA((n,))],
        name="rs_scatter",
    )(*pairs)


def _join_halves(shards):
    n = len(shards)

    def body(*refs):
        bufs = refs[n:2 * n]
        send_sems, recv_sems = refs[2 * n:]
        x, y, c = _position()
        started = []
        for a in range(n):
            r2 = shards[a].shape[0] // 2
            mine = bufs[a].at[pl.ds(c * r2, r2), :]
            cp = pltpu.make_async_remote_copy(
                src_ref=mine, dst_ref=mine, send_sem=send_sems.at[a], recv_sem=recv_sems.at[a],
                device_id=(x, y, 1 - c), device_id_type=MESH)
            cp.start()
            started.append(cp)
        for cp in started:
            cp.wait()

    return pl.pallas_call(
        body,
        in_specs=[ANY] * n,
        out_specs=[ANY] * n,
        out_shape=[jax.ShapeDtypeStruct(t.shape, t.dtype) for t in shards],
        input_output_aliases={a: a for a in range(n)},
        scratch_shapes=[pltpu.SemaphoreType.DMA((n,)), pltpu.SemaphoreType.DMA((n,))],
        name="rs_join_halves",
    )(*shards)


def _add_pair(g, got, c):
    _, r2, cols = got.shape

    def body(c_ref, g_ref, got_ref, o_ref):
        o_ref[...] = (g_ref[...].astype(F32) + got_ref[...].astype(F32)).astype(BF16)

    spec = pl.BlockSpec((1, r2, cols), lambda s, c_ref: (s, 0, 0))
    return pl.pallas_call(
        body,
        grid_spec=pltpu.PrefetchScalarGridSpec(
            num_scalar_prefetch=1,
            grid=(N_CHIPS,),
            in_specs=[pl.BlockSpec((1, r2, cols), lambda s, c_ref: (s, c_ref[0], 0)), spec],
            out_specs=spec,
        ),
        out_shape=jax.ShapeDtypeStruct(got.shape, BF16),
        name="rs_add_pair",
    )(c, g, got)


def _add_chips(parts, c):
    _, r2, cols = parts.shape

    def body(c_ref, p0, p1, p2, p3, o_ref):
        o_ref[...] = ((p0[0].astype(F32) + p1[0].astype(F32)) + p2[0].astype(F32)) + p3[0].astype(F32)

    specs = [pl.BlockSpec((1, r2, cols), functools.partial(lambda i, c_ref, s: (s, 0, 0), s=s))
             for s in range(N_CHIPS)]
    return pl.pallas_call(
        body,
        grid_spec=pltpu.PrefetchScalarGridSpec(
            num_scalar_prefetch=1,
            grid=(1,),
            in_specs=specs,
            out_specs=pl.BlockSpec((r2, cols), lambda i, c_ref: (c_ref[0], 0)),
        ),
        out_shape=jax.ShapeDtypeStruct((2 * r2, cols), F32),
        name="rs_add_chips",
    )(c, parts, parts, parts, parts)


def _allreduce_small(part):
    shape = part.shape

    def body(in_ref, out_ref, gather_ref, send_sems, recv_sems):
        x, y, c = _position()
        me = 4 * x + 2 * y + c
        relations = [(a, b, d) for a in (0, 1) for b in (0, 1) for d in (0, 1)][1:]
        flip = lambda v, f: 1 - v if f else v
        copies = []
        for k, (a, b, d) in enumerate(relations):
            cp = pltpu.make_async_remote_copy(
                src_ref=in_ref, dst_ref=gather_ref.at[me], send_sem=send_sems.at[k], recv_sem=recv_sems.at[k],
                device_id=(flip(x, a), flip(y, b), flip(c, d)), device_id_type=MESH)
            cp.start()
            copies.append(cp)
        gather_ref[me] = in_ref[...]
        for cp in copies:
            cp.wait()
        total = gather_ref[0]
        for dev in range(1, 8):
            total = total + gather_ref[dev]
        out_ref[...] = total

    vmem = pl.BlockSpec(memory_space=pltpu.VMEM)
    return pl.pallas_call(
        body,
        in_specs=[vmem],
        out_specs=vmem,
        out_shape=jax.ShapeDtypeStruct(shape, F32),
        scratch_shapes=[pltpu.VMEM((8,) + shape, F32), pltpu.SemaphoreType.DMA((7,)), pltpu.SemaphoreType.DMA((7,))],
        name="allreduce_small",
    )(part)


def _adamw(w, g, m, v):
    rows, cols = w.shape
    tr = _row_block(rows, cols * 4, budget=MIB)
    c1 = 1.0 / (1.0 - ADAM_B1 ** ADAM_STEP)
    c2 = 1.0 / (1.0 - ADAM_B2 ** ADAM_STEP)

    def body(w_ref, g_ref, m_ref, v_ref, d_ref, nm_ref, nv_ref):
        g_ = g_ref[...]
        nm = ADAM_B1 * m_ref[...] + (1.0 - ADAM_B1) * g_
        nv = ADAM_B2 * v_ref[...] + (1.0 - ADAM_B2) * (g_ * g_)
        nm_ref[...] = nm
        nv_ref[...] = nv
        d_ref[...] = -ADAM_LR * ((nm * c1) / (jnp.sqrt(nv * c2) + ADAM_EPS) + ADAM_WD * w_ref[...])

    spec = pl.BlockSpec((tr, cols), lambda i: (i, 0))
    out = jax.ShapeDtypeStruct((rows, cols), F32)
    return pl.pallas_call(
        body,
        grid=(rows // tr,),
        in_specs=[spec] * 4,
        out_specs=[spec] * 3,
        out_shape=[out] * 3,
        name="adamw",
    )(w, g, m, v)


BIG = ["ffn1_w_gate", "ffn1_w_up", "ffn1_w_down", "w_in", "w_branch_fox", "w_branch_sb", "w_out",
       "ffn2_w_gate", "ffn2_w_up", "ffn2_w_down", "w_ple_gate", "w_ple_proj"]
SMALL = ["ffn1_norm", "mix_norm", "ffn2_norm", "ple_norm", "forget_bias", "q_norm", "k_norm"]
COLUMN_SHARDED = ["ffn1_w_gate", "ffn1_w_up", "w_in", "w_branch_fox", "w_branch_sb",
                  "ffn2_w_gate", "ffn2_w_up", "w_ple_proj"]
KEPT_AS_SHARDS = ["ffn1_w_gate", "ffn1_w_up", "ffn1_w_down", "ffn2_w_gate", "ffn2_w_up", "ffn2_w_down"]
ORDER = ["ffn1_norm", "ffn1_w_gate", "ffn1_w_up", "ffn1_w_down", "mix_norm", "w_in", "forget_bias", "q_norm",
         "k_norm", "w_branch_fox", "w_branch_sb", "w_out", "ffn2_norm", "ffn2_w_gate", "ffn2_w_up",
         "ffn2_w_down", "ple_norm", "w_ple_gate", "w_ple_proj"]
SMALL_ROWS = {"ffn1_norm": 0, "mix_norm": 1, "ffn2_norm": 2, "ple_norm": 3}
SMALL_COLS = {"forget_bias": (0, N_HEADS), "q_norm": (N_HEADS, HEAD_DIM), "k_norm": (N_HEADS + HEAD_DIM, HEAD_DIM)}
LOSS_ROW = 5


def _whole(name, gathered):
    if name in COLUMN_SHARDED:
        return jnp.concatenate([gathered[s] for s in range(N_CHIPS)], axis=1)
    return gathered.reshape(-1, gathered.shape[-1])


def _as_shards(name, whole):
    if name in COLUMN_SHARDED:
        k, n = whole.shape
        return whole.reshape(k, N_CHIPS, n // N_CHIPS).transpose(1, 0, 2)
    return whole.reshape(N_CHIPS, whole.shape[0] // N_CHIPS, whole.shape[1])


def _pack_small(values, extra=None):
    rows = [values[k] for k in ("ffn1_norm", "mix_norm", "ffn2_norm", "ple_norm")]
    tail = jnp.concatenate([values["forget_bias"], values["q_norm"], values["k_norm"]], axis=1)
    rows.append(jnp.pad(tail, ((0, 0), (0, D_MODEL - tail.shape[1]))))
    packed = jnp.concatenate(rows + [jnp.zeros((3, D_MODEL), F32)], axis=0)
    if extra is not None:
        packed = packed.at[LOSS_ROW, 0].set(extra)
    return packed


def _unpack_small(packed):
    out = {k: packed[r:r + 1] for k, r in SMALL_ROWS.items()}
    for k, (start, size) in SMALL_COLS.items():
        out[k] = packed[4:5, start:start + size]
    return out


def kernel(x, p, ffn1_norm, ffn1_w_gate, ffn1_w_up, ffn1_w_down, mix_norm, w_in, forget_bias, q_norm, k_norm, w_branch_fox, w_branch_sb, w_out, ffn2_norm, ffn2_w_gate, ffn2_w_up, ffn2_w_down, ple_norm, w_ple_gate, w_ple_proj, loss_target, m_ffn1_norm, m_ffn1_w_gate, m_ffn1_w_up, m_ffn1_w_down, m_mix_norm, m_w_in, m_forget_bias, m_q_norm, m_k_norm, m_w_branch_fox, m_w_branch_sb, m_w_out, m_ffn2_norm, m_ffn2_w_gate, m_ffn2_w_up, m_ffn2_w_down, m_ple_norm, m_w_ple_gate, m_w_ple_proj, v_ffn1_norm, v_ffn1_w_gate, v_ffn1_w_up, v_ffn1_w_down, v_mix_norm, v_w_in, v_forget_bias, v_q_norm, v_k_norm, v_w_branch_fox, v_w_branch_sb, v_w_out, v_ffn2_norm, v_ffn2_w_gate, v_ffn2_w_up, v_ffn2_w_down, v_ple_norm, v_w_ple_gate, v_w_ple_proj):
    args = dict(locals())
    weights = {k: args[k] for k in ORDER}
    moments_m = {k: args["m_" + k] for k in ORDER}
    moments_v = {k: args["v_" + k] for k in ORDER}

    c_idx = lax.axis_index("c").astype(jnp.int32).reshape(1)
    q_idx = (2 * lax.axis_index("x") + lax.axis_index("y")).astype(jnp.int32).reshape(1)
    gathered = _allgather_weights([_place_own_shard(weights[k][0], q_idx) for k in BIG])
    full = {}
    for k, gth in zip(BIG, gathered):
        full[k] = gth if k in KEPT_AS_SHARDS else _whole(k, gth)
    small = {k: weights[k] for k in SMALL}

    loss_sum, grad_x, gw, gs = _local_grads(x[0], p[0, 0], loss_target[0], small, full)

    slots = [gw[k] if k in KEPT_AS_SHARDS else _as_shards(k, gw[k]) for k in BIG]
    from_core = _exchange_pair_halves(slots)
    pairs = [_add_pair(g, got, c_idx) for g, got in zip(slots, from_core)]
    parts = _scatter_to_owner_chips(pairs)
    grads_big = dict(zip(BIG, _join_halves([_add_chips(t, c_idx) for t in parts])))
    reduced = _allreduce_small(_pack_small(gs, extra=loss_sum[0, 0]))
    grads_small = _unpack_small(reduced)
    loss = reduced[LOSS_ROW, 0]

    grads, deltas, new_m, new_v = {}, {}, {}, {}
    for k in BIG:
        grads[k] = grads_big[k][None]
        d, nm, nv = _adamw(weights[k][0], grads_big[k], moments_m[k][0], moments_v[k][0])
        deltas[k], new_m[k], new_v[k] = d[None], nm[None], nv[None]
    d_s, nm_s, nv_s = _adamw(_pack_small({k: weights[k] for k in SMALL}), reduced,
                             _pack_small({k: moments_m[k] for k in SMALL}),
                             _pack_small({k: moments_v[k] for k in SMALL}))
    for k in SMALL:
        grads[k] = grads_small[k]
    for name, packed in (("d", d_s), ("m", nm_s), ("v", nv_s)):
        target = {"d": deltas, "m": new_m, "v": new_v}[name]
        target.update(_unpack_small(packed))

    return (loss, grad_x[None], *[grads[k] for k in ORDER], *[deltas[k] for k in ORDER],
            *[new_m[k] for k in ORDER], *[new_v[k] for k in ORDER])
```

```python
import functools

import jax
import jax.numpy as jnp
from jax import lax
from jax.experimental import pallas as pl
from jax.experimental.pallas import tpu as pltpu

F32 = jnp.float32
BF16 = jnp.bfloat16

D_MODEL = 1024
D_FF = 2816
N_CHIPS = 4
FF_SHARD = D_FF // N_CHIPS
HEAD_DIM = 64
N_HEADS = 8
ATT_W = N_HEADS * HEAD_DIM
PAIR_W = 2 * HEAD_DIM
N_PAIRS = N_HEADS // 2
PLE_DIM = 256
IN_WIDTH = 3 * ATT_W + N_HEADS + 3 * ATT_W + 2 * D_MODEL
EPS = 1e-6
QK_SCALE = HEAD_DIM ** -0.5
LANES = 128
ATT_BLOCK = 256
ATT_Q_BLOCK = 512
NEG_BIG = -1e30
EXP_UNDERFLOW = 110.0

ADAM_LR = 0.001
ADAM_B1 = 0.9
ADAM_B2 = 0.999
ADAM_EPS = 1e-08
ADAM_WD = 0.01
ADAM_STEP = 10

MESH = pl.DeviceIdType.MESH
MIB = 1024 * 1024


def _cparams(vmem_mib=48):
    return pltpu.CompilerParams(vmem_limit_bytes=vmem_mib * MIB)


def _dot(a, b):
    return jnp.dot(a, b, preferred_element_type=F32)


def _dot_tn(a, b):
    return lax.dot_general(a, b, (((0,), (0,)), ((), ())), preferred_element_type=F32)


def _dot_nt(a, b):
    return lax.dot_general(a, b, (((1,), (1,)), ((), ())), preferred_element_type=F32)


def _sigmoid(x):
    return 1.0 / (1.0 + jnp.exp(-x))


def _split2(x):
    hi = x.astype(BF16)
    lo = (x - hi.astype(F32)).astype(BF16)
    return hi, lo


def _dot_split2(x, m):
    hi, lo = _split2(x)
    return _dot(hi, m) + _dot(lo, m)


def _split3(x):
    hi = x.astype(BF16)
    rest = x - hi.astype(F32)
    mid = rest.astype(BF16)
    lo = (rest - mid.astype(F32)).astype(BF16)
    return hi, mid, lo


def _rms(x):
    r = lax.rsqrt(jnp.mean(x * x, axis=-1, keepdims=True) + EPS)
    return x * r, r


def _rms_bwd(dh, xn, r, g):
    dxn = dh * g
    return r * (dxn - xn * jnp.mean(dxn * xn, axis=-1, keepdims=True))


def _colsum(x):
    return jnp.sum(x, axis=0, keepdims=True)


def _row_block(rows, row_bytes, budget):
    best = None
    for t in range(8, rows + 1, 8):
        if rows % t == 0 and t * row_bytes <= budget:
            best = t
    return best if best is not None else rows


def _ffn_fwd(x, g, wg, wu, wd, tm=512):
    s_len = x.shape[0]

    def body(x_ref, g_ref, wg_ref, wu_ref, wd_ref, o_ref, h_s, acc_s):
        j = pl.program_id(1)

        @pl.when(j == 0)
        def _():
            xn, _ = _rms(x_ref[...])
            h_s[...] = (xn * g_ref[...]).astype(BF16)
            acc_s[...] = jnp.zeros_like(acc_s)

        h = h_s[...]
        a = _dot(h, wg_ref[0])
        b = _dot(h, wu_ref[0])
        u = (a * _sigmoid(a) * b).astype(BF16)
        acc_s[...] += _dot(u, wd_ref[0])

        @pl.when(j == N_CHIPS - 1)
        def _():
            o_ref[...] = x_ref[...] + 0.5 * acc_s[...]

    return pl.pallas_call(
        body,
        grid=(s_len // tm, N_CHIPS),
        in_specs=[
            pl.BlockSpec((tm, D_MODEL), lambda i, j: (i, 0)),
            pl.BlockSpec((1, D_MODEL), lambda i, j: (0, 0)),
            pl.BlockSpec((1, D_MODEL, FF_SHARD), lambda i, j: (j, 0, 0)),
            pl.BlockSpec((1, D_MODEL, FF_SHARD), lambda i, j: (j, 0, 0)),
            pl.BlockSpec((1, FF_SHARD, D_MODEL), lambda i, j: (j, 0, 0)),
        ],
        out_specs=pl.BlockSpec((tm, D_MODEL), lambda i, j: (i, 0)),
        out_shape=jax.ShapeDtypeStruct((s_len, D_MODEL), F32),
        scratch_shapes=[pltpu.VMEM((tm, D_MODEL), BF16), pltpu.VMEM((tm, D_MODEL), F32)],
        compiler_params=_cparams(48),
        name="ffn_fwd",
    )(x, g, wg, wu, wd)


def _ffn_bwd(x, d, g, wg, wu, wd, tm=512):
    s_len = x.shape[0]
    nb = s_len // tm

    def body(x_ref, d_ref, g_ref, wg_ref, wu_ref, wd_ref,
             dx_ref, u_ref, da_ref, db_ref, h_ref, dbf_ref, dg_ref, h_s, dbf_s, dh_s):
        i = pl.program_id(0)
        j = pl.program_id(1)

        @pl.when(j == 0)
        def _():
            xn, _ = _rms(x_ref[...])
            h = (xn * g_ref[...]).astype(BF16)
            h_s[...] = h
            h_ref[...] = h
            dbf = d_ref[...].astype(BF16)
            dbf_s[...] = dbf
            dbf_ref[...] = dbf
            dh_s[...] = jnp.zeros_like(dh_s)

        @pl.when((i == 0) & (j == 0))
        def _():
            dg_ref[...] = jnp.zeros_like(dg_ref)

        h = h_s[...]
        a = _dot(h, wg_ref[0])
        b = _dot(h, wu_ref[0])
        du = 0.5 * _dot_nt(dbf_s[...], wd_ref[0])
        s = _sigmoid(a)
        silu = a * s
        da = (du * b * (s * (1.0 + a * (1.0 - s)))).astype(BF16)
        db = (du * silu).astype(BF16)
        u_ref[0] = (silu * b).astype(BF16)
        da_ref[0] = da
        db_ref[0] = db
        dh_s[...] += _dot_nt(da, wg_ref[0]) + _dot_nt(db, wu_ref[0])

        @pl.when(j == N_CHIPS - 1)
        def _():
            xn, r = _rms(x_ref[...])
            dh = dh_s[...]
            dx_ref[...] = d_ref[...] + _rms_bwd(dh, xn, r, g_ref[...])
            dg_ref[0:1, :] += _colsum(dh * xn)

    row = lambda i, j: (i, 0)
    shard = lambda i, j: (j, 0, 0)
    act = lambda i, j: (j, i, 0)
    return pl.pallas_call(
        body,
        grid=(nb, N_CHIPS),
        in_specs=[
            pl.BlockSpec((tm, D_MODEL), row),
            pl.BlockSpec((tm, D_MODEL), row),
            pl.BlockSpec((1, D_MODEL), lambda i, j: (0, 0)),
            pl.BlockSpec((1, D_MODEL, FF_SHARD), shard),
            pl.BlockSpec((1, D_MODEL, FF_SHARD), shard),
            pl.BlockSpec((1, FF_SHARD, D_MODEL), shard),
        ],
        out_specs=[
            pl.BlockSpec((tm, D_MODEL), row),
            pl.BlockSpec((1, tm, FF_SHARD), act),
            pl.BlockSpec((1, tm, FF_SHARD), act),
            pl.BlockSpec((1, tm, FF_SHARD), act),
            pl.BlockSpec((tm, D_MODEL), row),
            pl.BlockSpec((tm, D_MODEL), row),
            pl.BlockSpec((8, D_MODEL), lambda i, j: (0, 0)),
        ],
        out_shape=[
            jax.ShapeDtypeStruct((s_len, D_MODEL), F32),
            jax.ShapeDtypeStruct((N_CHIPS, s_len, FF_SHARD), BF16),
            jax.ShapeDtypeStruct((N_CHIPS, s_len, FF_SHARD), BF16),
            jax.ShapeDtypeStruct((N_CHIPS, s_len, FF_SHARD), BF16),
            jax.ShapeDtypeStruct((s_len, D_MODEL), BF16),
            jax.ShapeDtypeStruct((s_len, D_MODEL), BF16),
            jax.ShapeDtypeStruct((8, D_MODEL), F32),
        ],
        scratch_shapes=[
            pltpu.VMEM((tm, D_MODEL), BF16),
            pltpu.VMEM((tm, D_MODEL), BF16),
            pltpu.VMEM((tm, D_MODEL), F32),
        ],
        compiler_params=_cparams(56),
        name="ffn_bwd",
    )(x, d, g, wg, wu, wd)


def _wgrad(a, b, scale=1.0, name="wgrad"):
    na, s_len, k_dim = a.shape
    nb, _, n_dim = b.shape
    n = max(na, nb)
    ts = min(s_len, 1024)
    steps = s_len // ts

    def body(a_ref, b_ref, o_ref, acc_s):
        s = pl.program_id(1)

        @pl.when(s == 0)
        def _():
            acc_s[...] = jnp.zeros_like(acc_s)

        acc_s[...] += _dot_tn(a_ref[0].astype(BF16), b_ref[0].astype(BF16))

        @pl.when(s == steps - 1)
        def _():
            o_ref[0] = (acc_s[...] * scale).astype(BF16)

    a_map = (lambda m, s: (m, s, 0)) if na > 1 else (lambda m, s: (0, s, 0))
    b_map = (lambda m, s: (m, s, 0)) if nb > 1 else (lambda m, s: (0, s, 0))
    return pl.pallas_call(
        body,
        grid=(n, steps),
        in_specs=[pl.BlockSpec((1, ts, k_dim), a_map), pl.BlockSpec((1, ts, n_dim), b_map)],
        out_specs=pl.BlockSpec((1, k_dim, n_dim), lambda m, s: (m, 0, 0)),
        out_shape=jax.ShapeDtypeStruct((n, k_dim, n_dim), BF16),
        scratch_shapes=[pltpu.VMEM((k_dim, n_dim), F32)],
        compiler_params=_cparams(56),
        name=name,
    )(a, b)


def _head_sum_matrices():
    lane = lax.broadcasted_iota(jnp.int32, (ATT_W, LANES), 0) // HEAD_DIM
    col = lax.broadcasted_iota(jnp.int32, (ATT_W, LANES), 1)
    bd = (lane == col).astype(BF16)
    return bd, bd.T


def _head_mean(t, bd, bd_t):
    per_head = _dot_split2(t, bd) * (1.0 / HEAD_DIM)
    return _dot_split2(per_head, bd_t)


def _head_rms(x, bd, bd_t):
    per_head = _dot_split2(x * x, bd) * (1.0 / HEAD_DIM)
    r = lax.rsqrt(per_head + EPS)
    rw = _dot_split2(r, bd_t)
    return x * rw, rw


def _log_sigmoid(z):
    return jnp.minimum(z, 0.0) - jnp.log(1.0 + jnp.exp(-jnp.abs(z)))


def _inproj_fwd(x1, g, w_fox, w_fl, w_sb, w_gates, bias, qn, kn, bd, bd_t, tm=256):
    s_len = x1.shape[0]

    def body(x_ref, g_ref, wf_ref, wl_ref, ws_ref, wg_ref, bias_ref, qn_ref, kn_ref, bd_ref, bdt_ref,
             fq_ref, fk_ref, qs_ref, kf_ref, vf_ref, logf_ref, sq_ref, sk_ref, sv_ref, gates_ref):
        xn, _ = _rms(x_ref[...])
        h = (xn * g_ref[...]).astype(BF16)
        zf = _dot(h, wf_ref[...])
        fq = zf[:, 0:ATT_W]
        fk = zf[:, ATT_W:2 * ATT_W]
        fq_ref[...] = fq
        fk_ref[...] = fk
        bd_m = bd_ref[...]
        bdt_m = bdt_ref[...]
        fqn, _ = _head_rms(fq, bd_m, bdt_m)
        fkn, _ = _head_rms(fk, bd_m, bdt_m)
        qs_ref[...] = (fqn * qn_ref[...]).astype(BF16) * QK_SCALE
        kf_ref[...] = (fkn * kn_ref[...]).astype(BF16)
        vf_ref[...] = zf[:, 2 * ATT_W:3 * ATT_W].astype(BF16)
        logf_ref[...] = _log_sigmoid(_dot(h, wl_ref[...]) + bias_ref[...])
        zs = _dot(h, ws_ref[...])
        sq_ref[...] = zs[:, 0:ATT_W].astype(BF16) * QK_SCALE
        sk_ref[...] = zs[:, ATT_W:2 * ATT_W].astype(BF16)
        sv_ref[...] = zs[:, 2 * ATT_W:3 * ATT_W].astype(BF16)
        gates_ref[...] = _dot(h, wg_ref[...])

    row = lambda i: (i, 0)
    full = lambda i: (0, 0)
    att = lambda dt: jax.ShapeDtypeStruct((s_len, ATT_W), dt)
    return pl.pallas_call(
        body,
        grid=(s_len // tm,),
        in_specs=[
            pl.BlockSpec((tm, D_MODEL), row),
            pl.BlockSpec((1, D_MODEL), full),
            pl.BlockSpec((D_MODEL, 3 * ATT_W), full),
            pl.BlockSpec((D_MODEL, LANES), full),
            pl.BlockSpec((D_MODEL, 3 * ATT_W), full),
            pl.BlockSpec((D_MODEL, 2 * D_MODEL), full),
            pl.BlockSpec((1, LANES), full),
            pl.BlockSpec((1, ATT_W), full),
            pl.BlockSpec((1, ATT_W), full),
            pl.BlockSpec((ATT_W, LANES), full),
            pl.BlockSpec((LANES, ATT_W), full),
        ],
        out_specs=[
            pl.BlockSpec((tm, ATT_W), row), pl.BlockSpec((tm, ATT_W), row),
            pl.BlockSpec((tm, ATT_W), row), pl.BlockSpec((tm, ATT_W), row), pl.BlockSpec((tm, ATT_W), row),
            pl.BlockSpec((tm, LANES), row),
            pl.BlockSpec((tm, ATT_W), row), pl.BlockSpec((tm, ATT_W), row), pl.BlockSpec((tm, ATT_W), row),
            pl.BlockSpec((tm, 2 * D_MODEL), row),
        ],
        out_shape=[
            att(F32), att(F32), att(BF16), att(BF16), att(BF16),
            jax.ShapeDtypeStruct((s_len, LANES), F32),
            att(BF16), att(BF16), att(BF16),
            jax.ShapeDtypeStruct((s_len, 2 * D_MODEL), F32),
        ],
        compiler_params=_cparams(56),
        name="inproj_fwd",
    )(x1, g, w_fox, w_fl, w_sb, w_gates, bias, qn, kn, bd, bd_t)


def _tri(n, kind):
    r = lax.broadcasted_iota(jnp.int32, (n, n), 0)
    c = lax.broadcasted_iota(jnp.int32, (n, n), 1)
    m = {"row_ge_col": r >= c, "row_le_col": r <= c, "row_gt_col": r > c, "row_lt_col": r < c}[kind]
    return m.astype(BF16)


def _cumsum_rows(x, reverse, tm=256):
    s_len = x.shape[0]
    nb = s_len // tm
    tri = _tri(tm, "row_le_col" if reverse else "row_ge_col")
    edge = 0 if reverse else tm - 1

    def body(x_ref, tri_ref, o_ref, carry_s):
        @pl.when(pl.program_id(0) == 0)
        def _():
            carry_s[...] = jnp.zeros_like(carry_s)

        hi, mid, lo = _split3(x_ref[...])
        t = tri_ref[...]
        y = _dot(t, hi) + _dot(t, mid) + _dot(t, lo) + carry_s[...]
        o_ref[...] = y
        carry_s[...] = y[edge:edge + 1, :]

    order = (lambda i: (nb - 1 - i, 0)) if reverse else (lambda i: (i, 0))
    return pl.pallas_call(
        body,
        grid=(nb,),
        in_specs=[pl.BlockSpec((tm, LANES), order), pl.BlockSpec((tm, tm), lambda i: (0, 0))],
        out_specs=pl.BlockSpec((tm, LANES), order),
        out_shape=jax.ShapeDtypeStruct((s_len, LANES), F32),
        scratch_shapes=[pltpu.VMEM((1, LANES), F32)],
        name="cumsum_rev" if reverse else "cumsum_fwd",
    )(x, tri)


def _unblocked_t(t4):
    _, nb, _, blk = t4.shape
    return t4.transpose(1, 3, 0, 2).reshape(nb * blk, ATT_W)


def _blocked_rows(t, blk):
    return t.reshape(t.shape[0] // blk, blk, t.shape[1])


def _pair_rows_t(f8, blk):
    nb = f8.shape[0] // blk
    t = f8.reshape(nb, blk, N_PAIRS, 2).transpose(2, 0, 3, 1)
    return jnp.pad(t, ((0, 0), (0, 0), (0, 6), (0, 0)))


def _unpair_rows_t(t4):
    _, nb, _, blk = t4.shape
    return t4[:, :, 0:2, :].transpose(1, 3, 0, 2).reshape(nb * blk, N_HEADS)


def _head_masks(tq):
    lane = lax.broadcasted_iota(jnp.int32, (tq, PAIR_W), 1)
    return lane < HEAD_DIM


def _causal_mask(tq, tk, offset, strict):
    d = lax.broadcasted_iota(jnp.int32, (tq, tk), 1) - lax.broadcasted_iota(jnp.int32, (tq, tk), 0)
    return (d < offset) if strict else (d <= offset)


def _heads_of(ref, first):
    t = ref[...]
    zero = jnp.zeros_like(t)
    return [jnp.where(first, t, zero), jnp.where(first, zero, t)]


def _head_cols(ref):
    t = ref[...]
    return [t[:, 0:1], t[:, HEAD_DIM:HEAD_DIM + 1]]


def _att_specs(s_len):
    tq, tk = ATT_Q_BLOCK, ATT_BLOCK
    nq, nk = s_len // tq, s_len // tk
    return dict(
        nq=nq,
        q=pl.BlockSpec((tq, PAIR_W), lambda p, i: (i, p)),
        k_t=pl.BlockSpec((1, nk, PAIR_W, tk), lambda p, i: (p, 0, 0, 0)),
        k_rows=pl.BlockSpec((nk, tk, PAIR_W), lambda p, i: (0, 0, p)),
        f_t=pl.BlockSpec((1, nk, 8, tk), lambda p, i: (p, 0, 0, 0)),
        first=pl.BlockSpec((1, 1, 8, LANES), lambda p, i: (p, i, 0, 0)),
        wide=jax.ShapeDtypeStruct((s_len, ATT_W), F32),
        k_t_out=jax.ShapeDtypeStruct((N_PAIRS, nk, PAIR_W, tk), F32),
        f_t_out=jax.ShapeDtypeStruct((N_PAIRS, nk, 8, tk), F32),
        first_out=jax.ShapeDtypeStruct((N_PAIRS, nq, 8, LANES), F32),
        acc=pltpu.VMEM((2, tq, PAIR_W), F32),
    )


def _first_block(first_ref, limit):
    return jnp.clip(jnp.max(first_ref[0, 0]).astype(jnp.int32), 0, limit)


def _key_norm_bound(k):
    sq = jnp.sum(jnp.square(k.astype(F32)).reshape(k.shape[0], N_HEADS, HEAD_DIM), axis=-1)
    bound = jnp.sqrt(jnp.max(sq, axis=0)).reshape(N_PAIRS, 2)
    return jnp.broadcast_to(jnp.pad(bound, ((0, 0), (0, 6)))[:, :, None], (N_PAIRS, 8, LANES))


def _fox_fwd(qs, k3, v3, fw, ft4, kmax):
    sp = _att_specs(qs.shape[0])
    tq, tk = ATT_Q_BLOCK, ATT_BLOCK
    ratio = tq // tk

    def body(q_ref, k_ref, v_ref, fw_ref, ft_ref, kmax_ref, y_ref, lse_ref, first_ref, acc_ref, max_ref, sum_ref):
        i = pl.program_id(1)
        first = _head_masks(tq)
        qh = _heads_of(q_ref, first)
        fqh = _head_cols(fw_ref)
        acc_ref[...] = jnp.zeros_like(acc_ref)
        sum_ref[...] = jnp.zeros_like(sum_ref)
        max_ref[...] = jnp.full(max_ref.shape, NEG_BIG, F32)
        reach = []
        for n in range(2):
            qf = qh[n].astype(F32)
            reach.append(jnp.sqrt(jnp.sum(qf * qf, axis=-1, keepdims=True)) * kmax_ref[0, n:n + 1, 0:1] + fqh[n])

        def logits(j, shift, diag):
            k, fk = k_ref[j], ft_ref[0, j]
            raw = [_dot_nt(qh[n], k) for n in range(2)]
            out = []
            for n in range(2):
                s = raw[n] + (shift[n] - fk[n:n + 1, :])
                if diag:
                    s = jnp.where(_causal_mask(tq, tk, i * tq - j * tk, strict=False), s, NEG_BIG)
                out.append(s)
            return out

        def max_pass(j, diag):
            ss = logits(j, fqh, diag)
            for n in range(2):
                max_ref[n] = jnp.maximum(max_ref[n], ss[n])

        def sum_pass(j, shift, diag):
            ps = [jnp.exp(s) for s in logits(j, shift, diag)]
            v = v_ref[j]
            for n in range(2):
                sum_ref[n] += ps[n]
            for n in range(2):
                acc_ref[n] += _dot(ps[n].astype(BF16), v)

        for d in range(ratio):
            max_pass(ratio * i + d, True)

        def block_matters(j):
            gap = []
            for n in range(2):
                m_run = jnp.max(max_ref[n], axis=-1, keepdims=True)
                f_end = ft_ref[0, jnp.maximum(j, 0)][n:n + 1, tk - 1:tk]
                gap.append(jnp.max(reach[n] - m_run) - jnp.max(f_end))
            return (j >= 0) & (jnp.maximum(gap[0], gap[1]) > -EXP_UNDERFLOW)

        def walk_left(j):
            max_pass(j, False)
            return j - 1

        j_first = lax.while_loop(block_matters, walk_left, ratio * i - 1) + 1
        m = [jnp.max(max_ref[n], axis=-1, keepdims=True) for n in range(2)]
        shift = [fqh[n] - m[n] for n in range(2)]

        def one(j, c):
            sum_pass(j, shift, False)
            return c
        lax.fori_loop(j_first, ratio * i, one, 0)
        for d in range(ratio):
            sum_pass(ratio * i + d, shift, True)
        l = [jnp.sum(sum_ref[n], axis=-1, keepdims=True) for n in range(2)]
        y_ref[...] = jnp.where(first, acc_ref[0] / l[0], acc_ref[1] / l[1])
        lse_ref[...] = jnp.where(first, m[0] + jnp.log(l[0]), m[1] + jnp.log(l[1]))
        first_ref[...] = jnp.ones(first_ref.shape, F32) * j_first.astype(F32)

    tile = pltpu.VMEM((2, tq, tk), F32)
    return pl.pallas_call(
        body,
        grid=(N_PAIRS, sp["nq"]),
        in_specs=[sp["q"], sp["k_rows"], sp["k_rows"], sp["q"], sp["f_t"],
                  pl.BlockSpec((1, 8, LANES), lambda p, i: (p, 0, 0))],
        out_specs=[sp["q"], sp["q"], sp["first"]],
        out_shape=[sp["wide"], sp["wide"], sp["first_out"]],
        scratch_shapes=[sp["acc"], tile, tile],
        compiler_params=_cparams(56),
        name="fox_fwd",
    )(qs, k3, v3, fw, ft4, kmax)


def _fox_bwd(qs, k3, v3, dy, y, lse, fw, ft4, first_block):
    sp = _att_specs(qs.shape[0])
    tq, tk = ATT_Q_BLOCK, ATT_BLOCK
    ratio = tq // tk

    def body(q_ref, k_ref, v_ref, dy_ref, y_ref, lse_ref, fw_ref, ft_ref, first_ref,
             dq_ref, dfq_ref, dkt_ref, dvt_ref, dft_ref, acc_ref):
        i = pl.program_id(1)

        @pl.when(i == 0)
        def _():
            dkt_ref[...] = jnp.zeros_like(dkt_ref)
            dvt_ref[...] = jnp.zeros_like(dvt_ref)
            dft_ref[...] = jnp.zeros_like(dft_ref)

        first = _head_masks(tq)
        qh = _heads_of(q_ref, first)
        dyv = dy_ref[...]
        dyb = dyv.astype(BF16)
        zero = jnp.zeros_like(dyb)
        dyh = [jnp.where(first, dyb, zero), jnp.where(first, zero, dyb)]
        prod = dyv * y_ref[...]
        zf = jnp.zeros_like(prod)
        delta = [jnp.sum(jnp.where(first, prod, zf), axis=-1, keepdims=True),
                 jnp.sum(jnp.where(first, zf, prod), axis=-1, keepdims=True)]
        fqh = _head_cols(fw_ref)
        lseh = _head_cols(lse_ref)
        shift = [fqh[n] - lseh[n] for n in range(2)]
        acc_ref[...] = jnp.zeros_like(acc_ref)

        def block(j, rows, diag):
            mask = _causal_mask(tq, tk, i * tq - j * tk, strict=False) if diag else None
            k, v, fk = k_ref[j], v_ref[j], ft_ref[0, j]
            logits = [_dot_nt(qh[n], k) for n in range(2)]
            dps = [_dot_nt(dyh[n], v) for n in range(2)]
            pbs, dsbs, out = [], [], []
            for n in range(2):
                p = jnp.exp(logits[n] + (shift[n] - fk[n:n + 1, :]))
                if diag:
                    p = jnp.where(mask, p, 0.0)
                ds = p * (dps[n] - delta[n])
                pbs.append(p.astype(BF16))
                dsbs.append(ds.astype(BF16))
                out.append(rows[n] + jnp.sum(ds, axis=-1, keepdims=True))
                dft_ref[0, j, n:n + 1, :] -= _colsum(ds)
            for n in range(2):
                acc_ref[n] += _dot(dsbs[n], k)
            dkt_ref[0, j] += _dot_tn(qh[0], dsbs[0]) + _dot_tn(qh[1], dsbs[1])
            dvt_ref[0, j] += _dot_tn(dyh[0], pbs[0]) + _dot_tn(dyh[1], pbs[1])
            return tuple(out)

        rows = (jnp.zeros((tq, 1), F32),) * 2
        rows = lax.fori_loop(_first_block(first_ref, ratio * i), ratio * i, lambda j, c: block(j, c, False), rows)
        for d in range(ratio):
            rows = block(ratio * i + d, rows, True)
        dq_ref[...] = jnp.where(first, acc_ref[0], acc_ref[1])
        dfq_ref[...] = jnp.where(first, rows[0], rows[1])

    return pl.pallas_call(
        body,
        grid=(N_PAIRS, sp["nq"]),
        in_specs=[sp["q"], sp["k_rows"], sp["k_rows"], sp["q"], sp["q"], sp["q"], sp["q"], sp["f_t"], sp["first"]],
        out_specs=[sp["q"], sp["q"], sp["k_t"], sp["k_t"], sp["f_t"]],
        out_shape=[sp["wide"], sp["wide"], sp["k_t_out"], sp["k_t_out"], sp["f_t_out"]],
        scratch_shapes=[sp["acc"]],
        compiler_params=_cparams(56),
        name="fox_bwd",
    )(qs, k3, v3, dy, y, lse, fw, ft4, first_block)


SIGN_BIT = 0x80000000


def _sb_terms(z, mask, diag):
    neg_abs = pltpu.bitcast(pltpu.bitcast(z, jnp.uint32) | jnp.uint32(SIGN_BIT), F32)
    lb = jnp.minimum(z, 0.0) - jnp.log(1.0 + jnp.exp(neg_abs))
    l1m = lb - z
    if diag:
        l1m = jnp.where(mask, l1m, 0.0)
    return lb, l1m


def _dot_split2_stacked(x, m2):
    hi, lo = _split2(x)
    return _dot(jnp.concatenate([hi, lo], axis=1), m2)


def _tri_stacked(kind):
    t = _tri(ATT_BLOCK, kind)
    return jnp.concatenate([t, t], axis=0)


def _sb_fwd(qs, k3, v3):
    sp = _att_specs(qs.shape[0])
    tq, tk = ATT_Q_BLOCK, ATT_BLOCK
    ratio = tq // tk
    upper = _tri_stacked("row_gt_col")

    def body(q_ref, k_ref, v_ref, u_ref, y_ref, rtot_ref, first_ref, acc_ref):
        i = pl.program_id(1)
        first = _head_masks(tq)
        qh = _heads_of(q_ref, first)
        u = u_ref[...]
        acc_ref[...] = jnp.zeros_like(acc_ref)

        def block(j, rs, diag):
            mask = _causal_mask(tq, tk, i * tq - j * tk, strict=True) if diag else None
            k, v = k_ref[j], v_ref[j]
            logits = [_dot_nt(qh[n], k) for n in range(2)]
            terms = [_sb_terms(z, mask, diag) for z in logits]
            right = [_dot_split2_stacked(l1m, u) for _, l1m in terms]
            weights = []
            for n in range(2):
                a = jnp.exp(terms[n][0] + right[n] + rs[n])
                if diag:
                    a = jnp.where(mask, a, 0.0)
                weights.append(a.astype(BF16))
            for n in range(2):
                acc_ref[n] += _dot(weights[n], v)
            return tuple(rs[n] + jnp.sum(terms[n][1], axis=-1, keepdims=True) for n in range(2))

        rs = (jnp.zeros((tq, 1), F32),) * 2
        for d in range(ratio):
            rs = block(ratio * i + (ratio - 1 - d), rs, True)

        def block_matters(c):
            j, r0, r1 = c
            return (j >= 0) & (jnp.max(jnp.maximum(r0, r1)) > -EXP_UNDERFLOW)

        def walk_left(c):
            j, r0, r1 = c
            r0, r1 = block(j, (r0, r1), False)
            return j - 1, r0, r1

        j, r0, r1 = lax.while_loop(block_matters, walk_left, (ratio * i - 1, rs[0], rs[1]))
        y_ref[...] = jnp.where(first, acc_ref[0], acc_ref[1])
        rtot_ref[...] = jnp.where(first, r0, r1)
        first_ref[...] = jnp.ones(first_ref.shape, F32) * (j + 1).astype(F32)

    return pl.pallas_call(
        body,
        grid=(N_PAIRS, sp["nq"]),
        in_specs=[sp["q"], sp["k_rows"], sp["k_rows"], pl.BlockSpec((2 * tk, tk), lambda p, i: (0, 0))],
        out_specs=[sp["q"], sp["q"], sp["first"]],
        out_shape=[sp["wide"], sp["wide"], sp["first_out"]],
        scratch_shapes=[sp["acc"]],
        compiler_params=_cparams(56),
        name="sb_fwd",
    )(qs, k3, v3, upper)


def _sb_bwd(qs, k3, v3, dy, rtot, first_block):
    sp = _att_specs(qs.shape[0])
    tq, tk = ATT_Q_BLOCK, ATT_BLOCK
    ratio = tq // tk
    lower_in = _tri_stacked("row_le_col")
    lower = _tri(tk, "row_lt_col")

    def body(q_ref, k_ref, v_ref, dy_ref, rtot_ref, first_ref, li_ref, l_ref, dq_ref, dkt_ref, dvt_ref, acc_ref):
        i = pl.program_id(1)

        @pl.when(i == 0)
        def _():
            dkt_ref[...] = jnp.zeros_like(dkt_ref)
            dvt_ref[...] = jnp.zeros_like(dvt_ref)

        first = _head_masks(tq)
        qh = _heads_of(q_ref, first)
        dyb = dy_ref[...].astype(BF16)
        zero = jnp.zeros_like(dyb)
        dyh = [jnp.where(first, dyb, zero), jnp.where(first, zero, dyb)]
        rtoth = _head_cols(rtot_ref)
        li = li_ref[...]
        lo_tri = l_ref[...]
        acc_ref[...] = jnp.zeros_like(acc_ref)

        def block(j, carry, diag):
            mask = _causal_mask(tq, tk, i * tq - j * tk, strict=True) if diag else None
            k, v = k_ref[j], v_ref[j]
            logits = [_dot_nt(qh[n], k) for n in range(2)]
            das = [_dot_nt(dyh[n], v) for n in range(2)]
            terms = [_sb_terms(z, mask, diag) for z in logits]
            upto = [_dot_split2_stacked(l1m, li) for _, l1m in terms]
            des, weights = [], []
            for n in range(2):
                a = jnp.exp(terms[n][0] + ((rtoth[n] - carry[2 * n]) - upto[n]))
                if diag:
                    a = jnp.where(mask, a, 0.0)
                des.append(a * das[n])
                weights.append(a.astype(BF16))
            lefts = [_dot(de.astype(BF16), lo_tri) for de in des]
            dzbs, out = [], []
            for n in range(2):
                beta = jnp.exp(terms[n][0])
                dz = des[n] - (des[n] + (carry[2 * n + 1] + lefts[n])) * beta
                if diag:
                    dz = jnp.where(mask, dz, 0.0)
                dzbs.append(dz.astype(BF16))
                out += [carry[2 * n] + jnp.sum(terms[n][1], axis=-1, keepdims=True),
                        carry[2 * n + 1] + jnp.sum(des[n], axis=-1, keepdims=True)]
            for n in range(2):
                acc_ref[n] += _dot(dzbs[n], k)
            dkt_ref[0, j] += _dot_tn(qh[0], dzbs[0]) + _dot_tn(qh[1], dzbs[1])
            dvt_ref[0, j] += _dot_tn(dyh[0], weights[0]) + _dot_tn(dyh[1], weights[1])
            return tuple(out)

        carry = (jnp.zeros((tq, 1), F32),) * 4
        carry = lax.fori_loop(_first_block(first_ref, ratio * i), ratio * i, lambda j, c: block(j, c, False), carry)
        for d in range(ratio):
            carry = block(ratio * i + d, carry, True)
        dq_ref[...] = jnp.where(first, acc_ref[0], acc_ref[1])

    return pl.pallas_call(
        body,
        grid=(N_PAIRS, sp["nq"]),
        in_specs=[sp["q"], sp["k_rows"], sp["k_rows"], sp["q"], sp["q"], sp["first"],
                  pl.BlockSpec((2 * tk, tk), lambda p, i: (0, 0)), pl.BlockSpec((tk, tk), lambda p, i: (0, 0))],
        out_specs=[sp["q"], sp["k_t"], sp["k_t"]],
        out_shape=[sp["wide"], sp["k_t_out"], sp["k_t_out"]],
        scratch_shapes=[sp["acc"]],
        compiler_params=_cparams(56),
        name="sb_bwd",
    )(qs, k3, v3, dy, rtot, first_block, lower_in, lower)


def _merge_fwd(x1, gates, y_fox, y_sb, w_bf, w_bs, w_out, tm=512):
    s_len = x1.shape[0]

    def body(x_ref, g_ref, yf_ref, ys_ref, wbf_ref, wbs_ref, wo_ref, o_ref):
        g = g_ref[...]
        of = _dot(yf_ref[...].astype(BF16), wbf_ref[...])
        os_ = _dot(ys_ref[...].astype(BF16), wbs_ref[...])
        merged = _sigmoid(g[:, 0:D_MODEL]) * of + _sigmoid(g[:, D_MODEL:]) * os_
        o_ref[...] = x_ref[...] + _dot(merged.astype(BF16), wo_ref[...])

    row = lambda i: (i, 0)
    full = lambda i: (0, 0)
    return pl.pallas_call(
        body,
        grid=(s_len // tm,),
        in_specs=[
            pl.BlockSpec((tm, D_MODEL), row),
            pl.BlockSpec((tm, 2 * D_MODEL), row),
            pl.BlockSpec((tm, ATT_W), row),
            pl.BlockSpec((tm, ATT_W), row),
            pl.BlockSpec((ATT_W, D_MODEL), full),
            pl.BlockSpec((ATT_W, D_MODEL), full),
            pl.BlockSpec((D_MODEL, D_MODEL), full),
        ],
        out_specs=pl.BlockSpec((tm, D_MODEL), row),
        out_shape=jax.ShapeDtypeStruct((s_len, D_MODEL), F32),
        compiler_params=_cparams(48),
        name="merge_fwd",
    )(x1, gates, y_fox, y_sb, w_bf, w_bs, w_out)


def _merge_bwd(dx2, gates, y_fox, y_sb, w_bf, w_bs, w_out, tm=512):
    s_len = dx2.shape[0]

    def body(d_ref, g_ref, yf_ref, ys_ref, wbf_ref, wbs_ref, wo_ref,
             dyf_ref, dys_ref, dg_ref, dof_ref, dos_ref, m_ref, dbf_ref):
        dbf = d_ref[...].astype(BF16)
        dbf_ref[...] = dbf
        dm = _dot_nt(dbf, wo_ref[...])
        g = g_ref[...]
        of = _dot(yf_ref[...].astype(BF16), wbf_ref[...])
        os_ = _dot(ys_ref[...].astype(BF16), wbs_ref[...])
        sf = _sigmoid(g[:, 0:D_MODEL])
        ss = _sigmoid(g[:, D_MODEL:])
        m_ref[...] = (sf * of + ss * os_).astype(BF16)
        d_of = (dm * sf).astype(BF16)
        d_os = (dm * ss).astype(BF16)
        dof_ref[...] = d_of
        dos_ref[...] = d_os
        dg_ref[:, 0:D_MODEL] = (dm * of * sf * (1.0 - sf)).astype(BF16)
        dg_ref[:, D_MODEL:] = (dm * os_ * ss * (1.0 - ss)).astype(BF16)
        dyf_ref[...] = _dot_nt(d_of, wbf_ref[...])
        dys_ref[...] = _dot_nt(d_os, wbs_ref[...])

    row = lambda i: (i, 0)
    full = lambda i: (0, 0)
    return pl.pallas_call(
        body,
        grid=(s_len // tm,),
        in_specs=[
            pl.BlockSpec((tm, D_MODEL), row),
            pl.BlockSpec((tm, 2 * D_MODEL), row),
            pl.BlockSpec((tm, ATT_W), row),
            pl.BlockSpec((tm, ATT_W), row),
            pl.BlockSpec((ATT_W, D_MODEL), full),
            pl.BlockSpec((ATT_W, D_MODEL), full),
            pl.BlockSpec((D_MODEL, D_MODEL), full),
        ],
        out_specs=[
            pl.BlockSpec((tm, ATT_W), row), pl.BlockSpec((tm, ATT_W), row),
            pl.BlockSpec((tm, 2 * D_MODEL), row),
            pl.BlockSpec((tm, D_MODEL), row), pl.BlockSpec((tm, D_MODEL), row),
            pl.BlockSpec((tm, D_MODEL), row), pl.BlockSpec((tm, D_MODEL), row),
        ],
        out_shape=[
            jax.ShapeDtypeStruct((s_len, ATT_W), F32), jax.ShapeDtypeStruct((s_len, ATT_W), F32),
            jax.ShapeDtypeStruct((s_len, 2 * D_MODEL), BF16),
            jax.ShapeDtypeStruct((s_len, D_MODEL), BF16), jax.ShapeDtypeStruct((s_len, D_MODEL), BF16),
            jax.ShapeDtypeStruct((s_len, D_MODEL), BF16), jax.ShapeDtypeStruct((s_len, D_MODEL), BF16),
        ],
        compiler_params=_cparams(56),
        name="merge_bwd",
    )(dx2, gates, y_fox, y_sb, w_bf, w_bs, w_out)


def _ple_loss(x3, p, g, w_pg, w_pp, target, tm=512):
    s_len = x3.shape[0]
    inv_d = 1.0 / D_MODEL

    def body(x_ref, p_ref, g_ref, wpg_ref, wpp_ref, t_ref,
             dx_ref, du_ref, dt_ref, hn_ref, dg_ref, loss_ref):
        @pl.when(pl.program_id(0) == 0)
        def _():
            dg_ref[...] = jnp.zeros_like(dg_ref)
            loss_ref[...] = jnp.zeros_like(loss_ref)

        x = x_ref[...]
        xn, r = _rms(x)
        gain = g_ref[...]
        hn = (xn * gain).astype(BF16)
        hn_ref[...] = hn
        sg = _sigmoid(_dot(hn, wpg_ref[...]))
        t = _dot(p_ref[...].astype(BF16), wpp_ref[...])
        err = x + sg * t - t_ref[...]
        sq = jnp.sum(_colsum(err * err), axis=-1, keepdims=True)
        loss_ref[...] += (0.5 * inv_d) * sq
        dy = err * inv_d
        du = (dy * t * sg * (1.0 - sg)).astype(BF16)
        du_ref[...] = du
        dt_ref[...] = (dy * sg).astype(BF16)
        dh = _dot_nt(du, wpg_ref[...])
        dx_ref[...] = dy + _rms_bwd(dh, xn, r, gain)
        dg_ref[0:1, :] += _colsum(dh * xn)

    row = lambda i: (i, 0)
    full = lambda i: (0, 0)
    bf = jax.ShapeDtypeStruct((s_len, D_MODEL), BF16)
    return pl.pallas_call(
        body,
        grid=(s_len // tm,),
        in_specs=[
            pl.BlockSpec((tm, D_MODEL), row),
            pl.BlockSpec((tm, PLE_DIM), row),
            pl.BlockSpec((1, D_MODEL), full),
            pl.BlockSpec((D_MODEL, D_MODEL), full),
            pl.BlockSpec((PLE_DIM, D_MODEL), full),
            pl.BlockSpec((tm, D_MODEL), row),
        ],
        out_specs=[
            pl.BlockSpec((tm, D_MODEL), row), pl.BlockSpec((tm, D_MODEL), row),
            pl.BlockSpec((tm, D_MODEL), row), pl.BlockSpec((tm, D_MODEL), row),
            pl.BlockSpec((8, D_MODEL), full), pl.BlockSpec((8, LANES), full),
        ],
        out_shape=[
            jax.ShapeDtypeStruct((s_len, D_MODEL), F32), bf, bf, bf,
            jax.ShapeDtypeStruct((8, D_MODEL), F32), jax.ShapeDtypeStruct((8, LANES), F32),
        ],
        compiler_params=_cparams(48),
        name="ple_loss",
    )(x3, p, g, w_pg, w_pp, target)


def _qknorm_bwd(fq, fk, dqs, dk, dv, qn, kn, bd, bd_t, tm=256):
    s_len = fq.shape[0]

    def body(fq_ref, fk_ref, dq_ref, dk_ref, dv_ref, qn_ref, kn_ref, bd_ref, bdt_ref,
             dz_ref, dqn_ref, dkn_ref):
        @pl.when(pl.program_id(0) == 0)
        def _():
            dqn_ref[...] = jnp.zeros_like(dqn_ref)
            dkn_ref[...] = jnp.zeros_like(dkn_ref)

        bd_m = bd_ref[...]
        bdt_m = bdt_ref[...]

        def one(x, dy, gain, dgain_ref):
            xn, rw = _head_rms(x, bd_m, bdt_m)
            dgain_ref[0:1, :] += _colsum(dy * xn)
            dxn = dy * gain
            return rw * (dxn - xn * _head_mean(dxn * xn, bd_m, bdt_m))

        dz_ref[:, 0:ATT_W] = one(fq_ref[...], dq_ref[...] * QK_SCALE, qn_ref[...], dqn_ref).astype(BF16)
        dz_ref[:, ATT_W:2 * ATT_W] = one(fk_ref[...], dk_ref[...], kn_ref[...], dkn_ref).astype(BF16)
        dz_ref[:, 2 * ATT_W:] = dv_ref[...].astype(BF16)

    row = lambda i: (i, 0)
    full = lambda i: (0, 0)
    att = pl.BlockSpec((tm, ATT_W), row)
    return pl.pallas_call(
        body,
        grid=(s_len // tm,),
        in_specs=[att, att, att, att, att,
                  pl.BlockSpec((1, ATT_W), full), pl.BlockSpec((1, ATT_W), full),
                  pl.BlockSpec((ATT_W, LANES), full), pl.BlockSpec((LANES, ATT_W), full)],
        out_specs=[pl.BlockSpec((tm, 3 * ATT_W), row), pl.BlockSpec((8, ATT_W), full), pl.BlockSpec((8, ATT_W), full)],
        out_shape=[jax.ShapeDtypeStruct((s_len, 3 * ATT_W), BF16),
                   jax.ShapeDtypeStruct((8, ATT_W), F32), jax.ShapeDtypeStruct((8, ATT_W), F32)],
        name="qknorm_bwd",
    )(fq, fk, dqs, dk, dv, qn, kn, bd, bd_t)


def _inproj_bwd(x1, dx2, g, dzf, dlogf, logf, dzs, dgates, w_fox, w_fl, w_sb, w_gates, tm=256):
    s_len = x1.shape[0]

    def body(x_ref, d_ref, g_ref, dzf_ref, dlf_ref, lf_ref, dzs_ref, dgt_ref, wf_ref, wl_ref, ws_ref, wg_ref,
             dx_ref, h_ref, dfl_ref, dg_ref, db_ref):
        @pl.when(pl.program_id(0) == 0)
        def _():
            dg_ref[...] = jnp.zeros_like(dg_ref)
            db_ref[...] = jnp.zeros_like(db_ref)

        xn, r = _rms(x_ref[...])
        gain = g_ref[...]
        h_ref[...] = (xn * gain).astype(BF16)
        lane = lax.broadcasted_iota(jnp.int32, (tm, LANES), 1)
        dfl = jnp.where(lane < N_HEADS, dlf_ref[...] * (1.0 - jnp.exp(lf_ref[...])), 0.0)
        db_ref[0:1, :] += _colsum(dfl)
        dflb = dfl.astype(BF16)
        dfl_ref[...] = dflb
        dh = (_dot_nt(dzf_ref[...], wf_ref[...]) + _dot_nt(dflb, wl_ref[...])
              + _dot_nt(dzs_ref[...], ws_ref[...]) + _dot_nt(dgt_ref[...], wg_ref[...]))
        dx_ref[...] = d_ref[...] + _rms_bwd(dh, xn, r, gain)
        dg_ref[0:1, :] += _colsum(dh * xn)

    row = lambda i: (i, 0)
    full = lambda i: (0, 0)
    return pl.pallas_call(
        body,
        grid=(s_len // tm,),
        in_specs=[
            pl.BlockSpec((tm, D_MODEL), row),
            pl.BlockSpec((tm, D_MODEL), row),
            pl.BlockSpec((1, D_MODEL), full),
            pl.BlockSpec((tm, 3 * ATT_W), row),
            pl.BlockSpec((tm, LANES), row),
            pl.BlockSpec((tm, LANES), row),
            pl.BlockSpec((tm, 3 * ATT_W), row),
            pl.BlockSpec((tm, 2 * D_MODEL), row),
            pl.BlockSpec((D_MODEL, 3 * ATT_W), full),
            pl.BlockSpec((D_MODEL, LANES), full),
            pl.BlockSpec((D_MODEL, 3 * ATT_W), full),
            pl.BlockSpec((D_MODEL, 2 * D_MODEL), full),
        ],
        out_specs=[
            pl.BlockSpec((tm, D_MODEL), row), pl.BlockSpec((tm, D_MODEL), row), pl.BlockSpec((tm, LANES), row),
            pl.BlockSpec((8, D_MODEL), full), pl.BlockSpec((8, LANES), full),
        ],
        out_shape=[
            jax.ShapeDtypeStruct((s_len, D_MODEL), F32), jax.ShapeDtypeStruct((s_len, D_MODEL), BF16),
            jax.ShapeDtypeStruct((s_len, LANES), BF16),
            jax.ShapeDtypeStruct((8, D_MODEL), F32), jax.ShapeDtypeStruct((8, LANES), F32),
        ],
        compiler_params=_cparams(56),
        name="inproj_bwd",
    )(x1, dx2, g, dzf, dlogf, logf, dzs, dgates, w_fox, w_fl, w_sb, w_gates)


def _split_w_in(w_in):
    o = 3 * ATT_W
    w_fox = w_in[:, 0:o]
    w_fl = jnp.pad(w_in[:, o:o + N_HEADS], ((0, 0), (0, LANES - N_HEADS)))
    w_sb = w_in[:, o + N_HEADS:2 * o + N_HEADS]
    w_gates = w_in[:, 2 * o + N_HEADS:]
    return w_fox, w_fl, w_sb, w_gates


def _local_grads(x, p, target, small, full):
    blk = ATT_BLOCK
    bd, bd_t = _head_sum_matrices()

    w_fox, w_fl, w_sb, w_gates = _split_w_in(full["w_in"])
    bias = jnp.pad(small["forget_bias"], ((0, 0), (0, LANES - N_HEADS)))
    qn = jnp.tile(small["q_norm"], (1, N_HEADS))
    kn = jnp.tile(small["k_norm"], (1, N_HEADS))

    x1 = _ffn_fwd(x, small["ffn1_norm"], full["ffn1_w_gate"], full["ffn1_w_up"], full["ffn1_w_down"])
    fq, fk, f_qs, f_k, f_v, logf, s_qs, s_k, s_v, gates = _inproj_fwd(
        x1, small["mix_norm"], w_fox, w_fl, w_sb, w_gates, bias, qn, kn, bd, bd_t)
    f_cum = _cumsum_rows(logf, reverse=False)
    f8 = f_cum[:, 0:N_HEADS]
    fw = jnp.repeat(f8, HEAD_DIM, axis=1)
    ft4 = _pair_rows_t(f8, blk)
    f_k3, f_v3 = _blocked_rows(f_k, blk), _blocked_rows(f_v, blk)
    y_fox, lse, f_first = _fox_fwd(f_qs, f_k3, f_v3, fw, ft4, _key_norm_bound(f_k))
    s_k3, s_v3 = _blocked_rows(s_k, blk), _blocked_rows(s_v, blk)
    y_sb, s_rtot, s_first = _sb_fwd(s_qs, s_k3, s_v3)
    x2 = _merge_fwd(x1, gates, y_fox, y_sb, full["w_branch_fox"], full["w_branch_sb"], full["w_out"])
    x3 = _ffn_fwd(x2, small["ffn2_norm"], full["ffn2_w_gate"], full["ffn2_w_up"], full["ffn2_w_down"])

    dx3, du_ple, dt_ple, hn_ple, dg_ple, loss_sum = _ple_loss(
        x3, p, small["ple_norm"], full["w_ple_gate"], full["w_ple_proj"], target)
    dx2, u2, da2, db2, h_ffn2, d3_bf, dg_ffn2 = _ffn_bwd(
        x2, dx3, small["ffn2_norm"], full["ffn2_w_gate"], full["ffn2_w_up"], full["ffn2_w_down"])
    dy_fox, dy_sb, dgates, d_of, d_os, merged, d2_bf = _merge_bwd(
        dx2, gates, y_fox, y_sb, full["w_branch_fox"], full["w_branch_sb"], full["w_out"])

    f_dqs, dfq_w, f_dkt4, f_dvt4, dft4 = _fox_bwd(f_qs, f_k3, f_v3, dy_fox, y_fox, lse, fw, ft4, f_first)
    s_dqs, s_dkt4, s_dvt4 = _sb_bwd(s_qs, s_k3, s_v3, dy_sb, s_rtot, s_first)

    dzf, dqn8, dkn8 = _qknorm_bwd(fq, fk, f_dqs, _unblocked_t(f_dkt4), _unblocked_t(f_dvt4), qn, kn, bd, bd_t)
    dzs = jnp.concatenate([s_dqs * QK_SCALE, _unblocked_t(s_dkt4), _unblocked_t(s_dvt4)], axis=1).astype(BF16)
    df8 = _unpair_rows_t(dft4) + dfq_w[:, ::HEAD_DIM]
    dlogf = _cumsum_rows(jnp.pad(df8, ((0, 0), (0, LANES - N_HEADS))), reverse=True)
    dx1, h_mix, dfl, dg_mix, dbias8 = _inproj_bwd(
        x1, dx2, small["mix_norm"], dzf, dlogf, logf, dzs, dgates, w_fox, w_fl, w_sb, w_gates)
    grad_x, u1, da1, db1, h_ffn1, d1_bf, dg_ffn1 = _ffn_bwd(
        x, dx1, small["ffn1_norm"], full["ffn1_w_gate"], full["ffn1_w_up"], full["ffn1_w_down"])

    one = lambda t: t[None]
    gw = {}
    gw["ffn1_w_gate"] = _wgrad(one(h_ffn1), da1, name="wgrad_ffn1_gate")
    gw["ffn1_w_up"] = _wgrad(one(h_ffn1), db1, name="wgrad_ffn1_up")
    gw["ffn1_w_down"] = _wgrad(u1, one(d1_bf), scale=0.5, name="wgrad_ffn1_down")
    gw["ffn2_w_gate"] = _wgrad(one(h_ffn2), da2, name="wgrad_ffn2_gate")
    gw["ffn2_w_up"] = _wgrad(one(h_ffn2), db2, name="wgrad_ffn2_up")
    gw["ffn2_w_down"] = _wgrad(u2, one(d3_bf), scale=0.5, name="wgrad_ffn2_down")
    g_fox = _wgrad(one(h_mix), one(dzf), name="wgrad_in_fox")[0]
    g_fl = _wgrad(one(h_mix), one(dfl), name="wgrad_in_forget")[0]
    g_sb = _wgrad(one(h_mix), one(dzs), name="wgrad_in_sb")[0]
    g_gt = _wgrad(one(h_mix), one(dgates), name="wgrad_in_gates")[0]
    gw["w_in"] = jnp.concatenate([g_fox, g_fl[:, 0:N_HEADS], g_sb, g_gt], axis=1)
    gw["w_branch_fox"] = _wgrad(one(y_fox), one(d_of), name="wgrad_branch_fox")[0]
    gw["w_branch_sb"] = _wgrad(one(y_sb), one(d_os), name="wgrad_branch_sb")[0]
    gw["w_out"] = _wgrad(one(merged), one(d2_bf), name="wgrad_out")[0]
    gw["w_ple_gate"] = _wgrad(one(hn_ple), one(du_ple), name="wgrad_ple_gate")[0]
    gw["w_ple_proj"] = _wgrad(one(p), one(dt_ple), name="wgrad_ple_proj")[0]

    fold = lambda t: jnp.sum(t[0:1].reshape(N_HEADS, HEAD_DIM), axis=0, keepdims=True)
    gs = {
        "ffn1_norm": dg_ffn1[0:1], "mix_norm": dg_mix[0:1], "ffn2_norm": dg_ffn2[0:1], "ple_norm": dg_ple[0:1],
        "forget_bias": dbias8[0:1, 0:N_HEADS], "q_norm": fold(dqn8), "k_norm": fold(dkn8),
    }
    return loss_sum, grad_x, gw, gs


def _position():
    return lax.axis_index("x"), lax.axis_index("y"), lax.axis_index("c")


def _other_chips(x, y):
    return [(1 - x, y), (x, 1 - y), (1 - x, 1 - y)]


ANY = pl.BlockSpec(memory_space=pl.ANY)


def _place_own_shard(w, q):
    rows, cols = w.shape
    tr = _row_block(rows, cols * 4, budget=2 * MIB)

    def body(q_ref, w_ref, o_ref):
        o_ref[0] = w_ref[...].astype(BF16)

    return pl.pallas_call(
        body,
        grid_spec=pltpu.PrefetchScalarGridSpec(
            num_scalar_prefetch=1,
            grid=(rows // tr,),
            in_specs=[pl.BlockSpec((tr, cols), lambda i, q_ref: (i, 0))],
            out_specs=pl.BlockSpec((1, tr, cols), lambda i, q_ref: (q_ref[0], i, 0)),
        ),
        out_shape=jax.ShapeDtypeStruct((N_CHIPS, rows, cols), BF16),
        name="place_own_shard",
    )(q, w)


def _allgather_weights(slots):
    n = len(slots)

    def body(*refs):
        bufs = refs[n:2 * n]
        send_sems, recv_sems = refs[2 * n:]
        x, y, c = _position()
        q = 2 * x + y
        chips = _other_chips(x, y)
        sibling = (x, y, 1 - c)

        def half(a, slot, which):
            r2 = slots[a].shape[1] // 2
            return bufs[a].at[slot, pl.ds(which * r2, r2), :]

        def copy(a, k, region, to):
            return pltpu.make_async_remote_copy(
                src_ref=region, dst_ref=region, send_sem=send_sems.at[6 * a + k], recv_sem=recv_sems.at[6 * a + k],
                device_id=to, device_id_type=MESH)

        sent = []
        for a in range(n):
            for k, (tx, ty) in enumerate(chips):
                cp = copy(a, k, half(a, q, c), (tx, ty, c))
                cp.start()
                sent.append(cp)
        for a in range(n):
            for k, (tx, ty) in enumerate(chips):
                landed = half(a, 2 * tx + ty, c)
                copy(a, k, landed, (tx, ty, c)).wait_recv()
                fwd = copy(a, 3 + k, landed, sibling)
                fwd.start()
                sent.append(fwd)
        for a in range(n):
            for k, (tx, ty) in enumerate(chips):
                copy(a, 3 + k, half(a, 2 * tx + ty, 1 - c), sibling).wait_recv()
        for cp in sent:
            cp.wait_send()

    return pl.pallas_call(
        body,
        in_specs=[ANY] * n,
        out_specs=[ANY] * n,
        out_shape=[jax.ShapeDtypeStruct(s.shape, s.dtype) for s in slots],
        input_output_aliases={a: a for a in range(n)},
        scratch_shapes=[pltpu.SemaphoreType.DMA((6 * n,)), pltpu.SemaphoreType.DMA((6 * n,))],
        name="allgather_weights",
    )(*slots)


def _exchange_pair_halves(grads):
    n = len(grads)

    def body(*refs):
        ins, outs = refs[0:n], refs[n:2 * n]
        send_sems, recv_sems = refs[2 * n:]
        x, y, c = _position()
        copies = []
        for a in range(n):
            r2 = grads[a].shape[1] // 2
            cp = pltpu.make_async_remote_copy(
                src_ref=ins[a].at[:, pl.ds((1 - c) * r2, r2), :], dst_ref=outs[a],
                send_sem=send_sems.at[a], recv_sem=recv_sems.at[a], device_id=(x, y, 1 - c), device_id_type=MESH)
            cp.start()
            copies.append(cp)
        for cp in copies:
            cp.wait()

    return pl.pallas_call(
        body,
        in_specs=[ANY] * n,
        out_specs=[ANY] * n,
        out_shape=[jax.ShapeDtypeStruct((N_CHIPS, g.shape[1] // 2, g.shape[2]), g.dtype) for g in grads],
        scratch_shapes=[pltpu.SemaphoreType.DMA((n,)), pltpu.SemaphoreType.DMA((n,))],
        name="rs_pair_exchange",
    )(*grads)


def _scatter_to_owner_chips(pairs):
    n = len(pairs)

    def body(*refs):
        ins, outs = refs[0:n], refs[n:2 * n]
        send_sems, recv_sems, local_sems = refs[2 * n:]
        x, y, c = _position()
        q = 2 * x + y
        chips = _other_chips(x, y)
        started = []
        for a in range(n):
            mine = pltpu.make_async_copy(ins[a].at[q], outs[a].at[q], local_sems.at[a])
            mine.start()
            started.append(mine)
            for k, (tx, ty) in enumerate(chips):
                cp = pltpu.make_async_remote_copy(
                    src_ref=ins[a].at[2 * tx + ty], dst_ref=outs[a].at[q],
                    send_sem=send_sems.at[3 * a + k], recv_sem=recv_sems.at[3 * a + k],
                    device_id=(tx, ty, c), device_id_type=MESH)
                cp.start()
                started.append(cp)
        for cp in started:
            cp.wait()

    return pl.pallas_call(
        body,
        in_specs=[ANY] * n,
        out_specs=[ANY] * n,
        out_shape=[jax.ShapeDtypeStruct(p.shape, p.dtype) for p in pairs],
        scratch_shapes=[pltpu.SemaphoreType.DMA((3 * n,)), pltpu.SemaphoreType.DMA((3 * n,)),
                        pltpu.SemaphoreType.DMA((n,))],
        name="rs_scatter",
    )(*pairs)


def _join_halves(shards):
    n = len(shards)

    def body(*refs):
        bufs = refs[n:2 * n]
        send_sems, recv_sems = refs[2 * n:]
        x, y, c = _position()
        started = []
        for a in range(n):
            r2 = shards[a].shape[0] // 2
            mine = bufs[a].at[pl.ds(c * r2, r2), :]
            cp = pltpu.make_async_remote_copy(
                src_ref=mine, dst_ref=mine, send_sem=send_sems.at[a], recv_sem=recv_sems.at[a],
                device_id=(x, y, 1 - c), device_id_type=MESH)
            cp.start()
            started.append(cp)
        for cp in started:
            cp.wait()

    return pl.pallas_call(
        body,
        in_specs=[ANY] * n,
        out_specs=[ANY] * n,
        out_shape=[jax.ShapeDtypeStruct(t.shape, t.dtype) for t in shards],
        input_output_aliases={a: a for a in range(n)},
        scratch_shapes=[pltpu.SemaphoreType.DMA((n,)), pltpu.SemaphoreType.DMA((n,))],
        name="rs_join_halves",
    )(*shards)


def _add_pair(g, got, c):
    _, r2, cols = got.shape

    def body(c_ref, g_ref, got_ref, o_ref):
        o_ref[...] = (g_ref[...].astype(F32) + got_ref[...].astype(F32)).astype(BF16)

    spec = pl.BlockSpec((1, r2, cols), lambda s, c_ref: (s, 0, 0))
    return pl.pallas_call(
        body,
        grid_spec=pltpu.PrefetchScalarGridSpec(
            num_scalar_prefetch=1,
            grid=(N_CHIPS,),
            in_specs=[pl.BlockSpec((1, r2, cols), lambda s, c_ref: (s, c_ref[0], 0)), spec],
            out_specs=spec,
        ),
        out_shape=jax.ShapeDtypeStruct(got.shape, BF16),
        name="rs_add_pair",
    )(c, g, got)


def _add_chips(parts, c):
    _, r2, cols = parts.shape

    def body(c_ref, p0, p1, p2, p3, o_ref):
        o_ref[...] = ((p0[0].astype(F32) + p1[0].astype(F32)) + p2[0].astype(F32)) + p3[0].astype(F32)

    specs = [pl.BlockSpec((1, r2, cols), functools.partial(lambda i, c_ref, s: (s, 0, 0), s=s))
             for s in range(N_CHIPS)]
    return pl.pallas_call(
        body,
        grid_spec=pltpu.PrefetchScalarGridSpec(
            num_scalar_prefetch=1,
            grid=(1,),
            in_specs=specs,
            out_specs=pl.BlockSpec((r2, cols), lambda i, c_ref: (c_ref[0], 0)),
        ),
        out_shape=jax.ShapeDtypeStruct((2 * r2, cols), F32),
        name="rs_add_chips",
    )(c, parts, parts, parts, parts)


def _allreduce_small(part):
    shape = part.shape

    def body(in_ref, out_ref, gather_ref, send_sems, recv_sems):
        x, y, c = _position()
        me = 4 * x + 2 * y + c
        relations = [(a, b, d) for a in (0, 1) for b in (0, 1) for d in (0, 1)][1:]
        flip = lambda v, f: 1 - v if f else v
        copies = []
        for k, (a, b, d) in enumerate(relations):
            cp = pltpu.make_async_remote_copy(
                src_ref=in_ref, dst_ref=gather_ref.at[me], send_sem=send_sems.at[k], recv_sem=recv_sems.at[k],
                device_id=(flip(x, a), flip(y, b), flip(c, d)), device_id_type=MESH)
            cp.start()
            copies.append(cp)
        gather_ref[me] = in_ref[...]
        for cp in copies:
            cp.wait()
        total = gather_ref[0]
        for dev in range(1, 8):
            total = total + gather_ref[dev]
        out_ref[...] = total

    vmem = pl.BlockSpec(memory_space=pltpu.VMEM)
    return pl.pallas_call(
        body,
        in_specs=[vmem],
        out_specs=vmem,
        out_shape=jax.ShapeDtypeStruct(shape, F32),
        scratch_shapes=[pltpu.VMEM((8,) + shape, F32), pltpu.SemaphoreType.DMA((7,)), pltpu.SemaphoreType.DMA((7,))],
        name="allreduce_small",
    )(part)


def _adamw(w, g, m, v):
    rows, cols = w.shape
    tr = _row_block(rows, cols * 4, budget=MIB)
    c1 = 1.0 / (1.0 - ADAM_B1 ** ADAM_STEP)
    c2 = 1.0 / (1.0 - ADAM_B2 ** ADAM_STEP)

    def body(w_ref, g_ref, m_ref, v_ref, d_ref, nm_ref, nv_ref):
        g_ = g_ref[...]
        nm = ADAM_B1 * m_ref[...] + (1.0 - ADAM_B1) * g_
        nv = ADAM_B2 * v_ref[...] + (1.0 - ADAM_B2) * (g_ * g_)
        nm_ref[...] = nm
        nv_ref[...] = nv
        d_ref[...] = -ADAM_LR * ((nm * c1) / (jnp.sqrt(nv * c2) + ADAM_EPS) + ADAM_WD * w_ref[...])

    spec = pl.BlockSpec((tr, cols), lambda i: (i, 0))
    out = jax.ShapeDtypeStruct((rows, cols), F32)
    return pl.pallas_call(
        body,
        grid=(rows // tr,),
        in_specs=[spec] * 4,
        out_specs=[spec] * 3,
        out_shape=[out] * 3,
        name="adamw",
    )(w, g, m, v)


BIG = ["ffn1_w_gate", "ffn1_w_up", "ffn1_w_down", "w_in", "w_branch_fox", "w_branch_sb", "w_out",
       "ffn2_w_gate", "ffn2_w_up", "ffn2_w_down", "w_ple_gate", "w_ple_proj"]
SMALL = ["ffn1_norm", "mix_norm", "ffn2_norm", "ple_norm", "forget_bias", "q_norm", "k_norm"]
COLUMN_SHARDED = ["ffn1_w_gate", "ffn1_w_up", "w_in", "w_branch_fox", "w_branch_sb",
                  "ffn2_w_gate", "ffn2_w_up", "w_ple_proj"]
KEPT_AS_SHARDS = ["ffn1_w_gate", "ffn1_w_up", "ffn1_w_down", "ffn2_w_gate", "ffn2_w_up", "ffn2_w_down"]
ORDER = ["ffn1_norm", "ffn1_w_gate", "ffn1_w_up", "ffn1_w_down", "mix_norm", "w_in", "forget_bias", "q_norm",
         "k_norm", "w_branch_fox", "w_branch_sb", "w_out", "ffn2_norm", "ffn2_w_gate", "ffn2_w_up",
         "ffn2_w_down", "ple_norm", "w_ple_gate", "w_ple_proj"]
SMALL_ROWS = {"ffn1_norm": 0, "mix_norm": 1, "ffn2_norm": 2, "ple_norm": 3}
SMALL_COLS = {"forget_bias": (0, N_HEADS), "q_norm": (N_HEADS, HEAD_DIM), "k_norm": (N_HEADS + HEAD_DIM, HEAD_DIM)}
LOSS_ROW = 5


def _whole(name, gathered):
    if name in COLUMN_SHARDED:
        return jnp.concatenate([gathered[s] for s in range(N_CHIPS)], axis=1)
    return gathered.reshape(-1, gathered.shape[-1])


def _as_shards(name, whole):
    if name in COLUMN_SHARDED:
        k, n = whole.shape
        return whole.reshape(k, N_CHIPS, n // N_CHIPS).transpose(1, 0, 2)
    return whole.reshape(N_CHIPS, whole.shape[0] // N_CHIPS, whole.shape[1])


def _pack_small(values, extra=None):
    rows = [values[k] for k in ("ffn1_norm", "mix_norm", "ffn2_norm", "ple_norm")]
    tail = jnp.concatenate([values["forget_bias"], values["q_norm"], values["k_norm"]], axis=1)
    rows.append(jnp.pad(tail, ((0, 0), (0, D_MODEL - tail.shape[1]))))
    packed = jnp.concatenate(rows + [jnp.zeros((3, D_MODEL), F32)], axis=0)
    if extra is not None:
        packed = packed.at[LOSS_ROW, 0].set(extra)
    return packed


def _unpack_small(packed):
    out = {k: packed[r:r + 1] for k, r in SMALL_ROWS.items()}
    for k, (start, size) in SMALL_COLS.items():
        out[k] = packed[4:5, start:start + size]
    return out


def kernel(x, p, ffn1_norm, ffn1_w_gate, ffn1_w_up, ffn1_w_down, mix_norm, w_in, forget_bias, q_norm, k_norm, w_branch_fox, w_branch_sb, w_out, ffn2_norm, ffn2_w_gate, ffn2_w_up, ffn2_w_down, ple_norm, w_ple_gate, w_ple_proj, loss_target, m_ffn1_norm, m_ffn1_w_gate, m_ffn1_w_up, m_ffn1_w_down, m_mix_norm, m_w_in, m_forget_bias, m_q_norm, m_k_norm, m_w_branch_fox, m_w_branch_sb, m_w_out, m_ffn2_norm, m_ffn2_w_gate, m_ffn2_w_up, m_ffn2_w_down, m_ple_norm, m_w_ple_gate, m_w_ple_proj, v_ffn1_norm, v_ffn1_w_gate, v_ffn1_w_up, v_ffn1_w_down, v_mix_norm, v_w_in, v_forget_bias, v_q_norm, v_k_norm, v_w_branch_fox, v_w_branch_sb, v_w_out, v_ffn2_norm, v_ffn2_w_gate, v_ffn2_w_up, v_ffn2_w_down, v_ple_norm, v_w_ple_gate, v_w_ple_proj):
    args = dict(locals())
    weights = {k: args[k] for k in ORDER}
    moments_m = {k: args["m_" + k] for k in ORDER}
    moments_v = {k: args["v_" + k] for k in ORDER}

    c_idx = lax.axis_index("c").astype(jnp.int32).reshape(1)
    q_idx = (2 * lax.axis_index("x") + lax.axis_index("y")).astype(jnp.int32).reshape(1)
    gathered = _allgather_weights([_place_own_shard(weights[k][0], q_idx) for k in BIG])
    full = {}
    for k, gth in zip(BIG, gathered):
        full[k] = gth if k in KEPT_AS_SHARDS else _whole(k, gth)
    small = {k: weights[k] for k in SMALL}

    loss_sum, grad_x, gw, gs = _local_grads(x[0], p[0, 0], loss_target[0], small, full)

    slots = [gw[k] if k in KEPT_AS_SHARDS else _as_shards(k, gw[k]) for k in BIG]
    from_core = _exchange_pair_halves(slots)
    pairs = [_add_pair(g, got, c_idx) for g, got in zip(slots, from_core)]
    parts = _scatter_to_owner_chips(pairs)
    grads_big = dict(zip(BIG, _join_halves([_add_chips(t, c_idx) for t in parts])))
    reduced = _allreduce_small(_pack_small(gs, extra=loss_sum[0, 0]))
    grads_small = _unpack_small(reduced)
    loss = reduced[LOSS_ROW, 0]

    grads, deltas, new_m, new_v = {}, {}, {}, {}
    for k in BIG:
        grads[k] = grads_big[k][None]
        d, nm, nv = _adamw(weights[k][0], grads_big[k], moments_m[k][0], moments_v[k][0])
        deltas[k], new_m[k], new_v[k] = d[None], nm[None], nv[None]
    d_s, nm_s, nv_s = _adamw(_pack_small({k: weights[k] for k in SMALL}), reduced,
                             _pack_small({k: moments_m[k] for k in SMALL}),
                             _pack_small({k: moments_v[k] for k in SMALL}))
    for k in SMALL:
        grads[k] = grads_small[k]
    for name, packed in (("d", d_s), ("m", nm_s), ("v", nv_s)):
        target = {"d": deltas, "m": new_m, "v": new_v}[name]
        target.update(_unpack_small(packed))

    return (loss, grad_x[None], *[grads[k] for k in ORDER], *[deltas[k] for k in ORDER],
            *[new_m[k] for k in ORDER], *[new_v[k] for k in ORDER])
```

```python
import functools

import jax
import jax.numpy as jnp
from jax import lax
from jax.experimental import pallas as pl
from jax.experimental.pallas import tpu as pltpu

F32 = jnp.float32
BF16 = jnp.bfloat16

D_MODEL = 1024
D_FF = 2816
N_CHIPS = 4
FF_SHARD = D_FF // N_CHIPS
HEAD_DIM = 64
N_HEADS = 8
ATT_W = N_HEADS * HEAD_DIM
PAIR_W = 2 * HEAD_DIM
N_PAIRS = N_HEADS // 2
PLE_DIM = 256
IN_WIDTH = 3 * ATT_W + N_HEADS + 3 * ATT_W + 2 * D_MODEL
EPS = 1e-6
QK_SCALE = HEAD_DIM ** -0.5
LANES = 128
ATT_BLOCK = 256
ATT_Q_BLOCK = 512
NEG_BIG = -1e30
EXP_UNDERFLOW = 110.0

ADAM_LR = 0.001
ADAM_B1 = 0.9
ADAM_B2 = 0.999
ADAM_EPS = 1e-08
ADAM_WD = 0.01
ADAM_STEP = 10

MESH = pl.DeviceIdType.MESH
MIB = 1024 * 1024


def _cparams(vmem_mib=48):
    return pltpu.CompilerParams(vmem_limit_bytes=vmem_mib * MIB)


def _dot(a, b):
    return jnp.dot(a, b, preferred_element_type=F32)


def _dot_tn(a, b):
    return lax.dot_general(a, b, (((0,), (0,)), ((), ())), preferred_element_type=F32)


def _dot_nt(a, b):
    return lax.dot_general(a, b, (((1,), (1,)), ((), ())), preferred_element_type=F32)


def _sigmoid(x):
    return 1.0 / (1.0 + jnp.exp(-x))


def _split2(x):
    hi = x.astype(BF16)
    lo = (x - hi.astype(F32)).astype(BF16)
    return hi, lo


def _dot_split2(x, m):
    hi, lo = _split2(x)
    return _dot(hi, m) + _dot(lo, m)


def _split3(x):
    hi = x.astype(BF16)
    rest = x - hi.astype(F32)
    mid = rest.astype(BF16)
    lo = (rest - mid.astype(F32)).astype(BF16)
    return hi, mid, lo


def _rms(x):
    r = lax.rsqrt(jnp.mean(x * x, axis=-1, keepdims=True) + EPS)
    return x * r, r


def _rms_bwd(dh, xn, r, g):
    dxn = dh * g
    return r * (dxn - xn * jnp.mean(dxn * xn, axis=-1, keepdims=True))


def _colsum(x):
    return jnp.sum(x, axis=0, keepdims=True)


def _row_block(rows, row_bytes, budget):
    best = None
    for t in range(8, rows + 1, 8):
        if rows % t == 0 and t * row_bytes <= budget:
            best = t
    return best if best is not None else rows


def _ffn_fwd(x, g, wg, wu, wd, tm=512):
    s_len = x.shape[0]

    def body(x_ref, g_ref, wg_ref, wu_ref, wd_ref, o_ref, a_ref, b_ref, h_s, acc_s):
        j = pl.program_id(1)

        @pl.when(j == 0)
        def _():
            xn, _ = _rms(x_ref[...])
            h_s[...] = (xn * g_ref[...]).astype(BF16)
            acc_s[...] = jnp.zeros_like(acc_s)

        h = h_s[...]
        a = _dot_nt(h, wg_ref[0])
        b = _dot_nt(h, wu_ref[0])
        a_ref[0] = a.astype(BF16)
        b_ref[0] = b.astype(BF16)
        u = (a * _sigmoid(a) * b).astype(BF16)
        acc_s[...] += _dot(u, wd_ref[0])

        @pl.when(j == N_CHIPS - 1)
        def _():
            o_ref[...] = x_ref[...] + 0.5 * acc_s[...]

    return pl.pallas_call(
        body,
        grid=(s_len // tm, N_CHIPS),
        in_specs=[
            pl.BlockSpec((tm, D_MODEL), lambda i, j: (i, 0)),
            pl.BlockSpec((1, D_MODEL), lambda i, j: (0, 0)),
            pl.BlockSpec((1, FF_SHARD, D_MODEL), lambda i, j: (j, 0, 0)),
            pl.BlockSpec((1, FF_SHARD, D_MODEL), lambda i, j: (j, 0, 0)),
            pl.BlockSpec((1, FF_SHARD, D_MODEL), lambda i, j: (j, 0, 0)),
        ],
        out_specs=[pl.BlockSpec((tm, D_MODEL), lambda i, j: (i, 0)),
                   pl.BlockSpec((1, tm, FF_SHARD), lambda i, j: (j, i, 0)),
                   pl.BlockSpec((1, tm, FF_SHARD), lambda i, j: (j, i, 0))],
        out_shape=[jax.ShapeDtypeStruct((s_len, D_MODEL), F32),
                   jax.ShapeDtypeStruct((N_CHIPS, s_len, FF_SHARD), BF16),
                   jax.ShapeDtypeStruct((N_CHIPS, s_len, FF_SHARD), BF16)],
        scratch_shapes=[pltpu.VMEM((tm, D_MODEL), BF16), pltpu.VMEM((tm, D_MODEL), F32)],
        compiler_params=_cparams(48),
        name="ffn_fwd",
    )(x, g, wg, wu, wd)


def _ffn_bwd(x, d, g, a_pre, b_pre, wg, wu, wd, tm=512):
    s_len = x.shape[0]
    nb = s_len // tm

    def body(x_ref, d_ref, g_ref, a_ref, b_ref, wg_ref, wu_ref, wd_ref,
             dx_ref, u_ref, da_ref, db_ref, h_ref, dbf_ref, dg_ref, dbf_s, dh_s):
        i = pl.program_id(0)
        j = pl.program_id(1)

        @pl.when(j == 0)
        def _():
            xn, _ = _rms(x_ref[...])
            h_ref[...] = (xn * g_ref[...]).astype(BF16)
            dbf = d_ref[...].astype(BF16)
            dbf_s[...] = dbf
            dbf_ref[...] = dbf
            dh_s[...] = jnp.zeros_like(dh_s)

        @pl.when((i == 0) & (j == 0))
        def _():
            dg_ref[...] = jnp.zeros_like(dg_ref)

        a = a_ref[0].astype(F32)
        b = b_ref[0].astype(F32)
        du = 0.5 * _dot_nt(dbf_s[...], wd_ref[0])
        s = _sigmoid(a)
        silu = a * s
        da = (du * b * (s * (1.0 + a * (1.0 - s)))).astype(BF16)
        db = (du * silu).astype(BF16)
        u_ref[0] = (silu * b).astype(BF16)
        da_ref[0] = da
        db_ref[0] = db
        dh_s[...] += _dot(da, wg_ref[0]) + _dot(db, wu_ref[0])

        @pl.when(j == N_CHIPS - 1)
        def _():
            xn, r = _rms(x_ref[...])
            dh = dh_s[...]
            dx_ref[...] = d_ref[...] + _rms_bwd(dh, xn, r, g_ref[...])
            dg_ref[0:1, :] += _colsum(dh * xn)

    row = lambda i, j: (i, 0)
    shard = lambda i, j: (j, 0, 0)
    act = lambda i, j: (j, i, 0)
    return pl.pallas_call(
        body,
        grid=(nb, N_CHIPS),
        in_specs=[
            pl.BlockSpec((tm, D_MODEL), row),
            pl.BlockSpec((tm, D_MODEL), row),
            pl.BlockSpec((1, D_MODEL), lambda i, j: (0, 0)),
            pl.BlockSpec((1, tm, FF_SHARD), act),
            pl.BlockSpec((1, tm, FF_SHARD), act),
            pl.BlockSpec((1, FF_SHARD, D_MODEL), shard),
            pl.BlockSpec((1, FF_SHARD, D_MODEL), shard),
            pl.BlockSpec((1, FF_SHARD, D_MODEL), shard),
        ],
        out_specs=[
            pl.BlockSpec((tm, D_MODEL), row),
            pl.BlockSpec((1, tm, FF_SHARD), act),
            pl.BlockSpec((1, tm, FF_SHARD), act),
            pl.BlockSpec((1, tm, FF_SHARD), act),
            pl.BlockSpec((tm, D_MODEL), row),
            pl.BlockSpec((tm, D_MODEL), row),
            pl.BlockSpec((8, D_MODEL), lambda i, j: (0, 0)),
        ],
        out_shape=[
            jax.ShapeDtypeStruct((s_len, D_MODEL), F32),
            jax.ShapeDtypeStruct((N_CHIPS, s_len, FF_SHARD), BF16),
            jax.ShapeDtypeStruct((N_CHIPS, s_len, FF_SHARD), BF16),
            jax.ShapeDtypeStruct((N_CHIPS, s_len, FF_SHARD), BF16),
            jax.ShapeDtypeStruct((s_len, D_MODEL), BF16),
            jax.ShapeDtypeStruct((s_len, D_MODEL), BF16),
            jax.ShapeDtypeStruct((8, D_MODEL), F32),
        ],
        scratch_shapes=[
            pltpu.VMEM((tm, D_MODEL), BF16),
            pltpu.VMEM((tm, D_MODEL), F32),
        ],
        compiler_params=_cparams(56),
        name="ffn_bwd",
    )(x, d, g, a_pre, b_pre, wg, wu, wd)


def _wgrad(a, b, scale=1.0, name="wgrad"):
    na, s_len, k_dim = a.shape
    nb, _, n_dim = b.shape
    n = max(na, nb)
    ts = min(s_len, 1024)
    steps = s_len // ts

    def body(a_ref, b_ref, o_ref, acc_s):
        s = pl.program_id(1)

        @pl.when(s == 0)
        def _():
            acc_s[...] = jnp.zeros_like(acc_s)

        acc_s[...] += _dot_tn(a_ref[0].astype(BF16), b_ref[0].astype(BF16))

        @pl.when(s == steps - 1)
        def _():
            o_ref[0] = (acc_s[...] * scale).astype(BF16)

    a_map = (lambda m, s: (m, s, 0)) if na > 1 else (lambda m, s: (0, s, 0))
    b_map = (lambda m, s: (m, s, 0)) if nb > 1 else (lambda m, s: (0, s, 0))
    return pl.pallas_call(
        body,
        grid=(n, steps),
        in_specs=[pl.BlockSpec((1, ts, k_dim), a_map), pl.BlockSpec((1, ts, n_dim), b_map)],
        out_specs=pl.BlockSpec((1, k_dim, n_dim), lambda m, s: (m, 0, 0)),
        out_shape=jax.ShapeDtypeStruct((n, k_dim, n_dim), BF16),
        scratch_shapes=[pltpu.VMEM((k_dim, n_dim), F32)],
        compiler_params=_cparams(56),
        name=name,
    )(a, b)


def _head_sum_matrices():
    lane = lax.broadcasted_iota(jnp.int32, (ATT_W, LANES), 0) // HEAD_DIM
    col = lax.broadcasted_iota(jnp.int32, (ATT_W, LANES), 1)
    bd = (lane == col).astype(BF16)
    return bd, bd.T


def _head_mean(t, bd, bd_t):
    per_head = _dot_split2(t, bd) * (1.0 / HEAD_DIM)
    return _dot_split2(per_head, bd_t)


def _head_rms(x, bd, bd_t):
    per_head = _dot_split2(x * x, bd) * (1.0 / HEAD_DIM)
    r = lax.rsqrt(per_head + EPS)
    rw = _dot_split2(r, bd_t)
    return x * rw, rw


def _log_sigmoid(z):
    return jnp.minimum(z, 0.0) - jnp.log(1.0 + jnp.exp(-jnp.abs(z)))


def _inproj_fwd(x1, g, w_fox, w_fl, w_sb, w_gates, bias, qn, kn, bd, bd_t, tm=256):
    s_len = x1.shape[0]

    def body(x_ref, g_ref, wf_ref, wl_ref, ws_ref, wg_ref, bias_ref, qn_ref, kn_ref, bd_ref, bdt_ref,
             fq_ref, fk_ref, qs_ref, kf_ref, vf_ref, logf_ref, sq_ref, sk_ref, sv_ref, gates_ref):
        xn, _ = _rms(x_ref[...])
        h = (xn * g_ref[...]).astype(BF16)
        zf = _dot(h, wf_ref[...])
        fq = zf[:, 0:ATT_W]
        fk = zf[:, ATT_W:2 * ATT_W]
        fq_ref[...] = fq
        fk_ref[...] = fk
        bd_m = bd_ref[...]
        bdt_m = bdt_ref[...]
        fqn, _ = _head_rms(fq, bd_m, bdt_m)
        fkn, _ = _head_rms(fk, bd_m, bdt_m)
        qs_ref[...] = (fqn * qn_ref[...]).astype(BF16) * QK_SCALE
        kf_ref[...] = (fkn * kn_ref[...]).astype(BF16)
        vf_ref[...] = zf[:, 2 * ATT_W:3 * ATT_W].astype(BF16)
        logf_ref[...] = _log_sigmoid(_dot(h, wl_ref[...]) + bias_ref[...])
        zs = _dot(h, ws_ref[...])
        sq_ref[...] = zs[:, 0:ATT_W].astype(BF16) * QK_SCALE
        sk_ref[...] = zs[:, ATT_W:2 * ATT_W].astype(BF16)
        sv_ref[...] = zs[:, 2 * ATT_W:3 * ATT_W].astype(BF16)
        gates_ref[...] = _dot(h, wg_ref[...])

    row = lambda i: (i, 0)
    full = lambda i: (0, 0)
    att = lambda dt: jax.ShapeDtypeStruct((s_len, ATT_W), dt)
    return pl.pallas_call(
        body,
        grid=(s_len // tm,),
        in_specs=[
            pl.BlockSpec((tm, D_MODEL), row),
            pl.BlockSpec((1, D_MODEL), full),
            pl.BlockSpec((D_MODEL, 3 * ATT_W), full),
            pl.BlockSpec((D_MODEL, LANES), full),
            pl.BlockSpec((D_MODEL, 3 * ATT_W), full),
            pl.BlockSpec((D_MODEL, 2 * D_MODEL), full),
            pl.BlockSpec((1, LANES), full),
            pl.BlockSpec((1, ATT_W), full),
            pl.BlockSpec((1, ATT_W), full),
            pl.BlockSpec((ATT_W, LANES), full),
            pl.BlockSpec((LANES, ATT_W), full),
        ],
        out_specs=[
            pl.BlockSpec((tm, ATT_W), row), pl.BlockSpec((tm, ATT_W), row),
            pl.BlockSpec((tm, ATT_W), row), pl.BlockSpec((tm, ATT_W), row), pl.BlockSpec((tm, ATT_W), row),
            pl.BlockSpec((tm, LANES), row),
            pl.BlockSpec((tm, ATT_W), row), pl.BlockSpec((tm, ATT_W), row), pl.BlockSpec((tm, ATT_W), row),
            pl.BlockSpec((tm, 2 * D_MODEL), row),
        ],
        out_shape=[
            att(F32), att(F32), att(BF16), att(BF16), att(BF16),
            jax.ShapeDtypeStruct((s_len, LANES), F32),
            att(BF16), att(BF16), att(BF16),
            jax.ShapeDtypeStruct((s_len, 2 * D_MODEL), F32),
        ],
        compiler_params=_cparams(56),
        name="inproj_fwd",
    )(x1, g, w_fox, w_fl, w_sb, w_gates, bias, qn, kn, bd, bd_t)


def _tri(n, kind):
    r = lax.broadcasted_iota(jnp.int32, (n, n), 0)
    c = lax.broadcasted_iota(jnp.int32, (n, n), 1)
    m = {"row_ge_col": r >= c, "row_le_col": r <= c, "row_gt_col": r > c, "row_lt_col": r < c}[kind]
    return m.astype(BF16)


def _cumsum_rows(x, reverse, tm=256):
    s_len = x.shape[0]
    nb = s_len // tm
    tri = _tri(tm, "row_le_col" if reverse else "row_ge_col")
    edge = 0 if reverse else tm - 1

    def body(x_ref, tri_ref, o_ref, carry_s):
        @pl.when(pl.program_id(0) == 0)
        def _():
            carry_s[...] = jnp.zeros_like(carry_s)

        hi, mid, lo = _split3(x_ref[...])
        t = tri_ref[...]
        y = _dot(t, hi) + _dot(t, mid) + _dot(t, lo) + carry_s[...]
        o_ref[...] = y
        carry_s[...] = y[edge:edge + 1, :]

    order = (lambda i: (nb - 1 - i, 0)) if reverse else (lambda i: (i, 0))
    return pl.pallas_call(
        body,
        grid=(nb,),
        in_specs=[pl.BlockSpec((tm, LANES), order), pl.BlockSpec((tm, tm), lambda i: (0, 0))],
        out_specs=pl.BlockSpec((tm, LANES), order),
        out_shape=jax.ShapeDtypeStruct((s_len, LANES), F32),
        scratch_shapes=[pltpu.VMEM((1, LANES), F32)],
        name="cumsum_rev" if reverse else "cumsum_fwd",
    )(x, tri)


def _unblocked_t(t4):
    _, nb, _, blk = t4.shape
    return t4.transpose(1, 3, 0, 2).reshape(nb * blk, ATT_W)


def _blocked_rows(t, blk):
    return t.reshape(t.shape[0] // blk, blk, t.shape[1])


def _pair_rows_t(f8, blk):
    nb = f8.shape[0] // blk
    t = f8.reshape(nb, blk, N_PAIRS, 2).transpose(2, 0, 3, 1)
    return jnp.pad(t, ((0, 0), (0, 0), (0, 6), (0, 0)))


def _unpair_rows_t(t4):
    _, nb, _, blk = t4.shape
    return t4[:, :, 0:2, :].transpose(1, 3, 0, 2).reshape(nb * blk, N_HEADS)


def _head_masks(tq):
    lane = lax.broadcasted_iota(jnp.int32, (tq, PAIR_W), 1)
    return lane < HEAD_DIM


def _causal_mask(tq, tk, offset, strict):
    d = lax.broadcasted_iota(jnp.int32, (tq, tk), 1) - lax.broadcasted_iota(jnp.int32, (tq, tk), 0)
    return (d < offset) if strict else (d <= offset)


def _heads_of(ref, first):
    t = ref[...]
    zero = jnp.zeros_like(t)
    return [jnp.where(first, t, zero), jnp.where(first, zero, t)]


def _head_cols(ref):
    t = ref[...]
    return [t[:, 0:1], t[:, HEAD_DIM:HEAD_DIM + 1]]


def _att_specs(s_len):
    tq, tk = ATT_Q_BLOCK, ATT_BLOCK
    nq, nk = s_len // tq, s_len // tk
    return dict(
        nq=nq,
        q=pl.BlockSpec((tq, PAIR_W), lambda p, i: (i, p)),
        k_t=pl.BlockSpec((1, nk, PAIR_W, tk), lambda p, i: (p, 0, 0, 0)),
        k_rows=pl.BlockSpec((nk, tk, PAIR_W), lambda p, i: (0, 0, p)),
        f_t=pl.BlockSpec((1, nk, 8, tk), lambda p, i: (p, 0, 0, 0)),
        first=pl.BlockSpec((1, 1, 8, LANES), lambda p, i: (p, i, 0, 0)),
        wide=jax.ShapeDtypeStruct((s_len, ATT_W), F32),
        k_t_out=jax.ShapeDtypeStruct((N_PAIRS, nk, PAIR_W, tk), F32),
        f_t_out=jax.ShapeDtypeStruct((N_PAIRS, nk, 8, tk), F32),
        first_out=jax.ShapeDtypeStruct((N_PAIRS, nq, 8, LANES), F32),
        acc=pltpu.VMEM((2, tq, PAIR_W), F32),
    )


def _first_block(first_ref, limit):
    return jnp.clip(jnp.max(first_ref[0, 0]).astype(jnp.int32), 0, limit)


def _key_norm_bound(k):
    sq = jnp.sum(jnp.square(k.astype(F32)).reshape(k.shape[0], N_HEADS, HEAD_DIM), axis=-1)
    bound = jnp.sqrt(jnp.max(sq, axis=0)).reshape(N_PAIRS, 2)
    return jnp.broadcast_to(jnp.pad(bound, ((0, 0), (0, 6)))[:, :, None], (N_PAIRS, 8, LANES))


def _fox_fwd(qs, k3, v3, fw, ft4, kmax):
    sp = _att_specs(qs.shape[0])
    tq, tk = ATT_Q_BLOCK, ATT_BLOCK
    ratio = tq // tk

    def body(q_ref, k_ref, v_ref, fw_ref, ft_ref, kmax_ref, y_ref, lse_ref, first_ref, acc_ref, max_ref, sum_ref):
        i = pl.program_id(1)
        first = _head_masks(tq)
        qh = _heads_of(q_ref, first)
        fqh = _head_cols(fw_ref)
        acc_ref[...] = jnp.zeros_like(acc_ref)
        sum_ref[...] = jnp.zeros_like(sum_ref)
        max_ref[...] = jnp.full(max_ref.shape, NEG_BIG, F32)
        reach = []
        for n in range(2):
            qf = qh[n].astype(F32)
            reach.append(jnp.sqrt(jnp.sum(qf * qf, axis=-1, keepdims=True)) * kmax_ref[0, n:n + 1, 0:1] + fqh[n])

        def logits(j, shift, diag):
            k, fk = k_ref[j], ft_ref[0, j]
            raw = [_dot_nt(qh[n], k) for n in range(2)]
            out = []
            for n in range(2):
                s = raw[n] + (shift[n] - fk[n:n + 1, :])
                if diag:
                    s = jnp.where(_causal_mask(tq, tk, i * tq - j * tk, strict=False), s, NEG_BIG)
                out.append(s)
            return out

        def max_pass(j, diag):
            ss = logits(j, fqh, diag)
            for n in range(2):
                max_ref[n] = jnp.maximum(max_ref[n], ss[n])

        def sum_pass(j, shift, diag):
            ps = [jnp.exp(s) for s in logits(j, shift, diag)]
            v = v_ref[j]
            for n in range(2):
                sum_ref[n] += ps[n]
            for n in range(2):
                acc_ref[n] += _dot(ps[n].astype(BF16), v)

        for d in range(ratio):
            max_pass(ratio * i + d, True)

        def block_matters(j):
            gap = []
            for n in range(2):
                m_run = jnp.max(max_ref[n], axis=-1, keepdims=True)
                f_end = ft_ref[0, jnp.maximum(j, 0)][n:n + 1, tk - 1:tk]
                gap.append(jnp.max(reach[n] - m_run) - jnp.max(f_end))
            return (j >= 0) & (jnp.maximum(gap[0], gap[1]) > -EXP_UNDERFLOW)

        def walk_left(j):
            max_pass(j, False)
            return j - 1

        j_first = lax.while_loop(block_matters, walk_left, ratio * i - 1) + 1
        m = [jnp.max(max_ref[n], axis=-1, keepdims=True) for n in range(2)]
        shift = [fqh[n] - m[n] for n in range(2)]

        def one(j, c):
            sum_pass(j, shift, False)
            return c
        lax.fori_loop(j_first, ratio * i, one, 0)
        for d in range(ratio):
            sum_pass(ratio * i + d, shift, True)
        l = [jnp.sum(sum_ref[n], axis=-1, keepdims=True) for n in range(2)]
        y_ref[...] = jnp.where(first, acc_ref[0] / l[0], acc_ref[1] / l[1])
        lse_ref[...] = jnp.where(first, m[0] + jnp.log(l[0]), m[1] + jnp.log(l[1]))
        first_ref[...] = jnp.ones(first_ref.shape, F32) * j_first.astype(F32)

    tile = pltpu.VMEM((2, tq, tk), F32)
    return pl.pallas_call(
        body,
        grid=(N_PAIRS, sp["nq"]),
        in_specs=[sp["q"], sp["k_rows"], sp["k_rows"], sp["q"], sp["f_t"],
                  pl.BlockSpec((1, 8, LANES), lambda p, i: (p, 0, 0))],
        out_specs=[sp["q"], sp["q"], sp["first"]],
        out_shape=[sp["wide"], sp["wide"], sp["first_out"]],
        scratch_shapes=[sp["acc"], tile, tile],
        compiler_params=_cparams(56),
        name="fox_fwd",
    )(qs, k3, v3, fw, ft4, kmax)


def _fox_bwd(qs, k3, v3, dy, y, lse, fw, ft4, first_block):
    sp = _att_specs(qs.shape[0])
    tq, tk = ATT_Q_BLOCK, ATT_BLOCK
    ratio = tq // tk

    def body(q_ref, k_ref, v_ref, dy_ref, y_ref, lse_ref, fw_ref, ft_ref, first_ref,
             dq_ref, dfq_ref, dkt_ref, dvt_ref, dft_ref, acc_ref):
        i = pl.program_id(1)

        @pl.when(i == 0)
        def _():
            dkt_ref[...] = jnp.zeros_like(dkt_ref)
            dvt_ref[...] = jnp.zeros_like(dvt_ref)
            dft_ref[...] = jnp.zeros_like(dft_ref)

        first = _head_masks(tq)
        qh = _heads_of(q_ref, first)
        dyv = dy_ref[...]
        dyb = dyv.astype(BF16)
        zero = jnp.zeros_like(dyb)
        dyh = [jnp.where(first, dyb, zero), jnp.where(first, zero, dyb)]
        prod = dyv * y_ref[...]
        zf = jnp.zeros_like(prod)
        delta = [jnp.sum(jnp.where(first, prod, zf), axis=-1, keepdims=True),
                 jnp.sum(jnp.where(first, zf, prod), axis=-1, keepdims=True)]
        fqh = _head_cols(fw_ref)
        lseh = _head_cols(lse_ref)
        shift = [fqh[n] - lseh[n] for n in range(2)]
        acc_ref[...] = jnp.zeros_like(acc_ref)

        def block(j, rows, diag):
            mask = _causal_mask(tq, tk, i * tq - j * tk, strict=False) if diag else None
            k, v, fk = k_ref[j], v_ref[j], ft_ref[0, j]
            logits = [_dot_nt(qh[n], k) for n in range(2)]
            dps = [_dot_nt(dyh[n], v) for n in range(2)]
            pbs, dsbs, out = [], [], []
            for n in range(2):
                p = jnp.exp(logits[n] + (shift[n] - fk[n:n + 1, :]))
                if diag:
                    p = jnp.where(mask, p, 0.0)
                ds = p * (dps[n] - delta[n])
                pbs.append(p.astype(BF16))
                dsbs.append(ds.astype(BF16))
                out.append(rows[n] + jnp.sum(ds, axis=-1, keepdims=True))
                dft_ref[0, j, n:n + 1, :] -= _colsum(ds)
            for n in range(2):
                acc_ref[n] += _dot(dsbs[n], k)
            dkt_ref[0, j] += _dot_tn(qh[0], dsbs[0]) + _dot_tn(qh[1], dsbs[1])
            dvt_ref[0, j] += _dot_tn(dyh[0], pbs[0]) + _dot_tn(dyh[1], pbs[1])
            return tuple(out)

        rows = (jnp.zeros((tq, 1), F32),) * 2
        rows = lax.fori_loop(_first_block(first_ref, ratio * i), ratio * i, lambda j, c: block(j, c, False), rows)
        for d in range(ratio):
            rows = block(ratio * i + d, rows, True)
        dq_ref[...] = jnp.where(first, acc_ref[0], acc_ref[1])
        dfq_ref[...] = jnp.where(first, rows[0], rows[1])

    return pl.pallas_call(
        body,
        grid=(N_PAIRS, sp["nq"]),
        in_specs=[sp["q"], sp["k_rows"], sp["k_rows"], sp["q"], sp["q"], sp["q"], sp["q"], sp["f_t"], sp["first"]],
        out_specs=[sp["q"], sp["q"], sp["k_t"], sp["k_t"], sp["f_t"]],
        out_shape=[sp["wide"], sp["wide"], sp["k_t_out"], sp["k_t_out"], sp["f_t_out"]],
        scratch_shapes=[sp["acc"]],
        compiler_params=_cparams(56),
        name="fox_bwd",
    )(qs, k3, v3, dy, y, lse, fw, ft4, first_block)


SIGN_BIT = 0x80000000


def _sb_terms(z, mask, diag):
    neg_abs = pltpu.bitcast(pltpu.bitcast(z, jnp.uint32) | jnp.uint32(SIGN_BIT), F32)
    lb = jnp.minimum(z, 0.0) - jnp.log(1.0 + jnp.exp(neg_abs))
    l1m = lb - z
    if diag:
        l1m = jnp.where(mask, l1m, 0.0)
    return lb, l1m


def _dot_split2_stacked(x, m2):
    hi, lo = _split2(x)
    return _dot(jnp.concatenate([hi, lo], axis=1), m2)


def _tri_stacked(kind):
    t = _tri(ATT_BLOCK, kind)
    return jnp.concatenate([t, t], axis=0)


def _sb_fwd(qs, k3, v3):
    sp = _att_specs(qs.shape[0])
    tq, tk = ATT_Q_BLOCK, ATT_BLOCK
    ratio = tq // tk
    upper = _tri_stacked("row_gt_col")

    def body(q_ref, k_ref, v_ref, u_ref, y_ref, rtot_ref, first_ref, acc_ref):
        i = pl.program_id(1)
        first = _head_masks(tq)
        qh = _heads_of(q_ref, first)
        u = u_ref[...]
        acc_ref[...] = jnp.zeros_like(acc_ref)

        def block(j, rs, diag):
            mask = _causal_mask(tq, tk, i * tq - j * tk, strict=True) if diag else None
            k, v = k_ref[j], v_ref[j]
            logits = [_dot_nt(qh[n], k) for n in range(2)]
            terms = [_sb_terms(z, mask, diag) for z in logits]
            right = [_dot_split2_stacked(l1m, u) for _, l1m in terms]
            weights = []
            for n in range(2):
                a = jnp.exp(terms[n][0] + right[n] + rs[n])
                if diag:
                    a = jnp.where(mask, a, 0.0)
                weights.append(a.astype(BF16))
            for n in range(2):
                acc_ref[n] += _dot(weights[n], v)
            return tuple(rs[n] + jnp.sum(terms[n][1], axis=-1, keepdims=True) for n in range(2))

        rs = (jnp.zeros((tq, 1), F32),) * 2
        for d in range(ratio):
            rs = block(ratio * i + (ratio - 1 - d), rs, True)

        def block_matters(c):
            j, r0, r1 = c
            return (j >= 0) & (jnp.max(jnp.maximum(r0, r1)) > -EXP_UNDERFLOW)

        def walk_left(c):
            j, r0, r1 = c
            r0, r1 = block(j, (r0, r1), False)
            return j - 1, r0, r1

        j, r0, r1 = lax.while_loop(block_matters, walk_left, (ratio * i - 1, rs[0], rs[1]))
        y_ref[...] = jnp.where(first, acc_ref[0], acc_ref[1])
        rtot_ref[...] = jnp.where(first, r0, r1)
        first_ref[...] = jnp.ones(first_ref.shape, F32) * (j + 1).astype(F32)

    return pl.pallas_call(
        body,
        grid=(N_PAIRS, sp["nq"]),
        in_specs=[sp["q"], sp["k_rows"], sp["k_rows"], pl.BlockSpec((2 * tk, tk), lambda p, i: (0, 0))],
        out_specs=[sp["q"], sp["q"], sp["first"]],
        out_shape=[sp["wide"], sp["wide"], sp["first_out"]],
        scratch_shapes=[sp["acc"]],
        compiler_params=_cparams(56),
        name="sb_fwd",
    )(qs, k3, v3, upper)


def _sb_bwd(qs, k3, v3, dy, rtot, first_block):
    sp = _att_specs(qs.shape[0])
    tq, tk = ATT_Q_BLOCK, ATT_BLOCK
    ratio = tq // tk
    lower_in = _tri_stacked("row_le_col")
    lower = _tri(tk, "row_lt_col")

    def body(q_ref, k_ref, v_ref, dy_ref, rtot_ref, first_ref, li_ref, l_ref, dq_ref, dkt_ref, dvt_ref, acc_ref):
        i = pl.program_id(1)

        @pl.when(i == 0)
        def _():
            dkt_ref[...] = jnp.zeros_like(dkt_ref)
            dvt_ref[...] = jnp.zeros_like(dvt_ref)

        first = _head_masks(tq)
        qh = _heads_of(q_ref, first)
        dyb = dy_ref[...].astype(BF16)
        zero = jnp.zeros_like(dyb)
        dyh = [jnp.where(first, dyb, zero), jnp.where(first, zero, dyb)]
        rtoth = _head_cols(rtot_ref)
        li = li_ref[...]
        lo_tri = l_ref[...]
        acc_ref[...] = jnp.zeros_like(acc_ref)

        def block(j, carry, diag):
            mask = _causal_mask(tq, tk, i * tq - j * tk, strict=True) if diag else None
            k, v = k_ref[j], v_ref[j]
            logits = [_dot_nt(qh[n], k) for n in range(2)]
            das = [_dot_nt(dyh[n], v) for n in range(2)]
            terms = [_sb_terms(z, mask, diag) for z in logits]
            upto = [_dot_split2_stacked(l1m, li) for _, l1m in terms]
            des, weights = [], []
            for n in range(2):
                a = jnp.exp(terms[n][0] + ((rtoth[n] - carry[2 * n]) - upto[n]))
                if diag:
                    a = jnp.where(mask, a, 0.0)
                des.append(a * das[n])
                weights.append(a.astype(BF16))
            lefts = [_dot(de.astype(BF16), lo_tri) for de in des]
            dzbs, out = [], []
            for n in range(2):
                beta = jnp.exp(terms[n][0])
                dz = des[n] - (des[n] + (carry[2 * n + 1] + lefts[n])) * beta
                if diag:
                    dz = jnp.where(mask, dz, 0.0)
                dzbs.append(dz.astype(BF16))
                out += [carry[2 * n] + jnp.sum(terms[n][1], axis=-1, keepdims=True),
                        carry[2 * n + 1] + jnp.sum(des[n], axis=-1, keepdims=True)]
            for n in range(2):
                acc_ref[n] += _dot(dzbs[n], k)
            dkt_ref[0, j] += _dot_tn(qh[0], dzbs[0]) + _dot_tn(qh[1], dzbs[1])
            dvt_ref[0, j] += _dot_tn(dyh[0], weights[0]) + _dot_tn(dyh[1], weights[1])
            return tuple(out)

        carry = (jnp.zeros((tq, 1), F32),) * 4
        carry = lax.fori_loop(_first_block(first_ref, ratio * i), ratio * i, lambda j, c: block(j, c, False), carry)
        for d in range(ratio):
            carry = block(ratio * i + d, carry, True)
        dq_ref[...] = jnp.where(first, acc_ref[0], acc_ref[1])

    return pl.pallas_call(
        body,
        grid=(N_PAIRS, sp["nq"]),
        in_specs=[sp["q"], sp["k_rows"], sp["k_rows"], sp["q"], sp["q"], sp["first"],
                  pl.BlockSpec((2 * tk, tk), lambda p, i: (0, 0)), pl.BlockSpec((tk, tk), lambda p, i: (0, 0))],
        out_specs=[sp["q"], sp["k_t"], sp["k_t"]],
        out_shape=[sp["wide"], sp["k_t_out"], sp["k_t_out"]],
        scratch_shapes=[sp["acc"]],
        compiler_params=_cparams(56),
        name="sb_bwd",
    )(qs, k3, v3, dy, rtot, first_block, lower_in, lower)


def _merge_fwd(x1, gates, y_fox, y_sb, w_bf, w_bs, w_out, tm=512):
    s_len = x1.shape[0]

    def body(x_ref, g_ref, yf_ref, ys_ref, wbf_ref, wbs_ref, wo_ref, o_ref):
        g = g_ref[...]
        of = _dot(yf_ref[...].astype(BF16), wbf_ref[...])
        os_ = _dot(ys_ref[...].astype(BF16), wbs_ref[...])
        merged = _sigmoid(g[:, 0:D_MODEL]) * of + _sigmoid(g[:, D_MODEL:]) * os_
        o_ref[...] = x_ref[...] + _dot(merged.astype(BF16), wo_ref[...])

    row = lambda i: (i, 0)
    full = lambda i: (0, 0)
    return pl.pallas_call(
        body,
        grid=(s_len // tm,),
        in_specs=[
            pl.BlockSpec((tm, D_MODEL), row),
            pl.BlockSpec((tm, 2 * D_MODEL), row),
            pl.BlockSpec((tm, ATT_W), row),
            pl.BlockSpec((tm, ATT_W), row),
            pl.BlockSpec((ATT_W, D_MODEL), full),
            pl.BlockSpec((ATT_W, D_MODEL), full),
            pl.BlockSpec((D_MODEL, D_MODEL), full),
        ],
        out_specs=pl.BlockSpec((tm, D_MODEL), row),
        out_shape=jax.ShapeDtypeStruct((s_len, D_MODEL), F32),
        compiler_params=_cparams(48),
        name="merge_fwd",
    )(x1, gates, y_fox, y_sb, w_bf, w_bs, w_out)


def _merge_bwd(dx2, gates, y_fox, y_sb, w_bf, w_bs, w_out, tm=512):
    s_len = dx2.shape[0]

    def body(d_ref, g_ref, yf_ref, ys_ref, wbf_ref, wbs_ref, wo_ref,
             dyf_ref, dys_ref, dg_ref, dof_ref, dos_ref, m_ref, dbf_ref):
        dbf = d_ref[...].astype(BF16)
        dbf_ref[...] = dbf
        dm = _dot_nt(dbf, wo_ref[...])
        g = g_ref[...]
        of = _dot(yf_ref[...].astype(BF16), wbf_ref[...])
        os_ = _dot(ys_ref[...].astype(BF16), wbs_ref[...])
        sf = _sigmoid(g[:, 0:D_MODEL])
        ss = _sigmoid(g[:, D_MODEL:])
        m_ref[...] = (sf * of + ss * os_).astype(BF16)
        d_of = (dm * sf).astype(BF16)
        d_os = (dm * ss).astype(BF16)
        dof_ref[...] = d_of
        dos_ref[...] = d_os
        dg_ref[:, 0:D_MODEL] = (dm * of * sf * (1.0 - sf)).astype(BF16)
        dg_ref[:, D_MODEL:] = (dm * os_ * ss * (1.0 - ss)).astype(BF16)
        dyf_ref[...] = _dot_nt(d_of, wbf_ref[...])
        dys_ref[...] = _dot_nt(d_os, wbs_ref[...])

    row = lambda i: (i, 0)
    full = lambda i: (0, 0)
    return pl.pallas_call(
        body,
        grid=(s_len // tm,),
        in_specs=[
            pl.BlockSpec((tm, D_MODEL), row),
            pl.BlockSpec((tm, 2 * D_MODEL), row),
            pl.BlockSpec((tm, ATT_W), row),
            pl.BlockSpec((tm, ATT_W), row),
            pl.BlockSpec((ATT_W, D_MODEL), full),
            pl.BlockSpec((ATT_W, D_MODEL), full),
            pl.BlockSpec((D_MODEL, D_MODEL), full),
        ],
        out_specs=[
            pl.BlockSpec((tm, ATT_W), row), pl.BlockSpec((tm, ATT_W), row),
            pl.BlockSpec((tm, 2 * D_MODEL), row),
            pl.BlockSpec((tm, D_MODEL), row), pl.BlockSpec((tm, D_MODEL), row),
            pl.BlockSpec((tm, D_MODEL), row), pl.BlockSpec((tm, D_MODEL), row),
        ],
        out_shape=[
            jax.ShapeDtypeStruct((s_len, ATT_W), F32), jax.ShapeDtypeStruct((s_len, ATT_W), F32),
            jax.ShapeDtypeStruct((s_len, 2 * D_MODEL), BF16),
            jax.ShapeDtypeStruct((s_len, D_MODEL), BF16), jax.ShapeDtypeStruct((s_len, D_MODEL), BF16),
            jax.ShapeDtypeStruct((s_len, D_MODEL), BF16), jax.ShapeDtypeStruct((s_len, D_MODEL), BF16),
        ],
        compiler_params=_cparams(56),
        name="merge_bwd",
    )(dx2, gates, y_fox, y_sb, w_bf, w_bs, w_out)


def _ple_loss(x3, p, g, w_pg, w_pp, target, tm=512):
    s_len = x3.shape[0]
    inv_d = 1.0 / D_MODEL

    def body(x_ref, p_ref, g_ref, wpg_ref, wpp_ref, t_ref,
             dx_ref, du_ref, dt_ref, hn_ref, dg_ref, loss_ref):
        @pl.when(pl.program_id(0) == 0)
        def _():
            dg_ref[...] = jnp.zeros_like(dg_ref)
            loss_ref[...] = jnp.zeros_like(loss_ref)

        x = x_ref[...]
        xn, r = _rms(x)
        gain = g_ref[...]
        hn = (xn * gain).astype(BF16)
        hn_ref[...] = hn
        sg = _sigmoid(_dot(hn, wpg_ref[...]))
        t = _dot(p_ref[...].astype(BF16), wpp_ref[...])
        err = x + sg * t - t_ref[...]
        sq = jnp.sum(_colsum(err * err), axis=-1, keepdims=True)
        loss_ref[...] += (0.5 * inv_d) * sq
        dy = err * inv_d
        du = (dy * t * sg * (1.0 - sg)).astype(BF16)
        du_ref[...] = du
        dt_ref[...] = (dy * sg).astype(BF16)
        dh = _dot_nt(du, wpg_ref[...])
        dx_ref[...] = dy + _rms_bwd(dh, xn, r, gain)
        dg_ref[0:1, :] += _colsum(dh * xn)

    row = lambda i: (i, 0)
    full = lambda i: (0, 0)
    bf = jax.ShapeDtypeStruct((s_len, D_MODEL), BF16)
    return pl.pallas_call(
        body,
        grid=(s_len // tm,),
        in_specs=[
            pl.BlockSpec((tm, D_MODEL), row),
            pl.BlockSpec((tm, PLE_DIM), row),
            pl.BlockSpec((1, D_MODEL), full),
            pl.BlockSpec((D_MODEL, D_MODEL), full),
            pl.BlockSpec((PLE_DIM, D_MODEL), full),
            pl.BlockSpec((tm, D_MODEL), row),
        ],
        out_specs=[
            pl.BlockSpec((tm, D_MODEL), row), pl.BlockSpec((tm, D_MODEL), row),
            pl.BlockSpec((tm, D_MODEL), row), pl.BlockSpec((tm, D_MODEL), row),
            pl.BlockSpec((8, D_MODEL), full), pl.BlockSpec((8, LANES), full),
        ],
        out_shape=[
            jax.ShapeDtypeStruct((s_len, D_MODEL), F32), bf, bf, bf,
            jax.ShapeDtypeStruct((8, D_MODEL), F32), jax.ShapeDtypeStruct((8, LANES), F32),
        ],
        compiler_params=_cparams(48),
        name="ple_loss",
    )(x3, p, g, w_pg, w_pp, target)


def _qknorm_bwd(fq, fk, dqs, dk, dv, qn, kn, bd, bd_t, tm=256):
    s_len = fq.shape[0]

    def body(fq_ref, fk_ref, dq_ref, dk_ref, dv_ref, qn_ref, kn_ref, bd_ref, bdt_ref,
             dz_ref, dqn_ref, dkn_ref):
        @pl.when(pl.program_id(0) == 0)
        def _():
            dqn_ref[...] = jnp.zeros_like(dqn_ref)
            dkn_ref[...] = jnp.zeros_like(dkn_ref)

        bd_m = bd_ref[...]
        bdt_m = bdt_ref[...]

        def one(x, dy, gain, dgain_ref):
            xn, rw = _head_rms(x, bd_m, bdt_m)
            dgain_ref[0:1, :] += _colsum(dy * xn)
            dxn = dy * gain
            return rw * (dxn - xn * _head_mean(dxn * xn, bd_m, bdt_m))

        dz_ref[:, 0:ATT_W] = one(fq_ref[...], dq_ref[...] * QK_SCALE, qn_ref[...], dqn_ref).astype(BF16)
        dz_ref[:, ATT_W:2 * ATT_W] = one(fk_ref[...], dk_ref[...], kn_ref[...], dkn_ref).astype(BF16)
        dz_ref[:, 2 * ATT_W:] = dv_ref[...].astype(BF16)

    row = lambda i: (i, 0)
    full = lambda i: (0, 0)
    att = pl.BlockSpec((tm, ATT_W), row)
    return pl.pallas_call(
        body,
        grid=(s_len // tm,),
        in_specs=[att, att, att, att, att,
                  pl.BlockSpec((1, ATT_W), full), pl.BlockSpec((1, ATT_W), full),
                  pl.BlockSpec((ATT_W, LANES), full), pl.BlockSpec((LANES, ATT_W), full)],
        out_specs=[pl.BlockSpec((tm, 3 * ATT_W), row), pl.BlockSpec((8, ATT_W), full), pl.BlockSpec((8, ATT_W), full)],
        out_shape=[jax.ShapeDtypeStruct((s_len, 3 * ATT_W), BF16),
                   jax.ShapeDtypeStruct((8, ATT_W), F32), jax.ShapeDtypeStruct((8, ATT_W), F32)],
        name="qknorm_bwd",
    )(fq, fk, dqs, dk, dv, qn, kn, bd, bd_t)


def _inproj_bwd(x1, dx2, g, dzf, dlogf, logf, dzs, dgates, w_fox, w_fl, w_sb, w_gates, tm=256):
    s_len = x1.shape[0]

    def body(x_ref, d_ref, g_ref, dzf_ref, dlf_ref, lf_ref, dzs_ref, dgt_ref, wf_ref, wl_ref, ws_ref, wg_ref,
             dx_ref, h_ref, dfl_ref, dg_ref, db_ref):
        @pl.when(pl.program_id(0) == 0)
        def _():
            dg_ref[...] = jnp.zeros_like(dg_ref)
            db_ref[...] = jnp.zeros_like(db_ref)

        xn, r = _rms(x_ref[...])
        gain = g_ref[...]
        h_ref[...] = (xn * gain).astype(BF16)
        lane = lax.broadcasted_iota(jnp.int32, (tm, LANES), 1)
        dfl = jnp.where(lane < N_HEADS, dlf_ref[...] * (1.0 - jnp.exp(lf_ref[...])), 0.0)
        db_ref[0:1, :] += _colsum(dfl)
        dflb = dfl.astype(BF16)
        dfl_ref[...] = dflb
        dh = (_dot_nt(dzf_ref[...], wf_ref[...]) + _dot_nt(dflb, wl_ref[...])
              + _dot_nt(dzs_ref[...], ws_ref[...]) + _dot_nt(dgt_ref[...], wg_ref[...]))
        dx_ref[...] = d_ref[...] + _rms_bwd(dh, xn, r, gain)
        dg_ref[0:1, :] += _colsum(dh * xn)

    row = lambda i: (i, 0)
    full = lambda i: (0, 0)
    return pl.pallas_call(
        body,
        grid=(s_len // tm,),
        in_specs=[
            pl.BlockSpec((tm, D_MODEL), row),
            pl.BlockSpec((tm, D_MODEL), row),
            pl.BlockSpec((1, D_MODEL), full),
            pl.BlockSpec((tm, 3 * ATT_W), row),
            pl.BlockSpec((tm, LANES), row),
            pl.BlockSpec((tm, LANES), row),
            pl.BlockSpec((tm, 3 * ATT_W), row),
            pl.BlockSpec((tm, 2 * D_MODEL), row),
            pl.BlockSpec((D_MODEL, 3 * ATT_W), full),
            pl.BlockSpec((D_MODEL, LANES), full),
            pl.BlockSpec((D_MODEL, 3 * ATT_W), full),
            pl.BlockSpec((D_MODEL, 2 * D_MODEL), full),
        ],
        out_specs=[
            pl.BlockSpec((tm, D_MODEL), row), pl.BlockSpec((tm, D_MODEL), row), pl.BlockSpec((tm, LANES), row),
            pl.BlockSpec((8, D_MODEL), full), pl.BlockSpec((8, LANES), full),
        ],
        out_shape=[
            jax.ShapeDtypeStruct((s_len, D_MODEL), F32), jax.ShapeDtypeStruct((s_len, D_MODEL), BF16),
            jax.ShapeDtypeStruct((s_len, LANES), BF16),
            jax.ShapeDtypeStruct((8, D_MODEL), F32), jax.ShapeDtypeStruct((8, LANES), F32),
        ],
        compiler_params=_cparams(56),
        name="inproj_bwd",
    )(x1, dx2, g, dzf, dlogf, logf, dzs, dgates, w_fox, w_fl, w_sb, w_gates)


def _split_w_in(w_in):
    o = 3 * ATT_W
    w_fox = w_in[:, 0:o]
    w_fl = jnp.pad(w_in[:, o:o + N_HEADS], ((0, 0), (0, LANES - N_HEADS)))
    w_sb = w_in[:, o + N_HEADS:2 * o + N_HEADS]
    w_gates = w_in[:, 2 * o + N_HEADS:]
    return w_fox, w_fl, w_sb, w_gates


def _local_grads(x, p, target, small, full):
    blk = ATT_BLOCK
    bd, bd_t = _head_sum_matrices()

    w_fox, w_fl, w_sb, w_gates = _split_w_in(full["w_in"])
    bias = jnp.pad(small["forget_bias"], ((0, 0), (0, LANES - N_HEADS)))
    qn = jnp.tile(small["q_norm"], (1, N_HEADS))
    kn = jnp.tile(small["k_norm"], (1, N_HEADS))

    x1, a1, b1 = _ffn_fwd(x, small["ffn1_norm"], full["ffn1_w_gate"], full["ffn1_w_up"], full["ffn1_w_down"])
    fq, fk, f_qs, f_k, f_v, logf, s_qs, s_k, s_v, gates = _inproj_fwd(
        x1, small["mix_norm"], w_fox, w_fl, w_sb, w_gates, bias, qn, kn, bd, bd_t)
    f_cum = _cumsum_rows(logf, reverse=False)
    f8 = f_cum[:, 0:N_HEADS]
    fw = jnp.repeat(f8, HEAD_DIM, axis=1)
    ft4 = _pair_rows_t(f8, blk)
    f_k3, f_v3 = _blocked_rows(f_k, blk), _blocked_rows(f_v, blk)
    y_fox, lse, f_first = _fox_fwd(f_qs, f_k3, f_v3, fw, ft4, _key_norm_bound(f_k))
    s_k3, s_v3 = _blocked_rows(s_k, blk), _blocked_rows(s_v, blk)
    y_sb, s_rtot, s_first = _sb_fwd(s_qs, s_k3, s_v3)
    x2 = _merge_fwd(x1, gates, y_fox, y_sb, full["w_branch_fox"], full["w_branch_sb"], full["w_out"])
    x3, a2, b2 = _ffn_fwd(x2, small["ffn2_norm"], full["ffn2_w_gate"], full["ffn2_w_up"], full["ffn2_w_down"])

    dx3, du_ple, dt_ple, hn_ple, dg_ple, loss_sum = _ple_loss(
        x3, p, small["ple_norm"], full["w_ple_gate"], full["w_ple_proj"], target)
    dx2, u2, da2, db2, h_ffn2, d3_bf, dg_ffn2 = _ffn_bwd(
        x2, dx3, small["ffn2_norm"], a2, b2, full["ffn2_w_gate"], full["ffn2_w_up"], full["ffn2_w_down"])
    dy_fox, dy_sb, dgates, d_of, d_os, merged, d2_bf = _merge_bwd(
        dx2, gates, y_fox, y_sb, full["w_branch_fox"], full["w_branch_sb"], full["w_out"])

    f_dqs, dfq_w, f_dkt4, f_dvt4, dft4 = _fox_bwd(f_qs, f_k3, f_v3, dy_fox, y_fox, lse, fw, ft4, f_first)
    s_dqs, s_dkt4, s_dvt4 = _sb_bwd(s_qs, s_k3, s_v3, dy_sb, s_rtot, s_first)

    dzf, dqn8, dkn8 = _qknorm_bwd(fq, fk, f_dqs, _unblocked_t(f_dkt4), _unblocked_t(f_dvt4), qn, kn, bd, bd_t)
    dzs = jnp.concatenate([s_dqs * QK_SCALE, _unblocked_t(s_dkt4), _unblocked_t(s_dvt4)], axis=1).astype(BF16)
    df8 = _unpair_rows_t(dft4) + dfq_w[:, ::HEAD_DIM]
    dlogf = _cumsum_rows(jnp.pad(df8, ((0, 0), (0, LANES - N_HEADS))), reverse=True)
    dx1, h_mix, dfl, dg_mix, dbias8 = _inproj_bwd(
        x1, dx2, small["mix_norm"], dzf, dlogf, logf, dzs, dgates, w_fox, w_fl, w_sb, w_gates)
    grad_x, u1, da1, db1, h_ffn1, d1_bf, dg_ffn1 = _ffn_bwd(
        x, dx1, small["ffn1_norm"], a1, b1, full["ffn1_w_gate"], full["ffn1_w_up"], full["ffn1_w_down"])

    one = lambda t: t[None]
    gw = {}
    gw["ffn1_w_gate"] = _wgrad(da1, one(h_ffn1), name="wgrad_ffn1_gate")
    gw["ffn1_w_up"] = _wgrad(db1, one(h_ffn1), name="wgrad_ffn1_up")
    gw["ffn1_w_down"] = _wgrad(u1, one(d1_bf), scale=0.5, name="wgrad_ffn1_down")
    gw["ffn2_w_gate"] = _wgrad(da2, one(h_ffn2), name="wgrad_ffn2_gate")
    gw["ffn2_w_up"] = _wgrad(db2, one(h_ffn2), name="wgrad_ffn2_up")
    gw["ffn2_w_down"] = _wgrad(u2, one(d3_bf), scale=0.5, name="wgrad_ffn2_down")
    g_fox = _wgrad(one(h_mix), one(dzf), name="wgrad_in_fox")[0]
    g_fl = _wgrad(one(h_mix), one(dfl), name="wgrad_in_forget")[0]
    g_sb = _wgrad(one(h_mix), one(dzs), name="wgrad_in_sb")[0]
    g_gt = _wgrad(one(h_mix), one(dgates), name="wgrad_in_gates")[0]
    gw["w_in"] = jnp.concatenate([g_fox, g_fl[:, 0:N_HEADS], g_sb, g_gt], axis=1)
    gw["w_branch_fox"] = _wgrad(one(y_fox), one(d_of), name="wgrad_branch_fox")[0]
    gw["w_branch_sb"] = _wgrad(one(y_sb), one(d_os), name="wgrad_branch_sb")[0]
    gw["w_out"] = _wgrad(one(merged), one(d2_bf), name="wgrad_out")[0]
    gw["w_ple_gate"] = _wgrad(one(hn_ple), one(du_ple), name="wgrad_ple_gate")[0]
    gw["w_ple_proj"] = _wgrad(one(p), one(dt_ple), name="wgrad_ple_proj")[0]

    fold = lambda t: jnp.sum(t[0:1].reshape(N_HEADS, HEAD_DIM), axis=0, keepdims=True)
    gs = {
        "ffn1_norm": dg_ffn1[0:1], "mix_norm": dg_mix[0:1], "ffn2_norm": dg_ffn2[0:1], "ple_norm": dg_ple[0:1],
        "forget_bias": dbias8[0:1, 0:N_HEADS], "q_norm": fold(dqn8), "k_norm": fold(dkn8),
    }
    return loss_sum, grad_x, gw, gs


def _position():
    return lax.axis_index("x"), lax.axis_index("y"), lax.axis_index("c")


def _other_chips(x, y):
    return [(1 - x, y), (x, 1 - y), (1 - x, 1 - y)]


ANY = pl.BlockSpec(memory_space=pl.ANY)


def _place_own_shard(w, q):
    rows, cols = w.shape
    tr = _row_block(rows, cols * 4, budget=2 * MIB)

    def body(q_ref, w_ref, o_ref):
        o_ref[0] = w_ref[...].astype(BF16)

    return pl.pallas_call(
        body,
        grid_spec=pltpu.PrefetchScalarGridSpec(
            num_scalar_prefetch=1,
            grid=(rows // tr,),
            in_specs=[pl.BlockSpec((tr, cols), lambda i, q_ref: (i, 0))],
            out_specs=pl.BlockSpec((1, tr, cols), lambda i, q_ref: (q_ref[0], i, 0)),
        ),
        out_shape=jax.ShapeDtypeStruct((N_CHIPS, rows, cols), BF16),
        name="place_own_shard",
    )(q, w)


def _allgather_weights(slots):
    n = len(slots)

    def body(*refs):
        bufs = refs[n:2 * n]
        send_sems, recv_sems = refs[2 * n:]
        x, y, c = _position()
        q = 2 * x + y
        chips = _other_chips(x, y)
        sibling = (x, y, 1 - c)

        def half(a, slot, which):
            r2 = slots[a].shape[1] // 2
            return bufs[a].at[slot, pl.ds(which * r2, r2), :]

        def copy(a, k, region, to):
            return pltpu.make_async_remote_copy(
                src_ref=region, dst_ref=region, send_sem=send_sems.at[6 * a + k], recv_sem=recv_sems.at[6 * a + k],
                device_id=to, device_id_type=MESH)

        sent = []
        for a in range(n):
            for k, (tx, ty) in enumerate(chips):
                cp = copy(a, k, half(a, q, c), (tx, ty, c))
                cp.start()
                sent.append(cp)
        for a in range(n):
            for k, (tx, ty) in enumerate(chips):
                landed = half(a, 2 * tx + ty, c)
                copy(a, k, landed, (tx, ty, c)).wait_recv()
                fwd = copy(a, 3 + k, landed, sibling)
                fwd.start()
                sent.append(fwd)
        for a in range(n):
            for k, (tx, ty) in enumerate(chips):
                copy(a, 3 + k, half(a, 2 * tx + ty, 1 - c), sibling).wait_recv()
        for cp in sent:
            cp.wait_send()

    return pl.pallas_call(
        body,
        in_specs=[ANY] * n,
        out_specs=[ANY] * n,
        out_shape=[jax.ShapeDtypeStruct(s.shape, s.dtype) for s in slots],
        input_output_aliases={a: a for a in range(n)},
        scratch_shapes=[pltpu.SemaphoreType.DMA((6 * n,)), pltpu.SemaphoreType.DMA((6 * n,))],
        name="allgather_weights",
    )(*slots)


def _exchange_pair_halves(grads):
    n = len(grads)

    def body(*refs):
        ins, outs = refs[0:n], refs[n:2 * n]
        send_sems, recv_sems = refs[2 * n:]
        x, y, c = _position()
        copies = []
        for a in range(n):
            r2 = grads[a].shape[1] // 2
            cp = pltpu.make_async_remote_copy(
                src_ref=ins[a].at[:, pl.ds((1 - c) * r2, r2), :], dst_ref=outs[a],
                send_sem=send_sems.at[a], recv_sem=recv_sems.at[a], device_id=(x, y, 1 - c), device_id_type=MESH)
            cp.start()
            copies.append(cp)
        for cp in copies:
            cp.wait()

    return pl.pallas_call(
        body,
        in_specs=[ANY] * n,
        out_specs=[ANY] * n,
        out_shape=[jax.ShapeDtypeStruct((N_CHIPS, g.shape[1] // 2, g.shape[2]), g.dtype) for g in grads],
        scratch_shapes=[pltpu.SemaphoreType.DMA((n,)), pltpu.SemaphoreType.DMA((n,))],
        name="rs_pair_exchange",
    )(*grads)


def _scatter_to_owner_chips(pairs):
    n = len(pairs)

    def body(*refs):
        ins, outs = refs[0:n], refs[n:2 * n]
        send_sems, recv_sems, local_sems = refs[2 * n:]
        x, y, c = _position()
        q = 2 * x + y
        chips = _other_chips(x, y)
        started = []
        for a in range(n):
            mine = pltpu.make_async_copy(ins[a].at[q], outs[a].at[q], local_sems.at[a])
            mine.start()
            started.append(mine)
            for k, (tx, ty) in enumerate(chips):
                cp = pltpu.make_async_remote_copy(
                    src_ref=ins[a].at[2 * tx + ty], dst_ref=outs[a].at[q],
                    send_sem=send_sems.at[3 * a + k], recv_sem=recv_sems.at[3 * a + k],
                    device_id=(tx, ty, c), device_id_type=MESH)
                cp.start()
                started.append(cp)
        for cp in started:
            cp.wait()

    return pl.pallas_call(
        body,
        in_specs=[ANY] * n,
        out_specs=[ANY] * n,
        out_shape=[jax.ShapeDtypeStruct(p.shape, p.dtype) for p in pairs],
        scratch_shapes=[pltpu.SemaphoreType.DMA((3 * n,)), pltpu.SemaphoreType.DMA((3 * n,)),
                        pltpu.SemaphoreType.DMA((n,))],
        name="rs_scatter",
    )(*pairs)


def _join_halves(shards):
    n = len(shards)

    def body(*refs):
        bufs = refs[n:2 * n]
        send_sems, recv_sems = refs[2 * n:]
        x, y, c = _position()
        started = []
        for a in range(n):
            r2 = shards[a].shape[0] // 2
            mine = bufs[a].at[pl.ds(c * r2, r2), :]
            cp = pltpu.make_async_remote_copy(
                src_ref=mine, dst_ref=mine, send_sem=send_sems.at[a], recv_sem=recv_sems.at[a],
                device_id=(x, y, 1 - c), device_id_type=MESH)
            cp.start()
            started.append(cp)
        for cp in started:
            cp.wait()

    return pl.pallas_call(
        body,
        in_specs=[ANY] * n,
        out_specs=[ANY] * n,
        out_shape=[jax.ShapeDtypeStruct(t.shape, t.dtype) for t in shards],
        input_output_aliases={a: a for a in range(n)},
        scratch_shapes=[pltpu.SemaphoreType.DMA((n,)), pltpu.SemaphoreType.DMA((n,))],
        name="rs_join_halves",
    )(*shards)


def _add_pair(g, got, c):
    _, r2, cols = got.shape

    def body(c_ref, g_ref, got_ref, o_ref):
        o_ref[...] = (g_ref[...].astype(F32) + got_ref[...].astype(F32)).astype(BF16)

    spec = pl.BlockSpec((1, r2, cols), lambda s, c_ref: (s, 0, 0))
    return pl.pallas_call(
        body,
        grid_spec=pltpu.PrefetchScalarGridSpec(
            num_scalar_prefetch=1,
            grid=(N_CHIPS,),
            in_specs=[pl.BlockSpec((1, r2, cols), lambda s, c_ref: (s, c_ref[0], 0)), spec],
            out_specs=spec,
        ),
        out_shape=jax.ShapeDtypeStruct(got.shape, BF16),
        name="rs_add_pair",
    )(c, g, got)


def _add_chips(parts, c):
    _, r2, cols = parts.shape

    def body(c_ref, p0, p1, p2, p3, o_ref):
        o_ref[...] = ((p0[0].astype(F32) + p1[0].astype(F32)) + p2[0].astype(F32)) + p3[0].astype(F32)

    specs = [pl.BlockSpec((1, r2, cols), functools.partial(lambda i, c_ref, s: (s, 0, 0), s=s))
             for s in range(N_CHIPS)]
    return pl.pallas_call(
        body,
        grid_spec=pltpu.PrefetchScalarGridSpec(
            num_scalar_prefetch=1,
            grid=(1,),
            in_specs=specs,
            out_specs=pl.BlockSpec((r2, cols), lambda i, c_ref: (c_ref[0], 0)),
        ),
        out_shape=jax.ShapeDtypeStruct((2 * r2, cols), F32),
        name="rs_add_chips",
    )(c, parts, parts, parts, parts)


def _allreduce_small(part):
    shape = part.shape

    def body(in_ref, out_ref, gather_ref, send_sems, recv_sems):
        x, y, c = _position()
        me = 4 * x + 2 * y + c
        relations = [(a, b, d) for a in (0, 1) for b in (0, 1) for d in (0, 1)][1:]
        flip = lambda v, f: 1 - v if f else v
        copies = []
        for k, (a, b, d) in enumerate(relations):
            cp = pltpu.make_async_remote_copy(
                src_ref=in_ref, dst_ref=gather_ref.at[me], send_sem=send_sems.at[k], recv_sem=recv_sems.at[k],
                device_id=(flip(x, a), flip(y, b), flip(c, d)), device_id_type=MESH)
            cp.start()
            copies.append(cp)
        gather_ref[me] = in_ref[...]
        for cp in copies:
            cp.wait()
        total = gather_ref[0]
        for dev in range(1, 8):
            total = total + gather_ref[dev]
        out_ref[...] = total

    vmem = pl.BlockSpec(memory_space=pltpu.VMEM)
    return pl.pallas_call(
        body,
        in_specs=[vmem],
        out_specs=vmem,
        out_shape=jax.ShapeDtypeStruct(shape, F32),
        scratch_shapes=[pltpu.VMEM((8,) + shape, F32), pltpu.SemaphoreType.DMA((7,)), pltpu.SemaphoreType.DMA((7,))],
        name="allreduce_small",
    )(part)


def _adamw(w, g, m, v):
    rows, cols = w.shape
    tr = _row_block(rows, cols * 4, budget=MIB)
    c1 = 1.0 / (1.0 - ADAM_B1 ** ADAM_STEP)
    c2 = 1.0 / (1.0 - ADAM_B2 ** ADAM_STEP)

    def body(w_ref, g_ref, m_ref, v_ref, d_ref, nm_ref, nv_ref):
        g_ = g_ref[...]
        nm = ADAM_B1 * m_ref[...] + (1.0 - ADAM_B1) * g_
        nv = ADAM_B2 * v_ref[...] + (1.0 - ADAM_B2) * (g_ * g_)
        nm_ref[...] = nm
        nv_ref[...] = nv
        d_ref[...] = -ADAM_LR * ((nm * c1) / (jnp.sqrt(nv * c2) + ADAM_EPS) + ADAM_WD * w_ref[...])

    spec = pl.BlockSpec((tr, cols), lambda i: (i, 0))
    out = jax.ShapeDtypeStruct((rows, cols), F32)
    return pl.pallas_call(
        body,
        grid=(rows // tr,),
        in_specs=[spec] * 4,
        out_specs=[spec] * 3,
        out_shape=[out] * 3,
        name="adamw",
    )(w, g, m, v)


BIG = ["ffn1_w_gate", "ffn1_w_up", "ffn1_w_down", "w_in", "w_branch_fox", "w_branch_sb", "w_out",
       "ffn2_w_gate", "ffn2_w_up", "ffn2_w_down", "w_ple_gate", "w_ple_proj"]
SMALL = ["ffn1_norm", "mix_norm", "ffn2_norm", "ple_norm", "forget_bias", "q_norm", "k_norm"]
COLUMN_SHARDED = ["ffn1_w_gate", "ffn1_w_up", "w_in", "w_branch_fox", "w_branch_sb",
                  "ffn2_w_gate", "ffn2_w_up", "w_ple_proj"]
KEPT_AS_SHARDS = ["ffn1_w_gate", "ffn1_w_up", "ffn1_w_down", "ffn2_w_gate", "ffn2_w_up", "ffn2_w_down"]
WORKED_TRANSPOSED = ["ffn1_w_gate", "ffn1_w_up", "ffn2_w_gate", "ffn2_w_up"]
ORDER = ["ffn1_norm", "ffn1_w_gate", "ffn1_w_up", "ffn1_w_down", "mix_norm", "w_in", "forget_bias", "q_norm",
         "k_norm", "w_branch_fox", "w_branch_sb", "w_out", "ffn2_norm", "ffn2_w_gate", "ffn2_w_up",
         "ffn2_w_down", "ple_norm", "w_ple_gate", "w_ple_proj"]
SMALL_ROWS = {"ffn1_norm": 0, "mix_norm": 1, "ffn2_norm": 2, "ple_norm": 3}
SMALL_COLS = {"forget_bias": (0, N_HEADS), "q_norm": (N_HEADS, HEAD_DIM), "k_norm": (N_HEADS + HEAD_DIM, HEAD_DIM)}
LOSS_ROW = 5


def _stored(name, a):
    return jnp.swapaxes(a[0], 0, 1) if name in WORKED_TRANSPOSED else a[0]


def _returned(name, a):
    return (jnp.swapaxes(a, 0, 1) if name in WORKED_TRANSPOSED else a)[None]


def _whole(name, gathered):
    if name in COLUMN_SHARDED:
        return jnp.concatenate([gathered[s] for s in range(N_CHIPS)], axis=1)
    return gathered.reshape(-1, gathered.shape[-1])


def _as_shards(name, whole):
    if name in COLUMN_SHARDED:
        k, n = whole.shape
        return whole.reshape(k, N_CHIPS, n // N_CHIPS).transpose(1, 0, 2)
    return whole.reshape(N_CHIPS, whole.shape[0] // N_CHIPS, whole.shape[1])


def _pack_small(values, extra=None):
    rows = [values[k] for k in ("ffn1_norm", "mix_norm", "ffn2_norm", "ple_norm")]
    tail = jnp.concatenate([values["forget_bias"], values["q_norm"], values["k_norm"]], axis=1)
    rows.append(jnp.pad(tail, ((0, 0), (0, D_MODEL - tail.shape[1]))))
    packed = jnp.concatenate(rows + [jnp.zeros((3, D_MODEL), F32)], axis=0)
    if extra is not None:
        packed = packed.at[LOSS_ROW, 0].set(extra)
    return packed


def _unpack_small(packed):
    out = {k: packed[r:r + 1] for k, r in SMALL_ROWS.items()}
    for k, (start, size) in SMALL_COLS.items():
        out[k] = packed[4:5, start:start + size]
    return out


def kernel(x, p, ffn1_norm, ffn1_w_gate, ffn1_w_up, ffn1_w_down, mix_norm, w_in, forget_bias, q_norm, k_norm, w_branch_fox, w_branch_sb, w_out, ffn2_norm, ffn2_w_gate, ffn2_w_up, ffn2_w_down, ple_norm, w_ple_gate, w_ple_proj, loss_target, m_ffn1_norm, m_ffn1_w_gate, m_ffn1_w_up, m_ffn1_w_down, m_mix_norm, m_w_in, m_forget_bias, m_q_norm, m_k_norm, m_w_branch_fox, m_w_branch_sb, m_w_out, m_ffn2_norm, m_ffn2_w_gate, m_ffn2_w_up, m_ffn2_w_down, m_ple_norm, m_w_ple_gate, m_w_ple_proj, v_ffn1_norm, v_ffn1_w_gate, v_ffn1_w_up, v_ffn1_w_down, v_mix_norm, v_w_in, v_forget_bias, v_q_norm, v_k_norm, v_w_branch_fox, v_w_branch_sb, v_w_out, v_ffn2_norm, v_ffn2_w_gate, v_ffn2_w_up, v_ffn2_w_down, v_ple_norm, v_w_ple_gate, v_w_ple_proj):
    args = dict(locals())
    weights = {k: args[k] for k in ORDER}
    moments_m = {k: args["m_" + k] for k in ORDER}
    moments_v = {k: args["v_" + k] for k in ORDER}

    c_idx = lax.axis_index("c").astype(jnp.int32).reshape(1)
    q_idx = (2 * lax.axis_index("x") + lax.axis_index("y")).astype(jnp.int32).reshape(1)
    gathered = _allgather_weights([_place_own_shard(_stored(k, weights[k]), q_idx) for k in BIG])
    full = {}
    for k, gth in zip(BIG, gathered):
        full[k] = gth if k in KEPT_AS_SHARDS else _whole(k, gth)
    small = {k: weights[k] for k in SMALL}

    loss_sum, grad_x, gw, gs = _local_grads(x[0], p[0, 0], loss_target[0], small, full)

    slots = [gw[k] if k in KEPT_AS_SHARDS else _as_shards(k, gw[k]) for k in BIG]
    from_core = _exchange_pair_halves(slots)
    pairs = [_add_pair(g, got, c_idx) for g, got in zip(slots, from_core)]
    parts = _scatter_to_owner_chips(pairs)
    grads_big = dict(zip(BIG, _join_halves([_add_chips(t, c_idx) for t in parts])))
    reduced = _allreduce_small(_pack_small(gs, extra=loss_sum[0, 0]))
    grads_small = _unpack_small(reduced)
    loss = reduced[LOSS_ROW, 0]

    grads, deltas, new_m, new_v = {}, {}, {}, {}
    for k in BIG:
        d, nm, nv = _adamw(_stored(k, weights[k]), grads_big[k], _stored(k, moments_m[k]), _stored(k, moments_v[k]))
        grads[k], deltas[k], new_m[k], new_v[k] = (_returned(k, t) for t in (grads_big[k], d, nm, nv))
    d_s, nm_s, nv_s = _adamw(_pack_small({k: weights[k] for k in SMALL}), reduced,
                             _pack_small({k: moments_m[k] for k in SMALL}),
                             _pack_small({k: moments_v[k] for k in SMALL}))
    for k in SMALL:
        grads[k] = grads_small[k]
    for name, packed in (("d", d_s), ("m", nm_s), ("v", nv_s)):
        target = {"d": deltas, "m": new_m, "v": new_v}[name]
        target.update(_unpack_small(packed))

    return (loss, grad_x[None], *[grads[k] for k in ORDER], *[deltas[k] for k in ORDER],
            *[new_m[k] for k in ORDER], *[new_v[k] for k in ORDER])
```

```python
import functools

import jax
import jax.numpy as jnp
from jax import lax
from jax.experimental import pallas as pl
from jax.experimental.pallas import tpu as pltpu

F32 = jnp.float32
BF16 = jnp.bfloat16

D_MODEL = 1024
D_FF = 2816
N_CHIPS = 4
FF_SHARD = D_FF // N_CHIPS
HEAD_DIM = 64
N_HEADS = 8
ATT_W = N_HEADS * HEAD_DIM
PAIR_W = 2 * HEAD_DIM
N_PAIRS = N_HEADS // 2
PLE_DIM = 256
IN_WIDTH = 3 * ATT_W + N_HEADS + 3 * ATT_W + 2 * D_MODEL
EPS = 1e-6
QK_SCALE = HEAD_DIM ** -0.5
LANES = 128
ATT_BLOCK = 256
ATT_Q_BLOCK = 512
NEG_BIG = -1e30
EXP_UNDERFLOW = 110.0

ADAM_LR = 0.001
ADAM_B1 = 0.9
ADAM_B2 = 0.999
ADAM_EPS = 1e-08
ADAM_WD = 0.01
ADAM_STEP = 10

MESH = pl.DeviceIdType.MESH
MIB = 1024 * 1024


def _cparams(vmem_mib=48):
    return pltpu.CompilerParams(vmem_limit_bytes=vmem_mib * MIB)


def _dot(a, b):
    return jnp.dot(a, b, preferred_element_type=F32)


def _dot_tn(a, b):
    return lax.dot_general(a, b, (((0,), (0,)), ((), ())), preferred_element_type=F32)


def _dot_nt(a, b):
    return lax.dot_general(a, b, (((1,), (1,)), ((), ())), preferred_element_type=F32)


def _sigmoid(x):
    return 1.0 / (1.0 + jnp.exp(-x))


def _split2(x):
    hi = x.astype(BF16)
    lo = (x - hi.astype(F32)).astype(BF16)
    return hi, lo


def _dot_split2(x, m):
    hi, lo = _split2(x)
    return _dot(hi, m) + _dot(lo, m)


def _split3(x):
    hi = x.astype(BF16)
    rest = x - hi.astype(F32)
    mid = rest.astype(BF16)
    lo = (rest - mid.astype(F32)).astype(BF16)
    return hi, mid, lo


def _rms(x):
    r = lax.rsqrt(jnp.mean(x * x, axis=-1, keepdims=True) + EPS)
    return x * r, r


def _rms_bwd(dh, xn, r, g):
    dxn = dh * g
    return r * (dxn - xn * jnp.mean(dxn * xn, axis=-1, keepdims=True))


def _colsum(x):
    return jnp.sum(x, axis=0, keepdims=True)


def _row_block(rows, row_bytes, budget):
    best = None
    for t in range(8, rows + 1, 8):
        if rows % t == 0 and t * row_bytes <= budget:
            best = t
    return best if best is not None else rows


def _ffn_fwd(x, g, wg, wu, wd, gather=(), tm=512):
    s_len = x.shape[0]
    n = len(gather)
    steps = s_len // tm

    def body(x_ref, g_ref, wg_ref, wu_ref, wd_ref, *rest):
        o_ref, a_ref, b_ref = rest[n:n + 3]
        h_s, acc_s = rest[2 * n + 3:2 * n + 5]
        i = pl.program_id(0)
        j = pl.program_id(1)
        if n:
            start, finish = _gather_steps(rest[n + 3:2 * n + 3], *rest[2 * n + 5:])
            pl.when((i == 0) & (j == 0))(start)

        @pl.when(j == 0)
        def _():
            xn, _ = _rms(x_ref[...])
            h_s[...] = (xn * g_ref[...]).astype(BF16)
            acc_s[...] = jnp.zeros_like(acc_s)

        h = h_s[...]
        a = _dot_nt(h, wg_ref[0])
        b = _dot_nt(h, wu_ref[0])
        a_ref[0] = a.astype(BF16)
        b_ref[0] = b.astype(BF16)
        u = (a * _sigmoid(a) * b).astype(BF16)
        acc_s[...] += _dot(u, wd_ref[0])

        @pl.when(j == N_CHIPS - 1)
        def _():
            o_ref[...] = x_ref[...] + 0.5 * acc_s[...]

        if n:
            pl.when((i == steps - 1) & (j == N_CHIPS - 1))(finish)

    return pl.pallas_call(
        body,
        grid=(steps, N_CHIPS),
        in_specs=[
            pl.BlockSpec((tm, D_MODEL), lambda i, j: (i, 0)),
            pl.BlockSpec((1, D_MODEL), lambda i, j: (0, 0)),
            pl.BlockSpec((1, FF_SHARD, D_MODEL), lambda i, j: (j, 0, 0)),
            pl.BlockSpec((1, FF_SHARD, D_MODEL), lambda i, j: (j, 0, 0)),
            pl.BlockSpec((1, FF_SHARD, D_MODEL), lambda i, j: (j, 0, 0)),
        ] + [ANY] * n,
        out_specs=[pl.BlockSpec((tm, D_MODEL), lambda i, j: (i, 0)),
                   pl.BlockSpec((1, tm, FF_SHARD), lambda i, j: (j, i, 0)),
                   pl.BlockSpec((1, tm, FF_SHARD), lambda i, j: (j, i, 0))] + [ANY] * n,
        out_shape=[jax.ShapeDtypeStruct((s_len, D_MODEL), F32),
                   jax.ShapeDtypeStruct((N_CHIPS, s_len, FF_SHARD), BF16),
                   jax.ShapeDtypeStruct((N_CHIPS, s_len, FF_SHARD), BF16)]
        + [jax.ShapeDtypeStruct(s.shape, s.dtype) for s in gather],
        input_output_aliases={5 + a: 3 + a for a in range(n)},
        scratch_shapes=[pltpu.VMEM((tm, D_MODEL), BF16), pltpu.VMEM((tm, D_MODEL), F32)]
        + (_gather_semaphores(n) if n else []),
        compiler_params=_cparams(48),
        name="ffn_fwd_gathering" if n else "ffn_fwd",
    )(x, g, wg, wu, wd, *gather)


def _ffn_bwd(x, d, g, a_pre, b_pre, wg, wu, wd, tm=512):
    s_len = x.shape[0]
    nb = s_len // tm

    def body(x_ref, d_ref, g_ref, a_ref, b_ref, wg_ref, wu_ref, wd_ref,
             dx_ref, u_ref, da_ref, db_ref, h_ref, dbf_ref, dg_ref, dbf_s, dh_s):
        i = pl.program_id(0)
        j = pl.program_id(1)

        @pl.when(j == 0)
        def _():
            xn, _ = _rms(x_ref[...])
            h_ref[...] = (xn * g_ref[...]).astype(BF16)
            dbf = d_ref[...].astype(BF16)
            dbf_s[...] = dbf
            dbf_ref[...] = dbf
            dh_s[...] = jnp.zeros_like(dh_s)

        @pl.when((i == 0) & (j == 0))
        def _():
            dg_ref[...] = jnp.zeros_like(dg_ref)

        a = a_ref[0].astype(F32)
        b = b_ref[0].astype(F32)
        du = 0.5 * _dot_nt(dbf_s[...], wd_ref[0])
        s = _sigmoid(a)
        silu = a * s
        da = (du * b * (s * (1.0 + a * (1.0 - s)))).astype(BF16)
        db = (du * silu).astype(BF16)
        u_ref[0] = (silu * b).astype(BF16)
        da_ref[0] = da
        db_ref[0] = db
        dh_s[...] += _dot(da, wg_ref[0]) + _dot(db, wu_ref[0])

        @pl.when(j == N_CHIPS - 1)
        def _():
            xn, r = _rms(x_ref[...])
            dh = dh_s[...]
            dx_ref[...] = d_ref[...] + _rms_bwd(dh, xn, r, g_ref[...])
            dg_ref[0:1, :] += _colsum(dh * xn)

    row = lambda i, j: (i, 0)
    shard = lambda i, j: (j, 0, 0)
    act = lambda i, j: (j, i, 0)
    return pl.pallas_call(
        body,
        grid=(nb, N_CHIPS),
        in_specs=[
            pl.BlockSpec((tm, D_MODEL), row),
            pl.BlockSpec((tm, D_MODEL), row),
            pl.BlockSpec((1, D_MODEL), lambda i, j: (0, 0)),
            pl.BlockSpec((1, tm, FF_SHARD), act),
            pl.BlockSpec((1, tm, FF_SHARD), act),
            pl.BlockSpec((1, FF_SHARD, D_MODEL), shard),
            pl.BlockSpec((1, FF_SHARD, D_MODEL), shard),
            pl.BlockSpec((1, FF_SHARD, D_MODEL), shard),
        ],
        out_specs=[
            pl.BlockSpec((tm, D_MODEL), row),
            pl.BlockSpec((1, tm, FF_SHARD), act),
            pl.BlockSpec((1, tm, FF_SHARD), act),
            pl.BlockSpec((1, tm, FF_SHARD), act),
            pl.BlockSpec((tm, D_MODEL), row),
            pl.BlockSpec((tm, D_MODEL), row),
            pl.BlockSpec((8, D_MODEL), lambda i, j: (0, 0)),
        ],
        out_shape=[
            jax.ShapeDtypeStruct((s_len, D_MODEL), F32),
            jax.ShapeDtypeStruct((N_CHIPS, s_len, FF_SHARD), BF16),
            jax.ShapeDtypeStruct((N_CHIPS, s_len, FF_SHARD), BF16),
            jax.ShapeDtypeStruct((N_CHIPS, s_len, FF_SHARD), BF16),
            jax.ShapeDtypeStruct((s_len, D_MODEL), BF16),
            jax.ShapeDtypeStruct((s_len, D_MODEL), BF16),
            jax.ShapeDtypeStruct((8, D_MODEL), F32),
        ],
        scratch_shapes=[
            pltpu.VMEM((tm, D_MODEL), BF16),
            pltpu.VMEM((tm, D_MODEL), F32),
        ],
        compiler_params=_cparams(56),
        name="ffn_bwd",
    )(x, d, g, a_pre, b_pre, wg, wu, wd)


def _wgrad(a, b, scale=1.0, name="wgrad"):
    na, s_len, k_dim = a.shape
    nb, _, n_dim = b.shape
    n = max(na, nb)
    ts = min(s_len, 1024)
    steps = s_len // ts

    def body(a_ref, b_ref, o_ref, acc_s):
        s = pl.program_id(1)

        @pl.when(s == 0)
        def _():
            acc_s[...] = jnp.zeros_like(acc_s)

        acc_s[...] += _dot_tn(a_ref[0].astype(BF16), b_ref[0].astype(BF16))

        @pl.when(s == steps - 1)
        def _():
            o_ref[0] = (acc_s[...] * scale).astype(BF16)

    a_map = (lambda m, s: (m, s, 0)) if na > 1 else (lambda m, s: (0, s, 0))
    b_map = (lambda m, s: (m, s, 0)) if nb > 1 else (lambda m, s: (0, s, 0))
    return pl.pallas_call(
        body,
        grid=(n, steps),
        in_specs=[pl.BlockSpec((1, ts, k_dim), a_map), pl.BlockSpec((1, ts, n_dim), b_map)],
        out_specs=pl.BlockSpec((1, k_dim, n_dim), lambda m, s: (m, 0, 0)),
        out_shape=jax.ShapeDtypeStruct((n, k_dim, n_dim), BF16),
        scratch_shapes=[pltpu.VMEM((k_dim, n_dim), F32)],
        compiler_params=_cparams(56),
        name=name,
    )(a, b)


def _head_sum_matrices():
    lane = lax.broadcasted_iota(jnp.int32, (ATT_W, LANES), 0) // HEAD_DIM
    col = lax.broadcasted_iota(jnp.int32, (ATT_W, LANES), 1)
    bd = (lane == col).astype(BF16)
    return bd, bd.T


def _head_mean(t, bd, bd_t):
    per_head = _dot_split2(t, bd) * (1.0 / HEAD_DIM)
    return _dot_split2(per_head, bd_t)


def _head_rms(x, bd, bd_t):
    per_head = _dot_split2(x * x, bd) * (1.0 / HEAD_DIM)
    r = lax.rsqrt(per_head + EPS)
    rw = _dot_split2(r, bd_t)
    return x * rw, rw


def _log_sigmoid(z):
    return jnp.minimum(z, 0.0) - jnp.log(1.0 + jnp.exp(-jnp.abs(z)))


def _inproj_fwd(x1, g, w_fox, w_fl, w_sb, w_gates, bias, qn, kn, bd, bd_t, tm=256):
    s_len = x1.shape[0]

    def body(x_ref, g_ref, wf_ref, wl_ref, ws_ref, wg_ref, bias_ref, qn_ref, kn_ref, bd_ref, bdt_ref,
             fq_ref, fk_ref, qs_ref, kf_ref, vf_ref, logf_ref, sq_ref, sk_ref, sv_ref, gates_ref):
        xn, _ = _rms(x_ref[...])
        h = (xn * g_ref[...]).astype(BF16)
        zf = _dot(h, wf_ref[...])
        fq = zf[:, 0:ATT_W]
        fk = zf[:, ATT_W:2 * ATT_W]
        fq_ref[...] = fq
        fk_ref[...] = fk
        bd_m = bd_ref[...]
        bdt_m = bdt_ref[...]
        fqn, _ = _head_rms(fq, bd_m, bdt_m)
        fkn, _ = _head_rms(fk, bd_m, bdt_m)
        qs_ref[...] = (fqn * qn_ref[...]).astype(BF16) * QK_SCALE
        kf_ref[...] = (fkn * kn_ref[...]).astype(BF16)
        vf_ref[...] = zf[:, 2 * ATT_W:3 * ATT_W].astype(BF16)
        logf_ref[...] = _log_sigmoid(_dot(h, wl_ref[...]) + bias_ref[...])
        zs = _dot(h, ws_ref[...])
        sq_ref[...] = zs[:, 0:ATT_W].astype(BF16) * QK_SCALE
        sk_ref[...] = zs[:, ATT_W:2 * ATT_W].astype(BF16)
        sv_ref[...] = zs[:, 2 * ATT_W:3 * ATT_W].astype(BF16)
        gates_ref[...] = _dot(h, wg_ref[...])

    row = lambda i: (i, 0)
    full = lambda i: (0, 0)
    att = lambda dt: jax.ShapeDtypeStruct((s_len, ATT_W), dt)
    return pl.pallas_call(
        body,
        grid=(s_len // tm,),
        in_specs=[
            pl.BlockSpec((tm, D_MODEL), row),
            pl.BlockSpec((1, D_MODEL), full),
            pl.BlockSpec((D_MODEL, 3 * ATT_W), full),
            pl.BlockSpec((D_MODEL, LANES), full),
            pl.BlockSpec((D_MODEL, 3 * ATT_W), full),
            pl.BlockSpec((D_MODEL, 2 * D_MODEL), full),
            pl.BlockSpec((1, LANES), full),
            pl.BlockSpec((1, ATT_W), full),
            pl.BlockSpec((1, ATT_W), full),
            pl.BlockSpec((ATT_W, LANES), full),
            pl.BlockSpec((LANES, ATT_W), full),
        ],
        out_specs=[
            pl.BlockSpec((tm, ATT_W), row), pl.BlockSpec((tm, ATT_W), row),
            pl.BlockSpec((tm, ATT_W), row), pl.BlockSpec((tm, ATT_W), row), pl.BlockSpec((tm, ATT_W), row),
            pl.BlockSpec((tm, LANES), row),
            pl.BlockSpec((tm, ATT_W), row), pl.BlockSpec((tm, ATT_W), row), pl.BlockSpec((tm, ATT_W), row),
            pl.BlockSpec((tm, 2 * D_MODEL), row),
        ],
        out_shape=[
            att(F32), att(F32), att(BF16), att(BF16), att(BF16),
            jax.ShapeDtypeStruct((s_len, LANES), F32),
            att(BF16), att(BF16), att(BF16),
            jax.ShapeDtypeStruct((s_len, 2 * D_MODEL), F32),
        ],
        compiler_params=_cparams(56),
        name="inproj_fwd",
    )(x1, g, w_fox, w_fl, w_sb, w_gates, bias, qn, kn, bd, bd_t)


def _tri(n, kind):
    r = lax.broadcasted_iota(jnp.int32, (n, n), 0)
    c = lax.broadcasted_iota(jnp.int32, (n, n), 1)
    m = {"row_ge_col": r >= c, "row_le_col": r <= c, "row_gt_col": r > c, "row_lt_col": r < c}[kind]
    return m.astype(BF16)


def _cumsum_rows(x, reverse, tm=256):
    s_len = x.shape[0]
    nb = s_len // tm
    tri = _tri(tm, "row_le_col" if reverse else "row_ge_col")
    edge = 0 if reverse else tm - 1

    def body(x_ref, tri_ref, o_ref, carry_s):
        @pl.when(pl.program_id(0) == 0)
        def _():
            carry_s[...] = jnp.zeros_like(carry_s)

        hi, mid, lo = _split3(x_ref[...])
        t = tri_ref[...]
        y = _dot(t, hi) + _dot(t, mid) + _dot(t, lo) + carry_s[...]
        o_ref[...] = y
        carry_s[...] = y[edge:edge + 1, :]

    order = (lambda i: (nb - 1 - i, 0)) if reverse else (lambda i: (i, 0))
    return pl.pallas_call(
        body,
        grid=(nb,),
        in_specs=[pl.BlockSpec((tm, LANES), order), pl.BlockSpec((tm, tm), lambda i: (0, 0))],
        out_specs=pl.BlockSpec((tm, LANES), order),
        out_shape=jax.ShapeDtypeStruct((s_len, LANES), F32),
        scratch_shapes=[pltpu.VMEM((1, LANES), F32)],
        name="cumsum_rev" if reverse else "cumsum_fwd",
    )(x, tri)


def _unblocked_t(t4):
    _, nb, _, blk = t4.shape
    return t4.transpose(1, 3, 0, 2).reshape(nb * blk, ATT_W)


def _blocked_rows(t, blk):
    return t.reshape(t.shape[0] // blk, blk, t.shape[1])


def _pair_rows_t(f8, blk):
    nb = f8.shape[0] // blk
    t = f8.reshape(nb, blk, N_PAIRS, 2).transpose(2, 0, 3, 1)
    return jnp.pad(t, ((0, 0), (0, 0), (0, 6), (0, 0)))


def _unpair_rows_t(t4):
    _, nb, _, blk = t4.shape
    return t4[:, :, 0:2, :].transpose(1, 3, 0, 2).reshape(nb * blk, N_HEADS)


def _head_masks(tq):
    lane = lax.broadcasted_iota(jnp.int32, (tq, PAIR_W), 1)
    return lane < HEAD_DIM


def _causal_mask(tq, tk, offset, strict):
    d = lax.broadcasted_iota(jnp.int32, (tq, tk), 1) - lax.broadcasted_iota(jnp.int32, (tq, tk), 0)
    return (d < offset) if strict else (d <= offset)


def _heads_of(ref, first):
    t = ref[...]
    zero = jnp.zeros_like(t)
    return [jnp.where(first, t, zero), jnp.where(first, zero, t)]


def _head_cols(ref):
    t = ref[...]
    return [t[:, 0:1], t[:, HEAD_DIM:HEAD_DIM + 1]]


def _att_specs(s_len):
    tq, tk = ATT_Q_BLOCK, ATT_BLOCK
    nq, nk = s_len // tq, s_len // tk
    return dict(
        nq=nq,
        q=pl.BlockSpec((tq, PAIR_W), lambda p, i: (i, p)),
        k_t=pl.BlockSpec((1, nk, PAIR_W, tk), lambda p, i: (p, 0, 0, 0)),
        k_rows=pl.BlockSpec((nk, tk, PAIR_W), lambda p, i: (0, 0, p)),
        f_t=pl.BlockSpec((1, nk, 8, tk), lambda p, i: (p, 0, 0, 0)),
        first=pl.BlockSpec((1, 1, 8, LANES), lambda p, i: (p, i, 0, 0)),
        wide=jax.ShapeDtypeStruct((s_len, ATT_W), F32),
        k_t_out=jax.ShapeDtypeStruct((N_PAIRS, nk, PAIR_W, tk), F32),
        f_t_out=jax.ShapeDtypeStruct((N_PAIRS, nk, 8, tk), F32),
        first_out=jax.ShapeDtypeStruct((N_PAIRS, nq, 8, LANES), F32),
        acc=pltpu.VMEM((2, tq, PAIR_W), F32),
    )


def _first_block(first_ref, limit):
    return jnp.clip(jnp.max(first_ref[0, 0]).astype(jnp.int32), 0, limit)


def _key_norm_bound(k):
    sq = jnp.sum(jnp.square(k.astype(F32)).reshape(k.shape[0], N_HEADS, HEAD_DIM), axis=-1)
    bound = jnp.sqrt(jnp.max(sq, axis=0)).reshape(N_PAIRS, 2)
    return jnp.broadcast_to(jnp.pad(bound, ((0, 0), (0, 6)))[:, :, None], (N_PAIRS, 8, LANES))


def _fox_fwd(qs, k3, v3, fw, ft4, kmax):
    sp = _att_specs(qs.shape[0])
    tq, tk = ATT_Q_BLOCK, ATT_BLOCK
    ratio = tq // tk

    def body(q_ref, k_ref, v_ref, fw_ref, ft_ref, kmax_ref, y_ref, lse_ref, first_ref, acc_ref, max_ref, sum_ref):
        i = pl.program_id(1)
        first = _head_masks(tq)
        qh = _heads_of(q_ref, first)
        fqh = _head_cols(fw_ref)
        acc_ref[...] = jnp.zeros_like(acc_ref)
        sum_ref[...] = jnp.zeros_like(sum_ref)
        max_ref[...] = jnp.full(max_ref.shape, NEG_BIG, F32)
        reach = []
        for n in range(2):
            qf = qh[n].astype(F32)
            reach.append(jnp.sqrt(jnp.sum(qf * qf, axis=-1, keepdims=True)) * kmax_ref[0, n:n + 1, 0:1] + fqh[n])

        def logits(j, shift, diag):
            k, fk = k_ref[j], ft_ref[0, j]
            raw = [_dot_nt(qh[n], k) for n in range(2)]
            out = []
            for n in range(2):
                s = raw[n] + (shift[n] - fk[n:n + 1, :])
                if diag:
                    s = jnp.where(_causal_mask(tq, tk, i * tq - j * tk, strict=False), s, NEG_BIG)
                out.append(s)
            return out

        def max_pass(j, diag):
            ss = logits(j, fqh, diag)
            for n in range(2):
                max_ref[n] = jnp.maximum(max_ref[n], ss[n])

        def sum_pass(j, shift, diag):
            ps = [jnp.exp(s) for s in logits(j, shift, diag)]
            v = v_ref[j]
            for n in range(2):
                sum_ref[n] += ps[n]
            for n in range(2):
                acc_ref[n] += _dot(ps[n].astype(BF16), v)

        for d in range(ratio):
            max_pass(ratio * i + d, True)

        def block_matters(j):
            gap = []
            for n in range(2):
                m_run = jnp.max(max_ref[n], axis=-1, keepdims=True)
                f_end = ft_ref[0, jnp.maximum(j, 0)][n:n + 1, tk - 1:tk]
                gap.append(jnp.max(reach[n] - m_run) - jnp.max(f_end))
            return (j >= 0) & (jnp.maximum(gap[0], gap[1]) > -EXP_UNDERFLOW)

        def walk_left(j):
            max_pass(j, False)
            return j - 1

        j_first = lax.while_loop(block_matters, walk_left, ratio * i - 1) + 1
        m = [jnp.max(max_ref[n], axis=-1, keepdims=True) for n in range(2)]
        shift = [fqh[n] - m[n] for n in range(2)]

        def one(j, c):
            sum_pass(j, shift, False)
            return c
        lax.fori_loop(j_first, ratio * i, one, 0)
        for d in range(ratio):
            sum_pass(ratio * i + d, shift, True)
        l = [jnp.sum(sum_ref[n], axis=-1, keepdims=True) for n in range(2)]
        y_ref[...] = jnp.where(first, acc_ref[0] / l[0], acc_ref[1] / l[1])
        lse_ref[...] = jnp.where(first, m[0] + jnp.log(l[0]), m[1] + jnp.log(l[1]))
        first_ref[...] = jnp.ones(first_ref.shape, F32) * j_first.astype(F32)

    tile = pltpu.VMEM((2, tq, tk), F32)
    return pl.pallas_call(
        body,
        grid=(N_PAIRS, sp["nq"]),
        in_specs=[sp["q"], sp["k_rows"], sp["k_rows"], sp["q"], sp["f_t"],
                  pl.BlockSpec((1, 8, LANES), lambda p, i: (p, 0, 0))],
        out_specs=[sp["q"], sp["q"], sp["first"]],
        out_shape=[sp["wide"], sp["wide"], sp["first_out"]],
        scratch_shapes=[sp["acc"], tile, tile],
        compiler_params=_cparams(56),
        name="fox_fwd",
    )(qs, k3, v3, fw, ft4, kmax)


def _fox_bwd(qs, k3, v3, dy, y, lse, fw, ft4, first_block):
    sp = _att_specs(qs.shape[0])
    tq, tk = ATT_Q_BLOCK, ATT_BLOCK
    ratio = tq // tk

    def body(q_ref, k_ref, v_ref, dy_ref, y_ref, lse_ref, fw_ref, ft_ref, first_ref,
             dq_ref, dfq_ref, dkt_ref, dvt_ref, dft_ref, acc_ref):
        i = pl.program_id(1)

        @pl.when(i == 0)
        def _():
            dkt_ref[...] = jnp.zeros_like(dkt_ref)
            dvt_ref[...] = jnp.zeros_like(dvt_ref)
            dft_ref[...] = jnp.zeros_like(dft_ref)

        first = _head_masks(tq)
        qh = _heads_of(q_ref, first)
        dyv = dy_ref[...]
        dyb = dyv.astype(BF16)
        zero = jnp.zeros_like(dyb)
        dyh = [jnp.where(first, dyb, zero), jnp.where(first, zero, dyb)]
        prod = dyv * y_ref[...]
        zf = jnp.zeros_like(prod)
        delta = [jnp.sum(jnp.where(first, prod, zf), axis=-1, keepdims=True),
                 jnp.sum(jnp.where(first, zf, prod), axis=-1, keepdims=True)]
        fqh = _head_cols(fw_ref)
        lseh = _head_cols(lse_ref)
        shift = [fqh[n] - lseh[n] for n in range(2)]
        acc_ref[...] = jnp.zeros_like(acc_ref)

        def block(j, rows, diag):
            mask = _causal_mask(tq, tk, i * tq - j * tk, strict=False) if diag else None
            k, v, fk = k_ref[j], v_ref[j], ft_ref[0, j]
            logits = [_dot_nt(qh[n], k) for n in range(2)]
            dps = [_dot_nt(dyh[n], v) for n in range(2)]
            pbs, dsbs, out = [], [], []
            for n in range(2):
                p = jnp.exp(logits[n] + (shift[n] - fk[n:n + 1, :]))
                if diag:
                    p = jnp.where(mask, p, 0.0)
                ds = p * (dps[n] - delta[n])
                pbs.append(p.astype(BF16))
                dsbs.append(ds.astype(BF16))
                out.append(rows[n] + jnp.sum(ds, axis=-1, keepdims=True))
                dft_ref[0, j, n:n + 1, :] -= _colsum(ds)
            for n in range(2):
                acc_ref[n] += _dot(dsbs[n], k)
            dkt_ref[0, j] += _dot_tn(qh[0], dsbs[0]) + _dot_tn(qh[1], dsbs[1])
            dvt_ref[0, j] += _dot_tn(dyh[0], pbs[0]) + _dot_tn(dyh[1], pbs[1])
            return tuple(out)

        rows = (jnp.zeros((tq, 1), F32),) * 2
        rows = lax.fori_loop(_first_block(first_ref, ratio * i), ratio * i, lambda j, c: block(j, c, False), rows)
        for d in range(ratio):
            rows = block(ratio * i + d, rows, True)
        dq_ref[...] = jnp.where(first, acc_ref[0], acc_ref[1])
        dfq_ref[...] = jnp.where(first, rows[0], rows[1])

    return pl.pallas_call(
        body,
        grid=(N_PAIRS, sp["nq"]),
        in_specs=[sp["q"], sp["k_rows"], sp["k_rows"], sp["q"], sp["q"], sp["q"], sp["q"], sp["f_t"], sp["first"]],
        out_specs=[sp["q"], sp["q"], sp["k_t"], sp["k_t"], sp["f_t"]],
        out_shape=[sp["wide"], sp["wide"], sp["k_t_out"], sp["k_t_out"], sp["f_t_out"]],
        scratch_shapes=[sp["acc"]],
        compiler_params=_cparams(56),
        name="fox_bwd",
    )(qs, k3, v3, dy, y, lse, fw, ft4, first_block)


SIGN_BIT = 0x80000000


def _sb_terms(z, mask, diag):
    neg_abs = pltpu.bitcast(pltpu.bitcast(z, jnp.uint32) | jnp.uint32(SIGN_BIT), F32)
    lb = jnp.minimum(z, 0.0) - jnp.log(1.0 + jnp.exp(neg_abs))
    l1m = lb - z
    if diag:
        l1m = jnp.where(mask, l1m, 0.0)
    return lb, l1m


def _dot_split2_stacked(x, m2):
    hi, lo = _split2(x)
    return _dot(jnp.concatenate([hi, lo], axis=1), m2)


def _tri_stacked(kind):
    t = _tri(ATT_BLOCK, kind)
    return jnp.concatenate([t, t], axis=0)


def _sb_fwd(qs, k3, v3):
    sp = _att_specs(qs.shape[0])
    tq, tk = ATT_Q_BLOCK, ATT_BLOCK
    ratio = tq // tk
    upper = _tri_stacked("row_gt_col")

    def body(q_ref, k_ref, v_ref, u_ref, y_ref, rtot_ref, first_ref, acc_ref):
        i = pl.program_id(1)
        first = _head_masks(tq)
        qh = _heads_of(q_ref, first)
        u = u_ref[...]
        acc_ref[...] = jnp.zeros_like(acc_ref)

        def block(j, rs, diag):
            mask = _causal_mask(tq, tk, i * tq - j * tk, strict=True) if diag else None
            k, v = k_ref[j], v_ref[j]
            logits = [_dot_nt(qh[n], k) for n in range(2)]
            terms = [_sb_terms(z, mask, diag) for z in logits]
            right = [_dot_split2_stacked(l1m, u) for _, l1m in terms]
            weights = []
            for n in range(2):
                a = jnp.exp(terms[n][0] + right[n] + rs[n])
                if diag:
                    a = jnp.where(mask, a, 0.0)
                weights.append(a.astype(BF16))
            for n in range(2):
                acc_ref[n] += _dot(weights[n], v)
            return tuple(rs[n] + jnp.sum(terms[n][1], axis=-1, keepdims=True) for n in range(2))

        rs = (jnp.zeros((tq, 1), F32),) * 2
        for d in range(ratio):
            rs = block(ratio * i + (ratio - 1 - d), rs, True)

        def block_matters(c):
            j, r0, r1 = c
            return (j >= 0) & (jnp.max(jnp.maximum(r0, r1)) > -EXP_UNDERFLOW)

        def walk_left(c):
            j, r0, r1 = c
            r0, r1 = block(j, (r0, r1), False)
            return j - 1, r0, r1

        j, r0, r1 = lax.while_loop(block_matters, walk_left, (ratio * i - 1, rs[0], rs[1]))
        y_ref[...] = jnp.where(first, acc_ref[0], acc_ref[1])
        rtot_ref[...] = jnp.where(first, r0, r1)
        first_ref[...] = jnp.ones(first_ref.shape, F32) * (j + 1).astype(F32)

    return pl.pallas_call(
        body,
        grid=(N_PAIRS, sp["nq"]),
        in_specs=[sp["q"], sp["k_rows"], sp["k_rows"], pl.BlockSpec((2 * tk, tk), lambda p, i: (0, 0))],
        out_specs=[sp["q"], sp["q"], sp["first"]],
        out_shape=[sp["wide"], sp["wide"], sp["first_out"]],
        scratch_shapes=[sp["acc"]],
        compiler_params=_cparams(56),
        name="sb_fwd",
    )(qs, k3, v3, upper)


def _sb_bwd(qs, k3, v3, dy, rtot, first_block):
    sp = _att_specs(qs.shape[0])
    tq, tk = ATT_Q_BLOCK, ATT_BLOCK
    ratio = tq // tk
    lower_in = _tri_stacked("row_le_col")
    lower = _tri(tk, "row_lt_col")

    def body(q_ref, k_ref, v_ref, dy_ref, rtot_ref, first_ref, li_ref, l_ref, dq_ref, dkt_ref, dvt_ref, acc_ref):
        i = pl.program_id(1)

        @pl.when(i == 0)
        def _():
            dkt_ref[...] = jnp.zeros_like(dkt_ref)
            dvt_ref[...] = jnp.zeros_like(dvt_ref)

        first = _head_masks(tq)
        qh = _heads_of(q_ref, first)
        dyb = dy_ref[...].astype(BF16)
        zero = jnp.zeros_like(dyb)
        dyh = [jnp.where(first, dyb, zero), jnp.where(first, zero, dyb)]
        rtoth = _head_cols(rtot_ref)
        li = li_ref[...]
        lo_tri = l_ref[...]
        acc_ref[...] = jnp.zeros_like(acc_ref)

        def block(j, carry, diag):
            mask = _causal_mask(tq, tk, i * tq - j * tk, strict=True) if diag else None
            k, v = k_ref[j], v_ref[j]
            logits = [_dot_nt(qh[n], k) for n in range(2)]
            das = [_dot_nt(dyh[n], v) for n in range(2)]
            terms = [_sb_terms(z, mask, diag) for z in logits]
            upto = [_dot_split2_stacked(l1m, li) for _, l1m in terms]
            des, weights = [], []
            for n in range(2):
                a = jnp.exp(terms[n][0] + ((rtoth[n] - carry[2 * n]) - upto[n]))
                if diag:
                    a = jnp.where(mask, a, 0.0)
                des.append(a * das[n])
                weights.append(a.astype(BF16))
            lefts = [_dot(de.astype(BF16), lo_tri) for de in des]
            dzbs, out = [], []
            for n in range(2):
                beta = jnp.exp(terms[n][0])
                dz = des[n] - (des[n] + (carry[2 * n + 1] + lefts[n])) * beta
                if diag:
                    dz = jnp.where(mask, dz, 0.0)
                dzbs.append(dz.astype(BF16))
                out += [carry[2 * n] + jnp.sum(terms[n][1], axis=-1, keepdims=True),
                        carry[2 * n + 1] + jnp.sum(des[n], axis=-1, keepdims=True)]
            for n in range(2):
                acc_ref[n] += _dot(dzbs[n], k)
            dkt_ref[0, j] += _dot_tn(qh[0], dzbs[0]) + _dot_tn(qh[1], dzbs[1])
            dvt_ref[0, j] += _dot_tn(dyh[0], weights[0]) + _dot_tn(dyh[1], weights[1])
            return tuple(out)

        carry = (jnp.zeros((tq, 1), F32),) * 4
        carry = lax.fori_loop(_first_block(first_ref, ratio * i), ratio * i, lambda j, c: block(j, c, False), carry)
        for d in range(ratio):
            carry = block(ratio * i + d, carry, True)
        dq_ref[...] = jnp.where(first, acc_ref[0], acc_ref[1])

    return pl.pallas_call(
        body,
        grid=(N_PAIRS, sp["nq"]),
        in_specs=[sp["q"], sp["k_rows"], sp["k_rows"], sp["q"], sp["q"], sp["first"],
                  pl.BlockSpec((2 * tk, tk), lambda p, i: (0, 0)), pl.BlockSpec((tk, tk), lambda p, i: (0, 0))],
        out_specs=[sp["q"], sp["k_t"], sp["k_t"]],
        out_shape=[sp["wide"], sp["k_t_out"], sp["k_t_out"]],
        scratch_shapes=[sp["acc"]],
        compiler_params=_cparams(56),
        name="sb_bwd",
    )(qs, k3, v3, dy, rtot, first_block, lower_in, lower)


def _merge_fwd(x1, gates, y_fox, y_sb, w_bf, w_bs, w_out, tm=512):
    s_len = x1.shape[0]

    def body(x_ref, g_ref, yf_ref, ys_ref, wbf_ref, wbs_ref, wo_ref, o_ref):
        g = g_ref[...]
        of = _dot(yf_ref[...].astype(BF16), wbf_ref[...])
        os_ = _dot(ys_ref[...].astype(BF16), wbs_ref[...])
        merged = _sigmoid(g[:, 0:D_MODEL]) * of + _sigmoid(g[:, D_MODEL:]) * os_
        o_ref[...] = x_ref[...] + _dot(merged.astype(BF16), wo_ref[...])

    row = lambda i: (i, 0)
    full = lambda i: (0, 0)
    return pl.pallas_call(
        body,
        grid=(s_len // tm,),
        in_specs=[
            pl.BlockSpec((tm, D_MODEL), row),
            pl.BlockSpec((tm, 2 * D_MODEL), row),
            pl.BlockSpec((tm, ATT_W), row),
            pl.BlockSpec((tm, ATT_W), row),
            pl.BlockSpec((ATT_W, D_MODEL), full),
            pl.BlockSpec((ATT_W, D_MODEL), full),
            pl.BlockSpec((D_MODEL, D_MODEL), full),
        ],
        out_specs=pl.BlockSpec((tm, D_MODEL), row),
        out_shape=jax.ShapeDtypeStruct((s_len, D_MODEL), F32),
        compiler_params=_cparams(48),
        name="merge_fwd",
    )(x1, gates, y_fox, y_sb, w_bf, w_bs, w_out)


def _merge_bwd(dx2, gates, y_fox, y_sb, w_bf, w_bs, w_out, tm=512):
    s_len = dx2.shape[0]

    def body(d_ref, g_ref, yf_ref, ys_ref, wbf_ref, wbs_ref, wo_ref,
             dyf_ref, dys_ref, dg_ref, dof_ref, dos_ref, m_ref, dbf_ref):
        dbf = d_ref[...].astype(BF16)
        dbf_ref[...] = dbf
        dm = _dot_nt(dbf, wo_ref[...])
        g = g_ref[...]
        of = _dot(yf_ref[...].astype(BF16), wbf_ref[...])
        os_ = _dot(ys_ref[...].astype(BF16), wbs_ref[...])
        sf = _sigmoid(g[:, 0:D_MODEL])
        ss = _sigmoid(g[:, D_MODEL:])
        m_ref[...] = (sf * of + ss * os_).astype(BF16)
        d_of = (dm * sf).astype(BF16)
        d_os = (dm * ss).astype(BF16)
        dof_ref[...] = d_of
        dos_ref[...] = d_os
        dg_ref[:, 0:D_MODEL] = (dm * of * sf * (1.0 - sf)).astype(BF16)
        dg_ref[:, D_MODEL:] = (dm * os_ * ss * (1.0 - ss)).astype(BF16)
        dyf_ref[...] = _dot_nt(d_of, wbf_ref[...])
        dys_ref[...] = _dot_nt(d_os, wbs_ref[...])

    row = lambda i: (i, 0)
    full = lambda i: (0, 0)
    return pl.pallas_call(
        body,
        grid=(s_len // tm,),
        in_specs=[
            pl.BlockSpec((tm, D_MODEL), row),
            pl.BlockSpec((tm, 2 * D_MODEL), row),
            pl.BlockSpec((tm, ATT_W), row),
            pl.BlockSpec((tm, ATT_W), row),
            pl.BlockSpec((ATT_W, D_MODEL), full),
            pl.BlockSpec((ATT_W, D_MODEL), full),
            pl.BlockSpec((D_MODEL, D_MODEL), full),
        ],
        out_specs=[
            pl.BlockSpec((tm, ATT_W), row), pl.BlockSpec((tm, ATT_W), row),
            pl.BlockSpec((tm, 2 * D_MODEL), row),
            pl.BlockSpec((tm, D_MODEL), row), pl.BlockSpec((tm, D_MODEL), row),
            pl.BlockSpec((tm, D_MODEL), row), pl.BlockSpec((tm, D_MODEL), row),
        ],
        out_shape=[
            jax.ShapeDtypeStruct((s_len, ATT_W), F32), jax.ShapeDtypeStruct((s_len, ATT_W), F32),
            jax.ShapeDtypeStruct((s_len, 2 * D_MODEL), BF16),
            jax.ShapeDtypeStruct((s_len, D_MODEL), BF16), jax.ShapeDtypeStruct((s_len, D_MODEL), BF16),
            jax.ShapeDtypeStruct((s_len, D_MODEL), BF16), jax.ShapeDtypeStruct((s_len, D_MODEL), BF16),
        ],
        compiler_params=_cparams(56),
        name="merge_bwd",
    )(dx2, gates, y_fox, y_sb, w_bf, w_bs, w_out)


def _ple_loss(x3, p, g, w_pg, w_pp, target, tm=512):
    s_len = x3.shape[0]
    inv_d = 1.0 / D_MODEL

    def body(x_ref, p_ref, g_ref, wpg_ref, wpp_ref, t_ref,
             dx_ref, du_ref, dt_ref, hn_ref, dg_ref, loss_ref):
        @pl.when(pl.program_id(0) == 0)
        def _():
            dg_ref[...] = jnp.zeros_like(dg_ref)
            loss_ref[...] = jnp.zeros_like(loss_ref)

        x = x_ref[...]
        xn, r = _rms(x)
        gain = g_ref[...]
        hn = (xn * gain).astype(BF16)
        hn_ref[...] = hn
        sg = _sigmoid(_dot(hn, wpg_ref[...]))
        t = _dot(p_ref[...].astype(BF16), wpp_ref[...])
        err = x + sg * t - t_ref[...]
        sq = jnp.sum(_colsum(err * err), axis=-1, keepdims=True)
        loss_ref[...] += (0.5 * inv_d) * sq
        dy = err * inv_d
        du = (dy * t * sg * (1.0 - sg)).astype(BF16)
        du_ref[...] = du
        dt_ref[...] = (dy * sg).astype(BF16)
        dh = _dot_nt(du, wpg_ref[...])
        dx_ref[...] = dy + _rms_bwd(dh, xn, r, gain)
        dg_ref[0:1, :] += _colsum(dh * xn)

    row = lambda i: (i, 0)
    full = lambda i: (0, 0)
    bf = jax.ShapeDtypeStruct((s_len, D_MODEL), BF16)
    return pl.pallas_call(
        body,
        grid=(s_len // tm,),
        in_specs=[
            pl.BlockSpec((tm, D_MODEL), row),
            pl.BlockSpec((tm, PLE_DIM), row),
            pl.BlockSpec((1, D_MODEL), full),
            pl.BlockSpec((D_MODEL, D_MODEL), full),
            pl.BlockSpec((PLE_DIM, D_MODEL), full),
            pl.BlockSpec((tm, D_MODEL), row),
        ],
        out_specs=[
            pl.BlockSpec((tm, D_MODEL), row), pl.BlockSpec((tm, D_MODEL), row),
            pl.BlockSpec((tm, D_MODEL), row), pl.BlockSpec((tm, D_MODEL), row),
            pl.BlockSpec((8, D_MODEL), full), pl.BlockSpec((8, LANES), full),
        ],
        out_shape=[
            jax.ShapeDtypeStruct((s_len, D_MODEL), F32), bf, bf, bf,
            jax.ShapeDtypeStruct((8, D_MODEL), F32), jax.ShapeDtypeStruct((8, LANES), F32),
        ],
        compiler_params=_cparams(48),
        name="ple_loss",
    )(x3, p, g, w_pg, w_pp, target)


def _qknorm_bwd(fq, fk, dqs, dk, dv, qn, kn, bd, bd_t, tm=256):
    s_len = fq.shape[0]

    def body(fq_ref, fk_ref, dq_ref, dk_ref, dv_ref, qn_ref, kn_ref, bd_ref, bdt_ref,
             dz_ref, dqn_ref, dkn_ref):
        @pl.when(pl.program_id(0) == 0)
        def _():
            dqn_ref[...] = jnp.zeros_like(dqn_ref)
            dkn_ref[...] = jnp.zeros_like(dkn_ref)

        bd_m = bd_ref[...]
        bdt_m = bdt_ref[...]

        def one(x, dy, gain, dgain_ref):
            xn, rw = _head_rms(x, bd_m, bdt_m)
            dgain_ref[0:1, :] += _colsum(dy * xn)
            dxn = dy * gain
            return rw * (dxn - xn * _head_mean(dxn * xn, bd_m, bdt_m))

        dz_ref[:, 0:ATT_W] = one(fq_ref[...], dq_ref[...] * QK_SCALE, qn_ref[...], dqn_ref).astype(BF16)
        dz_ref[:, ATT_W:2 * ATT_W] = one(fk_ref[...], dk_ref[...], kn_ref[...], dkn_ref).astype(BF16)
        dz_ref[:, 2 * ATT_W:] = dv_ref[...].astype(BF16)

    row = lambda i: (i, 0)
    full = lambda i: (0, 0)
    att = pl.BlockSpec((tm, ATT_W), row)
    return pl.pallas_call(
        body,
        grid=(s_len // tm,),
        in_specs=[att, att, att, att, att,
                  pl.BlockSpec((1, ATT_W), full), pl.BlockSpec((1, ATT_W), full),
                  pl.BlockSpec((ATT_W, LANES), full), pl.BlockSpec((LANES, ATT_W), full)],
        out_specs=[pl.BlockSpec((tm, 3 * ATT_W), row), pl.BlockSpec((8, ATT_W), full), pl.BlockSpec((8, ATT_W), full)],
        out_shape=[jax.ShapeDtypeStruct((s_len, 3 * ATT_W), BF16),
                   jax.ShapeDtypeStruct((8, ATT_W), F32), jax.ShapeDtypeStruct((8, ATT_W), F32)],
        name="qknorm_bwd",
    )(fq, fk, dqs, dk, dv, qn, kn, bd, bd_t)


def _inproj_bwd(x1, dx2, g, dzf, dlogf, logf, dzs, dgates, w_fox, w_fl, w_sb, w_gates, tm=256):
    s_len = x1.shape[0]

    def body(x_ref, d_ref, g_ref, dzf_ref, dlf_ref, lf_ref, dzs_ref, dgt_ref, wf_ref, wl_ref, ws_ref, wg_ref,
             dx_ref, h_ref, dfl_ref, dg_ref, db_ref):
        @pl.when(pl.program_id(0) == 0)
        def _():
            dg_ref[...] = jnp.zeros_like(dg_ref)
            db_ref[...] = jnp.zeros_like(db_ref)

        xn, r = _rms(x_ref[...])
        gain = g_ref[...]
        h_ref[...] = (xn * gain).astype(BF16)
        lane = lax.broadcasted_iota(jnp.int32, (tm, LANES), 1)
        dfl = jnp.where(lane < N_HEADS, dlf_ref[...] * (1.0 - jnp.exp(lf_ref[...])), 0.0)
        db_ref[0:1, :] += _colsum(dfl)
        dflb = dfl.astype(BF16)
        dfl_ref[...] = dflb
        dh = (_dot_nt(dzf_ref[...], wf_ref[...]) + _dot_nt(dflb, wl_ref[...])
              + _dot_nt(dzs_ref[...], ws_ref[...]) + _dot_nt(dgt_ref[...], wg_ref[...]))
        dx_ref[...] = d_ref[...] + _rms_bwd(dh, xn, r, gain)
        dg_ref[0:1, :] += _colsum(dh * xn)

    row = lambda i: (i, 0)
    full = lambda i: (0, 0)
    return pl.pallas_call(
        body,
        grid=(s_len // tm,),
        in_specs=[
            pl.BlockSpec((tm, D_MODEL), row),
            pl.BlockSpec((tm, D_MODEL), row),
            pl.BlockSpec((1, D_MODEL), full),
            pl.BlockSpec((tm, 3 * ATT_W), row),
            pl.BlockSpec((tm, LANES), row),
            pl.BlockSpec((tm, LANES), row),
            pl.BlockSpec((tm, 3 * ATT_W), row),
            pl.BlockSpec((tm, 2 * D_MODEL), row),
            pl.BlockSpec((D_MODEL, 3 * ATT_W), full),
            pl.BlockSpec((D_MODEL, LANES), full),
            pl.BlockSpec((D_MODEL, 3 * ATT_W), full),
            pl.BlockSpec((D_MODEL, 2 * D_MODEL), full),
        ],
        out_specs=[
            pl.BlockSpec((tm, D_MODEL), row), pl.BlockSpec((tm, D_MODEL), row), pl.BlockSpec((tm, LANES), row),
            pl.BlockSpec((8, D_MODEL), full), pl.BlockSpec((8, LANES), full),
        ],
        out_shape=[
            jax.ShapeDtypeStruct((s_len, D_MODEL), F32), jax.ShapeDtypeStruct((s_len, D_MODEL), BF16),
            jax.ShapeDtypeStruct((s_len, LANES), BF16),
            jax.ShapeDtypeStruct((8, D_MODEL), F32), jax.ShapeDtypeStruct((8, LANES), F32),
        ],
        compiler_params=_cparams(56),
        name="inproj_bwd",
    )(x1, dx2, g, dzf, dlogf, logf, dzs, dgates, w_fox, w_fl, w_sb, w_gates)


def _split_w_in(w_in):
    o = 3 * ATT_W
    w_fox = w_in[:, 0:o]
    w_fl = jnp.pad(w_in[:, o:o + N_HEADS], ((0, 0), (0, LANES - N_HEADS)))
    w_sb = w_in[:, o + N_HEADS:2 * o + N_HEADS]
    w_gates = w_in[:, 2 * o + N_HEADS:]
    return w_fox, w_fl, w_sb, w_gates


def _local_grads(x, p, target, small, full, pending=None):
    blk = ATT_BLOCK
    bd, bd_t = _head_sum_matrices()
    full = dict(full)
    late = list(pending) if pending else []

    x1, a1, b1, *gathered = _ffn_fwd(x, small["ffn1_norm"], full["ffn1_w_gate"], full["ffn1_w_up"],
                                     full["ffn1_w_down"], gather=[pending[k] for k in late])
    for k, gth in zip(late, gathered):
        full[k] = gth if k in KEPT_AS_SHARDS else _whole(k, gth)
    w_fox, w_fl, w_sb, w_gates = _split_w_in(full["w_in"])
    bias = jnp.pad(small["forget_bias"], ((0, 0), (0, LANES - N_HEADS)))
    qn = jnp.tile(small["q_norm"], (1, N_HEADS))
    kn = jnp.tile(small["k_norm"], (1, N_HEADS))
    fq, fk, f_qs, f_k, f_v, logf, s_qs, s_k, s_v, gates = _inproj_fwd(
        x1, small["mix_norm"], w_fox, w_fl, w_sb, w_gates, bias, qn, kn, bd, bd_t)
    f_cum = _cumsum_rows(logf, reverse=False)
    f8 = f_cum[:, 0:N_HEADS]
    fw = jnp.repeat(f8, HEAD_DIM, axis=1)
    ft4 = _pair_rows_t(f8, blk)
    f_k3, f_v3 = _blocked_rows(f_k, blk), _blocked_rows(f_v, blk)
    y_fox, lse, f_first = _fox_fwd(f_qs, f_k3, f_v3, fw, ft4, _key_norm_bound(f_k))
    s_k3, s_v3 = _blocked_rows(s_k, blk), _blocked_rows(s_v, blk)
    y_sb, s_rtot, s_first = _sb_fwd(s_qs, s_k3, s_v3)
    x2 = _merge_fwd(x1, gates, y_fox, y_sb, full["w_branch_fox"], full["w_branch_sb"], full["w_out"])
    x3, a2, b2 = _ffn_fwd(x2, small["ffn2_norm"], full["ffn2_w_gate"], full["ffn2_w_up"], full["ffn2_w_down"])

    dx3, du_ple, dt_ple, hn_ple, dg_ple, loss_sum = _ple_loss(
        x3, p, small["ple_norm"], full["w_ple_gate"], full["w_ple_proj"], target)
    dx2, u2, da2, db2, h_ffn2, d3_bf, dg_ffn2 = _ffn_bwd(
        x2, dx3, small["ffn2_norm"], a2, b2, full["ffn2_w_gate"], full["ffn2_w_up"], full["ffn2_w_down"])
    dy_fox, dy_sb, dgates, d_of, d_os, merged, d2_bf = _merge_bwd(
        dx2, gates, y_fox, y_sb, full["w_branch_fox"], full["w_branch_sb"], full["w_out"])

    f_dqs, dfq_w, f_dkt4, f_dvt4, dft4 = _fox_bwd(f_qs, f_k3, f_v3, dy_fox, y_fox, lse, fw, ft4, f_first)
    s_dqs, s_dkt4, s_dvt4 = _sb_bwd(s_qs, s_k3, s_v3, dy_sb, s_rtot, s_first)

    dzf, dqn8, dkn8 = _qknorm_bwd(fq, fk, f_dqs, _unblocked_t(f_dkt4), _unblocked_t(f_dvt4), qn, kn, bd, bd_t)
    dzs = jnp.concatenate([s_dqs * QK_SCALE, _unblocked_t(s_dkt4), _unblocked_t(s_dvt4)], axis=1).astype(BF16)
    df8 = _unpair_rows_t(dft4) + dfq_w[:, ::HEAD_DIM]
    dlogf = _cumsum_rows(jnp.pad(df8, ((0, 0), (0, LANES - N_HEADS))), reverse=True)
    dx1, h_mix, dfl, dg_mix, dbias8 = _inproj_bwd(
        x1, dx2, small["mix_norm"], dzf, dlogf, logf, dzs, dgates, w_fox, w_fl, w_sb, w_gates)
    grad_x, u1, da1, db1, h_ffn1, d1_bf, dg_ffn1 = _ffn_bwd(
        x, dx1, small["ffn1_norm"], a1, b1, full["ffn1_w_gate"], full["ffn1_w_up"], full["ffn1_w_down"])

    one = lambda t: t[None]
    gw = {}
    gw["ffn1_w_gate"] = _wgrad(da1, one(h_ffn1), name="wgrad_ffn1_gate")
    gw["ffn1_w_up"] = _wgrad(db1, one(h_ffn1), name="wgrad_ffn1_up")
    gw["ffn1_w_down"] = _wgrad(u1, one(d1_bf), scale=0.5, name="wgrad_ffn1_down")
    gw["ffn2_w_gate"] = _wgrad(da2, one(h_ffn2), name="wgrad_ffn2_gate")
    gw["ffn2_w_up"] = _wgrad(db2, one(h_ffn2), name="wgrad_ffn2_up")
    gw["ffn2_w_down"] = _wgrad(u2, one(d3_bf), scale=0.5, name="wgrad_ffn2_down")
    g_fox = _wgrad(one(h_mix), one(dzf), name="wgrad_in_fox")[0]
    g_fl = _wgrad(one(h_mix), one(dfl), name="wgrad_in_forget")[0]
    g_sb = _wgrad(one(h_mix), one(dzs), name="wgrad_in_sb")[0]
    g_gt = _wgrad(one(h_mix), one(dgates), name="wgrad_in_gates")[0]
    gw["w_in"] = jnp.concatenate([g_fox, g_fl[:, 0:N_HEADS], g_sb, g_gt], axis=1)
    gw["w_branch_fox"] = _wgrad(one(y_fox), one(d_of), name="wgrad_branch_fox")[0]
    gw["w_branch_sb"] = _wgrad(one(y_sb), one(d_os), name="wgrad_branch_sb")[0]
    gw["w_out"] = _wgrad(one(merged), one(d2_bf), name="wgrad_out")[0]
    gw["w_ple_gate"] = _wgrad(one(hn_ple), one(du_ple), name="wgrad_ple_gate")[0]
    gw["w_ple_proj"] = _wgrad(one(p), one(dt_ple), name="wgrad_ple_proj")[0]

    fold = lambda t: jnp.sum(t[0:1].reshape(N_HEADS, HEAD_DIM), axis=0, keepdims=True)
    gs = {
        "ffn1_norm": dg_ffn1[0:1], "mix_norm": dg_mix[0:1], "ffn2_norm": dg_ffn2[0:1], "ple_norm": dg_ple[0:1],
        "forget_bias": dbias8[0:1, 0:N_HEADS], "q_norm": fold(dqn8), "k_norm": fold(dkn8),
    }
    return loss_sum, grad_x, gw, gs


def _position():
    return lax.axis_index("x"), lax.axis_index("y"), lax.axis_index("c")


def _other_chips(x, y):
    return [(1 - x, y), (x, 1 - y), (1 - x, 1 - y)]


ANY = pl.BlockSpec(memory_space=pl.ANY)


def _place_own_shard(w, q):
    rows, cols = w.shape
    tr = _row_block(rows, cols * 4, budget=2 * MIB)

    def body(q_ref, w_ref, o_ref):
        o_ref[0] = w_ref[...].astype(BF16)

    return pl.pallas_call(
        body,
        grid_spec=pltpu.PrefetchScalarGridSpec(
            num_scalar_prefetch=1,
            grid=(rows // tr,),
            in_specs=[pl.BlockSpec((tr, cols), lambda i, q_ref: (i, 0))],
            out_specs=pl.BlockSpec((1, tr, cols), lambda i, q_ref: (q_ref[0], i, 0)),
        ),
        out_shape=jax.ShapeDtypeStruct((N_CHIPS, rows, cols), BF16),
        name="place_own_shard",
    )(q, w)


def _gather_semaphores(n):
    return [pltpu.SemaphoreType.DMA((6 * n,)), pltpu.SemaphoreType.DMA((6 * n,))]


def _gather_steps(bufs, send_sems, recv_sems):
    n = len(bufs)
    x, y, c = _position()
    q = 2 * x + y
    chips = _other_chips(x, y)
    sibling = (x, y, 1 - c)

    def half(a, slot, which):
        r2 = bufs[a].shape[1] // 2
        return bufs[a].at[slot, pl.ds(which * r2, r2), :]

    def copy(a, k, region, to):
        return pltpu.make_async_remote_copy(
            src_ref=region, dst_ref=region, send_sem=send_sems.at[6 * a + k], recv_sem=recv_sems.at[6 * a + k],
            device_id=to, device_id_type=MESH)

    def to_chip(a, k):
        tx, ty = chips[k]
        return copy(a, k, half(a, q, c), (tx, ty, c))

    def to_sibling(a, k):
        tx, ty = chips[k]
        return copy(a, 3 + k, half(a, 2 * tx + ty, c), sibling)

    def start():
        for a in range(n):
            for k in range(3):
                to_chip(a, k).start()

    def finish():
        for a in range(n):
            for k, (tx, ty) in enumerate(chips):
                copy(a, k, half(a, 2 * tx + ty, c), (tx, ty, c)).wait_recv()
                to_sibling(a, k).start()
        for a in range(n):
            for k, (tx, ty) in enumerate(chips):
                copy(a, 3 + k, half(a, 2 * tx + ty, 1 - c), sibling).wait_recv()
        for a in range(n):
            for k in range(3):
                to_chip(a, k).wait_send()
                to_sibling(a, k).wait_send()

    return start, finish


def _allgather_weights(slots):
    n = len(slots)

    def body(*refs):
        start, finish = _gather_steps(refs[n:2 * n], *refs[2 * n:])
        start()
        finish()

    return pl.pallas_call(
        body,
        in_specs=[ANY] * n,
        out_specs=[ANY] * n,
        out_shape=[jax.ShapeDtypeStruct(s.shape, s.dtype) for s in slots],
        input_output_aliases={a: a for a in range(n)},
        scratch_shapes=_gather_semaphores(n),
        name="allgather_weights",
    )(*slots)


def _exchange_pair_halves(grads):
    n = len(grads)

    def body(*refs):
        ins, outs = refs[0:n], refs[n:2 * n]
        send_sems, recv_sems = refs[2 * n:]
        x, y, c = _position()
        copies = []
        for a in range(n):
            r2 = grads[a].shape[1] // 2
            cp = pltpu.make_async_remote_copy(
                src_ref=ins[a].at[:, pl.ds((1 - c) * r2, r2), :], dst_ref=outs[a],
                send_sem=send_sems.at[a], recv_sem=recv_sems.at[a], device_id=(x, y, 1 - c), device_id_type=MESH)
            cp.start()
            copies.append(cp)
        for cp in copies:
            cp.wait()

    return pl.pallas_call(
        body,
        in_specs=[ANY] * n,
        out_specs=[ANY] * n,
        out_shape=[jax.ShapeDtypeStruct((N_CHIPS, g.shape[1] // 2, g.shape[2]), g.dtype) for g in grads],
        scratch_shapes=[pltpu.SemaphoreType.DMA((n,)), pltpu.SemaphoreType.DMA((n,))],
        name="rs_pair_exchange",
    )(*grads)


def _scatter_to_owner_chips(pairs):
    n = len(pairs)

    def body(*refs):
        ins, outs = refs[0:n], refs[n:2 * n]
        send_sems, recv_sems, local_sems = refs[2 * n:]
        x, y, c = _position()
        q = 2 * x + y
        chips = _other_chips(x, y)
        started = []
        for a in range(n):
            mine = pltpu.make_async_copy(ins[a].at[q], outs[a].at[q], local_sems.at[a])
            mine.start()
            started.append(mine)
            for k, (tx, ty) in enumerate(chips):
                cp = pltpu.make_async_remote_copy(
                    src_ref=ins[a].at[2 * tx + ty], dst_ref=outs[a].at[q],
                    send_sem=send_sems.at[3 * a + k], recv_sem=recv_sems.at[3 * a + k],
                    device_id=(tx, ty, c), device_id_type=MESH)
                cp.start()
                started.append(cp)
        for cp in started:
            cp.wait()

    return pl.pallas_call(
        body,
        in_specs=[ANY] * n,
        out_specs=[ANY] * n,
        out_shape=[jax.ShapeDtypeStruct(p.shape, p.dtype) for p in pairs],
        scratch_shapes=[pltpu.SemaphoreType.DMA((3 * n,)), pltpu.SemaphoreType.DMA((3 * n,)),
                        pltpu.SemaphoreType.DMA((n,))],
        name="rs_scatter",
    )(*pairs)


def _join_halves(shards):
    n = len(shards)

    def body(*refs):
        bufs = refs[n:2 * n]
        send_sems, recv_sems = refs[2 * n:]
        x, y, c = _position()
        started = []
        for a in range(n):
            r2 = shards[a].shape[0] // 2
            mine = bufs[a].at[pl.ds(c * r2, r2), :]
            cp = pltpu.make_async_remote_copy(
                src_ref=mine, dst_ref=mine, send_sem=send_sems.at[a], recv_sem=recv_sems.at[a],
                device_id=(x, y, 1 - c), device_id_type=MESH)
            cp.start()
            started.append(cp)
        for cp in started:
            cp.wait()

    return pl.pallas_call(
        body,
        in_specs=[ANY] * n,
        out_specs=[ANY] * n,
        out_shape=[jax.ShapeDtypeStruct(t.shape, t.dtype) for t in shards],
        input_output_aliases={a: a for a in range(n)},
        scratch_shapes=[pltpu.SemaphoreType.DMA((n,)), pltpu.SemaphoreType.DMA((n,))],
        name="rs_join_halves",
    )(*shards)


def _add_pair(g, got, c):
    _, r2, cols = got.shape

    def body(c_ref, g_ref, got_ref, o_ref):
        o_ref[...] = (g_ref[...].astype(F32) + got_ref[...].astype(F32)).astype(BF16)

    spec = pl.BlockSpec((1, r2, cols), lambda s, c_ref: (s, 0, 0))
    return pl.pallas_call(
        body,
        grid_spec=pltpu.PrefetchScalarGridSpec(
            num_scalar_prefetch=1,
            grid=(N_CHIPS,),
            in_specs=[pl.BlockSpec((1, r2, cols), lambda s, c_ref: (s, c_ref[0], 0)), spec],
            out_specs=spec,
        ),
        out_shape=jax.ShapeDtypeStruct(got.shape, BF16),
        name="rs_add_pair",
    )(c, g, got)


def _add_chips(parts, c):
    _, r2, cols = parts.shape

    def body(c_ref, p0, p1, p2, p3, o_ref):
        o_ref[...] = ((p0[0].astype(F32) + p1[0].astype(F32)) + p2[0].astype(F32)) + p3[0].astype(F32)

    specs = [pl.BlockSpec((1, r2, cols), functools.partial(lambda i, c_ref, s: (s, 0, 0), s=s))
             for s in range(N_CHIPS)]
    return pl.pallas_call(
        body,
        grid_spec=pltpu.PrefetchScalarGridSpec(
            num_scalar_prefetch=1,
            grid=(1,),
            in_specs=specs,
            out_specs=pl.BlockSpec((r2, cols), lambda i, c_ref: (c_ref[0], 0)),
        ),
        out_shape=jax.ShapeDtypeStruct((2 * r2, cols), F32),
        name="rs_add_chips",
    )(c, parts, parts, parts, parts)


def _allreduce_small(part):
    shape = part.shape

    def body(in_ref, out_ref, gather_ref, send_sems, recv_sems):
        x, y, c = _position()
        me = 4 * x + 2 * y + c
        relations = [(a, b, d) for a in (0, 1) for b in (0, 1) for d in (0, 1)][1:]
        flip = lambda v, f: 1 - v if f else v
        copies = []
        for k, (a, b, d) in enumerate(relations):
            cp = pltpu.make_async_remote_copy(
                src_ref=in_ref, dst_ref=gather_ref.at[me], send_sem=send_sems.at[k], recv_sem=recv_sems.at[k],
                device_id=(flip(x, a), flip(y, b), flip(c, d)), device_id_type=MESH)
            cp.start()
            copies.append(cp)
        gather_ref[me] = in_ref[...]
        for cp in copies:
            cp.wait()
        total = gather_ref[0]
        for dev in range(1, 8):
            total = total + gather_ref[dev]
        out_ref[...] = total

    vmem = pl.BlockSpec(memory_space=pltpu.VMEM)
    return pl.pallas_call(
        body,
        in_specs=[vmem],
        out_specs=vmem,
        out_shape=jax.ShapeDtypeStruct(shape, F32),
        scratch_shapes=[pltpu.VMEM((8,) + shape, F32), pltpu.SemaphoreType.DMA((7,)), pltpu.SemaphoreType.DMA((7,))],
        name="allreduce_small",
    )(part)


def _adamw(w, g, m, v):
    rows, cols = w.shape
    tr = _row_block(rows, cols * 4, budget=MIB)
    c1 = 1.0 / (1.0 - ADAM_B1 ** ADAM_STEP)
    c2 = 1.0 / (1.0 - ADAM_B2 ** ADAM_STEP)

    def body(w_ref, g_ref, m_ref, v_ref, d_ref, nm_ref, nv_ref):
        g_ = g_ref[...]
        nm = ADAM_B1 * m_ref[...] + (1.0 - ADAM_B1) * g_
        nv = ADAM_B2 * v_ref[...] + (1.0 - ADAM_B2) * (g_ * g_)
        nm_ref[...] = nm
        nv_ref[...] = nv
        d_ref[...] = -ADAM_LR * ((nm * c1) / (jnp.sqrt(nv * c2) + ADAM_EPS) + ADAM_WD * w_ref[...])

    spec = pl.BlockSpec((tr, cols), lambda i: (i, 0))
    out = jax.ShapeDtypeStruct((rows, cols), F32)
    return pl.pallas_call(
        body,
        grid=(rows // tr,),
        in_specs=[spec] * 4,
        out_specs=[spec] * 3,
        out_shape=[out] * 3,
        name="adamw",
    )(w, g, m, v)


BIG = ["ffn1_w_gate", "ffn1_w_up", "ffn1_w_down", "w_in", "w_branch_fox", "w_branch_sb", "w_out",
       "ffn2_w_gate", "ffn2_w_up", "ffn2_w_down", "w_ple_gate", "w_ple_proj"]
SMALL = ["ffn1_norm", "mix_norm", "ffn2_norm", "ple_norm", "forget_bias", "q_norm", "k_norm"]
COLUMN_SHARDED = ["ffn1_w_gate", "ffn1_w_up", "w_in", "w_branch_fox", "w_branch_sb",
                  "ffn2_w_gate", "ffn2_w_up", "w_ple_proj"]
KEPT_AS_SHARDS = ["ffn1_w_gate", "ffn1_w_up", "ffn1_w_down", "ffn2_w_gate", "ffn2_w_up", "ffn2_w_down"]
WORKED_TRANSPOSED = ["ffn1_w_gate", "ffn1_w_up", "ffn2_w_gate", "ffn2_w_up"]
NEEDED_FIRST = ["ffn1_w_gate", "ffn1_w_up", "ffn1_w_down"]
ORDER = ["ffn1_norm", "ffn1_w_gate", "ffn1_w_up", "ffn1_w_down", "mix_norm", "w_in", "forget_bias", "q_norm",
         "k_norm", "w_branch_fox", "w_branch_sb", "w_out", "ffn2_norm", "ffn2_w_gate", "ffn2_w_up",
         "ffn2_w_down", "ple_norm", "w_ple_gate", "w_ple_proj"]
SMALL_ROWS = {"ffn1_norm": 0, "mix_norm": 1, "ffn2_norm": 2, "ple_norm": 3}
SMALL_COLS = {"forget_bias": (0, N_HEADS), "q_norm": (N_HEADS, HEAD_DIM), "k_norm": (N_HEADS + HEAD_DIM, HEAD_DIM)}
LOSS_ROW = 5


def _stored(name, a):
    return jnp.swapaxes(a[0], 0, 1) if name in WORKED_TRANSPOSED else a[0]


def _returned(name, a):
    return (jnp.swapaxes(a, 0, 1) if name in WORKED_TRANSPOSED else a)[None]


def _whole(name, gathered):
    if name in COLUMN_SHARDED:
        return jnp.concatenate([gathered[s] for s in range(N_CHIPS)], axis=1)
    return gathered.reshape(-1, gathered.shape[-1])


def _as_shards(name, whole):
    if name in COLUMN_SHARDED:
        k, n = whole.shape
        return whole.reshape(k, N_CHIPS, n // N_CHIPS).transpose(1, 0, 2)
    return whole.reshape(N_CHIPS, whole.shape[0] // N_CHIPS, whole.shape[1])


def _pack_small(values, extra=None):
    rows = [values[k] for k in ("ffn1_norm", "mix_norm", "ffn2_norm", "ple_norm")]
    tail = jnp.concatenate([values["forget_bias"], values["q_norm"], values["k_norm"]], axis=1)
    rows.append(jnp.pad(tail, ((0, 0), (0, D_MODEL - tail.shape[1]))))
    packed = jnp.concatenate(rows + [jnp.zeros((3, D_MODEL), F32)], axis=0)
    if extra is not None:
        packed = packed.at[LOSS_ROW, 0].set(extra)
    return packed


def _unpack_small(packed):
    out = {k: packed[r:r + 1] for k, r in SMALL_ROWS.items()}
    for k, (start, size) in SMALL_COLS.items():
        out[k] = packed[4:5, start:start + size]
    return out


def kernel(x, p, ffn1_norm, ffn1_w_gate, ffn1_w_up, ffn1_w_down, mix_norm, w_in, forget_bias, q_norm, k_norm, w_branch_fox, w_branch_sb, w_out, ffn2_norm, ffn2_w_gate, ffn2_w_up, ffn2_w_down, ple_norm, w_ple_gate, w_ple_proj, loss_target, m_ffn1_norm, m_ffn1_w_gate, m_ffn1_w_up, m_ffn1_w_down, m_mix_norm, m_w_in, m_forget_bias, m_q_norm, m_k_norm, m_w_branch_fox, m_w_branch_sb, m_w_out, m_ffn2_norm, m_ffn2_w_gate, m_ffn2_w_up, m_ffn2_w_down, m_ple_norm, m_w_ple_gate, m_w_ple_proj, v_ffn1_norm, v_ffn1_w_gate, v_ffn1_w_up, v_ffn1_w_down, v_mix_norm, v_w_in, v_forget_bias, v_q_norm, v_k_norm, v_w_branch_fox, v_w_branch_sb, v_w_out, v_ffn2_norm, v_ffn2_w_gate, v_ffn2_w_up, v_ffn2_w_down, v_ple_norm, v_w_ple_gate, v_w_ple_proj):
    args = dict(locals())
    weights = {k: args[k] for k in ORDER}
    moments_m = {k: args["m_" + k] for k in ORDER}
    moments_v = {k: args["v_" + k] for k in ORDER}

    c_idx = lax.axis_index("c").astype(jnp.int32).reshape(1)
    q_idx = (2 * lax.axis_index("x") + lax.axis_index("y")).astype(jnp.int32).reshape(1)
    own = {k: _place_own_shard(_stored(k, weights[k]), q_idx) for k in BIG}
    full = dict(zip(NEEDED_FIRST, _allgather_weights([own[k] for k in NEEDED_FIRST])))
    pending = {k: own[k] for k in BIG if k not in NEEDED_FIRST}
    small = {k: weights[k] for k in SMALL}

    loss_sum, grad_x, gw, gs = _local_grads(x[0], p[0, 0], loss_target[0], small, full, pending)

    slots = [gw[k] if k in KEPT_AS_SHARDS else _as_shards(k, gw[k]) for k in BIG]
    from_core = _exchange_pair_halves(slots)
    pairs = [_add_pair(g, got, c_idx) for g, got in zip(slots, from_core)]
    parts = _scatter_to_owner_chips(pairs)
    grads_big = dict(zip(BIG, _join_halves([_add_chips(t, c_idx) for t in parts])))
    reduced = _allreduce_small(_pack_small(gs, extra=loss_sum[0, 0]))
    grads_small = _unpack_small(reduced)
    loss = reduced[LOSS_ROW, 0]

    grads, deltas, new_m, new_v = {}, {}, {}, {}
    for k in BIG:
        d, nm, nv = _adamw(_stored(k, weights[k]), grads_big[k], _stored(k, moments_m[k]), _stored(k, moments_v[k]))
        grads[k], deltas[k], new_m[k], new_v[k] = (_returned(k, t) for t in (grads_big[k], d, nm, nv))
    d_s, nm_s, nv_s = _adamw(_pack_small({k: weights[k] for k in SMALL}), reduced,
                             _pack_small({k: moments_m[k] for k in SMALL}),
                             _pack_small({k: moments_v[k] for k in SMALL}))
    for k in SMALL:
        grads[k] = grads_small[k]
    for name, packed in (("d", d_s), ("m", nm_s), ("v", nv_s)):
        target = {"d": deltas, "m": new_m, "v": new_v}[name]
        target.update(_unpack_small(packed))

    return (loss, grad_x[None], *[grads[k] for k in ORDER], *[deltas[k] for k in ORDER],
            *[new_m[k] for k in ORDER], *[new_v[k] for k in ORDER])
```

```python
import functools

import jax
import jax.numpy as jnp
from jax import lax
from jax.experimental import pallas as pl
from jax.experimental.pallas import tpu as pltpu

F32 = jnp.float32
BF16 = jnp.bfloat16

D_MODEL = 1024
D_FF = 2816
N_CHIPS = 4
FF_SHARD = D_FF // N_CHIPS
HEAD_DIM = 64
N_HEADS = 8
ATT_W = N_HEADS * HEAD_DIM
PAIR_W = 2 * HEAD_DIM
N_PAIRS = N_HEADS // 2
PLE_DIM = 256
IN_WIDTH = 3 * ATT_W + N_HEADS + 3 * ATT_W + 2 * D_MODEL
EPS = 1e-6
QK_SCALE = HEAD_DIM ** -0.5
LANES = 128
ATT_BLOCK = 256
ATT_Q_BLOCK = 512
NEG_BIG = -1e30
EXP_UNDERFLOW = 110.0

ADAM_LR = 0.001
ADAM_B1 = 0.9
ADAM_B2 = 0.999
ADAM_EPS = 1e-08
ADAM_WD = 0.01
ADAM_STEP = 10

MESH = pl.DeviceIdType.MESH
MIB = 1024 * 1024


def _cparams(vmem_mib=48):
    return pltpu.CompilerParams(vmem_limit_bytes=vmem_mib * MIB)


def _dot(a, b):
    return jnp.dot(a, b, preferred_element_type=F32)


def _dot_tn(a, b):
    return lax.dot_general(a, b, (((0,), (0,)), ((), ())), preferred_element_type=F32)


def _dot_nt(a, b):
    return lax.dot_general(a, b, (((1,), (1,)), ((), ())), preferred_element_type=F32)


def _sigmoid(x):
    return 1.0 / (1.0 + jnp.exp(-x))


def _split2(x):
    hi = x.astype(BF16)
    lo = (x - hi.astype(F32)).astype(BF16)
    return hi, lo


def _dot_split2(x, m):
    hi, lo = _split2(x)
    return _dot(hi, m) + _dot(lo, m)


def _split3(x):
    hi = x.astype(BF16)
    rest = x - hi.astype(F32)
    mid = rest.astype(BF16)
    lo = (rest - mid.astype(F32)).astype(BF16)
    return hi, mid, lo


def _rms(x):
    r = lax.rsqrt(jnp.mean(x * x, axis=-1, keepdims=True) + EPS)
    return x * r, r


def _rms_bwd(dh, xn, r, g):
    dxn = dh * g
    return r * (dxn - xn * jnp.mean(dxn * xn, axis=-1, keepdims=True))


def _colsum(x):
    return jnp.sum(x, axis=0, keepdims=True)


def _row_block(rows, row_bytes, budget):
    best = None
    for t in range(8, rows + 1, 8):
        if rows % t == 0 and t * row_bytes <= budget:
            best = t
    return best if best is not None else rows


def _ffn_fwd(x, g, wg, wu, wd, gather=(), tm=512):
    s_len = x.shape[0]
    n = len(gather)
    steps = s_len // tm

    def body(x_ref, g_ref, wg_ref, wu_ref, wd_ref, *rest):
        o_ref, a_ref, b_ref = rest[n:n + 3]
        h_s, acc_s = rest[2 * n + 3:2 * n + 5]
        i = pl.program_id(0)
        j = pl.program_id(1)
        if n:
            start, finish = _gather_steps(rest[n + 3:2 * n + 3], *rest[2 * n + 5:])
            pl.when((i == 0) & (j == 0))(start)

        @pl.when(j == 0)
        def _():
            xn, _ = _rms(x_ref[...])
            h_s[...] = (xn * g_ref[...]).astype(BF16)
            acc_s[...] = jnp.zeros_like(acc_s)

        h = h_s[...]
        a = _dot_nt(h, wg_ref[0])
        b = _dot_nt(h, wu_ref[0])
        a_ref[0] = a.astype(BF16)
        b_ref[0] = b.astype(BF16)
        u = (a * _sigmoid(a) * b).astype(BF16)
        acc_s[...] += _dot(u, wd_ref[0])

        @pl.when(j == N_CHIPS - 1)
        def _():
            o_ref[...] = x_ref[...] + 0.5 * acc_s[...]

        if n:
            pl.when((i == steps - 1) & (j == N_CHIPS - 1))(finish)

    return pl.pallas_call(
        body,
        grid=(steps, N_CHIPS),
        in_specs=[
            pl.BlockSpec((tm, D_MODEL), lambda i, j: (i, 0)),
            pl.BlockSpec((1, D_MODEL), lambda i, j: (0, 0)),
            pl.BlockSpec((1, FF_SHARD, D_MODEL), lambda i, j: (j, 0, 0)),
            pl.BlockSpec((1, FF_SHARD, D_MODEL), lambda i, j: (j, 0, 0)),
            pl.BlockSpec((1, FF_SHARD, D_MODEL), lambda i, j: (j, 0, 0)),
        ] + [ANY] * n,
        out_specs=[pl.BlockSpec((tm, D_MODEL), lambda i, j: (i, 0)),
                   pl.BlockSpec((1, tm, FF_SHARD), lambda i, j: (j, i, 0)),
                   pl.BlockSpec((1, tm, FF_SHARD), lambda i, j: (j, i, 0))] + [ANY] * n,
        out_shape=[jax.ShapeDtypeStruct((s_len, D_MODEL), F32),
                   jax.ShapeDtypeStruct((N_CHIPS, s_len, FF_SHARD), BF16),
                   jax.ShapeDtypeStruct((N_CHIPS, s_len, FF_SHARD), BF16)]
        + [jax.ShapeDtypeStruct(s.shape, s.dtype) for s in gather],
        input_output_aliases={5 + a: 3 + a for a in range(n)},
        scratch_shapes=[pltpu.VMEM((tm, D_MODEL), BF16), pltpu.VMEM((tm, D_MODEL), F32)]
        + (_gather_semaphores(n) if n else []),
        compiler_params=_cparams(48),
        name="ffn_fwd_gathering" if n else "ffn_fwd",
    )(x, g, wg, wu, wd, *gather)


def _ffn_bwd(x, d, g, a_pre, b_pre, wg, wu, wd, scatter=(), tm=512):
    s_len = x.shape[0]
    nb = s_len // tm
    n = len(scatter)

    def body(x_ref, d_ref, g_ref, a_ref, b_ref, wg_ref, wu_ref, wd_ref, *rest):
        dx_ref, u_ref, da_ref, db_ref, h_ref, dbf_ref, dg_ref = rest[n:n + 7]
        dbf_s, dh_s = rest[2 * n + 7:2 * n + 9]
        i = pl.program_id(0)
        j = pl.program_id(1)
        if n:
            start, finish = _scatter_steps(rest[0:n], rest[n + 7:2 * n + 7], *rest[2 * n + 9:])
            pl.when((i == 0) & (j == 0))(start)

        @pl.when(j == 0)
        def _():
            xn, _ = _rms(x_ref[...])
            h_ref[...] = (xn * g_ref[...]).astype(BF16)
            dbf = d_ref[...].astype(BF16)
            dbf_s[...] = dbf
            dbf_ref[...] = dbf
            dh_s[...] = jnp.zeros_like(dh_s)

        @pl.when((i == 0) & (j == 0))
        def _():
            dg_ref[...] = jnp.zeros_like(dg_ref)

        a = a_ref[0].astype(F32)
        b = b_ref[0].astype(F32)
        du = 0.5 * _dot_nt(dbf_s[...], wd_ref[0])
        s = _sigmoid(a)
        silu = a * s
        da = (du * b * (s * (1.0 + a * (1.0 - s)))).astype(BF16)
        db = (du * silu).astype(BF16)
        u_ref[0] = (silu * b).astype(BF16)
        da_ref[0] = da
        db_ref[0] = db
        dh_s[...] += _dot(da, wg_ref[0]) + _dot(db, wu_ref[0])

        @pl.when(j == N_CHIPS - 1)
        def _():
            xn, r = _rms(x_ref[...])
            dh = dh_s[...]
            dx_ref[...] = d_ref[...] + _rms_bwd(dh, xn, r, g_ref[...])
            dg_ref[0:1, :] += _colsum(dh * xn)

        if n:
            pl.when((i == nb - 1) & (j == N_CHIPS - 1))(finish)

    row = lambda i, j: (i, 0)
    shard = lambda i, j: (j, 0, 0)
    act = lambda i, j: (j, i, 0)
    return pl.pallas_call(
        body,
        grid=(nb, N_CHIPS),
        in_specs=[
            pl.BlockSpec((tm, D_MODEL), row),
            pl.BlockSpec((tm, D_MODEL), row),
            pl.BlockSpec((1, D_MODEL), lambda i, j: (0, 0)),
            pl.BlockSpec((1, tm, FF_SHARD), act),
            pl.BlockSpec((1, tm, FF_SHARD), act),
            pl.BlockSpec((1, FF_SHARD, D_MODEL), shard),
            pl.BlockSpec((1, FF_SHARD, D_MODEL), shard),
            pl.BlockSpec((1, FF_SHARD, D_MODEL), shard),
        ] + [ANY] * n,
        out_specs=[
            pl.BlockSpec((tm, D_MODEL), row),
            pl.BlockSpec((1, tm, FF_SHARD), act),
            pl.BlockSpec((1, tm, FF_SHARD), act),
            pl.BlockSpec((1, tm, FF_SHARD), act),
            pl.BlockSpec((tm, D_MODEL), row),
            pl.BlockSpec((tm, D_MODEL), row),
            pl.BlockSpec((8, D_MODEL), lambda i, j: (0, 0)),
        ] + [ANY] * n,
        out_shape=[
            jax.ShapeDtypeStruct((s_len, D_MODEL), F32),
            jax.ShapeDtypeStruct((N_CHIPS, s_len, FF_SHARD), BF16),
            jax.ShapeDtypeStruct((N_CHIPS, s_len, FF_SHARD), BF16),
            jax.ShapeDtypeStruct((N_CHIPS, s_len, FF_SHARD), BF16),
            jax.ShapeDtypeStruct((s_len, D_MODEL), BF16),
            jax.ShapeDtypeStruct((s_len, D_MODEL), BF16),
            jax.ShapeDtypeStruct((8, D_MODEL), F32),
        ] + [jax.ShapeDtypeStruct(s.shape, s.dtype) for s in scatter],
        scratch_shapes=[
            pltpu.VMEM((tm, D_MODEL), BF16),
            pltpu.VMEM((tm, D_MODEL), F32),
        ] + (_scatter_semaphores(n) if n else []),
        compiler_params=_cparams(56),
        name="ffn_bwd_scattering" if n else "ffn_bwd",
    )(x, d, g, a_pre, b_pre, wg, wu, wd, *scatter)


def _wgrad(a, b, scale=1.0, name="wgrad"):
    na, s_len, k_dim = a.shape
    nb, _, n_dim = b.shape
    n = max(na, nb)
    ts = min(s_len, 1024)
    steps = s_len // ts

    def body(a_ref, b_ref, o_ref, acc_s):
        s = pl.program_id(1)

        @pl.when(s == 0)
        def _():
            acc_s[...] = jnp.zeros_like(acc_s)

        acc_s[...] += _dot_tn(a_ref[0].astype(BF16), b_ref[0].astype(BF16))

        @pl.when(s == steps - 1)
        def _():
            o_ref[0] = (acc_s[...] * scale).astype(BF16)

    a_map = (lambda m, s: (m, s, 0)) if na > 1 else (lambda m, s: (0, s, 0))
    b_map = (lambda m, s: (m, s, 0)) if nb > 1 else (lambda m, s: (0, s, 0))
    return pl.pallas_call(
        body,
        grid=(n, steps),
        in_specs=[pl.BlockSpec((1, ts, k_dim), a_map), pl.BlockSpec((1, ts, n_dim), b_map)],
        out_specs=pl.BlockSpec((1, k_dim, n_dim), lambda m, s: (m, 0, 0)),
        out_shape=jax.ShapeDtypeStruct((n, k_dim, n_dim), BF16),
        scratch_shapes=[pltpu.VMEM((k_dim, n_dim), F32)],
        compiler_params=_cparams(56),
        name=name,
    )(a, b)


def _head_sum_matrices():
    lane = lax.broadcasted_iota(jnp.int32, (ATT_W, LANES), 0) // HEAD_DIM
    col = lax.broadcasted_iota(jnp.int32, (ATT_W, LANES), 1)
    bd = (lane == col).astype(BF16)
    return bd, bd.T


def _head_mean(t, bd, bd_t):
    per_head = _dot_split2(t, bd) * (1.0 / HEAD_DIM)
    return _dot_split2(per_head, bd_t)


def _head_rms(x, bd, bd_t):
    per_head = _dot_split2(x * x, bd) * (1.0 / HEAD_DIM)
    r = lax.rsqrt(per_head + EPS)
    rw = _dot_split2(r, bd_t)
    return x * rw, rw


def _log_sigmoid(z):
    return jnp.minimum(z, 0.0) - jnp.log(1.0 + jnp.exp(-jnp.abs(z)))


def _inproj_fwd(x1, g, w_fox, w_fl, w_sb, w_gates, bias, qn, kn, bd, bd_t, tm=256):
    s_len = x1.shape[0]

    def body(x_ref, g_ref, wf_ref, wl_ref, ws_ref, wg_ref, bias_ref, qn_ref, kn_ref, bd_ref, bdt_ref,
             fq_ref, fk_ref, qs_ref, kf_ref, vf_ref, logf_ref, sq_ref, sk_ref, sv_ref, gates_ref):
        xn, _ = _rms(x_ref[...])
        h = (xn * g_ref[...]).astype(BF16)
        zf = _dot(h, wf_ref[...])
        fq = zf[:, 0:ATT_W]
        fk = zf[:, ATT_W:2 * ATT_W]
        fq_ref[...] = fq
        fk_ref[...] = fk
        bd_m = bd_ref[...]
        bdt_m = bdt_ref[...]
        fqn, _ = _head_rms(fq, bd_m, bdt_m)
        fkn, _ = _head_rms(fk, bd_m, bdt_m)
        qs_ref[...] = (fqn * qn_ref[...]).astype(BF16) * QK_SCALE
        kf_ref[...] = (fkn * kn_ref[...]).astype(BF16)
        vf_ref[...] = zf[:, 2 * ATT_W:3 * ATT_W].astype(BF16)
        logf_ref[...] = _log_sigmoid(_dot(h, wl_ref[...]) + bias_ref[...])
        zs = _dot(h, ws_ref[...])
        sq_ref[...] = zs[:, 0:ATT_W].astype(BF16) * QK_SCALE
        sk_ref[...] = zs[:, ATT_W:2 * ATT_W].astype(BF16)
        sv_ref[...] = zs[:, 2 * ATT_W:3 * ATT_W].astype(BF16)
        gates_ref[...] = _dot(h, wg_ref[...])

    row = lambda i: (i, 0)
    full = lambda i: (0, 0)
    att = lambda dt: jax.ShapeDtypeStruct((s_len, ATT_W), dt)
    return pl.pallas_call(
        body,
        grid=(s_len // tm,),
        in_specs=[
            pl.BlockSpec((tm, D_MODEL), row),
            pl.BlockSpec((1, D_MODEL), full),
            pl.BlockSpec((D_MODEL, 3 * ATT_W), full),
            pl.BlockSpec((D_MODEL, LANES), full),
            pl.BlockSpec((D_MODEL, 3 * ATT_W), full),
            pl.BlockSpec((D_MODEL, 2 * D_MODEL), full),
            pl.BlockSpec((1, LANES), full),
            pl.BlockSpec((1, ATT_W), full),
            pl.BlockSpec((1, ATT_W), full),
            pl.BlockSpec((ATT_W, LANES), full),
            pl.BlockSpec((LANES, ATT_W), full),
        ],
        out_specs=[
            pl.BlockSpec((tm, ATT_W), row), pl.BlockSpec((tm, ATT_W), row),
            pl.BlockSpec((tm, ATT_W), row), pl.BlockSpec((tm, ATT_W), row), pl.BlockSpec((tm, ATT_W), row),
            pl.BlockSpec((tm, LANES), row),
            pl.BlockSpec((tm, ATT_W), row), pl.BlockSpec((tm, ATT_W), row), pl.BlockSpec((tm, ATT_W), row),
            pl.BlockSpec((tm, 2 * D_MODEL), row),
        ],
        out_shape=[
            att(F32), att(F32), att(BF16), att(BF16), att(BF16),
            jax.ShapeDtypeStruct((s_len, LANES), F32),
            att(BF16), att(BF16), att(BF16),
            jax.ShapeDtypeStruct((s_len, 2 * D_MODEL), F32),
        ],
        compiler_params=_cparams(56),
        name="inproj_fwd",
    )(x1, g, w_fox, w_fl, w_sb, w_gates, bias, qn, kn, bd, bd_t)


def _tri(n, kind):
    r = lax.broadcasted_iota(jnp.int32, (n, n), 0)
    c = lax.broadcasted_iota(jnp.int32, (n, n), 1)
    m = {"row_ge_col": r >= c, "row_le_col": r <= c, "row_gt_col": r > c, "row_lt_col": r < c}[kind]
    return m.astype(BF16)


def _cumsum_rows(x, reverse, tm=256):
    s_len = x.shape[0]
    nb = s_len // tm
    tri = _tri(tm, "row_le_col" if reverse else "row_ge_col")
    edge = 0 if reverse else tm - 1

    def body(x_ref, tri_ref, o_ref, carry_s):
        @pl.when(pl.program_id(0) == 0)
        def _():
            carry_s[...] = jnp.zeros_like(carry_s)

        hi, mid, lo = _split3(x_ref[...])
        t = tri_ref[...]
        y = _dot(t, hi) + _dot(t, mid) + _dot(t, lo) + carry_s[...]
        o_ref[...] = y
        carry_s[...] = y[edge:edge + 1, :]

    order = (lambda i: (nb - 1 - i, 0)) if reverse else (lambda i: (i, 0))
    return pl.pallas_call(
        body,
        grid=(nb,),
        in_specs=[pl.BlockSpec((tm, LANES), order), pl.BlockSpec((tm, tm), lambda i: (0, 0))],
        out_specs=pl.BlockSpec((tm, LANES), order),
        out_shape=jax.ShapeDtypeStruct((s_len, LANES), F32),
        scratch_shapes=[pltpu.VMEM((1, LANES), F32)],
        name="cumsum_rev" if reverse else "cumsum_fwd",
    )(x, tri)


def _unblocked_t(t4):
    _, nb, _, blk = t4.shape
    return t4.transpose(1, 3, 0, 2).reshape(nb * blk, ATT_W)


def _blocked_rows(t, blk):
    return t.reshape(t.shape[0] // blk, blk, t.shape[1])


def _pair_rows_t(f8, blk):
    nb = f8.shape[0] // blk
    t = f8.reshape(nb, blk, N_PAIRS, 2).transpose(2, 0, 3, 1)
    return jnp.pad(t, ((0, 0), (0, 0), (0, 6), (0, 0)))


def _unpair_rows_t(t4):
    _, nb, _, blk = t4.shape
    return t4[:, :, 0:2, :].transpose(1, 3, 0, 2).reshape(nb * blk, N_HEADS)


def _head_masks(tq):
    lane = lax.broadcasted_iota(jnp.int32, (tq, PAIR_W), 1)
    return lane < HEAD_DIM


def _causal_mask(tq, tk, offset, strict):
    d = lax.broadcasted_iota(jnp.int32, (tq, tk), 1) - lax.broadcasted_iota(jnp.int32, (tq, tk), 0)
    return (d < offset) if strict else (d <= offset)


def _heads_of(ref, first):
    t = ref[...]
    zero = jnp.zeros_like(t)
    return [jnp.where(first, t, zero), jnp.where(first, zero, t)]


def _head_cols(ref):
    t = ref[...]
    return [t[:, 0:1], t[:, HEAD_DIM:HEAD_DIM + 1]]


def _att_specs(s_len):
    tq, tk = ATT_Q_BLOCK, ATT_BLOCK
    nq, nk = s_len // tq, s_len // tk
    return dict(
        nq=nq,
        q=pl.BlockSpec((tq, PAIR_W), lambda p, i: (i, p)),
        k_t=pl.BlockSpec((1, nk, PAIR_W, tk), lambda p, i: (p, 0, 0, 0)),
        k_rows=pl.BlockSpec((nk, tk, PAIR_W), lambda p, i: (0, 0, p)),
        f_t=pl.BlockSpec((1, nk, 8, tk), lambda p, i: (p, 0, 0, 0)),
        first=pl.BlockSpec((1, 1, 8, LANES), lambda p, i: (p, i, 0, 0)),
        wide=jax.ShapeDtypeStruct((s_len, ATT_W), F32),
        k_t_out=jax.ShapeDtypeStruct((N_PAIRS, nk, PAIR_W, tk), F32),
        f_t_out=jax.ShapeDtypeStruct((N_PAIRS, nk, 8, tk), F32),
        first_out=jax.ShapeDtypeStruct((N_PAIRS, nq, 8, LANES), F32),
        acc=pltpu.VMEM((2, tq, PAIR_W), F32),
    )


def _first_block(first_ref, limit):
    return jnp.clip(jnp.max(first_ref[0, 0]).astype(jnp.int32), 0, limit)


def _key_norm_bound(k):
    sq = jnp.sum(jnp.square(k.astype(F32)).reshape(k.shape[0], N_HEADS, HEAD_DIM), axis=-1)
    bound = jnp.sqrt(jnp.max(sq, axis=0)).reshape(N_PAIRS, 2)
    return jnp.broadcast_to(jnp.pad(bound, ((0, 0), (0, 6)))[:, :, None], (N_PAIRS, 8, LANES))


def _fox_fwd(qs, k3, v3, fw, ft4, kmax):
    sp = _att_specs(qs.shape[0])
    tq, tk = ATT_Q_BLOCK, ATT_BLOCK
    ratio = tq // tk

    def body(q_ref, k_ref, v_ref, fw_ref, ft_ref, kmax_ref, y_ref, lse_ref, first_ref, acc_ref, max_ref, sum_ref):
        i = pl.program_id(1)
        first = _head_masks(tq)
        qh = _heads_of(q_ref, first)
        fqh = _head_cols(fw_ref)
        acc_ref[...] = jnp.zeros_like(acc_ref)
        sum_ref[...] = jnp.zeros_like(sum_ref)
        max_ref[...] = jnp.full(max_ref.shape, NEG_BIG, F32)
        reach = []
        for n in range(2):
            qf = qh[n].astype(F32)
            reach.append(jnp.sqrt(jnp.sum(qf * qf, axis=-1, keepdims=True)) * kmax_ref[0, n:n + 1, 0:1] + fqh[n])

        def logits(j, shift, diag):
            k, fk = k_ref[j], ft_ref[0, j]
            raw = [_dot_nt(qh[n], k) for n in range(2)]
            out = []
            for n in range(2):
                s = raw[n] + (shift[n] - fk[n:n + 1, :])
                if diag:
                    s = jnp.where(_causal_mask(tq, tk, i * tq - j * tk, strict=False), s, NEG_BIG)
                out.append(s)
            return out

        def max_pass(j, diag):
            ss = logits(j, fqh, diag)
            for n in range(2):
                max_ref[n] = jnp.maximum(max_ref[n], ss[n])

        def sum_pass(j, shift, diag):
            ps = [jnp.exp(s) for s in logits(j, shift, diag)]
            v = v_ref[j]
            for n in range(2):
                sum_ref[n] += ps[n]
            for n in range(2):
                acc_ref[n] += _dot(ps[n].astype(BF16), v)

        for d in range(ratio):
            max_pass(ratio * i + d, True)

        def block_matters(j):
            gap = []
            for n in range(2):
                m_run = jnp.max(max_ref[n], axis=-1, keepdims=True)
                f_end = ft_ref[0, jnp.maximum(j, 0)][n:n + 1, tk - 1:tk]
                gap.append(jnp.max(reach[n] - m_run) - jnp.max(f_end))
            return (j >= 0) & (jnp.maximum(gap[0], gap[1]) > -EXP_UNDERFLOW)

        def walk_left(j):
            max_pass(j, False)
            return j - 1

        j_first = lax.while_loop(block_matters, walk_left, ratio * i - 1) + 1
        m = [jnp.max(max_ref[n], axis=-1, keepdims=True) for n in range(2)]
        shift = [fqh[n] - m[n] for n in range(2)]

        def one(j, c):
            sum_pass(j, shift, False)
            return c
        lax.fori_loop(j_first, ratio * i, one, 0)
        for d in range(ratio):
            sum_pass(ratio * i + d, shift, True)
        l = [jnp.sum(sum_ref[n], axis=-1, keepdims=True) for n in range(2)]
        y_ref[...] = jnp.where(first, acc_ref[0] / l[0], acc_ref[1] / l[1])
        lse_ref[...] = jnp.where(first, m[0] + jnp.log(l[0]), m[1] + jnp.log(l[1]))
        first_ref[...] = jnp.ones(first_ref.shape, F32) * j_first.astype(F32)

    tile = pltpu.VMEM((2, tq, tk), F32)
    return pl.pallas_call(
        body,
        grid=(N_PAIRS, sp["nq"]),
        in_specs=[sp["q"], sp["k_rows"], sp["k_rows"], sp["q"], sp["f_t"],
                  pl.BlockSpec((1, 8, LANES), lambda p, i: (p, 0, 0))],
        out_specs=[sp["q"], sp["q"], sp["first"]],
        out_shape=[sp["wide"], sp["wide"], sp["first_out"]],
        scratch_shapes=[sp["acc"], tile, tile],
        compiler_params=_cparams(56),
        name="fox_fwd",
    )(qs, k3, v3, fw, ft4, kmax)


def _fox_bwd(qs, k3, v3, dy, y, lse, fw, ft4, first_block):
    sp = _att_specs(qs.shape[0])
    tq, tk = ATT_Q_BLOCK, ATT_BLOCK
    ratio = tq // tk

    def body(q_ref, k_ref, v_ref, dy_ref, y_ref, lse_ref, fw_ref, ft_ref, first_ref,
             dq_ref, dfq_ref, dkt_ref, dvt_ref, dft_ref, acc_ref):
        i = pl.program_id(1)

        @pl.when(i == 0)
        def _():
            dkt_ref[...] = jnp.zeros_like(dkt_ref)
            dvt_ref[...] = jnp.zeros_like(dvt_ref)
            dft_ref[...] = jnp.zeros_like(dft_ref)

        first = _head_masks(tq)
        qh = _heads_of(q_ref, first)
        dyv = dy_ref[...]
        dyb = dyv.astype(BF16)
        zero = jnp.zeros_like(dyb)
        dyh = [jnp.where(first, dyb, zero), jnp.where(first, zero, dyb)]
        prod = dyv * y_ref[...]
        zf = jnp.zeros_like(prod)
        delta = [jnp.sum(jnp.where(first, prod, zf), axis=-1, keepdims=True),
                 jnp.sum(jnp.where(first, zf, prod), axis=-1, keepdims=True)]
        fqh = _head_cols(fw_ref)
        lseh = _head_cols(lse_ref)
        shift = [fqh[n] - lseh[n] for n in range(2)]
        acc_ref[...] = jnp.zeros_like(acc_ref)

        def block(j, rows, diag):
            mask = _causal_mask(tq, tk, i * tq - j * tk, strict=False) if diag else None
            k, v, fk = k_ref[j], v_ref[j], ft_ref[0, j]
            logits = [_dot_nt(qh[n], k) for n in range(2)]
            dps = [_dot_nt(dyh[n], v) for n in range(2)]
            pbs, dsbs, out = [], [], []
            for n in range(2):
                p = jnp.exp(logits[n] + (shift[n] - fk[n:n + 1, :]))
                if diag:
                    p = jnp.where(mask, p, 0.0)
                ds = p * (dps[n] - delta[n])
                pbs.append(p.astype(BF16))
                dsbs.append(ds.astype(BF16))
                out.append(rows[n] + jnp.sum(ds, axis=-1, keepdims=True))
                dft_ref[0, j, n:n + 1, :] -= _colsum(ds)
            for n in range(2):
                acc_ref[n] += _dot(dsbs[n], k)
            dkt_ref[0, j] += _dot_tn(qh[0], dsbs[0]) + _dot_tn(qh[1], dsbs[1])
            dvt_ref[0, j] += _dot_tn(dyh[0], pbs[0]) + _dot_tn(dyh[1], pbs[1])
            return tuple(out)

        rows = (jnp.zeros((tq, 1), F32),) * 2
        rows = lax.fori_loop(_first_block(first_ref, ratio * i), ratio * i, lambda j, c: block(j, c, False), rows)
        for d in range(ratio):
            rows = block(ratio * i + d, rows, True)
        dq_ref[...] = jnp.where(first, acc_ref[0], acc_ref[1])
        dfq_ref[...] = jnp.where(first, rows[0], rows[1])

    return pl.pallas_call(
        body,
        grid=(N_PAIRS, sp["nq"]),
        in_specs=[sp["q"], sp["k_rows"], sp["k_rows"], sp["q"], sp["q"], sp["q"], sp["q"], sp["f_t"], sp["first"]],
        out_specs=[sp["q"], sp["q"], sp["k_t"], sp["k_t"], sp["f_t"]],
        out_shape=[sp["wide"], sp["wide"], sp["k_t_out"], sp["k_t_out"], sp["f_t_out"]],
        scratch_shapes=[sp["acc"]],
        compiler_params=_cparams(56),
        name="fox_bwd",
    )(qs, k3, v3, dy, y, lse, fw, ft4, first_block)


SIGN_BIT = 0x80000000


def _sb_terms(z, mask, diag):
    neg_abs = pltpu.bitcast(pltpu.bitcast(z, jnp.uint32) | jnp.uint32(SIGN_BIT), F32)
    lb = jnp.minimum(z, 0.0) - jnp.log(1.0 + jnp.exp(neg_abs))
    l1m = lb - z
    if diag:
        l1m = jnp.where(mask, l1m, 0.0)
    return lb, l1m


def _dot_split2_stacked(x, m2):
    hi, lo = _split2(x)
    return _dot(jnp.concatenate([hi, lo], axis=1), m2)


def _tri_stacked(kind):
    t = _tri(ATT_BLOCK, kind)
    return jnp.concatenate([t, t], axis=0)


def _sb_fwd(qs, k3, v3):
    sp = _att_specs(qs.shape[0])
    tq, tk = ATT_Q_BLOCK, ATT_BLOCK
    ratio = tq // tk
    upper = _tri_stacked("row_gt_col")

    def body(q_ref, k_ref, v_ref, u_ref, y_ref, rtot_ref, first_ref, acc_ref):
        i = pl.program_id(1)
        first = _head_masks(tq)
        qh = _heads_of(q_ref, first)
        u = u_ref[...]
        acc_ref[...] = jnp.zeros_like(acc_ref)

        def block(j, rs, diag):
            mask = _causal_mask(tq, tk, i * tq - j * tk, strict=True) if diag else None
            k, v = k_ref[j], v_ref[j]
            logits = [_dot_nt(qh[n], k) for n in range(2)]
            terms = [_sb_terms(z, mask, diag) for z in logits]
            right = [_dot_split2_stacked(l1m, u) for _, l1m in terms]
            weights = []
            for n in range(2):
                a = jnp.exp(terms[n][0] + right[n] + rs[n])
                if diag:
                    a = jnp.where(mask, a, 0.0)
                weights.append(a.astype(BF16))
            for n in range(2):
                acc_ref[n] += _dot(weights[n], v)
            return tuple(rs[n] + jnp.sum(terms[n][1], axis=-1, keepdims=True) for n in range(2))

        rs = (jnp.zeros((tq, 1), F32),) * 2
        for d in range(ratio):
            rs = block(ratio * i + (ratio - 1 - d), rs, True)

        def block_matters(c):
            j, r0, r1 = c
            return (j >= 0) & (jnp.max(jnp.maximum(r0, r1)) > -EXP_UNDERFLOW)

        def walk_left(c):
            j, r0, r1 = c
            r0, r1 = block(j, (r0, r1), False)
            return j - 1, r0, r1

        j, r0, r1 = lax.while_loop(block_matters, walk_left, (ratio * i - 1, rs[0], rs[1]))
        y_ref[...] = jnp.where(first, acc_ref[0], acc_ref[1])
        rtot_ref[...] = jnp.where(first, r0, r1)
        first_ref[...] = jnp.ones(first_ref.shape, F32) * (j + 1).astype(F32)

    return pl.pallas_call(
        body,
        grid=(N_PAIRS, sp["nq"]),
        in_specs=[sp["q"], sp["k_rows"], sp["k_rows"], pl.BlockSpec((2 * tk, tk), lambda p, i: (0, 0))],
        out_specs=[sp["q"], sp["q"], sp["first"]],
        out_shape=[sp["wide"], sp["wide"], sp["first_out"]],
        scratch_shapes=[sp["acc"]],
        compiler_params=_cparams(56),
        name="sb_fwd",
    )(qs, k3, v3, upper)


def _sb_bwd(qs, k3, v3, dy, rtot, first_block):
    sp = _att_specs(qs.shape[0])
    tq, tk = ATT_Q_BLOCK, ATT_BLOCK
    ratio = tq // tk
    lower_in = _tri_stacked("row_le_col")
    lower = _tri(tk, "row_lt_col")

    def body(q_ref, k_ref, v_ref, dy_ref, rtot_ref, first_ref, li_ref, l_ref, dq_ref, dkt_ref, dvt_ref, acc_ref):
        i = pl.program_id(1)

        @pl.when(i == 0)
        def _():
            dkt_ref[...] = jnp.zeros_like(dkt_ref)
            dvt_ref[...] = jnp.zeros_like(dvt_ref)

        first = _head_masks(tq)
        qh = _heads_of(q_ref, first)
        dyb = dy_ref[...].astype(BF16)
        zero = jnp.zeros_like(dyb)
        dyh = [jnp.where(first, dyb, zero), jnp.where(first, zero, dyb)]
        rtoth = _head_cols(rtot_ref)
        li = li_ref[...]
        lo_tri = l_ref[...]
        acc_ref[...] = jnp.zeros_like(acc_ref)

        def block(j, carry, diag):
            mask = _causal_mask(tq, tk, i * tq - j * tk, strict=True) if diag else None
            k, v = k_ref[j], v_ref[j]
            logits = [_dot_nt(qh[n], k) for n in range(2)]
            das = [_dot_nt(dyh[n], v) for n in range(2)]
            terms = [_sb_terms(z, mask, diag) for z in logits]
            upto = [_dot_split2_stacked(l1m, li) for _, l1m in terms]
            des, weights = [], []
            for n in range(2):
                a = jnp.exp(terms[n][0] + ((rtoth[n] - carry[2 * n]) - upto[n]))
                if diag:
                    a = jnp.where(mask, a, 0.0)
                des.append(a * das[n])
                weights.append(a.astype(BF16))
            lefts = [_dot(de.astype(BF16), lo_tri) for de in des]
            dzbs, out = [], []
            for n in range(2):
                beta = jnp.exp(terms[n][0])
                dz = des[n] - (des[n] + (carry[2 * n + 1] + lefts[n])) * beta
                if diag:
                    dz = jnp.where(mask, dz, 0.0)
                dzbs.append(dz.astype(BF16))
                out += [carry[2 * n] + jnp.sum(terms[n][1], axis=-1, keepdims=True),
                        carry[2 * n + 1] + jnp.sum(des[n], axis=-1, keepdims=True)]
            for n in range(2):
                acc_ref[n] += _dot(dzbs[n], k)
            dkt_ref[0, j] += _dot_tn(qh[0], dzbs[0]) + _dot_tn(qh[1], dzbs[1])
            dvt_ref[0, j] += _dot_tn(dyh[0], weights[0]) + _dot_tn(dyh[1], weights[1])
            return tuple(out)

        carry = (jnp.zeros((tq, 1), F32),) * 4
        carry = lax.fori_loop(_first_block(first_ref, ratio * i), ratio * i, lambda j, c: block(j, c, False), carry)
        for d in range(ratio):
            carry = block(ratio * i + d, carry, True)
        dq_ref[...] = jnp.where(first, acc_ref[0], acc_ref[1])

    return pl.pallas_call(
        body,
        grid=(N_PAIRS, sp["nq"]),
        in_specs=[sp["q"], sp["k_rows"], sp["k_rows"], sp["q"], sp["q"], sp["first"],
                  pl.BlockSpec((2 * tk, tk), lambda p, i: (0, 0)), pl.BlockSpec((tk, tk), lambda p, i: (0, 0))],
        out_specs=[sp["q"], sp["k_t"], sp["k_t"]],
        out_shape=[sp["wide"], sp["k_t_out"], sp["k_t_out"]],
        scratch_shapes=[sp["acc"]],
        compiler_params=_cparams(56),
        name="sb_bwd",
    )(qs, k3, v3, dy, rtot, first_block, lower_in, lower)


def _merge_fwd(x1, gates, y_fox, y_sb, w_bf, w_bs, w_out, tm=512):
    s_len = x1.shape[0]

    def body(x_ref, g_ref, yf_ref, ys_ref, wbf_ref, wbs_ref, wo_ref, o_ref):
        g = g_ref[...]
        of = _dot(yf_ref[...].astype(BF16), wbf_ref[...])
        os_ = _dot(ys_ref[...].astype(BF16), wbs_ref[...])
        merged = _sigmoid(g[:, 0:D_MODEL]) * of + _sigmoid(g[:, D_MODEL:]) * os_
        o_ref[...] = x_ref[...] + _dot(merged.astype(BF16), wo_ref[...])

    row = lambda i: (i, 0)
    full = lambda i: (0, 0)
    return pl.pallas_call(
        body,
        grid=(s_len // tm,),
        in_specs=[
            pl.BlockSpec((tm, D_MODEL), row),
            pl.BlockSpec((tm, 2 * D_MODEL), row),
            pl.BlockSpec((tm, ATT_W), row),
            pl.BlockSpec((tm, ATT_W), row),
            pl.BlockSpec((ATT_W, D_MODEL), full),
            pl.BlockSpec((ATT_W, D_MODEL), full),
            pl.BlockSpec((D_MODEL, D_MODEL), full),
        ],
        out_specs=pl.BlockSpec((tm, D_MODEL), row),
        out_shape=jax.ShapeDtypeStruct((s_len, D_MODEL), F32),
        compiler_params=_cparams(48),
        name="merge_fwd",
    )(x1, gates, y_fox, y_sb, w_bf, w_bs, w_out)


def _merge_bwd(dx2, gates, y_fox, y_sb, w_bf, w_bs, w_out, tm=512):
    s_len = dx2.shape[0]

    def body(d_ref, g_ref, yf_ref, ys_ref, wbf_ref, wbs_ref, wo_ref,
             dyf_ref, dys_ref, dg_ref, dof_ref, dos_ref, m_ref, dbf_ref):
        dbf = d_ref[...].astype(BF16)
        dbf_ref[...] = dbf
        dm = _dot_nt(dbf, wo_ref[...])
        g = g_ref[...]
        of = _dot(yf_ref[...].astype(BF16), wbf_ref[...])
        os_ = _dot(ys_ref[...].astype(BF16), wbs_ref[...])
        sf = _sigmoid(g[:, 0:D_MODEL])
        ss = _sigmoid(g[:, D_MODEL:])
        m_ref[...] = (sf * of + ss * os_).astype(BF16)
        d_of = (dm * sf).astype(BF16)
        d_os = (dm * ss).astype(BF16)
        dof_ref[...] = d_of
        dos_ref[...] = d_os
        dg_ref[:, 0:D_MODEL] = (dm * of * sf * (1.0 - sf)).astype(BF16)
        dg_ref[:, D_MODEL:] = (dm * os_ * ss * (1.0 - ss)).astype(BF16)
        dyf_ref[...] = _dot_nt(d_of, wbf_ref[...])
        dys_ref[...] = _dot_nt(d_os, wbs_ref[...])

    row = lambda i: (i, 0)
    full = lambda i: (0, 0)
    return pl.pallas_call(
        body,
        grid=(s_len // tm,),
        in_specs=[
            pl.BlockSpec((tm, D_MODEL), row),
            pl.BlockSpec((tm, 2 * D_MODEL), row),
            pl.BlockSpec((tm, ATT_W), row),
            pl.BlockSpec((tm, ATT_W), row),
            pl.BlockSpec((ATT_W, D_MODEL), full),
            pl.BlockSpec((ATT_W, D_MODEL), full),
            pl.BlockSpec((D_MODEL, D_MODEL), full),
        ],
        out_specs=[
            pl.BlockSpec((tm, ATT_W), row), pl.BlockSpec((tm, ATT_W), row),
            pl.BlockSpec((tm, 2 * D_MODEL), row),
            pl.BlockSpec((tm, D_MODEL), row), pl.BlockSpec((tm, D_MODEL), row),
            pl.BlockSpec((tm, D_MODEL), row), pl.BlockSpec((tm, D_MODEL), row),
        ],
        out_shape=[
            jax.ShapeDtypeStruct((s_len, ATT_W), F32), jax.ShapeDtypeStruct((s_len, ATT_W), F32),
            jax.ShapeDtypeStruct((s_len, 2 * D_MODEL), BF16),
            jax.ShapeDtypeStruct((s_len, D_MODEL), BF16), jax.ShapeDtypeStruct((s_len, D_MODEL), BF16),
            jax.ShapeDtypeStruct((s_len, D_MODEL), BF16), jax.ShapeDtypeStruct((s_len, D_MODEL), BF16),
        ],
        compiler_params=_cparams(56),
        name="merge_bwd",
    )(dx2, gates, y_fox, y_sb, w_bf, w_bs, w_out)


def _ple_loss(x3, p, g, w_pg, w_pp, target, tm=512):
    s_len = x3.shape[0]
    inv_d = 1.0 / D_MODEL

    def body(x_ref, p_ref, g_ref, wpg_ref, wpp_ref, t_ref,
             dx_ref, du_ref, dt_ref, hn_ref, dg_ref, loss_ref):
        @pl.when(pl.program_id(0) == 0)
        def _():
            dg_ref[...] = jnp.zeros_like(dg_ref)
            loss_ref[...] = jnp.zeros_like(loss_ref)

        x = x_ref[...]
        xn, r = _rms(x)
        gain = g_ref[...]
        hn = (xn * gain).astype(BF16)
        hn_ref[...] = hn
        sg = _sigmoid(_dot(hn, wpg_ref[...]))
        t = _dot(p_ref[...].astype(BF16), wpp_ref[...])
        err = x + sg * t - t_ref[...]
        sq = jnp.sum(_colsum(err * err), axis=-1, keepdims=True)
        loss_ref[...] += (0.5 * inv_d) * sq
        dy = err * inv_d
        du = (dy * t * sg * (1.0 - sg)).astype(BF16)
        du_ref[...] = du
        dt_ref[...] = (dy * sg).astype(BF16)
        dh = _dot_nt(du, wpg_ref[...])
        dx_ref[...] = dy + _rms_bwd(dh, xn, r, gain)
        dg_ref[0:1, :] += _colsum(dh * xn)

    row = lambda i: (i, 0)
    full = lambda i: (0, 0)
    bf = jax.ShapeDtypeStruct((s_len, D_MODEL), BF16)
    return pl.pallas_call(
        body,
        grid=(s_len // tm,),
        in_specs=[
            pl.BlockSpec((tm, D_MODEL), row),
            pl.BlockSpec((tm, PLE_DIM), row),
            pl.BlockSpec((1, D_MODEL), full),
            pl.BlockSpec((D_MODEL, D_MODEL), full),
            pl.BlockSpec((PLE_DIM, D_MODEL), full),
            pl.BlockSpec((tm, D_MODEL), row),
        ],
        out_specs=[
            pl.BlockSpec((tm, D_MODEL), row), pl.BlockSpec((tm, D_MODEL), row),
            pl.BlockSpec((tm, D_MODEL), row), pl.BlockSpec((tm, D_MODEL), row),
            pl.BlockSpec((8, D_MODEL), full), pl.BlockSpec((8, LANES), full),
        ],
        out_shape=[
            jax.ShapeDtypeStruct((s_len, D_MODEL), F32), bf, bf, bf,
            jax.ShapeDtypeStruct((8, D_MODEL), F32), jax.ShapeDtypeStruct((8, LANES), F32),
        ],
        compiler_params=_cparams(48),
        name="ple_loss",
    )(x3, p, g, w_pg, w_pp, target)


def _qknorm_bwd(fq, fk, dqs, dk, dv, qn, kn, bd, bd_t, tm=256):
    s_len = fq.shape[0]

    def body(fq_ref, fk_ref, dq_ref, dk_ref, dv_ref, qn_ref, kn_ref, bd_ref, bdt_ref,
             dz_ref, dqn_ref, dkn_ref):
        @pl.when(pl.program_id(0) == 0)
        def _():
            dqn_ref[...] = jnp.zeros_like(dqn_ref)
            dkn_ref[...] = jnp.zeros_like(dkn_ref)

        bd_m = bd_ref[...]
        bdt_m = bdt_ref[...]

        def one(x, dy, gain, dgain_ref):
            xn, rw = _head_rms(x, bd_m, bdt_m)
            dgain_ref[0:1, :] += _colsum(dy * xn)
            dxn = dy * gain
            return rw * (dxn - xn * _head_mean(dxn * xn, bd_m, bdt_m))

        dz_ref[:, 0:ATT_W] = one(fq_ref[...], dq_ref[...] * QK_SCALE, qn_ref[...], dqn_ref).astype(BF16)
        dz_ref[:, ATT_W:2 * ATT_W] = one(fk_ref[...], dk_ref[...], kn_ref[...], dkn_ref).astype(BF16)
        dz_ref[:, 2 * ATT_W:] = dv_ref[...].astype(BF16)

    row = lambda i: (i, 0)
    full = lambda i: (0, 0)
    att = pl.BlockSpec((tm, ATT_W), row)
    return pl.pallas_call(
        body,
        grid=(s_len // tm,),
        in_specs=[att, att, att, att, att,
                  pl.BlockSpec((1, ATT_W), full), pl.BlockSpec((1, ATT_W), full),
                  pl.BlockSpec((ATT_W, LANES), full), pl.BlockSpec((LANES, ATT_W), full)],
        out_specs=[pl.BlockSpec((tm, 3 * ATT_W), row), pl.BlockSpec((8, ATT_W), full), pl.BlockSpec((8, ATT_W), full)],
        out_shape=[jax.ShapeDtypeStruct((s_len, 3 * ATT_W), BF16),
                   jax.ShapeDtypeStruct((8, ATT_W), F32), jax.ShapeDtypeStruct((8, ATT_W), F32)],
        name="qknorm_bwd",
    )(fq, fk, dqs, dk, dv, qn, kn, bd, bd_t)


def _inproj_bwd(x1, dx2, g, dzf, dlogf, logf, dzs, dgates, w_fox, w_fl, w_sb, w_gates, tm=256):
    s_len = x1.shape[0]

    def body(x_ref, d_ref, g_ref, dzf_ref, dlf_ref, lf_ref, dzs_ref, dgt_ref, wf_ref, wl_ref, ws_ref, wg_ref,
             dx_ref, h_ref, dfl_ref, dg_ref, db_ref):
        @pl.when(pl.program_id(0) == 0)
        def _():
            dg_ref[...] = jnp.zeros_like(dg_ref)
            db_ref[...] = jnp.zeros_like(db_ref)

        xn, r = _rms(x_ref[...])
        gain = g_ref[...]
        h_ref[...] = (xn * gain).astype(BF16)
        lane = lax.broadcasted_iota(jnp.int32, (tm, LANES), 1)
        dfl = jnp.where(lane < N_HEADS, dlf_ref[...] * (1.0 - jnp.exp(lf_ref[...])), 0.0)
        db_ref[0:1, :] += _colsum(dfl)
        dflb = dfl.astype(BF16)
        dfl_ref[...] = dflb
        dh = (_dot_nt(dzf_ref[...], wf_ref[...]) + _dot_nt(dflb, wl_ref[...])
              + _dot_nt(dzs_ref[...], ws_ref[...]) + _dot_nt(dgt_ref[...], wg_ref[...]))
        dx_ref[...] = d_ref[...] + _rms_bwd(dh, xn, r, gain)
        dg_ref[0:1, :] += _colsum(dh * xn)

    row = lambda i: (i, 0)
    full = lambda i: (0, 0)
    return pl.pallas_call(
        body,
        grid=(s_len // tm,),
        in_specs=[
            pl.BlockSpec((tm, D_MODEL), row),
            pl.BlockSpec((tm, D_MODEL), row),
            pl.BlockSpec((1, D_MODEL), full),
            pl.BlockSpec((tm, 3 * ATT_W), row),
            pl.BlockSpec((tm, LANES), row),
            pl.BlockSpec((tm, LANES), row),
            pl.BlockSpec((tm, 3 * ATT_W), row),
            pl.BlockSpec((tm, 2 * D_MODEL), row),
            pl.BlockSpec((D_MODEL, 3 * ATT_W), full),
            pl.BlockSpec((D_MODEL, LANES), full),
            pl.BlockSpec((D_MODEL, 3 * ATT_W), full),
            pl.BlockSpec((D_MODEL, 2 * D_MODEL), full),
        ],
        out_specs=[
            pl.BlockSpec((tm, D_MODEL), row), pl.BlockSpec((tm, D_MODEL), row), pl.BlockSpec((tm, LANES), row),
            pl.BlockSpec((8, D_MODEL), full), pl.BlockSpec((8, LANES), full),
        ],
        out_shape=[
            jax.ShapeDtypeStruct((s_len, D_MODEL), F32), jax.ShapeDtypeStruct((s_len, D_MODEL), BF16),
            jax.ShapeDtypeStruct((s_len, LANES), BF16),
            jax.ShapeDtypeStruct((8, D_MODEL), F32), jax.ShapeDtypeStruct((8, LANES), F32),
        ],
        compiler_params=_cparams(56),
        name="inproj_bwd",
    )(x1, dx2, g, dzf, dlogf, logf, dzs, dgates, w_fox, w_fl, w_sb, w_gates)


def _split_w_in(w_in):
    o = 3 * ATT_W
    w_fox = w_in[:, 0:o]
    w_fl = jnp.pad(w_in[:, o:o + N_HEADS], ((0, 0), (0, LANES - N_HEADS)))
    w_sb = w_in[:, o + N_HEADS:2 * o + N_HEADS]
    w_gates = w_in[:, 2 * o + N_HEADS:]
    return w_fox, w_fl, w_sb, w_gates


def _local_grads(x, p, target, small, full, pending=None, send_early=None):
    blk = ATT_BLOCK
    bd, bd_t = _head_sum_matrices()
    full = dict(full)
    late = list(pending) if pending else []

    x1, a1, b1, *gathered = _ffn_fwd(x, small["ffn1_norm"], full["ffn1_w_gate"], full["ffn1_w_up"],
                                     full["ffn1_w_down"], gather=[pending[k] for k in late])
    for k, gth in zip(late, gathered):
        full[k] = gth if k in KEPT_AS_SHARDS else _whole(k, gth)
    w_fox, w_fl, w_sb, w_gates = _split_w_in(full["w_in"])
    bias = jnp.pad(small["forget_bias"], ((0, 0), (0, LANES - N_HEADS)))
    qn = jnp.tile(small["q_norm"], (1, N_HEADS))
    kn = jnp.tile(small["k_norm"], (1, N_HEADS))
    fq, fk, f_qs, f_k, f_v, logf, s_qs, s_k, s_v, gates = _inproj_fwd(
        x1, small["mix_norm"], w_fox, w_fl, w_sb, w_gates, bias, qn, kn, bd, bd_t)
    f_cum = _cumsum_rows(logf, reverse=False)
    f8 = f_cum[:, 0:N_HEADS]
    fw = jnp.repeat(f8, HEAD_DIM, axis=1)
    ft4 = _pair_rows_t(f8, blk)
    f_k3, f_v3 = _blocked_rows(f_k, blk), _blocked_rows(f_v, blk)
    y_fox, lse, f_first = _fox_fwd(f_qs, f_k3, f_v3, fw, ft4, _key_norm_bound(f_k))
    s_k3, s_v3 = _blocked_rows(s_k, blk), _blocked_rows(s_v, blk)
    y_sb, s_rtot, s_first = _sb_fwd(s_qs, s_k3, s_v3)
    x2 = _merge_fwd(x1, gates, y_fox, y_sb, full["w_branch_fox"], full["w_branch_sb"], full["w_out"])
    x3, a2, b2 = _ffn_fwd(x2, small["ffn2_norm"], full["ffn2_w_gate"], full["ffn2_w_up"], full["ffn2_w_down"])

    dx3, du_ple, dt_ple, hn_ple, dg_ple, loss_sum = _ple_loss(
        x3, p, small["ple_norm"], full["w_ple_gate"], full["w_ple_proj"], target)
    dx2, u2, da2, db2, h_ffn2, d3_bf, dg_ffn2 = _ffn_bwd(
        x2, dx3, small["ffn2_norm"], a2, b2, full["ffn2_w_gate"], full["ffn2_w_up"], full["ffn2_w_down"])
    dy_fox, dy_sb, dgates, d_of, d_os, merged, d2_bf = _merge_bwd(
        dx2, gates, y_fox, y_sb, full["w_branch_fox"], full["w_branch_sb"], full["w_out"])

    f_dqs, dfq_w, f_dkt4, f_dvt4, dft4 = _fox_bwd(f_qs, f_k3, f_v3, dy_fox, y_fox, lse, fw, ft4, f_first)
    s_dqs, s_dkt4, s_dvt4 = _sb_bwd(s_qs, s_k3, s_v3, dy_sb, s_rtot, s_first)

    dzf, dqn8, dkn8 = _qknorm_bwd(fq, fk, f_dqs, _unblocked_t(f_dkt4), _unblocked_t(f_dvt4), qn, kn, bd, bd_t)
    dzs = jnp.concatenate([s_dqs * QK_SCALE, _unblocked_t(s_dkt4), _unblocked_t(s_dvt4)], axis=1).astype(BF16)
    df8 = _unpair_rows_t(dft4) + dfq_w[:, ::HEAD_DIM]
    dlogf = _cumsum_rows(jnp.pad(df8, ((0, 0), (0, LANES - N_HEADS))), reverse=True)
    dx1, h_mix, dfl, dg_mix, dbias8 = _inproj_bwd(
        x1, dx2, small["mix_norm"], dzf, dlogf, logf, dzs, dgates, w_fox, w_fl, w_sb, w_gates)

    one = lambda t: t[None]
    gw = {}
    gw["ffn2_w_gate"] = _wgrad(da2, one(h_ffn2), name="wgrad_ffn2_gate")
    gw["ffn2_w_up"] = _wgrad(db2, one(h_ffn2), name="wgrad_ffn2_up")
    gw["ffn2_w_down"] = _wgrad(u2, one(d3_bf), scale=0.5, name="wgrad_ffn2_down")
    g_fox = _wgrad(one(h_mix), one(dzf), name="wgrad_in_fox")[0]
    g_fl = _wgrad(one(h_mix), one(dfl), name="wgrad_in_forget")[0]
    g_sb = _wgrad(one(h_mix), one(dzs), name="wgrad_in_sb")[0]
    g_gt = _wgrad(one(h_mix), one(dgates), name="wgrad_in_gates")[0]
    gw["w_in"] = jnp.concatenate([g_fox, g_fl[:, 0:N_HEADS], g_sb, g_gt], axis=1)
    gw["w_branch_fox"] = _wgrad(one(y_fox), one(d_of), name="wgrad_branch_fox")[0]
    gw["w_branch_sb"] = _wgrad(one(y_sb), one(d_os), name="wgrad_branch_sb")[0]
    gw["w_out"] = _wgrad(one(merged), one(d2_bf), name="wgrad_out")[0]
    gw["w_ple_gate"] = _wgrad(one(hn_ple), one(du_ple), name="wgrad_ple_gate")[0]
    gw["w_ple_proj"] = _wgrad(one(p), one(dt_ple), name="wgrad_ple_proj")[0]

    sent_names, to_send = send_early(gw) if send_early else ([], [])
    grad_x, u1, da1, db1, h_ffn1, d1_bf, dg_ffn1, *landed = _ffn_bwd(
        x, dx1, small["ffn1_norm"], a1, b1, full["ffn1_w_gate"], full["ffn1_w_up"], full["ffn1_w_down"],
        scatter=to_send)
    gw["ffn1_w_gate"] = _wgrad(da1, one(h_ffn1), name="wgrad_ffn1_gate")
    gw["ffn1_w_up"] = _wgrad(db1, one(h_ffn1), name="wgrad_ffn1_up")
    gw["ffn1_w_down"] = _wgrad(u1, one(d1_bf), scale=0.5, name="wgrad_ffn1_down")

    fold = lambda t: jnp.sum(t[0:1].reshape(N_HEADS, HEAD_DIM), axis=0, keepdims=True)
    gs = {
        "ffn1_norm": dg_ffn1[0:1], "mix_norm": dg_mix[0:1], "ffn2_norm": dg_ffn2[0:1], "ple_norm": dg_ple[0:1],
        "forget_bias": dbias8[0:1, 0:N_HEADS], "q_norm": fold(dqn8), "k_norm": fold(dkn8),
    }
    return loss_sum, grad_x, gw, gs, dict(zip(sent_names, landed))


def _position():
    return lax.axis_index("x"), lax.axis_index("y"), lax.axis_index("c")


def _other_chips(x, y):
    return [(1 - x, y), (x, 1 - y), (1 - x, 1 - y)]


ANY = pl.BlockSpec(memory_space=pl.ANY)


def _place_own_shard(w, q):
    rows, cols = w.shape
    tr = _row_block(rows, cols * 4, budget=2 * MIB)

    def body(q_ref, w_ref, o_ref):
        o_ref[0] = w_ref[...].astype(BF16)

    return pl.pallas_call(
        body,
        grid_spec=pltpu.PrefetchScalarGridSpec(
            num_scalar_prefetch=1,
            grid=(rows // tr,),
            in_specs=[pl.BlockSpec((tr, cols), lambda i, q_ref: (i, 0))],
            out_specs=pl.BlockSpec((1, tr, cols), lambda i, q_ref: (q_ref[0], i, 0)),
        ),
        out_shape=jax.ShapeDtypeStruct((N_CHIPS, rows, cols), BF16),
        name="place_own_shard",
    )(q, w)


def _gather_semaphores(n):
    return [pltpu.SemaphoreType.DMA((6 * n,)), pltpu.SemaphoreType.DMA((6 * n,))]


def _gather_steps(bufs, send_sems, recv_sems):
    n = len(bufs)
    x, y, c = _position()
    q = 2 * x + y
    chips = _other_chips(x, y)
    sibling = (x, y, 1 - c)

    def half(a, slot, which):
        r2 = bufs[a].shape[1] // 2
        return bufs[a].at[slot, pl.ds(which * r2, r2), :]

    def copy(a, k, region, to):
        return pltpu.make_async_remote_copy(
            src_ref=region, dst_ref=region, send_sem=send_sems.at[6 * a + k], recv_sem=recv_sems.at[6 * a + k],
            device_id=to, device_id_type=MESH)

    def to_chip(a, k):
        tx, ty = chips[k]
        return copy(a, k, half(a, q, c), (tx, ty, c))

    def to_sibling(a, k):
        tx, ty = chips[k]
        return copy(a, 3 + k, half(a, 2 * tx + ty, c), sibling)

    def start():
        for a in range(n):
            for k in range(3):
                to_chip(a, k).start()

    def finish():
        for a in range(n):
            for k, (tx, ty) in enumerate(chips):
                copy(a, k, half(a, 2 * tx + ty, c), (tx, ty, c)).wait_recv()
                to_sibling(a, k).start()
        for a in range(n):
            for k, (tx, ty) in enumerate(chips):
                copy(a, 3 + k, half(a, 2 * tx + ty, 1 - c), sibling).wait_recv()
        for a in range(n):
            for k in range(3):
                to_chip(a, k).wait_send()
                to_sibling(a, k).wait_send()

    return start, finish


def _allgather_weights(slots):
    n = len(slots)

    def body(*refs):
        start, finish = _gather_steps(refs[n:2 * n], *refs[2 * n:])
        start()
        finish()

    return pl.pallas_call(
        body,
        in_specs=[ANY] * n,
        out_specs=[ANY] * n,
        out_shape=[jax.ShapeDtypeStruct(s.shape, s.dtype) for s in slots],
        input_output_aliases={a: a for a in range(n)},
        scratch_shapes=_gather_semaphores(n),
        name="allgather_weights",
    )(*slots)


def _exchange_pair_halves(grads):
    n = len(grads)

    def body(*refs):
        ins, outs = refs[0:n], refs[n:2 * n]
        send_sems, recv_sems = refs[2 * n:]
        x, y, c = _position()
        copies = []
        for a in range(n):
            r2 = grads[a].shape[1] // 2
            cp = pltpu.make_async_remote_copy(
                src_ref=ins[a].at[:, pl.ds((1 - c) * r2, r2), :], dst_ref=outs[a],
                send_sem=send_sems.at[a], recv_sem=recv_sems.at[a], device_id=(x, y, 1 - c), device_id_type=MESH)
            cp.start()
            copies.append(cp)
        for cp in copies:
            cp.wait()

    return pl.pallas_call(
        body,
        in_specs=[ANY] * n,
        out_specs=[ANY] * n,
        out_shape=[jax.ShapeDtypeStruct((N_CHIPS, g.shape[1] // 2, g.shape[2]), g.dtype) for g in grads],
        scratch_shapes=[pltpu.SemaphoreType.DMA((n,)), pltpu.SemaphoreType.DMA((n,))],
        name="rs_pair_exchange",
    )(*grads)


def _scatter_semaphores(n):
    return [pltpu.SemaphoreType.DMA((3 * n,)), pltpu.SemaphoreType.DMA((3 * n,)), pltpu.SemaphoreType.DMA((n,))]


def _scatter_steps(ins, outs, send_sems, recv_sems, local_sems):
    n = len(ins)
    x, y, c = _position()
    q = 2 * x + y
    chips = _other_chips(x, y)

    def own(a):
        return pltpu.make_async_copy(ins[a].at[q], outs[a].at[q], local_sems.at[a])

    def to_chip(a, k):
        tx, ty = chips[k]
        return pltpu.make_async_remote_copy(
            src_ref=ins[a].at[2 * tx + ty], dst_ref=outs[a].at[q],
            send_sem=send_sems.at[3 * a + k], recv_sem=recv_sems.at[3 * a + k],
            device_id=(tx, ty, c), device_id_type=MESH)

    def start():
        for a in range(n):
            own(a).start()
            for k in range(3):
                to_chip(a, k).start()

    def finish():
        for a in range(n):
            own(a).wait()
            for k in range(3):
                to_chip(a, k).wait()

    return start, finish


def _scatter_to_owner_chips(pairs):
    n = len(pairs)

    def body(*refs):
        start, finish = _scatter_steps(refs[0:n], refs[n:2 * n], *refs[2 * n:])
        start()
        finish()

    return pl.pallas_call(
        body,
        in_specs=[ANY] * n,
        out_specs=[ANY] * n,
        out_shape=[jax.ShapeDtypeStruct(p.shape, p.dtype) for p in pairs],
        scratch_shapes=_scatter_semaphores(n),
        name="rs_scatter",
    )(*pairs)


def _join_halves(shards):
    n = len(shards)

    def body(*refs):
        bufs = refs[n:2 * n]
        send_sems, recv_sems = refs[2 * n:]
        x, y, c = _position()
        started = []
        for a in range(n):
            r2 = shards[a].shape[0] // 2
            mine = bufs[a].at[pl.ds(c * r2, r2), :]
            cp = pltpu.make_async_remote_copy(
                src_ref=mine, dst_ref=mine, send_sem=send_sems.at[a], recv_sem=recv_sems.at[a],
                device_id=(x, y, 1 - c), device_id_type=MESH)
            cp.start()
            started.append(cp)
        for cp in started:
            cp.wait()

    return pl.pallas_call(
        body,
        in_specs=[ANY] * n,
        out_specs=[ANY] * n,
        out_shape=[jax.ShapeDtypeStruct(t.shape, t.dtype) for t in shards],
        input_output_aliases={a: a for a in range(n)},
        scratch_shapes=[pltpu.SemaphoreType.DMA((n,)), pltpu.SemaphoreType.DMA((n,))],
        name="rs_join_halves",
    )(*shards)


def _add_pair(g, got, c):
    _, r2, cols = got.shape

    def body(c_ref, g_ref, got_ref, o_ref):
        o_ref[...] = (g_ref[...].astype(F32) + got_ref[...].astype(F32)).astype(BF16)

    spec = pl.BlockSpec((1, r2, cols), lambda s, c_ref: (s, 0, 0))
    return pl.pallas_call(
        body,
        grid_spec=pltpu.PrefetchScalarGridSpec(
            num_scalar_prefetch=1,
            grid=(N_CHIPS,),
            in_specs=[pl.BlockSpec((1, r2, cols), lambda s, c_ref: (s, c_ref[0], 0)), spec],
            out_specs=spec,
        ),
        out_shape=jax.ShapeDtypeStruct(got.shape, BF16),
        name="rs_add_pair",
    )(c, g, got)


def _add_chips(parts, c):
    _, r2, cols = parts.shape

    def body(c_ref, p0, p1, p2, p3, o_ref):
        o_ref[...] = ((p0[0].astype(F32) + p1[0].astype(F32)) + p2[0].astype(F32)) + p3[0].astype(F32)

    specs = [pl.BlockSpec((1, r2, cols), functools.partial(lambda i, c_ref, s: (s, 0, 0), s=s))
             for s in range(N_CHIPS)]
    return pl.pallas_call(
        body,
        grid_spec=pltpu.PrefetchScalarGridSpec(
            num_scalar_prefetch=1,
            grid=(1,),
            in_specs=specs,
            out_specs=pl.BlockSpec((r2, cols), lambda i, c_ref: (c_ref[0], 0)),
        ),
        out_shape=jax.ShapeDtypeStruct((2 * r2, cols), F32),
        name="rs_add_chips",
    )(c, parts, parts, parts, parts)


def _allreduce_small(part):
    shape = part.shape

    def body(in_ref, out_ref, gather_ref, send_sems, recv_sems):
        x, y, c = _position()
        me = 4 * x + 2 * y + c
        relations = [(a, b, d) for a in (0, 1) for b in (0, 1) for d in (0, 1)][1:]
        flip = lambda v, f: 1 - v if f else v
        copies = []
        for k, (a, b, d) in enumerate(relations):
            cp = pltpu.make_async_remote_copy(
                src_ref=in_ref, dst_ref=gather_ref.at[me], send_sem=send_sems.at[k], recv_sem=recv_sems.at[k],
                device_id=(flip(x, a), flip(y, b), flip(c, d)), device_id_type=MESH)
            cp.start()
            copies.append(cp)
        gather_ref[me] = in_ref[...]
        for cp in copies:
            cp.wait()
        total = gather_ref[0]
        for dev in range(1, 8):
            total = total + gather_ref[dev]
        out_ref[...] = total

    vmem = pl.BlockSpec(memory_space=pltpu.VMEM)
    return pl.pallas_call(
        body,
        in_specs=[vmem],
        out_specs=vmem,
        out_shape=jax.ShapeDtypeStruct(shape, F32),
        scratch_shapes=[pltpu.VMEM((8,) + shape, F32), pltpu.SemaphoreType.DMA((7,)), pltpu.SemaphoreType.DMA((7,))],
        name="allreduce_small",
    )(part)


def _adamw(w, g, m, v):
    rows, cols = w.shape
    tr = _row_block(rows, cols * 4, budget=MIB)
    c1 = 1.0 / (1.0 - ADAM_B1 ** ADAM_STEP)
    c2 = 1.0 / (1.0 - ADAM_B2 ** ADAM_STEP)

    def body(w_ref, g_ref, m_ref, v_ref, d_ref, nm_ref, nv_ref):
        g_ = g_ref[...]
        nm = ADAM_B1 * m_ref[...] + (1.0 - ADAM_B1) * g_
        nv = ADAM_B2 * v_ref[...] + (1.0 - ADAM_B2) * (g_ * g_)
        nm_ref[...] = nm
        nv_ref[...] = nv
        d_ref[...] = -ADAM_LR * ((nm * c1) / (jnp.sqrt(nv * c2) + ADAM_EPS) + ADAM_WD * w_ref[...])

    spec = pl.BlockSpec((tr, cols), lambda i: (i, 0))
    out = jax.ShapeDtypeStruct((rows, cols), F32)
    return pl.pallas_call(
        body,
        grid=(rows // tr,),
        in_specs=[spec] * 4,
        out_specs=[spec] * 3,
        out_shape=[out] * 3,
        name="adamw",
    )(w, g, m, v)


BIG = ["ffn1_w_gate", "ffn1_w_up", "ffn1_w_down", "w_in", "w_branch_fox", "w_branch_sb", "w_out",
       "ffn2_w_gate", "ffn2_w_up", "ffn2_w_down", "w_ple_gate", "w_ple_proj"]
SMALL = ["ffn1_norm", "mix_norm", "ffn2_norm", "ple_norm", "forget_bias", "q_norm", "k_norm"]
COLUMN_SHARDED = ["ffn1_w_gate", "ffn1_w_up", "w_in", "w_branch_fox", "w_branch_sb",
                  "ffn2_w_gate", "ffn2_w_up", "w_ple_proj"]
KEPT_AS_SHARDS = ["ffn1_w_gate", "ffn1_w_up", "ffn1_w_down", "ffn2_w_gate", "ffn2_w_up", "ffn2_w_down"]
WORKED_TRANSPOSED = ["ffn1_w_gate", "ffn1_w_up", "ffn2_w_gate", "ffn2_w_up"]
NEEDED_FIRST = ["ffn1_w_gate", "ffn1_w_up", "ffn1_w_down"]
ORDER = ["ffn1_norm", "ffn1_w_gate", "ffn1_w_up", "ffn1_w_down", "mix_norm", "w_in", "forget_bias", "q_norm",
         "k_norm", "w_branch_fox", "w_branch_sb", "w_out", "ffn2_norm", "ffn2_w_gate", "ffn2_w_up",
         "ffn2_w_down", "ple_norm", "w_ple_gate", "w_ple_proj"]
SMALL_ROWS = {"ffn1_norm": 0, "mix_norm": 1, "ffn2_norm": 2, "ple_norm": 3}
SMALL_COLS = {"forget_bias": (0, N_HEADS), "q_norm": (N_HEADS, HEAD_DIM), "k_norm": (N_HEADS + HEAD_DIM, HEAD_DIM)}
LOSS_ROW = 5


def _stored(name, a):
    return jnp.swapaxes(a[0], 0, 1) if name in WORKED_TRANSPOSED else a[0]


def _returned(name, a):
    return (jnp.swapaxes(a, 0, 1) if name in WORKED_TRANSPOSED else a)[None]


def _whole(name, gathered):
    if name in COLUMN_SHARDED:
        return jnp.concatenate([gathered[s] for s in range(N_CHIPS)], axis=1)
    return gathered.reshape(-1, gathered.shape[-1])


def _as_shards(name, whole):
    if name in COLUMN_SHARDED:
        k, n = whole.shape
        return whole.reshape(k, N_CHIPS, n // N_CHIPS).transpose(1, 0, 2)
    return whole.reshape(N_CHIPS, whole.shape[0] // N_CHIPS, whole.shape[1])


def _pack_small(values, extra=None):
    rows = [values[k] for k in ("ffn1_norm", "mix_norm", "ffn2_norm", "ple_norm")]
    tail = jnp.concatenate([values["forget_bias"], values["q_norm"], values["k_norm"]], axis=1)
    rows.append(jnp.pad(tail, ((0, 0), (0, D_MODEL - tail.shape[1]))))
    packed = jnp.concatenate(rows + [jnp.zeros((3, D_MODEL), F32)], axis=0)
    if extra is not None:
        packed = packed.at[LOSS_ROW, 0].set(extra)
    return packed


def _unpack_small(packed):
    out = {k: packed[r:r + 1] for k, r in SMALL_ROWS.items()}
    for k, (start, size) in SMALL_COLS.items():
        out[k] = packed[4:5, start:start + size]
    return out


def kernel(x, p, ffn1_norm, ffn1_w_gate, ffn1_w_up, ffn1_w_down, mix_norm, w_in, forget_bias, q_norm, k_norm, w_branch_fox, w_branch_sb, w_out, ffn2_norm, ffn2_w_gate, ffn2_w_up, ffn2_w_down, ple_norm, w_ple_gate, w_ple_proj, loss_target, m_ffn1_norm, m_ffn1_w_gate, m_ffn1_w_up, m_ffn1_w_down, m_mix_norm, m_w_in, m_forget_bias, m_q_norm, m_k_norm, m_w_branch_fox, m_w_branch_sb, m_w_out, m_ffn2_norm, m_ffn2_w_gate, m_ffn2_w_up, m_ffn2_w_down, m_ple_norm, m_w_ple_gate, m_w_ple_proj, v_ffn1_norm, v_ffn1_w_gate, v_ffn1_w_up, v_ffn1_w_down, v_mix_norm, v_w_in, v_forget_bias, v_q_norm, v_k_norm, v_w_branch_fox, v_w_branch_sb, v_w_out, v_ffn2_norm, v_ffn2_w_gate, v_ffn2_w_up, v_ffn2_w_down, v_ple_norm, v_w_ple_gate, v_w_ple_proj):
    args = dict(locals())
    weights = {k: args[k] for k in ORDER}
    moments_m = {k: args["m_" + k] for k in ORDER}
    moments_v = {k: args["v_" + k] for k in ORDER}

    c_idx = lax.axis_index("c").astype(jnp.int32).reshape(1)
    q_idx = (2 * lax.axis_index("x") + lax.axis_index("y")).astype(jnp.int32).reshape(1)
    own = {k: _place_own_shard(_stored(k, weights[k]), q_idx) for k in BIG}
    full = dict(zip(NEEDED_FIRST, _allgather_weights([own[k] for k in NEEDED_FIRST])))
    pending = {k: own[k] for k in BIG if k not in NEEDED_FIRST}
    small = {k: weights[k] for k in SMALL}

    def pair_sums(names, gw):
        slots = [gw[k] if k in KEPT_AS_SHARDS else _as_shards(k, gw[k]) for k in names]
        from_core = _exchange_pair_halves(slots)
        return [_add_pair(g, got, c_idx) for g, got in zip(slots, from_core)]

    late = [k for k in BIG if k not in NEEDED_FIRST]
    loss_sum, grad_x, gw, gs, parts = _local_grads(
        x[0], p[0, 0], loss_target[0], small, full, pending, lambda early: (late, pair_sums(late, early)))

    parts.update(zip(NEEDED_FIRST, _scatter_to_owner_chips(pair_sums(NEEDED_FIRST, gw))))
    grads_big = dict(zip(BIG, _join_halves([_add_chips(parts[k], c_idx) for k in BIG])))
    reduced = _allreduce_small(_pack_small(gs, extra=loss_sum[0, 0]))
    grads_small = _unpack_small(reduced)
    loss = reduced[LOSS_ROW, 0]

    grads, deltas, new_m, new_v = {}, {}, {}, {}
    for k in BIG:
        d, nm, nv = _adamw(_stored(k, weights[k]), grads_big[k], _stored(k, moments_m[k]), _stored(k, moments_v[k]))
        grads[k], deltas[k], new_m[k], new_v[k] = (_returned(k, t) for t in (grads_big[k], d, nm, nv))
    d_s, nm_s, nv_s = _adamw(_pack_small({k: weights[k] for k in SMALL}), reduced,
                             _pack_small({k: moments_m[k] for k in SMALL}),
                             _pack_small({k: moments_v[k] for k in SMALL}))
    for k in SMALL:
        grads[k] = grads_small[k]
    for name, packed in (("d", d_s), ("m", nm_s), ("v", nv_s)):
        target = {"d": deltas, "m": new_m, "v": new_v}[name]
        target.update(_unpack_small(packed))

    return (loss, grad_x[None], *[grads[k] for k in ORDER], *[deltas[k] for k in ORDER],
            *[new_m[k] for k in ORDER], *[new_v[k] for k in ORDER])
```

```python
import functools

import jax
import jax.numpy as jnp
from jax import lax
from jax.experimental import pallas as pl
from jax.experimental.pallas import tpu as pltpu

F32 = jnp.float32
BF16 = jnp.bfloat16

D_MODEL = 1024
D_FF = 2816
N_CHIPS = 4
FF_SHARD = D_FF // N_CHIPS
FFN_CHUNKS = 2
HEAD_DIM = 64
N_HEADS = 8
ATT_W = N_HEADS * HEAD_DIM
PAIR_W = 2 * HEAD_DIM
N_PAIRS = N_HEADS // 2
PLE_DIM = 256
IN_WIDTH = 3 * ATT_W + N_HEADS + 3 * ATT_W + 2 * D_MODEL
EPS = 1e-6
QK_SCALE = HEAD_DIM ** -0.5
LANES = 128
ATT_BLOCK = 256
ATT_Q_BLOCK = 512
NEG_BIG = -1e30
EXP_UNDERFLOW = 110.0

ADAM_LR = 0.001
ADAM_B1 = 0.9
ADAM_B2 = 0.999
ADAM_EPS = 1e-08
ADAM_WD = 0.01
ADAM_STEP = 10

MESH = pl.DeviceIdType.MESH
MIB = 1024 * 1024


def _cparams(vmem_mib=48):
    return pltpu.CompilerParams(vmem_limit_bytes=vmem_mib * MIB)


def _dot(a, b):
    return jnp.dot(a, b, preferred_element_type=F32)


def _dot_tn(a, b):
    return lax.dot_general(a, b, (((0,), (0,)), ((), ())), preferred_element_type=F32)


def _dot_nt(a, b):
    return lax.dot_general(a, b, (((1,), (1,)), ((), ())), preferred_element_type=F32)


def _sigmoid(x):
    return 1.0 / (1.0 + jnp.exp(-x))


def _split2(x):
    hi = x.astype(BF16)
    lo = (x - hi.astype(F32)).astype(BF16)
    return hi, lo


def _dot_split2(x, m):
    hi, lo = _split2(x)
    return _dot(hi, m) + _dot(lo, m)


def _split3(x):
    hi = x.astype(BF16)
    rest = x - hi.astype(F32)
    mid = rest.astype(BF16)
    lo = (rest - mid.astype(F32)).astype(BF16)
    return hi, mid, lo


def _rms(x):
    r = lax.rsqrt(jnp.mean(x * x, axis=-1, keepdims=True) + EPS)
    return x * r, r


def _rms_bwd(dh, xn, r, g):
    dxn = dh * g
    return r * (dxn - xn * jnp.mean(dxn * xn, axis=-1, keepdims=True))


def _colsum(x):
    return jnp.sum(x, axis=0, keepdims=True)


def _row_block(rows, row_bytes, budget):
    best = None
    for t in range(8, rows + 1, 8):
        if rows % t == 0 and t * row_bytes <= budget:
            best = t
    return best if best is not None else rows


def _ffn_fwd(x, g, wg, wu, wd, gather=(), tm=512):
    s_len = x.shape[0]
    n = len(gather)
    steps = s_len // tm

    def body(x_ref, g_ref, wg_ref, wu_ref, wd_ref, *rest):
        o_ref, a_ref, b_ref = rest[n:n + 3]
        h_s, acc_s = rest[2 * n + 3:2 * n + 5]
        i = pl.program_id(0)
        j = pl.program_id(1)
        if n:
            start, finish = _gather_steps(rest[n + 3:2 * n + 3], *rest[2 * n + 5:])
            pl.when((i == 0) & (j == 0))(start)

        @pl.when(j == 0)
        def _():
            xn, _ = _rms(x_ref[...])
            h_s[...] = (xn * g_ref[...]).astype(BF16)
            acc_s[...] = jnp.zeros_like(acc_s)

        chunks = [pl.ds(r * (tm // FFN_CHUNKS), tm // FFN_CHUNKS) for r in range(FFN_CHUNKS)]
        pre = [(_dot_nt(h_s[rows, :], wg_ref[0]), _dot_nt(h_s[rows, :], wu_ref[0])) for rows in chunks]
        us = []
        for rows, (a, b) in zip(chunks, pre):
            a_ref[0, rows, :] = a.astype(BF16)
            b_ref[0, rows, :] = b.astype(BF16)
            us.append((a * _sigmoid(a) * b).astype(BF16))
        for rows, u in zip(chunks, us):
            acc_s[rows, :] += _dot(u, wd_ref[0])

        @pl.when(j == N_CHIPS - 1)
        def _():
            o_ref[...] = x_ref[...] + 0.5 * acc_s[...]

        if n:
            pl.when((i == steps - 1) & (j == N_CHIPS - 1))(finish)

    return pl.pallas_call(
        body,
        grid=(steps, N_CHIPS),
        in_specs=[
            pl.BlockSpec((tm, D_MODEL), lambda i, j: (i, 0)),
            pl.BlockSpec((1, D_MODEL), lambda i, j: (0, 0)),
            pl.BlockSpec((1, FF_SHARD, D_MODEL), lambda i, j: (j, 0, 0)),
            pl.BlockSpec((1, FF_SHARD, D_MODEL), lambda i, j: (j, 0, 0)),
            pl.BlockSpec((1, FF_SHARD, D_MODEL), lambda i, j: (j, 0, 0)),
        ] + [ANY] * n,
        out_specs=[pl.BlockSpec((tm, D_MODEL), lambda i, j: (i, 0)),
                   pl.BlockSpec((1, tm, FF_SHARD), lambda i, j: (j, i, 0)),
                   pl.BlockSpec((1, tm, FF_SHARD), lambda i, j: (j, i, 0))] + [ANY] * n,
        out_shape=[jax.ShapeDtypeStruct((s_len, D_MODEL), F32),
                   jax.ShapeDtypeStruct((N_CHIPS, s_len, FF_SHARD), BF16),
                   jax.ShapeDtypeStruct((N_CHIPS, s_len, FF_SHARD), BF16)]
        + [jax.ShapeDtypeStruct(s.shape, s.dtype) for s in gather],
        input_output_aliases={5 + a: 3 + a for a in range(n)},
        scratch_shapes=[pltpu.VMEM((tm, D_MODEL), BF16), pltpu.VMEM((tm, D_MODEL), F32)]
        + (_gather_semaphores(n) if n else []),
        compiler_params=_cparams(48),
        name="ffn_fwd_gathering" if n else "ffn_fwd",
    )(x, g, wg, wu, wd, *gather)


def _ffn_bwd(x, d, g, a_pre, b_pre, wg, wu, wd, scatter=(), tm=512):
    s_len = x.shape[0]
    nb = s_len // tm
    n = len(scatter)

    def body(x_ref, d_ref, g_ref, a_ref, b_ref, wg_ref, wu_ref, wd_ref, *rest):
        dx_ref, u_ref, da_ref, db_ref, h_ref, dbf_ref, dg_ref = rest[n:n + 7]
        dbf_s, dh_s = rest[2 * n + 7:2 * n + 9]
        i = pl.program_id(0)
        j = pl.program_id(1)
        if n:
            start, finish = _scatter_steps(rest[0:n], rest[n + 7:2 * n + 7], *rest[2 * n + 9:])
            pl.when((i == 0) & (j == 0))(start)

        @pl.when(j == 0)
        def _():
            xn, _ = _rms(x_ref[...])
            h_ref[...] = (xn * g_ref[...]).astype(BF16)
            dbf = d_ref[...].astype(BF16)
            dbf_s[...] = dbf
            dbf_ref[...] = dbf
            dh_s[...] = jnp.zeros_like(dh_s)

        @pl.when((i == 0) & (j == 0))
        def _():
            dg_ref[...] = jnp.zeros_like(dg_ref)

        chunks = [pl.ds(r * (tm // FFN_CHUNKS), tm // FFN_CHUNKS) for r in range(FFN_CHUNKS)]
        dus = [0.5 * _dot_nt(dbf_s[rows, :], wd_ref[0]) for rows in chunks]
        das, dbs = [], []
        for rows, du in zip(chunks, dus):
            a = a_ref[0, rows, :].astype(F32)
            b = b_ref[0, rows, :].astype(F32)
            s = _sigmoid(a)
            silu = a * s
            da = (du * b * (s * (1.0 + a * (1.0 - s)))).astype(BF16)
            db = (du * silu).astype(BF16)
            u_ref[0, rows, :] = (silu * b).astype(BF16)
            da_ref[0, rows, :] = da
            db_ref[0, rows, :] = db
            das.append(da)
            dbs.append(db)
        for rows, da, db in zip(chunks, das, dbs):
            dh_s[rows, :] += _dot(da, wg_ref[0]) + _dot(db, wu_ref[0])

        @pl.when(j == N_CHIPS - 1)
        def _():
            xn, r = _rms(x_ref[...])
            dh = dh_s[...]
            dx_ref[...] = d_ref[...] + _rms_bwd(dh, xn, r, g_ref[...])
            dg_ref[0:1, :] += _colsum(dh * xn)

        if n:
            pl.when((i == nb - 1) & (j == N_CHIPS - 1))(finish)

    row = lambda i, j: (i, 0)
    shard = lambda i, j: (j, 0, 0)
    act = lambda i, j: (j, i, 0)
    return pl.pallas_call(
        body,
        grid=(nb, N_CHIPS),
        in_specs=[
            pl.BlockSpec((tm, D_MODEL), row),
            pl.BlockSpec((tm, D_MODEL), row),
            pl.BlockSpec((1, D_MODEL), lambda i, j: (0, 0)),
            pl.BlockSpec((1, tm, FF_SHARD), act),
            pl.BlockSpec((1, tm, FF_SHARD), act),
            pl.BlockSpec((1, FF_SHARD, D_MODEL), shard),
            pl.BlockSpec((1, FF_SHARD, D_MODEL), shard),
            pl.BlockSpec((1, FF_SHARD, D_MODEL), shard),
        ] + [ANY] * n,
        out_specs=[
            pl.BlockSpec((tm, D_MODEL), row),
            pl.BlockSpec((1, tm, FF_SHARD), act),
            pl.BlockSpec((1, tm, FF_SHARD), act),
            pl.BlockSpec((1, tm, FF_SHARD), act),
            pl.BlockSpec((tm, D_MODEL), row),
            pl.BlockSpec((tm, D_MODEL), row),
            pl.BlockSpec((8, D_MODEL), lambda i, j: (0, 0)),
        ] + [ANY] * n,
        out_shape=[
            jax.ShapeDtypeStruct((s_len, D_MODEL), F32),
            jax.ShapeDtypeStruct((N_CHIPS, s_len, FF_SHARD), BF16),
            jax.ShapeDtypeStruct((N_CHIPS, s_len, FF_SHARD), BF16),
            jax.ShapeDtypeStruct((N_CHIPS, s_len, FF_SHARD), BF16),
            jax.ShapeDtypeStruct((s_len, D_MODEL), BF16),
            jax.ShapeDtypeStruct((s_len, D_MODEL), BF16),
            jax.ShapeDtypeStruct((8, D_MODEL), F32),
        ] + [jax.ShapeDtypeStruct(s.shape, s.dtype) for s in scatter],
        scratch_shapes=[
            pltpu.VMEM((tm, D_MODEL), BF16),
            pltpu.VMEM((tm, D_MODEL), F32),
        ] + (_scatter_semaphores(n) if n else []),
        compiler_params=_cparams(56),
        name="ffn_bwd_scattering" if n else "ffn_bwd",
    )(x, d, g, a_pre, b_pre, wg, wu, wd, *scatter)


def _wgrad(a, b, scale=1.0, name="wgrad"):
    na, s_len, k_dim = a.shape
    nb, _, n_dim = b.shape
    n = max(na, nb)
    ts = min(s_len, 1024)
    steps = s_len // ts

    def body(a_ref, b_ref, o_ref, acc_s):
        s = pl.program_id(1)

        @pl.when(s == 0)
        def _():
            acc_s[...] = jnp.zeros_like(acc_s)

        acc_s[...] += _dot_tn(a_ref[0].astype(BF16), b_ref[0].astype(BF16))

        @pl.when(s == steps - 1)
        def _():
            o_ref[0] = (acc_s[...] * scale).astype(BF16)

    a_map = (lambda m, s: (m, s, 0)) if na > 1 else (lambda m, s: (0, s, 0))
    b_map = (lambda m, s: (m, s, 0)) if nb > 1 else (lambda m, s: (0, s, 0))
    return pl.pallas_call(
        body,
        grid=(n, steps),
        in_specs=[pl.BlockSpec((1, ts, k_dim), a_map), pl.BlockSpec((1, ts, n_dim), b_map)],
        out_specs=pl.BlockSpec((1, k_dim, n_dim), lambda m, s: (m, 0, 0)),
        out_shape=jax.ShapeDtypeStruct((n, k_dim, n_dim), BF16),
        scratch_shapes=[pltpu.VMEM((k_dim, n_dim), F32)],
        compiler_params=_cparams(56),
        name=name,
    )(a, b)


def _head_sum_matrices():
    lane = lax.broadcasted_iota(jnp.int32, (ATT_W, LANES), 0) // HEAD_DIM
    col = lax.broadcasted_iota(jnp.int32, (ATT_W, LANES), 1)
    bd = (lane == col).astype(BF16)
    return bd, bd.T


def _head_mean(t, bd, bd_t):
    per_head = _dot_split2(t, bd) * (1.0 / HEAD_DIM)
    return _dot_split2(per_head, bd_t)


def _head_rms(x, bd, bd_t):
    per_head = _dot_split2(x * x, bd) * (1.0 / HEAD_DIM)
    r = lax.rsqrt(per_head + EPS)
    rw = _dot_split2(r, bd_t)
    return x * rw, rw


def _log_sigmoid(z):
    return jnp.minimum(z, 0.0) - jnp.log(1.0 + jnp.exp(-jnp.abs(z)))


def _inproj_fwd(x1, g, w_fox, w_fl, w_sb, w_gates, bias, qn, kn, bd, bd_t, tm=256):
    s_len = x1.shape[0]

    def body(x_ref, g_ref, wf_ref, wl_ref, ws_ref, wg_ref, bias_ref, qn_ref, kn_ref, bd_ref, bdt_ref,
             fq_ref, fk_ref, qs_ref, kf_ref, vf_ref, logf_ref, sq_ref, sk_ref, sv_ref, gates_ref):
        xn, _ = _rms(x_ref[...])
        h = (xn * g_ref[...]).astype(BF16)
        zf = _dot(h, wf_ref[...])
        fq = zf[:, 0:ATT_W]
        fk = zf[:, ATT_W:2 * ATT_W]
        fq_ref[...] = fq
        fk_ref[...] = fk
        bd_m = bd_ref[...]
        bdt_m = bdt_ref[...]
        fqn, _ = _head_rms(fq, bd_m, bdt_m)
        fkn, _ = _head_rms(fk, bd_m, bdt_m)
        qs_ref[...] = (fqn * qn_ref[...]).astype(BF16) * QK_SCALE
        kf_ref[...] = (fkn * kn_ref[...]).astype(BF16)
        vf_ref[...] = zf[:, 2 * ATT_W:3 * ATT_W].astype(BF16)
        logf_ref[...] = _log_sigmoid(_dot(h, wl_ref[...]) + bias_ref[...])
        zs = _dot(h, ws_ref[...])
        sq_ref[...] = zs[:, 0:ATT_W].astype(BF16) * QK_SCALE
        sk_ref[...] = zs[:, ATT_W:2 * ATT_W].astype(BF16)
        sv_ref[...] = zs[:, 2 * ATT_W:3 * ATT_W].astype(BF16)
        gates_ref[...] = _dot(h, wg_ref[...])

    row = lambda i: (i, 0)
    full = lambda i: (0, 0)
    att = lambda dt: jax.ShapeDtypeStruct((s_len, ATT_W), dt)
    return pl.pallas_call(
        body,
        grid=(s_len // tm,),
        in_specs=[
            pl.BlockSpec((tm, D_MODEL), row),
            pl.BlockSpec((1, D_MODEL), full),
            pl.BlockSpec((D_MODEL, 3 * ATT_W), full),
            pl.BlockSpec((D_MODEL, LANES), full),
            pl.BlockSpec((D_MODEL, 3 * ATT_W), full),
            pl.BlockSpec((D_MODEL, 2 * D_MODEL), full),
            pl.BlockSpec((1, LANES), full),
            pl.BlockSpec((1, ATT_W), full),
            pl.BlockSpec((1, ATT_W), full),
            pl.BlockSpec((ATT_W, LANES), full),
            pl.BlockSpec((LANES, ATT_W), full),
        ],
        out_specs=[
            pl.BlockSpec((tm, ATT_W), row), pl.BlockSpec((tm, ATT_W), row),
            pl.BlockSpec((tm, ATT_W), row), pl.BlockSpec((tm, ATT_W), row), pl.BlockSpec((tm, ATT_W), row),
            pl.BlockSpec((tm, LANES), row),
            pl.BlockSpec((tm, ATT_W), row), pl.BlockSpec((tm, ATT_W), row), pl.BlockSpec((tm, ATT_W), row),
            pl.BlockSpec((tm, 2 * D_MODEL), row),
        ],
        out_shape=[
            att(F32), att(F32), att(BF16), att(BF16), att(BF16),
            jax.ShapeDtypeStruct((s_len, LANES), F32),
            att(BF16), att(BF16), att(BF16),
            jax.ShapeDtypeStruct((s_len, 2 * D_MODEL), F32),
        ],
        compiler_params=_cparams(56),
        name="inproj_fwd",
    )(x1, g, w_fox, w_fl, w_sb, w_gates, bias, qn, kn, bd, bd_t)


def _tri(n, kind):
    r = lax.broadcasted_iota(jnp.int32, (n, n), 0)
    c = lax.broadcasted_iota(jnp.int32, (n, n), 1)
    m = {"row_ge_col": r >= c, "row_le_col": r <= c, "row_gt_col": r > c, "row_lt_col": r < c}[kind]
    return m.astype(BF16)


def _cumsum_rows(x, reverse, tm=256):
    s_len = x.shape[0]
    nb = s_len // tm
    tri = _tri(tm, "row_le_col" if reverse else "row_ge_col")
    edge = 0 if reverse else tm - 1

    def body(x_ref, tri_ref, o_ref, carry_s):
        @pl.when(pl.program_id(0) == 0)
        def _():
            carry_s[...] = jnp.zeros_like(carry_s)

        hi, mid, lo = _split3(x_ref[...])
        t = tri_ref[...]
        y = _dot(t, hi) + _dot(t, mid) + _dot(t, lo) + carry_s[...]
        o_ref[...] = y
        carry_s[...] = y[edge:edge + 1, :]

    order = (lambda i: (nb - 1 - i, 0)) if reverse else (lambda i: (i, 0))
    return pl.pallas_call(
        body,
        grid=(nb,),
        in_specs=[pl.BlockSpec((tm, LANES), order), pl.BlockSpec((tm, tm), lambda i: (0, 0))],
        out_specs=pl.BlockSpec((tm, LANES), order),
        out_shape=jax.ShapeDtypeStruct((s_len, LANES), F32),
        scratch_shapes=[pltpu.VMEM((1, LANES), F32)],
        name="cumsum_rev" if reverse else "cumsum_fwd",
    )(x, tri)


def _unblocked_t(t4):
    _, nb, _, blk = t4.shape
    return t4.transpose(1, 3, 0, 2).reshape(nb * blk, ATT_W)


def _blocked_rows(t, blk):
    return t.reshape(t.shape[0] // blk, blk, t.shape[1])


def _pair_rows_t(f8, blk):
    nb = f8.shape[0] // blk
    t = f8.reshape(nb, blk, N_PAIRS, 2).transpose(2, 0, 3, 1)
    return jnp.pad(t, ((0, 0), (0, 0), (0, 6), (0, 0)))


def _unpair_rows_t(t4):
    _, nb, _, blk = t4.shape
    return t4[:, :, 0:2, :].transpose(1, 3, 0, 2).reshape(nb * blk, N_HEADS)


def _head_masks(tq):
    lane = lax.broadcasted_iota(jnp.int32, (tq, PAIR_W), 1)
    return lane < HEAD_DIM


def _causal_mask(tq, tk, offset, strict):
    d = lax.broadcasted_iota(jnp.int32, (tq, tk), 1) - lax.broadcasted_iota(jnp.int32, (tq, tk), 0)
    return (d < offset) if strict else (d <= offset)


def _heads_of(ref, first):
    t = ref[...]
    zero = jnp.zeros_like(t)
    return [jnp.where(first, t, zero), jnp.where(first, zero, t)]


def _head_cols(ref):
    t = ref[...]
    return [t[:, 0:1], t[:, HEAD_DIM:HEAD_DIM + 1]]


def _att_specs(s_len):
    tq, tk = ATT_Q_BLOCK, ATT_BLOCK
    nq, nk = s_len // tq, s_len // tk
    return dict(
        nq=nq,
        q=pl.BlockSpec((tq, PAIR_W), lambda p, i: (i, p)),
        k_t=pl.BlockSpec((1, nk, PAIR_W, tk), lambda p, i: (p, 0, 0, 0)),
        k_rows=pl.BlockSpec((nk, tk, PAIR_W), lambda p, i: (0, 0, p)),
        f_t=pl.BlockSpec((1, nk, 8, tk), lambda p, i: (p, 0, 0, 0)),
        first=pl.BlockSpec((1, 1, 8, LANES), lambda p, i: (p, i, 0, 0)),
        wide=jax.ShapeDtypeStruct((s_len, ATT_W), F32),
        k_t_out=jax.ShapeDtypeStruct((N_PAIRS, nk, PAIR_W, tk), F32),
        f_t_out=jax.ShapeDtypeStruct((N_PAIRS, nk, 8, tk), F32),
        first_out=jax.ShapeDtypeStruct((N_PAIRS, nq, 8, LANES), F32),
        acc=pltpu.VMEM((2, tq, PAIR_W), F32),
    )


def _first_block(first_ref, limit):
    return jnp.clip(jnp.max(first_ref[0, 0]).astype(jnp.int32), 0, limit)


def _key_norm_bound(k):
    sq = jnp.sum(jnp.square(k.astype(F32)).reshape(k.shape[0], N_HEADS, HEAD_DIM), axis=-1)
    bound = jnp.sqrt(jnp.max(sq, axis=0)).reshape(N_PAIRS, 2)
    return jnp.broadcast_to(jnp.pad(bound, ((0, 0), (0, 6)))[:, :, None], (N_PAIRS, 8, LANES))


def _fox_fwd(qs, k3, v3, fw, ft4, kmax):
    sp = _att_specs(qs.shape[0])
    tq, tk = ATT_Q_BLOCK, ATT_BLOCK
    ratio = tq // tk

    def body(q_ref, k_ref, v_ref, fw_ref, ft_ref, kmax_ref, y_ref, lse_ref, first_ref, acc_ref, max_ref, sum_ref):
        i = pl.program_id(1)
        first = _head_masks(tq)
        qh = _heads_of(q_ref, first)
        fqh = _head_cols(fw_ref)
        acc_ref[...] = jnp.zeros_like(acc_ref)
        sum_ref[...] = jnp.zeros_like(sum_ref)
        max_ref[...] = jnp.full(max_ref.shape, NEG_BIG, F32)
        reach = []
        for n in range(2):
            qf = qh[n].astype(F32)
            reach.append(jnp.sqrt(jnp.sum(qf * qf, axis=-1, keepdims=True)) * kmax_ref[0, n:n + 1, 0:1] + fqh[n])

        def logits(j, shift, diag):
            k, fk = k_ref[j], ft_ref[0, j]
            raw = [_dot_nt(qh[n], k) for n in range(2)]
            out = []
            for n in range(2):
                s = raw[n] + (shift[n] - fk[n:n + 1, :])
                if diag:
                    s = jnp.where(_causal_mask(tq, tk, i * tq - j * tk, strict=False), s, NEG_BIG)
                out.append(s)
            return out

        def max_pass(j, diag):
            ss = logits(j, fqh, diag)
            for n in range(2):
                max_ref[n] = jnp.maximum(max_ref[n], ss[n])

        def sum_pass(j, shift, diag):
            ps = [jnp.exp(s) for s in logits(j, shift, diag)]
            v = v_ref[j]
            for n in range(2):
                sum_ref[n] += ps[n]
            for n in range(2):
                acc_ref[n] += _dot(ps[n].astype(BF16), v)

        for d in range(ratio):
            max_pass(ratio * i + d, True)

        def block_matters(j):
            gap = []
            for n in range(2):
                m_run = jnp.max(max_ref[n], axis=-1, keepdims=True)
                f_end = ft_ref[0, jnp.maximum(j, 0)][n:n + 1, tk - 1:tk]
                gap.append(jnp.max(reach[n] - m_run) - jnp.max(f_end))
            return (j >= 0) & (jnp.maximum(gap[0], gap[1]) > -EXP_UNDERFLOW)

        def walk_left(j):
            max_pass(j, False)
            return j - 1

        j_first = lax.while_loop(block_matters, walk_left, ratio * i - 1) + 1
        m = [jnp.max(max_ref[n], axis=-1, keepdims=True) for n in range(2)]
        shift = [fqh[n] - m[n] for n in range(2)]

        def one(j, c):
            sum_pass(j, shift, False)
            return c
        lax.fori_loop(j_first, ratio * i, one, 0)
        for d in range(ratio):
            sum_pass(ratio * i + d, shift, True)
        l = [jnp.sum(sum_ref[n], axis=-1, keepdims=True) for n in range(2)]
        y_ref[...] = jnp.where(first, acc_ref[0] / l[0], acc_ref[1] / l[1])
        lse_ref[...] = jnp.where(first, m[0] + jnp.log(l[0]), m[1] + jnp.log(l[1]))
        first_ref[...] = jnp.ones(first_ref.shape, F32) * j_first.astype(F32)

    tile = pltpu.VMEM((2, tq, tk), F32)
    return pl.pallas_call(
        body,
        grid=(N_PAIRS, sp["nq"]),
        in_specs=[sp["q"], sp["k_rows"], sp["k_rows"], sp["q"], sp["f_t"],
                  pl.BlockSpec((1, 8, LANES), lambda p, i: (p, 0, 0))],
        out_specs=[sp["q"], sp["q"], sp["first"]],
        out_shape=[sp["wide"], sp["wide"], sp["first_out"]],
        scratch_shapes=[sp["acc"], tile, tile],
        compiler_params=_cparams(56),
        name="fox_fwd",
    )(qs, k3, v3, fw, ft4, kmax)


def _fox_bwd(qs, k3, v3, dy, y, lse, fw, ft4, first_block):
    sp = _att_specs(qs.shape[0])
    tq, tk = ATT_Q_BLOCK, ATT_BLOCK
    ratio = tq // tk

    def body(q_ref, k_ref, v_ref, dy_ref, y_ref, lse_ref, fw_ref, ft_ref, first_ref,
             dq_ref, dfq_ref, dkt_ref, dvt_ref, dft_ref, acc_ref):
        i = pl.program_id(1)

        @pl.when(i == 0)
        def _():
            dkt_ref[...] = jnp.zeros_like(dkt_ref)
            dvt_ref[...] = jnp.zeros_like(dvt_ref)
            dft_ref[...] = jnp.zeros_like(dft_ref)

        first = _head_masks(tq)
        qh = _heads_of(q_ref, first)
        dyv = dy_ref[...]
        dyb = dyv.astype(BF16)
        zero = jnp.zeros_like(dyb)
        dyh = [jnp.where(first, dyb, zero), jnp.where(first, zero, dyb)]
        prod = dyv * y_ref[...]
        zf = jnp.zeros_like(prod)
        delta = [jnp.sum(jnp.where(first, prod, zf), axis=-1, keepdims=True),
                 jnp.sum(jnp.where(first, zf, prod), axis=-1, keepdims=True)]
        fqh = _head_cols(fw_ref)
        lseh = _head_cols(lse_ref)
        shift = [fqh[n] - lseh[n] for n in range(2)]
        acc_ref[...] = jnp.zeros_like(acc_ref)

        def block(j, rows, diag):
            mask = _causal_mask(tq, tk, i * tq - j * tk, strict=False) if diag else None
            k, v, fk = k_ref[j], v_ref[j], ft_ref[0, j]
            logits = [_dot_nt(qh[n], k) for n in range(2)]
            dps = [_dot_nt(dyh[n], v) for n in range(2)]
            pbs, dsbs, out = [], [], []
            for n in range(2):
                p = jnp.exp(logits[n] + (shift[n] - fk[n:n + 1, :]))
                if diag:
                    p = jnp.where(mask, p, 0.0)
                ds = p * (dps[n] - delta[n])
                pbs.append(p.astype(BF16))
                dsbs.append(ds.astype(BF16))
                out.append(rows[n] + jnp.sum(ds, axis=-1, keepdims=True))
                dft_ref[0, j, n:n + 1, :] -= _colsum(ds)
            for n in range(2):
                acc_ref[n] += _dot(dsbs[n], k)
            dkt_ref[0, j] += _dot_tn(qh[0], dsbs[0]) + _dot_tn(qh[1], dsbs[1])
            dvt_ref[0, j] += _dot_tn(dyh[0], pbs[0]) + _dot_tn(dyh[1], pbs[1])
            return tuple(out)

        rows = (jnp.zeros((tq, 1), F32),) * 2
        rows = lax.fori_loop(_first_block(first_ref, ratio * i), ratio * i, lambda j, c: block(j, c, False), rows)
        for d in range(ratio):
            rows = block(ratio * i + d, rows, True)
        dq_ref[...] = jnp.where(first, acc_ref[0], acc_ref[1])
        dfq_ref[...] = jnp.where(first, rows[0], rows[1])

    return pl.pallas_call(
        body,
        grid=(N_PAIRS, sp["nq"]),
        in_specs=[sp["q"], sp["k_rows"], sp["k_rows"], sp["q"], sp["q"], sp["q"], sp["q"], sp["f_t"], sp["first"]],
        out_specs=[sp["q"], sp["q"], sp["k_t"], sp["k_t"], sp["f_t"]],
        out_shape=[sp["wide"], sp["wide"], sp["k_t_out"], sp["k_t_out"], sp["f_t_out"]],
        scratch_shapes=[sp["acc"]],
        compiler_params=_cparams(56),
        name="fox_bwd",
    )(qs, k3, v3, dy, y, lse, fw, ft4, first_block)


SIGN_BIT = 0x80000000


def _sb_terms(z, mask, diag):
    neg_abs = pltpu.bitcast(pltpu.bitcast(z, jnp.uint32) | jnp.uint32(SIGN_BIT), F32)
    lb = jnp.minimum(z, 0.0) - jnp.log(1.0 + jnp.exp(neg_abs))
    l1m = lb - z
    if diag:
        l1m = jnp.where(mask, l1m, 0.0)
    return lb, l1m


def _dot_split2_stacked(x, m2):
    hi, lo = _split2(x)
    return _dot(jnp.concatenate([hi, lo], axis=1), m2)


def _tri_stacked(kind):
    t = _tri(ATT_BLOCK, kind)
    return jnp.concatenate([t, t], axis=0)


def _sb_fwd(qs, k3, v3):
    sp = _att_specs(qs.shape[0])
    tq, tk = ATT_Q_BLOCK, ATT_BLOCK
    ratio = tq // tk
    upper = _tri_stacked("row_gt_col")

    def body(q_ref, k_ref, v_ref, u_ref, y_ref, rtot_ref, first_ref, acc_ref):
        i = pl.program_id(1)
        first = _head_masks(tq)
        qh = _heads_of(q_ref, first)
        u = u_ref[...]
        acc_ref[...] = jnp.zeros_like(acc_ref)

        def block(j, rs, diag):
            mask = _causal_mask(tq, tk, i * tq - j * tk, strict=True) if diag else None
            k, v = k_ref[j], v_ref[j]
            logits = [_dot_nt(qh[n], k) for n in range(2)]
            terms = [_sb_terms(z, mask, diag) for z in logits]
            right = [_dot_split2_stacked(l1m, u) for _, l1m in terms]
            weights = []
            for n in range(2):
                a = jnp.exp(terms[n][0] + right[n] + rs[n])
                if diag:
                    a = jnp.where(mask, a, 0.0)
                weights.append(a.astype(BF16))
            for n in range(2):
                acc_ref[n] += _dot(weights[n], v)
            return tuple(rs[n] + jnp.sum(terms[n][1], axis=-1, keepdims=True) for n in range(2))

        rs = (jnp.zeros((tq, 1), F32),) * 2
        for d in range(ratio):
            rs = block(ratio * i + (ratio - 1 - d), rs, True)

        def block_matters(c):
            j, r0, r1 = c
            return (j >= 0) & (jnp.max(jnp.maximum(r0, r1)) > -EXP_UNDERFLOW)

        def walk_left(c):
            j, r0, r1 = c
            r0, r1 = block(j, (r0, r1), False)
            return j - 1, r0, r1

        j, r0, r1 = lax.while_loop(block_matters, walk_left, (ratio * i - 1, rs[0], rs[1]))
        y_ref[...] = jnp.where(first, acc_ref[0], acc_ref[1])
        rtot_ref[...] = jnp.where(first, r0, r1)
        first_ref[...] = jnp.ones(first_ref.shape, F32) * (j + 1).astype(F32)

    return pl.pallas_call(
        body,
        grid=(N_PAIRS, sp["nq"]),
        in_specs=[sp["q"], sp["k_rows"], sp["k_rows"], pl.BlockSpec((2 * tk, tk), lambda p, i: (0, 0))],
        out_specs=[sp["q"], sp["q"], sp["first"]],
        out_shape=[sp["wide"], sp["wide"], sp["first_out"]],
        scratch_shapes=[sp["acc"]],
        compiler_params=_cparams(56),
        name="sb_fwd",
    )(qs, k3, v3, upper)


def _sb_bwd(qs, k3, v3, dy, rtot, first_block):
    sp = _att_specs(qs.shape[0])
    tq, tk = ATT_Q_BLOCK, ATT_BLOCK
    ratio = tq // tk
    lower_in = _tri_stacked("row_le_col")
    lower = _tri(tk, "row_lt_col")

    def body(q_ref, k_ref, v_ref, dy_ref, rtot_ref, first_ref, li_ref, l_ref, dq_ref, dkt_ref, dvt_ref, acc_ref):
        i = pl.program_id(1)

        @pl.when(i == 0)
        def _():
            dkt_ref[...] = jnp.zeros_like(dkt_ref)
            dvt_ref[...] = jnp.zeros_like(dvt_ref)

        first = _head_masks(tq)
        qh = _heads_of(q_ref, first)
        dyb = dy_ref[...].astype(BF16)
        zero = jnp.zeros_like(dyb)
        dyh = [jnp.where(first, dyb, zero), jnp.where(first, zero, dyb)]
        rtoth = _head_cols(rtot_ref)
        li = li_ref[...]
        lo_tri = l_ref[...]
        acc_ref[...] = jnp.zeros_like(acc_ref)

        def block(j, carry, diag):
            mask = _causal_mask(tq, tk, i * tq - j * tk, strict=True) if diag else None
            k, v = k_ref[j], v_ref[j]
            logits = [_dot_nt(qh[n], k) for n in range(2)]
            das = [_dot_nt(dyh[n], v) for n in range(2)]
            terms = [_sb_terms(z, mask, diag) for z in logits]
            upto = [_dot_split2_stacked(l1m, li) for _, l1m in terms]
            des, weights = [], []
            for n in range(2):
                a = jnp.exp(terms[n][0] + ((rtoth[n] - carry[2 * n]) - upto[n]))
                if diag:
                    a = jnp.where(mask, a, 0.0)
                des.append(a * das[n])
                weights.append(a.astype(BF16))
            lefts = [_dot(de.astype(BF16), lo_tri) for de in des]
            dzbs, out = [], []
            for n in range(2):
                beta = jnp.exp(terms[n][0])
                dz = des[n] - (des[n] + (carry[2 * n + 1] + lefts[n])) * beta
                if diag:
                    dz = jnp.where(mask, dz, 0.0)
                dzbs.append(dz.astype(BF16))
                out += [carry[2 * n] + jnp.sum(terms[n][1], axis=-1, keepdims=True),
                        carry[2 * n + 1] + jnp.sum(des[n], axis=-1, keepdims=True)]
            for n in range(2):
                acc_ref[n] += _dot(dzbs[n], k)
            dkt_ref[0, j] += _dot_tn(qh[0], dzbs[0]) + _dot_tn(qh[1], dzbs[1])
            dvt_ref[0, j] += _dot_tn(dyh[0], weights[0]) + _dot_tn(dyh[1], weights[1])
            return tuple(out)

        carry = (jnp.zeros((tq, 1), F32),) * 4
        carry = lax.fori_loop(_first_block(first_ref, ratio * i), ratio * i, lambda j, c: block(j, c, False), carry)
        for d in range(ratio):
            carry = block(ratio * i + d, carry, True)
        dq_ref[...] = jnp.where(first, acc_ref[0], acc_ref[1])

    return pl.pallas_call(
        body,
        grid=(N_PAIRS, sp["nq"]),
        in_specs=[sp["q"], sp["k_rows"], sp["k_rows"], sp["q"], sp["q"], sp["first"],
                  pl.BlockSpec((2 * tk, tk), lambda p, i: (0, 0)), pl.BlockSpec((tk, tk), lambda p, i: (0, 0))],
        out_specs=[sp["q"], sp["k_t"], sp["k_t"]],
        out_shape=[sp["wide"], sp["k_t_out"], sp["k_t_out"]],
        scratch_shapes=[sp["acc"]],
        compiler_params=_cparams(56),
        name="sb_bwd",
    )(qs, k3, v3, dy, rtot, first_block, lower_in, lower)


def _merge_fwd(x1, gates, y_fox, y_sb, w_bf, w_bs, w_out, tm=512):
    s_len = x1.shape[0]

    def body(x_ref, g_ref, yf_ref, ys_ref, wbf_ref, wbs_ref, wo_ref, o_ref):
        g = g_ref[...]
        of = _dot(yf_ref[...].astype(BF16), wbf_ref[...])
        os_ = _dot(ys_ref[...].astype(BF16), wbs_ref[...])
        merged = _sigmoid(g[:, 0:D_MODEL]) * of + _sigmoid(g[:, D_MODEL:]) * os_
        o_ref[...] = x_ref[...] + _dot(merged.astype(BF16), wo_ref[...])

    row = lambda i: (i, 0)
    full = lambda i: (0, 0)
    return pl.pallas_call(
        body,
        grid=(s_len // tm,),
        in_specs=[
            pl.BlockSpec((tm, D_MODEL), row),
            pl.BlockSpec((tm, 2 * D_MODEL), row),
            pl.BlockSpec((tm, ATT_W), row),
            pl.BlockSpec((tm, ATT_W), row),
            pl.BlockSpec((ATT_W, D_MODEL), full),
            pl.BlockSpec((ATT_W, D_MODEL), full),
            pl.BlockSpec((D_MODEL, D_MODEL), full),
        ],
        out_specs=pl.BlockSpec((tm, D_MODEL), row),
        out_shape=jax.ShapeDtypeStruct((s_len, D_MODEL), F32),
        compiler_params=_cparams(48),
        name="merge_fwd",
    )(x1, gates, y_fox, y_sb, w_bf, w_bs, w_out)


def _merge_bwd(dx2, gates, y_fox, y_sb, w_bf, w_bs, w_out, tm=512):
    s_len = dx2.shape[0]

    def body(d_ref, g_ref, yf_ref, ys_ref, wbf_ref, wbs_ref, wo_ref,
             dyf_ref, dys_ref, dg_ref, dof_ref, dos_ref, m_ref, dbf_ref):
        dbf = d_ref[...].astype(BF16)
        dbf_ref[...] = dbf
        dm = _dot_nt(dbf, wo_ref[...])
        g = g_ref[...]
        of = _dot(yf_ref[...].astype(BF16), wbf_ref[...])
        os_ = _dot(ys_ref[...].astype(BF16), wbs_ref[...])
        sf = _sigmoid(g[:, 0:D_MODEL])
        ss = _sigmoid(g[:, D_MODEL:])
        m_ref[...] = (sf * of + ss * os_).astype(BF16)
        d_of = (dm * sf).astype(BF16)
        d_os = (dm * ss).astype(BF16)
        dof_ref[...] = d_of
        dos_ref[...] = d_os
        dg_ref[:, 0:D_MODEL] = (dm * of * sf * (1.0 - sf)).astype(BF16)
        dg_ref[:, D_MODEL:] = (dm * os_ * ss * (1.0 - ss)).astype(BF16)
        dyf_ref[...] = _dot_nt(d_of, wbf_ref[...])
        dys_ref[...] = _dot_nt(d_os, wbs_ref[...])

    row = lambda i: (i, 0)
    full = lambda i: (0, 0)
    return pl.pallas_call(
        body,
        grid=(s_len // tm,),
        in_specs=[
            pl.BlockSpec((tm, D_MODEL), row),
            pl.BlockSpec((tm, 2 * D_MODEL), row),
            pl.BlockSpec((tm, ATT_W), row),
            pl.BlockSpec((tm, ATT_W), row),
            pl.BlockSpec((ATT_W, D_MODEL), full),
            pl.BlockSpec((ATT_W, D_MODEL), full),
            pl.BlockSpec((D_MODEL, D_MODEL), full),
        ],
        out_specs=[
            pl.BlockSpec((tm, ATT_W), row), pl.BlockSpec((tm, ATT_W), row),
            pl.BlockSpec((tm, 2 * D_MODEL), row),
            pl.BlockSpec((tm, D_MODEL), row), pl.BlockSpec((tm, D_MODEL), row),
            pl.BlockSpec((tm, D_MODEL), row), pl.BlockSpec((tm, D_MODEL), row),
        ],
        out_shape=[
            jax.ShapeDtypeStruct((s_len, ATT_W), F32), jax.ShapeDtypeStruct((s_len, ATT_W), F32),
            jax.ShapeDtypeStruct((s_len, 2 * D_MODEL), BF16),
            jax.ShapeDtypeStruct((s_len, D_MODEL), BF16), jax.ShapeDtypeStruct((s_len, D_MODEL), BF16),
            jax.ShapeDtypeStruct((s_len, D_MODEL), BF16), jax.ShapeDtypeStruct((s_len, D_MODEL), BF16),
        ],
        compiler_params=_cparams(56),
        name="merge_bwd",
    )(dx2, gates, y_fox, y_sb, w_bf, w_bs, w_out)


def _ple_loss(x3, p, g, w_pg, w_pp, target, tm=512):
    s_len = x3.shape[0]
    inv_d = 1.0 / D_MODEL

    def body(x_ref, p_ref, g_ref, wpg_ref, wpp_ref, t_ref,
             dx_ref, du_ref, dt_ref, hn_ref, dg_ref, loss_ref):
        @pl.when(pl.program_id(0) == 0)
        def _():
            dg_ref[...] = jnp.zeros_like(dg_ref)
            loss_ref[...] = jnp.zeros_like(loss_ref)

        x = x_ref[...]
        xn, r = _rms(x)
        gain = g_ref[...]
        hn = (xn * gain).astype(BF16)
        hn_ref[...] = hn
        sg = _sigmoid(_dot(hn, wpg_ref[...]))
        t = _dot(p_ref[...].astype(BF16), wpp_ref[...])
        err = x + sg * t - t_ref[...]
        sq = jnp.sum(_colsum(err * err), axis=-1, keepdims=True)
        loss_ref[...] += (0.5 * inv_d) * sq
        dy = err * inv_d
        du = (dy * t * sg * (1.0 - sg)).astype(BF16)
        du_ref[...] = du
        dt_ref[...] = (dy * sg).astype(BF16)
        dh = _dot_nt(du, wpg_ref[...])
        dx_ref[...] = dy + _rms_bwd(dh, xn, r, gain)
        dg_ref[0:1, :] += _colsum(dh * xn)

    row = lambda i: (i, 0)
    full = lambda i: (0, 0)
    bf = jax.ShapeDtypeStruct((s_len, D_MODEL), BF16)
    return pl.pallas_call(
        body,
        grid=(s_len // tm,),
        in_specs=[
            pl.BlockSpec((tm, D_MODEL), row),
            pl.BlockSpec((tm, PLE_DIM), row),
            pl.BlockSpec((1, D_MODEL), full),
            pl.BlockSpec((D_MODEL, D_MODEL), full),
            pl.BlockSpec((PLE_DIM, D_MODEL), full),
            pl.BlockSpec((tm, D_MODEL), row),
        ],
        out_specs=[
            pl.BlockSpec((tm, D_MODEL), row), pl.BlockSpec((tm, D_MODEL), row),
            pl.BlockSpec((tm, D_MODEL), row), pl.BlockSpec((tm, D_MODEL), row),
            pl.BlockSpec((8, D_MODEL), full), pl.BlockSpec((8, LANES), full),
        ],
        out_shape=[
            jax.ShapeDtypeStruct((s_len, D_MODEL), F32), bf, bf, bf,
            jax.ShapeDtypeStruct((8, D_MODEL), F32), jax.ShapeDtypeStruct((8, LANES), F32),
        ],
        compiler_params=_cparams(48),
        name="ple_loss",
    )(x3, p, g, w_pg, w_pp, target)


def _qknorm_bwd(fq, fk, dqs, dk, dv, qn, kn, bd, bd_t, tm=256):
    s_len = fq.shape[0]

    def body(fq_ref, fk_ref, dq_ref, dk_ref, dv_ref, qn_ref, kn_ref, bd_ref, bdt_ref,
             dz_ref, dqn_ref, dkn_ref):
        @pl.when(pl.program_id(0) == 0)
        def _():
            dqn_ref[...] = jnp.zeros_like(dqn_ref)
            dkn_ref[...] = jnp.zeros_like(dkn_ref)

        bd_m = bd_ref[...]
        bdt_m = bdt_ref[...]

        def one(x, dy, gain, dgain_ref):
            xn, rw = _head_rms(x, bd_m, bdt_m)
            dgain_ref[0:1, :] += _colsum(dy * xn)
            dxn = dy * gain
            return rw * (dxn - xn * _head_mean(dxn * xn, bd_m, bdt_m))

        dz_ref[:, 0:ATT_W] = one(fq_ref[...], dq_ref[...] * QK_SCALE, qn_ref[...], dqn_ref).astype(BF16)
        dz_ref[:, ATT_W:2 * ATT_W] = one(fk_ref[...], dk_ref[...], kn_ref[...], dkn_ref).astype(BF16)
        dz_ref[:, 2 * ATT_W:] = dv_ref[...].astype(BF16)

    row = lambda i: (i, 0)
    full = lambda i: (0, 0)
    att = pl.BlockSpec((tm, ATT_W), row)
    return pl.pallas_call(
        body,
        grid=(s_len // tm,),
        in_specs=[att, att, att, att, att,
                  pl.BlockSpec((1, ATT_W), full), pl.BlockSpec((1, ATT_W), full),
                  pl.BlockSpec((ATT_W, LANES), full), pl.BlockSpec((LANES, ATT_W), full)],
        out_specs=[pl.BlockSpec((tm, 3 * ATT_W), row), pl.BlockSpec((8, ATT_W), full), pl.BlockSpec((8, ATT_W), full)],
        out_shape=[jax.ShapeDtypeStruct((s_len, 3 * ATT_W), BF16),
                   jax.ShapeDtypeStruct((8, ATT_W), F32), jax.ShapeDtypeStruct((8, ATT_W), F32)],
        name="qknorm_bwd",
    )(fq, fk, dqs, dk, dv, qn, kn, bd, bd_t)


def _inproj_bwd(x1, dx2, g, dzf, dlogf, logf, dzs, dgates, w_fox, w_fl, w_sb, w_gates, tm=256):
    s_len = x1.shape[0]

    def body(x_ref, d_ref, g_ref, dzf_ref, dlf_ref, lf_ref, dzs_ref, dgt_ref, wf_ref, wl_ref, ws_ref, wg_ref,
             dx_ref, h_ref, dfl_ref, dg_ref, db_ref):
        @pl.when(pl.program_id(0) == 0)
        def _():
            dg_ref[...] = jnp.zeros_like(dg_ref)
            db_ref[...] = jnp.zeros_like(db_ref)

        xn, r = _rms(x_ref[...])
        gain = g_ref[...]
        h_ref[...] = (xn * gain).astype(BF16)
        lane = lax.broadcasted_iota(jnp.int32, (tm, LANES), 1)
        dfl = jnp.where(lane < N_HEADS, dlf_ref[...] * (1.0 - jnp.exp(lf_ref[...])), 0.0)
        db_ref[0:1, :] += _colsum(dfl)
        dflb = dfl.astype(BF16)
        dfl_ref[...] = dflb
        dh = (_dot_nt(dzf_ref[...], wf_ref[...]) + _dot_nt(dflb, wl_ref[...])
              + _dot_nt(dzs_ref[...], ws_ref[...]) + _dot_nt(dgt_ref[...], wg_ref[...]))
        dx_ref[...] = d_ref[...] + _rms_bwd(dh, xn, r, gain)
        dg_ref[0:1, :] += _colsum(dh * xn)

    row = lambda i: (i, 0)
    full = lambda i: (0, 0)
    return pl.pallas_call(
        body,
        grid=(s_len // tm,),
        in_specs=[
            pl.BlockSpec((tm, D_MODEL), row),
            pl.BlockSpec((tm, D_MODEL), row),
            pl.BlockSpec((1, D_MODEL), full),
            pl.BlockSpec((tm, 3 * ATT_W), row),
            pl.BlockSpec((tm, LANES), row),
            pl.BlockSpec((tm, LANES), row),
            pl.BlockSpec((tm, 3 * ATT_W), row),
            pl.BlockSpec((tm, 2 * D_MODEL), row),
            pl.BlockSpec((D_MODEL, 3 * ATT_W), full),
            pl.BlockSpec((D_MODEL, LANES), full),
            pl.BlockSpec((D_MODEL, 3 * ATT_W), full),
            pl.BlockSpec((D_MODEL, 2 * D_MODEL), full),
        ],
        out_specs=[
            pl.BlockSpec((tm, D_MODEL), row), pl.BlockSpec((tm, D_MODEL), row), pl.BlockSpec((tm, LANES), row),
            pl.BlockSpec((8, D_MODEL), full), pl.BlockSpec((8, LANES), full),
        ],
        out_shape=[
            jax.ShapeDtypeStruct((s_len, D_MODEL), F32), jax.ShapeDtypeStruct((s_len, D_MODEL), BF16),
            jax.ShapeDtypeStruct((s_len, LANES), BF16),
            jax.ShapeDtypeStruct((8, D_MODEL), F32), jax.ShapeDtypeStruct((8, LANES), F32),
        ],
        compiler_params=_cparams(56),
        name="inproj_bwd",
    )(x1, dx2, g, dzf, dlogf, logf, dzs, dgates, w_fox, w_fl, w_sb, w_gates)


def _split_w_in(w_in):
    o = 3 * ATT_W
    w_fox = w_in[:, 0:o]
    w_fl = jnp.pad(w_in[:, o:o + N_HEADS], ((0, 0), (0, LANES - N_HEADS)))
    w_sb = w_in[:, o + N_HEADS:2 * o + N_HEADS]
    w_gates = w_in[:, 2 * o + N_HEADS:]
    return w_fox, w_fl, w_sb, w_gates


def _local_grads(x, p, target, small, full, pending=None, send_early=None):
    blk = ATT_BLOCK
    bd, bd_t = _head_sum_matrices()
    full = dict(full)
    late = list(pending) if pending else []

    x1, a1, b1, *gathered = _ffn_fwd(x, small["ffn1_norm"], full["ffn1_w_gate"], full["ffn1_w_up"],
                                     full["ffn1_w_down"], gather=[pending[k] for k in late])
    for k, gth in zip(late, gathered):
        full[k] = gth if k in KEPT_AS_SHARDS else _whole(k, gth)
    w_fox, w_fl, w_sb, w_gates = _split_w_in(full["w_in"])
    bias = jnp.pad(small["forget_bias"], ((0, 0), (0, LANES - N_HEADS)))
    qn = jnp.tile(small["q_norm"], (1, N_HEADS))
    kn = jnp.tile(small["k_norm"], (1, N_HEADS))
    fq, fk, f_qs, f_k, f_v, logf, s_qs, s_k, s_v, gates = _inproj_fwd(
        x1, small["mix_norm"], w_fox, w_fl, w_sb, w_gates, bias, qn, kn, bd, bd_t)
    f_cum = _cumsum_rows(logf, reverse=False)
    f8 = f_cum[:, 0:N_HEADS]
    fw = jnp.repeat(f8, HEAD_DIM, axis=1)
    ft4 = _pair_rows_t(f8, blk)
    f_k3, f_v3 = _blocked_rows(f_k, blk), _blocked_rows(f_v, blk)
    y_fox, lse, f_first = _fox_fwd(f_qs, f_k3, f_v3, fw, ft4, _key_norm_bound(f_k))
    s_k3, s_v3 = _blocked_rows(s_k, blk), _blocked_rows(s_v, blk)
    y_sb, s_rtot, s_first = _sb_fwd(s_qs, s_k3, s_v3)
    x2 = _merge_fwd(x1, gates, y_fox, y_sb, full["w_branch_fox"], full["w_branch_sb"], full["w_out"])
    x3, a2, b2 = _ffn_fwd(x2, small["ffn2_norm"], full["ffn2_w_gate"], full["ffn2_w_up"], full["ffn2_w_down"])

    dx3, du_ple, dt_ple, hn_ple, dg_ple, loss_sum = _ple_loss(
        x3, p, small["ple_norm"], full["w_ple_gate"], full["w_ple_proj"], target)
    dx2, u2, da2, db2, h_ffn2, d3_bf, dg_ffn2 = _ffn_bwd(
        x2, dx3, small["ffn2_norm"], a2, b2, full["ffn2_w_gate"], full["ffn2_w_up"], full["ffn2_w_down"])
    dy_fox, dy_sb, dgates, d_of, d_os, merged, d2_bf = _merge_bwd(
        dx2, gates, y_fox, y_sb, full["w_branch_fox"], full["w_branch_sb"], full["w_out"])

    f_dqs, dfq_w, f_dkt4, f_dvt4, dft4 = _fox_bwd(f_qs, f_k3, f_v3, dy_fox, y_fox, lse, fw, ft4, f_first)
    s_dqs, s_dkt4, s_dvt4 = _sb_bwd(s_qs, s_k3, s_v3, dy_sb, s_rtot, s_first)

    dzf, dqn8, dkn8 = _qknorm_bwd(fq, fk, f_dqs, _unblocked_t(f_dkt4), _unblocked_t(f_dvt4), qn, kn, bd, bd_t)
    dzs = jnp.concatenate([s_dqs * QK_SCALE, _unblocked_t(s_dkt4), _unblocked_t(s_dvt4)], axis=1).astype(BF16)
    df8 = _unpair_rows_t(dft4) + dfq_w[:, ::HEAD_DIM]
    dlogf = _cumsum_rows(jnp.pad(df8, ((0, 0), (0, LANES - N_HEADS))), reverse=True)
    dx1, h_mix, dfl, dg_mix, dbias8 = _inproj_bwd(
        x1, dx2, small["mix_norm"], dzf, dlogf, logf, dzs, dgates, w_fox, w_fl, w_sb, w_gates)

    one = lambda t: t[None]
    gw = {}
    gw["ffn2_w_gate"] = _wgrad(da2, one(h_ffn2), name="wgrad_ffn2_gate")
    gw["ffn2_w_up"] = _wgrad(db2, one(h_ffn2), name="wgrad_ffn2_up")
    gw["ffn2_w_down"] = _wgrad(u2, one(d3_bf), scale=0.5, name="wgrad_ffn2_down")
    g_fox = _wgrad(one(h_mix), one(dzf), name="wgrad_in_fox")[0]
    g_fl = _wgrad(one(h_mix), one(dfl), name="wgrad_in_forget")[0]
    g_sb = _wgrad(one(h_mix), one(dzs), name="wgrad_in_sb")[0]
    g_gt = _wgrad(one(h_mix), one(dgates), name="wgrad_in_gates")[0]
    gw["w_in"] = jnp.concatenate([g_fox, g_fl[:, 0:N_HEADS], g_sb, g_gt], axis=1)
    gw["w_branch_fox"] = _wgrad(one(y_fox), one(d_of), name="wgrad_branch_fox")[0]
    gw["w_branch_sb"] = _wgrad(one(y_sb), one(d_os), name="wgrad_branch_sb")[0]
    gw["w_out"] = _wgrad(one(merged), one(d2_bf), name="wgrad_out")[0]
    gw["w_ple_gate"] = _wgrad(one(hn_ple), one(du_ple), name="wgrad_ple_gate")[0]
    gw["w_ple_proj"] = _wgrad(one(p), one(dt_ple), name="wgrad_ple_proj")[0]

    sent_names, to_send = send_early(gw) if send_early else ([], [])
    grad_x, u1, da1, db1, h_ffn1, d1_bf, dg_ffn1, *landed = _ffn_bwd(
        x, dx1, small["ffn1_norm"], a1, b1, full["ffn1_w_gate"], full["ffn1_w_up"], full["ffn1_w_down"],
        scatter=to_send)
    gw["ffn1_w_gate"] = _wgrad(da1, one(h_ffn1), name="wgrad_ffn1_gate")
    gw["ffn1_w_up"] = _wgrad(db1, one(h_ffn1), name="wgrad_ffn1_up")
    gw["ffn1_w_down"] = _wgrad(u1, one(d1_bf), scale=0.5, name="wgrad_ffn1_down")

    fold = lambda t: jnp.sum(t[0:1].reshape(N_HEADS, HEAD_DIM), axis=0, keepdims=True)
    gs = {
        "ffn1_norm": dg_ffn1[0:1], "mix_norm": dg_mix[0:1], "ffn2_norm": dg_ffn2[0:1], "ple_norm": dg_ple[0:1],
        "forget_bias": dbias8[0:1, 0:N_HEADS], "q_norm": fold(dqn8), "k_norm": fold(dkn8),
    }
    return loss_sum, grad_x, gw, gs, dict(zip(sent_names, landed))


def _position():
    return lax.axis_index("x"), lax.axis_index("y"), lax.axis_index("c")


def _other_chips(x, y):
    return [(1 - x, y), (x, 1 - y), (1 - x, 1 - y)]


ANY = pl.BlockSpec(memory_space=pl.ANY)


def _place_own_shard(w, q):
    rows, cols = w.shape
    tr = _row_block(rows, cols * 4, budget=2 * MIB)

    def body(q_ref, w_ref, o_ref):
        o_ref[0] = w_ref[...].astype(BF16)

    return pl.pallas_call(
        body,
        grid_spec=pltpu.PrefetchScalarGridSpec(
            num_scalar_prefetch=1,
            grid=(rows // tr,),
            in_specs=[pl.BlockSpec((tr, cols), lambda i, q_ref: (i, 0))],
            out_specs=pl.BlockSpec((1, tr, cols), lambda i, q_ref: (q_ref[0], i, 0)),
        ),
        out_shape=jax.ShapeDtypeStruct((N_CHIPS, rows, cols), BF16),
        name="place_own_shard",
    )(q, w)


def _gather_semaphores(n):
    return [pltpu.SemaphoreType.DMA((6 * n,)), pltpu.SemaphoreType.DMA((6 * n,))]


def _gather_steps(bufs, send_sems, recv_sems):
    n = len(bufs)
    x, y, c = _position()
    q = 2 * x + y
    chips = _other_chips(x, y)
    sibling = (x, y, 1 - c)

    def half(a, slot, which):
        r2 = bufs[a].shape[1] // 2
        return bufs[a].at[slot, pl.ds(which * r2, r2), :]

    def copy(a, k, region, to):
        return pltpu.make_async_remote_copy(
            src_ref=region, dst_ref=region, send_sem=send_sems.at[6 * a + k], recv_sem=recv_sems.at[6 * a + k],
            device_id=to, device_id_type=MESH)

    def to_chip(a, k):
        tx, ty = chips[k]
        return copy(a, k, half(a, q, c), (tx, ty, c))

    def to_sibling(a, k):
        tx, ty = chips[k]
        return copy(a, 3 + k, half(a, 2 * tx + ty, c), sibling)

    def start():
        for a in range(n):
            for k in range(3):
                to_chip(a, k).start()

    def finish():
        for a in range(n):
            for k, (tx, ty) in enumerate(chips):
                copy(a, k, half(a, 2 * tx + ty, c), (tx, ty, c)).wait_recv()
                to_sibling(a, k).start()
        for a in range(n):
            for k, (tx, ty) in enumerate(chips):
                copy(a, 3 + k, half(a, 2 * tx + ty, 1 - c), sibling).wait_recv()
        for a in range(n):
            for k in range(3):
                to_chip(a, k).wait_send()
                to_sibling(a, k).wait_send()

    return start, finish


def _allgather_weights(slots):
    n = len(slots)

    def body(*refs):
        start, finish = _gather_steps(refs[n:2 * n], *refs[2 * n:])
        start()
        finish()

    return pl.pallas_call(
        body,
        in_specs=[ANY] * n,
        out_specs=[ANY] * n,
        out_shape=[jax.ShapeDtypeStruct(s.shape, s.dtype) for s in slots],
        input_output_aliases={a: a for a in range(n)},
        scratch_shapes=_gather_semaphores(n),
        name="allgather_weights",
    )(*slots)


def _exchange_pair_halves(grads):
    n = len(grads)

    def body(*refs):
        ins, outs = refs[0:n], refs[n:2 * n]
        send_sems, recv_sems = refs[2 * n:]
        x, y, c = _position()
        copies = []
        for a in range(n):
            r2 = grads[a].shape[1] // 2
            cp = pltpu.make_async_remote_copy(
                src_ref=ins[a].at[:, pl.ds((1 - c) * r2, r2), :], dst_ref=outs[a],
                send_sem=send_sems.at[a], recv_sem=recv_sems.at[a], device_id=(x, y, 1 - c), device_id_type=MESH)
            cp.start()
            copies.append(cp)
        for cp in copies:
            cp.wait()

    return pl.pallas_call(
        body,
        in_specs=[ANY] * n,
        out_specs=[ANY] * n,
        out_shape=[jax.ShapeDtypeStruct((N_CHIPS, g.shape[1] // 2, g.shape[2]), g.dtype) for g in grads],
        scratch_shapes=[pltpu.SemaphoreType.DMA((n,)), pltpu.SemaphoreType.DMA((n,))],
        name="rs_pair_exchange",
    )(*grads)


def _scatter_semaphores(n):
    return [pltpu.SemaphoreType.DMA((3 * n,)), pltpu.SemaphoreType.DMA((3 * n,)), pltpu.SemaphoreType.DMA((n,))]


def _scatter_steps(ins, outs, send_sems, recv_sems, local_sems):
    n = len(ins)
    x, y, c = _position()
    q = 2 * x + y
    chips = _other_chips(x, y)

    def own(a):
        return pltpu.make_async_copy(ins[a].at[q], outs[a].at[q], local_sems.at[a])

    def to_chip(a, k):
        tx, ty = chips[k]
        return pltpu.make_async_remote_copy(
            src_ref=ins[a].at[2 * tx + ty], dst_ref=outs[a].at[q],
            send_sem=send_sems.at[3 * a + k], recv_sem=recv_sems.at[3 * a + k],
            device_id=(tx, ty, c), device_id_type=MESH)

    def start():
        for a in range(n):
            own(a).start()
            for k in range(3):
                to_chip(a, k).start()

    def finish():
        for a in range(n):
            own(a).wait()
            for k in range(3):
                to_chip(a, k).wait()

    return start, finish


def _scatter_to_owner_chips(pairs):
    n = len(pairs)

    def body(*refs):
        start, finish = _scatter_steps(refs[0:n], refs[n:2 * n], *refs[2 * n:])
        start()
        finish()

    return pl.pallas_call(
        body,
        in_specs=[ANY] * n,
        out_specs=[ANY] * n,
        out_shape=[jax.ShapeDtypeStruct(p.shape, p.dtype) for p in pairs],
        scratch_shapes=_scatter_semaphores(n),
        name="rs_scatter",
    )(*pairs)


def _join_halves(shards):
    n = len(shards)

    def body(*refs):
        bufs = refs[n:2 * n]
        send_sems, recv_sems = refs[2 * n:]
        x, y, c = _position()
        started = []
        for a in range(n):
            r2 = shards[a].shape[0] // 2
            mine = bufs[a].at[pl.ds(c * r2, r2), :]
            cp = pltpu.make_async_remote_copy(
                src_ref=mine, dst_ref=mine, send_sem=send_sems.at[a], recv_sem=recv_sems.at[a],
                device_id=(x, y, 1 - c), device_id_type=MESH)
            cp.start()
            started.append(cp)
        for cp in started:
            cp.wait()

    return pl.pallas_call(
        body,
        in_specs=[ANY] * n,
        out_specs=[ANY] * n,
        out_shape=[jax.ShapeDtypeStruct(t.shape, t.dtype) for t in shards],
        input_output_aliases={a: a for a in range(n)},
        scratch_shapes=[pltpu.SemaphoreType.DMA((n,)), pltpu.SemaphoreType.DMA((n,))],
        name="rs_join_halves",
    )(*shards)


def _add_pair(g, got, c):
    _, r2, cols = got.shape

    def body(c_ref, g_ref, got_ref, o_ref):
        o_ref[...] = (g_ref[...].astype(F32) + got_ref[...].astype(F32)).astype(BF16)

    spec = pl.BlockSpec((1, r2, cols), lambda s, c_ref: (s, 0, 0))
    return pl.pallas_call(
        body,
        grid_spec=pltpu.PrefetchScalarGridSpec(
            num_scalar_prefetch=1,
            grid=(N_CHIPS,),
            in_specs=[pl.BlockSpec((1, r2, cols), lambda s, c_ref: (s, c_ref[0], 0)), spec],
            out_specs=spec,
        ),
        out_shape=jax.ShapeDtypeStruct(got.shape, BF16),
        name="rs_add_pair",
    )(c, g, got)


def _add_chips(parts, c):
    _, r2, cols = parts.shape

    def body(c_ref, p0, p1, p2, p3, o_ref):
        o_ref[...] = ((p0[0].astype(F32) + p1[0].astype(F32)) + p2[0].astype(F32)) + p3[0].astype(F32)

    specs = [pl.BlockSpec((1, r2, cols), functools.partial(lambda i, c_ref, s: (s, 0, 0), s=s))
             for s in range(N_CHIPS)]
    return pl.pallas_call(
        body,
        grid_spec=pltpu.PrefetchScalarGridSpec(
            num_scalar_prefetch=1,
            grid=(1,),
            in_specs=specs,
            out_specs=pl.BlockSpec((r2, cols), lambda i, c_ref: (c_ref[0], 0)),
        ),
        out_shape=jax.ShapeDtypeStruct((2 * r2, cols), F32),
        name="rs_add_chips",
    )(c, parts, parts, parts, parts)


def _allreduce_small(part):
    shape = part.shape

    def body(in_ref, out_ref, gather_ref, send_sems, recv_sems):
        x, y, c = _position()
        me = 4 * x + 2 * y + c
        relations = [(a, b, d) for a in (0, 1) for b in (0, 1) for d in (0, 1)][1:]
        flip = lambda v, f: 1 - v if f else v
        copies = []
        for k, (a, b, d) in enumerate(relations):
            cp = pltpu.make_async_remote_copy(
                src_ref=in_ref, dst_ref=gather_ref.at[me], send_sem=send_sems.at[k], recv_sem=recv_sems.at[k],
                device_id=(flip(x, a), flip(y, b), flip(c, d)), device_id_type=MESH)
            cp.start()
            copies.append(cp)
        gather_ref[me] = in_ref[...]
        for cp in copies:
            cp.wait()
        total = gather_ref[0]
        for dev in range(1, 8):
            total = total + gather_ref[dev]
        out_ref[...] = total

    vmem = pl.BlockSpec(memory_space=pltpu.VMEM)
    return pl.pallas_call(
        body,
        in_specs=[vmem],
        out_specs=vmem,
        out_shape=jax.ShapeDtypeStruct(shape, F32),
        scratch_shapes=[pltpu.VMEM((8,) + shape, F32), pltpu.SemaphoreType.DMA((7,)), pltpu.SemaphoreType.DMA((7,))],
        name="allreduce_small",
    )(part)


def _adamw(w, g, m, v):
    rows, cols = w.shape
    tr = _row_block(rows, cols * 4, budget=MIB)
    c1 = 1.0 / (1.0 - ADAM_B1 ** ADAM_STEP)
    c2 = 1.0 / (1.0 - ADAM_B2 ** ADAM_STEP)

    def body(w_ref, g_ref, m_ref, v_ref, d_ref, nm_ref, nv_ref):
        g_ = g_ref[...]
        nm = ADAM_B1 * m_ref[...] + (1.0 - ADAM_B1) * g_
        nv = ADAM_B2 * v_ref[...] + (1.0 - ADAM_B2) * (g_ * g_)
        nm_ref[...] = nm
        nv_ref[...] = nv
        d_ref[...] = -ADAM_LR * ((nm * c1) / (jnp.sqrt(nv * c2) + ADAM_EPS) + ADAM_WD * w_ref[...])

    spec = pl.BlockSpec((tr, cols), lambda i: (i, 0))
    out = jax.ShapeDtypeStruct((rows, cols), F32)
    return pl.pallas_call(
        body,
        grid=(rows // tr,),
        in_specs=[spec] * 4,
        out_specs=[spec] * 3,
        out_shape=[out] * 3,
        name="adamw",
    )(w, g, m, v)


BIG = ["ffn1_w_gate", "ffn1_w_up", "ffn1_w_down", "w_in", "w_branch_fox", "w_branch_sb", "w_out",
       "ffn2_w_gate", "ffn2_w_up", "ffn2_w_down", "w_ple_gate", "w_ple_proj"]
SMALL = ["ffn1_norm", "mix_norm", "ffn2_norm", "ple_norm", "forget_bias", "q_norm", "k_norm"]
COLUMN_SHARDED = ["ffn1_w_gate", "ffn1_w_up", "w_in", "w_branch_fox", "w_branch_sb",
                  "ffn2_w_gate", "ffn2_w_up", "w_ple_proj"]
KEPT_AS_SHARDS = ["ffn1_w_gate", "ffn1_w_up", "ffn1_w_down", "ffn2_w_gate", "ffn2_w_up", "ffn2_w_down"]
WORKED_TRANSPOSED = ["ffn1_w_gate", "ffn1_w_up", "ffn2_w_gate", "ffn2_w_up"]
NEEDED_FIRST = ["ffn1_w_gate", "ffn1_w_up", "ffn1_w_down"]
ORDER = ["ffn1_norm", "ffn1_w_gate", "ffn1_w_up", "ffn1_w_down", "mix_norm", "w_in", "forget_bias", "q_norm",
         "k_norm", "w_branch_fox", "w_branch_sb", "w_out", "ffn2_norm", "ffn2_w_gate", "ffn2_w_up",
         "ffn2_w_down", "ple_norm", "w_ple_gate", "w_ple_proj"]
SMALL_ROWS = {"ffn1_norm": 0, "mix_norm": 1, "ffn2_norm": 2, "ple_norm": 3}
SMALL_COLS = {"forget_bias": (0, N_HEADS), "q_norm": (N_HEADS, HEAD_DIM), "k_norm": (N_HEADS + HEAD_DIM, HEAD_DIM)}
LOSS_ROW = 5


def _stored(name, a):
    return jnp.swapaxes(a[0], 0, 1) if name in WORKED_TRANSPOSED else a[0]


def _returned(name, a):
    return (jnp.swapaxes(a, 0, 1) if name in WORKED_TRANSPOSED else a)[None]


def _whole(name, gathered):
    if name in COLUMN_SHARDED:
        return jnp.concatenate([gathered[s] for s in range(N_CHIPS)], axis=1)
    return gathered.reshape(-1, gathered.shape[-1])


def _as_shards(name, whole):
    if name in COLUMN_SHARDED:
        k, n = whole.shape
        return whole.reshape(k, N_CHIPS, n // N_CHIPS).transpose(1, 0, 2)
    return whole.reshape(N_CHIPS, whole.shape[0] // N_CHIPS, whole.shape[1])


def _pack_small(values, extra=None):
    rows = [values[k] for k in ("ffn1_norm", "mix_norm", "ffn2_norm", "ple_norm")]
    tail = jnp.concatenate([values["forget_bias"], values["q_norm"], values["k_norm"]], axis=1)
    rows.append(jnp.pad(tail, ((0, 0), (0, D_MODEL - tail.shape[1]))))
    packed = jnp.concatenate(rows + [jnp.zeros((3, D_MODEL), F32)], axis=0)
    if extra is not None:
        packed = packed.at[LOSS_ROW, 0].set(extra)
    return packed


def _unpack_small(packed):
    out = {k: packed[r:r + 1] for k, r in SMALL_ROWS.items()}
    for k, (start, size) in SMALL_COLS.items():
        out[k] = packed[4:5, start:start + size]
    return out


def kernel(x, p, ffn1_norm, ffn1_w_gate, ffn1_w_up, ffn1_w_down, mix_norm, w_in, forget_bias, q_norm, k_norm, w_branch_fox, w_branch_sb, w_out, ffn2_norm, ffn2_w_gate, ffn2_w_up, ffn2_w_down, ple_norm, w_ple_gate, w_ple_proj, loss_target, m_ffn1_norm, m_ffn1_w_gate, m_ffn1_w_up, m_ffn1_w_down, m_mix_norm, m_w_in, m_forget_bias, m_q_norm, m_k_norm, m_w_branch_fox, m_w_branch_sb, m_w_out, m_ffn2_norm, m_ffn2_w_gate, m_ffn2_w_up, m_ffn2_w_down, m_ple_norm, m_w_ple_gate, m_w_ple_proj, v_ffn1_norm, v_ffn1_w_gate, v_ffn1_w_up, v_ffn1_w_down, v_mix_norm, v_w_in, v_forget_bias, v_q_norm, v_k_norm, v_w_branch_fox, v_w_branch_sb, v_w_out, v_ffn2_norm, v_ffn2_w_gate, v_ffn2_w_up, v_ffn2_w_down, v_ple_norm, v_w_ple_gate, v_w_ple_proj):
    args = dict(locals())
    weights = {k: args[k] for k in ORDER}
    moments_m = {k: args["m_" + k] for k in ORDER}
    moments_v = {k: args["v_" + k] for k in ORDER}

    c_idx = lax.axis_index("c").astype(jnp.int32).reshape(1)
    q_idx = (2 * lax.axis_index("x") + lax.axis_index("y")).astype(jnp.int32).reshape(1)
    own = {k: _place_own_shard(_stored(k, weights[k]), q_idx) for k in BIG}
    full = dict(zip(NEEDED_FIRST, _allgather_weights([own[k] for k in NEEDED_FIRST])))
    pending = {k: own[k] for k in BIG if k not in NEEDED_FIRST}
    small = {k: weights[k] for k in SMALL}

    def pair_sums(names, gw):
        slots = [gw[k] if k in KEPT_AS_SHARDS else _as_shards(k, gw[k]) for k in names]
        from_core = _exchange_pair_halves(slots)
        return [_add_pair(g, got, c_idx) for g, got in zip(slots, from_core)]

    late = [k for k in BIG if k not in NEEDED_FIRST]
    loss_sum, grad_x, gw, gs, parts = _local_grads(
        x[0], p[0, 0], loss_target[0], small, full, pending, lambda early: (late, pair_sums(late, early)))

    parts.update(zip(NEEDED_FIRST, _scatter_to_owner_chips(pair_sums(NEEDED_FIRST, gw))))
    grads_big = dict(zip(BIG, _join_halves([_add_chips(parts[k], c_idx) for k in BIG])))
    reduced = _allreduce_small(_pack_small(gs, extra=loss_sum[0, 0]))
    grads_small = _unpack_small(reduced)
    loss = reduced[LOSS_ROW, 0]

    grads, deltas, new_m, new_v = {}, {}, {}, {}
    for k in BIG:
        d, nm, nv = _adamw(_stored(k, weights[k]), grads_big[k], _stored(k, moments_m[k]), _stored(k, moments_v[k]))
        grads[k], deltas[k], new_m[k], new_v[k] = (_returned(k, t) for t in (grads_big[k], d, nm, nv))
    d_s, nm_s, nv_s = _adamw(_pack_small({k: weights[k] for k in SMALL}), reduced,
                             _pack_small({k: moments_m[k] for k in SMALL}),
                             _pack_small({k: moments_v[k] for k in SMALL}))
    for k in SMALL:
        grads[k] = grads_small[k]
    for name, packed in (("d", d_s), ("m", nm_s), ("v", nv_s)):
        target = {"d": deltas, "m": new_m, "v": new_v}[name]
        target.update(_unpack_small(packed))

    return (loss, grad_x[None], *[grads[k] for k in ORDER], *[deltas[k] for k in ORDER],
            *[new_m[k] for k in ORDER], *[new_v[k] for k in ORDER])
```

```python
import functools

import jax
import jax.numpy as jnp
from jax import lax
from jax.experimental import pallas as pl
from jax.experimental.pallas import tpu as pltpu

F32 = jnp.float32
BF16 = jnp.bfloat16

D_MODEL = 1024
D_FF = 2816
N_CHIPS = 4
FF_SHARD = D_FF // N_CHIPS
FFN_CHUNKS = 2
HEAD_DIM = 64
N_HEADS = 8
ATT_W = N_HEADS * HEAD_DIM
PAIR_W = 2 * HEAD_DIM
N_PAIRS = N_HEADS // 2
PLE_DIM = 256
IN_WIDTH = 3 * ATT_W + N_HEADS + 3 * ATT_W + 2 * D_MODEL
EPS = 1e-6
QK_SCALE = HEAD_DIM ** -0.5
LANES = 128
ATT_BLOCK = 256
ATT_Q_BLOCK = 512
NEG_BIG = -1e30
EXP_UNDERFLOW = 110.0

ADAM_LR = 0.001
ADAM_B1 = 0.9
ADAM_B2 = 0.999
ADAM_EPS = 1e-08
ADAM_WD = 0.01
ADAM_STEP = 10

MESH = pl.DeviceIdType.MESH
MIB = 1024 * 1024


def _cparams(vmem_mib=48):
    return pltpu.CompilerParams(vmem_limit_bytes=vmem_mib * MIB)


def _dot(a, b):
    return jnp.dot(a, b, preferred_element_type=F32)


def _dot_tn(a, b):
    return lax.dot_general(a, b, (((0,), (0,)), ((), ())), preferred_element_type=F32)


def _dot_nt(a, b):
    return lax.dot_general(a, b, (((1,), (1,)), ((), ())), preferred_element_type=F32)


def _sigmoid(x):
    return 1.0 / (1.0 + jnp.exp(-x))


def _split2(x):
    hi = x.astype(BF16)
    lo = (x - hi.astype(F32)).astype(BF16)
    return hi, lo


def _dot_split2(x, m):
    hi, lo = _split2(x)
    return _dot(hi, m) + _dot(lo, m)


def _split3(x):
    hi = x.astype(BF16)
    rest = x - hi.astype(F32)
    mid = rest.astype(BF16)
    lo = (rest - mid.astype(F32)).astype(BF16)
    return hi, mid, lo


def _rms(x):
    r = lax.rsqrt(jnp.mean(x * x, axis=-1, keepdims=True) + EPS)
    return x * r, r


def _rms_bwd(dh, xn, r, g):
    dxn = dh * g
    return r * (dxn - xn * jnp.mean(dxn * xn, axis=-1, keepdims=True))


def _colsum(x):
    return jnp.sum(x, axis=0, keepdims=True)


def _row_block(rows, row_bytes, budget):
    best = None
    for t in range(8, rows + 1, 8):
        if rows % t == 0 and t * row_bytes <= budget:
            best = t
    return best if best is not None else rows


def _ffn_fwd(x, g, wg, wu, wd, gather=(), tm=512):
    s_len = x.shape[0]
    n = len(gather)
    steps = s_len // tm

    def body(x_ref, g_ref, wg_ref, wu_ref, wd_ref, *rest):
        o_ref, a_ref, b_ref = rest[n:n + 3]
        h_s, acc_s = rest[2 * n + 3:2 * n + 5]
        i = pl.program_id(0)
        j = pl.program_id(1)
        if n:
            start, finish = _gather_steps(rest[n + 3:2 * n + 3], *rest[2 * n + 5:])
            pl.when((i == 0) & (j == 0))(start)

        @pl.when(j == 0)
        def _():
            xn, _ = _rms(x_ref[...])
            h_s[...] = (xn * g_ref[...]).astype(BF16)
            acc_s[...] = jnp.zeros_like(acc_s)

        chunks = [pl.ds(r * (tm // FFN_CHUNKS), tm // FFN_CHUNKS) for r in range(FFN_CHUNKS)]
        pre = [(_dot_nt(h_s[rows, :], wg_ref[0]), _dot_nt(h_s[rows, :], wu_ref[0])) for rows in chunks]
        us = []
        for rows, (a, b) in zip(chunks, pre):
            a_ref[0, rows, :] = a.astype(BF16)
            b_ref[0, rows, :] = b.astype(BF16)
            us.append((a * _sigmoid(a) * b).astype(BF16))
        for rows, u in zip(chunks, us):
            acc_s[rows, :] += _dot(u, wd_ref[0])

        @pl.when(j == N_CHIPS - 1)
        def _():
            o_ref[...] = x_ref[...] + 0.5 * acc_s[...]

        if n:
            pl.when((i == steps - 1) & (j == N_CHIPS - 1))(finish)

    return pl.pallas_call(
        body,
        grid=(steps, N_CHIPS),
        in_specs=[
            pl.BlockSpec((tm, D_MODEL), lambda i, j: (i, 0)),
            pl.BlockSpec((1, D_MODEL), lambda i, j: (0, 0)),
            pl.BlockSpec((1, FF_SHARD, D_MODEL), lambda i, j: (j, 0, 0)),
            pl.BlockSpec((1, FF_SHARD, D_MODEL), lambda i, j: (j, 0, 0)),
            pl.BlockSpec((1, FF_SHARD, D_MODEL), lambda i, j: (j, 0, 0)),
        ] + [ANY] * n,
        out_specs=[pl.BlockSpec((tm, D_MODEL), lambda i, j: (i, 0)),
                   pl.BlockSpec((1, tm, FF_SHARD), lambda i, j: (j, i, 0)),
                   pl.BlockSpec((1, tm, FF_SHARD), lambda i, j: (j, i, 0))] + [ANY] * n,
        out_shape=[jax.ShapeDtypeStruct((s_len, D_MODEL), F32),
                   jax.ShapeDtypeStruct((N_CHIPS, s_len, FF_SHARD), BF16),
                   jax.ShapeDtypeStruct((N_CHIPS, s_len, FF_SHARD), BF16)]
        + [jax.ShapeDtypeStruct(s.shape, s.dtype) for s in gather],
        input_output_aliases={5 + a: 3 + a for a in range(n)},
        scratch_shapes=[pltpu.VMEM((tm, D_MODEL), BF16), pltpu.VMEM((tm, D_MODEL), F32)]
        + (_gather_semaphores(n) if n else []),
        compiler_params=_cparams(48),
        name="ffn_fwd_gathering" if n else "ffn_fwd",
    )(x, g, wg, wu, wd, *gather)


def _ffn_bwd(x, d, g, a_pre, b_pre, wg, wu, wd, scatter=(), tm=512):
    s_len = x.shape[0]
    nb = s_len // tm
    n = len(scatter)

    def body(x_ref, d_ref, g_ref, a_ref, b_ref, wg_ref, wu_ref, wd_ref, *rest):
        dx_ref, u_ref, da_ref, db_ref, h_ref, dbf_ref, dg_ref = rest[n:n + 7]
        dbf_s, dh_s = rest[2 * n + 7:2 * n + 9]
        i = pl.program_id(0)
        j = pl.program_id(1)
        if n:
            start, finish = _scatter_steps(rest[0:n], rest[n + 7:2 * n + 7], *rest[2 * n + 9:])
            pl.when((i == 0) & (j == 0))(start)

        @pl.when(j == 0)
        def _():
            xn, _ = _rms(x_ref[...])
            h_ref[...] = (xn * g_ref[...]).astype(BF16)
            dbf = d_ref[...].astype(BF16)
            dbf_s[...] = dbf
            dbf_ref[...] = dbf
            dh_s[...] = jnp.zeros_like(dh_s)

        @pl.when((i == 0) & (j == 0))
        def _():
            dg_ref[...] = jnp.zeros_like(dg_ref)

        chunks = [pl.ds(r * (tm // FFN_CHUNKS), tm // FFN_CHUNKS) for r in range(FFN_CHUNKS)]
        dus = [0.5 * _dot_nt(dbf_s[rows, :], wd_ref[0]) for rows in chunks]
        das, dbs = [], []
        for rows, du in zip(chunks, dus):
            a = a_ref[0, rows, :].astype(F32)
            b = b_ref[0, rows, :].astype(F32)
            s = _sigmoid(a)
            silu = a * s
            da = (du * b * (s * (1.0 + a * (1.0 - s)))).astype(BF16)
            db = (du * silu).astype(BF16)
            u_ref[0, rows, :] = (silu * b).astype(BF16)
            da_ref[0, rows, :] = da
            db_ref[0, rows, :] = db
            das.append(da)
            dbs.append(db)
        for rows, da, db in zip(chunks, das, dbs):
            dh_s[rows, :] += _dot(da, wg_ref[0]) + _dot(db, wu_ref[0])

        @pl.when(j == N_CHIPS - 1)
        def _():
            xn, r = _rms(x_ref[...])
            dh = dh_s[...]
            dx_ref[...] = d_ref[...] + _rms_bwd(dh, xn, r, g_ref[...])
            dg_ref[0:1, :] += _colsum(dh * xn)

        if n:
            pl.when((i == nb - 1) & (j == N_CHIPS - 1))(finish)

    row = lambda i, j: (i, 0)
    shard = lambda i, j: (j, 0, 0)
    act = lambda i, j: (j, i, 0)
    return pl.pallas_call(
        body,
        grid=(nb, N_CHIPS),
        in_specs=[
            pl.BlockSpec((tm, D_MODEL), row),
            pl.BlockSpec((tm, D_MODEL), row),
            pl.BlockSpec((1, D_MODEL), lambda i, j: (0, 0)),
            pl.BlockSpec((1, tm, FF_SHARD), act),
            pl.BlockSpec((1, tm, FF_SHARD), act),
            pl.BlockSpec((1, FF_SHARD, D_MODEL), shard),
            pl.BlockSpec((1, FF_SHARD, D_MODEL), shard),
            pl.BlockSpec((1, FF_SHARD, D_MODEL), shard),
        ] + [ANY] * n,
        out_specs=[
            pl.BlockSpec((tm, D_MODEL), row),
            pl.BlockSpec((1, tm, FF_SHARD), act),
            pl.BlockSpec((1, tm, FF_SHARD), act),
            pl.BlockSpec((1, tm, FF_SHARD), act),
            pl.BlockSpec((tm, D_MODEL), row),
            pl.BlockSpec((tm, D_MODEL), row),
            pl.BlockSpec((8, D_MODEL), lambda i, j: (0, 0)),
        ] + [ANY] * n,
        out_shape=[
            jax.ShapeDtypeStruct((s_len, D_MODEL), F32),
            jax.ShapeDtypeStruct((N_CHIPS, s_len, FF_SHARD), BF16),
            jax.ShapeDtypeStruct((N_CHIPS, s_len, FF_SHARD), BF16),
            jax.ShapeDtypeStruct((N_CHIPS, s_len, FF_SHARD), BF16),
            jax.ShapeDtypeStruct((s_len, D_MODEL), BF16),
            jax.ShapeDtypeStruct((s_len, D_MODEL), BF16),
            jax.ShapeDtypeStruct((8, D_MODEL), F32),
        ] + [jax.ShapeDtypeStruct(s.shape, s.dtype) for s in scatter],
        scratch_shapes=[
            pltpu.VMEM((tm, D_MODEL), BF16),
            pltpu.VMEM((tm, D_MODEL), F32),
        ] + (_scatter_semaphores(n) if n else []),
        compiler_params=_cparams(56),
        name="ffn_bwd_scattering" if n else "ffn_bwd",
    )(x, d, g, a_pre, b_pre, wg, wu, wd, *scatter)


def _wgrad(a, b, scale=1.0, name="wgrad"):
    na, s_len, k_dim = a.shape
    nb, _, n_dim = b.shape
    n = max(na, nb)
    ts = min(s_len, 1024)
    steps = s_len // ts

    def body(a_ref, b_ref, o_ref, acc_s):
        s = pl.program_id(1)

        @pl.when(s == 0)
        def _():
            acc_s[...] = jnp.zeros_like(acc_s)

        acc_s[...] += _dot_tn(a_ref[0].astype(BF16), b_ref[0].astype(BF16))

        @pl.when(s == steps - 1)
        def _():
            o_ref[0] = (acc_s[...] * scale).astype(BF16)

    a_map = (lambda m, s: (m, s, 0)) if na > 1 else (lambda m, s: (0, s, 0))
    b_map = (lambda m, s: (m, s, 0)) if nb > 1 else (lambda m, s: (0, s, 0))
    return pl.pallas_call(
        body,
        grid=(n, steps),
        in_specs=[pl.BlockSpec((1, ts, k_dim), a_map), pl.BlockSpec((1, ts, n_dim), b_map)],
        out_specs=pl.BlockSpec((1, k_dim, n_dim), lambda m, s: (m, 0, 0)),
        out_shape=jax.ShapeDtypeStruct((n, k_dim, n_dim), BF16),
        scratch_shapes=[pltpu.VMEM((k_dim, n_dim), F32)],
        compiler_params=_cparams(56),
        name=name,
    )(a, b)


def _head_sum_matrices():
    lane = lax.broadcasted_iota(jnp.int32, (ATT_W, LANES), 0) // HEAD_DIM
    col = lax.broadcasted_iota(jnp.int32, (ATT_W, LANES), 1)
    bd = (lane == col).astype(BF16)
    return bd, bd.T


def _head_mean(t, bd, bd_t):
    per_head = _dot_split2(t, bd) * (1.0 / HEAD_DIM)
    return _dot_split2(per_head, bd_t)


def _head_rms(x, bd, bd_t):
    per_head = _dot_split2(x * x, bd) * (1.0 / HEAD_DIM)
    r = lax.rsqrt(per_head + EPS)
    rw = _dot_split2(r, bd_t)
    return x * rw, rw


def _log_sigmoid(z):
    return jnp.minimum(z, 0.0) - jnp.log(1.0 + jnp.exp(-jnp.abs(z)))


def _inproj_fwd(x1, g, w_fox, w_fl, w_sb, w_gates, bias, qn, kn, bd, bd_t, tm=256):
    s_len = x1.shape[0]

    def body(x_ref, g_ref, wf_ref, wl_ref, ws_ref, wg_ref, bias_ref, qn_ref, kn_ref, bd_ref, bdt_ref,
             fq_ref, fk_ref, qs_ref, kf_ref, vf_ref, logf_ref, sq_ref, sk_ref, sv_ref, gates_ref):
        xn, _ = _rms(x_ref[...])
        h = (xn * g_ref[...]).astype(BF16)
        zf = _dot_nt(h, wf_ref[...])
        fq = zf[:, 0:ATT_W]
        fk = zf[:, ATT_W:2 * ATT_W]
        fq_ref[...] = fq
        fk_ref[...] = fk
        bd_m = bd_ref[...]
        bdt_m = bdt_ref[...]
        fqn, _ = _head_rms(fq, bd_m, bdt_m)
        fkn, _ = _head_rms(fk, bd_m, bdt_m)
        qs_ref[...] = (fqn * qn_ref[...]).astype(BF16) * QK_SCALE
        kf_ref[...] = (fkn * kn_ref[...]).astype(BF16)
        vf_ref[...] = zf[:, 2 * ATT_W:3 * ATT_W].astype(BF16)
        logf_ref[...] = _log_sigmoid(_dot_nt(h, wl_ref[...]) + bias_ref[...])
        zs = _dot_nt(h, ws_ref[...])
        sq_ref[...] = zs[:, 0:ATT_W].astype(BF16) * QK_SCALE
        sk_ref[...] = zs[:, ATT_W:2 * ATT_W].astype(BF16)
        sv_ref[...] = zs[:, 2 * ATT_W:3 * ATT_W].astype(BF16)
        gates_ref[...] = _dot_nt(h, wg_ref[...])

    row = lambda i: (i, 0)
    full = lambda i: (0, 0)
    att = lambda dt: jax.ShapeDtypeStruct((s_len, ATT_W), dt)
    return pl.pallas_call(
        body,
        grid=(s_len // tm,),
        in_specs=[
            pl.BlockSpec((tm, D_MODEL), row),
            pl.BlockSpec((1, D_MODEL), full),
            pl.BlockSpec((3 * ATT_W, D_MODEL), full),
            pl.BlockSpec((LANES, D_MODEL), full),
            pl.BlockSpec((3 * ATT_W, D_MODEL), full),
            pl.BlockSpec((2 * D_MODEL, D_MODEL), full),
            pl.BlockSpec((1, LANES), full),
            pl.BlockSpec((1, ATT_W), full),
            pl.BlockSpec((1, ATT_W), full),
            pl.BlockSpec((ATT_W, LANES), full),
            pl.BlockSpec((LANES, ATT_W), full),
        ],
        out_specs=[
            pl.BlockSpec((tm, ATT_W), row), pl.BlockSpec((tm, ATT_W), row),
            pl.BlockSpec((tm, ATT_W), row), pl.BlockSpec((tm, ATT_W), row), pl.BlockSpec((tm, ATT_W), row),
            pl.BlockSpec((tm, LANES), row),
            pl.BlockSpec((tm, ATT_W), row), pl.BlockSpec((tm, ATT_W), row), pl.BlockSpec((tm, ATT_W), row),
            pl.BlockSpec((tm, 2 * D_MODEL), row),
        ],
        out_shape=[
            att(F32), att(F32), att(BF16), att(BF16), att(BF16),
            jax.ShapeDtypeStruct((s_len, LANES), F32),
            att(BF16), att(BF16), att(BF16),
            jax.ShapeDtypeStruct((s_len, 2 * D_MODEL), F32),
        ],
        compiler_params=_cparams(56),
        name="inproj_fwd",
    )(x1, g, w_fox, w_fl, w_sb, w_gates, bias, qn, kn, bd, bd_t)


def _tri(n, kind):
    r = lax.broadcasted_iota(jnp.int32, (n, n), 0)
    c = lax.broadcasted_iota(jnp.int32, (n, n), 1)
    m = {"row_ge_col": r >= c, "row_le_col": r <= c, "row_gt_col": r > c, "row_lt_col": r < c}[kind]
    return m.astype(BF16)


def _cumsum_rows(x, reverse, tm=256):
    s_len = x.shape[0]
    nb = s_len // tm
    tri = _tri(tm, "row_le_col" if reverse else "row_ge_col")
    edge = 0 if reverse else tm - 1

    def body(x_ref, tri_ref, o_ref, carry_s):
        @pl.when(pl.program_id(0) == 0)
        def _():
            carry_s[...] = jnp.zeros_like(carry_s)

        hi, mid, lo = _split3(x_ref[...])
        t = tri_ref[...]
        y = _dot(t, hi) + _dot(t, mid) + _dot(t, lo) + carry_s[...]
        o_ref[...] = y
        carry_s[...] = y[edge:edge + 1, :]

    order = (lambda i: (nb - 1 - i, 0)) if reverse else (lambda i: (i, 0))
    return pl.pallas_call(
        body,
        grid=(nb,),
        in_specs=[pl.BlockSpec((tm, LANES), order), pl.BlockSpec((tm, tm), lambda i: (0, 0))],
        out_specs=pl.BlockSpec((tm, LANES), order),
        out_shape=jax.ShapeDtypeStruct((s_len, LANES), F32),
        scratch_shapes=[pltpu.VMEM((1, LANES), F32)],
        name="cumsum_rev" if reverse else "cumsum_fwd",
    )(x, tri)


def _unblocked_t(t4):
    _, nb, _, blk = t4.shape
    return t4.transpose(1, 3, 0, 2).reshape(nb * blk, ATT_W)


def _blocked_rows(t, blk):
    return t.reshape(t.shape[0] // blk, blk, t.shape[1])


def _pair_rows_t(f8, blk):
    nb = f8.shape[0] // blk
    t = f8.reshape(nb, blk, N_PAIRS, 2).transpose(2, 0, 3, 1)
    return jnp.pad(t, ((0, 0), (0, 0), (0, 6), (0, 0)))


def _unpair_rows_t(t4):
    _, nb, _, blk = t4.shape
    return t4[:, :, 0:2, :].transpose(1, 3, 0, 2).reshape(nb * blk, N_HEADS)


def _head_masks(tq):
    lane = lax.broadcasted_iota(jnp.int32, (tq, PAIR_W), 1)
    return lane < HEAD_DIM


def _causal_mask(tq, tk, offset, strict):
    d = lax.broadcasted_iota(jnp.int32, (tq, tk), 1) - lax.broadcasted_iota(jnp.int32, (tq, tk), 0)
    return (d < offset) if strict else (d <= offset)


def _heads_of(ref, first):
    t = ref[...]
    zero = jnp.zeros_like(t)
    return [jnp.where(first, t, zero), jnp.where(first, zero, t)]


def _head_cols(ref):
    t = ref[...]
    return [t[:, 0:1], t[:, HEAD_DIM:HEAD_DIM + 1]]


def _att_specs(s_len):
    tq, tk = ATT_Q_BLOCK, ATT_BLOCK
    nq, nk = s_len // tq, s_len // tk
    return dict(
        nq=nq,
        q=pl.BlockSpec((tq, PAIR_W), lambda p, i: (i, p)),
        k_t=pl.BlockSpec((1, nk, PAIR_W, tk), lambda p, i: (p, 0, 0, 0)),
        k_rows=pl.BlockSpec((nk, tk, PAIR_W), lambda p, i: (0, 0, p)),
        f_t=pl.BlockSpec((1, nk, 8, tk), lambda p, i: (p, 0, 0, 0)),
        first=pl.BlockSpec((1, 1, 8, LANES), lambda p, i: (p, i, 0, 0)),
        wide=jax.ShapeDtypeStruct((s_len, ATT_W), F32),
        k_t_out=jax.ShapeDtypeStruct((N_PAIRS, nk, PAIR_W, tk), F32),
        f_t_out=jax.ShapeDtypeStruct((N_PAIRS, nk, 8, tk), F32),
        first_out=jax.ShapeDtypeStruct((N_PAIRS, nq, 8, LANES), F32),
        acc=pltpu.VMEM((2, tq, PAIR_W), F32),
    )


def _first_block(first_ref, limit):
    return jnp.clip(jnp.max(first_ref[0, 0]).astype(jnp.int32), 0, limit)


def _key_norm_bound(k):
    sq = jnp.sum(jnp.square(k.astype(F32)).reshape(k.shape[0], N_HEADS, HEAD_DIM), axis=-1)
    bound = jnp.sqrt(jnp.max(sq, axis=0)).reshape(N_PAIRS, 2)
    return jnp.broadcast_to(jnp.pad(bound, ((0, 0), (0, 6)))[:, :, None], (N_PAIRS, 8, LANES))


def _fox_fwd(qs, k3, v3, fw, ft4, kmax):
    sp = _att_specs(qs.shape[0])
    tq, tk = ATT_Q_BLOCK, ATT_BLOCK
    ratio = tq // tk

    def body(q_ref, k_ref, v_ref, fw_ref, ft_ref, kmax_ref, y_ref, lse_ref, first_ref, acc_ref, max_ref, sum_ref):
        i = pl.program_id(1)
        first = _head_masks(tq)
        qh = _heads_of(q_ref, first)
        fqh = _head_cols(fw_ref)
        acc_ref[...] = jnp.zeros_like(acc_ref)
        sum_ref[...] = jnp.zeros_like(sum_ref)
        max_ref[...] = jnp.full(max_ref.shape, NEG_BIG, F32)
        reach = []
        for n in range(2):
            qf = qh[n].astype(F32)
            reach.append(jnp.sqrt(jnp.sum(qf * qf, axis=-1, keepdims=True)) * kmax_ref[0, n:n + 1, 0:1] + fqh[n])

        def logits(j, shift, diag):
            k, fk = k_ref[j], ft_ref[0, j]
            raw = [_dot_nt(qh[n], k) for n in range(2)]
            out = []
            for n in range(2):
                s = raw[n] + (shift[n] - fk[n:n + 1, :])
                if diag:
                    s = jnp.where(_causal_mask(tq, tk, i * tq - j * tk, strict=False), s, NEG_BIG)
                out.append(s)
            return out

        def max_pass(j, diag):
            ss = logits(j, fqh, diag)
            for n in range(2):
                max_ref[n] = jnp.maximum(max_ref[n], ss[n])

        def sum_pass(j, shift, diag):
            ps = [jnp.exp(s) for s in logits(j, shift, diag)]
            v = v_ref[j]
            for n in range(2):
                sum_ref[n] += ps[n]
            for n in range(2):
                acc_ref[n] += _dot(ps[n].astype(BF16), v)

        for d in range(ratio):
            max_pass(ratio * i + d, True)

        def block_matters(j):
            gap = []
            for n in range(2):
                m_run = jnp.max(max_ref[n], axis=-1, keepdims=True)
                f_end = ft_ref[0, jnp.maximum(j, 0)][n:n + 1, tk - 1:tk]
                gap.append(jnp.max(reach[n] - m_run) - jnp.max(f_end))
            return (j >= 0) & (jnp.maximum(gap[0], gap[1]) > -EXP_UNDERFLOW)

        def walk_left(j):
            max_pass(j, False)
            return j - 1

        j_first = lax.while_loop(block_matters, walk_left, ratio * i - 1) + 1
        m = [jnp.max(max_ref[n], axis=-1, keepdims=True) for n in range(2)]
        shift = [fqh[n] - m[n] for n in range(2)]

        def one(j, c):
            sum_pass(j, shift, False)
            return c
        lax.fori_loop(j_first, ratio * i, one, 0)
        for d in range(ratio):
            sum_pass(ratio * i + d, shift, True)
        l = [jnp.sum(sum_ref[n], axis=-1, keepdims=True) for n in range(2)]
        y_ref[...] = jnp.where(first, acc_ref[0] / l[0], acc_ref[1] / l[1])
        lse_ref[...] = jnp.where(first, m[0] + jnp.log(l[0]), m[1] + jnp.log(l[1]))
        first_ref[...] = jnp.ones(first_ref.shape, F32) * j_first.astype(F32)

    tile = pltpu.VMEM((2, tq, tk), F32)
    return pl.pallas_call(
        body,
        grid=(N_PAIRS, sp["nq"]),
        in_specs=[sp["q"], sp["k_rows"], sp["k_rows"], sp["q"], sp["f_t"],
                  pl.BlockSpec((1, 8, LANES), lambda p, i: (p, 0, 0))],
        out_specs=[sp["q"], sp["q"], sp["first"]],
        out_shape=[sp["wide"], sp["wide"], sp["first_out"]],
        scratch_shapes=[sp["acc"], tile, tile],
        compiler_params=_cparams(56),
        name="fox_fwd",
    )(qs, k3, v3, fw, ft4, kmax)


def _fox_bwd(qs, k3, v3, dy, y, lse, fw, ft4, first_block):
    sp = _att_specs(qs.shape[0])
    tq, tk = ATT_Q_BLOCK, ATT_BLOCK
    ratio = tq // tk

    def body(q_ref, k_ref, v_ref, dy_ref, y_ref, lse_ref, fw_ref, ft_ref, first_ref,
             dq_ref, dfq_ref, dkt_ref, dvt_ref, dft_ref, acc_ref):
        i = pl.program_id(1)

        @pl.when(i == 0)
        def _():
            dkt_ref[...] = jnp.zeros_like(dkt_ref)
            dvt_ref[...] = jnp.zeros_like(dvt_ref)
            dft_ref[...] = jnp.zeros_like(dft_ref)

        first = _head_masks(tq)
        qh = _heads_of(q_ref, first)
        dyv = dy_ref[...]
        dyb = dyv.astype(BF16)
        zero = jnp.zeros_like(dyb)
        dyh = [jnp.where(first, dyb, zero), jnp.where(first, zero, dyb)]
        prod = dyv * y_ref[...]
        zf = jnp.zeros_like(prod)
        delta = [jnp.sum(jnp.where(first, prod, zf), axis=-1, keepdims=True),
                 jnp.sum(jnp.where(first, zf, prod), axis=-1, keepdims=True)]
        fqh = _head_cols(fw_ref)
        lseh = _head_cols(lse_ref)
        shift = [fqh[n] - lseh[n] for n in range(2)]
        acc_ref[...] = jnp.zeros_like(acc_ref)

        def block(j, rows, diag):
            mask = _causal_mask(tq, tk, i * tq - j * tk, strict=False) if diag else None
            k, v, fk = k_ref[j], v_ref[j], ft_ref[0, j]
            logits = [_dot_nt(qh[n], k) for n in range(2)]
            dps = [_dot_nt(dyh[n], v) for n in range(2)]
            pbs, dsbs, out = [], [], []
            for n in range(2):
                p = jnp.exp(logits[n] + (shift[n] - fk[n:n + 1, :]))
                if diag:
                    p = jnp.where(mask, p, 0.0)
                ds = p * (dps[n] - delta[n])
                pbs.append(p.astype(BF16))
                dsbs.append(ds.astype(BF16))
                out.append(rows[n] + jnp.sum(ds, axis=-1, keepdims=True))
                dft_ref[0, j, n:n + 1, :] -= _colsum(ds)
            for n in range(2):
                acc_ref[n] += _dot(dsbs[n], k)
            dkt_ref[0, j] += _dot_tn(qh[0], dsbs[0]) + _dot_tn(qh[1], dsbs[1])
            dvt_ref[0, j] += _dot_tn(dyh[0], pbs[0]) + _dot_tn(dyh[1], pbs[1])
            return tuple(out)

        rows = (jnp.zeros((tq, 1), F32),) * 2
        rows = lax.fori_loop(_first_block(first_ref, ratio * i), ratio * i, lambda j, c: block(j, c, False), rows)
        for d in range(ratio):
            rows = block(ratio * i + d, rows, True)
        dq_ref[...] = jnp.where(first, acc_ref[0], acc_ref[1])
        dfq_ref[...] = jnp.where(first, rows[0], rows[1])

    return pl.pallas_call(
        body,
        grid=(N_PAIRS, sp["nq"]),
        in_specs=[sp["q"], sp["k_rows"], sp["k_rows"], sp["q"], sp["q"], sp["q"], sp["q"], sp["f_t"], sp["first"]],
        out_specs=[sp["q"], sp["q"], sp["k_t"], sp["k_t"], sp["f_t"]],
        out_shape=[sp["wide"], sp["wide"], sp["k_t_out"], sp["k_t_out"], sp["f_t_out"]],
        scratch_shapes=[sp["acc"]],
        compiler_params=_cparams(56),
        name="fox_bwd",
    )(qs, k3, v3, dy, y, lse, fw, ft4, first_block)


SIGN_BIT = 0x80000000


def _sb_terms(z, mask, diag):
    neg_abs = pltpu.bitcast(pltpu.bitcast(z, jnp.uint32) | jnp.uint32(SIGN_BIT), F32)
    lb = jnp.minimum(z, 0.0) - jnp.log(1.0 + jnp.exp(neg_abs))
    l1m = lb - z
    if diag:
        l1m = jnp.where(mask, l1m, 0.0)
    return lb, l1m


def _dot_split2_stacked(x, m2):
    hi, lo = _split2(x)
    return _dot(jnp.concatenate([hi, lo], axis=1), m2)


def _tri_stacked(kind):
    t = _tri(ATT_BLOCK, kind)
    return jnp.concatenate([t, t], axis=0)


def _sb_fwd(qs, k3, v3):
    sp = _att_specs(qs.shape[0])
    tq, tk = ATT_Q_BLOCK, ATT_BLOCK
    ratio = tq // tk
    upper = _tri_stacked("row_gt_col")

    def body(q_ref, k_ref, v_ref, u_ref, y_ref, rtot_ref, first_ref, acc_ref):
        i = pl.program_id(1)
        first = _head_masks(tq)
        qh = _heads_of(q_ref, first)
        u = u_ref[...]
        acc_ref[...] = jnp.zeros_like(acc_ref)

        def block(j, rs, diag):
            mask = _causal_mask(tq, tk, i * tq - j * tk, strict=True) if diag else None
            k, v = k_ref[j], v_ref[j]
            logits = [_dot_nt(qh[n], k) for n in range(2)]
            terms = [_sb_terms(z, mask, diag) for z in logits]
            right = [_dot_split2_stacked(l1m, u) for _, l1m in terms]
            weights = []
            for n in range(2):
                a = jnp.exp(terms[n][0] + right[n] + rs[n])
                if diag:
                    a = jnp.where(mask, a, 0.0)
                weights.append(a.astype(BF16))
            for n in range(2):
                acc_ref[n] += _dot(weights[n], v)
            return tuple(rs[n] + jnp.sum(terms[n][1], axis=-1, keepdims=True) for n in range(2))

        rs = (jnp.zeros((tq, 1), F32),) * 2
        for d in range(ratio):
            rs = block(ratio * i + (ratio - 1 - d), rs, True)

        def block_matters(c):
            j, r0, r1 = c
            return (j >= 0) & (jnp.max(jnp.maximum(r0, r1)) > -EXP_UNDERFLOW)

        def walk_left(c):
            j, r0, r1 = c
            r0, r1 = block(j, (r0, r1), False)
            return j - 1, r0, r1

        j, r0, r1 = lax.while_loop(block_matters, walk_left, (ratio * i - 1, rs[0], rs[1]))
        y_ref[...] = jnp.where(first, acc_ref[0], acc_ref[1])
        rtot_ref[...] = jnp.where(first, r0, r1)
        first_ref[...] = jnp.ones(first_ref.shape, F32) * (j + 1).astype(F32)

    return pl.pallas_call(
        body,
        grid=(N_PAIRS, sp["nq"]),
        in_specs=[sp["q"], sp["k_rows"], sp["k_rows"], pl.BlockSpec((2 * tk, tk), lambda p, i: (0, 0))],
        out_specs=[sp["q"], sp["q"], sp["first"]],
        out_shape=[sp["wide"], sp["wide"], sp["first_out"]],
        scratch_shapes=[sp["acc"]],
        compiler_params=_cparams(56),
        name="sb_fwd",
    )(qs, k3, v3, upper)


def _sb_bwd(qs, k3, v3, dy, rtot, first_block):
    sp = _att_specs(qs.shape[0])
    tq, tk = ATT_Q_BLOCK, ATT_BLOCK
    ratio = tq // tk
    lower_in = _tri_stacked("row_le_col")
    lower = _tri(tk, "row_lt_col")

    def body(q_ref, k_ref, v_ref, dy_ref, rtot_ref, first_ref, li_ref, l_ref, dq_ref, dkt_ref, dvt_ref, acc_ref):
        i = pl.program_id(1)

        @pl.when(i == 0)
        def _():
            dkt_ref[...] = jnp.zeros_like(dkt_ref)
            dvt_ref[...] = jnp.zeros_like(dvt_ref)

        first = _head_masks(tq)
        qh = _heads_of(q_ref, first)
        dyb = dy_ref[...].astype(BF16)
        zero = jnp.zeros_like(dyb)
        dyh = [jnp.where(first, dyb, zero), jnp.where(first, zero, dyb)]
        rtoth = _head_cols(rtot_ref)
        li = li_ref[...]
        lo_tri = l_ref[...]
        acc_ref[...] = jnp.zeros_like(acc_ref)

        def block(j, carry, diag):
            mask = _causal_mask(tq, tk, i * tq - j * tk, strict=True) if diag else None
            k, v = k_ref[j], v_ref[j]
            logits = [_dot_nt(qh[n], k) for n in range(2)]
            das = [_dot_nt(dyh[n], v) for n in range(2)]
            terms = [_sb_terms(z, mask, diag) for z in logits]
            upto = [_dot_split2_stacked(l1m, li) for _, l1m in terms]
            des, weights = [], []
            for n in range(2):
                a = jnp.exp(terms[n][0] + ((rtoth[n] - carry[2 * n]) - upto[n]))
                if diag:
                    a = jnp.where(mask, a, 0.0)
                des.append(a * das[n])
                weights.append(a.astype(BF16))
            lefts = [_dot(de.astype(BF16), lo_tri) for de in des]
            dzbs, out = [], []
            for n in range(2):
                beta = jnp.exp(terms[n][0])
                dz = des[n] - (des[n] + (carry[2 * n + 1] + lefts[n])) * beta
                if diag:
                    dz = jnp.where(mask, dz, 0.0)
                dzbs.append(dz.astype(BF16))
                out += [carry[2 * n] + jnp.sum(terms[n][1], axis=-1, keepdims=True),
                        carry[2 * n + 1] + jnp.sum(des[n], axis=-1, keepdims=True)]
            for n in range(2):
                acc_ref[n] += _dot(dzbs[n], k)
            dkt_ref[0, j] += _dot_tn(qh[0], dzbs[0]) + _dot_tn(qh[1], dzbs[1])
            dvt_ref[0, j] += _dot_tn(dyh[0], weights[0]) + _dot_tn(dyh[1], weights[1])
            return tuple(out)

        carry = (jnp.zeros((tq, 1), F32),) * 4
        carry = lax.fori_loop(_first_block(first_ref, ratio * i), ratio * i, lambda j, c: block(j, c, False), carry)
        for d in range(ratio):
            carry = block(ratio * i + d, carry, True)
        dq_ref[...] = jnp.where(first, acc_ref[0], acc_ref[1])

    return pl.pallas_call(
        body,
        grid=(N_PAIRS, sp["nq"]),
        in_specs=[sp["q"], sp["k_rows"], sp["k_rows"], sp["q"], sp["q"], sp["first"],
                  pl.BlockSpec((2 * tk, tk), lambda p, i: (0, 0)), pl.BlockSpec((tk, tk), lambda p, i: (0, 0))],
        out_specs=[sp["q"], sp["k_t"], sp["k_t"]],
        out_shape=[sp["wide"], sp["k_t_out"], sp["k_t_out"]],
        scratch_shapes=[sp["acc"]],
        compiler_params=_cparams(56),
        name="sb_bwd",
    )(qs, k3, v3, dy, rtot, first_block, lower_in, lower)


def _merge_fwd(x1, gates, y_fox, y_sb, w_bf, w_bs, w_out, tm=512):
    s_len = x1.shape[0]

    def body(x_ref, g_ref, yf_ref, ys_ref, wbf_ref, wbs_ref, wo_ref, o_ref):
        g = g_ref[...]
        of = _dot(yf_ref[...].astype(BF16), wbf_ref[...])
        os_ = _dot(ys_ref[...].astype(BF16), wbs_ref[...])
        merged = _sigmoid(g[:, 0:D_MODEL]) * of + _sigmoid(g[:, D_MODEL:]) * os_
        o_ref[...] = x_ref[...] + _dot(merged.astype(BF16), wo_ref[...])

    row = lambda i: (i, 0)
    full = lambda i: (0, 0)
    return pl.pallas_call(
        body,
        grid=(s_len // tm,),
        in_specs=[
            pl.BlockSpec((tm, D_MODEL), row),
            pl.BlockSpec((tm, 2 * D_MODEL), row),
            pl.BlockSpec((tm, ATT_W), row),
            pl.BlockSpec((tm, ATT_W), row),
            pl.BlockSpec((ATT_W, D_MODEL), full),
            pl.BlockSpec((ATT_W, D_MODEL), full),
            pl.BlockSpec((D_MODEL, D_MODEL), full),
        ],
        out_specs=pl.BlockSpec((tm, D_MODEL), row),
        out_shape=jax.ShapeDtypeStruct((s_len, D_MODEL), F32),
        compiler_params=_cparams(48),
        name="merge_fwd",
    )(x1, gates, y_fox, y_sb, w_bf, w_bs, w_out)


def _merge_bwd(dx2, gates, y_fox, y_sb, w_bf, w_bs, w_out, tm=512):
    s_len = dx2.shape[0]

    def body(d_ref, g_ref, yf_ref, ys_ref, wbf_ref, wbs_ref, wo_ref,
             dyf_ref, dys_ref, dg_ref, dof_ref, dos_ref, m_ref, dbf_ref):
        dbf = d_ref[...].astype(BF16)
        dbf_ref[...] = dbf
        dm = _dot_nt(dbf, wo_ref[...])
        g = g_ref[...]
        of = _dot(yf_ref[...].astype(BF16), wbf_ref[...])
        os_ = _dot(ys_ref[...].astype(BF16), wbs_ref[...])
        sf = _sigmoid(g[:, 0:D_MODEL])
        ss = _sigmoid(g[:, D_MODEL:])
        m_ref[...] = (sf * of + ss * os_).astype(BF16)
        d_of = (dm * sf).astype(BF16)
        d_os = (dm * ss).astype(BF16)
        dof_ref[...] = d_of
        dos_ref[...] = d_os
        dg_ref[:, 0:D_MODEL] = (dm * of * sf * (1.0 - sf)).astype(BF16)
        dg_ref[:, D_MODEL:] = (dm * os_ * ss * (1.0 - ss)).astype(BF16)
        dyf_ref[...] = _dot_nt(d_of, wbf_ref[...])
        dys_ref[...] = _dot_nt(d_os, wbs_ref[...])

    row = lambda i: (i, 0)
    full = lambda i: (0, 0)
    return pl.pallas_call(
        body,
        grid=(s_len // tm,),
        in_specs=[
            pl.BlockSpec((tm, D_MODEL), row),
            pl.BlockSpec((tm, 2 * D_MODEL), row),
            pl.BlockSpec((tm, ATT_W), row),
            pl.BlockSpec((tm, ATT_W), row),
            pl.BlockSpec((ATT_W, D_MODEL), full),
            pl.BlockSpec((ATT_W, D_MODEL), full),
            pl.BlockSpec((D_MODEL, D_MODEL), full),
        ],
        out_specs=[
            pl.BlockSpec((tm, ATT_W), row), pl.BlockSpec((tm, ATT_W), row),
            pl.BlockSpec((tm, 2 * D_MODEL), row),
            pl.BlockSpec((tm, D_MODEL), row), pl.BlockSpec((tm, D_MODEL), row),
            pl.BlockSpec((tm, D_MODEL), row), pl.BlockSpec((tm, D_MODEL), row),
        ],
        out_shape=[
            jax.ShapeDtypeStruct((s_len, ATT_W), F32), jax.ShapeDtypeStruct((s_len, ATT_W), F32),
            jax.ShapeDtypeStruct((s_len, 2 * D_MODEL), BF16),
            jax.ShapeDtypeStruct((s_len, D_MODEL), BF16), jax.ShapeDtypeStruct((s_len, D_MODEL), BF16),
            jax.ShapeDtypeStruct((s_len, D_MODEL), BF16), jax.ShapeDtypeStruct((s_len, D_MODEL), BF16),
        ],
        compiler_params=_cparams(56),
        name="merge_bwd",
    )(dx2, gates, y_fox, y_sb, w_bf, w_bs, w_out)


def _ple_loss(x3, p, g, w_pg, w_pp, target, tm=512):
    s_len = x3.shape[0]
    inv_d = 1.0 / D_MODEL

    def body(x_ref, p_ref, g_ref, wpg_ref, wpp_ref, t_ref,
             dx_ref, du_ref, dt_ref, hn_ref, dg_ref, loss_ref):
        @pl.when(pl.program_id(0) == 0)
        def _():
            dg_ref[...] = jnp.zeros_like(dg_ref)
            loss_ref[...] = jnp.zeros_like(loss_ref)

        x = x_ref[...]
        xn, r = _rms(x)
        gain = g_ref[...]
        hn = (xn * gain).astype(BF16)
        hn_ref[...] = hn
        sg = _sigmoid(_dot(hn, wpg_ref[...]))
        t = _dot(p_ref[...].astype(BF16), wpp_ref[...])
        err = x + sg * t - t_ref[...]
        sq = jnp.sum(_colsum(err * err), axis=-1, keepdims=True)
        loss_ref[...] += (0.5 * inv_d) * sq
        dy = err * inv_d
        du = (dy * t * sg * (1.0 - sg)).astype(BF16)
        du_ref[...] = du
        dt_ref[...] = (dy * sg).astype(BF16)
        dh = _dot_nt(du, wpg_ref[...])
        dx_ref[...] = dy + _rms_bwd(dh, xn, r, gain)
        dg_ref[0:1, :] += _colsum(dh * xn)

    row = lambda i: (i, 0)
    full = lambda i: (0, 0)
    bf = jax.ShapeDtypeStruct((s_len, D_MODEL), BF16)
    return pl.pallas_call(
        body,
        grid=(s_len // tm,),
        in_specs=[
            pl.BlockSpec((tm, D_MODEL), row),
            pl.BlockSpec((tm, PLE_DIM), row),
            pl.BlockSpec((1, D_MODEL), full),
            pl.BlockSpec((D_MODEL, D_MODEL), full),
            pl.BlockSpec((PLE_DIM, D_MODEL), full),
            pl.BlockSpec((tm, D_MODEL), row),
        ],
        out_specs=[
            pl.BlockSpec((tm, D_MODEL), row), pl.BlockSpec((tm, D_MODEL), row),
            pl.BlockSpec((tm, D_MODEL), row), pl.BlockSpec((tm, D_MODEL), row),
            pl.BlockSpec((8, D_MODEL), full), pl.BlockSpec((8, LANES), full),
        ],
        out_shape=[
            jax.ShapeDtypeStruct((s_len, D_MODEL), F32), bf, bf, bf,
            jax.ShapeDtypeStruct((8, D_MODEL), F32), jax.ShapeDtypeStruct((8, LANES), F32),
        ],
        compiler_params=_cparams(48),
        name="ple_loss",
    )(x3, p, g, w_pg, w_pp, target)


def _qknorm_bwd(fq, fk, dqs, dk, dv, qn, kn, bd, bd_t, tm=256):
    s_len = fq.shape[0]

    def body(fq_ref, fk_ref, dq_ref, dk_ref, dv_ref, qn_ref, kn_ref, bd_ref, bdt_ref,
             dz_ref, dqn_ref, dkn_ref):
        @pl.when(pl.program_id(0) == 0)
        def _():
            dqn_ref[...] = jnp.zeros_like(dqn_ref)
            dkn_ref[...] = jnp.zeros_like(dkn_ref)

        bd_m = bd_ref[...]
        bdt_m = bdt_ref[...]

        def one(x, dy, gain, dgain_ref):
            xn, rw = _head_rms(x, bd_m, bdt_m)
            dgain_ref[0:1, :] += _colsum(dy * xn)
            dxn = dy * gain
            return rw * (dxn - xn * _head_mean(dxn * xn, bd_m, bdt_m))

        dz_ref[:, 0:ATT_W] = one(fq_ref[...], dq_ref[...] * QK_SCALE, qn_ref[...], dqn_ref).astype(BF16)
        dz_ref[:, ATT_W:2 * ATT_W] = one(fk_ref[...], dk_ref[...], kn_ref[...], dkn_ref).astype(BF16)
        dz_ref[:, 2 * ATT_W:] = dv_ref[...].astype(BF16)

    row = lambda i: (i, 0)
    full = lambda i: (0, 0)
    att = pl.BlockSpec((tm, ATT_W), row)
    return pl.pallas_call(
        body,
        grid=(s_len // tm,),
        in_specs=[att, att, att, att, att,
                  pl.BlockSpec((1, ATT_W), full), pl.BlockSpec((1, ATT_W), full),
                  pl.BlockSpec((ATT_W, LANES), full), pl.BlockSpec((LANES, ATT_W), full)],
        out_specs=[pl.BlockSpec((tm, 3 * ATT_W), row), pl.BlockSpec((8, ATT_W), full), pl.BlockSpec((8, ATT_W), full)],
        out_shape=[jax.ShapeDtypeStruct((s_len, 3 * ATT_W), BF16),
                   jax.ShapeDtypeStruct((8, ATT_W), F32), jax.ShapeDtypeStruct((8, ATT_W), F32)],
        name="qknorm_bwd",
    )(fq, fk, dqs, dk, dv, qn, kn, bd, bd_t)


def _inproj_bwd(x1, dx2, g, dzf, dlogf, logf, dzs, dgates, w_fox, w_fl, w_sb, w_gates, tm=256):
    s_len = x1.shape[0]

    def body(x_ref, d_ref, g_ref, dzf_ref, dlf_ref, lf_ref, dzs_ref, dgt_ref, wf_ref, wl_ref, ws_ref, wg_ref,
             dx_ref, h_ref, dfl_ref, dg_ref, db_ref):
        @pl.when(pl.program_id(0) == 0)
        def _():
            dg_ref[...] = jnp.zeros_like(dg_ref)
            db_ref[...] = jnp.zeros_like(db_ref)

        xn, r = _rms(x_ref[...])
        gain = g_ref[...]
        h_ref[...] = (xn * gain).astype(BF16)
        lane = lax.broadcasted_iota(jnp.int32, (tm, LANES), 1)
        dfl = jnp.where(lane < N_HEADS, dlf_ref[...] * (1.0 - jnp.exp(lf_ref[...])), 0.0)
        db_ref[0:1, :] += _colsum(dfl)
        dflb = dfl.astype(BF16)
        dfl_ref[...] = dflb
        dh = (_dot(dzf_ref[...], wf_ref[...]) + _dot(dflb, wl_ref[...])
              + _dot(dzs_ref[...], ws_ref[...]) + _dot(dgt_ref[...], wg_ref[...]))
        dx_ref[...] = d_ref[...] + _rms_bwd(dh, xn, r, gain)
        dg_ref[0:1, :] += _colsum(dh * xn)

    row = lambda i: (i, 0)
    full = lambda i: (0, 0)
    return pl.pallas_call(
        body,
        grid=(s_len // tm,),
        in_specs=[
            pl.BlockSpec((tm, D_MODEL), row),
            pl.BlockSpec((tm, D_MODEL), row),
            pl.BlockSpec((1, D_MODEL), full),
            pl.BlockSpec((tm, 3 * ATT_W), row),
            pl.BlockSpec((tm, LANES), row),
            pl.BlockSpec((tm, LANES), row),
            pl.BlockSpec((tm, 3 * ATT_W), row),
            pl.BlockSpec((tm, 2 * D_MODEL), row),
            pl.BlockSpec((3 * ATT_W, D_MODEL), full),
            pl.BlockSpec((LANES, D_MODEL), full),
            pl.BlockSpec((3 * ATT_W, D_MODEL), full),
            pl.BlockSpec((2 * D_MODEL, D_MODEL), full),
        ],
        out_specs=[
            pl.BlockSpec((tm, D_MODEL), row), pl.BlockSpec((tm, D_MODEL), row), pl.BlockSpec((tm, LANES), row),
            pl.BlockSpec((8, D_MODEL), full), pl.BlockSpec((8, LANES), full),
        ],
        out_shape=[
            jax.ShapeDtypeStruct((s_len, D_MODEL), F32), jax.ShapeDtypeStruct((s_len, D_MODEL), BF16),
            jax.ShapeDtypeStruct((s_len, LANES), BF16),
            jax.ShapeDtypeStruct((8, D_MODEL), F32), jax.ShapeDtypeStruct((8, LANES), F32),
        ],
        compiler_params=_cparams(56),
        name="inproj_bwd",
    )(x1, dx2, g, dzf, dlogf, logf, dzs, dgates, w_fox, w_fl, w_sb, w_gates)


def _split_w_in(w_in_t):
    o = 3 * ATT_W
    w_fox = w_in_t[0:o]
    w_fl = jnp.pad(w_in_t[o:o + N_HEADS], ((0, LANES - N_HEADS), (0, 0)))
    w_sb = w_in_t[o + N_HEADS:2 * o + N_HEADS]
    w_gates = w_in_t[2 * o + N_HEADS:]
    return w_fox, w_fl, w_sb, w_gates


def _local_grads(x, p, target, small, full, pending=None, send_early=None):
    blk = ATT_BLOCK
    bd, bd_t = _head_sum_matrices()
    full = dict(full)
    late = list(pending) if pending else []

    x1, a1, b1, *gathered = _ffn_fwd(x, small["ffn1_norm"], full["ffn1_w_gate"], full["ffn1_w_up"],
                                     full["ffn1_w_down"], gather=[pending[k] for k in late])
    for k, gth in zip(late, gathered):
        full[k] = gth if k in KEPT_AS_SHARDS else _whole(k, gth)
    w_fox, w_fl, w_sb, w_gates = _split_w_in(full["w_in"])
    bias = jnp.pad(small["forget_bias"], ((0, 0), (0, LANES - N_HEADS)))
    qn = jnp.tile(small["q_norm"], (1, N_HEADS))
    kn = jnp.tile(small["k_norm"], (1, N_HEADS))
    fq, fk, f_qs, f_k, f_v, logf, s_qs, s_k, s_v, gates = _inproj_fwd(
        x1, small["mix_norm"], w_fox, w_fl, w_sb, w_gates, bias, qn, kn, bd, bd_t)
    f_cum = _cumsum_rows(logf, reverse=False)
    f8 = f_cum[:, 0:N_HEADS]
    fw = jnp.repeat(f8, HEAD_DIM, axis=1)
    ft4 = _pair_rows_t(f8, blk)
    f_k3, f_v3 = _blocked_rows(f_k, blk), _blocked_rows(f_v, blk)
    y_fox, lse, f_first = _fox_fwd(f_qs, f_k3, f_v3, fw, ft4, _key_norm_bound(f_k))
    s_k3, s_v3 = _blocked_rows(s_k, blk), _blocked_rows(s_v, blk)
    y_sb, s_rtot, s_first = _sb_fwd(s_qs, s_k3, s_v3)
    x2 = _merge_fwd(x1, gates, y_fox, y_sb, full["w_branch_fox"], full["w_branch_sb"], full["w_out"])
    x3, a2, b2 = _ffn_fwd(x2, small["ffn2_norm"], full["ffn2_w_gate"], full["ffn2_w_up"], full["ffn2_w_down"])

    dx3, du_ple, dt_ple, hn_ple, dg_ple, loss_sum = _ple_loss(
        x3, p, small["ple_norm"], full["w_ple_gate"], full["w_ple_proj"], target)
    dx2, u2, da2, db2, h_ffn2, d3_bf, dg_ffn2 = _ffn_bwd(
        x2, dx3, small["ffn2_norm"], a2, b2, full["ffn2_w_gate"], full["ffn2_w_up"], full["ffn2_w_down"])
    dy_fox, dy_sb, dgates, d_of, d_os, merged, d2_bf = _merge_bwd(
        dx2, gates, y_fox, y_sb, full["w_branch_fox"], full["w_branch_sb"], full["w_out"])

    f_dqs, dfq_w, f_dkt4, f_dvt4, dft4 = _fox_bwd(f_qs, f_k3, f_v3, dy_fox, y_fox, lse, fw, ft4, f_first)
    s_dqs, s_dkt4, s_dvt4 = _sb_bwd(s_qs, s_k3, s_v3, dy_sb, s_rtot, s_first)

    dzf, dqn8, dkn8 = _qknorm_bwd(fq, fk, f_dqs, _unblocked_t(f_dkt4), _unblocked_t(f_dvt4), qn, kn, bd, bd_t)
    dzs = jnp.concatenate([s_dqs * QK_SCALE, _unblocked_t(s_dkt4), _unblocked_t(s_dvt4)], axis=1).astype(BF16)
    df8 = _unpair_rows_t(dft4) + dfq_w[:, ::HEAD_DIM]
    dlogf = _cumsum_rows(jnp.pad(df8, ((0, 0), (0, LANES - N_HEADS))), reverse=True)
    dx1, h_mix, dfl, dg_mix, dbias8 = _inproj_bwd(
        x1, dx2, small["mix_norm"], dzf, dlogf, logf, dzs, dgates, w_fox, w_fl, w_sb, w_gates)

    one = lambda t: t[None]
    gw = {}
    gw["ffn2_w_gate"] = _wgrad(da2, one(h_ffn2), name="wgrad_ffn2_gate")
    gw["ffn2_w_up"] = _wgrad(db2, one(h_ffn2), name="wgrad_ffn2_up")
    gw["ffn2_w_down"] = _wgrad(u2, one(d3_bf), scale=0.5, name="wgrad_ffn2_down")
    g_fox = _wgrad(one(dzf), one(h_mix), name="wgrad_in_fox")[0]
    g_fl = _wgrad(one(dfl), one(h_mix), name="wgrad_in_forget")[0]
    g_sb = _wgrad(one(dzs), one(h_mix), name="wgrad_in_sb")[0]
    g_gt = _wgrad(one(dgates), one(h_mix), name="wgrad_in_gates")[0]
    gw["w_in"] = jnp.concatenate([g_fox, g_fl[0:N_HEADS], g_sb, g_gt], axis=0)
    gw["w_branch_fox"] = _wgrad(one(y_fox), one(d_of), name="wgrad_branch_fox")[0]
    gw["w_branch_sb"] = _wgrad(one(y_sb), one(d_os), name="wgrad_branch_sb")[0]
    gw["w_out"] = _wgrad(one(merged), one(d2_bf), name="wgrad_out")[0]
    gw["w_ple_gate"] = _wgrad(one(hn_ple), one(du_ple), name="wgrad_ple_gate")[0]
    gw["w_ple_proj"] = _wgrad(one(p), one(dt_ple), name="wgrad_ple_proj")[0]

    sent_names, to_send = send_early(gw) if send_early else ([], [])
    grad_x, u1, da1, db1, h_ffn1, d1_bf, dg_ffn1, *landed = _ffn_bwd(
        x, dx1, small["ffn1_norm"], a1, b1, full["ffn1_w_gate"], full["ffn1_w_up"], full["ffn1_w_down"],
        scatter=to_send)
    gw["ffn1_w_gate"] = _wgrad(da1, one(h_ffn1), name="wgrad_ffn1_gate")
    gw["ffn1_w_up"] = _wgrad(db1, one(h_ffn1), name="wgrad_ffn1_up")
    gw["ffn1_w_down"] = _wgrad(u1, one(d1_bf), scale=0.5, name="wgrad_ffn1_down")

    fold = lambda t: jnp.sum(t[0:1].reshape(N_HEADS, HEAD_DIM), axis=0, keepdims=True)
    gs = {
        "ffn1_norm": dg_ffn1[0:1], "mix_norm": dg_mix[0:1], "ffn2_norm": dg_ffn2[0:1], "ple_norm": dg_ple[0:1],
        "forget_bias": dbias8[0:1, 0:N_HEADS], "q_norm": fold(dqn8), "k_norm": fold(dkn8),
    }
    return loss_sum, grad_x, gw, gs, dict(zip(sent_names, landed))


def _position():
    return lax.axis_index("x"), lax.axis_index("y"), lax.axis_index("c")


def _other_chips(x, y):
    return [(1 - x, y), (x, 1 - y), (1 - x, 1 - y)]


ANY = pl.BlockSpec(memory_space=pl.ANY)


def _place_own_shard(w, q):
    rows, cols = w.shape
    tr = _row_block(rows, cols * 4, budget=2 * MIB)

    def body(q_ref, w_ref, o_ref):
        o_ref[0] = w_ref[...].astype(BF16)

    return pl.pallas_call(
        body,
        grid_spec=pltpu.PrefetchScalarGridSpec(
            num_scalar_prefetch=1,
            grid=(rows // tr,),
            in_specs=[pl.BlockSpec((tr, cols), lambda i, q_ref: (i, 0))],
            out_specs=pl.BlockSpec((1, tr, cols), lambda i, q_ref: (q_ref[0], i, 0)),
        ),
        out_shape=jax.ShapeDtypeStruct((N_CHIPS, rows, cols), BF16),
        name="place_own_shard",
    )(q, w)


def _gather_semaphores(n):
    return [pltpu.SemaphoreType.DMA((6 * n,)), pltpu.SemaphoreType.DMA((6 * n,))]


def _gather_steps(bufs, send_sems, recv_sems):
    n = len(bufs)
    x, y, c = _position()
    q = 2 * x + y
    chips = _other_chips(x, y)
    sibling = (x, y, 1 - c)

    def half(a, slot, which):
        r2 = bufs[a].shape[1] // 2
        return bufs[a].at[slot, pl.ds(which * r2, r2), :]

    def copy(a, k, region, to):
        return pltpu.make_async_remote_copy(
            src_ref=region, dst_ref=region, send_sem=send_sems.at[6 * a + k], recv_sem=recv_sems.at[6 * a + k],
            device_id=to, device_id_type=MESH)

    def to_chip(a, k):
        tx, ty = chips[k]
        return copy(a, k, half(a, q, c), (tx, ty, c))

    def to_sibling(a, k):
        tx, ty = chips[k]
        return copy(a, 3 + k, half(a, 2 * tx + ty, c), sibling)

    def start():
        for a in range(n):
            for k in range(3):
                to_chip(a, k).start()

    def finish():
        for a in range(n):
            for k, (tx, ty) in enumerate(chips):
                copy(a, k, half(a, 2 * tx + ty, c), (tx, ty, c)).wait_recv()
                to_sibling(a, k).start()
        for a in range(n):
            for k, (tx, ty) in enumerate(chips):
                copy(a, 3 + k, half(a, 2 * tx + ty, 1 - c), sibling).wait_recv()
        for a in range(n):
            for k in range(3):
                to_chip(a, k).wait_send()
                to_sibling(a, k).wait_send()

    return start, finish


def _allgather_weights(slots):
    n = len(slots)

    def body(*refs):
        start, finish = _gather_steps(refs[n:2 * n], *refs[2 * n:])
        start()
        finish()

    return pl.pallas_call(
        body,
        in_specs=[ANY] * n,
        out_specs=[ANY] * n,
        out_shape=[jax.ShapeDtypeStruct(s.shape, s.dtype) for s in slots],
        input_output_aliases={a: a for a in range(n)},
        scratch_shapes=_gather_semaphores(n),
        name="allgather_weights",
    )(*slots)


def _exchange_pair_halves(grads):
    n = len(grads)

    def body(*refs):
        ins, outs = refs[0:n], refs[n:2 * n]
        send_sems, recv_sems = refs[2 * n:]
        x, y, c = _position()
        copies = []
        for a in range(n):
            r2 = grads[a].shape[1] // 2
            cp = pltpu.make_async_remote_copy(
                src_ref=ins[a].at[:, pl.ds((1 - c) * r2, r2), :], dst_ref=outs[a],
                send_sem=send_sems.at[a], recv_sem=recv_sems.at[a], device_id=(x, y, 1 - c), device_id_type=MESH)
            cp.start()
            copies.append(cp)
        for cp in copies:
            cp.wait()

    return pl.pallas_call(
        body,
        in_specs=[ANY] * n,
        out_specs=[ANY] * n,
        out_shape=[jax.ShapeDtypeStruct((N_CHIPS, g.shape[1] // 2, g.shape[2]), g.dtype) for g in grads],
        scratch_shapes=[pltpu.SemaphoreType.DMA((n,)), pltpu.SemaphoreType.DMA((n,))],
        name="rs_pair_exchange",
    )(*grads)


def _scatter_semaphores(n):
    return [pltpu.SemaphoreType.DMA((3 * n,)), pltpu.SemaphoreType.DMA((3 * n,)), pltpu.SemaphoreType.DMA((n,))]


def _scatter_steps(ins, outs, send_sems, recv_sems, local_sems):
    n = len(ins)
    x, y, c = _position()
    q = 2 * x + y
    chips = _other_chips(x, y)

    def own(a):
        return pltpu.make_async_copy(ins[a].at[q], outs[a].at[q], local_sems.at[a])

    def to_chip(a, k):
        tx, ty = chips[k]
        return pltpu.make_async_remote_copy(
            src_ref=ins[a].at[2 * tx + ty], dst_ref=outs[a].at[q],
            send_sem=send_sems.at[3 * a + k], recv_sem=recv_sems.at[3 * a + k],
            device_id=(tx, ty, c), device_id_type=MESH)

    def start():
        for a in range(n):
            own(a).start()
            for k in range(3):
                to_chip(a, k).start()

    def finish():
        for a in range(n):
            own(a).wait()
            for k in range(3):
                to_chip(a, k).wait()

    return start, finish


def _scatter_to_owner_chips(pairs):
    n = len(pairs)

    def body(*refs):
        start, finish = _scatter_steps(refs[0:n], refs[n:2 * n], *refs[2 * n:])
        start()
        finish()

    return pl.pallas_call(
        body,
        in_specs=[ANY] * n,
        out_specs=[ANY] * n,
        out_shape=[jax.ShapeDtypeStruct(p.shape, p.dtype) for p in pairs],
        scratch_shapes=_scatter_semaphores(n),
        name="rs_scatter",
    )(*pairs)


def _join_halves(shards):
    n = len(shards)

    def body(*refs):
        bufs = refs[n:2 * n]
        send_sems, recv_sems = refs[2 * n:]
        x, y, c = _position()
        started = []
        for a in range(n):
            r2 = shards[a].shape[0] // 2
            mine = bufs[a].at[pl.ds(c * r2, r2), :]
            cp = pltpu.make_async_remote_copy(
                src_ref=mine, dst_ref=mine, send_sem=send_sems.at[a], recv_sem=recv_sems.at[a],
                device_id=(x, y, 1 - c), device_id_type=MESH)
            cp.start()
            started.append(cp)
        for cp in started:
            cp.wait()

    return pl.pallas_call(
        body,
        in_specs=[ANY] * n,
        out_specs=[ANY] * n,
        out_shape=[jax.ShapeDtypeStruct(t.shape, t.dtype) for t in shards],
        input_output_aliases={a: a for a in range(n)},
        scratch_shapes=[pltpu.SemaphoreType.DMA((n,)), pltpu.SemaphoreType.DMA((n,))],
        name="rs_join_halves",
    )(*shards)


def _add_pair(g, got, c):
    _, r2, cols = got.shape

    def body(c_ref, g_ref, got_ref, o_ref):
        o_ref[...] = (g_ref[...].astype(F32) + got_ref[...].astype(F32)).astype(BF16)

    spec = pl.BlockSpec((1, r2, cols), lambda s, c_ref: (s, 0, 0))
    return pl.pallas_call(
        body,
        grid_spec=pltpu.PrefetchScalarGridSpec(
            num_scalar_prefetch=1,
            grid=(N_CHIPS,),
            in_specs=[pl.BlockSpec((1, r2, cols), lambda s, c_ref: (s, c_ref[0], 0)), spec],
            out_specs=spec,
        ),
        out_shape=jax.ShapeDtypeStruct(got.shape, BF16),
        name="rs_add_pair",
    )(c, g, got)


def _add_chips(parts, c):
    _, r2, cols = parts.shape

    def body(c_ref, p0, p1, p2, p3, o_ref):
        o_ref[...] = ((p0[0].astype(F32) + p1[0].astype(F32)) + p2[0].astype(F32)) + p3[0].astype(F32)

    specs = [pl.BlockSpec((1, r2, cols), functools.partial(lambda i, c_ref, s: (s, 0, 0), s=s))
             for s in range(N_CHIPS)]
    return pl.pallas_call(
        body,
        grid_spec=pltpu.PrefetchScalarGridSpec(
            num_scalar_prefetch=1,
            grid=(1,),
            in_specs=specs,
            out_specs=pl.BlockSpec((r2, cols), lambda i, c_ref: (c_ref[0], 0)),
        ),
        out_shape=jax.ShapeDtypeStruct((2 * r2, cols), F32),
        name="rs_add_chips",
    )(c, parts, parts, parts, parts)


def _allreduce_small(part):
    shape = part.shape

    def body(in_ref, out_ref, gather_ref, send_sems, recv_sems):
        x, y, c = _position()
        me = 4 * x + 2 * y + c
        relations = [(a, b, d) for a in (0, 1) for b in (0, 1) for d in (0, 1)][1:]
        flip = lambda v, f: 1 - v if f else v
        copies = []
        for k, (a, b, d) in enumerate(relations):
            cp = pltpu.make_async_remote_copy(
                src_ref=in_ref, dst_ref=gather_ref.at[me], send_sem=send_sems.at[k], recv_sem=recv_sems.at[k],
                device_id=(flip(x, a), flip(y, b), flip(c, d)), device_id_type=MESH)
            cp.start()
            copies.append(cp)
        gather_ref[me] = in_ref[...]
        for cp in copies:
            cp.wait()
        total = gather_ref[0]
        for dev in range(1, 8):
            total = total + gather_ref[dev]
        out_ref[...] = total

    vmem = pl.BlockSpec(memory_space=pltpu.VMEM)
    return pl.pallas_call(
        body,
        in_specs=[vmem],
        out_specs=vmem,
        out_shape=jax.ShapeDtypeStruct(shape, F32),
        scratch_shapes=[pltpu.VMEM((8,) + shape, F32), pltpu.SemaphoreType.DMA((7,)), pltpu.SemaphoreType.DMA((7,))],
        name="allreduce_small",
    )(part)


def _adamw(w, g, m, v):
    rows, cols = w.shape
    tr = _row_block(rows, cols * 4, budget=MIB)
    c1 = 1.0 / (1.0 - ADAM_B1 ** ADAM_STEP)
    c2 = 1.0 / (1.0 - ADAM_B2 ** ADAM_STEP)

    def body(w_ref, g_ref, m_ref, v_ref, d_ref, nm_ref, nv_ref):
        g_ = g_ref[...]
        nm = ADAM_B1 * m_ref[...] + (1.0 - ADAM_B1) * g_
        nv = ADAM_B2 * v_ref[...] + (1.0 - ADAM_B2) * (g_ * g_)
        nm_ref[...] = nm
        nv_ref[...] = nv
        d_ref[...] = -ADAM_LR * ((nm * c1) / (jnp.sqrt(nv * c2) + ADAM_EPS) + ADAM_WD * w_ref[...])

    spec = pl.BlockSpec((tr, cols), lambda i: (i, 0))
    out = jax.ShapeDtypeStruct((rows, cols), F32)
    return pl.pallas_call(
        body,
        grid=(rows // tr,),
        in_specs=[spec] * 4,
        out_specs=[spec] * 3,
        out_shape=[out] * 3,
        name="adamw",
    )(w, g, m, v)


BIG = ["ffn1_w_gate", "ffn1_w_up", "ffn1_w_down", "w_in", "w_branch_fox", "w_branch_sb", "w_out",
       "ffn2_w_gate", "ffn2_w_up", "ffn2_w_down", "w_ple_gate", "w_ple_proj"]
SMALL = ["ffn1_norm", "mix_norm", "ffn2_norm", "ple_norm", "forget_bias", "q_norm", "k_norm"]
COLUMN_SHARDED = ["w_branch_fox", "w_branch_sb", "w_ple_proj"]
KEPT_AS_SHARDS = ["ffn1_w_gate", "ffn1_w_up", "ffn1_w_down", "ffn2_w_gate", "ffn2_w_up", "ffn2_w_down"]
WORKED_TRANSPOSED = ["ffn1_w_gate", "ffn1_w_up", "ffn2_w_gate", "ffn2_w_up", "w_in"]
W_IN_SHARD = IN_WIDTH // N_CHIPS
W_IN_PAD = -W_IN_SHARD % 32
NEEDED_FIRST = ["ffn1_w_gate", "ffn1_w_up", "ffn1_w_down"]
ORDER = ["ffn1_norm", "ffn1_w_gate", "ffn1_w_up", "ffn1_w_down", "mix_norm", "w_in", "forget_bias", "q_norm",
         "k_norm", "w_branch_fox", "w_branch_sb", "w_out", "ffn2_norm", "ffn2_w_gate", "ffn2_w_up",
         "ffn2_w_down", "ple_norm", "w_ple_gate", "w_ple_proj"]
SMALL_ROWS = {"ffn1_norm": 0, "mix_norm": 1, "ffn2_norm": 2, "ple_norm": 3}
SMALL_COLS = {"forget_bias": (0, N_HEADS), "q_norm": (N_HEADS, HEAD_DIM), "k_norm": (N_HEADS + HEAD_DIM, HEAD_DIM)}
LOSS_ROW = 5


def _stored(name, a):
    t = jnp.swapaxes(a[0], 0, 1) if name in WORKED_TRANSPOSED else a[0]
    return jnp.pad(t, ((0, W_IN_PAD), (0, 0))) if name == "w_in" else t


def _returned(name, t):
    t = t[0:W_IN_SHARD] if name == "w_in" else t
    return (jnp.swapaxes(t, 0, 1) if name in WORKED_TRANSPOSED else t)[None]


def _whole(name, gathered):
    if name == "w_in":
        return gathered[:, 0:W_IN_SHARD].reshape(IN_WIDTH, D_MODEL)
    if name in COLUMN_SHARDED:
        return jnp.concatenate([gathered[s] for s in range(N_CHIPS)], axis=1)
    return gathered.reshape(-1, gathered.shape[-1])


def _as_shards(name, whole):
    if name == "w_in":
        return jnp.pad(whole.reshape(N_CHIPS, W_IN_SHARD, D_MODEL), ((0, 0), (0, W_IN_PAD), (0, 0)))
    if name in COLUMN_SHARDED:
        k, n = whole.shape
        return whole.reshape(k, N_CHIPS, n // N_CHIPS).transpose(1, 0, 2)
    return whole.reshape(N_CHIPS, whole.shape[0] // N_CHIPS, whole.shape[1])


def _pack_small(values, extra=None):
    rows = [values[k] for k in ("ffn1_norm", "mix_norm", "ffn2_norm", "ple_norm")]
    tail = jnp.concatenate([values["forget_bias"], values["q_norm"], values["k_norm"]], axis=1)
    rows.append(jnp.pad(tail, ((0, 0), (0, D_MODEL - tail.shape[1]))))
    packed = jnp.concatenate(rows + [jnp.zeros((3, D_MODEL), F32)], axis=0)
    if extra is not None:
        packed = packed.at[LOSS_ROW, 0].set(extra)
    return packed


def _unpack_small(packed):
    out = {k: packed[r:r + 1] for k, r in SMALL_ROWS.items()}
    for k, (start, size) in SMALL_COLS.items():
        out[k] = packed[4:5, start:start + size]
    return out


def kernel(x, p, ffn1_norm, ffn1_w_gate, ffn1_w_up, ffn1_w_down, mix_norm, w_in, forget_bias, q_norm, k_norm, w_branch_fox, w_branch_sb, w_out, ffn2_norm, ffn2_w_gate, ffn2_w_up, ffn2_w_down, ple_norm, w_ple_gate, w_ple_proj, loss_target, m_ffn1_norm, m_ffn1_w_gate, m_ffn1_w_up, m_ffn1_w_down, m_mix_norm, m_w_in, m_forget_bias, m_q_norm, m_k_norm, m_w_branch_fox, m_w_branch_sb, m_w_out, m_ffn2_norm, m_ffn2_w_gate, m_ffn2_w_up, m_ffn2_w_down, m_ple_norm, m_w_ple_gate, m_w_ple_proj, v_ffn1_norm, v_ffn1_w_gate, v_ffn1_w_up, v_ffn1_w_down, v_mix_norm, v_w_in, v_forget_bias, v_q_norm, v_k_norm, v_w_branch_fox, v_w_branch_sb, v_w_out, v_ffn2_norm, v_ffn2_w_gate, v_ffn2_w_up, v_ffn2_w_down, v_ple_norm, v_w_ple_gate, v_w_ple_proj):
    args = dict(locals())
    weights = {k: args[k] for k in ORDER}
    moments_m = {k: args["m_" + k] for k in ORDER}
    moments_v = {k: args["v_" + k] for k in ORDER}

    c_idx = lax.axis_index("c").astype(jnp.int32).reshape(1)
    q_idx = (2 * lax.axis_index("x") + lax.axis_index("y")).astype(jnp.int32).reshape(1)
    own = {k: _place_own_shard(_stored(k, weights[k]), q_idx) for k in BIG}
    full = dict(zip(NEEDED_FIRST, _allgather_weights([own[k] for k in NEEDED_FIRST])))
    pending = {k: own[k] for k in BIG if k not in NEEDED_FIRST}
    small = {k: weights[k] for k in SMALL}

    def pair_sums(names, gw):
        slots = [gw[k] if k in KEPT_AS_SHARDS else _as_shards(k, gw[k]) for k in names]
        from_core = _exchange_pair_halves(slots)
        return [_add_pair(g, got, c_idx) for g, got in zip(slots, from_core)]

    late = [k for k in BIG if k not in NEEDED_FIRST]
    loss_sum, grad_x, gw, gs, parts = _local_grads(
        x[0], p[0, 0], loss_target[0], small, full, pending, lambda early: (late, pair_sums(late, early)))

    parts.update(zip(NEEDED_FIRST, _scatter_to_owner_chips(pair_sums(NEEDED_FIRST, gw))))
    grads_big = dict(zip(BIG, _join_halves([_add_chips(parts[k], c_idx) for k in BIG])))
    reduced = _allreduce_small(_pack_small(gs, extra=loss_sum[0, 0]))
    grads_small = _unpack_small(reduced)
    loss = reduced[LOSS_ROW, 0]

    grads, deltas, new_m, new_v = {}, {}, {}, {}
    for k in BIG:
        d, nm, nv = _adamw(_stored(k, weights[k]), grads_big[k], _stored(k, moments_m[k]), _stored(k, moments_v[k]))
        grads[k], deltas[k], new_m[k], new_v[k] = (_returned(k, t) for t in (grads_big[k], d, nm, nv))
    d_s, nm_s, nv_s = _adamw(_pack_small({k: weights[k] for k in SMALL}), reduced,
                             _pack_small({k: moments_m[k] for k in SMALL}),
                             _pack_small({k: moments_v[k] for k in SMALL}))
    for k in SMALL:
        grads[k] = grads_small[k]
    for name, packed in (("d", d_s), ("m", nm_s), ("v", nv_s)):
        target = {"d": deltas, "m": new_m, "v": new_v}[name]
        target.update(_unpack_small(packed))

    return (loss, grad_x[None], *[grads[k] for k in ORDER], *[deltas[k] for k in ORDER],
            *[new_m[k] for k in ORDER], *[new_v[k] for k in ORDER])
```

```python
import functools

import jax
import jax.numpy as jnp
from jax import lax
from jax.experimental import pallas as pl
from jax.experimental.pallas import tpu as pltpu

F32 = jnp.float32
BF16 = jnp.bfloat16

D_MODEL = 1024
D_FF = 2816
N_CHIPS = 4
FF_SHARD = D_FF // N_CHIPS
FFN_CHUNKS = 2
WGRAD_TOKENS = 2048
HEAD_DIM = 64
N_HEADS = 8
ATT_W = N_HEADS * HEAD_DIM
PAIR_W = 2 * HEAD_DIM
N_PAIRS = N_HEADS // 2
PLE_DIM = 256
IN_WIDTH = 3 * ATT_W + N_HEADS + 3 * ATT_W + 2 * D_MODEL
EPS = 1e-6
QK_SCALE = HEAD_DIM ** -0.5
LANES = 128
ATT_BLOCK = 256
ATT_Q_BLOCK = 512
NEG_BIG = -1e30
EXP_UNDERFLOW = 110.0

ADAM_LR = 0.001
ADAM_B1 = 0.9
ADAM_B2 = 0.999
ADAM_EPS = 1e-08
ADAM_WD = 0.01
ADAM_STEP = 10

MESH = pl.DeviceIdType.MESH
MIB = 1024 * 1024


def _cparams(vmem_mib=48):
    return pltpu.CompilerParams(vmem_limit_bytes=vmem_mib * MIB)


def _dot(a, b):
    return jnp.dot(a, b, preferred_element_type=F32)


def _dot_tn(a, b):
    return lax.dot_general(a, b, (((0,), (0,)), ((), ())), preferred_element_type=F32)


def _dot_nt(a, b):
    return lax.dot_general(a, b, (((1,), (1,)), ((), ())), preferred_element_type=F32)


def _sigmoid(x):
    return 1.0 / (1.0 + jnp.exp(-x))


def _split2(x):
    hi = x.astype(BF16)
    lo = (x - hi.astype(F32)).astype(BF16)
    return hi, lo


def _dot_split2(x, m):
    hi, lo = _split2(x)
    return _dot(hi, m) + _dot(lo, m)


def _split3(x):
    hi = x.astype(BF16)
    rest = x - hi.astype(F32)
    mid = rest.astype(BF16)
    lo = (rest - mid.astype(F32)).astype(BF16)
    return hi, mid, lo


def _rms(x):
    r = lax.rsqrt(jnp.mean(x * x, axis=-1, keepdims=True) + EPS)
    return x * r, r


def _rms_bwd(dh, xn, r, g):
    dxn = dh * g
    return r * (dxn - xn * jnp.mean(dxn * xn, axis=-1, keepdims=True))


def _colsum(x):
    return jnp.sum(x, axis=0, keepdims=True)


def _row_block(rows, row_bytes, budget):
    best = None
    for t in range(8, rows + 1, 8):
        if rows % t == 0 and t * row_bytes <= budget:
            best = t
    return best if best is not None else rows


def _ffn_fwd(x, g, wg, wu, wd, gather=(), tm=512):
    s_len = x.shape[0]
    n = len(gather)
    steps = s_len // tm

    def body(x_ref, g_ref, wg_ref, wu_ref, wd_ref, *rest):
        o_ref, a_ref, b_ref = rest[n:n + 3]
        h_s, acc_s = rest[2 * n + 3:2 * n + 5]
        i = pl.program_id(0)
        j = pl.program_id(1)
        if n:
            start, finish = _gather_steps(rest[n + 3:2 * n + 3], *rest[2 * n + 5:])
            pl.when((i == 0) & (j == 0))(start)

        @pl.when(j == 0)
        def _():
            xn, _ = _rms(x_ref[...])
            h_s[...] = (xn * g_ref[...]).astype(BF16)
            acc_s[...] = jnp.zeros_like(acc_s)

        chunks = [pl.ds(r * (tm // FFN_CHUNKS), tm // FFN_CHUNKS) for r in range(FFN_CHUNKS)]
        pre = [(_dot_nt(h_s[rows, :], wg_ref[0]), _dot_nt(h_s[rows, :], wu_ref[0])) for rows in chunks]
        us = []
        for rows, (a, b) in zip(chunks, pre):
            a_ref[0, rows, :] = a.astype(BF16)
            b_ref[0, rows, :] = b.astype(BF16)
            us.append((a * _sigmoid(a) * b).astype(BF16))
        for rows, u in zip(chunks, us):
            acc_s[rows, :] += _dot(u, wd_ref[0])

        @pl.when(j == N_CHIPS - 1)
        def _():
            o_ref[...] = x_ref[...] + 0.5 * acc_s[...]

        if n:
            pl.when((i == steps - 1) & (j == N_CHIPS - 1))(finish)

    return pl.pallas_call(
        body,
        grid=(steps, N_CHIPS),
        in_specs=[
            pl.BlockSpec((tm, D_MODEL), lambda i, j: (i, 0)),
            pl.BlockSpec((1, D_MODEL), lambda i, j: (0, 0)),
            pl.BlockSpec((1, FF_SHARD, D_MODEL), lambda i, j: (j, 0, 0)),
            pl.BlockSpec((1, FF_SHARD, D_MODEL), lambda i, j: (j, 0, 0)),
            pl.BlockSpec((1, FF_SHARD, D_MODEL), lambda i, j: (j, 0, 0)),
        ] + [ANY] * n,
        out_specs=[pl.BlockSpec((tm, D_MODEL), lambda i, j: (i, 0)),
                   pl.BlockSpec((1, tm, FF_SHARD), lambda i, j: (j, i, 0)),
                   pl.BlockSpec((1, tm, FF_SHARD), lambda i, j: (j, i, 0))] + [ANY] * n,
        out_shape=[jax.ShapeDtypeStruct((s_len, D_MODEL), F32),
                   jax.ShapeDtypeStruct((N_CHIPS, s_len, FF_SHARD), BF16),
                   jax.ShapeDtypeStruct((N_CHIPS, s_len, FF_SHARD), BF16)]
        + [jax.ShapeDtypeStruct(s.shape, s.dtype) for s in gather],
        input_output_aliases={5 + a: 3 + a for a in range(n)},
        scratch_shapes=[pltpu.VMEM((tm, D_MODEL), BF16), pltpu.VMEM((tm, D_MODEL), F32)]
        + (_gather_semaphores(n) if n else []),
        compiler_params=_cparams(48),
        name="ffn_fwd_gathering" if n else "ffn_fwd",
    )(x, g, wg, wu, wd, *gather)


def _ffn_bwd(x, d, g, a_pre, b_pre, wg, wu, wd, scatter=(), tm=512):
    s_len = x.shape[0]
    nb = s_len // tm
    n = len(scatter)

    def body(x_ref, d_ref, g_ref, a_ref, b_ref, wg_ref, wu_ref, wd_ref, *rest):
        dx_ref, u_ref, da_ref, db_ref, h_ref, dbf_ref, dg_ref = rest[n:n + 7]
        dbf_s, dh_s = rest[2 * n + 7:2 * n + 9]
        i = pl.program_id(0)
        j = pl.program_id(1)
        if n:
            start, finish = _scatter_steps(rest[0:n], rest[n + 7:2 * n + 7], *rest[2 * n + 9:])
            pl.when((i == 0) & (j == 0))(start)

        @pl.when(j == 0)
        def _():
            xn, _ = _rms(x_ref[...])
            h_ref[...] = (xn * g_ref[...]).astype(BF16)
            dbf = d_ref[...].astype(BF16)
            dbf_s[...] = dbf
            dbf_ref[...] = dbf
            dh_s[...] = jnp.zeros_like(dh_s)

        @pl.when((i == 0) & (j == 0))
        def _():
            dg_ref[...] = jnp.zeros_like(dg_ref)

        chunks = [pl.ds(r * (tm // FFN_CHUNKS), tm // FFN_CHUNKS) for r in range(FFN_CHUNKS)]
        dus = [0.5 * _dot_nt(dbf_s[rows, :], wd_ref[0]) for rows in chunks]
        das, dbs = [], []
        for rows, du in zip(chunks, dus):
            a = a_ref[0, rows, :].astype(F32)
            b = b_ref[0, rows, :].astype(F32)
            s = _sigmoid(a)
            silu = a * s
            da = (du * b * (s * (1.0 + a * (1.0 - s)))).astype(BF16)
            db = (du * silu).astype(BF16)
            u_ref[0, rows, :] = (silu * b).astype(BF16)
            da_ref[0, rows, :] = da
            db_ref[0, rows, :] = db
            das.append(da)
            dbs.append(db)
        for rows, da, db in zip(chunks, das, dbs):
            dh_s[rows, :] += _dot(da, wg_ref[0]) + _dot(db, wu_ref[0])

        @pl.when(j == N_CHIPS - 1)
        def _():
            xn, r = _rms(x_ref[...])
            dh = dh_s[...]
            dx_ref[...] = d_ref[...] + _rms_bwd(dh, xn, r, g_ref[...])
            dg_ref[0:1, :] += _colsum(dh * xn)

        if n:
            pl.when((i == nb - 1) & (j == N_CHIPS - 1))(finish)

    row = lambda i, j: (i, 0)
    shard = lambda i, j: (j, 0, 0)
    act = lambda i, j: (j, i, 0)
    return pl.pallas_call(
        body,
        grid=(nb, N_CHIPS),
        in_specs=[
            pl.BlockSpec((tm, D_MODEL), row),
            pl.BlockSpec((tm, D_MODEL), row),
            pl.BlockSpec((1, D_MODEL), lambda i, j: (0, 0)),
            pl.BlockSpec((1, tm, FF_SHARD), act),
            pl.BlockSpec((1, tm, FF_SHARD), act),
            pl.BlockSpec((1, FF_SHARD, D_MODEL), shard),
            pl.BlockSpec((1, FF_SHARD, D_MODEL), shard),
            pl.BlockSpec((1, FF_SHARD, D_MODEL), shard),
        ] + [ANY] * n,
        out_specs=[
            pl.BlockSpec((tm, D_MODEL), row),
            pl.BlockSpec((1, tm, FF_SHARD), act),
            pl.BlockSpec((1, tm, FF_SHARD), act),
            pl.BlockSpec((1, tm, FF_SHARD), act),
            pl.BlockSpec((tm, D_MODEL), row),
            pl.BlockSpec((tm, D_MODEL), row),
            pl.BlockSpec((8, D_MODEL), lambda i, j: (0, 0)),
        ] + [ANY] * n,
        out_shape=[
            jax.ShapeDtypeStruct((s_len, D_MODEL), F32),
            jax.ShapeDtypeStruct((N_CHIPS, s_len, FF_SHARD), BF16),
            jax.ShapeDtypeStruct((N_CHIPS, s_len, FF_SHARD), BF16),
            jax.ShapeDtypeStruct((N_CHIPS, s_len, FF_SHARD), BF16),
            jax.ShapeDtypeStruct((s_len, D_MODEL), BF16),
            jax.ShapeDtypeStruct((s_len, D_MODEL), BF16),
            jax.ShapeDtypeStruct((8, D_MODEL), F32),
        ] + [jax.ShapeDtypeStruct(s.shape, s.dtype) for s in scatter],
        scratch_shapes=[
            pltpu.VMEM((tm, D_MODEL), BF16),
            pltpu.VMEM((tm, D_MODEL), F32),
        ] + (_scatter_semaphores(n) if n else []),
        compiler_params=_cparams(56),
        name="ffn_bwd_scattering" if n else "ffn_bwd",
    )(x, d, g, a_pre, b_pre, wg, wu, wd, *scatter)


def _wgrad(a, b, scale=1.0, name="wgrad"):
    na, s_len, k_dim = a.shape
    nb, _, n_dim = b.shape
    n = max(na, nb)
    ts = min(s_len, WGRAD_TOKENS)
    steps = s_len // ts

    def body(a_ref, b_ref, o_ref, acc_s):
        s = pl.program_id(1)

        @pl.when(s == 0)
        def _():
            acc_s[...] = jnp.zeros_like(acc_s)

        acc_s[...] += _dot_tn(a_ref[0].astype(BF16), b_ref[0].astype(BF16))

        @pl.when(s == steps - 1)
        def _():
            o_ref[0] = (acc_s[...] * scale).astype(BF16)

    a_map = (lambda m, s: (m, s, 0)) if na > 1 else (lambda m, s: (0, s, 0))
    b_map = (lambda m, s: (m, s, 0)) if nb > 1 else (lambda m, s: (0, s, 0))
    return pl.pallas_call(
        body,
        grid=(n, steps),
        in_specs=[pl.BlockSpec((1, ts, k_dim), a_map), pl.BlockSpec((1, ts, n_dim), b_map)],
        out_specs=pl.BlockSpec((1, k_dim, n_dim), lambda m, s: (m, 0, 0)),
        out_shape=jax.ShapeDtypeStruct((n, k_dim, n_dim), BF16),
        scratch_shapes=[pltpu.VMEM((k_dim, n_dim), F32)],
        compiler_params=_cparams(56),
        name=name,
    )(a, b)


def _head_sum_matrices():
    lane = lax.broadcasted_iota(jnp.int32, (ATT_W, LANES), 0) // HEAD_DIM
    col = lax.broadcasted_iota(jnp.int32, (ATT_W, LANES), 1)
    bd = (lane == col).astype(BF16)
    return bd, bd.T


def _head_mean(t, bd, bd_t):
    per_head = _dot_split2(t, bd) * (1.0 / HEAD_DIM)
    return _dot_split2(per_head, bd_t)


def _head_rms(x, bd, bd_t):
    per_head = _dot_split2(x * x, bd) * (1.0 / HEAD_DIM)
    r = lax.rsqrt(per_head + EPS)
    rw = _dot_split2(r, bd_t)
    return x * rw, rw


def _log_sigmoid(z):
    return jnp.minimum(z, 0.0) - jnp.log(1.0 + jnp.exp(-jnp.abs(z)))


def _inproj_fwd(x1, g, w_fox, w_fl, w_sb, w_gates, bias, qn, kn, bd, bd_t, tm=256):
    s_len = x1.shape[0]

    def body(x_ref, g_ref, wf_ref, wl_ref, ws_ref, wg_ref, bias_ref, qn_ref, kn_ref, bd_ref, bdt_ref,
             fq_ref, fk_ref, qs_ref, kf_ref, vf_ref, logf_ref, sq_ref, sk_ref, sv_ref, gates_ref):
        xn, _ = _rms(x_ref[...])
        h = (xn * g_ref[...]).astype(BF16)
        zf = _dot(h, wf_ref[...])
        fq = zf[:, 0:ATT_W]
        fk = zf[:, ATT_W:2 * ATT_W]
        fq_ref[...] = fq
        fk_ref[...] = fk
        bd_m = bd_ref[...]
        bdt_m = bdt_ref[...]
        fqn, _ = _head_rms(fq, bd_m, bdt_m)
        fkn, _ = _head_rms(fk, bd_m, bdt_m)
        qs_ref[...] = (fqn * qn_ref[...]).astype(BF16) * QK_SCALE
        kf_ref[...] = (fkn * kn_ref[...]).astype(BF16)
        vf_ref[...] = zf[:, 2 * ATT_W:3 * ATT_W].astype(BF16)
        logf_ref[...] = _log_sigmoid(_dot(h, wl_ref[...]) + bias_ref[...])
        zs = _dot(h, ws_ref[...])
        sq_ref[...] = zs[:, 0:ATT_W].astype(BF16) * QK_SCALE
        sk_ref[...] = zs[:, ATT_W:2 * ATT_W].astype(BF16)
        sv_ref[...] = zs[:, 2 * ATT_W:3 * ATT_W].astype(BF16)
        gates_ref[...] = _dot(h, wg_ref[...])

    row = lambda i: (i, 0)
    full = lambda i: (0, 0)
    att = lambda dt: jax.ShapeDtypeStruct((s_len, ATT_W), dt)
    return pl.pallas_call(
        body,
        grid=(s_len // tm,),
        in_specs=[
            pl.BlockSpec((tm, D_MODEL), row),
            pl.BlockSpec((1, D_MODEL), full),
            pl.BlockSpec((D_MODEL, 3 * ATT_W), full),
            pl.BlockSpec((D_MODEL, LANES), full),
            pl.BlockSpec((D_MODEL, 3 * ATT_W), full),
            pl.BlockSpec((D_MODEL, 2 * D_MODEL), full),
            pl.BlockSpec((1, LANES), full),
            pl.BlockSpec((1, ATT_W), full),
            pl.BlockSpec((1, ATT_W), full),
            pl.BlockSpec((ATT_W, LANES), full),
            pl.BlockSpec((LANES, ATT_W), full),
        ],
        out_specs=[
            pl.BlockSpec((tm, ATT_W), row), pl.BlockSpec((tm, ATT_W), row),
            pl.BlockSpec((tm, ATT_W), row), pl.BlockSpec((tm, ATT_W), row), pl.BlockSpec((tm, ATT_W), row),
            pl.BlockSpec((tm, LANES), row),
            pl.BlockSpec((tm, ATT_W), row), pl.BlockSpec((tm, ATT_W), row), pl.BlockSpec((tm, ATT_W), row),
            pl.BlockSpec((tm, 2 * D_MODEL), row),
        ],
        out_shape=[
            att(F32), att(F32), att(BF16), att(BF16), att(BF16),
            jax.ShapeDtypeStruct((s_len, LANES), F32),
            att(BF16), att(BF16), att(BF16),
            jax.ShapeDtypeStruct((s_len, 2 * D_MODEL), F32),
        ],
        compiler_params=_cparams(56),
        name="inproj_fwd",
    )(x1, g, w_fox, w_fl, w_sb, w_gates, bias, qn, kn, bd, bd_t)


def _tri(n, kind):
    r = lax.broadcasted_iota(jnp.int32, (n, n), 0)
    c = lax.broadcasted_iota(jnp.int32, (n, n), 1)
    m = {"row_ge_col": r >= c, "row_le_col": r <= c, "row_gt_col": r > c, "row_lt_col": r < c}[kind]
    return m.astype(BF16)


def _cumsum_rows(x, reverse, tm=256):
    s_len = x.shape[0]
    nb = s_len // tm
    tri = _tri(tm, "row_le_col" if reverse else "row_ge_col")
    edge = 0 if reverse else tm - 1

    def body(x_ref, tri_ref, o_ref, carry_s):
        @pl.when(pl.program_id(0) == 0)
        def _():
            carry_s[...] = jnp.zeros_like(carry_s)

        hi, mid, lo = _split3(x_ref[...])
        t = tri_ref[...]
        y = _dot(t, hi) + _dot(t, mid) + _dot(t, lo) + carry_s[...]
        o_ref[...] = y
        carry_s[...] = y[edge:edge + 1, :]

    order = (lambda i: (nb - 1 - i, 0)) if reverse else (lambda i: (i, 0))
    return pl.pallas_call(
        body,
        grid=(nb,),
        in_specs=[pl.BlockSpec((tm, LANES), order), pl.BlockSpec((tm, tm), lambda i: (0, 0))],
        out_specs=pl.BlockSpec((tm, LANES), order),
        out_shape=jax.ShapeDtypeStruct((s_len, LANES), F32),
        scratch_shapes=[pltpu.VMEM((1, LANES), F32)],
        name="cumsum_rev" if reverse else "cumsum_fwd",
    )(x, tri)


def _unblocked_t(t4):
    _, nb, _, blk = t4.shape
    return t4.transpose(1, 3, 0, 2).reshape(nb * blk, ATT_W)


def _blocked_rows(t, blk):
    return t.reshape(t.shape[0] // blk, blk, t.shape[1])


def _pair_rows_t(f8, blk):
    nb = f8.shape[0] // blk
    t = f8.reshape(nb, blk, N_PAIRS, 2).transpose(2, 0, 3, 1)
    return jnp.pad(t, ((0, 0), (0, 0), (0, 6), (0, 0)))


def _unpair_rows_t(t4):
    _, nb, _, blk = t4.shape
    return t4[:, :, 0:2, :].transpose(1, 3, 0, 2).reshape(nb * blk, N_HEADS)


def _head_masks(tq):
    lane = lax.broadcasted_iota(jnp.int32, (tq, PAIR_W), 1)
    return lane < HEAD_DIM


def _causal_mask(tq, tk, offset, strict):
    d = lax.broadcasted_iota(jnp.int32, (tq, tk), 1) - lax.broadcasted_iota(jnp.int32, (tq, tk), 0)
    return (d < offset) if strict else (d <= offset)


def _heads_of(ref, first):
    t = ref[...]
    zero = jnp.zeros_like(t)
    return [jnp.where(first, t, zero), jnp.where(first, zero, t)]


def _head_cols(ref):
    t = ref[...]
    return [t[:, 0:1], t[:, HEAD_DIM:HEAD_DIM + 1]]


def _att_specs(s_len):
    tq, tk = ATT_Q_BLOCK, ATT_BLOCK
    nq, nk = s_len // tq, s_len // tk
    return dict(
        nq=nq,
        q=pl.BlockSpec((tq, PAIR_W), lambda p, i: (i, p)),
        k_t=pl.BlockSpec((1, nk, PAIR_W, tk), lambda p, i: (p, 0, 0, 0)),
        k_rows=pl.BlockSpec((nk, tk, PAIR_W), lambda p, i: (0, 0, p)),
        f_t=pl.BlockSpec((1, nk, 8, tk), lambda p, i: (p, 0, 0, 0)),
        first=pl.BlockSpec((1, 1, 8, LANES), lambda p, i: (p, i, 0, 0)),
        wide=jax.ShapeDtypeStruct((s_len, ATT_W), F32),
        k_t_out=jax.ShapeDtypeStruct((N_PAIRS, nk, PAIR_W, tk), F32),
        f_t_out=jax.ShapeDtypeStruct((N_PAIRS, nk, 8, tk), F32),
        first_out=jax.ShapeDtypeStruct((N_PAIRS, nq, 8, LANES), F32),
        acc=pltpu.VMEM((2, tq, PAIR_W), F32),
    )


def _first_block(first_ref, limit):
    return jnp.clip(jnp.max(first_ref[0, 0]).astype(jnp.int32), 0, limit)


def _key_norm_bound(k):
    sq = jnp.sum(jnp.square(k.astype(F32)).reshape(k.shape[0], N_HEADS, HEAD_DIM), axis=-1)
    bound = jnp.sqrt(jnp.max(sq, axis=0)).reshape(N_PAIRS, 2)
    return jnp.broadcast_to(jnp.pad(bound, ((0, 0), (0, 6)))[:, :, None], (N_PAIRS, 8, LANES))


def _fox_fwd(qs, k3, v3, fw, ft4, kmax):
    sp = _att_specs(qs.shape[0])
    tq, tk = ATT_Q_BLOCK, ATT_BLOCK
    ratio = tq // tk

    def body(q_ref, k_ref, v_ref, fw_ref, ft_ref, kmax_ref, y_ref, lse_ref, first_ref, acc_ref, max_ref, sum_ref):
        i = pl.program_id(1)
        first = _head_masks(tq)
        qh = _heads_of(q_ref, first)
        fqh = _head_cols(fw_ref)
        acc_ref[...] = jnp.zeros_like(acc_ref)
        sum_ref[...] = jnp.zeros_like(sum_ref)
        max_ref[...] = jnp.full(max_ref.shape, NEG_BIG, F32)
        reach = []
        for n in range(2):
            qf = qh[n].astype(F32)
            reach.append(jnp.sqrt(jnp.sum(qf * qf, axis=-1, keepdims=True)) * kmax_ref[0, n:n + 1, 0:1] + fqh[n])

        def logits(j, shift, diag):
            k, fk = k_ref[j], ft_ref[0, j]
            raw = [_dot_nt(qh[n], k) for n in range(2)]
            out = []
            for n in range(2):
                s = raw[n] + (shift[n] - fk[n:n + 1, :])
                if diag:
                    s = jnp.where(_causal_mask(tq, tk, i * tq - j * tk, strict=False), s, NEG_BIG)
                out.append(s)
            return out

        def max_pass(j, diag):
            ss = logits(j, fqh, diag)
            for n in range(2):
                max_ref[n] = jnp.maximum(max_ref[n], ss[n])

        def sum_pass(j, shift, diag):
            ps = [jnp.exp(s) for s in logits(j, shift, diag)]
            v = v_ref[j]
            for n in range(2):
                sum_ref[n] += ps[n]
            for n in range(2):
                acc_ref[n] += _dot(ps[n].astype(BF16), v)

        for d in range(ratio):
            max_pass(ratio * i + d, True)

        def block_matters(j):
            gap = []
            for n in range(2):
                m_run = jnp.max(max_ref[n], axis=-1, keepdims=True)
                f_end = ft_ref[0, jnp.maximum(j, 0)][n:n + 1, tk - 1:tk]
                gap.append(jnp.max(reach[n] - m_run) - jnp.max(f_end))
            return (j >= 0) & (jnp.maximum(gap[0], gap[1]) > -EXP_UNDERFLOW)

        def walk_left(j):
            max_pass(j, False)
            return j - 1

        j_first = lax.while_loop(block_matters, walk_left, ratio * i - 1) + 1
        m = [jnp.max(max_ref[n], axis=-1, keepdims=True) for n in range(2)]
        shift = [fqh[n] - m[n] for n in range(2)]

        def one(j, c):
            sum_pass(j, shift, False)
            return c
        lax.fori_loop(j_first, ratio * i, one, 0)
        for d in range(ratio):
            sum_pass(ratio * i + d, shift, True)
        l = [jnp.sum(sum_ref[n], axis=-1, keepdims=True) for n in range(2)]
        y_ref[...] = jnp.where(first, acc_ref[0] / l[0], acc_ref[1] / l[1])
        lse_ref[...] = jnp.where(first, m[0] + jnp.log(l[0]), m[1] + jnp.log(l[1]))
        first_ref[...] = jnp.ones(first_ref.shape, F32) * j_first.astype(F32)

    tile = pltpu.VMEM((2, tq, tk), F32)
    return pl.pallas_call(
        body,
        grid=(N_PAIRS, sp["nq"]),
        in_specs=[sp["q"], sp["k_rows"], sp["k_rows"], sp["q"], sp["f_t"],
                  pl.BlockSpec((1, 8, LANES), lambda p, i: (p, 0, 0))],
        out_specs=[sp["q"], sp["q"], sp["first"]],
        out_shape=[sp["wide"], sp["wide"], sp["first_out"]],
        scratch_shapes=[sp["acc"], tile, tile],
        compiler_params=_cparams(56),
        name="fox_fwd",
    )(qs, k3, v3, fw, ft4, kmax)


def _fox_bwd(qs, k3, v3, dy, y, lse, fw, ft4, first_block):
    sp = _att_specs(qs.shape[0])
    tq, tk = ATT_Q_BLOCK, ATT_BLOCK
    ratio = tq // tk

    def body(q_ref, k_ref, v_ref, dy_ref, y_ref, lse_ref, fw_ref, ft_ref, first_ref,
             dq_ref, dfq_ref, dkt_ref, dvt_ref, dft_ref, acc_ref):
        i = pl.program_id(1)

        @pl.when(i == 0)
        def _():
            dkt_ref[...] = jnp.zeros_like(dkt_ref)
            dvt_ref[...] = jnp.zeros_like(dvt_ref)
            dft_ref[...] = jnp.zeros_like(dft_ref)

        first = _head_masks(tq)
        qh = _heads_of(q_ref, first)
        dyv = dy_ref[...]
        dyb = dyv.astype(BF16)
        zero = jnp.zeros_like(dyb)
        dyh = [jnp.where(first, dyb, zero), jnp.where(first, zero, dyb)]
        prod = dyv * y_ref[...]
        zf = jnp.zeros_like(prod)
        delta = [jnp.sum(jnp.where(first, prod, zf), axis=-1, keepdims=True),
                 jnp.sum(jnp.where(first, zf, prod), axis=-1, keepdims=True)]
        fqh = _head_cols(fw_ref)
        lseh = _head_cols(lse_ref)
        shift = [fqh[n] - lseh[n] for n in range(2)]
        acc_ref[...] = jnp.zeros_like(acc_ref)

        def block(j, rows, diag):
            mask = _causal_mask(tq, tk, i * tq - j * tk, strict=False) if diag else None
            k, v, fk = k_ref[j], v_ref[j], ft_ref[0, j]
            logits = [_dot_nt(qh[n], k) for n in range(2)]
            dps = [_dot_nt(dyh[n], v) for n in range(2)]
            pbs, dsbs, out = [], [], []
            for n in range(2):
                p = jnp.exp(logits[n] + (shift[n] - fk[n:n + 1, :]))
                if diag:
                    p = jnp.where(mask, p, 0.0)
                ds = p * (dps[n] - delta[n])
                pbs.append(p.astype(BF16))
                dsbs.append(ds.astype(BF16))
                out.append(rows[n] + jnp.sum(ds, axis=-1, keepdims=True))
                dft_ref[0, j, n:n + 1, :] -= _colsum(ds)
            for n in range(2):
                acc_ref[n] += _dot(dsbs[n], k)
            dkt_ref[0, j] += _dot_tn(qh[0], dsbs[0]) + _dot_tn(qh[1], dsbs[1])
            dvt_ref[0, j] += _dot_tn(dyh[0], pbs[0]) + _dot_tn(dyh[1], pbs[1])
            return tuple(out)

        rows = (jnp.zeros((tq, 1), F32),) * 2
        rows = lax.fori_loop(_first_block(first_ref, ratio * i), ratio * i, lambda j, c: block(j, c, False), rows)
        for d in range(ratio):
            rows = block(ratio * i + d, rows, True)
        dq_ref[...] = jnp.where(first, acc_ref[0], acc_ref[1])
        lane = lax.broadcasted_iota(jnp.int32, (tq, 8), 1)
        dfq_ref[0] = jnp.where(lane == 0, rows[0], jnp.where(lane == 1, rows[1], 0.0))

    return pl.pallas_call(
        body,
        grid=(N_PAIRS, sp["nq"]),
        in_specs=[sp["q"], sp["k_rows"], sp["k_rows"], sp["q"], sp["q"], sp["q"], sp["q"], sp["f_t"], sp["first"]],
        out_specs=[sp["q"], pl.BlockSpec((1, tq, 8), lambda p, i: (p, i, 0)), sp["k_t"], sp["k_t"], sp["f_t"]],
        out_shape=[sp["wide"], jax.ShapeDtypeStruct((N_PAIRS, qs.shape[0], 8), F32),
                   sp["k_t_out"], sp["k_t_out"], sp["f_t_out"]],
        scratch_shapes=[sp["acc"]],
        compiler_params=_cparams(56),
        name="fox_bwd",
    )(qs, k3, v3, dy, y, lse, fw, ft4, first_block)


SIGN_BIT = 0x80000000


def _sb_terms(z, mask, diag):
    neg_abs = pltpu.bitcast(pltpu.bitcast(z, jnp.uint32) | jnp.uint32(SIGN_BIT), F32)
    lb = jnp.minimum(z, 0.0) - jnp.log(1.0 + jnp.exp(neg_abs))
    l1m = lb - z
    if diag:
        l1m = jnp.where(mask, l1m, 0.0)
    return lb, l1m


def _dot_split2_stacked(x, m2):
    hi, lo = _split2(x)
    return _dot(jnp.concatenate([hi, lo], axis=1), m2)


def _tri_stacked(kind):
    t = _tri(ATT_BLOCK, kind)
    return jnp.concatenate([t, t], axis=0)


def _sb_fwd(qs, k3, v3):
    sp = _att_specs(qs.shape[0])
    tq, tk = ATT_Q_BLOCK, ATT_BLOCK
    ratio = tq // tk
    upper = _tri_stacked("row_gt_col")

    def body(q_ref, k_ref, v_ref, u_ref, y_ref, rtot_ref, first_ref, acc_ref):
        i = pl.program_id(1)
        first = _head_masks(tq)
        qh = _heads_of(q_ref, first)
        u = u_ref[...]
        acc_ref[...] = jnp.zeros_like(acc_ref)

        def block(j, rs, diag):
            mask = _causal_mask(tq, tk, i * tq - j * tk, strict=True) if diag else None
            k, v = k_ref[j], v_ref[j]
            logits = [_dot_nt(qh[n], k) for n in range(2)]
            terms = [_sb_terms(z, mask, diag) for z in logits]
            right = [_dot_split2_stacked(l1m, u) for _, l1m in terms]
            weights = []
            for n in range(2):
                a = jnp.exp(terms[n][0] + right[n] + rs[n])
                if diag:
                    a = jnp.where(mask, a, 0.0)
                weights.append(a.astype(BF16))
            for n in range(2):
                acc_ref[n] += _dot(weights[n], v)
            return tuple(rs[n] + jnp.sum(terms[n][1], axis=-1, keepdims=True) for n in range(2))

        rs = (jnp.zeros((tq, 1), F32),) * 2
        for d in range(ratio):
            rs = block(ratio * i + (ratio - 1 - d), rs, True)

        def block_matters(c):
            j, r0, r1 = c
            return (j >= 0) & (jnp.max(jnp.maximum(r0, r1)) > -EXP_UNDERFLOW)

        def walk_left(c):
            j, r0, r1 = c
            r0, r1 = block(j, (r0, r1), False)
            return j - 1, r0, r1

        j, r0, r1 = lax.while_loop(block_matters, walk_left, (ratio * i - 1, rs[0], rs[1]))
        y_ref[...] = jnp.where(first, acc_ref[0], acc_ref[1])
        rtot_ref[...] = jnp.where(first, r0, r1)
        first_ref[...] = jnp.ones(first_ref.shape, F32) * (j + 1).astype(F32)

    return pl.pallas_call(
        body,
        grid=(N_PAIRS, sp["nq"]),
        in_specs=[sp["q"], sp["k_rows"], sp["k_rows"], pl.BlockSpec((2 * tk, tk), lambda p, i: (0, 0))],
        out_specs=[sp["q"], sp["q"], sp["first"]],
        out_shape=[sp["wide"], sp["wide"], sp["first_out"]],
        scratch_shapes=[sp["acc"]],
        compiler_params=_cparams(56),
        name="sb_fwd",
    )(qs, k3, v3, upper)


def _sb_bwd(qs, k3, v3, dy, rtot, first_block):
    sp = _att_specs(qs.shape[0])
    tq, tk = ATT_Q_BLOCK, ATT_BLOCK
    ratio = tq // tk
    lower_in = _tri_stacked("row_le_col")
    lower = _tri(tk, "row_lt_col")

    def body(q_ref, k_ref, v_ref, dy_ref, rtot_ref, first_ref, li_ref, l_ref, dq_ref, dkt_ref, dvt_ref, acc_ref):
        i = pl.program_id(1)

        @pl.when(i == 0)
        def _():
            dkt_ref[...] = jnp.zeros_like(dkt_ref)
            dvt_ref[...] = jnp.zeros_like(dvt_ref)

        first = _head_masks(tq)
        qh = _heads_of(q_ref, first)
        dyb = dy_ref[...].astype(BF16)
        zero = jnp.zeros_like(dyb)
        dyh = [jnp.where(first, dyb, zero), jnp.where(first, zero, dyb)]
        rtoth = _head_cols(rtot_ref)
        li = li_ref[...]
        lo_tri = l_ref[...]
        acc_ref[...] = jnp.zeros_like(acc_ref)

        def block(j, carry, diag):
            mask = _causal_mask(tq, tk, i * tq - j * tk, strict=True) if diag else None
            k, v = k_ref[j], v_ref[j]
            logits = [_dot_nt(qh[n], k) for n in range(2)]
            das = [_dot_nt(dyh[n], v) for n in range(2)]
            terms = [_sb_terms(z, mask, diag) for z in logits]
            upto = [_dot_split2_stacked(l1m, li) for _, l1m in terms]
            des, weights = [], []
            for n in range(2):
                a = jnp.exp(terms[n][0] + ((rtoth[n] - carry[2 * n]) - upto[n]))
                if diag:
                    a = jnp.where(mask, a, 0.0)
                des.append(a * das[n])
                weights.append(a.astype(BF16))
            lefts = [_dot(de.astype(BF16), lo_tri) for de in des]
            dzbs, out = [], []
            for n in range(2):
                beta = jnp.exp(terms[n][0])
                dz = des[n] - (des[n] + (carry[2 * n + 1] + lefts[n])) * beta
                if diag:
                    dz = jnp.where(mask, dz, 0.0)
                dzbs.append(dz.astype(BF16))
                out += [carry[2 * n] + jnp.sum(terms[n][1], axis=-1, keepdims=True),
                        carry[2 * n + 1] + jnp.sum(des[n], axis=-1, keepdims=True)]
            for n in range(2):
                acc_ref[n] += _dot(dzbs[n], k)
            dkt_ref[0, j] += _dot_tn(qh[0], dzbs[0]) + _dot_tn(qh[1], dzbs[1])
            dvt_ref[0, j] += _dot_tn(dyh[0], weights[0]) + _dot_tn(dyh[1], weights[1])
            return tuple(out)

        carry = (jnp.zeros((tq, 1), F32),) * 4
        carry = lax.fori_loop(_first_block(first_ref, ratio * i), ratio * i, lambda j, c: block(j, c, False), carry)
        for d in range(ratio):
            carry = block(ratio * i + d, carry, True)
        dq_ref[...] = jnp.where(first, acc_ref[0], acc_ref[1])

    return pl.pallas_call(
        body,
        grid=(N_PAIRS, sp["nq"]),
        in_specs=[sp["q"], sp["k_rows"], sp["k_rows"], sp["q"], sp["q"], sp["first"],
                  pl.BlockSpec((2 * tk, tk), lambda p, i: (0, 0)), pl.BlockSpec((tk, tk), lambda p, i: (0, 0))],
        out_specs=[sp["q"], sp["k_t"], sp["k_t"]],
        out_shape=[sp["wide"], sp["k_t_out"], sp["k_t_out"]],
        scratch_shapes=[sp["acc"]],
        compiler_params=_cparams(56),
        name="sb_bwd",
    )(qs, k3, v3, dy, rtot, first_block, lower_in, lower)


def _merge_fwd(x1, gates, y_fox, y_sb, w_bf, w_bs, w_out, tm=512):
    s_len = x1.shape[0]

    def body(x_ref, g_ref, yf_ref, ys_ref, wbf_ref, wbs_ref, wo_ref, o_ref):
        g = g_ref[...]
        of = _dot(yf_ref[...].astype(BF16), wbf_ref[...])
        os_ = _dot(ys_ref[...].astype(BF16), wbs_ref[...])
        merged = _sigmoid(g[:, 0:D_MODEL]) * of + _sigmoid(g[:, D_MODEL:]) * os_
        o_ref[...] = x_ref[...] + _dot(merged.astype(BF16), wo_ref[...])

    row = lambda i: (i, 0)
    full = lambda i: (0, 0)
    return pl.pallas_call(
        body,
        grid=(s_len // tm,),
        in_specs=[
            pl.BlockSpec((tm, D_MODEL), row),
            pl.BlockSpec((tm, 2 * D_MODEL), row),
            pl.BlockSpec((tm, ATT_W), row),
            pl.BlockSpec((tm, ATT_W), row),
            pl.BlockSpec((ATT_W, D_MODEL), full),
            pl.BlockSpec((ATT_W, D_MODEL), full),
            pl.BlockSpec((D_MODEL, D_MODEL), full),
        ],
        out_specs=pl.BlockSpec((tm, D_MODEL), row),
        out_shape=jax.ShapeDtypeStruct((s_len, D_MODEL), F32),
        compiler_params=_cparams(48),
        name="merge_fwd",
    )(x1, gates, y_fox, y_sb, w_bf, w_bs, w_out)


def _merge_bwd(dx2, gates, y_fox, y_sb, w_bf, w_bs, w_out, tm=512):
    s_len = dx2.shape[0]

    def body(d_ref, g_ref, yf_ref, ys_ref, wbf_ref, wbs_ref, wo_ref,
             dyf_ref, dys_ref, dg_ref, dof_ref, dos_ref, m_ref, dbf_ref):
        dbf = d_ref[...].astype(BF16)
        dbf_ref[...] = dbf
        dm = _dot_nt(dbf, wo_ref[...])
        g = g_ref[...]
        of = _dot(yf_ref[...].astype(BF16), wbf_ref[...])
        os_ = _dot(ys_ref[...].astype(BF16), wbs_ref[...])
        sf = _sigmoid(g[:, 0:D_MODEL])
        ss = _sigmoid(g[:, D_MODEL:])
        m_ref[...] = (sf * of + ss * os_).astype(BF16)
        d_of = (dm * sf).astype(BF16)
        d_os = (dm * ss).astype(BF16)
        dof_ref[...] = d_of
        dos_ref[...] = d_os
        dg_ref[:, 0:D_MODEL] = (dm * of * sf * (1.0 - sf)).astype(BF16)
        dg_ref[:, D_MODEL:] = (dm * os_ * ss * (1.0 - ss)).astype(BF16)
        dyf_ref[...] = _dot_nt(d_of, wbf_ref[...])
        dys_ref[...] = _dot_nt(d_os, wbs_ref[...])

    row = lambda i: (i, 0)
    full = lambda i: (0, 0)
    return pl.pallas_call(
        body,
        grid=(s_len // tm,),
        in_specs=[
            pl.BlockSpec((tm, D_MODEL), row),
            pl.BlockSpec((tm, 2 * D_MODEL), row),
            pl.BlockSpec((tm, ATT_W), row),
            pl.BlockSpec((tm, ATT_W), row),
            pl.BlockSpec((ATT_W, D_MODEL), full),
            pl.BlockSpec((ATT_W, D_MODEL), full),
            pl.BlockSpec((D_MODEL, D_MODEL), full),
        ],
        out_specs=[
            pl.BlockSpec((tm, ATT_W), row), pl.BlockSpec((tm, ATT_W), row),
            pl.BlockSpec((tm, 2 * D_MODEL), row),
            pl.BlockSpec((tm, D_MODEL), row), pl.BlockSpec((tm, D_MODEL), row),
            pl.BlockSpec((tm, D_MODEL), row), pl.BlockSpec((tm, D_MODEL), row),
        ],
        out_shape=[
            jax.ShapeDtypeStruct((s_len, ATT_W), F32), jax.ShapeDtypeStruct((s_len, ATT_W), F32),
            jax.ShapeDtypeStruct((s_len, 2 * D_MODEL), BF16),
            jax.ShapeDtypeStruct((s_len, D_MODEL), BF16), jax.ShapeDtypeStruct((s_len, D_MODEL), BF16),
            jax.ShapeDtypeStruct((s_len, D_MODEL), BF16), jax.ShapeDtypeStruct((s_len, D_MODEL), BF16),
        ],
        compiler_params=_cparams(56),
        name="merge_bwd",
    )(dx2, gates, y_fox, y_sb, w_bf, w_bs, w_out)


def _ple_loss(x3, p, g, w_pg, w_pp, target, tm=512):
    s_len = x3.shape[0]
    inv_d = 1.0 / D_MODEL

    def body(x_ref, p_ref, g_ref, wpg_ref, wpp_ref, t_ref,
             dx_ref, du_ref, dt_ref, hn_ref, dg_ref, loss_ref):
        @pl.when(pl.program_id(0) == 0)
        def _():
            dg_ref[...] = jnp.zeros_like(dg_ref)
            loss_ref[...] = jnp.zeros_like(loss_ref)

        x = x_ref[...]
        xn, r = _rms(x)
        gain = g_ref[...]
        hn = (xn * gain).astype(BF16)
        hn_ref[...] = hn
        sg = _sigmoid(_dot(hn, wpg_ref[...]))
        t = _dot(p_ref[...].astype(BF16), wpp_ref[...])
        err = x + sg * t - t_ref[...]
        sq = jnp.sum(_colsum(err * err), axis=-1, keepdims=True)
        loss_ref[...] += (0.5 * inv_d) * sq
        dy = err * inv_d
        du = (dy * t * sg * (1.0 - sg)).astype(BF16)
        du_ref[...] = du
        dt_ref[...] = (dy * sg).astype(BF16)
        dh = _dot_nt(du, wpg_ref[...])
        dx_ref[...] = dy + _rms_bwd(dh, xn, r, gain)
        dg_ref[0:1, :] += _colsum(dh * xn)

    row = lambda i: (i, 0)
    full = lambda i: (0, 0)
    bf = jax.ShapeDtypeStruct((s_len, D_MODEL), BF16)
    return pl.pallas_call(
        body,
        grid=(s_len // tm,),
        in_specs=[
            pl.BlockSpec((tm, D_MODEL), row),
            pl.BlockSpec((tm, PLE_DIM), row),
            pl.BlockSpec((1, D_MODEL), full),
            pl.BlockSpec((D_MODEL, D_MODEL), full),
            pl.BlockSpec((PLE_DIM, D_MODEL), full),
            pl.BlockSpec((tm, D_MODEL), row),
        ],
        out_specs=[
            pl.BlockSpec((tm, D_MODEL), row), pl.BlockSpec((tm, D_MODEL), row),
            pl.BlockSpec((tm, D_MODEL), row), pl.BlockSpec((tm, D_MODEL), row),
            pl.BlockSpec((8, D_MODEL), full), pl.BlockSpec((8, LANES), full),
        ],
        out_shape=[
            jax.ShapeDtypeStruct((s_len, D_MODEL), F32), bf, bf, bf,
            jax.ShapeDtypeStruct((8, D_MODEL), F32), jax.ShapeDtypeStruct((8, LANES), F32),
        ],
        compiler_params=_cparams(48),
        name="ple_loss",
    )(x3, p, g, w_pg, w_pp, target)


def _qknorm_bwd(fq, fk, dqs, dk, dv, qn, kn, bd, bd_t, tm=256):
    s_len = fq.shape[0]

    def body(fq_ref, fk_ref, dq_ref, dk_ref, dv_ref, qn_ref, kn_ref, bd_ref, bdt_ref,
             dz_ref, dqn_ref, dkn_ref):
        @pl.when(pl.program_id(0) == 0)
        def _():
            dqn_ref[...] = jnp.zeros_like(dqn_ref)
            dkn_ref[...] = jnp.zeros_like(dkn_ref)

        bd_m = bd_ref[...]
        bdt_m = bdt_ref[...]

        def one(x, dy, gain, dgain_ref):
            xn, rw = _head_rms(x, bd_m, bdt_m)
            dgain_ref[0:1, :] += _colsum(dy * xn)
            dxn = dy * gain
            return rw * (dxn - xn * _head_mean(dxn * xn, bd_m, bdt_m))

        dz_ref[:, 0:ATT_W] = one(fq_ref[...], dq_ref[...] * QK_SCALE, qn_ref[...], dqn_ref).astype(BF16)
        dz_ref[:, ATT_W:2 * ATT_W] = one(fk_ref[...], dk_ref[...], kn_ref[...], dkn_ref).astype(BF16)
        dz_ref[:, 2 * ATT_W:] = dv_ref[...].astype(BF16)

    row = lambda i: (i, 0)
    full = lambda i: (0, 0)
    att = pl.BlockSpec((tm, ATT_W), row)
    return pl.pallas_call(
        body,
        grid=(s_len // tm,),
        in_specs=[att, att, att, att, att,
                  pl.BlockSpec((1, ATT_W), full), pl.BlockSpec((1, ATT_W), full),
                  pl.BlockSpec((ATT_W, LANES), full), pl.BlockSpec((LANES, ATT_W), full)],
        out_specs=[pl.BlockSpec((tm, 3 * ATT_W), row), pl.BlockSpec((8, ATT_W), full), pl.BlockSpec((8, ATT_W), full)],
        out_shape=[jax.ShapeDtypeStruct((s_len, 3 * ATT_W), BF16),
                   jax.ShapeDtypeStruct((8, ATT_W), F32), jax.ShapeDtypeStruct((8, ATT_W), F32)],
        name="qknorm_bwd",
    )(fq, fk, dqs, dk, dv, qn, kn, bd, bd_t)


def _inproj_bwd(x1, dx2, g, dzf, dlogf, logf, dzs, dgates, w_fox, w_fl, w_sb, w_gates, tm=256):
    s_len = x1.shape[0]

    def body(x_ref, d_ref, g_ref, dzf_ref, dlf_ref, lf_ref, dzs_ref, dgt_ref, wf_ref, wl_ref, ws_ref, wg_ref,
             dx_ref, h_ref, dfl_ref, dg_ref, db_ref):
        @pl.when(pl.program_id(0) == 0)
        def _():
            dg_ref[...] = jnp.zeros_like(dg_ref)
            db_ref[...] = jnp.zeros_like(db_ref)

        xn, r = _rms(x_ref[...])
        gain = g_ref[...]
        h_ref[...] = (xn * gain).astype(BF16)
        lane = lax.broadcasted_iota(jnp.int32, (tm, LANES), 1)
        dfl = jnp.where(lane < N_HEADS, dlf_ref[...] * (1.0 - jnp.exp(lf_ref[...])), 0.0)
        db_ref[0:1, :] += _colsum(dfl)
        dflb = dfl.astype(BF16)
        dfl_ref[...] = dflb
        dh = (_dot_nt(dzf_ref[...], wf_ref[...]) + _dot_nt(dflb, wl_ref[...])
              + _dot_nt(dzs_ref[...], ws_ref[...]) + _dot_nt(dgt_ref[...], wg_ref[...]))
        dx_ref[...] = d_ref[...] + _rms_bwd(dh, xn, r, gain)
        dg_ref[0:1, :] += _colsum(dh * xn)

    row = lambda i: (i, 0)
    full = lambda i: (0, 0)
    return pl.pallas_call(
        body,
        grid=(s_len // tm,),
        in_specs=[
            pl.BlockSpec((tm, D_MODEL), row),
            pl.BlockSpec((tm, D_MODEL), row),
            pl.BlockSpec((1, D_MODEL), full),
            pl.BlockSpec((tm, 3 * ATT_W), row),
            pl.BlockSpec((tm, LANES), row),
            pl.BlockSpec((tm, LANES), row),
            pl.BlockSpec((tm, 3 * ATT_W), row),
            pl.BlockSpec((tm, 2 * D_MODEL), row),
            pl.BlockSpec((D_MODEL, 3 * ATT_W), full),
            pl.BlockSpec((D_MODEL, LANES), full),
            pl.BlockSpec((D_MODEL, 3 * ATT_W), full),
            pl.BlockSpec((D_MODEL, 2 * D_MODEL), full),
        ],
        out_specs=[
            pl.BlockSpec((tm, D_MODEL), row), pl.BlockSpec((tm, D_MODEL), row), pl.BlockSpec((tm, LANES), row),
            pl.BlockSpec((8, D_MODEL), full), pl.BlockSpec((8, LANES), full),
        ],
        out_shape=[
            jax.ShapeDtypeStruct((s_len, D_MODEL), F32), jax.ShapeDtypeStruct((s_len, D_MODEL), BF16),
            jax.ShapeDtypeStruct((s_len, LANES), BF16),
            jax.ShapeDtypeStruct((8, D_MODEL), F32), jax.ShapeDtypeStruct((8, LANES), F32),
        ],
        compiler_params=_cparams(56),
        name="inproj_bwd",
    )(x1, dx2, g, dzf, dlogf, logf, dzs, dgates, w_fox, w_fl, w_sb, w_gates)


def _split_w_in(w_in):
    o = 3 * ATT_W
    w_fox = w_in[:, 0:o]
    w_fl = jnp.pad(w_in[:, o:o + N_HEADS], ((0, 0), (0, LANES - N_HEADS)))
    w_sb = w_in[:, o + N_HEADS:2 * o + N_HEADS]
    w_gates = w_in[:, 2 * o + N_HEADS:]
    return w_fox, w_fl, w_sb, w_gates


def _local_grads(x, p, target, small, full, pending=None, send_early=None):
    blk = ATT_BLOCK
    bd, bd_t = _head_sum_matrices()
    full = dict(full)
    late = list(pending) if pending else []

    x1, a1, b1, *gathered = _ffn_fwd(x, small["ffn1_norm"], full["ffn1_w_gate"], full["ffn1_w_up"],
                                     full["ffn1_w_down"], gather=[pending[k] for k in late])
    for k, gth in zip(late, gathered):
        full[k] = gth if k in KEPT_AS_SHARDS else _whole(k, gth)
    w_fox, w_fl, w_sb, w_gates = _split_w_in(full["w_in"])
    bias = jnp.pad(small["forget_bias"], ((0, 0), (0, LANES - N_HEADS)))
    qn = jnp.tile(small["q_norm"], (1, N_HEADS))
    kn = jnp.tile(small["k_norm"], (1, N_HEADS))
    fq, fk, f_qs, f_k, f_v, logf, s_qs, s_k, s_v, gates = _inproj_fwd(
        x1, small["mix_norm"], w_fox, w_fl, w_sb, w_gates, bias, qn, kn, bd, bd_t)
    f_cum = _cumsum_rows(logf, reverse=False)
    f8 = f_cum[:, 0:N_HEADS]
    fw = jnp.repeat(f8, HEAD_DIM, axis=1)
    ft4 = _pair_rows_t(f8, blk)
    f_k3, f_v3 = _blocked_rows(f_k, blk), _blocked_rows(f_v, blk)
    y_fox, lse, f_first = _fox_fwd(f_qs, f_k3, f_v3, fw, ft4, _key_norm_bound(f_k))
    s_k3, s_v3 = _blocked_rows(s_k, blk), _blocked_rows(s_v, blk)
    y_sb, s_rtot, s_first = _sb_fwd(s_qs, s_k3, s_v3)
    x2 = _merge_fwd(x1, gates, y_fox, y_sb, full["w_branch_fox"], full["w_branch_sb"], full["w_out"])
    x3, a2, b2 = _ffn_fwd(x2, small["ffn2_norm"], full["ffn2_w_gate"], full["ffn2_w_up"], full["ffn2_w_down"])

    dx3, du_ple, dt_ple, hn_ple, dg_ple, loss_sum = _ple_loss(
        x3, p, small["ple_norm"], full["w_ple_gate"], full["w_ple_proj"], target)
    dx2, u2, da2, db2, h_ffn2, d3_bf, dg_ffn2 = _ffn_bwd(
        x2, dx3, small["ffn2_norm"], a2, b2, full["ffn2_w_gate"], full["ffn2_w_up"], full["ffn2_w_down"])
    dy_fox, dy_sb, dgates, d_of, d_os, merged, d2_bf = _merge_bwd(
        dx2, gates, y_fox, y_sb, full["w_branch_fox"], full["w_branch_sb"], full["w_out"])

    f_dqs, dfq_p, f_dkt4, f_dvt4, dft4 = _fox_bwd(f_qs, f_k3, f_v3, dy_fox, y_fox, lse, fw, ft4, f_first)
    s_dqs, s_dkt4, s_dvt4 = _sb_bwd(s_qs, s_k3, s_v3, dy_sb, s_rtot, s_first)

    dzf, dqn8, dkn8 = _qknorm_bwd(fq, fk, f_dqs, _unblocked_t(f_dkt4), _unblocked_t(f_dvt4), qn, kn, bd, bd_t)
    dzs = jnp.concatenate([s_dqs * QK_SCALE, _unblocked_t(s_dkt4), _unblocked_t(s_dvt4)], axis=1).astype(BF16)
    df8 = _unpair_rows_t(dft4) + dfq_p[:, :, 0:2].transpose(1, 0, 2).reshape(-1, N_HEADS)
    dlogf = _cumsum_rows(jnp.pad(df8, ((0, 0), (0, LANES - N_HEADS))), reverse=True)
    dx1, h_mix, dfl, dg_mix, dbias8 = _inproj_bwd(
        x1, dx2, small["mix_norm"], dzf, dlogf, logf, dzs, dgates, w_fox, w_fl, w_sb, w_gates)

    one = lambda t: t[None]
    gw = {}
    gw["ffn2_w_gate"] = _wgrad(da2, one(h_ffn2), name="wgrad_ffn2_gate")
    gw["ffn2_w_up"] = _wgrad(db2, one(h_ffn2), name="wgrad_ffn2_up")
    gw["ffn2_w_down"] = _wgrad(u2, one(d3_bf), scale=0.5, name="wgrad_ffn2_down")
    g_fox = _wgrad(one(h_mix), one(dzf), name="wgrad_in_fox")[0]
    g_fl = _wgrad(one(h_mix), one(dfl), name="wgrad_in_forget")[0]
    g_sb = _wgrad(one(h_mix), one(dzs), name="wgrad_in_sb")[0]
    g_gt = _wgrad(one(h_mix), one(dgates), name="wgrad_in_gates")[0]
    gw["w_in"] = jnp.concatenate([g_fox, g_fl[:, 0:N_HEADS], g_sb, g_gt], axis=1)
    gw["w_branch_fox"] = _wgrad(one(y_fox), one(d_of), name="wgrad_branch_fox")[0]
    gw["w_branch_sb"] = _wgrad(one(y_sb), one(d_os), name="wgrad_branch_sb")[0]
    gw["w_out"] = _wgrad(one(merged), one(d2_bf), name="wgrad_out")[0]
    gw["w_ple_gate"] = _wgrad(one(hn_ple), one(du_ple), name="wgrad_ple_gate")[0]
    gw["w_ple_proj"] = _wgrad(one(p), one(dt_ple), name="wgrad_ple_proj")[0]

    sent_names, to_send = send_early(gw) if send_early else ([], [])
    grad_x, u1, da1, db1, h_ffn1, d1_bf, dg_ffn1, *landed = _ffn_bwd(
        x, dx1, small["ffn1_norm"], a1, b1, full["ffn1_w_gate"], full["ffn1_w_up"], full["ffn1_w_down"],
        scatter=to_send)
    gw["ffn1_w_gate"] = _wgrad(da1, one(h_ffn1), name="wgrad_ffn1_gate")
    gw["ffn1_w_up"] = _wgrad(db1, one(h_ffn1), name="wgrad_ffn1_up")
    gw["ffn1_w_down"] = _wgrad(u1, one(d1_bf), scale=0.5, name="wgrad_ffn1_down")

    fold = lambda t: jnp.sum(t[0:1].reshape(N_HEADS, HEAD_DIM), axis=0, keepdims=True)
    gs = {
        "ffn1_norm": dg_ffn1[0:1], "mix_norm": dg_mix[0:1], "ffn2_norm": dg_ffn2[0:1], "ple_norm": dg_ple[0:1],
        "forget_bias": dbias8[0:1, 0:N_HEADS], "q_norm": fold(dqn8), "k_norm": fold(dkn8),
    }
    return loss_sum, grad_x, gw, gs, dict(zip(sent_names, landed))


def _position():
    return lax.axis_index("x"), lax.axis_index("y"), lax.axis_index("c")


def _other_chips(x, y):
    return [(1 - x, y), (x, 1 - y), (1 - x, 1 - y)]


ANY = pl.BlockSpec(memory_space=pl.ANY)


def _place_own_shard(w, q):
    rows, cols = w.shape
    tr = _row_block(rows, cols * 4, budget=2 * MIB)

    def body(q_ref, w_ref, o_ref):
        o_ref[0] = w_ref[...].astype(BF16)

    return pl.pallas_call(
        body,
        grid_spec=pltpu.PrefetchScalarGridSpec(
            num_scalar_prefetch=1,
            grid=(rows // tr,),
            in_specs=[pl.BlockSpec((tr, cols), lambda i, q_ref: (i, 0))],
            out_specs=pl.BlockSpec((1, tr, cols), lambda i, q_ref: (q_ref[0], i, 0)),
        ),
        out_shape=jax.ShapeDtypeStruct((N_CHIPS, rows, cols), BF16),
        name="place_own_shard",
    )(q, w)


def _gather_semaphores(n):
    return [pltpu.SemaphoreType.DMA((6 * n,)), pltpu.SemaphoreType.DMA((6 * n,))]


def _gather_steps(bufs, send_sems, recv_sems):
    n = len(bufs)
    x, y, c = _position()
    q = 2 * x + y
    chips = _other_chips(x, y)
    sibling = (x, y, 1 - c)

    def half(a, slot, which):
        r2 = bufs[a].shape[1] // 2
        return bufs[a].at[slot, pl.ds(which * r2, r2), :]

    def copy(a, k, region, to):
        return pltpu.make_async_remote_copy(
            src_ref=region, dst_ref=region, send_sem=send_sems.at[6 * a + k], recv_sem=recv_sems.at[6 * a + k],
            device_id=to, device_id_type=MESH)

    def to_chip(a, k):
        tx, ty = chips[k]
        return copy(a, k, half(a, q, c), (tx, ty, c))

    def to_sibling(a, k):
        tx, ty = chips[k]
        return copy(a, 3 + k, half(a, 2 * tx + ty, c), sibling)

    def start():
        for a in range(n):
            for k in range(3):
                to_chip(a, k).start()

    def finish():
        for a in range(n):
            for k, (tx, ty) in enumerate(chips):
                copy(a, k, half(a, 2 * tx + ty, c), (tx, ty, c)).wait_recv()
                to_sibling(a, k).start()
        for a in range(n):
            for k, (tx, ty) in enumerate(chips):
                copy(a, 3 + k, half(a, 2 * tx + ty, 1 - c), sibling).wait_recv()
        for a in range(n):
            for k in range(3):
                to_chip(a, k).wait_send()
                to_sibling(a, k).wait_send()

    return start, finish


def _allgather_weights(slots):
    n = len(slots)

    def body(*refs):
        start, finish = _gather_steps(refs[n:2 * n], *refs[2 * n:])
        start()
        finish()

    return pl.pallas_call(
        body,
        in_specs=[ANY] * n,
        out_specs=[ANY] * n,
        out_shape=[jax.ShapeDtypeStruct(s.shape, s.dtype) for s in slots],
        input_output_aliases={a: a for a in range(n)},
        scratch_shapes=_gather_semaphores(n),
        name="allgather_weights",
    )(*slots)


def _exchange_pair_halves(grads):
    n = len(grads)

    def body(*refs):
        ins, outs = refs[0:n], refs[n:2 * n]
        send_sems, recv_sems = refs[2 * n:]
        x, y, c = _position()
        copies = []
        for a in range(n):
            r2 = grads[a].shape[1] // 2
            cp = pltpu.make_async_remote_copy(
                src_ref=ins[a].at[:, pl.ds((1 - c) * r2, r2), :], dst_ref=outs[a],
                send_sem=send_sems.at[a], recv_sem=recv_sems.at[a], device_id=(x, y, 1 - c), device_id_type=MESH)
            cp.start()
            copies.append(cp)
        for cp in copies:
            cp.wait()

    return pl.pallas_call(
        body,
        in_specs=[ANY] * n,
        out_specs=[ANY] * n,
        out_shape=[jax.ShapeDtypeStruct((N_CHIPS, g.shape[1] // 2, g.shape[2]), g.dtype) for g in grads],
        scratch_shapes=[pltpu.SemaphoreType.DMA((n,)), pltpu.SemaphoreType.DMA((n,))],
        name="rs_pair_exchange",
    )(*grads)


def _scatter_semaphores(n):
    return [pltpu.SemaphoreType.DMA((3 * n,)), pltpu.SemaphoreType.DMA((3 * n,)), pltpu.SemaphoreType.DMA((n,))]


def _scatter_steps(ins, outs, send_sems, recv_sems, local_sems):
    n = len(ins)
    x, y, c = _position()
    q = 2 * x + y
    chips = _other_chips(x, y)

    def own(a):
        return pltpu.make_async_copy(ins[a].at[q], outs[a].at[q], local_sems.at[a])

    def to_chip(a, k):
        tx, ty = chips[k]
        return pltpu.make_async_remote_copy(
            src_ref=ins[a].at[2 * tx + ty], dst_ref=outs[a].at[q],
            send_sem=send_sems.at[3 * a + k], recv_sem=recv_sems.at[3 * a + k],
            device_id=(tx, ty, c), device_id_type=MESH)

    def start():
        for a in range(n):
            own(a).start()
            for k in range(3):
                to_chip(a, k).start()

    def finish():
        for a in range(n):
            own(a).wait()
            for k in range(3):
                to_chip(a, k).wait()

    return start, finish


def _scatter_to_owner_chips(pairs):
    n = len(pairs)

    def body(*refs):
        start, finish = _scatter_steps(refs[0:n], refs[n:2 * n], *refs[2 * n:])
        start()
        finish()

    return pl.pallas_call(
        body,
        in_specs=[ANY] * n,
        out_specs=[ANY] * n,
        out_shape=[jax.ShapeDtypeStruct(p.shape, p.dtype) for p in pairs],
        scratch_shapes=_scatter_semaphores(n),
        name="rs_scatter",
    )(*pairs)


def _join_halves(shards):
    n = len(shards)

    def body(*refs):
        bufs = refs[n:2 * n]
        send_sems, recv_sems = refs[2 * n:]
        x, y, c = _position()
        started = []
        for a in range(n):
            r2 = shards[a].shape[0] // 2
            mine = bufs[a].at[pl.ds(c * r2, r2), :]
            cp = pltpu.make_async_remote_copy(
                src_ref=mine, dst_ref=mine, send_sem=send_sems.at[a], recv_sem=recv_sems.at[a],
                device_id=(x, y, 1 - c), device_id_type=MESH)
            cp.start()
            started.append(cp)
        for cp in started:
            cp.wait()

    return pl.pallas_call(
        body,
        in_specs=[ANY] * n,
        out_specs=[ANY] * n,
        out_shape=[jax.ShapeDtypeStruct(t.shape, t.dtype) for t in shards],
        input_output_aliases={a: a for a in range(n)},
        scratch_shapes=[pltpu.SemaphoreType.DMA((n,)), pltpu.SemaphoreType.DMA((n,))],
        name="rs_join_halves",
    )(*shards)


def _add_pair(g, got, c):
    _, r2, cols = got.shape

    def body(c_ref, g_ref, got_ref, o_ref):
        o_ref[...] = (g_ref[...].astype(F32) + got_ref[...].astype(F32)).astype(BF16)

    spec = pl.BlockSpec((1, r2, cols), lambda s, c_ref: (s, 0, 0))
    return pl.pallas_call(
        body,
        grid_spec=pltpu.PrefetchScalarGridSpec(
            num_scalar_prefetch=1,
            grid=(N_CHIPS,),
            in_specs=[pl.BlockSpec((1, r2, cols), lambda s, c_ref: (s, c_ref[0], 0)), spec],
            out_specs=spec,
        ),
        out_shape=jax.ShapeDtypeStruct(got.shape, BF16),
        name="rs_add_pair",
    )(c, g, got)


def _add_chips(parts, c):
    _, r2, cols = parts.shape

    def body(c_ref, p0, p1, p2, p3, o_ref):
        o_ref[...] = ((p0[0].astype(F32) + p1[0].astype(F32)) + p2[0].astype(F32)) + p3[0].astype(F32)

    specs = [pl.BlockSpec((1, r2, cols), functools.partial(lambda i, c_ref, s: (s, 0, 0), s=s))
             for s in range(N_CHIPS)]
    return pl.pallas_call(
        body,
        grid_spec=pltpu.PrefetchScalarGridSpec(
            num_scalar_prefetch=1,
            grid=(1,),
            in_specs=specs,
            out_specs=pl.BlockSpec((r2, cols), lambda i, c_ref: (c_ref[0], 0)),
        ),
        out_shape=jax.ShapeDtypeStruct((2 * r2, cols), F32),
        name="rs_add_chips",
    )(c, parts, parts, parts, parts)


def _allreduce_small(part):
    shape = part.shape

    def body(in_ref, out_ref, gather_ref, send_sems, recv_sems):
        x, y, c = _position()
        me = 4 * x + 2 * y + c
        relations = [(a, b, d) for a in (0, 1) for b in (0, 1) for d in (0, 1)][1:]
        flip = lambda v, f: 1 - v if f else v
        copies = []
        for k, (a, b, d) in enumerate(relations):
            cp = pltpu.make_async_remote_copy(
                src_ref=in_ref, dst_ref=gather_ref.at[me], send_sem=send_sems.at[k], recv_sem=recv_sems.at[k],
                device_id=(flip(x, a), flip(y, b), flip(c, d)), device_id_type=MESH)
            cp.start()
            copies.append(cp)
        gather_ref[me] = in_ref[...]
        for cp in copies:
            cp.wait()
        total = gather_ref[0]
        for dev in range(1, 8):
            total = total + gather_ref[dev]
        out_ref[...] = total

    vmem = pl.BlockSpec(memory_space=pltpu.VMEM)
    return pl.pallas_call(
        body,
        in_specs=[vmem],
        out_specs=vmem,
        out_shape=jax.ShapeDtypeStruct(shape, F32),
        scratch_shapes=[pltpu.VMEM((8,) + shape, F32), pltpu.SemaphoreType.DMA((7,)), pltpu.SemaphoreType.DMA((7,))],
        name="allreduce_small",
    )(part)


def _adamw(w, g, m, v):
    rows, cols = w.shape
    tr = _row_block(rows, cols * 4, budget=MIB)
    c1 = 1.0 / (1.0 - ADAM_B1 ** ADAM_STEP)
    c2 = 1.0 / (1.0 - ADAM_B2 ** ADAM_STEP)

    def body(w_ref, g_ref, m_ref, v_ref, d_ref, nm_ref, nv_ref):
        g_ = g_ref[...]
        nm = ADAM_B1 * m_ref[...] + (1.0 - ADAM_B1) * g_
        nv = ADAM_B2 * v_ref[...] + (1.0 - ADAM_B2) * (g_ * g_)
        nm_ref[...] = nm
        nv_ref[...] = nv
        d_ref[...] = -ADAM_LR * ((nm * c1) / (jnp.sqrt(nv * c2) + ADAM_EPS) + ADAM_WD * w_ref[...])

    spec = pl.BlockSpec((tr, cols), lambda i: (i, 0))
    out = jax.ShapeDtypeStruct((rows, cols), F32)
    return pl.pallas_call(
        body,
        grid=(rows // tr,),
        in_specs=[spec] * 4,
        out_specs=[spec] * 3,
        out_shape=[out] * 3,
        name="adamw",
    )(w, g, m, v)


BIG = ["ffn1_w_gate", "ffn1_w_up", "ffn1_w_down", "w_in", "w_branch_fox", "w_branch_sb", "w_out",
       "ffn2_w_gate", "ffn2_w_up", "ffn2_w_down", "w_ple_gate", "w_ple_proj"]
SMALL = ["ffn1_norm", "mix_norm", "ffn2_norm", "ple_norm", "forget_bias", "q_norm", "k_norm"]
COLUMN_SHARDED = ["w_in", "w_branch_fox", "w_branch_sb", "w_ple_proj"]
KEPT_AS_SHARDS = ["ffn1_w_gate", "ffn1_w_up", "ffn1_w_down", "ffn2_w_gate", "ffn2_w_up", "ffn2_w_down"]
WORKED_TRANSPOSED = ["ffn1_w_gate", "ffn1_w_up", "ffn2_w_gate", "ffn2_w_up"]
NEEDED_FIRST = ["ffn1_w_gate", "ffn1_w_up", "ffn1_w_down"]
ORDER = ["ffn1_norm", "ffn1_w_gate", "ffn1_w_up", "ffn1_w_down", "mix_norm", "w_in", "forget_bias", "q_norm",
         "k_norm", "w_branch_fox", "w_branch_sb", "w_out", "ffn2_norm", "ffn2_w_gate", "ffn2_w_up",
         "ffn2_w_down", "ple_norm", "w_ple_gate", "w_ple_proj"]
SMALL_ROWS = {"ffn1_norm": 0, "mix_norm": 1, "ffn2_norm": 2, "ple_norm": 3}
SMALL_COLS = {"forget_bias": (0, N_HEADS), "q_norm": (N_HEADS, HEAD_DIM), "k_norm": (N_HEADS + HEAD_DIM, HEAD_DIM)}
LOSS_ROW = 5


def _stored(name, a):
    return jnp.swapaxes(a[0], 0, 1) if name in WORKED_TRANSPOSED else a[0]


def _returned(name, t):
    return (jnp.swapaxes(t, 0, 1) if name in WORKED_TRANSPOSED else t)[None]


def _whole(name, gathered):
    if name in COLUMN_SHARDED:
        return jnp.concatenate([gathered[s] for s in range(N_CHIPS)], axis=1)
    return gathered.reshape(-1, gathered.shape[-1])


def _as_shards(name, whole):
    if name in COLUMN_SHARDED:
        k, n = whole.shape
        return whole.reshape(k, N_CHIPS, n // N_CHIPS).transpose(1, 0, 2)
    return whole.reshape(N_CHIPS, whole.shape[0] // N_CHIPS, whole.shape[1])


def _pack_small(values, extra=None):
    rows = [values[k] for k in ("ffn1_norm", "mix_norm", "ffn2_norm", "ple_norm")]
    tail = jnp.concatenate([values["forget_bias"], values["q_norm"], values["k_norm"]], axis=1)
    rows.append(jnp.pad(tail, ((0, 0), (0, D_MODEL - tail.shape[1]))))
    packed = jnp.concatenate(rows + [jnp.zeros((3, D_MODEL), F32)], axis=0)
    if extra is not None:
        packed = packed.at[LOSS_ROW, 0].set(extra)
    return packed


def _unpack_small(packed):
    out = {k: packed[r:r + 1] for k, r in SMALL_ROWS.items()}
    for k, (start, size) in SMALL_COLS.items():
        out[k] = packed[4:5, start:start + size]
    return out


def kernel(x, p, ffn1_norm, ffn1_w_gate, ffn1_w_up, ffn1_w_down, mix_norm, w_in, forget_bias, q_norm, k_norm, w_branch_fox, w_branch_sb, w_out, ffn2_norm, ffn2_w_gate, ffn2_w_up, ffn2_w_down, ple_norm, w_ple_gate, w_ple_proj, loss_target, m_ffn1_norm, m_ffn1_w_gate, m_ffn1_w_up, m_ffn1_w_down, m_mix_norm, m_w_in, m_forget_bias, m_q_norm, m_k_norm, m_w_branch_fox, m_w_branch_sb, m_w_out, m_ffn2_norm, m_ffn2_w_gate, m_ffn2_w_up, m_ffn2_w_down, m_ple_norm, m_w_ple_gate, m_w_ple_proj, v_ffn1_norm, v_ffn1_w_gate, v_ffn1_w_up, v_ffn1_w_down, v_mix_norm, v_w_in, v_forget_bias, v_q_norm, v_k_norm, v_w_branch_fox, v_w_branch_sb, v_w_out, v_ffn2_norm, v_ffn2_w_gate, v_ffn2_w_up, v_ffn2_w_down, v_ple_norm, v_w_ple_gate, v_w_ple_proj):
    args = dict(locals())
    weights = {k: args[k] for k in ORDER}
    moments_m = {k: args["m_" + k] for k in ORDER}
    moments_v = {k: args["v_" + k] for k in ORDER}

    c_idx = lax.axis_index("c").astype(jnp.int32).reshape(1)
    q_idx = (2 * lax.axis_index("x") + lax.axis_index("y")).astype(jnp.int32).reshape(1)
    own = {k: _place_own_shard(_stored(k, weights[k]), q_idx) for k in BIG}
    full = dict(zip(NEEDED_FIRST, _allgather_weights([own[k] for k in NEEDED_FIRST])))
    pending = {k: own[k] for k in BIG if k not in NEEDED_FIRST}
    small = {k: weights[k] for k in SMALL}

    def pair_sums(names, gw):
        slots = [gw[k] if k in KEPT_AS_SHARDS else _as_shards(k, gw[k]) for k in names]
        from_core = _exchange_pair_halves(slots)
        return [_add_pair(g, got, c_idx) for g, got in zip(slots, from_core)]

    late = [k for k in BIG if k not in NEEDED_FIRST]
    loss_sum, grad_x, gw, gs, parts = _local_grads(
        x[0], p[0, 0], loss_target[0], small, full, pending, lambda early: (late, pair_sums(late, early)))

    parts.update(zip(NEEDED_FIRST, _scatter_to_owner_chips(pair_sums(NEEDED_FIRST, gw))))
    grads_big = dict(zip(BIG, _join_halves([_add_chips(parts[k], c_idx) for k in BIG])))
    reduced = _allreduce_small(_pack_small(gs, extra=loss_sum[0, 0]))
    grads_small = _unpack_small(reduced)
    loss = reduced[LOSS_ROW, 0]

    grads, deltas, new_m, new_v = {}, {}, {}, {}
    for k in BIG:
        d, nm, nv = _adamw(_stored(k, weights[k]), grads_big[k], _stored(k, moments_m[k]), _stored(k, moments_v[k]))
        grads[k], deltas[k], new_m[k], new_v[k] = (_returned(k, t) for t in (grads_big[k], d, nm, nv))
    d_s, nm_s, nv_s = _adamw(_pack_small({k: weights[k] for k in SMALL}), reduced,
                             _pack_small({k: moments_m[k] for k in SMALL}),
                             _pack_small({k: moments_v[k] for k in SMALL}))
    for k in SMALL:
        grads[k] = grads_small[k]
    for name, packed in (("d", d_s), ("m", nm_s), ("v", nv_s)):
        target = {"d": deltas, "m": new_m, "v": new_v}[name]
        target.update(_unpack_small(packed))

    return (loss, grad_x[None], *[grads[k] for k in ORDER], *[deltas[k] for k in ORDER],
            *[new_m[k] for k in ORDER], *[new_v[k] for k in ORDER])
```

```python
import functools

import jax
import jax.numpy as jnp
from jax import lax
from jax.experimental import pallas as pl
from jax.experimental.pallas import tpu as pltpu

F32 = jnp.float32
BF16 = jnp.bfloat16

D_MODEL = 1024
D_FF = 2816
N_CHIPS = 4
FF_SHARD = D_FF // N_CHIPS
FFN_CHUNKS = 2
WGRAD_TOKENS = 2048
HEAD_DIM = 64
N_HEADS = 8
ATT_W = N_HEADS * HEAD_DIM
PAIR_W = 2 * HEAD_DIM
N_PAIRS = N_HEADS // 2
PLE_DIM = 256
IN_WIDTH = 3 * ATT_W + N_HEADS + 3 * ATT_W + 2 * D_MODEL
EPS = 1e-6
QK_SCALE = HEAD_DIM ** -0.5
LANES = 128
ATT_BLOCK = 256
FOX_Q_BLOCK = 512
SB_Q_BLOCK = 256
NEG_BIG = -1e30
EXP_UNDERFLOW = 110.0

ADAM_LR = 0.001
ADAM_B1 = 0.9
ADAM_B2 = 0.999
ADAM_EPS = 1e-08
ADAM_WD = 0.01
ADAM_STEP = 10

MESH = pl.DeviceIdType.MESH
MIB = 1024 * 1024


def _cparams(vmem_mib=48):
    return pltpu.CompilerParams(vmem_limit_bytes=vmem_mib * MIB)


def _dot(a, b):
    return jnp.dot(a, b, preferred_element_type=F32)


def _dot_tn(a, b):
    return lax.dot_general(a, b, (((0,), (0,)), ((), ())), preferred_element_type=F32)


def _dot_nt(a, b):
    return lax.dot_general(a, b, (((1,), (1,)), ((), ())), preferred_element_type=F32)


def _sigmoid(x):
    return 1.0 / (1.0 + jnp.exp(-x))


def _split2(x):
    hi = x.astype(BF16)
    lo = (x - hi.astype(F32)).astype(BF16)
    return hi, lo


def _dot_split2(x, m):
    hi, lo = _split2(x)
    return _dot(hi, m) + _dot(lo, m)


def _split3(x):
    hi = x.astype(BF16)
    rest = x - hi.astype(F32)
    mid = rest.astype(BF16)
    lo = (rest - mid.astype(F32)).astype(BF16)
    return hi, mid, lo


def _rms(x):
    r = lax.rsqrt(jnp.mean(x * x, axis=-1, keepdims=True) + EPS)
    return x * r, r


def _rms_bwd(dh, xn, r, g):
    dxn = dh * g
    return r * (dxn - xn * jnp.mean(dxn * xn, axis=-1, keepdims=True))


def _colsum(x):
    return jnp.sum(x, axis=0, keepdims=True)


def _row_block(rows, row_bytes, budget):
    best = None
    for t in range(8, rows + 1, 8):
        if rows % t == 0 and t * row_bytes <= budget:
            best = t
    return best if best is not None else rows


def _ffn_fwd(x, g, wg, wu, wd, gather=(), tm=512):
    s_len = x.shape[0]
    n = len(gather)
    steps = s_len // tm

    def body(x_ref, g_ref, wg_ref, wu_ref, wd_ref, *rest):
        o_ref, a_ref, b_ref = rest[n:n + 3]
        h_s, acc_s = rest[2 * n + 3:2 * n + 5]
        i = pl.program_id(0)
        j = pl.program_id(1)
        if n:
            start, finish = _gather_steps(rest[n + 3:2 * n + 3], *rest[2 * n + 5:])
            pl.when((i == 0) & (j == 0))(start)

        @pl.when(j == 0)
        def _():
            xn, _ = _rms(x_ref[...])
            h_s[...] = (xn * g_ref[...]).astype(BF16)
            acc_s[...] = jnp.zeros_like(acc_s)

        chunks = [pl.ds(r * (tm // FFN_CHUNKS), tm // FFN_CHUNKS) for r in range(FFN_CHUNKS)]
        pre = [(_dot_nt(h_s[rows, :], wg_ref[0]), _dot_nt(h_s[rows, :], wu_ref[0])) for rows in chunks]
        us = []
        for rows, (a, b) in zip(chunks, pre):
            a_ref[0, rows, :] = a.astype(BF16)
            b_ref[0, rows, :] = b.astype(BF16)
            us.append((a * _sigmoid(a) * b).astype(BF16))
        for rows, u in zip(chunks, us):
            acc_s[rows, :] += _dot(u, wd_ref[0])

        @pl.when(j == N_CHIPS - 1)
        def _():
            o_ref[...] = x_ref[...] + 0.5 * acc_s[...]

        if n:
            pl.when((i == steps - 1) & (j == N_CHIPS - 1))(finish)

    return pl.pallas_call(
        body,
        grid=(steps, N_CHIPS),
        in_specs=[
            pl.BlockSpec((tm, D_MODEL), lambda i, j: (i, 0)),
            pl.BlockSpec((1, D_MODEL), lambda i, j: (0, 0)),
            pl.BlockSpec((1, FF_SHARD, D_MODEL), lambda i, j: (j, 0, 0)),
            pl.BlockSpec((1, FF_SHARD, D_MODEL), lambda i, j: (j, 0, 0)),
            pl.BlockSpec((1, FF_SHARD, D_MODEL), lambda i, j: (j, 0, 0)),
        ] + [ANY] * n,
        out_specs=[pl.BlockSpec((tm, D_MODEL), lambda i, j: (i, 0)),
                   pl.BlockSpec((1, tm, FF_SHARD), lambda i, j: (j, i, 0)),
                   pl.BlockSpec((1, tm, FF_SHARD), lambda i, j: (j, i, 0))] + [ANY] * n,
        out_shape=[jax.ShapeDtypeStruct((s_len, D_MODEL), F32),
                   jax.ShapeDtypeStruct((N_CHIPS, s_len, FF_SHARD), BF16),
                   jax.ShapeDtypeStruct((N_CHIPS, s_len, FF_SHARD), BF16)]
        + [jax.ShapeDtypeStruct(s.shape, s.dtype) for s in gather],
        input_output_aliases={5 + a: 3 + a for a in range(n)},
        scratch_shapes=[pltpu.VMEM((tm, D_MODEL), BF16), pltpu.VMEM((tm, D_MODEL), F32)]
        + (_gather_semaphores(n) if n else []),
        compiler_params=_cparams(48),
        name="ffn_fwd_gathering" if n else "ffn_fwd",
    )(x, g, wg, wu, wd, *gather)


def _ffn_bwd(x, d, g, a_pre, b_pre, wg, wu, wd, scatter=(), tm=512):
    s_len = x.shape[0]
    nb = s_len // tm
    n = len(scatter)

    def body(x_ref, d_ref, g_ref, a_ref, b_ref, wg_ref, wu_ref, wd_ref, *rest):
        dx_ref, u_ref, da_ref, db_ref, h_ref, dbf_ref, dg_ref = rest[n:n + 7]
        dbf_s, dh_s = rest[2 * n + 7:2 * n + 9]
        i = pl.program_id(0)
        j = pl.program_id(1)
        if n:
            start, finish = _scatter_steps(rest[0:n], rest[n + 7:2 * n + 7], *rest[2 * n + 9:])
            pl.when((i == 0) & (j == 0))(start)

        @pl.when(j == 0)
        def _():
            xn, _ = _rms(x_ref[...])
            h_ref[...] = (xn * g_ref[...]).astype(BF16)
            dbf = d_ref[...].astype(BF16)
            dbf_s[...] = dbf
            dbf_ref[...] = dbf
            dh_s[...] = jnp.zeros_like(dh_s)

        @pl.when((i == 0) & (j == 0))
        def _():
            dg_ref[...] = jnp.zeros_like(dg_ref)

        chunks = [pl.ds(r * (tm // FFN_CHUNKS), tm // FFN_CHUNKS) for r in range(FFN_CHUNKS)]
        dus = [0.5 * _dot_nt(dbf_s[rows, :], wd_ref[0]) for rows in chunks]
        das, dbs = [], []
        for rows, du in zip(chunks, dus):
            a = a_ref[0, rows, :].astype(F32)
            b = b_ref[0, rows, :].astype(F32)
            s = _sigmoid(a)
            silu = a * s
            da = (du * b * (s * (1.0 + a * (1.0 - s)))).astype(BF16)
            db = (du * silu).astype(BF16)
            u_ref[0, rows, :] = (silu * b).astype(BF16)
            da_ref[0, rows, :] = da
            db_ref[0, rows, :] = db
            das.append(da)
            dbs.append(db)
        for rows, da, db in zip(chunks, das, dbs):
            dh_s[rows, :] += _dot(da, wg_ref[0]) + _dot(db, wu_ref[0])

        @pl.when(j == N_CHIPS - 1)
        def _():
            xn, r = _rms(x_ref[...])
            dh = dh_s[...]
            dx_ref[...] = d_ref[...] + _rms_bwd(dh, xn, r, g_ref[...])
            dg_ref[0:1, :] += _colsum(dh * xn)

        if n:
            pl.when((i == nb - 1) & (j == N_CHIPS - 1))(finish)

    row = lambda i, j: (i, 0)
    shard = lambda i, j: (j, 0, 0)
    act = lambda i, j: (j, i, 0)
    return pl.pallas_call(
        body,
        grid=(nb, N_CHIPS),
        in_specs=[
            pl.BlockSpec((tm, D_MODEL), row),
            pl.BlockSpec((tm, D_MODEL), row),
            pl.BlockSpec((1, D_MODEL), lambda i, j: (0, 0)),
            pl.BlockSpec((1, tm, FF_SHARD), act),
            pl.BlockSpec((1, tm, FF_SHARD), act),
            pl.BlockSpec((1, FF_SHARD, D_MODEL), shard),
            pl.BlockSpec((1, FF_SHARD, D_MODEL), shard),
            pl.BlockSpec((1, FF_SHARD, D_MODEL), shard),
        ] + [ANY] * n,
        out_specs=[
            pl.BlockSpec((tm, D_MODEL), row),
            pl.BlockSpec((1, tm, FF_SHARD), act),
            pl.BlockSpec((1, tm, FF_SHARD), act),
            pl.BlockSpec((1, tm, FF_SHARD), act),
            pl.BlockSpec((tm, D_MODEL), row),
            pl.BlockSpec((tm, D_MODEL), row),
            pl.BlockSpec((8, D_MODEL), lambda i, j: (0, 0)),
        ] + [ANY] * n,
        out_shape=[
            jax.ShapeDtypeStruct((s_len, D_MODEL), F32),
            jax.ShapeDtypeStruct((N_CHIPS, s_len, FF_SHARD), BF16),
            jax.ShapeDtypeStruct((N_CHIPS, s_len, FF_SHARD), BF16),
            jax.ShapeDtypeStruct((N_CHIPS, s_len, FF_SHARD), BF16),
            jax.ShapeDtypeStruct((s_len, D_MODEL), BF16),
            jax.ShapeDtypeStruct((s_len, D_MODEL), BF16),
            jax.ShapeDtypeStruct((8, D_MODEL), F32),
        ] + [jax.ShapeDtypeStruct(s.shape, s.dtype) for s in scatter],
        scratch_shapes=[
            pltpu.VMEM((tm, D_MODEL), BF16),
            pltpu.VMEM((tm, D_MODEL), F32),
        ] + (_scatter_semaphores(n) if n else []),
        compiler_params=_cparams(56),
        name="ffn_bwd_scattering" if n else "ffn_bwd",
    )(x, d, g, a_pre, b_pre, wg, wu, wd, *scatter)


def _wgrad(a, b, scale=1.0, name="wgrad"):
    na, s_len, k_dim = a.shape
    nb, _, n_dim = b.shape
    n = max(na, nb)
    ts = min(s_len, WGRAD_TOKENS)
    steps = s_len // ts

    def body(a_ref, b_ref, o_ref, acc_s):
        s = pl.program_id(1)

        @pl.when(s == 0)
        def _():
            acc_s[...] = jnp.zeros_like(acc_s)

        acc_s[...] += _dot_tn(a_ref[0].astype(BF16), b_ref[0].astype(BF16))

        @pl.when(s == steps - 1)
        def _():
            o_ref[0] = (acc_s[...] * scale).astype(BF16)

    a_map = (lambda m, s: (m, s, 0)) if na > 1 else (lambda m, s: (0, s, 0))
    b_map = (lambda m, s: (m, s, 0)) if nb > 1 else (lambda m, s: (0, s, 0))
    return pl.pallas_call(
        body,
        grid=(n, steps),
        in_specs=[pl.BlockSpec((1, ts, k_dim), a_map), pl.BlockSpec((1, ts, n_dim), b_map)],
        out_specs=pl.BlockSpec((1, k_dim, n_dim), lambda m, s: (m, 0, 0)),
        out_shape=jax.ShapeDtypeStruct((n, k_dim, n_dim), BF16),
        scratch_shapes=[pltpu.VMEM((k_dim, n_dim), F32)],
        compiler_params=_cparams(56),
        name=name,
    )(a, b)


def _head_sum_matrices():
    lane = lax.broadcasted_iota(jnp.int32, (ATT_W, LANES), 0) // HEAD_DIM
    col = lax.broadcasted_iota(jnp.int32, (ATT_W, LANES), 1)
    bd = (lane == col).astype(BF16)
    return bd, bd.T


def _head_mean(t, bd, bd_t):
    per_head = _dot_split2(t, bd) * (1.0 / HEAD_DIM)
    return _dot_split2(per_head, bd_t)


def _head_rms(x, bd, bd_t):
    per_head = _dot_split2(x * x, bd) * (1.0 / HEAD_DIM)
    r = lax.rsqrt(per_head + EPS)
    rw = _dot_split2(r, bd_t)
    return x * rw, rw


def _log_sigmoid(z):
    return jnp.minimum(z, 0.0) - jnp.log(1.0 + jnp.exp(-jnp.abs(z)))


def _inproj_fwd(x1, g, w_fox, w_fl, w_sb, w_gates, bias, qn, kn, bd, bd_t, tm=256):
    s_len = x1.shape[0]

    def body(x_ref, g_ref, wf_ref, wl_ref, ws_ref, wg_ref, bias_ref, qn_ref, kn_ref, bd_ref, bdt_ref,
             fq_ref, fk_ref, qs_ref, kf_ref, vf_ref, logf_ref, sq_ref, sk_ref, sv_ref, gates_ref):
        xn, _ = _rms(x_ref[...])
        h = (xn * g_ref[...]).astype(BF16)
        zf = _dot(h, wf_ref[...])
        fq = zf[:, 0:ATT_W]
        fk = zf[:, ATT_W:2 * ATT_W]
        fq_ref[...] = fq
        fk_ref[...] = fk
        bd_m = bd_ref[...]
        bdt_m = bdt_ref[...]
        fqn, _ = _head_rms(fq, bd_m, bdt_m)
        fkn, _ = _head_rms(fk, bd_m, bdt_m)
        qs_ref[...] = (fqn * qn_ref[...]).astype(BF16) * QK_SCALE
        kf_ref[...] = (fkn * kn_ref[...]).astype(BF16)
        vf_ref[...] = zf[:, 2 * ATT_W:3 * ATT_W].astype(BF16)
        logf_ref[...] = _log_sigmoid(_dot(h, wl_ref[...]) + bias_ref[...])
        zs = _dot(h, ws_ref[...])
        sq_ref[...] = zs[:, 0:ATT_W].astype(BF16) * QK_SCALE
        sk_ref[...] = zs[:, ATT_W:2 * ATT_W].astype(BF16)
        sv_ref[...] = zs[:, 2 * ATT_W:3 * ATT_W].astype(BF16)
        gates_ref[...] = _dot(h, wg_ref[...])

    row = lambda i: (i, 0)
    full = lambda i: (0, 0)
    att = lambda dt: jax.ShapeDtypeStruct((s_len, ATT_W), dt)
    return pl.pallas_call(
        body,
        grid=(s_len // tm,),
        in_specs=[
            pl.BlockSpec((tm, D_MODEL), row),
            pl.BlockSpec((1, D_MODEL), full),
            pl.BlockSpec((D_MODEL, 3 * ATT_W), full),
            pl.BlockSpec((D_MODEL, LANES), full),
            pl.BlockSpec((D_MODEL, 3 * ATT_W), full),
            pl.BlockSpec((D_MODEL, 2 * D_MODEL), full),
            pl.BlockSpec((1, LANES), full),
            pl.BlockSpec((1, ATT_W), full),
            pl.BlockSpec((1, ATT_W), full),
            pl.BlockSpec((ATT_W, LANES), full),
            pl.BlockSpec((LANES, ATT_W), full),
        ],
        out_specs=[
            pl.BlockSpec((tm, ATT_W), row), pl.BlockSpec((tm, ATT_W), row),
            pl.BlockSpec((tm, ATT_W), row), pl.BlockSpec((tm, ATT_W), row), pl.BlockSpec((tm, ATT_W), row),
            pl.BlockSpec((tm, LANES), row),
            pl.BlockSpec((tm, ATT_W), row), pl.BlockSpec((tm, ATT_W), row), pl.BlockSpec((tm, ATT_W), row),
            pl.BlockSpec((tm, 2 * D_MODEL), row),
        ],
        out_shape=[
            att(F32), att(F32), att(BF16), att(BF16), att(BF16),
            jax.ShapeDtypeStruct((s_len, LANES), F32),
            att(BF16), att(BF16), att(BF16),
            jax.ShapeDtypeStruct((s_len, 2 * D_MODEL), F32),
        ],
        compiler_params=_cparams(56),
        name="inproj_fwd",
    )(x1, g, w_fox, w_fl, w_sb, w_gates, bias, qn, kn, bd, bd_t)


def _tri(n, kind):
    r = lax.broadcasted_iota(jnp.int32, (n, n), 0)
    c = lax.broadcasted_iota(jnp.int32, (n, n), 1)
    m = {"row_ge_col": r >= c, "row_le_col": r <= c, "row_gt_col": r > c, "row_lt_col": r < c}[kind]
    return m.astype(BF16)


def _cumsum_rows(x, reverse, tm=256):
    s_len = x.shape[0]
    nb = s_len // tm
    tri = _tri(tm, "row_le_col" if reverse else "row_ge_col")
    edge = 0 if reverse else tm - 1

    def body(x_ref, tri_ref, o_ref, carry_s):
        @pl.when(pl.program_id(0) == 0)
        def _():
            carry_s[...] = jnp.zeros_like(carry_s)

        hi, mid, lo = _split3(x_ref[...])
        t = tri_ref[...]
        y = _dot(t, hi) + _dot(t, mid) + _dot(t, lo) + carry_s[...]
        o_ref[...] = y
        carry_s[...] = y[edge:edge + 1, :]

    order = (lambda i: (nb - 1 - i, 0)) if reverse else (lambda i: (i, 0))
    return pl.pallas_call(
        body,
        grid=(nb,),
        in_specs=[pl.BlockSpec((tm, LANES), order), pl.BlockSpec((tm, tm), lambda i: (0, 0))],
        out_specs=pl.BlockSpec((tm, LANES), order),
        out_shape=jax.ShapeDtypeStruct((s_len, LANES), F32),
        scratch_shapes=[pltpu.VMEM((1, LANES), F32)],
        name="cumsum_rev" if reverse else "cumsum_fwd",
    )(x, tri)


def _unblocked_t(t4):
    _, nb, _, blk = t4.shape
    return t4.transpose(1, 3, 0, 2).reshape(nb * blk, ATT_W)


def _blocked_rows(t, blk):
    return t.reshape(t.shape[0] // blk, blk, t.shape[1])


def _pair_rows_t(f8, blk):
    nb = f8.shape[0] // blk
    t = f8.reshape(nb, blk, N_PAIRS, 2).transpose(2, 0, 3, 1)
    return jnp.pad(t, ((0, 0), (0, 0), (0, 6), (0, 0)))


def _unpair_rows_t(t4):
    _, nb, _, blk = t4.shape
    return t4[:, :, 0:2, :].transpose(1, 3, 0, 2).reshape(nb * blk, N_HEADS)


def _head_masks(tq):
    lane = lax.broadcasted_iota(jnp.int32, (tq, PAIR_W), 1)
    return lane < HEAD_DIM


def _causal_mask(tq, tk, offset, strict):
    d = lax.broadcasted_iota(jnp.int32, (tq, tk), 1) - lax.broadcasted_iota(jnp.int32, (tq, tk), 0)
    return (d < offset) if strict else (d <= offset)


def _heads_of(ref, first):
    t = ref[...]
    zero = jnp.zeros_like(t)
    return [jnp.where(first, t, zero), jnp.where(first, zero, t)]


def _head_cols(ref):
    t = ref[...]
    return [t[:, 0:1], t[:, HEAD_DIM:HEAD_DIM + 1]]


def _att_specs(s_len, tq):
    tk = ATT_BLOCK
    nq, nk = s_len // tq, s_len // tk
    return dict(
        nq=nq,
        q=pl.BlockSpec((tq, PAIR_W), lambda p, i: (i, p)),
        k_t=pl.BlockSpec((1, nk, PAIR_W, tk), lambda p, i: (p, 0, 0, 0)),
        k_rows=pl.BlockSpec((nk, tk, PAIR_W), lambda p, i: (0, 0, p)),
        f_t=pl.BlockSpec((1, nk, 8, tk), lambda p, i: (p, 0, 0, 0)),
        first=pl.BlockSpec((1, 1, 8, LANES), lambda p, i: (p, i, 0, 0)),
        wide=jax.ShapeDtypeStruct((s_len, ATT_W), F32),
        k_t_out=jax.ShapeDtypeStruct((N_PAIRS, nk, PAIR_W, tk), F32),
        f_t_out=jax.ShapeDtypeStruct((N_PAIRS, nk, 8, tk), F32),
        first_out=jax.ShapeDtypeStruct((N_PAIRS, nq, 8, LANES), F32),
        acc=pltpu.VMEM((2, tq, PAIR_W), F32),
    )


def _first_block(first_ref, limit):
    return jnp.clip(jnp.max(first_ref[0, 0]).astype(jnp.int32), 0, limit)


def _key_norm_bound(k):
    sq = jnp.sum(jnp.square(k.astype(F32)).reshape(k.shape[0], N_HEADS, HEAD_DIM), axis=-1)
    bound = jnp.sqrt(jnp.max(sq, axis=0)).reshape(N_PAIRS, 2)
    return jnp.broadcast_to(jnp.pad(bound, ((0, 0), (0, 6)))[:, :, None], (N_PAIRS, 8, LANES))


def _fox_fwd(qs, k3, v3, fw, ft4, kmax):
    tq, tk = FOX_Q_BLOCK, ATT_BLOCK
    sp = _att_specs(qs.shape[0], tq)
    ratio = tq // tk

    def body(q_ref, k_ref, v_ref, fw_ref, ft_ref, kmax_ref, y_ref, lse_ref, first_ref, acc_ref, max_ref, sum_ref):
        i = pl.program_id(1)
        first = _head_masks(tq)
        qh = _heads_of(q_ref, first)
        fqh = _head_cols(fw_ref)
        acc_ref[...] = jnp.zeros_like(acc_ref)
        sum_ref[...] = jnp.zeros_like(sum_ref)
        max_ref[...] = jnp.full(max_ref.shape, NEG_BIG, F32)
        reach = []
        for n in range(2):
            qf = qh[n].astype(F32)
            reach.append(jnp.sqrt(jnp.sum(qf * qf, axis=-1, keepdims=True)) * kmax_ref[0, n:n + 1, 0:1] + fqh[n])

        def logits(j, shift, diag):
            k, fk = k_ref[j], ft_ref[0, j]
            raw = [_dot_nt(qh[n], k) for n in range(2)]
            out = []
            for n in range(2):
                s = raw[n] + (shift[n] - fk[n:n + 1, :])
                if diag:
                    s = jnp.where(_causal_mask(tq, tk, i * tq - j * tk, strict=False), s, NEG_BIG)
                out.append(s)
            return out

        def max_pass(j, diag):
            ss = logits(j, fqh, diag)
            for n in range(2):
                max_ref[n] = jnp.maximum(max_ref[n], ss[n])

        def sum_pass(j, shift, diag):
            ps = [jnp.exp(s) for s in logits(j, shift, diag)]
            v = v_ref[j]
            for n in range(2):
                sum_ref[n] += ps[n]
            for n in range(2):
                acc_ref[n] += _dot(ps[n].astype(BF16), v)

        for d in range(ratio):
            max_pass(ratio * i + d, True)

        def block_matters(j):
            gap = []
            for n in range(2):
                m_run = jnp.max(max_ref[n], axis=-1, keepdims=True)
                f_end = ft_ref[0, jnp.maximum(j, 0)][n:n + 1, tk - 1:tk]
                gap.append(jnp.max(reach[n] - m_run) - jnp.max(f_end))
            return (j >= 0) & (jnp.maximum(gap[0], gap[1]) > -EXP_UNDERFLOW)

        def walk_left(j):
            max_pass(j, False)
            return j - 1

        j_first = lax.while_loop(block_matters, walk_left, ratio * i - 1) + 1
        m = [jnp.max(max_ref[n], axis=-1, keepdims=True) for n in range(2)]
        shift = [fqh[n] - m[n] for n in range(2)]

        def one(j, c):
            sum_pass(j, shift, False)
            return c
        lax.fori_loop(j_first, ratio * i, one, 0)
        for d in range(ratio):
            sum_pass(ratio * i + d, shift, True)
        l = [jnp.sum(sum_ref[n], axis=-1, keepdims=True) for n in range(2)]
        y_ref[...] = jnp.where(first, acc_ref[0] / l[0], acc_ref[1] / l[1])
        lse_ref[...] = jnp.where(first, m[0] + jnp.log(l[0]), m[1] + jnp.log(l[1]))
        first_ref[...] = jnp.ones(first_ref.shape, F32) * j_first.astype(F32)

    tile = pltpu.VMEM((2, tq, tk), F32)
    return pl.pallas_call(
        body,
        grid=(N_PAIRS, sp["nq"]),
        in_specs=[sp["q"], sp["k_rows"], sp["k_rows"], sp["q"], sp["f_t"],
                  pl.BlockSpec((1, 8, LANES), lambda p, i: (p, 0, 0))],
        out_specs=[sp["q"], sp["q"], sp["first"]],
        out_shape=[sp["wide"], sp["wide"], sp["first_out"]],
        scratch_shapes=[sp["acc"], tile, tile],
        compiler_params=_cparams(56),
        name="fox_fwd",
    )(qs, k3, v3, fw, ft4, kmax)


def _fox_bwd(qs, k3, v3, dy, y, lse, fw, ft4, first_block):
    tq, tk = FOX_Q_BLOCK, ATT_BLOCK
    sp = _att_specs(qs.shape[0], tq)
    ratio = tq // tk

    def body(q_ref, k_ref, v_ref, dy_ref, y_ref, lse_ref, fw_ref, ft_ref, first_ref,
             dq_ref, dfq_ref, dkt_ref, dvt_ref, dft_ref, acc_ref):
        i = pl.program_id(1)

        @pl.when(i == 0)
        def _():
            dkt_ref[...] = jnp.zeros_like(dkt_ref)
            dvt_ref[...] = jnp.zeros_like(dvt_ref)
            dft_ref[...] = jnp.zeros_like(dft_ref)

        first = _head_masks(tq)
        qh = _heads_of(q_ref, first)
        dyv = dy_ref[...]
        dyb = dyv.astype(BF16)
        zero = jnp.zeros_like(dyb)
        dyh = [jnp.where(first, dyb, zero), jnp.where(first, zero, dyb)]
        prod = dyv * y_ref[...]
        zf = jnp.zeros_like(prod)
        delta = [jnp.sum(jnp.where(first, prod, zf), axis=-1, keepdims=True),
                 jnp.sum(jnp.where(first, zf, prod), axis=-1, keepdims=True)]
        fqh = _head_cols(fw_ref)
        lseh = _head_cols(lse_ref)
        shift = [fqh[n] - lseh[n] for n in range(2)]
        acc_ref[...] = jnp.zeros_like(acc_ref)

        def block(j, rows, diag):
            mask = _causal_mask(tq, tk, i * tq - j * tk, strict=False) if diag else None
            k, v, fk = k_ref[j], v_ref[j], ft_ref[0, j]
            logits = [_dot_nt(qh[n], k) for n in range(2)]
            dps = [_dot_nt(dyh[n], v) for n in range(2)]
            pbs, dsbs, out = [], [], []
            for n in range(2):
                p = jnp.exp(logits[n] + (shift[n] - fk[n:n + 1, :]))
                if diag:
                    p = jnp.where(mask, p, 0.0)
                ds = p * (dps[n] - delta[n])
                pbs.append(p.astype(BF16))
                dsbs.append(ds.astype(BF16))
                out.append(rows[n] + jnp.sum(ds, axis=-1, keepdims=True))
                dft_ref[0, j, n:n + 1, :] -= _colsum(ds)
            for n in range(2):
                acc_ref[n] += _dot(dsbs[n], k)
            dkt_ref[0, j] += _dot_tn(qh[0], dsbs[0]) + _dot_tn(qh[1], dsbs[1])
            dvt_ref[0, j] += _dot_tn(dyh[0], pbs[0]) + _dot_tn(dyh[1], pbs[1])
            return tuple(out)

        rows = (jnp.zeros((tq, 1), F32),) * 2
        rows = lax.fori_loop(_first_block(first_ref, ratio * i), ratio * i, lambda j, c: block(j, c, False), rows)
        for d in range(ratio):
            rows = block(ratio * i + d, rows, True)
        dq_ref[...] = jnp.where(first, acc_ref[0], acc_ref[1])
        lane = lax.broadcasted_iota(jnp.int32, (tq, 8), 1)
        dfq_ref[0] = jnp.where(lane == 0, rows[0], jnp.where(lane == 1, rows[1], 0.0))

    return pl.pallas_call(
        body,
        grid=(N_PAIRS, sp["nq"]),
        in_specs=[sp["q"], sp["k_rows"], sp["k_rows"], sp["q"], sp["q"], sp["q"], sp["q"], sp["f_t"], sp["first"]],
        out_specs=[sp["q"], pl.BlockSpec((1, tq, 8), lambda p, i: (p, i, 0)), sp["k_t"], sp["k_t"], sp["f_t"]],
        out_shape=[sp["wide"], jax.ShapeDtypeStruct((N_PAIRS, qs.shape[0], 8), F32),
                   sp["k_t_out"], sp["k_t_out"], sp["f_t_out"]],
        scratch_shapes=[sp["acc"]],
        compiler_params=_cparams(56),
        name="fox_bwd",
    )(qs, k3, v3, dy, y, lse, fw, ft4, first_block)


SIGN_BIT = 0x80000000


def _sb_terms(z, mask, diag):
    neg_abs = pltpu.bitcast(pltpu.bitcast(z, jnp.uint32) | jnp.uint32(SIGN_BIT), F32)
    lb = jnp.minimum(z, 0.0) - jnp.log(1.0 + jnp.exp(neg_abs))
    l1m = lb - z
    if diag:
        l1m = jnp.where(mask, l1m, 0.0)
    return lb, l1m


def _dot_split2_stacked(x, m2):
    hi, lo = _split2(x)
    return _dot(jnp.concatenate([hi, lo], axis=1), m2)


def _tri_stacked(kind):
    t = _tri(ATT_BLOCK, kind)
    return jnp.concatenate([t, t], axis=0)


def _sb_fwd(qs, k3, v3):
    tq, tk = SB_Q_BLOCK, ATT_BLOCK
    sp = _att_specs(qs.shape[0], tq)
    ratio = tq // tk
    upper = _tri_stacked("row_gt_col")

    def body(q_ref, k_ref, v_ref, u_ref, y_ref, rtot_ref, first_ref, acc_ref):
        i = pl.program_id(1)
        first = _head_masks(tq)
        qh = _heads_of(q_ref, first)
        u = u_ref[...]
        acc_ref[...] = jnp.zeros_like(acc_ref)

        def block(j, rs, diag):
            mask = _causal_mask(tq, tk, i * tq - j * tk, strict=True) if diag else None
            k, v = k_ref[j], v_ref[j]
            logits = [_dot_nt(qh[n], k) for n in range(2)]
            terms = [_sb_terms(z, mask, diag) for z in logits]
            right = [_dot_split2_stacked(l1m, u) for _, l1m in terms]
            weights = []
            for n in range(2):
                a = jnp.exp(terms[n][0] + right[n] + rs[n])
                if diag:
                    a = jnp.where(mask, a, 0.0)
                weights.append(a.astype(BF16))
            for n in range(2):
                acc_ref[n] += _dot(weights[n], v)
            return tuple(rs[n] + jnp.sum(terms[n][1], axis=-1, keepdims=True) for n in range(2))

        rs = (jnp.zeros((tq, 1), F32),) * 2
        for d in range(ratio):
            rs = block(ratio * i + (ratio - 1 - d), rs, True)

        def block_matters(c):
            j, r0, r1 = c
            return (j >= 0) & (jnp.max(jnp.maximum(r0, r1)) > -EXP_UNDERFLOW)

        def walk_left(c):
            j, r0, r1 = c
            r0, r1 = block(j, (r0, r1), False)
            return j - 1, r0, r1

        j, r0, r1 = lax.while_loop(block_matters, walk_left, (ratio * i - 1, rs[0], rs[1]))
        y_ref[...] = jnp.where(first, acc_ref[0], acc_ref[1])
        rtot_ref[...] = jnp.where(first, r0, r1)
        first_ref[...] = jnp.ones(first_ref.shape, F32) * (j + 1).astype(F32)

    return pl.pallas_call(
        body,
        grid=(N_PAIRS, sp["nq"]),
        in_specs=[sp["q"], sp["k_rows"], sp["k_rows"], pl.BlockSpec((2 * tk, tk), lambda p, i: (0, 0))],
        out_specs=[sp["q"], sp["q"], sp["first"]],
        out_shape=[sp["wide"], sp["wide"], sp["first_out"]],
        scratch_shapes=[sp["acc"]],
        compiler_params=_cparams(56),
        name="sb_fwd",
    )(qs, k3, v3, upper)


def _sb_bwd(qs, k3, v3, dy, rtot, first_block):
    tq, tk = SB_Q_BLOCK, ATT_BLOCK
    sp = _att_specs(qs.shape[0], tq)
    ratio = tq // tk
    lower_in = _tri_stacked("row_le_col")
    lower = _tri(tk, "row_lt_col")

    def body(q_ref, k_ref, v_ref, dy_ref, rtot_ref, first_ref, li_ref, l_ref, dq_ref, dkt_ref, dvt_ref, acc_ref):
        i = pl.program_id(1)

        @pl.when(i == 0)
        def _():
            dkt_ref[...] = jnp.zeros_like(dkt_ref)
            dvt_ref[...] = jnp.zeros_like(dvt_ref)

        first = _head_masks(tq)
        qh = _heads_of(q_ref, first)
        dyb = dy_ref[...].astype(BF16)
        zero = jnp.zeros_like(dyb)
        dyh = [jnp.where(first, dyb, zero), jnp.where(first, zero, dyb)]
        rtoth = _head_cols(rtot_ref)
        li = li_ref[...]
        lo_tri = l_ref[...]
        acc_ref[...] = jnp.zeros_like(acc_ref)

        def block(j, carry, diag):
            mask = _causal_mask(tq, tk, i * tq - j * tk, strict=True) if diag else None
            k, v = k_ref[j], v_ref[j]
            logits = [_dot_nt(qh[n], k) for n in range(2)]
            das = [_dot_nt(dyh[n], v) for n in range(2)]
            terms = [_sb_terms(z, mask, diag) for z in logits]
            upto = [_dot_split2_stacked(l1m, li) for _, l1m in terms]
            des, weights = [], []
            for n in range(2):
                a = jnp.exp(terms[n][0] + ((rtoth[n] - carry[2 * n]) - upto[n]))
                if diag:
                    a = jnp.where(mask, a, 0.0)
                des.append(a * das[n])
                weights.append(a.astype(BF16))
            lefts = [_dot(de.astype(BF16), lo_tri) for de in des]
            dzbs, out = [], []
            for n in range(2):
                beta = jnp.exp(terms[n][0])
                dz = des[n] - (des[n] + (carry[2 * n + 1] + lefts[n])) * beta
                if diag:
                    dz = jnp.where(mask, dz, 0.0)
                dzbs.append(dz.astype(BF16))
                out += [carry[2 * n] + jnp.sum(terms[n][1], axis=-1, keepdims=True),
                        carry[2 * n + 1] + jnp.sum(des[n], axis=-1, keepdims=True)]
            for n in range(2):
                acc_ref[n] += _dot(dzbs[n], k)
            dkt_ref[0, j] += _dot_tn(qh[0], dzbs[0]) + _dot_tn(qh[1], dzbs[1])
            dvt_ref[0, j] += _dot_tn(dyh[0], weights[0]) + _dot_tn(dyh[1], weights[1])
            return tuple(out)

        carry = (jnp.zeros((tq, 1), F32),) * 4
        carry = lax.fori_loop(_first_block(first_ref, ratio * i), ratio * i, lambda j, c: block(j, c, False), carry)
        for d in range(ratio):
            carry = block(ratio * i + d, carry, True)
        dq_ref[...] = jnp.where(first, acc_ref[0], acc_ref[1])

    return pl.pallas_call(
        body,
        grid=(N_PAIRS, sp["nq"]),
        in_specs=[sp["q"], sp["k_rows"], sp["k_rows"], sp["q"], sp["q"], sp["first"],
                  pl.BlockSpec((2 * tk, tk), lambda p, i: (0, 0)), pl.BlockSpec((tk, tk), lambda p, i: (0, 0))],
        out_specs=[sp["q"], sp["k_t"], sp["k_t"]],
        out_shape=[sp["wide"], sp["k_t_out"], sp["k_t_out"]],
        scratch_shapes=[sp["acc"]],
        compiler_params=_cparams(56),
        name="sb_bwd",
    )(qs, k3, v3, dy, rtot, first_block, lower_in, lower)


def _merge_fwd(x1, gates, y_fox, y_sb, w_bf, w_bs, w_out, tm=512):
    s_len = x1.shape[0]

    def body(x_ref, g_ref, yf_ref, ys_ref, wbf_ref, wbs_ref, wo_ref, o_ref):
        g = g_ref[...]
        of = _dot(yf_ref[...].astype(BF16), wbf_ref[...])
        os_ = _dot(ys_ref[...].astype(BF16), wbs_ref[...])
        merged = _sigmoid(g[:, 0:D_MODEL]) * of + _sigmoid(g[:, D_MODEL:]) * os_
        o_ref[...] = x_ref[...] + _dot(merged.astype(BF16), wo_ref[...])

    row = lambda i: (i, 0)
    full = lambda i: (0, 0)
    return pl.pallas_call(
        body,
        grid=(s_len // tm,),
        in_specs=[
            pl.BlockSpec((tm, D_MODEL), row),
            pl.BlockSpec((tm, 2 * D_MODEL), row),
            pl.BlockSpec((tm, ATT_W), row),
            pl.BlockSpec((tm, ATT_W), row),
            pl.BlockSpec((ATT_W, D_MODEL), full),
            pl.BlockSpec((ATT_W, D_MODEL), full),
            pl.BlockSpec((D_MODEL, D_MODEL), full),
        ],
        out_specs=pl.BlockSpec((tm, D_MODEL), row),
        out_shape=jax.ShapeDtypeStruct((s_len, D_MODEL), F32),
        compiler_params=_cparams(48),
        name="merge_fwd",
    )(x1, gates, y_fox, y_sb, w_bf, w_bs, w_out)


def _merge_bwd(dx2, gates, y_fox, y_sb, w_bf, w_bs, w_out, tm=512):
    s_len = dx2.shape[0]

    def body(d_ref, g_ref, yf_ref, ys_ref, wbf_ref, wbs_ref, wo_ref,
             dyf_ref, dys_ref, dg_ref, dof_ref, dos_ref, m_ref, dbf_ref):
        dbf = d_ref[...].astype(BF16)
        dbf_ref[...] = dbf
        dm = _dot_nt(dbf, wo_ref[...])
        g = g_ref[...]
        of = _dot(yf_ref[...].astype(BF16), wbf_ref[...])
        os_ = _dot(ys_ref[...].astype(BF16), wbs_ref[...])
        sf = _sigmoid(g[:, 0:D_MODEL])
        ss = _sigmoid(g[:, D_MODEL:])
        m_ref[...] = (sf * of + ss * os_).astype(BF16)
        d_of = (dm * sf).astype(BF16)
        d_os = (dm * ss).astype(BF16)
        dof_ref[...] = d_of
        dos_ref[...] = d_os
        dg_ref[:, 0:D_MODEL] = (dm * of * sf * (1.0 - sf)).astype(BF16)
        dg_ref[:, D_MODEL:] = (dm * os_ * ss * (1.0 - ss)).astype(BF16)
        dyf_ref[...] = _dot_nt(d_of, wbf_ref[...])
        dys_ref[...] = _dot_nt(d_os, wbs_ref[...])

    row = lambda i: (i, 0)
    full = lambda i: (0, 0)
    return pl.pallas_call(
        body,
        grid=(s_len // tm,),
        in_specs=[
            pl.BlockSpec((tm, D_MODEL), row),
            pl.BlockSpec((tm, 2 * D_MODEL), row),
            pl.BlockSpec((tm, ATT_W), row),
            pl.BlockSpec((tm, ATT_W), row),
            pl.BlockSpec((ATT_W, D_MODEL), full),
            pl.BlockSpec((ATT_W, D_MODEL), full),
            pl.BlockSpec((D_MODEL, D_MODEL), full),
        ],
        out_specs=[
            pl.BlockSpec((tm, ATT_W), row), pl.BlockSpec((tm, ATT_W), row),
            pl.BlockSpec((tm, 2 * D_MODEL), row),
            pl.BlockSpec((tm, D_MODEL), row), pl.BlockSpec((tm, D_MODEL), row),
            pl.BlockSpec((tm, D_MODEL), row), pl.BlockSpec((tm, D_MODEL), row),
        ],
        out_shape=[
            jax.ShapeDtypeStruct((s_len, ATT_W), F32), jax.ShapeDtypeStruct((s_len, ATT_W), F32),
            jax.ShapeDtypeStruct((s_len, 2 * D_MODEL), BF16),
            jax.ShapeDtypeStruct((s_len, D_MODEL), BF16), jax.ShapeDtypeStruct((s_len, D_MODEL), BF16),
            jax.ShapeDtypeStruct((s_len, D_MODEL), BF16), jax.ShapeDtypeStruct((s_len, D_MODEL), BF16),
        ],
        compiler_params=_cparams(56),
        name="merge_bwd",
    )(dx2, gates, y_fox, y_sb, w_bf, w_bs, w_out)


def _ple_loss(x3, p, g, w_pg, w_pp, target, tm=512):
    s_len = x3.shape[0]
    inv_d = 1.0 / D_MODEL

    def body(x_ref, p_ref, g_ref, wpg_ref, wpp_ref, t_ref,
             dx_ref, du_ref, dt_ref, hn_ref, dg_ref, loss_ref):
        @pl.when(pl.program_id(0) == 0)
        def _():
            dg_ref[...] = jnp.zeros_like(dg_ref)
            loss_ref[...] = jnp.zeros_like(loss_ref)

        x = x_ref[...]
        xn, r = _rms(x)
        gain = g_ref[...]
        hn = (xn * gain).astype(BF16)
        hn_ref[...] = hn
        sg = _sigmoid(_dot(hn, wpg_ref[...]))
        t = _dot(p_ref[...].astype(BF16), wpp_ref[...])
        err = x + sg * t - t_ref[...]
        sq = jnp.sum(_colsum(err * err), axis=-1, keepdims=True)
        loss_ref[...] += (0.5 * inv_d) * sq
        dy = err * inv_d
        du = (dy * t * sg * (1.0 - sg)).astype(BF16)
        du_ref[...] = du
        dt_ref[...] = (dy * sg).astype(BF16)
        dh = _dot_nt(du, wpg_ref[...])
        dx_ref[...] = dy + _rms_bwd(dh, xn, r, gain)
        dg_ref[0:1, :] += _colsum(dh * xn)

    row = lambda i: (i, 0)
    full = lambda i: (0, 0)
    bf = jax.ShapeDtypeStruct((s_len, D_MODEL), BF16)
    return pl.pallas_call(
        body,
        grid=(s_len // tm,),
        in_specs=[
            pl.BlockSpec((tm, D_MODEL), row),
            pl.BlockSpec((tm, PLE_DIM), row),
            pl.BlockSpec((1, D_MODEL), full),
            pl.BlockSpec((D_MODEL, D_MODEL), full),
            pl.BlockSpec((PLE_DIM, D_MODEL), full),
            pl.BlockSpec((tm, D_MODEL), row),
        ],
        out_specs=[
            pl.BlockSpec((tm, D_MODEL), row), pl.BlockSpec((tm, D_MODEL), row),
            pl.BlockSpec((tm, D_MODEL), row), pl.BlockSpec((tm, D_MODEL), row),
            pl.BlockSpec((8, D_MODEL), full), pl.BlockSpec((8, LANES), full),
        ],
        out_shape=[
            jax.ShapeDtypeStruct((s_len, D_MODEL), F32), bf, bf, bf,
            jax.ShapeDtypeStruct((8, D_MODEL), F32), jax.ShapeDtypeStruct((8, LANES), F32),
        ],
        compiler_params=_cparams(48),
        name="ple_loss",
    )(x3, p, g, w_pg, w_pp, target)


def _qknorm_bwd(fq, fk, dqs, dk, dv, qn, kn, bd, bd_t, tm=256):
    s_len = fq.shape[0]

    def body(fq_ref, fk_ref, dq_ref, dk_ref, dv_ref, qn_ref, kn_ref, bd_ref, bdt_ref,
             dz_ref, dqn_ref, dkn_ref):
        @pl.when(pl.program_id(0) == 0)
        def _():
            dqn_ref[...] = jnp.zeros_like(dqn_ref)
            dkn_ref[...] = jnp.zeros_like(dkn_ref)

        bd_m = bd_ref[...]
        bdt_m = bdt_ref[...]

        def one(x, dy, gain, dgain_ref):
            xn, rw = _head_rms(x, bd_m, bdt_m)
            dgain_ref[0:1, :] += _colsum(dy * xn)
            dxn = dy * gain
            return rw * (dxn - xn * _head_mean(dxn * xn, bd_m, bdt_m))

        dz_ref[:, 0:ATT_W] = one(fq_ref[...], dq_ref[...] * QK_SCALE, qn_ref[...], dqn_ref).astype(BF16)
        dz_ref[:, ATT_W:2 * ATT_W] = one(fk_ref[...], dk_ref[...], kn_ref[...], dkn_ref).astype(BF16)
        dz_ref[:, 2 * ATT_W:] = dv_ref[...].astype(BF16)

    row = lambda i: (i, 0)
    full = lambda i: (0, 0)
    att = pl.BlockSpec((tm, ATT_W), row)
    return pl.pallas_call(
        body,
        grid=(s_len // tm,),
        in_specs=[att, att, att, att, att,
                  pl.BlockSpec((1, ATT_W), full), pl.BlockSpec((1, ATT_W), full),
                  pl.BlockSpec((ATT_W, LANES), full), pl.BlockSpec((LANES, ATT_W), full)],
        out_specs=[pl.BlockSpec((tm, 3 * ATT_W), row), pl.BlockSpec((8, ATT_W), full), pl.BlockSpec((8, ATT_W), full)],
        out_shape=[jax.ShapeDtypeStruct((s_len, 3 * ATT_W), BF16),
                   jax.ShapeDtypeStruct((8, ATT_W), F32), jax.ShapeDtypeStruct((8, ATT_W), F32)],
        name="qknorm_bwd",
    )(fq, fk, dqs, dk, dv, qn, kn, bd, bd_t)


def _inproj_bwd(x1, dx2, g, dzf, dlogf, logf, dzs, dgates, w_fox, w_fl, w_sb, w_gates, tm=256):
    s_len = x1.shape[0]

    def body(x_ref, d_ref, g_ref, dzf_ref, dlf_ref, lf_ref, dzs_ref, dgt_ref, wf_ref, wl_ref, ws_ref, wg_ref,
             dx_ref, h_ref, dfl_ref, dg_ref, db_ref):
        @pl.when(pl.program_id(0) == 0)
        def _():
            dg_ref[...] = jnp.zeros_like(dg_ref)
            db_ref[...] = jnp.zeros_like(db_ref)

        xn, r = _rms(x_ref[...])
        gain = g_ref[...]
        h_ref[...] = (xn * gain).astype(BF16)
        lane = lax.broadcasted_iota(jnp.int32, (tm, LANES), 1)
        dfl = jnp.where(lane < N_HEADS, dlf_ref[...] * (1.0 - jnp.exp(lf_ref[...])), 0.0)
        db_ref[0:1, :] += _colsum(dfl)
        dflb = dfl.astype(BF16)
        dfl_ref[...] = dflb
        dh = (_dot_nt(dzf_ref[...], wf_ref[...]) + _dot_nt(dflb, wl_ref[...])
              + _dot_nt(dzs_ref[...], ws_ref[...]) + _dot_nt(dgt_ref[...], wg_ref[...]))
        dx_ref[...] = d_ref[...] + _rms_bwd(dh, xn, r, gain)
        dg_ref[0:1, :] += _colsum(dh * xn)

    row = lambda i: (i, 0)
    full = lambda i: (0, 0)
    return pl.pallas_call(
        body,
        grid=(s_len // tm,),
        in_specs=[
            pl.BlockSpec((tm, D_MODEL), row),
            pl.BlockSpec((tm, D_MODEL), row),
            pl.BlockSpec((1, D_MODEL), full),
            pl.BlockSpec((tm, 3 * ATT_W), row),
            pl.BlockSpec((tm, LANES), row),
            pl.BlockSpec((tm, LANES), row),
            pl.BlockSpec((tm, 3 * ATT_W), row),
            pl.BlockSpec((tm, 2 * D_MODEL), row),
            pl.BlockSpec((D_MODEL, 3 * ATT_W), full),
            pl.BlockSpec((D_MODEL, LANES), full),
            pl.BlockSpec((D_MODEL, 3 * ATT_W), full),
            pl.BlockSpec((D_MODEL, 2 * D_MODEL), full),
        ],
        out_specs=[
            pl.BlockSpec((tm, D_MODEL), row), pl.BlockSpec((tm, D_MODEL), row), pl.BlockSpec((tm, LANES), row),
            pl.BlockSpec((8, D_MODEL), full), pl.BlockSpec((8, LANES), full),
        ],
        out_shape=[
            jax.ShapeDtypeStruct((s_len, D_MODEL), F32), jax.ShapeDtypeStruct((s_len, D_MODEL), BF16),
            jax.ShapeDtypeStruct((s_len, LANES), BF16),
            jax.ShapeDtypeStruct((8, D_MODEL), F32), jax.ShapeDtypeStruct((8, LANES), F32),
        ],
        compiler_params=_cparams(56),
        name="inproj_bwd",
    )(x1, dx2, g, dzf, dlogf, logf, dzs, dgates, w_fox, w_fl, w_sb, w_gates)


def _split_w_in(w_in):
    o = 3 * ATT_W
    w_fox = w_in[:, 0:o]
    w_fl = jnp.pad(w_in[:, o:o + N_HEADS], ((0, 0), (0, LANES - N_HEADS)))
    w_sb = w_in[:, o + N_HEADS:2 * o + N_HEADS]
    w_gates = w_in[:, 2 * o + N_HEADS:]
    return w_fox, w_fl, w_sb, w_gates


def _local_grads(x, p, target, small, full, pending=None, send_early=None):
    blk = ATT_BLOCK
    bd, bd_t = _head_sum_matrices()
    full = dict(full)
    late = list(pending) if pending else []

    x1, a1, b1, *gathered = _ffn_fwd(x, small["ffn1_norm"], full["ffn1_w_gate"], full["ffn1_w_up"],
                                     full["ffn1_w_down"], gather=[pending[k] for k in late])
    for k, gth in zip(late, gathered):
        full[k] = gth if k in KEPT_AS_SHARDS else _whole(k, gth)
    w_fox, w_fl, w_sb, w_gates = _split_w_in(full["w_in"])
    bias = jnp.pad(small["forget_bias"], ((0, 0), (0, LANES - N_HEADS)))
    qn = jnp.tile(small["q_norm"], (1, N_HEADS))
    kn = jnp.tile(small["k_norm"], (1, N_HEADS))
    fq, fk, f_qs, f_k, f_v, logf, s_qs, s_k, s_v, gates = _inproj_fwd(
        x1, small["mix_norm"], w_fox, w_fl, w_sb, w_gates, bias, qn, kn, bd, bd_t)
    f_cum = _cumsum_rows(logf, reverse=False)
    f8 = f_cum[:, 0:N_HEADS]
    fw = jnp.repeat(f8, HEAD_DIM, axis=1)
    ft4 = _pair_rows_t(f8, blk)
    f_k3, f_v3 = _blocked_rows(f_k, blk), _blocked_rows(f_v, blk)
    y_fox, lse, f_first = _fox_fwd(f_qs, f_k3, f_v3, fw, ft4, _key_norm_bound(f_k))
    s_k3, s_v3 = _blocked_rows(s_k, blk), _blocked_rows(s_v, blk)
    y_sb, s_rtot, s_first = _sb_fwd(s_qs, s_k3, s_v3)
    x2 = _merge_fwd(x1, gates, y_fox, y_sb, full["w_branch_fox"], full["w_branch_sb"], full["w_out"])
    x3, a2, b2 = _ffn_fwd(x2, small["ffn2_norm"], full["ffn2_w_gate"], full["ffn2_w_up"], full["ffn2_w_down"])

    dx3, du_ple, dt_ple, hn_ple, dg_ple, loss_sum = _ple_loss(
        x3, p, small["ple_norm"], full["w_ple_gate"], full["w_ple_proj"], target)
    dx2, u2, da2, db2, h_ffn2, d3_bf, dg_ffn2 = _ffn_bwd(
        x2, dx3, small["ffn2_norm"], a2, b2, full["ffn2_w_gate"], full["ffn2_w_up"], full["ffn2_w_down"])
    dy_fox, dy_sb, dgates, d_of, d_os, merged, d2_bf = _merge_bwd(
        dx2, gates, y_fox, y_sb, full["w_branch_fox"], full["w_branch_sb"], full["w_out"])

    f_dqs, dfq_p, f_dkt4, f_dvt4, dft4 = _fox_bwd(f_qs, f_k3, f_v3, dy_fox, y_fox, lse, fw, ft4, f_first)
    s_dqs, s_dkt4, s_dvt4 = _sb_bwd(s_qs, s_k3, s_v3, dy_sb, s_rtot, s_first)

    dzf, dqn8, dkn8 = _qknorm_bwd(fq, fk, f_dqs, _unblocked_t(f_dkt4), _unblocked_t(f_dvt4), qn, kn, bd, bd_t)
    dzs = jnp.concatenate([s_dqs * QK_SCALE, _unblocked_t(s_dkt4), _unblocked_t(s_dvt4)], axis=1).astype(BF16)
    df8 = _unpair_rows_t(dft4) + dfq_p[:, :, 0:2].transpose(1, 0, 2).reshape(-1, N_HEADS)
    dlogf = _cumsum_rows(jnp.pad(df8, ((0, 0), (0, LANES - N_HEADS))), reverse=True)
    dx1, h_mix, dfl, dg_mix, dbias8 = _inproj_bwd(
        x1, dx2, small["mix_norm"], dzf, dlogf, logf, dzs, dgates, w_fox, w_fl, w_sb, w_gates)

    one = lambda t: t[None]
    gw = {}
    gw["ffn2_w_gate"] = _wgrad(da2, one(h_ffn2), name="wgrad_ffn2_gate")
    gw["ffn2_w_up"] = _wgrad(db2, one(h_ffn2), name="wgrad_ffn2_up")
    gw["ffn2_w_down"] = _wgrad(u2, one(d3_bf), scale=0.5, name="wgrad_ffn2_down")
    g_fox = _wgrad(one(h_mix), one(dzf), name="wgrad_in_fox")[0]
    g_fl = _wgrad(one(h_mix), one(dfl), name="wgrad_in_forget")[0]
    g_sb = _wgrad(one(h_mix), one(dzs), name="wgrad_in_sb")[0]
    g_gt = _wgrad(one(h_mix), one(dgates), name="wgrad_in_gates")[0]
    gw["w_in"] = jnp.concatenate([g_fox, g_fl[:, 0:N_HEADS], g_sb, g_gt], axis=1)
    gw["w_branch_fox"] = _wgrad(one(y_fox), one(d_of), name="wgrad_branch_fox")[0]
    gw["w_branch_sb"] = _wgrad(one(y_sb), one(d_os), name="wgrad_branch_sb")[0]
    gw["w_out"] = _wgrad(one(merged), one(d2_bf), name="wgrad_out")[0]
    gw["w_ple_gate"] = _wgrad(one(hn_ple), one(du_ple), name="wgrad_ple_gate")[0]
    gw["w_ple_proj"] = _wgrad(one(p), one(dt_ple), name="wgrad_ple_proj")[0]

    sent_names, to_send = send_early(gw) if send_early else ([], [])
    grad_x, u1, da1, db1, h_ffn1, d1_bf, dg_ffn1, *landed = _ffn_bwd(
        x, dx1, small["ffn1_norm"], a1, b1, full["ffn1_w_gate"], full["ffn1_w_up"], full["ffn1_w_down"],
        scatter=to_send)
    gw["ffn1_w_gate"] = _wgrad(da1, one(h_ffn1), name="wgrad_ffn1_gate")
    gw["ffn1_w_up"] = _wgrad(db1, one(h_ffn1), name="wgrad_ffn1_up")
    gw["ffn1_w_down"] = _wgrad(u1, one(d1_bf), scale=0.5, name="wgrad_ffn1_down")

    fold = lambda t: jnp.sum(t[0:1].reshape(N_HEADS, HEAD_DIM), axis=0, keepdims=True)
    gs = {
        "ffn1_norm": dg_ffn1[0:1], "mix_norm": dg_mix[0:1], "ffn2_norm": dg_ffn2[0:1], "ple_norm": dg_ple[0:1],
        "forget_bias": dbias8[0:1, 0:N_HEADS], "q_norm": fold(dqn8), "k_norm": fold(dkn8),
    }
    return loss_sum, grad_x, gw, gs, dict(zip(sent_names, landed))


def _position():
    return lax.axis_index("x"), lax.axis_index("y"), lax.axis_index("c")


def _other_chips(x, y):
    return [(1 - x, y), (x, 1 - y), (1 - x, 1 - y)]


ANY = pl.BlockSpec(memory_space=pl.ANY)


def _place_own_shard(w, q):
    rows, cols = w.shape
    tr = _row_block(rows, cols * 4, budget=2 * MIB)

    def body(q_ref, w_ref, o_ref):
        o_ref[0] = w_ref[...].astype(BF16)

    return pl.pallas_call(
        body,
        grid_spec=pltpu.PrefetchScalarGridSpec(
            num_scalar_prefetch=1,
            grid=(rows // tr,),
            in_specs=[pl.BlockSpec((tr, cols), lambda i, q_ref: (i, 0))],
            out_specs=pl.BlockSpec((1, tr, cols), lambda i, q_ref: (q_ref[0], i, 0)),
        ),
        out_shape=jax.ShapeDtypeStruct((N_CHIPS, rows, cols), BF16),
        name="place_own_shard",
    )(q, w)


def _gather_semaphores(n):
    return [pltpu.SemaphoreType.DMA((6 * n,)), pltpu.SemaphoreType.DMA((6 * n,))]


def _gather_steps(bufs, send_sems, recv_sems):
    n = len(bufs)
    x, y, c = _position()
    q = 2 * x + y
    chips = _other_chips(x, y)
    sibling = (x, y, 1 - c)

    def half(a, slot, which):
        r2 = bufs[a].shape[1] // 2
        return bufs[a].at[slot, pl.ds(which * r2, r2), :]

    def copy(a, k, region, to):
        return pltpu.make_async_remote_copy(
            src_ref=region, dst_ref=region, send_sem=send_sems.at[6 * a + k], recv_sem=recv_sems.at[6 * a + k],
            device_id=to, device_id_type=MESH)

    def to_chip(a, k):
        tx, ty = chips[k]
        return copy(a, k, half(a, q, c), (tx, ty, c))

    def to_sibling(a, k):
        tx, ty = chips[k]
        return copy(a, 3 + k, half(a, 2 * tx + ty, c), sibling)

    def start():
        for a in range(n):
            for k in range(3):
                to_chip(a, k).start()

    def finish():
        for a in range(n):
            for k, (tx, ty) in enumerate(chips):
                copy(a, k, half(a, 2 * tx + ty, c), (tx, ty, c)).wait_recv()
                to_sibling(a, k).start()
        for a in range(n):
            for k, (tx, ty) in enumerate(chips):
                copy(a, 3 + k, half(a, 2 * tx + ty, 1 - c), sibling).wait_recv()
        for a in range(n):
            for k in range(3):
                to_chip(a, k).wait_send()
                to_sibling(a, k).wait_send()

    return start, finish


def _allgather_weights(slots):
    n = len(slots)

    def body(*refs):
        start, finish = _gather_steps(refs[n:2 * n], *refs[2 * n:])
        start()
        finish()

    return pl.pallas_call(
        body,
        in_specs=[ANY] * n,
        out_specs=[ANY] * n,
        out_shape=[jax.ShapeDtypeStruct(s.shape, s.dtype) for s in slots],
        input_output_aliases={a: a for a in range(n)},
        scratch_shapes=_gather_semaphores(n),
        name="allgather_weights",
    )(*slots)


def _exchange_pair_halves(grads):
    n = len(grads)

    def body(*refs):
        ins, outs = refs[0:n], refs[n:2 * n]
        send_sems, recv_sems = refs[2 * n:]
        x, y, c = _position()
        copies = []
        for a in range(n):
            r2 = grads[a].shape[1] // 2
            cp = pltpu.make_async_remote_copy(
                src_ref=ins[a].at[:, pl.ds((1 - c) * r2, r2), :], dst_ref=outs[a],
                send_sem=send_sems.at[a], recv_sem=recv_sems.at[a], device_id=(x, y, 1 - c), device_id_type=MESH)
            cp.start()
            copies.append(cp)
        for cp in copies:
            cp.wait()

    return pl.pallas_call(
        body,
        in_specs=[ANY] * n,
        out_specs=[ANY] * n,
        out_shape=[jax.ShapeDtypeStruct((N_CHIPS, g.shape[1] // 2, g.shape[2]), g.dtype) for g in grads],
        scratch_shapes=[pltpu.SemaphoreType.DMA((n,)), pltpu.SemaphoreType.DMA((n,))],
        name="rs_pair_exchange",
    )(*grads)


def _scatter_semaphores(n):
    return [pltpu.SemaphoreType.DMA((3 * n,)), pltpu.SemaphoreType.DMA((3 * n,)), pltpu.SemaphoreType.DMA((n,))]


def _scatter_steps(ins, outs, send_sems, recv_sems, local_sems):
    n = len(ins)
    x, y, c = _position()
    q = 2 * x + y
    chips = _other_chips(x, y)

    def own(a):
        return pltpu.make_async_copy(ins[a].at[q], outs[a].at[q], local_sems.at[a])

    def to_chip(a, k):
        tx, ty = chips[k]
        return pltpu.make_async_remote_copy(
            src_ref=ins[a].at[2 * tx + ty], dst_ref=outs[a].at[q],
            send_sem=send_sems.at[3 * a + k], recv_sem=recv_sems.at[3 * a + k],
            device_id=(tx, ty, c), device_id_type=MESH)

    def start():
        for a in range(n):
            own(a).start()
            for k in range(3):
                to_chip(a, k).start()

    def finish():
        for a in range(n):
            own(a).wait()
            for k in range(3):
                to_chip(a, k).wait()

    return start, finish


def _scatter_to_owner_chips(pairs):
    n = len(pairs)

    def body(*refs):
        start, finish = _scatter_steps(refs[0:n], refs[n:2 * n], *refs[2 * n:])
        start()
        finish()

    return pl.pallas_call(
        body,
        in_specs=[ANY] * n,
        out_specs=[ANY] * n,
        out_shape=[jax.ShapeDtypeStruct(p.shape, p.dtype) for p in pairs],
        scratch_shapes=_scatter_semaphores(n),
        name="rs_scatter",
    )(*pairs)


def _join_halves(shards):
    n = len(shards)

    def body(*refs):
        bufs = refs[n:2 * n]
        send_sems, recv_sems = refs[2 * n:]
        x, y, c = _position()
        started = []
        for a in range(n):
            r2 = shards[a].shape[0] // 2
            mine = bufs[a].at[pl.ds(c * r2, r2), :]
            cp = pltpu.make_async_remote_copy(
                src_ref=mine, dst_ref=mine, send_sem=send_sems.at[a], recv_sem=recv_sems.at[a],
                device_id=(x, y, 1 - c), device_id_type=MESH)
            cp.start()
            started.append(cp)
        for cp in started:
            cp.wait()

    return pl.pallas_call(
        body,
        in_specs=[ANY] * n,
        out_specs=[ANY] * n,
        out_shape=[jax.ShapeDtypeStruct(t.shape, t.dtype) for t in shards],
        input_output_aliases={a: a for a in range(n)},
        scratch_shapes=[pltpu.SemaphoreType.DMA((n,)), pltpu.SemaphoreType.DMA((n,))],
        name="rs_join_halves",
    )(*shards)


def _add_pair(g, got, c):
    _, r2, cols = got.shape

    def body(c_ref, g_ref, got_ref, o_ref):
        o_ref[...] = (g_ref[...].astype(F32) + got_ref[...].astype(F32)).astype(BF16)

    spec = pl.BlockSpec((1, r2, cols), lambda s, c_ref: (s, 0, 0))
    return pl.pallas_call(
        body,
        grid_spec=pltpu.PrefetchScalarGridSpec(
            num_scalar_prefetch=1,
            grid=(N_CHIPS,),
            in_specs=[pl.BlockSpec((1, r2, cols), lambda s, c_ref: (s, c_ref[0], 0)), spec],
            out_specs=spec,
        ),
        out_shape=jax.ShapeDtypeStruct(got.shape, BF16),
        name="rs_add_pair",
    )(c, g, got)


def _add_chips(parts, c):
    _, r2, cols = parts.shape

    def body(c_ref, p0, p1, p2, p3, o_ref):
        o_ref[...] = ((p0[0].astype(F32) + p1[0].astype(F32)) + p2[0].astype(F32)) + p3[0].astype(F32)

    specs = [pl.BlockSpec((1, r2, cols), functools.partial(lambda i, c_ref, s: (s, 0, 0), s=s))
             for s in range(N_CHIPS)]
    return pl.pallas_call(
        body,
        grid_spec=pltpu.PrefetchScalarGridSpec(
            num_scalar_prefetch=1,
            grid=(1,),
            in_specs=specs,
            out_specs=pl.BlockSpec((r2, cols), lambda i, c_ref: (c_ref[0], 0)),
        ),
        out_shape=jax.ShapeDtypeStruct((2 * r2, cols), F32),
        name="rs_add_chips",
    )(c, parts, parts, parts, parts)


def _allreduce_small(part):
    shape = part.shape

    def body(in_ref, out_ref, gather_ref, send_sems, recv_sems):
        x, y, c = _position()
        me = 4 * x + 2 * y + c
        relations = [(a, b, d) for a in (0, 1) for b in (0, 1) for d in (0, 1)][1:]
        flip = lambda v, f: 1 - v if f else v
        copies = []
        for k, (a, b, d) in enumerate(relations):
            cp = pltpu.make_async_remote_copy(
                src_ref=in_ref, dst_ref=gather_ref.at[me], send_sem=send_sems.at[k], recv_sem=recv_sems.at[k],
                device_id=(flip(x, a), flip(y, b), flip(c, d)), device_id_type=MESH)
            cp.start()
            copies.append(cp)
        gather_ref[me] = in_ref[...]
        for cp in copies:
            cp.wait()
        total = gather_ref[0]
        for dev in range(1, 8):
            total = total + gather_ref[dev]
        out_ref[...] = total

    vmem = pl.BlockSpec(memory_space=pltpu.VMEM)
    return pl.pallas_call(
        body,
        in_specs=[vmem],
        out_specs=vmem,
        out_shape=jax.ShapeDtypeStruct(shape, F32),
        scratch_shapes=[pltpu.VMEM((8,) + shape, F32), pltpu.SemaphoreType.DMA((7,)), pltpu.SemaphoreType.DMA((7,))],
        name="allreduce_small",
    )(part)


def _adamw(w, g, m, v):
    rows, cols = w.shape
    tr = _row_block(rows, cols * 4, budget=MIB)
    c1 = 1.0 / (1.0 - ADAM_B1 ** ADAM_STEP)
    c2 = 1.0 / (1.0 - ADAM_B2 ** ADAM_STEP)

    def body(w_ref, g_ref, m_ref, v_ref, d_ref, nm_ref, nv_ref):
        g_ = g_ref[...]
        nm = ADAM_B1 * m_ref[...] + (1.0 - ADAM_B1) * g_
        nv = ADAM_B2 * v_ref[...] + (1.0 - ADAM_B2) * (g_ * g_)
        nm_ref[...] = nm
        nv_ref[...] = nv
        d_ref[...] = -ADAM_LR * ((nm * c1) / (jnp.sqrt(nv * c2) + ADAM_EPS) + ADAM_WD * w_ref[...])

    spec = pl.BlockSpec((tr, cols), lambda i: (i, 0))
    out = jax.ShapeDtypeStruct((rows, cols), F32)
    return pl.pallas_call(
        body,
        grid=(rows // tr,),
        in_specs=[spec] * 4,
        out_specs=[spec] * 3,
        out_shape=[out] * 3,
        name="adamw",
    )(w, g, m, v)


BIG = ["ffn1_w_gate", "ffn1_w_up", "ffn1_w_down", "w_in", "w_branch_fox", "w_branch_sb", "w_out",
       "ffn2_w_gate", "ffn2_w_up", "ffn2_w_down", "w_ple_gate", "w_ple_proj"]
SMALL = ["ffn1_norm", "mix_norm", "ffn2_norm", "ple_norm", "forget_bias", "q_norm", "k_norm"]
COLUMN_SHARDED = ["w_in", "w_branch_fox", "w_branch_sb", "w_ple_proj"]
KEPT_AS_SHARDS = ["ffn1_w_gate", "ffn1_w_up", "ffn1_w_down", "ffn2_w_gate", "ffn2_w_up", "ffn2_w_down"]
WORKED_TRANSPOSED = ["ffn1_w_gate", "ffn1_w_up", "ffn2_w_gate", "ffn2_w_up"]
NEEDED_FIRST = ["ffn1_w_gate", "ffn1_w_up", "ffn1_w_down"]
ORDER = ["ffn1_norm", "ffn1_w_gate", "ffn1_w_up", "ffn1_w_down", "mix_norm", "w_in", "forget_bias", "q_norm",
         "k_norm", "w_branch_fox", "w_branch_sb", "w_out", "ffn2_norm", "ffn2_w_gate", "ffn2_w_up",
         "ffn2_w_down", "ple_norm", "w_ple_gate", "w_ple_proj"]
SMALL_ROWS = {"ffn1_norm": 0, "mix_norm": 1, "ffn2_norm": 2, "ple_norm": 3}
SMALL_COLS = {"forget_bias": (0, N_HEADS), "q_norm": (N_HEADS, HEAD_DIM), "k_norm": (N_HEADS + HEAD_DIM, HEAD_DIM)}
LOSS_ROW = 5


def _stored(name, a):
    return jnp.swapaxes(a[0], 0, 1) if name in WORKED_TRANSPOSED else a[0]


def _returned(name, t):
    return (jnp.swapaxes(t, 0, 1) if name in WORKED_TRANSPOSED else t)[None]


def _whole(name, gathered):
    if name in COLUMN_SHARDED:
        return jnp.concatenate([gathered[s] for s in range(N_CHIPS)], axis=1)
    return gathered.reshape(-1, gathered.shape[-1])


def _as_shards(name, whole):
    if name in COLUMN_SHARDED:
        k, n = whole.shape
        return whole.reshape(k, N_CHIPS, n // N_CHIPS).transpose(1, 0, 2)
    return whole.reshape(N_CHIPS, whole.shape[0] // N_CHIPS, whole.shape[1])


def _pack_small(values, extra=None):
    rows = [values[k] for k in ("ffn1_norm", "mix_norm", "ffn2_norm", "ple_norm")]
    tail = jnp.concatenate([values["forget_bias"], values["q_norm"], values["k_norm"]], axis=1)
    rows.append(jnp.pad(tail, ((0, 0), (0, D_MODEL - tail.shape[1]))))
    packed = jnp.concatenate(rows + [jnp.zeros((3, D_MODEL), F32)], axis=0)
    if extra is not None:
        packed = packed.at[LOSS_ROW, 0].set(extra)
    return packed


def _unpack_small(packed):
    out = {k: packed[r:r + 1] for k, r in SMALL_ROWS.items()}
    for k, (start, size) in SMALL_COLS.items():
        out[k] = packed[4:5, start:start + size]
    return out


def kernel(x, p, ffn1_norm, ffn1_w_gate, ffn1_w_up, ffn1_w_down, mix_norm, w_in, forget_bias, q_norm, k_norm, w_branch_fox, w_branch_sb, w_out, ffn2_norm, ffn2_w_gate, ffn2_w_up, ffn2_w_down, ple_norm, w_ple_gate, w_ple_proj, loss_target, m_ffn1_norm, m_ffn1_w_gate, m_ffn1_w_up, m_ffn1_w_down, m_mix_norm, m_w_in, m_forget_bias, m_q_norm, m_k_norm, m_w_branch_fox, m_w_branch_sb, m_w_out, m_ffn2_norm, m_ffn2_w_gate, m_ffn2_w_up, m_ffn2_w_down, m_ple_norm, m_w_ple_gate, m_w_ple_proj, v_ffn1_norm, v_ffn1_w_gate, v_ffn1_w_up, v_ffn1_w_down, v_mix_norm, v_w_in, v_forget_bias, v_q_norm, v_k_norm, v_w_branch_fox, v_w_branch_sb, v_w_out, v_ffn2_norm, v_ffn2_w_gate, v_ffn2_w_up, v_ffn2_w_down, v_ple_norm, v_w_ple_gate, v_w_ple_proj):
    args = dict(locals())
    weights = {k: args[k] for k in ORDER}
    moments_m = {k: args["m_" + k] for k in ORDER}
    moments_v = {k: args["v_" + k] for k in ORDER}

    c_idx = lax.axis_index("c").astype(jnp.int32).reshape(1)
    q_idx = (2 * lax.axis_index("x") + lax.axis_index("y")).astype(jnp.int32).reshape(1)
    own = {k: _place_own_shard(_stored(k, weights[k]), q_idx) for k in BIG}
    full = dict(zip(NEEDED_FIRST, _allgather_weights([own[k] for k in NEEDED_FIRST])))
    pending = {k: own[k] for k in BIG if k not in NEEDED_FIRST}
    small = {k: weights[k] for k in SMALL}

    def pair_sums(names, gw):
        slots = [gw[k] if k in KEPT_AS_SHARDS else _as_shards(k, gw[k]) for k in names]
        from_core = _exchange_pair_halves(slots)
        return [_add_pair(g, got, c_idx) for g, got in zip(slots, from_core)]

    late = [k for k in BIG if k not in NEEDED_FIRST]
    loss_sum, grad_x, gw, gs, parts = _local_grads(
        x[0], p[0, 0], loss_target[0], small, full, pending, lambda early: (late, pair_sums(late, early)))

    parts.update(zip(NEEDED_FIRST, _scatter_to_owner_chips(pair_sums(NEEDED_FIRST, gw))))
    grads_big = dict(zip(BIG, _join_halves([_add_chips(parts[k], c_idx) for k in BIG])))
    reduced = _allreduce_small(_pack_small(gs, extra=loss_sum[0, 0]))
    grads_small = _unpack_small(reduced)
    loss = reduced[LOSS_ROW, 0]

    grads, deltas, new_m, new_v = {}, {}, {}, {}
    for k in BIG:
        d, nm, nv = _adamw(_stored(k, weights[k]), grads_big[k], _stored(k, moments_m[k]), _stored(k, moments_v[k]))
        grads[k], deltas[k], new_m[k], new_v[k] = (_returned(k, t) for t in (grads_big[k], d, nm, nv))
    d_s, nm_s, nv_s = _adamw(_pack_small({k: weights[k] for k in SMALL}), reduced,
                             _pack_small({k: moments_m[k] for k in SMALL}),
                             _pack_small({k: moments_v[k] for k in SMALL}))
    for k in SMALL:
        grads[k] = grads_small[k]
    for name, packed in (("d", d_s), ("m", nm_s), ("v", nv_s)):
        target = {"d": deltas, "m": new_m, "v": new_v}[name]
        target.update(_unpack_small(packed))

    return (loss, grad_x[None], *[grads[k] for k in ORDER], *[deltas[k] for k in ORDER],
            *[new_m[k] for k in ORDER], *[new_v[k] for k in ORDER])
```

```python
import functools

import jax
import jax.numpy as jnp
from jax import lax
from jax.experimental import pallas as pl
from jax.experimental.pallas import tpu as pltpu

F32 = jnp.float32
BF16 = jnp.bfloat16

D_MODEL = 1024
D_FF = 2816
N_CHIPS = 4
FF_SHARD = D_FF // N_CHIPS
FFN_CHUNKS = 2
WGRAD_TOKENS = 2048
HEAD_DIM = 64
N_HEADS = 8
ATT_W = N_HEADS * HEAD_DIM
PAIR_W = 2 * HEAD_DIM
N_PAIRS = N_HEADS // 2
PLE_DIM = 256
IN_WIDTH = 3 * ATT_W + N_HEADS + 3 * ATT_W + 2 * D_MODEL
EPS = 1e-6
QK_SCALE = HEAD_DIM ** -0.5
LANES = 128
ATT_BLOCK = 256
FOX_Q_BLOCK = 512
SB_Q_BLOCK = 256
NEG_BIG = -1e30
EXP_UNDERFLOW = 110.0

ADAM_LR = 0.001
ADAM_B1 = 0.9
ADAM_B2 = 0.999
ADAM_EPS = 1e-08
ADAM_WD = 0.01
ADAM_STEP = 10

MESH = pl.DeviceIdType.MESH
MIB = 1024 * 1024


def _cparams(vmem_mib=48):
    return pltpu.CompilerParams(vmem_limit_bytes=vmem_mib * MIB)


def _dot(a, b):
    return jnp.dot(a, b, preferred_element_type=F32)


def _dot_tn(a, b):
    return lax.dot_general(a, b, (((0,), (0,)), ((), ())), preferred_element_type=F32)


def _dot_nt(a, b):
    return lax.dot_general(a, b, (((1,), (1,)), ((), ())), preferred_element_type=F32)


def _sigmoid(x):
    return 1.0 / (1.0 + jnp.exp(-x))


def _split2(x):
    hi = x.astype(BF16)
    lo = (x - hi.astype(F32)).astype(BF16)
    return hi, lo


def _dot_split2(x, m):
    hi, lo = _split2(x)
    return _dot(hi, m) + _dot(lo, m)


def _split3(x):
    hi = x.astype(BF16)
    rest = x - hi.astype(F32)
    mid = rest.astype(BF16)
    lo = (rest - mid.astype(F32)).astype(BF16)
    return hi, mid, lo


def _rms(x):
    r = lax.rsqrt(jnp.mean(x * x, axis=-1, keepdims=True) + EPS)
    return x * r, r


def _rms_bwd(dh, xn, r, g):
    dxn = dh * g
    return r * (dxn - xn * jnp.mean(dxn * xn, axis=-1, keepdims=True))


def _colsum(x):
    return jnp.sum(x, axis=0, keepdims=True)


def _row_block(rows, row_bytes, budget):
    best = None
    for t in range(8, rows + 1, 8):
        if rows % t == 0 and t * row_bytes <= budget:
            best = t
    return best if best is not None else rows


def _ffn_fwd(x, g, wg, wu, wd, gather=(), tm=1024):
    s_len = x.shape[0]
    n = len(gather)
    steps = s_len // tm

    def body(x_ref, g_ref, wg_ref, wu_ref, wd_ref, *rest):
        o_ref, a_ref, b_ref = rest[n:n + 3]
        h_s, acc_s = rest[2 * n + 3:2 * n + 5]
        i = pl.program_id(0)
        j = pl.program_id(1)
        if n:
            start, finish = _gather_steps(rest[n + 3:2 * n + 3], *rest[2 * n + 5:])
            pl.when((i == 0) & (j == 0))(start)

        @pl.when(j == 0)
        def _():
            xn, _ = _rms(x_ref[...])
            h_s[...] = (xn * g_ref[...]).astype(BF16)
            acc_s[...] = jnp.zeros_like(acc_s)

        chunks = [pl.ds(r * (tm // FFN_CHUNKS), tm // FFN_CHUNKS) for r in range(FFN_CHUNKS)]
        pre = [(_dot_nt(h_s[rows, :], wg_ref[0]), _dot_nt(h_s[rows, :], wu_ref[0])) for rows in chunks]
        us = []
        for rows, (a, b) in zip(chunks, pre):
            a_ref[0, rows, :] = a.astype(BF16)
            b_ref[0, rows, :] = b.astype(BF16)
            us.append((a * _sigmoid(a) * b).astype(BF16))
        for rows, u in zip(chunks, us):
            acc_s[rows, :] += _dot(u, wd_ref[0])

        @pl.when(j == N_CHIPS - 1)
        def _():
            o_ref[...] = x_ref[...] + 0.5 * acc_s[...]

        if n:
            pl.when((i == steps - 1) & (j == N_CHIPS - 1))(finish)

    return pl.pallas_call(
        body,
        grid=(steps, N_CHIPS),
        in_specs=[
            pl.BlockSpec((tm, D_MODEL), lambda i, j: (i, 0)),
            pl.BlockSpec((1, D_MODEL), lambda i, j: (0, 0)),
            pl.BlockSpec((1, FF_SHARD, D_MODEL), lambda i, j: (j, 0, 0)),
            pl.BlockSpec((1, FF_SHARD, D_MODEL), lambda i, j: (j, 0, 0)),
            pl.BlockSpec((1, FF_SHARD, D_MODEL), lambda i, j: (j, 0, 0)),
        ] + [ANY] * n,
        out_specs=[pl.BlockSpec((tm, D_MODEL), lambda i, j: (i, 0)),
                   pl.BlockSpec((1, tm, FF_SHARD), lambda i, j: (j, i, 0)),
                   pl.BlockSpec((1, tm, FF_SHARD), lambda i, j: (j, i, 0))] + [ANY] * n,
        out_shape=[jax.ShapeDtypeStruct((s_len, D_MODEL), F32),
                   jax.ShapeDtypeStruct((N_CHIPS, s_len, FF_SHARD), BF16),
                   jax.ShapeDtypeStruct((N_CHIPS, s_len, FF_SHARD), BF16)]
        + [jax.ShapeDtypeStruct(s.shape, s.dtype) for s in gather],
        input_output_aliases={5 + a: 3 + a for a in range(n)},
        scratch_shapes=[pltpu.VMEM((tm, D_MODEL), BF16), pltpu.VMEM((tm, D_MODEL), F32)]
        + (_gather_semaphores(n) if n else []),
        compiler_params=_cparams(56),
        name="ffn_fwd_gathering" if n else "ffn_fwd",
    )(x, g, wg, wu, wd, *gather)


def _ffn_bwd(x, d, g, a_pre, b_pre, wg, wu, wd, scatter=(), tm=512):
    s_len = x.shape[0]
    nb = s_len // tm
    n = len(scatter)

    def body(x_ref, d_ref, g_ref, a_ref, b_ref, wg_ref, wu_ref, wd_ref, *rest):
        dx_ref, u_ref, da_ref, db_ref, h_ref, dbf_ref, dg_ref = rest[n:n + 7]
        dbf_s, dh_s = rest[2 * n + 7:2 * n + 9]
        i = pl.program_id(0)
        j = pl.program_id(1)
        if n:
            start, finish = _scatter_steps(rest[0:n], rest[n + 7:2 * n + 7], *rest[2 * n + 9:])
            pl.when((i == 0) & (j == 0))(start)

        @pl.when(j == 0)
        def _():
            xn, _ = _rms(x_ref[...])
            h_ref[...] = (xn * g_ref[...]).astype(BF16)
            dbf = d_ref[...].astype(BF16)
            dbf_s[...] = dbf
            dbf_ref[...] = dbf
            dh_s[...] = jnp.zeros_like(dh_s)

        @pl.when((i == 0) & (j == 0))
        def _():
            dg_ref[...] = jnp.zeros_like(dg_ref)

        chunks = [pl.ds(r * (tm // FFN_CHUNKS), tm // FFN_CHUNKS) for r in range(FFN_CHUNKS)]
        dus = [0.5 * _dot_nt(dbf_s[rows, :], wd_ref[0]) for rows in chunks]
        das, dbs = [], []
        for rows, du in zip(chunks, dus):
            a = a_ref[0, rows, :].astype(F32)
            b = b_ref[0, rows, :].astype(F32)
            s = _sigmoid(a)
            silu = a * s
            da = (du * b * (s * (1.0 + a * (1.0 - s)))).astype(BF16)
            db = (du * silu).astype(BF16)
            u_ref[0, rows, :] = (silu * b).astype(BF16)
            da_ref[0, rows, :] = da
            db_ref[0, rows, :] = db
            das.append(da)
            dbs.append(db)
        for rows, da, db in zip(chunks, das, dbs):
            dh_s[rows, :] += _dot(da, wg_ref[0]) + _dot(db, wu_ref[0])

        @pl.when(j == N_CHIPS - 1)
        def _():
            xn, r = _rms(x_ref[...])
            dh = dh_s[...]
            dx_ref[...] = d_ref[...] + _rms_bwd(dh, xn, r, g_ref[...])
            dg_ref[0:1, :] += _colsum(dh * xn)

        if n:
            pl.when((i == nb - 1) & (j == N_CHIPS - 1))(finish)

    row = lambda i, j: (i, 0)
    shard = lambda i, j: (j, 0, 0)
    act = lambda i, j: (j, i, 0)
    return pl.pallas_call(
        body,
        grid=(nb, N_CHIPS),
        in_specs=[
            pl.BlockSpec((tm, D_MODEL), row),
            pl.BlockSpec((tm, D_MODEL), row),
            pl.BlockSpec((1, D_MODEL), lambda i, j: (0, 0)),
            pl.BlockSpec((1, tm, FF_SHARD), act),
            pl.BlockSpec((1, tm, FF_SHARD), act),
            pl.BlockSpec((1, FF_SHARD, D_MODEL), shard),
            pl.BlockSpec((1, FF_SHARD, D_MODEL), shard),
            pl.BlockSpec((1, FF_SHARD, D_MODEL), shard),
        ] + [ANY] * n,
        out_specs=[
            pl.BlockSpec((tm, D_MODEL), row),
            pl.BlockSpec((1, tm, FF_SHARD), act),
            pl.BlockSpec((1, tm, FF_SHARD), act),
            pl.BlockSpec((1, tm, FF_SHARD), act),
            pl.BlockSpec((tm, D_MODEL), row),
            pl.BlockSpec((tm, D_MODEL), row),
            pl.BlockSpec((8, D_MODEL), lambda i, j: (0, 0)),
        ] + [ANY] * n,
        out_shape=[
            jax.ShapeDtypeStruct((s_len, D_MODEL), F32),
            jax.ShapeDtypeStruct((N_CHIPS, s_len, FF_SHARD), BF16),
            jax.ShapeDtypeStruct((N_CHIPS, s_len, FF_SHARD), BF16),
            jax.ShapeDtypeStruct((N_CHIPS, s_len, FF_SHARD), BF16),
            jax.ShapeDtypeStruct((s_len, D_MODEL), BF16),
            jax.ShapeDtypeStruct((s_len, D_MODEL), BF16),
            jax.ShapeDtypeStruct((8, D_MODEL), F32),
        ] + [jax.ShapeDtypeStruct(s.shape, s.dtype) for s in scatter],
        scratch_shapes=[
            pltpu.VMEM((tm, D_MODEL), BF16),
            pltpu.VMEM((tm, D_MODEL), F32),
        ] + (_scatter_semaphores(n) if n else []),
        compiler_params=_cparams(56),
        name="ffn_bwd_scattering" if n else "ffn_bwd",
    )(x, d, g, a_pre, b_pre, wg, wu, wd, *scatter)


def _wgrad(a, b, scale=1.0, name="wgrad"):
    na, s_len, k_dim = a.shape
    nb, _, n_dim = b.shape
    n = max(na, nb)
    ts = min(s_len, WGRAD_TOKENS)
    steps = s_len // ts

    def body(a_ref, b_ref, o_ref, acc_s):
        s = pl.program_id(1)

        @pl.when(s == 0)
        def _():
            acc_s[...] = jnp.zeros_like(acc_s)

        acc_s[...] += _dot_tn(a_ref[0].astype(BF16), b_ref[0].astype(BF16))

        @pl.when(s == steps - 1)
        def _():
            o_ref[0] = (acc_s[...] * scale).astype(BF16)

    a_map = (lambda m, s: (m, s, 0)) if na > 1 else (lambda m, s: (0, s, 0))
    b_map = (lambda m, s: (m, s, 0)) if nb > 1 else (lambda m, s: (0, s, 0))
    return pl.pallas_call(
        body,
        grid=(n, steps),
        in_specs=[pl.BlockSpec((1, ts, k_dim), a_map), pl.BlockSpec((1, ts, n_dim), b_map)],
        out_specs=pl.BlockSpec((1, k_dim, n_dim), lambda m, s: (m, 0, 0)),
        out_shape=jax.ShapeDtypeStruct((n, k_dim, n_dim), BF16),
        scratch_shapes=[pltpu.VMEM((k_dim, n_dim), F32)],
        compiler_params=_cparams(56),
        name=name,
    )(a, b)


def _head_sum_matrices():
    lane = lax.broadcasted_iota(jnp.int32, (ATT_W, LANES), 0) // HEAD_DIM
    col = lax.broadcasted_iota(jnp.int32, (ATT_W, LANES), 1)
    bd = (lane == col).astype(BF16)
    return bd, bd.T


def _head_mean(t, bd, bd_t):
    per_head = _dot_split2(t, bd) * (1.0 / HEAD_DIM)
    return _dot_split2(per_head, bd_t)


def _head_rms(x, bd, bd_t):
    per_head = _dot_split2(x * x, bd) * (1.0 / HEAD_DIM)
    r = lax.rsqrt(per_head + EPS)
    rw = _dot_split2(r, bd_t)
    return x * rw, rw


def _log_sigmoid(z):
    return jnp.minimum(z, 0.0) - jnp.log(1.0 + jnp.exp(-jnp.abs(z)))


def _inproj_fwd(x1, g, w_fox, w_fl, w_sb, w_gates, bias, qn, kn, bd, bd_t, tm=256):
    s_len = x1.shape[0]

    def body(x_ref, g_ref, wf_ref, wl_ref, ws_ref, wg_ref, bias_ref, qn_ref, kn_ref, bd_ref, bdt_ref,
             fq_ref, fk_ref, qs_ref, kf_ref, vf_ref, logf_ref, sq_ref, sk_ref, sv_ref, gates_ref):
        xn, _ = _rms(x_ref[...])
        h = (xn * g_ref[...]).astype(BF16)
        zf = _dot(h, wf_ref[...])
        fq = zf[:, 0:ATT_W]
        fk = zf[:, ATT_W:2 * ATT_W]
        fq_ref[...] = fq
        fk_ref[...] = fk
        bd_m = bd_ref[...]
        bdt_m = bdt_ref[...]
        fqn, _ = _head_rms(fq, bd_m, bdt_m)
        fkn, _ = _head_rms(fk, bd_m, bdt_m)
        qs_ref[...] = (fqn * qn_ref[...]).astype(BF16) * QK_SCALE
        kf_ref[...] = (fkn * kn_ref[...]).astype(BF16)
        vf_ref[...] = zf[:, 2 * ATT_W:3 * ATT_W].astype(BF16)
        logf_ref[...] = _log_sigmoid(_dot(h, wl_ref[...]) + bias_ref[...])
        zs = _dot(h, ws_ref[...])
        sq_ref[...] = zs[:, 0:ATT_W].astype(BF16) * QK_SCALE
        sk_ref[...] = zs[:, ATT_W:2 * ATT_W].astype(BF16)
        sv_ref[...] = zs[:, 2 * ATT_W:3 * ATT_W].astype(BF16)
        gates_ref[...] = _dot(h, wg_ref[...])

    row = lambda i: (i, 0)
    full = lambda i: (0, 0)
    att = lambda dt: jax.ShapeDtypeStruct((s_len, ATT_W), dt)
    return pl.pallas_call(
        body,
        grid=(s_len // tm,),
        in_specs=[
            pl.BlockSpec((tm, D_MODEL), row),
            pl.BlockSpec((1, D_MODEL), full),
            pl.BlockSpec((D_MODEL, 3 * ATT_W), full),
            pl.BlockSpec((D_MODEL, LANES), full),
            pl.BlockSpec((D_MODEL, 3 * ATT_W), full),
            pl.BlockSpec((D_MODEL, 2 * D_MODEL), full),
            pl.BlockSpec((1, LANES), full),
            pl.BlockSpec((1, ATT_W), full),
            pl.BlockSpec((1, ATT_W), full),
            pl.BlockSpec((ATT_W, LANES), full),
            pl.BlockSpec((LANES, ATT_W), full),
        ],
        out_specs=[
            pl.BlockSpec((tm, ATT_W), row), pl.BlockSpec((tm, ATT_W), row),
            pl.BlockSpec((tm, ATT_W), row), pl.BlockSpec((tm, ATT_W), row), pl.BlockSpec((tm, ATT_W), row),
            pl.BlockSpec((tm, LANES), row),
            pl.BlockSpec((tm, ATT_W), row), pl.BlockSpec((tm, ATT_W), row), pl.BlockSpec((tm, ATT_W), row),
            pl.BlockSpec((tm, 2 * D_MODEL), row),
        ],
        out_shape=[
            att(F32), att(F32), att(BF16), att(BF16), att(BF16),
            jax.ShapeDtypeStruct((s_len, LANES), F32),
            att(BF16), att(BF16), att(BF16),
            jax.ShapeDtypeStruct((s_len, 2 * D_MODEL), F32),
        ],
        compiler_params=_cparams(56),
        name="inproj_fwd",
    )(x1, g, w_fox, w_fl, w_sb, w_gates, bias, qn, kn, bd, bd_t)


def _tri(n, kind):
    r = lax.broadcasted_iota(jnp.int32, (n, n), 0)
    c = lax.broadcasted_iota(jnp.int32, (n, n), 1)
    m = {"row_ge_col": r >= c, "row_le_col": r <= c, "row_gt_col": r > c, "row_lt_col": r < c}[kind]
    return m.astype(BF16)


def _cumsum_rows(x, reverse, tm=256):
    s_len = x.shape[0]
    nb = s_len // tm
    tri = _tri(tm, "row_le_col" if reverse else "row_ge_col")
    edge = 0 if reverse else tm - 1

    def body(x_ref, tri_ref, o_ref, carry_s):
        @pl.when(pl.program_id(0) == 0)
        def _():
            carry_s[...] = jnp.zeros_like(carry_s)

        hi, mid, lo = _split3(x_ref[...])
        t = tri_ref[...]
        y = _dot(t, hi) + _dot(t, mid) + _dot(t, lo) + carry_s[...]
        o_ref[...] = y
        carry_s[...] = y[edge:edge + 1, :]

    order = (lambda i: (nb - 1 - i, 0)) if reverse else (lambda i: (i, 0))
    return pl.pallas_call(
        body,
        grid=(nb,),
        in_specs=[pl.BlockSpec((tm, LANES), order), pl.BlockSpec((tm, tm), lambda i: (0, 0))],
        out_specs=pl.BlockSpec((tm, LANES), order),
        out_shape=jax.ShapeDtypeStruct((s_len, LANES), F32),
        scratch_shapes=[pltpu.VMEM((1, LANES), F32)],
        name="cumsum_rev" if reverse else "cumsum_fwd",
    )(x, tri)


def _unblocked_t(t4):
    _, nb, _, blk = t4.shape
    return t4.transpose(1, 3, 0, 2).reshape(nb * blk, ATT_W)


def _blocked_rows(t, blk):
    return t.reshape(t.shape[0] // blk, blk, t.shape[1])


def _pair_rows_t(f8, blk):
    nb = f8.shape[0] // blk
    t = f8.reshape(nb, blk, N_PAIRS, 2).transpose(2, 0, 3, 1)
    return jnp.pad(t, ((0, 0), (0, 0), (0, 6), (0, 0)))


def _unpair_rows_t(t4):
    _, nb, _, blk = t4.shape
    return t4[:, :, 0:2, :].transpose(1, 3, 0, 2).reshape(nb * blk, N_HEADS)


def _head_masks(tq):
    lane = lax.broadcasted_iota(jnp.int32, (tq, PAIR_W), 1)
    return lane < HEAD_DIM


def _causal_mask(tq, tk, offset, strict):
    d = lax.broadcasted_iota(jnp.int32, (tq, tk), 1) - lax.broadcasted_iota(jnp.int32, (tq, tk), 0)
    return (d < offset) if strict else (d <= offset)


def _heads_of(ref, first):
    t = ref[...]
    zero = jnp.zeros_like(t)
    return [jnp.where(first, t, zero), jnp.where(first, zero, t)]


def _head_cols(ref):
    t = ref[...]
    return [t[:, 0:1], t[:, HEAD_DIM:HEAD_DIM + 1]]


def _att_specs(s_len, tq):
    tk = ATT_BLOCK
    nq, nk = s_len // tq, s_len // tk
    return dict(
        nq=nq,
        q=pl.BlockSpec((tq, PAIR_W), lambda p, i: (i, p)),
        k_t=pl.BlockSpec((1, nk, PAIR_W, tk), lambda p, i: (p, 0, 0, 0)),
        k_rows=pl.BlockSpec((nk, tk, PAIR_W), lambda p, i: (0, 0, p)),
        f_t=pl.BlockSpec((1, nk, 8, tk), lambda p, i: (p, 0, 0, 0)),
        first=pl.BlockSpec((1, 1, 8, LANES), lambda p, i: (p, i, 0, 0)),
        wide=jax.ShapeDtypeStruct((s_len, ATT_W), F32),
        k_t_out=jax.ShapeDtypeStruct((N_PAIRS, nk, PAIR_W, tk), F32),
        f_t_out=jax.ShapeDtypeStruct((N_PAIRS, nk, 8, tk), F32),
        first_out=jax.ShapeDtypeStruct((N_PAIRS, nq, 8, LANES), F32),
        acc=pltpu.VMEM((2, tq, PAIR_W), F32),
    )


def _first_block(first_ref, limit):
    return jnp.clip(jnp.max(first_ref[0, 0]).astype(jnp.int32), 0, limit)


def _key_norm_bound(k):
    sq = jnp.sum(jnp.square(k.astype(F32)).reshape(k.shape[0], N_HEADS, HEAD_DIM), axis=-1)
    bound = jnp.sqrt(jnp.max(sq, axis=0)).reshape(N_PAIRS, 2)
    return jnp.broadcast_to(jnp.pad(bound, ((0, 0), (0, 6)))[:, :, None], (N_PAIRS, 8, LANES))


def _fox_fwd(qs, k3, v3, fw, ft4, kmax):
    tq, tk = FOX_Q_BLOCK, ATT_BLOCK
    sp = _att_specs(qs.shape[0], tq)
    ratio = tq // tk

    def body(q_ref, k_ref, v_ref, fw_ref, ft_ref, kmax_ref, y_ref, lse_ref, first_ref, acc_ref, max_ref, sum_ref):
        i = pl.program_id(1)
        first = _head_masks(tq)
        qh = _heads_of(q_ref, first)
        fqh = _head_cols(fw_ref)
        acc_ref[...] = jnp.zeros_like(acc_ref)
        sum_ref[...] = jnp.zeros_like(sum_ref)
        max_ref[...] = jnp.full(max_ref.shape, NEG_BIG, F32)
        reach = []
        for n in range(2):
            qf = qh[n].astype(F32)
            reach.append(jnp.sqrt(jnp.sum(qf * qf, axis=-1, keepdims=True)) * kmax_ref[0, n:n + 1, 0:1] + fqh[n])

        def logits(j, shift, diag):
            k, fk = k_ref[j], ft_ref[0, j]
            raw = [_dot_nt(qh[n], k) for n in range(2)]
            out = []
            for n in range(2):
                s = raw[n] + (shift[n] - fk[n:n + 1, :])
                if diag:
                    s = jnp.where(_causal_mask(tq, tk, i * tq - j * tk, strict=False), s, NEG_BIG)
                out.append(s)
            return out

        def max_pass(j, diag):
            ss = logits(j, fqh, diag)
            for n in range(2):
                max_ref[n] = jnp.maximum(max_ref[n], ss[n])

        def sum_pass(j, shift, diag):
            ps = [jnp.exp(s) for s in logits(j, shift, diag)]
            v = v_ref[j]
            for n in range(2):
                sum_ref[n] += ps[n]
            for n in range(2):
                acc_ref[n] += _dot(ps[n].astype(BF16), v)

        for d in range(ratio):
            max_pass(ratio * i + d, True)

        slack = [jnp.max(reach[n] - jnp.max(max_ref[n], axis=-1, keepdims=True)) for n in range(2)]

        def block_matters(j):
            f_end = ft_ref[0, jnp.maximum(j, 0)]
            gap = [slack[n] - jnp.max(f_end[n:n + 1, tk - 1:tk]) for n in range(2)]
            return (j >= 0) & (jnp.maximum(gap[0], gap[1]) > -EXP_UNDERFLOW)

        def walk_left(j):
            max_pass(j, False)
            return j - 1

        j_first = lax.while_loop(block_matters, walk_left, ratio * i - 1) + 1
        m = [jnp.max(max_ref[n], axis=-1, keepdims=True) for n in range(2)]
        shift = [fqh[n] - m[n] for n in range(2)]

        def one(j, c):
            sum_pass(j, shift, False)
            return c
        lax.fori_loop(j_first, ratio * i, one, 0)
        for d in range(ratio):
            sum_pass(ratio * i + d, shift, True)
        l = [jnp.sum(sum_ref[n], axis=-1, keepdims=True) for n in range(2)]
        y_ref[...] = jnp.where(first, acc_ref[0] / l[0], acc_ref[1] / l[1])
        lse_ref[...] = jnp.where(first, m[0] + jnp.log(l[0]), m[1] + jnp.log(l[1]))
        first_ref[...] = jnp.ones(first_ref.shape, F32) * j_first.astype(F32)

    tile = pltpu.VMEM((2, tq, tk), F32)
    return pl.pallas_call(
        body,
        grid=(N_PAIRS, sp["nq"]),
        in_specs=[sp["q"], sp["k_rows"], sp["k_rows"], sp["q"], sp["f_t"],
                  pl.BlockSpec((1, 8, LANES), lambda p, i: (p, 0, 0))],
        out_specs=[sp["q"], sp["q"], sp["first"]],
        out_shape=[sp["wide"], sp["wide"], sp["first_out"]],
        scratch_shapes=[sp["acc"], tile, tile],
        compiler_params=_cparams(56),
        name="fox_fwd",
    )(qs, k3, v3, fw, ft4, kmax)


def _fox_bwd(qs, k3, v3, dy, y, lse, fw, ft4, first_block):
    tq, tk = FOX_Q_BLOCK, ATT_BLOCK
    sp = _att_specs(qs.shape[0], tq)
    ratio = tq // tk

    def body(q_ref, k_ref, v_ref, dy_ref, y_ref, lse_ref, fw_ref, ft_ref, first_ref,
             dq_ref, dfq_ref, dkt_ref, dvt_ref, dft_ref, acc_ref):
        i = pl.program_id(1)

        @pl.when(i == 0)
        def _():
            dkt_ref[...] = jnp.zeros_like(dkt_ref)
            dvt_ref[...] = jnp.zeros_like(dvt_ref)
            dft_ref[...] = jnp.zeros_like(dft_ref)

        first = _head_masks(tq)
        qh = _heads_of(q_ref, first)
        dyv = dy_ref[...]
        dyb = dyv.astype(BF16)
        zero = jnp.zeros_like(dyb)
        dyh = [jnp.where(first, dyb, zero), jnp.where(first, zero, dyb)]
        prod = dyv * y_ref[...]
        zf = jnp.zeros_like(prod)
        delta = [jnp.sum(jnp.where(first, prod, zf), axis=-1, keepdims=True),
                 jnp.sum(jnp.where(first, zf, prod), axis=-1, keepdims=True)]
        fqh = _head_cols(fw_ref)
        lseh = _head_cols(lse_ref)
        shift = [fqh[n] - lseh[n] for n in range(2)]
        acc_ref[...] = jnp.zeros_like(acc_ref)

        def block(j, rows, diag):
            mask = _causal_mask(tq, tk, i * tq - j * tk, strict=False) if diag else None
            k, v, fk = k_ref[j], v_ref[j], ft_ref[0, j]
            logits = [_dot_nt(qh[n], k) for n in range(2)]
            dps = [_dot_nt(dyh[n], v) for n in range(2)]
            pbs, dsbs, out = [], [], []
            for n in range(2):
                p = jnp.exp(logits[n] + (shift[n] - fk[n:n + 1, :]))
                if diag:
                    p = jnp.where(mask, p, 0.0)
                ds = p * (dps[n] - delta[n])
                pbs.append(p.astype(BF16))
                dsbs.append(ds.astype(BF16))
                out.append(rows[n] + jnp.sum(ds, axis=-1, keepdims=True))
                dft_ref[0, j, n:n + 1, :] -= _colsum(ds)
            for n in range(2):
                acc_ref[n] += _dot(dsbs[n], k)
            dkt_ref[0, j] += _dot_tn(qh[0], dsbs[0]) + _dot_tn(qh[1], dsbs[1])
            dvt_ref[0, j] += _dot_tn(dyh[0], pbs[0]) + _dot_tn(dyh[1], pbs[1])
            return tuple(out)

        rows = (jnp.zeros((tq, 1), F32),) * 2
        rows = lax.fori_loop(_first_block(first_ref, ratio * i), ratio * i, lambda j, c: block(j, c, False), rows)
        for d in range(ratio):
            rows = block(ratio * i + d, rows, True)
        dq_ref[...] = jnp.where(first, acc_ref[0], acc_ref[1])
        lane = lax.broadcasted_iota(jnp.int32, (tq, 8), 1)
        dfq_ref[0] = jnp.where(lane == 0, rows[0], jnp.where(lane == 1, rows[1], 0.0))

    return pl.pallas_call(
        body,
        grid=(N_PAIRS, sp["nq"]),
        in_specs=[sp["q"], sp["k_rows"], sp["k_rows"], sp["q"], sp["q"], sp["q"], sp["q"], sp["f_t"], sp["first"]],
        out_specs=[sp["q"], pl.BlockSpec((1, tq, 8), lambda p, i: (p, i, 0)), sp["k_t"], sp["k_t"], sp["f_t"]],
        out_shape=[sp["wide"], jax.ShapeDtypeStruct((N_PAIRS, qs.shape[0], 8), F32),
                   sp["k_t_out"], sp["k_t_out"], sp["f_t_out"]],
        scratch_shapes=[sp["acc"]],
        compiler_params=_cparams(56),
        name="fox_bwd",
    )(qs, k3, v3, dy, y, lse, fw, ft4, first_block)


SIGN_BIT = 0x80000000


def _sb_terms(z, mask, diag):
    neg_abs = pltpu.bitcast(pltpu.bitcast(z, jnp.uint32) | jnp.uint32(SIGN_BIT), F32)
    lb = jnp.minimum(z, 0.0) - jnp.log(1.0 + jnp.exp(neg_abs))
    l1m = lb - z
    if diag:
        l1m = jnp.where(mask, l1m, 0.0)
    return lb, l1m


def _dot_split2_stacked(x, m2):
    hi, lo = _split2(x)
    return _dot(jnp.concatenate([hi, lo], axis=1), m2)


def _tri_stacked(kind):
    t = _tri(ATT_BLOCK, kind)
    return jnp.concatenate([t, t], axis=0)


def _sb_fwd(qs, k3, v3):
    tq, tk = SB_Q_BLOCK, ATT_BLOCK
    sp = _att_specs(qs.shape[0], tq)
    ratio = tq // tk
    upper = _tri_stacked("row_gt_col")

    def body(q_ref, k_ref, v_ref, u_ref, y_ref, rtot_ref, first_ref, acc_ref):
        i = pl.program_id(1)
        first = _head_masks(tq)
        qh = _heads_of(q_ref, first)
        u = u_ref[...]
        acc_ref[...] = jnp.zeros_like(acc_ref)

        def block(j, rs, diag):
            mask = _causal_mask(tq, tk, i * tq - j * tk, strict=True) if diag else None
            k, v = k_ref[j], v_ref[j]
            logits = [_dot_nt(qh[n], k) for n in range(2)]
            terms = [_sb_terms(z, mask, diag) for z in logits]
            right = [_dot_split2_stacked(l1m, u) for _, l1m in terms]
            weights = []
            for n in range(2):
                a = jnp.exp(terms[n][0] + right[n] + rs[n])
                if diag:
                    a = jnp.where(mask, a, 0.0)
                weights.append(a.astype(BF16))
            for n in range(2):
                acc_ref[n] += _dot(weights[n], v)
            return tuple(rs[n] + jnp.sum(terms[n][1], axis=-1, keepdims=True) for n in range(2))

        rs = (jnp.zeros((tq, 1), F32),) * 2
        for d in range(ratio):
            rs = block(ratio * i + (ratio - 1 - d), rs, True)

        def block_matters(c):
            j, r0, r1 = c
            return (j >= 0) & (jnp.max(jnp.maximum(r0, r1)) > -EXP_UNDERFLOW)

        def walk_left(c):
            j, r0, r1 = c
            r0, r1 = block(j, (r0, r1), False)
            return j - 1, r0, r1

        j, r0, r1 = lax.while_loop(block_matters, walk_left, (ratio * i - 1, rs[0], rs[1]))
        y_ref[...] = jnp.where(first, acc_ref[0], acc_ref[1])
        rtot_ref[...] = jnp.where(first, r0, r1)
        first_ref[...] = jnp.ones(first_ref.shape, F32) * (j + 1).astype(F32)

    return pl.pallas_call(
        body,
        grid=(N_PAIRS, sp["nq"]),
        in_specs=[sp["q"], sp["k_rows"], sp["k_rows"], pl.BlockSpec((2 * tk, tk), lambda p, i: (0, 0))],
        out_specs=[sp["q"], sp["q"], sp["first"]],
        out_shape=[sp["wide"], sp["wide"], sp["first_out"]],
        scratch_shapes=[sp["acc"]],
        compiler_params=_cparams(56),
        name="sb_fwd",
    )(qs, k3, v3, upper)


def _sb_bwd(qs, k3, v3, dy, rtot, first_block):
    tq, tk = SB_Q_BLOCK, ATT_BLOCK
    sp = _att_specs(qs.shape[0], tq)
    ratio = tq // tk
    lower_in = _tri_stacked("row_le_col")
    lower = _tri(tk, "row_lt_col")

    def body(q_ref, k_ref, v_ref, dy_ref, rtot_ref, first_ref, li_ref, l_ref, dq_ref, dkt_ref, dvt_ref, acc_ref):
        i = pl.program_id(1)

        @pl.when(i == 0)
        def _():
            dkt_ref[...] = jnp.zeros_like(dkt_ref)
            dvt_ref[...] = jnp.zeros_like(dvt_ref)

        first = _head_masks(tq)
        qh = _heads_of(q_ref, first)
        dyb = dy_ref[...].astype(BF16)
        zero = jnp.zeros_like(dyb)
        dyh = [jnp.where(first, dyb, zero), jnp.where(first, zero, dyb)]
        rtoth = _head_cols(rtot_ref)
        li = li_ref[...]
        lo_tri = l_ref[...]
        acc_ref[...] = jnp.zeros_like(acc_ref)

        def block(j, carry, diag):
            mask = _causal_mask(tq, tk, i * tq - j * tk, strict=True) if diag else None
            k, v = k_ref[j], v_ref[j]
            logits = [_dot_nt(qh[n], k) for n in range(2)]
            das = [_dot_nt(dyh[n], v) for n in range(2)]
            terms = [_sb_terms(z, mask, diag) for z in logits]
            upto = [_dot_split2_stacked(l1m, li) for _, l1m in terms]
            des, weights = [], []
            for n in range(2):
                a = jnp.exp(terms[n][0] + ((rtoth[n] - carry[2 * n]) - upto[n]))
                if diag:
                    a = jnp.where(mask, a, 0.0)
                des.append(a * das[n])
                weights.append(a.astype(BF16))
            lefts = [_dot(de.astype(BF16), lo_tri) for de in des]
            dzbs, out = [], []
            for n in range(2):
                beta = jnp.exp(terms[n][0])
                dz = des[n] - (des[n] + (carry[2 * n + 1] + lefts[n])) * beta
                if diag:
                    dz = jnp.where(mask, dz, 0.0)
                dzbs.append(dz.astype(BF16))
                out += [carry[2 * n] + jnp.sum(terms[n][1], axis=-1, keepdims=True),
                        carry[2 * n + 1] + jnp.sum(des[n], axis=-1, keepdims=True)]
            for n in range(2):
                acc_ref[n] += _dot(dzbs[n], k)
            dkt_ref[0, j] += _dot_tn(qh[0], dzbs[0]) + _dot_tn(qh[1], dzbs[1])
            dvt_ref[0, j] += _dot_tn(dyh[0], weights[0]) + _dot_tn(dyh[1], weights[1])
            return tuple(out)

        carry = (jnp.zeros((tq, 1), F32),) * 4
        carry = lax.fori_loop(_first_block(first_ref, ratio * i), ratio * i, lambda j, c: block(j, c, False), carry)
        for d in range(ratio):
            carry = block(ratio * i + d, carry, True)
        dq_ref[...] = jnp.where(first, acc_ref[0], acc_ref[1])

    return pl.pallas_call(
        body,
        grid=(N_PAIRS, sp["nq"]),
        in_specs=[sp["q"], sp["k_rows"], sp["k_rows"], sp["q"], sp["q"], sp["first"],
                  pl.BlockSpec((2 * tk, tk), lambda p, i: (0, 0)), pl.BlockSpec((tk, tk), lambda p, i: (0, 0))],
        out_specs=[sp["q"], sp["k_t"], sp["k_t"]],
        out_shape=[sp["wide"], sp["k_t_out"], sp["k_t_out"]],
        scratch_shapes=[sp["acc"]],
        compiler_params=_cparams(56),
        name="sb_bwd",
    )(qs, k3, v3, dy, rtot, first_block, lower_in, lower)


def _merge_fwd(x1, gates, y_fox, y_sb, w_bf, w_bs, w_out, tm=512):
    s_len = x1.shape[0]

    def body(x_ref, g_ref, yf_ref, ys_ref, wbf_ref, wbs_ref, wo_ref, o_ref):
        g = g_ref[...]
        of = _dot(yf_ref[...].astype(BF16), wbf_ref[...])
        os_ = _dot(ys_ref[...].astype(BF16), wbs_ref[...])
        merged = _sigmoid(g[:, 0:D_MODEL]) * of + _sigmoid(g[:, D_MODEL:]) * os_
        o_ref[...] = x_ref[...] + _dot(merged.astype(BF16), wo_ref[...])

    row = lambda i: (i, 0)
    full = lambda i: (0, 0)
    return pl.pallas_call(
        body,
        grid=(s_len // tm,),
        in_specs=[
            pl.BlockSpec((tm, D_MODEL), row),
            pl.BlockSpec((tm, 2 * D_MODEL), row),
            pl.BlockSpec((tm, ATT_W), row),
            pl.BlockSpec((tm, ATT_W), row),
            pl.BlockSpec((ATT_W, D_MODEL), full),
            pl.BlockSpec((ATT_W, D_MODEL), full),
            pl.BlockSpec((D_MODEL, D_MODEL), full),
        ],
        out_specs=pl.BlockSpec((tm, D_MODEL), row),
        out_shape=jax.ShapeDtypeStruct((s_len, D_MODEL), F32),
        compiler_params=_cparams(48),
        name="merge_fwd",
    )(x1, gates, y_fox, y_sb, w_bf, w_bs, w_out)


def _merge_bwd(dx2, gates, y_fox, y_sb, w_bf, w_bs, w_out, tm=512):
    s_len = dx2.shape[0]

    def body(d_ref, g_ref, yf_ref, ys_ref, wbf_ref, wbs_ref, wo_ref,
             dyf_ref, dys_ref, dg_ref, dof_ref, dos_ref, m_ref, dbf_ref):
        dbf = d_ref[...].astype(BF16)
        dbf_ref[...] = dbf
        dm = _dot_nt(dbf, wo_ref[...])
        g = g_ref[...]
        of = _dot(yf_ref[...].astype(BF16), wbf_ref[...])
        os_ = _dot(ys_ref[...].astype(BF16), wbs_ref[...])
        sf = _sigmoid(g[:, 0:D_MODEL])
        ss = _sigmoid(g[:, D_MODEL:])
        m_ref[...] = (sf * of + ss * os_).astype(BF16)
        d_of = (dm * sf).astype(BF16)
        d_os = (dm * ss).astype(BF16)
        dof_ref[...] = d_of
        dos_ref[...] = d_os
        dg_ref[:, 0:D_MODEL] = (dm * of * sf * (1.0 - sf)).astype(BF16)
        dg_ref[:, D_MODEL:] = (dm * os_ * ss * (1.0 - ss)).astype(BF16)
        dyf_ref[...] = _dot_nt(d_of, wbf_ref[...])
        dys_ref[...] = _dot_nt(d_os, wbs_ref[...])

    row = lambda i: (i, 0)
    full = lambda i: (0, 0)
    return pl.pallas_call(
        body,
        grid=(s_len // tm,),
        in_specs=[
            pl.BlockSpec((tm, D_MODEL), row),
            pl.BlockSpec((tm, 2 * D_MODEL), row),
            pl.BlockSpec((tm, ATT_W), row),
            pl.BlockSpec((tm, ATT_W), row),
            pl.BlockSpec((ATT_W, D_MODEL), full),
            pl.BlockSpec((ATT_W, D_MODEL), full),
            pl.BlockSpec((D_MODEL, D_MODEL), full),
        ],
        out_specs=[
            pl.BlockSpec((tm, ATT_W), row), pl.BlockSpec((tm, ATT_W), row),
            pl.BlockSpec((tm, 2 * D_MODEL), row),
            pl.BlockSpec((tm, D_MODEL), row), pl.BlockSpec((tm, D_MODEL), row),
            pl.BlockSpec((tm, D_MODEL), row), pl.BlockSpec((tm, D_MODEL), row),
        ],
        out_shape=[
            jax.ShapeDtypeStruct((s_len, ATT_W), F32), jax.ShapeDtypeStruct((s_len, ATT_W), F32),
            jax.ShapeDtypeStruct((s_len, 2 * D_MODEL), BF16),
            jax.ShapeDtypeStruct((s_len, D_MODEL), BF16), jax.ShapeDtypeStruct((s_len, D_MODEL), BF16),
            jax.ShapeDtypeStruct((s_len, D_MODEL), BF16), jax.ShapeDtypeStruct((s_len, D_MODEL), BF16),
        ],
        compiler_params=_cparams(56),
        name="merge_bwd",
    )(dx2, gates, y_fox, y_sb, w_bf, w_bs, w_out)


def _ple_loss(x3, p, g, w_pg, w_pp, target, tm=512):
    s_len = x3.shape[0]
    inv_d = 1.0 / D_MODEL

    def body(x_ref, p_ref, g_ref, wpg_ref, wpp_ref, t_ref,
             dx_ref, du_ref, dt_ref, hn_ref, dg_ref, loss_ref):
        @pl.when(pl.program_id(0) == 0)
        def _():
            dg_ref[...] = jnp.zeros_like(dg_ref)
            loss_ref[...] = jnp.zeros_like(loss_ref)

        x = x_ref[...]
        xn, r = _rms(x)
        gain = g_ref[...]
        hn = (xn * gain).astype(BF16)
        hn_ref[...] = hn
        sg = _sigmoid(_dot(hn, wpg_ref[...]))
        t = _dot(p_ref[...].astype(BF16), wpp_ref[...])
        err = x + sg * t - t_ref[...]
        sq = jnp.sum(_colsum(err * err), axis=-1, keepdims=True)
        loss_ref[...] += (0.5 * inv_d) * sq
        dy = err * inv_d
        du = (dy * t * sg * (1.0 - sg)).astype(BF16)
        du_ref[...] = du
        dt_ref[...] = (dy * sg).astype(BF16)
        dh = _dot_nt(du, wpg_ref[...])
        dx_ref[...] = dy + _rms_bwd(dh, xn, r, gain)
        dg_ref[0:1, :] += _colsum(dh * xn)

    row = lambda i: (i, 0)
    full = lambda i: (0, 0)
    bf = jax.ShapeDtypeStruct((s_len, D_MODEL), BF16)
    return pl.pallas_call(
        body,
        grid=(s_len // tm,),
        in_specs=[
            pl.BlockSpec((tm, D_MODEL), row),
            pl.BlockSpec((tm, PLE_DIM), row),
            pl.BlockSpec((1, D_MODEL), full),
            pl.BlockSpec((D_MODEL, D_MODEL), full),
            pl.BlockSpec((PLE_DIM, D_MODEL), full),
            pl.BlockSpec((tm, D_MODEL), row),
        ],
        out_specs=[
            pl.BlockSpec((tm, D_MODEL), row), pl.BlockSpec((tm, D_MODEL), row),
            pl.BlockSpec((tm, D_MODEL), row), pl.BlockSpec((tm, D_MODEL), row),
            pl.BlockSpec((8, D_MODEL), full), pl.BlockSpec((8, LANES), full),
        ],
        out_shape=[
            jax.ShapeDtypeStruct((s_len, D_MODEL), F32), bf, bf, bf,
            jax.ShapeDtypeStruct((8, D_MODEL), F32), jax.ShapeDtypeStruct((8, LANES), F32),
        ],
        compiler_params=_cparams(48),
        name="ple_loss",
    )(x3, p, g, w_pg, w_pp, target)


def _qknorm_bwd(fq, fk, dqs, dk, dv, qn, kn, bd, bd_t, tm=256):
    s_len = fq.shape[0]

    def body(fq_ref, fk_ref, dq_ref, dk_ref, dv_ref, qn_ref, kn_ref, bd_ref, bdt_ref,
             dz_ref, dqn_ref, dkn_ref):
        @pl.when(pl.program_id(0) == 0)
        def _():
            dqn_ref[...] = jnp.zeros_like(dqn_ref)
            dkn_ref[...] = jnp.zeros_like(dkn_ref)

        bd_m = bd_ref[...]
        bdt_m = bdt_ref[...]

        def one(x, dy, gain, dgain_ref):
            xn, rw = _head_rms(x, bd_m, bdt_m)
            dgain_ref[0:1, :] += _colsum(dy * xn)
            dxn = dy * gain
            return rw * (dxn - xn * _head_mean(dxn * xn, bd_m, bdt_m))

        dz_ref[:, 0:ATT_W] = one(fq_ref[...], dq_ref[...] * QK_SCALE, qn_ref[...], dqn_ref).astype(BF16)
        dz_ref[:, ATT_W:2 * ATT_W] = one(fk_ref[...], dk_ref[...], kn_ref[...], dkn_ref).astype(BF16)
        dz_ref[:, 2 * ATT_W:] = dv_ref[...].astype(BF16)

    row = lambda i: (i, 0)
    full = lambda i: (0, 0)
    att = pl.BlockSpec((tm, ATT_W), row)
    return pl.pallas_call(
        body,
        grid=(s_len // tm,),
        in_specs=[att, att, att, att, att,
                  pl.BlockSpec((1, ATT_W), full), pl.BlockSpec((1, ATT_W), full),
                  pl.BlockSpec((ATT_W, LANES), full), pl.BlockSpec((LANES, ATT_W), full)],
        out_specs=[pl.BlockSpec((tm, 3 * ATT_W), row), pl.BlockSpec((8, ATT_W), full), pl.BlockSpec((8, ATT_W), full)],
        out_shape=[jax.ShapeDtypeStruct((s_len, 3 * ATT_W), BF16),
                   jax.ShapeDtypeStruct((8, ATT_W), F32), jax.ShapeDtypeStruct((8, ATT_W), F32)],
        name="qknorm_bwd",
    )(fq, fk, dqs, dk, dv, qn, kn, bd, bd_t)


def _inproj_bwd(x1, dx2, g, dzf, dlogf, logf, dzs, dgates, w_fox, w_fl, w_sb, w_gates, tm=256):
    s_len = x1.shape[0]

    def body(x_ref, d_ref, g_ref, dzf_ref, dlf_ref, lf_ref, dzs_ref, dgt_ref, wf_ref, wl_ref, ws_ref, wg_ref,
             dx_ref, h_ref, dfl_ref, dg_ref, db_ref):
        @pl.when(pl.program_id(0) == 0)
        def _():
            dg_ref[...] = jnp.zeros_like(dg_ref)
            db_ref[...] = jnp.zeros_like(db_ref)

        xn, r = _rms(x_ref[...])
        gain = g_ref[...]
        h_ref[...] = (xn * gain).astype(BF16)
        lane = lax.broadcasted_iota(jnp.int32, (tm, LANES), 1)
        dfl = jnp.where(lane < N_HEADS, dlf_ref[...] * (1.0 - jnp.exp(lf_ref[...])), 0.0)
        db_ref[0:1, :] += _colsum(dfl)
        dflb = dfl.astype(BF16)
        dfl_ref[...] = dflb
        dh = (_dot_nt(dzf_ref[...], wf_ref[...]) + _dot_nt(dflb, wl_ref[...])
              + _dot_nt(dzs_ref[...], ws_ref[...]) + _dot_nt(dgt_ref[...], wg_ref[...]))
        dx_ref[...] = d_ref[...] + _rms_bwd(dh, xn, r, gain)
        dg_ref[0:1, :] += _colsum(dh * xn)

    row = lambda i: (i, 0)
    full = lambda i: (0, 0)
    return pl.pallas_call(
        body,
        grid=(s_len // tm,),
        in_specs=[
            pl.BlockSpec((tm, D_MODEL), row),
            pl.BlockSpec((tm, D_MODEL), row),
            pl.BlockSpec((1, D_MODEL), full),
            pl.BlockSpec((tm, 3 * ATT_W), row),
            pl.BlockSpec((tm, LANES), row),
            pl.BlockSpec((tm, LANES), row),
            pl.BlockSpec((tm, 3 * ATT_W), row),
            pl.BlockSpec((tm, 2 * D_MODEL), row),
            pl.BlockSpec((D_MODEL, 3 * ATT_W), full),
            pl.BlockSpec((D_MODEL, LANES), full),
            pl.BlockSpec((D_MODEL, 3 * ATT_W), full),
            pl.BlockSpec((D_MODEL, 2 * D_MODEL), full),
        ],
        out_specs=[
            pl.BlockSpec((tm, D_MODEL), row), pl.BlockSpec((tm, D_MODEL), row), pl.BlockSpec((tm, LANES), row),
            pl.BlockSpec((8, D_MODEL), full), pl.BlockSpec((8, LANES), full),
        ],
        out_shape=[
            jax.ShapeDtypeStruct((s_len, D_MODEL), F32), jax.ShapeDtypeStruct((s_len, D_MODEL), BF16),
            jax.ShapeDtypeStruct((s_len, LANES), BF16),
            jax.ShapeDtypeStruct((8, D_MODEL), F32), jax.ShapeDtypeStruct((8, LANES), F32),
        ],
        compiler_params=_cparams(56),
        name="inproj_bwd",
    )(x1, dx2, g, dzf, dlogf, logf, dzs, dgates, w_fox, w_fl, w_sb, w_gates)


def _split_w_in(w_in):
    o = 3 * ATT_W
    w_fox = w_in[:, 0:o]
    w_fl = jnp.pad(w_in[:, o:o + N_HEADS], ((0, 0), (0, LANES - N_HEADS)))
    w_sb = w_in[:, o + N_HEADS:2 * o + N_HEADS]
    w_gates = w_in[:, 2 * o + N_HEADS:]
    return w_fox, w_fl, w_sb, w_gates


def _local_grads(x, p, target, small, full, pending=None, send_early=None):
    blk = ATT_BLOCK
    bd, bd_t = _head_sum_matrices()
    full = dict(full)
    late = list(pending) if pending else []

    x1, a1, b1, *gathered = _ffn_fwd(x, small["ffn1_norm"], full["ffn1_w_gate"], full["ffn1_w_up"],
                                     full["ffn1_w_down"], gather=[pending[k] for k in late])
    for k, gth in zip(late, gathered):
        full[k] = gth if k in KEPT_AS_SHARDS else _whole(k, gth)
    w_fox, w_fl, w_sb, w_gates = _split_w_in(full["w_in"])
    bias = jnp.pad(small["forget_bias"], ((0, 0), (0, LANES - N_HEADS)))
    qn = jnp.tile(small["q_norm"], (1, N_HEADS))
    kn = jnp.tile(small["k_norm"], (1, N_HEADS))
    fq, fk, f_qs, f_k, f_v, logf, s_qs, s_k, s_v, gates = _inproj_fwd(
        x1, small["mix_norm"], w_fox, w_fl, w_sb, w_gates, bias, qn, kn, bd, bd_t)
    f_cum = _cumsum_rows(logf, reverse=False)
    f8 = f_cum[:, 0:N_HEADS]
    fw = jnp.repeat(f8, HEAD_DIM, axis=1)
    ft4 = _pair_rows_t(f8, blk)
    f_k3, f_v3 = _blocked_rows(f_k, blk), _blocked_rows(f_v, blk)
    y_fox, lse, f_first = _fox_fwd(f_qs, f_k3, f_v3, fw, ft4, _key_norm_bound(f_k))
    s_k3, s_v3 = _blocked_rows(s_k, blk), _blocked_rows(s_v, blk)
    y_sb, s_rtot, s_first = _sb_fwd(s_qs, s_k3, s_v3)
    x2 = _merge_fwd(x1, gates, y_fox, y_sb, full["w_branch_fox"], full["w_branch_sb"], full["w_out"])
    x3, a2, b2 = _ffn_fwd(x2, small["ffn2_norm"], full["ffn2_w_gate"], full["ffn2_w_up"], full["ffn2_w_down"])

    dx3, du_ple, dt_ple, hn_ple, dg_ple, loss_sum = _ple_loss(
        x3, p, small["ple_norm"], full["w_ple_gate"], full["w_ple_proj"], target)
    dx2, u2, da2, db2, h_ffn2, d3_bf, dg_ffn2 = _ffn_bwd(
        x2, dx3, small["ffn2_norm"], a2, b2, full["ffn2_w_gate"], full["ffn2_w_up"], full["ffn2_w_down"])
    dy_fox, dy_sb, dgates, d_of, d_os, merged, d2_bf = _merge_bwd(
        dx2, gates, y_fox, y_sb, full["w_branch_fox"], full["w_branch_sb"], full["w_out"])

    f_dqs, dfq_p, f_dkt4, f_dvt4, dft4 = _fox_bwd(f_qs, f_k3, f_v3, dy_fox, y_fox, lse, fw, ft4, f_first)
    s_dqs, s_dkt4, s_dvt4 = _sb_bwd(s_qs, s_k3, s_v3, dy_sb, s_rtot, s_first)

    dzf, dqn8, dkn8 = _qknorm_bwd(fq, fk, f_dqs, _unblocked_t(f_dkt4), _unblocked_t(f_dvt4), qn, kn, bd, bd_t)
    dzs = jnp.concatenate([s_dqs * QK_SCALE, _unblocked_t(s_dkt4), _unblocked_t(s_dvt4)], axis=1).astype(BF16)
    df8 = _unpair_rows_t(dft4) + dfq_p[:, :, 0:2].transpose(1, 0, 2).reshape(-1, N_HEADS)
    dlogf = _cumsum_rows(jnp.pad(df8, ((0, 0), (0, LANES - N_HEADS))), reverse=True)
    dx1, h_mix, dfl, dg_mix, dbias8 = _inproj_bwd(
        x1, dx2, small["mix_norm"], dzf, dlogf, logf, dzs, dgates, w_fox, w_fl, w_sb, w_gates)

    one = lambda t: t[None]
    gw = {}
    gw["ffn2_w_gate"] = _wgrad(da2, one(h_ffn2), name="wgrad_ffn2_gate")
    gw["ffn2_w_up"] = _wgrad(db2, one(h_ffn2), name="wgrad_ffn2_up")
    gw["ffn2_w_down"] = _wgrad(u2, one(d3_bf), scale=0.5, name="wgrad_ffn2_down")
    g_fox = _wgrad(one(h_mix), one(dzf), name="wgrad_in_fox")[0]
    g_fl = _wgrad(one(h_mix), one(dfl), name="wgrad_in_forget")[0]
    g_sb = _wgrad(one(h_mix), one(dzs), name="wgrad_in_sb")[0]
    g_gt = _wgrad(one(h_mix), one(dgates), name="wgrad_in_gates")[0]
    gw["w_in"] = jnp.concatenate([g_fox, g_fl[:, 0:N_HEADS], g_sb, g_gt], axis=1)
    gw["w_branch_fox"] = _wgrad(one(y_fox), one(d_of), name="wgrad_branch_fox")[0]
    gw["w_branch_sb"] = _wgrad(one(y_sb), one(d_os), name="wgrad_branch_sb")[0]
    gw["w_out"] = _wgrad(one(merged), one(d2_bf), name="wgrad_out")[0]
    gw["w_ple_gate"] = _wgrad(one(hn_ple), one(du_ple), name="wgrad_ple_gate")[0]
    gw["w_ple_proj"] = _wgrad(one(p), one(dt_ple), name="wgrad_ple_proj")[0]

    sent_names, to_send = send_early(gw) if send_early else ([], [])
    grad_x, u1, da1, db1, h_ffn1, d1_bf, dg_ffn1, *landed = _ffn_bwd(
        x, dx1, small["ffn1_norm"], a1, b1, full["ffn1_w_gate"], full["ffn1_w_up"], full["ffn1_w_down"],
        scatter=to_send)
    gw["ffn1_w_gate"] = _wgrad(da1, one(h_ffn1), name="wgrad_ffn1_gate")
    gw["ffn1_w_up"] = _wgrad(db1, one(h_ffn1), name="wgrad_ffn1_up")
    gw["ffn1_w_down"] = _wgrad(u1, one(d1_bf), scale=0.5, name="wgrad_ffn1_down")

    fold = lambda t: jnp.sum(t[0:1].reshape(N_HEADS, HEAD_DIM), axis=0, keepdims=True)
    gs = {
        "ffn1_norm": dg_ffn1[0:1], "mix_norm": dg_mix[0:1], "ffn2_norm": dg_ffn2[0:1], "ple_norm": dg_ple[0:1],
        "forget_bias": dbias8[0:1, 0:N_HEADS], "q_norm": fold(dqn8), "k_norm": fold(dkn8),
    }
    return loss_sum, grad_x, gw, gs, dict(zip(sent_names, landed))


def _position():
    return lax.axis_index("x"), lax.axis_index("y"), lax.axis_index("c")


def _other_chips(x, y):
    return [(1 - x, y), (x, 1 - y), (1 - x, 1 - y)]


ANY = pl.BlockSpec(memory_space=pl.ANY)


def _place_own_shard(w, q):
    rows, cols = w.shape
    tr = _row_block(rows, cols * 4, budget=2 * MIB)

    def body(q_ref, w_ref, o_ref):
        o_ref[0] = w_ref[...].astype(BF16)

    return pl.pallas_call(
        body,
        grid_spec=pltpu.PrefetchScalarGridSpec(
            num_scalar_prefetch=1,
            grid=(rows // tr,),
            in_specs=[pl.BlockSpec((tr, cols), lambda i, q_ref: (i, 0))],
            out_specs=pl.BlockSpec((1, tr, cols), lambda i, q_ref: (q_ref[0], i, 0)),
        ),
        out_shape=jax.ShapeDtypeStruct((N_CHIPS, rows, cols), BF16),
        name="place_own_shard",
    )(q, w)


def _gather_semaphores(n):
    return [pltpu.SemaphoreType.DMA((6 * n,)), pltpu.SemaphoreType.DMA((6 * n,))]


def _gather_steps(bufs, send_sems, recv_sems):
    n = len(bufs)
    x, y, c = _position()
    q = 2 * x + y
    chips = _other_chips(x, y)
    sibling = (x, y, 1 - c)

    def half(a, slot, which):
        r2 = bufs[a].shape[1] // 2
        return bufs[a].at[slot, pl.ds(which * r2, r2), :]

    def copy(a, k, region, to):
        return pltpu.make_async_remote_copy(
            src_ref=region, dst_ref=region, send_sem=send_sems.at[6 * a + k], recv_sem=recv_sems.at[6 * a + k],
            device_id=to, device_id_type=MESH)

    def to_chip(a, k):
        tx, ty = chips[k]
        return copy(a, k, half(a, q, c), (tx, ty, c))

    def to_sibling(a, k):
        tx, ty = chips[k]
        return copy(a, 3 + k, half(a, 2 * tx + ty, c), sibling)

    def start():
        for a in range(n):
            for k in range(3):
                to_chip(a, k).start()

    def finish():
        for a in range(n):
            for k, (tx, ty) in enumerate(chips):
                copy(a, k, half(a, 2 * tx + ty, c), (tx, ty, c)).wait_recv()
                to_sibling(a, k).start()
        for a in range(n):
            for k, (tx, ty) in enumerate(chips):
                copy(a, 3 + k, half(a, 2 * tx + ty, 1 - c), sibling).wait_recv()
        for a in range(n):
            for k in range(3):
                to_chip(a, k).wait_send()
                to_sibling(a, k).wait_send()

    return start, finish


def _allgather_weights(slots):
    n = len(slots)

    def body(*refs):
        start, finish = _gather_steps(refs[n:2 * n], *refs[2 * n:])
        start()
        finish()

    return pl.pallas_call(
        body,
        in_specs=[ANY] * n,
        out_specs=[ANY] * n,
        out_shape=[jax.ShapeDtypeStruct(s.shape, s.dtype) for s in slots],
        input_output_aliases={a: a for a in range(n)},
        scratch_shapes=_gather_semaphores(n),
        name="allgather_weights",
    )(*slots)


def _exchange_pair_halves(grads):
    n = len(grads)

    def body(*refs):
        ins, outs = refs[0:n], refs[n:2 * n]
        send_sems, recv_sems = refs[2 * n:]
        x, y, c = _position()
        copies = []
        for a in range(n):
            r2 = grads[a].shape[1] // 2
            cp = pltpu.make_async_remote_copy(
                src_ref=ins[a].at[:, pl.ds((1 - c) * r2, r2), :], dst_ref=outs[a],
                send_sem=send_sems.at[a], recv_sem=recv_sems.at[a], device_id=(x, y, 1 - c), device_id_type=MESH)
            cp.start()
            copies.append(cp)
        for cp in copies:
            cp.wait()

    return pl.pallas_call(
        body,
        in_specs=[ANY] * n,
        out_specs=[ANY] * n,
        out_shape=[jax.ShapeDtypeStruct((N_CHIPS, g.shape[1] // 2, g.shape[2]), g.dtype) for g in grads],
        scratch_shapes=[pltpu.SemaphoreType.DMA((n,)), pltpu.SemaphoreType.DMA((n,))],
        name="rs_pair_exchange",
    )(*grads)


def _scatter_semaphores(n):
    return [pltpu.SemaphoreType.DMA((3 * n,)), pltpu.SemaphoreType.DMA((3 * n,)), pltpu.SemaphoreType.DMA((n,))]


def _scatter_steps(ins, outs, send_sems, recv_sems, local_sems):
    n = len(ins)
    x, y, c = _position()
    q = 2 * x + y
    chips = _other_chips(x, y)

    def own(a):
        return pltpu.make_async_copy(ins[a].at[q], outs[a].at[q], local_sems.at[a])

    def to_chip(a, k):
        tx, ty = chips[k]
        return pltpu.make_async_remote_copy(
            src_ref=ins[a].at[2 * tx + ty], dst_ref=outs[a].at[q],
            send_sem=send_sems.at[3 * a + k], recv_sem=recv_sems.at[3 * a + k],
            device_id=(tx, ty, c), device_id_type=MESH)

    def start():
        for a in range(n):
            own(a).start()
            for k in range(3):
                to_chip(a, k).start()

    def finish():
        for a in range(n):
            own(a).wait()
            for k in range(3):
                to_chip(a, k).wait()

    return start, finish


def _scatter_to_owner_chips(pairs):
    n = len(pairs)

    def body(*refs):
        start, finish = _scatter_steps(refs[0:n], refs[n:2 * n], *refs[2 * n:])
        start()
        finish()

    return pl.pallas_call(
        body,
        in_specs=[ANY] * n,
        out_specs=[ANY] * n,
        out_shape=[jax.ShapeDtypeStruct(p.shape, p.dtype) for p in pairs],
        scratch_shapes=_scatter_semaphores(n),
        name="rs_scatter",
    )(*pairs)


def _join_halves(shards):
    n = len(shards)

    def body(*refs):
        bufs = refs[n:2 * n]
        send_sems, recv_sems = refs[2 * n:]
        x, y, c = _position()
        started = []
        for a in range(n):
            r2 = shards[a].shape[0] // 2
            mine = bufs[a].at[pl.ds(c * r2, r2), :]
            cp = pltpu.make_async_remote_copy(
                src_ref=mine, dst_ref=mine, send_sem=send_sems.at[a], recv_sem=recv_sems.at[a],
                device_id=(x, y, 1 - c), device_id_type=MESH)
            cp.start()
            started.append(cp)
        for cp in started:
            cp.wait()

    return pl.pallas_call(
        body,
        in_specs=[ANY] * n,
        out_specs=[ANY] * n,
        out_shape=[jax.ShapeDtypeStruct(t.shape, t.dtype) for t in shards],
        input_output_aliases={a: a for a in range(n)},
        scratch_shapes=[pltpu.SemaphoreType.DMA((n,)), pltpu.SemaphoreType.DMA((n,))],
        name="rs_join_halves",
    )(*shards)


def _add_pair(g, got, c):
    _, r2, cols = got.shape

    def body(c_ref, g_ref, got_ref, o_ref):
        o_ref[...] = (g_ref[...].astype(F32) + got_ref[...].astype(F32)).astype(BF16)

    spec = pl.BlockSpec((1, r2, cols), lambda s, c_ref: (s, 0, 0))
    return pl.pallas_call(
        body,
        grid_spec=pltpu.PrefetchScalarGridSpec(
            num_scalar_prefetch=1,
            grid=(N_CHIPS,),
            in_specs=[pl.BlockSpec((1, r2, cols), lambda s, c_ref: (s, c_ref[0], 0)), spec],
            out_specs=spec,
        ),
        out_shape=jax.ShapeDtypeStruct(got.shape, BF16),
        name="rs_add_pair",
    )(c, g, got)


def _add_chips(parts, c):
    _, r2, cols = parts.shape

    def body(c_ref, p0, p1, p2, p3, o_ref):
        o_ref[...] = ((p0[0].astype(F32) + p1[0].astype(F32)) + p2[0].astype(F32)) + p3[0].astype(F32)

    specs = [pl.BlockSpec((1, r2, cols), functools.partial(lambda i, c_ref, s: (s, 0, 0), s=s))
             for s in range(N_CHIPS)]
    return pl.pallas_call(
        body,
        grid_spec=pltpu.PrefetchScalarGridSpec(
            num_scalar_prefetch=1,
            grid=(1,),
            in_specs=specs,
            out_specs=pl.BlockSpec((r2, cols), lambda i, c_ref: (c_ref[0], 0)),
        ),
        out_shape=jax.ShapeDtypeStruct((2 * r2, cols), F32),
        name="rs_add_chips",
    )(c, parts, parts, parts, parts)


def _allreduce_small(part):
    shape = part.shape

    def body(in_ref, out_ref, gather_ref, send_sems, recv_sems):
        x, y, c = _position()
        me = 4 * x + 2 * y + c
        relations = [(a, b, d) for a in (0, 1) for b in (0, 1) for d in (0, 1)][1:]
        flip = lambda v, f: 1 - v if f else v
        copies = []
        for k, (a, b, d) in enumerate(relations):
            cp = pltpu.make_async_remote_copy(
                src_ref=in_ref, dst_ref=gather_ref.at[me], send_sem=send_sems.at[k], recv_sem=recv_sems.at[k],
                device_id=(flip(x, a), flip(y, b), flip(c, d)), device_id_type=MESH)
            cp.start()
            copies.append(cp)
        gather_ref[me] = in_ref[...]
        for cp in copies:
            cp.wait()
        total = gather_ref[0]
        for dev in range(1, 8):
            total = total + gather_ref[dev]
        out_ref[...] = total

    vmem = pl.BlockSpec(memory_space=pltpu.VMEM)
    return pl.pallas_call(
        body,
        in_specs=[vmem],
        out_specs=vmem,
        out_shape=jax.ShapeDtypeStruct(shape, F32),
        scratch_shapes=[pltpu.VMEM((8,) + shape, F32), pltpu.SemaphoreType.DMA((7,)), pltpu.SemaphoreType.DMA((7,))],
        name="allreduce_small",
    )(part)


def _adamw(w, g, m, v):
    rows, cols = w.shape
    tr = _row_block(rows, cols * 4, budget=MIB)
    c1 = 1.0 / (1.0 - ADAM_B1 ** ADAM_STEP)
    c2 = 1.0 / (1.0 - ADAM_B2 ** ADAM_STEP)

    def body(w_ref, g_ref, m_ref, v_ref, d_ref, nm_ref, nv_ref):
        g_ = g_ref[...]
        nm = ADAM_B1 * m_ref[...] + (1.0 - ADAM_B1) * g_
        nv = ADAM_B2 * v_ref[...] + (1.0 - ADAM_B2) * (g_ * g_)
        nm_ref[...] = nm
        nv_ref[...] = nv
        d_ref[...] = -ADAM_LR * ((nm * c1) / (jnp.sqrt(nv * c2) + ADAM_EPS) + ADAM_WD * w_ref[...])

    spec = pl.BlockSpec((tr, cols), lambda i: (i, 0))
    out = jax.ShapeDtypeStruct((rows, cols), F32)
    return pl.pallas_call(
        body,
        grid=(rows // tr,),
        in_specs=[spec] * 4,
        out_specs=[spec] * 3,
        out_shape=[out] * 3,
        name="adamw",
    )(w, g, m, v)


BIG = ["ffn1_w_gate", "ffn1_w_up", "ffn1_w_down", "w_in", "w_branch_fox", "w_branch_sb", "w_out",
       "ffn2_w_gate", "ffn2_w_up", "ffn2_w_down", "w_ple_gate", "w_ple_proj"]
SMALL = ["ffn1_norm", "mix_norm", "ffn2_norm", "ple_norm", "forget_bias", "q_norm", "k_norm"]
COLUMN_SHARDED = ["w_in", "w_branch_fox", "w_branch_sb", "w_ple_proj"]
KEPT_AS_SHARDS = ["ffn1_w_gate", "ffn1_w_up", "ffn1_w_down", "ffn2_w_gate", "ffn2_w_up", "ffn2_w_down"]
WORKED_TRANSPOSED = ["ffn1_w_gate", "ffn1_w_up", "ffn2_w_gate", "ffn2_w_up"]
NEEDED_FIRST = ["ffn1_w_gate", "ffn1_w_up", "ffn1_w_down"]
ORDER = ["ffn1_norm", "ffn1_w_gate", "ffn1_w_up", "ffn1_w_down", "mix_norm", "w_in", "forget_bias", "q_norm",
         "k_norm", "w_branch_fox", "w_branch_sb", "w_out", "ffn2_norm", "ffn2_w_gate", "ffn2_w_up",
         "ffn2_w_down", "ple_norm", "w_ple_gate", "w_ple_proj"]
SMALL_ROWS = {"ffn1_norm": 0, "mix_norm": 1, "ffn2_norm": 2, "ple_norm": 3}
SMALL_COLS = {"forget_bias": (0, N_HEADS), "q_norm": (N_HEADS, HEAD_DIM), "k_norm": (N_HEADS + HEAD_DIM, HEAD_DIM)}
LOSS_ROW = 5


def _stored(name, a):
    return jnp.swapaxes(a[0], 0, 1) if name in WORKED_TRANSPOSED else a[0]


def _returned(name, t):
    return (jnp.swapaxes(t, 0, 1) if name in WORKED_TRANSPOSED else t)[None]


def _whole(name, gathered):
    if name in COLUMN_SHARDED:
        return jnp.concatenate([gathered[s] for s in range(N_CHIPS)], axis=1)
    return gathered.reshape(-1, gathered.shape[-1])


def _as_shards(name, whole):
    if name in COLUMN_SHARDED:
        k, n = whole.shape
        return whole.reshape(k, N_CHIPS, n // N_CHIPS).transpose(1, 0, 2)
    return whole.reshape(N_CHIPS, whole.shape[0] // N_CHIPS, whole.shape[1])


def _pack_small(values, extra=None):
    rows = [values[k] for k in ("ffn1_norm", "mix_norm", "ffn2_norm", "ple_norm")]
    tail = jnp.concatenate([values["forget_bias"], values["q_norm"], values["k_norm"]], axis=1)
    rows.append(jnp.pad(tail, ((0, 0), (0, D_MODEL - tail.shape[1]))))
    packed = jnp.concatenate(rows + [jnp.zeros((3, D_MODEL), F32)], axis=0)
    if extra is not None:
        packed = packed.at[LOSS_ROW, 0].set(extra)
    return packed


def _unpack_small(packed):
    out = {k: packed[r:r + 1] for k, r in SMALL_ROWS.items()}
    for k, (start, size) in SMALL_COLS.items():
        out[k] = packed[4:5, start:start + size]
    return out


def kernel(x, p, ffn1_norm, ffn1_w_gate, ffn1_w_up, ffn1_w_down, mix_norm, w_in, forget_bias, q_norm, k_norm, w_branch_fox, w_branch_sb, w_out, ffn2_norm, ffn2_w_gate, ffn2_w_up, ffn2_w_down, ple_norm, w_ple_gate, w_ple_proj, loss_target, m_ffn1_norm, m_ffn1_w_gate, m_ffn1_w_up, m_ffn1_w_down, m_mix_norm, m_w_in, m_forget_bias, m_q_norm, m_k_norm, m_w_branch_fox, m_w_branch_sb, m_w_out, m_ffn2_norm, m_ffn2_w_gate, m_ffn2_w_up, m_ffn2_w_down, m_ple_norm, m_w_ple_gate, m_w_ple_proj, v_ffn1_norm, v_ffn1_w_gate, v_ffn1_w_up, v_ffn1_w_down, v_mix_norm, v_w_in, v_forget_bias, v_q_norm, v_k_norm, v_w_branch_fox, v_w_branch_sb, v_w_out, v_ffn2_norm, v_ffn2_w_gate, v_ffn2_w_up, v_ffn2_w_down, v_ple_norm, v_w_ple_gate, v_w_ple_proj):
    args = dict(locals())
    weights = {k: args[k] for k in ORDER}
    moments_m = {k: args["m_" + k] for k in ORDER}
    moments_v = {k: args["v_" + k] for k in ORDER}

    c_idx = lax.axis_index("c").astype(jnp.int32).reshape(1)
    q_idx = (2 * lax.axis_index("x") + lax.axis_index("y")).astype(jnp.int32).reshape(1)
    own = {k: _place_own_shard(_stored(k, weights[k]), q_idx) for k in BIG}
    full = dict(zip(NEEDED_FIRST, _allgather_weights([own[k] for k in NEEDED_FIRST])))
    pending = {k: own[k] for k in BIG if k not in NEEDED_FIRST}
    small = {k: weights[k] for k in SMALL}

    def pair_sums(names, gw):
        slots = [gw[k] if k in KEPT_AS_SHARDS else _as_shards(k, gw[k]) for k in names]
        from_core = _exchange_pair_halves(slots)
        return [_add_pair(g, got, c_idx) for g, got in zip(slots, from_core)]

    late = [k for k in BIG if k not in NEEDED_FIRST]
    loss_sum, grad_x, gw, gs, parts = _local_grads(
        x[0], p[0, 0], loss_target[0], small, full, pending, lambda early: (late, pair_sums(late, early)))

    parts.update(zip(NEEDED_FIRST, _scatter_to_owner_chips(pair_sums(NEEDED_FIRST, gw))))
    grads_big = dict(zip(BIG, _join_halves([_add_chips(parts[k], c_idx) for k in BIG])))
    reduced = _allreduce_small(_pack_small(gs, extra=loss_sum[0, 0]))
    grads_small = _unpack_small(reduced)
    loss = reduced[LOSS_ROW, 0]

    grads, deltas, new_m, new_v = {}, {}, {}, {}
    for k in BIG:
        d, nm, nv = _adamw(_stored(k, weights[k]), grads_big[k], _stored(k, moments_m[k]), _stored(k, moments_v[k]))
        grads[k], deltas[k], new_m[k], new_v[k] = (_returned(k, t) for t in (grads_big[k], d, nm, nv))
    d_s, nm_s, nv_s = _adamw(_pack_small({k: weights[k] for k in SMALL}), reduced,
                             _pack_small({k: moments_m[k] for k in SMALL}),
                             _pack_small({k: moments_v[k] for k in SMALL}))
    for k in SMALL:
        grads[k] = grads_small[k]
    for name, packed in (("d", d_s), ("m", nm_s), ("v", nv_s)):
        target = {"d": deltas, "m": new_m, "v": new_v}[name]
        target.update(_unpack_small(packed))

    return (loss, grad_x[None], *[grads[k] for k in ORDER], *[deltas[k] for k in ORDER],
            *[new_m[k] for k in ORDER], *[new_v[k] for k in ORDER])
```

```python
import functools

import jax
import jax.numpy as jnp
from jax import lax
from jax.experimental import pallas as pl
from jax.experimental.pallas import tpu as pltpu

F32 = jnp.float32
BF16 = jnp.bfloat16

D_MODEL = 1024
D_FF = 2816
N_CHIPS = 4
FF_SHARD = D_FF // N_CHIPS
FFN_CHUNKS = 2
WGRAD_TOKENS = 4096
WGRAD_VMEM = 30 * 1024 * 1024
HEAD_DIM = 64
N_HEADS = 8
ATT_W = N_HEADS * HEAD_DIM
PAIR_W = 2 * HEAD_DIM
N_PAIRS = N_HEADS // 2
PLE_DIM = 256
IN_WIDTH = 3 * ATT_W + N_HEADS + 3 * ATT_W + 2 * D_MODEL
EPS = 1e-6
QK_SCALE = HEAD_DIM ** -0.5
LANES = 128
ATT_BLOCK = 256
FOX_Q_BLOCK = 512
SB_Q_BLOCK = 256
NEG_BIG = -1e30
EXP_UNDERFLOW = 110.0

ADAM_LR = 0.001
ADAM_B1 = 0.9
ADAM_B2 = 0.999
ADAM_EPS = 1e-08
ADAM_WD = 0.01
ADAM_STEP = 10

MESH = pl.DeviceIdType.MESH
MIB = 1024 * 1024


def _cparams(vmem_mib=48):
    return pltpu.CompilerParams(vmem_limit_bytes=vmem_mib * MIB)


def _dot(a, b):
    return jnp.dot(a, b, preferred_element_type=F32)


def _dot_tn(a, b):
    return lax.dot_general(a, b, (((0,), (0,)), ((), ())), preferred_element_type=F32)


def _dot_nt(a, b):
    return lax.dot_general(a, b, (((1,), (1,)), ((), ())), preferred_element_type=F32)


def _sigmoid(x):
    return 1.0 / (1.0 + jnp.exp(-x))


def _split2(x):
    hi = x.astype(BF16)
    lo = (x - hi.astype(F32)).astype(BF16)
    return hi, lo


def _dot_split2(x, m):
    hi, lo = _split2(x)
    return _dot(hi, m) + _dot(lo, m)


def _split3(x):
    hi = x.astype(BF16)
    rest = x - hi.astype(F32)
    mid = rest.astype(BF16)
    lo = (rest - mid.astype(F32)).astype(BF16)
    return hi, mid, lo


def _rms(x):
    r = lax.rsqrt(jnp.mean(x * x, axis=-1, keepdims=True) + EPS)
    return x * r, r


def _rms_bwd(dh, xn, r, g):
    dxn = dh * g
    return r * (dxn - xn * jnp.mean(dxn * xn, axis=-1, keepdims=True))


def _colsum(x):
    return jnp.sum(x, axis=0, keepdims=True)


def _row_block(rows, row_bytes, budget):
    best = None
    for t in range(8, rows + 1, 8):
        if rows % t == 0 and t * row_bytes <= budget:
            best = t
    return best if best is not None else rows


def _ffn_fwd(x, g, wg, wu, wd, gather=(), tm=1024):
    s_len = x.shape[0]
    n = len(gather)
    steps = s_len // tm

    def body(x_ref, g_ref, wg_ref, wu_ref, wd_ref, *rest):
        o_ref, a_ref, b_ref = rest[n:n + 3]
        h_s, acc_s = rest[2 * n + 3:2 * n + 5]
        i = pl.program_id(0)
        j = pl.program_id(1)
        if n:
            start, finish = _gather_steps(rest[n + 3:2 * n + 3], *rest[2 * n + 5:])
            pl.when((i == 0) & (j == 0))(start)

        @pl.when(j == 0)
        def _():
            xn, _ = _rms(x_ref[...])
            h_s[...] = (xn * g_ref[...]).astype(BF16)
            acc_s[...] = jnp.zeros_like(acc_s)

        chunks = [pl.ds(r * (tm // FFN_CHUNKS), tm // FFN_CHUNKS) for r in range(FFN_CHUNKS)]
        pre = [(_dot_nt(h_s[rows, :], wg_ref[0]), _dot_nt(h_s[rows, :], wu_ref[0])) for rows in chunks]
        us = []
        for rows, (a, b) in zip(chunks, pre):
            a_ref[0, rows, :] = a.astype(BF16)
            b_ref[0, rows, :] = b.astype(BF16)
            us.append((a * _sigmoid(a) * b).astype(BF16))
        for rows, u in zip(chunks, us):
            acc_s[rows, :] += _dot(u, wd_ref[0])

        @pl.when(j == N_CHIPS - 1)
        def _():
            o_ref[...] = x_ref[...] + 0.5 * acc_s[...]

        if n:
            pl.when((i == steps - 1) & (j == N_CHIPS - 1))(finish)

    return pl.pallas_call(
        body,
        grid=(steps, N_CHIPS),
        in_specs=[
            pl.BlockSpec((tm, D_MODEL), lambda i, j: (i, 0)),
            pl.BlockSpec((1, D_MODEL), lambda i, j: (0, 0)),
            pl.BlockSpec((1, FF_SHARD, D_MODEL), lambda i, j: (j, 0, 0)),
            pl.BlockSpec((1, FF_SHARD, D_MODEL), lambda i, j: (j, 0, 0)),
            pl.BlockSpec((1, FF_SHARD, D_MODEL), lambda i, j: (j, 0, 0)),
        ] + [ANY] * n,
        out_specs=[pl.BlockSpec((tm, D_MODEL), lambda i, j: (i, 0)),
                   pl.BlockSpec((1, tm, FF_SHARD), lambda i, j: (j, i, 0)),
                   pl.BlockSpec((1, tm, FF_SHARD), lambda i, j: (j, i, 0))] + [ANY] * n,
        out_shape=[jax.ShapeDtypeStruct((s_len, D_MODEL), F32),
                   jax.ShapeDtypeStruct((N_CHIPS, s_len, FF_SHARD), BF16),
                   jax.ShapeDtypeStruct((N_CHIPS, s_len, FF_SHARD), BF16)]
        + [jax.ShapeDtypeStruct(s.shape, s.dtype) for s in gather],
        input_output_aliases={5 + a: 3 + a for a in range(n)},
        scratch_shapes=[pltpu.VMEM((tm, D_MODEL), BF16), pltpu.VMEM((tm, D_MODEL), F32)]
        + (_gather_semaphores(n) if n else []),
        compiler_params=_cparams(56),
        name="ffn_fwd_gathering" if n else "ffn_fwd",
    )(x, g, wg, wu, wd, *gather)


def _ffn_bwd(x, d, g, a_pre, b_pre, wg, wu, wd, scatter=(), tm=512):
    s_len = x.shape[0]
    nb = s_len // tm
    n = len(scatter)

    def body(x_ref, d_ref, g_ref, a_ref, b_ref, wg_ref, wu_ref, wd_ref, *rest):
        dx_ref, u_ref, da_ref, db_ref, h_ref, dbf_ref, dg_ref = rest[n:n + 7]
        dbf_s, dh_s = rest[2 * n + 7:2 * n + 9]
        i = pl.program_id(0)
        j = pl.program_id(1)
        if n:
            start, finish = _scatter_steps(rest[0:n], rest[n + 7:2 * n + 7], *rest[2 * n + 9:])
            pl.when((i == 0) & (j == 0))(start)

        @pl.when(j == 0)
        def _():
            xn, _ = _rms(x_ref[...])
            h_ref[...] = (xn * g_ref[...]).astype(BF16)
            dbf = d_ref[...].astype(BF16)
            dbf_s[...] = dbf
            dbf_ref[...] = dbf
            dh_s[...] = jnp.zeros_like(dh_s)

        @pl.when((i == 0) & (j == 0))
        def _():
            dg_ref[...] = jnp.zeros_like(dg_ref)

        chunks = [pl.ds(r * (tm // FFN_CHUNKS), tm // FFN_CHUNKS) for r in range(FFN_CHUNKS)]
        dus = [0.5 * _dot_nt(dbf_s[rows, :], wd_ref[0]) for rows in chunks]
        das, dbs = [], []
        for rows, du in zip(chunks, dus):
            a = a_ref[0, rows, :].astype(F32)
            b = b_ref[0, rows, :].astype(F32)
            s = _sigmoid(a)
            silu = a * s
            da = (du * b * (s * (1.0 + a * (1.0 - s)))).astype(BF16)
            db = (du * silu).astype(BF16)
            u_ref[0, rows, :] = (silu * b).astype(BF16)
            da_ref[0, rows, :] = da
            db_ref[0, rows, :] = db
            das.append(da)
            dbs.append(db)
        for rows, da, db in zip(chunks, das, dbs):
            dh_s[rows, :] += _dot(da, wg_ref[0]) + _dot(db, wu_ref[0])

        @pl.when(j == N_CHIPS - 1)
        def _():
            xn, r = _rms(x_ref[...])
            dh = dh_s[...]
            dx_ref[...] = d_ref[...] + _rms_bwd(dh, xn, r, g_ref[...])
            dg_ref[0:1, :] += _colsum(dh * xn)

        if n:
            pl.when((i == nb - 1) & (j == N_CHIPS - 1))(finish)

    row = lambda i, j: (i, 0)
    shard = lambda i, j: (j, 0, 0)
    act = lambda i, j: (j, i, 0)
    return pl.pallas_call(
        body,
        grid=(nb, N_CHIPS),
        in_specs=[
            pl.BlockSpec((tm, D_MODEL), row),
            pl.BlockSpec((tm, D_MODEL), row),
            pl.BlockSpec((1, D_MODEL), lambda i, j: (0, 0)),
            pl.BlockSpec((1, tm, FF_SHARD), act),
            pl.BlockSpec((1, tm, FF_SHARD), act),
            pl.BlockSpec((1, FF_SHARD, D_MODEL), shard),
            pl.BlockSpec((1, FF_SHARD, D_MODEL), shard),
            pl.BlockSpec((1, FF_SHARD, D_MODEL), shard),
        ] + [ANY] * n,
        out_specs=[
            pl.BlockSpec((tm, D_MODEL), row),
            pl.BlockSpec((1, tm, FF_SHARD), act),
            pl.BlockSpec((1, tm, FF_SHARD), act),
            pl.BlockSpec((1, tm, FF_SHARD), act),
            pl.BlockSpec((tm, D_MODEL), row),
            pl.BlockSpec((tm, D_MODEL), row),
            pl.BlockSpec((8, D_MODEL), lambda i, j: (0, 0)),
        ] + [ANY] * n,
        out_shape=[
            jax.ShapeDtypeStruct((s_len, D_MODEL), F32),
            jax.ShapeDtypeStruct((N_CHIPS, s_len, FF_SHARD), BF16),
            jax.ShapeDtypeStruct((N_CHIPS, s_len, FF_SHARD), BF16),
            jax.ShapeDtypeStruct((N_CHIPS, s_len, FF_SHARD), BF16),
            jax.ShapeDtypeStruct((s_len, D_MODEL), BF16),
            jax.ShapeDtypeStruct((s_len, D_MODEL), BF16),
            jax.ShapeDtypeStruct((8, D_MODEL), F32),
        ] + [jax.ShapeDtypeStruct(s.shape, s.dtype) for s in scatter],
        scratch_shapes=[
            pltpu.VMEM((tm, D_MODEL), BF16),
            pltpu.VMEM((tm, D_MODEL), F32),
        ] + (_scatter_semaphores(n) if n else []),
        compiler_params=_cparams(56),
        name="ffn_bwd_scattering" if n else "ffn_bwd",
    )(x, d, g, a_pre, b_pre, wg, wu, wd, *scatter)


def _wgrad(a, b, scale=1.0, name="wgrad"):
    na, s_len, k_dim = a.shape
    nb, _, n_dim = b.shape
    n = max(na, nb)
    ts = WGRAD_TOKENS
    while ts > 512 and (ts > s_len or 2 * ts * (k_dim * a.dtype.itemsize + n_dim * b.dtype.itemsize) > WGRAD_VMEM):
        ts //= 2
    steps = s_len // ts

    def body(a_ref, b_ref, o_ref, acc_s):
        s = pl.program_id(1)

        @pl.when(s == 0)
        def _():
            acc_s[...] = jnp.zeros_like(acc_s)

        acc_s[...] += _dot_tn(a_ref[0].astype(BF16), b_ref[0].astype(BF16))

        @pl.when(s == steps - 1)
        def _():
            o_ref[0] = (acc_s[...] * scale).astype(BF16)

    a_map = (lambda m, s: (m, s, 0)) if na > 1 else (lambda m, s: (0, s, 0))
    b_map = (lambda m, s: (m, s, 0)) if nb > 1 else (lambda m, s: (0, s, 0))
    return pl.pallas_call(
        body,
        grid=(n, steps),
        in_specs=[pl.BlockSpec((1, ts, k_dim), a_map), pl.BlockSpec((1, ts, n_dim), b_map)],
        out_specs=pl.BlockSpec((1, k_dim, n_dim), lambda m, s: (m, 0, 0)),
        out_shape=jax.ShapeDtypeStruct((n, k_dim, n_dim), BF16),
        scratch_shapes=[pltpu.VMEM((k_dim, n_dim), F32)],
        compiler_params=_cparams(56),
        name=name,
    )(a, b)


def _head_sum_matrices():
    lane = lax.broadcasted_iota(jnp.int32, (ATT_W, LANES), 0) // HEAD_DIM
    col = lax.broadcasted_iota(jnp.int32, (ATT_W, LANES), 1)
    bd = (lane == col).astype(BF16)
    return bd, bd.T


def _head_mean(t, bd, bd_t):
    per_head = _dot_split2(t, bd) * (1.0 / HEAD_DIM)
    return _dot_split2(per_head, bd_t)


def _head_rms(x, bd, bd_t):
    per_head = _dot_split2(x * x, bd) * (1.0 / HEAD_DIM)
    r = lax.rsqrt(per_head + EPS)
    rw = _dot_split2(r, bd_t)
    return x * rw, rw


def _log_sigmoid(z):
    return jnp.minimum(z, 0.0) - jnp.log(1.0 + jnp.exp(-jnp.abs(z)))


def _inproj_fwd(x1, g, w_fox, w_fl, w_sb, w_gates, bias, qn, kn, bd, bd_t, tm=512):
    s_len = x1.shape[0]

    def body(x_ref, g_ref, wf_ref, wl_ref, ws_ref, wg_ref, bias_ref, qn_ref, kn_ref, bd_ref, bdt_ref,
             fq_ref, fk_ref, qs_ref, kf_ref, vf_ref, logf_ref, sq_ref, sk_ref, sv_ref, gates_ref):
        xn, _ = _rms(x_ref[...])
        h = (xn * g_ref[...]).astype(BF16)
        zf = _dot(h, wf_ref[...])
        fq = zf[:, 0:ATT_W]
        fk = zf[:, ATT_W:2 * ATT_W]
        fq_ref[...] = fq
        fk_ref[...] = fk
        bd_m = bd_ref[...]
        bdt_m = bdt_ref[...]
        fqn, _ = _head_rms(fq, bd_m, bdt_m)
        fkn, _ = _head_rms(fk, bd_m, bdt_m)
        qs_ref[...] = (fqn * qn_ref[...]).astype(BF16) * QK_SCALE
        kf_ref[...] = (fkn * kn_ref[...]).astype(BF16)
        vf_ref[...] = zf[:, 2 * ATT_W:3 * ATT_W].astype(BF16)
        logf_ref[...] = _log_sigmoid(_dot(h, wl_ref[...]) + bias_ref[...])
        zs = _dot(h, ws_ref[...])
        sq_ref[...] = zs[:, 0:ATT_W].astype(BF16) * QK_SCALE
        sk_ref[...] = zs[:, ATT_W:2 * ATT_W].astype(BF16)
        sv_ref[...] = zs[:, 2 * ATT_W:3 * ATT_W].astype(BF16)
        gates_ref[...] = _dot(h, wg_ref[...])

    row = lambda i: (i, 0)
    full = lambda i: (0, 0)
    att = lambda dt: jax.ShapeDtypeStruct((s_len, ATT_W), dt)
    return pl.pallas_call(
        body,
        grid=(s_len // tm,),
        in_specs=[
            pl.BlockSpec((tm, D_MODEL), row),
            pl.BlockSpec((1, D_MODEL), full),
            pl.BlockSpec((D_MODEL, 3 * ATT_W), full),
            pl.BlockSpec((D_MODEL, LANES), full),
            pl.BlockSpec((D_MODEL, 3 * ATT_W), full),
            pl.BlockSpec((D_MODEL, 2 * D_MODEL), full),
            pl.BlockSpec((1, LANES), full),
            pl.BlockSpec((1, ATT_W), full),
            pl.BlockSpec((1, ATT_W), full),
            pl.BlockSpec((ATT_W, LANES), full),
            pl.BlockSpec((LANES, ATT_W), full),
        ],
        out_specs=[
            pl.BlockSpec((tm, ATT_W), row), pl.BlockSpec((tm, ATT_W), row),
            pl.BlockSpec((tm, ATT_W), row), pl.BlockSpec((tm, ATT_W), row), pl.BlockSpec((tm, ATT_W), row),
            pl.BlockSpec((tm, LANES), row),
            pl.BlockSpec((tm, ATT_W), row), pl.BlockSpec((tm, ATT_W), row), pl.BlockSpec((tm, ATT_W), row),
            pl.BlockSpec((tm, 2 * D_MODEL), row),
        ],
        out_shape=[
            att(F32), att(F32), att(BF16), att(BF16), att(BF16),
            jax.ShapeDtypeStruct((s_len, LANES), F32),
            att(BF16), att(BF16), att(BF16),
            jax.ShapeDtypeStruct((s_len, 2 * D_MODEL), F32),
        ],
        compiler_params=_cparams(56),
        name="inproj_fwd",
    )(x1, g, w_fox, w_fl, w_sb, w_gates, bias, qn, kn, bd, bd_t)


def _tri(n, kind):
    r = lax.broadcasted_iota(jnp.int32, (n, n), 0)
    c = lax.broadcasted_iota(jnp.int32, (n, n), 1)
    m = {"row_ge_col": r >= c, "row_le_col": r <= c, "row_gt_col": r > c, "row_lt_col": r < c}[kind]
    return m.astype(BF16)


def _cumsum_rows(x, reverse, tm=256):
    s_len = x.shape[0]
    nb = s_len // tm
    tri = _tri(tm, "row_le_col" if reverse else "row_ge_col")
    edge = 0 if reverse else tm - 1

    def body(x_ref, tri_ref, o_ref, carry_s):
        @pl.when(pl.program_id(0) == 0)
        def _():
            carry_s[...] = jnp.zeros_like(carry_s)

        hi, mid, lo = _split3(x_ref[...])
        t = tri_ref[...]
        y = _dot(t, hi) + _dot(t, mid) + _dot(t, lo) + carry_s[...]
        o_ref[...] = y
        carry_s[...] = y[edge:edge + 1, :]

    order = (lambda i: (nb - 1 - i, 0)) if reverse else (lambda i: (i, 0))
    return pl.pallas_call(
        body,
        grid=(nb,),
        in_specs=[pl.BlockSpec((tm, LANES), order), pl.BlockSpec((tm, tm), lambda i: (0, 0))],
        out_specs=pl.BlockSpec((tm, LANES), order),
        out_shape=jax.ShapeDtypeStruct((s_len, LANES), F32),
        scratch_shapes=[pltpu.VMEM((1, LANES), F32)],
        name="cumsum_rev" if reverse else "cumsum_fwd",
    )(x, tri)


def _unblocked_t(t4):
    _, nb, _, blk = t4.shape
    return t4.transpose(1, 3, 0, 2).reshape(nb * blk, ATT_W)


def _blocked_rows(t, blk):
    return t.reshape(t.shape[0] // blk, blk, t.shape[1])


def _pair_rows_t(f8, blk):
    nb = f8.shape[0] // blk
    t = f8.reshape(nb, blk, N_PAIRS, 2).transpose(2, 0, 3, 1)
    return jnp.pad(t, ((0, 0), (0, 0), (0, 6), (0, 0)))


def _unpair_rows_t(t4):
    _, nb, _, blk = t4.shape
    return t4[:, :, 0:2, :].transpose(1, 3, 0, 2).reshape(nb * blk, N_HEADS)


def _head_masks(tq):
    lane = lax.broadcasted_iota(jnp.int32, (tq, PAIR_W), 1)
    return lane < HEAD_DIM


def _causal_mask(tq, tk, offset, strict):
    d = lax.broadcasted_iota(jnp.int32, (tq, tk), 1) - lax.broadcasted_iota(jnp.int32, (tq, tk), 0)
    return (d < offset) if strict else (d <= offset)


def _heads_of(ref, first):
    t = ref[...]
    zero = jnp.zeros_like(t)
    return [jnp.where(first, t, zero), jnp.where(first, zero, t)]


def _head_cols(ref):
    t = ref[...]
    return [t[:, 0:1], t[:, HEAD_DIM:HEAD_DIM + 1]]


def _att_specs(s_len, tq):
    tk = ATT_BLOCK
    nq, nk = s_len // tq, s_len // tk
    return dict(
        nq=nq,
        q=pl.BlockSpec((tq, PAIR_W), lambda p, i: (i, p)),
        k_t=pl.BlockSpec((1, nk, PAIR_W, tk), lambda p, i: (p, 0, 0, 0)),
        k_rows=pl.BlockSpec((nk, tk, PAIR_W), lambda p, i: (0, 0, p)),
        f_t=pl.BlockSpec((1, nk, 8, tk), lambda p, i: (p, 0, 0, 0)),
        first=pl.BlockSpec((1, 1, 8, LANES), lambda p, i: (p, i, 0, 0)),
        wide=jax.ShapeDtypeStruct((s_len, ATT_W), F32),
        k_t_out=jax.ShapeDtypeStruct((N_PAIRS, nk, PAIR_W, tk), F32),
        f_t_out=jax.ShapeDtypeStruct((N_PAIRS, nk, 8, tk), F32),
        first_out=jax.ShapeDtypeStruct((N_PAIRS, nq, 8, LANES), F32),
        acc=pltpu.VMEM((2, tq, PAIR_W), F32),
    )


def _first_block(first_ref, limit):
    return jnp.clip(jnp.max(first_ref[0, 0]).astype(jnp.int32), 0, limit)


def _key_norm_bound(k):
    sq = jnp.sum(jnp.square(k.astype(F32)).reshape(k.shape[0], N_HEADS, HEAD_DIM), axis=-1)
    bound = jnp.sqrt(jnp.max(sq, axis=0)).reshape(N_PAIRS, 2)
    return jnp.broadcast_to(jnp.pad(bound, ((0, 0), (0, 6)))[:, :, None], (N_PAIRS, 8, LANES))


def _fox_fwd(qs, k3, v3, fw, ft4, kmax):
    tq, tk = FOX_Q_BLOCK, ATT_BLOCK
    sp = _att_specs(qs.shape[0], tq)
    ratio = tq // tk

    def body(q_ref, k_ref, v_ref, fw_ref, ft_ref, kmax_ref, y_ref, lse_ref, first_ref, acc_ref, max_ref, sum_ref):
        i = pl.program_id(1)
        first = _head_masks(tq)
        qh = _heads_of(q_ref, first)
        fqh = _head_cols(fw_ref)
        acc_ref[...] = jnp.zeros_like(acc_ref)
        sum_ref[...] = jnp.zeros_like(sum_ref)
        max_ref[...] = jnp.full(max_ref.shape, NEG_BIG, F32)
        reach = []
        for n in range(2):
            qf = qh[n].astype(F32)
            reach.append(jnp.sqrt(jnp.sum(qf * qf, axis=-1, keepdims=True)) * kmax_ref[0, n:n + 1, 0:1] + fqh[n])

        def logits(j, shift, diag):
            k, fk = k_ref[j], ft_ref[0, j]
            raw = [_dot_nt(qh[n], k) for n in range(2)]
            out = []
            for n in range(2):
                s = raw[n] + (shift[n] - fk[n:n + 1, :])
                if diag:
                    s = jnp.where(_causal_mask(tq, tk, i * tq - j * tk, strict=False), s, NEG_BIG)
                out.append(s)
            return out

        def max_pass(j, diag):
            ss = logits(j, fqh, diag)
            for n in range(2):
                max_ref[n] = jnp.maximum(max_ref[n], ss[n])

        def sum_pass(j, shift, diag):
            ps = [jnp.exp(s) for s in logits(j, shift, diag)]
            v = v_ref[j]
            for n in range(2):
                sum_ref[n] += ps[n]
            for n in range(2):
                acc_ref[n] += _dot(ps[n].astype(BF16), v)

        for d in range(ratio):
            max_pass(ratio * i + d, True)

        slack = [jnp.max(reach[n] - jnp.max(max_ref[n], axis=-1, keepdims=True)) for n in range(2)]

        def block_matters(j):
            f_end = ft_ref[0, jnp.maximum(j, 0)]
            gap = [slack[n] - jnp.max(f_end[n:n + 1, tk - 1:tk]) for n in range(2)]
            return (j >= 0) & (jnp.maximum(gap[0], gap[1]) > -EXP_UNDERFLOW)

        def walk_left(j):
            max_pass(j, False)
            return j - 1

        j_first = lax.while_loop(block_matters, walk_left, ratio * i - 1) + 1
        m = [jnp.max(max_ref[n], axis=-1, keepdims=True) for n in range(2)]
        shift = [fqh[n] - m[n] for n in range(2)]

        def one(j, c):
            sum_pass(j, shift, False)
            return c
        lax.fori_loop(j_first, ratio * i, one, 0)
        for d in range(ratio):
            sum_pass(ratio * i + d, shift, True)
        l = [jnp.sum(sum_ref[n], axis=-1, keepdims=True) for n in range(2)]
        y_ref[...] = jnp.where(first, acc_ref[0] / l[0], acc_ref[1] / l[1])
        lse_ref[...] = jnp.where(first, m[0] + jnp.log(l[0]), m[1] + jnp.log(l[1]))
        first_ref[...] = jnp.ones(first_ref.shape, F32) * j_first.astype(F32)

    tile = pltpu.VMEM((2, tq, tk), F32)
    return pl.pallas_call(
        body,
        grid=(N_PAIRS, sp["nq"]),
        in_specs=[sp["q"], sp["k_rows"], sp["k_rows"], sp["q"], sp["f_t"],
                  pl.BlockSpec((1, 8, LANES), lambda p, i: (p, 0, 0))],
        out_specs=[sp["q"], sp["q"], sp["first"]],
        out_shape=[sp["wide"], sp["wide"], sp["first_out"]],
        scratch_shapes=[sp["acc"], tile, tile],
        compiler_params=_cparams(56),
        name="fox_fwd",
    )(qs, k3, v3, fw, ft4, kmax)


def _fox_bwd(qs, k3, v3, dy, y, lse, fw, ft4, first_block):
    tq, tk = FOX_Q_BLOCK, ATT_BLOCK
    sp = _att_specs(qs.shape[0], tq)
    ratio = tq // tk

    def body(q_ref, k_ref, v_ref, dy_ref, y_ref, lse_ref, fw_ref, ft_ref, first_ref,
             dq_ref, dfq_ref, dkt_ref, dvt_ref, dft_ref, acc_ref):
        i = pl.program_id(1)

        @pl.when(i == 0)
        def _():
            dkt_ref[...] = jnp.zeros_like(dkt_ref)
            dvt_ref[...] = jnp.zeros_like(dvt_ref)
            dft_ref[...] = jnp.zeros_like(dft_ref)

        first = _head_masks(tq)
        qh = _heads_of(q_ref, first)
        dyv = dy_ref[...]
        dyb = dyv.astype(BF16)
        zero = jnp.zeros_like(dyb)
        dyh = [jnp.where(first, dyb, zero), jnp.where(first, zero, dyb)]
        prod = dyv * y_ref[...]
        zf = jnp.zeros_like(prod)
        delta = [jnp.sum(jnp.where(first, prod, zf), axis=-1, keepdims=True),
                 jnp.sum(jnp.where(first, zf, prod), axis=-1, keepdims=True)]
        fqh = _head_cols(fw_ref)
        lseh = _head_cols(lse_ref)
        shift = [fqh[n] - lseh[n] for n in range(2)]
        acc_ref[...] = jnp.zeros_like(acc_ref)

        def block(j, rows, diag):
            mask = _causal_mask(tq, tk, i * tq - j * tk, strict=False) if diag else None
            k, v, fk = k_ref[j], v_ref[j], ft_ref[0, j]
            logits = [_dot_nt(qh[n], k) for n in range(2)]
            dps = [_dot_nt(dyh[n], v) for n in range(2)]
            pbs, dsbs, out = [], [], []
            for n in range(2):
                p = jnp.exp(logits[n] + (shift[n] - fk[n:n + 1, :]))
                if diag:
                    p = jnp.where(mask, p, 0.0)
                ds = p * (dps[n] - delta[n])
                pbs.append(p.astype(BF16))
                dsbs.append(ds.astype(BF16))
                out.append(rows[n] + jnp.sum(ds, axis=-1, keepdims=True))
                dft_ref[0, j, n:n + 1, :] -= _colsum(ds)
            for n in range(2):
                acc_ref[n] += _dot(dsbs[n], k)
            dkt_ref[0, j] += _dot_tn(qh[0], dsbs[0]) + _dot_tn(qh[1], dsbs[1])
            dvt_ref[0, j] += _dot_tn(dyh[0], pbs[0]) + _dot_tn(dyh[1], pbs[1])
            return tuple(out)

        rows = (jnp.zeros((tq, 1), F32),) * 2
        rows = lax.fori_loop(_first_block(first_ref, ratio * i), ratio * i, lambda j, c: block(j, c, False), rows)
        for d in range(ratio):
            rows = block(ratio * i + d, rows, True)
        dq_ref[...] = jnp.where(first, acc_ref[0], acc_ref[1])
        lane = lax.broadcasted_iota(jnp.int32, (tq, 8), 1)
        dfq_ref[0] = jnp.where(lane == 0, rows[0], jnp.where(lane == 1, rows[1], 0.0))

    return pl.pallas_call(
        body,
        grid=(N_PAIRS, sp["nq"]),
        in_specs=[sp["q"], sp["k_rows"], sp["k_rows"], sp["q"], sp["q"], sp["q"], sp["q"], sp["f_t"], sp["first"]],
        out_specs=[sp["q"], pl.BlockSpec((1, tq, 8), lambda p, i: (p, i, 0)), sp["k_t"], sp["k_t"], sp["f_t"]],
        out_shape=[sp["wide"], jax.ShapeDtypeStruct((N_PAIRS, qs.shape[0], 8), F32),
                   sp["k_t_out"], sp["k_t_out"], sp["f_t_out"]],
        scratch_shapes=[sp["acc"]],
        compiler_params=_cparams(56),
        name="fox_bwd",
    )(qs, k3, v3, dy, y, lse, fw, ft4, first_block)


SIGN_BIT = 0x80000000


def _sb_terms(z, mask, diag):
    neg_abs = pltpu.bitcast(pltpu.bitcast(z, jnp.uint32) | jnp.uint32(SIGN_BIT), F32)
    lb = jnp.minimum(z, 0.0) - jnp.log(1.0 + jnp.exp(neg_abs))
    l1m = lb - z
    if diag:
        l1m = jnp.where(mask, l1m, 0.0)
    return lb, l1m


def _dot_split2_stacked(x, m2):
    hi, lo = _split2(x)
    return _dot(jnp.concatenate([hi, lo], axis=1), m2)


def _tri_stacked(kind):
    t = _tri(ATT_BLOCK, kind)
    return jnp.concatenate([t, t], axis=0)


def _sb_fwd(qs, k3, v3):
    tq, tk = SB_Q_BLOCK, ATT_BLOCK
    sp = _att_specs(qs.shape[0], tq)
    ratio = tq // tk
    upper = _tri_stacked("row_gt_col")

    def body(q_ref, k_ref, v_ref, u_ref, y_ref, rtot_ref, first_ref, acc_ref):
        i = pl.program_id(1)
        first = _head_masks(tq)
        qh = _heads_of(q_ref, first)
        u = u_ref[...]
        acc_ref[...] = jnp.zeros_like(acc_ref)

        def block(j, rs, diag):
            mask = _causal_mask(tq, tk, i * tq - j * tk, strict=True) if diag else None
            k, v = k_ref[j], v_ref[j]
            logits = [_dot_nt(qh[n], k) for n in range(2)]
            terms = [_sb_terms(z, mask, diag) for z in logits]
            right = [_dot_split2_stacked(l1m, u) for _, l1m in terms]
            weights = []
            for n in range(2):
                a = jnp.exp(terms[n][0] + right[n] + rs[n])
                if diag:
                    a = jnp.where(mask, a, 0.0)
                weights.append(a.astype(BF16))
            for n in range(2):
                acc_ref[n] += _dot(weights[n], v)
            return tuple(rs[n] + jnp.sum(terms[n][1], axis=-1, keepdims=True) for n in range(2))

        rs = (jnp.zeros((tq, 1), F32),) * 2
        for d in range(ratio):
            rs = block(ratio * i + (ratio - 1 - d), rs, True)

        def block_matters(c):
            j, r0, r1 = c
            return (j >= 0) & (jnp.max(jnp.maximum(r0, r1)) > -EXP_UNDERFLOW)

        def walk_left(c):
            j, r0, r1 = c
            r0, r1 = block(j, (r0, r1), False)
            return j - 1, r0, r1

        j, r0, r1 = lax.while_loop(block_matters, walk_left, (ratio * i - 1, rs[0], rs[1]))
        y_ref[...] = jnp.where(first, acc_ref[0], acc_ref[1])
        rtot_ref[...] = jnp.where(first, r0, r1)
        first_ref[...] = jnp.ones(first_ref.shape, F32) * (j + 1).astype(F32)

    return pl.pallas_call(
        body,
        grid=(N_PAIRS, sp["nq"]),
        in_specs=[sp["q"], sp["k_rows"], sp["k_rows"], pl.BlockSpec((2 * tk, tk), lambda p, i: (0, 0))],
        out_specs=[sp["q"], sp["q"], sp["first"]],
        out_shape=[sp["wide"], sp["wide"], sp["first_out"]],
        scratch_shapes=[sp["acc"]],
        compiler_params=_cparams(56),
        name="sb_fwd",
    )(qs, k3, v3, upper)


def _sb_bwd(qs, k3, v3, dy, rtot, first_block):
    tq, tk = SB_Q_BLOCK, ATT_BLOCK
    sp = _att_specs(qs.shape[0], tq)
    ratio = tq // tk
    lower_in = _tri_stacked("row_le_col")
    lower = _tri(tk, "row_lt_col")

    def body(q_ref, k_ref, v_ref, dy_ref, rtot_ref, first_ref, li_ref, l_ref, dq_ref, dkt_ref, dvt_ref, acc_ref):
        i = pl.program_id(1)

        @pl.when(i == 0)
        def _():
            dkt_ref[...] = jnp.zeros_like(dkt_ref)
            dvt_ref[...] = jnp.zeros_like(dvt_ref)

        first = _head_masks(tq)
        qh = _heads_of(q_ref, first)
        dyb = dy_ref[...].astype(BF16)
        zero = jnp.zeros_like(dyb)
        dyh = [jnp.where(first, dyb, zero), jnp.where(first, zero, dyb)]
        rtoth = _head_cols(rtot_ref)
        li = li_ref[...]
        lo_tri = l_ref[...]
        acc_ref[...] = jnp.zeros_like(acc_ref)

        def block(j, carry, diag):
            mask = _causal_mask(tq, tk, i * tq - j * tk, strict=True) if diag else None
            k, v = k_ref[j], v_ref[j]
            logits = [_dot_nt(qh[n], k) for n in range(2)]
            das = [_dot_nt(dyh[n], v) for n in range(2)]
            terms = [_sb_terms(z, mask, diag) for z in logits]
            upto = [_dot_split2_stacked(l1m, li) for _, l1m in terms]
            des, weights = [], []
            for n in range(2):
                a = jnp.exp(terms[n][0] + ((rtoth[n] - carry[2 * n]) - upto[n]))
                if diag:
                    a = jnp.where(mask, a, 0.0)
                des.append(a * das[n])
                weights.append(a.astype(BF16))
            lefts = [_dot(de.astype(BF16), lo_tri) for de in des]
            dzbs, out = [], []
            for n in range(2):
                beta = jnp.exp(terms[n][0])
                dz = des[n] - (des[n] + (carry[2 * n + 1] + lefts[n])) * beta
                if diag:
                    dz = jnp.where(mask, dz, 0.0)
                dzbs.append(dz.astype(BF16))
                out += [carry[2 * n] + jnp.sum(terms[n][1], axis=-1, keepdims=True),
                        carry[2 * n + 1] + jnp.sum(des[n], axis=-1, keepdims=True)]
            for n in range(2):
                acc_ref[n] += _dot(dzbs[n], k)
            dkt_ref[0, j] += _dot_tn(qh[0], dzbs[0]) + _dot_tn(qh[1], dzbs[1])
            dvt_ref[0, j] += _dot_tn(dyh[0], weights[0]) + _dot_tn(dyh[1], weights[1])
            return tuple(out)

        carry = (jnp.zeros((tq, 1), F32),) * 4
        carry = lax.fori_loop(_first_block(first_ref, ratio * i), ratio * i, lambda j, c: block(j, c, False), carry)
        for d in range(ratio):
            carry = block(ratio * i + d, carry, True)
        dq_ref[...] = jnp.where(first, acc_ref[0], acc_ref[1])

    return pl.pallas_call(
        body,
        grid=(N_PAIRS, sp["nq"]),
        in_specs=[sp["q"], sp["k_rows"], sp["k_rows"], sp["q"], sp["q"], sp["first"],
                  pl.BlockSpec((2 * tk, tk), lambda p, i: (0, 0)), pl.BlockSpec((tk, tk), lambda p, i: (0, 0))],
        out_specs=[sp["q"], sp["k_t"], sp["k_t"]],
        out_shape=[sp["wide"], sp["k_t_out"], sp["k_t_out"]],
        scratch_shapes=[sp["acc"]],
        compiler_params=_cparams(56),
        name="sb_bwd",
    )(qs, k3, v3, dy, rtot, first_block, lower_in, lower)


def _merge_fwd(x1, gates, y_fox, y_sb, w_bf, w_bs, w_out, tm=512):
    s_len = x1.shape[0]

    def body(x_ref, g_ref, yf_ref, ys_ref, wbf_ref, wbs_ref, wo_ref, o_ref):
        g = g_ref[...]
        of = _dot(yf_ref[...].astype(BF16), wbf_ref[...])
        os_ = _dot(ys_ref[...].astype(BF16), wbs_ref[...])
        merged = _sigmoid(g[:, 0:D_MODEL]) * of + _sigmoid(g[:, D_MODEL:]) * os_
        o_ref[...] = x_ref[...] + _dot(merged.astype(BF16), wo_ref[...])

    row = lambda i: (i, 0)
    full = lambda i: (0, 0)
    return pl.pallas_call(
        body,
        grid=(s_len // tm,),
        in_specs=[
            pl.BlockSpec((tm, D_MODEL), row),
            pl.BlockSpec((tm, 2 * D_MODEL), row),
            pl.BlockSpec((tm, ATT_W), row),
            pl.BlockSpec((tm, ATT_W), row),
            pl.BlockSpec((ATT_W, D_MODEL), full),
            pl.BlockSpec((ATT_W, D_MODEL), full),
            pl.BlockSpec((D_MODEL, D_MODEL), full),
        ],
        out_specs=pl.BlockSpec((tm, D_MODEL), row),
        out_shape=jax.ShapeDtypeStruct((s_len, D_MODEL), F32),
        compiler_params=_cparams(48),
        name="merge_fwd",
    )(x1, gates, y_fox, y_sb, w_bf, w_bs, w_out)


def _merge_bwd(dx2, gates, y_fox, y_sb, w_bf, w_bs, w_out, tm=512):
    s_len = dx2.shape[0]

    def body(d_ref, g_ref, yf_ref, ys_ref, wbf_ref, wbs_ref, wo_ref,
             dyf_ref, dys_ref, dg_ref, dof_ref, dos_ref, m_ref, dbf_ref):
        dbf = d_ref[...].astype(BF16)
        dbf_ref[...] = dbf
        dm = _dot_nt(dbf, wo_ref[...])
        g = g_ref[...]
        of = _dot(yf_ref[...].astype(BF16), wbf_ref[...])
        os_ = _dot(ys_ref[...].astype(BF16), wbs_ref[...])
        sf = _sigmoid(g[:, 0:D_MODEL])
        ss = _sigmoid(g[:, D_MODEL:])
        m_ref[...] = (sf * of + ss * os_).astype(BF16)
        d_of = (dm * sf).astype(BF16)
        d_os = (dm * ss).astype(BF16)
        dof_ref[...] = d_of
        dos_ref[...] = d_os
        dg_ref[:, 0:D_MODEL] = (dm * of * sf * (1.0 - sf)).astype(BF16)
        dg_ref[:, D_MODEL:] = (dm * os_ * ss * (1.0 - ss)).astype(BF16)
        dyf_ref[...] = _dot_nt(d_of, wbf_ref[...])
        dys_ref[...] = _dot_nt(d_os, wbs_ref[...])

    row = lambda i: (i, 0)
    full = lambda i: (0, 0)
    return pl.pallas_call(
        body,
        grid=(s_len // tm,),
        in_specs=[
            pl.BlockSpec((tm, D_MODEL), row),
            pl.BlockSpec((tm, 2 * D_MODEL), row),
            pl.BlockSpec((tm, ATT_W), row),
            pl.BlockSpec((tm, ATT_W), row),
            pl.BlockSpec((ATT_W, D_MODEL), full),
            pl.BlockSpec((ATT_W, D_MODEL), full),
            pl.BlockSpec((D_MODEL, D_MODEL), full),
        ],
        out_specs=[
            pl.BlockSpec((tm, ATT_W), row), pl.BlockSpec((tm, ATT_W), row),
            pl.BlockSpec((tm, 2 * D_MODEL), row),
            pl.BlockSpec((tm, D_MODEL), row), pl.BlockSpec((tm, D_MODEL), row),
            pl.BlockSpec((tm, D_MODEL), row), pl.BlockSpec((tm, D_MODEL), row),
        ],
        out_shape=[
            jax.ShapeDtypeStruct((s_len, ATT_W), F32), jax.ShapeDtypeStruct((s_len, ATT_W), F32),
            jax.ShapeDtypeStruct((s_len, 2 * D_MODEL), BF16),
            jax.ShapeDtypeStruct((s_len, D_MODEL), BF16), jax.ShapeDtypeStruct((s_len, D_MODEL), BF16),
            jax.ShapeDtypeStruct((s_len, D_MODEL), BF16), jax.ShapeDtypeStruct((s_len, D_MODEL), BF16),
        ],
        compiler_params=_cparams(56),
        name="merge_bwd",
    )(dx2, gates, y_fox, y_sb, w_bf, w_bs, w_out)


def _ple_loss(x3, p, g, w_pg, w_pp, target, tm=512):
    s_len = x3.shape[0]
    inv_d = 1.0 / D_MODEL

    def body(x_ref, p_ref, g_ref, wpg_ref, wpp_ref, t_ref,
             dx_ref, du_ref, dt_ref, hn_ref, dg_ref, loss_ref):
        @pl.when(pl.program_id(0) == 0)
        def _():
            dg_ref[...] = jnp.zeros_like(dg_ref)
            loss_ref[...] = jnp.zeros_like(loss_ref)

        x = x_ref[...]
        xn, r = _rms(x)
        gain = g_ref[...]
        hn = (xn * gain).astype(BF16)
        hn_ref[...] = hn
        sg = _sigmoid(_dot(hn, wpg_ref[...]))
        t = _dot(p_ref[...].astype(BF16), wpp_ref[...])
        err = x + sg * t - t_ref[...]
        sq = jnp.sum(_colsum(err * err), axis=-1, keepdims=True)
        loss_ref[...] += (0.5 * inv_d) * sq
        dy = err * inv_d
        du = (dy * t * sg * (1.0 - sg)).astype(BF16)
        du_ref[...] = du
        dt_ref[...] = (dy * sg).astype(BF16)
        dh = _dot_nt(du, wpg_ref[...])
        dx_ref[...] = dy + _rms_bwd(dh, xn, r, gain)
        dg_ref[0:1, :] += _colsum(dh * xn)

    row = lambda i: (i, 0)
    full = lambda i: (0, 0)
    bf = jax.ShapeDtypeStruct((s_len, D_MODEL), BF16)
    return pl.pallas_call(
        body,
        grid=(s_len // tm,),
        in_specs=[
            pl.BlockSpec((tm, D_MODEL), row),
            pl.BlockSpec((tm, PLE_DIM), row),
            pl.BlockSpec((1, D_MODEL), full),
            pl.BlockSpec((D_MODEL, D_MODEL), full),
            pl.BlockSpec((PLE_DIM, D_MODEL), full),
            pl.BlockSpec((tm, D_MODEL), row),
        ],
        out_specs=[
            pl.BlockSpec((tm, D_MODEL), row), pl.BlockSpec((tm, D_MODEL), row),
            pl.BlockSpec((tm, D_MODEL), row), pl.BlockSpec((tm, D_MODEL), row),
            pl.BlockSpec((8, D_MODEL), full), pl.BlockSpec((8, LANES), full),
        ],
        out_shape=[
            jax.ShapeDtypeStruct((s_len, D_MODEL), F32), bf, bf, bf,
            jax.ShapeDtypeStruct((8, D_MODEL), F32), jax.ShapeDtypeStruct((8, LANES), F32),
        ],
        compiler_params=_cparams(48),
        name="ple_loss",
    )(x3, p, g, w_pg, w_pp, target)


def _qknorm_bwd(fq, fk, dqs, dk, dv, qn, kn, bd, bd_t, tm=512):
    s_len = fq.shape[0]

    def body(fq_ref, fk_ref, dq_ref, dk_ref, dv_ref, qn_ref, kn_ref, bd_ref, bdt_ref,
             dz_ref, dqn_ref, dkn_ref):
        @pl.when(pl.program_id(0) == 0)
        def _():
            dqn_ref[...] = jnp.zeros_like(dqn_ref)
            dkn_ref[...] = jnp.zeros_like(dkn_ref)

        bd_m = bd_ref[...]
        bdt_m = bdt_ref[...]

        def one(x, dy, gain, dgain_ref):
            xn, rw = _head_rms(x, bd_m, bdt_m)
            dgain_ref[0:1, :] += _colsum(dy * xn)
            dxn = dy * gain
            return rw * (dxn - xn * _head_mean(dxn * xn, bd_m, bdt_m))

        dz_ref[:, 0:ATT_W] = one(fq_ref[...], dq_ref[...] * QK_SCALE, qn_ref[...], dqn_ref).astype(BF16)
        dz_ref[:, ATT_W:2 * ATT_W] = one(fk_ref[...], dk_ref[...], kn_ref[...], dkn_ref).astype(BF16)
        dz_ref[:, 2 * ATT_W:] = dv_ref[...].astype(BF16)

    row = lambda i: (i, 0)
    full = lambda i: (0, 0)
    att = pl.BlockSpec((tm, ATT_W), row)
    return pl.pallas_call(
        body,
        grid=(s_len // tm,),
        in_specs=[att, att, att, att, att,
                  pl.BlockSpec((1, ATT_W), full), pl.BlockSpec((1, ATT_W), full),
                  pl.BlockSpec((ATT_W, LANES), full), pl.BlockSpec((LANES, ATT_W), full)],
        out_specs=[pl.BlockSpec((tm, 3 * ATT_W), row), pl.BlockSpec((8, ATT_W), full), pl.BlockSpec((8, ATT_W), full)],
        out_shape=[jax.ShapeDtypeStruct((s_len, 3 * ATT_W), BF16),
                   jax.ShapeDtypeStruct((8, ATT_W), F32), jax.ShapeDtypeStruct((8, ATT_W), F32)],
        name="qknorm_bwd",
    )(fq, fk, dqs, dk, dv, qn, kn, bd, bd_t)


def _inproj_bwd(x1, dx2, g, dzf, dlogf, logf, dzs, dgates, w_fox, w_fl, w_sb, w_gates, tm=512):
    s_len = x1.shape[0]

    def body(x_ref, d_ref, g_ref, dzf_ref, dlf_ref, lf_ref, dzs_ref, dgt_ref, wf_ref, wl_ref, ws_ref, wg_ref,
             dx_ref, h_ref, dfl_ref, dg_ref, db_ref):
        @pl.when(pl.program_id(0) == 0)
        def _():
            dg_ref[...] = jnp.zeros_like(dg_ref)
            db_ref[...] = jnp.zeros_like(db_ref)

        xn, r = _rms(x_ref[...])
        gain = g_ref[...]
        h_ref[...] = (xn * gain).astype(BF16)
        lane = lax.broadcasted_iota(jnp.int32, (tm, LANES), 1)
        dfl = jnp.where(lane < N_HEADS, dlf_ref[...] * (1.0 - jnp.exp(lf_ref[...])), 0.0)
        db_ref[0:1, :] += _colsum(dfl)
        dflb = dfl.astype(BF16)
        dfl_ref[...] = dflb
        dh = (_dot_nt(dzf_ref[...], wf_ref[...]) + _dot_nt(dflb, wl_ref[...])
              + _dot_nt(dzs_ref[...], ws_ref[...]) + _dot_nt(dgt_ref[...], wg_ref[...]))
        dx_ref[...] = d_ref[...] + _rms_bwd(dh, xn, r, gain)
        dg_ref[0:1, :] += _colsum(dh * xn)

    row = lambda i: (i, 0)
    full = lambda i: (0, 0)
    return pl.pallas_call(
        body,
        grid=(s_len // tm,),
        in_specs=[
            pl.BlockSpec((tm, D_MODEL), row),
            pl.BlockSpec((tm, D_MODEL), row),
            pl.BlockSpec((1, D_MODEL), full),
            pl.BlockSpec((tm, 3 * ATT_W), row),
            pl.BlockSpec((tm, LANES), row),
            pl.BlockSpec((tm, LANES), row),
            pl.BlockSpec((tm, 3 * ATT_W), row),
            pl.BlockSpec((tm, 2 * D_MODEL), row),
            pl.BlockSpec((D_MODEL, 3 * ATT_W), full),
            pl.BlockSpec((D_MODEL, LANES), full),
            pl.BlockSpec((D_MODEL, 3 * ATT_W), full),
            pl.BlockSpec((D_MODEL, 2 * D_MODEL), full),
        ],
        out_specs=[
            pl.BlockSpec((tm, D_MODEL), row), pl.BlockSpec((tm, D_MODEL), row), pl.BlockSpec((tm, LANES), row),
            pl.BlockSpec((8, D_MODEL), full), pl.BlockSpec((8, LANES), full),
        ],
        out_shape=[
            jax.ShapeDtypeStruct((s_len, D_MODEL), F32), jax.ShapeDtypeStruct((s_len, D_MODEL), BF16),
            jax.ShapeDtypeStruct((s_len, LANES), BF16),
            jax.ShapeDtypeStruct((8, D_MODEL), F32), jax.ShapeDtypeStruct((8, LANES), F32),
        ],
        compiler_params=_cparams(56),
        name="inproj_bwd",
    )(x1, dx2, g, dzf, dlogf, logf, dzs, dgates, w_fox, w_fl, w_sb, w_gates)


def _split_w_in(w_in):
    o = 3 * ATT_W
    w_fox = w_in[:, 0:o]
    w_fl = jnp.pad(w_in[:, o:o + N_HEADS], ((0, 0), (0, LANES - N_HEADS)))
    w_sb = w_in[:, o + N_HEADS:2 * o + N_HEADS]
    w_gates = w_in[:, 2 * o + N_HEADS:]
    return w_fox, w_fl, w_sb, w_gates


def _local_grads(x, p, target, small, full, pending=None, send_early=None):
    blk = ATT_BLOCK
    bd, bd_t = _head_sum_matrices()
    full = dict(full)
    late = list(pending) if pending else []

    x1, a1, b1, *gathered = _ffn_fwd(x, small["ffn1_norm"], full["ffn1_w_gate"], full["ffn1_w_up"],
                                     full["ffn1_w_down"], gather=[pending[k] for k in late])
    for k, gth in zip(late, gathered):
        full[k] = gth if k in KEPT_AS_SHARDS else _whole(k, gth)
    w_fox, w_fl, w_sb, w_gates = _split_w_in(full["w_in"])
    bias = jnp.pad(small["forget_bias"], ((0, 0), (0, LANES - N_HEADS)))
    qn = jnp.tile(small["q_norm"], (1, N_HEADS))
    kn = jnp.tile(small["k_norm"], (1, N_HEADS))
    fq, fk, f_qs, f_k, f_v, logf, s_qs, s_k, s_v, gates = _inproj_fwd(
        x1, small["mix_norm"], w_fox, w_fl, w_sb, w_gates, bias, qn, kn, bd, bd_t)
    f_cum = _cumsum_rows(logf, reverse=False)
    f8 = f_cum[:, 0:N_HEADS]
    fw = jnp.repeat(f8, HEAD_DIM, axis=1)
    ft4 = _pair_rows_t(f8, blk)
    f_k3, f_v3 = _blocked_rows(f_k, blk), _blocked_rows(f_v, blk)
    y_fox, lse, f_first = _fox_fwd(f_qs, f_k3, f_v3, fw, ft4, _key_norm_bound(f_k))
    s_k3, s_v3 = _blocked_rows(s_k, blk), _blocked_rows(s_v, blk)
    y_sb, s_rtot, s_first = _sb_fwd(s_qs, s_k3, s_v3)
    x2 = _merge_fwd(x1, gates, y_fox, y_sb, full["w_branch_fox"], full["w_branch_sb"], full["w_out"])
    x3, a2, b2 = _ffn_fwd(x2, small["ffn2_norm"], full["ffn2_w_gate"], full["ffn2_w_up"], full["ffn2_w_down"])

    dx3, du_ple, dt_ple, hn_ple, dg_ple, loss_sum = _ple_loss(
        x3, p, small["ple_norm"], full["w_ple_gate"], full["w_ple_proj"], target)
    dx2, u2, da2, db2, h_ffn2, d3_bf, dg_ffn2 = _ffn_bwd(
        x2, dx3, small["ffn2_norm"], a2, b2, full["ffn2_w_gate"], full["ffn2_w_up"], full["ffn2_w_down"])
    dy_fox, dy_sb, dgates, d_of, d_os, merged, d2_bf = _merge_bwd(
        dx2, gates, y_fox, y_sb, full["w_branch_fox"], full["w_branch_sb"], full["w_out"])

    f_dqs, dfq_p, f_dkt4, f_dvt4, dft4 = _fox_bwd(f_qs, f_k3, f_v3, dy_fox, y_fox, lse, fw, ft4, f_first)
    s_dqs, s_dkt4, s_dvt4 = _sb_bwd(s_qs, s_k3, s_v3, dy_sb, s_rtot, s_first)

    dzf, dqn8, dkn8 = _qknorm_bwd(fq, fk, f_dqs, _unblocked_t(f_dkt4), _unblocked_t(f_dvt4), qn, kn, bd, bd_t)
    dzs = jnp.concatenate([s_dqs * QK_SCALE, _unblocked_t(s_dkt4), _unblocked_t(s_dvt4)], axis=1).astype(BF16)
    df8 = _unpair_rows_t(dft4) + dfq_p[:, :, 0:2].transpose(1, 0, 2).reshape(-1, N_HEADS)
    dlogf = _cumsum_rows(jnp.pad(df8, ((0, 0), (0, LANES - N_HEADS))), reverse=True)
    dx1, h_mix, dfl, dg_mix, dbias8 = _inproj_bwd(
        x1, dx2, small["mix_norm"], dzf, dlogf, logf, dzs, dgates, w_fox, w_fl, w_sb, w_gates)

    one = lambda t: t[None]
    gw = {}
    gw["ffn2_w_gate"] = _wgrad(da2, one(h_ffn2), name="wgrad_ffn2_gate")
    gw["ffn2_w_up"] = _wgrad(db2, one(h_ffn2), name="wgrad_ffn2_up")
    gw["ffn2_w_down"] = _wgrad(u2, one(d3_bf), scale=0.5, name="wgrad_ffn2_down")
    g_fox = _wgrad(one(h_mix), one(dzf), name="wgrad_in_fox")[0]
    g_fl = _wgrad(one(h_mix), one(dfl), name="wgrad_in_forget")[0]
    g_sb = _wgrad(one(h_mix), one(dzs), name="wgrad_in_sb")[0]
    g_gt = _wgrad(one(h_mix), one(dgates), name="wgrad_in_gates")[0]
    gw["w_in"] = jnp.concatenate([g_fox, g_fl[:, 0:N_HEADS], g_sb, g_gt], axis=1)
    gw["w_branch_fox"] = _wgrad(one(y_fox), one(d_of), name="wgrad_branch_fox")[0]
    gw["w_branch_sb"] = _wgrad(one(y_sb), one(d_os), name="wgrad_branch_sb")[0]
    gw["w_out"] = _wgrad(one(merged), one(d2_bf), name="wgrad_out")[0]
    gw["w_ple_gate"] = _wgrad(one(hn_ple), one(du_ple), name="wgrad_ple_gate")[0]
    gw["w_ple_proj"] = _wgrad(one(p), one(dt_ple), name="wgrad_ple_proj")[0]

    sent_names, to_send = send_early(gw) if send_early else ([], [])
    grad_x, u1, da1, db1, h_ffn1, d1_bf, dg_ffn1, *landed = _ffn_bwd(
        x, dx1, small["ffn1_norm"], a1, b1, full["ffn1_w_gate"], full["ffn1_w_up"], full["ffn1_w_down"],
        scatter=to_send)
    gw["ffn1_w_gate"] = _wgrad(da1, one(h_ffn1), name="wgrad_ffn1_gate")
    gw["ffn1_w_up"] = _wgrad(db1, one(h_ffn1), name="wgrad_ffn1_up")
    gw["ffn1_w_down"] = _wgrad(u1, one(d1_bf), scale=0.5, name="wgrad_ffn1_down")

    fold = lambda t: jnp.sum(t[0:1].reshape(N_HEADS, HEAD_DIM), axis=0, keepdims=True)
    gs = {
        "ffn1_norm": dg_ffn1[0:1], "mix_norm": dg_mix[0:1], "ffn2_norm": dg_ffn2[0:1], "ple_norm": dg_ple[0:1],
        "forget_bias": dbias8[0:1, 0:N_HEADS], "q_norm": fold(dqn8), "k_norm": fold(dkn8),
    }
    return loss_sum, grad_x, gw, gs, dict(zip(sent_names, landed))


def _position():
    return lax.axis_index("x"), lax.axis_index("y"), lax.axis_index("c")


def _other_chips(x, y):
    return [(1 - x, y), (x, 1 - y), (1 - x, 1 - y)]


ANY = pl.BlockSpec(memory_space=pl.ANY)


def _place_own_shard(w, q):
    rows, cols = w.shape
    tr = _row_block(rows, cols * 4, budget=2 * MIB)

    def body(q_ref, w_ref, o_ref):
        o_ref[0] = w_ref[...].astype(BF16)

    return pl.pallas_call(
        body,
        grid_spec=pltpu.PrefetchScalarGridSpec(
            num_scalar_prefetch=1,
            grid=(rows // tr,),
            in_specs=[pl.BlockSpec((tr, cols), lambda i, q_ref: (i, 0))],
            out_specs=pl.BlockSpec((1, tr, cols), lambda i, q_ref: (q_ref[0], i, 0)),
        ),
        out_shape=jax.ShapeDtypeStruct((N_CHIPS, rows, cols), BF16),
        name="place_own_shard",
    )(q, w)


def _gather_semaphores(n):
    return [pltpu.SemaphoreType.DMA((6 * n,)), pltpu.SemaphoreType.DMA((6 * n,))]


def _gather_steps(bufs, send_sems, recv_sems):
    n = len(bufs)
    x, y, c = _position()
    q = 2 * x + y
    chips = _other_chips(x, y)
    sibling = (x, y, 1 - c)

    def half(a, slot, which):
        r2 = bufs[a].shape[1] // 2
        return bufs[a].at[slot, pl.ds(which * r2, r2), :]

    def copy(a, k, region, to):
        return pltpu.make_async_remote_copy(
            src_ref=region, dst_ref=region, send_sem=send_sems.at[6 * a + k], recv_sem=recv_sems.at[6 * a + k],
            device_id=to, device_id_type=MESH)

    def to_chip(a, k):
        tx, ty = chips[k]
        return copy(a, k, half(a, q, c), (tx, ty, c))

    def to_sibling(a, k):
        tx, ty = chips[k]
        return copy(a, 3 + k, half(a, 2 * tx + ty, c), sibling)

    def start():
        for a in range(n):
            for k in range(3):
                to_chip(a, k).start()

    def finish():
        for a in range(n):
            for k, (tx, ty) in enumerate(chips):
                copy(a, k, half(a, 2 * tx + ty, c), (tx, ty, c)).wait_recv()
                to_sibling(a, k).start()
        for a in range(n):
            for k, (tx, ty) in enumerate(chips):
                copy(a, 3 + k, half(a, 2 * tx + ty, 1 - c), sibling).wait_recv()
        for a in range(n):
            for k in range(3):
                to_chip(a, k).wait_send()
                to_sibling(a, k).wait_send()

    return start, finish


def _allgather_weights(slots):
    n = len(slots)

    def body(*refs):
        start, finish = _gather_steps(refs[n:2 * n], *refs[2 * n:])
        start()
        finish()

    return pl.pallas_call(
        body,
        in_specs=[ANY] * n,
        out_specs=[ANY] * n,
        out_shape=[jax.ShapeDtypeStruct(s.shape, s.dtype) for s in slots],
        input_output_aliases={a: a for a in range(n)},
        scratch_shapes=_gather_semaphores(n),
        name="allgather_weights",
    )(*slots)


def _exchange_pair_halves(grads):
    n = len(grads)

    def body(*refs):
        ins, outs = refs[0:n], refs[n:2 * n]
        send_sems, recv_sems = refs[2 * n:]
        x, y, c = _position()
        copies = []
        for a in range(n):
            r2 = grads[a].shape[1] // 2
            cp = pltpu.make_async_remote_copy(
                src_ref=ins[a].at[:, pl.ds((1 - c) * r2, r2), :], dst_ref=outs[a],
                send_sem=send_sems.at[a], recv_sem=recv_sems.at[a], device_id=(x, y, 1 - c), device_id_type=MESH)
            cp.start()
            copies.append(cp)
        for cp in copies:
            cp.wait()

    return pl.pallas_call(
        body,
        in_specs=[ANY] * n,
        out_specs=[ANY] * n,
        out_shape=[jax.ShapeDtypeStruct((N_CHIPS, g.shape[1] // 2, g.shape[2]), g.dtype) for g in grads],
        scratch_shapes=[pltpu.SemaphoreType.DMA((n,)), pltpu.SemaphoreType.DMA((n,))],
        name="rs_pair_exchange",
    )(*grads)


def _scatter_semaphores(n):
    return [pltpu.SemaphoreType.DMA((3 * n,)), pltpu.SemaphoreType.DMA((3 * n,)), pltpu.SemaphoreType.DMA((n,))]


def _scatter_steps(ins, outs, send_sems, recv_sems, local_sems):
    n = len(ins)
    x, y, c = _position()
    q = 2 * x + y
    chips = _other_chips(x, y)

    def own(a):
        return pltpu.make_async_copy(ins[a].at[q], outs[a].at[q], local_sems.at[a])

    def to_chip(a, k):
        tx, ty = chips[k]
        return pltpu.make_async_remote_copy(
            src_ref=ins[a].at[2 * tx + ty], dst_ref=outs[a].at[q],
            send_sem=send_sems.at[3 * a + k], recv_sem=recv_sems.at[3 * a + k],
            device_id=(tx, ty, c), device_id_type=MESH)

    def start():
        for a in range(n):
            own(a).start()
            for k in range(3):
                to_chip(a, k).start()

    def finish():
        for a in range(n):
            own(a).wait()
            for k in range(3):
                to_chip(a, k).wait()

    return start, finish


def _scatter_to_owner_chips(pairs):
    n = len(pairs)

    def body(*refs):
        start, finish = _scatter_steps(refs[0:n], refs[n:2 * n], *refs[2 * n:])
        start()
        finish()

    return pl.pallas_call(
        body,
        in_specs=[ANY] * n,
        out_specs=[ANY] * n,
        out_shape=[jax.ShapeDtypeStruct(p.shape, p.dtype) for p in pairs],
        scratch_shapes=_scatter_semaphores(n),
        name="rs_scatter",
    )(*pairs)


def _join_halves(shards):
    n = len(shards)

    def body(*refs):
        bufs = refs[n:2 * n]
        send_sems, recv_sems = refs[2 * n:]
        x, y, c = _position()
        started = []
        for a in range(n):
            r2 = shards[a].shape[0] // 2
            mine = bufs[a].at[pl.ds(c * r2, r2), :]
            cp = pltpu.make_async_remote_copy(
                src_ref=mine, dst_ref=mine, send_sem=send_sems.at[a], recv_sem=recv_sems.at[a],
                device_id=(x, y, 1 - c), device_id_type=MESH)
            cp.start()
            started.append(cp)
        for cp in started:
            cp.wait()

    return pl.pallas_call(
        body,
        in_specs=[ANY] * n,
        out_specs=[ANY] * n,
        out_shape=[jax.ShapeDtypeStruct(t.shape, t.dtype) for t in shards],
        input_output_aliases={a: a for a in range(n)},
        scratch_shapes=[pltpu.SemaphoreType.DMA((n,)), pltpu.SemaphoreType.DMA((n,))],
        name="rs_join_halves",
    )(*shards)


def _add_pair(g, got, c):
    _, r2, cols = got.shape

    def body(c_ref, g_ref, got_ref, o_ref):
        o_ref[...] = (g_ref[...].astype(F32) + got_ref[...].astype(F32)).astype(BF16)

    spec = pl.BlockSpec((1, r2, cols), lambda s, c_ref: (s, 0, 0))
    return pl.pallas_call(
        body,
        grid_spec=pltpu.PrefetchScalarGridSpec(
            num_scalar_prefetch=1,
            grid=(N_CHIPS,),
            in_specs=[pl.BlockSpec((1, r2, cols), lambda s, c_ref: (s, c_ref[0], 0)), spec],
            out_specs=spec,
        ),
        out_shape=jax.ShapeDtypeStruct(got.shape, BF16),
        name="rs_add_pair",
    )(c, g, got)


def _add_chips(parts, c):
    _, r2, cols = parts.shape

    def body(c_ref, p0, p1, p2, p3, o_ref):
        o_ref[...] = ((p0[0].astype(F32) + p1[0].astype(F32)) + p2[0].astype(F32)) + p3[0].astype(F32)

    specs = [pl.BlockSpec((1, r2, cols), functools.partial(lambda i, c_ref, s: (s, 0, 0), s=s))
             for s in range(N_CHIPS)]
    return pl.pallas_call(
        body,
        grid_spec=pltpu.PrefetchScalarGridSpec(
            num_scalar_prefetch=1,
            grid=(1,),
            in_specs=specs,
            out_specs=pl.BlockSpec((r2, cols), lambda i, c_ref: (c_ref[0], 0)),
        ),
        out_shape=jax.ShapeDtypeStruct((2 * r2, cols), F32),
        name="rs_add_chips",
    )(c, parts, parts, parts, parts)


def _allreduce_small(part):
    shape = part.shape

    def body(in_ref, out_ref, gather_ref, send_sems, recv_sems):
        x, y, c = _position()
        me = 4 * x + 2 * y + c
        relations = [(a, b, d) for a in (0, 1) for b in (0, 1) for d in (0, 1)][1:]
        flip = lambda v, f: 1 - v if f else v
        copies = []
        for k, (a, b, d) in enumerate(relations):
            cp = pltpu.make_async_remote_copy(
                src_ref=in_ref, dst_ref=gather_ref.at[me], send_sem=send_sems.at[k], recv_sem=recv_sems.at[k],
                device_id=(flip(x, a), flip(y, b), flip(c, d)), device_id_type=MESH)
            cp.start()
            copies.append(cp)
        gather_ref[me] = in_ref[...]
        for cp in copies:
            cp.wait()
        total = gather_ref[0]
        for dev in range(1, 8):
            total = total + gather_ref[dev]
        out_ref[...] = total

    vmem = pl.BlockSpec(memory_space=pltpu.VMEM)
    return pl.pallas_call(
        body,
        in_specs=[vmem],
        out_specs=vmem,
        out_shape=jax.ShapeDtypeStruct(shape, F32),
        scratch_shapes=[pltpu.VMEM((8,) + shape, F32), pltpu.SemaphoreType.DMA((7,)), pltpu.SemaphoreType.DMA((7,))],
        name="allreduce_small",
    )(part)


def _adamw(w, g, m, v):
    rows, cols = w.shape
    tr = _row_block(rows, cols * 4, budget=MIB)
    c1 = 1.0 / (1.0 - ADAM_B1 ** ADAM_STEP)
    c2 = 1.0 / (1.0 - ADAM_B2 ** ADAM_STEP)

    def body(w_ref, g_ref, m_ref, v_ref, d_ref, nm_ref, nv_ref):
        g_ = g_ref[...]
        nm = ADAM_B1 * m_ref[...] + (1.0 - ADAM_B1) * g_
        nv = ADAM_B2 * v_ref[...] + (1.0 - ADAM_B2) * (g_ * g_)
        nm_ref[...] = nm
        nv_ref[...] = nv
        d_ref[...] = -ADAM_LR * ((nm * c1) / (jnp.sqrt(nv * c2) + ADAM_EPS) + ADAM_WD * w_ref[...])

    spec = pl.BlockSpec((tr, cols), lambda i: (i, 0))
    out = jax.ShapeDtypeStruct((rows, cols), F32)
    return pl.pallas_call(
        body,
        grid=(rows // tr,),
        in_specs=[spec] * 4,
        out_specs=[spec] * 3,
        out_shape=[out] * 3,
        name="adamw",
    )(w, g, m, v)


BIG = ["ffn1_w_gate", "ffn1_w_up", "ffn1_w_down", "w_in", "w_branch_fox", "w_branch_sb", "w_out",
       "ffn2_w_gate", "ffn2_w_up", "ffn2_w_down", "w_ple_gate", "w_ple_proj"]
SMALL = ["ffn1_norm", "mix_norm", "ffn2_norm", "ple_norm", "forget_bias", "q_norm", "k_norm"]
COLUMN_SHARDED = ["w_in", "w_branch_fox", "w_branch_sb", "w_ple_proj"]
KEPT_AS_SHARDS = ["ffn1_w_gate", "ffn1_w_up", "ffn1_w_down", "ffn2_w_gate", "ffn2_w_up", "ffn2_w_down"]
WORKED_TRANSPOSED = ["ffn1_w_gate", "ffn1_w_up", "ffn2_w_gate", "ffn2_w_up"]
NEEDED_FIRST = ["ffn1_w_gate", "ffn1_w_up", "ffn1_w_down"]
ORDER = ["ffn1_norm", "ffn1_w_gate", "ffn1_w_up", "ffn1_w_down", "mix_norm", "w_in", "forget_bias", "q_norm",
         "k_norm", "w_branch_fox", "w_branch_sb", "w_out", "ffn2_norm", "ffn2_w_gate", "ffn2_w_up",
         "ffn2_w_down", "ple_norm", "w_ple_gate", "w_ple_proj"]
SMALL_ROWS = {"ffn1_norm": 0, "mix_norm": 1, "ffn2_norm": 2, "ple_norm": 3}
SMALL_COLS = {"forget_bias": (0, N_HEADS), "q_norm": (N_HEADS, HEAD_DIM), "k_norm": (N_HEADS + HEAD_DIM, HEAD_DIM)}
LOSS_ROW = 5


def _stored(name, a):
    return jnp.swapaxes(a[0], 0, 1) if name in WORKED_TRANSPOSED else a[0]


def _returned(name, t):
    return (jnp.swapaxes(t, 0, 1) if name in WORKED_TRANSPOSED else t)[None]


def _whole(name, gathered):
    if name in COLUMN_SHARDED:
        return jnp.concatenate([gathered[s] for s in range(N_CHIPS)], axis=1)
    return gathered.reshape(-1, gathered.shape[-1])


def _as_shards(name, whole):
    if name in COLUMN_SHARDED:
        k, n = whole.shape
        return whole.reshape(k, N_CHIPS, n // N_CHIPS).transpose(1, 0, 2)
    return whole.reshape(N_CHIPS, whole.shape[0] // N_CHIPS, whole.shape[1])


def _pack_small(values, extra=None):
    rows = [values[k] for k in ("ffn1_norm", "mix_norm", "ffn2_norm", "ple_norm")]
    tail = jnp.concatenate([values["forget_bias"], values["q_norm"], values["k_norm"]], axis=1)
    rows.append(jnp.pad(tail, ((0, 0), (0, D_MODEL - tail.shape[1]))))
    packed = jnp.concatenate(rows + [jnp.zeros((3, D_MODEL), F32)], axis=0)
    if extra is not None:
        packed = packed.at[LOSS_ROW, 0].set(extra)
    return packed


def _unpack_small(packed):
    out = {k: packed[r:r + 1] for k, r in SMALL_ROWS.items()}
    for k, (start, size) in SMALL_COLS.items():
        out[k] = packed[4:5, start:start + size]
    return out


def kernel(x, p, ffn1_norm, ffn1_w_gate, ffn1_w_up, ffn1_w_down, mix_norm, w_in, forget_bias, q_norm, k_norm, w_branch_fox, w_branch_sb, w_out, ffn2_norm, ffn2_w_gate, ffn2_w_up, ffn2_w_down, ple_norm, w_ple_gate, w_ple_proj, loss_target, m_ffn1_norm, m_ffn1_w_gate, m_ffn1_w_up, m_ffn1_w_down, m_mix_norm, m_w_in, m_forget_bias, m_q_norm, m_k_norm, m_w_branch_fox, m_w_branch_sb, m_w_out, m_ffn2_norm, m_ffn2_w_gate, m_ffn2_w_up, m_ffn2_w_down, m_ple_norm, m_w_ple_gate, m_w_ple_proj, v_ffn1_norm, v_ffn1_w_gate, v_ffn1_w_up, v_ffn1_w_down, v_mix_norm, v_w_in, v_forget_bias, v_q_norm, v_k_norm, v_w_branch_fox, v_w_branch_sb, v_w_out, v_ffn2_norm, v_ffn2_w_gate, v_ffn2_w_up, v_ffn2_w_down, v_ple_norm, v_w_ple_gate, v_w_ple_proj):
    args = dict(locals())
    weights = {k: args[k] for k in ORDER}
    moments_m = {k: args["m_" + k] for k in ORDER}
    moments_v = {k: args["v_" + k] for k in ORDER}

    c_idx = lax.axis_index("c").astype(jnp.int32).reshape(1)
    q_idx = (2 * lax.axis_index("x") + lax.axis_index("y")).astype(jnp.int32).reshape(1)
    own = {k: _place_own_shard(_stored(k, weights[k]), q_idx) for k in BIG}
    full = dict(zip(NEEDED_FIRST, _allgather_weights([own[k] for k in NEEDED_FIRST])))
    pending = {k: own[k] for k in BIG if k not in NEEDED_FIRST}
    small = {k: weights[k] for k in SMALL}

    def pair_sums(names, gw):
        slots = [gw[k] if k in KEPT_AS_SHARDS else _as_shards(k, gw[k]) for k in names]
        from_core = _exchange_pair_halves(slots)
        return [_add_pair(g, got, c_idx) for g, got in zip(slots, from_core)]

    late = [k for k in BIG if k not in NEEDED_FIRST]
    loss_sum, grad_x, gw, gs, parts = _local_grads(
        x[0], p[0, 0], loss_target[0], small, full, pending, lambda early: (late, pair_sums(late, early)))

    parts.update(zip(NEEDED_FIRST, _scatter_to_owner_chips(pair_sums(NEEDED_FIRST, gw))))
    grads_big = dict(zip(BIG, _join_halves([_add_chips(parts[k], c_idx) for k in BIG])))
    reduced = _allreduce_small(_pack_small(gs, extra=loss_sum[0, 0]))
    grads_small = _unpack_small(reduced)
    loss = reduced[LOSS_ROW, 0]

    grads, deltas, new_m, new_v = {}, {}, {}, {}
    for k in BIG:
        d, nm, nv = _adamw(_stored(k, weights[k]), grads_big[k], _stored(k, moments_m[k]), _stored(k, moments_v[k]))
        grads[k], deltas[k], new_m[k], new_v[k] = (_returned(k, t) for t in (grads_big[k], d, nm, nv))
    d_s, nm_s, nv_s = _adamw(_pack_small({k: weights[k] for k in SMALL}), reduced,
                             _pack_small({k: moments_m[k] for k in SMALL}),
                             _pack_small({k: moments_v[k] for k in SMALL}))
    for k in SMALL:
        grads[k] = grads_small[k]
    for name, packed in (("d", d_s), ("m", nm_s), ("v", nv_s)):
        target = {"d": deltas, "m": new_m, "v": new_v}[name]
        target.update(_unpack_small(packed))

    return (loss, grad_x[None], *[grads[k] for k in ORDER], *[deltas[k] for k in ORDER],
            *[new_m[k] for k in ORDER], *[new_v[k] for k in ORDER])
```

```python
import functools

import jax
import jax.numpy as jnp
from jax import lax
from jax.experimental import pallas as pl
from jax.experimental.pallas import tpu as pltpu

F32 = jnp.float32
BF16 = jnp.bfloat16

D_MODEL = 1024
D_FF = 2816
N_CHIPS = 4
FF_SHARD = D_FF // N_CHIPS
FFN_CHUNKS = 2
WGRAD_TOKENS = 4096
WGRAD_VMEM = 30 * 1024 * 1024
HEAD_DIM = 64
N_HEADS = 8
ATT_W = N_HEADS * HEAD_DIM
PAIR_W = 2 * HEAD_DIM
N_PAIRS = N_HEADS // 2
PLE_DIM = 256
IN_WIDTH = 3 * ATT_W + N_HEADS + 3 * ATT_W + 2 * D_MODEL
EPS = 1e-6
QK_SCALE = HEAD_DIM ** -0.5
LANES = 128
ATT_BLOCK = 256
FOX_Q_BLOCK = 512
SB_Q_BLOCK = 256
NEG_BIG = -1e30
EXP_UNDERFLOW = 110.0

ADAM_LR = 0.001
ADAM_B1 = 0.9
ADAM_B2 = 0.999
ADAM_EPS = 1e-08
ADAM_WD = 0.01
ADAM_STEP = 10

MESH = pl.DeviceIdType.MESH
MIB = 1024 * 1024


def _cparams(vmem_mib=48):
    return pltpu.CompilerParams(vmem_limit_bytes=vmem_mib * MIB)


def _dot(a, b):
    return jnp.dot(a, b, preferred_element_type=F32)


def _dot_tn(a, b):
    return lax.dot_general(a, b, (((0,), (0,)), ((), ())), preferred_element_type=F32)


def _dot_nt(a, b):
    return lax.dot_general(a, b, (((1,), (1,)), ((), ())), preferred_element_type=F32)


def _sigmoid(x):
    return 1.0 / (1.0 + jnp.exp(-x))


def _split2(x):
    hi = x.astype(BF16)
    lo = (x - hi.astype(F32)).astype(BF16)
    return hi, lo


def _dot_split2(x, m):
    hi, lo = _split2(x)
    return _dot(hi, m) + _dot(lo, m)


def _split3(x):
    hi = x.astype(BF16)
    rest = x - hi.astype(F32)
    mid = rest.astype(BF16)
    lo = (rest - mid.astype(F32)).astype(BF16)
    return hi, mid, lo


def _rms(x):
    r = lax.rsqrt(jnp.mean(x * x, axis=-1, keepdims=True) + EPS)
    return x * r, r


def _rms_bwd(dh, xn, r, g):
    dxn = dh * g
    return r * (dxn - xn * jnp.mean(dxn * xn, axis=-1, keepdims=True))


def _colsum(x):
    return jnp.sum(x, axis=0, keepdims=True)


def _row_block(rows, row_bytes, budget):
    best = None
    for t in range(8, rows + 1, 8):
        if rows % t == 0 and t * row_bytes <= budget:
            best = t
    return best if best is not None else rows


def _ffn_fwd(x, g, wg, wu, wd, gather=(), tm=1024):
    s_len = x.shape[0]
    n = len(gather)
    steps = s_len // tm

    def body(x_ref, g_ref, wg_ref, wu_ref, wd_ref, *rest):
        o_ref, a_ref, b_ref, u_ref = rest[n:n + 4]
        h_s, acc_s = rest[2 * n + 4:2 * n + 6]
        i = pl.program_id(0)
        j = pl.program_id(1)
        if n:
            start, finish = _gather_steps(rest[n + 4:2 * n + 4], *rest[2 * n + 6:])
            pl.when((i == 0) & (j == 0))(start)

        @pl.when(j == 0)
        def _():
            xn, _ = _rms(x_ref[...])
            h_s[...] = (xn * g_ref[...]).astype(BF16)
            acc_s[...] = jnp.zeros_like(acc_s)

        chunks = [pl.ds(r * (tm // FFN_CHUNKS), tm // FFN_CHUNKS) for r in range(FFN_CHUNKS)]
        pre = [(_dot_nt(h_s[rows, :], wg_ref[0]), _dot_nt(h_s[rows, :], wu_ref[0])) for rows in chunks]
        us = []
        for rows, (a, b) in zip(chunks, pre):
            a_ref[0, rows, :] = a.astype(BF16)
            b_ref[0, rows, :] = b.astype(BF16)
            u = (a * _sigmoid(a) * b).astype(BF16)
            u_ref[0, rows, :] = u
            us.append(u)
        for rows, u in zip(chunks, us):
            acc_s[rows, :] += _dot(u, wd_ref[0])

        @pl.when(j == N_CHIPS - 1)
        def _():
            o_ref[...] = x_ref[...] + 0.5 * acc_s[...]

        if n:
            pl.when((i == steps - 1) & (j == N_CHIPS - 1))(finish)

    return pl.pallas_call(
        body,
        grid=(steps, N_CHIPS),
        in_specs=[
            pl.BlockSpec((tm, D_MODEL), lambda i, j: (i, 0)),
            pl.BlockSpec((1, D_MODEL), lambda i, j: (0, 0)),
            pl.BlockSpec((1, FF_SHARD, D_MODEL), lambda i, j: (j, 0, 0)),
            pl.BlockSpec((1, FF_SHARD, D_MODEL), lambda i, j: (j, 0, 0)),
            pl.BlockSpec((1, FF_SHARD, D_MODEL), lambda i, j: (j, 0, 0)),
        ] + [ANY] * n,
        out_specs=[pl.BlockSpec((tm, D_MODEL), lambda i, j: (i, 0))]
        + [pl.BlockSpec((1, tm, FF_SHARD), lambda i, j: (j, i, 0))] * 3 + [ANY] * n,
        out_shape=[jax.ShapeDtypeStruct((s_len, D_MODEL), F32)]
        + [jax.ShapeDtypeStruct((N_CHIPS, s_len, FF_SHARD), BF16)] * 3
        + [jax.ShapeDtypeStruct(s.shape, s.dtype) for s in gather],
        input_output_aliases={5 + a: 4 + a for a in range(n)},
        scratch_shapes=[pltpu.VMEM((tm, D_MODEL), BF16), pltpu.VMEM((tm, D_MODEL), F32)]
        + (_gather_semaphores(n) if n else []),
        compiler_params=_cparams(56),
        name="ffn_fwd_gathering" if n else "ffn_fwd",
    )(x, g, wg, wu, wd, *gather)


def _ffn_bwd(x, d, g, a_pre, b_pre, wg, wu, wd, scatter=(), tm=512):
    s_len = x.shape[0]
    nb = s_len // tm
    n = len(scatter)

    def body(x_ref, d_ref, g_ref, a_ref, b_ref, wg_ref, wu_ref, wd_ref, *rest):
        dx_ref, da_ref, db_ref, h_ref, dbf_ref, dg_ref = rest[n:n + 6]
        dbf_s, dh_s = rest[2 * n + 6:2 * n + 8]
        i = pl.program_id(0)
        j = pl.program_id(1)
        if n:
            start, finish = _scatter_steps(rest[0:n], rest[n + 6:2 * n + 6], *rest[2 * n + 8:])
            pl.when((i == 0) & (j == 0))(start)

        @pl.when(j == 0)
        def _():
            xn, _ = _rms(x_ref[...])
            h_ref[...] = (xn * g_ref[...]).astype(BF16)
            dbf = d_ref[...].astype(BF16)
            dbf_s[...] = dbf
            dbf_ref[...] = dbf
            dh_s[...] = jnp.zeros_like(dh_s)

        @pl.when((i == 0) & (j == 0))
        def _():
            dg_ref[...] = jnp.zeros_like(dg_ref)

        chunks = [pl.ds(r * (tm // FFN_CHUNKS), tm // FFN_CHUNKS) for r in range(FFN_CHUNKS)]
        dus = [0.5 * _dot_nt(dbf_s[rows, :], wd_ref[0]) for rows in chunks]
        das, dbs = [], []
        for rows, du in zip(chunks, dus):
            a = a_ref[0, rows, :].astype(F32)
            b = b_ref[0, rows, :].astype(F32)
            s = _sigmoid(a)
            silu = a * s
            da = (du * b * (s * (1.0 + a * (1.0 - s)))).astype(BF16)
            db = (du * silu).astype(BF16)
            da_ref[0, rows, :] = da
            db_ref[0, rows, :] = db
            das.append(da)
            dbs.append(db)
        for rows, da, db in zip(chunks, das, dbs):
            dh_s[rows, :] += _dot(da, wg_ref[0]) + _dot(db, wu_ref[0])

        @pl.when(j == N_CHIPS - 1)
        def _():
            xn, r = _rms(x_ref[...])
            dh = dh_s[...]
            dx_ref[...] = d_ref[...] + _rms_bwd(dh, xn, r, g_ref[...])
            dg_ref[0:1, :] += _colsum(dh * xn)

        if n:
            pl.when((i == nb - 1) & (j == N_CHIPS - 1))(finish)

    row = lambda i, j: (i, 0)
    shard = lambda i, j: (j, 0, 0)
    act = lambda i, j: (j, i, 0)
    return pl.pallas_call(
        body,
        grid=(nb, N_CHIPS),
        in_specs=[
            pl.BlockSpec((tm, D_MODEL), row),
            pl.BlockSpec((tm, D_MODEL), row),
            pl.BlockSpec((1, D_MODEL), lambda i, j: (0, 0)),
            pl.BlockSpec((1, tm, FF_SHARD), act),
            pl.BlockSpec((1, tm, FF_SHARD), act),
            pl.BlockSpec((1, FF_SHARD, D_MODEL), shard),
            pl.BlockSpec((1, FF_SHARD, D_MODEL), shard),
            pl.BlockSpec((1, FF_SHARD, D_MODEL), shard),
        ] + [ANY] * n,
        out_specs=[
            pl.BlockSpec((tm, D_MODEL), row),
            pl.BlockSpec((1, tm, FF_SHARD), act),
            pl.BlockSpec((1, tm, FF_SHARD), act),
            pl.BlockSpec((tm, D_MODEL), row),
            pl.BlockSpec((tm, D_MODEL), row),
            pl.BlockSpec((8, D_MODEL), lambda i, j: (0, 0)),
        ] + [ANY] * n,
        out_shape=[
            jax.ShapeDtypeStruct((s_len, D_MODEL), F32),
            jax.ShapeDtypeStruct((N_CHIPS, s_len, FF_SHARD), BF16),
            jax.ShapeDtypeStruct((N_CHIPS, s_len, FF_SHARD), BF16),
            jax.ShapeDtypeStruct((s_len, D_MODEL), BF16),
            jax.ShapeDtypeStruct((s_len, D_MODEL), BF16),
            jax.ShapeDtypeStruct((8, D_MODEL), F32),
        ] + [jax.ShapeDtypeStruct(s.shape, s.dtype) for s in scatter],
        scratch_shapes=[
            pltpu.VMEM((tm, D_MODEL), BF16),
            pltpu.VMEM((tm, D_MODEL), F32),
        ] + (_scatter_semaphores(n) if n else []),
        compiler_params=_cparams(56),
        name="ffn_bwd_scattering" if n else "ffn_bwd",
    )(x, d, g, a_pre, b_pre, wg, wu, wd, *scatter)


def _wgrad(a, b, scale=1.0, name="wgrad"):
    na, s_len, k_dim = a.shape
    nb, _, n_dim = b.shape
    n = max(na, nb)
    ts = WGRAD_TOKENS
    while ts > 512 and (ts > s_len or 2 * ts * (k_dim * a.dtype.itemsize + n_dim * b.dtype.itemsize) > WGRAD_VMEM):
        ts //= 2
    steps = s_len // ts

    def body(a_ref, b_ref, o_ref, acc_s):
        s = pl.program_id(1)

        @pl.when(s == 0)
        def _():
            acc_s[...] = jnp.zeros_like(acc_s)

        acc_s[...] += _dot_tn(a_ref[0].astype(BF16), b_ref[0].astype(BF16))

        @pl.when(s == steps - 1)
        def _():
            o_ref[0] = (acc_s[...] * scale).astype(BF16)

    a_map = (lambda m, s: (m, s, 0)) if na > 1 else (lambda m, s: (0, s, 0))
    b_map = (lambda m, s: (m, s, 0)) if nb > 1 else (lambda m, s: (0, s, 0))
    return pl.pallas_call(
        body,
        grid=(n, steps),
        in_specs=[pl.BlockSpec((1, ts, k_dim), a_map), pl.BlockSpec((1, ts, n_dim), b_map)],
        out_specs=pl.BlockSpec((1, k_dim, n_dim), lambda m, s: (m, 0, 0)),
        out_shape=jax.ShapeDtypeStruct((n, k_dim, n_dim), BF16),
        scratch_shapes=[pltpu.VMEM((k_dim, n_dim), F32)],
        compiler_params=_cparams(56),
        name=name,
    )(a, b)


def _head_sum_matrices():
    lane = lax.broadcasted_iota(jnp.int32, (ATT_W, LANES), 0) // HEAD_DIM
    col = lax.broadcasted_iota(jnp.int32, (ATT_W, LANES), 1)
    bd = (lane == col).astype(BF16)
    return bd, bd.T


def _head_mean(t, bd, bd_t):
    per_head = _dot_split2(t, bd) * (1.0 / HEAD_DIM)
    return _dot_split2(per_head, bd_t)


def _head_rms(x, bd, bd_t):
    per_head = _dot_split2(x * x, bd) * (1.0 / HEAD_DIM)
    r = lax.rsqrt(per_head + EPS)
    rw = _dot_split2(r, bd_t)
    return x * rw, rw


def _log_sigmoid(z):
    return jnp.minimum(z, 0.0) - jnp.log(1.0 + jnp.exp(-jnp.abs(z)))


def _inproj_fwd(x1, g, w_fox, w_fl, w_sb, w_gates, bias, qn, kn, bd, bd_t, tm=512):
    s_len = x1.shape[0]

    def body(x_ref, g_ref, wf_ref, wl_ref, ws_ref, wg_ref, bias_ref, qn_ref, kn_ref, bd_ref, bdt_ref,
             fq_ref, fk_ref, qs_ref, kf_ref, vf_ref, logf_ref, sq_ref, sk_ref, sv_ref, gates_ref):
        xn, _ = _rms(x_ref[...])
        h = (xn * g_ref[...]).astype(BF16)
        zf = _dot(h, wf_ref[...])
        fq = zf[:, 0:ATT_W]
        fk = zf[:, ATT_W:2 * ATT_W]
        fq_ref[...] = fq
        fk_ref[...] = fk
        bd_m = bd_ref[...]
        bdt_m = bdt_ref[...]
        fqn, _ = _head_rms(fq, bd_m, bdt_m)
        fkn, _ = _head_rms(fk, bd_m, bdt_m)
        qs_ref[...] = (fqn * qn_ref[...]).astype(BF16) * QK_SCALE
        kf_ref[...] = (fkn * kn_ref[...]).astype(BF16)
        vf_ref[...] = zf[:, 2 * ATT_W:3 * ATT_W].astype(BF16)
        logf_ref[...] = _log_sigmoid(_dot(h, wl_ref[...]) + bias_ref[...])
        zs = _dot(h, ws_ref[...])
        sq_ref[...] = zs[:, 0:ATT_W].astype(BF16) * QK_SCALE
        sk_ref[...] = zs[:, ATT_W:2 * ATT_W].astype(BF16)
        sv_ref[...] = zs[:, 2 * ATT_W:3 * ATT_W].astype(BF16)
        gates_ref[...] = _dot(h, wg_ref[...]).astype(BF16)

    row = lambda i: (i, 0)
    full = lambda i: (0, 0)
    att = lambda dt: jax.ShapeDtypeStruct((s_len, ATT_W), dt)
    return pl.pallas_call(
        body,
        grid=(s_len // tm,),
        in_specs=[
            pl.BlockSpec((tm, D_MODEL), row),
            pl.BlockSpec((1, D_MODEL), full),
            pl.BlockSpec((D_MODEL, 3 * ATT_W), full),
            pl.BlockSpec((D_MODEL, LANES), full),
            pl.BlockSpec((D_MODEL, 3 * ATT_W), full),
            pl.BlockSpec((D_MODEL, 2 * D_MODEL), full),
            pl.BlockSpec((1, LANES), full),
            pl.BlockSpec((1, ATT_W), full),
            pl.BlockSpec((1, ATT_W), full),
            pl.BlockSpec((ATT_W, LANES), full),
            pl.BlockSpec((LANES, ATT_W), full),
        ],
        out_specs=[
            pl.BlockSpec((tm, ATT_W), row), pl.BlockSpec((tm, ATT_W), row),
            pl.BlockSpec((tm, ATT_W), row), pl.BlockSpec((tm, ATT_W), row), pl.BlockSpec((tm, ATT_W), row),
            pl.BlockSpec((tm, LANES), row),
            pl.BlockSpec((tm, ATT_W), row), pl.BlockSpec((tm, ATT_W), row), pl.BlockSpec((tm, ATT_W), row),
            pl.BlockSpec((tm, 2 * D_MODEL), row),
        ],
        out_shape=[
            att(F32), att(F32), att(BF16), att(BF16), att(BF16),
            jax.ShapeDtypeStruct((s_len, LANES), F32),
            att(BF16), att(BF16), att(BF16),
            jax.ShapeDtypeStruct((s_len, 2 * D_MODEL), BF16),
        ],
        compiler_params=_cparams(56),
        name="inproj_fwd",
    )(x1, g, w_fox, w_fl, w_sb, w_gates, bias, qn, kn, bd, bd_t)


def _tri(n, kind):
    r = lax.broadcasted_iota(jnp.int32, (n, n), 0)
    c = lax.broadcasted_iota(jnp.int32, (n, n), 1)
    m = {"row_ge_col": r >= c, "row_le_col": r <= c, "row_gt_col": r > c, "row_lt_col": r < c}[kind]
    return m.astype(BF16)


def _cumsum_rows(x, reverse, tm=256):
    s_len = x.shape[0]
    nb = s_len // tm
    tri = _tri(tm, "row_le_col" if reverse else "row_ge_col")
    edge = 0 if reverse else tm - 1

    def body(x_ref, tri_ref, o_ref, carry_s):
        @pl.when(pl.program_id(0) == 0)
        def _():
            carry_s[...] = jnp.zeros_like(carry_s)

        hi, mid, lo = _split3(x_ref[...])
        t = tri_ref[...]
        y = _dot(t, hi) + _dot(t, mid) + _dot(t, lo) + carry_s[...]
        o_ref[...] = y
        carry_s[...] = y[edge:edge + 1, :]

    order = (lambda i: (nb - 1 - i, 0)) if reverse else (lambda i: (i, 0))
    return pl.pallas_call(
        body,
        grid=(nb,),
        in_specs=[pl.BlockSpec((tm, LANES), order), pl.BlockSpec((tm, tm), lambda i: (0, 0))],
        out_specs=pl.BlockSpec((tm, LANES), order),
        out_shape=jax.ShapeDtypeStruct((s_len, LANES), F32),
        scratch_shapes=[pltpu.VMEM((1, LANES), F32)],
        name="cumsum_rev" if reverse else "cumsum_fwd",
    )(x, tri)


def _unblocked_t(t4):
    _, nb, _, blk = t4.shape
    return t4.transpose(1, 3, 0, 2).reshape(nb * blk, ATT_W)


def _blocked_rows(t, blk):
    return t.reshape(t.shape[0] // blk, blk, t.shape[1])


def _pair_rows_t(f8, blk):
    nb = f8.shape[0] // blk
    t = f8.reshape(nb, blk, N_PAIRS, 2).transpose(2, 0, 3, 1)
    return jnp.pad(t, ((0, 0), (0, 0), (0, 6), (0, 0)))


def _unpair_rows_t(t4):
    _, nb, _, blk = t4.shape
    return t4[:, :, 0:2, :].transpose(1, 3, 0, 2).reshape(nb * blk, N_HEADS)


def _head_masks(tq):
    lane = lax.broadcasted_iota(jnp.int32, (tq, PAIR_W), 1)
    return lane < HEAD_DIM


def _causal_mask(tq, tk, offset, strict):
    d = lax.broadcasted_iota(jnp.int32, (tq, tk), 1) - lax.broadcasted_iota(jnp.int32, (tq, tk), 0)
    return (d < offset) if strict else (d <= offset)


def _heads_of(ref, first):
    t = ref[...]
    zero = jnp.zeros_like(t)
    return [jnp.where(first, t, zero), jnp.where(first, zero, t)]


def _head_cols(ref):
    t = ref[...]
    return [t[:, 0:1], t[:, HEAD_DIM:HEAD_DIM + 1]]


def _att_specs(s_len, tq):
    tk = ATT_BLOCK
    nq, nk = s_len // tq, s_len // tk
    return dict(
        nq=nq,
        q=pl.BlockSpec((tq, PAIR_W), lambda p, i: (i, p)),
        k_t=pl.BlockSpec((1, nk, PAIR_W, tk), lambda p, i: (p, 0, 0, 0)),
        k_rows=pl.BlockSpec((nk, tk, PAIR_W), lambda p, i: (0, 0, p)),
        f_t=pl.BlockSpec((1, nk, 8, tk), lambda p, i: (p, 0, 0, 0)),
        first=pl.BlockSpec((1, 1, 8, LANES), lambda p, i: (p, i, 0, 0)),
        wide=jax.ShapeDtypeStruct((s_len, ATT_W), F32),
        k_t_out=jax.ShapeDtypeStruct((N_PAIRS, nk, PAIR_W, tk), F32),
        f_t_out=jax.ShapeDtypeStruct((N_PAIRS, nk, 8, tk), F32),
        first_out=jax.ShapeDtypeStruct((N_PAIRS, nq, 8, LANES), F32),
        acc=pltpu.VMEM((2, tq, PAIR_W), F32),
    )


def _first_block(first_ref, limit):
    return jnp.clip(jnp.max(first_ref[0, 0]).astype(jnp.int32), 0, limit)


def _key_norm_bound(k):
    sq = jnp.sum(jnp.square(k.astype(F32)).reshape(k.shape[0], N_HEADS, HEAD_DIM), axis=-1)
    bound = jnp.sqrt(jnp.max(sq, axis=0)).reshape(N_PAIRS, 2)
    return jnp.broadcast_to(jnp.pad(bound, ((0, 0), (0, 6)))[:, :, None], (N_PAIRS, 8, LANES))


def _fox_fwd(qs, k3, v3, fw, ft4, kmax):
    tq, tk = FOX_Q_BLOCK, ATT_BLOCK
    sp = _att_specs(qs.shape[0], tq)
    ratio = tq // tk

    def body(q_ref, k_ref, v_ref, fw_ref, ft_ref, kmax_ref, y_ref, lse_ref, first_ref, acc_ref, max_ref, sum_ref):
        i = pl.program_id(1)
        first = _head_masks(tq)
        qh = _heads_of(q_ref, first)
        fqh = _head_cols(fw_ref)
        acc_ref[...] = jnp.zeros_like(acc_ref)
        sum_ref[...] = jnp.zeros_like(sum_ref)
        max_ref[...] = jnp.full(max_ref.shape, NEG_BIG, F32)
        reach = []
        for n in range(2):
            qf = qh[n].astype(F32)
            reach.append(jnp.sqrt(jnp.sum(qf * qf, axis=-1, keepdims=True)) * kmax_ref[0, n:n + 1, 0:1] + fqh[n])

        def logits(j, shift, diag):
            k, fk = k_ref[j], ft_ref[0, j]
            raw = [_dot_nt(qh[n], k) for n in range(2)]
            out = []
            for n in range(2):
                s = raw[n] + (shift[n] - fk[n:n + 1, :])
                if diag:
                    s = jnp.where(_causal_mask(tq, tk, i * tq - j * tk, strict=False), s, NEG_BIG)
                out.append(s)
            return out

        def max_pass(j, diag):
            ss = logits(j, fqh, diag)
            for n in range(2):
                max_ref[n] = jnp.maximum(max_ref[n], ss[n])

        def sum_pass(j, shift, diag):
            ps = [jnp.exp(s) for s in logits(j, shift, diag)]
            v = v_ref[j]
            for n in range(2):
                sum_ref[n] += ps[n]
            for n in range(2):
                acc_ref[n] += _dot(ps[n].astype(BF16), v)

        for d in range(ratio):
            max_pass(ratio * i + d, True)

        slack = [jnp.max(reach[n] - jnp.max(max_ref[n], axis=-1, keepdims=True)) for n in range(2)]

        def block_matters(j):
            f_end = ft_ref[0, jnp.maximum(j, 0)]
            gap = [slack[n] - jnp.max(f_end[n:n + 1, tk - 1:tk]) for n in range(2)]
            return (j >= 0) & (jnp.maximum(gap[0], gap[1]) > -EXP_UNDERFLOW)

        def walk_left(j):
            max_pass(j, False)
            return j - 1

        j_first = lax.while_loop(block_matters, walk_left, ratio * i - 1) + 1
        m = [jnp.max(max_ref[n], axis=-1, keepdims=True) for n in range(2)]
        shift = [fqh[n] - m[n] for n in range(2)]

        def one(j, c):
            sum_pass(j, shift, False)
            return c
        lax.fori_loop(j_first, ratio * i, one, 0)
        for d in range(ratio):
            sum_pass(ratio * i + d, shift, True)
        l = [jnp.sum(sum_ref[n], axis=-1, keepdims=True) for n in range(2)]
        y_ref[...] = jnp.where(first, acc_ref[0] / l[0], acc_ref[1] / l[1])
        lse_ref[...] = jnp.where(first, m[0] + jnp.log(l[0]), m[1] + jnp.log(l[1]))
        first_ref[...] = jnp.ones(first_ref.shape, F32) * j_first.astype(F32)

    tile = pltpu.VMEM((2, tq, tk), F32)
    return pl.pallas_call(
        body,
        grid=(N_PAIRS, sp["nq"]),
        in_specs=[sp["q"], sp["k_rows"], sp["k_rows"], sp["q"], sp["f_t"],
                  pl.BlockSpec((1, 8, LANES), lambda p, i: (p, 0, 0))],
        out_specs=[sp["q"], sp["q"], sp["first"]],
        out_shape=[sp["wide"], sp["wide"], sp["first_out"]],
        scratch_shapes=[sp["acc"], tile, tile],
        compiler_params=_cparams(56),
        name="fox_fwd",
    )(qs, k3, v3, fw, ft4, kmax)


def _fox_bwd(qs, k3, v3, dy, y, lse, fw, ft4, first_block):
    tq, tk = FOX_Q_BLOCK, ATT_BLOCK
    sp = _att_specs(qs.shape[0], tq)
    ratio = tq // tk

    def body(q_ref, k_ref, v_ref, dy_ref, y_ref, lse_ref, fw_ref, ft_ref, first_ref,
             dq_ref, dfq_ref, dkt_ref, dvt_ref, dft_ref, acc_ref):
        i = pl.program_id(1)

        @pl.when(i == 0)
        def _():
            dkt_ref[...] = jnp.zeros_like(dkt_ref)
            dvt_ref[...] = jnp.zeros_like(dvt_ref)
            dft_ref[...] = jnp.zeros_like(dft_ref)

        first = _head_masks(tq)
        qh = _heads_of(q_ref, first)
        dyv = dy_ref[...]
        dyb = dyv.astype(BF16)
        zero = jnp.zeros_like(dyb)
        dyh = [jnp.where(first, dyb, zero), jnp.where(first, zero, dyb)]
        prod = dyv * y_ref[...]
        zf = jnp.zeros_like(prod)
        delta = [jnp.sum(jnp.where(first, prod, zf), axis=-1, keepdims=True),
                 jnp.sum(jnp.where(first, zf, prod), axis=-1, keepdims=True)]
        fqh = _head_cols(fw_ref)
        lseh = _head_cols(lse_ref)
        shift = [fqh[n] - lseh[n] for n in range(2)]
        acc_ref[...] = jnp.zeros_like(acc_ref)

        def block(j, rows, diag):
            mask = _causal_mask(tq, tk, i * tq - j * tk, strict=False) if diag else None
            k, v, fk = k_ref[j], v_ref[j], ft_ref[0, j]
            logits = [_dot_nt(qh[n], k) for n in range(2)]
            dps = [_dot_nt(dyh[n], v) for n in range(2)]
            pbs, dsbs, out = [], [], []
            for n in range(2):
                p = jnp.exp(logits[n] + (shift[n] - fk[n:n + 1, :]))
                if diag:
                    p = jnp.where(mask, p, 0.0)
                ds = p * (dps[n] - delta[n])
                pbs.append(p.astype(BF16))
                dsbs.append(ds.astype(BF16))
                out.append(rows[n] + jnp.sum(ds, axis=-1, keepdims=True))
                dft_ref[0, j, n:n + 1, :] -= _colsum(ds)
            for n in range(2):
                acc_ref[n] += _dot(dsbs[n], k)
            dkt_ref[0, j] += _dot_tn(qh[0], dsbs[0]) + _dot_tn(qh[1], dsbs[1])
            dvt_ref[0, j] += _dot_tn(dyh[0], pbs[0]) + _dot_tn(dyh[1], pbs[1])
            return tuple(out)

        rows = (jnp.zeros((tq, 1), F32),) * 2
        rows = lax.fori_loop(_first_block(first_ref, ratio * i), ratio * i, lambda j, c: block(j, c, False), rows)
        for d in range(ratio):
            rows = block(ratio * i + d, rows, True)
        dq_ref[...] = jnp.where(first, acc_ref[0], acc_ref[1])
        lane = lax.broadcasted_iota(jnp.int32, (tq, 8), 1)
        dfq_ref[0] = jnp.where(lane == 0, rows[0], jnp.where(lane == 1, rows[1], 0.0))

    return pl.pallas_call(
        body,
        grid=(N_PAIRS, sp["nq"]),
        in_specs=[sp["q"], sp["k_rows"], sp["k_rows"], sp["q"], sp["q"], sp["q"], sp["q"], sp["f_t"], sp["first"]],
        out_specs=[sp["q"], pl.BlockSpec((1, tq, 8), lambda p, i: (p, i, 0)), sp["k_t"], sp["k_t"], sp["f_t"]],
        out_shape=[sp["wide"], jax.ShapeDtypeStruct((N_PAIRS, qs.shape[0], 8), F32),
                   sp["k_t_out"], sp["k_t_out"], sp["f_t_out"]],
        scratch_shapes=[sp["acc"]],
        compiler_params=_cparams(56),
        name="fox_bwd",
    )(qs, k3, v3, dy, y, lse, fw, ft4, first_block)


SIGN_BIT = 0x80000000


def _sb_terms(z, mask, diag):
    neg_abs = pltpu.bitcast(pltpu.bitcast(z, jnp.uint32) | jnp.uint32(SIGN_BIT), F32)
    lb = jnp.minimum(z, 0.0) - jnp.log(1.0 + jnp.exp(neg_abs))
    l1m = lb - z
    if diag:
        l1m = jnp.where(mask, l1m, 0.0)
    return lb, l1m


def _dot_split2_stacked(x, m2):
    hi, lo = _split2(x)
    return _dot(jnp.concatenate([hi, lo], axis=1), m2)


def _tri_stacked(kind):
    t = _tri(ATT_BLOCK, kind)
    return jnp.concatenate([t, t], axis=0)


def _sb_fwd(qs, k3, v3):
    tq, tk = SB_Q_BLOCK, ATT_BLOCK
    sp = _att_specs(qs.shape[0], tq)
    ratio = tq // tk
    upper = _tri_stacked("row_gt_col")

    def body(q_ref, k_ref, v_ref, u_ref, y_ref, rtot_ref, first_ref, acc_ref):
        i = pl.program_id(1)
        first = _head_masks(tq)
        qh = _heads_of(q_ref, first)
        u = u_ref[...]
        acc_ref[...] = jnp.zeros_like(acc_ref)

        def block(j, rs, diag):
            mask = _causal_mask(tq, tk, i * tq - j * tk, strict=True) if diag else None
            k, v = k_ref[j], v_ref[j]
            logits = [_dot_nt(qh[n], k) for n in range(2)]
            terms = [_sb_terms(z, mask, diag) for z in logits]
            right = [_dot_split2_stacked(l1m, u) for _, l1m in terms]
            weights = []
            for n in range(2):
                a = jnp.exp(terms[n][0] + right[n] + rs[n])
                if diag:
                    a = jnp.where(mask, a, 0.0)
                weights.append(a.astype(BF16))
            for n in range(2):
                acc_ref[n] += _dot(weights[n], v)
            return tuple(rs[n] + jnp.sum(terms[n][1], axis=-1, keepdims=True) for n in range(2))

        rs = (jnp.zeros((tq, 1), F32),) * 2
        for d in range(ratio):
            rs = block(ratio * i + (ratio - 1 - d), rs, True)

        def block_matters(c):
            j, r0, r1 = c
            return (j >= 0) & (jnp.max(jnp.maximum(r0, r1)) > -EXP_UNDERFLOW)

        def walk_left(c):
            j, r0, r1 = c
            r0, r1 = block(j, (r0, r1), False)
            return j - 1, r0, r1

        j, r0, r1 = lax.while_loop(block_matters, walk_left, (ratio * i - 1, rs[0], rs[1]))
        y_ref[...] = jnp.where(first, acc_ref[0], acc_ref[1])
        rtot_ref[...] = jnp.where(first, r0, r1)
        first_ref[...] = jnp.ones(first_ref.shape, F32) * (j + 1).astype(F32)

    return pl.pallas_call(
        body,
        grid=(N_PAIRS, sp["nq"]),
        in_specs=[sp["q"], sp["k_rows"], sp["k_rows"], pl.BlockSpec((2 * tk, tk), lambda p, i: (0, 0))],
        out_specs=[sp["q"], sp["q"], sp["first"]],
        out_shape=[sp["wide"], sp["wide"], sp["first_out"]],
        scratch_shapes=[sp["acc"]],
        compiler_params=_cparams(56),
        name="sb_fwd",
    )(qs, k3, v3, upper)


def _sb_bwd(qs, k3, v3, dy, rtot, first_block):
    tq, tk = SB_Q_BLOCK, ATT_BLOCK
    sp = _att_specs(qs.shape[0], tq)
    ratio = tq // tk
    lower_in = _tri_stacked("row_le_col")
    lower = _tri(tk, "row_lt_col")

    def body(q_ref, k_ref, v_ref, dy_ref, rtot_ref, first_ref, li_ref, l_ref, dq_ref, dkt_ref, dvt_ref, acc_ref):
        i = pl.program_id(1)

        @pl.when(i == 0)
        def _():
            dkt_ref[...] = jnp.zeros_like(dkt_ref)
            dvt_ref[...] = jnp.zeros_like(dvt_ref)

        first = _head_masks(tq)
        qh = _heads_of(q_ref, first)
        dyb = dy_ref[...].astype(BF16)
        zero = jnp.zeros_like(dyb)
        dyh = [jnp.where(first, dyb, zero), jnp.where(first, zero, dyb)]
        rtoth = _head_cols(rtot_ref)
        li = li_ref[...]
        lo_tri = l_ref[...]
        acc_ref[...] = jnp.zeros_like(acc_ref)

        def block(j, carry, diag):
            mask = _causal_mask(tq, tk, i * tq - j * tk, strict=True) if diag else None
            k, v = k_ref[j], v_ref[j]
            logits = [_dot_nt(qh[n], k) for n in range(2)]
            das = [_dot_nt(dyh[n], v) for n in range(2)]
            terms = [_sb_terms(z, mask, diag) for z in logits]
            upto = [_dot_split2_stacked(l1m, li) for _, l1m in terms]
            des, weights = [], []
            for n in range(2):
                a = jnp.exp(terms[n][0] + ((rtoth[n] - carry[2 * n]) - upto[n]))
                if diag:
                    a = jnp.where(mask, a, 0.0)
                des.append(a * das[n])
                weights.append(a.astype(BF16))
            lefts = [_dot(de.astype(BF16), lo_tri) for de in des]
            dzbs, out = [], []
            for n in range(2):
                beta = jnp.exp(terms[n][0])
                dz = des[n] - (des[n] + (carry[2 * n + 1] + lefts[n])) * beta
                if diag:
                    dz = jnp.where(mask, dz, 0.0)
                dzbs.append(dz.astype(BF16))
                out += [carry[2 * n] + jnp.sum(terms[n][1], axis=-1, keepdims=True),
                        carry[2 * n + 1] + jnp.sum(des[n], axis=-1, keepdims=True)]
            for n in range(2):
                acc_ref[n] += _dot(dzbs[n], k)
            dkt_ref[0, j] += _dot_tn(qh[0], dzbs[0]) + _dot_tn(qh[1], dzbs[1])
            dvt_ref[0, j] += _dot_tn(dyh[0], weights[0]) + _dot_tn(dyh[1], weights[1])
            return tuple(out)

        carry = (jnp.zeros((tq, 1), F32),) * 4
        carry = lax.fori_loop(_first_block(first_ref, ratio * i), ratio * i, lambda j, c: block(j, c, False), carry)
        for d in range(ratio):
            carry = block(ratio * i + d, carry, True)
        dq_ref[...] = jnp.where(first, acc_ref[0], acc_ref[1])

    return pl.pallas_call(
        body,
        grid=(N_PAIRS, sp["nq"]),
        in_specs=[sp["q"], sp["k_rows"], sp["k_rows"], sp["q"], sp["q"], sp["first"],
                  pl.BlockSpec((2 * tk, tk), lambda p, i: (0, 0)), pl.BlockSpec((tk, tk), lambda p, i: (0, 0))],
        out_specs=[sp["q"], sp["k_t"], sp["k_t"]],
        out_shape=[sp["wide"], sp["k_t_out"], sp["k_t_out"]],
        scratch_shapes=[sp["acc"]],
        compiler_params=_cparams(56),
        name="sb_bwd",
    )(qs, k3, v3, dy, rtot, first_block, lower_in, lower)


def _merge_fwd(x1, gates, y_fox, y_sb, w_bf, w_bs, w_out, tm=512):
    s_len = x1.shape[0]

    def body(x_ref, g_ref, yf_ref, ys_ref, wbf_ref, wbs_ref, wo_ref, o_ref):
        g = g_ref[...].astype(F32)
        of = _dot(yf_ref[...].astype(BF16), wbf_ref[...])
        os_ = _dot(ys_ref[...].astype(BF16), wbs_ref[...])
        merged = _sigmoid(g[:, 0:D_MODEL]) * of + _sigmoid(g[:, D_MODEL:]) * os_
        o_ref[...] = x_ref[...] + _dot(merged.astype(BF16), wo_ref[...])

    row = lambda i: (i, 0)
    full = lambda i: (0, 0)
    return pl.pallas_call(
        body,
        grid=(s_len // tm,),
        in_specs=[
            pl.BlockSpec((tm, D_MODEL), row),
            pl.BlockSpec((tm, 2 * D_MODEL), row),
            pl.BlockSpec((tm, ATT_W), row),
            pl.BlockSpec((tm, ATT_W), row),
            pl.BlockSpec((ATT_W, D_MODEL), full),
            pl.BlockSpec((ATT_W, D_MODEL), full),
            pl.BlockSpec((D_MODEL, D_MODEL), full),
        ],
        out_specs=pl.BlockSpec((tm, D_MODEL), row),
        out_shape=jax.ShapeDtypeStruct((s_len, D_MODEL), F32),
        compiler_params=_cparams(48),
        name="merge_fwd",
    )(x1, gates, y_fox, y_sb, w_bf, w_bs, w_out)


def _merge_bwd(dx2, gates, y_fox, y_sb, w_bf, w_bs, w_out, tm=512):
    s_len = dx2.shape[0]

    def body(d_ref, g_ref, yf_ref, ys_ref, wbf_ref, wbs_ref, wo_ref,
             dyf_ref, dys_ref, dg_ref, dof_ref, dos_ref, m_ref, dbf_ref):
        dbf = d_ref[...].astype(BF16)
        dbf_ref[...] = dbf
        dm = _dot_nt(dbf, wo_ref[...])
        g = g_ref[...].astype(F32)
        of = _dot(yf_ref[...].astype(BF16), wbf_ref[...])
        os_ = _dot(ys_ref[...].astype(BF16), wbs_ref[...])
        sf = _sigmoid(g[:, 0:D_MODEL])
        ss = _sigmoid(g[:, D_MODEL:])
        m_ref[...] = (sf * of + ss * os_).astype(BF16)
        d_of = (dm * sf).astype(BF16)
        d_os = (dm * ss).astype(BF16)
        dof_ref[...] = d_of
        dos_ref[...] = d_os
        dg_ref[:, 0:D_MODEL] = (dm * of * sf * (1.0 - sf)).astype(BF16)
        dg_ref[:, D_MODEL:] = (dm * os_ * ss * (1.0 - ss)).astype(BF16)
        dyf_ref[...] = _dot_nt(d_of, wbf_ref[...])
        dys_ref[...] = _dot_nt(d_os, wbs_ref[...])

    row = lambda i: (i, 0)
    full = lambda i: (0, 0)
    return pl.pallas_call(
        body,
        grid=(s_len // tm,),
        in_specs=[
            pl.BlockSpec((tm, D_MODEL), row),
            pl.BlockSpec((tm, 2 * D_MODEL), row),
            pl.BlockSpec((tm, ATT_W), row),
            pl.BlockSpec((tm, ATT_W), row),
            pl.BlockSpec((ATT_W, D_MODEL), full),
            pl.BlockSpec((ATT_W, D_MODEL), full),
            pl.BlockSpec((D_MODEL, D_MODEL), full),
        ],
        out_specs=[
            pl.BlockSpec((tm, ATT_W), row), pl.BlockSpec((tm, ATT_W), row),
            pl.BlockSpec((tm, 2 * D_MODEL), row),
            pl.BlockSpec((tm, D_MODEL), row), pl.BlockSpec((tm, D_MODEL), row),
            pl.BlockSpec((tm, D_MODEL), row), pl.BlockSpec((tm, D_MODEL), row),
        ],
        out_shape=[
            jax.ShapeDtypeStruct((s_len, ATT_W), F32), jax.ShapeDtypeStruct((s_len, ATT_W), F32),
            jax.ShapeDtypeStruct((s_len, 2 * D_MODEL), BF16),
            jax.ShapeDtypeStruct((s_len, D_MODEL), BF16), jax.ShapeDtypeStruct((s_len, D_MODEL), BF16),
            jax.ShapeDtypeStruct((s_len, D_MODEL), BF16), jax.ShapeDtypeStruct((s_len, D_MODEL), BF16),
        ],
        compiler_params=_cparams(56),
        name="merge_bwd",
    )(dx2, gates, y_fox, y_sb, w_bf, w_bs, w_out)


def _ple_loss(x3, p, g, w_pg, w_pp, target, tm=512):
    s_len = x3.shape[0]
    inv_d = 1.0 / D_MODEL

    def body(x_ref, p_ref, g_ref, wpg_ref, wpp_ref, t_ref,
             dx_ref, du_ref, dt_ref, hn_ref, dg_ref, loss_ref):
        @pl.when(pl.program_id(0) == 0)
        def _():
            dg_ref[...] = jnp.zeros_like(dg_ref)
            loss_ref[...] = jnp.zeros_like(loss_ref)

        x = x_ref[...]
        xn, r = _rms(x)
        gain = g_ref[...]
        hn = (xn * gain).astype(BF16)
        hn_ref[...] = hn
        sg = _sigmoid(_dot(hn, wpg_ref[...]))
        t = _dot(p_ref[...].astype(BF16), wpp_ref[...])
        err = x + sg * t - t_ref[...]
        sq = jnp.sum(_colsum(err * err), axis=-1, keepdims=True)
        loss_ref[...] += (0.5 * inv_d) * sq
        dy = err * inv_d
        du = (dy * t * sg * (1.0 - sg)).astype(BF16)
        du_ref[...] = du
        dt_ref[...] = (dy * sg).astype(BF16)
        dh = _dot_nt(du, wpg_ref[...])
        dx_ref[...] = dy + _rms_bwd(dh, xn, r, gain)
        dg_ref[0:1, :] += _colsum(dh * xn)

    row = lambda i: (i, 0)
    full = lambda i: (0, 0)
    bf = jax.ShapeDtypeStruct((s_len, D_MODEL), BF16)
    return pl.pallas_call(
        body,
        grid=(s_len // tm,),
        in_specs=[
            pl.BlockSpec((tm, D_MODEL), row),
            pl.BlockSpec((tm, PLE_DIM), row),
            pl.BlockSpec((1, D_MODEL), full),
            pl.BlockSpec((D_MODEL, D_MODEL), full),
            pl.BlockSpec((PLE_DIM, D_MODEL), full),
            pl.BlockSpec((tm, D_MODEL), row),
        ],
        out_specs=[
            pl.BlockSpec((tm, D_MODEL), row), pl.BlockSpec((tm, D_MODEL), row),
            pl.BlockSpec((tm, D_MODEL), row), pl.BlockSpec((tm, D_MODEL), row),
            pl.BlockSpec((8, D_MODEL), full), pl.BlockSpec((8, LANES), full),
        ],
        out_shape=[
            jax.ShapeDtypeStruct((s_len, D_MODEL), F32), bf, bf, bf,
            jax.ShapeDtypeStruct((8, D_MODEL), F32), jax.ShapeDtypeStruct((8, LANES), F32),
        ],
        compiler_params=_cparams(48),
        name="ple_loss",
    )(x3, p, g, w_pg, w_pp, target)


def _qknorm_bwd(fq, fk, dqs, dk, dv, qn, kn, bd, bd_t, tm=512):
    s_len = fq.shape[0]

    def body(fq_ref, fk_ref, dq_ref, dk_ref, dv_ref, qn_ref, kn_ref, bd_ref, bdt_ref,
             dz_ref, dqn_ref, dkn_ref):
        @pl.when(pl.program_id(0) == 0)
        def _():
            dqn_ref[...] = jnp.zeros_like(dqn_ref)
            dkn_ref[...] = jnp.zeros_like(dkn_ref)

        bd_m = bd_ref[...]
        bdt_m = bdt_ref[...]

        def one(x, dy, gain, dgain_ref):
            xn, rw = _head_rms(x, bd_m, bdt_m)
            dgain_ref[0:1, :] += _colsum(dy * xn)
            dxn = dy * gain
            return rw * (dxn - xn * _head_mean(dxn * xn, bd_m, bdt_m))

        dz_ref[:, 0:ATT_W] = one(fq_ref[...], dq_ref[...] * QK_SCALE, qn_ref[...], dqn_ref).astype(BF16)
        dz_ref[:, ATT_W:2 * ATT_W] = one(fk_ref[...], dk_ref[...], kn_ref[...], dkn_ref).astype(BF16)
        dz_ref[:, 2 * ATT_W:] = dv_ref[...].astype(BF16)

    row = lambda i: (i, 0)
    full = lambda i: (0, 0)
    att = pl.BlockSpec((tm, ATT_W), row)
    return pl.pallas_call(
        body,
        grid=(s_len // tm,),
        in_specs=[att, att, att, att, att,
                  pl.BlockSpec((1, ATT_W), full), pl.BlockSpec((1, ATT_W), full),
                  pl.BlockSpec((ATT_W, LANES), full), pl.BlockSpec((LANES, ATT_W), full)],
        out_specs=[pl.BlockSpec((tm, 3 * ATT_W), row), pl.BlockSpec((8, ATT_W), full), pl.BlockSpec((8, ATT_W), full)],
        out_shape=[jax.ShapeDtypeStruct((s_len, 3 * ATT_W), BF16),
                   jax.ShapeDtypeStruct((8, ATT_W), F32), jax.ShapeDtypeStruct((8, ATT_W), F32)],
        name="qknorm_bwd",
    )(fq, fk, dqs, dk, dv, qn, kn, bd, bd_t)


def _inproj_bwd(x1, dx2, g, dzf, dlogf, logf, dzs, dgates, w_fox, w_fl, w_sb, w_gates, tm=512):
    s_len = x1.shape[0]

    def body(x_ref, d_ref, g_ref, dzf_ref, dlf_ref, lf_ref, dzs_ref, dgt_ref, wf_ref, wl_ref, ws_ref, wg_ref,
             dx_ref, h_ref, dfl_ref, dg_ref, db_ref):
        @pl.when(pl.program_id(0) == 0)
        def _():
            dg_ref[...] = jnp.zeros_like(dg_ref)
            db_ref[...] = jnp.zeros_like(db_ref)

        xn, r = _rms(x_ref[...])
        gain = g_ref[...]
        h_ref[...] = (xn * gain).astype(BF16)
        lane = lax.broadcasted_iota(jnp.int32, (tm, LANES), 1)
        dfl = jnp.where(lane < N_HEADS, dlf_ref[...] * (1.0 - jnp.exp(lf_ref[...])), 0.0)
        db_ref[0:1, :] += _colsum(dfl)
        dflb = dfl.astype(BF16)
        dfl_ref[...] = dflb
        dh = (_dot_nt(dzf_ref[...], wf_ref[...]) + _dot_nt(dflb, wl_ref[...])
              + _dot_nt(dzs_ref[...], ws_ref[...]) + _dot_nt(dgt_ref[...], wg_ref[...]))
        dx_ref[...] = d_ref[...] + _rms_bwd(dh, xn, r, gain)
        dg_ref[0:1, :] += _colsum(dh * xn)

    row = lambda i: (i, 0)
    full = lambda i: (0, 0)
    return pl.pallas_call(
        body,
        grid=(s_len // tm,),
        in_specs=[
            pl.BlockSpec((tm, D_MODEL), row),
            pl.BlockSpec((tm, D_MODEL), row),
            pl.BlockSpec((1, D_MODEL), full),
            pl.BlockSpec((tm, 3 * ATT_W), row),
            pl.BlockSpec((tm, LANES), row),
            pl.BlockSpec((tm, LANES), row),
            pl.BlockSpec((tm, 3 * ATT_W), row),
            pl.BlockSpec((tm, 2 * D_MODEL), row),
            pl.BlockSpec((D_MODEL, 3 * ATT_W), full),
            pl.BlockSpec((D_MODEL, LANES), full),
            pl.BlockSpec((D_MODEL, 3 * ATT_W), full),
            pl.BlockSpec((D_MODEL, 2 * D_MODEL), full),
        ],
        out_specs=[
            pl.BlockSpec((tm, D_MODEL), row), pl.BlockSpec((tm, D_MODEL), row), pl.BlockSpec((tm, LANES), row),
            pl.BlockSpec((8, D_MODEL), full), pl.BlockSpec((8, LANES), full),
        ],
        out_shape=[
            jax.ShapeDtypeStruct((s_len, D_MODEL), F32), jax.ShapeDtypeStruct((s_len, D_MODEL), BF16),
            jax.ShapeDtypeStruct((s_len, LANES), BF16),
            jax.ShapeDtypeStruct((8, D_MODEL), F32), jax.ShapeDtypeStruct((8, LANES), F32),
        ],
        compiler_params=_cparams(56),
        name="inproj_bwd",
    )(x1, dx2, g, dzf, dlogf, logf, dzs, dgates, w_fox, w_fl, w_sb, w_gates)


def _split_w_in(w_in):
    o = 3 * ATT_W
    w_fox = w_in[:, 0:o]
    w_fl = jnp.pad(w_in[:, o:o + N_HEADS], ((0, 0), (0, LANES - N_HEADS)))
    w_sb = w_in[:, o + N_HEADS:2 * o + N_HEADS]
    w_gates = w_in[:, 2 * o + N_HEADS:]
    return w_fox, w_fl, w_sb, w_gates


def _local_grads(x, p, target, small, full, pending=None, send_early=None):
    blk = ATT_BLOCK
    bd, bd_t = _head_sum_matrices()
    full = dict(full)
    late = list(pending) if pending else []

    x1, a1, b1, u1, *gathered = _ffn_fwd(x, small["ffn1_norm"], full["ffn1_w_gate"], full["ffn1_w_up"],
                                     full["ffn1_w_down"], gather=[pending[k] for k in late])
    for k, gth in zip(late, gathered):
        full[k] = gth if k in KEPT_AS_SHARDS else _whole(k, gth)
    w_fox, w_fl, w_sb, w_gates = _split_w_in(full["w_in"])
    bias = jnp.pad(small["forget_bias"], ((0, 0), (0, LANES - N_HEADS)))
    qn = jnp.tile(small["q_norm"], (1, N_HEADS))
    kn = jnp.tile(small["k_norm"], (1, N_HEADS))
    fq, fk, f_qs, f_k, f_v, logf, s_qs, s_k, s_v, gates = _inproj_fwd(
        x1, small["mix_norm"], w_fox, w_fl, w_sb, w_gates, bias, qn, kn, bd, bd_t)
    f_cum = _cumsum_rows(logf, reverse=False)
    f8 = f_cum[:, 0:N_HEADS]
    fw = jnp.repeat(f8, HEAD_DIM, axis=1)
    ft4 = _pair_rows_t(f8, blk)
    f_k3, f_v3 = _blocked_rows(f_k, blk), _blocked_rows(f_v, blk)
    y_fox, lse, f_first = _fox_fwd(f_qs, f_k3, f_v3, fw, ft4, _key_norm_bound(f_k))
    s_k3, s_v3 = _blocked_rows(s_k, blk), _blocked_rows(s_v, blk)
    y_sb, s_rtot, s_first = _sb_fwd(s_qs, s_k3, s_v3)
    x2 = _merge_fwd(x1, gates, y_fox, y_sb, full["w_branch_fox"], full["w_branch_sb"], full["w_out"])
    x3, a2, b2, u2 = _ffn_fwd(x2, small["ffn2_norm"], full["ffn2_w_gate"], full["ffn2_w_up"], full["ffn2_w_down"])

    dx3, du_ple, dt_ple, hn_ple, dg_ple, loss_sum = _ple_loss(
        x3, p, small["ple_norm"], full["w_ple_gate"], full["w_ple_proj"], target)
    dx2, da2, db2, h_ffn2, d3_bf, dg_ffn2 = _ffn_bwd(
        x2, dx3, small["ffn2_norm"], a2, b2, full["ffn2_w_gate"], full["ffn2_w_up"], full["ffn2_w_down"])
    dy_fox, dy_sb, dgates, d_of, d_os, merged, d2_bf = _merge_bwd(
        dx2, gates, y_fox, y_sb, full["w_branch_fox"], full["w_branch_sb"], full["w_out"])

    f_dqs, dfq_p, f_dkt4, f_dvt4, dft4 = _fox_bwd(f_qs, f_k3, f_v3, dy_fox, y_fox, lse, fw, ft4, f_first)
    s_dqs, s_dkt4, s_dvt4 = _sb_bwd(s_qs, s_k3, s_v3, dy_sb, s_rtot, s_first)

    dzf, dqn8, dkn8 = _qknorm_bwd(fq, fk, f_dqs, _unblocked_t(f_dkt4), _unblocked_t(f_dvt4), qn, kn, bd, bd_t)
    dzs = jnp.concatenate([s_dqs * QK_SCALE, _unblocked_t(s_dkt4), _unblocked_t(s_dvt4)], axis=1).astype(BF16)
    df8 = _unpair_rows_t(dft4) + dfq_p[:, :, 0:2].transpose(1, 0, 2).reshape(-1, N_HEADS)
    dlogf = _cumsum_rows(jnp.pad(df8, ((0, 0), (0, LANES - N_HEADS))), reverse=True)
    dx1, h_mix, dfl, dg_mix, dbias8 = _inproj_bwd(
        x1, dx2, small["mix_norm"], dzf, dlogf, logf, dzs, dgates, w_fox, w_fl, w_sb, w_gates)

    one = lambda t: t[None]
    gw = {}
    gw["ffn2_w_gate"] = _wgrad(da2, one(h_ffn2), name="wgrad_ffn2_gate")
    gw["ffn2_w_up"] = _wgrad(db2, one(h_ffn2), name="wgrad_ffn2_up")
    gw["ffn2_w_down"] = _wgrad(u2, one(d3_bf), scale=0.5, name="wgrad_ffn2_down")
    g_fox = _wgrad(one(h_mix), one(dzf), name="wgrad_in_fox")[0]
    g_fl = _wgrad(one(h_mix), one(dfl), name="wgrad_in_forget")[0]
    g_sb = _wgrad(one(h_mix), one(dzs), name="wgrad_in_sb")[0]
    g_gt = _wgrad(one(h_mix), one(dgates), name="wgrad_in_gates")[0]
    gw["w_in"] = jnp.concatenate([g_fox, g_fl[:, 0:N_HEADS], g_sb, g_gt], axis=1)
    gw["w_branch_fox"] = _wgrad(one(y_fox), one(d_of), name="wgrad_branch_fox")[0]
    gw["w_branch_sb"] = _wgrad(one(y_sb), one(d_os), name="wgrad_branch_sb")[0]
    gw["w_out"] = _wgrad(one(merged), one(d2_bf), name="wgrad_out")[0]
    gw["w_ple_gate"] = _wgrad(one(hn_ple), one(du_ple), name="wgrad_ple_gate")[0]
    gw["w_ple_proj"] = _wgrad(one(p), one(dt_ple), name="wgrad_ple_proj")[0]

    gw["ffn1_w_down"] = _wgrad(u1, one(dx1), scale=0.5, name="wgrad_ffn1_down")

    sent_names, to_send = send_early(gw) if send_early else ([], [])
    grad_x, da1, db1, h_ffn1, _, dg_ffn1, *landed = _ffn_bwd(
        x, dx1, small["ffn1_norm"], a1, b1, full["ffn1_w_gate"], full["ffn1_w_up"], full["ffn1_w_down"],
        scatter=to_send)
    gw["ffn1_w_gate"] = _wgrad(da1, one(h_ffn1), name="wgrad_ffn1_gate")
    gw["ffn1_w_up"] = _wgrad(db1, one(h_ffn1), name="wgrad_ffn1_up")

    fold = lambda t: jnp.sum(t[0:1].reshape(N_HEADS, HEAD_DIM), axis=0, keepdims=True)
    gs = {
        "ffn1_norm": dg_ffn1[0:1], "mix_norm": dg_mix[0:1], "ffn2_norm": dg_ffn2[0:1], "ple_norm": dg_ple[0:1],
        "forget_bias": dbias8[0:1, 0:N_HEADS], "q_norm": fold(dqn8), "k_norm": fold(dkn8),
    }
    return loss_sum, grad_x, gw, gs, dict(zip(sent_names, landed))


def _position():
    return lax.axis_index("x"), lax.axis_index("y"), lax.axis_index("c")


def _other_chips(x, y):
    return [(1 - x, y), (x, 1 - y), (1 - x, 1 - y)]


ANY = pl.BlockSpec(memory_space=pl.ANY)


def _place_own_shard(w, q):
    rows, cols = w.shape
    tr = _row_block(rows, cols * 4, budget=2 * MIB)

    def body(q_ref, w_ref, o_ref):
        o_ref[0] = w_ref[...].astype(BF16)

    return pl.pallas_call(
        body,
        grid_spec=pltpu.PrefetchScalarGridSpec(
            num_scalar_prefetch=1,
            grid=(rows // tr,),
            in_specs=[pl.BlockSpec((tr, cols), lambda i, q_ref: (i, 0))],
            out_specs=pl.BlockSpec((1, tr, cols), lambda i, q_ref: (q_ref[0], i, 0)),
        ),
        out_shape=jax.ShapeDtypeStruct((N_CHIPS, rows, cols), BF16),
        name="place_own_shard",
    )(q, w)


def _gather_semaphores(n):
    return [pltpu.SemaphoreType.DMA((6 * n,)), pltpu.SemaphoreType.DMA((6 * n,))]


def _gather_steps(bufs, send_sems, recv_sems):
    n = len(bufs)
    x, y, c = _position()
    q = 2 * x + y
    chips = _other_chips(x, y)
    sibling = (x, y, 1 - c)

    def half(a, slot, which):
        r2 = bufs[a].shape[1] // 2
        return bufs[a].at[slot, pl.ds(which * r2, r2), :]

    def copy(a, k, region, to):
        return pltpu.make_async_remote_copy(
            src_ref=region, dst_ref=region, send_sem=send_sems.at[6 * a + k], recv_sem=recv_sems.at[6 * a + k],
            device_id=to, device_id_type=MESH)

    def to_chip(a, k):
        tx, ty = chips[k]
        return copy(a, k, half(a, q, c), (tx, ty, c))

    def to_sibling(a, k):
        tx, ty = chips[k]
        return copy(a, 3 + k, half(a, 2 * tx + ty, c), sibling)

    def start():
        for a in range(n):
            for k in range(3):
                to_chip(a, k).start()

    def finish():
        for a in range(n):
            for k, (tx, ty) in enumerate(chips):
                copy(a, k, half(a, 2 * tx + ty, c), (tx, ty, c)).wait_recv()
                to_sibling(a, k).start()
        for a in range(n):
            for k, (tx, ty) in enumerate(chips):
                copy(a, 3 + k, half(a, 2 * tx + ty, 1 - c), sibling).wait_recv()
        for a in range(n):
            for k in range(3):
                to_chip(a, k).wait_send()
                to_sibling(a, k).wait_send()

    return start, finish


def _allgather_weights(slots):
    n = len(slots)

    def body(*refs):
        start, finish = _gather_steps(refs[n:2 * n], *refs[2 * n:])
        start()
        finish()

    return pl.pallas_call(
        body,
        in_specs=[ANY] * n,
        out_specs=[ANY] * n,
        out_shape=[jax.ShapeDtypeStruct(s.shape, s.dtype) for s in slots],
        input_output_aliases={a: a for a in range(n)},
        scratch_shapes=_gather_semaphores(n),
        name="allgather_weights",
    )(*slots)


def _exchange_pair_halves(grads):
    n = len(grads)

    def body(*refs):
        ins, outs = refs[0:n], refs[n:2 * n]
        send_sems, recv_sems = refs[2 * n:]
        x, y, c = _position()
        copies = []
        for a in range(n):
            r2 = grads[a].shape[1] // 2
            cp = pltpu.make_async_remote_copy(
                src_ref=ins[a].at[:, pl.ds((1 - c) * r2, r2), :], dst_ref=outs[a],
                send_sem=send_sems.at[a], recv_sem=recv_sems.at[a], device_id=(x, y, 1 - c), device_id_type=MESH)
            cp.start()
            copies.append(cp)
        for cp in copies:
            cp.wait()

    return pl.pallas_call(
        body,
        in_specs=[ANY] * n,
        out_specs=[ANY] * n,
        out_shape=[jax.ShapeDtypeStruct((N_CHIPS, g.shape[1] // 2, g.shape[2]), g.dtype) for g in grads],
        scratch_shapes=[pltpu.SemaphoreType.DMA((n,)), pltpu.SemaphoreType.DMA((n,))],
        name="rs_pair_exchange",
    )(*grads)


def _scatter_semaphores(n):
    return [pltpu.SemaphoreType.DMA((3 * n,)), pltpu.SemaphoreType.DMA((3 * n,)), pltpu.SemaphoreType.DMA((n,))]


def _scatter_steps(ins, outs, send_sems, recv_sems, local_sems):
    n = len(ins)
    x, y, c = _position()
    q = 2 * x + y
    chips = _other_chips(x, y)

    def own(a):
        return pltpu.make_async_copy(ins[a].at[q], outs[a].at[q], local_sems.at[a])

    def to_chip(a, k):
        tx, ty = chips[k]
        return pltpu.make_async_remote_copy(
            src_ref=ins[a].at[2 * tx + ty], dst_ref=outs[a].at[q],
            send_sem=send_sems.at[3 * a + k], recv_sem=recv_sems.at[3 * a + k],
            device_id=(tx, ty, c), device_id_type=MESH)

    def start():
        for a in range(n):
            own(a).start()
            for k in range(3):
                to_chip(a, k).start()

    def finish():
        for a in range(n):
            own(a).wait()
            for k in range(3):
                to_chip(a, k).wait()

    return start, finish


def _scatter_to_owner_chips(pairs):
    n = len(pairs)

    def body(*refs):
        start, finish = _scatter_steps(refs[0:n], refs[n:2 * n], *refs[2 * n:])
        start()
        finish()

    return pl.pallas_call(
        body,
        in_specs=[ANY] * n,
        out_specs=[ANY] * n,
        out_shape=[jax.ShapeDtypeStruct(p.shape, p.dtype) for p in pairs],
        scratch_shapes=_scatter_semaphores(n),
        name="rs_scatter",
    )(*pairs)


def _join_halves(shards):
    n = len(shards)

    def body(*refs):
        bufs = refs[n:2 * n]
        send_sems, recv_sems = refs[2 * n:]
        x, y, c = _position()
        started = []
        for a in range(n):
            r2 = shards[a].shape[0] // 2
            mine = bufs[a].at[pl.ds(c * r2, r2), :]
            cp = pltpu.make_async_remote_copy(
                src_ref=mine, dst_ref=mine, send_sem=send_sems.at[a], recv_sem=recv_sems.at[a],
                device_id=(x, y, 1 - c), device_id_type=MESH)
            cp.start()
            started.append(cp)
        for cp in started:
            cp.wait()

    return pl.pallas_call(
        body,
        in_specs=[ANY] * n,
        out_specs=[ANY] * n,
        out_shape=[jax.ShapeDtypeStruct(t.shape, t.dtype) for t in shards],
        input_output_aliases={a: a for a in range(n)},
        scratch_shapes=[pltpu.SemaphoreType.DMA((n,)), pltpu.SemaphoreType.DMA((n,))],
        name="rs_join_halves",
    )(*shards)


def _add_pair(g, got, c):
    _, r2, cols = got.shape

    def body(c_ref, g_ref, got_ref, o_ref):
        o_ref[...] = (g_ref[...].astype(F32) + got_ref[...].astype(F32)).astype(BF16)

    spec = pl.BlockSpec((1, r2, cols), lambda s, c_ref: (s, 0, 0))
    return pl.pallas_call(
        body,
        grid_spec=pltpu.PrefetchScalarGridSpec(
            num_scalar_prefetch=1,
            grid=(N_CHIPS,),
            in_specs=[pl.BlockSpec((1, r2, cols), lambda s, c_ref: (s, c_ref[0], 0)), spec],
            out_specs=spec,
        ),
        out_shape=jax.ShapeDtypeStruct(got.shape, BF16),
        name="rs_add_pair",
    )(c, g, got)


def _add_chips(parts, c):
    _, r2, cols = parts.shape

    def body(c_ref, p0, p1, p2, p3, o_ref):
        o_ref[...] = ((p0[0].astype(F32) + p1[0].astype(F32)) + p2[0].astype(F32)) + p3[0].astype(F32)

    specs = [pl.BlockSpec((1, r2, cols), functools.partial(lambda i, c_ref, s: (s, 0, 0), s=s))
             for s in range(N_CHIPS)]
    return pl.pallas_call(
        body,
        grid_spec=pltpu.PrefetchScalarGridSpec(
            num_scalar_prefetch=1,
            grid=(1,),
            in_specs=specs,
            out_specs=pl.BlockSpec((r2, cols), lambda i, c_ref: (c_ref[0], 0)),
        ),
        out_shape=jax.ShapeDtypeStruct((2 * r2, cols), F32),
        name="rs_add_chips",
    )(c, parts, parts, parts, parts)


def _allreduce_small(part):
    shape = part.shape

    def body(in_ref, out_ref, gather_ref, send_sems, recv_sems):
        x, y, c = _position()
        me = 4 * x + 2 * y + c
        relations = [(a, b, d) for a in (0, 1) for b in (0, 1) for d in (0, 1)][1:]
        flip = lambda v, f: 1 - v if f else v
        copies = []
        for k, (a, b, d) in enumerate(relations):
            cp = pltpu.make_async_remote_copy(
                src_ref=in_ref, dst_ref=gather_ref.at[me], send_sem=send_sems.at[k], recv_sem=recv_sems.at[k],
                device_id=(flip(x, a), flip(y, b), flip(c, d)), device_id_type=MESH)
            cp.start()
            copies.append(cp)
        gather_ref[me] = in_ref[...]
        for cp in copies:
            cp.wait()
        total = gather_ref[0]
        for dev in range(1, 8):
            total = total + gather_ref[dev]
        out_ref[...] = total

    vmem = pl.BlockSpec(memory_space=pltpu.VMEM)
    return pl.pallas_call(
        body,
        in_specs=[vmem],
        out_specs=vmem,
        out_shape=jax.ShapeDtypeStruct(shape, F32),
        scratch_shapes=[pltpu.VMEM((8,) + shape, F32), pltpu.SemaphoreType.DMA((7,)), pltpu.SemaphoreType.DMA((7,))],
        name="allreduce_small",
    )(part)


def _adamw(w, g, m, v):
    rows, cols = w.shape
    tr = _row_block(rows, cols * 4, budget=MIB)
    c1 = 1.0 / (1.0 - ADAM_B1 ** ADAM_STEP)
    c2 = 1.0 / (1.0 - ADAM_B2 ** ADAM_STEP)

    def body(w_ref, g_ref, m_ref, v_ref, d_ref, nm_ref, nv_ref):
        g_ = g_ref[...]
        nm = ADAM_B1 * m_ref[...] + (1.0 - ADAM_B1) * g_
        nv = ADAM_B2 * v_ref[...] + (1.0 - ADAM_B2) * (g_ * g_)
        nm_ref[...] = nm
        nv_ref[...] = nv
        d_ref[...] = -ADAM_LR * ((nm * c1) / (jnp.sqrt(nv * c2) + ADAM_EPS) + ADAM_WD * w_ref[...])

    spec = pl.BlockSpec((tr, cols), lambda i: (i, 0))
    out = jax.ShapeDtypeStruct((rows, cols), F32)
    return pl.pallas_call(
        body,
        grid=(rows // tr,),
        in_specs=[spec] * 4,
        out_specs=[spec] * 3,
        out_shape=[out] * 3,
        name="adamw",
    )(w, g, m, v)


BIG = ["ffn1_w_gate", "ffn1_w_up", "ffn1_w_down", "w_in", "w_branch_fox", "w_branch_sb", "w_out",
       "ffn2_w_gate", "ffn2_w_up", "ffn2_w_down", "w_ple_gate", "w_ple_proj"]
SMALL = ["ffn1_norm", "mix_norm", "ffn2_norm", "ple_norm", "forget_bias", "q_norm", "k_norm"]
COLUMN_SHARDED = ["w_in", "w_branch_fox", "w_branch_sb", "w_ple_proj"]
KEPT_AS_SHARDS = ["ffn1_w_gate", "ffn1_w_up", "ffn1_w_down", "ffn2_w_gate", "ffn2_w_up", "ffn2_w_down"]
WORKED_TRANSPOSED = ["ffn1_w_gate", "ffn1_w_up", "ffn2_w_gate", "ffn2_w_up"]
NEEDED_FIRST = ["ffn1_w_gate", "ffn1_w_up", "ffn1_w_down"]
READY_LAST = ["ffn1_w_gate", "ffn1_w_up"]
ORDER = ["ffn1_norm", "ffn1_w_gate", "ffn1_w_up", "ffn1_w_down", "mix_norm", "w_in", "forget_bias", "q_norm",
         "k_norm", "w_branch_fox", "w_branch_sb", "w_out", "ffn2_norm", "ffn2_w_gate", "ffn2_w_up",
         "ffn2_w_down", "ple_norm", "w_ple_gate", "w_ple_proj"]
SMALL_ROWS = {"ffn1_norm": 0, "mix_norm": 1, "ffn2_norm": 2, "ple_norm": 3}
SMALL_COLS = {"forget_bias": (0, N_HEADS), "q_norm": (N_HEADS, HEAD_DIM), "k_norm": (N_HEADS + HEAD_DIM, HEAD_DIM)}
LOSS_ROW = 5


def _stored(name, a):
    return jnp.swapaxes(a[0], 0, 1) if name in WORKED_TRANSPOSED else a[0]


def _returned(name, t):
    return (jnp.swapaxes(t, 0, 1) if name in WORKED_TRANSPOSED else t)[None]


def _whole(name, gathered):
    if name in COLUMN_SHARDED:
        return jnp.concatenate([gathered[s] for s in range(N_CHIPS)], axis=1)
    return gathered.reshape(-1, gathered.shape[-1])


def _as_shards(name, whole):
    if name in COLUMN_SHARDED:
        k, n = whole.shape
        return whole.reshape(k, N_CHIPS, n // N_CHIPS).transpose(1, 0, 2)
    return whole.reshape(N_CHIPS, whole.shape[0] // N_CHIPS, whole.shape[1])


def _pack_small(values, extra=None):
    rows = [values[k] for k in ("ffn1_norm", "mix_norm", "ffn2_norm", "ple_norm")]
    tail = jnp.concatenate([values["forget_bias"], values["q_norm"], values["k_norm"]], axis=1)
    rows.append(jnp.pad(tail, ((0, 0), (0, D_MODEL - tail.shape[1]))))
    packed = jnp.concatenate(rows + [jnp.zeros((3, D_MODEL), F32)], axis=0)
    if extra is not None:
        packed = packed.at[LOSS_ROW, 0].set(extra)
    return packed


def _unpack_small(packed):
    out = {k: packed[r:r + 1] for k, r in SMALL_ROWS.items()}
    for k, (start, size) in SMALL_COLS.items():
        out[k] = packed[4:5, start:start + size]
    return out


def kernel(x, p, ffn1_norm, ffn1_w_gate, ffn1_w_up, ffn1_w_down, mix_norm, w_in, forget_bias, q_norm, k_norm, w_branch_fox, w_branch_sb, w_out, ffn2_norm, ffn2_w_gate, ffn2_w_up, ffn2_w_down, ple_norm, w_ple_gate, w_ple_proj, loss_target, m_ffn1_norm, m_ffn1_w_gate, m_ffn1_w_up, m_ffn1_w_down, m_mix_norm, m_w_in, m_forget_bias, m_q_norm, m_k_norm, m_w_branch_fox, m_w_branch_sb, m_w_out, m_ffn2_norm, m_ffn2_w_gate, m_ffn2_w_up, m_ffn2_w_down, m_ple_norm, m_w_ple_gate, m_w_ple_proj, v_ffn1_norm, v_ffn1_w_gate, v_ffn1_w_up, v_ffn1_w_down, v_mix_norm, v_w_in, v_forget_bias, v_q_norm, v_k_norm, v_w_branch_fox, v_w_branch_sb, v_w_out, v_ffn2_norm, v_ffn2_w_gate, v_ffn2_w_up, v_ffn2_w_down, v_ple_norm, v_w_ple_gate, v_w_ple_proj):
    args = dict(locals())
    weights = {k: args[k] for k in ORDER}
    moments_m = {k: args["m_" + k] for k in ORDER}
    moments_v = {k: args["v_" + k] for k in ORDER}

    c_idx = lax.axis_index("c").astype(jnp.int32).reshape(1)
    q_idx = (2 * lax.axis_index("x") + lax.axis_index("y")).astype(jnp.int32).reshape(1)
    own = {k: _place_own_shard(_stored(k, weights[k]), q_idx) for k in BIG}
    full = dict(zip(NEEDED_FIRST, _allgather_weights([own[k] for k in NEEDED_FIRST])))
    pending = {k: own[k] for k in BIG if k not in NEEDED_FIRST}
    small = {k: weights[k] for k in SMALL}

    def pair_sums(names, gw):
        slots = [gw[k] if k in KEPT_AS_SHARDS else _as_shards(k, gw[k]) for k in names]
        from_core = _exchange_pair_halves(slots)
        return [_add_pair(g, got, c_idx) for g, got in zip(slots, from_core)]

    early = [k for k in BIG if k not in READY_LAST]
    loss_sum, grad_x, gw, gs, parts = _local_grads(
        x[0], p[0, 0], loss_target[0], small, full, pending, lambda ready: (early, pair_sums(early, ready)))

    parts.update(zip(READY_LAST, _scatter_to_owner_chips(pair_sums(READY_LAST, gw))))
    grads_big = dict(zip(BIG, _join_halves([_add_chips(parts[k], c_idx) for k in BIG])))
    reduced = _allreduce_small(_pack_small(gs, extra=loss_sum[0, 0]))
    grads_small = _unpack_small(reduced)
    loss = reduced[LOSS_ROW, 0]

    grads, deltas, new_m, new_v = {}, {}, {}, {}
    for k in BIG:
        d, nm, nv = _adamw(_stored(k, weights[k]), grads_big[k], _stored(k, moments_m[k]), _stored(k, moments_v[k]))
        grads[k], deltas[k], new_m[k], new_v[k] = (_returned(k, t) for t in (grads_big[k], d, nm, nv))
    d_s, nm_s, nv_s = _adamw(_pack_small({k: weights[k] for k in SMALL}), reduced,
                             _pack_small({k: moments_m[k] for k in SMALL}),
                             _pack_small({k: moments_v[k] for k in SMALL}))
    for k in SMALL:
        grads[k] = grads_small[k]
    for name, packed in (("d", d_s), ("m", nm_s), ("v", nv_s)):
        target = {"d": deltas, "m": new_m, "v": new_v}[name]
        target.update(_unpack_small(packed))

    return (loss, grad_x[None], *[grads[k] for k in ORDER], *[deltas[k] for k in ORDER],
            *[new_m[k] for k in ORDER], *[new_v[k] for k in ORDER])
```

```python
import functools

import jax
import jax.numpy as jnp
from jax import lax
from jax.experimental import pallas as pl
from jax.experimental.pallas import tpu as pltpu

F32 = jnp.float32
BF16 = jnp.bfloat16

D_MODEL = 1024
D_FF = 2816
N_CHIPS = 4
FF_SHARD = D_FF // N_CHIPS
FFN_CHUNKS = 2
WGRAD_TOKENS = 4096
WGRAD_VMEM = 30 * 1024 * 1024
HEAD_DIM = 64
N_HEADS = 8
ATT_W = N_HEADS * HEAD_DIM
PAIR_W = 2 * HEAD_DIM
N_PAIRS = N_HEADS // 2
PLE_DIM = 256
IN_WIDTH = 3 * ATT_W + N_HEADS + 3 * ATT_W + 2 * D_MODEL
EPS = 1e-6
QK_SCALE = HEAD_DIM ** -0.5
LANES = 128
ATT_BLOCK = 256
FOX_Q_BLOCK = 512
SB_Q_BLOCK = 256
NEG_BIG = -1e30
EXP_UNDERFLOW = 110.0
MAX_REFERENCE_EXCESS = 40.0

ADAM_LR = 0.001
ADAM_B1 = 0.9
ADAM_B2 = 0.999
ADAM_EPS = 1e-08
ADAM_WD = 0.01
ADAM_STEP = 10

MESH = pl.DeviceIdType.MESH
MIB = 1024 * 1024


def _cparams(vmem_mib=48):
    return pltpu.CompilerParams(vmem_limit_bytes=vmem_mib * MIB)


def _dot(a, b):
    return jnp.dot(a, b, preferred_element_type=F32)


def _dot_tn(a, b):
    return lax.dot_general(a, b, (((0,), (0,)), ((), ())), preferred_element_type=F32)


def _dot_nt(a, b):
    return lax.dot_general(a, b, (((1,), (1,)), ((), ())), preferred_element_type=F32)


def _sigmoid(x):
    return 1.0 / (1.0 + jnp.exp(-x))


def _split2(x):
    hi = x.astype(BF16)
    lo = (x - hi.astype(F32)).astype(BF16)
    return hi, lo


def _dot_split2(x, m):
    hi, lo = _split2(x)
    return _dot(hi, m) + _dot(lo, m)


def _split3(x):
    hi = x.astype(BF16)
    rest = x - hi.astype(F32)
    mid = rest.astype(BF16)
    lo = (rest - mid.astype(F32)).astype(BF16)
    return hi, mid, lo


def _rms(x):
    r = lax.rsqrt(jnp.mean(x * x, axis=-1, keepdims=True) + EPS)
    return x * r, r


def _rms_bwd(dh, xn, r, g):
    dxn = dh * g
    return r * (dxn - xn * jnp.mean(dxn * xn, axis=-1, keepdims=True))


def _colsum(x):
    return jnp.sum(x, axis=0, keepdims=True)


def _row_block(rows, row_bytes, budget):
    best = None
    for t in range(8, rows + 1, 8):
        if rows % t == 0 and t * row_bytes <= budget:
            best = t
    return best if best is not None else rows


def _ffn_fwd(x, g, wg, wu, wd, gather=(), tm=1024):
    s_len = x.shape[0]
    n = len(gather)
    steps = s_len // tm

    def body(x_ref, g_ref, wg_ref, wu_ref, wd_ref, *rest):
        o_ref, a_ref, b_ref, u_ref = rest[n:n + 4]
        h_s, acc_s = rest[2 * n + 4:2 * n + 6]
        i = pl.program_id(0)
        j = pl.program_id(1)
        if n:
            start, finish = _gather_steps(rest[n + 4:2 * n + 4], *rest[2 * n + 6:])
            pl.when((i == 0) & (j == 0))(start)

        @pl.when(j == 0)
        def _():
            xn, _ = _rms(x_ref[...])
            h_s[...] = (xn * g_ref[...]).astype(BF16)
            acc_s[...] = jnp.zeros_like(acc_s)

        chunks = [pl.ds(r * (tm // FFN_CHUNKS), tm // FFN_CHUNKS) for r in range(FFN_CHUNKS)]
        pre = [(_dot_nt(h_s[rows, :], wg_ref[0]), _dot_nt(h_s[rows, :], wu_ref[0])) for rows in chunks]
        us = []
        for rows, (a, b) in zip(chunks, pre):
            a_ref[0, rows, :] = a.astype(BF16)
            b_ref[0, rows, :] = b.astype(BF16)
            u = (a * _sigmoid(a) * b).astype(BF16)
            u_ref[0, rows, :] = u
            us.append(u)
        for rows, u in zip(chunks, us):
            acc_s[rows, :] += _dot(u, wd_ref[0])

        @pl.when(j == N_CHIPS - 1)
        def _():
            o_ref[...] = x_ref[...] + 0.5 * acc_s[...]

        if n:
            pl.when((i == steps - 1) & (j == N_CHIPS - 1))(finish)

    return pl.pallas_call(
        body,
        grid=(steps, N_CHIPS),
        in_specs=[
            pl.BlockSpec((tm, D_MODEL), lambda i, j: (i, 0)),
            pl.BlockSpec((1, D_MODEL), lambda i, j: (0, 0)),
            pl.BlockSpec((1, FF_SHARD, D_MODEL), lambda i, j: (j, 0, 0)),
            pl.BlockSpec((1, FF_SHARD, D_MODEL), lambda i, j: (j, 0, 0)),
            pl.BlockSpec((1, FF_SHARD, D_MODEL), lambda i, j: (j, 0, 0)),
        ] + [ANY] * n,
        out_specs=[pl.BlockSpec((tm, D_MODEL), lambda i, j: (i, 0))]
        + [pl.BlockSpec((1, tm, FF_SHARD), lambda i, j: (j, i, 0))] * 3 + [ANY] * n,
        out_shape=[jax.ShapeDtypeStruct((s_len, D_MODEL), F32)]
        + [jax.ShapeDtypeStruct((N_CHIPS, s_len, FF_SHARD), BF16)] * 3
        + [jax.ShapeDtypeStruct(s.shape, s.dtype) for s in gather],
        input_output_aliases={5 + a: 4 + a for a in range(n)},
        scratch_shapes=[pltpu.VMEM((tm, D_MODEL), BF16), pltpu.VMEM((tm, D_MODEL), F32)]
        + (_gather_semaphores(n) if n else []),
        compiler_params=_cparams(56),
        name="ffn_fwd_gathering" if n else "ffn_fwd",
    )(x, g, wg, wu, wd, *gather)


def _ffn_bwd(x, d, g, a_pre, b_pre, wg, wu, wd, scatter=(), tm=512):
    s_len = x.shape[0]
    nb = s_len // tm
    n = len(scatter)

    def body(x_ref, d_ref, g_ref, a_ref, b_ref, wg_ref, wu_ref, wd_ref, *rest):
        dx_ref, da_ref, db_ref, h_ref, dbf_ref, dg_ref = rest[n:n + 6]
        dbf_s, dh_s = rest[2 * n + 6:2 * n + 8]
        i = pl.program_id(0)
        j = pl.program_id(1)
        if n:
            start, finish = _scatter_steps(rest[0:n], rest[n + 6:2 * n + 6], *rest[2 * n + 8:])
            pl.when((i == 0) & (j == 0))(start)

        @pl.when(j == 0)
        def _():
            xn, _ = _rms(x_ref[...])
            h_ref[...] = (xn * g_ref[...]).astype(BF16)
            dbf = d_ref[...].astype(BF16)
            dbf_s[...] = dbf
            dbf_ref[...] = dbf
            dh_s[...] = jnp.zeros_like(dh_s)

        @pl.when((i == 0) & (j == 0))
        def _():
            dg_ref[...] = jnp.zeros_like(dg_ref)

        chunks = [pl.ds(r * (tm // FFN_CHUNKS), tm // FFN_CHUNKS) for r in range(FFN_CHUNKS)]
        dus = [0.5 * _dot_nt(dbf_s[rows, :], wd_ref[0]) for rows in chunks]
        das, dbs = [], []
        for rows, du in zip(chunks, dus):
            a = a_ref[0, rows, :].astype(F32)
            b = b_ref[0, rows, :].astype(F32)
            s = _sigmoid(a)
            silu = a * s
            da = (du * b * (s * (1.0 + a * (1.0 - s)))).astype(BF16)
            db = (du * silu).astype(BF16)
            da_ref[0, rows, :] = da
            db_ref[0, rows, :] = db
            das.append(da)
            dbs.append(db)
        for rows, da, db in zip(chunks, das, dbs):
            dh_s[rows, :] += _dot(da, wg_ref[0]) + _dot(db, wu_ref[0])

        @pl.when(j == N_CHIPS - 1)
        def _():
            xn, r = _rms(x_ref[...])
            dh = dh_s[...]
            dx_ref[...] = d_ref[...] + _rms_bwd(dh, xn, r, g_ref[...])
            dg_ref[0:1, :] += _colsum(dh * xn)

        if n:
            pl.when((i == nb - 1) & (j == N_CHIPS - 1))(finish)

    row = lambda i, j: (i, 0)
    shard = lambda i, j: (j, 0, 0)
    act = lambda i, j: (j, i, 0)
    return pl.pallas_call(
        body,
        grid=(nb, N_CHIPS),
        in_specs=[
            pl.BlockSpec((tm, D_MODEL), row),
            pl.BlockSpec((tm, D_MODEL), row),
            pl.BlockSpec((1, D_MODEL), lambda i, j: (0, 0)),
            pl.BlockSpec((1, tm, FF_SHARD), act),
            pl.BlockSpec((1, tm, FF_SHARD), act),
            pl.BlockSpec((1, FF_SHARD, D_MODEL), shard),
            pl.BlockSpec((1, FF_SHARD, D_MODEL), shard),
            pl.BlockSpec((1, FF_SHARD, D_MODEL), shard),
        ] + [ANY] * n,
        out_specs=[
            pl.BlockSpec((tm, D_MODEL), row),
            pl.BlockSpec((1, tm, FF_SHARD), act),
            pl.BlockSpec((1, tm, FF_SHARD), act),
            pl.BlockSpec((tm, D_MODEL), row),
            pl.BlockSpec((tm, D_MODEL), row),
            pl.BlockSpec((8, D_MODEL), lambda i, j: (0, 0)),
        ] + [ANY] * n,
        out_shape=[
            jax.ShapeDtypeStruct((s_len, D_MODEL), F32),
            jax.ShapeDtypeStruct((N_CHIPS, s_len, FF_SHARD), BF16),
            jax.ShapeDtypeStruct((N_CHIPS, s_len, FF_SHARD), BF16),
            jax.ShapeDtypeStruct((s_len, D_MODEL), BF16),
            jax.ShapeDtypeStruct((s_len, D_MODEL), BF16),
            jax.ShapeDtypeStruct((8, D_MODEL), F32),
        ] + [jax.ShapeDtypeStruct(s.shape, s.dtype) for s in scatter],
        scratch_shapes=[
            pltpu.VMEM((tm, D_MODEL), BF16),
            pltpu.VMEM((tm, D_MODEL), F32),
        ] + (_scatter_semaphores(n) if n else []),
        compiler_params=_cparams(56),
        name="ffn_bwd_scattering" if n else "ffn_bwd",
    )(x, d, g, a_pre, b_pre, wg, wu, wd, *scatter)


def _wgrad(a, b, scale=1.0, name="wgrad"):
    na, s_len, k_dim = a.shape
    nb, _, n_dim = b.shape
    n = max(na, nb)
    ts = WGRAD_TOKENS
    while ts > 512 and (ts > s_len or 2 * ts * (k_dim * a.dtype.itemsize + n_dim * b.dtype.itemsize) > WGRAD_VMEM):
        ts //= 2
    steps = s_len // ts

    def body(a_ref, b_ref, o_ref, acc_s):
        s = pl.program_id(1)

        @pl.when(s == 0)
        def _():
            acc_s[...] = jnp.zeros_like(acc_s)

        acc_s[...] += _dot_tn(a_ref[0].astype(BF16), b_ref[0].astype(BF16))

        @pl.when(s == steps - 1)
        def _():
            o_ref[0] = (acc_s[...] * scale).astype(BF16)

    a_map = (lambda m, s: (m, s, 0)) if na > 1 else (lambda m, s: (0, s, 0))
    b_map = (lambda m, s: (m, s, 0)) if nb > 1 else (lambda m, s: (0, s, 0))
    return pl.pallas_call(
        body,
        grid=(n, steps),
        in_specs=[pl.BlockSpec((1, ts, k_dim), a_map), pl.BlockSpec((1, ts, n_dim), b_map)],
        out_specs=pl.BlockSpec((1, k_dim, n_dim), lambda m, s: (m, 0, 0)),
        out_shape=jax.ShapeDtypeStruct((n, k_dim, n_dim), BF16),
        scratch_shapes=[pltpu.VMEM((k_dim, n_dim), F32)],
        compiler_params=_cparams(56),
        name=name,
    )(a, b)


def _head_sum_matrices():
    lane = lax.broadcasted_iota(jnp.int32, (ATT_W, LANES), 0) // HEAD_DIM
    col = lax.broadcasted_iota(jnp.int32, (ATT_W, LANES), 1)
    bd = (lane == col).astype(BF16)
    return bd, bd.T


def _head_mean(t, bd, bd_t):
    per_head = _dot_split2(t, bd) * (1.0 / HEAD_DIM)
    return _dot_split2(per_head, bd_t)


def _head_rms(x, bd, bd_t):
    per_head = _dot_split2(x * x, bd) * (1.0 / HEAD_DIM)
    r = lax.rsqrt(per_head + EPS)
    rw = _dot_split2(r, bd_t)
    return x * rw, rw


def _log_sigmoid(z):
    return jnp.minimum(z, 0.0) - jnp.log(1.0 + jnp.exp(-jnp.abs(z)))


def _inproj_fwd(x1, g, w_fox, w_fl, w_sb, w_gates, bias, qn, kn, bd, bd_t, tm=512):
    s_len = x1.shape[0]

    def body(x_ref, g_ref, wf_ref, wl_ref, ws_ref, wg_ref, bias_ref, qn_ref, kn_ref, bd_ref, bdt_ref,
             fq_ref, fk_ref, qs_ref, kf_ref, vf_ref, logf_ref, sq_ref, sk_ref, sv_ref, gates_ref):
        xn, _ = _rms(x_ref[...])
        h = (xn * g_ref[...]).astype(BF16)
        zf = _dot(h, wf_ref[...])
        fq = zf[:, 0:ATT_W]
        fk = zf[:, ATT_W:2 * ATT_W]
        fq_ref[...] = fq
        fk_ref[...] = fk
        bd_m = bd_ref[...]
        bdt_m = bdt_ref[...]
        fqn, _ = _head_rms(fq, bd_m, bdt_m)
        fkn, _ = _head_rms(fk, bd_m, bdt_m)
        qs_ref[...] = (fqn * qn_ref[...]).astype(BF16) * QK_SCALE
        kf_ref[...] = (fkn * kn_ref[...]).astype(BF16)
        vf_ref[...] = zf[:, 2 * ATT_W:3 * ATT_W].astype(BF16)
        logf_ref[...] = _log_sigmoid(_dot(h, wl_ref[...]) + bias_ref[...])
        zs = _dot(h, ws_ref[...])
        sq_ref[...] = zs[:, 0:ATT_W].astype(BF16) * QK_SCALE
        sk_ref[...] = zs[:, ATT_W:2 * ATT_W].astype(BF16)
        sv_ref[...] = zs[:, 2 * ATT_W:3 * ATT_W].astype(BF16)
        gates_ref[...] = _dot(h, wg_ref[...]).astype(BF16)

    row = lambda i: (i, 0)
    full = lambda i: (0, 0)
    att = lambda dt: jax.ShapeDtypeStruct((s_len, ATT_W), dt)
    return pl.pallas_call(
        body,
        grid=(s_len // tm,),
        in_specs=[
            pl.BlockSpec((tm, D_MODEL), row),
            pl.BlockSpec((1, D_MODEL), full),
            pl.BlockSpec((D_MODEL, 3 * ATT_W), full),
            pl.BlockSpec((D_MODEL, LANES), full),
            pl.BlockSpec((D_MODEL, 3 * ATT_W), full),
            pl.BlockSpec((D_MODEL, 2 * D_MODEL), full),
            pl.BlockSpec((1, LANES), full),
            pl.BlockSpec((1, ATT_W), full),
            pl.BlockSpec((1, ATT_W), full),
            pl.BlockSpec((ATT_W, LANES), full),
            pl.BlockSpec((LANES, ATT_W), full),
        ],
        out_specs=[
            pl.BlockSpec((tm, ATT_W), row), pl.BlockSpec((tm, ATT_W), row),
            pl.BlockSpec((tm, ATT_W), row), pl.BlockSpec((tm, ATT_W), row), pl.BlockSpec((tm, ATT_W), row),
            pl.BlockSpec((tm, LANES), row),
            pl.BlockSpec((tm, ATT_W), row), pl.BlockSpec((tm, ATT_W), row), pl.BlockSpec((tm, ATT_W), row),
            pl.BlockSpec((tm, 2 * D_MODEL), row),
        ],
        out_shape=[
            att(F32), att(F32), att(BF16), att(BF16), att(BF16),
            jax.ShapeDtypeStruct((s_len, LANES), F32),
            att(BF16), att(BF16), att(BF16),
            jax.ShapeDtypeStruct((s_len, 2 * D_MODEL), BF16),
        ],
        compiler_params=_cparams(56),
        name="inproj_fwd",
    )(x1, g, w_fox, w_fl, w_sb, w_gates, bias, qn, kn, bd, bd_t)


def _tri(n, kind):
    r = lax.broadcasted_iota(jnp.int32, (n, n), 0)
    c = lax.broadcasted_iota(jnp.int32, (n, n), 1)
    m = {"row_ge_col": r >= c, "row_le_col": r <= c, "row_gt_col": r > c, "row_lt_col": r < c}[kind]
    return m.astype(BF16)


def _cumsum_rows(x, reverse, tm=256):
    s_len = x.shape[0]
    nb = s_len // tm
    tri = _tri(tm, "row_le_col" if reverse else "row_ge_col")
    edge = 0 if reverse else tm - 1

    def body(x_ref, tri_ref, o_ref, carry_s):
        @pl.when(pl.program_id(0) == 0)
        def _():
            carry_s[...] = jnp.zeros_like(carry_s)

        hi, mid, lo = _split3(x_ref[...])
        t = tri_ref[...]
        y = _dot(t, hi) + _dot(t, mid) + _dot(t, lo) + carry_s[...]
        o_ref[...] = y
        carry_s[...] = y[edge:edge + 1, :]

    order = (lambda i: (nb - 1 - i, 0)) if reverse else (lambda i: (i, 0))
    return pl.pallas_call(
        body,
        grid=(nb,),
        in_specs=[pl.BlockSpec((tm, LANES), order), pl.BlockSpec((tm, tm), lambda i: (0, 0))],
        out_specs=pl.BlockSpec((tm, LANES), order),
        out_shape=jax.ShapeDtypeStruct((s_len, LANES), F32),
        scratch_shapes=[pltpu.VMEM((1, LANES), F32)],
        name="cumsum_rev" if reverse else "cumsum_fwd",
    )(x, tri)


def _unblocked_t(t4):
    _, nb, _, blk = t4.shape
    return t4.transpose(1, 3, 0, 2).reshape(nb * blk, ATT_W)


def _blocked_rows(t, blk):
    return t.reshape(t.shape[0] // blk, blk, t.shape[1])


def _pair_rows_t(f8, blk):
    nb = f8.shape[0] // blk
    t = f8.reshape(nb, blk, N_PAIRS, 2).transpose(2, 0, 3, 1)
    return jnp.pad(t, ((0, 0), (0, 0), (0, 6), (0, 0)))


def _unpair_rows_t(t4):
    _, nb, _, blk = t4.shape
    return t4[:, :, 0:2, :].transpose(1, 3, 0, 2).reshape(nb * blk, N_HEADS)


def _head_masks(tq):
    lane = lax.broadcasted_iota(jnp.int32, (tq, PAIR_W), 1)
    return lane < HEAD_DIM


def _causal_mask(tq, tk, offset, strict):
    d = lax.broadcasted_iota(jnp.int32, (tq, tk), 1) - lax.broadcasted_iota(jnp.int32, (tq, tk), 0)
    return (d < offset) if strict else (d <= offset)


def _heads_of(ref, first):
    t = ref[...]
    zero = jnp.zeros_like(t)
    return [jnp.where(first, t, zero), jnp.where(first, zero, t)]


def _head_cols(ref):
    t = ref[...]
    return [t[:, 0:1], t[:, HEAD_DIM:HEAD_DIM + 1]]


def _att_specs(s_len, tq):
    tk = ATT_BLOCK
    nq, nk = s_len // tq, s_len // tk
    return dict(
        nq=nq,
        q=pl.BlockSpec((tq, PAIR_W), lambda p, i: (i, p)),
        k_t=pl.BlockSpec((1, nk, PAIR_W, tk), lambda p, i: (p, 0, 0, 0)),
        k_rows=pl.BlockSpec((nk, tk, PAIR_W), lambda p, i: (0, 0, p)),
        f_t=pl.BlockSpec((1, nk, 8, tk), lambda p, i: (p, 0, 0, 0)),
        first=pl.BlockSpec((1, 1, 8, LANES), lambda p, i: (p, i, 0, 0)),
        wide=jax.ShapeDtypeStruct((s_len, ATT_W), F32),
        k_t_out=jax.ShapeDtypeStruct((N_PAIRS, nk, PAIR_W, tk), F32),
        f_t_out=jax.ShapeDtypeStruct((N_PAIRS, nk, 8, tk), F32),
        first_out=jax.ShapeDtypeStruct((N_PAIRS, nq, 8, LANES), F32),
        acc=pltpu.VMEM((2, tq, PAIR_W), F32),
    )


def _first_block(first_ref, limit):
    return jnp.clip(jnp.max(first_ref[0, 0]).astype(jnp.int32), 0, limit)


def _key_norm_bound(k):
    sq = jnp.sum(jnp.square(k.astype(F32)).reshape(k.shape[0], N_HEADS, HEAD_DIM), axis=-1)
    bound = jnp.sqrt(jnp.max(sq, axis=0)).reshape(N_PAIRS, 2)
    return jnp.broadcast_to(jnp.pad(bound, ((0, 0), (0, 6)))[:, :, None], (N_PAIRS, 8, LANES))


def _fox_fwd(qs, k3, v3, fw, ft4, kmax):
    tq, tk = FOX_Q_BLOCK, ATT_BLOCK
    sp = _att_specs(qs.shape[0], tq)
    ratio = tq // tk

    def body(q_ref, k_ref, v_ref, fw_ref, ft_ref, kmax_ref, y_ref, lse_ref, first_ref, acc_ref, max_ref, sum_ref):
        i = pl.program_id(1)
        first = _head_masks(tq)
        qh = _heads_of(q_ref, first)
        fqh = _head_cols(fw_ref)
        acc_ref[...] = jnp.zeros_like(acc_ref)
        sum_ref[...] = jnp.zeros_like(sum_ref)
        max_ref[...] = jnp.full(max_ref.shape, NEG_BIG, F32)
        reach = []
        for n in range(2):
            qf = qh[n].astype(F32)
            reach.append(jnp.sqrt(jnp.sum(qf * qf, axis=-1, keepdims=True)) * kmax_ref[0, n:n + 1, 0:1] + fqh[n])

        def logits(j, shift, diag):
            k, fk = k_ref[j], ft_ref[0, j]
            raw = [_dot_nt(qh[n], k) for n in range(2)]
            out = []
            for n in range(2):
                s = raw[n] + (shift[n] - fk[n:n + 1, :])
                if diag:
                    s = jnp.where(_causal_mask(tq, tk, i * tq - j * tk, strict=False), s, NEG_BIG)
                out.append(s)
            return out

        def max_pass(j, diag):
            ss = logits(j, fqh, diag)
            for n in range(2):
                max_ref[n] = jnp.maximum(max_ref[n], ss[n])

        def sum_pass(j, shift, diag):
            ps = [jnp.exp(s) for s in logits(j, shift, diag)]
            v = v_ref[j]
            for n in range(2):
                sum_ref[n] += ps[n]
            for n in range(2):
                acc_ref[n] += _dot(ps[n].astype(BF16), v)

        for d in range(ratio):
            max_pass(ratio * i + d, True)

        m_diag = [jnp.max(max_ref[n], axis=-1, keepdims=True) for n in range(2)]
        slack = [jnp.max(reach[n] - m_diag[n]) for n in range(2)]

        def f_end(j):
            return ft_ref[0, jnp.maximum(j, 0)][:, tk - 1:tk]

        def block_matters(j):
            gap = [slack[n] - jnp.max(f_end(j)[n:n + 1]) for n in range(2)]
            return (j >= 0) & (jnp.maximum(gap[0], gap[1]) > -EXP_UNDERFLOW)

        last_left = ratio * i - 1
        j_first = lax.while_loop(block_matters, lambda j: j - 1, last_left) + 1

        bound = [reach[n] - f_end(last_left)[n:n + 1] for n in range(2)]
        excess = jnp.maximum(jnp.max(bound[0] - m_diag[0]), jnp.max(bound[1] - m_diag[1]))
        exact = excess > MAX_REFERENCE_EXCESS

        def one_max(j, c):
            max_pass(j, False)
            return c
        lax.fori_loop(jnp.where(exact, j_first, ratio * i), ratio * i, one_max, 0)
        m_seen = [jnp.max(max_ref[n], axis=-1, keepdims=True) for n in range(2)]
        bounded = jnp.logical_not(exact) & (j_first < ratio * i)
        m = [jnp.where(bounded, jnp.maximum(m_diag[n], bound[n]), m_seen[n]) for n in range(2)]
        shift = [fqh[n] - m[n] for n in range(2)]

        def one(j, c):
            sum_pass(j, shift, False)
            return c
        lax.fori_loop(j_first, ratio * i, one, 0)
        for d in range(ratio):
            sum_pass(ratio * i + d, shift, True)
        l = [jnp.sum(sum_ref[n], axis=-1, keepdims=True) for n in range(2)]
        y_ref[...] = jnp.where(first, acc_ref[0] / l[0], acc_ref[1] / l[1])
        lse_ref[...] = jnp.where(first, m[0] + jnp.log(l[0]), m[1] + jnp.log(l[1]))
        first_ref[...] = jnp.ones(first_ref.shape, F32) * j_first.astype(F32)

    tile = pltpu.VMEM((2, tq, tk), F32)
    return pl.pallas_call(
        body,
        grid=(N_PAIRS, sp["nq"]),
        in_specs=[sp["q"], sp["k_rows"], sp["k_rows"], sp["q"], sp["f_t"],
                  pl.BlockSpec((1, 8, LANES), lambda p, i: (p, 0, 0))],
        out_specs=[sp["q"], sp["q"], sp["first"]],
        out_shape=[sp["wide"], sp["wide"], sp["first_out"]],
        scratch_shapes=[sp["acc"], tile, tile],
        compiler_params=_cparams(56),
        name="fox_fwd",
    )(qs, k3, v3, fw, ft4, kmax)


def _fox_bwd(qs, k3, v3, dy, y, lse, fw, ft4, first_block):
    tq, tk = FOX_Q_BLOCK, ATT_BLOCK
    sp = _att_specs(qs.shape[0], tq)
    ratio = tq // tk

    def body(q_ref, k_ref, v_ref, dy_ref, y_ref, lse_ref, fw_ref, ft_ref, first_ref,
             dq_ref, dfq_ref, dkt_ref, dvt_ref, dft_ref, acc_ref):
        i = pl.program_id(1)

        @pl.when(i == 0)
        def _():
            dkt_ref[...] = jnp.zeros_like(dkt_ref)
            dvt_ref[...] = jnp.zeros_like(dvt_ref)
            dft_ref[...] = jnp.zeros_like(dft_ref)

        first = _head_masks(tq)
        qh = _heads_of(q_ref, first)
        dyv = dy_ref[...]
        dyb = dyv.astype(BF16)
        zero = jnp.zeros_like(dyb)
        dyh = [jnp.where(first, dyb, zero), jnp.where(first, zero, dyb)]
        prod = dyv * y_ref[...]
        zf = jnp.zeros_like(prod)
        delta = [jnp.sum(jnp.where(first, prod, zf), axis=-1, keepdims=True),
                 jnp.sum(jnp.where(first, zf, prod), axis=-1, keepdims=True)]
        fqh = _head_cols(fw_ref)
        lseh = _head_cols(lse_ref)
        shift = [fqh[n] - lseh[n] for n in range(2)]
        acc_ref[...] = jnp.zeros_like(acc_ref)

        def block(j, rows, diag):
            mask = _causal_mask(tq, tk, i * tq - j * tk, strict=False) if diag else None
            k, v, fk = k_ref[j], v_ref[j], ft_ref[0, j]
            logits = [_dot_nt(qh[n], k) for n in range(2)]
            dps = [_dot_nt(dyh[n], v) for n in range(2)]
            pbs, dsbs, out = [], [], []
            for n in range(2):
                p = jnp.exp(logits[n] + (shift[n] - fk[n:n + 1, :]))
                if diag:
                    p = jnp.where(mask, p, 0.0)
                ds = p * (dps[n] - delta[n])
                pbs.append(p.astype(BF16))
                dsbs.append(ds.astype(BF16))
                out.append(rows[n] + jnp.sum(ds, axis=-1, keepdims=True))
                dft_ref[0, j, n:n + 1, :] -= _colsum(ds)
            for n in range(2):
                acc_ref[n] += _dot(dsbs[n], k)
            dkt_ref[0, j] += _dot_tn(qh[0], dsbs[0]) + _dot_tn(qh[1], dsbs[1])
            dvt_ref[0, j] += _dot_tn(dyh[0], pbs[0]) + _dot_tn(dyh[1], pbs[1])
            return tuple(out)

        rows = (jnp.zeros((tq, 1), F32),) * 2
        rows = lax.fori_loop(_first_block(first_ref, ratio * i), ratio * i, lambda j, c: block(j, c, False), rows)
        for d in range(ratio):
            rows = block(ratio * i + d, rows, True)
        dq_ref[...] = jnp.where(first, acc_ref[0], acc_ref[1])
        lane = lax.broadcasted_iota(jnp.int32, (tq, 8), 1)
        dfq_ref[0] = jnp.where(lane == 0, rows[0], jnp.where(lane == 1, rows[1], 0.0))

    return pl.pallas_call(
        body,
        grid=(N_PAIRS, sp["nq"]),
        in_specs=[sp["q"], sp["k_rows"], sp["k_rows"], sp["q"], sp["q"], sp["q"], sp["q"], sp["f_t"], sp["first"]],
        out_specs=[sp["q"], pl.BlockSpec((1, tq, 8), lambda p, i: (p, i, 0)), sp["k_t"], sp["k_t"], sp["f_t"]],
        out_shape=[sp["wide"], jax.ShapeDtypeStruct((N_PAIRS, qs.shape[0], 8), F32),
                   sp["k_t_out"], sp["k_t_out"], sp["f_t_out"]],
        scratch_shapes=[sp["acc"]],
        compiler_params=_cparams(56),
        name="fox_bwd",
    )(qs, k3, v3, dy, y, lse, fw, ft4, first_block)


SIGN_BIT = 0x80000000


def _sb_terms(z, mask, diag):
    neg_abs = pltpu.bitcast(pltpu.bitcast(z, jnp.uint32) | jnp.uint32(SIGN_BIT), F32)
    lb = jnp.minimum(z, 0.0) - jnp.log(1.0 + jnp.exp(neg_abs))
    l1m = lb - z
    if diag:
        l1m = jnp.where(mask, l1m, 0.0)
    return lb, l1m


def _dot_split2_stacked(x, m2):
    hi, lo = _split2(x)
    return _dot(jnp.concatenate([hi, lo], axis=1), m2)


def _tri_stacked(kind):
    t = _tri(ATT_BLOCK, kind)
    return jnp.concatenate([t, t], axis=0)


def _sb_fwd(qs, k3, v3):
    tq, tk = SB_Q_BLOCK, ATT_BLOCK
    sp = _att_specs(qs.shape[0], tq)
    ratio = tq // tk
    upper = _tri_stacked("row_gt_col")

    def body(q_ref, k_ref, v_ref, u_ref, y_ref, rtot_ref, first_ref, acc_ref):
        i = pl.program_id(1)
        first = _head_masks(tq)
        qh = _heads_of(q_ref, first)
        u = u_ref[...]
        acc_ref[...] = jnp.zeros_like(acc_ref)

        def block(j, rs, diag):
            mask = _causal_mask(tq, tk, i * tq - j * tk, strict=True) if diag else None
            k, v = k_ref[j], v_ref[j]
            logits = [_dot_nt(qh[n], k) for n in range(2)]
            terms = [_sb_terms(z, mask, diag) for z in logits]
            right = [_dot_split2_stacked(l1m, u) for _, l1m in terms]
            weights = []
            for n in range(2):
                a = jnp.exp(terms[n][0] + right[n] + rs[n])
                if diag:
                    a = jnp.where(mask, a, 0.0)
                weights.append(a.astype(BF16))
            for n in range(2):
                acc_ref[n] += _dot(weights[n], v)
            return tuple(rs[n] + jnp.sum(terms[n][1], axis=-1, keepdims=True) for n in range(2))

        rs = (jnp.zeros((tq, 1), F32),) * 2
        for d in range(ratio):
            rs = block(ratio * i + (ratio - 1 - d), rs, True)

        def block_matters(c):
            j, r0, r1 = c
            return (j >= 0) & (jnp.max(jnp.maximum(r0, r1)) > -EXP_UNDERFLOW)

        def walk_left(c):
            j, r0, r1 = c
            r0, r1 = block(j, (r0, r1), False)
            return j - 1, r0, r1

        j, r0, r1 = lax.while_loop(block_matters, walk_left, (ratio * i - 1, rs[0], rs[1]))
        y_ref[...] = jnp.where(first, acc_ref[0], acc_ref[1])
        rtot_ref[...] = jnp.where(first, r0, r1)
        first_ref[...] = jnp.ones(first_ref.shape, F32) * (j + 1).astype(F32)

    return pl.pallas_call(
        body,
        grid=(N_PAIRS, sp["nq"]),
        in_specs=[sp["q"], sp["k_rows"], sp["k_rows"], pl.BlockSpec((2 * tk, tk), lambda p, i: (0, 0))],
        out_specs=[sp["q"], sp["q"], sp["first"]],
        out_shape=[sp["wide"], sp["wide"], sp["first_out"]],
        scratch_shapes=[sp["acc"]],
        compiler_params=_cparams(56),
        name="sb_fwd",
    )(qs, k3, v3, upper)


def _sb_bwd(qs, k3, v3, dy, rtot, first_block):
    tq, tk = SB_Q_BLOCK, ATT_BLOCK
    sp = _att_specs(qs.shape[0], tq)
    ratio = tq // tk
    lower_in = _tri_stacked("row_le_col")
    lower = _tri(tk, "row_lt_col")

    def body(q_ref, k_ref, v_ref, dy_ref, rtot_ref, first_ref, li_ref, l_ref, dq_ref, dkt_ref, dvt_ref, acc_ref):
        i = pl.program_id(1)

        @pl.when(i == 0)
        def _():
            dkt_ref[...] = jnp.zeros_like(dkt_ref)
            dvt_ref[...] = jnp.zeros_like(dvt_ref)

        first = _head_masks(tq)
        qh = _heads_of(q_ref, first)
        dyb = dy_ref[...].astype(BF16)
        zero = jnp.zeros_like(dyb)
        dyh = [jnp.where(first, dyb, zero), jnp.where(first, zero, dyb)]
        rtoth = _head_cols(rtot_ref)
        li = li_ref[...]
        lo_tri = l_ref[...]
        acc_ref[...] = jnp.zeros_like(acc_ref)

        def block(j, carry, diag):
            mask = _causal_mask(tq, tk, i * tq - j * tk, strict=True) if diag else None
            k, v = k_ref[j], v_ref[j]
            logits = [_dot_nt(qh[n], k) for n in range(2)]
            das = [_dot_nt(dyh[n], v) for n in range(2)]
            terms = [_sb_terms(z, mask, diag) for z in logits]
            upto = [_dot_split2_stacked(l1m, li) for _, l1m in terms]
            des, weights = [], []
            for n in range(2):
                a = jnp.exp(terms[n][0] + ((rtoth[n] - carry[2 * n]) - upto[n]))
                if diag:
                    a = jnp.where(mask, a, 0.0)
                des.append(a * das[n])
                weights.append(a.astype(BF16))
            lefts = [_dot(de.astype(BF16), lo_tri) for de in des]
            dzbs, out = [], []
            for n in range(2):
                beta = jnp.exp(terms[n][0])
                dz = des[n] - (des[n] + (carry[2 * n + 1] + lefts[n])) * beta
                if diag:
                    dz = jnp.where(mask, dz, 0.0)
                dzbs.append(dz.astype(BF16))
                out += [carry[2 * n] + jnp.sum(terms[n][1], axis=-1, keepdims=True),
                        carry[2 * n + 1] + jnp.sum(des[n], axis=-1, keepdims=True)]
            for n in range(2):
                acc_ref[n] += _dot(dzbs[n], k)
            dkt_ref[0, j] += _dot_tn(qh[0], dzbs[0]) + _dot_tn(qh[1], dzbs[1])
            dvt_ref[0, j] += _dot_tn(dyh[0], weights[0]) + _dot_tn(dyh[1], weights[1])
            return tuple(out)

        carry = (jnp.zeros((tq, 1), F32),) * 4
        carry = lax.fori_loop(_first_block(first_ref, ratio * i), ratio * i, lambda j, c: block(j, c, False), carry)
        for d in range(ratio):
            carry = block(ratio * i + d, carry, True)
        dq_ref[...] = jnp.where(first, acc_ref[0], acc_ref[1])

    return pl.pallas_call(
        body,
        grid=(N_PAIRS, sp["nq"]),
        in_specs=[sp["q"], sp["k_rows"], sp["k_rows"], sp["q"], sp["q"], sp["first"],
                  pl.BlockSpec((2 * tk, tk), lambda p, i: (0, 0)), pl.BlockSpec((tk, tk), lambda p, i: (0, 0))],
        out_specs=[sp["q"], sp["k_t"], sp["k_t"]],
        out_shape=[sp["wide"], sp["k_t_out"], sp["k_t_out"]],
        scratch_shapes=[sp["acc"]],
        compiler_params=_cparams(56),
        name="sb_bwd",
    )(qs, k3, v3, dy, rtot, first_block, lower_in, lower)


def _merge_fwd(x1, gates, y_fox, y_sb, w_bf, w_bs, w_out, tm=512):
    s_len = x1.shape[0]

    def body(x_ref, g_ref, yf_ref, ys_ref, wbf_ref, wbs_ref, wo_ref, o_ref):
        g = g_ref[...].astype(F32)
        of = _dot(yf_ref[...].astype(BF16), wbf_ref[...])
        os_ = _dot(ys_ref[...].astype(BF16), wbs_ref[...])
        merged = _sigmoid(g[:, 0:D_MODEL]) * of + _sigmoid(g[:, D_MODEL:]) * os_
        o_ref[...] = x_ref[...] + _dot(merged.astype(BF16), wo_ref[...])

    row = lambda i: (i, 0)
    full = lambda i: (0, 0)
    return pl.pallas_call(
        body,
        grid=(s_len // tm,),
        in_specs=[
            pl.BlockSpec((tm, D_MODEL), row),
            pl.BlockSpec((tm, 2 * D_MODEL), row),
            pl.BlockSpec((tm, ATT_W), row),
            pl.BlockSpec((tm, ATT_W), row),
            pl.BlockSpec((ATT_W, D_MODEL), full),
            pl.BlockSpec((ATT_W, D_MODEL), full),
            pl.BlockSpec((D_MODEL, D_MODEL), full),
        ],
        out_specs=pl.BlockSpec((tm, D_MODEL), row),
        out_shape=jax.ShapeDtypeStruct((s_len, D_MODEL), F32),
        compiler_params=_cparams(48),
        name="merge_fwd",
    )(x1, gates, y_fox, y_sb, w_bf, w_bs, w_out)


def _merge_bwd(dx2, gates, y_fox, y_sb, w_bf, w_bs, w_out, tm=512):
    s_len = dx2.shape[0]

    def body(d_ref, g_ref, yf_ref, ys_ref, wbf_ref, wbs_ref, wo_ref,
             dyf_ref, dys_ref, dg_ref, dof_ref, dos_ref, m_ref, dbf_ref):
        dbf = d_ref[...].astype(BF16)
        dbf_ref[...] = dbf
        dm = _dot_nt(dbf, wo_ref[...])
        g = g_ref[...].astype(F32)
        of = _dot(yf_ref[...].astype(BF16), wbf_ref[...])
        os_ = _dot(ys_ref[...].astype(BF16), wbs_ref[...])
        sf = _sigmoid(g[:, 0:D_MODEL])
        ss = _sigmoid(g[:, D_MODEL:])
        m_ref[...] = (sf * of + ss * os_).astype(BF16)
        d_of = (dm * sf).astype(BF16)
        d_os = (dm * ss).astype(BF16)
        dof_ref[...] = d_of
        dos_ref[...] = d_os
        dg_ref[:, 0:D_MODEL] = (dm * of * sf * (1.0 - sf)).astype(BF16)
        dg_ref[:, D_MODEL:] = (dm * os_ * ss * (1.0 - ss)).astype(BF16)
        dyf_ref[...] = _dot_nt(d_of, wbf_ref[...])
        dys_ref[...] = _dot_nt(d_os, wbs_ref[...])

    row = lambda i: (i, 0)
    full = lambda i: (0, 0)
    return pl.pallas_call(
        body,
        grid=(s_len // tm,),
        in_specs=[
            pl.BlockSpec((tm, D_MODEL), row),
            pl.BlockSpec((tm, 2 * D_MODEL), row),
            pl.BlockSpec((tm, ATT_W), row),
            pl.BlockSpec((tm, ATT_W), row),
            pl.BlockSpec((ATT_W, D_MODEL), full),
            pl.BlockSpec((ATT_W, D_MODEL), full),
            pl.BlockSpec((D_MODEL, D_MODEL), full),
        ],
        out_specs=[
            pl.BlockSpec((tm, ATT_W), row), pl.BlockSpec((tm, ATT_W), row),
            pl.BlockSpec((tm, 2 * D_MODEL), row),
            pl.BlockSpec((tm, D_MODEL), row), pl.BlockSpec((tm, D_MODEL), row),
            pl.BlockSpec((tm, D_MODEL), row), pl.BlockSpec((tm, D_MODEL), row),
        ],
        out_shape=[
            jax.ShapeDtypeStruct((s_len, ATT_W), F32), jax.ShapeDtypeStruct((s_len, ATT_W), F32),
            jax.ShapeDtypeStruct((s_len, 2 * D_MODEL), BF16),
            jax.ShapeDtypeStruct((s_len, D_MODEL), BF16), jax.ShapeDtypeStruct((s_len, D_MODEL), BF16),
            jax.ShapeDtypeStruct((s_len, D_MODEL), BF16), jax.ShapeDtypeStruct((s_len, D_MODEL), BF16),
        ],
        compiler_params=_cparams(56),
        name="merge_bwd",
    )(dx2, gates, y_fox, y_sb, w_bf, w_bs, w_out)


def _ple_loss(x3, p, g, w_pg, w_pp, target, tm=512):
    s_len = x3.shape[0]
    inv_d = 1.0 / D_MODEL

    def body(x_ref, p_ref, g_ref, wpg_ref, wpp_ref, t_ref,
             dx_ref, du_ref, dt_ref, hn_ref, dg_ref, loss_ref):
        @pl.when(pl.program_id(0) == 0)
        def _():
            dg_ref[...] = jnp.zeros_like(dg_ref)
            loss_ref[...] = jnp.zeros_like(loss_ref)

        x = x_ref[...]
        xn, r = _rms(x)
        gain = g_ref[...]
        hn = (xn * gain).astype(BF16)
        hn_ref[...] = hn
        sg = _sigmoid(_dot(hn, wpg_ref[...]))
        t = _dot(p_ref[...].astype(BF16), wpp_ref[...])
        err = x + sg * t - t_ref[...]
        sq = jnp.sum(_colsum(err * err), axis=-1, keepdims=True)
        loss_ref[...] += (0.5 * inv_d) * sq
        dy = err * inv_d
        du = (dy * t * sg * (1.0 - sg)).astype(BF16)
        du_ref[...] = du
        dt_ref[...] = (dy * sg).astype(BF16)
        dh = _dot_nt(du, wpg_ref[...])
        dx_ref[...] = dy + _rms_bwd(dh, xn, r, gain)
        dg_ref[0:1, :] += _colsum(dh * xn)

    row = lambda i: (i, 0)
    full = lambda i: (0, 0)
    bf = jax.ShapeDtypeStruct((s_len, D_MODEL), BF16)
    return pl.pallas_call(
        body,
        grid=(s_len // tm,),
        in_specs=[
            pl.BlockSpec((tm, D_MODEL), row),
            pl.BlockSpec((tm, PLE_DIM), row),
            pl.BlockSpec((1, D_MODEL), full),
            pl.BlockSpec((D_MODEL, D_MODEL), full),
            pl.BlockSpec((PLE_DIM, D_MODEL), full),
            pl.BlockSpec((tm, D_MODEL), row),
        ],
        out_specs=[
            pl.BlockSpec((tm, D_MODEL), row), pl.BlockSpec((tm, D_MODEL), row),
            pl.BlockSpec((tm, D_MODEL), row), pl.BlockSpec((tm, D_MODEL), row),
            pl.BlockSpec((8, D_MODEL), full), pl.BlockSpec((8, LANES), full),
        ],
        out_shape=[
            jax.ShapeDtypeStruct((s_len, D_MODEL), F32), bf, bf, bf,
            jax.ShapeDtypeStruct((8, D_MODEL), F32), jax.ShapeDtypeStruct((8, LANES), F32),
        ],
        compiler_params=_cparams(48),
        name="ple_loss",
    )(x3, p, g, w_pg, w_pp, target)


def _qknorm_bwd(fq, fk, dqs, dk, dv, qn, kn, bd, bd_t, tm=512):
    s_len = fq.shape[0]

    def body(fq_ref, fk_ref, dq_ref, dk_ref, dv_ref, qn_ref, kn_ref, bd_ref, bdt_ref,
             dz_ref, dqn_ref, dkn_ref):
        @pl.when(pl.program_id(0) == 0)
        def _():
            dqn_ref[...] = jnp.zeros_like(dqn_ref)
            dkn_ref[...] = jnp.zeros_like(dkn_ref)

        bd_m = bd_ref[...]
        bdt_m = bdt_ref[...]

        def one(x, dy, gain, dgain_ref):
            xn, rw = _head_rms(x, bd_m, bdt_m)
            dgain_ref[0:1, :] += _colsum(dy * xn)
            dxn = dy * gain
            return rw * (dxn - xn * _head_mean(dxn * xn, bd_m, bdt_m))

        dz_ref[:, 0:ATT_W] = one(fq_ref[...], dq_ref[...] * QK_SCALE, qn_ref[...], dqn_ref).astype(BF16)
        dz_ref[:, ATT_W:2 * ATT_W] = one(fk_ref[...], dk_ref[...], kn_ref[...], dkn_ref).astype(BF16)
        dz_ref[:, 2 * ATT_W:] = dv_ref[...].astype(BF16)

    row = lambda i: (i, 0)
    full = lambda i: (0, 0)
    att = pl.BlockSpec((tm, ATT_W), row)
    return pl.pallas_call(
        body,
        grid=(s_len // tm,),
        in_specs=[att, att, att, att, att,
                  pl.BlockSpec((1, ATT_W), full), pl.BlockSpec((1, ATT_W), full),
                  pl.BlockSpec((ATT_W, LANES), full), pl.BlockSpec((LANES, ATT_W), full)],
        out_specs=[pl.BlockSpec((tm, 3 * ATT_W), row), pl.BlockSpec((8, ATT_W), full), pl.BlockSpec((8, ATT_W), full)],
        out_shape=[jax.ShapeDtypeStruct((s_len, 3 * ATT_W), BF16),
                   jax.ShapeDtypeStruct((8, ATT_W), F32), jax.ShapeDtypeStruct((8, ATT_W), F32)],
        name="qknorm_bwd",
    )(fq, fk, dqs, dk, dv, qn, kn, bd, bd_t)


def _inproj_bwd(x1, dx2, g, dzf, dlogf, logf, dzs, dgates, w_fox, w_fl, w_sb, w_gates, tm=512):
    s_len = x1.shape[0]

    def body(x_ref, d_ref, g_ref, dzf_ref, dlf_ref, lf_ref, dzs_ref, dgt_ref, wf_ref, wl_ref, ws_ref, wg_ref,
             dx_ref, h_ref, dfl_ref, dg_ref, db_ref):
        @pl.when(pl.program_id(0) == 0)
        def _():
            dg_ref[...] = jnp.zeros_like(dg_ref)
            db_ref[...] = jnp.zeros_like(db_ref)

        xn, r = _rms(x_ref[...])
        gain = g_ref[...]
        h_ref[...] = (xn * gain).astype(BF16)
        lane = lax.broadcasted_iota(jnp.int32, (tm, LANES), 1)
        dfl = jnp.where(lane < N_HEADS, dlf_ref[...] * (1.0 - jnp.exp(lf_ref[...])), 0.0)
        db_ref[0:1, :] += _colsum(dfl)
        dflb = dfl.astype(BF16)
        dfl_ref[...] = dflb
        dh = (_dot_nt(dzf_ref[...], wf_ref[...]) + _dot_nt(dflb, wl_ref[...])
              + _dot_nt(dzs_ref[...], ws_ref[...]) + _dot_nt(dgt_ref[...], wg_ref[...]))
        dx_ref[...] = d_ref[...] + _rms_bwd(dh, xn, r, gain)
        dg_ref[0:1, :] += _colsum(dh * xn)

    row = lambda i: (i, 0)
    full = lambda i: (0, 0)
    return pl.pallas_call(
        body,
        grid=(s_len // tm,),
        in_specs=[
            pl.BlockSpec((tm, D_MODEL), row),
            pl.BlockSpec((tm, D_MODEL), row),
            pl.BlockSpec((1, D_MODEL), full),
            pl.BlockSpec((tm, 3 * ATT_W), row),
            pl.BlockSpec((tm, LANES), row),
            pl.BlockSpec((tm, LANES), row),
            pl.BlockSpec((tm, 3 * ATT_W), row),
            pl.BlockSpec((tm, 2 * D_MODEL), row),
            pl.BlockSpec((D_MODEL, 3 * ATT_W), full),
            pl.BlockSpec((D_MODEL, LANES), full),
            pl.BlockSpec((D_MODEL, 3 * ATT_W), full),
            pl.BlockSpec((D_MODEL, 2 * D_MODEL), full),
        ],
        out_specs=[
            pl.BlockSpec((tm, D_MODEL), row), pl.BlockSpec((tm, D_MODEL), row), pl.BlockSpec((tm, LANES), row),
            pl.BlockSpec((8, D_MODEL), full), pl.BlockSpec((8, LANES), full),
        ],
        out_shape=[
            jax.ShapeDtypeStruct((s_len, D_MODEL), F32), jax.ShapeDtypeStruct((s_len, D_MODEL), BF16),
            jax.ShapeDtypeStruct((s_len, LANES), BF16),
            jax.ShapeDtypeStruct((8, D_MODEL), F32), jax.ShapeDtypeStruct((8, LANES), F32),
        ],
        compiler_params=_cparams(56),
        name="inproj_bwd",
    )(x1, dx2, g, dzf, dlogf, logf, dzs, dgates, w_fox, w_fl, w_sb, w_gates)


def _split_w_in(w_in):
    o = 3 * ATT_W
    w_fox = w_in[:, 0:o]
    w_fl = jnp.pad(w_in[:, o:o + N_HEADS], ((0, 0), (0, LANES - N_HEADS)))
    w_sb = w_in[:, o + N_HEADS:2 * o + N_HEADS]
    w_gates = w_in[:, 2 * o + N_HEADS:]
    return w_fox, w_fl, w_sb, w_gates


def _local_grads(x, p, target, small, full, pending=None, send_early=None):
    blk = ATT_BLOCK
    bd, bd_t = _head_sum_matrices()
    full = dict(full)
    late = list(pending) if pending else []

    x1, a1, b1, u1, *gathered = _ffn_fwd(x, small["ffn1_norm"], full["ffn1_w_gate"], full["ffn1_w_up"],
                                     full["ffn1_w_down"], gather=[pending[k] for k in late])
    for k, gth in zip(late, gathered):
        full[k] = gth if k in KEPT_AS_SHARDS else _whole(k, gth)
    w_fox, w_fl, w_sb, w_gates = _split_w_in(full["w_in"])
    bias = jnp.pad(small["forget_bias"], ((0, 0), (0, LANES - N_HEADS)))
    qn = jnp.tile(small["q_norm"], (1, N_HEADS))
    kn = jnp.tile(small["k_norm"], (1, N_HEADS))
    fq, fk, f_qs, f_k, f_v, logf, s_qs, s_k, s_v, gates = _inproj_fwd(
        x1, small["mix_norm"], w_fox, w_fl, w_sb, w_gates, bias, qn, kn, bd, bd_t)
    f_cum = _cumsum_rows(logf, reverse=False)
    f8 = f_cum[:, 0:N_HEADS]
    fw = jnp.repeat(f8, HEAD_DIM, axis=1)
    ft4 = _pair_rows_t(f8, blk)
    f_k3, f_v3 = _blocked_rows(f_k, blk), _blocked_rows(f_v, blk)
    y_fox, lse, f_first = _fox_fwd(f_qs, f_k3, f_v3, fw, ft4, _key_norm_bound(f_k))
    s_k3, s_v3 = _blocked_rows(s_k, blk), _blocked_rows(s_v, blk)
    y_sb, s_rtot, s_first = _sb_fwd(s_qs, s_k3, s_v3)
    x2 = _merge_fwd(x1, gates, y_fox, y_sb, full["w_branch_fox"], full["w_branch_sb"], full["w_out"])
    x3, a2, b2, u2 = _ffn_fwd(x2, small["ffn2_norm"], full["ffn2_w_gate"], full["ffn2_w_up"], full["ffn2_w_down"])

    dx3, du_ple, dt_ple, hn_ple, dg_ple, loss_sum = _ple_loss(
        x3, p, small["ple_norm"], full["w_ple_gate"], full["w_ple_proj"], target)
    dx2, da2, db2, h_ffn2, d3_bf, dg_ffn2 = _ffn_bwd(
        x2, dx3, small["ffn2_norm"], a2, b2, full["ffn2_w_gate"], full["ffn2_w_up"], full["ffn2_w_down"])
    dy_fox, dy_sb, dgates, d_of, d_os, merged, d2_bf = _merge_bwd(
        dx2, gates, y_fox, y_sb, full["w_branch_fox"], full["w_branch_sb"], full["w_out"])

    f_dqs, dfq_p, f_dkt4, f_dvt4, dft4 = _fox_bwd(f_qs, f_k3, f_v3, dy_fox, y_fox, lse, fw, ft4, f_first)
    s_dqs, s_dkt4, s_dvt4 = _sb_bwd(s_qs, s_k3, s_v3, dy_sb, s_rtot, s_first)

    dzf, dqn8, dkn8 = _qknorm_bwd(fq, fk, f_dqs, _unblocked_t(f_dkt4), _unblocked_t(f_dvt4), qn, kn, bd, bd_t)
    dzs = jnp.concatenate([s_dqs * QK_SCALE, _unblocked_t(s_dkt4), _unblocked_t(s_dvt4)], axis=1).astype(BF16)
    df8 = _unpair_rows_t(dft4) + dfq_p[:, :, 0:2].transpose(1, 0, 2).reshape(-1, N_HEADS)
    dlogf = _cumsum_rows(jnp.pad(df8, ((0, 0), (0, LANES - N_HEADS))), reverse=True)
    dx1, h_mix, dfl, dg_mix, dbias8 = _inproj_bwd(
        x1, dx2, small["mix_norm"], dzf, dlogf, logf, dzs, dgates, w_fox, w_fl, w_sb, w_gates)

    one = lambda t: t[None]
    gw = {}
    gw["ffn2_w_gate"] = _wgrad(da2, one(h_ffn2), name="wgrad_ffn2_gate")
    gw["ffn2_w_up"] = _wgrad(db2, one(h_ffn2), name="wgrad_ffn2_up")
    gw["ffn2_w_down"] = _wgrad(u2, one(d3_bf), scale=0.5, name="wgrad_ffn2_down")
    g_fox = _wgrad(one(h_mix), one(dzf), name="wgrad_in_fox")[0]
    g_fl = _wgrad(one(h_mix), one(dfl), name="wgrad_in_forget")[0]
    g_sb = _wgrad(one(h_mix), one(dzs), name="wgrad_in_sb")[0]
    g_gt = _wgrad(one(h_mix), one(dgates), name="wgrad_in_gates")[0]
    gw["w_in"] = jnp.concatenate([g_fox, g_fl[:, 0:N_HEADS], g_sb, g_gt], axis=1)
    gw["w_branch_fox"] = _wgrad(one(y_fox), one(d_of), name="wgrad_branch_fox")[0]
    gw["w_branch_sb"] = _wgrad(one(y_sb), one(d_os), name="wgrad_branch_sb")[0]
    gw["w_out"] = _wgrad(one(merged), one(d2_bf), name="wgrad_out")[0]
    gw["w_ple_gate"] = _wgrad(one(hn_ple), one(du_ple), name="wgrad_ple_gate")[0]
    gw["w_ple_proj"] = _wgrad(one(p), one(dt_ple), name="wgrad_ple_proj")[0]

    gw["ffn1_w_down"] = _wgrad(u1, one(dx1), scale=0.5, name="wgrad_ffn1_down")

    sent_names, to_send = send_early(gw) if send_early else ([], [])
    grad_x, da1, db1, h_ffn1, _, dg_ffn1, *landed = _ffn_bwd(
        x, dx1, small["ffn1_norm"], a1, b1, full["ffn1_w_gate"], full["ffn1_w_up"], full["ffn1_w_down"],
        scatter=to_send)
    gw["ffn1_w_gate"] = _wgrad(da1, one(h_ffn1), name="wgrad_ffn1_gate")
    gw["ffn1_w_up"] = _wgrad(db1, one(h_ffn1), name="wgrad_ffn1_up")

    fold = lambda t: jnp.sum(t[0:1].reshape(N_HEADS, HEAD_DIM), axis=0, keepdims=True)
    gs = {
        "ffn1_norm": dg_ffn1[0:1], "mix_norm": dg_mix[0:1], "ffn2_norm": dg_ffn2[0:1], "ple_norm": dg_ple[0:1],
        "forget_bias": dbias8[0:1, 0:N_HEADS], "q_norm": fold(dqn8), "k_norm": fold(dkn8),
    }
    return loss_sum, grad_x, gw, gs, dict(zip(sent_names, landed))


def _position():
    return lax.axis_index("x"), lax.axis_index("y"), lax.axis_index("c")


def _other_chips(x, y):
    return [(1 - x, y), (x, 1 - y), (1 - x, 1 - y)]


ANY = pl.BlockSpec(memory_space=pl.ANY)


def _place_own_shard(w, q):
    rows, cols = w.shape
    tr = _row_block(rows, cols * 4, budget=2 * MIB)

    def body(q_ref, w_ref, o_ref):
        o_ref[0] = w_ref[...].astype(BF16)

    return pl.pallas_call(
        body,
        grid_spec=pltpu.PrefetchScalarGridSpec(
            num_scalar_prefetch=1,
            grid=(rows // tr,),
            in_specs=[pl.BlockSpec((tr, cols), lambda i, q_ref: (i, 0))],
            out_specs=pl.BlockSpec((1, tr, cols), lambda i, q_ref: (q_ref[0], i, 0)),
        ),
        out_shape=jax.ShapeDtypeStruct((N_CHIPS, rows, cols), BF16),
        name="place_own_shard",
    )(q, w)


def _gather_semaphores(n):
    return [pltpu.SemaphoreType.DMA((6 * n,)), pltpu.SemaphoreType.DMA((6 * n,))]


def _gather_steps(bufs, send_sems, recv_sems):
    n = len(bufs)
    x, y, c = _position()
    q = 2 * x + y
    chips = _other_chips(x, y)
    sibling = (x, y, 1 - c)

    def half(a, slot, which):
        r2 = bufs[a].shape[1] // 2
        return bufs[a].at[slot, pl.ds(which * r2, r2), :]

    def copy(a, k, region, to):
        return pltpu.make_async_remote_copy(
            src_ref=region, dst_ref=region, send_sem=send_sems.at[6 * a + k], recv_sem=recv_sems.at[6 * a + k],
            device_id=to, device_id_type=MESH)

    def to_chip(a, k):
        tx, ty = chips[k]
        return copy(a, k, half(a, q, c), (tx, ty, c))

    def to_sibling(a, k):
        tx, ty = chips[k]
        return copy(a, 3 + k, half(a, 2 * tx + ty, c), sibling)

    def start():
        for a in range(n):
            for k in range(3):
                to_chip(a, k).start()

    def finish():
        for a in range(n):
            for k, (tx, ty) in enumerate(chips):
                copy(a, k, half(a, 2 * tx + ty, c), (tx, ty, c)).wait_recv()
                to_sibling(a, k).start()
        for a in range(n):
            for k, (tx, ty) in enumerate(chips):
                copy(a, 3 + k, half(a, 2 * tx + ty, 1 - c), sibling).wait_recv()
        for a in range(n):
            for k in range(3):
                to_chip(a, k).wait_send()
                to_sibling(a, k).wait_send()

    return start, finish


def _allgather_weights(slots):
    n = len(slots)

    def body(*refs):
        start, finish = _gather_steps(refs[n:2 * n], *refs[2 * n:])
        start()
        finish()

    return pl.pallas_call(
        body,
        in_specs=[ANY] * n,
        out_specs=[ANY] * n,
        out_shape=[jax.ShapeDtypeStruct(s.shape, s.dtype) for s in slots],
        input_output_aliases={a: a for a in range(n)},
        scratch_shapes=_gather_semaphores(n),
        name="allgather_weights",
    )(*slots)


def _exchange_pair_halves(grads):
    n = len(grads)

    def body(*refs):
        ins, outs = refs[0:n], refs[n:2 * n]
        send_sems, recv_sems = refs[2 * n:]
        x, y, c = _position()
        copies = []
        for a in range(n):
            r2 = grads[a].shape[1] // 2
            cp = pltpu.make_async_remote_copy(
                src_ref=ins[a].at[:, pl.ds((1 - c) * r2, r2), :], dst_ref=outs[a],
                send_sem=send_sems.at[a], recv_sem=recv_sems.at[a], device_id=(x, y, 1 - c), device_id_type=MESH)
            cp.start()
            copies.append(cp)
        for cp in copies:
            cp.wait()

    return pl.pallas_call(
        body,
        in_specs=[ANY] * n,
        out_specs=[ANY] * n,
        out_shape=[jax.ShapeDtypeStruct((N_CHIPS, g.shape[1] // 2, g.shape[2]), g.dtype) for g in grads],
        scratch_shapes=[pltpu.SemaphoreType.DMA((n,)), pltpu.SemaphoreType.DMA((n,))],
        name="rs_pair_exchange",
    )(*grads)


def _scatter_semaphores(n):
    return [pltpu.SemaphoreType.DMA((3 * n,)), pltpu.SemaphoreType.DMA((3 * n,)), pltpu.SemaphoreType.DMA((n,))]


def _scatter_steps(ins, outs, send_sems, recv_sems, local_sems):
    n = len(ins)
    x, y, c = _position()
    q = 2 * x + y
    chips = _other_chips(x, y)

    def own(a):
        return pltpu.make_async_copy(ins[a].at[q], outs[a].at[q], local_sems.at[a])

    def to_chip(a, k):
        tx, ty = chips[k]
        return pltpu.make_async_remote_copy(
            src_ref=ins[a].at[2 * tx + ty], dst_ref=outs[a].at[q],
            send_sem=send_sems.at[3 * a + k], recv_sem=recv_sems.at[3 * a + k],
            device_id=(tx, ty, c), device_id_type=MESH)

    def start():
        for a in range(n):
            own(a).start()
            for k in range(3):
                to_chip(a, k).start()

    def finish():
        for a in range(n):
            own(a).wait()
            for k in range(3):
                to_chip(a, k).wait()

    return start, finish


def _scatter_to_owner_chips(pairs):
    n = len(pairs)

    def body(*refs):
        start, finish = _scatter_steps(refs[0:n], refs[n:2 * n], *refs[2 * n:])
        start()
        finish()

    return pl.pallas_call(
        body,
        in_specs=[ANY] * n,
        out_specs=[ANY] * n,
        out_shape=[jax.ShapeDtypeStruct(p.shape, p.dtype) for p in pairs],
        scratch_shapes=_scatter_semaphores(n),
        name="rs_scatter",
    )(*pairs)


def _join_halves(shards):
    n = len(shards)

    def body(*refs):
        bufs = refs[n:2 * n]
        send_sems, recv_sems = refs[2 * n:]
        x, y, c = _position()
        started = []
        for a in range(n):
            r2 = shards[a].shape[0] // 2
            mine = bufs[a].at[pl.ds(c * r2, r2), :]
            cp = pltpu.make_async_remote_copy(
                src_ref=mine, dst_ref=mine, send_sem=send_sems.at[a], recv_sem=recv_sems.at[a],
                device_id=(x, y, 1 - c), device_id_type=MESH)
            cp.start()
            started.append(cp)
        for cp in started:
            cp.wait()

    return pl.pallas_call(
        body,
        in_specs=[ANY] * n,
        out_specs=[ANY] * n,
        out_shape=[jax.ShapeDtypeStruct(t.shape, t.dtype) for t in shards],
        input_output_aliases={a: a for a in range(n)},
        scratch_shapes=[pltpu.SemaphoreType.DMA((n,)), pltpu.SemaphoreType.DMA((n,))],
        name="rs_join_halves",
    )(*shards)


def _add_pair(g, got, c):
    _, r2, cols = got.shape

    def body(c_ref, g_ref, got_ref, o_ref):
        o_ref[...] = (g_ref[...].astype(F32) + got_ref[...].astype(F32)).astype(BF16)

    spec = pl.BlockSpec((1, r2, cols), lambda s, c_ref: (s, 0, 0))
    return pl.pallas_call(
        body,
        grid_spec=pltpu.PrefetchScalarGridSpec(
            num_scalar_prefetch=1,
            grid=(N_CHIPS,),
            in_specs=[pl.BlockSpec((1, r2, cols), lambda s, c_ref: (s, c_ref[0], 0)), spec],
            out_specs=spec,
        ),
        out_shape=jax.ShapeDtypeStruct(got.shape, BF16),
        name="rs_add_pair",
    )(c, g, got)


def _add_chips(parts, c):
    _, r2, cols = parts.shape

    def body(c_ref, p0, p1, p2, p3, o_ref):
        o_ref[...] = ((p0[0].astype(F32) + p1[0].astype(F32)) + p2[0].astype(F32)) + p3[0].astype(F32)

    specs = [pl.BlockSpec((1, r2, cols), functools.partial(lambda i, c_ref, s: (s, 0, 0), s=s))
             for s in range(N_CHIPS)]
    return pl.pallas_call(
        body,
        grid_spec=pltpu.PrefetchScalarGridSpec(
            num_scalar_prefetch=1,
            grid=(1,),
            in_specs=specs,
            out_specs=pl.BlockSpec((r2, cols), lambda i, c_ref: (c_ref[0], 0)),
        ),
        out_shape=jax.ShapeDtypeStruct((2 * r2, cols), F32),
        name="rs_add_chips",
    )(c, parts, parts, parts, parts)


def _allreduce_small(part):
    shape = part.shape

    def body(in_ref, out_ref, gather_ref, send_sems, recv_sems):
        x, y, c = _position()
        me = 4 * x + 2 * y + c
        relations = [(a, b, d) for a in (0, 1) for b in (0, 1) for d in (0, 1)][1:]
        flip = lambda v, f: 1 - v if f else v
        copies = []
        for k, (a, b, d) in enumerate(relations):
            cp = pltpu.make_async_remote_copy(
                src_ref=in_ref, dst_ref=gather_ref.at[me], send_sem=send_sems.at[k], recv_sem=recv_sems.at[k],
                device_id=(flip(x, a), flip(y, b), flip(c, d)), device_id_type=MESH)
            cp.start()
            copies.append(cp)
        gather_ref[me] = in_ref[...]
        for cp in copies:
            cp.wait()
        total = gather_ref[0]
        for dev in range(1, 8):
            total = total + gather_ref[dev]
        out_ref[...] = total

    vmem = pl.BlockSpec(memory_space=pltpu.VMEM)
    return pl.pallas_call(
        body,
        in_specs=[vmem],
        out_specs=vmem,
        out_shape=jax.ShapeDtypeStruct(shape, F32),
        scratch_shapes=[pltpu.VMEM((8,) + shape, F32), pltpu.SemaphoreType.DMA((7,)), pltpu.SemaphoreType.DMA((7,))],
        name="allreduce_small",
    )(part)


def _adamw(w, g, m, v):
    rows, cols = w.shape
    tr = _row_block(rows, cols * 4, budget=MIB)
    c1 = 1.0 / (1.0 - ADAM_B1 ** ADAM_STEP)
    c2 = 1.0 / (1.0 - ADAM_B2 ** ADAM_STEP)

    def body(w_ref, g_ref, m_ref, v_ref, d_ref, nm_ref, nv_ref):
        g_ = g_ref[...]
        nm = ADAM_B1 * m_ref[...] + (1.0 - ADAM_B1) * g_
        nv = ADAM_B2 * v_ref[...] + (1.0 - ADAM_B2) * (g_ * g_)
        nm_ref[...] = nm
        nv_ref[...] = nv
        d_ref[...] = -ADAM_LR * ((nm * c1) / (jnp.sqrt(nv * c2) + ADAM_EPS) + ADAM_WD * w_ref[...])

    spec = pl.BlockSpec((tr, cols), lambda i: (i, 0))
    out = jax.ShapeDtypeStruct((rows, cols), F32)
    return pl.pallas_call(
        body,
        grid=(rows // tr,),
        in_specs=[spec] * 4,
        out_specs=[spec] * 3,
        out_shape=[out] * 3,
        name="adamw",
    )(w, g, m, v)


BIG = ["ffn1_w_gate", "ffn1_w_up", "ffn1_w_down", "w_in", "w_branch_fox", "w_branch_sb", "w_out",
       "ffn2_w_gate", "ffn2_w_up", "ffn2_w_down", "w_ple_gate", "w_ple_proj"]
SMALL = ["ffn1_norm", "mix_norm", "ffn2_norm", "ple_norm", "forget_bias", "q_norm", "k_norm"]
COLUMN_SHARDED = ["w_in", "w_branch_fox", "w_branch_sb", "w_ple_proj"]
KEPT_AS_SHARDS = ["ffn1_w_gate", "ffn1_w_up", "ffn1_w_down", "ffn2_w_gate", "ffn2_w_up", "ffn2_w_down"]
WORKED_TRANSPOSED = ["ffn1_w_gate", "ffn1_w_up", "ffn2_w_gate", "ffn2_w_up"]
NEEDED_FIRST = ["ffn1_w_gate", "ffn1_w_up", "ffn1_w_down"]
READY_LAST = ["ffn1_w_gate", "ffn1_w_up"]
ORDER = ["ffn1_norm", "ffn1_w_gate", "ffn1_w_up", "ffn1_w_down", "mix_norm", "w_in", "forget_bias", "q_norm",
         "k_norm", "w_branch_fox", "w_branch_sb", "w_out", "ffn2_norm", "ffn2_w_gate", "ffn2_w_up",
         "ffn2_w_down", "ple_norm", "w_ple_gate", "w_ple_proj"]
SMALL_ROWS = {"ffn1_norm": 0, "mix_norm": 1, "ffn2_norm": 2, "ple_norm": 3}
SMALL_COLS = {"forget_bias": (0, N_HEADS), "q_norm": (N_HEADS, HEAD_DIM), "k_norm": (N_HEADS + HEAD_DIM, HEAD_DIM)}
LOSS_ROW = 5


def _stored(name, a):
    return jnp.swapaxes(a[0], 0, 1) if name in WORKED_TRANSPOSED else a[0]


def _returned(name, t):
    return (jnp.swapaxes(t, 0, 1) if name in WORKED_TRANSPOSED else t)[None]


def _whole(name, gathered):
    if name in COLUMN_SHARDED:
        return jnp.concatenate([gathered[s] for s in range(N_CHIPS)], axis=1)
    return gathered.reshape(-1, gathered.shape[-1])


def _as_shards(name, whole):
    if name in COLUMN_SHARDED:
        k, n = whole.shape
        return whole.reshape(k, N_CHIPS, n // N_CHIPS).transpose(1, 0, 2)
    return whole.reshape(N_CHIPS, whole.shape[0] // N_CHIPS, whole.shape[1])


def _pack_small(values, extra=None):
    rows = [values[k] for k in ("ffn1_norm", "mix_norm", "ffn2_norm", "ple_norm")]
    tail = jnp.concatenate([values["forget_bias"], values["q_norm"], values["k_norm"]], axis=1)
    rows.append(jnp.pad(tail, ((0, 0), (0, D_MODEL - tail.shape[1]))))
    packed = jnp.concatenate(rows + [jnp.zeros((3, D_MODEL), F32)], axis=0)
    if extra is not None:
        packed = packed.at[LOSS_ROW, 0].set(extra)
    return packed


def _unpack_small(packed):
    out = {k: packed[r:r + 1] for k, r in SMALL_ROWS.items()}
    for k, (start, size) in SMALL_COLS.items():
        out[k] = packed[4:5, start:start + size]
    return out


def kernel(x, p, ffn1_norm, ffn1_w_gate, ffn1_w_up, ffn1_w_down, mix_norm, w_in, forget_bias, q_norm, k_norm, w_branch_fox, w_branch_sb, w_out, ffn2_norm, ffn2_w_gate, ffn2_w_up, ffn2_w_down, ple_norm, w_ple_gate, w_ple_proj, loss_target, m_ffn1_norm, m_ffn1_w_gate, m_ffn1_w_up, m_ffn1_w_down, m_mix_norm, m_w_in, m_forget_bias, m_q_norm, m_k_norm, m_w_branch_fox, m_w_branch_sb, m_w_out, m_ffn2_norm, m_ffn2_w_gate, m_ffn2_w_up, m_ffn2_w_down, m_ple_norm, m_w_ple_gate, m_w_ple_proj, v_ffn1_norm, v_ffn1_w_gate, v_ffn1_w_up, v_ffn1_w_down, v_mix_norm, v_w_in, v_forget_bias, v_q_norm, v_k_norm, v_w_branch_fox, v_w_branch_sb, v_w_out, v_ffn2_norm, v_ffn2_w_gate, v_ffn2_w_up, v_ffn2_w_down, v_ple_norm, v_w_ple_gate, v_w_ple_proj):
    args = dict(locals())
    weights = {k: args[k] for k in ORDER}
    moments_m = {k: args["m_" + k] for k in ORDER}
    moments_v = {k: args["v_" + k] for k in ORDER}

    c_idx = lax.axis_index("c").astype(jnp.int32).reshape(1)
    q_idx = (2 * lax.axis_index("x") + lax.axis_index("y")).astype(jnp.int32).reshape(1)
    own = {k: _place_own_shard(_stored(k, weights[k]), q_idx) for k in BIG}
    full = dict(zip(NEEDED_FIRST, _allgather_weights([own[k] for k in NEEDED_FIRST])))
    pending = {k: own[k] for k in BIG if k not in NEEDED_FIRST}
    small = {k: weights[k] for k in SMALL}

    def pair_sums(names, gw):
        slots = [gw[k] if k in KEPT_AS_SHARDS else _as_shards(k, gw[k]) for k in names]
        from_core = _exchange_pair_halves(slots)
        return [_add_pair(g, got, c_idx) for g, got in zip(slots, from_core)]

    early = [k for k in BIG if k not in READY_LAST]
    loss_sum, grad_x, gw, gs, parts = _local_grads(
        x[0], p[0, 0], loss_target[0], small, full, pending, lambda ready: (early, pair_sums(early, ready)))

    parts.update(zip(READY_LAST, _scatter_to_owner_chips(pair_sums(READY_LAST, gw))))
    grads_big = dict(zip(BIG, _join_halves([_add_chips(parts[k], c_idx) for k in BIG])))
    reduced = _allreduce_small(_pack_small(gs, extra=loss_sum[0, 0]))
    grads_small = _unpack_small(reduced)
    loss = reduced[LOSS_ROW, 0]

    grads, deltas, new_m, new_v = {}, {}, {}, {}
    for k in BIG:
        d, nm, nv = _adamw(_stored(k, weights[k]), grads_big[k], _stored(k, moments_m[k]), _stored(k, moments_v[k]))
        grads[k], deltas[k], new_m[k], new_v[k] = (_returned(k, t) for t in (grads_big[k], d, nm, nv))
    d_s, nm_s, nv_s = _adamw(_pack_small({k: weights[k] for k in SMALL}), reduced,
                             _pack_small({k: moments_m[k] for k in SMALL}),
                             _pack_small({k: moments_v[k] for k in SMALL}))
    for k in SMALL:
        grads[k] = grads_small[k]
    for name, packed in (("d", d_s), ("m", nm_s), ("v", nv_s)):
        target = {"d": deltas, "m": new_m, "v": new_v}[name]
        target.update(_unpack_small(packed))

    return (loss, grad_x[None], *[grads[k] for k in ORDER], *[deltas[k] for k in ORDER],
            *[new_m[k] for k in ORDER], *[new_v[k] for k in ORDER])
```

```python
import functools

import jax
import jax.numpy as jnp
from jax import lax
from jax.experimental import pallas as pl
from jax.experimental.pallas import tpu as pltpu

F32 = jnp.float32
BF16 = jnp.bfloat16

D_MODEL = 1024
D_FF = 2816
N_CHIPS = 4
FF_SHARD = D_FF // N_CHIPS
FFN_CHUNKS = 2
WGRAD_TOKENS = 4096
WGRAD_VMEM = 30 * 1024 * 1024
HEAD_DIM = 64
N_HEADS = 8
ATT_W = N_HEADS * HEAD_DIM
PAIR_W = 2 * HEAD_DIM
N_PAIRS = N_HEADS // 2
PLE_DIM = 256
IN_WIDTH = 3 * ATT_W + N_HEADS + 3 * ATT_W + 2 * D_MODEL
EPS = 1e-6
QK_SCALE = HEAD_DIM ** -0.5
LANES = 128
ATT_BLOCK = 256
FOX_Q_BLOCK = 512
SB_Q_BLOCK = 256
NEG_BIG = -1e30
EXP_UNDERFLOW = 110.0
MAX_REFERENCE_EXCESS = 40.0

ADAM_LR = 0.001
ADAM_B1 = 0.9
ADAM_B2 = 0.999
ADAM_EPS = 1e-08
ADAM_WD = 0.01
ADAM_STEP = 10

MESH = pl.DeviceIdType.MESH
MIB = 1024 * 1024


def _cparams(vmem_mib=48):
    return pltpu.CompilerParams(vmem_limit_bytes=vmem_mib * MIB)


def _dot(a, b):
    return jnp.dot(a, b, preferred_element_type=F32)


def _dot_tn(a, b):
    return lax.dot_general(a, b, (((0,), (0,)), ((), ())), preferred_element_type=F32)


def _dot_nt(a, b):
    return lax.dot_general(a, b, (((1,), (1,)), ((), ())), preferred_element_type=F32)


def _sigmoid(x):
    return 1.0 / (1.0 + jnp.exp(-x))


def _split2(x):
    hi = x.astype(BF16)
    lo = (x - hi.astype(F32)).astype(BF16)
    return hi, lo


def _dot_split2(x, m):
    hi, lo = _split2(x)
    return _dot(hi, m) + _dot(lo, m)


def _split3(x):
    hi = x.astype(BF16)
    rest = x - hi.astype(F32)
    mid = rest.astype(BF16)
    lo = (rest - mid.astype(F32)).astype(BF16)
    return hi, mid, lo


def _rms(x):
    r = lax.rsqrt(jnp.mean(x * x, axis=-1, keepdims=True) + EPS)
    return x * r, r


def _rms_bwd(dh, xn, r, g):
    dxn = dh * g
    return r * (dxn - xn * jnp.mean(dxn * xn, axis=-1, keepdims=True))


def _colsum(x):
    return jnp.sum(x, axis=0, keepdims=True)


def _row_block(rows, row_bytes, budget):
    best = None
    for t in range(8, rows + 1, 8):
        if rows % t == 0 and t * row_bytes <= budget:
            best = t
    return best if best is not None else rows


def _ffn_fwd(x, g, wg, wu, wd, gather=(), tm=1024):
    s_len = x.shape[0]
    n = len(gather)
    steps = s_len // tm

    def body(x_ref, g_ref, wg_ref, wu_ref, wd_ref, *rest):
        o_ref, a_ref, b_ref, u_ref = rest[n:n + 4]
        h_s, acc_s = rest[2 * n + 4:2 * n + 6]
        i = pl.program_id(0)
        j = pl.program_id(1)
        if n:
            start, finish = _gather_steps(rest[n + 4:2 * n + 4], *rest[2 * n + 6:])
            pl.when((i == 0) & (j == 0))(start)

        @pl.when(j == 0)
        def _():
            xn, _ = _rms(x_ref[...])
            h_s[...] = (xn * g_ref[...]).astype(BF16)
            acc_s[...] = jnp.zeros_like(acc_s)

        chunks = [pl.ds(r * (tm // FFN_CHUNKS), tm // FFN_CHUNKS) for r in range(FFN_CHUNKS)]
        pre = [(_dot_nt(h_s[rows, :], wg_ref[0]), _dot_nt(h_s[rows, :], wu_ref[0])) for rows in chunks]
        us = []
        for rows, (a, b) in zip(chunks, pre):
            a_ref[0, rows, :] = a.astype(BF16)
            b_ref[0, rows, :] = b.astype(BF16)
            u = (a * _sigmoid(a) * b).astype(BF16)
            u_ref[0, rows, :] = u
            us.append(u)
        for rows, u in zip(chunks, us):
            acc_s[rows, :] += _dot(u, wd_ref[0])

        @pl.when(j == N_CHIPS - 1)
        def _():
            o_ref[...] = x_ref[...] + 0.5 * acc_s[...]

        if n:
            pl.when((i == steps - 1) & (j == N_CHIPS - 1))(finish)

    return pl.pallas_call(
        body,
        grid=(steps, N_CHIPS),
        in_specs=[
            pl.BlockSpec((tm, D_MODEL), lambda i, j: (i, 0)),
            pl.BlockSpec((1, D_MODEL), lambda i, j: (0, 0)),
            pl.BlockSpec((1, FF_SHARD, D_MODEL), lambda i, j: (j, 0, 0)),
            pl.BlockSpec((1, FF_SHARD, D_MODEL), lambda i, j: (j, 0, 0)),
            pl.BlockSpec((1, FF_SHARD, D_MODEL), lambda i, j: (j, 0, 0)),
        ] + [ANY] * n,
        out_specs=[pl.BlockSpec((tm, D_MODEL), lambda i, j: (i, 0))]
        + [pl.BlockSpec((1, tm, FF_SHARD), lambda i, j: (j, i, 0))] * 3 + [ANY] * n,
        out_shape=[jax.ShapeDtypeStruct((s_len, D_MODEL), F32)]
        + [jax.ShapeDtypeStruct((N_CHIPS, s_len, FF_SHARD), BF16)] * 3
        + [jax.ShapeDtypeStruct(s.shape, s.dtype) for s in gather],
        input_output_aliases={5 + a: 4 + a for a in range(n)},
        scratch_shapes=[pltpu.VMEM((tm, D_MODEL), BF16), pltpu.VMEM((tm, D_MODEL), F32)]
        + (_gather_semaphores(n) if n else []),
        compiler_params=_cparams(56),
        name="ffn_fwd_gathering" if n else "ffn_fwd",
    )(x, g, wg, wu, wd, *gather)


def _ffn_bwd(x, d, g, a_pre, b_pre, wg, wu, wd, scatter=(), tm=512):
    s_len = x.shape[0]
    nb = s_len // tm
    n = len(scatter)

    def body(x_ref, d_ref, g_ref, a_ref, b_ref, wg_ref, wu_ref, wd_ref, *rest):
        dx_ref, da_ref, db_ref, h_ref, dbf_ref, dg_ref = rest[n:n + 6]
        dbf_s, dh_s = rest[2 * n + 6:2 * n + 8]
        i = pl.program_id(0)
        j = pl.program_id(1)
        if n:
            start, finish = _scatter_steps(rest[0:n], rest[n + 6:2 * n + 6], *rest[2 * n + 8:])
            pl.when((i == 0) & (j == 0))(start)

        @pl.when(j == 0)
        def _():
            xn, _ = _rms(x_ref[...])
            h_ref[...] = (xn * g_ref[...]).astype(BF16)
            dbf = d_ref[...].astype(BF16)
            dbf_s[...] = dbf
            dbf_ref[...] = dbf
            dh_s[...] = jnp.zeros_like(dh_s)

        @pl.when((i == 0) & (j == 0))
        def _():
            dg_ref[...] = jnp.zeros_like(dg_ref)

        chunks = [pl.ds(r * (tm // FFN_CHUNKS), tm // FFN_CHUNKS) for r in range(FFN_CHUNKS)]
        dus = [0.5 * _dot_nt(dbf_s[rows, :], wd_ref[0]) for rows in chunks]
        das, dbs = [], []
        for rows, du in zip(chunks, dus):
            a = a_ref[0, rows, :].astype(F32)
            b = b_ref[0, rows, :].astype(F32)
            s = _sigmoid(a)
            silu = a * s
            da = (du * b * (s * (1.0 + a * (1.0 - s)))).astype(BF16)
            db = (du * silu).astype(BF16)
            da_ref[0, rows, :] = da
            db_ref[0, rows, :] = db
            das.append(da)
            dbs.append(db)
        for rows, da, db in zip(chunks, das, dbs):
            dh_s[rows, :] += _dot(da, wg_ref[0]) + _dot(db, wu_ref[0])

        @pl.when(j == N_CHIPS - 1)
        def _():
            xn, r = _rms(x_ref[...])
            dh = dh_s[...]
            dx_ref[...] = d_ref[...] + _rms_bwd(dh, xn, r, g_ref[...])
            dg_ref[0:1, :] += _colsum(dh * xn)

        if n:
            pl.when((i == nb - 1) & (j == N_CHIPS - 1))(finish)

    row = lambda i, j: (i, 0)
    shard = lambda i, j: (j, 0, 0)
    act = lambda i, j: (j, i, 0)
    return pl.pallas_call(
        body,
        grid=(nb, N_CHIPS),
        in_specs=[
            pl.BlockSpec((tm, D_MODEL), row),
            pl.BlockSpec((tm, D_MODEL), row),
            pl.BlockSpec((1, D_MODEL), lambda i, j: (0, 0)),
            pl.BlockSpec((1, tm, FF_SHARD), act),
            pl.BlockSpec((1, tm, FF_SHARD), act),
            pl.BlockSpec((1, FF_SHARD, D_MODEL), shard),
            pl.BlockSpec((1, FF_SHARD, D_MODEL), shard),
            pl.BlockSpec((1, FF_SHARD, D_MODEL), shard),
        ] + [ANY] * n,
        out_specs=[
            pl.BlockSpec((tm, D_MODEL), row),
            pl.BlockSpec((1, tm, FF_SHARD), act),
            pl.BlockSpec((1, tm, FF_SHARD), act),
            pl.BlockSpec((tm, D_MODEL), row),
            pl.BlockSpec((tm, D_MODEL), row),
            pl.BlockSpec((8, D_MODEL), lambda i, j: (0, 0)),
        ] + [ANY] * n,
        out_shape=[
            jax.ShapeDtypeStruct((s_len, D_MODEL), F32),
            jax.ShapeDtypeStruct((N_CHIPS, s_len, FF_SHARD), BF16),
            jax.ShapeDtypeStruct((N_CHIPS, s_len, FF_SHARD), BF16),
            jax.ShapeDtypeStruct((s_len, D_MODEL), BF16),
            jax.ShapeDtypeStruct((s_len, D_MODEL), BF16),
            jax.ShapeDtypeStruct((8, D_MODEL), F32),
        ] + [jax.ShapeDtypeStruct(s.shape, s.dtype) for s in scatter],
        scratch_shapes=[
            pltpu.VMEM((tm, D_MODEL), BF16),
            pltpu.VMEM((tm, D_MODEL), F32),
        ] + (_scatter_semaphores(n) if n else []),
        compiler_params=_cparams(56),
        name="ffn_bwd_scattering" if n else "ffn_bwd",
    )(x, d, g, a_pre, b_pre, wg, wu, wd, *scatter)


def _wgrad(a, b, scale=1.0, name="wgrad"):
    na, s_len, k_dim = a.shape
    nb, _, n_dim = b.shape
    n = max(na, nb)
    ts = WGRAD_TOKENS
    while ts > 512 and (ts > s_len or 2 * ts * (k_dim * a.dtype.itemsize + n_dim * b.dtype.itemsize) > WGRAD_VMEM):
        ts //= 2
    steps = s_len // ts

    def body(a_ref, b_ref, o_ref, acc_s):
        s = pl.program_id(1)

        @pl.when(s == 0)
        def _():
            acc_s[...] = jnp.zeros_like(acc_s)

        acc_s[...] += _dot_tn(a_ref[0].astype(BF16), b_ref[0].astype(BF16))

        @pl.when(s == steps - 1)
        def _():
            o_ref[0] = (acc_s[...] * scale).astype(BF16)

    a_map = (lambda m, s: (m, s, 0)) if na > 1 else (lambda m, s: (0, s, 0))
    b_map = (lambda m, s: (m, s, 0)) if nb > 1 else (lambda m, s: (0, s, 0))
    return pl.pallas_call(
        body,
        grid=(n, steps),
        in_specs=[pl.BlockSpec((1, ts, k_dim), a_map), pl.BlockSpec((1, ts, n_dim), b_map)],
        out_specs=pl.BlockSpec((1, k_dim, n_dim), lambda m, s: (m, 0, 0)),
        out_shape=jax.ShapeDtypeStruct((n, k_dim, n_dim), BF16),
        scratch_shapes=[pltpu.VMEM((k_dim, n_dim), F32)],
        compiler_params=_cparams(56),
        name=name,
    )(a, b)


def _head_sum_matrices():
    lane = lax.broadcasted_iota(jnp.int32, (ATT_W, LANES), 0) // HEAD_DIM
    col = lax.broadcasted_iota(jnp.int32, (ATT_W, LANES), 1)
    bd = (lane == col).astype(BF16)
    return bd, bd.T


def _head_mean(t, bd, bd_t):
    per_head = _dot_split2(t, bd) * (1.0 / HEAD_DIM)
    return _dot_split2(per_head, bd_t)


def _head_rms(x, bd, bd_t):
    per_head = _dot_split2(x * x, bd) * (1.0 / HEAD_DIM)
    r = lax.rsqrt(per_head + EPS)
    rw = _dot_split2(r, bd_t)
    return x * rw, rw


def _log_sigmoid(z):
    return jnp.minimum(z, 0.0) - jnp.log(1.0 + jnp.exp(-jnp.abs(z)))


def _inproj_fwd(x1, g, w_fox, w_fl, w_sb, w_gates, bias, qn, kn, bd, bd_t, tm=512):
    s_len = x1.shape[0]

    def body(x_ref, g_ref, wf_ref, wl_ref, ws_ref, wg_ref, bias_ref, qn_ref, kn_ref, bd_ref, bdt_ref,
             fq_ref, fk_ref, qs_ref, kf_ref, vf_ref, logf_ref, sq_ref, sk_ref, sv_ref, gates_ref):
        xn, _ = _rms(x_ref[...])
        h = (xn * g_ref[...]).astype(BF16)
        zf = _dot(h, wf_ref[...])
        fq = zf[:, 0:ATT_W]
        fk = zf[:, ATT_W:2 * ATT_W]
        fq_ref[...] = fq
        fk_ref[...] = fk
        bd_m = bd_ref[...]
        bdt_m = bdt_ref[...]
        fqn, _ = _head_rms(fq, bd_m, bdt_m)
        fkn, _ = _head_rms(fk, bd_m, bdt_m)
        qs_ref[...] = (fqn * qn_ref[...]).astype(BF16) * QK_SCALE
        kf_ref[...] = (fkn * kn_ref[...]).astype(BF16)
        vf_ref[...] = zf[:, 2 * ATT_W:3 * ATT_W].astype(BF16)
        logf_ref[...] = _log_sigmoid(_dot(h, wl_ref[...]) + bias_ref[...])
        zs = _dot(h, ws_ref[...])
        sq_ref[...] = zs[:, 0:ATT_W].astype(BF16) * QK_SCALE
        sk_ref[...] = zs[:, ATT_W:2 * ATT_W].astype(BF16)
        sv_ref[...] = zs[:, 2 * ATT_W:3 * ATT_W].astype(BF16)
        gates_ref[...] = _dot(h, wg_ref[...]).astype(BF16)

    row = lambda i: (i, 0)
    full = lambda i: (0, 0)
    att = lambda dt: jax.ShapeDtypeStruct((s_len, ATT_W), dt)
    return pl.pallas_call(
        body,
        grid=(s_len // tm,),
        in_specs=[
            pl.BlockSpec((tm, D_MODEL), row),
            pl.BlockSpec((1, D_MODEL), full),
            pl.BlockSpec((D_MODEL, 3 * ATT_W), full),
            pl.BlockSpec((D_MODEL, LANES), full),
            pl.BlockSpec((D_MODEL, 3 * ATT_W), full),
            pl.BlockSpec((D_MODEL, 2 * D_MODEL), full),
            pl.BlockSpec((1, LANES), full),
            pl.BlockSpec((1, ATT_W), full),
            pl.BlockSpec((1, ATT_W), full),
            pl.BlockSpec((ATT_W, LANES), full),
            pl.BlockSpec((LANES, ATT_W), full),
        ],
        out_specs=[
            pl.BlockSpec((tm, ATT_W), row), pl.BlockSpec((tm, ATT_W), row),
            pl.BlockSpec((tm, ATT_W), row), pl.BlockSpec((tm, ATT_W), row), pl.BlockSpec((tm, ATT_W), row),
            pl.BlockSpec((tm, LANES), row),
            pl.BlockSpec((tm, ATT_W), row), pl.BlockSpec((tm, ATT_W), row), pl.BlockSpec((tm, ATT_W), row),
            pl.BlockSpec((tm, 2 * D_MODEL), row),
        ],
        out_shape=[
            att(F32), att(F32), att(BF16), att(BF16), att(BF16),
            jax.ShapeDtypeStruct((s_len, LANES), F32),
            att(BF16), att(BF16), att(BF16),
            jax.ShapeDtypeStruct((s_len, 2 * D_MODEL), BF16),
        ],
        compiler_params=_cparams(56),
        name="inproj_fwd",
    )(x1, g, w_fox, w_fl, w_sb, w_gates, bias, qn, kn, bd, bd_t)


def _tri(n, kind):
    r = lax.broadcasted_iota(jnp.int32, (n, n), 0)
    c = lax.broadcasted_iota(jnp.int32, (n, n), 1)
    m = {"row_ge_col": r >= c, "row_le_col": r <= c, "row_gt_col": r > c, "row_lt_col": r < c}[kind]
    return m.astype(BF16)


def _cumsum_rows(x, reverse, tm=256):
    s_len = x.shape[0]
    nb = s_len // tm
    tri = _tri(tm, "row_le_col" if reverse else "row_ge_col")
    edge = 0 if reverse else tm - 1

    def body(x_ref, tri_ref, o_ref, carry_s):
        @pl.when(pl.program_id(0) == 0)
        def _():
            carry_s[...] = jnp.zeros_like(carry_s)

        hi, mid, lo = _split3(x_ref[...])
        t = tri_ref[...]
        y = _dot(t, hi) + _dot(t, mid) + _dot(t, lo) + carry_s[...]
        o_ref[...] = y
        carry_s[...] = y[edge:edge + 1, :]

    order = (lambda i: (nb - 1 - i, 0)) if reverse else (lambda i: (i, 0))
    return pl.pallas_call(
        body,
        grid=(nb,),
        in_specs=[pl.BlockSpec((tm, LANES), order), pl.BlockSpec((tm, tm), lambda i: (0, 0))],
        out_specs=pl.BlockSpec((tm, LANES), order),
        out_shape=jax.ShapeDtypeStruct((s_len, LANES), F32),
        scratch_shapes=[pltpu.VMEM((1, LANES), F32)],
        name="cumsum_rev" if reverse else "cumsum_fwd",
    )(x, tri)


def _unblocked_t(t4):
    _, nb, _, blk = t4.shape
    return t4.transpose(1, 3, 0, 2).reshape(nb * blk, ATT_W)


def _blocked_rows(t, blk):
    return t.reshape(t.shape[0] // blk, blk, t.shape[1])


def _pair_rows_t(f8, blk):
    nb = f8.shape[0] // blk
    t = f8.reshape(nb, blk, N_PAIRS, 2).transpose(2, 0, 3, 1)
    return jnp.pad(t, ((0, 0), (0, 0), (0, 6), (0, 0)))


def _unpair_rows_t(t4):
    _, nb, _, blk = t4.shape
    return t4[:, :, 0:2, :].transpose(1, 3, 0, 2).reshape(nb * blk, N_HEADS)


def _head_masks(tq):
    lane = lax.broadcasted_iota(jnp.int32, (tq, PAIR_W), 1)
    return lane < HEAD_DIM


def _causal_mask(tq, tk, offset, strict):
    d = lax.broadcasted_iota(jnp.int32, (tq, tk), 1) - lax.broadcasted_iota(jnp.int32, (tq, tk), 0)
    return (d < offset) if strict else (d <= offset)


def _heads_of(ref, first):
    t = ref[...]
    zero = jnp.zeros_like(t)
    return [jnp.where(first, t, zero), jnp.where(first, zero, t)]


def _head_cols(ref):
    t = ref[...]
    return [t[:, 0:1], t[:, HEAD_DIM:HEAD_DIM + 1]]


def _att_specs(s_len, tq):
    tk = ATT_BLOCK
    nq, nk = s_len // tq, s_len // tk
    return dict(
        nq=nq,
        q=pl.BlockSpec((tq, PAIR_W), lambda p, i: (i, p)),
        k_t=pl.BlockSpec((1, nk, PAIR_W, tk), lambda p, i: (p, 0, 0, 0)),
        k_rows=pl.BlockSpec((nk, tk, PAIR_W), lambda p, i: (0, 0, p)),
        f_t=pl.BlockSpec((1, nk, 8, tk), lambda p, i: (p, 0, 0, 0)),
        first=pl.BlockSpec((1, 1, 8, LANES), lambda p, i: (p, i, 0, 0)),
        wide=jax.ShapeDtypeStruct((s_len, ATT_W), F32),
        k_t_out=jax.ShapeDtypeStruct((N_PAIRS, nk, PAIR_W, tk), F32),
        f_t_out=jax.ShapeDtypeStruct((N_PAIRS, nk, 8, tk), F32),
        first_out=jax.ShapeDtypeStruct((N_PAIRS, nq, 8, LANES), F32),
        acc=pltpu.VMEM((2, tq, PAIR_W), F32),
    )


def _first_block(first_ref, limit):
    return jnp.clip(jnp.max(first_ref[0, 0]).astype(jnp.int32), 0, limit)


def _key_norm_bound(k):
    sq = jnp.sum(jnp.square(k.astype(F32)).reshape(k.shape[0], N_HEADS, HEAD_DIM), axis=-1)
    bound = jnp.sqrt(jnp.max(sq, axis=0)).reshape(N_PAIRS, 2)
    return jnp.broadcast_to(jnp.pad(bound, ((0, 0), (0, 6)))[:, :, None], (N_PAIRS, 8, LANES))


def _fox_fwd(qs, k3, v3, fw, ft4, kmax):
    tq, tk = FOX_Q_BLOCK, ATT_BLOCK
    sp = _att_specs(qs.shape[0], tq)
    ratio, nk = tq // tk, qs.shape[0] // tk
    f_block_ends = ft4[:, :, :2, tk - 1].reshape(-1)

    def body(fend_ref, q_ref, k_ref, v_ref, fw_ref, ft_ref, kmax_ref, y_ref, lse_ref, first_ref,
             acc_ref, max_ref, sum_ref):
        pair, i = pl.program_id(0), pl.program_id(1)
        first = _head_masks(tq)
        qh = _heads_of(q_ref, first)
        fqh = _head_cols(fw_ref)
        reach = []
        for n in range(2):
            qf = qh[n].astype(F32)
            reach.append(jnp.sqrt(jnp.sum(qf * qf, axis=-1, keepdims=True)) * kmax_ref[0, n:n + 1, 0:1] + fqh[n])

        def logits(j, shift, r0=0, diag=False):
            k, fk = k_ref[j], ft_ref[0, j]
            raw = [_dot_nt(qh[n][r0:], k) for n in range(2)]
            out = []
            for n in range(2):
                s = raw[n] + (shift[n][r0:] - fk[n:n + 1, :])
                if diag:
                    s = jnp.where(_causal_mask(tq - r0, tk, 0, strict=False), s, NEG_BIG)
                out.append(s)
            return out

        def max_pass(j, r0=0, diag=False, assign=False):
            ss = logits(j, fqh, r0, diag)
            for n in range(2):
                max_ref[n, r0:] = ss[n] if assign else jnp.maximum(max_ref[n, r0:], ss[n])

        def sum_pass(j, shift, r0=0, diag=False, assign=False):
            ps = [jnp.exp(s) for s in logits(j, shift, r0, diag)]
            v = v_ref[j]
            for n in range(2):
                sum_ref[n, r0:] = ps[n] if assign else sum_ref[n, r0:] + ps[n]
            for n in range(2):
                pv = _dot(ps[n].astype(BF16), v)
                acc_ref[n, r0:] = pv if assign else acc_ref[n, r0:] + pv

        for d in range(ratio):
            max_pass(ratio * i + d, d * tk, True, d == 0)

        m_diag = [jnp.max(max_ref[n], axis=-1, keepdims=True) for n in range(2)]
        slack = [jnp.max(reach[n] - m_diag[n]) for n in range(2)]

        def f_end(j, n):
            return fend_ref[(pair * nk + jnp.maximum(j, 0)) * 2 + n]

        def block_matters(j):
            gap = jnp.maximum(slack[0] - f_end(j, 0), slack[1] - f_end(j, 1))
            return (j >= 0) & (gap > -EXP_UNDERFLOW)

        last_left = ratio * i - 1
        j_first = lax.while_loop(block_matters, lambda j: j - 1, last_left) + 1

        bound = [reach[n] - f_end(last_left, n) for n in range(2)]
        excess = jnp.maximum(jnp.max(bound[0] - m_diag[0]), jnp.max(bound[1] - m_diag[1]))
        exact = excess > MAX_REFERENCE_EXCESS

        def exact_max():
            def one_max(j, c):
                max_pass(j)
                return c
            lax.fori_loop(j_first, ratio * i, one_max, 0)
            return [jnp.max(max_ref[n], axis=-1, keepdims=True) for n in range(2)]

        def bounded_max():
            walked_left = j_first < ratio * i
            return [jnp.maximum(m_diag[n], jnp.where(walked_left, bound[n], NEG_BIG)) for n in range(2)]

        m = lax.cond(exact, exact_max, bounded_max)
        shift = [fqh[n] - m[n] for n in range(2)]

        for d in range(ratio):
            sum_pass(ratio * i + d, shift, d * tk, True, d == 0)

        def one(j, c):
            sum_pass(j, shift)
            return c
        lax.fori_loop(j_first, ratio * i, one, 0)
        l = [jnp.sum(sum_ref[n], axis=-1, keepdims=True) for n in range(2)]
        y_ref[...] = jnp.where(first, acc_ref[0] / l[0], acc_ref[1] / l[1])
        lse_ref[...] = jnp.where(first, m[0] + jnp.log(l[0]), m[1] + jnp.log(l[1]))
        first_ref[...] = jnp.ones(first_ref.shape, F32) * j_first.astype(F32)

    tile = pltpu.VMEM((2, tq, tk), F32)
    return pl.pallas_call(
        body,
        grid=(N_PAIRS, sp["nq"]),
        in_specs=[pl.BlockSpec(memory_space=pltpu.SMEM), sp["q"], sp["k_rows"], sp["k_rows"], sp["q"], sp["f_t"],
                  pl.BlockSpec((1, 8, LANES), lambda p, i: (p, 0, 0))],
        out_specs=[sp["q"], sp["q"], sp["first"]],
        out_shape=[sp["wide"], sp["wide"], sp["first_out"]],
        scratch_shapes=[sp["acc"], tile, tile],
        compiler_params=_cparams(56),
        name="fox_fwd",
    )(f_block_ends, qs, k3, v3, fw, ft4, kmax)


def _fox_bwd(qs, k3, v3, dy, y, lse, fw, ft4, first_block):
    tq, tk = FOX_Q_BLOCK, ATT_BLOCK
    sp = _att_specs(qs.shape[0], tq)
    ratio = tq // tk

    def body(q_ref, k_ref, v_ref, dy_ref, y_ref, lse_ref, fw_ref, ft_ref, first_ref,
             dq_ref, dfq_ref, dkt_ref, dvt_ref, dft_ref, acc_ref):
        i = pl.program_id(1)

        @pl.when(i == 0)
        def _():
            dkt_ref[...] = jnp.zeros_like(dkt_ref)
            dvt_ref[...] = jnp.zeros_like(dvt_ref)
            dft_ref[...] = jnp.zeros_like(dft_ref)

        first = _head_masks(tq)
        qh = _heads_of(q_ref, first)
        dyv = dy_ref[...]
        dyb = dyv.astype(BF16)
        zero = jnp.zeros_like(dyb)
        dyh = [jnp.where(first, dyb, zero), jnp.where(first, zero, dyb)]
        prod = dyv * y_ref[...]
        zf = jnp.zeros_like(prod)
        delta = [jnp.sum(jnp.where(first, prod, zf), axis=-1, keepdims=True),
                 jnp.sum(jnp.where(first, zf, prod), axis=-1, keepdims=True)]
        fqh = _head_cols(fw_ref)
        lseh = _head_cols(lse_ref)
        shift = [fqh[n] - lseh[n] for n in range(2)]
        acc_ref[...] = jnp.zeros_like(acc_ref)

        def block(j, rows, r0=0, diag=False):
            mask = _causal_mask(tq - r0, tk, 0, strict=False) if diag else None
            k, v, fk = k_ref[j], v_ref[j], ft_ref[0, j]
            q_part, dy_part = [t[r0:] for t in qh], [t[r0:] for t in dyh]
            logits = [_dot_nt(q_part[n], k) for n in range(2)]
            dps = [_dot_nt(dy_part[n], v) for n in range(2)]
            pbs, dsbs, out = [], [], []
            for n in range(2):
                p = jnp.exp(logits[n] + (shift[n][r0:] - fk[n:n + 1, :]))
                if diag:
                    p = jnp.where(mask, p, 0.0)
                ds = p * (dps[n] - delta[n][r0:])
                pbs.append(p.astype(BF16))
                dsbs.append(ds.astype(BF16))
                row_sum = jnp.sum(ds, axis=-1, keepdims=True)
                if r0:
                    row_sum = jnp.concatenate([jnp.zeros((r0, 1), F32), row_sum], axis=0)
                out.append(rows[n] + row_sum)
                dft_ref[0, j, n:n + 1, :] -= _colsum(ds)
            for n in range(2):
                acc_ref[n, r0:] += _dot(dsbs[n], k)
            dkt_ref[0, j] += _dot_tn(q_part[0], dsbs[0]) + _dot_tn(q_part[1], dsbs[1])
            dvt_ref[0, j] += _dot_tn(dy_part[0], pbs[0]) + _dot_tn(dy_part[1], pbs[1])
            return tuple(out)

        rows = (jnp.zeros((tq, 1), F32),) * 2
        rows = lax.fori_loop(_first_block(first_ref, ratio * i), ratio * i, lambda j, c: block(j, c), rows)
        for d in range(ratio):
            rows = block(ratio * i + d, rows, d * tk, True)
        dq_ref[...] = jnp.where(first, acc_ref[0], acc_ref[1])
        lane = lax.broadcasted_iota(jnp.int32, (tq, 8), 1)
        dfq_ref[0] = jnp.where(lane == 0, rows[0], jnp.where(lane == 1, rows[1], 0.0))

    return pl.pallas_call(
        body,
        grid=(N_PAIRS, sp["nq"]),
        in_specs=[sp["q"], sp["k_rows"], sp["k_rows"], sp["q"], sp["q"], sp["q"], sp["q"], sp["f_t"], sp["first"]],
        out_specs=[sp["q"], pl.BlockSpec((1, tq, 8), lambda p, i: (p, i, 0)), sp["k_t"], sp["k_t"], sp["f_t"]],
        out_shape=[sp["wide"], jax.ShapeDtypeStruct((N_PAIRS, qs.shape[0], 8), F32),
                   sp["k_t_out"], sp["k_t_out"], sp["f_t_out"]],
        scratch_shapes=[sp["acc"]],
        compiler_params=_cparams(56),
        name="fox_bwd",
    )(qs, k3, v3, dy, y, lse, fw, ft4, first_block)


SIGN_BIT = 0x80000000


def _sb_terms(z, mask, diag):
    neg_abs = pltpu.bitcast(pltpu.bitcast(z, jnp.uint32) | jnp.uint32(SIGN_BIT), F32)
    lb = jnp.minimum(z, 0.0) - jnp.log(1.0 + jnp.exp(neg_abs))
    l1m = lb - z
    if diag:
        l1m = jnp.where(mask, l1m, 0.0)
    return lb, l1m


def _dot_split2_stacked(x, m2):
    hi, lo = _split2(x)
    return _dot(jnp.concatenate([hi, lo], axis=1), m2)


def _tri_stacked(kind):
    t = _tri(ATT_BLOCK, kind)
    return jnp.concatenate([t, t], axis=0)


def _sb_fwd(qs, k3, v3):
    tq, tk = SB_Q_BLOCK, ATT_BLOCK
    sp = _att_specs(qs.shape[0], tq)
    ratio = tq // tk
    upper = _tri_stacked("row_gt_col")

    def body(q_ref, k_ref, v_ref, u_ref, y_ref, rtot_ref, first_ref, acc_ref):
        i = pl.program_id(1)
        first = _head_masks(tq)
        qh = _heads_of(q_ref, first)
        u = u_ref[...]
        acc_ref[...] = jnp.zeros_like(acc_ref)

        def block(j, rs, diag):
            mask = _causal_mask(tq, tk, i * tq - j * tk, strict=True) if diag else None
            k, v = k_ref[j], v_ref[j]
            logits = [_dot_nt(qh[n], k) for n in range(2)]
            terms = [_sb_terms(z, mask, diag) for z in logits]
            right = [_dot_split2_stacked(l1m, u) for _, l1m in terms]
            weights = []
            for n in range(2):
                a = jnp.exp(terms[n][0] + right[n] + rs[n])
                if diag:
                    a = jnp.where(mask, a, 0.0)
                weights.append(a.astype(BF16))
            for n in range(2):
                acc_ref[n] += _dot(weights[n], v)
            return tuple(rs[n] + jnp.sum(terms[n][1], axis=-1, keepdims=True) for n in range(2))

        rs = (jnp.zeros((tq, 1), F32),) * 2
        for d in range(ratio):
            rs = block(ratio * i + (ratio - 1 - d), rs, True)

        def block_matters(c):
            j, r0, r1 = c
            return (j >= 0) & (jnp.max(jnp.maximum(r0, r1)) > -EXP_UNDERFLOW)

        def walk_left(c):
            j, r0, r1 = c
            r0, r1 = block(j, (r0, r1), False)
            return j - 1, r0, r1

        j, r0, r1 = lax.while_loop(block_matters, walk_left, (ratio * i - 1, rs[0], rs[1]))
        y_ref[...] = jnp.where(first, acc_ref[0], acc_ref[1])
        rtot_ref[...] = jnp.where(first, r0, r1)
        first_ref[...] = jnp.ones(first_ref.shape, F32) * (j + 1).astype(F32)

    return pl.pallas_call(
        body,
        grid=(N_PAIRS, sp["nq"]),
        in_specs=[sp["q"], sp["k_rows"], sp["k_rows"], pl.BlockSpec((2 * tk, tk), lambda p, i: (0, 0))],
        out_specs=[sp["q"], sp["q"], sp["first"]],
        out_shape=[sp["wide"], sp["wide"], sp["first_out"]],
        scratch_shapes=[sp["acc"]],
        compiler_params=_cparams(56),
        name="sb_fwd",
    )(qs, k3, v3, upper)


def _sb_bwd(qs, k3, v3, dy, rtot, first_block):
    tq, tk = SB_Q_BLOCK, ATT_BLOCK
    sp = _att_specs(qs.shape[0], tq)
    ratio = tq // tk
    lower_in = _tri_stacked("row_le_col")
    lower = _tri(tk, "row_lt_col")

    def body(q_ref, k_ref, v_ref, dy_ref, rtot_ref, first_ref, li_ref, l_ref, dq_ref, dkt_ref, dvt_ref, acc_ref):
        i = pl.program_id(1)

        @pl.when(i == 0)
        def _():
            dkt_ref[...] = jnp.zeros_like(dkt_ref)
            dvt_ref[...] = jnp.zeros_like(dvt_ref)

        first = _head_masks(tq)
        qh = _heads_of(q_ref, first)
        dyb = dy_ref[...].astype(BF16)
        zero = jnp.zeros_like(dyb)
        dyh = [jnp.where(first, dyb, zero), jnp.where(first, zero, dyb)]
        rtoth = _head_cols(rtot_ref)
        li = li_ref[...]
        lo_tri = l_ref[...]
        acc_ref[...] = jnp.zeros_like(acc_ref)

        def block(j, carry, diag):
            mask = _causal_mask(tq, tk, i * tq - j * tk, strict=True) if diag else None
            k, v = k_ref[j], v_ref[j]
            logits = [_dot_nt(qh[n], k) for n in range(2)]
            das = [_dot_nt(dyh[n], v) for n in range(2)]
            terms = [_sb_terms(z, mask, diag) for z in logits]
            upto = [_dot_split2_stacked(l1m, li) for _, l1m in terms]
            des, weights = [], []
            for n in range(2):
                a = jnp.exp(terms[n][0] + ((rtoth[n] - carry[2 * n]) - upto[n]))
                if diag:
                    a = jnp.where(mask, a, 0.0)
                des.append(a * das[n])
                weights.append(a.astype(BF16))
            lefts = [_dot(de.astype(BF16), lo_tri) for de in des]
            dzbs, out = [], []
            for n in range(2):
                beta = jnp.exp(terms[n][0])
                dz = des[n] - (des[n] + (carry[2 * n + 1] + lefts[n])) * beta
                if diag:
                    dz = jnp.where(mask, dz, 0.0)
                dzbs.append(dz.astype(BF16))
                out += [carry[2 * n] + jnp.sum(terms[n][1], axis=-1, keepdims=True),
                        carry[2 * n + 1] + jnp.sum(des[n], axis=-1, keepdims=True)]
            for n in range(2):
                acc_ref[n] += _dot(dzbs[n], k)
            dkt_ref[0, j] += _dot_tn(qh[0], dzbs[0]) + _dot_tn(qh[1], dzbs[1])
            dvt_ref[0, j] += _dot_tn(dyh[0], weights[0]) + _dot_tn(dyh[1], weights[1])
            return tuple(out)

        carry = (jnp.zeros((tq, 1), F32),) * 4
        carry = lax.fori_loop(_first_block(first_ref, ratio * i), ratio * i, lambda j, c: block(j, c, False), carry)
        for d in range(ratio):
            carry = block(ratio * i + d, carry, True)
        dq_ref[...] = jnp.where(first, acc_ref[0], acc_ref[1])

    return pl.pallas_call(
        body,
        grid=(N_PAIRS, sp["nq"]),
        in_specs=[sp["q"], sp["k_rows"], sp["k_rows"], sp["q"], sp["q"], sp["first"],
                  pl.BlockSpec((2 * tk, tk), lambda p, i: (0, 0)), pl.BlockSpec((tk, tk), lambda p, i: (0, 0))],
        out_specs=[sp["q"], sp["k_t"], sp["k_t"]],
        out_shape=[sp["wide"], sp["k_t_out"], sp["k_t_out"]],
        scratch_shapes=[sp["acc"]],
        compiler_params=_cparams(56),
        name="sb_bwd",
    )(qs, k3, v3, dy, rtot, first_block, lower_in, lower)


def _merge_fwd(x1, gates, y_fox, y_sb, w_bf, w_bs, w_out, tm=512):
    s_len = x1.shape[0]

    def body(x_ref, g_ref, yf_ref, ys_ref, wbf_ref, wbs_ref, wo_ref, o_ref):
        g = g_ref[...].astype(F32)
        of = _dot(yf_ref[...].astype(BF16), wbf_ref[...])
        os_ = _dot(ys_ref[...].astype(BF16), wbs_ref[...])
        merged = _sigmoid(g[:, 0:D_MODEL]) * of + _sigmoid(g[:, D_MODEL:]) * os_
        o_ref[...] = x_ref[...] + _dot(merged.astype(BF16), wo_ref[...])

    row = lambda i: (i, 0)
    full = lambda i: (0, 0)
    return pl.pallas_call(
        body,
        grid=(s_len // tm,),
        in_specs=[
            pl.BlockSpec((tm, D_MODEL), row),
            pl.BlockSpec((tm, 2 * D_MODEL), row),
            pl.BlockSpec((tm, ATT_W), row),
            pl.BlockSpec((tm, ATT_W), row),
            pl.BlockSpec((ATT_W, D_MODEL), full),
            pl.BlockSpec((ATT_W, D_MODEL), full),
            pl.BlockSpec((D_MODEL, D_MODEL), full),
        ],
        out_specs=pl.BlockSpec((tm, D_MODEL), row),
        out_shape=jax.ShapeDtypeStruct((s_len, D_MODEL), F32),
        compiler_params=_cparams(48),
        name="merge_fwd",
    )(x1, gates, y_fox, y_sb, w_bf, w_bs, w_out)


def _merge_bwd(dx2, gates, y_fox, y_sb, w_bf, w_bs, w_out, tm=512):
    s_len = dx2.shape[0]

    def body(d_ref, g_ref, yf_ref, ys_ref, wbf_ref, wbs_ref, wo_ref,
             dyf_ref, dys_ref, dg_ref, dof_ref, dos_ref, m_ref, dbf_ref):
        dbf = d_ref[...].astype(BF16)
        dbf_ref[...] = dbf
        dm = _dot_nt(dbf, wo_ref[...])
        g = g_ref[...].astype(F32)
        of = _dot(yf_ref[...].astype(BF16), wbf_ref[...])
        os_ = _dot(ys_ref[...].astype(BF16), wbs_ref[...])
        sf = _sigmoid(g[:, 0:D_MODEL])
        ss = _sigmoid(g[:, D_MODEL:])
        m_ref[...] = (sf * of + ss * os_).astype(BF16)
        d_of = (dm * sf).astype(BF16)
        d_os = (dm * ss).astype(BF16)
        dof_ref[...] = d_of
        dos_ref[...] = d_os
        dg_ref[:, 0:D_MODEL] = (dm * of * sf * (1.0 - sf)).astype(BF16)
        dg_ref[:, D_MODEL:] = (dm * os_ * ss * (1.0 - ss)).astype(BF16)
        dyf_ref[...] = _dot_nt(d_of, wbf_ref[...])
        dys_ref[...] = _dot_nt(d_os, wbs_ref[...])

    row = lambda i: (i, 0)
    full = lambda i: (0, 0)
    return pl.pallas_call(
        body,
        grid=(s_len // tm,),
        in_specs=[
            pl.BlockSpec((tm, D_MODEL), row),
            pl.BlockSpec((tm, 2 * D_MODEL), row),
            pl.BlockSpec((tm, ATT_W), row),
            pl.BlockSpec((tm, ATT_W), row),
            pl.BlockSpec((ATT_W, D_MODEL), full),
            pl.BlockSpec((ATT_W, D_MODEL), full),
            pl.BlockSpec((D_MODEL, D_MODEL), full),
        ],
        out_specs=[
            pl.BlockSpec((tm, ATT_W), row), pl.BlockSpec((tm, ATT_W), row),
            pl.BlockSpec((tm, 2 * D_MODEL), row),
            pl.BlockSpec((tm, D_MODEL), row), pl.BlockSpec((tm, D_MODEL), row),
            pl.BlockSpec((tm, D_MODEL), row), pl.BlockSpec((tm, D_MODEL), row),
        ],
        out_shape=[
            jax.ShapeDtypeStruct((s_len, ATT_W), F32), jax.ShapeDtypeStruct((s_len, ATT_W), F32),
            jax.ShapeDtypeStruct((s_len, 2 * D_MODEL), BF16),
            jax.ShapeDtypeStruct((s_len, D_MODEL), BF16), jax.ShapeDtypeStruct((s_len, D_MODEL), BF16),
            jax.ShapeDtypeStruct((s_len, D_MODEL), BF16), jax.ShapeDtypeStruct((s_len, D_MODEL), BF16),
        ],
        compiler_params=_cparams(56),
        name="merge_bwd",
    )(dx2, gates, y_fox, y_sb, w_bf, w_bs, w_out)


def _ple_loss(x3, p, g, w_pg, w_pp, target, tm=512):
    s_len = x3.shape[0]
    inv_d = 1.0 / D_MODEL

    def body(x_ref, p_ref, g_ref, wpg_ref, wpp_ref, t_ref,
             dx_ref, du_ref, dt_ref, hn_ref, dg_ref, loss_ref):
        @pl.when(pl.program_id(0) == 0)
        def _():
            dg_ref[...] = jnp.zeros_like(dg_ref)
            loss_ref[...] = jnp.zeros_like(loss_ref)

        x = x_ref[...]
        xn, r = _rms(x)
        gain = g_ref[...]
        hn = (xn * gain).astype(BF16)
        hn_ref[...] = hn
        sg = _sigmoid(_dot(hn, wpg_ref[...]))
        t = _dot(p_ref[...].astype(BF16), wpp_ref[...])
        err = x + sg * t - t_ref[...]
        sq = jnp.sum(_colsum(err * err), axis=-1, keepdims=True)
        loss_ref[...] += (0.5 * inv_d) * sq
        dy = err * inv_d
        du = (dy * t * sg * (1.0 - sg)).astype(BF16)
        du_ref[...] = du
        dt_ref[...] = (dy * sg).astype(BF16)
        dh = _dot_nt(du, wpg_ref[...])
        dx_ref[...] = dy + _rms_bwd(dh, xn, r, gain)
        dg_ref[0:1, :] += _colsum(dh * xn)

    row = lambda i: (i, 0)
    full = lambda i: (0, 0)
    bf = jax.ShapeDtypeStruct((s_len, D_MODEL), BF16)
    return pl.pallas_call(
        body,
        grid=(s_len // tm,),
        in_specs=[
            pl.BlockSpec((tm, D_MODEL), row),
            pl.BlockSpec((tm, PLE_DIM), row),
            pl.BlockSpec((1, D_MODEL), full),
            pl.BlockSpec((D_MODEL, D_MODEL), full),
            pl.BlockSpec((PLE_DIM, D_MODEL), full),
            pl.BlockSpec((tm, D_MODEL), row),
        ],
        out_specs=[
            pl.BlockSpec((tm, D_MODEL), row), pl.BlockSpec((tm, D_MODEL), row),
            pl.BlockSpec((tm, D_MODEL), row), pl.BlockSpec((tm, D_MODEL), row),
            pl.BlockSpec((8, D_MODEL), full), pl.BlockSpec((8, LANES), full),
        ],
        out_shape=[
            jax.ShapeDtypeStruct((s_len, D_MODEL), F32), bf, bf, bf,
            jax.ShapeDtypeStruct((8, D_MODEL), F32), jax.ShapeDtypeStruct((8, LANES), F32),
        ],
        compiler_params=_cparams(48),
        name="ple_loss",
    )(x3, p, g, w_pg, w_pp, target)


def _qknorm_bwd(fq, fk, dqs, dk, dv, qn, kn, bd, bd_t, tm=512):
    s_len = fq.shape[0]

    def body(fq_ref, fk_ref, dq_ref, dk_ref, dv_ref, qn_ref, kn_ref, bd_ref, bdt_ref,
             dz_ref, dqn_ref, dkn_ref):
        @pl.when(pl.program_id(0) == 0)
        def _():
            dqn_ref[...] = jnp.zeros_like(dqn_ref)
            dkn_ref[...] = jnp.zeros_like(dkn_ref)

        bd_m = bd_ref[...]
        bdt_m = bdt_ref[...]

        def one(x, dy, gain, dgain_ref):
            xn, rw = _head_rms(x, bd_m, bdt_m)
            dgain_ref[0:1, :] += _colsum(dy * xn)
            dxn = dy * gain
            return rw * (dxn - xn * _head_mean(dxn * xn, bd_m, bdt_m))

        dz_ref[:, 0:ATT_W] = one(fq_ref[...], dq_ref[...] * QK_SCALE, qn_ref[...], dqn_ref).astype(BF16)
        dz_ref[:, ATT_W:2 * ATT_W] = one(fk_ref[...], dk_ref[...], kn_ref[...], dkn_ref).astype(BF16)
        dz_ref[:, 2 * ATT_W:] = dv_ref[...].astype(BF16)

    row = lambda i: (i, 0)
    full = lambda i: (0, 0)
    att = pl.BlockSpec((tm, ATT_W), row)
    return pl.pallas_call(
        body,
        grid=(s_len // tm,),
        in_specs=[att, att, att, att, att,
                  pl.BlockSpec((1, ATT_W), full), pl.BlockSpec((1, ATT_W), full),
                  pl.BlockSpec((ATT_W, LANES), full), pl.BlockSpec((LANES, ATT_W), full)],
        out_specs=[pl.BlockSpec((tm, 3 * ATT_W), row), pl.BlockSpec((8, ATT_W), full), pl.BlockSpec((8, ATT_W), full)],
        out_shape=[jax.ShapeDtypeStruct((s_len, 3 * ATT_W), BF16),
                   jax.ShapeDtypeStruct((8, ATT_W), F32), jax.ShapeDtypeStruct((8, ATT_W), F32)],
        name="qknorm_bwd",
    )(fq, fk, dqs, dk, dv, qn, kn, bd, bd_t)


def _inproj_bwd(x1, dx2, g, dzf, dlogf, logf, dzs, dgates, w_fox, w_fl, w_sb, w_gates, tm=512):
    s_len = x1.shape[0]

    def body(x_ref, d_ref, g_ref, dzf_ref, dlf_ref, lf_ref, dzs_ref, dgt_ref, wf_ref, wl_ref, ws_ref, wg_ref,
             dx_ref, h_ref, dfl_ref, dg_ref, db_ref):
        @pl.when(pl.program_id(0) == 0)
        def _():
            dg_ref[...] = jnp.zeros_like(dg_ref)
            db_ref[...] = jnp.zeros_like(db_ref)

        xn, r = _rms(x_ref[...])
        gain = g_ref[...]
        h_ref[...] = (xn * gain).astype(BF16)
        lane = lax.broadcasted_iota(jnp.int32, (tm, LANES), 1)
        dfl = jnp.where(lane < N_HEADS, dlf_ref[...] * (1.0 - jnp.exp(lf_ref[...])), 0.0)
        db_ref[0:1, :] += _colsum(dfl)
        dflb = dfl.astype(BF16)
        dfl_ref[...] = dflb
        dh = (_dot_nt(dzf_ref[...], wf_ref[...]) + _dot_nt(dflb, wl_ref[...])
              + _dot_nt(dzs_ref[...], ws_ref[...]) + _dot_nt(dgt_ref[...], wg_ref[...]))
        dx_ref[...] = d_ref[...] + _rms_bwd(dh, xn, r, gain)
        dg_ref[0:1, :] += _colsum(dh * xn)

    row = lambda i: (i, 0)
    full = lambda i: (0, 0)
    return pl.pallas_call(
        body,
        grid=(s_len // tm,),
        in_specs=[
            pl.BlockSpec((tm, D_MODEL), row),
            pl.BlockSpec((tm, D_MODEL), row),
            pl.BlockSpec((1, D_MODEL), full),
            pl.BlockSpec((tm, 3 * ATT_W), row),
            pl.BlockSpec((tm, LANES), row),
            pl.BlockSpec((tm, LANES), row),
            pl.BlockSpec((tm, 3 * ATT_W), row),
            pl.BlockSpec((tm, 2 * D_MODEL), row),
            pl.BlockSpec((D_MODEL, 3 * ATT_W), full),
            pl.BlockSpec((D_MODEL, LANES), full),
            pl.BlockSpec((D_MODEL, 3 * ATT_W), full),
            pl.BlockSpec((D_MODEL, 2 * D_MODEL), full),
        ],
        out_specs=[
            pl.BlockSpec((tm, D_MODEL), row), pl.BlockSpec((tm, D_MODEL), row), pl.BlockSpec((tm, LANES), row),
            pl.BlockSpec((8, D_MODEL), full), pl.BlockSpec((8, LANES), full),
        ],
        out_shape=[
            jax.ShapeDtypeStruct((s_len, D_MODEL), F32), jax.ShapeDtypeStruct((s_len, D_MODEL), BF16),
            jax.ShapeDtypeStruct((s_len, LANES), BF16),
            jax.ShapeDtypeStruct((8, D_MODEL), F32), jax.ShapeDtypeStruct((8, LANES), F32),
        ],
        compiler_params=_cparams(56),
        name="inproj_bwd",
    )(x1, dx2, g, dzf, dlogf, logf, dzs, dgates, w_fox, w_fl, w_sb, w_gates)


def _split_w_in(w_in):
    o = 3 * ATT_W
    w_fox = w_in[:, 0:o]
    w_fl = jnp.pad(w_in[:, o:o + N_HEADS], ((0, 0), (0, LANES - N_HEADS)))
    w_sb = w_in[:, o + N_HEADS:2 * o + N_HEADS]
    w_gates = w_in[:, 2 * o + N_HEADS:]
    return w_fox, w_fl, w_sb, w_gates


def _local_grads(x, p, target, small, full, pending=None, send_early=None):
    blk = ATT_BLOCK
    bd, bd_t = _head_sum_matrices()
    full = dict(full)
    late = list(pending) if pending else []

    x1, a1, b1, u1, *gathered = _ffn_fwd(x, small["ffn1_norm"], full["ffn1_w_gate"], full["ffn1_w_up"],
                                     full["ffn1_w_down"], gather=[pending[k] for k in late])
    for k, gth in zip(late, gathered):
        full[k] = gth if k in KEPT_AS_SHARDS else _whole(k, gth)
    w_fox, w_fl, w_sb, w_gates = _split_w_in(full["w_in"])
    bias = jnp.pad(small["forget_bias"], ((0, 0), (0, LANES - N_HEADS)))
    qn = jnp.tile(small["q_norm"], (1, N_HEADS))
    kn = jnp.tile(small["k_norm"], (1, N_HEADS))
    fq, fk, f_qs, f_k, f_v, logf, s_qs, s_k, s_v, gates = _inproj_fwd(
        x1, small["mix_norm"], w_fox, w_fl, w_sb, w_gates, bias, qn, kn, bd, bd_t)
    f_cum = _cumsum_rows(logf, reverse=False)
    f8 = f_cum[:, 0:N_HEADS]
    fw = jnp.repeat(f8, HEAD_DIM, axis=1)
    ft4 = _pair_rows_t(f8, blk)
    f_k3, f_v3 = _blocked_rows(f_k, blk), _blocked_rows(f_v, blk)
    y_fox, lse, f_first = _fox_fwd(f_qs, f_k3, f_v3, fw, ft4, _key_norm_bound(f_k))
    s_k3, s_v3 = _blocked_rows(s_k, blk), _blocked_rows(s_v, blk)
    y_sb, s_rtot, s_first = _sb_fwd(s_qs, s_k3, s_v3)
    x2 = _merge_fwd(x1, gates, y_fox, y_sb, full["w_branch_fox"], full["w_branch_sb"], full["w_out"])
    x3, a2, b2, u2 = _ffn_fwd(x2, small["ffn2_norm"], full["ffn2_w_gate"], full["ffn2_w_up"], full["ffn2_w_down"])

    dx3, du_ple, dt_ple, hn_ple, dg_ple, loss_sum = _ple_loss(
        x3, p, small["ple_norm"], full["w_ple_gate"], full["w_ple_proj"], target)
    dx2, da2, db2, h_ffn2, d3_bf, dg_ffn2 = _ffn_bwd(
        x2, dx3, small["ffn2_norm"], a2, b2, full["ffn2_w_gate"], full["ffn2_w_up"], full["ffn2_w_down"])
    dy_fox, dy_sb, dgates, d_of, d_os, merged, d2_bf = _merge_bwd(
        dx2, gates, y_fox, y_sb, full["w_branch_fox"], full["w_branch_sb"], full["w_out"])

    f_dqs, dfq_p, f_dkt4, f_dvt4, dft4 = _fox_bwd(f_qs, f_k3, f_v3, dy_fox, y_fox, lse, fw, ft4, f_first)
    s_dqs, s_dkt4, s_dvt4 = _sb_bwd(s_qs, s_k3, s_v3, dy_sb, s_rtot, s_first)

    dzf, dqn8, dkn8 = _qknorm_bwd(fq, fk, f_dqs, _unblocked_t(f_dkt4), _unblocked_t(f_dvt4), qn, kn, bd, bd_t)
    dzs = jnp.concatenate([s_dqs * QK_SCALE, _unblocked_t(s_dkt4), _unblocked_t(s_dvt4)], axis=1).astype(BF16)
    df8 = _unpair_rows_t(dft4) + dfq_p[:, :, 0:2].transpose(1, 0, 2).reshape(-1, N_HEADS)
    dlogf = _cumsum_rows(jnp.pad(df8, ((0, 0), (0, LANES - N_HEADS))), reverse=True)
    dx1, h_mix, dfl, dg_mix, dbias8 = _inproj_bwd(
        x1, dx2, small["mix_norm"], dzf, dlogf, logf, dzs, dgates, w_fox, w_fl, w_sb, w_gates)

    one = lambda t: t[None]
    gw = {}
    gw["ffn2_w_gate"] = _wgrad(da2, one(h_ffn2), name="wgrad_ffn2_gate")
    gw["ffn2_w_up"] = _wgrad(db2, one(h_ffn2), name="wgrad_ffn2_up")
    gw["ffn2_w_down"] = _wgrad(u2, one(d3_bf), scale=0.5, name="wgrad_ffn2_down")
    g_fox = _wgrad(one(h_mix), one(dzf), name="wgrad_in_fox")[0]
    g_fl = _wgrad(one(h_mix), one(dfl), name="wgrad_in_forget")[0]
    g_sb = _wgrad(one(h_mix), one(dzs), name="wgrad_in_sb")[0]
    g_gt = _wgrad(one(h_mix), one(dgates), name="wgrad_in_gates")[0]
    gw["w_in"] = jnp.concatenate([g_fox, g_fl[:, 0:N_HEADS], g_sb, g_gt], axis=1)
    gw["w_branch_fox"] = _wgrad(one(y_fox), one(d_of), name="wgrad_branch_fox")[0]
    gw["w_branch_sb"] = _wgrad(one(y_sb), one(d_os), name="wgrad_branch_sb")[0]
    gw["w_out"] = _wgrad(one(merged), one(d2_bf), name="wgrad_out")[0]
    gw["w_ple_gate"] = _wgrad(one(hn_ple), one(du_ple), name="wgrad_ple_gate")[0]
    gw["w_ple_proj"] = _wgrad(one(p), one(dt_ple), name="wgrad_ple_proj")[0]

    gw["ffn1_w_down"] = _wgrad(u1, one(dx1), scale=0.5, name="wgrad_ffn1_down")

    sent_names, to_send = send_early(gw) if send_early else ([], [])
    grad_x, da1, db1, h_ffn1, _, dg_ffn1, *landed = _ffn_bwd(
        x, dx1, small["ffn1_norm"], a1, b1, full["ffn1_w_gate"], full["ffn1_w_up"], full["ffn1_w_down"],
        scatter=to_send)
    gw["ffn1_w_gate"] = _wgrad(da1, one(h_ffn1), name="wgrad_ffn1_gate")
    gw["ffn1_w_up"] = _wgrad(db1, one(h_ffn1), name="wgrad_ffn1_up")

    fold = lambda t: jnp.sum(t[0:1].reshape(N_HEADS, HEAD_DIM), axis=0, keepdims=True)
    gs = {
        "ffn1_norm": dg_ffn1[0:1], "mix_norm": dg_mix[0:1], "ffn2_norm": dg_ffn2[0:1], "ple_norm": dg_ple[0:1],
        "forget_bias": dbias8[0:1, 0:N_HEADS], "q_norm": fold(dqn8), "k_norm": fold(dkn8),
    }
    return loss_sum, grad_x, gw, gs, dict(zip(sent_names, landed))


def _position():
    return lax.axis_index("x"), lax.axis_index("y"), lax.axis_index("c")


def _other_chips(x, y):
    return [(1 - x, y), (x, 1 - y), (1 - x, 1 - y)]


ANY = pl.BlockSpec(memory_space=pl.ANY)


def _place_own_shard(w, q):
    rows, cols = w.shape
    tr = _row_block(rows, cols * 4, budget=2 * MIB)

    def body(q_ref, w_ref, o_ref):
        o_ref[0] = w_ref[...].astype(BF16)

    return pl.pallas_call(
        body,
        grid_spec=pltpu.PrefetchScalarGridSpec(
            num_scalar_prefetch=1,
            grid=(rows // tr,),
            in_specs=[pl.BlockSpec((tr, cols), lambda i, q_ref: (i, 0))],
            out_specs=pl.BlockSpec((1, tr, cols), lambda i, q_ref: (q_ref[0], i, 0)),
        ),
        out_shape=jax.ShapeDtypeStruct((N_CHIPS, rows, cols), BF16),
        name="place_own_shard",
    )(q, w)


def _gather_semaphores(n):
    return [pltpu.SemaphoreType.DMA((6 * n,)), pltpu.SemaphoreType.DMA((6 * n,))]


def _gather_steps(bufs, send_sems, recv_sems):
    n = len(bufs)
    x, y, c = _position()
    q = 2 * x + y
    chips = _other_chips(x, y)
    sibling = (x, y, 1 - c)

    def half(a, slot, which):
        r2 = bufs[a].shape[1] // 2
        return bufs[a].at[slot, pl.ds(which * r2, r2), :]

    def copy(a, k, region, to):
        return pltpu.make_async_remote_copy(
            src_ref=region, dst_ref=region, send_sem=send_sems.at[6 * a + k], recv_sem=recv_sems.at[6 * a + k],
            device_id=to, device_id_type=MESH)

    def to_chip(a, k):
        tx, ty = chips[k]
        return copy(a, k, half(a, q, c), (tx, ty, c))

    def to_sibling(a, k):
        tx, ty = chips[k]
        return copy(a, 3 + k, half(a, 2 * tx + ty, c), sibling)

    def start():
        for a in range(n):
            for k in range(3):
                to_chip(a, k).start()

    def finish():
        for a in range(n):
            for k, (tx, ty) in enumerate(chips):
                copy(a, k, half(a, 2 * tx + ty, c), (tx, ty, c)).wait_recv()
                to_sibling(a, k).start()
        for a in range(n):
            for k, (tx, ty) in enumerate(chips):
                copy(a, 3 + k, half(a, 2 * tx + ty, 1 - c), sibling).wait_recv()
        for a in range(n):
            for k in range(3):
                to_chip(a, k).wait_send()
                to_sibling(a, k).wait_send()

    return start, finish


def _allgather_weights(slots):
    n = len(slots)

    def body(*refs):
        start, finish = _gather_steps(refs[n:2 * n], *refs[2 * n:])
        start()
        finish()

    return pl.pallas_call(
        body,
        in_specs=[ANY] * n,
        out_specs=[ANY] * n,
        out_shape=[jax.ShapeDtypeStruct(s.shape, s.dtype) for s in slots],
        input_output_aliases={a: a for a in range(n)},
        scratch_shapes=_gather_semaphores(n),
        name="allgather_weights",
    )(*slots)


def _exchange_pair_halves(grads):
    n = len(grads)

    def body(*refs):
        ins, outs = refs[0:n], refs[n:2 * n]
        send_sems, recv_sems = refs[2 * n:]
        x, y, c = _position()
        copies = []
        for a in range(n):
            r2 = grads[a].shape[1] // 2
            cp = pltpu.make_async_remote_copy(
                src_ref=ins[a].at[:, pl.ds((1 - c) * r2, r2), :], dst_ref=outs[a],
                send_sem=send_sems.at[a], recv_sem=recv_sems.at[a], device_id=(x, y, 1 - c), device_id_type=MESH)
            cp.start()
            copies.append(cp)
        for cp in copies:
            cp.wait()

    return pl.pallas_call(
        body,
        in_specs=[ANY] * n,
        out_specs=[ANY] * n,
        out_shape=[jax.ShapeDtypeStruct((N_CHIPS, g.shape[1] // 2, g.shape[2]), g.dtype) for g in grads],
        scratch_shapes=[pltpu.SemaphoreType.DMA((n,)), pltpu.SemaphoreType.DMA((n,))],
        name="rs_pair_exchange",
    )(*grads)


def _scatter_semaphores(n):
    return [pltpu.SemaphoreType.DMA((3 * n,)), pltpu.SemaphoreType.DMA((3 * n,)), pltpu.SemaphoreType.DMA((n,))]


def _scatter_steps(ins, outs, send_sems, recv_sems, local_sems):
    n = len(ins)
    x, y, c = _position()
    q = 2 * x + y
    chips = _other_chips(x, y)

    def own(a):
        return pltpu.make_async_copy(ins[a].at[q], outs[a].at[q], local_sems.at[a])

    def to_chip(a, k):
        tx, ty = chips[k]
        return pltpu.make_async_remote_copy(
            src_ref=ins[a].at[2 * tx + ty], dst_ref=outs[a].at[q],
            send_sem=send_sems.at[3 * a + k], recv_sem=recv_sems.at[3 * a + k],
            device_id=(tx, ty, c), device_id_type=MESH)

    def start():
        for a in range(n):
            own(a).start()
            for k in range(3):
                to_chip(a, k).start()

    def finish():
        for a in range(n):
            own(a).wait()
            for k in range(3):
                to_chip(a, k).wait()

    return start, finish


def _scatter_to_owner_chips(pairs):
    n = len(pairs)

    def body(*refs):
        start, finish = _scatter_steps(refs[0:n], refs[n:2 * n], *refs[2 * n:])
        start()
        finish()

    return pl.pallas_call(
        body,
        in_specs=[ANY] * n,
        out_specs=[ANY] * n,
        out_shape=[jax.ShapeDtypeStruct(p.shape, p.dtype) for p in pairs],
        scratch_shapes=_scatter_semaphores(n),
        name="rs_scatter",
    )(*pairs)


def _join_halves(shards):
    n = len(shards)

    def body(*refs):
        bufs = refs[n:2 * n]
        send_sems, recv_sems = refs[2 * n:]
        x, y, c = _position()
        started = []
        for a in range(n):
            r2 = shards[a].shape[0] // 2
            mine = bufs[a].at[pl.ds(c * r2, r2), :]
            cp = pltpu.make_async_remote_copy(
                src_ref=mine, dst_ref=mine, send_sem=send_sems.at[a], recv_sem=recv_sems.at[a],
                device_id=(x, y, 1 - c), device_id_type=MESH)
            cp.start()
            started.append(cp)
        for cp in started:
            cp.wait()

    return pl.pallas_call(
        body,
        in_specs=[ANY] * n,
        out_specs=[ANY] * n,
        out_shape=[jax.ShapeDtypeStruct(t.shape, t.dtype) for t in shards],
        input_output_aliases={a: a for a in range(n)},
        scratch_shapes=[pltpu.SemaphoreType.DMA((n,)), pltpu.SemaphoreType.DMA((n,))],
        name="rs_join_halves",
    )(*shards)


def _add_pair(g, got, c):
    _, r2, cols = got.shape

    def body(c_ref, g_ref, got_ref, o_ref):
        o_ref[...] = (g_ref[...].astype(F32) + got_ref[...].astype(F32)).astype(BF16)

    spec = pl.BlockSpec((1, r2, cols), lambda s, c_ref: (s, 0, 0))
    return pl.pallas_call(
        body,
        grid_spec=pltpu.PrefetchScalarGridSpec(
            num_scalar_prefetch=1,
            grid=(N_CHIPS,),
            in_specs=[pl.BlockSpec((1, r2, cols), lambda s, c_ref: (s, c_ref[0], 0)), spec],
            out_specs=spec,
        ),
        out_shape=jax.ShapeDtypeStruct(got.shape, BF16),
        name="rs_add_pair",
    )(c, g, got)


def _add_chips(parts, c):
    _, r2, cols = parts.shape

    def body(c_ref, p0, p1, p2, p3, o_ref):
        o_ref[...] = ((p0[0].astype(F32) + p1[0].astype(F32)) + p2[0].astype(F32)) + p3[0].astype(F32)

    specs = [pl.BlockSpec((1, r2, cols), functools.partial(lambda i, c_ref, s: (s, 0, 0), s=s))
             for s in range(N_CHIPS)]
    return pl.pallas_call(
        body,
        grid_spec=pltpu.PrefetchScalarGridSpec(
            num_scalar_prefetch=1,
            grid=(1,),
            in_specs=specs,
            out_specs=pl.BlockSpec((r2, cols), lambda i, c_ref: (c_ref[0], 0)),
        ),
        out_shape=jax.ShapeDtypeStruct((2 * r2, cols), F32),
        name="rs_add_chips",
    )(c, parts, parts, parts, parts)


def _allreduce_small(part):
    shape = part.shape

    def body(in_ref, out_ref, gather_ref, send_sems, recv_sems):
        x, y, c = _position()
        me = 4 * x + 2 * y + c
        relations = [(a, b, d) for a in (0, 1) for b in (0, 1) for d in (0, 1)][1:]
        flip = lambda v, f: 1 - v if f else v
        copies = []
        for k, (a, b, d) in enumerate(relations):
            cp = pltpu.make_async_remote_copy(
                src_ref=in_ref, dst_ref=gather_ref.at[me], send_sem=send_sems.at[k], recv_sem=recv_sems.at[k],
                device_id=(flip(x, a), flip(y, b), flip(c, d)), device_id_type=MESH)
            cp.start()
            copies.append(cp)
        gather_ref[me] = in_ref[...]
        for cp in copies:
            cp.wait()
        total = gather_ref[0]
        for dev in range(1, 8):
            total = total + gather_ref[dev]
        out_ref[...] = total

    vmem = pl.BlockSpec(memory_space=pltpu.VMEM)
    return pl.pallas_call(
        body,
        in_specs=[vmem],
        out_specs=vmem,
        out_shape=jax.ShapeDtypeStruct(shape, F32),
        scratch_shapes=[pltpu.VMEM((8,) + shape, F32), pltpu.SemaphoreType.DMA((7,)), pltpu.SemaphoreType.DMA((7,))],
        name="allreduce_small",
    )(part)


def _adamw(w, g, m, v):
    rows, cols = w.shape
    tr = _row_block(rows, cols * 4, budget=MIB)
    c1 = 1.0 / (1.0 - ADAM_B1 ** ADAM_STEP)
    c2 = 1.0 / (1.0 - ADAM_B2 ** ADAM_STEP)

    def body(w_ref, g_ref, m_ref, v_ref, d_ref, nm_ref, nv_ref):
        g_ = g_ref[...]
        nm = ADAM_B1 * m_ref[...] + (1.0 - ADAM_B1) * g_
        nv = ADAM_B2 * v_ref[...] + (1.0 - ADAM_B2) * (g_ * g_)
        nm_ref[...] = nm
        nv_ref[...] = nv
        d_ref[...] = -ADAM_LR * ((nm * c1) / (jnp.sqrt(nv * c2) + ADAM_EPS) + ADAM_WD * w_ref[...])

    spec = pl.BlockSpec((tr, cols), lambda i: (i, 0))
    out = jax.ShapeDtypeStruct((rows, cols), F32)
    return pl.pallas_call(
        body,
        grid=(rows // tr,),
        in_specs=[spec] * 4,
        out_specs=[spec] * 3,
        out_shape=[out] * 3,
        name="adamw",
    )(w, g, m, v)


BIG = ["ffn1_w_gate", "ffn1_w_up", "ffn1_w_down", "w_in", "w_branch_fox", "w_branch_sb", "w_out",
       "ffn2_w_gate", "ffn2_w_up", "ffn2_w_down", "w_ple_gate", "w_ple_proj"]
SMALL = ["ffn1_norm", "mix_norm", "ffn2_norm", "ple_norm", "forget_bias", "q_norm", "k_norm"]
COLUMN_SHARDED = ["w_in", "w_branch_fox", "w_branch_sb", "w_ple_proj"]
KEPT_AS_SHARDS = ["ffn1_w_gate", "ffn1_w_up", "ffn1_w_down", "ffn2_w_gate", "ffn2_w_up", "ffn2_w_down"]
WORKED_TRANSPOSED = ["ffn1_w_gate", "ffn1_w_up", "ffn2_w_gate", "ffn2_w_up"]
NEEDED_FIRST = ["ffn1_w_gate", "ffn1_w_up", "ffn1_w_down"]
READY_LAST = ["ffn1_w_gate", "ffn1_w_up"]
ORDER = ["ffn1_norm", "ffn1_w_gate", "ffn1_w_up", "ffn1_w_down", "mix_norm", "w_in", "forget_bias", "q_norm",
         "k_norm", "w_branch_fox", "w_branch_sb", "w_out", "ffn2_norm", "ffn2_w_gate", "ffn2_w_up",
         "ffn2_w_down", "ple_norm", "w_ple_gate", "w_ple_proj"]
SMALL_ROWS = {"ffn1_norm": 0, "mix_norm": 1, "ffn2_norm": 2, "ple_norm": 3}
SMALL_COLS = {"forget_bias": (0, N_HEADS), "q_norm": (N_HEADS, HEAD_DIM), "k_norm": (N_HEADS + HEAD_DIM, HEAD_DIM)}
LOSS_ROW = 5


def _stored(name, a):
    return jnp.swapaxes(a[0], 0, 1) if name in WORKED_TRANSPOSED else a[0]


def _returned(name, t):
    return (jnp.swapaxes(t, 0, 1) if name in WORKED_TRANSPOSED else t)[None]


def _whole(name, gathered):
    if name in COLUMN_SHARDED:
        return jnp.concatenate([gathered[s] for s in range(N_CHIPS)], axis=1)
    return gathered.reshape(-1, gathered.shape[-1])


def _as_shards(name, whole):
    if name in COLUMN_SHARDED:
        k, n = whole.shape
        return whole.reshape(k, N_CHIPS, n // N_CHIPS).transpose(1, 0, 2)
    return whole.reshape(N_CHIPS, whole.shape[0] // N_CHIPS, whole.shape[1])


def _pack_small(values, extra=None):
    rows = [values[k] for k in ("ffn1_norm", "mix_norm", "ffn2_norm", "ple_norm")]
    tail = jnp.concatenate([values["forget_bias"], values["q_norm"], values["k_norm"]], axis=1)
    rows.append(jnp.pad(tail, ((0, 0), (0, D_MODEL - tail.shape[1]))))
    packed = jnp.concatenate(rows + [jnp.zeros((3, D_MODEL), F32)], axis=0)
    if extra is not None:
        packed = packed.at[LOSS_ROW, 0].set(extra)
    return packed


def _unpack_small(packed):
    out = {k: packed[r:r + 1] for k, r in SMALL_ROWS.items()}
    for k, (start, size) in SMALL_COLS.items():
        out[k] = packed[4:5, start:start + size]
    return out


def kernel(x, p, ffn1_norm, ffn1_w_gate, ffn1_w_up, ffn1_w_down, mix_norm, w_in, forget_bias, q_norm, k_norm, w_branch_fox, w_branch_sb, w_out, ffn2_norm, ffn2_w_gate, ffn2_w_up, ffn2_w_down, ple_norm, w_ple_gate, w_ple_proj, loss_target, m_ffn1_norm, m_ffn1_w_gate, m_ffn1_w_up, m_ffn1_w_down, m_mix_norm, m_w_in, m_forget_bias, m_q_norm, m_k_norm, m_w_branch_fox, m_w_branch_sb, m_w_out, m_ffn2_norm, m_ffn2_w_gate, m_ffn2_w_up, m_ffn2_w_down, m_ple_norm, m_w_ple_gate, m_w_ple_proj, v_ffn1_norm, v_ffn1_w_gate, v_ffn1_w_up, v_ffn1_w_down, v_mix_norm, v_w_in, v_forget_bias, v_q_norm, v_k_norm, v_w_branch_fox, v_w_branch_sb, v_w_out, v_ffn2_norm, v_ffn2_w_gate, v_ffn2_w_up, v_ffn2_w_down, v_ple_norm, v_w_ple_gate, v_w_ple_proj):
    args = dict(locals())
    weights = {k: args[k] for k in ORDER}
    moments_m = {k: args["m_" + k] for k in ORDER}
    moments_v = {k: args["v_" + k] for k in ORDER}

    c_idx = lax.axis_index("c").astype(jnp.int32).reshape(1)
    q_idx = (2 * lax.axis_index("x") + lax.axis_index("y")).astype(jnp.int32).reshape(1)
    own = {k: _place_own_shard(_stored(k, weights[k]), q_idx) for k in BIG}
    full = dict(zip(NEEDED_FIRST, _allgather_weights([own[k] for k in NEEDED_FIRST])))
    pending = {k: own[k] for k in BIG if k not in NEEDED_FIRST}
    small = {k: weights[k] for k in SMALL}

    def pair_sums(names, gw):
        slots = [gw[k] if k in KEPT_AS_SHARDS else _as_shards(k, gw[k]) for k in names]
        from_core = _exchange_pair_halves(slots)
        return [_add_pair(g, got, c_idx) for g, got in zip(slots, from_core)]

    early = [k for k in BIG if k not in READY_LAST]
    loss_sum, grad_x, gw, gs, parts = _local_grads(
        x[0], p[0, 0], loss_target[0], small, full, pending, lambda ready: (early, pair_sums(early, ready)))

    parts.update(zip(READY_LAST, _scatter_to_owner_chips(pair_sums(READY_LAST, gw))))
    grads_big = dict(zip(BIG, _join_halves([_add_chips(parts[k], c_idx) for k in BIG])))
    reduced = _allreduce_small(_pack_small(gs, extra=loss_sum[0, 0]))
    grads_small = _unpack_small(reduced)
    loss = reduced[LOSS_ROW, 0]

    grads, deltas, new_m, new_v = {}, {}, {}, {}
    for k in BIG:
        d, nm, nv = _adamw(_stored(k, weights[k]), grads_big[k], _stored(k, moments_m[k]), _stored(k, moments_v[k]))
        grads[k], deltas[k], new_m[k], new_v[k] = (_returned(k, t) for t in (grads_big[k], d, nm, nv))
    d_s, nm_s, nv_s = _adamw(_pack_small({k: weights[k] for k in SMALL}), reduced,
                             _pack_small({k: moments_m[k] for k in SMALL}),
                             _pack_small({k: moments_v[k] for k in SMALL}))
    for k in SMALL:
        grads[k] = grads_small[k]
    for name, packed in (("d", d_s), ("m", nm_s), ("v", nv_s)):
        target = {"d": deltas, "m": new_m, "v": new_v}[name]
        target.update(_unpack_small(packed))

    return (loss, grad_x[None], *[grads[k] for k in ORDER], *[deltas[k] for k in ORDER],
            *[new_m[k] for k in ORDER], *[new_v[k] for k in ORDER])
```

```python
import functools

import jax
import jax.numpy as jnp
from jax import lax
from jax.experimental import pallas as pl
from jax.experimental.pallas import tpu as pltpu

F32 = jnp.float32
BF16 = jnp.bfloat16

D_MODEL = 1024
D_FF = 2816
N_CHIPS = 4
FF_SHARD = D_FF // N_CHIPS
FFN_CHUNKS = 2
WGRAD_TOKENS = 4096
WGRAD_VMEM = 30 * 1024 * 1024
HEAD_DIM = 64
N_HEADS = 8
ATT_W = N_HEADS * HEAD_DIM
PAIR_W = 2 * HEAD_DIM
N_PAIRS = N_HEADS // 2
PLE_DIM = 256
IN_WIDTH = 3 * ATT_W + N_HEADS + 3 * ATT_W + 2 * D_MODEL
EPS = 1e-6
QK_SCALE = HEAD_DIM ** -0.5
LANES = 128
ATT_BLOCK = 256
FOX_Q_BLOCK = 512
SB_Q_BLOCK = 256
NEG_BIG = -1e30
EXP_UNDERFLOW = 110.0
MAX_REFERENCE_EXCESS = 40.0
NORM_BOUND_MARGIN = 1.001

ADAM_LR = 0.001
ADAM_B1 = 0.9
ADAM_B2 = 0.999
ADAM_EPS = 1e-08
ADAM_WD = 0.01
ADAM_STEP = 10

MESH = pl.DeviceIdType.MESH
MIB = 1024 * 1024


def _cparams(vmem_mib=48):
    return pltpu.CompilerParams(vmem_limit_bytes=vmem_mib * MIB)


def _dot(a, b):
    return jnp.dot(a, b, preferred_element_type=F32)


def _dot_tn(a, b):
    return lax.dot_general(a, b, (((0,), (0,)), ((), ())), preferred_element_type=F32)


def _dot_nt(a, b):
    return lax.dot_general(a, b, (((1,), (1,)), ((), ())), preferred_element_type=F32)


def _sigmoid(x):
    return 1.0 / (1.0 + jnp.exp(-x))


def _split2(x):
    hi = x.astype(BF16)
    lo = (x - hi.astype(F32)).astype(BF16)
    return hi, lo


def _dot_split2(x, m):
    hi, lo = _split2(x)
    return _dot(hi, m) + _dot(lo, m)


def _split3(x):
    hi = x.astype(BF16)
    rest = x - hi.astype(F32)
    mid = rest.astype(BF16)
    lo = (rest - mid.astype(F32)).astype(BF16)
    return hi, mid, lo


def _rms(x):
    r = lax.rsqrt(jnp.mean(x * x, axis=-1, keepdims=True) + EPS)
    return x * r, r


def _rms_bwd(dh, xn, r, g):
    dxn = dh * g
    return r * (dxn - xn * jnp.mean(dxn * xn, axis=-1, keepdims=True))


def _colsum(x):
    return jnp.sum(x, axis=0, keepdims=True)


def _row_block(rows, row_bytes, budget):
    best = None
    for t in range(8, rows + 1, 8):
        if rows % t == 0 and t * row_bytes <= budget:
            best = t
    return best if best is not None else rows


def _ffn_fwd(x, g, wg, wu, wd, gather=(), tm=1024):
    s_len = x.shape[0]
    n = len(gather)
    steps = s_len // tm

    def body(x_ref, g_ref, wg_ref, wu_ref, wd_ref, *rest):
        o_ref, a_ref, b_ref, u_ref = rest[n:n + 4]
        h_s, acc_s = rest[2 * n + 4:2 * n + 6]
        i = pl.program_id(0)
        j = pl.program_id(1)
        if n:
            start, finish = _gather_steps(rest[n + 4:2 * n + 4], *rest[2 * n + 6:])
            pl.when((i == 0) & (j == 0))(start)

        @pl.when(j == 0)
        def _():
            xn, _ = _rms(x_ref[...])
            h_s[...] = (xn * g_ref[...]).astype(BF16)
            acc_s[...] = jnp.zeros_like(acc_s)

        chunks = [pl.ds(r * (tm // FFN_CHUNKS), tm // FFN_CHUNKS) for r in range(FFN_CHUNKS)]
        pre = [(_dot_nt(h_s[rows, :], wg_ref[0]), _dot_nt(h_s[rows, :], wu_ref[0])) for rows in chunks]
        us = []
        for rows, (a, b) in zip(chunks, pre):
            a_ref[0, rows, :] = a.astype(BF16)
            b_ref[0, rows, :] = b.astype(BF16)
            u = (a * _sigmoid(a) * b).astype(BF16)
            u_ref[0, rows, :] = u
            us.append(u)
        for rows, u in zip(chunks, us):
            acc_s[rows, :] += _dot(u, wd_ref[0])

        @pl.when(j == N_CHIPS - 1)
        def _():
            o_ref[...] = x_ref[...] + 0.5 * acc_s[...]

        if n:
            pl.when((i == steps - 1) & (j == N_CHIPS - 1))(finish)

    return pl.pallas_call(
        body,
        grid=(steps, N_CHIPS),
        in_specs=[
            pl.BlockSpec((tm, D_MODEL), lambda i, j: (i, 0)),
            pl.BlockSpec((1, D_MODEL), lambda i, j: (0, 0)),
            pl.BlockSpec((1, FF_SHARD, D_MODEL), lambda i, j: (j, 0, 0)),
            pl.BlockSpec((1, FF_SHARD, D_MODEL), lambda i, j: (j, 0, 0)),
            pl.BlockSpec((1, FF_SHARD, D_MODEL), lambda i, j: (j, 0, 0)),
        ] + [ANY] * n,
        out_specs=[pl.BlockSpec((tm, D_MODEL), lambda i, j: (i, 0))]
        + [pl.BlockSpec((1, tm, FF_SHARD), lambda i, j: (j, i, 0))] * 3 + [ANY] * n,
        out_shape=[jax.ShapeDtypeStruct((s_len, D_MODEL), F32)]
        + [jax.ShapeDtypeStruct((N_CHIPS, s_len, FF_SHARD), BF16)] * 3
        + [jax.ShapeDtypeStruct(s.shape, s.dtype) for s in gather],
        input_output_aliases={5 + a: 4 + a for a in range(n)},
        scratch_shapes=[pltpu.VMEM((tm, D_MODEL), BF16), pltpu.VMEM((tm, D_MODEL), F32)]
        + (_gather_semaphores(n) if n else []),
        compiler_params=_cparams(56),
        name="ffn_fwd_gathering" if n else "ffn_fwd",
    )(x, g, wg, wu, wd, *gather)


def _ffn_bwd(x, d, g, a_pre, b_pre, wg, wu, wd, scatter=(), tm=512):
    s_len = x.shape[0]
    nb = s_len // tm
    n = len(scatter)

    def body(x_ref, d_ref, g_ref, a_ref, b_ref, wg_ref, wu_ref, wd_ref, *rest):
        dx_ref, da_ref, db_ref, h_ref, dbf_ref, dg_ref = rest[n:n + 6]
        dbf_s, dh_s = rest[2 * n + 6:2 * n + 8]
        i = pl.program_id(0)
        j = pl.program_id(1)
        if n:
            start, finish = _scatter_steps(rest[0:n], rest[n + 6:2 * n + 6], *rest[2 * n + 8:])
            pl.when((i == 0) & (j == 0))(start)

        @pl.when(j == 0)
        def _():
            xn, _ = _rms(x_ref[...])
            h_ref[...] = (xn * g_ref[...]).astype(BF16)
            dbf = d_ref[...].astype(BF16)
            dbf_s[...] = dbf
            dbf_ref[...] = dbf
            dh_s[...] = jnp.zeros_like(dh_s)

        @pl.when((i == 0) & (j == 0))
        def _():
            dg_ref[...] = jnp.zeros_like(dg_ref)

        chunks = [pl.ds(r * (tm // FFN_CHUNKS), tm // FFN_CHUNKS) for r in range(FFN_CHUNKS)]
        dus = [0.5 * _dot_nt(dbf_s[rows, :], wd_ref[0]) for rows in chunks]
        das, dbs = [], []
        for rows, du in zip(chunks, dus):
            a = a_ref[0, rows, :].astype(F32)
            b = b_ref[0, rows, :].astype(F32)
            s = _sigmoid(a)
            silu = a * s
            da = (du * b * (s * (1.0 + a * (1.0 - s)))).astype(BF16)
            db = (du * silu).astype(BF16)
            da_ref[0, rows, :] = da
            db_ref[0, rows, :] = db
            das.append(da)
            dbs.append(db)
        for rows, da, db in zip(chunks, das, dbs):
            dh_s[rows, :] += _dot(da, wg_ref[0]) + _dot(db, wu_ref[0])

        @pl.when(j == N_CHIPS - 1)
        def _():
            xn, r = _rms(x_ref[...])
            dh = dh_s[...]
            dx_ref[...] = d_ref[...] + _rms_bwd(dh, xn, r, g_ref[...])
            dg_ref[0:1, :] += _colsum(dh * xn)

        if n:
            pl.when((i == nb - 1) & (j == N_CHIPS - 1))(finish)

    row = lambda i, j: (i, 0)
    shard = lambda i, j: (j, 0, 0)
    act = lambda i, j: (j, i, 0)
    return pl.pallas_call(
        body,
        grid=(nb, N_CHIPS),
        in_specs=[
            pl.BlockSpec((tm, D_MODEL), row),
            pl.BlockSpec((tm, D_MODEL), row),
            pl.BlockSpec((1, D_MODEL), lambda i, j: (0, 0)),
            pl.BlockSpec((1, tm, FF_SHARD), act),
            pl.BlockSpec((1, tm, FF_SHARD), act),
            pl.BlockSpec((1, FF_SHARD, D_MODEL), shard),
            pl.BlockSpec((1, FF_SHARD, D_MODEL), shard),
            pl.BlockSpec((1, FF_SHARD, D_MODEL), shard),
        ] + [ANY] * n,
        out_specs=[
            pl.BlockSpec((tm, D_MODEL), row),
            pl.BlockSpec((1, tm, FF_SHARD), act),
            pl.BlockSpec((1, tm, FF_SHARD), act),
            pl.BlockSpec((tm, D_MODEL), row),
            pl.BlockSpec((tm, D_MODEL), row),
            pl.BlockSpec((8, D_MODEL), lambda i, j: (0, 0)),
        ] + [ANY] * n,
        out_shape=[
            jax.ShapeDtypeStruct((s_len, D_MODEL), F32),
            jax.ShapeDtypeStruct((N_CHIPS, s_len, FF_SHARD), BF16),
            jax.ShapeDtypeStruct((N_CHIPS, s_len, FF_SHARD), BF16),
            jax.ShapeDtypeStruct((s_len, D_MODEL), BF16),
            jax.ShapeDtypeStruct((s_len, D_MODEL), BF16),
            jax.ShapeDtypeStruct((8, D_MODEL), F32),
        ] + [jax.ShapeDtypeStruct(s.shape, s.dtype) for s in scatter],
        scratch_shapes=[
            pltpu.VMEM((tm, D_MODEL), BF16),
            pltpu.VMEM((tm, D_MODEL), F32),
        ] + (_scatter_semaphores(n) if n else []),
        compiler_params=_cparams(56),
        name="ffn_bwd_scattering" if n else "ffn_bwd",
    )(x, d, g, a_pre, b_pre, wg, wu, wd, *scatter)


def _wgrad(a, b, scale=1.0, name="wgrad"):
    na, s_len, k_dim = a.shape
    nb, _, n_dim = b.shape
    n = max(na, nb)
    ts = WGRAD_TOKENS
    while ts > 512 and (ts > s_len or 2 * ts * (k_dim * a.dtype.itemsize + n_dim * b.dtype.itemsize) > WGRAD_VMEM):
        ts //= 2
    steps = s_len // ts

    def body(a_ref, b_ref, o_ref, acc_s):
        s = pl.program_id(1)

        @pl.when(s == 0)
        def _():
            acc_s[...] = jnp.zeros_like(acc_s)

        acc_s[...] += _dot_tn(a_ref[0].astype(BF16), b_ref[0].astype(BF16))

        @pl.when(s == steps - 1)
        def _():
            o_ref[0] = (acc_s[...] * scale).astype(BF16)

    a_map = (lambda m, s: (m, s, 0)) if na > 1 else (lambda m, s: (0, s, 0))
    b_map = (lambda m, s: (m, s, 0)) if nb > 1 else (lambda m, s: (0, s, 0))
    return pl.pallas_call(
        body,
        grid=(n, steps),
        in_specs=[pl.BlockSpec((1, ts, k_dim), a_map), pl.BlockSpec((1, ts, n_dim), b_map)],
        out_specs=pl.BlockSpec((1, k_dim, n_dim), lambda m, s: (m, 0, 0)),
        out_shape=jax.ShapeDtypeStruct((n, k_dim, n_dim), BF16),
        scratch_shapes=[pltpu.VMEM((k_dim, n_dim), F32)],
        compiler_params=_cparams(56),
        name=name,
    )(a, b)


def _head_sum_matrices():
    lane = lax.broadcasted_iota(jnp.int32, (ATT_W, LANES), 0) // HEAD_DIM
    col = lax.broadcasted_iota(jnp.int32, (ATT_W, LANES), 1)
    bd = (lane == col).astype(BF16)
    return bd, bd.T


def _head_mean(t, bd, bd_t):
    per_head = _dot_split2(t, bd) * (1.0 / HEAD_DIM)
    return _dot_split2(per_head, bd_t)


def _head_rms(x, bd, bd_t):
    per_head = _dot_split2(x * x, bd) * (1.0 / HEAD_DIM)
    r = lax.rsqrt(per_head + EPS)
    rw = _dot_split2(r, bd_t)
    return x * rw, rw


def _log_sigmoid(z):
    return jnp.minimum(z, 0.0) - jnp.log(1.0 + jnp.exp(-jnp.abs(z)))


def _inproj_fwd(x1, g, w_fox, w_fl, w_sb, w_gates, bias, qn, kn, bd, bd_t, tm=512):
    s_len = x1.shape[0]

    def body(x_ref, g_ref, wf_ref, wl_ref, ws_ref, wg_ref, bias_ref, qn_ref, kn_ref, bd_ref, bdt_ref,
             fq_ref, fk_ref, qs_ref, kf_ref, vf_ref, logf_ref, sq_ref, sk_ref, sv_ref, gates_ref, knorm_ref):
        xn, _ = _rms(x_ref[...])
        h = (xn * g_ref[...]).astype(BF16)
        zf = _dot(h, wf_ref[...])
        fq = zf[:, 0:ATT_W]
        fk = zf[:, ATT_W:2 * ATT_W]
        fq_ref[...] = fq
        fk_ref[...] = fk
        bd_m = bd_ref[...]
        bdt_m = bdt_ref[...]
        fqn, _ = _head_rms(fq, bd_m, bdt_m)
        fkn, _ = _head_rms(fk, bd_m, bdt_m)
        qs_ref[...] = (fqn * qn_ref[...]).astype(BF16) * QK_SCALE
        kf = (fkn * kn_ref[...]).astype(BF16)
        kf_ref[...] = kf
        k_sq = _dot_split2(jnp.square(kf.astype(F32)), bd_m)
        knorm_ref[...] = jnp.broadcast_to(jnp.max(k_sq, axis=0, keepdims=True), knorm_ref.shape)
        vf_ref[...] = zf[:, 2 * ATT_W:3 * ATT_W].astype(BF16)
        logf_ref[...] = _log_sigmoid(_dot(h, wl_ref[...]) + bias_ref[...])
        zs = _dot(h, ws_ref[...])
        sq_ref[...] = zs[:, 0:ATT_W].astype(BF16) * QK_SCALE
        sk_ref[...] = zs[:, ATT_W:2 * ATT_W].astype(BF16)
        sv_ref[...] = zs[:, 2 * ATT_W:3 * ATT_W].astype(BF16)
        gates_ref[...] = _dot(h, wg_ref[...]).astype(BF16)

    row = lambda i: (i, 0)
    full = lambda i: (0, 0)
    att = lambda dt: jax.ShapeDtypeStruct((s_len, ATT_W), dt)
    return pl.pallas_call(
        body,
        grid=(s_len // tm,),
        in_specs=[
            pl.BlockSpec((tm, D_MODEL), row),
            pl.BlockSpec((1, D_MODEL), full),
            pl.BlockSpec((D_MODEL, 3 * ATT_W), full),
            pl.BlockSpec((D_MODEL, LANES), full),
            pl.BlockSpec((D_MODEL, 3 * ATT_W), full),
            pl.BlockSpec((D_MODEL, 2 * D_MODEL), full),
            pl.BlockSpec((1, LANES), full),
            pl.BlockSpec((1, ATT_W), full),
            pl.BlockSpec((1, ATT_W), full),
            pl.BlockSpec((ATT_W, LANES), full),
            pl.BlockSpec((LANES, ATT_W), full),
        ],
        out_specs=[
            pl.BlockSpec((tm, ATT_W), row), pl.BlockSpec((tm, ATT_W), row),
            pl.BlockSpec((tm, ATT_W), row), pl.BlockSpec((tm, ATT_W), row), pl.BlockSpec((tm, ATT_W), row),
            pl.BlockSpec((tm, LANES), row),
            pl.BlockSpec((tm, ATT_W), row), pl.BlockSpec((tm, ATT_W), row), pl.BlockSpec((tm, ATT_W), row),
            pl.BlockSpec((tm, 2 * D_MODEL), row),
            pl.BlockSpec((8, LANES), row),
        ],
        out_shape=[
            att(F32), att(F32), att(BF16), att(BF16), att(BF16),
            jax.ShapeDtypeStruct((s_len, LANES), F32),
            att(BF16), att(BF16), att(BF16),
            jax.ShapeDtypeStruct((s_len, 2 * D_MODEL), BF16),
            jax.ShapeDtypeStruct((8 * (s_len // tm), LANES), F32),
        ],
        compiler_params=_cparams(56),
        name="inproj_fwd",
    )(x1, g, w_fox, w_fl, w_sb, w_gates, bias, qn, kn, bd, bd_t)


def _tri(n, kind):
    r = lax.broadcasted_iota(jnp.int32, (n, n), 0)
    c = lax.broadcasted_iota(jnp.int32, (n, n), 1)
    m = {"row_ge_col": r >= c, "row_le_col": r <= c, "row_gt_col": r > c, "row_lt_col": r < c}[kind]
    return m.astype(BF16)


def _cumsum_rows(x, reverse, tm=256):
    s_len = x.shape[0]
    nb = s_len // tm
    tri = _tri(tm, "row_le_col" if reverse else "row_ge_col")
    edge = 0 if reverse else tm - 1

    def body(x_ref, tri_ref, o_ref, carry_s):
        @pl.when(pl.program_id(0) == 0)
        def _():
            carry_s[...] = jnp.zeros_like(carry_s)

        hi, mid, lo = _split3(x_ref[...])
        t = tri_ref[...]
        y = _dot(t, hi) + _dot(t, mid) + _dot(t, lo) + carry_s[...]
        o_ref[...] = y
        carry_s[...] = y[edge:edge + 1, :]

    order = (lambda i: (nb - 1 - i, 0)) if reverse else (lambda i: (i, 0))
    return pl.pallas_call(
        body,
        grid=(nb,),
        in_specs=[pl.BlockSpec((tm, LANES), order), pl.BlockSpec((tm, tm), lambda i: (0, 0))],
        out_specs=pl.BlockSpec((tm, LANES), order),
        out_shape=jax.ShapeDtypeStruct((s_len, LANES), F32),
        scratch_shapes=[pltpu.VMEM((1, LANES), F32)],
        name="cumsum_rev" if reverse else "cumsum_fwd",
    )(x, tri)


def _unblocked_t(t4):
    _, nb, _, blk = t4.shape
    return t4.transpose(1, 3, 0, 2).reshape(nb * blk, ATT_W)


def _blocked_rows(t, blk):
    return t.reshape(t.shape[0] // blk, blk, t.shape[1])


def _pair_rows_t(f8, blk):
    nb = f8.shape[0] // blk
    t = f8.reshape(nb, blk, N_PAIRS, 2).transpose(2, 0, 3, 1)
    return jnp.pad(t, ((0, 0), (0, 0), (0, 6), (0, 0)))


def _unpair_rows_t(t4):
    _, nb, _, blk = t4.shape
    return t4[:, :, 0:2, :].transpose(1, 3, 0, 2).reshape(nb * blk, N_HEADS)


def _head_masks(tq):
    lane = lax.broadcasted_iota(jnp.int32, (tq, PAIR_W), 1)
    return lane < HEAD_DIM


def _causal_mask(tq, tk, offset, strict):
    d = lax.broadcasted_iota(jnp.int32, (tq, tk), 1) - lax.broadcasted_iota(jnp.int32, (tq, tk), 0)
    return (d < offset) if strict else (d <= offset)


def _heads_of(ref, first):
    t = ref[...]
    zero = jnp.zeros_like(t)
    return [jnp.where(first, t, zero), jnp.where(first, zero, t)]


def _head_cols(ref):
    t = ref[...]
    return [t[:, 0:1], t[:, HEAD_DIM:HEAD_DIM + 1]]


def _att_specs(s_len, tq):
    tk = ATT_BLOCK
    nq, nk = s_len // tq, s_len // tk
    return dict(
        nq=nq,
        q=pl.BlockSpec((tq, PAIR_W), lambda p, i: (i, p)),
        k_t=pl.BlockSpec((1, nk, PAIR_W, tk), lambda p, i: (p, 0, 0, 0)),
        k_rows=pl.BlockSpec((nk, tk, PAIR_W), lambda p, i: (0, 0, p)),
        f_t=pl.BlockSpec((1, nk, 8, tk), lambda p, i: (p, 0, 0, 0)),
        first=pl.BlockSpec((1, 1, 8, LANES), lambda p, i: (p, i, 0, 0)),
        wide=jax.ShapeDtypeStruct((s_len, ATT_W), F32),
        k_t_out=jax.ShapeDtypeStruct((N_PAIRS, nk, PAIR_W, tk), F32),
        f_t_out=jax.ShapeDtypeStruct((N_PAIRS, nk, 8, tk), F32),
        first_out=jax.ShapeDtypeStruct((N_PAIRS, nq, 8, LANES), F32),
        acc=pltpu.VMEM((2, tq, PAIR_W), F32),
    )


def _first_block(first_ref, limit):
    return jnp.clip(jnp.max(first_ref[0, 0]).astype(jnp.int32), 0, limit)


def _key_norm_bound(k_sq):
    bound = jnp.sqrt(jnp.max(k_sq[:, 0:N_HEADS], axis=0)).reshape(N_PAIRS, 2) * NORM_BOUND_MARGIN
    return jnp.broadcast_to(jnp.pad(bound, ((0, 0), (0, 6)))[:, :, None], (N_PAIRS, 8, LANES))


def _fox_fwd(qs, k3, v3, fw, ft4, kmax):
    tq, tk = FOX_Q_BLOCK, ATT_BLOCK
    sp = _att_specs(qs.shape[0], tq)
    ratio, nk = tq // tk, qs.shape[0] // tk
    f_block_ends = ft4[:, :, :2, tk - 1].reshape(-1)

    def body(fend_ref, q_ref, k_ref, v_ref, fw_ref, ft_ref, kmax_ref, y_ref, lse_ref, first_ref,
             acc_ref, max_ref, sum_ref):
        pair, i = pl.program_id(0), pl.program_id(1)
        first = _head_masks(tq)
        qh = _heads_of(q_ref, first)
        fqh = _head_cols(fw_ref)
        reach = []
        for n in range(2):
            qf = qh[n].astype(F32)
            reach.append(jnp.sqrt(jnp.sum(qf * qf, axis=-1, keepdims=True)) * kmax_ref[0, n:n + 1, 0:1] + fqh[n])

        def logits(j, shift, r0=0, diag=False):
            k, fk = k_ref[j], ft_ref[0, j]
            raw = [_dot_nt(qh[n][r0:], k) for n in range(2)]
            out = []
            for n in range(2):
                s = raw[n] + (shift[n][r0:] - fk[n:n + 1, :])
                if diag:
                    s = jnp.where(_causal_mask(tq - r0, tk, 0, strict=False), s, NEG_BIG)
                out.append(s)
            return out

        def max_pass(j, r0=0, diag=False, assign=False):
            ss = logits(j, fqh, r0, diag)
            for n in range(2):
                max_ref[n, r0:] = ss[n] if assign else jnp.maximum(max_ref[n, r0:], ss[n])

        def sum_pass(j, shift, r0=0, diag=False, assign=False):
            ps = [jnp.exp(s) for s in logits(j, shift, r0, diag)]
            v = v_ref[j]
            for n in range(2):
                sum_ref[n, r0:] = ps[n] if assign else sum_ref[n, r0:] + ps[n]
            for n in range(2):
                pv = _dot(ps[n].astype(BF16), v)
                acc_ref[n, r0:] = pv if assign else acc_ref[n, r0:] + pv

        for d in range(ratio):
            max_pass(ratio * i + d, d * tk, True, d == 0)

        m_diag = [jnp.max(max_ref[n], axis=-1, keepdims=True) for n in range(2)]
        slack = [jnp.max(reach[n] - m_diag[n]) for n in range(2)]

        def f_end(j, n):
            return fend_ref[(pair * nk + jnp.maximum(j, 0)) * 2 + n]

        def block_matters(j):
            gap = jnp.maximum(slack[0] - f_end(j, 0), slack[1] - f_end(j, 1))
            return (j >= 0) & (gap > -EXP_UNDERFLOW)

        last_left = ratio * i - 1
        j_first = lax.while_loop(block_matters, lambda j: j - 1, last_left) + 1

        bound = [reach[n] - f_end(last_left, n) for n in range(2)]
        excess = jnp.maximum(jnp.max(bound[0] - m_diag[0]), jnp.max(bound[1] - m_diag[1]))
        exact = excess > MAX_REFERENCE_EXCESS

        def exact_max():
            def one_max(j, c):
                max_pass(j)
                return c
            lax.fori_loop(j_first, ratio * i, one_max, 0)
            return [jnp.max(max_ref[n], axis=-1, keepdims=True) for n in range(2)]

        def bounded_max():
            walked_left = j_first < ratio * i
            return [jnp.maximum(m_diag[n], jnp.where(walked_left, bound[n], NEG_BIG)) for n in range(2)]

        m = lax.cond(exact, exact_max, bounded_max)
        shift = [fqh[n] - m[n] for n in range(2)]

        for d in range(ratio):
            sum_pass(ratio * i + d, shift, d * tk, True, d == 0)

        def one(j, c):
            sum_pass(j, shift)
            return c
        lax.fori_loop(j_first, ratio * i, one, 0)
        l = [jnp.sum(sum_ref[n], axis=-1, keepdims=True) for n in range(2)]
        y_ref[...] = jnp.where(first, acc_ref[0] / l[0], acc_ref[1] / l[1])
        lse_ref[...] = jnp.where(first, m[0] + jnp.log(l[0]), m[1] + jnp.log(l[1]))
        first_ref[...] = jnp.ones(first_ref.shape, F32) * j_first.astype(F32)

    tile = pltpu.VMEM((2, tq, tk), F32)
    return pl.pallas_call(
        body,
        grid=(N_PAIRS, sp["nq"]),
        in_specs=[pl.BlockSpec(memory_space=pltpu.SMEM), sp["q"], sp["k_rows"], sp["k_rows"], sp["q"], sp["f_t"],
                  pl.BlockSpec((1, 8, LANES), lambda p, i: (p, 0, 0))],
        out_specs=[sp["q"], sp["q"], sp["first"]],
        out_shape=[sp["wide"], sp["wide"], sp["first_out"]],
        scratch_shapes=[sp["acc"], tile, tile],
        compiler_params=_cparams(56),
        name="fox_fwd",
    )(f_block_ends, qs, k3, v3, fw, ft4, kmax)


def _fox_bwd(qs, k3, v3, dy, y, lse, fw, ft4, first_block):
    tq, tk = FOX_Q_BLOCK, ATT_BLOCK
    sp = _att_specs(qs.shape[0], tq)
    ratio = tq // tk

    def body(q_ref, k_ref, v_ref, dy_ref, y_ref, lse_ref, fw_ref, ft_ref, first_ref,
             dq_ref, dfq_ref, dkt_ref, dvt_ref, dft_ref, acc_ref):
        i = pl.program_id(1)

        @pl.when(i == 0)
        def _():
            dkt_ref[...] = jnp.zeros_like(dkt_ref)
            dvt_ref[...] = jnp.zeros_like(dvt_ref)
            dft_ref[...] = jnp.zeros_like(dft_ref)

        first = _head_masks(tq)
        qh = _heads_of(q_ref, first)
        dyv = dy_ref[...]
        dyb = dyv.astype(BF16)
        zero = jnp.zeros_like(dyb)
        dyh = [jnp.where(first, dyb, zero), jnp.where(first, zero, dyb)]
        prod = dyv * y_ref[...]
        zf = jnp.zeros_like(prod)
        delta = [jnp.sum(jnp.where(first, prod, zf), axis=-1, keepdims=True),
                 jnp.sum(jnp.where(first, zf, prod), axis=-1, keepdims=True)]
        fqh = _head_cols(fw_ref)
        lseh = _head_cols(lse_ref)
        shift = [fqh[n] - lseh[n] for n in range(2)]
        acc_ref[...] = jnp.zeros_like(acc_ref)

        def block(j, rows, r0=0, diag=False):
            mask = _causal_mask(tq - r0, tk, 0, strict=False) if diag else None
            k, v, fk = k_ref[j], v_ref[j], ft_ref[0, j]
            q_part, dy_part = [t[r0:] for t in qh], [t[r0:] for t in dyh]
            logits = [_dot_nt(q_part[n], k) for n in range(2)]
            dps = [_dot_nt(dy_part[n], v) for n in range(2)]
            pbs, dsbs, out = [], [], []
            for n in range(2):
                p = jnp.exp(logits[n] + (shift[n][r0:] - fk[n:n + 1, :]))
                if diag:
                    p = jnp.where(mask, p, 0.0)
                ds = p * (dps[n] - delta[n][r0:])
                pbs.append(p.astype(BF16))
                dsbs.append(ds.astype(BF16))
                row_sum = jnp.sum(ds, axis=-1, keepdims=True)
                if r0:
                    row_sum = jnp.concatenate([jnp.zeros((r0, 1), F32), row_sum], axis=0)
                out.append(rows[n] + row_sum)
                dft_ref[0, j, n:n + 1, :] -= _colsum(ds)
            for n in range(2):
                acc_ref[n, r0:] += _dot(dsbs[n], k)
            dkt_ref[0, j] += _dot_tn(q_part[0], dsbs[0]) + _dot_tn(q_part[1], dsbs[1])
            dvt_ref[0, j] += _dot_tn(dy_part[0], pbs[0]) + _dot_tn(dy_part[1], pbs[1])
            return tuple(out)

        rows = (jnp.zeros((tq, 1), F32),) * 2
        rows = lax.fori_loop(_first_block(first_ref, ratio * i), ratio * i, lambda j, c: block(j, c), rows)
        for d in range(ratio):
            rows = block(ratio * i + d, rows, d * tk, True)
        dq_ref[...] = jnp.where(first, acc_ref[0], acc_ref[1])
        lane = lax.broadcasted_iota(jnp.int32, (tq, 8), 1)
        dfq_ref[0] = jnp.where(lane == 0, rows[0], jnp.where(lane == 1, rows[1], 0.0))

    return pl.pallas_call(
        body,
        grid=(N_PAIRS, sp["nq"]),
        in_specs=[sp["q"], sp["k_rows"], sp["k_rows"], sp["q"], sp["q"], sp["q"], sp["q"], sp["f_t"], sp["first"]],
        out_specs=[sp["q"], pl.BlockSpec((1, tq, 8), lambda p, i: (p, i, 0)), sp["k_t"], sp["k_t"], sp["f_t"]],
        out_shape=[sp["wide"], jax.ShapeDtypeStruct((N_PAIRS, qs.shape[0], 8), F32),
                   sp["k_t_out"], sp["k_t_out"], sp["f_t_out"]],
        scratch_shapes=[sp["acc"]],
        compiler_params=_cparams(56),
        name="fox_bwd",
    )(qs, k3, v3, dy, y, lse, fw, ft4, first_block)


SIGN_BIT = 0x80000000


def _sb_terms(z, mask, diag):
    neg_abs = pltpu.bitcast(pltpu.bitcast(z, jnp.uint32) | jnp.uint32(SIGN_BIT), F32)
    lb = jnp.minimum(z, 0.0) - jnp.log(1.0 + jnp.exp(neg_abs))
    l1m = lb - z
    if diag:
        l1m = jnp.where(mask, l1m, 0.0)
    return lb, l1m


def _dot_split2_stacked(x, m2):
    hi, lo = _split2(x)
    return _dot(jnp.concatenate([hi, lo], axis=1), m2)


def _tri_stacked(kind):
    t = _tri(ATT_BLOCK, kind)
    return jnp.concatenate([t, t], axis=0)


def _sb_fwd(qs, k3, v3):
    tq, tk = SB_Q_BLOCK, ATT_BLOCK
    sp = _att_specs(qs.shape[0], tq)
    ratio = tq // tk
    upper = _tri_stacked("row_gt_col")

    def body(q_ref, k_ref, v_ref, u_ref, y_ref, rtot_ref, first_ref, acc_ref):
        i = pl.program_id(1)
        first = _head_masks(tq)
        qh = _heads_of(q_ref, first)
        u = u_ref[...]
        acc_ref[...] = jnp.zeros_like(acc_ref)

        def block(j, rs, diag):
            mask = _causal_mask(tq, tk, i * tq - j * tk, strict=True) if diag else None
            k, v = k_ref[j], v_ref[j]
            logits = [_dot_nt(qh[n], k) for n in range(2)]
            terms = [_sb_terms(z, mask, diag) for z in logits]
            right = [_dot_split2_stacked(l1m, u) for _, l1m in terms]
            weights = []
            for n in range(2):
                a = jnp.exp(terms[n][0] + right[n] + rs[n])
                if diag:
                    a = jnp.where(mask, a, 0.0)
                weights.append(a.astype(BF16))
            for n in range(2):
                acc_ref[n] += _dot(weights[n], v)
            return tuple(rs[n] + jnp.sum(terms[n][1], axis=-1, keepdims=True) for n in range(2))

        rs = (jnp.zeros((tq, 1), F32),) * 2
        for d in range(ratio):
            rs = block(ratio * i + (ratio - 1 - d), rs, True)

        def block_matters(c):
            j, r0, r1 = c
            return (j >= 0) & (jnp.max(jnp.maximum(r0, r1)) > -EXP_UNDERFLOW)

        def walk_left(c):
            j, r0, r1 = c
            r0, r1 = block(j, (r0, r1), False)
            return j - 1, r0, r1

        j, r0, r1 = lax.while_loop(block_matters, walk_left, (ratio * i - 1, rs[0], rs[1]))
        y_ref[...] = jnp.where(first, acc_ref[0], acc_ref[1])
        rtot_ref[...] = jnp.where(first, r0, r1)
        first_ref[...] = jnp.ones(first_ref.shape, F32) * (j + 1).astype(F32)

    return pl.pallas_call(
        body,
        grid=(N_PAIRS, sp["nq"]),
        in_specs=[sp["q"], sp["k_rows"], sp["k_rows"], pl.BlockSpec((2 * tk, tk), lambda p, i: (0, 0))],
        out_specs=[sp["q"], sp["q"], sp["first"]],
        out_shape=[sp["wide"], sp["wide"], sp["first_out"]],
        scratch_shapes=[sp["acc"]],
        compiler_params=_cparams(56),
        name="sb_fwd",
    )(qs, k3, v3, upper)


def _sb_bwd(qs, k3, v3, dy, rtot, first_block):
    tq, tk = SB_Q_BLOCK, ATT_BLOCK
    sp = _att_specs(qs.shape[0], tq)
    ratio = tq // tk
    lower_in = _tri_stacked("row_le_col")
    lower = _tri(tk, "row_lt_col")

    def body(q_ref, k_ref, v_ref, dy_ref, rtot_ref, first_ref, li_ref, l_ref, dq_ref, dkt_ref, dvt_ref, acc_ref):
        i = pl.program_id(1)

        @pl.when(i == 0)
        def _():
            dkt_ref[...] = jnp.zeros_like(dkt_ref)
            dvt_ref[...] = jnp.zeros_like(dvt_ref)

        first = _head_masks(tq)
        qh = _heads_of(q_ref, first)
        dyb = dy_ref[...].astype(BF16)
        zero = jnp.zeros_like(dyb)
        dyh = [jnp.where(first, dyb, zero), jnp.where(first, zero, dyb)]
        rtoth = _head_cols(rtot_ref)
        li = li_ref[...]
        lo_tri = l_ref[...]
        acc_ref[...] = jnp.zeros_like(acc_ref)

        def block(j, carry, diag):
            mask = _causal_mask(tq, tk, i * tq - j * tk, strict=True) if diag else None
            k, v = k_ref[j], v_ref[j]
            logits = [_dot_nt(qh[n], k) for n in range(2)]
            das = [_dot_nt(dyh[n], v) for n in range(2)]
            terms = [_sb_terms(z, mask, diag) for z in logits]
            upto = [_dot_split2_stacked(l1m, li) for _, l1m in terms]
            des, weights = [], []
            for n in range(2):
                a = jnp.exp(terms[n][0] + ((rtoth[n] - carry[2 * n]) - upto[n]))
                if diag:
                    a = jnp.where(mask, a, 0.0)
                des.append(a * das[n])
                weights.append(a.astype(BF16))
            lefts = [_dot(de.astype(BF16), lo_tri) for de in des]
            dzbs, out = [], []
            for n in range(2):
                beta = jnp.exp(terms[n][0])
                dz = des[n] - (des[n] + (carry[2 * n + 1] + lefts[n])) * beta
                if diag:
                    dz = jnp.where(mask, dz, 0.0)
                dzbs.append(dz.astype(BF16))
                out += [carry[2 * n] + jnp.sum(terms[n][1], axis=-1, keepdims=True),
                        carry[2 * n + 1] + jnp.sum(des[n], axis=-1, keepdims=True)]
            for n in range(2):
                acc_ref[n] += _dot(dzbs[n], k)
            dkt_ref[0, j] += _dot_tn(qh[0], dzbs[0]) + _dot_tn(qh[1], dzbs[1])
            dvt_ref[0, j] += _dot_tn(dyh[0], weights[0]) + _dot_tn(dyh[1], weights[1])
            return tuple(out)

        carry = (jnp.zeros((tq, 1), F32),) * 4
        carry = lax.fori_loop(_first_block(first_ref, ratio * i), ratio * i, lambda j, c: block(j, c, False), carry)
        for d in range(ratio):
            carry = block(ratio * i + d, carry, True)
        dq_ref[...] = jnp.where(first, acc_ref[0], acc_ref[1])

    return pl.pallas_call(
        body,
        grid=(N_PAIRS, sp["nq"]),
        in_specs=[sp["q"], sp["k_rows"], sp["k_rows"], sp["q"], sp["q"], sp["first"],
                  pl.BlockSpec((2 * tk, tk), lambda p, i: (0, 0)), pl.BlockSpec((tk, tk), lambda p, i: (0, 0))],
        out_specs=[sp["q"], sp["k_t"], sp["k_t"]],
        out_shape=[sp["wide"], sp["k_t_out"], sp["k_t_out"]],
        scratch_shapes=[sp["acc"]],
        compiler_params=_cparams(56),
        name="sb_bwd",
    )(qs, k3, v3, dy, rtot, first_block, lower_in, lower)


def _merge_fwd(x1, gates, y_fox, y_sb, w_bf, w_bs, w_out, tm=512):
    s_len = x1.shape[0]

    def body(x_ref, g_ref, yf_ref, ys_ref, wbf_ref, wbs_ref, wo_ref, o_ref):
        g = g_ref[...].astype(F32)
        of = _dot(yf_ref[...].astype(BF16), wbf_ref[...])
        os_ = _dot(ys_ref[...].astype(BF16), wbs_ref[...])
        merged = _sigmoid(g[:, 0:D_MODEL]) * of + _sigmoid(g[:, D_MODEL:]) * os_
        o_ref[...] = x_ref[...] + _dot(merged.astype(BF16), wo_ref[...])

    row = lambda i: (i, 0)
    full = lambda i: (0, 0)
    return pl.pallas_call(
        body,
        grid=(s_len // tm,),
        in_specs=[
            pl.BlockSpec((tm, D_MODEL), row),
            pl.BlockSpec((tm, 2 * D_MODEL), row),
            pl.BlockSpec((tm, ATT_W), row),
            pl.BlockSpec((tm, ATT_W), row),
            pl.BlockSpec((ATT_W, D_MODEL), full),
            pl.BlockSpec((ATT_W, D_MODEL), full),
            pl.BlockSpec((D_MODEL, D_MODEL), full),
        ],
        out_specs=pl.BlockSpec((tm, D_MODEL), row),
        out_shape=jax.ShapeDtypeStruct((s_len, D_MODEL), F32),
        compiler_params=_cparams(48),
        name="merge_fwd",
    )(x1, gates, y_fox, y_sb, w_bf, w_bs, w_out)


def _merge_bwd(dx2, gates, y_fox, y_sb, w_bf, w_bs, w_out, tm=512):
    s_len = dx2.shape[0]

    def body(d_ref, g_ref, yf_ref, ys_ref, wbf_ref, wbs_ref, wo_ref,
             dyf_ref, dys_ref, dg_ref, dof_ref, dos_ref, m_ref, dbf_ref):
        dbf = d_ref[...].astype(BF16)
        dbf_ref[...] = dbf
        dm = _dot_nt(dbf, wo_ref[...])
        g = g_ref[...].astype(F32)
        of = _dot(yf_ref[...].astype(BF16), wbf_ref[...])
        os_ = _dot(ys_ref[...].astype(BF16), wbs_ref[...])
        sf = _sigmoid(g[:, 0:D_MODEL])
        ss = _sigmoid(g[:, D_MODEL:])
        m_ref[...] = (sf * of + ss * os_).astype(BF16)
        d_of = (dm * sf).astype(BF16)
        d_os = (dm * ss).astype(BF16)
        dof_ref[...] = d_of
        dos_ref[...] = d_os
        dg_ref[:, 0:D_MODEL] = (dm * of * sf * (1.0 - sf)).astype(BF16)
        dg_ref[:, D_MODEL:] = (dm * os_ * ss * (1.0 - ss)).astype(BF16)
        dyf_ref[...] = _dot_nt(d_of, wbf_ref[...])
        dys_ref[...] = _dot_nt(d_os, wbs_ref[...])

    row = lambda i: (i, 0)
    full = lambda i: (0, 0)
    return pl.pallas_call(
        body,
        grid=(s_len // tm,),
        in_specs=[
            pl.BlockSpec((tm, D_MODEL), row),
            pl.BlockSpec((tm, 2 * D_MODEL), row),
            pl.BlockSpec((tm, ATT_W), row),
            pl.BlockSpec((tm, ATT_W), row),
            pl.BlockSpec((ATT_W, D_MODEL), full),
            pl.BlockSpec((ATT_W, D_MODEL), full),
            pl.BlockSpec((D_MODEL, D_MODEL), full),
        ],
        out_specs=[
            pl.BlockSpec((tm, ATT_W), row), pl.BlockSpec((tm, ATT_W), row),
            pl.BlockSpec((tm, 2 * D_MODEL), row),
            pl.BlockSpec((tm, D_MODEL), row), pl.BlockSpec((tm, D_MODEL), row),
            pl.BlockSpec((tm, D_MODEL), row), pl.BlockSpec((tm, D_MODEL), row),
        ],
        out_shape=[
            jax.ShapeDtypeStruct((s_len, ATT_W), F32), jax.ShapeDtypeStruct((s_len, ATT_W), F32),
            jax.ShapeDtypeStruct((s_len, 2 * D_MODEL), BF16),
            jax.ShapeDtypeStruct((s_len, D_MODEL), BF16), jax.ShapeDtypeStruct((s_len, D_MODEL), BF16),
            jax.ShapeDtypeStruct((s_len, D_MODEL), BF16), jax.ShapeDtypeStruct((s_len, D_MODEL), BF16),
        ],
        compiler_params=_cparams(56),
        name="merge_bwd",
    )(dx2, gates, y_fox, y_sb, w_bf, w_bs, w_out)


def _ple_loss(x3, p, g, w_pg, w_pp, target, tm=512):
    s_len = x3.shape[0]
    inv_d = 1.0 / D_MODEL

    def body(x_ref, p_ref, g_ref, wpg_ref, wpp_ref, t_ref,
             dx_ref, du_ref, dt_ref, hn_ref, dg_ref, loss_ref):
        @pl.when(pl.program_id(0) == 0)
        def _():
            dg_ref[...] = jnp.zeros_like(dg_ref)
            loss_ref[...] = jnp.zeros_like(loss_ref)

        x = x_ref[...]
        xn, r = _rms(x)
        gain = g_ref[...]
        hn = (xn * gain).astype(BF16)
        hn_ref[...] = hn
        sg = _sigmoid(_dot(hn, wpg_ref[...]))
        t = _dot(p_ref[...].astype(BF16), wpp_ref[...])
        err = x + sg * t - t_ref[...]
        sq = jnp.sum(_colsum(err * err), axis=-1, keepdims=True)
        loss_ref[...] += (0.5 * inv_d) * sq
        dy = err * inv_d
        du = (dy * t * sg * (1.0 - sg)).astype(BF16)
        du_ref[...] = du
        dt_ref[...] = (dy * sg).astype(BF16)
        dh = _dot_nt(du, wpg_ref[...])
        dx_ref[...] = dy + _rms_bwd(dh, xn, r, gain)
        dg_ref[0:1, :] += _colsum(dh * xn)

    row = lambda i: (i, 0)
    full = lambda i: (0, 0)
    bf = jax.ShapeDtypeStruct((s_len, D_MODEL), BF16)
    return pl.pallas_call(
        body,
        grid=(s_len // tm,),
        in_specs=[
            pl.BlockSpec((tm, D_MODEL), row),
            pl.BlockSpec((tm, PLE_DIM), row),
            pl.BlockSpec((1, D_MODEL), full),
            pl.BlockSpec((D_MODEL, D_MODEL), full),
            pl.BlockSpec((PLE_DIM, D_MODEL), full),
            pl.BlockSpec((tm, D_MODEL), row),
        ],
        out_specs=[
            pl.BlockSpec((tm, D_MODEL), row), pl.BlockSpec((tm, D_MODEL), row),
            pl.BlockSpec((tm, D_MODEL), row), pl.BlockSpec((tm, D_MODEL), row),
            pl.BlockSpec((8, D_MODEL), full), pl.BlockSpec((8, LANES), full),
        ],
        out_shape=[
            jax.ShapeDtypeStruct((s_len, D_MODEL), F32), bf, bf, bf,
            jax.ShapeDtypeStruct((8, D_MODEL), F32), jax.ShapeDtypeStruct((8, LANES), F32),
        ],
        compiler_params=_cparams(48),
        name="ple_loss",
    )(x3, p, g, w_pg, w_pp, target)


def _qknorm_bwd(fq, fk, dqs, dk, dv, qn, kn, bd, bd_t, tm=512):
    s_len = fq.shape[0]

    def body(fq_ref, fk_ref, dq_ref, dk_ref, dv_ref, qn_ref, kn_ref, bd_ref, bdt_ref,
             dz_ref, dqn_ref, dkn_ref):
        @pl.when(pl.program_id(0) == 0)
        def _():
            dqn_ref[...] = jnp.zeros_like(dqn_ref)
            dkn_ref[...] = jnp.zeros_like(dkn_ref)

        bd_m = bd_ref[...]
        bdt_m = bdt_ref[...]

        def one(x, dy, gain, dgain_ref):
            xn, rw = _head_rms(x, bd_m, bdt_m)
            dgain_ref[0:1, :] += _colsum(dy * xn)
            dxn = dy * gain
            return rw * (dxn - xn * _head_mean(dxn * xn, bd_m, bdt_m))

        dz_ref[:, 0:ATT_W] = one(fq_ref[...], dq_ref[...] * QK_SCALE, qn_ref[...], dqn_ref).astype(BF16)
        dz_ref[:, ATT_W:2 * ATT_W] = one(fk_ref[...], dk_ref[...], kn_ref[...], dkn_ref).astype(BF16)
        dz_ref[:, 2 * ATT_W:] = dv_ref[...].astype(BF16)

    row = lambda i: (i, 0)
    full = lambda i: (0, 0)
    att = pl.BlockSpec((tm, ATT_W), row)
    return pl.pallas_call(
        body,
        grid=(s_len // tm,),
        in_specs=[att, att, att, att, att,
                  pl.BlockSpec((1, ATT_W), full), pl.BlockSpec((1, ATT_W), full),
                  pl.BlockSpec((ATT_W, LANES), full), pl.BlockSpec((LANES, ATT_W), full)],
        out_specs=[pl.BlockSpec((tm, 3 * ATT_W), row), pl.BlockSpec((8, ATT_W), full), pl.BlockSpec((8, ATT_W), full)],
        out_shape=[jax.ShapeDtypeStruct((s_len, 3 * ATT_W), BF16),
                   jax.ShapeDtypeStruct((8, ATT_W), F32), jax.ShapeDtypeStruct((8, ATT_W), F32)],
        name="qknorm_bwd",
    )(fq, fk, dqs, dk, dv, qn, kn, bd, bd_t)


def _inproj_bwd(x1, dx2, g, dzf, dlogf, logf, dzs, dgates, w_fox, w_fl, w_sb, w_gates, tm=512):
    s_len = x1.shape[0]

    def body(x_ref, d_ref, g_ref, dzf_ref, dlf_ref, lf_ref, dzs_ref, dgt_ref, wf_ref, wl_ref, ws_ref, wg_ref,
             dx_ref, h_ref, dfl_ref, dg_ref, db_ref):
        @pl.when(pl.program_id(0) == 0)
        def _():
            dg_ref[...] = jnp.zeros_like(dg_ref)
            db_ref[...] = jnp.zeros_like(db_ref)

        xn, r = _rms(x_ref[...])
        gain = g_ref[...]
        h_ref[...] = (xn * gain).astype(BF16)
        lane = lax.broadcasted_iota(jnp.int32, (tm, LANES), 1)
        dfl = jnp.where(lane < N_HEADS, dlf_ref[...] * (1.0 - jnp.exp(lf_ref[...])), 0.0)
        db_ref[0:1, :] += _colsum(dfl)
        dflb = dfl.astype(BF16)
        dfl_ref[...] = dflb
        dh = (_dot_nt(dzf_ref[...], wf_ref[...]) + _dot_nt(dflb, wl_ref[...])
              + _dot_nt(dzs_ref[...], ws_ref[...]) + _dot_nt(dgt_ref[...], wg_ref[...]))
        dx_ref[...] = d_ref[...] + _rms_bwd(dh, xn, r, gain)
        dg_ref[0:1, :] += _colsum(dh * xn)

    row = lambda i: (i, 0)
    full = lambda i: (0, 0)
    return pl.pallas_call(
        body,
        grid=(s_len // tm,),
        in_specs=[
            pl.BlockSpec((tm, D_MODEL), row),
            pl.BlockSpec((tm, D_MODEL), row),
            pl.BlockSpec((1, D_MODEL), full),
            pl.BlockSpec((tm, 3 * ATT_W), row),
            pl.BlockSpec((tm, LANES), row),
            pl.BlockSpec((tm, LANES), row),
            pl.BlockSpec((tm, 3 * ATT_W), row),
            pl.BlockSpec((tm, 2 * D_MODEL), row),
            pl.BlockSpec((D_MODEL, 3 * ATT_W), full),
            pl.BlockSpec((D_MODEL, LANES), full),
            pl.BlockSpec((D_MODEL, 3 * ATT_W), full),
            pl.BlockSpec((D_MODEL, 2 * D_MODEL), full),
        ],
        out_specs=[
            pl.BlockSpec((tm, D_MODEL), row), pl.BlockSpec((tm, D_MODEL), row), pl.BlockSpec((tm, LANES), row),
            pl.BlockSpec((8, D_MODEL), full), pl.BlockSpec((8, LANES), full),
        ],
        out_shape=[
            jax.ShapeDtypeStruct((s_len, D_MODEL), F32), jax.ShapeDtypeStruct((s_len, D_MODEL), BF16),
            jax.ShapeDtypeStruct((s_len, LANES), BF16),
            jax.ShapeDtypeStruct((8, D_MODEL), F32), jax.ShapeDtypeStruct((8, LANES), F32),
        ],
        compiler_params=_cparams(56),
        name="inproj_bwd",
    )(x1, dx2, g, dzf, dlogf, logf, dzs, dgates, w_fox, w_fl, w_sb, w_gates)


def _split_w_in(w_in):
    o = 3 * ATT_W
    w_fox = w_in[:, 0:o]
    w_fl = jnp.pad(w_in[:, o:o + N_HEADS], ((0, 0), (0, LANES - N_HEADS)))
    w_sb = w_in[:, o + N_HEADS:2 * o + N_HEADS]
    w_gates = w_in[:, 2 * o + N_HEADS:]
    return w_fox, w_fl, w_sb, w_gates


def _local_grads(x, p, target, small, full, pending=None, send_early=None):
    blk = ATT_BLOCK
    bd, bd_t = _head_sum_matrices()
    full = dict(full)
    late = list(pending) if pending else []

    x1, a1, b1, u1, *gathered = _ffn_fwd(x, small["ffn1_norm"], full["ffn1_w_gate"], full["ffn1_w_up"],
                                     full["ffn1_w_down"], gather=[pending[k] for k in late])
    for k, gth in zip(late, gathered):
        full[k] = gth if k in KEPT_AS_SHARDS else _whole(k, gth)
    w_fox, w_fl, w_sb, w_gates = _split_w_in(full["w_in"])
    bias = jnp.pad(small["forget_bias"], ((0, 0), (0, LANES - N_HEADS)))
    qn = jnp.tile(small["q_norm"], (1, N_HEADS))
    kn = jnp.tile(small["k_norm"], (1, N_HEADS))
    fq, fk, f_qs, f_k, f_v, logf, s_qs, s_k, s_v, gates, f_k_sq = _inproj_fwd(
        x1, small["mix_norm"], w_fox, w_fl, w_sb, w_gates, bias, qn, kn, bd, bd_t)
    f_cum = _cumsum_rows(logf, reverse=False)
    f8 = f_cum[:, 0:N_HEADS]
    fw = jnp.repeat(f8, HEAD_DIM, axis=1)
    ft4 = _pair_rows_t(f8, blk)
    f_k3, f_v3 = _blocked_rows(f_k, blk), _blocked_rows(f_v, blk)
    y_fox, lse, f_first = _fox_fwd(f_qs, f_k3, f_v3, fw, ft4, _key_norm_bound(f_k_sq))
    s_k3, s_v3 = _blocked_rows(s_k, blk), _blocked_rows(s_v, blk)
    y_sb, s_rtot, s_first = _sb_fwd(s_qs, s_k3, s_v3)
    x2 = _merge_fwd(x1, gates, y_fox, y_sb, full["w_branch_fox"], full["w_branch_sb"], full["w_out"])
    x3, a2, b2, u2 = _ffn_fwd(x2, small["ffn2_norm"], full["ffn2_w_gate"], full["ffn2_w_up"], full["ffn2_w_down"])

    dx3, du_ple, dt_ple, hn_ple, dg_ple, loss_sum = _ple_loss(
        x3, p, small["ple_norm"], full["w_ple_gate"], full["w_ple_proj"], target)
    dx2, da2, db2, h_ffn2, d3_bf, dg_ffn2 = _ffn_bwd(
        x2, dx3, small["ffn2_norm"], a2, b2, full["ffn2_w_gate"], full["ffn2_w_up"], full["ffn2_w_down"])
    dy_fox, dy_sb, dgates, d_of, d_os, merged, d2_bf = _merge_bwd(
        dx2, gates, y_fox, y_sb, full["w_branch_fox"], full["w_branch_sb"], full["w_out"])

    f_dqs, dfq_p, f_dkt4, f_dvt4, dft4 = _fox_bwd(f_qs, f_k3, f_v3, dy_fox, y_fox, lse, fw, ft4, f_first)
    s_dqs, s_dkt4, s_dvt4 = _sb_bwd(s_qs, s_k3, s_v3, dy_sb, s_rtot, s_first)

    dzf, dqn8, dkn8 = _qknorm_bwd(fq, fk, f_dqs, _unblocked_t(f_dkt4), _unblocked_t(f_dvt4), qn, kn, bd, bd_t)
    dzs = jnp.concatenate([s_dqs * QK_SCALE, _unblocked_t(s_dkt4), _unblocked_t(s_dvt4)], axis=1).astype(BF16)
    dzs = lax.optimization_barrier(dzs)
    df8 = _unpair_rows_t(dft4) + dfq_p[:, :, 0:2].transpose(1, 0, 2).reshape(-1, N_HEADS)
    dlogf = _cumsum_rows(jnp.pad(df8, ((0, 0), (0, LANES - N_HEADS))), reverse=True)
    dx1, h_mix, dfl, dg_mix, dbias8 = _inproj_bwd(
        x1, dx2, small["mix_norm"], dzf, dlogf, logf, dzs, dgates, w_fox, w_fl, w_sb, w_gates)

    one = lambda t: t[None]
    gw = {}
    gw["ffn2_w_gate"] = _wgrad(da2, one(h_ffn2), name="wgrad_ffn2_gate")
    gw["ffn2_w_up"] = _wgrad(db2, one(h_ffn2), name="wgrad_ffn2_up")
    gw["ffn2_w_down"] = _wgrad(u2, one(d3_bf), scale=0.5, name="wgrad_ffn2_down")
    g_fox = _wgrad(one(h_mix), one(dzf), name="wgrad_in_fox")[0]
    g_fl = _wgrad(one(h_mix), one(dfl), name="wgrad_in_forget")[0]
    g_sb = _wgrad(one(h_mix), one(dzs), name="wgrad_in_sb")[0]
    g_gt = _wgrad(one(h_mix), one(dgates), name="wgrad_in_gates")[0]
    gw["w_in"] = jnp.concatenate([g_fox, g_fl[:, 0:N_HEADS], g_sb, g_gt], axis=1)
    gw["w_branch_fox"] = _wgrad(one(y_fox), one(d_of), name="wgrad_branch_fox")[0]
    gw["w_branch_sb"] = _wgrad(one(y_sb), one(d_os), name="wgrad_branch_sb")[0]
    gw["w_out"] = _wgrad(one(merged), one(d2_bf), name="wgrad_out")[0]
    gw["w_ple_gate"] = _wgrad(one(hn_ple), one(du_ple), name="wgrad_ple_gate")[0]
    gw["w_ple_proj"] = _wgrad(one(p), one(dt_ple), name="wgrad_ple_proj")[0]

    gw["ffn1_w_down"] = _wgrad(u1, one(dx1), scale=0.5, name="wgrad_ffn1_down")

    sent_names, to_send = send_early(gw) if send_early else ([], [])
    grad_x, da1, db1, h_ffn1, _, dg_ffn1, *landed = _ffn_bwd(
        x, dx1, small["ffn1_norm"], a1, b1, full["ffn1_w_gate"], full["ffn1_w_up"], full["ffn1_w_down"],
        scatter=to_send)
    gw["ffn1_w_gate"] = _wgrad(da1, one(h_ffn1), name="wgrad_ffn1_gate")
    gw["ffn1_w_up"] = _wgrad(db1, one(h_ffn1), name="wgrad_ffn1_up")

    fold = lambda t: jnp.sum(t[0:1].reshape(N_HEADS, HEAD_DIM), axis=0, keepdims=True)
    gs = {
        "ffn1_norm": dg_ffn1[0:1], "mix_norm": dg_mix[0:1], "ffn2_norm": dg_ffn2[0:1], "ple_norm": dg_ple[0:1],
        "forget_bias": dbias8[0:1, 0:N_HEADS], "q_norm": fold(dqn8), "k_norm": fold(dkn8),
    }
    return loss_sum, grad_x, gw, gs, dict(zip(sent_names, landed))


def _position():
    return lax.axis_index("x"), lax.axis_index("y"), lax.axis_index("c")


def _other_chips(x, y):
    return [(1 - x, y), (x, 1 - y), (1 - x, 1 - y)]


ANY = pl.BlockSpec(memory_space=pl.ANY)


def _place_own_shard(w, q):
    rows, cols = w.shape
    tr = _row_block(rows, cols * 4, budget=2 * MIB)

    def body(q_ref, w_ref, o_ref):
        o_ref[0] = w_ref[...].astype(BF16)

    return pl.pallas_call(
        body,
        grid_spec=pltpu.PrefetchScalarGridSpec(
            num_scalar_prefetch=1,
            grid=(rows // tr,),
            in_specs=[pl.BlockSpec((tr, cols), lambda i, q_ref: (i, 0))],
            out_specs=pl.BlockSpec((1, tr, cols), lambda i, q_ref: (q_ref[0], i, 0)),
        ),
        out_shape=jax.ShapeDtypeStruct((N_CHIPS, rows, cols), BF16),
        name="place_own_shard",
    )(q, w)


def _gather_semaphores(n):
    return [pltpu.SemaphoreType.DMA((6 * n,)), pltpu.SemaphoreType.DMA((6 * n,))]


def _gather_steps(bufs, send_sems, recv_sems):
    n = len(bufs)
    x, y, c = _position()
    q = 2 * x + y
    chips = _other_chips(x, y)
    sibling = (x, y, 1 - c)

    def half(a, slot, which):
        r2 = bufs[a].shape[1] // 2
        return bufs[a].at[slot, pl.ds(which * r2, r2), :]

    def copy(a, k, region, to):
        return pltpu.make_async_remote_copy(
            src_ref=region, dst_ref=region, send_sem=send_sems.at[6 * a + k], recv_sem=recv_sems.at[6 * a + k],
            device_id=to, device_id_type=MESH)

    def to_chip(a, k):
        tx, ty = chips[k]
        return copy(a, k, half(a, q, c), (tx, ty, c))

    def to_sibling(a, k):
        tx, ty = chips[k]
        return copy(a, 3 + k, half(a, 2 * tx + ty, c), sibling)

    def start():
        for a in range(n):
            for k in range(3):
                to_chip(a, k).start()

    def finish():
        for a in range(n):
            for k, (tx, ty) in enumerate(chips):
                copy(a, k, half(a, 2 * tx + ty, c), (tx, ty, c)).wait_recv()
                to_sibling(a, k).start()
        for a in range(n):
            for k, (tx, ty) in enumerate(chips):
                copy(a, 3 + k, half(a, 2 * tx + ty, 1 - c), sibling).wait_recv()
        for a in range(n):
            for k in range(3):
                to_chip(a, k).wait_send()
                to_sibling(a, k).wait_send()

    return start, finish


def _allgather_weights(slots):
    n = len(slots)

    def body(*refs):
        start, finish = _gather_steps(refs[n:2 * n], *refs[2 * n:])
        start()
        finish()

    return pl.pallas_call(
        body,
        in_specs=[ANY] * n,
        out_specs=[ANY] * n,
        out_shape=[jax.ShapeDtypeStruct(s.shape, s.dtype) for s in slots],
        input_output_aliases={a: a for a in range(n)},
        scratch_shapes=_gather_semaphores(n),
        name="allgather_weights",
    )(*slots)


def _exchange_pair_halves(grads):
    n = len(grads)

    def body(*refs):
        ins, outs = refs[0:n], refs[n:2 * n]
        send_sems, recv_sems = refs[2 * n:]
        x, y, c = _position()
        copies = []
        for a in range(n):
            r2 = grads[a].shape[1] // 2
            cp = pltpu.make_async_remote_copy(
                src_ref=ins[a].at[:, pl.ds((1 - c) * r2, r2), :], dst_ref=outs[a],
                send_sem=send_sems.at[a], recv_sem=recv_sems.at[a], device_id=(x, y, 1 - c), device_id_type=MESH)
            cp.start()
            copies.append(cp)
        for cp in copies:
            cp.wait()

    return pl.pallas_call(
        body,
        in_specs=[ANY] * n,
        out_specs=[ANY] * n,
        out_shape=[jax.ShapeDtypeStruct((N_CHIPS, g.shape[1] // 2, g.shape[2]), g.dtype) for g in grads],
        scratch_shapes=[pltpu.SemaphoreType.DMA((n,)), pltpu.SemaphoreType.DMA((n,))],
        name="rs_pair_exchange",
    )(*grads)


def _scatter_semaphores(n):
    return [pltpu.SemaphoreType.DMA((3 * n,)), pltpu.SemaphoreType.DMA((3 * n,)), pltpu.SemaphoreType.DMA((n,))]


def _scatter_steps(ins, outs, send_sems, recv_sems, local_sems):
    n = len(ins)
    x, y, c = _position()
    q = 2 * x + y
    chips = _other_chips(x, y)

    def own(a):
        return pltpu.make_async_copy(ins[a].at[q], outs[a].at[q], local_sems.at[a])

    def to_chip(a, k):
        tx, ty = chips[k]
        return pltpu.make_async_remote_copy(
            src_ref=ins[a].at[2 * tx + ty], dst_ref=outs[a].at[q],
            send_sem=send_sems.at[3 * a + k], recv_sem=recv_sems.at[3 * a + k],
            device_id=(tx, ty, c), device_id_type=MESH)

    def start():
        for a in range(n):
            own(a).start()
            for k in range(3):
                to_chip(a, k).start()

    def finish():
        for a in range(n):
            own(a).wait()
            for k in range(3):
                to_chip(a, k).wait()

    return start, finish


def _scatter_to_owner_chips(pairs):
    n = len(pairs)

    def body(*refs):
        start, finish = _scatter_steps(refs[0:n], refs[n:2 * n], *refs[2 * n:])
        start()
        finish()

    return pl.pallas_call(
        body,
        in_specs=[ANY] * n,
        out_specs=[ANY] * n,
        out_shape=[jax.ShapeDtypeStruct(p.shape, p.dtype) for p in pairs],
        scratch_shapes=_scatter_semaphores(n),
        name="rs_scatter",
    )(*pairs)


def _join_halves(shards):
    n = len(shards)

    def body(*refs):
        bufs = refs[n:2 * n]
        send_sems, recv_sems = refs[2 * n:]
        x, y, c = _position()
        started = []
        for a in range(n):
            r2 = shards[a].shape[0] // 2
            mine = bufs[a].at[pl.ds(c * r2, r2), :]
            cp = pltpu.make_async_remote_copy(
                src_ref=mine, dst_ref=mine, send_sem=send_sems.at[a], recv_sem=recv_sems.at[a],
                device_id=(x, y, 1 - c), device_id_type=MESH)
            cp.start()
            started.append(cp)
        for cp in started:
            cp.wait()

    return pl.pallas_call(
        body,
        in_specs=[ANY] * n,
        out_specs=[ANY] * n,
        out_shape=[jax.ShapeDtypeStruct(t.shape, t.dtype) for t in shards],
        input_output_aliases={a: a for a in range(n)},
        scratch_shapes=[pltpu.SemaphoreType.DMA((n,)), pltpu.SemaphoreType.DMA((n,))],
        name="rs_join_halves",
    )(*shards)


def _add_pair(g, got, c):
    _, r2, cols = got.shape

    def body(c_ref, g_ref, got_ref, o_ref):
        o_ref[...] = (g_ref[...].astype(F32) + got_ref[...].astype(F32)).astype(BF16)

    spec = pl.BlockSpec((1, r2, cols), lambda s, c_ref: (s, 0, 0))
    return pl.pallas_call(
        body,
        grid_spec=pltpu.PrefetchScalarGridSpec(
            num_scalar_prefetch=1,
            grid=(N_CHIPS,),
            in_specs=[pl.BlockSpec((1, r2, cols), lambda s, c_ref: (s, c_ref[0], 0)), spec],
            out_specs=spec,
        ),
        out_shape=jax.ShapeDtypeStruct(got.shape, BF16),
        name="rs_add_pair",
    )(c, g, got)


def _add_chips(parts, c):
    _, r2, cols = parts.shape

    def body(c_ref, p0, p1, p2, p3, o_ref):
        o_ref[...] = ((p0[0].astype(F32) + p1[0].astype(F32)) + p2[0].astype(F32)) + p3[0].astype(F32)

    specs = [pl.BlockSpec((1, r2, cols), functools.partial(lambda i, c_ref, s: (s, 0, 0), s=s))
             for s in range(N_CHIPS)]
    return pl.pallas_call(
        body,
        grid_spec=pltpu.PrefetchScalarGridSpec(
            num_scalar_prefetch=1,
            grid=(1,),
            in_specs=specs,
            out_specs=pl.BlockSpec((r2, cols), lambda i, c_ref: (c_ref[0], 0)),
        ),
        out_shape=jax.ShapeDtypeStruct((2 * r2, cols), F32),
        name="rs_add_chips",
    )(c, parts, parts, parts, parts)


def _allreduce_small(part):
    shape = part.shape

    def body(in_ref, out_ref, gather_ref, send_sems, recv_sems):
        x, y, c = _position()
        me = 4 * x + 2 * y + c
        relations = [(a, b, d) for a in (0, 1) for b in (0, 1) for d in (0, 1)][1:]
        flip = lambda v, f: 1 - v if f else v
        copies = []
        for k, (a, b, d) in enumerate(relations):
            cp = pltpu.make_async_remote_copy(
                src_ref=in_ref, dst_ref=gather_ref.at[me], send_sem=send_sems.at[k], recv_sem=recv_sems.at[k],
                device_id=(flip(x, a), flip(y, b), flip(c, d)), device_id_type=MESH)
            cp.start()
            copies.append(cp)
        gather_ref[me] = in_ref[...]
        for cp in copies:
            cp.wait()
        total = gather_ref[0]
        for dev in range(1, 8):
            total = total + gather_ref[dev]
        out_ref[...] = total

    vmem = pl.BlockSpec(memory_space=pltpu.VMEM)
    return pl.pallas_call(
        body,
        in_specs=[vmem],
        out_specs=vmem,
        out_shape=jax.ShapeDtypeStruct(shape, F32),
        scratch_shapes=[pltpu.VMEM((8,) + shape, F32), pltpu.SemaphoreType.DMA((7,)), pltpu.SemaphoreType.DMA((7,))],
        name="allreduce_small",
    )(part)


def _adamw(w, g, m, v):
    rows, cols = w.shape
    tr = _row_block(rows, cols * 4, budget=MIB)
    c1 = 1.0 / (1.0 - ADAM_B1 ** ADAM_STEP)
    c2 = 1.0 / (1.0 - ADAM_B2 ** ADAM_STEP)

    def body(w_ref, g_ref, m_ref, v_ref, d_ref, nm_ref, nv_ref):
        g_ = g_ref[...]
        nm = ADAM_B1 * m_ref[...] + (1.0 - ADAM_B1) * g_
        nv = ADAM_B2 * v_ref[...] + (1.0 - ADAM_B2) * (g_ * g_)
        nm_ref[...] = nm
        nv_ref[...] = nv
        d_ref[...] = -ADAM_LR * ((nm * c1) / (jnp.sqrt(nv * c2) + ADAM_EPS) + ADAM_WD * w_ref[...])

    spec = pl.BlockSpec((tr, cols), lambda i: (i, 0))
    out = jax.ShapeDtypeStruct((rows, cols), F32)
    return pl.pallas_call(
        body,
        grid=(rows // tr,),
        in_specs=[spec] * 4,
        out_specs=[spec] * 3,
        out_shape=[out] * 3,
        name="adamw",
    )(w, g, m, v)


BIG = ["ffn1_w_gate", "ffn1_w_up", "ffn1_w_down", "w_in", "w_branch_fox", "w_branch_sb", "w_out",
       "ffn2_w_gate", "ffn2_w_up", "ffn2_w_down", "w_ple_gate", "w_ple_proj"]
SMALL = ["ffn1_norm", "mix_norm", "ffn2_norm", "ple_norm", "forget_bias", "q_norm", "k_norm"]
COLUMN_SHARDED = ["w_in", "w_branch_fox", "w_branch_sb", "w_ple_proj"]
KEPT_AS_SHARDS = ["ffn1_w_gate", "ffn1_w_up", "ffn1_w_down", "ffn2_w_gate", "ffn2_w_up", "ffn2_w_down"]
WORKED_TRANSPOSED = ["ffn1_w_gate", "ffn1_w_up", "ffn2_w_gate", "ffn2_w_up"]
NEEDED_FIRST = ["ffn1_w_gate", "ffn1_w_up", "ffn1_w_down"]
READY_LAST = ["ffn1_w_gate", "ffn1_w_up"]
ORDER = ["ffn1_norm", "ffn1_w_gate", "ffn1_w_up", "ffn1_w_down", "mix_norm", "w_in", "forget_bias", "q_norm",
         "k_norm", "w_branch_fox", "w_branch_sb", "w_out", "ffn2_norm", "ffn2_w_gate", "ffn2_w_up",
         "ffn2_w_down", "ple_norm", "w_ple_gate", "w_ple_proj"]
SMALL_ROWS = {"ffn1_norm": 0, "mix_norm": 1, "ffn2_norm": 2, "ple_norm": 3}
SMALL_COLS = {"forget_bias": (0, N_HEADS), "q_norm": (N_HEADS, HEAD_DIM), "k_norm": (N_HEADS + HEAD_DIM, HEAD_DIM)}
LOSS_ROW = 5


def _stored(name, a):
    return jnp.swapaxes(a[0], 0, 1) if name in WORKED_TRANSPOSED else a[0]


def _returned(name, t):
    return (jnp.swapaxes(t, 0, 1) if name in WORKED_TRANSPOSED else t)[None]


def _whole(name, gathered):
    if name in COLUMN_SHARDED:
        return jnp.concatenate([gathered[s] for s in range(N_CHIPS)], axis=1)
    return gathered.reshape(-1, gathered.shape[-1])


def _as_shards(name, whole):
    if name in COLUMN_SHARDED:
        k, n = whole.shape
        return whole.reshape(k, N_CHIPS, n // N_CHIPS).transpose(1, 0, 2)
    return whole.reshape(N_CHIPS, whole.shape[0] // N_CHIPS, whole.shape[1])


def _pack_small(values, extra=None):
    rows = [values[k] for k in ("ffn1_norm", "mix_norm", "ffn2_norm", "ple_norm")]
    tail = jnp.concatenate([values["forget_bias"], values["q_norm"], values["k_norm"]], axis=1)
    rows.append(jnp.pad(tail, ((0, 0), (0, D_MODEL - tail.shape[1]))))
    packed = jnp.concatenate(rows + [jnp.zeros((3, D_MODEL), F32)], axis=0)
    if extra is not None:
        packed = packed.at[LOSS_ROW, 0].set(extra)
    return packed


def _unpack_small(packed):
    out = {k: packed[r:r + 1] for k, r in SMALL_ROWS.items()}
    for k, (start, size) in SMALL_COLS.items():
        out[k] = packed[4:5, start:start + size]
    return out


def kernel(x, p, ffn1_norm, ffn1_w_gate, ffn1_w_up, ffn1_w_down, mix_norm, w_in, forget_bias, q_norm, k_norm, w_branch_fox, w_branch_sb, w_out, ffn2_norm, ffn2_w_gate, ffn2_w_up, ffn2_w_down, ple_norm, w_ple_gate, w_ple_proj, loss_target, m_ffn1_norm, m_ffn1_w_gate, m_ffn1_w_up, m_ffn1_w_down, m_mix_norm, m_w_in, m_forget_bias, m_q_norm, m_k_norm, m_w_branch_fox, m_w_branch_sb, m_w_out, m_ffn2_norm, m_ffn2_w_gate, m_ffn2_w_up, m_ffn2_w_down, m_ple_norm, m_w_ple_gate, m_w_ple_proj, v_ffn1_norm, v_ffn1_w_gate, v_ffn1_w_up, v_ffn1_w_down, v_mix_norm, v_w_in, v_forget_bias, v_q_norm, v_k_norm, v_w_branch_fox, v_w_branch_sb, v_w_out, v_ffn2_norm, v_ffn2_w_gate, v_ffn2_w_up, v_ffn2_w_down, v_ple_norm, v_w_ple_gate, v_w_ple_proj):
    args = dict(locals())
    weights = {k: args[k] for k in ORDER}
    moments_m = {k: args["m_" + k] for k in ORDER}
    moments_v = {k: args["v_" + k] for k in ORDER}

    c_idx = lax.axis_index("c").astype(jnp.int32).reshape(1)
    q_idx = (2 * lax.axis_index("x") + lax.axis_index("y")).astype(jnp.int32).reshape(1)
    own = {k: _place_own_shard(_stored(k, weights[k]), q_idx) for k in BIG}
    full = dict(zip(NEEDED_FIRST, _allgather_weights([own[k] for k in NEEDED_FIRST])))
    pending = {k: own[k] for k in BIG if k not in NEEDED_FIRST}
    small = {k: weights[k] for k in SMALL}

    def pair_sums(names, gw):
        slots = [gw[k] if k in KEPT_AS_SHARDS else _as_shards(k, gw[k]) for k in names]
        from_core = _exchange_pair_halves(slots)
        return [_add_pair(g, got, c_idx) for g, got in zip(slots, from_core)]

    early = [k for k in BIG if k not in READY_LAST]
    loss_sum, grad_x, gw, gs, parts = _local_grads(
        x[0], p[0, 0], loss_target[0], small, full, pending, lambda ready: (early, pair_sums(early, ready)))

    parts.update(zip(READY_LAST, _scatter_to_owner_chips(pair_sums(READY_LAST, gw))))
    grads_big = dict(zip(BIG, _join_halves([_add_chips(parts[k], c_idx) for k in BIG])))
    reduced = _allreduce_small(_pack_small(gs, extra=loss_sum[0, 0]))
    grads_small = _unpack_small(reduced)
    loss = reduced[LOSS_ROW, 0]

    grads, deltas, new_m, new_v = {}, {}, {}, {}
    for k in BIG:
        d, nm, nv = _adamw(_stored(k, weights[k]), grads_big[k], _stored(k, moments_m[k]), _stored(k, moments_v[k]))
        grads[k], deltas[k], new_m[k], new_v[k] = (_returned(k, t) for t in (grads_big[k], d, nm, nv))
    d_s, nm_s, nv_s = _adamw(_pack_small({k: weights[k] for k in SMALL}), reduced,
                             _pack_small({k: moments_m[k] for k in SMALL}),
                             _pack_small({k: moments_v[k] for k in SMALL}))
    for k in SMALL:
        grads[k] = grads_small[k]
    for name, packed in (("d", d_s), ("m", nm_s), ("v", nv_s)):
        target = {"d": deltas, "m": new_m, "v": new_v}[name]
        target.update(_unpack_small(packed))

    return (loss, grad_x[None], *[grads[k] for k in ORDER], *[deltas[k] for k in ORDER],
            *[new_m[k] for k in ORDER], *[new_v[k] for k in ORDER])
```

```python
import functools

import jax
import jax.numpy as jnp
from jax import lax
from jax.experimental import pallas as pl
from jax.experimental.pallas import tpu as pltpu

F32 = jnp.float32
BF16 = jnp.bfloat16

D_MODEL = 1024
D_FF = 2816
N_CHIPS = 4
FF_SHARD = D_FF // N_CHIPS
FFN_CHUNKS = 2
WGRAD_TOKENS = 4096
WGRAD_VMEM = 30 * 1024 * 1024
HEAD_DIM = 64
N_HEADS = 8
ATT_W = N_HEADS * HEAD_DIM
PAIR_W = 2 * HEAD_DIM
N_PAIRS = N_HEADS // 2
PLE_DIM = 256
IN_WIDTH = 3 * ATT_W + N_HEADS + 3 * ATT_W + 2 * D_MODEL
EPS = 1e-6
QK_SCALE = HEAD_DIM ** -0.5
LANES = 128
ATT_BLOCK = 256
FOX_Q_BLOCK = 512
SB_Q_BLOCK = 256
NEG_BIG = -1e30
EXP_UNDERFLOW = 110.0
MAX_REFERENCE_EXCESS = 40.0
NORM_BOUND_MARGIN = 1.001

ADAM_LR = 0.001
ADAM_B1 = 0.9
ADAM_B2 = 0.999
ADAM_EPS = 1e-08
ADAM_WD = 0.01
ADAM_STEP = 10

MESH = pl.DeviceIdType.MESH
MIB = 1024 * 1024


def _cparams(vmem_mib=48):
    return pltpu.CompilerParams(vmem_limit_bytes=vmem_mib * MIB)


def _dot(a, b):
    return jnp.dot(a, b, preferred_element_type=F32)


def _dot_tn(a, b):
    return lax.dot_general(a, b, (((0,), (0,)), ((), ())), preferred_element_type=F32)


def _dot_nt(a, b):
    return lax.dot_general(a, b, (((1,), (1,)), ((), ())), preferred_element_type=F32)


def _sigmoid(x):
    return 1.0 / (1.0 + jnp.exp(-x))


def _split2(x):
    hi = x.astype(BF16)
    lo = (x - hi.astype(F32)).astype(BF16)
    return hi, lo


def _dot_split2(x, m):
    hi, lo = _split2(x)
    return _dot(hi, m) + _dot(lo, m)


def _split3(x):
    hi = x.astype(BF16)
    rest = x - hi.astype(F32)
    mid = rest.astype(BF16)
    lo = (rest - mid.astype(F32)).astype(BF16)
    return hi, mid, lo


def _rms(x):
    r = lax.rsqrt(jnp.mean(x * x, axis=-1, keepdims=True) + EPS)
    return x * r, r


def _rms_bwd(dh, xn, r, g):
    dxn = dh * g
    return r * (dxn - xn * jnp.mean(dxn * xn, axis=-1, keepdims=True))


def _colsum(x):
    return jnp.sum(x, axis=0, keepdims=True)


def _row_block(rows, row_bytes, budget):
    best = None
    for t in range(8, rows + 1, 8):
        if rows % t == 0 and t * row_bytes <= budget:
            best = t
    return best if best is not None else rows


def _ffn_fwd(x, g, wg, wu, wd, gather=(), tm=1024):
    s_len = x.shape[0]
    n = len(gather)
    steps = s_len // tm

    def body(x_ref, g_ref, wg_ref, wu_ref, wd_ref, *rest):
        o_ref, a_ref, b_ref, u_ref = rest[n:n + 4]
        h_s, acc_s = rest[2 * n + 4:2 * n + 6]
        i = pl.program_id(0)
        j = pl.program_id(1)
        if n:
            start, finish = _gather_steps(rest[n + 4:2 * n + 4], *rest[2 * n + 6:])
            pl.when((i == 0) & (j == 0))(start)

        @pl.when(j == 0)
        def _():
            xn, _ = _rms(x_ref[...])
            h_s[...] = (xn * g_ref[...]).astype(BF16)
            acc_s[...] = jnp.zeros_like(acc_s)

        chunks = [pl.ds(r * (tm // FFN_CHUNKS), tm // FFN_CHUNKS) for r in range(FFN_CHUNKS)]
        pre = [(_dot_nt(h_s[rows, :], wg_ref[0]), _dot_nt(h_s[rows, :], wu_ref[0])) for rows in chunks]
        us = []
        for rows, (a, b) in zip(chunks, pre):
            a_ref[0, rows, :] = a.astype(BF16)
            b_ref[0, rows, :] = b.astype(BF16)
            u = (a * _sigmoid(a) * b).astype(BF16)
            u_ref[0, rows, :] = u
            us.append(u)
        for rows, u in zip(chunks, us):
            acc_s[rows, :] += _dot(u, wd_ref[0])

        @pl.when(j == N_CHIPS - 1)
        def _():
            o_ref[...] = x_ref[...] + 0.5 * acc_s[...]

        if n:
            pl.when((i == steps - 1) & (j == N_CHIPS - 1))(finish)

    return pl.pallas_call(
        body,
        grid=(steps, N_CHIPS),
        in_specs=[
            pl.BlockSpec((tm, D_MODEL), lambda i, j: (i, 0)),
            pl.BlockSpec((1, D_MODEL), lambda i, j: (0, 0)),
            pl.BlockSpec((1, FF_SHARD, D_MODEL), lambda i, j: (j, 0, 0)),
            pl.BlockSpec((1, FF_SHARD, D_MODEL), lambda i, j: (j, 0, 0)),
            pl.BlockSpec((1, FF_SHARD, D_MODEL), lambda i, j: (j, 0, 0)),
        ] + [ANY] * n,
        out_specs=[pl.BlockSpec((tm, D_MODEL), lambda i, j: (i, 0))]
        + [pl.BlockSpec((1, tm, FF_SHARD), lambda i, j: (j, i, 0))] * 3 + [ANY] * n,
        out_shape=[jax.ShapeDtypeStruct((s_len, D_MODEL), F32)]
        + [jax.ShapeDtypeStruct((N_CHIPS, s_len, FF_SHARD), BF16)] * 3
        + [jax.ShapeDtypeStruct(s.shape, s.dtype) for s in gather],
        input_output_aliases={5 + a: 4 + a for a in range(n)},
        scratch_shapes=[pltpu.VMEM((tm, D_MODEL), BF16), pltpu.VMEM((tm, D_MODEL), F32)]
        + (_gather_semaphores(n) if n else []),
        compiler_params=_cparams(56),
        name="ffn_fwd_gathering" if n else "ffn_fwd",
    )(x, g, wg, wu, wd, *gather)


def _ffn_bwd(x, d, g, a_pre, b_pre, wg, wu, wd, scatter=(), tm=512):
    s_len = x.shape[0]
    nb = s_len // tm
    n = len(scatter)

    def body(x_ref, d_ref, g_ref, a_ref, b_ref, wg_ref, wu_ref, wd_ref, *rest):
        dx_ref, da_ref, db_ref, h_ref, dbf_ref, dg_ref = rest[n:n + 6]
        dbf_s, dh_s = rest[2 * n + 6:2 * n + 8]
        i = pl.program_id(0)
        j = pl.program_id(1)
        if n:
            start, finish = _scatter_steps(rest[0:n], rest[n + 6:2 * n + 6], *rest[2 * n + 8:])
            pl.when((i == 0) & (j == 0))(start)

        @pl.when(j == 0)
        def _():
            xn, _ = _rms(x_ref[...])
            h_ref[...] = (xn * g_ref[...]).astype(BF16)
            dbf = d_ref[...].astype(BF16)
            dbf_s[...] = dbf
            dbf_ref[...] = dbf
            dh_s[...] = jnp.zeros_like(dh_s)

        @pl.when((i == 0) & (j == 0))
        def _():
            dg_ref[...] = jnp.zeros_like(dg_ref)

        chunks = [pl.ds(r * (tm // FFN_CHUNKS), tm // FFN_CHUNKS) for r in range(FFN_CHUNKS)]
        dus = [0.5 * _dot_nt(dbf_s[rows, :], wd_ref[0]) for rows in chunks]
        das, dbs = [], []
        for rows, du in zip(chunks, dus):
            a = a_ref[0, rows, :].astype(F32)
            b = b_ref[0, rows, :].astype(F32)
            s = _sigmoid(a)
            silu = a * s
            da = (du * b * (s * (1.0 + a * (1.0 - s)))).astype(BF16)
            db = (du * silu).astype(BF16)
            da_ref[0, rows, :] = da
            db_ref[0, rows, :] = db
            das.append(da)
            dbs.append(db)
        for rows, da, db in zip(chunks, das, dbs):
            dh_s[rows, :] += _dot(da, wg_ref[0]) + _dot(db, wu_ref[0])

        @pl.when(j == N_CHIPS - 1)
        def _():
            xn, r = _rms(x_ref[...])
            dh = dh_s[...]
            dx_ref[...] = d_ref[...] + _rms_bwd(dh, xn, r, g_ref[...])
            dg_ref[0:1, :] += _colsum(dh * xn)

        if n:
            pl.when((i == nb - 1) & (j == N_CHIPS - 1))(finish)

    row = lambda i, j: (i, 0)
    shard = lambda i, j: (j, 0, 0)
    act = lambda i, j: (j, i, 0)
    return pl.pallas_call(
        body,
        grid=(nb, N_CHIPS),
        in_specs=[
            pl.BlockSpec((tm, D_MODEL), row),
            pl.BlockSpec((tm, D_MODEL), row),
            pl.BlockSpec((1, D_MODEL), lambda i, j: (0, 0)),
            pl.BlockSpec((1, tm, FF_SHARD), act),
            pl.BlockSpec((1, tm, FF_SHARD), act),
            pl.BlockSpec((1, FF_SHARD, D_MODEL), shard),
            pl.BlockSpec((1, FF_SHARD, D_MODEL), shard),
            pl.BlockSpec((1, FF_SHARD, D_MODEL), shard),
        ] + [ANY] * n,
        out_specs=[
            pl.BlockSpec((tm, D_MODEL), row),
            pl.BlockSpec((1, tm, FF_SHARD), act),
            pl.BlockSpec((1, tm, FF_SHARD), act),
            pl.BlockSpec((tm, D_MODEL), row),
            pl.BlockSpec((tm, D_MODEL), row),
            pl.BlockSpec((8, D_MODEL), lambda i, j: (0, 0)),
        ] + [ANY] * n,
        out_shape=[
            jax.ShapeDtypeStruct((s_len, D_MODEL), F32),
            jax.ShapeDtypeStruct((N_CHIPS, s_len, FF_SHARD), BF16),
            jax.ShapeDtypeStruct((N_CHIPS, s_len, FF_SHARD), BF16),
            jax.ShapeDtypeStruct((s_len, D_MODEL), BF16),
            jax.ShapeDtypeStruct((s_len, D_MODEL), BF16),
            jax.ShapeDtypeStruct((8, D_MODEL), F32),
        ] + [jax.ShapeDtypeStruct(s.shape, s.dtype) for s in scatter],
        scratch_shapes=[
            pltpu.VMEM((tm, D_MODEL), BF16),
            pltpu.VMEM((tm, D_MODEL), F32),
        ] + (_scatter_semaphores(n) if n else []),
        compiler_params=_cparams(56),
        name="ffn_bwd_scattering" if n else "ffn_bwd",
    )(x, d, g, a_pre, b_pre, wg, wu, wd, *scatter)


def _wgrad(a, b, scale=1.0, name="wgrad"):
    na, s_len, k_dim = a.shape
    nb, _, n_dim = b.shape
    n = max(na, nb)
    ts = WGRAD_TOKENS
    while ts > 512 and (ts > s_len or 2 * ts * (k_dim * a.dtype.itemsize + n_dim * b.dtype.itemsize) > WGRAD_VMEM):
        ts //= 2
    steps = s_len // ts

    def body(a_ref, b_ref, o_ref, acc_s):
        s = pl.program_id(1)

        @pl.when(s == 0)
        def _():
            acc_s[...] = jnp.zeros_like(acc_s)

        acc_s[...] += _dot_tn(a_ref[0].astype(BF16), b_ref[0].astype(BF16))

        @pl.when(s == steps - 1)
        def _():
            o_ref[0] = (acc_s[...] * scale).astype(BF16)

    a_map = (lambda m, s: (m, s, 0)) if na > 1 else (lambda m, s: (0, s, 0))
    b_map = (lambda m, s: (m, s, 0)) if nb > 1 else (lambda m, s: (0, s, 0))
    return pl.pallas_call(
        body,
        grid=(n, steps),
        in_specs=[pl.BlockSpec((1, ts, k_dim), a_map), pl.BlockSpec((1, ts, n_dim), b_map)],
        out_specs=pl.BlockSpec((1, k_dim, n_dim), lambda m, s: (m, 0, 0)),
        out_shape=jax.ShapeDtypeStruct((n, k_dim, n_dim), BF16),
        scratch_shapes=[pltpu.VMEM((k_dim, n_dim), F32)],
        compiler_params=_cparams(56),
        name=name,
    )(a, b)


def _head_sum_matrices():
    lane = lax.broadcasted_iota(jnp.int32, (ATT_W, LANES), 0) // HEAD_DIM
    col = lax.broadcasted_iota(jnp.int32, (ATT_W, LANES), 1)
    bd = (lane == col).astype(BF16)
    return bd, bd.T


def _head_mean(t, bd, bd_t):
    per_head = _dot_split2(t, bd) * (1.0 / HEAD_DIM)
    return _dot_split2(per_head, bd_t)


def _head_rms(x, bd, bd_t):
    per_head = _dot_split2(x * x, bd) * (1.0 / HEAD_DIM)
    r = lax.rsqrt(per_head + EPS)
    rw = _dot_split2(r, bd_t)
    return x * rw, rw


def _log_sigmoid(z):
    return jnp.minimum(z, 0.0) - jnp.log(1.0 + jnp.exp(-jnp.abs(z)))


def _inproj_fwd(x1, g, w_fox, w_fl, w_sb, w_gates, bias, qn, kn, bd, bd_t, tm=512):
    s_len = x1.shape[0]

    def body(x_ref, g_ref, wf_ref, wl_ref, ws_ref, wg_ref, bias_ref, qn_ref, kn_ref, bd_ref, bdt_ref,
             fq_ref, fk_ref, qs_ref, kf_ref, vf_ref, logf_ref, sq_ref, sk_ref, sv_ref, gates_ref, knorm_ref):
        xn, _ = _rms(x_ref[...])
        h = (xn * g_ref[...]).astype(BF16)
        zf = _dot(h, wf_ref[...])
        fq = zf[:, 0:ATT_W]
        fk = zf[:, ATT_W:2 * ATT_W]
        fq_ref[...] = fq
        fk_ref[...] = fk
        bd_m = bd_ref[...]
        bdt_m = bdt_ref[...]
        fqn, _ = _head_rms(fq, bd_m, bdt_m)
        fkn, _ = _head_rms(fk, bd_m, bdt_m)
        qs_ref[...] = (fqn * qn_ref[...]).astype(BF16) * QK_SCALE
        kf = (fkn * kn_ref[...]).astype(BF16)
        kf_ref[...] = kf
        k_sq = _dot_split2(jnp.square(kf.astype(F32)), bd_m)
        knorm_ref[...] = jnp.broadcast_to(jnp.max(k_sq, axis=0, keepdims=True), knorm_ref.shape)
        vf_ref[...] = zf[:, 2 * ATT_W:3 * ATT_W].astype(BF16)
        logf_ref[...] = _log_sigmoid(_dot(h, wl_ref[...]) + bias_ref[...])
        zs = _dot(h, ws_ref[...])
        sq_ref[...] = zs[:, 0:ATT_W].astype(BF16) * QK_SCALE
        sk_ref[...] = zs[:, ATT_W:2 * ATT_W].astype(BF16)
        sv_ref[...] = zs[:, 2 * ATT_W:3 * ATT_W].astype(BF16)
        gates_ref[...] = _dot(h, wg_ref[...]).astype(BF16)

    row = lambda i: (i, 0)
    full = lambda i: (0, 0)
    att = lambda dt: jax.ShapeDtypeStruct((s_len, ATT_W), dt)
    return pl.pallas_call(
        body,
        grid=(s_len // tm,),
        in_specs=[
            pl.BlockSpec((tm, D_MODEL), row),
            pl.BlockSpec((1, D_MODEL), full),
            pl.BlockSpec((D_MODEL, 3 * ATT_W), full),
            pl.BlockSpec((D_MODEL, LANES), full),
            pl.BlockSpec((D_MODEL, 3 * ATT_W), full),
            pl.BlockSpec((D_MODEL, 2 * D_MODEL), full),
            pl.BlockSpec((1, LANES), full),
            pl.BlockSpec((1, ATT_W), full),
            pl.BlockSpec((1, ATT_W), full),
            pl.BlockSpec((ATT_W, LANES), full),
            pl.BlockSpec((LANES, ATT_W), full),
        ],
        out_specs=[
            pl.BlockSpec((tm, ATT_W), row), pl.BlockSpec((tm, ATT_W), row),
            pl.BlockSpec((tm, ATT_W), row), pl.BlockSpec((tm, ATT_W), row), pl.BlockSpec((tm, ATT_W), row),
            pl.BlockSpec((tm, LANES), row),
            pl.BlockSpec((tm, ATT_W), row), pl.BlockSpec((tm, ATT_W), row), pl.BlockSpec((tm, ATT_W), row),
            pl.BlockSpec((tm, 2 * D_MODEL), row),
            pl.BlockSpec((8, LANES), row),
        ],
        out_shape=[
            att(F32), att(F32), att(BF16), att(BF16), att(BF16),
            jax.ShapeDtypeStruct((s_len, LANES), F32),
            att(BF16), att(BF16), att(BF16),
            jax.ShapeDtypeStruct((s_len, 2 * D_MODEL), BF16),
            jax.ShapeDtypeStruct((8 * (s_len // tm), LANES), F32),
        ],
        compiler_params=_cparams(56),
        name="inproj_fwd",
    )(x1, g, w_fox, w_fl, w_sb, w_gates, bias, qn, kn, bd, bd_t)


def _tri(n, kind):
    r = lax.broadcasted_iota(jnp.int32, (n, n), 0)
    c = lax.broadcasted_iota(jnp.int32, (n, n), 1)
    m = {"row_ge_col": r >= c, "row_le_col": r <= c, "row_gt_col": r > c, "row_lt_col": r < c}[kind]
    return m.astype(BF16)


def _cumsum_rows(x, reverse, tm=256):
    s_len = x.shape[0]
    nb = s_len // tm
    tri = _tri(tm, "row_le_col" if reverse else "row_ge_col")
    edge = 0 if reverse else tm - 1

    def body(x_ref, tri_ref, o_ref, carry_s):
        @pl.when(pl.program_id(0) == 0)
        def _():
            carry_s[...] = jnp.zeros_like(carry_s)

        hi, mid, lo = _split3(x_ref[...])
        t = tri_ref[...]
        y = _dot(t, hi) + _dot(t, mid) + _dot(t, lo) + carry_s[...]
        o_ref[...] = y
        carry_s[...] = y[edge:edge + 1, :]

    order = (lambda i: (nb - 1 - i, 0)) if reverse else (lambda i: (i, 0))
    return pl.pallas_call(
        body,
        grid=(nb,),
        in_specs=[pl.BlockSpec((tm, LANES), order), pl.BlockSpec((tm, tm), lambda i: (0, 0))],
        out_specs=pl.BlockSpec((tm, LANES), order),
        out_shape=jax.ShapeDtypeStruct((s_len, LANES), F32),
        scratch_shapes=[pltpu.VMEM((1, LANES), F32)],
        name="cumsum_rev" if reverse else "cumsum_fwd",
    )(x, tri)


def _unblocked_t(t4):
    _, nb, _, blk = t4.shape
    return t4.transpose(1, 3, 0, 2).reshape(nb * blk, ATT_W)


def _transposed_spec(tm):
    return pl.BlockSpec((N_PAIRS, tm // ATT_BLOCK, PAIR_W, ATT_BLOCK), lambda i: (0, i, 0, 0))


def _rows_of_transposed(ref):
    return jnp.concatenate(
        [jnp.concatenate([ref[p, b].T for p in range(N_PAIRS)], axis=1) for b in range(ref.shape[1])], axis=0)


def _blocked_rows(t, blk):
    return t.reshape(t.shape[0] // blk, blk, t.shape[1])


def _pair_rows_t(f8, blk):
    nb = f8.shape[0] // blk
    t = f8.reshape(nb, blk, N_PAIRS, 2).transpose(2, 0, 3, 1)
    return jnp.pad(t, ((0, 0), (0, 0), (0, 6), (0, 0)))


def _unpair_rows_t(t4):
    _, nb, _, blk = t4.shape
    return t4[:, :, 0:2, :].transpose(1, 3, 0, 2).reshape(nb * blk, N_HEADS)


def _head_masks(tq):
    lane = lax.broadcasted_iota(jnp.int32, (tq, PAIR_W), 1)
    return lane < HEAD_DIM


def _causal_mask(tq, tk, offset, strict):
    d = lax.broadcasted_iota(jnp.int32, (tq, tk), 1) - lax.broadcasted_iota(jnp.int32, (tq, tk), 0)
    return (d < offset) if strict else (d <= offset)


def _heads_of(ref, first):
    t = ref[...]
    zero = jnp.zeros_like(t)
    return [jnp.where(first, t, zero), jnp.where(first, zero, t)]


def _head_cols(ref):
    t = ref[...]
    return [t[:, 0:1], t[:, HEAD_DIM:HEAD_DIM + 1]]


def _att_specs(s_len, tq):
    tk = ATT_BLOCK
    nq, nk = s_len // tq, s_len // tk
    return dict(
        nq=nq,
        q=pl.BlockSpec((tq, PAIR_W), lambda p, i: (i, p)),
        k_t=pl.BlockSpec((1, nk, PAIR_W, tk), lambda p, i: (p, 0, 0, 0)),
        k_rows=pl.BlockSpec((nk, tk, PAIR_W), lambda p, i: (0, 0, p)),
        f_t=pl.BlockSpec((1, nk, 8, tk), lambda p, i: (p, 0, 0, 0)),
        first=pl.BlockSpec((1, 1, 8, LANES), lambda p, i: (p, i, 0, 0)),
        wide=jax.ShapeDtypeStruct((s_len, ATT_W), F32),
        k_t_out=jax.ShapeDtypeStruct((N_PAIRS, nk, PAIR_W, tk), F32),
        f_t_out=jax.ShapeDtypeStruct((N_PAIRS, nk, 8, tk), F32),
        first_out=jax.ShapeDtypeStruct((N_PAIRS, nq, 8, LANES), F32),
        acc=pltpu.VMEM((2, tq, PAIR_W), F32),
    )


def _first_block(first_ref, limit):
    return jnp.clip(jnp.max(first_ref[0, 0]).astype(jnp.int32), 0, limit)


def _key_norm_bound(k_sq):
    bound = jnp.sqrt(jnp.max(k_sq[:, 0:N_HEADS], axis=0)).reshape(N_PAIRS, 2) * NORM_BOUND_MARGIN
    return jnp.broadcast_to(jnp.pad(bound, ((0, 0), (0, 6)))[:, :, None], (N_PAIRS, 8, LANES))


def _fox_fwd(qs, k3, v3, fw, ft4, kmax):
    tq, tk = FOX_Q_BLOCK, ATT_BLOCK
    sp = _att_specs(qs.shape[0], tq)
    ratio, nk = tq // tk, qs.shape[0] // tk
    f_block_ends = ft4[:, :, :2, tk - 1].reshape(-1)

    def body(fend_ref, q_ref, k_ref, v_ref, fw_ref, ft_ref, kmax_ref, y_ref, lse_ref, first_ref,
             acc_ref, max_ref, sum_ref):
        pair, i = pl.program_id(0), pl.program_id(1)
        first = _head_masks(tq)
        qh = _heads_of(q_ref, first)
        fqh = _head_cols(fw_ref)
        reach = []
        for n in range(2):
            qf = qh[n].astype(F32)
            reach.append(jnp.sqrt(jnp.sum(qf * qf, axis=-1, keepdims=True)) * kmax_ref[0, n:n + 1, 0:1] + fqh[n])

        def logits(j, shift, r0=0, diag=False):
            k, fk = k_ref[j], ft_ref[0, j]
            raw = [_dot_nt(qh[n][r0:], k) for n in range(2)]
            out = []
            for n in range(2):
                s = raw[n] + (shift[n][r0:] - fk[n:n + 1, :])
                if diag:
                    s = jnp.where(_causal_mask(tq - r0, tk, 0, strict=False), s, NEG_BIG)
                out.append(s)
            return out

        def max_pass(j, r0=0, diag=False, assign=False):
            ss = logits(j, fqh, r0, diag)
            for n in range(2):
                max_ref[n, r0:] = ss[n] if assign else jnp.maximum(max_ref[n, r0:], ss[n])

        def sum_pass(j, shift, r0=0, diag=False, assign=False):
            ps = [jnp.exp(s) for s in logits(j, shift, r0, diag)]
            v = v_ref[j]
            for n in range(2):
                sum_ref[n, r0:] = ps[n] if assign else sum_ref[n, r0:] + ps[n]
            for n in range(2):
                pv = _dot(ps[n].astype(BF16), v)
                acc_ref[n, r0:] = pv if assign else acc_ref[n, r0:] + pv

        for d in range(ratio):
            max_pass(ratio * i + d, d * tk, True, d == 0)

        m_diag = [jnp.max(max_ref[n], axis=-1, keepdims=True) for n in range(2)]
        slack = [jnp.max(reach[n] - m_diag[n]) for n in range(2)]

        def f_end(j, n):
            return fend_ref[(pair * nk + jnp.maximum(j, 0)) * 2 + n]

        def block_matters(j):
            gap = jnp.maximum(slack[0] - f_end(j, 0), slack[1] - f_end(j, 1))
            return (j >= 0) & (gap > -EXP_UNDERFLOW)

        last_left = ratio * i - 1
        j_first = lax.while_loop(block_matters, lambda j: j - 1, last_left) + 1

        bound = [reach[n] - f_end(last_left, n) for n in range(2)]
        excess = jnp.maximum(jnp.max(bound[0] - m_diag[0]), jnp.max(bound[1] - m_diag[1]))
        exact = excess > MAX_REFERENCE_EXCESS

        def exact_max():
            def one_max(j, c):
                max_pass(j)
                return c
            lax.fori_loop(j_first, ratio * i, one_max, 0)
            return [jnp.max(max_ref[n], axis=-1, keepdims=True) for n in range(2)]

        def bounded_max():
            walked_left = j_first < ratio * i
            return [jnp.maximum(m_diag[n], jnp.where(walked_left, bound[n], NEG_BIG)) for n in range(2)]

        m = lax.cond(exact, exact_max, bounded_max)
        shift = [fqh[n] - m[n] for n in range(2)]

        for d in range(ratio):
            sum_pass(ratio * i + d, shift, d * tk, True, d == 0)

        def one(j, c):
            sum_pass(j, shift)
            return c
        lax.fori_loop(j_first, ratio * i, one, 0)
        l = [jnp.sum(sum_ref[n], axis=-1, keepdims=True) for n in range(2)]
        y_ref[...] = jnp.where(first, acc_ref[0] / l[0], acc_ref[1] / l[1])
        lse_ref[...] = jnp.where(first, m[0] + jnp.log(l[0]), m[1] + jnp.log(l[1]))
        first_ref[...] = jnp.ones(first_ref.shape, F32) * j_first.astype(F32)

    tile = pltpu.VMEM((2, tq, tk), F32)
    return pl.pallas_call(
        body,
        grid=(N_PAIRS, sp["nq"]),
        in_specs=[pl.BlockSpec(memory_space=pltpu.SMEM), sp["q"], sp["k_rows"], sp["k_rows"], sp["q"], sp["f_t"],
                  pl.BlockSpec((1, 8, LANES), lambda p, i: (p, 0, 0))],
        out_specs=[sp["q"], sp["q"], sp["first"]],
        out_shape=[sp["wide"], sp["wide"], sp["first_out"]],
        scratch_shapes=[sp["acc"], tile, tile],
        compiler_params=_cparams(56),
        name="fox_fwd",
    )(f_block_ends, qs, k3, v3, fw, ft4, kmax)


def _fox_bwd(qs, k3, v3, dy, y, lse, fw, ft4, first_block):
    tq, tk = FOX_Q_BLOCK, ATT_BLOCK
    sp = _att_specs(qs.shape[0], tq)
    ratio = tq // tk

    def body(q_ref, k_ref, v_ref, dy_ref, y_ref, lse_ref, fw_ref, ft_ref, first_ref,
             dq_ref, dfq_ref, dkt_ref, dvt_ref, dft_ref, acc_ref):
        i = pl.program_id(1)

        @pl.when(i == 0)
        def _():
            dkt_ref[...] = jnp.zeros_like(dkt_ref)
            dvt_ref[...] = jnp.zeros_like(dvt_ref)
            dft_ref[...] = jnp.zeros_like(dft_ref)

        first = _head_masks(tq)
        qh = _heads_of(q_ref, first)
        dyv = dy_ref[...]
        dyb = dyv.astype(BF16)
        zero = jnp.zeros_like(dyb)
        dyh = [jnp.where(first, dyb, zero), jnp.where(first, zero, dyb)]
        prod = dyv * y_ref[...]
        zf = jnp.zeros_like(prod)
        delta = [jnp.sum(jnp.where(first, prod, zf), axis=-1, keepdims=True),
                 jnp.sum(jnp.where(first, zf, prod), axis=-1, keepdims=True)]
        fqh = _head_cols(fw_ref)
        lseh = _head_cols(lse_ref)
        shift = [fqh[n] - lseh[n] for n in range(2)]
        acc_ref[...] = jnp.zeros_like(acc_ref)

        def block(j, rows, r0=0, diag=False):
            mask = _causal_mask(tq - r0, tk, 0, strict=False) if diag else None
            k, v, fk = k_ref[j], v_ref[j], ft_ref[0, j]
            q_part, dy_part = [t[r0:] for t in qh], [t[r0:] for t in dyh]
            logits = [_dot_nt(q_part[n], k) for n in range(2)]
            dps = [_dot_nt(dy_part[n], v) for n in range(2)]
            pbs, dsbs, out = [], [], []
            for n in range(2):
                p = jnp.exp(logits[n] + (shift[n][r0:] - fk[n:n + 1, :]))
                if diag:
                    p = jnp.where(mask, p, 0.0)
                ds = p * (dps[n] - delta[n][r0:])
                pbs.append(p.astype(BF16))
                dsbs.append(ds.astype(BF16))
                row_sum = jnp.sum(ds, axis=-1, keepdims=True)
                if r0:
                    row_sum = jnp.concatenate([jnp.zeros((r0, 1), F32), row_sum], axis=0)
                out.append(rows[n] + row_sum)
                dft_ref[0, j, n:n + 1, :] -= _colsum(ds)
            for n in range(2):
                acc_ref[n, r0:] += _dot(dsbs[n], k)
            dkt_ref[0, j] += _dot_tn(q_part[0], dsbs[0]) + _dot_tn(q_part[1], dsbs[1])
            dvt_ref[0, j] += _dot_tn(dy_part[0], pbs[0]) + _dot_tn(dy_part[1], pbs[1])
            return tuple(out)

        rows = (jnp.zeros((tq, 1), F32),) * 2
        rows = lax.fori_loop(_first_block(first_ref, ratio * i), ratio * i, lambda j, c: block(j, c), rows)
        for d in range(ratio):
            rows = block(ratio * i + d, rows, d * tk, True)
        dq_ref[...] = jnp.where(first, acc_ref[0], acc_ref[1])
        lane = lax.broadcasted_iota(jnp.int32, (tq, 8), 1)
        dfq_ref[0] = jnp.where(lane == 0, rows[0], jnp.where(lane == 1, rows[1], 0.0))

    return pl.pallas_call(
        body,
        grid=(N_PAIRS, sp["nq"]),
        in_specs=[sp["q"], sp["k_rows"], sp["k_rows"], sp["q"], sp["q"], sp["q"], sp["q"], sp["f_t"], sp["first"]],
        out_specs=[sp["q"], pl.BlockSpec((1, tq, 8), lambda p, i: (p, i, 0)), sp["k_t"], sp["k_t"], sp["f_t"]],
        out_shape=[sp["wide"], jax.ShapeDtypeStruct((N_PAIRS, qs.shape[0], 8), F32),
                   sp["k_t_out"], sp["k_t_out"], sp["f_t_out"]],
        scratch_shapes=[sp["acc"]],
        compiler_params=_cparams(56),
        name="fox_bwd",
    )(qs, k3, v3, dy, y, lse, fw, ft4, first_block)


SIGN_BIT = 0x80000000


def _sb_terms(z, mask, diag):
    neg_abs = pltpu.bitcast(pltpu.bitcast(z, jnp.uint32) | jnp.uint32(SIGN_BIT), F32)
    lb = jnp.minimum(z, 0.0) - jnp.log(1.0 + jnp.exp(neg_abs))
    l1m = lb - z
    if diag:
        l1m = jnp.where(mask, l1m, 0.0)
    return lb, l1m


def _dot_split2_stacked(x, m2):
    hi, lo = _split2(x)
    return _dot(jnp.concatenate([hi, lo], axis=1), m2)


def _tri_stacked(kind):
    t = _tri(ATT_BLOCK, kind)
    return jnp.concatenate([t, t], axis=0)


def _sb_fwd(qs, k3, v3):
    tq, tk = SB_Q_BLOCK, ATT_BLOCK
    sp = _att_specs(qs.shape[0], tq)
    ratio = tq // tk
    upper = _tri_stacked("row_gt_col")

    def body(q_ref, k_ref, v_ref, u_ref, y_ref, rtot_ref, first_ref, acc_ref):
        i = pl.program_id(1)
        first = _head_masks(tq)
        qh = _heads_of(q_ref, first)
        u = u_ref[...]
        acc_ref[...] = jnp.zeros_like(acc_ref)

        def block(j, rs, diag):
            mask = _causal_mask(tq, tk, i * tq - j * tk, strict=True) if diag else None
            k, v = k_ref[j], v_ref[j]
            logits = [_dot_nt(qh[n], k) for n in range(2)]
            terms = [_sb_terms(z, mask, diag) for z in logits]
            right = [_dot_split2_stacked(l1m, u) for _, l1m in terms]
            weights = []
            for n in range(2):
                a = jnp.exp(terms[n][0] + right[n] + rs[n])
                if diag:
                    a = jnp.where(mask, a, 0.0)
                weights.append(a.astype(BF16))
            for n in range(2):
                acc_ref[n] += _dot(weights[n], v)
            return tuple(rs[n] + jnp.sum(terms[n][1], axis=-1, keepdims=True) for n in range(2))

        rs = (jnp.zeros((tq, 1), F32),) * 2
        for d in range(ratio):
            rs = block(ratio * i + (ratio - 1 - d), rs, True)

        def block_matters(c):
            j, r0, r1 = c
            return (j >= 0) & (jnp.max(jnp.maximum(r0, r1)) > -EXP_UNDERFLOW)

        def walk_left(c):
            j, r0, r1 = c
            r0, r1 = block(j, (r0, r1), False)
            return j - 1, r0, r1

        j, r0, r1 = lax.while_loop(block_matters, walk_left, (ratio * i - 1, rs[0], rs[1]))
        y_ref[...] = jnp.where(first, acc_ref[0], acc_ref[1])
        rtot_ref[...] = jnp.where(first, r0, r1)
        first_ref[...] = jnp.ones(first_ref.shape, F32) * (j + 1).astype(F32)

    return pl.pallas_call(
        body,
        grid=(N_PAIRS, sp["nq"]),
        in_specs=[sp["q"], sp["k_rows"], sp["k_rows"], pl.BlockSpec((2 * tk, tk), lambda p, i: (0, 0))],
        out_specs=[sp["q"], sp["q"], sp["first"]],
        out_shape=[sp["wide"], sp["wide"], sp["first_out"]],
        scratch_shapes=[sp["acc"]],
        compiler_params=_cparams(56),
        name="sb_fwd",
    )(qs, k3, v3, upper)


def _sb_bwd(qs, k3, v3, dy, rtot, first_block):
    tq, tk = SB_Q_BLOCK, ATT_BLOCK
    sp = _att_specs(qs.shape[0], tq)
    ratio = tq // tk
    lower_in = _tri_stacked("row_le_col")
    lower = _tri(tk, "row_lt_col")

    def body(q_ref, k_ref, v_ref, dy_ref, rtot_ref, first_ref, li_ref, l_ref, dq_ref, dkt_ref, dvt_ref, acc_ref):
        i = pl.program_id(1)

        @pl.when(i == 0)
        def _():
            dkt_ref[...] = jnp.zeros_like(dkt_ref)
            dvt_ref[...] = jnp.zeros_like(dvt_ref)

        first = _head_masks(tq)
        qh = _heads_of(q_ref, first)
        dyb = dy_ref[...].astype(BF16)
        zero = jnp.zeros_like(dyb)
        dyh = [jnp.where(first, dyb, zero), jnp.where(first, zero, dyb)]
        rtoth = _head_cols(rtot_ref)
        li = li_ref[...]
        lo_tri = l_ref[...]
        acc_ref[...] = jnp.zeros_like(acc_ref)

        def block(j, carry, diag):
            mask = _causal_mask(tq, tk, i * tq - j * tk, strict=True) if diag else None
            k, v = k_ref[j], v_ref[j]
            logits = [_dot_nt(qh[n], k) for n in range(2)]
            das = [_dot_nt(dyh[n], v) for n in range(2)]
            terms = [_sb_terms(z, mask, diag) for z in logits]
            upto = [_dot_split2_stacked(l1m, li) for _, l1m in terms]
            des, weights = [], []
            for n in range(2):
                a = jnp.exp(terms[n][0] + ((rtoth[n] - carry[2 * n]) - upto[n]))
                if diag:
                    a = jnp.where(mask, a, 0.0)
                des.append(a * das[n])
                weights.append(a.astype(BF16))
            lefts = [_dot(de.astype(BF16), lo_tri) for de in des]
            dzbs, out = [], []
            for n in range(2):
                beta = jnp.exp(terms[n][0])
                dz = des[n] - (des[n] + (carry[2 * n + 1] + lefts[n])) * beta
                if diag:
                    dz = jnp.where(mask, dz, 0.0)
                dzbs.append(dz.astype(BF16))
                out += [carry[2 * n] + jnp.sum(terms[n][1], axis=-1, keepdims=True),
                        carry[2 * n + 1] + jnp.sum(des[n], axis=-1, keepdims=True)]
            for n in range(2):
                acc_ref[n] += _dot(dzbs[n], k)
            dkt_ref[0, j] += _dot_tn(qh[0], dzbs[0]) + _dot_tn(qh[1], dzbs[1])
            dvt_ref[0, j] += _dot_tn(dyh[0], weights[0]) + _dot_tn(dyh[1], weights[1])
            return tuple(out)

        carry = (jnp.zeros((tq, 1), F32),) * 4
        carry = lax.fori_loop(_first_block(first_ref, ratio * i), ratio * i, lambda j, c: block(j, c, False), carry)
        for d in range(ratio):
            carry = block(ratio * i + d, carry, True)
        dq_ref[...] = jnp.where(first, acc_ref[0], acc_ref[1])

    return pl.pallas_call(
        body,
        grid=(N_PAIRS, sp["nq"]),
        in_specs=[sp["q"], sp["k_rows"], sp["k_rows"], sp["q"], sp["q"], sp["first"],
                  pl.BlockSpec((2 * tk, tk), lambda p, i: (0, 0)), pl.BlockSpec((tk, tk), lambda p, i: (0, 0))],
        out_specs=[sp["q"], sp["k_t"], sp["k_t"]],
        out_shape=[sp["wide"], sp["k_t_out"], sp["k_t_out"]],
        scratch_shapes=[sp["acc"]],
        compiler_params=_cparams(56),
        name="sb_bwd",
    )(qs, k3, v3, dy, rtot, first_block, lower_in, lower)


def _merge_fwd(x1, gates, y_fox, y_sb, w_bf, w_bs, w_out, tm=512):
    s_len = x1.shape[0]

    def body(x_ref, g_ref, yf_ref, ys_ref, wbf_ref, wbs_ref, wo_ref, o_ref):
        g = g_ref[...].astype(F32)
        of = _dot(yf_ref[...].astype(BF16), wbf_ref[...])
        os_ = _dot(ys_ref[...].astype(BF16), wbs_ref[...])
        merged = _sigmoid(g[:, 0:D_MODEL]) * of + _sigmoid(g[:, D_MODEL:]) * os_
        o_ref[...] = x_ref[...] + _dot(merged.astype(BF16), wo_ref[...])

    row = lambda i: (i, 0)
    full = lambda i: (0, 0)
    return pl.pallas_call(
        body,
        grid=(s_len // tm,),
        in_specs=[
            pl.BlockSpec((tm, D_MODEL), row),
            pl.BlockSpec((tm, 2 * D_MODEL), row),
            pl.BlockSpec((tm, ATT_W), row),
            pl.BlockSpec((tm, ATT_W), row),
            pl.BlockSpec((ATT_W, D_MODEL), full),
            pl.BlockSpec((ATT_W, D_MODEL), full),
            pl.BlockSpec((D_MODEL, D_MODEL), full),
        ],
        out_specs=pl.BlockSpec((tm, D_MODEL), row),
        out_shape=jax.ShapeDtypeStruct((s_len, D_MODEL), F32),
        compiler_params=_cparams(48),
        name="merge_fwd",
    )(x1, gates, y_fox, y_sb, w_bf, w_bs, w_out)


def _merge_bwd(dx2, gates, y_fox, y_sb, w_bf, w_bs, w_out, tm=512):
    s_len = dx2.shape[0]

    def body(d_ref, g_ref, yf_ref, ys_ref, wbf_ref, wbs_ref, wo_ref,
             dyf_ref, dys_ref, dg_ref, dof_ref, dos_ref, m_ref, dbf_ref):
        dbf = d_ref[...].astype(BF16)
        dbf_ref[...] = dbf
        dm = _dot_nt(dbf, wo_ref[...])
        g = g_ref[...].astype(F32)
        of = _dot(yf_ref[...].astype(BF16), wbf_ref[...])
        os_ = _dot(ys_ref[...].astype(BF16), wbs_ref[...])
        sf = _sigmoid(g[:, 0:D_MODEL])
        ss = _sigmoid(g[:, D_MODEL:])
        m_ref[...] = (sf * of + ss * os_).astype(BF16)
        d_of = (dm * sf).astype(BF16)
        d_os = (dm * ss).astype(BF16)
        dof_ref[...] = d_of
        dos_ref[...] = d_os
        dg_ref[:, 0:D_MODEL] = (dm * of * sf * (1.0 - sf)).astype(BF16)
        dg_ref[:, D_MODEL:] = (dm * os_ * ss * (1.0 - ss)).astype(BF16)
        dyf_ref[...] = _dot_nt(d_of, wbf_ref[...])
        dys_ref[...] = _dot_nt(d_os, wbs_ref[...])

    row = lambda i: (i, 0)
    full = lambda i: (0, 0)
    return pl.pallas_call(
        body,
        grid=(s_len // tm,),
        in_specs=[
            pl.BlockSpec((tm, D_MODEL), row),
            pl.BlockSpec((tm, 2 * D_MODEL), row),
            pl.BlockSpec((tm, ATT_W), row),
            pl.BlockSpec((tm, ATT_W), row),
            pl.BlockSpec((ATT_W, D_MODEL), full),
            pl.BlockSpec((ATT_W, D_MODEL), full),
            pl.BlockSpec((D_MODEL, D_MODEL), full),
        ],
        out_specs=[
            pl.BlockSpec((tm, ATT_W), row), pl.BlockSpec((tm, ATT_W), row),
            pl.BlockSpec((tm, 2 * D_MODEL), row),
            pl.BlockSpec((tm, D_MODEL), row), pl.BlockSpec((tm, D_MODEL), row),
            pl.BlockSpec((tm, D_MODEL), row), pl.BlockSpec((tm, D_MODEL), row),
        ],
        out_shape=[
            jax.ShapeDtypeStruct((s_len, ATT_W), F32), jax.ShapeDtypeStruct((s_len, ATT_W), F32),
            jax.ShapeDtypeStruct((s_len, 2 * D_MODEL), BF16),
            jax.ShapeDtypeStruct((s_len, D_MODEL), BF16), jax.ShapeDtypeStruct((s_len, D_MODEL), BF16),
            jax.ShapeDtypeStruct((s_len, D_MODEL), BF16), jax.ShapeDtypeStruct((s_len, D_MODEL), BF16),
        ],
        compiler_params=_cparams(56),
        name="merge_bwd",
    )(dx2, gates, y_fox, y_sb, w_bf, w_bs, w_out)


def _ple_loss(x3, p, g, w_pg, w_pp, target, tm=512):
    s_len = x3.shape[0]
    inv_d = 1.0 / D_MODEL

    def body(x_ref, p_ref, g_ref, wpg_ref, wpp_ref, t_ref,
             dx_ref, du_ref, dt_ref, hn_ref, dg_ref, loss_ref):
        @pl.when(pl.program_id(0) == 0)
        def _():
            dg_ref[...] = jnp.zeros_like(dg_ref)
            loss_ref[...] = jnp.zeros_like(loss_ref)

        x = x_ref[...]
        xn, r = _rms(x)
        gain = g_ref[...]
        hn = (xn * gain).astype(BF16)
        hn_ref[...] = hn
        sg = _sigmoid(_dot(hn, wpg_ref[...]))
        t = _dot(p_ref[...].astype(BF16), wpp_ref[...])
        err = x + sg * t - t_ref[...]
        sq = jnp.sum(_colsum(err * err), axis=-1, keepdims=True)
        loss_ref[...] += (0.5 * inv_d) * sq
        dy = err * inv_d
        du = (dy * t * sg * (1.0 - sg)).astype(BF16)
        du_ref[...] = du
        dt_ref[...] = (dy * sg).astype(BF16)
        dh = _dot_nt(du, wpg_ref[...])
        dx_ref[...] = dy + _rms_bwd(dh, xn, r, gain)
        dg_ref[0:1, :] += _colsum(dh * xn)

    row = lambda i: (i, 0)
    full = lambda i: (0, 0)
    bf = jax.ShapeDtypeStruct((s_len, D_MODEL), BF16)
    return pl.pallas_call(
        body,
        grid=(s_len // tm,),
        in_specs=[
            pl.BlockSpec((tm, D_MODEL), row),
            pl.BlockSpec((tm, PLE_DIM), row),
            pl.BlockSpec((1, D_MODEL), full),
            pl.BlockSpec((D_MODEL, D_MODEL), full),
            pl.BlockSpec((PLE_DIM, D_MODEL), full),
            pl.BlockSpec((tm, D_MODEL), row),
        ],
        out_specs=[
            pl.BlockSpec((tm, D_MODEL), row), pl.BlockSpec((tm, D_MODEL), row),
            pl.BlockSpec((tm, D_MODEL), row), pl.BlockSpec((tm, D_MODEL), row),
            pl.BlockSpec((8, D_MODEL), full), pl.BlockSpec((8, LANES), full),
        ],
        out_shape=[
            jax.ShapeDtypeStruct((s_len, D_MODEL), F32), bf, bf, bf,
            jax.ShapeDtypeStruct((8, D_MODEL), F32), jax.ShapeDtypeStruct((8, LANES), F32),
        ],
        compiler_params=_cparams(48),
        name="ple_loss",
    )(x3, p, g, w_pg, w_pp, target)


def _sb_grads_packed(dqs, dkt4, dvt4, tm=512):
    s_len = dqs.shape[0]

    def body(dq_ref, dkt_ref, dvt_ref, o_ref):
        o_ref[:, 0:ATT_W] = (dq_ref[...] * QK_SCALE).astype(BF16)
        o_ref[:, ATT_W:2 * ATT_W] = _rows_of_transposed(dkt_ref).astype(BF16)
        o_ref[:, 2 * ATT_W:] = _rows_of_transposed(dvt_ref).astype(BF16)

    return pl.pallas_call(
        body,
        grid=(s_len // tm,),
        in_specs=[pl.BlockSpec((tm, ATT_W), lambda i: (i, 0)), _transposed_spec(tm), _transposed_spec(tm)],
        out_specs=pl.BlockSpec((tm, 3 * ATT_W), lambda i: (i, 0)),
        out_shape=jax.ShapeDtypeStruct((s_len, 3 * ATT_W), BF16),
        name="sb_grads_packed",
    )(dqs, dkt4, dvt4)


def _qknorm_bwd(fq, fk, dqs, dkt4, dvt4, qn, kn, bd, bd_t, tm=512):
    s_len = fq.shape[0]

    def body(fq_ref, fk_ref, dq_ref, dkt_ref, dvt_ref, qn_ref, kn_ref, bd_ref, bdt_ref,
             dz_ref, dqn_ref, dkn_ref):
        @pl.when(pl.program_id(0) == 0)
        def _():
            dqn_ref[...] = jnp.zeros_like(dqn_ref)
            dkn_ref[...] = jnp.zeros_like(dkn_ref)

        bd_m = bd_ref[...]
        bdt_m = bdt_ref[...]

        def one(x, dy, gain, dgain_ref):
            xn, rw = _head_rms(x, bd_m, bdt_m)
            dgain_ref[0:1, :] += _colsum(dy * xn)
            dxn = dy * gain
            return rw * (dxn - xn * _head_mean(dxn * xn, bd_m, bdt_m))

        dz_ref[:, 0:ATT_W] = one(fq_ref[...], dq_ref[...] * QK_SCALE, qn_ref[...], dqn_ref).astype(BF16)
        dz_ref[:, ATT_W:2 * ATT_W] = one(fk_ref[...], _rows_of_transposed(dkt_ref), kn_ref[...], dkn_ref).astype(BF16)
        dz_ref[:, 2 * ATT_W:] = _rows_of_transposed(dvt_ref).astype(BF16)

    row = lambda i: (i, 0)
    full = lambda i: (0, 0)
    att = pl.BlockSpec((tm, ATT_W), row)
    return pl.pallas_call(
        body,
        grid=(s_len // tm,),
        in_specs=[att, att, att, _transposed_spec(tm), _transposed_spec(tm),
                  pl.BlockSpec((1, ATT_W), full), pl.BlockSpec((1, ATT_W), full),
                  pl.BlockSpec((ATT_W, LANES), full), pl.BlockSpec((LANES, ATT_W), full)],
        out_specs=[pl.BlockSpec((tm, 3 * ATT_W), row), pl.BlockSpec((8, ATT_W), full), pl.BlockSpec((8, ATT_W), full)],
        out_shape=[jax.ShapeDtypeStruct((s_len, 3 * ATT_W), BF16),
                   jax.ShapeDtypeStruct((8, ATT_W), F32), jax.ShapeDtypeStruct((8, ATT_W), F32)],
        name="qknorm_bwd",
    )(fq, fk, dqs, dkt4, dvt4, qn, kn, bd, bd_t)


def _inproj_bwd(x1, dx2, g, dzf, dlogf, logf, dzs, dgates, w_fox, w_fl, w_sb, w_gates, tm=512):
    s_len = x1.shape[0]

    def body(x_ref, d_ref, g_ref, dzf_ref, dlf_ref, lf_ref, dzs_ref, dgt_ref, wf_ref, wl_ref, ws_ref, wg_ref,
             dx_ref, h_ref, dfl_ref, dg_ref, db_ref):
        @pl.when(pl.program_id(0) == 0)
        def _():
            dg_ref[...] = jnp.zeros_like(dg_ref)
            db_ref[...] = jnp.zeros_like(db_ref)

        xn, r = _rms(x_ref[...])
        gain = g_ref[...]
        h_ref[...] = (xn * gain).astype(BF16)
        lane = lax.broadcasted_iota(jnp.int32, (tm, LANES), 1)
        dfl = jnp.where(lane < N_HEADS, dlf_ref[...] * (1.0 - jnp.exp(lf_ref[...])), 0.0)
        db_ref[0:1, :] += _colsum(dfl)
        dflb = dfl.astype(BF16)
        dfl_ref[...] = dflb
        dh = (_dot_nt(dzf_ref[...], wf_ref[...]) + _dot_nt(dflb, wl_ref[...])
              + _dot_nt(dzs_ref[...], ws_ref[...]) + _dot_nt(dgt_ref[...], wg_ref[...]))
        dx_ref[...] = d_ref[...] + _rms_bwd(dh, xn, r, gain)
        dg_ref[0:1, :] += _colsum(dh * xn)

    row = lambda i: (i, 0)
    full = lambda i: (0, 0)
    return pl.pallas_call(
        body,
        grid=(s_len // tm,),
        in_specs=[
            pl.BlockSpec((tm, D_MODEL), row),
            pl.BlockSpec((tm, D_MODEL), row),
            pl.BlockSpec((1, D_MODEL), full),
            pl.BlockSpec((tm, 3 * ATT_W), row),
            pl.BlockSpec((tm, LANES), row),
            pl.BlockSpec((tm, LANES), row),
            pl.BlockSpec((tm, 3 * ATT_W), row),
            pl.BlockSpec((tm, 2 * D_MODEL), row),
            pl.BlockSpec((D_MODEL, 3 * ATT_W), full),
            pl.BlockSpec((D_MODEL, LANES), full),
            pl.BlockSpec((D_MODEL, 3 * ATT_W), full),
            pl.BlockSpec((D_MODEL, 2 * D_MODEL), full),
        ],
        out_specs=[
            pl.BlockSpec((tm, D_MODEL), row), pl.BlockSpec((tm, D_MODEL), row), pl.BlockSpec((tm, LANES), row),
            pl.BlockSpec((8, D_MODEL), full), pl.BlockSpec((8, LANES), full),
        ],
        out_shape=[
            jax.ShapeDtypeStruct((s_len, D_MODEL), F32), jax.ShapeDtypeStruct((s_len, D_MODEL), BF16),
            jax.ShapeDtypeStruct((s_len, LANES), BF16),
            jax.ShapeDtypeStruct((8, D_MODEL), F32), jax.ShapeDtypeStruct((8, LANES), F32),
        ],
        compiler_params=_cparams(56),
        name="inproj_bwd",
    )(x1, dx2, g, dzf, dlogf, logf, dzs, dgates, w_fox, w_fl, w_sb, w_gates)


def _split_w_in(w_in):
    o = 3 * ATT_W
    w_fox = w_in[:, 0:o]
    w_fl = jnp.pad(w_in[:, o:o + N_HEADS], ((0, 0), (0, LANES - N_HEADS)))
    w_sb = w_in[:, o + N_HEADS:2 * o + N_HEADS]
    w_gates = w_in[:, 2 * o + N_HEADS:]
    return w_fox, w_fl, w_sb, w_gates


def _local_grads(x, p, target, small, full, pending=None, send_early=None):
    blk = ATT_BLOCK
    bd, bd_t = _head_sum_matrices()
    full = dict(full)
    late = list(pending) if pending else []

    x1, a1, b1, u1, *gathered = _ffn_fwd(x, small["ffn1_norm"], full["ffn1_w_gate"], full["ffn1_w_up"],
                                     full["ffn1_w_down"], gather=[pending[k] for k in late])
    for k, gth in zip(late, gathered):
        full[k] = gth if k in KEPT_AS_SHARDS else _whole(k, gth)
    w_fox, w_fl, w_sb, w_gates = _split_w_in(full["w_in"])
    bias = jnp.pad(small["forget_bias"], ((0, 0), (0, LANES - N_HEADS)))
    qn = jnp.tile(small["q_norm"], (1, N_HEADS))
    kn = jnp.tile(small["k_norm"], (1, N_HEADS))
    fq, fk, f_qs, f_k, f_v, logf, s_qs, s_k, s_v, gates, f_k_sq = _inproj_fwd(
        x1, small["mix_norm"], w_fox, w_fl, w_sb, w_gates, bias, qn, kn, bd, bd_t)
    f_cum = _cumsum_rows(logf, reverse=False)
    f8 = f_cum[:, 0:N_HEADS]
    fw = jnp.repeat(f8, HEAD_DIM, axis=1)
    ft4 = _pair_rows_t(f8, blk)
    f_k3, f_v3 = _blocked_rows(f_k, blk), _blocked_rows(f_v, blk)
    y_fox, lse, f_first = _fox_fwd(f_qs, f_k3, f_v3, fw, ft4, _key_norm_bound(f_k_sq))
    s_k3, s_v3 = _blocked_rows(s_k, blk), _blocked_rows(s_v, blk)
    y_sb, s_rtot, s_first = _sb_fwd(s_qs, s_k3, s_v3)
    x2 = _merge_fwd(x1, gates, y_fox, y_sb, full["w_branch_fox"], full["w_branch_sb"], full["w_out"])
    x3, a2, b2, u2 = _ffn_fwd(x2, small["ffn2_norm"], full["ffn2_w_gate"], full["ffn2_w_up"], full["ffn2_w_down"])

    dx3, du_ple, dt_ple, hn_ple, dg_ple, loss_sum = _ple_loss(
        x3, p, small["ple_norm"], full["w_ple_gate"], full["w_ple_proj"], target)
    dx2, da2, db2, h_ffn2, d3_bf, dg_ffn2 = _ffn_bwd(
        x2, dx3, small["ffn2_norm"], a2, b2, full["ffn2_w_gate"], full["ffn2_w_up"], full["ffn2_w_down"])
    dy_fox, dy_sb, dgates, d_of, d_os, merged, d2_bf = _merge_bwd(
        dx2, gates, y_fox, y_sb, full["w_branch_fox"], full["w_branch_sb"], full["w_out"])

    f_dqs, dfq_p, f_dkt4, f_dvt4, dft4 = _fox_bwd(f_qs, f_k3, f_v3, dy_fox, y_fox, lse, fw, ft4, f_first)
    s_dqs, s_dkt4, s_dvt4 = _sb_bwd(s_qs, s_k3, s_v3, dy_sb, s_rtot, s_first)

    dzf, dqn8, dkn8 = _qknorm_bwd(fq, fk, f_dqs, f_dkt4, f_dvt4, qn, kn, bd, bd_t)
    dzs = _sb_grads_packed(s_dqs, s_dkt4, s_dvt4)
    df8 = _unpair_rows_t(dft4) + dfq_p[:, :, 0:2].transpose(1, 0, 2).reshape(-1, N_HEADS)
    dlogf = _cumsum_rows(jnp.pad(df8, ((0, 0), (0, LANES - N_HEADS))), reverse=True)
    dx1, h_mix, dfl, dg_mix, dbias8 = _inproj_bwd(
        x1, dx2, small["mix_norm"], dzf, dlogf, logf, dzs, dgates, w_fox, w_fl, w_sb, w_gates)

    one = lambda t: t[None]
    gw = {}
    gw["ffn2_w_gate"] = _wgrad(da2, one(h_ffn2), name="wgrad_ffn2_gate")
    gw["ffn2_w_up"] = _wgrad(db2, one(h_ffn2), name="wgrad_ffn2_up")
    gw["ffn2_w_down"] = _wgrad(u2, one(d3_bf), scale=0.5, name="wgrad_ffn2_down")
    g_fox = _wgrad(one(h_mix), one(dzf), name="wgrad_in_fox")[0]
    g_fl = _wgrad(one(h_mix), one(dfl), name="wgrad_in_forget")[0]
    g_sb = _wgrad(one(h_mix), one(dzs), name="wgrad_in_sb")[0]
    g_gt = _wgrad(one(h_mix), one(dgates), name="wgrad_in_gates")[0]
    gw["w_in"] = jnp.concatenate([g_fox, g_fl[:, 0:N_HEADS], g_sb, g_gt], axis=1)
    gw["w_branch_fox"] = _wgrad(one(y_fox), one(d_of), name="wgrad_branch_fox")[0]
    gw["w_branch_sb"] = _wgrad(one(y_sb), one(d_os), name="wgrad_branch_sb")[0]
    gw["w_out"] = _wgrad(one(merged), one(d2_bf), name="wgrad_out")[0]
    gw["w_ple_gate"] = _wgrad(one(hn_ple), one(du_ple), name="wgrad_ple_gate")[0]
    gw["w_ple_proj"] = _wgrad(one(p), one(dt_ple), name="wgrad_ple_proj")[0]

    gw["ffn1_w_down"] = _wgrad(u1, one(dx1), scale=0.5, name="wgrad_ffn1_down")

    sent_names, to_send = send_early(gw) if send_early else ([], [])
    grad_x, da1, db1, h_ffn1, _, dg_ffn1, *landed = _ffn_bwd(
        x, dx1, small["ffn1_norm"], a1, b1, full["ffn1_w_gate"], full["ffn1_w_up"], full["ffn1_w_down"],
        scatter=to_send)
    gw["ffn1_w_gate"] = _wgrad(da1, one(h_ffn1), name="wgrad_ffn1_gate")
    gw["ffn1_w_up"] = _wgrad(db1, one(h_ffn1), name="wgrad_ffn1_up")

    fold = lambda t: jnp.sum(t[0:1].reshape(N_HEADS, HEAD_DIM), axis=0, keepdims=True)
    gs = {
        "ffn1_norm": dg_ffn1[0:1], "mix_norm": dg_mix[0:1], "ffn2_norm": dg_ffn2[0:1], "ple_norm": dg_ple[0:1],
        "forget_bias": dbias8[0:1, 0:N_HEADS], "q_norm": fold(dqn8), "k_norm": fold(dkn8),
    }
    return loss_sum, grad_x, gw, gs, dict(zip(sent_names, landed))


def _position():
    return lax.axis_index("x"), lax.axis_index("y"), lax.axis_index("c")


def _other_chips(x, y):
    return [(1 - x, y), (x, 1 - y), (1 - x, 1 - y)]


ANY = pl.BlockSpec(memory_space=pl.ANY)


def _place_own_shard(w, q):
    rows, cols = w.shape
    tr = _row_block(rows, cols * 4, budget=2 * MIB)

    def body(q_ref, w_ref, o_ref):
        o_ref[0] = w_ref[...].astype(BF16)

    return pl.pallas_call(
        body,
        grid_spec=pltpu.PrefetchScalarGridSpec(
            num_scalar_prefetch=1,
            grid=(rows // tr,),
            in_specs=[pl.BlockSpec((tr, cols), lambda i, q_ref: (i, 0))],
            out_specs=pl.BlockSpec((1, tr, cols), lambda i, q_ref: (q_ref[0], i, 0)),
        ),
        out_shape=jax.ShapeDtypeStruct((N_CHIPS, rows, cols), BF16),
        name="place_own_shard",
    )(q, w)


def _gather_semaphores(n):
    return [pltpu.SemaphoreType.DMA((6 * n,)), pltpu.SemaphoreType.DMA((6 * n,))]


def _gather_steps(bufs, send_sems, recv_sems):
    n = len(bufs)
    x, y, c = _position()
    q = 2 * x + y
    chips = _other_chips(x, y)
    sibling = (x, y, 1 - c)

    def half(a, slot, which):
        r2 = bufs[a].shape[1] // 2
        return bufs[a].at[slot, pl.ds(which * r2, r2), :]

    def copy(a, k, region, to):
        return pltpu.make_async_remote_copy(
            src_ref=region, dst_ref=region, send_sem=send_sems.at[6 * a + k], recv_sem=recv_sems.at[6 * a + k],
            device_id=to, device_id_type=MESH)

    def to_chip(a, k):
        tx, ty = chips[k]
        return copy(a, k, half(a, q, c), (tx, ty, c))

    def to_sibling(a, k):
        tx, ty = chips[k]
        return copy(a, 3 + k, half(a, 2 * tx + ty, c), sibling)

    def start():
        for a in range(n):
            for k in range(3):
                to_chip(a, k).start()

    def finish():
        for a in range(n):
            for k, (tx, ty) in enumerate(chips):
                copy(a, k, half(a, 2 * tx + ty, c), (tx, ty, c)).wait_recv()
                to_sibling(a, k).start()
        for a in range(n):
            for k, (tx, ty) in enumerate(chips):
                copy(a, 3 + k, half(a, 2 * tx + ty, 1 - c), sibling).wait_recv()
        for a in range(n):
            for k in range(3):
                to_chip(a, k).wait_send()
                to_sibling(a, k).wait_send()

    return start, finish


def _allgather_weights(slots):
    n = len(slots)

    def body(*refs):
        start, finish = _gather_steps(refs[n:2 * n], *refs[2 * n:])
        start()
        finish()

    return pl.pallas_call(
        body,
        in_specs=[ANY] * n,
        out_specs=[ANY] * n,
        out_shape=[jax.ShapeDtypeStruct(s.shape, s.dtype) for s in slots],
        input_output_aliases={a: a for a in range(n)},
        scratch_shapes=_gather_semaphores(n),
        name="allgather_weights",
    )(*slots)


def _exchange_pair_halves(grads):
    n = len(grads)

    def body(*refs):
        ins, outs = refs[0:n], refs[n:2 * n]
        send_sems, recv_sems = refs[2 * n:]
        x, y, c = _position()
        copies = []
        for a in range(n):
            r2 = grads[a].shape[1] // 2
            cp = pltpu.make_async_remote_copy(
                src_ref=ins[a].at[:, pl.ds((1 - c) * r2, r2), :], dst_ref=outs[a],
                send_sem=send_sems.at[a], recv_sem=recv_sems.at[a], device_id=(x, y, 1 - c), device_id_type=MESH)
            cp.start()
            copies.append(cp)
        for cp in copies:
            cp.wait()

    return pl.pallas_call(
        body,
        in_specs=[ANY] * n,
        out_specs=[ANY] * n,
        out_shape=[jax.ShapeDtypeStruct((N_CHIPS, g.shape[1] // 2, g.shape[2]), g.dtype) for g in grads],
        scratch_shapes=[pltpu.SemaphoreType.DMA((n,)), pltpu.SemaphoreType.DMA((n,))],
        name="rs_pair_exchange",
    )(*grads)


def _scatter_semaphores(n):
    return [pltpu.SemaphoreType.DMA((3 * n,)), pltpu.SemaphoreType.DMA((3 * n,)), pltpu.SemaphoreType.DMA((n,))]


def _scatter_steps(ins, outs, send_sems, recv_sems, local_sems):
    n = len(ins)
    x, y, c = _position()
    q = 2 * x + y
    chips = _other_chips(x, y)

    def own(a):
        return pltpu.make_async_copy(ins[a].at[q], outs[a].at[q], local_sems.at[a])

    def to_chip(a, k):
        tx, ty = chips[k]
        return pltpu.make_async_remote_copy(
            src_ref=ins[a].at[2 * tx + ty], dst_ref=outs[a].at[q],
            send_sem=send_sems.at[3 * a + k], recv_sem=recv_sems.at[3 * a + k],
            device_id=(tx, ty, c), device_id_type=MESH)

    def start():
        for a in range(n):
            own(a).start()
            for k in range(3):
                to_chip(a, k).start()

    def finish():
        for a in range(n):
            own(a).wait()
            for k in range(3):
                to_chip(a, k).wait()

    return start, finish


def _scatter_to_owner_chips(pairs):
    n = len(pairs)

    def body(*refs):
        start, finish = _scatter_steps(refs[0:n], refs[n:2 * n], *refs[2 * n:])
        start()
        finish()

    return pl.pallas_call(
        body,
        in_specs=[ANY] * n,
        out_specs=[ANY] * n,
        out_shape=[jax.ShapeDtypeStruct(p.shape, p.dtype) for p in pairs],
        scratch_shapes=_scatter_semaphores(n),
        name="rs_scatter",
    )(*pairs)


def _join_halves(shards):
    n = len(shards)

    def body(*refs):
        bufs = refs[n:2 * n]
        send_sems, recv_sems = refs[2 * n:]
        x, y, c = _position()
        started = []
        for a in range(n):
            r2 = shards[a].shape[0] // 2
            mine = bufs[a].at[pl.ds(c * r2, r2), :]
            cp = pltpu.make_async_remote_copy(
                src_ref=mine, dst_ref=mine, send_sem=send_sems.at[a], recv_sem=recv_sems.at[a],
                device_id=(x, y, 1 - c), device_id_type=MESH)
            cp.start()
            started.append(cp)
        for cp in started:
            cp.wait()

    return pl.pallas_call(
        body,
        in_specs=[ANY] * n,
        out_specs=[ANY] * n,
        out_shape=[jax.ShapeDtypeStruct(t.shape, t.dtype) for t in shards],
        input_output_aliases={a: a for a in range(n)},
        scratch_shapes=[pltpu.SemaphoreType.DMA((n,)), pltpu.SemaphoreType.DMA((n,))],
        name="rs_join_halves",
    )(*shards)


def _add_pair(g, got, c):
    _, r2, cols = got.shape

    def body(c_ref, g_ref, got_ref, o_ref):
        o_ref[...] = (g_ref[...].astype(F32) + got_ref[...].astype(F32)).astype(BF16)

    spec = pl.BlockSpec((1, r2, cols), lambda s, c_ref: (s, 0, 0))
    return pl.pallas_call(
        body,
        grid_spec=pltpu.PrefetchScalarGridSpec(
            num_scalar_prefetch=1,
            grid=(N_CHIPS,),
            in_specs=[pl.BlockSpec((1, r2, cols), lambda s, c_ref: (s, c_ref[0], 0)), spec],
            out_specs=spec,
        ),
        out_shape=jax.ShapeDtypeStruct(got.shape, BF16),
        name="rs_add_pair",
    )(c, g, got)


def _add_chips(parts, c):
    _, r2, cols = parts.shape

    def body(c_ref, p0, p1, p2, p3, o_ref):
        o_ref[...] = ((p0[0].astype(F32) + p1[0].astype(F32)) + p2[0].astype(F32)) + p3[0].astype(F32)

    specs = [pl.BlockSpec((1, r2, cols), functools.partial(lambda i, c_ref, s: (s, 0, 0), s=s))
             for s in range(N_CHIPS)]
    return pl.pallas_call(
        body,
        grid_spec=pltpu.PrefetchScalarGridSpec(
            num_scalar_prefetch=1,
            grid=(1,),
            in_specs=specs,
            out_specs=pl.BlockSpec((r2, cols), lambda i, c_ref: (c_ref[0], 0)),
        ),
        out_shape=jax.ShapeDtypeStruct((2 * r2, cols), F32),
        name="rs_add_chips",
    )(c, parts, parts, parts, parts)


def _allreduce_small(part):
    shape = part.shape

    def body(in_ref, out_ref, gather_ref, send_sems, recv_sems):
        x, y, c = _position()
        me = 4 * x + 2 * y + c
        relations = [(a, b, d) for a in (0, 1) for b in (0, 1) for d in (0, 1)][1:]
        flip = lambda v, f: 1 - v if f else v
        copies = []
        for k, (a, b, d) in enumerate(relations):
            cp = pltpu.make_async_remote_copy(
                src_ref=in_ref, dst_ref=gather_ref.at[me], send_sem=send_sems.at[k], recv_sem=recv_sems.at[k],
                device_id=(flip(x, a), flip(y, b), flip(c, d)), device_id_type=MESH)
            cp.start()
            copies.append(cp)
        gather_ref[me] = in_ref[...]
        for cp in copies:
            cp.wait()
        total = gather_ref[0]
        for dev in range(1, 8):
            total = total + gather_ref[dev]
        out_ref[...] = total

    vmem = pl.BlockSpec(memory_space=pltpu.VMEM)
    return pl.pallas_call(
        body,
        in_specs=[vmem],
        out_specs=vmem,
        out_shape=jax.ShapeDtypeStruct(shape, F32),
        scratch_shapes=[pltpu.VMEM((8,) + shape, F32), pltpu.SemaphoreType.DMA((7,)), pltpu.SemaphoreType.DMA((7,))],
        name="allreduce_small",
    )(part)


def _adamw(w, g, m, v):
    rows, cols = w.shape
    tr = _row_block(rows, cols * 4, budget=MIB)
    c1 = 1.0 / (1.0 - ADAM_B1 ** ADAM_STEP)
    c2 = 1.0 / (1.0 - ADAM_B2 ** ADAM_STEP)

    def body(w_ref, g_ref, m_ref, v_ref, d_ref, nm_ref, nv_ref):
        g_ = g_ref[...]
        nm = ADAM_B1 * m_ref[...] + (1.0 - ADAM_B1) * g_
        nv = ADAM_B2 * v_ref[...] + (1.0 - ADAM_B2) * (g_ * g_)
        nm_ref[...] = nm
        nv_ref[...] = nv
        d_ref[...] = -ADAM_LR * ((nm * c1) / (jnp.sqrt(nv * c2) + ADAM_EPS) + ADAM_WD * w_ref[...])

    spec = pl.BlockSpec((tr, cols), lambda i: (i, 0))
    out = jax.ShapeDtypeStruct((rows, cols), F32)
    return pl.pallas_call(
        body,
        grid=(rows // tr,),
        in_specs=[spec] * 4,
        out_specs=[spec] * 3,
        out_shape=[out] * 3,
        name="adamw",
    )(w, g, m, v)


BIG = ["ffn1_w_gate", "ffn1_w_up", "ffn1_w_down", "w_in", "w_branch_fox", "w_branch_sb", "w_out",
       "ffn2_w_gate", "ffn2_w_up", "ffn2_w_down", "w_ple_gate", "w_ple_proj"]
SMALL = ["ffn1_norm", "mix_norm", "ffn2_norm", "ple_norm", "forget_bias", "q_norm", "k_norm"]
COLUMN_SHARDED = ["w_in", "w_branch_fox", "w_branch_sb", "w_ple_proj"]
KEPT_AS_SHARDS = ["ffn1_w_gate", "ffn1_w_up", "ffn1_w_down", "ffn2_w_gate", "ffn2_w_up", "ffn2_w_down"]
WORKED_TRANSPOSED = ["ffn1_w_gate", "ffn1_w_up", "ffn2_w_gate", "ffn2_w_up"]
NEEDED_FIRST = ["ffn1_w_gate", "ffn1_w_up", "ffn1_w_down"]
READY_LAST = ["ffn1_w_gate", "ffn1_w_up"]
ORDER = ["ffn1_norm", "ffn1_w_gate", "ffn1_w_up", "ffn1_w_down", "mix_norm", "w_in", "forget_bias", "q_norm",
         "k_norm", "w_branch_fox", "w_branch_sb", "w_out", "ffn2_norm", "ffn2_w_gate", "ffn2_w_up",
         "ffn2_w_down", "ple_norm", "w_ple_gate", "w_ple_proj"]
SMALL_ROWS = {"ffn1_norm": 0, "mix_norm": 1, "ffn2_norm": 2, "ple_norm": 3}
SMALL_COLS = {"forget_bias": (0, N_HEADS), "q_norm": (N_HEADS, HEAD_DIM), "k_norm": (N_HEADS + HEAD_DIM, HEAD_DIM)}
LOSS_ROW = 5


def _stored(name, a):
    return jnp.swapaxes(a[0], 0, 1) if name in WORKED_TRANSPOSED else a[0]


def _returned(name, t):
    return (jnp.swapaxes(t, 0, 1) if name in WORKED_TRANSPOSED else t)[None]


def _whole(name, gathered):
    if name in COLUMN_SHARDED:
        return jnp.concatenate([gathered[s] for s in range(N_CHIPS)], axis=1)
    return gathered.reshape(-1, gathered.shape[-1])


def _as_shards(name, whole):
    if name in COLUMN_SHARDED:
        k, n = whole.shape
        return whole.reshape(k, N_CHIPS, n // N_CHIPS).transpose(1, 0, 2)
    return whole.reshape(N_CHIPS, whole.shape[0] // N_CHIPS, whole.shape[1])


def _pack_small(values, extra=None):
    rows = [values[k] for k in ("ffn1_norm", "mix_norm", "ffn2_norm", "ple_norm")]
    tail = jnp.concatenate([values["forget_bias"], values["q_norm"], values["k_norm"]], axis=1)
    rows.append(jnp.pad(tail, ((0, 0), (0, D_MODEL - tail.shape[1]))))
    packed = jnp.concatenate(rows + [jnp.zeros((3, D_MODEL), F32)], axis=0)
    if extra is not None:
        packed = packed.at[LOSS_ROW, 0].set(extra)
    return packed


def _unpack_small(packed):
    out = {k: packed[r:r + 1] for k, r in SMALL_ROWS.items()}
    for k, (start, size) in SMALL_COLS.items():
        out[k] = packed[4:5, start:start + size]
    return out


def kernel(x, p, ffn1_norm, ffn1_w_gate, ffn1_w_up, ffn1_w_down, mix_norm, w_in, forget_bias, q_norm, k_norm, w_branch_fox, w_branch_sb, w_out, ffn2_norm, ffn2_w_gate, ffn2_w_up, ffn2_w_down, ple_norm, w_ple_gate, w_ple_proj, loss_target, m_ffn1_norm, m_ffn1_w_gate, m_ffn1_w_up, m_ffn1_w_down, m_mix_norm, m_w_in, m_forget_bias, m_q_norm, m_k_norm, m_w_branch_fox, m_w_branch_sb, m_w_out, m_ffn2_norm, m_ffn2_w_gate, m_ffn2_w_up, m_ffn2_w_down, m_ple_norm, m_w_ple_gate, m_w_ple_proj, v_ffn1_norm, v_ffn1_w_gate, v_ffn1_w_up, v_ffn1_w_down, v_mix_norm, v_w_in, v_forget_bias, v_q_norm, v_k_norm, v_w_branch_fox, v_w_branch_sb, v_w_out, v_ffn2_norm, v_ffn2_w_gate, v_ffn2_w_up, v_ffn2_w_down, v_ple_norm, v_w_ple_gate, v_w_ple_proj):
    args = dict(locals())
    weights = {k: args[k] for k in ORDER}
    moments_m = {k: args["m_" + k] for k in ORDER}
    moments_v = {k: args["v_" + k] for k in ORDER}

    c_idx = lax.axis_index("c").astype(jnp.int32).reshape(1)
    q_idx = (2 * lax.axis_index("x") + lax.axis_index("y")).astype(jnp.int32).reshape(1)
    own = {k: _place_own_shard(_stored(k, weights[k]), q_idx) for k in BIG}
    full = dict(zip(NEEDED_FIRST, _allgather_weights([own[k] for k in NEEDED_FIRST])))
    pending = {k: own[k] for k in BIG if k not in NEEDED_FIRST}
    small = {k: weights[k] for k in SMALL}

    def pair_sums(names, gw):
        slots = [gw[k] if k in KEPT_AS_SHARDS else _as_shards(k, gw[k]) for k in names]
        from_core = _exchange_pair_halves(slots)
        return [_add_pair(g, got, c_idx) for g, got in zip(slots, from_core)]

    early = [k for k in BIG if k not in READY_LAST]
    loss_sum, grad_x, gw, gs, parts = _local_grads(
        x[0], p[0, 0], loss_target[0], small, full, pending, lambda ready: (early, pair_sums(early, ready)))

    parts.update(zip(READY_LAST, _scatter_to_owner_chips(pair_sums(READY_LAST, gw))))
    grads_big = dict(zip(BIG, _join_halves([_add_chips(parts[k], c_idx) for k in BIG])))
    reduced = _allreduce_small(_pack_small(gs, extra=loss_sum[0, 0]))
    grads_small = _unpack_small(reduced)
    loss = reduced[LOSS_ROW, 0]

    grads, deltas, new_m, new_v = {}, {}, {}, {}
    for k in BIG:
        d, nm, nv = _adamw(_stored(k, weights[k]), grads_big[k], _stored(k, moments_m[k]), _stored(k, moments_v[k]))
        grads[k], deltas[k], new_m[k], new_v[k] = (_returned(k, t) for t in (grads_big[k], d, nm, nv))
    d_s, nm_s, nv_s = _adamw(_pack_small({k: weights[k] for k in SMALL}), reduced,
                             _pack_small({k: moments_m[k] for k in SMALL}),
                             _pack_small({k: moments_v[k] for k in SMALL}))
    for k in SMALL:
        grads[k] = grads_small[k]
    for name, packed in (("d", d_s), ("m", nm_s), ("v", nv_s)):
        target = {"d": deltas, "m": new_m, "v": new_v}[name]
        target.update(_unpack_small(packed))

    return (loss, grad_x[None], *[grads[k] for k in ORDER], *[deltas[k] for k in ORDER],
            *[new_m[k] for k in ORDER], *[new_v[k] for k in ORDER])
```

```python
import functools

import jax
import jax.numpy as jnp
from jax import lax
from jax.experimental import pallas as pl
from jax.experimental.pallas import tpu as pltpu

F32 = jnp.float32
BF16 = jnp.bfloat16

D_MODEL = 1024
D_FF = 2816
N_CHIPS = 4
FF_SHARD = D_FF // N_CHIPS
FFN_CHUNKS = 2
WGRAD_TOKENS = 4096
WGRAD_VMEM = 30 * 1024 * 1024
HEAD_DIM = 64
N_HEADS = 8
ATT_W = N_HEADS * HEAD_DIM
PAIR_W = 2 * HEAD_DIM
N_PAIRS = N_HEADS // 2
PLE_DIM = 256
IN_WIDTH = 3 * ATT_W + N_HEADS + 3 * ATT_W + 2 * D_MODEL
EPS = 1e-6
QK_SCALE = HEAD_DIM ** -0.5
LANES = 128
ATT_BLOCK = 256
FOX_Q_BLOCK = 512
SB_Q_BLOCK = 256
NEG_BIG = -1e30
EXP_UNDERFLOW = 110.0
MAX_REFERENCE_EXCESS = 40.0
NORM_BOUND_MARGIN = 1.001

ADAM_LR = 0.001
ADAM_B1 = 0.9
ADAM_B2 = 0.999
ADAM_EPS = 1e-08
ADAM_WD = 0.01
ADAM_STEP = 10

MESH = pl.DeviceIdType.MESH
MIB = 1024 * 1024


def _cparams(vmem_mib=48):
    return pltpu.CompilerParams(vmem_limit_bytes=vmem_mib * MIB)


def _dot(a, b):
    return jnp.dot(a, b, preferred_element_type=F32)


def _dot_tn(a, b):
    return lax.dot_general(a, b, (((0,), (0,)), ((), ())), preferred_element_type=F32)


def _dot_nt(a, b):
    return lax.dot_general(a, b, (((1,), (1,)), ((), ())), preferred_element_type=F32)


def _sigmoid(x):
    return 1.0 / (1.0 + jnp.exp(-x))


def _split2(x):
    hi = x.astype(BF16)
    lo = (x - hi.astype(F32)).astype(BF16)
    return hi, lo


def _dot_split2(x, m):
    hi, lo = _split2(x)
    return _dot(hi, m) + _dot(lo, m)


def _split3(x):
    hi = x.astype(BF16)
    rest = x - hi.astype(F32)
    mid = rest.astype(BF16)
    lo = (rest - mid.astype(F32)).astype(BF16)
    return hi, mid, lo


def _rms(x):
    r = lax.rsqrt(jnp.mean(x * x, axis=-1, keepdims=True) + EPS)
    return x * r, r


def _rms_bwd(dh, xn, r, g):
    dxn = dh * g
    return r * (dxn - xn * jnp.mean(dxn * xn, axis=-1, keepdims=True))


def _colsum(x):
    return jnp.sum(x, axis=0, keepdims=True)


def _row_block(rows, row_bytes, budget):
    best = None
    for t in range(8, rows + 1, 8):
        if rows % t == 0 and t * row_bytes <= budget:
            best = t
    return best if best is not None else rows


def _ffn_fwd(x, g, wg, wu, wd, gather=(), tm=1024):
    s_len = x.shape[0]
    n = len(gather)
    steps = s_len // tm

    def body(x_ref, g_ref, wg_ref, wu_ref, wd_ref, *rest):
        o_ref, a_ref, b_ref, u_ref = rest[n:n + 4]
        h_s, acc_s = rest[2 * n + 4:2 * n + 6]
        i = pl.program_id(0)
        j = pl.program_id(1)
        if n:
            start, finish = _gather_steps(rest[n + 4:2 * n + 4], *rest[2 * n + 6:])
            pl.when((i == 0) & (j == 0))(start)

        @pl.when(j == 0)
        def _():
            xn, _ = _rms(x_ref[...])
            h_s[...] = (xn * g_ref[...]).astype(BF16)
            acc_s[...] = jnp.zeros_like(acc_s)

        chunks = [pl.ds(r * (tm // FFN_CHUNKS), tm // FFN_CHUNKS) for r in range(FFN_CHUNKS)]
        pre = [(_dot_nt(h_s[rows, :], wg_ref[0]), _dot_nt(h_s[rows, :], wu_ref[0])) for rows in chunks]
        us = []
        for rows, (a, b) in zip(chunks, pre):
            a_ref[0, rows, :] = a.astype(BF16)
            b_ref[0, rows, :] = b.astype(BF16)
            u = (a * _sigmoid(a) * b).astype(BF16)
            u_ref[0, rows, :] = u
            us.append(u)
        for rows, u in zip(chunks, us):
            acc_s[rows, :] += _dot(u, wd_ref[0])

        @pl.when(j == N_CHIPS - 1)
        def _():
            o_ref[...] = x_ref[...] + 0.5 * acc_s[...]

        if n:
            pl.when((i == steps - 1) & (j == N_CHIPS - 1))(finish)

    return pl.pallas_call(
        body,
        grid=(steps, N_CHIPS),
        in_specs=[
            pl.BlockSpec((tm, D_MODEL), lambda i, j: (i, 0)),
            pl.BlockSpec((1, D_MODEL), lambda i, j: (0, 0)),
            pl.BlockSpec((1, FF_SHARD, D_MODEL), lambda i, j: (j, 0, 0)),
            pl.BlockSpec((1, FF_SHARD, D_MODEL), lambda i, j: (j, 0, 0)),
            pl.BlockSpec((1, FF_SHARD, D_MODEL), lambda i, j: (j, 0, 0)),
        ] + [ANY] * n,
        out_specs=[pl.BlockSpec((tm, D_MODEL), lambda i, j: (i, 0))]
        + [pl.BlockSpec((1, tm, FF_SHARD), lambda i, j: (j, i, 0))] * 3 + [ANY] * n,
        out_shape=[jax.ShapeDtypeStruct((s_len, D_MODEL), F32)]
        + [jax.ShapeDtypeStruct((N_CHIPS, s_len, FF_SHARD), BF16)] * 3
        + [jax.ShapeDtypeStruct(s.shape, s.dtype) for s in gather],
        input_output_aliases={5 + a: 4 + a for a in range(n)},
        scratch_shapes=[pltpu.VMEM((tm, D_MODEL), BF16), pltpu.VMEM((tm, D_MODEL), F32)]
        + (_gather_semaphores(n) if n else []),
        compiler_params=_cparams(56),
        name="ffn_fwd_gathering" if n else "ffn_fwd",
    )(x, g, wg, wu, wd, *gather)


def _ffn_bwd(x, d, g, a_pre, b_pre, wg, wu, wd, scatter=(), tm=512):
    s_len = x.shape[0]
    nb = s_len // tm
    n = len(scatter)

    def body(x_ref, d_ref, g_ref, a_ref, b_ref, wg_ref, wu_ref, wd_ref, *rest):
        dx_ref, da_ref, db_ref, h_ref, dbf_ref, dg_ref = rest[n:n + 6]
        dbf_s, dh_s = rest[2 * n + 6:2 * n + 8]
        i = pl.program_id(0)
        j = pl.program_id(1)
        if n:
            start, finish = _scatter_steps(rest[0:n], rest[n + 6:2 * n + 6], *rest[2 * n + 8:])
            pl.when((i == 0) & (j == 0))(start)

        @pl.when(j == 0)
        def _():
            xn, _ = _rms(x_ref[...])
            h_ref[...] = (xn * g_ref[...]).astype(BF16)
            dbf = d_ref[...].astype(BF16)
            dbf_s[...] = dbf
            dbf_ref[...] = dbf
            dh_s[...] = jnp.zeros_like(dh_s)

        @pl.when((i == 0) & (j == 0))
        def _():
            dg_ref[...] = jnp.zeros_like(dg_ref)

        chunks = [pl.ds(r * (tm // FFN_CHUNKS), tm // FFN_CHUNKS) for r in range(FFN_CHUNKS)]
        dus = [0.5 * _dot_nt(dbf_s[rows, :], wd_ref[0]) for rows in chunks]
        das, dbs = [], []
        for rows, du in zip(chunks, dus):
            a = a_ref[0, rows, :].astype(F32)
            b = b_ref[0, rows, :].astype(F32)
            s = _sigmoid(a)
            silu = a * s
            da = (du * b * (s * (1.0 + a * (1.0 - s)))).astype(BF16)
            db = (du * silu).astype(BF16)
            da_ref[0, rows, :] = da
            db_ref[0, rows, :] = db
            das.append(da)
            dbs.append(db)
        for rows, da, db in zip(chunks, das, dbs):
            dh_s[rows, :] += _dot(da, wg_ref[0]) + _dot(db, wu_ref[0])

        @pl.when(j == N_CHIPS - 1)
        def _():
            xn, r = _rms(x_ref[...])
            dh = dh_s[...]
            dx_ref[...] = d_ref[...] + _rms_bwd(dh, xn, r, g_ref[...])
            dg_ref[0:1, :] += _colsum(dh * xn)

        if n:
            pl.when((i == nb - 1) & (j == N_CHIPS - 1))(finish)

    row = lambda i, j: (i, 0)
    shard = lambda i, j: (j, 0, 0)
    act = lambda i, j: (j, i, 0)
    return pl.pallas_call(
        body,
        grid=(nb, N_CHIPS),
        in_specs=[
            pl.BlockSpec((tm, D_MODEL), row),
            pl.BlockSpec((tm, D_MODEL), row),
            pl.BlockSpec((1, D_MODEL), lambda i, j: (0, 0)),
            pl.BlockSpec((1, tm, FF_SHARD), act),
            pl.BlockSpec((1, tm, FF_SHARD), act),
            pl.BlockSpec((1, FF_SHARD, D_MODEL), shard),
            pl.BlockSpec((1, FF_SHARD, D_MODEL), shard),
            pl.BlockSpec((1, FF_SHARD, D_MODEL), shard),
        ] + [ANY] * n,
        out_specs=[
            pl.BlockSpec((tm, D_MODEL), row),
            pl.BlockSpec((1, tm, FF_SHARD), act),
            pl.BlockSpec((1, tm, FF_SHARD), act),
            pl.BlockSpec((tm, D_MODEL), row),
            pl.BlockSpec((tm, D_MODEL), row),
            pl.BlockSpec((8, D_MODEL), lambda i, j: (0, 0)),
        ] + [ANY] * n,
        out_shape=[
            jax.ShapeDtypeStruct((s_len, D_MODEL), F32),
            jax.ShapeDtypeStruct((N_CHIPS, s_len, FF_SHARD), BF16),
            jax.ShapeDtypeStruct((N_CHIPS, s_len, FF_SHARD), BF16),
            jax.ShapeDtypeStruct((s_len, D_MODEL), BF16),
            jax.ShapeDtypeStruct((s_len, D_MODEL), BF16),
            jax.ShapeDtypeStruct((8, D_MODEL), F32),
        ] + [jax.ShapeDtypeStruct(s.shape, s.dtype) for s in scatter],
        scratch_shapes=[
            pltpu.VMEM((tm, D_MODEL), BF16),
            pltpu.VMEM((tm, D_MODEL), F32),
        ] + (_scatter_semaphores(n) if n else []),
        compiler_params=_cparams(56),
        name="ffn_bwd_scattering" if n else "ffn_bwd",
    )(x, d, g, a_pre, b_pre, wg, wu, wd, *scatter)


def _wgrad(a, b, scale=1.0, name="wgrad"):
    na, s_len, k_dim = a.shape
    nb, _, n_dim = b.shape
    n = max(na, nb)
    ts = WGRAD_TOKENS
    while ts > 512 and (ts > s_len or 2 * ts * (k_dim * a.dtype.itemsize + n_dim * b.dtype.itemsize) > WGRAD_VMEM):
        ts //= 2
    steps = s_len // ts

    def body(a_ref, b_ref, o_ref, acc_s):
        s = pl.program_id(1)

        @pl.when(s == 0)
        def _():
            acc_s[...] = jnp.zeros_like(acc_s)

        acc_s[...] += _dot_tn(a_ref[0].astype(BF16), b_ref[0].astype(BF16))

        @pl.when(s == steps - 1)
        def _():
            o_ref[0] = (acc_s[...] * scale).astype(BF16)

    a_map = (lambda m, s: (m, s, 0)) if na > 1 else (lambda m, s: (0, s, 0))
    b_map = (lambda m, s: (m, s, 0)) if nb > 1 else (lambda m, s: (0, s, 0))
    return pl.pallas_call(
        body,
        grid=(n, steps),
        in_specs=[pl.BlockSpec((1, ts, k_dim), a_map), pl.BlockSpec((1, ts, n_dim), b_map)],
        out_specs=pl.BlockSpec((1, k_dim, n_dim), lambda m, s: (m, 0, 0)),
        out_shape=jax.ShapeDtypeStruct((n, k_dim, n_dim), BF16),
        scratch_shapes=[pltpu.VMEM((k_dim, n_dim), F32)],
        compiler_params=_cparams(56),
        name=name,
    )(a, b)


def _head_sum_matrices():
    lane = lax.broadcasted_iota(jnp.int32, (ATT_W, LANES), 0) // HEAD_DIM
    col = lax.broadcasted_iota(jnp.int32, (ATT_W, LANES), 1)
    bd = (lane == col).astype(BF16)
    return bd, bd.T


def _head_mean(t, bd, bd_t):
    per_head = _dot_split2(t, bd) * (1.0 / HEAD_DIM)
    return _dot_split2(per_head, bd_t)


def _head_rms(x, bd, bd_t):
    per_head = _dot_split2(x * x, bd) * (1.0 / HEAD_DIM)
    r = lax.rsqrt(per_head + EPS)
    rw = _dot_split2(r, bd_t)
    return x * rw, rw


def _log_sigmoid(z):
    return jnp.minimum(z, 0.0) - jnp.log(1.0 + jnp.exp(-jnp.abs(z)))


def _inproj_fwd(x1, g, w_fox, w_fl, w_sb, w_gates, bias, qn, kn, bd, bd_t, tm=512):
    s_len = x1.shape[0]

    def body(x_ref, g_ref, wf_ref, wl_ref, ws_ref, wg_ref, bias_ref, qn_ref, kn_ref, bd_ref, bdt_ref,
             fq_ref, fk_ref, qs_ref, kf_ref, vf_ref, logf_ref, sq_ref, sk_ref, sv_ref, gates_ref, knorm_ref):
        xn, _ = _rms(x_ref[...])
        h = (xn * g_ref[...]).astype(BF16)
        zf = _dot(h, wf_ref[...])
        fq = zf[:, 0:ATT_W]
        fk = zf[:, ATT_W:2 * ATT_W]
        fq_ref[...] = fq
        fk_ref[...] = fk
        bd_m = bd_ref[...]
        bdt_m = bdt_ref[...]
        fqn, _ = _head_rms(fq, bd_m, bdt_m)
        fkn, _ = _head_rms(fk, bd_m, bdt_m)
        qs_ref[...] = (fqn * qn_ref[...]).astype(BF16) * QK_SCALE
        kf = (fkn * kn_ref[...]).astype(BF16)
        kf_ref[...] = kf
        k_sq = _dot_split2(jnp.square(kf.astype(F32)), bd_m)
        knorm_ref[...] = jnp.broadcast_to(jnp.max(k_sq, axis=0, keepdims=True), knorm_ref.shape)
        vf_ref[...] = zf[:, 2 * ATT_W:3 * ATT_W].astype(BF16)
        logf_ref[...] = _log_sigmoid(_dot(h, wl_ref[...]) + bias_ref[...])
        zs = _dot(h, ws_ref[...])
        sq_ref[...] = zs[:, 0:ATT_W].astype(BF16) * QK_SCALE
        sk_ref[...] = zs[:, ATT_W:2 * ATT_W].astype(BF16)
        sv_ref[...] = zs[:, 2 * ATT_W:3 * ATT_W].astype(BF16)
        gates_ref[...] = _dot(h, wg_ref[...]).astype(BF16)

    row = lambda i: (i, 0)
    full = lambda i: (0, 0)
    att = lambda dt: jax.ShapeDtypeStruct((s_len, ATT_W), dt)
    return pl.pallas_call(
        body,
        grid=(s_len // tm,),
        in_specs=[
            pl.BlockSpec((tm, D_MODEL), row),
            pl.BlockSpec((1, D_MODEL), full),
            pl.BlockSpec((D_MODEL, 3 * ATT_W), full),
            pl.BlockSpec((D_MODEL, LANES), full),
            pl.BlockSpec((D_MODEL, 3 * ATT_W), full),
            pl.BlockSpec((D_MODEL, 2 * D_MODEL), full),
            pl.BlockSpec((1, LANES), full),
            pl.BlockSpec((1, ATT_W), full),
            pl.BlockSpec((1, ATT_W), full),
            pl.BlockSpec((ATT_W, LANES), full),
            pl.BlockSpec((LANES, ATT_W), full),
        ],
        out_specs=[
            pl.BlockSpec((tm, ATT_W), row), pl.BlockSpec((tm, ATT_W), row),
            pl.BlockSpec((tm, ATT_W), row), pl.BlockSpec((tm, ATT_W), row), pl.BlockSpec((tm, ATT_W), row),
            pl.BlockSpec((tm, LANES), row),
            pl.BlockSpec((tm, ATT_W), row), pl.BlockSpec((tm, ATT_W), row), pl.BlockSpec((tm, ATT_W), row),
            pl.BlockSpec((tm, 2 * D_MODEL), row),
            pl.BlockSpec((8, LANES), row),
        ],
        out_shape=[
            att(F32), att(F32), att(BF16), att(BF16), att(BF16),
            jax.ShapeDtypeStruct((s_len, LANES), F32),
            att(BF16), att(BF16), att(BF16),
            jax.ShapeDtypeStruct((s_len, 2 * D_MODEL), BF16),
            jax.ShapeDtypeStruct((8 * (s_len // tm), LANES), F32),
        ],
        compiler_params=_cparams(56),
        name="inproj_fwd",
    )(x1, g, w_fox, w_fl, w_sb, w_gates, bias, qn, kn, bd, bd_t)


def _tri(n, kind):
    r = lax.broadcasted_iota(jnp.int32, (n, n), 0)
    c = lax.broadcasted_iota(jnp.int32, (n, n), 1)
    m = {"row_ge_col": r >= c, "row_le_col": r <= c, "row_gt_col": r > c, "row_lt_col": r < c}[kind]
    return m.astype(BF16)


def _cumsum_rows(x, reverse, spread=None, tm=256):
    s_len = x.shape[0]
    nb = s_len // tm
    tri = _tri(tm, "row_le_col" if reverse else "row_ge_col")
    edge = 0 if reverse else tm - 1

    def body(x_ref, tri_ref, *rest):
        spread_ref, o_ref, wide_ref, carry_s = rest if spread is not None else (None, rest[0], None, rest[1])

        @pl.when(pl.program_id(0) == 0)
        def _():
            carry_s[...] = jnp.zeros_like(carry_s)

        hi, mid, lo = _split3(x_ref[...])
        t = tri_ref[...]
        y = _dot(t, hi) + _dot(t, mid) + _dot(t, lo) + carry_s[...]
        o_ref[...] = y
        carry_s[...] = y[edge:edge + 1, :]
        if spread is not None:
            hi, mid, lo = _split3(y)
            m = spread_ref[...]
            wide_ref[...] = _dot(hi, m) + _dot(mid, m) + _dot(lo, m)

    order = (lambda i: (nb - 1 - i, 0)) if reverse else (lambda i: (i, 0))
    narrow = pl.BlockSpec((tm, LANES), order)
    in_specs, out_specs = [narrow, pl.BlockSpec((tm, tm), lambda i: (0, 0))], [narrow]
    out_shape = [jax.ShapeDtypeStruct((s_len, LANES), F32)]
    if spread is not None:
        width = spread.shape[1]
        in_specs.append(pl.BlockSpec((LANES, width), lambda i: (0, 0)))
        out_specs.append(pl.BlockSpec((tm, width), order))
        out_shape.append(jax.ShapeDtypeStruct((s_len, width), F32))
    out = pl.pallas_call(
        body,
        grid=(nb,),
        in_specs=in_specs,
        out_specs=out_specs,
        out_shape=out_shape,
        scratch_shapes=[pltpu.VMEM((1, LANES), F32)],
        name="cumsum_rev" if reverse else "cumsum_fwd",
    )(*([x, tri] if spread is None else [x, tri, spread]))
    return out[0] if spread is None else out


def _unblocked_t(t4):
    _, nb, _, blk = t4.shape
    return t4.transpose(1, 3, 0, 2).reshape(nb * blk, ATT_W)


def _transposed_spec(tm):
    return pl.BlockSpec((N_PAIRS, tm // ATT_BLOCK, PAIR_W, ATT_BLOCK), lambda i: (0, i, 0, 0))


def _rows_of_transposed(ref):
    return jnp.concatenate(
        [jnp.concatenate([ref[p, b].T for p in range(N_PAIRS)], axis=1) for b in range(ref.shape[1])], axis=0)


def _blocked_rows(t, blk):
    return t.reshape(t.shape[0] // blk, blk, t.shape[1])


def _pair_rows_t(f8, blk):
    nb = f8.shape[0] // blk
    t = f8.reshape(nb, blk, N_PAIRS, 2).transpose(2, 0, 3, 1)
    return jnp.pad(t, ((0, 0), (0, 0), (0, 6), (0, 0)))


def _unpair_rows_t(t4):
    _, nb, _, blk = t4.shape
    return t4[:, :, 0:2, :].transpose(1, 3, 0, 2).reshape(nb * blk, N_HEADS)


def _head_masks(tq):
    lane = lax.broadcasted_iota(jnp.int32, (tq, PAIR_W), 1)
    return lane < HEAD_DIM


def _causal_mask(tq, tk, offset, strict):
    d = lax.broadcasted_iota(jnp.int32, (tq, tk), 1) - lax.broadcasted_iota(jnp.int32, (tq, tk), 0)
    return (d < offset) if strict else (d <= offset)


def _heads_of(ref, first):
    t = ref[...]
    zero = jnp.zeros_like(t)
    return [jnp.where(first, t, zero), jnp.where(first, zero, t)]


def _head_cols(ref):
    t = ref[...]
    return [t[:, 0:1], t[:, HEAD_DIM:HEAD_DIM + 1]]


def _att_specs(s_len, tq):
    tk = ATT_BLOCK
    nq, nk = s_len // tq, s_len // tk
    return dict(
        nq=nq,
        q=pl.BlockSpec((tq, PAIR_W), lambda p, i: (i, p)),
        k_t=pl.BlockSpec((1, nk, PAIR_W, tk), lambda p, i: (p, 0, 0, 0)),
        k_rows=pl.BlockSpec((nk, tk, PAIR_W), lambda p, i: (0, 0, p)),
        f_t=pl.BlockSpec((1, nk, 8, tk), lambda p, i: (p, 0, 0, 0)),
        first=pl.BlockSpec((1, 1, 8, LANES), lambda p, i: (p, i, 0, 0)),
        wide=jax.ShapeDtypeStruct((s_len, ATT_W), F32),
        k_t_out=jax.ShapeDtypeStruct((N_PAIRS, nk, PAIR_W, tk), F32),
        f_t_out=jax.ShapeDtypeStruct((N_PAIRS, nk, 8, tk), F32),
        first_out=jax.ShapeDtypeStruct((N_PAIRS, nq, 8, LANES), F32),
        acc=pltpu.VMEM((2, tq, PAIR_W), F32),
    )


def _first_block(first_ref, limit):
    return jnp.clip(jnp.max(first_ref[0, 0]).astype(jnp.int32), 0, limit)


def _key_norm_bound(k_sq):
    bound = jnp.sqrt(jnp.max(k_sq[:, 0:N_HEADS], axis=0)).reshape(N_PAIRS, 2) * NORM_BOUND_MARGIN
    return jnp.broadcast_to(jnp.pad(bound, ((0, 0), (0, 6)))[:, :, None], (N_PAIRS, 8, LANES))


def _fox_fwd(qs, k3, v3, fw, ft4, kmax):
    tq, tk = FOX_Q_BLOCK, ATT_BLOCK
    sp = _att_specs(qs.shape[0], tq)
    ratio, nk = tq // tk, qs.shape[0] // tk
    f_block_ends = ft4[:, :, :2, tk - 1].reshape(-1)

    def body(fend_ref, q_ref, k_ref, v_ref, fw_ref, ft_ref, kmax_ref, y_ref, lse_ref, first_ref,
             acc_ref, max_ref, sum_ref):
        pair, i = pl.program_id(0), pl.program_id(1)
        first = _head_masks(tq)
        qh = _heads_of(q_ref, first)
        fqh = _head_cols(fw_ref)
        reach = []
        for n in range(2):
            qf = qh[n].astype(F32)
            reach.append(jnp.sqrt(jnp.sum(qf * qf, axis=-1, keepdims=True)) * kmax_ref[0, n:n + 1, 0:1] + fqh[n])

        def logits(j, shift, r0=0, diag=False):
            k, fk = k_ref[j], ft_ref[0, j]
            raw = [_dot_nt(qh[n][r0:], k) for n in range(2)]
            out = []
            for n in range(2):
                s = raw[n] + (shift[n][r0:] - fk[n:n + 1, :])
                if diag:
                    s = jnp.where(_causal_mask(tq - r0, tk, 0, strict=False), s, NEG_BIG)
                out.append(s)
            return out

        def max_pass(j, r0=0, diag=False, assign=False):
            ss = logits(j, fqh, r0, diag)
            for n in range(2):
                max_ref[n, r0:] = ss[n] if assign else jnp.maximum(max_ref[n, r0:], ss[n])

        def sum_pass(j, shift, r0=0, diag=False, assign=False):
            ps = [jnp.exp(s) for s in logits(j, shift, r0, diag)]
            v = v_ref[j]
            for n in range(2):
                sum_ref[n, r0:] = ps[n] if assign else sum_ref[n, r0:] + ps[n]
            for n in range(2):
                pv = _dot(ps[n].astype(BF16), v)
                acc_ref[n, r0:] = pv if assign else acc_ref[n, r0:] + pv

        for d in range(ratio):
            max_pass(ratio * i + d, d * tk, True, d == 0)

        m_diag = [jnp.max(max_ref[n], axis=-1, keepdims=True) for n in range(2)]
        slack = [jnp.max(reach[n] - m_diag[n]) for n in range(2)]

        def f_end(j, n):
            return fend_ref[(pair * nk + jnp.maximum(j, 0)) * 2 + n]

        def block_matters(j):
            gap = jnp.maximum(slack[0] - f_end(j, 0), slack[1] - f_end(j, 1))
            return (j >= 0) & (gap > -EXP_UNDERFLOW)

        last_left = ratio * i - 1
        j_first = lax.while_loop(block_matters, lambda j: j - 1, last_left) + 1

        bound = [reach[n] - f_end(last_left, n) for n in range(2)]
        excess = jnp.maximum(jnp.max(bound[0] - m_diag[0]), jnp.max(bound[1] - m_diag[1]))
        exact = excess > MAX_REFERENCE_EXCESS

        def exact_max():
            def one_max(j, c):
                max_pass(j)
                return c
            lax.fori_loop(j_first, ratio * i, one_max, 0)
            return [jnp.max(max_ref[n], axis=-1, keepdims=True) for n in range(2)]

        def bounded_max():
            walked_left = j_first < ratio * i
            return [jnp.maximum(m_diag[n], jnp.where(walked_left, bound[n], NEG_BIG)) for n in range(2)]

        m = lax.cond(exact, exact_max, bounded_max)
        shift = [fqh[n] - m[n] for n in range(2)]

        for d in range(ratio):
            sum_pass(ratio * i + d, shift, d * tk, True, d == 0)

        def one(j, c):
            sum_pass(j, shift)
            return c
        lax.fori_loop(j_first, ratio * i, one, 0)
        l = [jnp.sum(sum_ref[n], axis=-1, keepdims=True) for n in range(2)]
        y_ref[...] = jnp.where(first, acc_ref[0] / l[0], acc_ref[1] / l[1])
        lse_ref[...] = jnp.where(first, m[0] + jnp.log(l[0]), m[1] + jnp.log(l[1]))
        first_ref[...] = jnp.ones(first_ref.shape, F32) * j_first.astype(F32)

    tile = pltpu.VMEM((2, tq, tk), F32)
    return pl.pallas_call(
        body,
        grid=(N_PAIRS, sp["nq"]),
        in_specs=[pl.BlockSpec(memory_space=pltpu.SMEM), sp["q"], sp["k_rows"], sp["k_rows"], sp["q"], sp["f_t"],
                  pl.BlockSpec((1, 8, LANES), lambda p, i: (p, 0, 0))],
        out_specs=[sp["q"], sp["q"], sp["first"]],
        out_shape=[sp["wide"], sp["wide"], sp["first_out"]],
        scratch_shapes=[sp["acc"], tile, tile],
        compiler_params=_cparams(56),
        name="fox_fwd",
    )(f_block_ends, qs, k3, v3, fw, ft4, kmax)


def _fox_bwd(qs, k3, v3, dy, y, lse, fw, ft4, first_block):
    tq, tk = FOX_Q_BLOCK, ATT_BLOCK
    sp = _att_specs(qs.shape[0], tq)
    ratio = tq // tk

    def body(q_ref, k_ref, v_ref, dy_ref, y_ref, lse_ref, fw_ref, ft_ref, first_ref,
             dq_ref, dfq_ref, dkt_ref, dvt_ref, dft_ref, acc_ref):
        i = pl.program_id(1)

        @pl.when(i == 0)
        def _():
            dkt_ref[...] = jnp.zeros_like(dkt_ref)
            dvt_ref[...] = jnp.zeros_like(dvt_ref)
            dft_ref[...] = jnp.zeros_like(dft_ref)

        first = _head_masks(tq)
        qh = _heads_of(q_ref, first)
        dyv = dy_ref[...]
        dyb = dyv.astype(BF16)
        zero = jnp.zeros_like(dyb)
        dyh = [jnp.where(first, dyb, zero), jnp.where(first, zero, dyb)]
        prod = dyv * y_ref[...]
        zf = jnp.zeros_like(prod)
        delta = [jnp.sum(jnp.where(first, prod, zf), axis=-1, keepdims=True),
                 jnp.sum(jnp.where(first, zf, prod), axis=-1, keepdims=True)]
        fqh = _head_cols(fw_ref)
        lseh = _head_cols(lse_ref)
        shift = [fqh[n] - lseh[n] for n in range(2)]
        acc_ref[...] = jnp.zeros_like(acc_ref)

        def block(j, rows, r0=0, diag=False):
            mask = _causal_mask(tq - r0, tk, 0, strict=False) if diag else None
            k, v, fk = k_ref[j], v_ref[j], ft_ref[0, j]
            q_part, dy_part = [t[r0:] for t in qh], [t[r0:] for t in dyh]
            logits = [_dot_nt(q_part[n], k) for n in range(2)]
            dps = [_dot_nt(dy_part[n], v) for n in range(2)]
            pbs, dsbs, out = [], [], []
            for n in range(2):
                p = jnp.exp(logits[n] + (shift[n][r0:] - fk[n:n + 1, :]))
                if diag:
                    p = jnp.where(mask, p, 0.0)
                ds = p * (dps[n] - delta[n][r0:])
                pbs.append(p.astype(BF16))
                dsbs.append(ds.astype(BF16))
                row_sum = jnp.sum(ds, axis=-1, keepdims=True)
                if r0:
                    row_sum = jnp.concatenate([jnp.zeros((r0, 1), F32), row_sum], axis=0)
                out.append(rows[n] + row_sum)
                dft_ref[0, j, n:n + 1, :] -= _colsum(ds)
            for n in range(2):
                acc_ref[n, r0:] += _dot(dsbs[n], k)
            dkt_ref[0, j] += _dot_tn(q_part[0], dsbs[0]) + _dot_tn(q_part[1], dsbs[1])
            dvt_ref[0, j] += _dot_tn(dy_part[0], pbs[0]) + _dot_tn(dy_part[1], pbs[1])
            return tuple(out)

        rows = (jnp.zeros((tq, 1), F32),) * 2
        rows = lax.fori_loop(_first_block(first_ref, ratio * i), ratio * i, lambda j, c: block(j, c), rows)
        for d in range(ratio):
            rows = block(ratio * i + d, rows, d * tk, True)
        dq_ref[...] = jnp.where(first, acc_ref[0], acc_ref[1])
        lane = lax.broadcasted_iota(jnp.int32, (tq, 8), 1)
        dfq_ref[0] = jnp.where(lane == 0, rows[0], jnp.where(lane == 1, rows[1], 0.0))

    return pl.pallas_call(
        body,
        grid=(N_PAIRS, sp["nq"]),
        in_specs=[sp["q"], sp["k_rows"], sp["k_rows"], sp["q"], sp["q"], sp["q"], sp["q"], sp["f_t"], sp["first"]],
        out_specs=[sp["q"], pl.BlockSpec((1, tq, 8), lambda p, i: (p, i, 0)), sp["k_t"], sp["k_t"], sp["f_t"]],
        out_shape=[sp["wide"], jax.ShapeDtypeStruct((N_PAIRS, qs.shape[0], 8), F32),
                   sp["k_t_out"], sp["k_t_out"], sp["f_t_out"]],
        scratch_shapes=[sp["acc"]],
        compiler_params=_cparams(56),
        name="fox_bwd",
    )(qs, k3, v3, dy, y, lse, fw, ft4, first_block)


SIGN_BIT = 0x80000000


def _sb_terms(z, mask, diag):
    neg_abs = pltpu.bitcast(pltpu.bitcast(z, jnp.uint32) | jnp.uint32(SIGN_BIT), F32)
    lb = jnp.minimum(z, 0.0) - jnp.log(1.0 + jnp.exp(neg_abs))
    l1m = lb - z
    if diag:
        l1m = jnp.where(mask, l1m, 0.0)
    return lb, l1m


def _dot_split2_stacked(x, m2):
    hi, lo = _split2(x)
    return _dot(jnp.concatenate([hi, lo], axis=1), m2)


def _tri_stacked(kind):
    t = _tri(ATT_BLOCK, kind)
    return jnp.concatenate([t, t], axis=0)


def _sb_fwd(qs, k3, v3):
    tq, tk = SB_Q_BLOCK, ATT_BLOCK
    sp = _att_specs(qs.shape[0], tq)
    ratio = tq // tk
    upper = _tri_stacked("row_gt_col")

    def body(q_ref, k_ref, v_ref, u_ref, y_ref, rtot_ref, first_ref, acc_ref):
        i = pl.program_id(1)
        first = _head_masks(tq)
        qh = _heads_of(q_ref, first)
        u = u_ref[...]
        acc_ref[...] = jnp.zeros_like(acc_ref)

        def block(j, rs, diag):
            mask = _causal_mask(tq, tk, i * tq - j * tk, strict=True) if diag else None
            k, v = k_ref[j], v_ref[j]
            logits = [_dot_nt(qh[n], k) for n in range(2)]
            terms = [_sb_terms(z, mask, diag) for z in logits]
            right = [_dot_split2_stacked(l1m, u) for _, l1m in terms]
            weights = []
            for n in range(2):
                a = jnp.exp(terms[n][0] + right[n] + rs[n])
                if diag:
                    a = jnp.where(mask, a, 0.0)
                weights.append(a.astype(BF16))
            for n in range(2):
                acc_ref[n] += _dot(weights[n], v)
            return tuple(rs[n] + jnp.sum(terms[n][1], axis=-1, keepdims=True) for n in range(2))

        rs = (jnp.zeros((tq, 1), F32),) * 2
        for d in range(ratio):
            rs = block(ratio * i + (ratio - 1 - d), rs, True)

        def block_matters(c):
            j, r0, r1 = c
            return (j >= 0) & (jnp.max(jnp.maximum(r0, r1)) > -EXP_UNDERFLOW)

        def walk_left(c):
            j, r0, r1 = c
            r0, r1 = block(j, (r0, r1), False)
            return j - 1, r0, r1

        j, r0, r1 = lax.while_loop(block_matters, walk_left, (ratio * i - 1, rs[0], rs[1]))
        y_ref[...] = jnp.where(first, acc_ref[0], acc_ref[1])
        rtot_ref[...] = jnp.where(first, r0, r1)
        first_ref[...] = jnp.ones(first_ref.shape, F32) * (j + 1).astype(F32)

    return pl.pallas_call(
        body,
        grid=(N_PAIRS, sp["nq"]),
        in_specs=[sp["q"], sp["k_rows"], sp["k_rows"], pl.BlockSpec((2 * tk, tk), lambda p, i: (0, 0))],
        out_specs=[sp["q"], sp["q"], sp["first"]],
        out_shape=[sp["wide"], sp["wide"], sp["first_out"]],
        scratch_shapes=[sp["acc"]],
        compiler_params=_cparams(56),
        name="sb_fwd",
    )(qs, k3, v3, upper)


def _sb_bwd(qs, k3, v3, dy, rtot, first_block):
    tq, tk = SB_Q_BLOCK, ATT_BLOCK
    sp = _att_specs(qs.shape[0], tq)
    ratio = tq // tk
    lower_in = _tri_stacked("row_le_col")
    lower = _tri(tk, "row_lt_col")

    def body(q_ref, k_ref, v_ref, dy_ref, rtot_ref, first_ref, li_ref, l_ref, dq_ref, dkt_ref, dvt_ref, acc_ref):
        i = pl.program_id(1)

        @pl.when(i == 0)
        def _():
            dkt_ref[...] = jnp.zeros_like(dkt_ref)
            dvt_ref[...] = jnp.zeros_like(dvt_ref)

        first = _head_masks(tq)
        qh = _heads_of(q_ref, first)
        dyb = dy_ref[...].astype(BF16)
        zero = jnp.zeros_like(dyb)
        dyh = [jnp.where(first, dyb, zero), jnp.where(first, zero, dyb)]
        rtoth = _head_cols(rtot_ref)
        li = li_ref[...]
        lo_tri = l_ref[...]
        acc_ref[...] = jnp.zeros_like(acc_ref)

        def block(j, carry, diag):
            mask = _causal_mask(tq, tk, i * tq - j * tk, strict=True) if diag else None
            k, v = k_ref[j], v_ref[j]
            logits = [_dot_nt(qh[n], k) for n in range(2)]
            das = [_dot_nt(dyh[n], v) for n in range(2)]
            terms = [_sb_terms(z, mask, diag) for z in logits]
            upto = [_dot_split2_stacked(l1m, li) for _, l1m in terms]
            des, weights = [], []
            for n in range(2):
                a = jnp.exp(terms[n][0] + ((rtoth[n] - carry[2 * n]) - upto[n]))
                if diag:
                    a = jnp.where(mask, a, 0.0)
                des.append(a * das[n])
                weights.append(a.astype(BF16))
            lefts = [_dot(de.astype(BF16), lo_tri) for de in des]
            dzbs, out = [], []
            for n in range(2):
                beta = jnp.exp(terms[n][0])
                dz = des[n] - (des[n] + (carry[2 * n + 1] + lefts[n])) * beta
                if diag:
                    dz = jnp.where(mask, dz, 0.0)
                dzbs.append(dz.astype(BF16))
                out += [carry[2 * n] + jnp.sum(terms[n][1], axis=-1, keepdims=True),
                        carry[2 * n + 1] + jnp.sum(des[n], axis=-1, keepdims=True)]
            for n in range(2):
                acc_ref[n] += _dot(dzbs[n], k)
            dkt_ref[0, j] += _dot_tn(qh[0], dzbs[0]) + _dot_tn(qh[1], dzbs[1])
            dvt_ref[0, j] += _dot_tn(dyh[0], weights[0]) + _dot_tn(dyh[1], weights[1])
            return tuple(out)

        carry = (jnp.zeros((tq, 1), F32),) * 4
        carry = lax.fori_loop(_first_block(first_ref, ratio * i), ratio * i, lambda j, c: block(j, c, False), carry)
        for d in range(ratio):
            carry = block(ratio * i + d, carry, True)
        dq_ref[...] = jnp.where(first, acc_ref[0], acc_ref[1])

    return pl.pallas_call(
        body,
        grid=(N_PAIRS, sp["nq"]),
        in_specs=[sp["q"], sp["k_rows"], sp["k_rows"], sp["q"], sp["q"], sp["first"],
                  pl.BlockSpec((2 * tk, tk), lambda p, i: (0, 0)), pl.BlockSpec((tk, tk), lambda p, i: (0, 0))],
        out_specs=[sp["q"], sp["k_t"], sp["k_t"]],
        out_shape=[sp["wide"], sp["k_t_out"], sp["k_t_out"]],
        scratch_shapes=[sp["acc"]],
        compiler_params=_cparams(56),
        name="sb_bwd",
    )(qs, k3, v3, dy, rtot, first_block, lower_in, lower)


def _merge_fwd(x1, gates, y_fox, y_sb, w_bf, w_bs, w_out, tm=512):
    s_len = x1.shape[0]

    def body(x_ref, g_ref, yf_ref, ys_ref, wbf_ref, wbs_ref, wo_ref, o_ref):
        g = g_ref[...].astype(F32)
        of = _dot(yf_ref[...].astype(BF16), wbf_ref[...])
        os_ = _dot(ys_ref[...].astype(BF16), wbs_ref[...])
        merged = _sigmoid(g[:, 0:D_MODEL]) * of + _sigmoid(g[:, D_MODEL:]) * os_
        o_ref[...] = x_ref[...] + _dot(merged.astype(BF16), wo_ref[...])

    row = lambda i: (i, 0)
    full = lambda i: (0, 0)
    return pl.pallas_call(
        body,
        grid=(s_len // tm,),
        in_specs=[
            pl.BlockSpec((tm, D_MODEL), row),
            pl.BlockSpec((tm, 2 * D_MODEL), row),
            pl.BlockSpec((tm, ATT_W), row),
            pl.BlockSpec((tm, ATT_W), row),
            pl.BlockSpec((ATT_W, D_MODEL), full),
            pl.BlockSpec((ATT_W, D_MODEL), full),
            pl.BlockSpec((D_MODEL, D_MODEL), full),
        ],
        out_specs=pl.BlockSpec((tm, D_MODEL), row),
        out_shape=jax.ShapeDtypeStruct((s_len, D_MODEL), F32),
        compiler_params=_cparams(48),
        name="merge_fwd",
    )(x1, gates, y_fox, y_sb, w_bf, w_bs, w_out)


def _merge_bwd(dx2, gates, y_fox, y_sb, w_bf, w_bs, w_out, tm=512):
    s_len = dx2.shape[0]

    def body(d_ref, g_ref, yf_ref, ys_ref, wbf_ref, wbs_ref, wo_ref,
             dyf_ref, dys_ref, dg_ref, dof_ref, dos_ref, m_ref, dbf_ref):
        dbf = d_ref[...].astype(BF16)
        dbf_ref[...] = dbf
        dm = _dot_nt(dbf, wo_ref[...])
        g = g_ref[...].astype(F32)
        of = _dot(yf_ref[...].astype(BF16), wbf_ref[...])
        os_ = _dot(ys_ref[...].astype(BF16), wbs_ref[...])
        sf = _sigmoid(g[:, 0:D_MODEL])
        ss = _sigmoid(g[:, D_MODEL:])
        m_ref[...] = (sf * of + ss * os_).astype(BF16)
        d_of = (dm * sf).astype(BF16)
        d_os = (dm * ss).astype(BF16)
        dof_ref[...] = d_of
        dos_ref[...] = d_os
        dg_ref[:, 0:D_MODEL] = (dm * of * sf * (1.0 - sf)).astype(BF16)
        dg_ref[:, D_MODEL:] = (dm * os_ * ss * (1.0 - ss)).astype(BF16)
        dyf_ref[...] = _dot_nt(d_of, wbf_ref[...])
        dys_ref[...] = _dot_nt(d_os, wbs_ref[...])

    row = lambda i: (i, 0)
    full = lambda i: (0, 0)
    return pl.pallas_call(
        body,
        grid=(s_len // tm,),
        in_specs=[
            pl.BlockSpec((tm, D_MODEL), row),
            pl.BlockSpec((tm, 2 * D_MODEL), row),
            pl.BlockSpec((tm, ATT_W), row),
            pl.BlockSpec((tm, ATT_W), row),
            pl.BlockSpec((ATT_W, D_MODEL), full),
            pl.BlockSpec((ATT_W, D_MODEL), full),
            pl.BlockSpec((D_MODEL, D_MODEL), full),
        ],
        out_specs=[
            pl.BlockSpec((tm, ATT_W), row), pl.BlockSpec((tm, ATT_W), row),
            pl.BlockSpec((tm, 2 * D_MODEL), row),
            pl.BlockSpec((tm, D_MODEL), row), pl.BlockSpec((tm, D_MODEL), row),
            pl.BlockSpec((tm, D_MODEL), row), pl.BlockSpec((tm, D_MODEL), row),
        ],
        out_shape=[
            jax.ShapeDtypeStruct((s_len, ATT_W), F32), jax.ShapeDtypeStruct((s_len, ATT_W), F32),
            jax.ShapeDtypeStruct((s_len, 2 * D_MODEL), BF16),
            jax.ShapeDtypeStruct((s_len, D_MODEL), BF16), jax.ShapeDtypeStruct((s_len, D_MODEL), BF16),
            jax.ShapeDtypeStruct((s_len, D_MODEL), BF16), jax.ShapeDtypeStruct((s_len, D_MODEL), BF16),
        ],
        compiler_params=_cparams(56),
        name="merge_bwd",
    )(dx2, gates, y_fox, y_sb, w_bf, w_bs, w_out)


def _ple_loss(x3, p, g, w_pg, w_pp, target, tm=512):
    s_len = x3.shape[0]
    inv_d = 1.0 / D_MODEL

    def body(x_ref, p_ref, g_ref, wpg_ref, wpp_ref, t_ref,
             dx_ref, du_ref, dt_ref, hn_ref, dg_ref, loss_ref):
        @pl.when(pl.program_id(0) == 0)
        def _():
            dg_ref[...] = jnp.zeros_like(dg_ref)
            loss_ref[...] = jnp.zeros_like(loss_ref)

        x = x_ref[...]
        xn, r = _rms(x)
        gain = g_ref[...]
        hn = (xn * gain).astype(BF16)
        hn_ref[...] = hn
        sg = _sigmoid(_dot(hn, wpg_ref[...]))
        t = _dot(p_ref[...].astype(BF16), wpp_ref[...])
        err = x + sg * t - t_ref[...]
        sq = jnp.sum(_colsum(err * err), axis=-1, keepdims=True)
        loss_ref[...] += (0.5 * inv_d) * sq
        dy = err * inv_d
        du = (dy * t * sg * (1.0 - sg)).astype(BF16)
        du_ref[...] = du
        dt_ref[...] = (dy * sg).astype(BF16)
        dh = _dot_nt(du, wpg_ref[...])
        dx_ref[...] = dy + _rms_bwd(dh, xn, r, gain)
        dg_ref[0:1, :] += _colsum(dh * xn)

    row = lambda i: (i, 0)
    full = lambda i: (0, 0)
    bf = jax.ShapeDtypeStruct((s_len, D_MODEL), BF16)
    return pl.pallas_call(
        body,
        grid=(s_len // tm,),
        in_specs=[
            pl.BlockSpec((tm, D_MODEL), row),
            pl.BlockSpec((tm, PLE_DIM), row),
            pl.BlockSpec((1, D_MODEL), full),
            pl.BlockSpec((D_MODEL, D_MODEL), full),
            pl.BlockSpec((PLE_DIM, D_MODEL), full),
            pl.BlockSpec((tm, D_MODEL), row),
        ],
        out_specs=[
            pl.BlockSpec((tm, D_MODEL), row), pl.BlockSpec((tm, D_MODEL), row),
            pl.BlockSpec((tm, D_MODEL), row), pl.BlockSpec((tm, D_MODEL), row),
            pl.BlockSpec((8, D_MODEL), full), pl.BlockSpec((8, LANES), full),
        ],
        out_shape=[
            jax.ShapeDtypeStruct((s_len, D_MODEL), F32), bf, bf, bf,
            jax.ShapeDtypeStruct((8, D_MODEL), F32), jax.ShapeDtypeStruct((8, LANES), F32),
        ],
        compiler_params=_cparams(48),
        name="ple_loss",
    )(x3, p, g, w_pg, w_pp, target)


def _sb_grads_packed(dqs, dkt4, dvt4, tm=512):
    s_len = dqs.shape[0]

    def body(dq_ref, dkt_ref, dvt_ref, o_ref):
        o_ref[:, 0:ATT_W] = (dq_ref[...] * QK_SCALE).astype(BF16)
        o_ref[:, ATT_W:2 * ATT_W] = _rows_of_transposed(dkt_ref).astype(BF16)
        o_ref[:, 2 * ATT_W:] = _rows_of_transposed(dvt_ref).astype(BF16)

    return pl.pallas_call(
        body,
        grid=(s_len // tm,),
        in_specs=[pl.BlockSpec((tm, ATT_W), lambda i: (i, 0)), _transposed_spec(tm), _transposed_spec(tm)],
        out_specs=pl.BlockSpec((tm, 3 * ATT_W), lambda i: (i, 0)),
        out_shape=jax.ShapeDtypeStruct((s_len, 3 * ATT_W), BF16),
        name="sb_grads_packed",
    )(dqs, dkt4, dvt4)


def _qknorm_bwd(fq, fk, dqs, dkt4, dvt4, qn, kn, bd, bd_t, tm=512):
    s_len = fq.shape[0]

    def body(fq_ref, fk_ref, dq_ref, dkt_ref, dvt_ref, qn_ref, kn_ref, bd_ref, bdt_ref,
             dz_ref, dqn_ref, dkn_ref):
        @pl.when(pl.program_id(0) == 0)
        def _():
            dqn_ref[...] = jnp.zeros_like(dqn_ref)
            dkn_ref[...] = jnp.zeros_like(dkn_ref)

        bd_m = bd_ref[...]
        bdt_m = bdt_ref[...]

        def one(x, dy, gain, dgain_ref):
            xn, rw = _head_rms(x, bd_m, bdt_m)
            dgain_ref[0:1, :] += _colsum(dy * xn)
            dxn = dy * gain
            return rw * (dxn - xn * _head_mean(dxn * xn, bd_m, bdt_m))

        dz_ref[:, 0:ATT_W] = one(fq_ref[...], dq_ref[...] * QK_SCALE, qn_ref[...], dqn_ref).astype(BF16)
        dz_ref[:, ATT_W:2 * ATT_W] = one(fk_ref[...], _rows_of_transposed(dkt_ref), kn_ref[...], dkn_ref).astype(BF16)
        dz_ref[:, 2 * ATT_W:] = _rows_of_transposed(dvt_ref).astype(BF16)

    row = lambda i: (i, 0)
    full = lambda i: (0, 0)
    att = pl.BlockSpec((tm, ATT_W), row)
    return pl.pallas_call(
        body,
        grid=(s_len // tm,),
        in_specs=[att, att, att, _transposed_spec(tm), _transposed_spec(tm),
                  pl.BlockSpec((1, ATT_W), full), pl.BlockSpec((1, ATT_W), full),
                  pl.BlockSpec((ATT_W, LANES), full), pl.BlockSpec((LANES, ATT_W), full)],
        out_specs=[pl.BlockSpec((tm, 3 * ATT_W), row), pl.BlockSpec((8, ATT_W), full), pl.BlockSpec((8, ATT_W), full)],
        out_shape=[jax.ShapeDtypeStruct((s_len, 3 * ATT_W), BF16),
                   jax.ShapeDtypeStruct((8, ATT_W), F32), jax.ShapeDtypeStruct((8, ATT_W), F32)],
        name="qknorm_bwd",
    )(fq, fk, dqs, dkt4, dvt4, qn, kn, bd, bd_t)


def _inproj_bwd(x1, dx2, g, dzf, dlogf, logf, dzs, dgates, w_fox, w_fl, w_sb, w_gates, tm=512):
    s_len = x1.shape[0]

    def body(x_ref, d_ref, g_ref, dzf_ref, dlf_ref, lf_ref, dzs_ref, dgt_ref, wf_ref, wl_ref, ws_ref, wg_ref,
             dx_ref, h_ref, dfl_ref, dg_ref, db_ref):
        @pl.when(pl.program_id(0) == 0)
        def _():
            dg_ref[...] = jnp.zeros_like(dg_ref)
            db_ref[...] = jnp.zeros_like(db_ref)

        xn, r = _rms(x_ref[...])
        gain = g_ref[...]
        h_ref[...] = (xn * gain).astype(BF16)
        lane = lax.broadcasted_iota(jnp.int32, (tm, LANES), 1)
        dfl = jnp.where(lane < N_HEADS, dlf_ref[...] * (1.0 - jnp.exp(lf_ref[...])), 0.0)
        db_ref[0:1, :] += _colsum(dfl)
        dflb = dfl.astype(BF16)
        dfl_ref[...] = dflb
        dh = (_dot_nt(dzf_ref[...], wf_ref[...]) + _dot_nt(dflb, wl_ref[...])
              + _dot_nt(dzs_ref[...], ws_ref[...]) + _dot_nt(dgt_ref[...], wg_ref[...]))
        dx_ref[...] = d_ref[...] + _rms_bwd(dh, xn, r, gain)
        dg_ref[0:1, :] += _colsum(dh * xn)

    row = lambda i: (i, 0)
    full = lambda i: (0, 0)
    return pl.pallas_call(
        body,
        grid=(s_len // tm,),
        in_specs=[
            pl.BlockSpec((tm, D_MODEL), row),
            pl.BlockSpec((tm, D_MODEL), row),
            pl.BlockSpec((1, D_MODEL), full),
            pl.BlockSpec((tm, 3 * ATT_W), row),
            pl.BlockSpec((tm, LANES), row),
            pl.BlockSpec((tm, LANES), row),
            pl.BlockSpec((tm, 3 * ATT_W), row),
            pl.BlockSpec((tm, 2 * D_MODEL), row),
            pl.BlockSpec((D_MODEL, 3 * ATT_W), full),
            pl.BlockSpec((D_MODEL, LANES), full),
            pl.BlockSpec((D_MODEL, 3 * ATT_W), full),
            pl.BlockSpec((D_MODEL, 2 * D_MODEL), full),
        ],
        out_specs=[
            pl.BlockSpec((tm, D_MODEL), row), pl.BlockSpec((tm, D_MODEL), row), pl.BlockSpec((tm, LANES), row),
            pl.BlockSpec((8, D_MODEL), full), pl.BlockSpec((8, LANES), full),
        ],
        out_shape=[
            jax.ShapeDtypeStruct((s_len, D_MODEL), F32), jax.ShapeDtypeStruct((s_len, D_MODEL), BF16),
            jax.ShapeDtypeStruct((s_len, LANES), BF16),
            jax.ShapeDtypeStruct((8, D_MODEL), F32), jax.ShapeDtypeStruct((8, LANES), F32),
        ],
        compiler_params=_cparams(56),
        name="inproj_bwd",
    )(x1, dx2, g, dzf, dlogf, logf, dzs, dgates, w_fox, w_fl, w_sb, w_gates)


def _split_w_in(w_in):
    o = 3 * ATT_W
    w_fox = w_in[:, 0:o]
    w_fl = jnp.pad(w_in[:, o:o + N_HEADS], ((0, 0), (0, LANES - N_HEADS)))
    w_sb = w_in[:, o + N_HEADS:2 * o + N_HEADS]
    w_gates = w_in[:, 2 * o + N_HEADS:]
    return w_fox, w_fl, w_sb, w_gates


def _local_grads(x, p, target, small, full, pending=None, send_early=None):
    blk = ATT_BLOCK
    bd, bd_t = _head_sum_matrices()
    full = dict(full)
    late = list(pending) if pending else []

    x1, a1, b1, u1, *gathered = _ffn_fwd(x, small["ffn1_norm"], full["ffn1_w_gate"], full["ffn1_w_up"],
                                     full["ffn1_w_down"], gather=[pending[k] for k in late])
    for k, gth in zip(late, gathered):
        full[k] = gth if k in KEPT_AS_SHARDS else _whole(k, gth)
    w_fox, w_fl, w_sb, w_gates = _split_w_in(full["w_in"])
    bias = jnp.pad(small["forget_bias"], ((0, 0), (0, LANES - N_HEADS)))
    qn = jnp.tile(small["q_norm"], (1, N_HEADS))
    kn = jnp.tile(small["k_norm"], (1, N_HEADS))
    fq, fk, f_qs, f_k, f_v, logf, s_qs, s_k, s_v, gates, f_k_sq = _inproj_fwd(
        x1, small["mix_norm"], w_fox, w_fl, w_sb, w_gates, bias, qn, kn, bd, bd_t)
    f_cum, fw = _cumsum_rows(logf, reverse=False, spread=bd_t)
    f8 = f_cum[:, 0:N_HEADS]
    ft4 = _pair_rows_t(f8, blk)
    f_k3, f_v3 = _blocked_rows(f_k, blk), _blocked_rows(f_v, blk)
    y_fox, lse, f_first = _fox_fwd(f_qs, f_k3, f_v3, fw, ft4, _key_norm_bound(f_k_sq))
    s_k3, s_v3 = _blocked_rows(s_k, blk), _blocked_rows(s_v, blk)
    y_sb, s_rtot, s_first = _sb_fwd(s_qs, s_k3, s_v3)
    x2 = _merge_fwd(x1, gates, y_fox, y_sb, full["w_branch_fox"], full["w_branch_sb"], full["w_out"])
    x3, a2, b2, u2 = _ffn_fwd(x2, small["ffn2_norm"], full["ffn2_w_gate"], full["ffn2_w_up"], full["ffn2_w_down"])

    dx3, du_ple, dt_ple, hn_ple, dg_ple, loss_sum = _ple_loss(
        x3, p, small["ple_norm"], full["w_ple_gate"], full["w_ple_proj"], target)
    dx2, da2, db2, h_ffn2, d3_bf, dg_ffn2 = _ffn_bwd(
        x2, dx3, small["ffn2_norm"], a2, b2, full["ffn2_w_gate"], full["ffn2_w_up"], full["ffn2_w_down"])
    dy_fox, dy_sb, dgates, d_of, d_os, merged, d2_bf = _merge_bwd(
        dx2, gates, y_fox, y_sb, full["w_branch_fox"], full["w_branch_sb"], full["w_out"])

    f_dqs, dfq_p, f_dkt4, f_dvt4, dft4 = _fox_bwd(f_qs, f_k3, f_v3, dy_fox, y_fox, lse, fw, ft4, f_first)
    s_dqs, s_dkt4, s_dvt4 = _sb_bwd(s_qs, s_k3, s_v3, dy_sb, s_rtot, s_first)

    dzf, dqn8, dkn8 = _qknorm_bwd(fq, fk, f_dqs, f_dkt4, f_dvt4, qn, kn, bd, bd_t)
    dzs = _sb_grads_packed(s_dqs, s_dkt4, s_dvt4)
    df8 = _unpair_rows_t(dft4) + dfq_p[:, :, 0:2].transpose(1, 0, 2).reshape(-1, N_HEADS)
    dlogf = _cumsum_rows(jnp.pad(df8, ((0, 0), (0, LANES - N_HEADS))), reverse=True)
    dx1, h_mix, dfl, dg_mix, dbias8 = _inproj_bwd(
        x1, dx2, small["mix_norm"], dzf, dlogf, logf, dzs, dgates, w_fox, w_fl, w_sb, w_gates)

    one = lambda t: t[None]
    gw = {}
    gw["ffn2_w_gate"] = _wgrad(da2, one(h_ffn2), name="wgrad_ffn2_gate")
    gw["ffn2_w_up"] = _wgrad(db2, one(h_ffn2), name="wgrad_ffn2_up")
    gw["ffn2_w_down"] = _wgrad(u2, one(d3_bf), scale=0.5, name="wgrad_ffn2_down")
    g_fox = _wgrad(one(h_mix), one(dzf), name="wgrad_in_fox")[0]
    g_fl = _wgrad(one(h_mix), one(dfl), name="wgrad_in_forget")[0]
    g_sb = _wgrad(one(h_mix), one(dzs), name="wgrad_in_sb")[0]
    g_gt = _wgrad(one(h_mix), one(dgates), name="wgrad_in_gates")[0]
    gw["w_in"] = jnp.concatenate([g_fox, g_fl[:, 0:N_HEADS], g_sb, g_gt], axis=1)
    gw["w_branch_fox"] = _wgrad(one(y_fox), one(d_of), name="wgrad_branch_fox")[0]
    gw["w_branch_sb"] = _wgrad(one(y_sb), one(d_os), name="wgrad_branch_sb")[0]
    gw["w_out"] = _wgrad(one(merged), one(d2_bf), name="wgrad_out")[0]
    gw["w_ple_gate"] = _wgrad(one(hn_ple), one(du_ple), name="wgrad_ple_gate")[0]
    gw["w_ple_proj"] = _wgrad(one(p), one(dt_ple), name="wgrad_ple_proj")[0]

    gw["ffn1_w_down"] = _wgrad(u1, one(dx1), scale=0.5, name="wgrad_ffn1_down")

    sent_names, to_send = send_early(gw) if send_early else ([], [])
    grad_x, da1, db1, h_ffn1, _, dg_ffn1, *landed = _ffn_bwd(
        x, dx1, small["ffn1_norm"], a1, b1, full["ffn1_w_gate"], full["ffn1_w_up"], full["ffn1_w_down"],
        scatter=to_send)
    gw["ffn1_w_gate"] = _wgrad(da1, one(h_ffn1), name="wgrad_ffn1_gate")
    gw["ffn1_w_up"] = _wgrad(db1, one(h_ffn1), name="wgrad_ffn1_up")

    fold = lambda t: jnp.sum(t[0:1].reshape(N_HEADS, HEAD_DIM), axis=0, keepdims=True)
    gs = {
        "ffn1_norm": dg_ffn1[0:1], "mix_norm": dg_mix[0:1], "ffn2_norm": dg_ffn2[0:1], "ple_norm": dg_ple[0:1],
        "forget_bias": dbias8[0:1, 0:N_HEADS], "q_norm": fold(dqn8), "k_norm": fold(dkn8),
    }
    return loss_sum, grad_x, gw, gs, dict(zip(sent_names, landed))


def _position():
    return lax.axis_index("x"), lax.axis_index("y"), lax.axis_index("c")


def _other_chips(x, y):
    return [(1 - x, y), (x, 1 - y), (1 - x, 1 - y)]


ANY = pl.BlockSpec(memory_space=pl.ANY)


def _place_own_shard(w, q):
    rows, cols = w.shape
    tr = _row_block(rows, cols * 4, budget=2 * MIB)

    def body(q_ref, w_ref, o_ref):
        o_ref[0] = w_ref[...].astype(BF16)

    return pl.pallas_call(
        body,
        grid_spec=pltpu.PrefetchScalarGridSpec(
            num_scalar_prefetch=1,
            grid=(rows // tr,),
            in_specs=[pl.BlockSpec((tr, cols), lambda i, q_ref: (i, 0))],
            out_specs=pl.BlockSpec((1, tr, cols), lambda i, q_ref: (q_ref[0], i, 0)),
        ),
        out_shape=jax.ShapeDtypeStruct((N_CHIPS, rows, cols), BF16),
        name="place_own_shard",
    )(q, w)


def _gather_semaphores(n):
    return [pltpu.SemaphoreType.DMA((6 * n,)), pltpu.SemaphoreType.DMA((6 * n,))]


def _gather_steps(bufs, send_sems, recv_sems):
    n = len(bufs)
    x, y, c = _position()
    q = 2 * x + y
    chips = _other_chips(x, y)
    sibling = (x, y, 1 - c)

    def half(a, slot, which):
        r2 = bufs[a].shape[1] // 2
        return bufs[a].at[slot, pl.ds(which * r2, r2), :]

    def copy(a, k, region, to):
        return pltpu.make_async_remote_copy(
            src_ref=region, dst_ref=region, send_sem=send_sems.at[6 * a + k], recv_sem=recv_sems.at[6 * a + k],
            device_id=to, device_id_type=MESH)

    def to_chip(a, k):
        tx, ty = chips[k]
        return copy(a, k, half(a, q, c), (tx, ty, c))

    def to_sibling(a, k):
        tx, ty = chips[k]
        return copy(a, 3 + k, half(a, 2 * tx + ty, c), sibling)

    def start():
        for a in range(n):
            for k in range(3):
                to_chip(a, k).start()

    def finish():
        for a in range(n):
            for k, (tx, ty) in enumerate(chips):
                copy(a, k, half(a, 2 * tx + ty, c), (tx, ty, c)).wait_recv()
                to_sibling(a, k).start()
        for a in range(n):
            for k, (tx, ty) in enumerate(chips):
                copy(a, 3 + k, half(a, 2 * tx + ty, 1 - c), sibling).wait_recv()
        for a in range(n):
            for k in range(3):
                to_chip(a, k).wait_send()
                to_sibling(a, k).wait_send()

    return start, finish


def _allgather_weights(slots):
    n = len(slots)

    def body(*refs):
        start, finish = _gather_steps(refs[n:2 * n], *refs[2 * n:])
        start()
        finish()

    return pl.pallas_call(
        body,
        in_specs=[ANY] * n,
        out_specs=[ANY] * n,
        out_shape=[jax.ShapeDtypeStruct(s.shape, s.dtype) for s in slots],
        input_output_aliases={a: a for a in range(n)},
        scratch_shapes=_gather_semaphores(n),
        name="allgather_weights",
    )(*slots)


def _exchange_pair_halves(grads):
    n = len(grads)

    def body(*refs):
        ins, outs = refs[0:n], refs[n:2 * n]
        send_sems, recv_sems = refs[2 * n:]
        x, y, c = _position()
        copies = []
        for a in range(n):
            r2 = grads[a].shape[1] // 2
            cp = pltpu.make_async_remote_copy(
                src_ref=ins[a].at[:, pl.ds((1 - c) * r2, r2), :], dst_ref=outs[a],
                send_sem=send_sems.at[a], recv_sem=recv_sems.at[a], device_id=(x, y, 1 - c), device_id_type=MESH)
            cp.start()
            copies.append(cp)
        for cp in copies:
            cp.wait()

    return pl.pallas_call(
        body,
        in_specs=[ANY] * n,
        out_specs=[ANY] * n,
        out_shape=[jax.ShapeDtypeStruct((N_CHIPS, g.shape[1] // 2, g.shape[2]), g.dtype) for g in grads],
        scratch_shapes=[pltpu.SemaphoreType.DMA((n,)), pltpu.SemaphoreType.DMA((n,))],
        name="rs_pair_exchange",
    )(*grads)


def _scatter_semaphores(n):
    return [pltpu.SemaphoreType.DMA((3 * n,)), pltpu.SemaphoreType.DMA((3 * n,)), pltpu.SemaphoreType.DMA((n,))]


def _scatter_steps(ins, outs, send_sems, recv_sems, local_sems):
    n = len(ins)
    x, y, c = _position()
    q = 2 * x + y
    chips = _other_chips(x, y)

    def own(a):
        return pltpu.make_async_copy(ins[a].at[q], outs[a].at[q], local_sems.at[a])

    def to_chip(a, k):
        tx, ty = chips[k]
        return pltpu.make_async_remote_copy(
            src_ref=ins[a].at[2 * tx + ty], dst_ref=outs[a].at[q],
            send_sem=send_sems.at[3 * a + k], recv_sem=recv_sems.at[3 * a + k],
            device_id=(tx, ty, c), device_id_type=MESH)

    def start():
        for a in range(n):
            own(a).start()
            for k in range(3):
                to_chip(a, k).start()

    def finish():
        for a in range(n):
            own(a).wait()
            for k in range(3):
                to_chip(a, k).wait()

    return start, finish


def _scatter_to_owner_chips(pairs):
    n = len(pairs)

    def body(*refs):
        start, finish = _scatter_steps(refs[0:n], refs[n:2 * n], *refs[2 * n:])
        start()
        finish()

    return pl.pallas_call(
        body,
        in_specs=[ANY] * n,
        out_specs=[ANY] * n,
        out_shape=[jax.ShapeDtypeStruct(p.shape, p.dtype) for p in pairs],
        scratch_shapes=_scatter_semaphores(n),
        name="rs_scatter",
    )(*pairs)


def _join_halves(shards):
    n = len(shards)

    def body(*refs):
        bufs = refs[n:2 * n]
        send_sems, recv_sems = refs[2 * n:]
        x, y, c = _position()
        started = []
        for a in range(n):
            r2 = shards[a].shape[0] // 2
            mine = bufs[a].at[pl.ds(c * r2, r2), :]
            cp = pltpu.make_async_remote_copy(
                src_ref=mine, dst_ref=mine, send_sem=send_sems.at[a], recv_sem=recv_sems.at[a],
                device_id=(x, y, 1 - c), device_id_type=MESH)
            cp.start()
            started.append(cp)
        for cp in started:
            cp.wait()

    return pl.pallas_call(
        body,
        in_specs=[ANY] * n,
        out_specs=[ANY] * n,
        out_shape=[jax.ShapeDtypeStruct(t.shape, t.dtype) for t in shards],
        input_output_aliases={a: a for a in range(n)},
        scratch_shapes=[pltpu.SemaphoreType.DMA((n,)), pltpu.SemaphoreType.DMA((n,))],
        name="rs_join_halves",
    )(*shards)


def _add_pair(g, got, c):
    _, r2, cols = got.shape

    def body(c_ref, g_ref, got_ref, o_ref):
        o_ref[...] = (g_ref[...].astype(F32) + got_ref[...].astype(F32)).astype(BF16)

    spec = pl.BlockSpec((1, r2, cols), lambda s, c_ref: (s, 0, 0))
    return pl.pallas_call(
        body,
        grid_spec=pltpu.PrefetchScalarGridSpec(
            num_scalar_prefetch=1,
            grid=(N_CHIPS,),
            in_specs=[pl.BlockSpec((1, r2, cols), lambda s, c_ref: (s, c_ref[0], 0)), spec],
            out_specs=spec,
        ),
        out_shape=jax.ShapeDtypeStruct(got.shape, BF16),
        name="rs_add_pair",
    )(c, g, got)


def _add_chips(parts, c):
    _, r2, cols = parts.shape

    def body(c_ref, p0, p1, p2, p3, o_ref):
        o_ref[...] = ((p0[0].astype(F32) + p1[0].astype(F32)) + p2[0].astype(F32)) + p3[0].astype(F32)

    specs = [pl.BlockSpec((1, r2, cols), functools.partial(lambda i, c_ref, s: (s, 0, 0), s=s))
             for s in range(N_CHIPS)]
    return pl.pallas_call(
        body,
        grid_spec=pltpu.PrefetchScalarGridSpec(
            num_scalar_prefetch=1,
            grid=(1,),
            in_specs=specs,
            out_specs=pl.BlockSpec((r2, cols), lambda i, c_ref: (c_ref[0], 0)),
        ),
        out_shape=jax.ShapeDtypeStruct((2 * r2, cols), F32),
        name="rs_add_chips",
    )(c, parts, parts, parts, parts)


def _allreduce_small(part):
    shape = part.shape

    def body(in_ref, out_ref, gather_ref, send_sems, recv_sems):
        x, y, c = _position()
        me = 4 * x + 2 * y + c
        relations = [(a, b, d) for a in (0, 1) for b in (0, 1) for d in (0, 1)][1:]
        flip = lambda v, f: 1 - v if f else v
        copies = []
        for k, (a, b, d) in enumerate(relations):
            cp = pltpu.make_async_remote_copy(
                src_ref=in_ref, dst_ref=gather_ref.at[me], send_sem=send_sems.at[k], recv_sem=recv_sems.at[k],
                device_id=(flip(x, a), flip(y, b), flip(c, d)), device_id_type=MESH)
            cp.start()
            copies.append(cp)
        gather_ref[me] = in_ref[...]
        for cp in copies:
            cp.wait()
        total = gather_ref[0]
        for dev in range(1, 8):
            total = total + gather_ref[dev]
        out_ref[...] = total

    vmem = pl.BlockSpec(memory_space=pltpu.VMEM)
    return pl.pallas_call(
        body,
        in_specs=[vmem],
        out_specs=vmem,
        out_shape=jax.ShapeDtypeStruct(shape, F32),
        scratch_shapes=[pltpu.VMEM((8,) + shape, F32), pltpu.SemaphoreType.DMA((7,)), pltpu.SemaphoreType.DMA((7,))],
        name="allreduce_small",
    )(part)


def _adamw(w, g, m, v):
    rows, cols = w.shape
    tr = _row_block(rows, cols * 4, budget=MIB)
    c1 = 1.0 / (1.0 - ADAM_B1 ** ADAM_STEP)
    c2 = 1.0 / (1.0 - ADAM_B2 ** ADAM_STEP)

    def body(w_ref, g_ref, m_ref, v_ref, d_ref, nm_ref, nv_ref):
        g_ = g_ref[...]
        nm = ADAM_B1 * m_ref[...] + (1.0 - ADAM_B1) * g_
        nv = ADAM_B2 * v_ref[...] + (1.0 - ADAM_B2) * (g_ * g_)
        nm_ref[...] = nm
        nv_ref[...] = nv
        d_ref[...] = -ADAM_LR * ((nm * c1) / (jnp.sqrt(nv * c2) + ADAM_EPS) + ADAM_WD * w_ref[...])

    spec = pl.BlockSpec((tr, cols), lambda i: (i, 0))
    out = jax.ShapeDtypeStruct((rows, cols), F32)
    return pl.pallas_call(
        body,
        grid=(rows // tr,),
        in_specs=[spec] * 4,
        out_specs=[spec] * 3,
        out_shape=[out] * 3,
        name="adamw",
    )(w, g, m, v)


BIG = ["ffn1_w_gate", "ffn1_w_up", "ffn1_w_down", "w_in", "w_branch_fox", "w_branch_sb", "w_out",
       "ffn2_w_gate", "ffn2_w_up", "ffn2_w_down", "w_ple_gate", "w_ple_proj"]
SMALL = ["ffn1_norm", "mix_norm", "ffn2_norm", "ple_norm", "forget_bias", "q_norm", "k_norm"]
COLUMN_SHARDED = ["w_in", "w_branch_fox", "w_branch_sb", "w_ple_proj"]
KEPT_AS_SHARDS = ["ffn1_w_gate", "ffn1_w_up", "ffn1_w_down", "ffn2_w_gate", "ffn2_w_up", "ffn2_w_down"]
WORKED_TRANSPOSED = ["ffn1_w_gate", "ffn1_w_up", "ffn2_w_gate", "ffn2_w_up"]
NEEDED_FIRST = ["ffn1_w_gate", "ffn1_w_up", "ffn1_w_down"]
READY_LAST = ["ffn1_w_gate", "ffn1_w_up"]
ORDER = ["ffn1_norm", "ffn1_w_gate", "ffn1_w_up", "ffn1_w_down", "mix_norm", "w_in", "forget_bias", "q_norm",
         "k_norm", "w_branch_fox", "w_branch_sb", "w_out", "ffn2_norm", "ffn2_w_gate", "ffn2_w_up",
         "ffn2_w_down", "ple_norm", "w_ple_gate", "w_ple_proj"]
SMALL_ROWS = {"ffn1_norm": 0, "mix_norm": 1, "ffn2_norm": 2, "ple_norm": 3}
SMALL_COLS = {"forget_bias": (0, N_HEADS), "q_norm": (N_HEADS, HEAD_DIM), "k_norm": (N_HEADS + HEAD_DIM, HEAD_DIM)}
LOSS_ROW = 5


def _stored(name, a):
    return jnp.swapaxes(a[0], 0, 1) if name in WORKED_TRANSPOSED else a[0]


def _returned(name, t):
    return (jnp.swapaxes(t, 0, 1) if name in WORKED_TRANSPOSED else t)[None]


def _whole(name, gathered):
    if name in COLUMN_SHARDED:
        return jnp.concatenate([gathered[s] for s in range(N_CHIPS)], axis=1)
    return gathered.reshape(-1, gathered.shape[-1])


def _as_shards(name, whole):
    if name in COLUMN_SHARDED:
        k, n = whole.shape
        return whole.reshape(k, N_CHIPS, n // N_CHIPS).transpose(1, 0, 2)
    return whole.reshape(N_CHIPS, whole.shape[0] // N_CHIPS, whole.shape[1])


def _pack_small(values, extra=None):
    rows = [values[k] for k in ("ffn1_norm", "mix_norm", "ffn2_norm", "ple_norm")]
    tail = jnp.concatenate([values["forget_bias"], values["q_norm"], values["k_norm"]], axis=1)
    rows.append(jnp.pad(tail, ((0, 0), (0, D_MODEL - tail.shape[1]))))
    packed = jnp.concatenate(rows + [jnp.zeros((3, D_MODEL), F32)], axis=0)
    if extra is not None:
        packed = packed.at[LOSS_ROW, 0].set(extra)
    return packed


def _unpack_small(packed):
    out = {k: packed[r:r + 1] for k, r in SMALL_ROWS.items()}
    for k, (start, size) in SMALL_COLS.items():
        out[k] = packed[4:5, start:start + size]
    return out


def kernel(x, p, ffn1_norm, ffn1_w_gate, ffn1_w_up, ffn1_w_down, mix_norm, w_in, forget_bias, q_norm, k_norm, w_branch_fox, w_branch_sb, w_out, ffn2_norm, ffn2_w_gate, ffn2_w_up, ffn2_w_down, ple_norm, w_ple_gate, w_ple_proj, loss_target, m_ffn1_norm, m_ffn1_w_gate, m_ffn1_w_up, m_ffn1_w_down, m_mix_norm, m_w_in, m_forget_bias, m_q_norm, m_k_norm, m_w_branch_fox, m_w_branch_sb, m_w_out, m_ffn2_norm, m_ffn2_w_gate, m_ffn2_w_up, m_ffn2_w_down, m_ple_norm, m_w_ple_gate, m_w_ple_proj, v_ffn1_norm, v_ffn1_w_gate, v_ffn1_w_up, v_ffn1_w_down, v_mix_norm, v_w_in, v_forget_bias, v_q_norm, v_k_norm, v_w_branch_fox, v_w_branch_sb, v_w_out, v_ffn2_norm, v_ffn2_w_gate, v_ffn2_w_up, v_ffn2_w_down, v_ple_norm, v_w_ple_gate, v_w_ple_proj):
    args = dict(locals())
    weights = {k: args[k] for k in ORDER}
    moments_m = {k: args["m_" + k] for k in ORDER}
    moments_v = {k: args["v_" + k] for k in ORDER}

    c_idx = lax.axis_index("c").astype(jnp.int32).reshape(1)
    q_idx = (2 * lax.axis_index("x") + lax.axis_index("y")).astype(jnp.int32).reshape(1)
    own = {k: _place_own_shard(_stored(k, weights[k]), q_idx) for k in BIG}
    full = dict(zip(NEEDED_FIRST, _allgather_weights([own[k] for k in NEEDED_FIRST])))
    pending = {k: own[k] for k in BIG if k not in NEEDED_FIRST}
    small = {k: weights[k] for k in SMALL}

    def pair_sums(names, gw):
        slots = [gw[k] if k in KEPT_AS_SHARDS else _as_shards(k, gw[k]) for k in names]
        from_core = _exchange_pair_halves(slots)
        return [_add_pair(g, got, c_idx) for g, got in zip(slots, from_core)]

    early = [k for k in BIG if k not in READY_LAST]
    loss_sum, grad_x, gw, gs, parts = _local_grads(
        x[0], p[0, 0], loss_target[0], small, full, pending, lambda ready: (early, pair_sums(early, ready)))

    parts.update(zip(READY_LAST, _scatter_to_owner_chips(pair_sums(READY_LAST, gw))))
    grads_big = dict(zip(BIG, _join_halves([_add_chips(parts[k], c_idx) for k in BIG])))
    reduced = _allreduce_small(_pack_small(gs, extra=loss_sum[0, 0]))
    grads_small = _unpack_small(reduced)
    loss = reduced[LOSS_ROW, 0]

    grads, deltas, new_m, new_v = {}, {}, {}, {}
    for k in BIG:
        d, nm, nv = _adamw(_stored(k, weights[k]), grads_big[k], _stored(k, moments_m[k]), _stored(k, moments_v[k]))
        grads[k], deltas[k], new_m[k], new_v[k] = (_returned(k, t) for t in (grads_big[k], d, nm, nv))
    d_s, nm_s, nv_s = _adamw(_pack_small({k: weights[k] for k in SMALL}), reduced,
                             _pack_small({k: moments_m[k] for k in SMALL}),
                             _pack_small({k: moments_v[k] for k in SMALL}))
    for k in SMALL:
        grads[k] = grads_small[k]
    for name, packed in (("d", d_s), ("m", nm_s), ("v", nv_s)):
        target = {"d": deltas, "m": new_m, "v": new_v}[name]
        target.update(_unpack_small(packed))

    return (loss, grad_x[None], *[grads[k] for k in ORDER], *[deltas[k] for k in ORDER],
            *[new_m[k] for k in ORDER], *[new_v[k] for k in ORDER])
```

```python
import functools

import jax
import jax.numpy as jnp
from jax import lax
from jax.experimental import pallas as pl
from jax.experimental.pallas import tpu as pltpu

F32 = jnp.float32
BF16 = jnp.bfloat16

D_MODEL = 1024
D_FF = 2816
N_CHIPS = 4
FF_SHARD = D_FF // N_CHIPS
FFN_CHUNKS = 2
WGRAD_TOKENS = 4096
WGRAD_VMEM = 30 * 1024 * 1024
HEAD_DIM = 64
N_HEADS = 8
ATT_W = N_HEADS * HEAD_DIM
PAIR_W = 2 * HEAD_DIM
N_PAIRS = N_HEADS // 2
PLE_DIM = 256
IN_WIDTH = 3 * ATT_W + N_HEADS + 3 * ATT_W + 2 * D_MODEL
EPS = 1e-6
QK_SCALE = HEAD_DIM ** -0.5
LANES = 128
ATT_BLOCK = 256
FOX_Q_BLOCK = 512
SB_Q_BLOCK = 256
NEG_BIG = -1e30
EXP_UNDERFLOW = 110.0
MAX_REFERENCE_EXCESS = 40.0
NORM_BOUND_MARGIN = 1.02

ADAM_LR = 0.001
ADAM_B1 = 0.9
ADAM_B2 = 0.999
ADAM_EPS = 1e-08
ADAM_WD = 0.01
ADAM_STEP = 10

MESH = pl.DeviceIdType.MESH
MIB = 1024 * 1024


def _cparams(vmem_mib=48):
    return pltpu.CompilerParams(vmem_limit_bytes=vmem_mib * MIB)


def _dot(a, b):
    return jnp.dot(a, b, preferred_element_type=F32)


def _dot_tn(a, b):
    return lax.dot_general(a, b, (((0,), (0,)), ((), ())), preferred_element_type=F32)


def _dot_nt(a, b):
    return lax.dot_general(a, b, (((1,), (1,)), ((), ())), preferred_element_type=F32)


def _sigmoid(x):
    return 1.0 / (1.0 + jnp.exp(-x))


def _split2(x):
    hi = x.astype(BF16)
    lo = (x - hi.astype(F32)).astype(BF16)
    return hi, lo


def _dot_split2(x, m):
    hi, lo = _split2(x)
    return _dot(hi, m) + _dot(lo, m)


def _split3(x):
    hi = x.astype(BF16)
    rest = x - hi.astype(F32)
    mid = rest.astype(BF16)
    lo = (rest - mid.astype(F32)).astype(BF16)
    return hi, mid, lo


def _rms(x):
    r = lax.rsqrt(jnp.mean(x * x, axis=-1, keepdims=True) + EPS)
    return x * r, r


def _rms_bwd(dh, xn, r, g):
    dxn = dh * g
    return r * (dxn - xn * jnp.mean(dxn * xn, axis=-1, keepdims=True))


def _colsum(x):
    return jnp.sum(x, axis=0, keepdims=True)


def _row_block(rows, row_bytes, budget):
    best = None
    for t in range(8, rows + 1, 8):
        if rows % t == 0 and t * row_bytes <= budget:
            best = t
    return best if best is not None else rows


def _ffn_fwd(x, g, wg, wu, wd, gather=(), tm=1024):
    s_len = x.shape[0]
    n = len(gather)
    steps = s_len // tm

    def body(x_ref, g_ref, wg_ref, wu_ref, wd_ref, *rest):
        o_ref, a_ref, b_ref, u_ref = rest[n:n + 4]
        h_s, acc_s = rest[2 * n + 4:2 * n + 6]
        i = pl.program_id(0)
        j = pl.program_id(1)
        if n:
            start, finish = _gather_steps(rest[n + 4:2 * n + 4], *rest[2 * n + 6:])
            pl.when((i == 0) & (j == 0))(start)

        @pl.when(j == 0)
        def _():
            xn, _ = _rms(x_ref[...])
            h_s[...] = (xn * g_ref[...]).astype(BF16)
            acc_s[...] = jnp.zeros_like(acc_s)

        chunks = [pl.ds(r * (tm // FFN_CHUNKS), tm // FFN_CHUNKS) for r in range(FFN_CHUNKS)]
        pre = [(_dot_nt(h_s[rows, :], wg_ref[0]), _dot_nt(h_s[rows, :], wu_ref[0])) for rows in chunks]
        us = []
        for rows, (a, b) in zip(chunks, pre):
            a_ref[0, rows, :] = a.astype(BF16)
            b_ref[0, rows, :] = b.astype(BF16)
            u = (a * _sigmoid(a) * b).astype(BF16)
            u_ref[0, rows, :] = u
            us.append(u)
        for rows, u in zip(chunks, us):
            acc_s[rows, :] += _dot(u, wd_ref[0])

        @pl.when(j == N_CHIPS - 1)
        def _():
            o_ref[...] = x_ref[...] + 0.5 * acc_s[...]

        if n:
            pl.when((i == steps - 1) & (j == N_CHIPS - 1))(finish)

    return pl.pallas_call(
        body,
        grid=(steps, N_CHIPS),
        in_specs=[
            pl.BlockSpec((tm, D_MODEL), lambda i, j: (i, 0)),
            pl.BlockSpec((1, D_MODEL), lambda i, j: (0, 0)),
            pl.BlockSpec((1, FF_SHARD, D_MODEL), lambda i, j: (j, 0, 0)),
            pl.BlockSpec((1, FF_SHARD, D_MODEL), lambda i, j: (j, 0, 0)),
            pl.BlockSpec((1, FF_SHARD, D_MODEL), lambda i, j: (j, 0, 0)),
        ] + [ANY] * n,
        out_specs=[pl.BlockSpec((tm, D_MODEL), lambda i, j: (i, 0))]
        + [pl.BlockSpec((1, tm, FF_SHARD), lambda i, j: (j, i, 0))] * 3 + [ANY] * n,
        out_shape=[jax.ShapeDtypeStruct((s_len, D_MODEL), F32)]
        + [jax.ShapeDtypeStruct((N_CHIPS, s_len, FF_SHARD), BF16)] * 3
        + [jax.ShapeDtypeStruct(s.shape, s.dtype) for s in gather],
        input_output_aliases={5 + a: 4 + a for a in range(n)},
        scratch_shapes=[pltpu.VMEM((tm, D_MODEL), BF16), pltpu.VMEM((tm, D_MODEL), F32)]
        + (_gather_semaphores(n) if n else []),
        compiler_params=_cparams(56),
        name="ffn_fwd_gathering" if n else "ffn_fwd",
    )(x, g, wg, wu, wd, *gather)


def _ffn_bwd(x, d, g, a_pre, b_pre, wg, wu, wd, scatter=(), tm=512):
    s_len = x.shape[0]
    nb = s_len // tm
    n = len(scatter)

    def body(x_ref, d_ref, g_ref, a_ref, b_ref, wg_ref, wu_ref, wd_ref, *rest):
        dx_ref, da_ref, db_ref, h_ref, dbf_ref, dg_ref = rest[n:n + 6]
        dbf_s, dh_s = rest[2 * n + 6:2 * n + 8]
        i = pl.program_id(0)
        j = pl.program_id(1)
        if n:
            start, finish = _scatter_steps(rest[0:n], rest[n + 6:2 * n + 6], *rest[2 * n + 8:])
            pl.when((i == 0) & (j == 0))(start)

        @pl.when(j == 0)
        def _():
            xn, _ = _rms(x_ref[...])
            h_ref[...] = (xn * g_ref[...]).astype(BF16)
            dbf = d_ref[...].astype(BF16)
            dbf_s[...] = dbf
            dbf_ref[...] = dbf
            dh_s[...] = jnp.zeros_like(dh_s)

        @pl.when((i == 0) & (j == 0))
        def _():
            dg_ref[...] = jnp.zeros_like(dg_ref)

        chunks = [pl.ds(r * (tm // FFN_CHUNKS), tm // FFN_CHUNKS) for r in range(FFN_CHUNKS)]
        dus = [0.5 * _dot_nt(dbf_s[rows, :], wd_ref[0]) for rows in chunks]
        das, dbs = [], []
        for rows, du in zip(chunks, dus):
            a = a_ref[0, rows, :].astype(F32)
            b = b_ref[0, rows, :].astype(F32)
            s = _sigmoid(a)
            silu = a * s
            da = (du * b * (s * (1.0 + a * (1.0 - s)))).astype(BF16)
            db = (du * silu).astype(BF16)
            da_ref[0, rows, :] = da
            db_ref[0, rows, :] = db
            das.append(da)
            dbs.append(db)
        for rows, da, db in zip(chunks, das, dbs):
            dh_s[rows, :] += _dot(da, wg_ref[0]) + _dot(db, wu_ref[0])

        @pl.when(j == N_CHIPS - 1)
        def _():
            xn, r = _rms(x_ref[...])
            dh = dh_s[...]
            dx_ref[...] = d_ref[...] + _rms_bwd(dh, xn, r, g_ref[...])
            dg_ref[0:1, :] += _colsum(dh * xn)

        if n:
            pl.when((i == nb - 1) & (j == N_CHIPS - 1))(finish)

    row = lambda i, j: (i, 0)
    shard = lambda i, j: (j, 0, 0)
    act = lambda i, j: (j, i, 0)
    return pl.pallas_call(
        body,
        grid=(nb, N_CHIPS),
        in_specs=[
            pl.BlockSpec((tm, D_MODEL), row),
            pl.BlockSpec((tm, D_MODEL), row),
            pl.BlockSpec((1, D_MODEL), lambda i, j: (0, 0)),
            pl.BlockSpec((1, tm, FF_SHARD), act),
            pl.BlockSpec((1, tm, FF_SHARD), act),
            pl.BlockSpec((1, FF_SHARD, D_MODEL), shard),
            pl.BlockSpec((1, FF_SHARD, D_MODEL), shard),
            pl.BlockSpec((1, FF_SHARD, D_MODEL), shard),
        ] + [ANY] * n,
        out_specs=[
            pl.BlockSpec((tm, D_MODEL), row),
            pl.BlockSpec((1, tm, FF_SHARD), act),
            pl.BlockSpec((1, tm, FF_SHARD), act),
            pl.BlockSpec((tm, D_MODEL), row),
            pl.BlockSpec((tm, D_MODEL), row),
            pl.BlockSpec((8, D_MODEL), lambda i, j: (0, 0)),
        ] + [ANY] * n,
        out_shape=[
            jax.ShapeDtypeStruct((s_len, D_MODEL), F32),
            jax.ShapeDtypeStruct((N_CHIPS, s_len, FF_SHARD), BF16),
            jax.ShapeDtypeStruct((N_CHIPS, s_len, FF_SHARD), BF16),
            jax.ShapeDtypeStruct((s_len, D_MODEL), BF16),
            jax.ShapeDtypeStruct((s_len, D_MODEL), BF16),
            jax.ShapeDtypeStruct((8, D_MODEL), F32),
        ] + [jax.ShapeDtypeStruct(s.shape, s.dtype) for s in scatter],
        scratch_shapes=[
            pltpu.VMEM((tm, D_MODEL), BF16),
            pltpu.VMEM((tm, D_MODEL), F32),
        ] + (_scatter_semaphores(n) if n else []),
        compiler_params=_cparams(56),
        name="ffn_bwd_scattering" if n else "ffn_bwd",
    )(x, d, g, a_pre, b_pre, wg, wu, wd, *scatter)


def _wgrad(a, b, scale=1.0, name="wgrad"):
    na, s_len, k_dim = a.shape
    nb, _, n_dim = b.shape
    n = max(na, nb)
    ts = WGRAD_TOKENS
    while ts > 512 and (ts > s_len or 2 * ts * (k_dim * a.dtype.itemsize + n_dim * b.dtype.itemsize) > WGRAD_VMEM):
        ts //= 2
    steps = s_len // ts

    def body(a_ref, b_ref, o_ref, acc_s):
        s = pl.program_id(1)

        @pl.when(s == 0)
        def _():
            acc_s[...] = jnp.zeros_like(acc_s)

        acc_s[...] += _dot_tn(a_ref[0].astype(BF16), b_ref[0].astype(BF16))

        @pl.when(s == steps - 1)
        def _():
            o_ref[0] = (acc_s[...] * scale).astype(BF16)

    a_map = (lambda m, s: (m, s, 0)) if na > 1 else (lambda m, s: (0, s, 0))
    b_map = (lambda m, s: (m, s, 0)) if nb > 1 else (lambda m, s: (0, s, 0))
    return pl.pallas_call(
        body,
        grid=(n, steps),
        in_specs=[pl.BlockSpec((1, ts, k_dim), a_map), pl.BlockSpec((1, ts, n_dim), b_map)],
        out_specs=pl.BlockSpec((1, k_dim, n_dim), lambda m, s: (m, 0, 0)),
        out_shape=jax.ShapeDtypeStruct((n, k_dim, n_dim), BF16),
        scratch_shapes=[pltpu.VMEM((k_dim, n_dim), F32)],
        compiler_params=_cparams(56),
        name=name,
    )(a, b)


def _head_sum_matrices():
    lane = lax.broadcasted_iota(jnp.int32, (ATT_W, LANES), 0) // HEAD_DIM
    col = lax.broadcasted_iota(jnp.int32, (ATT_W, LANES), 1)
    bd = (lane == col).astype(BF16)
    return bd, bd.T


def _head_mean(t, bd, bd_t):
    per_head = _dot_split2(t, bd) * (1.0 / HEAD_DIM)
    return _dot_split2(per_head, bd_t)


def _head_rms(x, bd, bd_t):
    per_head = _dot_split2(x * x, bd) * (1.0 / HEAD_DIM)
    r = lax.rsqrt(per_head + EPS)
    rw = _dot_split2(r, bd_t)
    return x * rw, rw


def _log_sigmoid(z):
    return jnp.minimum(z, 0.0) - jnp.log(1.0 + jnp.exp(-jnp.abs(z)))


def _inproj_fwd(x1, g, w_fox, w_fl, w_sb, w_gates, bias, qn, kn, bd, bd_t, tm=512):
    s_len = x1.shape[0]

    def body(x_ref, g_ref, wf_ref, wl_ref, ws_ref, wg_ref, bias_ref, qn_ref, kn_ref, bd_ref, bdt_ref,
             fq_ref, fk_ref, qs_ref, kf_ref, vf_ref, logf_ref, sq_ref, sk_ref, sv_ref, gates_ref):
        xn, _ = _rms(x_ref[...])
        h = (xn * g_ref[...]).astype(BF16)
        zf = _dot(h, wf_ref[...])
        fq = zf[:, 0:ATT_W]
        fk = zf[:, ATT_W:2 * ATT_W]
        fq_ref[...] = fq
        fk_ref[...] = fk
        bd_m = bd_ref[...]
        bdt_m = bdt_ref[...]
        fqn, _ = _head_rms(fq, bd_m, bdt_m)
        fkn, _ = _head_rms(fk, bd_m, bdt_m)
        qs_ref[...] = (fqn * qn_ref[...]).astype(BF16) * QK_SCALE
        kf_ref[...] = (fkn * kn_ref[...]).astype(BF16)
        vf_ref[...] = zf[:, 2 * ATT_W:3 * ATT_W].astype(BF16)
        logf_ref[...] = _log_sigmoid(_dot(h, wl_ref[...]) + bias_ref[...])
        zs = _dot(h, ws_ref[...])
        sq_ref[...] = zs[:, 0:ATT_W].astype(BF16) * QK_SCALE
        sk_ref[...] = zs[:, ATT_W:2 * ATT_W].astype(BF16)
        sv_ref[...] = zs[:, 2 * ATT_W:3 * ATT_W].astype(BF16)
        gates_ref[...] = _dot(h, wg_ref[...]).astype(BF16)

    row = lambda i: (i, 0)
    full = lambda i: (0, 0)
    att = lambda dt: jax.ShapeDtypeStruct((s_len, ATT_W), dt)
    return pl.pallas_call(
        body,
        grid=(s_len // tm,),
        in_specs=[
            pl.BlockSpec((tm, D_MODEL), row),
            pl.BlockSpec((1, D_MODEL), full),
            pl.BlockSpec((D_MODEL, 3 * ATT_W), full),
            pl.BlockSpec((D_MODEL, LANES), full),
            pl.BlockSpec((D_MODEL, 3 * ATT_W), full),
            pl.BlockSpec((D_MODEL, 2 * D_MODEL), full),
            pl.BlockSpec((1, LANES), full),
            pl.BlockSpec((1, ATT_W), full),
            pl.BlockSpec((1, ATT_W), full),
            pl.BlockSpec((ATT_W, LANES), full),
            pl.BlockSpec((LANES, ATT_W), full),
        ],
        out_specs=[
            pl.BlockSpec((tm, ATT_W), row), pl.BlockSpec((tm, ATT_W), row),
            pl.BlockSpec((tm, ATT_W), row), pl.BlockSpec((tm, ATT_W), row), pl.BlockSpec((tm, ATT_W), row),
            pl.BlockSpec((tm, LANES), row),
            pl.BlockSpec((tm, ATT_W), row), pl.BlockSpec((tm, ATT_W), row), pl.BlockSpec((tm, ATT_W), row),
            pl.BlockSpec((tm, 2 * D_MODEL), row),
        ],
        out_shape=[
            att(F32), att(F32), att(BF16), att(BF16), att(BF16),
            jax.ShapeDtypeStruct((s_len, LANES), F32),
            att(BF16), att(BF16), att(BF16),
            jax.ShapeDtypeStruct((s_len, 2 * D_MODEL), BF16),
        ],
        compiler_params=_cparams(56),
        name="inproj_fwd",
    )(x1, g, w_fox, w_fl, w_sb, w_gates, bias, qn, kn, bd, bd_t)


def _tri(n, kind):
    r = lax.broadcasted_iota(jnp.int32, (n, n), 0)
    c = lax.broadcasted_iota(jnp.int32, (n, n), 1)
    m = {"row_ge_col": r >= c, "row_le_col": r <= c, "row_gt_col": r > c, "row_lt_col": r < c}[kind]
    return m.astype(BF16)


def _cumsum_rows(x, reverse, spread=None, tm=256):
    s_len = x.shape[0]
    nb = s_len // tm
    tri = _tri(tm, "row_le_col" if reverse else "row_ge_col")
    edge = 0 if reverse else tm - 1

    def body(x_ref, tri_ref, *rest):
        spread_ref, o_ref, wide_ref, carry_s = rest if spread is not None else (None, rest[0], None, rest[1])

        @pl.when(pl.program_id(0) == 0)
        def _():
            carry_s[...] = jnp.zeros_like(carry_s)

        hi, mid, lo = _split3(x_ref[...])
        t = tri_ref[...]
        y = _dot(t, hi) + _dot(t, mid) + _dot(t, lo) + carry_s[...]
        o_ref[...] = y
        carry_s[...] = y[edge:edge + 1, :]
        if spread is not None:
            hi, mid, lo = _split3(y)
            m = spread_ref[...]
            wide_ref[...] = _dot(hi, m) + _dot(mid, m) + _dot(lo, m)

    order = (lambda i: (nb - 1 - i, 0)) if reverse else (lambda i: (i, 0))
    narrow = pl.BlockSpec((tm, LANES), order)
    in_specs, out_specs = [narrow, pl.BlockSpec((tm, tm), lambda i: (0, 0))], [narrow]
    out_shape = [jax.ShapeDtypeStruct((s_len, LANES), F32)]
    if spread is not None:
        width = spread.shape[1]
        in_specs.append(pl.BlockSpec((LANES, width), lambda i: (0, 0)))
        out_specs.append(pl.BlockSpec((tm, width), order))
        out_shape.append(jax.ShapeDtypeStruct((s_len, width), F32))
    out = pl.pallas_call(
        body,
        grid=(nb,),
        in_specs=in_specs,
        out_specs=out_specs,
        out_shape=out_shape,
        scratch_shapes=[pltpu.VMEM((1, LANES), F32)],
        name="cumsum_rev" if reverse else "cumsum_fwd",
    )(*([x, tri] if spread is None else [x, tri, spread]))
    return out[0] if spread is None else out


def _unblocked_t(t4):
    _, nb, _, blk = t4.shape
    return t4.transpose(1, 3, 0, 2).reshape(nb * blk, ATT_W)


def _transposed_spec(tm):
    return pl.BlockSpec((N_PAIRS, tm // ATT_BLOCK, PAIR_W, ATT_BLOCK), lambda i: (0, i, 0, 0))


def _rows_of_transposed(ref):
    return jnp.concatenate(
        [jnp.concatenate([ref[p, b].T for p in range(N_PAIRS)], axis=1) for b in range(ref.shape[1])], axis=0)


def _blocked_rows(t, blk):
    return t.reshape(t.shape[0] // blk, blk, t.shape[1])


def _pair_rows_t(f8, blk):
    nb = f8.shape[0] // blk
    t = f8.reshape(nb, blk, N_PAIRS, 2).transpose(2, 0, 3, 1)
    return jnp.pad(t, ((0, 0), (0, 0), (0, 6), (0, 0)))


def _unpair_rows_t(t4):
    _, nb, _, blk = t4.shape
    return t4[:, :, 0:2, :].transpose(1, 3, 0, 2).reshape(nb * blk, N_HEADS)


def _head_masks(tq):
    lane = lax.broadcasted_iota(jnp.int32, (tq, PAIR_W), 1)
    return lane < HEAD_DIM


def _causal_mask(tq, tk, offset, strict):
    d = lax.broadcasted_iota(jnp.int32, (tq, tk), 1) - lax.broadcasted_iota(jnp.int32, (tq, tk), 0)
    return (d < offset) if strict else (d <= offset)


def _heads_of(ref, first):
    t = ref[...]
    zero = jnp.zeros_like(t)
    return [jnp.where(first, t, zero), jnp.where(first, zero, t)]


def _head_cols(ref):
    t = ref[...]
    return [t[:, 0:1], t[:, HEAD_DIM:HEAD_DIM + 1]]


def _att_specs(s_len, tq):
    tk = ATT_BLOCK
    nq, nk = s_len // tq, s_len // tk
    return dict(
        nq=nq,
        q=pl.BlockSpec((tq, PAIR_W), lambda p, i: (i, p)),
        k_t=pl.BlockSpec((1, nk, PAIR_W, tk), lambda p, i: (p, 0, 0, 0)),
        k_rows=pl.BlockSpec((nk, tk, PAIR_W), lambda p, i: (0, 0, p)),
        f_t=pl.BlockSpec((1, nk, 8, tk), lambda p, i: (p, 0, 0, 0)),
        first=pl.BlockSpec((1, 1, 8, LANES), lambda p, i: (p, i, 0, 0)),
        wide=jax.ShapeDtypeStruct((s_len, ATT_W), F32),
        k_t_out=jax.ShapeDtypeStruct((N_PAIRS, nk, PAIR_W, tk), F32),
        f_t_out=jax.ShapeDtypeStruct((N_PAIRS, nk, 8, tk), F32),
        first_out=jax.ShapeDtypeStruct((N_PAIRS, nq, 8, LANES), F32),
        acc=pltpu.VMEM((2, tq, PAIR_W), F32),
    )


def _first_block(first_ref, limit):
    return jnp.clip(jnp.max(first_ref[0, 0]).astype(jnp.int32), 0, limit)


def _normed_dot_bound(qn, kn):
    stretch = jnp.max(jnp.abs(qn)) * jnp.max(jnp.abs(kn))
    return (stretch * (HEAD_DIM * QK_SCALE * NORM_BOUND_MARGIN)).reshape(1)


def _fox_fwd(qs, k3, v3, fw, ft4, dot_bound):
    tq, tk = FOX_Q_BLOCK, ATT_BLOCK
    sp = _att_specs(qs.shape[0], tq)
    ratio, nk = tq // tk, qs.shape[0] // tk
    scalars = jnp.concatenate([ft4[:, :, :2, tk - 1].reshape(-1), dot_bound])

    def body(fend_ref, q_ref, k_ref, v_ref, fw_ref, ft_ref, y_ref, lse_ref, first_ref,
             acc_ref, max_ref, sum_ref):
        pair, i = pl.program_id(0), pl.program_id(1)
        first = _head_masks(tq)
        qh = _heads_of(q_ref, first)
        fqh = _head_cols(fw_ref)
        reach = [fqh[n] + fend_ref[N_PAIRS * nk * 2] for n in range(2)]

        def logits(j, shift, r0=0, diag=False):
            k, fk = k_ref[j], ft_ref[0, j]
            raw = [_dot_nt(qh[n][r0:], k) for n in range(2)]
            out = []
            for n in range(2):
                s = raw[n] + (shift[n][r0:] - fk[n:n + 1, :])
                if diag:
                    s = jnp.where(_causal_mask(tq - r0, tk, 0, strict=False), s, NEG_BIG)
                out.append(s)
            return out

        def max_pass(j, r0=0, diag=False, assign=False):
            ss = logits(j, fqh, r0, diag)
            for n in range(2):
                max_ref[n, r0:] = ss[n] if assign else jnp.maximum(max_ref[n, r0:], ss[n])

        def sum_pass(j, shift, r0=0, diag=False, assign=False):
            ps = [jnp.exp(s) for s in logits(j, shift, r0, diag)]
            v = v_ref[j]
            for n in range(2):
                sum_ref[n, r0:] = ps[n] if assign else sum_ref[n, r0:] + ps[n]
            for n in range(2):
                pv = _dot(ps[n].astype(BF16), v)
                acc_ref[n, r0:] = pv if assign else acc_ref[n, r0:] + pv

        for d in range(ratio):
            max_pass(ratio * i + d, d * tk, True, d == 0)

        m_diag = [jnp.max(max_ref[n], axis=-1, keepdims=True) for n in range(2)]
        slack = [jnp.max(reach[n] - m_diag[n]) for n in range(2)]

        def f_end(j, n):
            return fend_ref[(pair * nk + jnp.maximum(j, 0)) * 2 + n]

        def block_matters(j):
            gap = jnp.maximum(slack[0] - f_end(j, 0), slack[1] - f_end(j, 1))
            return (j >= 0) & (gap > -EXP_UNDERFLOW)

        last_left = ratio * i - 1
        j_first = lax.while_loop(block_matters, lambda j: j - 1, last_left) + 1

        bound = [reach[n] - f_end(last_left, n) for n in range(2)]
        excess = jnp.maximum(jnp.max(bound[0] - m_diag[0]), jnp.max(bound[1] - m_diag[1]))
        exact = excess > MAX_REFERENCE_EXCESS

        def exact_max():
            def one_max(j, c):
                max_pass(j)
                return c
            lax.fori_loop(j_first, ratio * i, one_max, 0)
            return [jnp.max(max_ref[n], axis=-1, keepdims=True) for n in range(2)]

        def bounded_max():
            walked_left = j_first < ratio * i
            return [jnp.maximum(m_diag[n], jnp.where(walked_left, bound[n], NEG_BIG)) for n in range(2)]

        m = lax.cond(exact, exact_max, bounded_max)
        shift = [fqh[n] - m[n] for n in range(2)]

        for d in range(ratio):
            sum_pass(ratio * i + d, shift, d * tk, True, d == 0)

        def one(j, c):
            sum_pass(j, shift)
            return c
        lax.fori_loop(j_first, ratio * i, one, 0)
        l = [jnp.sum(sum_ref[n], axis=-1, keepdims=True) for n in range(2)]
        y_ref[...] = jnp.where(first, acc_ref[0] / l[0], acc_ref[1] / l[1])
        lse_ref[...] = jnp.where(first, m[0] + jnp.log(l[0]), m[1] + jnp.log(l[1]))
        first_ref[...] = jnp.ones(first_ref.shape, F32) * j_first.astype(F32)

    tile = pltpu.VMEM((2, tq, tk), F32)
    return pl.pallas_call(
        body,
        grid=(N_PAIRS, sp["nq"]),
        in_specs=[pl.BlockSpec(memory_space=pltpu.SMEM), sp["q"], sp["k_rows"], sp["k_rows"], sp["q"], sp["f_t"]],
        out_specs=[sp["q"], sp["q"], sp["first"]],
        out_shape=[sp["wide"], sp["wide"], sp["first_out"]],
        scratch_shapes=[sp["acc"], tile, tile],
        compiler_params=_cparams(56),
        name="fox_fwd",
    )(scalars, qs, k3, v3, fw, ft4)


def _fox_bwd(qs, k3, v3, dy, y, lse, fw, ft4, first_block):
    tq, tk = FOX_Q_BLOCK, ATT_BLOCK
    sp = _att_specs(qs.shape[0], tq)
    ratio = tq // tk

    def body(q_ref, k_ref, v_ref, dy_ref, y_ref, lse_ref, fw_ref, ft_ref, first_ref,
             dq_ref, dfq_ref, dkt_ref, dvt_ref, dft_ref, acc_ref):
        i = pl.program_id(1)

        @pl.when(i == 0)
        def _():
            dkt_ref[...] = jnp.zeros_like(dkt_ref)
            dvt_ref[...] = jnp.zeros_like(dvt_ref)
            dft_ref[...] = jnp.zeros_like(dft_ref)

        first = _head_masks(tq)
        qh = _heads_of(q_ref, first)
        dyv = dy_ref[...]
        dyb = dyv.astype(BF16)
        zero = jnp.zeros_like(dyb)
        dyh = [jnp.where(first, dyb, zero), jnp.where(first, zero, dyb)]
        prod = dyv * y_ref[...]
        zf = jnp.zeros_like(prod)
        delta = [jnp.sum(jnp.where(first, prod, zf), axis=-1, keepdims=True),
                 jnp.sum(jnp.where(first, zf, prod), axis=-1, keepdims=True)]
        fqh = _head_cols(fw_ref)
        lseh = _head_cols(lse_ref)
        shift = [fqh[n] - lseh[n] for n in range(2)]
        acc_ref[...] = jnp.zeros_like(acc_ref)

        def block(j, rows, r0=0, diag=False):
            mask = _causal_mask(tq - r0, tk, 0, strict=False) if diag else None
            k, v, fk = k_ref[j], v_ref[j], ft_ref[0, j]
            q_part, dy_part = [t[r0:] for t in qh], [t[r0:] for t in dyh]
            logits = [_dot_nt(q_part[n], k) for n in range(2)]
            dps = [_dot_nt(dy_part[n], v) for n in range(2)]
            pbs, dsbs, out = [], [], []
            for n in range(2):
                p = jnp.exp(logits[n] + (shift[n][r0:] - fk[n:n + 1, :]))
                if diag:
                    p = jnp.where(mask, p, 0.0)
                ds = p * (dps[n] - delta[n][r0:])
                pbs.append(p.astype(BF16))
                dsbs.append(ds.astype(BF16))
                row_sum = jnp.sum(ds, axis=-1, keepdims=True)
                if r0:
                    row_sum = jnp.concatenate([jnp.zeros((r0, 1), F32), row_sum], axis=0)
                out.append(rows[n] + row_sum)
                dft_ref[0, j, n:n + 1, :] -= _colsum(ds)
            for n in range(2):
                acc_ref[n, r0:] += _dot(dsbs[n], k)
            dkt_ref[0, j] += _dot_tn(q_part[0], dsbs[0]) + _dot_tn(q_part[1], dsbs[1])
            dvt_ref[0, j] += _dot_tn(dy_part[0], pbs[0]) + _dot_tn(dy_part[1], pbs[1])
            return tuple(out)

        rows = (jnp.zeros((tq, 1), F32),) * 2
        rows = lax.fori_loop(_first_block(first_ref, ratio * i), ratio * i, lambda j, c: block(j, c), rows)
        for d in range(ratio):
            rows = block(ratio * i + d, rows, d * tk, True)
        dq_ref[...] = jnp.where(first, acc_ref[0], acc_ref[1])
        lane = lax.broadcasted_iota(jnp.int32, (tq, 8), 1)
        dfq_ref[0] = jnp.where(lane == 0, rows[0], jnp.where(lane == 1, rows[1], 0.0))

    return pl.pallas_call(
        body,
        grid=(N_PAIRS, sp["nq"]),
        in_specs=[sp["q"], sp["k_rows"], sp["k_rows"], sp["q"], sp["q"], sp["q"], sp["q"], sp["f_t"], sp["first"]],
        out_specs=[sp["q"], pl.BlockSpec((1, tq, 8), lambda p, i: (p, i, 0)), sp["k_t"], sp["k_t"], sp["f_t"]],
        out_shape=[sp["wide"], jax.ShapeDtypeStruct((N_PAIRS, qs.shape[0], 8), F32),
                   sp["k_t_out"], sp["k_t_out"], sp["f_t_out"]],
        scratch_shapes=[sp["acc"]],
        compiler_params=_cparams(56),
        name="fox_bwd",
    )(qs, k3, v3, dy, y, lse, fw, ft4, first_block)


SIGN_BIT = 0x80000000


def _sb_terms(z, mask, diag):
    neg_abs = pltpu.bitcast(pltpu.bitcast(z, jnp.uint32) | jnp.uint32(SIGN_BIT), F32)
    lb = jnp.minimum(z, 0.0) - jnp.log(1.0 + jnp.exp(neg_abs))
    l1m = lb - z
    if diag:
        l1m = jnp.where(mask, l1m, 0.0)
    return lb, l1m


def _dot_split2_stacked(x, m2):
    hi, lo = _split2(x)
    return _dot(jnp.concatenate([hi, lo], axis=1), m2)


def _tri_stacked(kind):
    t = _tri(ATT_BLOCK, kind)
    return jnp.concatenate([t, t], axis=0)


def _sb_fwd(qs, k3, v3):
    tq, tk = SB_Q_BLOCK, ATT_BLOCK
    sp = _att_specs(qs.shape[0], tq)
    ratio = tq // tk
    upper = _tri_stacked("row_gt_col")

    def body(q_ref, k_ref, v_ref, u_ref, y_ref, rtot_ref, first_ref, acc_ref):
        i = pl.program_id(1)
        first = _head_masks(tq)
        qh = _heads_of(q_ref, first)
        u = u_ref[...]
        acc_ref[...] = jnp.zeros_like(acc_ref)

        def block(j, rs, diag):
            mask = _causal_mask(tq, tk, i * tq - j * tk, strict=True) if diag else None
            k, v = k_ref[j], v_ref[j]
            logits = [_dot_nt(qh[n], k) for n in range(2)]
            terms = [_sb_terms(z, mask, diag) for z in logits]
            right = [_dot_split2_stacked(l1m, u) for _, l1m in terms]
            weights = []
            for n in range(2):
                a = jnp.exp(terms[n][0] + right[n] + rs[n])
                if diag:
                    a = jnp.where(mask, a, 0.0)
                weights.append(a.astype(BF16))
            for n in range(2):
                acc_ref[n] += _dot(weights[n], v)
            return tuple(rs[n] + jnp.sum(terms[n][1], axis=-1, keepdims=True) for n in range(2))

        rs = (jnp.zeros((tq, 1), F32),) * 2
        for d in range(ratio):
            rs = block(ratio * i + (ratio - 1 - d), rs, True)

        def block_matters(c):
            j, r0, r1 = c
            return (j >= 0) & (jnp.max(jnp.maximum(r0, r1)) > -EXP_UNDERFLOW)

        def walk_left(c):
            j, r0, r1 = c
            r0, r1 = block(j, (r0, r1), False)
            return j - 1, r0, r1

        j, r0, r1 = lax.while_loop(block_matters, walk_left, (ratio * i - 1, rs[0], rs[1]))
        y_ref[...] = jnp.where(first, acc_ref[0], acc_ref[1])
        rtot_ref[...] = jnp.where(first, r0, r1)
        first_ref[...] = jnp.ones(first_ref.shape, F32) * (j + 1).astype(F32)

    return pl.pallas_call(
        body,
        grid=(N_PAIRS, sp["nq"]),
        in_specs=[sp["q"], sp["k_rows"], sp["k_rows"], pl.BlockSpec((2 * tk, tk), lambda p, i: (0, 0))],
        out_specs=[sp["q"], sp["q"], sp["first"]],
        out_shape=[sp["wide"], sp["wide"], sp["first_out"]],
        scratch_shapes=[sp["acc"]],
        compiler_params=_cparams(56),
        name="sb_fwd",
    )(qs, k3, v3, upper)


def _sb_bwd(qs, k3, v3, dy, rtot, first_block):
    tq, tk = SB_Q_BLOCK, ATT_BLOCK
    sp = _att_specs(qs.shape[0], tq)
    ratio = tq // tk
    lower_in = _tri_stacked("row_le_col")
    lower = _tri(tk, "row_lt_col")

    def body(q_ref, k_ref, v_ref, dy_ref, rtot_ref, first_ref, li_ref, l_ref, dq_ref, dkt_ref, dvt_ref, acc_ref):
        i = pl.program_id(1)

        @pl.when(i == 0)
        def _():
            dkt_ref[...] = jnp.zeros_like(dkt_ref)
            dvt_ref[...] = jnp.zeros_like(dvt_ref)

        first = _head_masks(tq)
        qh = _heads_of(q_ref, first)
        dyb = dy_ref[...].astype(BF16)
        zero = jnp.zeros_like(dyb)
        dyh = [jnp.where(first, dyb, zero), jnp.where(first, zero, dyb)]
        rtoth = _head_cols(rtot_ref)
        li = li_ref[...]
        lo_tri = l_ref[...]
        acc_ref[...] = jnp.zeros_like(acc_ref)

        def block(j, carry, diag):
            mask = _causal_mask(tq, tk, i * tq - j * tk, strict=True) if diag else None
            k, v = k_ref[j], v_ref[j]
            logits = [_dot_nt(qh[n], k) for n in range(2)]
            das = [_dot_nt(dyh[n], v) for n in range(2)]
            terms = [_sb_terms(z, mask, diag) for z in logits]
            upto = [_dot_split2_stacked(l1m, li) for _, l1m in terms]
            des, weights = [], []
            for n in range(2):
                a = jnp.exp(terms[n][0] + ((rtoth[n] - carry[2 * n]) - upto[n]))
                if diag:
                    a = jnp.where(mask, a, 0.0)
                des.append(a * das[n])
                weights.append(a.astype(BF16))
            lefts = [_dot(de.astype(BF16), lo_tri) for de in des]
            dzbs, out = [], []
            for n in range(2):
                beta = jnp.exp(terms[n][0])
                dz = des[n] - (des[n] + (carry[2 * n + 1] + lefts[n])) * beta
                if diag:
                    dz = jnp.where(mask, dz, 0.0)
                dzbs.append(dz.astype(BF16))
                out += [carry[2 * n] + jnp.sum(terms[n][1], axis=-1, keepdims=True),
                        carry[2 * n + 1] + jnp.sum(des[n], axis=-1, keepdims=True)]
            for n in range(2):
                acc_ref[n] += _dot(dzbs[n], k)
            dkt_ref[0, j] += _dot_tn(qh[0], dzbs[0]) + _dot_tn(qh[1], dzbs[1])
            dvt_ref[0, j] += _dot_tn(dyh[0], weights[0]) + _dot_tn(dyh[1], weights[1])
            return tuple(out)

        carry = (jnp.zeros((tq, 1), F32),) * 4
        carry = lax.fori_loop(_first_block(first_ref, ratio * i), ratio * i, lambda j, c: block(j, c, False), carry)
        for d in range(ratio):
            carry = block(ratio * i + d, carry, True)
        dq_ref[...] = jnp.where(first, acc_ref[0], acc_ref[1])

    return pl.pallas_call(
        body,
        grid=(N_PAIRS, sp["nq"]),
        in_specs=[sp["q"], sp["k_rows"], sp["k_rows"], sp["q"], sp["q"], sp["first"],
                  pl.BlockSpec((2 * tk, tk), lambda p, i: (0, 0)), pl.BlockSpec((tk, tk), lambda p, i: (0, 0))],
        out_specs=[sp["q"], sp["k_t"], sp["k_t"]],
        out_shape=[sp["wide"], sp["k_t_out"], sp["k_t_out"]],
        scratch_shapes=[sp["acc"]],
        compiler_params=_cparams(56),
        name="sb_bwd",
    )(qs, k3, v3, dy, rtot, first_block, lower_in, lower)


def _merge_fwd(x1, gates, y_fox, y_sb, w_bf, w_bs, w_out, tm=512):
    s_len = x1.shape[0]

    def body(x_ref, g_ref, yf_ref, ys_ref, wbf_ref, wbs_ref, wo_ref, o_ref):
        g = g_ref[...].astype(F32)
        of = _dot(yf_ref[...].astype(BF16), wbf_ref[...])
        os_ = _dot(ys_ref[...].astype(BF16), wbs_ref[...])
        merged = _sigmoid(g[:, 0:D_MODEL]) * of + _sigmoid(g[:, D_MODEL:]) * os_
        o_ref[...] = x_ref[...] + _dot(merged.astype(BF16), wo_ref[...])

    row = lambda i: (i, 0)
    full = lambda i: (0, 0)
    return pl.pallas_call(
        body,
        grid=(s_len // tm,),
        in_specs=[
            pl.BlockSpec((tm, D_MODEL), row),
            pl.BlockSpec((tm, 2 * D_MODEL), row),
            pl.BlockSpec((tm, ATT_W), row),
            pl.BlockSpec((tm, ATT_W), row),
            pl.BlockSpec((ATT_W, D_MODEL), full),
            pl.BlockSpec((ATT_W, D_MODEL), full),
            pl.BlockSpec((D_MODEL, D_MODEL), full),
        ],
        out_specs=pl.BlockSpec((tm, D_MODEL), row),
        out_shape=jax.ShapeDtypeStruct((s_len, D_MODEL), F32),
        compiler_params=_cparams(48),
        name="merge_fwd",
    )(x1, gates, y_fox, y_sb, w_bf, w_bs, w_out)


def _merge_bwd(dx2, gates, y_fox, y_sb, w_bf, w_bs, w_out, tm=512):
    s_len = dx2.shape[0]

    def body(d_ref, g_ref, yf_ref, ys_ref, wbf_ref, wbs_ref, wo_ref,
             dyf_ref, dys_ref, dg_ref, dof_ref, dos_ref, m_ref, dbf_ref):
        dbf = d_ref[...].astype(BF16)
        dbf_ref[...] = dbf
        dm = _dot_nt(dbf, wo_ref[...])
        g = g_ref[...].astype(F32)
        of = _dot(yf_ref[...].astype(BF16), wbf_ref[...])
        os_ = _dot(ys_ref[...].astype(BF16), wbs_ref[...])
        sf = _sigmoid(g[:, 0:D_MODEL])
        ss = _sigmoid(g[:, D_MODEL:])
        m_ref[...] = (sf * of + ss * os_).astype(BF16)
        d_of = (dm * sf).astype(BF16)
        d_os = (dm * ss).astype(BF16)
        dof_ref[...] = d_of
        dos_ref[...] = d_os
        dg_ref[:, 0:D_MODEL] = (dm * of * sf * (1.0 - sf)).astype(BF16)
        dg_ref[:, D_MODEL:] = (dm * os_ * ss * (1.0 - ss)).astype(BF16)
        dyf_ref[...] = _dot_nt(d_of, wbf_ref[...])
        dys_ref[...] = _dot_nt(d_os, wbs_ref[...])

    row = lambda i: (i, 0)
    full = lambda i: (0, 0)
    return pl.pallas_call(
        body,
        grid=(s_len // tm,),
        in_specs=[
            pl.BlockSpec((tm, D_MODEL), row),
            pl.BlockSpec((tm, 2 * D_MODEL), row),
            pl.BlockSpec((tm, ATT_W), row),
            pl.BlockSpec((tm, ATT_W), row),
            pl.BlockSpec((ATT_W, D_MODEL), full),
            pl.BlockSpec((ATT_W, D_MODEL), full),
            pl.BlockSpec((D_MODEL, D_MODEL), full),
        ],
        out_specs=[
            pl.BlockSpec((tm, ATT_W), row), pl.BlockSpec((tm, ATT_W), row),
            pl.BlockSpec((tm, 2 * D_MODEL), row),
            pl.BlockSpec((tm, D_MODEL), row), pl.BlockSpec((tm, D_MODEL), row),
            pl.BlockSpec((tm, D_MODEL), row), pl.BlockSpec((tm, D_MODEL), row),
        ],
        out_shape=[
            jax.ShapeDtypeStruct((s_len, ATT_W), F32), jax.ShapeDtypeStruct((s_len, ATT_W), F32),
            jax.ShapeDtypeStruct((s_len, 2 * D_MODEL), BF16),
            jax.ShapeDtypeStruct((s_len, D_MODEL), BF16), jax.ShapeDtypeStruct((s_len, D_MODEL), BF16),
            jax.ShapeDtypeStruct((s_len, D_MODEL), BF16), jax.ShapeDtypeStruct((s_len, D_MODEL), BF16),
        ],
        compiler_params=_cparams(56),
        name="merge_bwd",
    )(dx2, gates, y_fox, y_sb, w_bf, w_bs, w_out)


def _ple_loss(x3, p, g, w_pg, w_pp, target, tm=512):
    s_len = x3.shape[0]
    inv_d = 1.0 / D_MODEL

    def body(x_ref, p_ref, g_ref, wpg_ref, wpp_ref, t_ref,
             dx_ref, du_ref, dt_ref, hn_ref, dg_ref, loss_ref):
        @pl.when(pl.program_id(0) == 0)
        def _():
            dg_ref[...] = jnp.zeros_like(dg_ref)
            loss_ref[...] = jnp.zeros_like(loss_ref)

        x = x_ref[...]
        xn, r = _rms(x)
        gain = g_ref[...]
        hn = (xn * gain).astype(BF16)
        hn_ref[...] = hn
        sg = _sigmoid(_dot(hn, wpg_ref[...]))
        t = _dot(p_ref[...].astype(BF16), wpp_ref[...])
        err = x + sg * t - t_ref[...]
        sq = jnp.sum(_colsum(err * err), axis=-1, keepdims=True)
        loss_ref[...] += (0.5 * inv_d) * sq
        dy = err * inv_d
        du = (dy * t * sg * (1.0 - sg)).astype(BF16)
        du_ref[...] = du
        dt_ref[...] = (dy * sg).astype(BF16)
        dh = _dot_nt(du, wpg_ref[...])
        dx_ref[...] = dy + _rms_bwd(dh, xn, r, gain)
        dg_ref[0:1, :] += _colsum(dh * xn)

    row = lambda i: (i, 0)
    full = lambda i: (0, 0)
    bf = jax.ShapeDtypeStruct((s_len, D_MODEL), BF16)
    return pl.pallas_call(
        body,
        grid=(s_len // tm,),
        in_specs=[
            pl.BlockSpec((tm, D_MODEL), row),
            pl.BlockSpec((tm, PLE_DIM), row),
            pl.BlockSpec((1, D_MODEL), full),
            pl.BlockSpec((D_MODEL, D_MODEL), full),
            pl.BlockSpec((PLE_DIM, D_MODEL), full),
            pl.BlockSpec((tm, D_MODEL), row),
        ],
        out_specs=[
            pl.BlockSpec((tm, D_MODEL), row), pl.BlockSpec((tm, D_MODEL), row),
            pl.BlockSpec((tm, D_MODEL), row), pl.BlockSpec((tm, D_MODEL), row),
            pl.BlockSpec((8, D_MODEL), full), pl.BlockSpec((8, LANES), full),
        ],
        out_shape=[
            jax.ShapeDtypeStruct((s_len, D_MODEL), F32), bf, bf, bf,
            jax.ShapeDtypeStruct((8, D_MODEL), F32), jax.ShapeDtypeStruct((8, LANES), F32),
        ],
        compiler_params=_cparams(48),
        name="ple_loss",
    )(x3, p, g, w_pg, w_pp, target)


def _sb_grads_packed(dqs, dkt4, dvt4, tm=512):
    s_len = dqs.shape[0]

    def body(dq_ref, dkt_ref, dvt_ref, o_ref):
        o_ref[:, 0:ATT_W] = (dq_ref[...] * QK_SCALE).astype(BF16)
        o_ref[:, ATT_W:2 * ATT_W] = _rows_of_transposed(dkt_ref).astype(BF16)
        o_ref[:, 2 * ATT_W:] = _rows_of_transposed(dvt_ref).astype(BF16)

    return pl.pallas_call(
        body,
        grid=(s_len // tm,),
        in_specs=[pl.BlockSpec((tm, ATT_W), lambda i: (i, 0)), _transposed_spec(tm), _transposed_spec(tm)],
        out_specs=pl.BlockSpec((tm, 3 * ATT_W), lambda i: (i, 0)),
        out_shape=jax.ShapeDtypeStruct((s_len, 3 * ATT_W), BF16),
        name="sb_grads_packed",
    )(dqs, dkt4, dvt4)


def _qknorm_bwd(fq, fk, dqs, dkt4, dvt4, qn, kn, bd, bd_t, tm=512):
    s_len = fq.shape[0]

    def body(fq_ref, fk_ref, dq_ref, dkt_ref, dvt_ref, qn_ref, kn_ref, bd_ref, bdt_ref,
             dz_ref, dqn_ref, dkn_ref):
        @pl.when(pl.program_id(0) == 0)
        def _():
            dqn_ref[...] = jnp.zeros_like(dqn_ref)
            dkn_ref[...] = jnp.zeros_like(dkn_ref)

        bd_m = bd_ref[...]
        bdt_m = bdt_ref[...]

        def one(x, dy, gain, dgain_ref):
            xn, rw = _head_rms(x, bd_m, bdt_m)
            dgain_ref[0:1, :] += _colsum(dy * xn)
            dxn = dy * gain
            return rw * (dxn - xn * _head_mean(dxn * xn, bd_m, bdt_m))

        dz_ref[:, 0:ATT_W] = one(fq_ref[...], dq_ref[...] * QK_SCALE, qn_ref[...], dqn_ref).astype(BF16)
        dz_ref[:, ATT_W:2 * ATT_W] = one(fk_ref[...], _rows_of_transposed(dkt_ref), kn_ref[...], dkn_ref).astype(BF16)
        dz_ref[:, 2 * ATT_W:] = _rows_of_transposed(dvt_ref).astype(BF16)

    row = lambda i: (i, 0)
    full = lambda i: (0, 0)
    att = pl.BlockSpec((tm, ATT_W), row)
    return pl.pallas_call(
        body,
        grid=(s_len // tm,),
        in_specs=[att, att, att, _transposed_spec(tm), _transposed_spec(tm),
                  pl.BlockSpec((1, ATT_W), full), pl.BlockSpec((1, ATT_W), full),
                  pl.BlockSpec((ATT_W, LANES), full), pl.BlockSpec((LANES, ATT_W), full)],
        out_specs=[pl.BlockSpec((tm, 3 * ATT_W), row), pl.BlockSpec((8, ATT_W), full), pl.BlockSpec((8, ATT_W), full)],
        out_shape=[jax.ShapeDtypeStruct((s_len, 3 * ATT_W), BF16),
                   jax.ShapeDtypeStruct((8, ATT_W), F32), jax.ShapeDtypeStruct((8, ATT_W), F32)],
        name="qknorm_bwd",
    )(fq, fk, dqs, dkt4, dvt4, qn, kn, bd, bd_t)


def _inproj_bwd(x1, dx2, g, dzf, dlogf, logf, dzs, dgates, w_fox, w_fl, w_sb, w_gates, tm=512):
    s_len = x1.shape[0]

    def body(x_ref, d_ref, g_ref, dzf_ref, dlf_ref, lf_ref, dzs_ref, dgt_ref, wf_ref, wl_ref, ws_ref, wg_ref,
             dx_ref, h_ref, dfl_ref, dg_ref, db_ref):
        @pl.when(pl.program_id(0) == 0)
        def _():
            dg_ref[...] = jnp.zeros_like(dg_ref)
            db_ref[...] = jnp.zeros_like(db_ref)

        xn, r = _rms(x_ref[...])
        gain = g_ref[...]
        h_ref[...] = (xn * gain).astype(BF16)
        lane = lax.broadcasted_iota(jnp.int32, (tm, LANES), 1)
        dfl = jnp.where(lane < N_HEADS, dlf_ref[...] * (1.0 - jnp.exp(lf_ref[...])), 0.0)
        db_ref[0:1, :] += _colsum(dfl)
        dflb = dfl.astype(BF16)
        dfl_ref[...] = dflb
        dh = (_dot_nt(dzf_ref[...], wf_ref[...]) + _dot_nt(dflb, wl_ref[...])
              + _dot_nt(dzs_ref[...], ws_ref[...]) + _dot_nt(dgt_ref[...], wg_ref[...]))
        dx_ref[...] = d_ref[...] + _rms_bwd(dh, xn, r, gain)
        dg_ref[0:1, :] += _colsum(dh * xn)

    row = lambda i: (i, 0)
    full = lambda i: (0, 0)
    return pl.pallas_call(
        body,
        grid=(s_len // tm,),
        in_specs=[
            pl.BlockSpec((tm, D_MODEL), row),
            pl.BlockSpec((tm, D_MODEL), row),
            pl.BlockSpec((1, D_MODEL), full),
            pl.BlockSpec((tm, 3 * ATT_W), row),
            pl.BlockSpec((tm, LANES), row),
            pl.BlockSpec((tm, LANES), row),
            pl.BlockSpec((tm, 3 * ATT_W), row),
            pl.BlockSpec((tm, 2 * D_MODEL), row),
            pl.BlockSpec((D_MODEL, 3 * ATT_W), full),
            pl.BlockSpec((D_MODEL, LANES), full),
            pl.BlockSpec((D_MODEL, 3 * ATT_W), full),
            pl.BlockSpec((D_MODEL, 2 * D_MODEL), full),
        ],
        out_specs=[
            pl.BlockSpec((tm, D_MODEL), row), pl.BlockSpec((tm, D_MODEL), row), pl.BlockSpec((tm, LANES), row),
            pl.BlockSpec((8, D_MODEL), full), pl.BlockSpec((8, LANES), full),
        ],
        out_shape=[
            jax.ShapeDtypeStruct((s_len, D_MODEL), F32), jax.ShapeDtypeStruct((s_len, D_MODEL), BF16),
            jax.ShapeDtypeStruct((s_len, LANES), BF16),
            jax.ShapeDtypeStruct((8, D_MODEL), F32), jax.ShapeDtypeStruct((8, LANES), F32),
        ],
        compiler_params=_cparams(56),
        name="inproj_bwd",
    )(x1, dx2, g, dzf, dlogf, logf, dzs, dgates, w_fox, w_fl, w_sb, w_gates)


def _split_w_in(w_in):
    o = 3 * ATT_W
    w_fox = w_in[:, 0:o]
    w_fl = jnp.pad(w_in[:, o:o + N_HEADS], ((0, 0), (0, LANES - N_HEADS)))
    w_sb = w_in[:, o + N_HEADS:2 * o + N_HEADS]
    w_gates = w_in[:, 2 * o + N_HEADS:]
    return w_fox, w_fl, w_sb, w_gates


def _local_grads(x, p, target, small, full, pending=None, send_early=None):
    blk = ATT_BLOCK
    bd, bd_t = _head_sum_matrices()
    full = dict(full)
    late = list(pending) if pending else []

    x1, a1, b1, u1, *gathered = _ffn_fwd(x, small["ffn1_norm"], full["ffn1_w_gate"], full["ffn1_w_up"],
                                     full["ffn1_w_down"], gather=[pending[k] for k in late])
    for k, gth in zip(late, gathered):
        full[k] = gth if k in KEPT_AS_SHARDS else _whole(k, gth)
    w_fox, w_fl, w_sb, w_gates = _split_w_in(full["w_in"])
    bias = jnp.pad(small["forget_bias"], ((0, 0), (0, LANES - N_HEADS)))
    qn = jnp.tile(small["q_norm"], (1, N_HEADS))
    kn = jnp.tile(small["k_norm"], (1, N_HEADS))
    fq, fk, f_qs, f_k, f_v, logf, s_qs, s_k, s_v, gates = _inproj_fwd(
        x1, small["mix_norm"], w_fox, w_fl, w_sb, w_gates, bias, qn, kn, bd, bd_t)
    f_cum, fw = _cumsum_rows(logf, reverse=False, spread=bd_t)
    f8 = f_cum[:, 0:N_HEADS]
    ft4 = _pair_rows_t(f8, blk)
    f_k3, f_v3 = _blocked_rows(f_k, blk), _blocked_rows(f_v, blk)
    y_fox, lse, f_first = _fox_fwd(f_qs, f_k3, f_v3, fw, ft4,
                                   _normed_dot_bound(small["q_norm"], small["k_norm"]))
    s_k3, s_v3 = _blocked_rows(s_k, blk), _blocked_rows(s_v, blk)
    y_sb, s_rtot, s_first = _sb_fwd(s_qs, s_k3, s_v3)
    x2 = _merge_fwd(x1, gates, y_fox, y_sb, full["w_branch_fox"], full["w_branch_sb"], full["w_out"])
    x3, a2, b2, u2 = _ffn_fwd(x2, small["ffn2_norm"], full["ffn2_w_gate"], full["ffn2_w_up"], full["ffn2_w_down"])

    dx3, du_ple, dt_ple, hn_ple, dg_ple, loss_sum = _ple_loss(
        x3, p, small["ple_norm"], full["w_ple_gate"], full["w_ple_proj"], target)
    dx2, da2, db2, h_ffn2, d3_bf, dg_ffn2 = _ffn_bwd(
        x2, dx3, small["ffn2_norm"], a2, b2, full["ffn2_w_gate"], full["ffn2_w_up"], full["ffn2_w_down"])
    dy_fox, dy_sb, dgates, d_of, d_os, merged, d2_bf = _merge_bwd(
        dx2, gates, y_fox, y_sb, full["w_branch_fox"], full["w_branch_sb"], full["w_out"])

    f_dqs, dfq_p, f_dkt4, f_dvt4, dft4 = _fox_bwd(f_qs, f_k3, f_v3, dy_fox, y_fox, lse, fw, ft4, f_first)
    s_dqs, s_dkt4, s_dvt4 = _sb_bwd(s_qs, s_k3, s_v3, dy_sb, s_rtot, s_first)

    dzf, dqn8, dkn8 = _qknorm_bwd(fq, fk, f_dqs, f_dkt4, f_dvt4, qn, kn, bd, bd_t)
    dzs = _sb_grads_packed(s_dqs, s_dkt4, s_dvt4)
    df8 = _unpair_rows_t(dft4) + dfq_p[:, :, 0:2].transpose(1, 0, 2).reshape(-1, N_HEADS)
    dlogf = _cumsum_rows(jnp.pad(df8, ((0, 0), (0, LANES - N_HEADS))), reverse=True)
    dx1, h_mix, dfl, dg_mix, dbias8 = _inproj_bwd(
        x1, dx2, small["mix_norm"], dzf, dlogf, logf, dzs, dgates, w_fox, w_fl, w_sb, w_gates)

    one = lambda t: t[None]
    gw = {}
    gw["ffn2_w_gate"] = _wgrad(da2, one(h_ffn2), name="wgrad_ffn2_gate")
    gw["ffn2_w_up"] = _wgrad(db2, one(h_ffn2), name="wgrad_ffn2_up")
    gw["ffn2_w_down"] = _wgrad(u2, one(d3_bf), scale=0.5, name="wgrad_ffn2_down")
    g_fox = _wgrad(one(h_mix), one(dzf), name="wgrad_in_fox")[0]
    g_fl = _wgrad(one(h_mix), one(dfl), name="wgrad_in_forget")[0]
    g_sb = _wgrad(one(h_mix), one(dzs), name="wgrad_in_sb")[0]
    g_gt = _wgrad(one(h_mix), one(dgates), name="wgrad_in_gates")[0]
    gw["w_in"] = jnp.concatenate([g_fox, g_fl[:, 0:N_HEADS], g_sb, g_gt], axis=1)
    gw["w_branch_fox"] = _wgrad(one(y_fox), one(d_of), name="wgrad_branch_fox")[0]
    gw["w_branch_sb"] = _wgrad(one(y_sb), one(d_os), name="wgrad_branch_sb")[0]
    gw["w_out"] = _wgrad(one(merged), one(d2_bf), name="wgrad_out")[0]
    gw["w_ple_gate"] = _wgrad(one(hn_ple), one(du_ple), name="wgrad_ple_gate")[0]
    gw["w_ple_proj"] = _wgrad(one(p), one(dt_ple), name="wgrad_ple_proj")[0]

    gw["ffn1_w_down"] = _wgrad(u1, one(dx1), scale=0.5, name="wgrad_ffn1_down")

    sent_names, to_send = send_early(gw) if send_early else ([], [])
    grad_x, da1, db1, h_ffn1, _, dg_ffn1, *landed = _ffn_bwd(
        x, dx1, small["ffn1_norm"], a1, b1, full["ffn1_w_gate"], full["ffn1_w_up"], full["ffn1_w_down"],
        scatter=to_send)
    gw["ffn1_w_gate"] = _wgrad(da1, one(h_ffn1), name="wgrad_ffn1_gate")
    gw["ffn1_w_up"] = _wgrad(db1, one(h_ffn1), name="wgrad_ffn1_up")

    fold = lambda t: jnp.sum(t[0:1].reshape(N_HEADS, HEAD_DIM), axis=0, keepdims=True)
    gs = {
        "ffn1_norm": dg_ffn1[0:1], "mix_norm": dg_mix[0:1], "ffn2_norm": dg_ffn2[0:1], "ple_norm": dg_ple[0:1],
        "forget_bias": dbias8[0:1, 0:N_HEADS], "q_norm": fold(dqn8), "k_norm": fold(dkn8),
    }
    return loss_sum, grad_x, gw, gs, dict(zip(sent_names, landed))


def _position():
    return lax.axis_index("x"), lax.axis_index("y"), lax.axis_index("c")


def _other_chips(x, y):
    return [(1 - x, y), (x, 1 - y), (1 - x, 1 - y)]


ANY = pl.BlockSpec(memory_space=pl.ANY)


def _place_own_shard(w, q):
    rows, cols = w.shape
    tr = _row_block(rows, cols * 4, budget=2 * MIB)

    def body(q_ref, w_ref, o_ref):
        o_ref[0] = w_ref[...].astype(BF16)

    return pl.pallas_call(
        body,
        grid_spec=pltpu.PrefetchScalarGridSpec(
            num_scalar_prefetch=1,
            grid=(rows // tr,),
            in_specs=[pl.BlockSpec((tr, cols), lambda i, q_ref: (i, 0))],
            out_specs=pl.BlockSpec((1, tr, cols), lambda i, q_ref: (q_ref[0], i, 0)),
        ),
        out_shape=jax.ShapeDtypeStruct((N_CHIPS, rows, cols), BF16),
        name="place_own_shard",
    )(q, w)


def _gather_semaphores(n):
    return [pltpu.SemaphoreType.DMA((6 * n,)), pltpu.SemaphoreType.DMA((6 * n,))]


def _gather_steps(bufs, send_sems, recv_sems):
    n = len(bufs)
    x, y, c = _position()
    q = 2 * x + y
    chips = _other_chips(x, y)
    sibling = (x, y, 1 - c)

    def half(a, slot, which):
        r2 = bufs[a].shape[1] // 2
        return bufs[a].at[slot, pl.ds(which * r2, r2), :]

    def copy(a, k, region, to):
        return pltpu.make_async_remote_copy(
            src_ref=region, dst_ref=region, send_sem=send_sems.at[6 * a + k], recv_sem=recv_sems.at[6 * a + k],
            device_id=to, device_id_type=MESH)

    def to_chip(a, k):
        tx, ty = chips[k]
        return copy(a, k, half(a, q, c), (tx, ty, c))

    def to_sibling(a, k):
        tx, ty = chips[k]
        return copy(a, 3 + k, half(a, 2 * tx + ty, c), sibling)

    def start():
        for a in range(n):
            for k in range(3):
                to_chip(a, k).start()

    def finish():
        for a in range(n):
            for k, (tx, ty) in enumerate(chips):
                copy(a, k, half(a, 2 * tx + ty, c), (tx, ty, c)).wait_recv()
                to_sibling(a, k).start()
        for a in range(n):
            for k, (tx, ty) in enumerate(chips):
                copy(a, 3 + k, half(a, 2 * tx + ty, 1 - c), sibling).wait_recv()
        for a in range(n):
            for k in range(3):
                to_chip(a, k).wait_send()
                to_sibling(a, k).wait_send()

    return start, finish


def _allgather_weights(slots):
    n = len(slots)

    def body(*refs):
        start, finish = _gather_steps(refs[n:2 * n], *refs[2 * n:])
        start()
        finish()

    return pl.pallas_call(
        body,
        in_specs=[ANY] * n,
        out_specs=[ANY] * n,
        out_shape=[jax.ShapeDtypeStruct(s.shape, s.dtype) for s in slots],
        input_output_aliases={a: a for a in range(n)},
        scratch_shapes=_gather_semaphores(n),
        name="allgather_weights",
    )(*slots)


def _exchange_pair_halves(grads):
    n = len(grads)

    def body(*refs):
        ins, outs = refs[0:n], refs[n:2 * n]
        send_sems, recv_sems = refs[2 * n:]
        x, y, c = _position()
        copies = []
        for a in range(n):
            r2 = grads[a].shape[1] // 2
            cp = pltpu.make_async_remote_copy(
                src_ref=ins[a].at[:, pl.ds((1 - c) * r2, r2), :], dst_ref=outs[a],
                send_sem=send_sems.at[a], recv_sem=recv_sems.at[a], device_id=(x, y, 1 - c), device_id_type=MESH)
            cp.start()
            copies.append(cp)
        for cp in copies:
            cp.wait()

    return pl.pallas_call(
        body,
        in_specs=[ANY] * n,
        out_specs=[ANY] * n,
        out_shape=[jax.ShapeDtypeStruct((N_CHIPS, g.shape[1] // 2, g.shape[2]), g.dtype) for g in grads],
        scratch_shapes=[pltpu.SemaphoreType.DMA((n,)), pltpu.SemaphoreType.DMA((n,))],
        name="rs_pair_exchange",
    )(*grads)


def _scatter_semaphores(n):
    return [pltpu.SemaphoreType.DMA((3 * n,)), pltpu.SemaphoreType.DMA((3 * n,)), pltpu.SemaphoreType.DMA((n,))]


def _scatter_steps(ins, outs, send_sems, recv_sems, local_sems):
    n = len(ins)
    x, y, c = _position()
    q = 2 * x + y
    chips = _other_chips(x, y)

    def own(a):
        return pltpu.make_async_copy(ins[a].at[q], outs[a].at[q], local_sems.at[a])

    def to_chip(a, k):
        tx, ty = chips[k]
        return pltpu.make_async_remote_copy(
            src_ref=ins[a].at[2 * tx + ty], dst_ref=outs[a].at[q],
            send_sem=send_sems.at[3 * a + k], recv_sem=recv_sems.at[3 * a + k],
            device_id=(tx, ty, c), device_id_type=MESH)

    def start():
        for a in range(n):
            own(a).start()
            for k in range(3):
                to_chip(a, k).start()

    def finish():
        for a in range(n):
            own(a).wait()
            for k in range(3):
                to_chip(a, k).wait()

    return start, finish


def _scatter_to_owner_chips(pairs):
    n = len(pairs)

    def body(*refs):
        start, finish = _scatter_steps(refs[0:n], refs[n:2 * n], *refs[2 * n:])
        start()
        finish()

    return pl.pallas_call(
        body,
        in_specs=[ANY] * n,
        out_specs=[ANY] * n,
        out_shape=[jax.ShapeDtypeStruct(p.shape, p.dtype) for p in pairs],
        scratch_shapes=_scatter_semaphores(n),
        name="rs_scatter",
    )(*pairs)


def _join_halves(shards):
    n = len(shards)

    def body(*refs):
        bufs = refs[n:2 * n]
        send_sems, recv_sems = refs[2 * n:]
        x, y, c = _position()
        started = []
        for a in range(n):
            r2 = shards[a].shape[0] // 2
            mine = bufs[a].at[pl.ds(c * r2, r2), :]
            cp = pltpu.make_async_remote_copy(
                src_ref=mine, dst_ref=mine, send_sem=send_sems.at[a], recv_sem=recv_sems.at[a],
                device_id=(x, y, 1 - c), device_id_type=MESH)
            cp.start()
            started.append(cp)
        for cp in started:
            cp.wait()

    return pl.pallas_call(
        body,
        in_specs=[ANY] * n,
        out_specs=[ANY] * n,
        out_shape=[jax.ShapeDtypeStruct(t.shape, t.dtype) for t in shards],
        input_output_aliases={a: a for a in range(n)},
        scratch_shapes=[pltpu.SemaphoreType.DMA((n,)), pltpu.SemaphoreType.DMA((n,))],
        name="rs_join_halves",
    )(*shards)


def _add_pair(g, got, c):
    _, r2, cols = got.shape

    def body(c_ref, g_ref, got_ref, o_ref):
        o_ref[...] = (g_ref[...].astype(F32) + got_ref[...].astype(F32)).astype(BF16)

    spec = pl.BlockSpec((1, r2, cols), lambda s, c_ref: (s, 0, 0))
    return pl.pallas_call(
        body,
        grid_spec=pltpu.PrefetchScalarGridSpec(
            num_scalar_prefetch=1,
            grid=(N_CHIPS,),
            in_specs=[pl.BlockSpec((1, r2, cols), lambda s, c_ref: (s, c_ref[0], 0)), spec],
            out_specs=spec,
        ),
        out_shape=jax.ShapeDtypeStruct(got.shape, BF16),
        name="rs_add_pair",
    )(c, g, got)


def _add_chips(parts, c):
    _, r2, cols = parts.shape

    def body(c_ref, p0, p1, p2, p3, o_ref):
        o_ref[...] = ((p0[0].astype(F32) + p1[0].astype(F32)) + p2[0].astype(F32)) + p3[0].astype(F32)

    specs = [pl.BlockSpec((1, r2, cols), functools.partial(lambda i, c_ref, s: (s, 0, 0), s=s))
             for s in range(N_CHIPS)]
    return pl.pallas_call(
        body,
        grid_spec=pltpu.PrefetchScalarGridSpec(
            num_scalar_prefetch=1,
            grid=(1,),
            in_specs=specs,
            out_specs=pl.BlockSpec((r2, cols), lambda i, c_ref: (c_ref[0], 0)),
        ),
        out_shape=jax.ShapeDtypeStruct((2 * r2, cols), F32),
        name="rs_add_chips",
    )(c, parts, parts, parts, parts)


def _allreduce_small(part):
    shape = part.shape

    def body(in_ref, out_ref, gather_ref, send_sems, recv_sems):
        x, y, c = _position()
        me = 4 * x + 2 * y + c
        relations = [(a, b, d) for a in (0, 1) for b in (0, 1) for d in (0, 1)][1:]
        flip = lambda v, f: 1 - v if f else v
        copies = []
        for k, (a, b, d) in enumerate(relations):
            cp = pltpu.make_async_remote_copy(
                src_ref=in_ref, dst_ref=gather_ref.at[me], send_sem=send_sems.at[k], recv_sem=recv_sems.at[k],
                device_id=(flip(x, a), flip(y, b), flip(c, d)), device_id_type=MESH)
            cp.start()
            copies.append(cp)
        gather_ref[me] = in_ref[...]
        for cp in copies:
            cp.wait()
        total = gather_ref[0]
        for dev in range(1, 8):
            total = total + gather_ref[dev]
        out_ref[...] = total

    vmem = pl.BlockSpec(memory_space=pltpu.VMEM)
    return pl.pallas_call(
        body,
        in_specs=[vmem],
        out_specs=vmem,
        out_shape=jax.ShapeDtypeStruct(shape, F32),
        scratch_shapes=[pltpu.VMEM((8,) + shape, F32), pltpu.SemaphoreType.DMA((7,)), pltpu.SemaphoreType.DMA((7,))],
        name="allreduce_small",
    )(part)


def _adamw(w, g, m, v):
    rows, cols = w.shape
    tr = _row_block(rows, cols * 4, budget=MIB)
    c1 = 1.0 / (1.0 - ADAM_B1 ** ADAM_STEP)
    c2 = 1.0 / (1.0 - ADAM_B2 ** ADAM_STEP)

    def body(w_ref, g_ref, m_ref, v_ref, d_ref, nm_ref, nv_ref):
        g_ = g_ref[...]
        nm = ADAM_B1 * m_ref[...] + (1.0 - ADAM_B1) * g_
        nv = ADAM_B2 * v_ref[...] + (1.0 - ADAM_B2) * (g_ * g_)
        nm_ref[...] = nm
        nv_ref[...] = nv
        d_ref[...] = -ADAM_LR * ((nm * c1) / (jnp.sqrt(nv * c2) + ADAM_EPS) + ADAM_WD * w_ref[...])

    spec = pl.BlockSpec((tr, cols), lambda i: (i, 0))
    out = jax.ShapeDtypeStruct((rows, cols), F32)
    return pl.pallas_call(
        body,
        grid=(rows // tr,),
        in_specs=[spec] * 4,
        out_specs=[spec] * 3,
        out_shape=[out] * 3,
        name="adamw",
    )(w, g, m, v)


BIG = ["ffn1_w_gate", "ffn1_w_up", "ffn1_w_down", "w_in", "w_branch_fox", "w_branch_sb", "w_out",
       "ffn2_w_gate", "ffn2_w_up", "ffn2_w_down", "w_ple_gate", "w_ple_proj"]
SMALL = ["ffn1_norm", "mix_norm", "ffn2_norm", "ple_norm", "forget_bias", "q_norm", "k_norm"]
COLUMN_SHARDED = ["w_in", "w_branch_fox", "w_branch_sb", "w_ple_proj"]
KEPT_AS_SHARDS = ["ffn1_w_gate", "ffn1_w_up", "ffn1_w_down", "ffn2_w_gate", "ffn2_w_up", "ffn2_w_down"]
WORKED_TRANSPOSED = ["ffn1_w_gate", "ffn1_w_up", "ffn2_w_gate", "ffn2_w_up"]
NEEDED_FIRST = ["ffn1_w_gate", "ffn1_w_up", "ffn1_w_down"]
READY_LAST = ["ffn1_w_gate", "ffn1_w_up"]
ORDER = ["ffn1_norm", "ffn1_w_gate", "ffn1_w_up", "ffn1_w_down", "mix_norm", "w_in", "forget_bias", "q_norm",
         "k_norm", "w_branch_fox", "w_branch_sb", "w_out", "ffn2_norm", "ffn2_w_gate", "ffn2_w_up",
         "ffn2_w_down", "ple_norm", "w_ple_gate", "w_ple_proj"]
SMALL_ROWS = {"ffn1_norm": 0, "mix_norm": 1, "ffn2_norm": 2, "ple_norm": 3}
SMALL_COLS = {"forget_bias": (0, N_HEADS), "q_norm": (N_HEADS, HEAD_DIM), "k_norm": (N_HEADS + HEAD_DIM, HEAD_DIM)}
LOSS_ROW = 5


def _stored(name, a):
    return jnp.swapaxes(a[0], 0, 1) if name in WORKED_TRANSPOSED else a[0]


def _returned(name, t):
    return (jnp.swapaxes(t, 0, 1) if name in WORKED_TRANSPOSED else t)[None]


def _whole(name, gathered):
    if name in COLUMN_SHARDED:
        return jnp.concatenate([gathered[s] for s in range(N_CHIPS)], axis=1)
    return gathered.reshape(-1, gathered.shape[-1])


def _as_shards(name, whole):
    if name in COLUMN_SHARDED:
        k, n = whole.shape
        return whole.reshape(k, N_CHIPS, n // N_CHIPS).transpose(1, 0, 2)
    return whole.reshape(N_CHIPS, whole.shape[0] // N_CHIPS, whole.shape[1])


def _pack_small(values, extra=None):
    rows = [values[k] for k in ("ffn1_norm", "mix_norm", "ffn2_norm", "ple_norm")]
    tail = jnp.concatenate([values["forget_bias"], values["q_norm"], values["k_norm"]], axis=1)
    rows.append(jnp.pad(tail, ((0, 0), (0, D_MODEL - tail.shape[1]))))
    packed = jnp.concatenate(rows + [jnp.zeros((3, D_MODEL), F32)], axis=0)
    if extra is not None:
        packed = packed.at[LOSS_ROW, 0].set(extra)
    return packed


def _unpack_small(packed):
    out = {k: packed[r:r + 1] for k, r in SMALL_ROWS.items()}
    for k, (start, size) in SMALL_COLS.items():
        out[k] = packed[4:5, start:start + size]
    return out


def kernel(x, p, ffn1_norm, ffn1_w_gate, ffn1_w_up, ffn1_w_down, mix_norm, w_in, forget_bias, q_norm, k_norm, w_branch_fox, w_branch_sb, w_out, ffn2_norm, ffn2_w_gate, ffn2_w_up, ffn2_w_down, ple_norm, w_ple_gate, w_ple_proj, loss_target, m_ffn1_norm, m_ffn1_w_gate, m_ffn1_w_up, m_ffn1_w_down, m_mix_norm, m_w_in, m_forget_bias, m_q_norm, m_k_norm, m_w_branch_fox, m_w_branch_sb, m_w_out, m_ffn2_norm, m_ffn2_w_gate, m_ffn2_w_up, m_ffn2_w_down, m_ple_norm, m_w_ple_gate, m_w_ple_proj, v_ffn1_norm, v_ffn1_w_gate, v_ffn1_w_up, v_ffn1_w_down, v_mix_norm, v_w_in, v_forget_bias, v_q_norm, v_k_norm, v_w_branch_fox, v_w_branch_sb, v_w_out, v_ffn2_norm, v_ffn2_w_gate, v_ffn2_w_up, v_ffn2_w_down, v_ple_norm, v_w_ple_gate, v_w_ple_proj):
    args = dict(locals())
    weights = {k: args[k] for k in ORDER}
    moments_m = {k: args["m_" + k] for k in ORDER}
    moments_v = {k: args["v_" + k] for k in ORDER}

    c_idx = lax.axis_index("c").astype(jnp.int32).reshape(1)
    q_idx = (2 * lax.axis_index("x") + lax.axis_index("y")).astype(jnp.int32).reshape(1)
    own = {k: _place_own_shard(_stored(k, weights[k]), q_idx) for k in BIG}
    full = dict(zip(NEEDED_FIRST, _allgather_weights([own[k] for k in NEEDED_FIRST])))
    pending = {k: own[k] for k in BIG if k not in NEEDED_FIRST}
    small = {k: weights[k] for k in SMALL}

    def pair_sums(names, gw):
        slots = [gw[k] if k in KEPT_AS_SHARDS else _as_shards(k, gw[k]) for k in names]
        from_core = _exchange_pair_halves(slots)
        return [_add_pair(g, got, c_idx) for g, got in zip(slots, from_core)]

    early = [k for k in BIG if k not in READY_LAST]
    loss_sum, grad_x, gw, gs, parts = _local_grads(
        x[0], p[0, 0], loss_target[0], small, full, pending, lambda ready: (early, pair_sums(early, ready)))

    parts.update(zip(READY_LAST, _scatter_to_owner_chips(pair_sums(READY_LAST, gw))))
    grads_big = dict(zip(BIG, _join_halves([_add_chips(parts[k], c_idx) for k in BIG])))
    reduced = _allreduce_small(_pack_small(gs, extra=loss_sum[0, 0]))
    grads_small = _unpack_small(reduced)
    loss = reduced[LOSS_ROW, 0]

    grads, deltas, new_m, new_v = {}, {}, {}, {}
    for k in BIG:
        d, nm, nv = _adamw(_stored(k, weights[k]), grads_big[k], _stored(k, moments_m[k]), _stored(k, moments_v[k]))
        grads[k], deltas[k], new_m[k], new_v[k] = (_returned(k, t) for t in (grads_big[k], d, nm, nv))
    d_s, nm_s, nv_s = _adamw(_pack_small({k: weights[k] for k in SMALL}), reduced,
                             _pack_small({k: moments_m[k] for k in SMALL}),
                             _pack_small({k: moments_v[k] for k in SMALL}))
    for k in SMALL:
        grads[k] = grads_small[k]
    for name, packed in (("d", d_s), ("m", nm_s), ("v", nv_s)):
        target = {"d": deltas, "m": new_m, "v": new_v}[name]
        target.update(_unpack_small(packed))

    return (loss, grad_x[None], *[grads[k] for k in ORDER], *[deltas[k] for k in ORDER],
            *[new_m[k] for k in ORDER], *[new_v[k] for k in ORDER])
```

```python
import functools

import jax
import jax.numpy as jnp
from jax import lax
from jax.experimental import pallas as pl
from jax.experimental.pallas import tpu as pltpu

F32 = jnp.float32
BF16 = jnp.bfloat16

D_MODEL = 1024
D_FF = 2816
N_CHIPS = 4
FF_SHARD = D_FF // N_CHIPS
FFN_CHUNKS = 2
WGRAD_TOKENS = 4096
WGRAD_VMEM = 30 * 1024 * 1024
HEAD_DIM = 64
N_HEADS = 8
ATT_W = N_HEADS * HEAD_DIM
PAIR_W = 2 * HEAD_DIM
N_PAIRS = N_HEADS // 2
PLE_DIM = 256
IN_WIDTH = 3 * ATT_W + N_HEADS + 3 * ATT_W + 2 * D_MODEL
EPS = 1e-6
QK_SCALE = HEAD_DIM ** -0.5
LANES = 128
ATT_BLOCK = 256
FOX_Q_BLOCK = 512
SB_Q_BLOCK = 256
NEG_BIG = -1e30
EXP_UNDERFLOW = 110.0
MAX_REFERENCE_EXCESS = 40.0
NORM_BOUND_MARGIN = 1.02

ADAM_LR = 0.001
ADAM_B1 = 0.9
ADAM_B2 = 0.999
ADAM_EPS = 1e-08
ADAM_WD = 0.01
ADAM_STEP = 10

MESH = pl.DeviceIdType.MESH
MIB = 1024 * 1024


def _cparams(vmem_mib=48):
    return pltpu.CompilerParams(vmem_limit_bytes=vmem_mib * MIB)


def _dot(a, b):
    return jnp.dot(a, b, preferred_element_type=F32)


def _dot_tn(a, b):
    return lax.dot_general(a, b, (((0,), (0,)), ((), ())), preferred_element_type=F32)


def _dot_nt(a, b):
    return lax.dot_general(a, b, (((1,), (1,)), ((), ())), preferred_element_type=F32)


def _sigmoid(x):
    return 1.0 / (1.0 + jnp.exp(-x))


def _split2(x):
    hi = x.astype(BF16)
    lo = (x - hi.astype(F32)).astype(BF16)
    return hi, lo


def _dot_split2(x, m):
    hi, lo = _split2(x)
    return _dot(hi, m) + _dot(lo, m)


def _split3(x):
    hi = x.astype(BF16)
    rest = x - hi.astype(F32)
    mid = rest.astype(BF16)
    lo = (rest - mid.astype(F32)).astype(BF16)
    return hi, mid, lo


def _rms(x):
    r = lax.rsqrt(jnp.mean(x * x, axis=-1, keepdims=True) + EPS)
    return x * r, r


def _rms_bwd(dh, xn, r, g):
    dxn = dh * g
    return r * (dxn - xn * jnp.mean(dxn * xn, axis=-1, keepdims=True))


def _colsum(x):
    return jnp.sum(x, axis=0, keepdims=True)


def _row_block(rows, row_bytes, budget):
    best = None
    for t in range(8, rows + 1, 8):
        if rows % t == 0 and t * row_bytes <= budget:
            best = t
    return best if best is not None else rows


def _ffn_fwd(x, g, wg, wu, wd, gather=(), tm=1024):
    s_len = x.shape[0]
    n = len(gather)
    steps = s_len // tm

    def body(x_ref, g_ref, wg_ref, wu_ref, wd_ref, *rest):
        o_ref, a_ref, b_ref, u_ref = rest[n:n + 4]
        h_s, acc_s = rest[2 * n + 4:2 * n + 6]
        i = pl.program_id(0)
        j = pl.program_id(1)
        if n:
            start, finish = _gather_steps(rest[n + 4:2 * n + 4], *rest[2 * n + 6:])
            pl.when((i == 0) & (j == 0))(start)

        @pl.when(j == 0)
        def _():
            xn, _ = _rms(x_ref[...])
            h_s[...] = (xn * g_ref[...]).astype(BF16)
            acc_s[...] = jnp.zeros_like(acc_s)

        chunks = [pl.ds(r * (tm // FFN_CHUNKS), tm // FFN_CHUNKS) for r in range(FFN_CHUNKS)]
        pre = [(_dot_nt(h_s[rows, :], wg_ref[0]), _dot_nt(h_s[rows, :], wu_ref[0])) for rows in chunks]
        us = []
        for rows, (a, b) in zip(chunks, pre):
            a_ref[0, rows, :] = a.astype(BF16)
            b_ref[0, rows, :] = b.astype(BF16)
            u = (a * _sigmoid(a) * b).astype(BF16)
            u_ref[0, rows, :] = u
            us.append(u)
        for rows, u in zip(chunks, us):
            acc_s[rows, :] += _dot(u, wd_ref[0])

        @pl.when(j == N_CHIPS - 1)
        def _():
            o_ref[...] = x_ref[...] + 0.5 * acc_s[...]

        if n:
            pl.when((i == steps - 1) & (j == N_CHIPS - 1))(finish)

    return pl.pallas_call(
        body,
        grid=(steps, N_CHIPS),
        in_specs=[
            pl.BlockSpec((tm, D_MODEL), lambda i, j: (i, 0)),
            pl.BlockSpec((1, D_MODEL), lambda i, j: (0, 0)),
            pl.BlockSpec((1, FF_SHARD, D_MODEL), lambda i, j: (j, 0, 0)),
            pl.BlockSpec((1, FF_SHARD, D_MODEL), lambda i, j: (j, 0, 0)),
            pl.BlockSpec((1, FF_SHARD, D_MODEL), lambda i, j: (j, 0, 0)),
        ] + [ANY] * n,
        out_specs=[pl.BlockSpec((tm, D_MODEL), lambda i, j: (i, 0))]
        + [pl.BlockSpec((1, tm, FF_SHARD), lambda i, j: (j, i, 0))] * 3 + [ANY] * n,
        out_shape=[jax.ShapeDtypeStruct((s_len, D_MODEL), F32)]
        + [jax.ShapeDtypeStruct((N_CHIPS, s_len, FF_SHARD), BF16)] * 3
        + [jax.ShapeDtypeStruct(s.shape, s.dtype) for s in gather],
        input_output_aliases={5 + a: 4 + a for a in range(n)},
        scratch_shapes=[pltpu.VMEM((tm, D_MODEL), BF16), pltpu.VMEM((tm, D_MODEL), F32)]
        + (_gather_semaphores(n) if n else []),
        compiler_params=_cparams(56),
        name="ffn_fwd_gathering" if n else "ffn_fwd",
    )(x, g, wg, wu, wd, *gather)


def _ffn_bwd(x, d, g, a_pre, b_pre, wg, wu, wd, scatter=(), tm=512):
    s_len = x.shape[0]
    nb = s_len // tm
    n = len(scatter)

    def body(x_ref, d_ref, g_ref, a_ref, b_ref, wg_ref, wu_ref, wd_ref, *rest):
        dx_ref, da_ref, db_ref, h_ref, dbf_ref, dg_ref = rest[n:n + 6]
        dbf_s, dh_s = rest[2 * n + 6:2 * n + 8]
        i = pl.program_id(0)
        j = pl.program_id(1)
        if n:
            start, finish = _scatter_steps(rest[0:n], rest[n + 6:2 * n + 6], *rest[2 * n + 8:])
            pl.when((i == 0) & (j == 0))(start)

        @pl.when(j == 0)
        def _():
            xn, _ = _rms(x_ref[...])
            h_ref[...] = (xn * g_ref[...]).astype(BF16)
            dbf = d_ref[...].astype(BF16)
            dbf_s[...] = dbf
            dbf_ref[...] = dbf
            dh_s[...] = jnp.zeros_like(dh_s)

        @pl.when((i == 0) & (j == 0))
        def _():
            dg_ref[...] = jnp.zeros_like(dg_ref)

        chunks = [pl.ds(r * (tm // FFN_CHUNKS), tm // FFN_CHUNKS) for r in range(FFN_CHUNKS)]
        dus = [0.5 * _dot_nt(dbf_s[rows, :], wd_ref[0]) for rows in chunks]
        das, dbs = [], []
        for rows, du in zip(chunks, dus):
            a = a_ref[0, rows, :].astype(F32)
            b = b_ref[0, rows, :].astype(F32)
            s = _sigmoid(a)
            silu = a * s
            da = (du * b * (s * (1.0 + a * (1.0 - s)))).astype(BF16)
            db = (du * silu).astype(BF16)
            da_ref[0, rows, :] = da
            db_ref[0, rows, :] = db
            das.append(da)
            dbs.append(db)
        for rows, da, db in zip(chunks, das, dbs):
            dh_s[rows, :] += _dot(da, wg_ref[0]) + _dot(db, wu_ref[0])

        @pl.when(j == N_CHIPS - 1)
        def _():
            xn, r = _rms(x_ref[...])
            dh = dh_s[...]
            dx_ref[...] = d_ref[...] + _rms_bwd(dh, xn, r, g_ref[...])
            dg_ref[0:1, :] += _colsum(dh * xn)

        if n:
            pl.when((i == nb - 1) & (j == N_CHIPS - 1))(finish)

    row = lambda i, j: (i, 0)
    shard = lambda i, j: (j, 0, 0)
    act = lambda i, j: (j, i, 0)
    return pl.pallas_call(
        body,
        grid=(nb, N_CHIPS),
        in_specs=[
            pl.BlockSpec((tm, D_MODEL), row),
            pl.BlockSpec((tm, D_MODEL), row),
            pl.BlockSpec((1, D_MODEL), lambda i, j: (0, 0)),
            pl.BlockSpec((1, tm, FF_SHARD), act),
            pl.BlockSpec((1, tm, FF_SHARD), act),
            pl.BlockSpec((1, FF_SHARD, D_MODEL), shard),
            pl.BlockSpec((1, FF_SHARD, D_MODEL), shard),
            pl.BlockSpec((1, FF_SHARD, D_MODEL), shard),
        ] + [ANY] * n,
        out_specs=[
            pl.BlockSpec((tm, D_MODEL), row),
            pl.BlockSpec((1, tm, FF_SHARD), act),
            pl.BlockSpec((1, tm, FF_SHARD), act),
            pl.BlockSpec((tm, D_MODEL), row),
            pl.BlockSpec((tm, D_MODEL), row),
            pl.BlockSpec((8, D_MODEL), lambda i, j: (0, 0)),
        ] + [ANY] * n,
        out_shape=[
            jax.ShapeDtypeStruct((s_len, D_MODEL), F32),
            jax.ShapeDtypeStruct((N_CHIPS, s_len, FF_SHARD), BF16),
            jax.ShapeDtypeStruct((N_CHIPS, s_len, FF_SHARD), BF16),
            jax.ShapeDtypeStruct((s_len, D_MODEL), BF16),
            jax.ShapeDtypeStruct((s_len, D_MODEL), BF16),
            jax.ShapeDtypeStruct((8, D_MODEL), F32),
        ] + [jax.ShapeDtypeStruct(s.shape, s.dtype) for s in scatter],
        scratch_shapes=[
            pltpu.VMEM((tm, D_MODEL), BF16),
            pltpu.VMEM((tm, D_MODEL), F32),
        ] + (_scatter_semaphores(n) if n else []),
        compiler_params=_cparams(56),
        name="ffn_bwd_scattering" if n else "ffn_bwd",
    )(x, d, g, a_pre, b_pre, wg, wu, wd, *scatter)


def _wgrad(a, b, scale=1.0, name="wgrad"):
    na, s_len, k_dim = a.shape
    nb, _, n_dim = b.shape
    n = max(na, nb)
    ts = WGRAD_TOKENS
    while ts > 512 and (ts > s_len or 2 * ts * (k_dim * a.dtype.itemsize + n_dim * b.dtype.itemsize) > WGRAD_VMEM):
        ts //= 2
    steps = s_len // ts

    def body(a_ref, b_ref, o_ref, acc_s):
        s = pl.program_id(1)

        @pl.when(s == 0)
        def _():
            acc_s[...] = jnp.zeros_like(acc_s)

        acc_s[...] += _dot_tn(a_ref[0].astype(BF16), b_ref[0].astype(BF16))

        @pl.when(s == steps - 1)
        def _():
            o_ref[0] = (acc_s[...] * scale).astype(BF16)

    a_map = (lambda m, s: (m, s, 0)) if na > 1 else (lambda m, s: (0, s, 0))
    b_map = (lambda m, s: (m, s, 0)) if nb > 1 else (lambda m, s: (0, s, 0))
    return pl.pallas_call(
        body,
        grid=(n, steps),
        in_specs=[pl.BlockSpec((1, ts, k_dim), a_map), pl.BlockSpec((1, ts, n_dim), b_map)],
        out_specs=pl.BlockSpec((1, k_dim, n_dim), lambda m, s: (m, 0, 0)),
        out_shape=jax.ShapeDtypeStruct((n, k_dim, n_dim), BF16),
        scratch_shapes=[pltpu.VMEM((k_dim, n_dim), F32)],
        compiler_params=_cparams(56),
        name=name,
    )(a, b)


def _head_sum_matrices():
    lane = lax.broadcasted_iota(jnp.int32, (ATT_W, LANES), 0) // HEAD_DIM
    col = lax.broadcasted_iota(jnp.int32, (ATT_W, LANES), 1)
    bd = (lane == col).astype(BF16)
    return bd, bd.T


def _head_mean(t, bd, bd_t):
    per_head = _dot_split2(t, bd) * (1.0 / HEAD_DIM)
    return _dot_split2(per_head, bd_t)


def _head_rms(x, bd, bd_t):
    per_head = _dot_split2(x * x, bd) * (1.0 / HEAD_DIM)
    r = lax.rsqrt(per_head + EPS)
    rw = _dot_split2(r, bd_t)
    return x * rw, rw


def _log_sigmoid(z):
    return jnp.minimum(z, 0.0) - jnp.log(1.0 + jnp.exp(-jnp.abs(z)))


def _inproj_fwd(x1, g, w_fox, w_fl, w_sb, w_gates, bias, qn, kn, bd, bd_t, tm=512):
    s_len = x1.shape[0]

    def body(x_ref, g_ref, wf_ref, wl_ref, ws_ref, wg_ref, bias_ref, qn_ref, kn_ref, bd_ref, bdt_ref,
             fq_ref, fk_ref, qs_ref, kf_ref, vf_ref, logf_ref, sq_ref, sk_ref, sv_ref, gates_ref):
        xn, _ = _rms(x_ref[...])
        h = (xn * g_ref[...]).astype(BF16)
        zf = _dot(h, wf_ref[...])
        fq = zf[:, 0:ATT_W]
        fk = zf[:, ATT_W:2 * ATT_W]
        fq_ref[...] = fq
        fk_ref[...] = fk
        bd_m = bd_ref[...]
        bdt_m = bdt_ref[...]
        fqn, _ = _head_rms(fq, bd_m, bdt_m)
        fkn, _ = _head_rms(fk, bd_m, bdt_m)
        qs_ref[...] = (fqn * qn_ref[...]).astype(BF16) * QK_SCALE
        kf_ref[...] = (fkn * kn_ref[...]).astype(BF16)
        vf_ref[...] = zf[:, 2 * ATT_W:3 * ATT_W].astype(BF16)
        logf_ref[...] = _log_sigmoid(_dot(h, wl_ref[...]) + bias_ref[...])
        zs = _dot(h, ws_ref[...])
        sq_ref[...] = zs[:, 0:ATT_W].astype(BF16) * QK_SCALE
        sk_ref[...] = zs[:, ATT_W:2 * ATT_W].astype(BF16)
        sv_ref[...] = zs[:, 2 * ATT_W:3 * ATT_W].astype(BF16)
        gates_ref[...] = _dot(h, wg_ref[...]).astype(BF16)

    row = lambda i: (i, 0)
    full = lambda i: (0, 0)
    att = lambda dt: jax.ShapeDtypeStruct((s_len, ATT_W), dt)
    return pl.pallas_call(
        body,
        grid=(s_len // tm,),
        in_specs=[
            pl.BlockSpec((tm, D_MODEL), row),
            pl.BlockSpec((1, D_MODEL), full),
            pl.BlockSpec((D_MODEL, 3 * ATT_W), full),
            pl.BlockSpec((D_MODEL, LANES), full),
            pl.BlockSpec((D_MODEL, 3 * ATT_W), full),
            pl.BlockSpec((D_MODEL, 2 * D_MODEL), full),
            pl.BlockSpec((1, LANES), full),
            pl.BlockSpec((1, ATT_W), full),
            pl.BlockSpec((1, ATT_W), full),
            pl.BlockSpec((ATT_W, LANES), full),
            pl.BlockSpec((LANES, ATT_W), full),
        ],
        out_specs=[
            pl.BlockSpec((tm, ATT_W), row), pl.BlockSpec((tm, ATT_W), row),
            pl.BlockSpec((tm, ATT_W), row), pl.BlockSpec((tm, ATT_W), row), pl.BlockSpec((tm, ATT_W), row),
            pl.BlockSpec((tm, LANES), row),
            pl.BlockSpec((tm, ATT_W), row), pl.BlockSpec((tm, ATT_W), row), pl.BlockSpec((tm, ATT_W), row),
            pl.BlockSpec((tm, 2 * D_MODEL), row),
        ],
        out_shape=[
            att(F32), att(F32), att(BF16), att(BF16), att(BF16),
            jax.ShapeDtypeStruct((s_len, LANES), F32),
            att(BF16), att(BF16), att(BF16),
            jax.ShapeDtypeStruct((s_len, 2 * D_MODEL), BF16),
        ],
        compiler_params=_cparams(56),
        name="inproj_fwd",
    )(x1, g, w_fox, w_fl, w_sb, w_gates, bias, qn, kn, bd, bd_t)


def _tri(n, kind):
    r = lax.broadcasted_iota(jnp.int32, (n, n), 0)
    c = lax.broadcasted_iota(jnp.int32, (n, n), 1)
    m = {"row_ge_col": r >= c, "row_le_col": r <= c, "row_gt_col": r > c, "row_lt_col": r < c}[kind]
    return m.astype(BF16)


def _cumsum_rows(x, reverse, spread=None, tm=512):
    s_len = x.shape[0]
    nb = s_len // tm
    tri = _tri(tm, "row_le_col" if reverse else "row_ge_col")
    edge = 0 if reverse else tm - 1

    def body(x_ref, tri_ref, *rest):
        spread_ref, o_ref, wide_ref, carry_s = rest if spread is not None else (None, rest[0], None, rest[1])

        @pl.when(pl.program_id(0) == 0)
        def _():
            carry_s[...] = jnp.zeros_like(carry_s)

        hi, mid, lo = _split3(x_ref[...])
        t = tri_ref[...]
        y = _dot(t, hi) + _dot(t, mid) + _dot(t, lo) + carry_s[...]
        o_ref[...] = y
        carry_s[...] = y[edge:edge + 1, :]
        if spread is not None:
            hi, mid, lo = _split3(y)
            m = spread_ref[...]
            wide_ref[...] = _dot(hi, m) + _dot(mid, m) + _dot(lo, m)

    order = (lambda i: (nb - 1 - i, 0)) if reverse else (lambda i: (i, 0))
    narrow = pl.BlockSpec((tm, LANES), order)
    in_specs, out_specs = [narrow, pl.BlockSpec((tm, tm), lambda i: (0, 0))], [narrow]
    out_shape = [jax.ShapeDtypeStruct((s_len, LANES), F32)]
    if spread is not None:
        width = spread.shape[1]
        in_specs.append(pl.BlockSpec((LANES, width), lambda i: (0, 0)))
        out_specs.append(pl.BlockSpec((tm, width), order))
        out_shape.append(jax.ShapeDtypeStruct((s_len, width), F32))
    out = pl.pallas_call(
        body,
        grid=(nb,),
        in_specs=in_specs,
        out_specs=out_specs,
        out_shape=out_shape,
        scratch_shapes=[pltpu.VMEM((1, LANES), F32)],
        name="cumsum_rev" if reverse else "cumsum_fwd",
    )(*([x, tri] if spread is None else [x, tri, spread]))
    return out[0] if spread is None else out


def _unblocked_t(t4):
    _, nb, _, blk = t4.shape
    return t4.transpose(1, 3, 0, 2).reshape(nb * blk, ATT_W)


def _transposed_spec(tm):
    return pl.BlockSpec((N_PAIRS, tm // ATT_BLOCK, PAIR_W, ATT_BLOCK), lambda i: (0, i, 0, 0))


def _rows_of_transposed(ref):
    return jnp.concatenate(
        [jnp.concatenate([ref[p, b].T for p in range(N_PAIRS)], axis=1) for b in range(ref.shape[1])], axis=0)


def _blocked_rows(t, blk):
    return t.reshape(t.shape[0] // blk, blk, t.shape[1])


def _pair_rows_t(f8, blk):
    nb = f8.shape[0] // blk
    t = f8.reshape(nb, blk, N_PAIRS, 2).transpose(2, 0, 3, 1)
    return jnp.pad(t, ((0, 0), (0, 0), (0, 6), (0, 0)))


def _unpair_rows_t(t4):
    _, nb, _, blk = t4.shape
    return t4[:, :, 0:2, :].transpose(1, 3, 0, 2).reshape(nb * blk, N_HEADS)


def _head_masks(tq):
    lane = lax.broadcasted_iota(jnp.int32, (tq, PAIR_W), 1)
    return lane < HEAD_DIM


def _causal_mask(tq, tk, offset, strict):
    d = lax.broadcasted_iota(jnp.int32, (tq, tk), 1) - lax.broadcasted_iota(jnp.int32, (tq, tk), 0)
    return (d < offset) if strict else (d <= offset)


def _heads_of(ref, first):
    t = ref[...]
    zero = jnp.zeros_like(t)
    return [jnp.where(first, t, zero), jnp.where(first, zero, t)]


def _head_cols(ref):
    t = ref[...]
    return [t[:, 0:1], t[:, HEAD_DIM:HEAD_DIM + 1]]


def _att_specs(s_len, tq):
    tk = ATT_BLOCK
    nq, nk = s_len // tq, s_len // tk
    return dict(
        nq=nq,
        q=pl.BlockSpec((tq, PAIR_W), lambda p, i: (i, p)),
        k_t=pl.BlockSpec((1, nk, PAIR_W, tk), lambda p, i: (p, 0, 0, 0)),
        k_rows=pl.BlockSpec((nk, tk, PAIR_W), lambda p, i: (0, 0, p)),
        f_t=pl.BlockSpec((1, nk, 8, tk), lambda p, i: (p, 0, 0, 0)),
        first=pl.BlockSpec((1, 1, 8, LANES), lambda p, i: (p, i, 0, 0)),
        wide=jax.ShapeDtypeStruct((s_len, ATT_W), F32),
        k_t_out=jax.ShapeDtypeStruct((N_PAIRS, nk, PAIR_W, tk), F32),
        f_t_out=jax.ShapeDtypeStruct((N_PAIRS, nk, 8, tk), F32),
        first_out=jax.ShapeDtypeStruct((N_PAIRS, nq, 8, LANES), F32),
        acc=pltpu.VMEM((2, tq, PAIR_W), F32),
    )


def _first_block(first_ref, limit):
    return jnp.clip(jnp.max(first_ref[0, 0]).astype(jnp.int32), 0, limit)


def _normed_dot_bound(qn, kn):
    stretch = jnp.max(jnp.abs(qn)) * jnp.max(jnp.abs(kn))
    return (stretch * (HEAD_DIM * QK_SCALE * NORM_BOUND_MARGIN)).reshape(1)


def _fox_fwd(qs, k3, v3, fw, ft4, dot_bound):
    tq, tk = FOX_Q_BLOCK, ATT_BLOCK
    sp = _att_specs(qs.shape[0], tq)
    ratio, nk = tq // tk, qs.shape[0] // tk
    scalars = jnp.concatenate([ft4[:, :, :2, tk - 1].reshape(-1), dot_bound])

    def body(fend_ref, q_ref, k_ref, v_ref, fw_ref, ft_ref, y_ref, lse_ref, first_ref,
             acc_ref, max_ref, sum_ref):
        pair, i = pl.program_id(0), pl.program_id(1)
        first = _head_masks(tq)
        qh = _heads_of(q_ref, first)
        fqh = _head_cols(fw_ref)
        reach = [fqh[n] + fend_ref[N_PAIRS * nk * 2] for n in range(2)]

        def logits(j, shift, r0=0, diag=False):
            k, fk = k_ref[j], ft_ref[0, j]
            raw = [_dot_nt(qh[n][r0:], k) for n in range(2)]
            out = []
            for n in range(2):
                s = raw[n] + (shift[n][r0:] - fk[n:n + 1, :])
                if diag:
                    s = jnp.where(_causal_mask(tq - r0, tk, 0, strict=False), s, NEG_BIG)
                out.append(s)
            return out

        def max_pass(j, r0=0, diag=False, assign=False):
            ss = logits(j, fqh, r0, diag)
            for n in range(2):
                max_ref[n, r0:] = ss[n] if assign else jnp.maximum(max_ref[n, r0:], ss[n])

        def sum_pass(j, shift, r0=0, diag=False, assign=False):
            ps = [jnp.exp(s) for s in logits(j, shift, r0, diag)]
            v = v_ref[j]
            for n in range(2):
                sum_ref[n, r0:] = ps[n] if assign else sum_ref[n, r0:] + ps[n]
            for n in range(2):
                pv = _dot(ps[n].astype(BF16), v)
                acc_ref[n, r0:] = pv if assign else acc_ref[n, r0:] + pv

        for d in range(ratio):
            max_pass(ratio * i + d, d * tk, True, d == 0)

        m_diag = [jnp.max(max_ref[n], axis=-1, keepdims=True) for n in range(2)]
        slack = [jnp.max(reach[n] - m_diag[n]) for n in range(2)]

        def f_end(j, n):
            return fend_ref[(pair * nk + jnp.maximum(j, 0)) * 2 + n]

        def block_matters(j):
            gap = jnp.maximum(slack[0] - f_end(j, 0), slack[1] - f_end(j, 1))
            return (j >= 0) & (gap > -EXP_UNDERFLOW)

        last_left = ratio * i - 1
        j_first = lax.while_loop(block_matters, lambda j: j - 1, last_left) + 1

        bound = [reach[n] - f_end(last_left, n) for n in range(2)]
        excess = jnp.maximum(jnp.max(bound[0] - m_diag[0]), jnp.max(bound[1] - m_diag[1]))
        exact = excess > MAX_REFERENCE_EXCESS

        def exact_max():
            def one_max(j, c):
                max_pass(j)
                return c
            lax.fori_loop(j_first, ratio * i, one_max, 0)
            return [jnp.max(max_ref[n], axis=-1, keepdims=True) for n in range(2)]

        def bounded_max():
            walked_left = j_first < ratio * i
            return [jnp.maximum(m_diag[n], jnp.where(walked_left, bound[n], NEG_BIG)) for n in range(2)]

        m = lax.cond(exact, exact_max, bounded_max)
        shift = [fqh[n] - m[n] for n in range(2)]

        for d in range(ratio):
            sum_pass(ratio * i + d, shift, d * tk, True, d == 0)

        def one(j, c):
            sum_pass(j, shift)
            return c
        lax.fori_loop(j_first, ratio * i, one, 0)
        l = [jnp.sum(sum_ref[n], axis=-1, keepdims=True) for n in range(2)]
        y_ref[...] = jnp.where(first, acc_ref[0] / l[0], acc_ref[1] / l[1])
        lse_ref[...] = jnp.where(first, m[0] + jnp.log(l[0]), m[1] + jnp.log(l[1]))
        first_ref[...] = jnp.ones(first_ref.shape, F32) * j_first.astype(F32)

    tile = pltpu.VMEM((2, tq, tk), F32)
    return pl.pallas_call(
        body,
        grid=(N_PAIRS, sp["nq"]),
        in_specs=[pl.BlockSpec(memory_space=pltpu.SMEM), sp["q"], sp["k_rows"], sp["k_rows"], sp["q"], sp["f_t"]],
        out_specs=[sp["q"], sp["q"], sp["first"]],
        out_shape=[sp["wide"], sp["wide"], sp["first_out"]],
        scratch_shapes=[sp["acc"], tile, tile],
        compiler_params=_cparams(56),
        name="fox_fwd",
    )(scalars, qs, k3, v3, fw, ft4)


def _fox_bwd(qs, k3, v3, dy, y, lse, fw, ft4, first_block):
    tq, tk = FOX_Q_BLOCK, ATT_BLOCK
    sp = _att_specs(qs.shape[0], tq)
    ratio = tq // tk

    def body(q_ref, k_ref, v_ref, dy_ref, y_ref, lse_ref, fw_ref, ft_ref, first_ref,
             dq_ref, dfq_ref, dkt_ref, dvt_ref, dft_ref, acc_ref):
        i = pl.program_id(1)

        @pl.when(i == 0)
        def _():
            dkt_ref[...] = jnp.zeros_like(dkt_ref)
            dvt_ref[...] = jnp.zeros_like(dvt_ref)
            dft_ref[...] = jnp.zeros_like(dft_ref)

        first = _head_masks(tq)
        qh = _heads_of(q_ref, first)
        dyv = dy_ref[...]
        dyb = dyv.astype(BF16)
        zero = jnp.zeros_like(dyb)
        dyh = [jnp.where(first, dyb, zero), jnp.where(first, zero, dyb)]
        prod = dyv * y_ref[...]
        zf = jnp.zeros_like(prod)
        delta = [jnp.sum(jnp.where(first, prod, zf), axis=-1, keepdims=True),
                 jnp.sum(jnp.where(first, zf, prod), axis=-1, keepdims=True)]
        fqh = _head_cols(fw_ref)
        lseh = _head_cols(lse_ref)
        shift = [fqh[n] - lseh[n] for n in range(2)]
        acc_ref[...] = jnp.zeros_like(acc_ref)

        def block(j, rows, r0=0, diag=False):
            mask = _causal_mask(tq - r0, tk, 0, strict=False) if diag else None
            k, v, fk = k_ref[j], v_ref[j], ft_ref[0, j]
            q_part, dy_part = [t[r0:] for t in qh], [t[r0:] for t in dyh]
            logits = [_dot_nt(q_part[n], k) for n in range(2)]
            dps = [_dot_nt(dy_part[n], v) for n in range(2)]
            pbs, dsbs, out = [], [], []
            for n in range(2):
                p = jnp.exp(logits[n] + (shift[n][r0:] - fk[n:n + 1, :]))
                if diag:
                    p = jnp.where(mask, p, 0.0)
                ds = p * (dps[n] - delta[n][r0:])
                pbs.append(p.astype(BF16))
                dsbs.append(ds.astype(BF16))
                row_sum = jnp.sum(ds, axis=-1, keepdims=True)
                if r0:
                    row_sum = jnp.concatenate([jnp.zeros((r0, 1), F32), row_sum], axis=0)
                out.append(rows[n] + row_sum)
                dft_ref[0, j, n:n + 1, :] -= _colsum(ds)
            for n in range(2):
                acc_ref[n, r0:] += _dot(dsbs[n], k)
            dkt_ref[0, j] += _dot_tn(q_part[0], dsbs[0]) + _dot_tn(q_part[1], dsbs[1])
            dvt_ref[0, j] += _dot_tn(dy_part[0], pbs[0]) + _dot_tn(dy_part[1], pbs[1])
            return tuple(out)

        rows = (jnp.zeros((tq, 1), F32),) * 2
        rows = lax.fori_loop(_first_block(first_ref, ratio * i), ratio * i, lambda j, c: block(j, c), rows)
        for d in range(ratio):
            rows = block(ratio * i + d, rows, d * tk, True)
        dq_ref[...] = jnp.where(first, acc_ref[0], acc_ref[1])
        lane = lax.broadcasted_iota(jnp.int32, (tq, 8), 1)
        dfq_ref[0] = jnp.where(lane == 0, rows[0], jnp.where(lane == 1, rows[1], 0.0))

    return pl.pallas_call(
        body,
        grid=(N_PAIRS, sp["nq"]),
        in_specs=[sp["q"], sp["k_rows"], sp["k_rows"], sp["q"], sp["q"], sp["q"], sp["q"], sp["f_t"], sp["first"]],
        out_specs=[sp["q"], pl.BlockSpec((1, tq, 8), lambda p, i: (p, i, 0)), sp["k_t"], sp["k_t"], sp["f_t"]],
        out_shape=[sp["wide"], jax.ShapeDtypeStruct((N_PAIRS, qs.shape[0], 8), F32),
                   sp["k_t_out"], sp["k_t_out"], sp["f_t_out"]],
        scratch_shapes=[sp["acc"]],
        compiler_params=_cparams(56),
        name="fox_bwd",
    )(qs, k3, v3, dy, y, lse, fw, ft4, first_block)


SIGN_BIT = 0x80000000


def _sb_terms(z, mask, diag):
    neg_abs = pltpu.bitcast(pltpu.bitcast(z, jnp.uint32) | jnp.uint32(SIGN_BIT), F32)
    lb = jnp.minimum(z, 0.0) - jnp.log(1.0 + jnp.exp(neg_abs))
    l1m = lb - z
    if diag:
        l1m = jnp.where(mask, l1m, 0.0)
    return lb, l1m


def _dot_split2_stacked(x, m2):
    hi, lo = _split2(x)
    return _dot(jnp.concatenate([hi, lo], axis=1), m2)


def _tri_stacked(kind):
    t = _tri(ATT_BLOCK, kind)
    return jnp.concatenate([t, t], axis=0)


def _sb_fwd(qs, k3, v3):
    tq, tk = SB_Q_BLOCK, ATT_BLOCK
    sp = _att_specs(qs.shape[0], tq)
    ratio = tq // tk
    upper = _tri_stacked("row_gt_col")

    def body(q_ref, k_ref, v_ref, u_ref, y_ref, rtot_ref, first_ref, acc_ref):
        i = pl.program_id(1)
        first = _head_masks(tq)
        qh = _heads_of(q_ref, first)
        u = u_ref[...]
        acc_ref[...] = jnp.zeros_like(acc_ref)

        def block(j, rs, diag):
            mask = _causal_mask(tq, tk, i * tq - j * tk, strict=True) if diag else None
            k, v = k_ref[j], v_ref[j]
            logits = [_dot_nt(qh[n], k) for n in range(2)]
            terms = [_sb_terms(z, mask, diag) for z in logits]
            right = [_dot_split2_stacked(l1m, u) for _, l1m in terms]
            weights = []
            for n in range(2):
                a = jnp.exp(terms[n][0] + right[n] + rs[n])
                if diag:
                    a = jnp.where(mask, a, 0.0)
                weights.append(a.astype(BF16))
            for n in range(2):
                acc_ref[n] += _dot(weights[n], v)
            return tuple(rs[n] + jnp.sum(terms[n][1], axis=-1, keepdims=True) for n in range(2))

        rs = (jnp.zeros((tq, 1), F32),) * 2
        for d in range(ratio):
            rs = block(ratio * i + (ratio - 1 - d), rs, True)

        def block_matters(c):
            j, r0, r1 = c
            return (j >= 0) & (jnp.max(jnp.maximum(r0, r1)) > -EXP_UNDERFLOW)

        def walk_left(c):
            j, r0, r1 = c
            r0, r1 = block(j, (r0, r1), False)
            return j - 1, r0, r1

        j, r0, r1 = lax.while_loop(block_matters, walk_left, (ratio * i - 1, rs[0], rs[1]))
        y_ref[...] = jnp.where(first, acc_ref[0], acc_ref[1])
        rtot_ref[...] = jnp.where(first, r0, r1)
        first_ref[...] = jnp.ones(first_ref.shape, F32) * (j + 1).astype(F32)

    return pl.pallas_call(
        body,
        grid=(N_PAIRS, sp["nq"]),
        in_specs=[sp["q"], sp["k_rows"], sp["k_rows"], pl.BlockSpec((2 * tk, tk), lambda p, i: (0, 0))],
        out_specs=[sp["q"], sp["q"], sp["first"]],
        out_shape=[sp["wide"], sp["wide"], sp["first_out"]],
        scratch_shapes=[sp["acc"]],
        compiler_params=_cparams(56),
        name="sb_fwd",
    )(qs, k3, v3, upper)


def _sb_bwd(qs, k3, v3, dy, rtot, first_block):
    tq, tk = SB_Q_BLOCK, ATT_BLOCK
    sp = _att_specs(qs.shape[0], tq)
    ratio = tq // tk
    lower_in = _tri_stacked("row_le_col")
    lower = _tri(tk, "row_lt_col")

    def body(q_ref, k_ref, v_ref, dy_ref, rtot_ref, first_ref, li_ref, l_ref, dq_ref, dkt_ref, dvt_ref, acc_ref):
        i = pl.program_id(1)

        @pl.when(i == 0)
        def _():
            dkt_ref[...] = jnp.zeros_like(dkt_ref)
            dvt_ref[...] = jnp.zeros_like(dvt_ref)

        first = _head_masks(tq)
        qh = _heads_of(q_ref, first)
        dyb = dy_ref[...].astype(BF16)
        zero = jnp.zeros_like(dyb)
        dyh = [jnp.where(first, dyb, zero), jnp.where(first, zero, dyb)]
        rtoth = _head_cols(rtot_ref)
        li = li_ref[...]
        lo_tri = l_ref[...]
        acc_ref[...] = jnp.zeros_like(acc_ref)

        def block(j, carry, diag):
            mask = _causal_mask(tq, tk, i * tq - j * tk, strict=True) if diag else None
            k, v = k_ref[j], v_ref[j]
            logits = [_dot_nt(qh[n], k) for n in range(2)]
            das = [_dot_nt(dyh[n], v) for n in range(2)]
            terms = [_sb_terms(z, mask, diag) for z in logits]
            upto = [_dot_split2_stacked(l1m, li) for _, l1m in terms]
            des, weights = [], []
            for n in range(2):
                a = jnp.exp(terms[n][0] + ((rtoth[n] - carry[2 * n]) - upto[n]))
                if diag:
                    a = jnp.where(mask, a, 0.0)
                des.append(a * das[n])
                weights.append(a.astype(BF16))
            lefts = [_dot(de.astype(BF16), lo_tri) for de in des]
            dzbs, out = [], []
            for n in range(2):
                beta = jnp.exp(terms[n][0])
                dz = des[n] - (des[n] + (carry[2 * n + 1] + lefts[n])) * beta
                if diag:
                    dz = jnp.where(mask, dz, 0.0)
                dzbs.append(dz.astype(BF16))
                out += [carry[2 * n] + jnp.sum(terms[n][1], axis=-1, keepdims=True),
                        carry[2 * n + 1] + jnp.sum(des[n], axis=-1, keepdims=True)]
            for n in range(2):
                acc_ref[n] += _dot(dzbs[n], k)
            dkt_ref[0, j] += _dot_tn(qh[0], dzbs[0]) + _dot_tn(qh[1], dzbs[1])
            dvt_ref[0, j] += _dot_tn(dyh[0], weights[0]) + _dot_tn(dyh[1], weights[1])
            return tuple(out)

        carry = (jnp.zeros((tq, 1), F32),) * 4
        carry = lax.fori_loop(_first_block(first_ref, ratio * i), ratio * i, lambda j, c: block(j, c, False), carry)
        for d in range(ratio):
            carry = block(ratio * i + d, carry, True)
        dq_ref[...] = jnp.where(first, acc_ref[0], acc_ref[1])

    return pl.pallas_call(
        body,
        grid=(N_PAIRS, sp["nq"]),
        in_specs=[sp["q"], sp["k_rows"], sp["k_rows"], sp["q"], sp["q"], sp["first"],
                  pl.BlockSpec((2 * tk, tk), lambda p, i: (0, 0)), pl.BlockSpec((tk, tk), lambda p, i: (0, 0))],
        out_specs=[sp["q"], sp["k_t"], sp["k_t"]],
        out_shape=[sp["wide"], sp["k_t_out"], sp["k_t_out"]],
        scratch_shapes=[sp["acc"]],
        compiler_params=_cparams(56),
        name="sb_bwd",
    )(qs, k3, v3, dy, rtot, first_block, lower_in, lower)


def _merge_fwd(x1, gates, y_fox, y_sb, w_bf, w_bs, w_out, tm=512):
    s_len = x1.shape[0]

    def body(x_ref, g_ref, yf_ref, ys_ref, wbf_ref, wbs_ref, wo_ref, o_ref):
        g = g_ref[...].astype(F32)
        of = _dot(yf_ref[...].astype(BF16), wbf_ref[...])
        os_ = _dot(ys_ref[...].astype(BF16), wbs_ref[...])
        merged = _sigmoid(g[:, 0:D_MODEL]) * of + _sigmoid(g[:, D_MODEL:]) * os_
        o_ref[...] = x_ref[...] + _dot(merged.astype(BF16), wo_ref[...])

    row = lambda i: (i, 0)
    full = lambda i: (0, 0)
    return pl.pallas_call(
        body,
        grid=(s_len // tm,),
        in_specs=[
            pl.BlockSpec((tm, D_MODEL), row),
            pl.BlockSpec((tm, 2 * D_MODEL), row),
            pl.BlockSpec((tm, ATT_W), row),
            pl.BlockSpec((tm, ATT_W), row),
            pl.BlockSpec((ATT_W, D_MODEL), full),
            pl.BlockSpec((ATT_W, D_MODEL), full),
            pl.BlockSpec((D_MODEL, D_MODEL), full),
        ],
        out_specs=pl.BlockSpec((tm, D_MODEL), row),
        out_shape=jax.ShapeDtypeStruct((s_len, D_MODEL), F32),
        compiler_params=_cparams(48),
        name="merge_fwd",
    )(x1, gates, y_fox, y_sb, w_bf, w_bs, w_out)


def _merge_bwd(dx2, gates, y_fox, y_sb, w_bf, w_bs, w_out, tm=512):
    s_len = dx2.shape[0]

    def body(d_ref, g_ref, yf_ref, ys_ref, wbf_ref, wbs_ref, wo_ref,
             dyf_ref, dys_ref, dg_ref, dof_ref, dos_ref, m_ref, dbf_ref):
        dbf = d_ref[...].astype(BF16)
        dbf_ref[...] = dbf
        dm = _dot_nt(dbf, wo_ref[...])
        g = g_ref[...].astype(F32)
        of = _dot(yf_ref[...].astype(BF16), wbf_ref[...])
        os_ = _dot(ys_ref[...].astype(BF16), wbs_ref[...])
        sf = _sigmoid(g[:, 0:D_MODEL])
        ss = _sigmoid(g[:, D_MODEL:])
        m_ref[...] = (sf * of + ss * os_).astype(BF16)
        d_of = (dm * sf).astype(BF16)
        d_os = (dm * ss).astype(BF16)
        dof_ref[...] = d_of
        dos_ref[...] = d_os
        dg_ref[:, 0:D_MODEL] = (dm * of * sf * (1.0 - sf)).astype(BF16)
        dg_ref[:, D_MODEL:] = (dm * os_ * ss * (1.0 - ss)).astype(BF16)
        dyf_ref[...] = _dot_nt(d_of, wbf_ref[...])
        dys_ref[...] = _dot_nt(d_os, wbs_ref[...])

    row = lambda i: (i, 0)
    full = lambda i: (0, 0)
    return pl.pallas_call(
        body,
        grid=(s_len // tm,),
        in_specs=[
            pl.BlockSpec((tm, D_MODEL), row),
            pl.BlockSpec((tm, 2 * D_MODEL), row),
            pl.BlockSpec((tm, ATT_W), row),
            pl.BlockSpec((tm, ATT_W), row),
            pl.BlockSpec((ATT_W, D_MODEL), full),
            pl.BlockSpec((ATT_W, D_MODEL), full),
            pl.BlockSpec((D_MODEL, D_MODEL), full),
        ],
        out_specs=[
            pl.BlockSpec((tm, ATT_W), row), pl.BlockSpec((tm, ATT_W), row),
            pl.BlockSpec((tm, 2 * D_MODEL), row),
            pl.BlockSpec((tm, D_MODEL), row), pl.BlockSpec((tm, D_MODEL), row),
            pl.BlockSpec((tm, D_MODEL), row), pl.BlockSpec((tm, D_MODEL), row),
        ],
        out_shape=[
            jax.ShapeDtypeStruct((s_len, ATT_W), F32), jax.ShapeDtypeStruct((s_len, ATT_W), F32),
            jax.ShapeDtypeStruct((s_len, 2 * D_MODEL), BF16),
            jax.ShapeDtypeStruct((s_len, D_MODEL), BF16), jax.ShapeDtypeStruct((s_len, D_MODEL), BF16),
            jax.ShapeDtypeStruct((s_len, D_MODEL), BF16), jax.ShapeDtypeStruct((s_len, D_MODEL), BF16),
        ],
        compiler_params=_cparams(56),
        name="merge_bwd",
    )(dx2, gates, y_fox, y_sb, w_bf, w_bs, w_out)


def _ple_loss(x3, p, g, w_pg, w_pp, target, tm=512):
    s_len = x3.shape[0]
    inv_d = 1.0 / D_MODEL

    def body(x_ref, p_ref, g_ref, wpg_ref, wpp_ref, t_ref,
             dx_ref, du_ref, dt_ref, hn_ref, dg_ref, loss_ref):
        @pl.when(pl.program_id(0) == 0)
        def _():
            dg_ref[...] = jnp.zeros_like(dg_ref)
            loss_ref[...] = jnp.zeros_like(loss_ref)

        x = x_ref[...]
        xn, r = _rms(x)
        gain = g_ref[...]
        hn = (xn * gain).astype(BF16)
        hn_ref[...] = hn
        sg = _sigmoid(_dot(hn, wpg_ref[...]))
        t = _dot(p_ref[...].astype(BF16), wpp_ref[...])
        err = x + sg * t - t_ref[...]
        sq = jnp.sum(_colsum(err * err), axis=-1, keepdims=True)
        loss_ref[...] += (0.5 * inv_d) * sq
        dy = err * inv_d
        du = (dy * t * sg * (1.0 - sg)).astype(BF16)
        du_ref[...] = du
        dt_ref[...] = (dy * sg).astype(BF16)
        dh = _dot_nt(du, wpg_ref[...])
        dx_ref[...] = dy + _rms_bwd(dh, xn, r, gain)
        dg_ref[0:1, :] += _colsum(dh * xn)

    row = lambda i: (i, 0)
    full = lambda i: (0, 0)
    bf = jax.ShapeDtypeStruct((s_len, D_MODEL), BF16)
    return pl.pallas_call(
        body,
        grid=(s_len // tm,),
        in_specs=[
            pl.BlockSpec((tm, D_MODEL), row),
            pl.BlockSpec((tm, PLE_DIM), row),
            pl.BlockSpec((1, D_MODEL), full),
            pl.BlockSpec((D_MODEL, D_MODEL), full),
            pl.BlockSpec((PLE_DIM, D_MODEL), full),
            pl.BlockSpec((tm, D_MODEL), row),
        ],
        out_specs=[
            pl.BlockSpec((tm, D_MODEL), row), pl.BlockSpec((tm, D_MODEL), row),
            pl.BlockSpec((tm, D_MODEL), row), pl.BlockSpec((tm, D_MODEL), row),
            pl.BlockSpec((8, D_MODEL), full), pl.BlockSpec((8, LANES), full),
        ],
        out_shape=[
            jax.ShapeDtypeStruct((s_len, D_MODEL), F32), bf, bf, bf,
            jax.ShapeDtypeStruct((8, D_MODEL), F32), jax.ShapeDtypeStruct((8, LANES), F32),
        ],
        compiler_params=_cparams(48),
        name="ple_loss",
    )(x3, p, g, w_pg, w_pp, target)


def _sb_grads_packed(dqs, dkt4, dvt4, tm=512):
    s_len = dqs.shape[0]

    def body(dq_ref, dkt_ref, dvt_ref, o_ref):
        o_ref[:, 0:ATT_W] = (dq_ref[...] * QK_SCALE).astype(BF16)
        o_ref[:, ATT_W:2 * ATT_W] = _rows_of_transposed(dkt_ref).astype(BF16)
        o_ref[:, 2 * ATT_W:] = _rows_of_transposed(dvt_ref).astype(BF16)

    return pl.pallas_call(
        body,
        grid=(s_len // tm,),
        in_specs=[pl.BlockSpec((tm, ATT_W), lambda i: (i, 0)), _transposed_spec(tm), _transposed_spec(tm)],
        out_specs=pl.BlockSpec((tm, 3 * ATT_W), lambda i: (i, 0)),
        out_shape=jax.ShapeDtypeStruct((s_len, 3 * ATT_W), BF16),
        name="sb_grads_packed",
    )(dqs, dkt4, dvt4)


def _qknorm_bwd(fq, fk, dqs, dkt4, dvt4, qn, kn, bd, bd_t, tm=512):
    s_len = fq.shape[0]

    def body(fq_ref, fk_ref, dq_ref, dkt_ref, dvt_ref, qn_ref, kn_ref, bd_ref, bdt_ref,
             dz_ref, dqn_ref, dkn_ref):
        @pl.when(pl.program_id(0) == 0)
        def _():
            dqn_ref[...] = jnp.zeros_like(dqn_ref)
            dkn_ref[...] = jnp.zeros_like(dkn_ref)

        bd_m = bd_ref[...]
        bdt_m = bdt_ref[...]

        def one(x, dy, gain, dgain_ref):
            xn, rw = _head_rms(x, bd_m, bdt_m)
            dgain_ref[0:1, :] += _colsum(dy * xn)
            dxn = dy * gain
            return rw * (dxn - xn * _head_mean(dxn * xn, bd_m, bdt_m))

        dz_ref[:, 0:ATT_W] = one(fq_ref[...], dq_ref[...] * QK_SCALE, qn_ref[...], dqn_ref).astype(BF16)
        dz_ref[:, ATT_W:2 * ATT_W] = one(fk_ref[...], _rows_of_transposed(dkt_ref), kn_ref[...], dkn_ref).astype(BF16)
        dz_ref[:, 2 * ATT_W:] = _rows_of_transposed(dvt_ref).astype(BF16)

    row = lambda i: (i, 0)
    full = lambda i: (0, 0)
    att = pl.BlockSpec((tm, ATT_W), row)
    return pl.pallas_call(
        body,
        grid=(s_len // tm,),
        in_specs=[att, att, att, _transposed_spec(tm), _transposed_spec(tm),
                  pl.BlockSpec((1, ATT_W), full), pl.BlockSpec((1, ATT_W), full),
                  pl.BlockSpec((ATT_W, LANES), full), pl.BlockSpec((LANES, ATT_W), full)],
        out_specs=[pl.BlockSpec((tm, 3 * ATT_W), row), pl.BlockSpec((8, ATT_W), full), pl.BlockSpec((8, ATT_W), full)],
        out_shape=[jax.ShapeDtypeStruct((s_len, 3 * ATT_W), BF16),
                   jax.ShapeDtypeStruct((8, ATT_W), F32), jax.ShapeDtypeStruct((8, ATT_W), F32)],
        name="qknorm_bwd",
    )(fq, fk, dqs, dkt4, dvt4, qn, kn, bd, bd_t)


def _inproj_bwd(x1, dx2, g, dzf, dlogf, logf, dzs, dgates, w_fox, w_fl, w_sb, w_gates, tm=512):
    s_len = x1.shape[0]

    def body(x_ref, d_ref, g_ref, dzf_ref, dlf_ref, lf_ref, dzs_ref, dgt_ref, wf_ref, wl_ref, ws_ref, wg_ref,
             dx_ref, h_ref, dfl_ref, dg_ref, db_ref):
        @pl.when(pl.program_id(0) == 0)
        def _():
            dg_ref[...] = jnp.zeros_like(dg_ref)
            db_ref[...] = jnp.zeros_like(db_ref)

        xn, r = _rms(x_ref[...])
        gain = g_ref[...]
        h_ref[...] = (xn * gain).astype(BF16)
        lane = lax.broadcasted_iota(jnp.int32, (tm, LANES), 1)
        dfl = jnp.where(lane < N_HEADS, dlf_ref[...] * (1.0 - jnp.exp(lf_ref[...])), 0.0)
        db_ref[0:1, :] += _colsum(dfl)
        dflb = dfl.astype(BF16)
        dfl_ref[...] = dflb
        dh = (_dot_nt(dzf_ref[...], wf_ref[...]) + _dot_nt(dflb, wl_ref[...])
              + _dot_nt(dzs_ref[...], ws_ref[...]) + _dot_nt(dgt_ref[...], wg_ref[...]))
        dx_ref[...] = d_ref[...] + _rms_bwd(dh, xn, r, gain)
        dg_ref[0:1, :] += _colsum(dh * xn)

    row = lambda i: (i, 0)
    full = lambda i: (0, 0)
    return pl.pallas_call(
        body,
        grid=(s_len // tm,),
        in_specs=[
            pl.BlockSpec((tm, D_MODEL), row),
            pl.BlockSpec((tm, D_MODEL), row),
            pl.BlockSpec((1, D_MODEL), full),
            pl.BlockSpec((tm, 3 * ATT_W), row),
            pl.BlockSpec((tm, LANES), row),
            pl.BlockSpec((tm, LANES), row),
            pl.BlockSpec((tm, 3 * ATT_W), row),
            pl.BlockSpec((tm, 2 * D_MODEL), row),
            pl.BlockSpec((D_MODEL, 3 * ATT_W), full),
            pl.BlockSpec((D_MODEL, LANES), full),
            pl.BlockSpec((D_MODEL, 3 * ATT_W), full),
            pl.BlockSpec((D_MODEL, 2 * D_MODEL), full),
        ],
        out_specs=[
            pl.BlockSpec((tm, D_MODEL), row), pl.BlockSpec((tm, D_MODEL), row), pl.BlockSpec((tm, LANES), row),
            pl.BlockSpec((8, D_MODEL), full), pl.BlockSpec((8, LANES), full),
        ],
        out_shape=[
            jax.ShapeDtypeStruct((s_len, D_MODEL), F32), jax.ShapeDtypeStruct((s_len, D_MODEL), BF16),
            jax.ShapeDtypeStruct((s_len, LANES), BF16),
            jax.ShapeDtypeStruct((8, D_MODEL), F32), jax.ShapeDtypeStruct((8, LANES), F32),
        ],
        compiler_params=_cparams(56),
        name="inproj_bwd",
    )(x1, dx2, g, dzf, dlogf, logf, dzs, dgates, w_fox, w_fl, w_sb, w_gates)


def _split_w_in(w_in):
    o = 3 * ATT_W
    w_fox = w_in[:, 0:o]
    w_fl = jnp.pad(w_in[:, o:o + N_HEADS], ((0, 0), (0, LANES - N_HEADS)))
    w_sb = w_in[:, o + N_HEADS:2 * o + N_HEADS]
    w_gates = w_in[:, 2 * o + N_HEADS:]
    return w_fox, w_fl, w_sb, w_gates


def _local_grads(x, p, target, small, full, pending=None, send_early=None):
    blk = ATT_BLOCK
    bd, bd_t = _head_sum_matrices()
    full = dict(full)
    late = list(pending) if pending else []

    x1, a1, b1, u1, *gathered = _ffn_fwd(x, small["ffn1_norm"], full["ffn1_w_gate"], full["ffn1_w_up"],
                                     full["ffn1_w_down"], gather=[pending[k] for k in late])
    for k, gth in zip(late, gathered):
        full[k] = gth if k in KEPT_AS_SHARDS else _whole(k, gth)
    w_fox, w_fl, w_sb, w_gates = _split_w_in(full["w_in"])
    bias = jnp.pad(small["forget_bias"], ((0, 0), (0, LANES - N_HEADS)))
    qn = jnp.tile(small["q_norm"], (1, N_HEADS))
    kn = jnp.tile(small["k_norm"], (1, N_HEADS))
    fq, fk, f_qs, f_k, f_v, logf, s_qs, s_k, s_v, gates = _inproj_fwd(
        x1, small["mix_norm"], w_fox, w_fl, w_sb, w_gates, bias, qn, kn, bd, bd_t)
    f_cum, fw = _cumsum_rows(logf, reverse=False, spread=bd_t)
    f8 = f_cum[:, 0:N_HEADS]
    ft4 = _pair_rows_t(f8, blk)
    f_k3, f_v3 = _blocked_rows(f_k, blk), _blocked_rows(f_v, blk)
    y_fox, lse, f_first = _fox_fwd(f_qs, f_k3, f_v3, fw, ft4,
                                   _normed_dot_bound(small["q_norm"], small["k_norm"]))
    s_k3, s_v3 = _blocked_rows(s_k, blk), _blocked_rows(s_v, blk)
    y_sb, s_rtot, s_first = _sb_fwd(s_qs, s_k3, s_v3)
    x2 = _merge_fwd(x1, gates, y_fox, y_sb, full["w_branch_fox"], full["w_branch_sb"], full["w_out"])
    x3, a2, b2, u2 = _ffn_fwd(x2, small["ffn2_norm"], full["ffn2_w_gate"], full["ffn2_w_up"], full["ffn2_w_down"])

    dx3, du_ple, dt_ple, hn_ple, dg_ple, loss_sum = _ple_loss(
        x3, p, small["ple_norm"], full["w_ple_gate"], full["w_ple_proj"], target)
    dx2, da2, db2, h_ffn2, d3_bf, dg_ffn2 = _ffn_bwd(
        x2, dx3, small["ffn2_norm"], a2, b2, full["ffn2_w_gate"], full["ffn2_w_up"], full["ffn2_w_down"])
    dy_fox, dy_sb, dgates, d_of, d_os, merged, d2_bf = _merge_bwd(
        dx2, gates, y_fox, y_sb, full["w_branch_fox"], full["w_branch_sb"], full["w_out"])

    f_dqs, dfq_p, f_dkt4, f_dvt4, dft4 = _fox_bwd(f_qs, f_k3, f_v3, dy_fox, y_fox, lse, fw, ft4, f_first)
    s_dqs, s_dkt4, s_dvt4 = _sb_bwd(s_qs, s_k3, s_v3, dy_sb, s_rtot, s_first)

    dzf, dqn8, dkn8 = _qknorm_bwd(fq, fk, f_dqs, f_dkt4, f_dvt4, qn, kn, bd, bd_t)
    dzs = _sb_grads_packed(s_dqs, s_dkt4, s_dvt4)
    df8 = _unpair_rows_t(dft4) + dfq_p[:, :, 0:2].transpose(1, 0, 2).reshape(-1, N_HEADS)
    dlogf = _cumsum_rows(jnp.pad(df8, ((0, 0), (0, LANES - N_HEADS))), reverse=True)
    dx1, h_mix, dfl, dg_mix, dbias8 = _inproj_bwd(
        x1, dx2, small["mix_norm"], dzf, dlogf, logf, dzs, dgates, w_fox, w_fl, w_sb, w_gates)

    one = lambda t: t[None]
    gw = {}
    gw["ffn2_w_gate"] = _wgrad(da2, one(h_ffn2), name="wgrad_ffn2_gate")
    gw["ffn2_w_up"] = _wgrad(db2, one(h_ffn2), name="wgrad_ffn2_up")
    gw["ffn2_w_down"] = _wgrad(u2, one(d3_bf), scale=0.5, name="wgrad_ffn2_down")
    g_fox = _wgrad(one(h_mix), one(dzf), name="wgrad_in_fox")[0]
    g_fl = _wgrad(one(h_mix), one(dfl), name="wgrad_in_forget")[0]
    g_sb = _wgrad(one(h_mix), one(dzs), name="wgrad_in_sb")[0]
    g_gt = _wgrad(one(h_mix), one(dgates), name="wgrad_in_gates")[0]
    gw["w_in"] = jnp.concatenate([g_fox, g_fl[:, 0:N_HEADS], g_sb, g_gt], axis=1)
    gw["w_branch_fox"] = _wgrad(one(y_fox), one(d_of), name="wgrad_branch_fox")[0]
    gw["w_branch_sb"] = _wgrad(one(y_sb), one(d_os), name="wgrad_branch_sb")[0]
    gw["w_out"] = _wgrad(one(merged), one(d2_bf), name="wgrad_out")[0]
    gw["w_ple_gate"] = _wgrad(one(hn_ple), one(du_ple), name="wgrad_ple_gate")[0]
    gw["w_ple_proj"] = _wgrad(one(p), one(dt_ple), name="wgrad_ple_proj")[0]

    gw["ffn1_w_down"] = _wgrad(u1, one(dx1), scale=0.5, name="wgrad_ffn1_down")

    sent_names, to_send = send_early(gw) if send_early else ([], [])
    grad_x, da1, db1, h_ffn1, _, dg_ffn1, *landed = _ffn_bwd(
        x, dx1, small["ffn1_norm"], a1, b1, full["ffn1_w_gate"], full["ffn1_w_up"], full["ffn1_w_down"],
        scatter=to_send)
    gw["ffn1_w_gate"] = _wgrad(da1, one(h_ffn1), name="wgrad_ffn1_gate")
    gw["ffn1_w_up"] = _wgrad(db1, one(h_ffn1), name="wgrad_ffn1_up")

    fold = lambda t: jnp.sum(t[0:1].reshape(N_HEADS, HEAD_DIM), axis=0, keepdims=True)
    gs = {
        "ffn1_norm": dg_ffn1[0:1], "mix_norm": dg_mix[0:1], "ffn2_norm": dg_ffn2[0:1], "ple_norm": dg_ple[0:1],
        "forget_bias": dbias8[0:1, 0:N_HEADS], "q_norm": fold(dqn8), "k_norm": fold(dkn8),
    }
    return loss_sum, grad_x, gw, gs, dict(zip(sent_names, landed))


def _position():
    return lax.axis_index("x"), lax.axis_index("y"), lax.axis_index("c")


def _other_chips(x, y):
    return [(1 - x, y), (x, 1 - y), (1 - x, 1 - y)]


ANY = pl.BlockSpec(memory_space=pl.ANY)


def _place_own_shard(w, q):
    rows, cols = w.shape
    tr = _row_block(rows, cols * 4, budget=2 * MIB)

    def body(q_ref, w_ref, o_ref):
        o_ref[0] = w_ref[...].astype(BF16)

    return pl.pallas_call(
        body,
        grid_spec=pltpu.PrefetchScalarGridSpec(
            num_scalar_prefetch=1,
            grid=(rows // tr,),
            in_specs=[pl.BlockSpec((tr, cols), lambda i, q_ref: (i, 0))],
            out_specs=pl.BlockSpec((1, tr, cols), lambda i, q_ref: (q_ref[0], i, 0)),
        ),
        out_shape=jax.ShapeDtypeStruct((N_CHIPS, rows, cols), BF16),
        name="place_own_shard",
    )(q, w)


def _gather_semaphores(n):
    return [pltpu.SemaphoreType.DMA((6 * n,)), pltpu.SemaphoreType.DMA((6 * n,))]


def _gather_steps(bufs, send_sems, recv_sems):
    n = len(bufs)
    x, y, c = _position()
    q = 2 * x + y
    chips = _other_chips(x, y)
    sibling = (x, y, 1 - c)

    def half(a, slot, which):
        r2 = bufs[a].shape[1] // 2
        return bufs[a].at[slot, pl.ds(which * r2, r2), :]

    def copy(a, k, region, to):
        return pltpu.make_async_remote_copy(
            src_ref=region, dst_ref=region, send_sem=send_sems.at[6 * a + k], recv_sem=recv_sems.at[6 * a + k],
            device_id=to, device_id_type=MESH)

    def to_chip(a, k):
        tx, ty = chips[k]
        return copy(a, k, half(a, q, c), (tx, ty, c))

    def to_sibling(a, k):
        tx, ty = chips[k]
        return copy(a, 3 + k, half(a, 2 * tx + ty, c), sibling)

    def start():
        for a in range(n):
            for k in range(3):
                to_chip(a, k).start()

    def finish():
        for a in range(n):
            for k, (tx, ty) in enumerate(chips):
                copy(a, k, half(a, 2 * tx + ty, c), (tx, ty, c)).wait_recv()
                to_sibling(a, k).start()
        for a in range(n):
            for k, (tx, ty) in enumerate(chips):
                copy(a, 3 + k, half(a, 2 * tx + ty, 1 - c), sibling).wait_recv()
        for a in range(n):
            for k in range(3):
                to_chip(a, k).wait_send()
                to_sibling(a, k).wait_send()

    return start, finish


def _allgather_weights(slots):
    n = len(slots)

    def body(*refs):
        start, finish = _gather_steps(refs[n:2 * n], *refs[2 * n:])
        start()
        finish()

    return pl.pallas_call(
        body,
        in_specs=[ANY] * n,
        out_specs=[ANY] * n,
        out_shape=[jax.ShapeDtypeStruct(s.shape, s.dtype) for s in slots],
        input_output_aliases={a: a for a in range(n)},
        scratch_shapes=_gather_semaphores(n),
        name="allgather_weights",
    )(*slots)


def _exchange_pair_halves(grads):
    n = len(grads)

    def body(*refs):
        ins, outs = refs[0:n], refs[n:2 * n]
        send_sems, recv_sems = refs[2 * n:]
        x, y, c = _position()
        copies = []
        for a in range(n):
            r2 = grads[a].shape[1] // 2
            cp = pltpu.make_async_remote_copy(
                src_ref=ins[a].at[:, pl.ds((1 - c) * r2, r2), :], dst_ref=outs[a],
                send_sem=send_sems.at[a], recv_sem=recv_sems.at[a], device_id=(x, y, 1 - c), device_id_type=MESH)
            cp.start()
            copies.append(cp)
        for cp in copies:
            cp.wait()

    return pl.pallas_call(
        body,
        in_specs=[ANY] * n,
        out_specs=[ANY] * n,
        out_shape=[jax.ShapeDtypeStruct((N_CHIPS, g.shape[1] // 2, g.shape[2]), g.dtype) for g in grads],
        scratch_shapes=[pltpu.SemaphoreType.DMA((n,)), pltpu.SemaphoreType.DMA((n,))],
        name="rs_pair_exchange",
    )(*grads)


def _scatter_semaphores(n):
    return [pltpu.SemaphoreType.DMA((3 * n,)), pltpu.SemaphoreType.DMA((3 * n,)), pltpu.SemaphoreType.DMA((n,))]


def _scatter_steps(ins, outs, send_sems, recv_sems, local_sems):
    n = len(ins)
    x, y, c = _position()
    q = 2 * x + y
    chips = _other_chips(x, y)

    def own(a):
        return pltpu.make_async_copy(ins[a].at[q], outs[a].at[q], local_sems.at[a])

    def to_chip(a, k):
        tx, ty = chips[k]
        return pltpu.make_async_remote_copy(
            src_ref=ins[a].at[2 * tx + ty], dst_ref=outs[a].at[q],
            send_sem=send_sems.at[3 * a + k], recv_sem=recv_sems.at[3 * a + k],
            device_id=(tx, ty, c), device_id_type=MESH)

    def start():
        for a in range(n):
            own(a).start()
            for k in range(3):
                to_chip(a, k).start()

    def finish():
        for a in range(n):
            own(a).wait()
            for k in range(3):
                to_chip(a, k).wait()

    return start, finish


def _scatter_to_owner_chips(pairs):
    n = len(pairs)

    def body(*refs):
        start, finish = _scatter_steps(refs[0:n], refs[n:2 * n], *refs[2 * n:])
        start()
        finish()

    return pl.pallas_call(
        body,
        in_specs=[ANY] * n,
        out_specs=[ANY] * n,
        out_shape=[jax.ShapeDtypeStruct(p.shape, p.dtype) for p in pairs],
        scratch_shapes=_scatter_semaphores(n),
        name="rs_scatter",
    )(*pairs)


def _join_halves(shards):
    n = len(shards)

    def body(*refs):
        bufs = refs[n:2 * n]
        send_sems, recv_sems = refs[2 * n:]
        x, y, c = _position()
        started = []
        for a in range(n):
            r2 = shards[a].shape[0] // 2
            mine = bufs[a].at[pl.ds(c * r2, r2), :]
            cp = pltpu.make_async_remote_copy(
                src_ref=mine, dst_ref=mine, send_sem=send_sems.at[a], recv_sem=recv_sems.at[a],
                device_id=(x, y, 1 - c), device_id_type=MESH)
            cp.start()
            started.append(cp)
        for cp in started:
            cp.wait()

    return pl.pallas_call(
        body,
        in_specs=[ANY] * n,
        out_specs=[ANY] * n,
        out_shape=[jax.ShapeDtypeStruct(t.shape, t.dtype) for t in shards],
        input_output_aliases={a: a for a in range(n)},
        scratch_shapes=[pltpu.SemaphoreType.DMA((n,)), pltpu.SemaphoreType.DMA((n,))],
        name="rs_join_halves",
    )(*shards)


def _add_pair(g, got, c):
    _, r2, cols = got.shape

    def body(c_ref, g_ref, got_ref, o_ref):
        o_ref[...] = (g_ref[...].astype(F32) + got_ref[...].astype(F32)).astype(BF16)

    spec = pl.BlockSpec((1, r2, cols), lambda s, c_ref: (s, 0, 0))
    return pl.pallas_call(
        body,
        grid_spec=pltpu.PrefetchScalarGridSpec(
            num_scalar_prefetch=1,
            grid=(N_CHIPS,),
            in_specs=[pl.BlockSpec((1, r2, cols), lambda s, c_ref: (s, c_ref[0], 0)), spec],
            out_specs=spec,
        ),
        out_shape=jax.ShapeDtypeStruct(got.shape, BF16),
        name="rs_add_pair",
    )(c, g, got)


def _add_chips(parts, c):
    _, r2, cols = parts.shape

    def body(c_ref, p0, p1, p2, p3, o_ref):
        o_ref[...] = ((p0[0].astype(F32) + p1[0].astype(F32)) + p2[0].astype(F32)) + p3[0].astype(F32)

    specs = [pl.BlockSpec((1, r2, cols), functools.partial(lambda i, c_ref, s: (s, 0, 0), s=s))
             for s in range(N_CHIPS)]
    return pl.pallas_call(
        body,
        grid_spec=pltpu.PrefetchScalarGridSpec(
            num_scalar_prefetch=1,
            grid=(1,),
            in_specs=specs,
            out_specs=pl.BlockSpec((r2, cols), lambda i, c_ref: (c_ref[0], 0)),
        ),
        out_shape=jax.ShapeDtypeStruct((2 * r2, cols), F32),
        name="rs_add_chips",
    )(c, parts, parts, parts, parts)


def _allreduce_small(part):
    shape = part.shape

    def body(in_ref, out_ref, gather_ref, send_sems, recv_sems):
        x, y, c = _position()
        me = 4 * x + 2 * y + c
        relations = [(a, b, d) for a in (0, 1) for b in (0, 1) for d in (0, 1)][1:]
        flip = lambda v, f: 1 - v if f else v
        copies = []
        for k, (a, b, d) in enumerate(relations):
            cp = pltpu.make_async_remote_copy(
                src_ref=in_ref, dst_ref=gather_ref.at[me], send_sem=send_sems.at[k], recv_sem=recv_sems.at[k],
                device_id=(flip(x, a), flip(y, b), flip(c, d)), device_id_type=MESH)
            cp.start()
            copies.append(cp)
        gather_ref[me] = in_ref[...]
        for cp in copies:
            cp.wait()
        total = gather_ref[0]
        for dev in range(1, 8):
            total = total + gather_ref[dev]
        out_ref[...] = total

    vmem = pl.BlockSpec(memory_space=pltpu.VMEM)
    return pl.pallas_call(
        body,
        in_specs=[vmem],
        out_specs=vmem,
        out_shape=jax.ShapeDtypeStruct(shape, F32),
        scratch_shapes=[pltpu.VMEM((8,) + shape, F32), pltpu.SemaphoreType.DMA((7,)), pltpu.SemaphoreType.DMA((7,))],
        name="allreduce_small",
    )(part)


def _adamw(w, g, m, v):
    rows, cols = w.shape
    tr = _row_block(rows, cols * 4, budget=MIB)
    c1 = 1.0 / (1.0 - ADAM_B1 ** ADAM_STEP)
    c2 = 1.0 / (1.0 - ADAM_B2 ** ADAM_STEP)

    def body(w_ref, g_ref, m_ref, v_ref, d_ref, nm_ref, nv_ref):
        g_ = g_ref[...]
        nm = ADAM_B1 * m_ref[...] + (1.0 - ADAM_B1) * g_
        nv = ADAM_B2 * v_ref[...] + (1.0 - ADAM_B2) * (g_ * g_)
        nm_ref[...] = nm
        nv_ref[...] = nv
        d_ref[...] = -ADAM_LR * ((nm * c1) / (jnp.sqrt(nv * c2) + ADAM_EPS) + ADAM_WD * w_ref[...])

    spec = pl.BlockSpec((tr, cols), lambda i: (i, 0))
    out = jax.ShapeDtypeStruct((rows, cols), F32)
    return pl.pallas_call(
        body,
        grid=(rows // tr,),
        in_specs=[spec] * 4,
        out_specs=[spec] * 3,
        out_shape=[out] * 3,
        name="adamw",
    )(w, g, m, v)


BIG = ["ffn1_w_gate", "ffn1_w_up", "ffn1_w_down", "w_in", "w_branch_fox", "w_branch_sb", "w_out",
       "ffn2_w_gate", "ffn2_w_up", "ffn2_w_down", "w_ple_gate", "w_ple_proj"]
SMALL = ["ffn1_norm", "mix_norm", "ffn2_norm", "ple_norm", "forget_bias", "q_norm", "k_norm"]
COLUMN_SHARDED = ["w_in", "w_branch_fox", "w_branch_sb", "w_ple_proj"]
KEPT_AS_SHARDS = ["ffn1_w_gate", "ffn1_w_up", "ffn1_w_down", "ffn2_w_gate", "ffn2_w_up", "ffn2_w_down"]
WORKED_TRANSPOSED = ["ffn1_w_gate", "ffn1_w_up", "ffn2_w_gate", "ffn2_w_up"]
NEEDED_FIRST = ["ffn1_w_gate", "ffn1_w_up", "ffn1_w_down"]
READY_LAST = ["ffn1_w_gate", "ffn1_w_up"]
ORDER = ["ffn1_norm", "ffn1_w_gate", "ffn1_w_up", "ffn1_w_down", "mix_norm", "w_in", "forget_bias", "q_norm",
         "k_norm", "w_branch_fox", "w_branch_sb", "w_out", "ffn2_norm", "ffn2_w_gate", "ffn2_w_up",
         "ffn2_w_down", "ple_norm", "w_ple_gate", "w_ple_proj"]
SMALL_ROWS = {"ffn1_norm": 0, "mix_norm": 1, "ffn2_norm": 2, "ple_norm": 3}
SMALL_COLS = {"forget_bias": (0, N_HEADS), "q_norm": (N_HEADS, HEAD_DIM), "k_norm": (N_HEADS + HEAD_DIM, HEAD_DIM)}
LOSS_ROW = 5


def _stored(name, a):
    return jnp.swapaxes(a[0], 0, 1) if name in WORKED_TRANSPOSED else a[0]


def _returned(name, t):
    return (jnp.swapaxes(t, 0, 1) if name in WORKED_TRANSPOSED else t)[None]


def _whole(name, gathered):
    if name in COLUMN_SHARDED:
        return jnp.concatenate([gathered[s] for s in range(N_CHIPS)], axis=1)
    return gathered.reshape(-1, gathered.shape[-1])


def _as_shards(name, whole):
    if name in COLUMN_SHARDED:
        k, n = whole.shape
        return whole.reshape(k, N_CHIPS, n // N_CHIPS).transpose(1, 0, 2)
    return whole.reshape(N_CHIPS, whole.shape[0] // N_CHIPS, whole.shape[1])


def _pack_small(values, extra=None):
    rows = [values[k] for k in ("ffn1_norm", "mix_norm", "ffn2_norm", "ple_norm")]
    tail = jnp.concatenate([values["forget_bias"], values["q_norm"], values["k_norm"]], axis=1)
    rows.append(jnp.pad(tail, ((0, 0), (0, D_MODEL - tail.shape[1]))))
    packed = jnp.concatenate(rows + [jnp.zeros((3, D_MODEL), F32)], axis=0)
    if extra is not None:
        packed = packed.at[LOSS_ROW, 0].set(extra)
    return packed


def _unpack_small(packed):
    out = {k: packed[r:r + 1] for k, r in SMALL_ROWS.items()}
    for k, (start, size) in SMALL_COLS.items():
        out[k] = packed[4:5, start:start + size]
    return out


def kernel(x, p, ffn1_norm, ffn1_w_gate, ffn1_w_up, ffn1_w_down, mix_norm, w_in, forget_bias, q_norm, k_norm, w_branch_fox, w_branch_sb, w_out, ffn2_norm, ffn2_w_gate, ffn2_w_up, ffn2_w_down, ple_norm, w_ple_gate, w_ple_proj, loss_target, m_ffn1_norm, m_ffn1_w_gate, m_ffn1_w_up, m_ffn1_w_down, m_mix_norm, m_w_in, m_forget_bias, m_q_norm, m_k_norm, m_w_branch_fox, m_w_branch_sb, m_w_out, m_ffn2_norm, m_ffn2_w_gate, m_ffn2_w_up, m_ffn2_w_down, m_ple_norm, m_w_ple_gate, m_w_ple_proj, v_ffn1_norm, v_ffn1_w_gate, v_ffn1_w_up, v_ffn1_w_down, v_mix_norm, v_w_in, v_forget_bias, v_q_norm, v_k_norm, v_w_branch_fox, v_w_branch_sb, v_w_out, v_ffn2_norm, v_ffn2_w_gate, v_ffn2_w_up, v_ffn2_w_down, v_ple_norm, v_w_ple_gate, v_w_ple_proj):
    args = dict(locals())
    weights = {k: args[k] for k in ORDER}
    moments_m = {k: args["m_" + k] for k in ORDER}
    moments_v = {k: args["v_" + k] for k in ORDER}

    c_idx = lax.axis_index("c").astype(jnp.int32).reshape(1)
    q_idx = (2 * lax.axis_index("x") + lax.axis_index("y")).astype(jnp.int32).reshape(1)
    own = {k: _place_own_shard(_stored(k, weights[k]), q_idx) for k in BIG}
    full = dict(zip(NEEDED_FIRST, _allgather_weights([own[k] for k in NEEDED_FIRST])))
    pending = {k: own[k] for k in BIG if k not in NEEDED_FIRST}
    small = {k: weights[k] for k in SMALL}

    def pair_sums(names, gw):
        slots = [gw[k] if k in KEPT_AS_SHARDS else _as_shards(k, gw[k]) for k in names]
        from_core = _exchange_pair_halves(slots)
        return [_add_pair(g, got, c_idx) for g, got in zip(slots, from_core)]

    early = [k for k in BIG if k not in READY_LAST]
    loss_sum, grad_x, gw, gs, parts = _local_grads(
        x[0], p[0, 0], loss_target[0], small, full, pending, lambda ready: (early, pair_sums(early, ready)))

    parts.update(zip(READY_LAST, _scatter_to_owner_chips(pair_sums(READY_LAST, gw))))
    grads_big = dict(zip(BIG, _join_halves([_add_chips(parts[k], c_idx) for k in BIG])))
    reduced = _allreduce_small(_pack_small(gs, extra=loss_sum[0, 0]))
    grads_small = _unpack_small(reduced)
    loss = reduced[LOSS_ROW, 0]

    grads, deltas, new_m, new_v = {}, {}, {}, {}
    for k in BIG:
        d, nm, nv = _adamw(_stored(k, weights[k]), grads_big[k], _stored(k, moments_m[k]), _stored(k, moments_v[k]))
        grads[k], deltas[k], new_m[k], new_v[k] = (_returned(k, t) for t in (grads_big[k], d, nm, nv))
    d_s, nm_s, nv_s = _adamw(_pack_small({k: weights[k] for k in SMALL}), reduced,
                             _pack_small({k: moments_m[k] for k in SMALL}),
                             _pack_small({k: moments_v[k] for k in SMALL}))
    for k in SMALL:
        grads[k] = grads_small[k]
    for name, packed in (("d", d_s), ("m", nm_s), ("v", nv_s)):
        target = {"d": deltas, "m": new_m, "v": new_v}[name]
        target.update(_unpack_small(packed))

    return (loss, grad_x[None], *[grads[k] for k in ORDER], *[deltas[k] for k in ORDER],
            *[new_m[k] for k in ORDER], *[new_v[k] for k in ORDER])
```

```python
import functools

import jax
import jax.numpy as jnp
from jax import lax
from jax.experimental import pallas as pl
from jax.experimental.pallas import tpu as pltpu

F32 = jnp.float32
BF16 = jnp.bfloat16

D_MODEL = 1024
D_FF = 2816
N_CHIPS = 4
FF_SHARD = D_FF // N_CHIPS
FFN_CHUNKS = 2
WGRAD_TOKENS = 4096
WGRAD_VMEM = 30 * 1024 * 1024
HEAD_DIM = 64
N_HEADS = 8
ATT_W = N_HEADS * HEAD_DIM
PAIR_W = 2 * HEAD_DIM
N_PAIRS = N_HEADS // 2
PLE_DIM = 256
IN_WIDTH = 3 * ATT_W + N_HEADS + 3 * ATT_W + 2 * D_MODEL
EPS = 1e-6
QK_SCALE = HEAD_DIM ** -0.5
LANES = 128
ATT_BLOCK = 256
FOX_BWD_Q_BLOCK = 256
FOX_Q_BLOCK = 512
SB_Q_BLOCK = 256
NEG_BIG = -1e30
EXP_UNDERFLOW = 110.0
MAX_REFERENCE_EXCESS = 40.0
NORM_BOUND_MARGIN = 1.02

ADAM_LR = 0.001
ADAM_B1 = 0.9
ADAM_B2 = 0.999
ADAM_EPS = 1e-08
ADAM_WD = 0.01
ADAM_STEP = 10

MESH = pl.DeviceIdType.MESH
MIB = 1024 * 1024


def _cparams(vmem_mib=48):
    return pltpu.CompilerParams(vmem_limit_bytes=vmem_mib * MIB)


def _dot(a, b):
    return jnp.dot(a, b, preferred_element_type=F32)


def _dot_tn(a, b):
    return lax.dot_general(a, b, (((0,), (0,)), ((), ())), preferred_element_type=F32)


def _dot_nt(a, b):
    return lax.dot_general(a, b, (((1,), (1,)), ((), ())), preferred_element_type=F32)


def _sigmoid(x):
    return 1.0 / (1.0 + jnp.exp(-x))


def _split2(x):
    hi = x.astype(BF16)
    lo = (x - hi.astype(F32)).astype(BF16)
    return hi, lo


def _dot_split2(x, m):
    hi, lo = _split2(x)
    return _dot(hi, m) + _dot(lo, m)


def _split3(x):
    hi = x.astype(BF16)
    rest = x - hi.astype(F32)
    mid = rest.astype(BF16)
    lo = (rest - mid.astype(F32)).astype(BF16)
    return hi, mid, lo


def _rms(x):
    r = lax.rsqrt(jnp.mean(x * x, axis=-1, keepdims=True) + EPS)
    return x * r, r


def _rms_bwd(dh, xn, r, g):
    dxn = dh * g
    return r * (dxn - xn * jnp.mean(dxn * xn, axis=-1, keepdims=True))


def _colsum(x):
    return jnp.sum(x, axis=0, keepdims=True)


def _row_block(rows, row_bytes, budget):
    best = None
    for t in range(8, rows + 1, 8):
        if rows % t == 0 and t * row_bytes <= budget:
            best = t
    return best if best is not None else rows


def _ffn_fwd(x, g, wg, wu, wd, gather=(), tm=1024):
    s_len = x.shape[0]
    n = len(gather)
    steps = s_len // tm

    def body(x_ref, g_ref, wg_ref, wu_ref, wd_ref, *rest):
        o_ref, a_ref, b_ref, u_ref = rest[n:n + 4]
        h_s, acc_s = rest[2 * n + 4:2 * n + 6]
        i = pl.program_id(0)
        j = pl.program_id(1)
        if n:
            start, finish = _gather_steps(rest[n + 4:2 * n + 4], *rest[2 * n + 6:])
            pl.when((i == 0) & (j == 0))(start)

        @pl.when(j == 0)
        def _():
            xn, _ = _rms(x_ref[...])
            h_s[...] = (xn * g_ref[...]).astype(BF16)
            acc_s[...] = jnp.zeros_like(acc_s)

        chunks = [pl.ds(r * (tm // FFN_CHUNKS), tm // FFN_CHUNKS) for r in range(FFN_CHUNKS)]
        pre = [(_dot_nt(h_s[rows, :], wg_ref[0]), _dot_nt(h_s[rows, :], wu_ref[0])) for rows in chunks]
        us = []
        for rows, (a, b) in zip(chunks, pre):
            a_ref[0, rows, :] = a.astype(BF16)
            b_ref[0, rows, :] = b.astype(BF16)
            u = (a * _sigmoid(a) * b).astype(BF16)
            u_ref[0, rows, :] = u
            us.append(u)
        for rows, u in zip(chunks, us):
            acc_s[rows, :] += _dot(u, wd_ref[0])

        @pl.when(j == N_CHIPS - 1)
        def _():
            o_ref[...] = x_ref[...] + 0.5 * acc_s[...]

        if n:
            pl.when((i == steps - 1) & (j == N_CHIPS - 1))(finish)

    return pl.pallas_call(
        body,
        grid=(steps, N_CHIPS),
        in_specs=[
            pl.BlockSpec((tm, D_MODEL), lambda i, j: (i, 0)),
            pl.BlockSpec((1, D_MODEL), lambda i, j: (0, 0)),
            pl.BlockSpec((1, FF_SHARD, D_MODEL), lambda i, j: (j, 0, 0)),
            pl.BlockSpec((1, FF_SHARD, D_MODEL), lambda i, j: (j, 0, 0)),
            pl.BlockSpec((1, FF_SHARD, D_MODEL), lambda i, j: (j, 0, 0)),
        ] + [ANY] * n,
        out_specs=[pl.BlockSpec((tm, D_MODEL), lambda i, j: (i, 0))]
        + [pl.BlockSpec((1, tm, FF_SHARD), lambda i, j: (j, i, 0))] * 3 + [ANY] * n,
        out_shape=[jax.ShapeDtypeStruct((s_len, D_MODEL), F32)]
        + [jax.ShapeDtypeStruct((N_CHIPS, s_len, FF_SHARD), BF16)] * 3
        + [jax.ShapeDtypeStruct(s.shape, s.dtype) for s in gather],
        input_output_aliases={5 + a: 4 + a for a in range(n)},
        scratch_shapes=[pltpu.VMEM((tm, D_MODEL), BF16), pltpu.VMEM((tm, D_MODEL), F32)]
        + (_gather_semaphores(n) if n else []),
        compiler_params=_cparams(56),
        name="ffn_fwd_gathering" if n else "ffn_fwd",
    )(x, g, wg, wu, wd, *gather)


def _ffn_bwd(x, d, g, a_pre, b_pre, wg, wu, wd, scatter=(), tm=512):
    s_len = x.shape[0]
    nb = s_len // tm
    n = len(scatter)

    def body(x_ref, d_ref, g_ref, a_ref, b_ref, wg_ref, wu_ref, wd_ref, *rest):
        dx_ref, da_ref, db_ref, h_ref, dbf_ref, dg_ref = rest[n:n + 6]
        dbf_s, dh_s = rest[2 * n + 6:2 * n + 8]
        i = pl.program_id(0)
        j = pl.program_id(1)
        if n:
            start, finish = _scatter_steps(rest[0:n], rest[n + 6:2 * n + 6], *rest[2 * n + 8:])
            pl.when((i == 0) & (j == 0))(start)

        @pl.when(j == 0)
        def _():
            xn, _ = _rms(x_ref[...])
            h_ref[...] = (xn * g_ref[...]).astype(BF16)
            dbf = d_ref[...].astype(BF16)
            dbf_s[...] = dbf
            dbf_ref[...] = dbf
            dh_s[...] = jnp.zeros_like(dh_s)

        @pl.when((i == 0) & (j == 0))
        def _():
            dg_ref[...] = jnp.zeros_like(dg_ref)

        chunks = [pl.ds(r * (tm // FFN_CHUNKS), tm // FFN_CHUNKS) for r in range(FFN_CHUNKS)]
        dus = [0.5 * _dot_nt(dbf_s[rows, :], wd_ref[0]) for rows in chunks]
        das, dbs = [], []
        for rows, du in zip(chunks, dus):
            a = a_ref[0, rows, :].astype(F32)
            b = b_ref[0, rows, :].astype(F32)
            s = _sigmoid(a)
            silu = a * s
            da = (du * b * (s * (1.0 + a * (1.0 - s)))).astype(BF16)
            db = (du * silu).astype(BF16)
            da_ref[0, rows, :] = da
            db_ref[0, rows, :] = db
            das.append(da)
            dbs.append(db)
        for rows, da, db in zip(chunks, das, dbs):
            dh_s[rows, :] += _dot(da, wg_ref[0]) + _dot(db, wu_ref[0])

        @pl.when(j == N_CHIPS - 1)
        def _():
            xn, r = _rms(x_ref[...])
            dh = dh_s[...]
            dx_ref[...] = d_ref[...] + _rms_bwd(dh, xn, r, g_ref[...])
            dg_ref[0:1, :] += _colsum(dh * xn)

        if n:
            pl.when((i == nb - 1) & (j == N_CHIPS - 1))(finish)

    row = lambda i, j: (i, 0)
    shard = lambda i, j: (j, 0, 0)
    act = lambda i, j: (j, i, 0)
    return pl.pallas_call(
        body,
        grid=(nb, N_CHIPS),
        in_specs=[
            pl.BlockSpec((tm, D_MODEL), row),
            pl.BlockSpec((tm, D_MODEL), row),
            pl.BlockSpec((1, D_MODEL), lambda i, j: (0, 0)),
            pl.BlockSpec((1, tm, FF_SHARD), act),
            pl.BlockSpec((1, tm, FF_SHARD), act),
            pl.BlockSpec((1, FF_SHARD, D_MODEL), shard),
            pl.BlockSpec((1, FF_SHARD, D_MODEL), shard),
            pl.BlockSpec((1, FF_SHARD, D_MODEL), shard),
        ] + [ANY] * n,
        out_specs=[
            pl.BlockSpec((tm, D_MODEL), row),
            pl.BlockSpec((1, tm, FF_SHARD), act),
            pl.BlockSpec((1, tm, FF_SHARD), act),
            pl.BlockSpec((tm, D_MODEL), row),
            pl.BlockSpec((tm, D_MODEL), row),
            pl.BlockSpec((8, D_MODEL), lambda i, j: (0, 0)),
        ] + [ANY] * n,
        out_shape=[
            jax.ShapeDtypeStruct((s_len, D_MODEL), F32),
            jax.ShapeDtypeStruct((N_CHIPS, s_len, FF_SHARD), BF16),
            jax.ShapeDtypeStruct((N_CHIPS, s_len, FF_SHARD), BF16),
            jax.ShapeDtypeStruct((s_len, D_MODEL), BF16),
            jax.ShapeDtypeStruct((s_len, D_MODEL), BF16),
            jax.ShapeDtypeStruct((8, D_MODEL), F32),
        ] + [jax.ShapeDtypeStruct(s.shape, s.dtype) for s in scatter],
        scratch_shapes=[
            pltpu.VMEM((tm, D_MODEL), BF16),
            pltpu.VMEM((tm, D_MODEL), F32),
        ] + (_scatter_semaphores(n) if n else []),
        compiler_params=_cparams(56),
        name="ffn_bwd_scattering" if n else "ffn_bwd",
    )(x, d, g, a_pre, b_pre, wg, wu, wd, *scatter)


def _wgrad(a, b, scale=1.0, name="wgrad"):
    na, s_len, k_dim = a.shape
    nb, _, n_dim = b.shape
    n = max(na, nb)
    ts = WGRAD_TOKENS
    while ts > 512 and (ts > s_len or 2 * ts * (k_dim * a.dtype.itemsize + n_dim * b.dtype.itemsize) > WGRAD_VMEM):
        ts //= 2
    steps = s_len // ts

    def body(a_ref, b_ref, o_ref, acc_s):
        s = pl.program_id(1)

        @pl.when(s == 0)
        def _():
            acc_s[...] = jnp.zeros_like(acc_s)

        acc_s[...] += _dot_tn(a_ref[0].astype(BF16), b_ref[0].astype(BF16))

        @pl.when(s == steps - 1)
        def _():
            o_ref[0] = (acc_s[...] * scale).astype(BF16)

    a_map = (lambda m, s: (m, s, 0)) if na > 1 else (lambda m, s: (0, s, 0))
    b_map = (lambda m, s: (m, s, 0)) if nb > 1 else (lambda m, s: (0, s, 0))
    return pl.pallas_call(
        body,
        grid=(n, steps),
        in_specs=[pl.BlockSpec((1, ts, k_dim), a_map), pl.BlockSpec((1, ts, n_dim), b_map)],
        out_specs=pl.BlockSpec((1, k_dim, n_dim), lambda m, s: (m, 0, 0)),
        out_shape=jax.ShapeDtypeStruct((n, k_dim, n_dim), BF16),
        scratch_shapes=[pltpu.VMEM((k_dim, n_dim), F32)],
        compiler_params=_cparams(56),
        name=name,
    )(a, b)


def _head_sum_matrices():
    lane = lax.broadcasted_iota(jnp.int32, (ATT_W, LANES), 0) // HEAD_DIM
    col = lax.broadcasted_iota(jnp.int32, (ATT_W, LANES), 1)
    bd = (lane == col).astype(BF16)
    return bd, bd.T


def _head_mean(t, bd, bd_t):
    per_head = _dot_split2(t, bd) * (1.0 / HEAD_DIM)
    return _dot_split2(per_head, bd_t)


def _head_rms(x, bd, bd_t):
    per_head = _dot_split2(x * x, bd) * (1.0 / HEAD_DIM)
    r = lax.rsqrt(per_head + EPS)
    rw = _dot_split2(r, bd_t)
    return x * rw, rw


def _log_sigmoid(z):
    return jnp.minimum(z, 0.0) - jnp.log(1.0 + jnp.exp(-jnp.abs(z)))


def _inproj_fwd(x1, g, w_fox, w_fl, w_sb, w_gates, bias, qn, kn, bd, bd_t, tm=512):
    s_len = x1.shape[0]

    def body(x_ref, g_ref, wf_ref, wl_ref, ws_ref, wg_ref, bias_ref, qn_ref, kn_ref, bd_ref, bdt_ref,
             fq_ref, fk_ref, qs_ref, kf_ref, vf_ref, logf_ref, sq_ref, sk_ref, sv_ref, gates_ref):
        xn, _ = _rms(x_ref[...])
        h = (xn * g_ref[...]).astype(BF16)
        zf = _dot(h, wf_ref[...])
        fq = zf[:, 0:ATT_W]
        fk = zf[:, ATT_W:2 * ATT_W]
        fq_ref[...] = fq
        fk_ref[...] = fk
        bd_m = bd_ref[...]
        bdt_m = bdt_ref[...]
        fqn, _ = _head_rms(fq, bd_m, bdt_m)
        fkn, _ = _head_rms(fk, bd_m, bdt_m)
        qs_ref[...] = (fqn * qn_ref[...]).astype(BF16) * QK_SCALE
        kf_ref[...] = (fkn * kn_ref[...]).astype(BF16)
        vf_ref[...] = zf[:, 2 * ATT_W:3 * ATT_W].astype(BF16)
        logf_ref[...] = _log_sigmoid(_dot(h, wl_ref[...]) + bias_ref[...])
        zs = _dot(h, ws_ref[...])
        sq_ref[...] = zs[:, 0:ATT_W].astype(BF16) * QK_SCALE
        sk_ref[...] = zs[:, ATT_W:2 * ATT_W].astype(BF16)
        sv_ref[...] = zs[:, 2 * ATT_W:3 * ATT_W].astype(BF16)
        gates_ref[...] = _dot(h, wg_ref[...]).astype(BF16)

    row = lambda i: (i, 0)
    full = lambda i: (0, 0)
    att = lambda dt: jax.ShapeDtypeStruct((s_len, ATT_W), dt)
    return pl.pallas_call(
        body,
        grid=(s_len // tm,),
        in_specs=[
            pl.BlockSpec((tm, D_MODEL), row),
            pl.BlockSpec((1, D_MODEL), full),
            pl.BlockSpec((D_MODEL, 3 * ATT_W), full),
            pl.BlockSpec((D_MODEL, LANES), full),
            pl.BlockSpec((D_MODEL, 3 * ATT_W), full),
            pl.BlockSpec((D_MODEL, 2 * D_MODEL), full),
            pl.BlockSpec((1, LANES), full),
            pl.BlockSpec((1, ATT_W), full),
            pl.BlockSpec((1, ATT_W), full),
            pl.BlockSpec((ATT_W, LANES), full),
            pl.BlockSpec((LANES, ATT_W), full),
        ],
        out_specs=[
            pl.BlockSpec((tm, ATT_W), row), pl.BlockSpec((tm, ATT_W), row),
            pl.BlockSpec((tm, ATT_W), row), pl.BlockSpec((tm, ATT_W), row), pl.BlockSpec((tm, ATT_W), row),
            pl.BlockSpec((tm, LANES), row),
            pl.BlockSpec((tm, ATT_W), row), pl.BlockSpec((tm, ATT_W), row), pl.BlockSpec((tm, ATT_W), row),
            pl.BlockSpec((tm, 2 * D_MODEL), row),
        ],
        out_shape=[
            att(F32), att(F32), att(BF16), att(BF16), att(BF16),
            jax.ShapeDtypeStruct((s_len, LANES), F32),
            att(BF16), att(BF16), att(BF16),
            jax.ShapeDtypeStruct((s_len, 2 * D_MODEL), BF16),
        ],
        compiler_params=_cparams(56),
        name="inproj_fwd",
    )(x1, g, w_fox, w_fl, w_sb, w_gates, bias, qn, kn, bd, bd_t)


def _tri(n, kind):
    r = lax.broadcasted_iota(jnp.int32, (n, n), 0)
    c = lax.broadcasted_iota(jnp.int32, (n, n), 1)
    m = {"row_ge_col": r >= c, "row_le_col": r <= c, "row_gt_col": r > c, "row_lt_col": r < c}[kind]
    return m.astype(BF16)


def _cumsum_rows(x, reverse, spread=None, tm=512):
    s_len = x.shape[0]
    nb = s_len // tm
    tri = _tri(tm, "row_le_col" if reverse else "row_ge_col")
    edge = 0 if reverse else tm - 1

    def body(x_ref, tri_ref, *rest):
        spread_ref, o_ref, wide_ref, carry_s = rest if spread is not None else (None, rest[0], None, rest[1])

        @pl.when(pl.program_id(0) == 0)
        def _():
            carry_s[...] = jnp.zeros_like(carry_s)

        hi, mid, lo = _split3(x_ref[...])
        t = tri_ref[...]
        y = _dot(t, hi) + _dot(t, mid) + _dot(t, lo) + carry_s[...]
        o_ref[...] = y
        carry_s[...] = y[edge:edge + 1, :]
        if spread is not None:
            hi, mid, lo = _split3(y)
            m = spread_ref[...]
            wide_ref[...] = _dot(hi, m) + _dot(mid, m) + _dot(lo, m)

    order = (lambda i: (nb - 1 - i, 0)) if reverse else (lambda i: (i, 0))
    narrow = pl.BlockSpec((tm, LANES), order)
    in_specs, out_specs = [narrow, pl.BlockSpec((tm, tm), lambda i: (0, 0))], [narrow]
    out_shape = [jax.ShapeDtypeStruct((s_len, LANES), F32)]
    if spread is not None:
        width = spread.shape[1]
        in_specs.append(pl.BlockSpec((LANES, width), lambda i: (0, 0)))
        out_specs.append(pl.BlockSpec((tm, width), order))
        out_shape.append(jax.ShapeDtypeStruct((s_len, width), F32))
    out = pl.pallas_call(
        body,
        grid=(nb,),
        in_specs=in_specs,
        out_specs=out_specs,
        out_shape=out_shape,
        scratch_shapes=[pltpu.VMEM((1, LANES), F32)],
        name="cumsum_rev" if reverse else "cumsum_fwd",
    )(*([x, tri] if spread is None else [x, tri, spread]))
    return out[0] if spread is None else out


def _unblocked_t(t4):
    _, nb, _, blk = t4.shape
    return t4.transpose(1, 3, 0, 2).reshape(nb * blk, ATT_W)


def _transposed_spec(tm):
    return pl.BlockSpec((N_PAIRS, tm // ATT_BLOCK, PAIR_W, ATT_BLOCK), lambda i: (0, i, 0, 0))


def _rows_of_transposed(ref):
    return jnp.concatenate(
        [jnp.concatenate([ref[p, b].T for p in range(N_PAIRS)], axis=1) for b in range(ref.shape[1])], axis=0)


def _blocked_rows(t, blk):
    return t.reshape(t.shape[0] // blk, blk, t.shape[1])


def _pair_rows_t(f8, blk):
    nb = f8.shape[0] // blk
    t = f8.reshape(nb, blk, N_PAIRS, 2).transpose(2, 0, 3, 1)
    return jnp.pad(t, ((0, 0), (0, 0), (0, 6), (0, 0)))


def _unpair_rows_t(t4):
    _, nb, _, blk = t4.shape
    return t4[:, :, 0:2, :].transpose(1, 3, 0, 2).reshape(nb * blk, N_HEADS)


def _head_masks(tq):
    lane = lax.broadcasted_iota(jnp.int32, (tq, PAIR_W), 1)
    return lane < HEAD_DIM


def _causal_mask(tq, tk, offset, strict):
    d = lax.broadcasted_iota(jnp.int32, (tq, tk), 1) - lax.broadcasted_iota(jnp.int32, (tq, tk), 0)
    return (d < offset) if strict else (d <= offset)


def _heads_of(ref, first):
    t = ref[...]
    zero = jnp.zeros_like(t)
    return [jnp.where(first, t, zero), jnp.where(first, zero, t)]


def _head_cols(ref):
    t = ref[...]
    return [t[:, 0:1], t[:, HEAD_DIM:HEAD_DIM + 1]]


def _att_specs(s_len, tq):
    tk = ATT_BLOCK
    nq, nk = s_len // tq, s_len // tk
    return dict(
        nq=nq,
        q=pl.BlockSpec((tq, PAIR_W), lambda p, i: (i, p)),
        k_t=pl.BlockSpec((1, nk, PAIR_W, tk), lambda p, i: (p, 0, 0, 0)),
        k_rows=pl.BlockSpec((nk, tk, PAIR_W), lambda p, i: (0, 0, p)),
        f_t=pl.BlockSpec((1, nk, 8, tk), lambda p, i: (p, 0, 0, 0)),
        first=pl.BlockSpec((1, 1, 8, LANES), lambda p, i: (p, i, 0, 0)),
        wide=jax.ShapeDtypeStruct((s_len, ATT_W), F32),
        k_t_out=jax.ShapeDtypeStruct((N_PAIRS, nk, PAIR_W, tk), F32),
        f_t_out=jax.ShapeDtypeStruct((N_PAIRS, nk, 8, tk), F32),
        first_out=jax.ShapeDtypeStruct((N_PAIRS, nq, 8, LANES), F32),
        acc=pltpu.VMEM((2, tq, PAIR_W), F32),
    )


def _first_block(first_ref, limit):
    return jnp.clip(jnp.max(first_ref[0, 0]).astype(jnp.int32), 0, limit)


def _normed_dot_bound(qn, kn):
    stretch = jnp.max(jnp.abs(qn)) * jnp.max(jnp.abs(kn))
    return (stretch * (HEAD_DIM * QK_SCALE * NORM_BOUND_MARGIN)).reshape(1)


def _fox_fwd(qs, k3, v3, fw, ft4, dot_bound):
    tq, tk = FOX_Q_BLOCK, ATT_BLOCK
    sp = _att_specs(qs.shape[0], tq)
    ratio, nk = tq // tk, qs.shape[0] // tk
    scalars = jnp.concatenate([ft4[:, :, :2, tk - 1].reshape(-1), dot_bound])

    def body(fend_ref, q_ref, k_ref, v_ref, fw_ref, ft_ref, y_ref, lse_ref, first_ref,
             acc_ref, max_ref, sum_ref):
        pair, i = pl.program_id(0), pl.program_id(1)
        first = _head_masks(tq)
        qh = _heads_of(q_ref, first)
        fqh = _head_cols(fw_ref)
        reach = [fqh[n] + fend_ref[N_PAIRS * nk * 2] for n in range(2)]

        def logits(j, shift, r0=0, diag=False):
            k, fk = k_ref[j], ft_ref[0, j]
            raw = [_dot_nt(qh[n][r0:], k) for n in range(2)]
            out = []
            for n in range(2):
                s = raw[n] + (shift[n][r0:] - fk[n:n + 1, :])
                if diag:
                    s = jnp.where(_causal_mask(tq - r0, tk, 0, strict=False), s, NEG_BIG)
                out.append(s)
            return out

        def max_pass(j, r0=0, diag=False, assign=False):
            ss = logits(j, fqh, r0, diag)
            for n in range(2):
                max_ref[n, r0:] = ss[n] if assign else jnp.maximum(max_ref[n, r0:], ss[n])

        def sum_pass(j, shift, r0=0, diag=False, assign=False):
            ps = [jnp.exp(s) for s in logits(j, shift, r0, diag)]
            v = v_ref[j]
            for n in range(2):
                sum_ref[n, r0:] = ps[n] if assign else sum_ref[n, r0:] + ps[n]
            for n in range(2):
                pv = _dot(ps[n].astype(BF16), v)
                acc_ref[n, r0:] = pv if assign else acc_ref[n, r0:] + pv

        for d in range(ratio):
            max_pass(ratio * i + d, d * tk, True, d == 0)

        m_diag = [jnp.max(max_ref[n], axis=-1, keepdims=True) for n in range(2)]
        slack = [jnp.max(reach[n] - m_diag[n]) for n in range(2)]

        def f_end(j, n):
            return fend_ref[(pair * nk + jnp.maximum(j, 0)) * 2 + n]

        def block_matters(j):
            gap = jnp.maximum(slack[0] - f_end(j, 0), slack[1] - f_end(j, 1))
            return (j >= 0) & (gap > -EXP_UNDERFLOW)

        last_left = ratio * i - 1
        j_first = lax.while_loop(block_matters, lambda j: j - 1, last_left) + 1

        bound = [reach[n] - f_end(last_left, n) for n in range(2)]
        excess = jnp.maximum(jnp.max(bound[0] - m_diag[0]), jnp.max(bound[1] - m_diag[1]))
        exact = excess > MAX_REFERENCE_EXCESS

        def exact_max():
            def one_max(j, c):
                max_pass(j)
                return c
            lax.fori_loop(j_first, ratio * i, one_max, 0)
            return [jnp.max(max_ref[n], axis=-1, keepdims=True) for n in range(2)]

        def bounded_max():
            walked_left = j_first < ratio * i
            return [jnp.maximum(m_diag[n], jnp.where(walked_left, bound[n], NEG_BIG)) for n in range(2)]

        m = lax.cond(exact, exact_max, bounded_max)
        shift = [fqh[n] - m[n] for n in range(2)]

        for d in range(ratio):
            sum_pass(ratio * i + d, shift, d * tk, True, d == 0)

        def one(j, c):
            sum_pass(j, shift)
            return c
        lax.fori_loop(j_first, ratio * i, one, 0)
        l = [jnp.sum(sum_ref[n], axis=-1, keepdims=True) for n in range(2)]
        y_ref[...] = jnp.where(first, acc_ref[0] / l[0], acc_ref[1] / l[1])
        lse_ref[...] = jnp.where(first, m[0] + jnp.log(l[0]), m[1] + jnp.log(l[1]))
        first_ref[...] = jnp.ones(first_ref.shape, F32) * j_first.astype(F32)

    tile = pltpu.VMEM((2, tq, tk), F32)
    return pl.pallas_call(
        body,
        grid=(N_PAIRS, sp["nq"]),
        in_specs=[pl.BlockSpec(memory_space=pltpu.SMEM), sp["q"], sp["k_rows"], sp["k_rows"], sp["q"], sp["f_t"]],
        out_specs=[sp["q"], sp["q"], sp["first"]],
        out_shape=[sp["wide"], sp["wide"], sp["first_out"]],
        scratch_shapes=[sp["acc"], tile, tile],
        compiler_params=_cparams(56),
        name="fox_fwd",
    )(scalars, qs, k3, v3, fw, ft4)


def _fox_bwd(qs, k3, v3, dy, y, lse, fw, ft4, first_block):
    tq, tk = FOX_BWD_Q_BLOCK, ATT_BLOCK
    sp = _att_specs(qs.shape[0], tq)
    ratio = tq // tk
    first_spec = pl.BlockSpec((1, 1, 8, LANES), lambda p, i: (p, (i * tq) // FOX_Q_BLOCK, 0, 0))

    def body(q_ref, k_ref, v_ref, dy_ref, y_ref, lse_ref, fw_ref, ft_ref, first_ref,
             dq_ref, dfq_ref, dkt_ref, dvt_ref, dft_ref, acc_ref):
        i = pl.program_id(1)

        @pl.when(i == 0)
        def _():
            dkt_ref[...] = jnp.zeros_like(dkt_ref)
            dvt_ref[...] = jnp.zeros_like(dvt_ref)
            dft_ref[...] = jnp.zeros_like(dft_ref)

        first = _head_masks(tq)
        qh = _heads_of(q_ref, first)
        dyv = dy_ref[...]
        dyb = dyv.astype(BF16)
        zero = jnp.zeros_like(dyb)
        dyh = [jnp.where(first, dyb, zero), jnp.where(first, zero, dyb)]
        prod = dyv * y_ref[...]
        zf = jnp.zeros_like(prod)
        delta = [jnp.sum(jnp.where(first, prod, zf), axis=-1, keepdims=True),
                 jnp.sum(jnp.where(first, zf, prod), axis=-1, keepdims=True)]
        fqh = _head_cols(fw_ref)
        lseh = _head_cols(lse_ref)
        shift = [fqh[n] - lseh[n] for n in range(2)]
        acc_ref[...] = jnp.zeros_like(acc_ref)

        def block(j, rows, r0=0, diag=False):
            mask = _causal_mask(tq - r0, tk, 0, strict=False) if diag else None
            k, v, fk = k_ref[j], v_ref[j], ft_ref[0, j]
            q_part, dy_part = [t[r0:] for t in qh], [t[r0:] for t in dyh]
            logits = [_dot_nt(q_part[n], k) for n in range(2)]
            dps = [_dot_nt(dy_part[n], v) for n in range(2)]
            pbs, dsbs, out = [], [], []
            for n in range(2):
                p = jnp.exp(logits[n] + (shift[n][r0:] - fk[n:n + 1, :]))
                if diag:
                    p = jnp.where(mask, p, 0.0)
                ds = p * (dps[n] - delta[n][r0:])
                pbs.append(p.astype(BF16))
                dsbs.append(ds.astype(BF16))
                row_sum = jnp.sum(ds, axis=-1, keepdims=True)
                if r0:
                    row_sum = jnp.concatenate([jnp.zeros((r0, 1), F32), row_sum], axis=0)
                out.append(rows[n] + row_sum)
                dft_ref[0, j, n:n + 1, :] -= _colsum(ds)
            for n in range(2):
                acc_ref[n, r0:] += _dot(dsbs[n], k)
            dkt_ref[0, j] += _dot_tn(q_part[0], dsbs[0]) + _dot_tn(q_part[1], dsbs[1])
            dvt_ref[0, j] += _dot_tn(dy_part[0], pbs[0]) + _dot_tn(dy_part[1], pbs[1])
            return tuple(out)

        rows = (jnp.zeros((tq, 1), F32),) * 2
        rows = lax.fori_loop(_first_block(first_ref, ratio * i), ratio * i, lambda j, c: block(j, c), rows)
        for d in range(ratio):
            rows = block(ratio * i + d, rows, d * tk, True)
        dq_ref[...] = jnp.where(first, acc_ref[0], acc_ref[1])
        lane = lax.broadcasted_iota(jnp.int32, (tq, 8), 1)
        dfq_ref[0] = jnp.where(lane == 0, rows[0], jnp.where(lane == 1, rows[1], 0.0))

    return pl.pallas_call(
        body,
        grid=(N_PAIRS, sp["nq"]),
        in_specs=[sp["q"], sp["k_rows"], sp["k_rows"], sp["q"], sp["q"], sp["q"], sp["q"], sp["f_t"], first_spec],
        out_specs=[sp["q"], pl.BlockSpec((1, tq, 8), lambda p, i: (p, i, 0)), sp["k_t"], sp["k_t"], sp["f_t"]],
        out_shape=[sp["wide"], jax.ShapeDtypeStruct((N_PAIRS, qs.shape[0], 8), F32),
                   sp["k_t_out"], sp["k_t_out"], sp["f_t_out"]],
        scratch_shapes=[sp["acc"]],
        compiler_params=_cparams(56),
        name="fox_bwd",
    )(qs, k3, v3, dy, y, lse, fw, ft4, first_block)


SIGN_BIT = 0x80000000


def _sb_terms(z, mask, diag):
    neg_abs = pltpu.bitcast(pltpu.bitcast(z, jnp.uint32) | jnp.uint32(SIGN_BIT), F32)
    lb = jnp.minimum(z, 0.0) - jnp.log(1.0 + jnp.exp(neg_abs))
    l1m = lb - z
    if diag:
        l1m = jnp.where(mask, l1m, 0.0)
    return lb, l1m


def _dot_split2_stacked(x, m2):
    hi, lo = _split2(x)
    return _dot(jnp.concatenate([hi, lo], axis=1), m2)


def _tri_stacked(kind):
    t = _tri(ATT_BLOCK, kind)
    return jnp.concatenate([t, t], axis=0)


def _sb_fwd(qs, k3, v3):
    tq, tk = SB_Q_BLOCK, ATT_BLOCK
    sp = _att_specs(qs.shape[0], tq)
    ratio = tq // tk
    upper = _tri_stacked("row_gt_col")

    def body(q_ref, k_ref, v_ref, u_ref, y_ref, rtot_ref, first_ref, acc_ref):
        i = pl.program_id(1)
        first = _head_masks(tq)
        qh = _heads_of(q_ref, first)
        u = u_ref[...]
        acc_ref[...] = jnp.zeros_like(acc_ref)

        def block(j, rs, diag):
            mask = _causal_mask(tq, tk, i * tq - j * tk, strict=True) if diag else None
            k, v = k_ref[j], v_ref[j]
            logits = [_dot_nt(qh[n], k) for n in range(2)]
            terms = [_sb_terms(z, mask, diag) for z in logits]
            right = [_dot_split2_stacked(l1m, u) for _, l1m in terms]
            weights = []
            for n in range(2):
                a = jnp.exp(terms[n][0] + right[n] + rs[n])
                if diag:
                    a = jnp.where(mask, a, 0.0)
                weights.append(a.astype(BF16))
            for n in range(2):
                acc_ref[n] += _dot(weights[n], v)
            return tuple(rs[n] + jnp.sum(terms[n][1], axis=-1, keepdims=True) for n in range(2))

        rs = (jnp.zeros((tq, 1), F32),) * 2
        for d in range(ratio):
            rs = block(ratio * i + (ratio - 1 - d), rs, True)

        def block_matters(c):
            j, r0, r1 = c
            return (j >= 0) & (jnp.max(jnp.maximum(r0, r1)) > -EXP_UNDERFLOW)

        def walk_left(c):
            j, r0, r1 = c
            r0, r1 = block(j, (r0, r1), False)
            return j - 1, r0, r1

        j, r0, r1 = lax.while_loop(block_matters, walk_left, (ratio * i - 1, rs[0], rs[1]))
        y_ref[...] = jnp.where(first, acc_ref[0], acc_ref[1])
        rtot_ref[...] = jnp.where(first, r0, r1)
        first_ref[...] = jnp.ones(first_ref.shape, F32) * (j + 1).astype(F32)

    return pl.pallas_call(
        body,
        grid=(N_PAIRS, sp["nq"]),
        in_specs=[sp["q"], sp["k_rows"], sp["k_rows"], pl.BlockSpec((2 * tk, tk), lambda p, i: (0, 0))],
        out_specs=[sp["q"], sp["q"], sp["first"]],
        out_shape=[sp["wide"], sp["wide"], sp["first_out"]],
        scratch_shapes=[sp["acc"]],
        compiler_params=_cparams(56),
        name="sb_fwd",
    )(qs, k3, v3, upper)


def _sb_bwd(qs, k3, v3, dy, rtot, first_block):
    tq, tk = SB_Q_BLOCK, ATT_BLOCK
    sp = _att_specs(qs.shape[0], tq)
    ratio = tq // tk
    lower_in = _tri_stacked("row_le_col")
    lower = _tri(tk, "row_lt_col")

    def body(q_ref, k_ref, v_ref, dy_ref, rtot_ref, first_ref, li_ref, l_ref, dq_ref, dkt_ref, dvt_ref, acc_ref):
        i = pl.program_id(1)

        @pl.when(i == 0)
        def _():
            dkt_ref[...] = jnp.zeros_like(dkt_ref)
            dvt_ref[...] = jnp.zeros_like(dvt_ref)

        first = _head_masks(tq)
        qh = _heads_of(q_ref, first)
        dyb = dy_ref[...].astype(BF16)
        zero = jnp.zeros_like(dyb)
        dyh = [jnp.where(first, dyb, zero), jnp.where(first, zero, dyb)]
        rtoth = _head_cols(rtot_ref)
        li = li_ref[...]
        lo_tri = l_ref[...]
        acc_ref[...] = jnp.zeros_like(acc_ref)

        def block(j, carry, diag):
            mask = _causal_mask(tq, tk, i * tq - j * tk, strict=True) if diag else None
            k, v = k_ref[j], v_ref[j]
            logits = [_dot_nt(qh[n], k) for n in range(2)]
            das = [_dot_nt(dyh[n], v) for n in range(2)]
            terms = [_sb_terms(z, mask, diag) for z in logits]
            upto = [_dot_split2_stacked(l1m, li) for _, l1m in terms]
            des, weights = [], []
            for n in range(2):
                a = jnp.exp(terms[n][0] + ((rtoth[n] - carry[2 * n]) - upto[n]))
                if diag:
                    a = jnp.where(mask, a, 0.0)
                des.append(a * das[n])
                weights.append(a.astype(BF16))
            lefts = [_dot(de.astype(BF16), lo_tri) for de in des]
            dzbs, out = [], []
            for n in range(2):
                beta = jnp.exp(terms[n][0])
                dz = des[n] - (des[n] + (carry[2 * n + 1] + lefts[n])) * beta
                if diag:
                    dz = jnp.where(mask, dz, 0.0)
                dzbs.append(dz.astype(BF16))
                out += [carry[2 * n] + jnp.sum(terms[n][1], axis=-1, keepdims=True),
                        carry[2 * n + 1] + jnp.sum(des[n], axis=-1, keepdims=True)]
            for n in range(2):
                acc_ref[n] += _dot(dzbs[n], k)
            dkt_ref[0, j] += _dot_tn(qh[0], dzbs[0]) + _dot_tn(qh[1], dzbs[1])
            dvt_ref[0, j] += _dot_tn(dyh[0], weights[0]) + _dot_tn(dyh[1], weights[1])
            return tuple(out)

        carry = (jnp.zeros((tq, 1), F32),) * 4
        carry = lax.fori_loop(_first_block(first_ref, ratio * i), ratio * i, lambda j, c: block(j, c, False), carry)
        for d in range(ratio):
            carry = block(ratio * i + d, carry, True)
        dq_ref[...] = jnp.where(first, acc_ref[0], acc_ref[1])

    return pl.pallas_call(
        body,
        grid=(N_PAIRS, sp["nq"]),
        in_specs=[sp["q"], sp["k_rows"], sp["k_rows"], sp["q"], sp["q"], sp["first"],
                  pl.BlockSpec((2 * tk, tk), lambda p, i: (0, 0)), pl.BlockSpec((tk, tk), lambda p, i: (0, 0))],
        out_specs=[sp["q"], sp["k_t"], sp["k_t"]],
        out_shape=[sp["wide"], sp["k_t_out"], sp["k_t_out"]],
        scratch_shapes=[sp["acc"]],
        compiler_params=_cparams(56),
        name="sb_bwd",
    )(qs, k3, v3, dy, rtot, first_block, lower_in, lower)


def _merge_fwd(x1, gates, y_fox, y_sb, w_bf, w_bs, w_out, tm=512):
    s_len = x1.shape[0]

    def body(x_ref, g_ref, yf_ref, ys_ref, wbf_ref, wbs_ref, wo_ref, o_ref):
        g = g_ref[...].astype(F32)
        of = _dot(yf_ref[...].astype(BF16), wbf_ref[...])
        os_ = _dot(ys_ref[...].astype(BF16), wbs_ref[...])
        merged = _sigmoid(g[:, 0:D_MODEL]) * of + _sigmoid(g[:, D_MODEL:]) * os_
        o_ref[...] = x_ref[...] + _dot(merged.astype(BF16), wo_ref[...])

    row = lambda i: (i, 0)
    full = lambda i: (0, 0)
    return pl.pallas_call(
        body,
        grid=(s_len // tm,),
        in_specs=[
            pl.BlockSpec((tm, D_MODEL), row),
            pl.BlockSpec((tm, 2 * D_MODEL), row),
            pl.BlockSpec((tm, ATT_W), row),
            pl.BlockSpec((tm, ATT_W), row),
            pl.BlockSpec((ATT_W, D_MODEL), full),
            pl.BlockSpec((ATT_W, D_MODEL), full),
            pl.BlockSpec((D_MODEL, D_MODEL), full),
        ],
        out_specs=pl.BlockSpec((tm, D_MODEL), row),
        out_shape=jax.ShapeDtypeStruct((s_len, D_MODEL), F32),
        compiler_params=_cparams(48),
        name="merge_fwd",
    )(x1, gates, y_fox, y_sb, w_bf, w_bs, w_out)


def _merge_bwd(dx2, gates, y_fox, y_sb, w_bf, w_bs, w_out, tm=512):
    s_len = dx2.shape[0]

    def body(d_ref, g_ref, yf_ref, ys_ref, wbf_ref, wbs_ref, wo_ref,
             dyf_ref, dys_ref, dg_ref, dof_ref, dos_ref, m_ref, dbf_ref):
        dbf = d_ref[...].astype(BF16)
        dbf_ref[...] = dbf
        dm = _dot_nt(dbf, wo_ref[...])
        g = g_ref[...].astype(F32)
        of = _dot(yf_ref[...].astype(BF16), wbf_ref[...])
        os_ = _dot(ys_ref[...].astype(BF16), wbs_ref[...])
        sf = _sigmoid(g[:, 0:D_MODEL])
        ss = _sigmoid(g[:, D_MODEL:])
        m_ref[...] = (sf * of + ss * os_).astype(BF16)
        d_of = (dm * sf).astype(BF16)
        d_os = (dm * ss).astype(BF16)
        dof_ref[...] = d_of
        dos_ref[...] = d_os
        dg_ref[:, 0:D_MODEL] = (dm * of * sf * (1.0 - sf)).astype(BF16)
        dg_ref[:, D_MODEL:] = (dm * os_ * ss * (1.0 - ss)).astype(BF16)
        dyf_ref[...] = _dot_nt(d_of, wbf_ref[...])
        dys_ref[...] = _dot_nt(d_os, wbs_ref[...])

    row = lambda i: (i, 0)
    full = lambda i: (0, 0)
    return pl.pallas_call(
        body,
        grid=(s_len // tm,),
        in_specs=[
            pl.BlockSpec((tm, D_MODEL), row),
            pl.BlockSpec((tm, 2 * D_MODEL), row),
            pl.BlockSpec((tm, ATT_W), row),
            pl.BlockSpec((tm, ATT_W), row),
            pl.BlockSpec((ATT_W, D_MODEL), full),
            pl.BlockSpec((ATT_W, D_MODEL), full),
            pl.BlockSpec((D_MODEL, D_MODEL), full),
        ],
        out_specs=[
            pl.BlockSpec((tm, ATT_W), row), pl.BlockSpec((tm, ATT_W), row),
            pl.BlockSpec((tm, 2 * D_MODEL), row),
            pl.BlockSpec((tm, D_MODEL), row), pl.BlockSpec((tm, D_MODEL), row),
            pl.BlockSpec((tm, D_MODEL), row), pl.BlockSpec((tm, D_MODEL), row),
        ],
        out_shape=[
            jax.ShapeDtypeStruct((s_len, ATT_W), F32), jax.ShapeDtypeStruct((s_len, ATT_W), F32),
            jax.ShapeDtypeStruct((s_len, 2 * D_MODEL), BF16),
            jax.ShapeDtypeStruct((s_len, D_MODEL), BF16), jax.ShapeDtypeStruct((s_len, D_MODEL), BF16),
            jax.ShapeDtypeStruct((s_len, D_MODEL), BF16), jax.ShapeDtypeStruct((s_len, D_MODEL), BF16),
        ],
        compiler_params=_cparams(56),
        name="merge_bwd",
    )(dx2, gates, y_fox, y_sb, w_bf, w_bs, w_out)


def _ple_loss(x3, p, g, w_pg, w_pp, target, tm=512):
    s_len = x3.shape[0]
    inv_d = 1.0 / D_MODEL

    def body(x_ref, p_ref, g_ref, wpg_ref, wpp_ref, t_ref,
             dx_ref, du_ref, dt_ref, hn_ref, dg_ref, loss_ref):
        @pl.when(pl.program_id(0) == 0)
        def _():
            dg_ref[...] = jnp.zeros_like(dg_ref)
            loss_ref[...] = jnp.zeros_like(loss_ref)

        x = x_ref[...]
        xn, r = _rms(x)
        gain = g_ref[...]
        hn = (xn * gain).astype(BF16)
        hn_ref[...] = hn
        sg = _sigmoid(_dot(hn, wpg_ref[...]))
        t = _dot(p_ref[...].astype(BF16), wpp_ref[...])
        err = x + sg * t - t_ref[...]
        sq = jnp.sum(_colsum(err * err), axis=-1, keepdims=True)
        loss_ref[...] += (0.5 * inv_d) * sq
        dy = err * inv_d
        du = (dy * t * sg * (1.0 - sg)).astype(BF16)
        du_ref[...] = du
        dt_ref[...] = (dy * sg).astype(BF16)
        dh = _dot_nt(du, wpg_ref[...])
        dx_ref[...] = dy + _rms_bwd(dh, xn, r, gain)
        dg_ref[0:1, :] += _colsum(dh * xn)

    row = lambda i: (i, 0)
    full = lambda i: (0, 0)
    bf = jax.ShapeDtypeStruct((s_len, D_MODEL), BF16)
    return pl.pallas_call(
        body,
        grid=(s_len // tm,),
        in_specs=[
            pl.BlockSpec((tm, D_MODEL), row),
            pl.BlockSpec((tm, PLE_DIM), row),
            pl.BlockSpec((1, D_MODEL), full),
            pl.BlockSpec((D_MODEL, D_MODEL), full),
            pl.BlockSpec((PLE_DIM, D_MODEL), full),
            pl.BlockSpec((tm, D_MODEL), row),
        ],
        out_specs=[
            pl.BlockSpec((tm, D_MODEL), row), pl.BlockSpec((tm, D_MODEL), row),
            pl.BlockSpec((tm, D_MODEL), row), pl.BlockSpec((tm, D_MODEL), row),
            pl.BlockSpec((8, D_MODEL), full), pl.BlockSpec((8, LANES), full),
        ],
        out_shape=[
            jax.ShapeDtypeStruct((s_len, D_MODEL), F32), bf, bf, bf,
            jax.ShapeDtypeStruct((8, D_MODEL), F32), jax.ShapeDtypeStruct((8, LANES), F32),
        ],
        compiler_params=_cparams(48),
        name="ple_loss",
    )(x3, p, g, w_pg, w_pp, target)


def _sb_grads_packed(dqs, dkt4, dvt4, tm=512):
    s_len = dqs.shape[0]

    def body(dq_ref, dkt_ref, dvt_ref, o_ref):
        o_ref[:, 0:ATT_W] = (dq_ref[...] * QK_SCALE).astype(BF16)
        o_ref[:, ATT_W:2 * ATT_W] = _rows_of_transposed(dkt_ref).astype(BF16)
        o_ref[:, 2 * ATT_W:] = _rows_of_transposed(dvt_ref).astype(BF16)

    return pl.pallas_call(
        body,
        grid=(s_len // tm,),
        in_specs=[pl.BlockSpec((tm, ATT_W), lambda i: (i, 0)), _transposed_spec(tm), _transposed_spec(tm)],
        out_specs=pl.BlockSpec((tm, 3 * ATT_W), lambda i: (i, 0)),
        out_shape=jax.ShapeDtypeStruct((s_len, 3 * ATT_W), BF16),
        name="sb_grads_packed",
    )(dqs, dkt4, dvt4)


def _qknorm_bwd(fq, fk, dqs, dkt4, dvt4, qn, kn, bd, bd_t, tm=512):
    s_len = fq.shape[0]

    def body(fq_ref, fk_ref, dq_ref, dkt_ref, dvt_ref, qn_ref, kn_ref, bd_ref, bdt_ref,
             dz_ref, dqn_ref, dkn_ref):
        @pl.when(pl.program_id(0) == 0)
        def _():
            dqn_ref[...] = jnp.zeros_like(dqn_ref)
            dkn_ref[...] = jnp.zeros_like(dkn_ref)

        bd_m = bd_ref[...]
        bdt_m = bdt_ref[...]

        def one(x, dy, gain, dgain_ref):
            xn, rw = _head_rms(x, bd_m, bdt_m)
            dgain_ref[0:1, :] += _colsum(dy * xn)
            dxn = dy * gain
            return rw * (dxn - xn * _head_mean(dxn * xn, bd_m, bdt_m))

        dz_ref[:, 0:ATT_W] = one(fq_ref[...], dq_ref[...] * QK_SCALE, qn_ref[...], dqn_ref).astype(BF16)
        dz_ref[:, ATT_W:2 * ATT_W] = one(fk_ref[...], _rows_of_transposed(dkt_ref), kn_ref[...], dkn_ref).astype(BF16)
        dz_ref[:, 2 * ATT_W:] = _rows_of_transposed(dvt_ref).astype(BF16)

    row = lambda i: (i, 0)
    full = lambda i: (0, 0)
    att = pl.BlockSpec((tm, ATT_W), row)
    return pl.pallas_call(
        body,
        grid=(s_len // tm,),
        in_specs=[att, att, att, _transposed_spec(tm), _transposed_spec(tm),
                  pl.BlockSpec((1, ATT_W), full), pl.BlockSpec((1, ATT_W), full),
                  pl.BlockSpec((ATT_W, LANES), full), pl.BlockSpec((LANES, ATT_W), full)],
        out_specs=[pl.BlockSpec((tm, 3 * ATT_W), row), pl.BlockSpec((8, ATT_W), full), pl.BlockSpec((8, ATT_W), full)],
        out_shape=[jax.ShapeDtypeStruct((s_len, 3 * ATT_W), BF16),
                   jax.ShapeDtypeStruct((8, ATT_W), F32), jax.ShapeDtypeStruct((8, ATT_W), F32)],
        name="qknorm_bwd",
    )(fq, fk, dqs, dkt4, dvt4, qn, kn, bd, bd_t)


def _inproj_bwd(x1, dx2, g, dzf, dlogf, logf, dzs, dgates, w_fox, w_fl, w_sb, w_gates, tm=512):
    s_len = x1.shape[0]

    def body(x_ref, d_ref, g_ref, dzf_ref, dlf_ref, lf_ref, dzs_ref, dgt_ref, wf_ref, wl_ref, ws_ref, wg_ref,
             dx_ref, h_ref, dfl_ref, dg_ref, db_ref):
        @pl.when(pl.program_id(0) == 0)
        def _():
            dg_ref[...] = jnp.zeros_like(dg_ref)
            db_ref[...] = jnp.zeros_like(db_ref)

        xn, r = _rms(x_ref[...])
        gain = g_ref[...]
        h_ref[...] = (xn * gain).astype(BF16)
        lane = lax.broadcasted_iota(jnp.int32, (tm, LANES), 1)
        dfl = jnp.where(lane < N_HEADS, dlf_ref[...] * (1.0 - jnp.exp(lf_ref[...])), 0.0)
        db_ref[0:1, :] += _colsum(dfl)
        dflb = dfl.astype(BF16)
        dfl_ref[...] = dflb
        dh = (_dot_nt(dzf_ref[...], wf_ref[...]) + _dot_nt(dflb, wl_ref[...])
              + _dot_nt(dzs_ref[...], ws_ref[...]) + _dot_nt(dgt_ref[...], wg_ref[...]))
        dx_ref[...] = d_ref[...] + _rms_bwd(dh, xn, r, gain)
        dg_ref[0:1, :] += _colsum(dh * xn)

    row = lambda i: (i, 0)
    full = lambda i: (0, 0)
    return pl.pallas_call(
        body,
        grid=(s_len // tm,),
        in_specs=[
            pl.BlockSpec((tm, D_MODEL), row),
            pl.BlockSpec((tm, D_MODEL), row),
            pl.BlockSpec((1, D_MODEL), full),
            pl.BlockSpec((tm, 3 * ATT_W), row),
            pl.BlockSpec((tm, LANES), row),
            pl.BlockSpec((tm, LANES), row),
            pl.BlockSpec((tm, 3 * ATT_W), row),
            pl.BlockSpec((tm, 2 * D_MODEL), row),
            pl.BlockSpec((D_MODEL, 3 * ATT_W), full),
            pl.BlockSpec((D_MODEL, LANES), full),
            pl.BlockSpec((D_MODEL, 3 * ATT_W), full),
            pl.BlockSpec((D_MODEL, 2 * D_MODEL), full),
        ],
        out_specs=[
            pl.BlockSpec((tm, D_MODEL), row), pl.BlockSpec((tm, D_MODEL), row), pl.BlockSpec((tm, LANES), row),
            pl.BlockSpec((8, D_MODEL), full), pl.BlockSpec((8, LANES), full),
        ],
        out_shape=[
            jax.ShapeDtypeStruct((s_len, D_MODEL), F32), jax.ShapeDtypeStruct((s_len, D_MODEL), BF16),
            jax.ShapeDtypeStruct((s_len, LANES), BF16),
            jax.ShapeDtypeStruct((8, D_MODEL), F32), jax.ShapeDtypeStruct((8, LANES), F32),
        ],
        compiler_params=_cparams(56),
        name="inproj_bwd",
    )(x1, dx2, g, dzf, dlogf, logf, dzs, dgates, w_fox, w_fl, w_sb, w_gates)


def _split_w_in(w_in):
    o = 3 * ATT_W
    w_fox = w_in[:, 0:o]
    w_fl = jnp.pad(w_in[:, o:o + N_HEADS], ((0, 0), (0, LANES - N_HEADS)))
    w_sb = w_in[:, o + N_HEADS:2 * o + N_HEADS]
    w_gates = w_in[:, 2 * o + N_HEADS:]
    return w_fox, w_fl, w_sb, w_gates


def _local_grads(x, p, target, small, full, pending=None, send_early=None):
    blk = ATT_BLOCK
    bd, bd_t = _head_sum_matrices()
    full = dict(full)
    late = list(pending) if pending else []

    x1, a1, b1, u1, *gathered = _ffn_fwd(x, small["ffn1_norm"], full["ffn1_w_gate"], full["ffn1_w_up"],
                                     full["ffn1_w_down"], gather=[pending[k] for k in late])
    for k, gth in zip(late, gathered):
        full[k] = gth if k in KEPT_AS_SHARDS else _whole(k, gth)
    w_fox, w_fl, w_sb, w_gates = _split_w_in(full["w_in"])
    bias = jnp.pad(small["forget_bias"], ((0, 0), (0, LANES - N_HEADS)))
    qn = jnp.tile(small["q_norm"], (1, N_HEADS))
    kn = jnp.tile(small["k_norm"], (1, N_HEADS))
    fq, fk, f_qs, f_k, f_v, logf, s_qs, s_k, s_v, gates = _inproj_fwd(
        x1, small["mix_norm"], w_fox, w_fl, w_sb, w_gates, bias, qn, kn, bd, bd_t)
    f_cum, fw = _cumsum_rows(logf, reverse=False, spread=bd_t)
    f8 = f_cum[:, 0:N_HEADS]
    ft4 = _pair_rows_t(f8, blk)
    f_k3, f_v3 = _blocked_rows(f_k, blk), _blocked_rows(f_v, blk)
    y_fox, lse, f_first = _fox_fwd(f_qs, f_k3, f_v3, fw, ft4,
                                   _normed_dot_bound(small["q_norm"], small["k_norm"]))
    s_k3, s_v3 = _blocked_rows(s_k, blk), _blocked_rows(s_v, blk)
    y_sb, s_rtot, s_first = _sb_fwd(s_qs, s_k3, s_v3)
    x2 = _merge_fwd(x1, gates, y_fox, y_sb, full["w_branch_fox"], full["w_branch_sb"], full["w_out"])
    x3, a2, b2, u2 = _ffn_fwd(x2, small["ffn2_norm"], full["ffn2_w_gate"], full["ffn2_w_up"], full["ffn2_w_down"])

    dx3, du_ple, dt_ple, hn_ple, dg_ple, loss_sum = _ple_loss(
        x3, p, small["ple_norm"], full["w_ple_gate"], full["w_ple_proj"], target)
    dx2, da2, db2, h_ffn2, d3_bf, dg_ffn2 = _ffn_bwd(
        x2, dx3, small["ffn2_norm"], a2, b2, full["ffn2_w_gate"], full["ffn2_w_up"], full["ffn2_w_down"])
    dy_fox, dy_sb, dgates, d_of, d_os, merged, d2_bf = _merge_bwd(
        dx2, gates, y_fox, y_sb, full["w_branch_fox"], full["w_branch_sb"], full["w_out"])

    f_dqs, dfq_p, f_dkt4, f_dvt4, dft4 = _fox_bwd(f_qs, f_k3, f_v3, dy_fox, y_fox, lse, fw, ft4, f_first)
    s_dqs, s_dkt4, s_dvt4 = _sb_bwd(s_qs, s_k3, s_v3, dy_sb, s_rtot, s_first)

    dzf, dqn8, dkn8 = _qknorm_bwd(fq, fk, f_dqs, f_dkt4, f_dvt4, qn, kn, bd, bd_t)
    dzs = _sb_grads_packed(s_dqs, s_dkt4, s_dvt4)
    df8 = _unpair_rows_t(dft4) + dfq_p[:, :, 0:2].transpose(1, 0, 2).reshape(-1, N_HEADS)
    dlogf = _cumsum_rows(jnp.pad(df8, ((0, 0), (0, LANES - N_HEADS))), reverse=True)
    dx1, h_mix, dfl, dg_mix, dbias8 = _inproj_bwd(
        x1, dx2, small["mix_norm"], dzf, dlogf, logf, dzs, dgates, w_fox, w_fl, w_sb, w_gates)

    one = lambda t: t[None]
    gw = {}
    gw["ffn2_w_gate"] = _wgrad(da2, one(h_ffn2), name="wgrad_ffn2_gate")
    gw["ffn2_w_up"] = _wgrad(db2, one(h_ffn2), name="wgrad_ffn2_up")
    gw["ffn2_w_down"] = _wgrad(u2, one(d3_bf), scale=0.5, name="wgrad_ffn2_down")
    g_fox = _wgrad(one(h_mix), one(dzf), name="wgrad_in_fox")[0]
    g_fl = _wgrad(one(h_mix), one(dfl), name="wgrad_in_forget")[0]
    g_sb = _wgrad(one(h_mix), one(dzs), name="wgrad_in_sb")[0]
    g_gt = _wgrad(one(h_mix), one(dgates), name="wgrad_in_gates")[0]
    gw["w_in"] = jnp.concatenate([g_fox, g_fl[:, 0:N_HEADS], g_sb, g_gt], axis=1)
    gw["w_branch_fox"] = _wgrad(one(y_fox), one(d_of), name="wgrad_branch_fox")[0]
    gw["w_branch_sb"] = _wgrad(one(y_sb), one(d_os), name="wgrad_branch_sb")[0]
    gw["w_out"] = _wgrad(one(merged), one(d2_bf), name="wgrad_out")[0]
    gw["w_ple_gate"] = _wgrad(one(hn_ple), one(du_ple), name="wgrad_ple_gate")[0]
    gw["w_ple_proj"] = _wgrad(one(p), one(dt_ple), name="wgrad_ple_proj")[0]

    gw["ffn1_w_down"] = _wgrad(u1, one(dx1), scale=0.5, name="wgrad_ffn1_down")

    sent_names, to_send = send_early(gw) if send_early else ([], [])
    grad_x, da1, db1, h_ffn1, _, dg_ffn1, *landed = _ffn_bwd(
        x, dx1, small["ffn1_norm"], a1, b1, full["ffn1_w_gate"], full["ffn1_w_up"], full["ffn1_w_down"],
        scatter=to_send)
    gw["ffn1_w_gate"] = _wgrad(da1, one(h_ffn1), name="wgrad_ffn1_gate")
    gw["ffn1_w_up"] = _wgrad(db1, one(h_ffn1), name="wgrad_ffn1_up")

    fold = lambda t: jnp.sum(t[0:1].reshape(N_HEADS, HEAD_DIM), axis=0, keepdims=True)
    gs = {
        "ffn1_norm": dg_ffn1[0:1], "mix_norm": dg_mix[0:1], "ffn2_norm": dg_ffn2[0:1], "ple_norm": dg_ple[0:1],
        "forget_bias": dbias8[0:1, 0:N_HEADS], "q_norm": fold(dqn8), "k_norm": fold(dkn8),
    }
    return loss_sum, grad_x, gw, gs, dict(zip(sent_names, landed))


def _position():
    return lax.axis_index("x"), lax.axis_index("y"), lax.axis_index("c")


def _other_chips(x, y):
    return [(1 - x, y), (x, 1 - y), (1 - x, 1 - y)]


ANY = pl.BlockSpec(memory_space=pl.ANY)


def _place_own_shard(w, q):
    rows, cols = w.shape
    tr = _row_block(rows, cols * 4, budget=2 * MIB)

    def body(q_ref, w_ref, o_ref):
        o_ref[0] = w_ref[...].astype(BF16)

    return pl.pallas_call(
        body,
        grid_spec=pltpu.PrefetchScalarGridSpec(
            num_scalar_prefetch=1,
            grid=(rows // tr,),
            in_specs=[pl.BlockSpec((tr, cols), lambda i, q_ref: (i, 0))],
            out_specs=pl.BlockSpec((1, tr, cols), lambda i, q_ref: (q_ref[0], i, 0)),
        ),
        out_shape=jax.ShapeDtypeStruct((N_CHIPS, rows, cols), BF16),
        name="place_own_shard",
    )(q, w)


def _gather_semaphores(n):
    return [pltpu.SemaphoreType.DMA((6 * n,)), pltpu.SemaphoreType.DMA((6 * n,))]


def _gather_steps(bufs, send_sems, recv_sems):
    n = len(bufs)
    x, y, c = _position()
    q = 2 * x + y
    chips = _other_chips(x, y)
    sibling = (x, y, 1 - c)

    def half(a, slot, which):
        r2 = bufs[a].shape[1] // 2
        return bufs[a].at[slot, pl.ds(which * r2, r2), :]

    def copy(a, k, region, to):
        return pltpu.make_async_remote_copy(
            src_ref=region, dst_ref=region, send_sem=send_sems.at[6 * a + k], recv_sem=recv_sems.at[6 * a + k],
            device_id=to, device_id_type=MESH)

    def to_chip(a, k):
        tx, ty = chips[k]
        return copy(a, k, half(a, q, c), (tx, ty, c))

    def to_sibling(a, k):
        tx, ty = chips[k]
        return copy(a, 3 + k, half(a, 2 * tx + ty, c), sibling)

    def start():
        for a in range(n):
            for k in range(3):
                to_chip(a, k).start()

    def finish():
        for a in range(n):
            for k, (tx, ty) in enumerate(chips):
                copy(a, k, half(a, 2 * tx + ty, c), (tx, ty, c)).wait_recv()
                to_sibling(a, k).start()
        for a in range(n):
            for k, (tx, ty) in enumerate(chips):
                copy(a, 3 + k, half(a, 2 * tx + ty, 1 - c), sibling).wait_recv()
        for a in range(n):
            for k in range(3):
                to_chip(a, k).wait_send()
                to_sibling(a, k).wait_send()

    return start, finish


def _allgather_weights(slots):
    n = len(slots)

    def body(*refs):
        start, finish = _gather_steps(refs[n:2 * n], *refs[2 * n:])
        start()
        finish()

    return pl.pallas_call(
        body,
        in_specs=[ANY] * n,
        out_specs=[ANY] * n,
        out_shape=[jax.ShapeDtypeStruct(s.shape, s.dtype) for s in slots],
        input_output_aliases={a: a for a in range(n)},
        scratch_shapes=_gather_semaphores(n),
        name="allgather_weights",
    )(*slots)


def _exchange_pair_halves(grads):
    n = len(grads)

    def body(*refs):
        ins, outs = refs[0:n], refs[n:2 * n]
        send_sems, recv_sems = refs[2 * n:]
        x, y, c = _position()
        copies = []
        for a in range(n):
            r2 = grads[a].shape[1] // 2
            cp = pltpu.make_async_remote_copy(
                src_ref=ins[a].at[:, pl.ds((1 - c) * r2, r2), :], dst_ref=outs[a],
                send_sem=send_sems.at[a], recv_sem=recv_sems.at[a], device_id=(x, y, 1 - c), device_id_type=MESH)
            cp.start()
            copies.append(cp)
        for cp in copies:
            cp.wait()

    return pl.pallas_call(
        body,
        in_specs=[ANY] * n,
        out_specs=[ANY] * n,
        out_shape=[jax.ShapeDtypeStruct((N_CHIPS, g.shape[1] // 2, g.shape[2]), g.dtype) for g in grads],
        scratch_shapes=[pltpu.SemaphoreType.DMA((n,)), pltpu.SemaphoreType.DMA((n,))],
        name="rs_pair_exchange",
    )(*grads)


def _scatter_semaphores(n):
    return [pltpu.SemaphoreType.DMA((3 * n,)), pltpu.SemaphoreType.DMA((3 * n,)), pltpu.SemaphoreType.DMA((n,))]


def _scatter_steps(ins, outs, send_sems, recv_sems, local_sems):
    n = len(ins)
    x, y, c = _position()
    q = 2 * x + y
    chips = _other_chips(x, y)

    def own(a):
        return pltpu.make_async_copy(ins[a].at[q], outs[a].at[q], local_sems.at[a])

    def to_chip(a, k):
        tx, ty = chips[k]
        return pltpu.make_async_remote_copy(
            src_ref=ins[a].at[2 * tx + ty], dst_ref=outs[a].at[q],
            send_sem=send_sems.at[3 * a + k], recv_sem=recv_sems.at[3 * a + k],
            device_id=(tx, ty, c), device_id_type=MESH)

    def start():
        for a in range(n):
            own(a).start()
            for k in range(3):
                to_chip(a, k).start()

    def finish():
        for a in range(n):
            own(a).wait()
            for k in range(3):
                to_chip(a, k).wait()

    return start, finish


def _scatter_to_owner_chips(pairs):
    n = len(pairs)

    def body(*refs):
        start, finish = _scatter_steps(refs[0:n], refs[n:2 * n], *refs[2 * n:])
        start()
        finish()

    return pl.pallas_call(
        body,
        in_specs=[ANY] * n,
        out_specs=[ANY] * n,
        out_shape=[jax.ShapeDtypeStruct(p.shape, p.dtype) for p in pairs],
        scratch_shapes=_scatter_semaphores(n),
        name="rs_scatter",
    )(*pairs)


def _join_halves(shards):
    n = len(shards)

    def body(*refs):
        bufs = refs[n:2 * n]
        send_sems, recv_sems = refs[2 * n:]
        x, y, c = _position()
        started = []
        for a in range(n):
            r2 = shards[a].shape[0] // 2
            mine = bufs[a].at[pl.ds(c * r2, r2), :]
            cp = pltpu.make_async_remote_copy(
                src_ref=mine, dst_ref=mine, send_sem=send_sems.at[a], recv_sem=recv_sems.at[a],
                device_id=(x, y, 1 - c), device_id_type=MESH)
            cp.start()
            started.append(cp)
        for cp in started:
            cp.wait()

    return pl.pallas_call(
        body,
        in_specs=[ANY] * n,
        out_specs=[ANY] * n,
        out_shape=[jax.ShapeDtypeStruct(t.shape, t.dtype) for t in shards],
        input_output_aliases={a: a for a in range(n)},
        scratch_shapes=[pltpu.SemaphoreType.DMA((n,)), pltpu.SemaphoreType.DMA((n,))],
        name="rs_join_halves",
    )(*shards)


def _add_pair(g, got, c):
    _, r2, cols = got.shape

    def body(c_ref, g_ref, got_ref, o_ref):
        o_ref[...] = (g_ref[...].astype(F32) + got_ref[...].astype(F32)).astype(BF16)

    spec = pl.BlockSpec((1, r2, cols), lambda s, c_ref: (s, 0, 0))
    return pl.pallas_call(
        body,
        grid_spec=pltpu.PrefetchScalarGridSpec(
            num_scalar_prefetch=1,
            grid=(N_CHIPS,),
            in_specs=[pl.BlockSpec((1, r2, cols), lambda s, c_ref: (s, c_ref[0], 0)), spec],
            out_specs=spec,
        ),
        out_shape=jax.ShapeDtypeStruct(got.shape, BF16),
        name="rs_add_pair",
    )(c, g, got)


def _add_chips(parts, c):
    _, r2, cols = parts.shape

    def body(c_ref, p0, p1, p2, p3, o_ref):
        o_ref[...] = ((p0[0].astype(F32) + p1[0].astype(F32)) + p2[0].astype(F32)) + p3[0].astype(F32)

    specs = [pl.BlockSpec((1, r2, cols), functools.partial(lambda i, c_ref, s: (s, 0, 0), s=s))
             for s in range(N_CHIPS)]
    return pl.pallas_call(
        body,
        grid_spec=pltpu.PrefetchScalarGridSpec(
            num_scalar_prefetch=1,
            grid=(1,),
            in_specs=specs,
            out_specs=pl.BlockSpec((r2, cols), lambda i, c_ref: (c_ref[0], 0)),
        ),
        out_shape=jax.ShapeDtypeStruct((2 * r2, cols), F32),
        name="rs_add_chips",
    )(c, parts, parts, parts, parts)


def _allreduce_small(part):
    shape = part.shape

    def body(in_ref, out_ref, gather_ref, send_sems, recv_sems):
        x, y, c = _position()
        me = 4 * x + 2 * y + c
        relations = [(a, b, d) for a in (0, 1) for b in (0, 1) for d in (0, 1)][1:]
        flip = lambda v, f: 1 - v if f else v
        copies = []
        for k, (a, b, d) in enumerate(relations):
            cp = pltpu.make_async_remote_copy(
                src_ref=in_ref, dst_ref=gather_ref.at[me], send_sem=send_sems.at[k], recv_sem=recv_sems.at[k],
                device_id=(flip(x, a), flip(y, b), flip(c, d)), device_id_type=MESH)
            cp.start()
            copies.append(cp)
        gather_ref[me] = in_ref[...]
        for cp in copies:
            cp.wait()
        total = gather_ref[0]
        for dev in range(1, 8):
            total = total + gather_ref[dev]
        out_ref[...] = total

    vmem = pl.BlockSpec(memory_space=pltpu.VMEM)
    return pl.pallas_call(
        body,
        in_specs=[vmem],
        out_specs=vmem,
        out_shape=jax.ShapeDtypeStruct(shape, F32),
        scratch_shapes=[pltpu.VMEM((8,) + shape, F32), pltpu.SemaphoreType.DMA((7,)), pltpu.SemaphoreType.DMA((7,))],
        name="allreduce_small",
    )(part)


def _adamw(w, g, m, v):
    rows, cols = w.shape
    tr = _row_block(rows, cols * 4, budget=MIB)
    c1 = 1.0 / (1.0 - ADAM_B1 ** ADAM_STEP)
    c2 = 1.0 / (1.0 - ADAM_B2 ** ADAM_STEP)

    def body(w_ref, g_ref, m_ref, v_ref, d_ref, nm_ref, nv_ref):
        g_ = g_ref[...]
        nm = ADAM_B1 * m_ref[...] + (1.0 - ADAM_B1) * g_
        nv = ADAM_B2 * v_ref[...] + (1.0 - ADAM_B2) * (g_ * g_)
        nm_ref[...] = nm
        nv_ref[...] = nv
        d_ref[...] = -ADAM_LR * ((nm * c1) / (jnp.sqrt(nv * c2) + ADAM_EPS) + ADAM_WD * w_ref[...])

    spec = pl.BlockSpec((tr, cols), lambda i: (i, 0))
    out = jax.ShapeDtypeStruct((rows, cols), F32)
    return pl.pallas_call(
        body,
        grid=(rows // tr,),
        in_specs=[spec] * 4,
        out_specs=[spec] * 3,
        out_shape=[out] * 3,
        name="adamw",
    )(w, g, m, v)


BIG = ["ffn1_w_gate", "ffn1_w_up", "ffn1_w_down", "w_in", "w_branch_fox", "w_branch_sb", "w_out",
       "ffn2_w_gate", "ffn2_w_up", "ffn2_w_down", "w_ple_gate", "w_ple_proj"]
SMALL = ["ffn1_norm", "mix_norm", "ffn2_norm", "ple_norm", "forget_bias", "q_norm", "k_norm"]
COLUMN_SHARDED = ["w_in", "w_branch_fox", "w_branch_sb", "w_ple_proj"]
KEPT_AS_SHARDS = ["ffn1_w_gate", "ffn1_w_up", "ffn1_w_down", "ffn2_w_gate", "ffn2_w_up", "ffn2_w_down"]
WORKED_TRANSPOSED = ["ffn1_w_gate", "ffn1_w_up", "ffn2_w_gate", "ffn2_w_up"]
NEEDED_FIRST = ["ffn1_w_gate", "ffn1_w_up", "ffn1_w_down"]
READY_LAST = ["ffn1_w_gate", "ffn1_w_up"]
ORDER = ["ffn1_norm", "ffn1_w_gate", "ffn1_w_up", "ffn1_w_down", "mix_norm", "w_in", "forget_bias", "q_norm",
         "k_norm", "w_branch_fox", "w_branch_sb", "w_out", "ffn2_norm", "ffn2_w_gate", "ffn2_w_up",
         "ffn2_w_down", "ple_norm", "w_ple_gate", "w_ple_proj"]
SMALL_ROWS = {"ffn1_norm": 0, "mix_norm": 1, "ffn2_norm": 2, "ple_norm": 3}
SMALL_COLS = {"forget_bias": (0, N_HEADS), "q_norm": (N_HEADS, HEAD_DIM), "k_norm": (N_HEADS + HEAD_DIM, HEAD_DIM)}
LOSS_ROW = 5


def _stored(name, a):
    return jnp.swapaxes(a[0], 0, 1) if name in WORKED_TRANSPOSED else a[0]


def _returned(name, t):
    return (jnp.swapaxes(t, 0, 1) if name in WORKED_TRANSPOSED else t)[None]


def _whole(name, gathered):
    if name in COLUMN_SHARDED:
        return jnp.concatenate([gathered[s] for s in range(N_CHIPS)], axis=1)
    return gathered.reshape(-1, gathered.shape[-1])


def _as_shards(name, whole):
    if name in COLUMN_SHARDED:
        k, n = whole.shape
        return whole.reshape(k, N_CHIPS, n // N_CHIPS).transpose(1, 0, 2)
    return whole.reshape(N_CHIPS, whole.shape[0] // N_CHIPS, whole.shape[1])


def _pack_small(values, extra=None):
    rows = [values[k] for k in ("ffn1_norm", "mix_norm", "ffn2_norm", "ple_norm")]
    tail = jnp.concatenate([values["forget_bias"], values["q_norm"], values["k_norm"]], axis=1)
    rows.append(jnp.pad(tail, ((0, 0), (0, D_MODEL - tail.shape[1]))))
    packed = jnp.concatenate(rows + [jnp.zeros((3, D_MODEL), F32)], axis=0)
    if extra is not None:
        packed = packed.at[LOSS_ROW, 0].set(extra)
    return packed


def _unpack_small(packed):
    out = {k: packed[r:r + 1] for k, r in SMALL_ROWS.items()}
    for k, (start, size) in SMALL_COLS.items():
        out[k] = packed[4:5, start:start + size]
    return out


def kernel(x, p, ffn1_norm, ffn1_w_gate, ffn1_w_up, ffn1_w_down, mix_norm, w_in, forget_bias, q_norm, k_norm, w_branch_fox, w_branch_sb, w_out, ffn2_norm, ffn2_w_gate, ffn2_w_up, ffn2_w_down, ple_norm, w_ple_gate, w_ple_proj, loss_target, m_ffn1_norm, m_ffn1_w_gate, m_ffn1_w_up, m_ffn1_w_down, m_mix_norm, m_w_in, m_forget_bias, m_q_norm, m_k_norm, m_w_branch_fox, m_w_branch_sb, m_w_out, m_ffn2_norm, m_ffn2_w_gate, m_ffn2_w_up, m_ffn2_w_down, m_ple_norm, m_w_ple_gate, m_w_ple_proj, v_ffn1_norm, v_ffn1_w_gate, v_ffn1_w_up, v_ffn1_w_down, v_mix_norm, v_w_in, v_forget_bias, v_q_norm, v_k_norm, v_w_branch_fox, v_w_branch_sb, v_w_out, v_ffn2_norm, v_ffn2_w_gate, v_ffn2_w_up, v_ffn2_w_down, v_ple_norm, v_w_ple_gate, v_w_ple_proj):
    args = dict(locals())
    weights = {k: args[k] for k in ORDER}
    moments_m = {k: args["m_" + k] for k in ORDER}
    moments_v = {k: args["v_" + k] for k in ORDER}

    c_idx = lax.axis_index("c").astype(jnp.int32).reshape(1)
    q_idx = (2 * lax.axis_index("x") + lax.axis_index("y")).astype(jnp.int32).reshape(1)
    own = {k: _place_own_shard(_stored(k, weights[k]), q_idx) for k in BIG}
    full = dict(zip(NEEDED_FIRST, _allgather_weights([own[k] for k in NEEDED_FIRST])))
    pending = {k: own[k] for k in BIG if k not in NEEDED_FIRST}
    small = {k: weights[k] for k in SMALL}

    def pair_sums(names, gw):
        slots = [gw[k] if k in KEPT_AS_SHARDS else _as_shards(k, gw[k]) for k in names]
        from_core = _exchange_pair_halves(slots)
        return [_add_pair(g, got, c_idx) for g, got in zip(slots, from_core)]

    early = [k for k in BIG if k not in READY_LAST]
    loss_sum, grad_x, gw, gs, parts = _local_grads(
        x[0], p[0, 0], loss_target[0], small, full, pending, lambda ready: (early, pair_sums(early, ready)))

    parts.update(zip(READY_LAST, _scatter_to_owner_chips(pair_sums(READY_LAST, gw))))
    grads_big = dict(zip(BIG, _join_halves([_add_chips(parts[k], c_idx) for k in BIG])))
    reduced = _allreduce_small(_pack_small(gs, extra=loss_sum[0, 0]))
    grads_small = _unpack_small(reduced)
    loss = reduced[LOSS_ROW, 0]

    grads, deltas, new_m, new_v = {}, {}, {}, {}
    for k in BIG:
        d, nm, nv = _adamw(_stored(k, weights[k]), grads_big[k], _stored(k, moments_m[k]), _stored(k, moments_v[k]))
        grads[k], deltas[k], new_m[k], new_v[k] = (_returned(k, t) for t in (grads_big[k], d, nm, nv))
    d_s, nm_s, nv_s = _adamw(_pack_small({k: weights[k] for k in SMALL}), reduced,
                             _pack_small({k: moments_m[k] for k in SMALL}),
                             _pack_small({k: moments_v[k] for k in SMALL}))
    for k in SMALL:
        grads[k] = grads_small[k]
    for name, packed in (("d", d_s), ("m", nm_s), ("v", nv_s)):
        target = {"d": deltas, "m": new_m, "v": new_v}[name]
        target.update(_unpack_small(packed))

    return (loss, grad_x[None], *[grads[k] for k in ORDER], *[deltas[k] for k in ORDER],
            *[new_m[k] for k in ORDER], *[new_v[k] for k in ORDER])
```
